```python
import math
import jax, jax.numpy as jnp
from jax import lax
import numpy as np

D_MODEL = 1024
BATCH = 16
SEQ = 2048
DEPTH = 1

MIX_WIDTH = D_MODEL
ATT_WIDTH = MIX_WIDTH // 2
HG_WIDTH = MIX_WIDTH - ATT_WIDTH
ATT_HEAD_DIM = 64
ATT_Q_HEADS = ATT_WIDTH // ATT_HEAD_DIM
ATT_KV_HEADS = 2
ATT_KV_COLS = ATT_KV_HEADS * ATT_HEAD_DIM
WINDOW = 128
ROPE_DIM = ATT_HEAD_DIM // 4
ROPE_THETA = 500000.0
HG_HEAD_DIM = 128
HG_HEADS = HG_WIDTH // HG_HEAD_DIM
HG_CHUNK = 32
IN_COLS = ATT_WIDTH + 2 * ATT_KV_COLS + 4 * HG_WIDTH
_SPLITS = list(np.cumsum([ATT_WIDTH, ATT_KV_COLS, ATT_KV_COLS, HG_WIDTH, HG_WIDTH, HG_WIDTH])[:].tolist())
D_FF = 4 * D_MODEL
N_MOD = 6
EPS = 1e-6

kernel_name = "hybrid_swa_sink_hgrn2_adaln_layer"


def rmsnorm(x, w):
    xf = x.astype(jnp.float32)
    y = xf * lax.rsqrt(jnp.mean(xf * xf, axis=-1, keepdims=True) + EPS)
    return (y * w.astype(jnp.float32)).astype(x.dtype)


def partial_rope(x):
    T = x.shape[1]
    half = ROPE_DIM // 2
    inv_freq = ROPE_THETA ** (-jnp.arange(0, ROPE_DIM, 2, dtype=jnp.float32) / ROPE_DIM)
    ang = jnp.arange(T, dtype=jnp.float32)[:, None] * inv_freq[None, :]
    cos = jnp.cos(ang)[None, :, None, :].astype(x.dtype)
    sin = jnp.sin(ang)[None, :, None, :].astype(x.dtype)
    x1, x2, rest = x[..., :half], x[..., half:ROPE_DIM], x[..., ROPE_DIM:]
    return jnp.concatenate([x1 * cos - x2 * sin, x2 * cos + x1 * sin, rest], axis=-1)


def sliding_window_sink_attention(q, k, v, sinks):
    B, T, Hq, D = q.shape
    nb = T // WINDOW
    G = Hq // ATT_KV_HEADS
    qb = q.reshape(B, nb, WINDOW, ATT_KV_HEADS, G, D)

    def band(a):
        ab = a.reshape(B, nb, WINDOW, ATT_KV_HEADS, D)
        prev = jnp.pad(ab, ((0, 0), (1, 0), (0, 0), (0, 0), (0, 0)))[:, :-1]
        return jnp.concatenate([prev, ab], axis=2)

    kk, vv = band(k), band(v)
    s = jnp.einsum('bnqhgd,bnkhd->bnhgqk', qb, kk).astype(jnp.float32) * (D ** -0.5)
    blk = jnp.arange(nb)[:, None]
    q_pos = blk * WINDOW + jnp.arange(WINDOW)[None, :]
    k_pos = (blk - 1) * WINDOW + jnp.arange(2 * WINDOW)[None, :]
    diff = q_pos[:, :, None] - k_pos[:, None, :]
    mask = (diff >= 0) & (diff < WINDOW) & (k_pos[:, None, :] >= 0)
    s = jnp.where(mask[None, :, None, None], s, jnp.finfo(jnp.float32).min)
    sink = sinks.astype(jnp.float32).reshape(ATT_KV_HEADS, G)[None, None, :, :, None, None]
    m = jnp.maximum(jnp.max(s, axis=-1, keepdims=True), sink)
    p = jnp.exp(s - m)
    p = p / (jnp.sum(p, axis=-1, keepdims=True) + jnp.exp(sink - m))
    o = jnp.einsum('bnhgqk,bnkhd->bnqhgd', p.astype(v.dtype), vv)
    return o.reshape(B, T, Hq * D)


def hgrn2_chunkwise(q, k, v, log_f):
    B, T, H, Dk = q.shape
    Dv = v.shape[-1]
    nc = T // HG_CHUNK

    def to_chunks(a):
        return a.astype(jnp.float32).reshape(B, nc, HG_CHUNK, H, a.shape[-1]).transpose(1, 0, 3, 2, 4)

    qc, kc, vc, gc = to_chunks(q), to_chunks(k), to_chunks(v), to_chunks(log_f)
    bc = jnp.cumsum(gc, axis=3)
    tri = jnp.tril(jnp.ones((HG_CHUNK, HG_CHUNK), dtype=bool))

    def step(S, inp):
        q_, k_, v_, b_ = inp
        b_last = b_[:, :, -1:, :]
        q_dec = q_ * jnp.exp(b_)
        k_dec = k_ * jnp.exp(-b_)
        a = jnp.where(tri, jnp.einsum('bhtk,bhsk->bhts', q_dec, k_dec), 0.0)
        o = jnp.einsum('bhts,bhsv->bhtv', a, v_) + jnp.einsum('bhtk,bhkv->bhtv', q_dec, S)
        S = S * jnp.exp(b_last[:, :, 0, :])[..., None] + \
            jnp.einsum('bhsk,bhsv->bhkv', k_ * jnp.exp(b_last - b_), v_)
        return S, o

    S0 = jnp.zeros((B, H, Dk, Dv), jnp.float32)
    _, o = lax.scan(step, S0, (qc, kc, vc, bc))
    return o.transpose(1, 0, 3, 2, 4).reshape(B, T, H, Dv)


def _fwd_setup_inputs(seed: int = 0) -> dict:
    key = jax.random.key(seed)
    ks = jax.random.split(key, 17)
    f32 = jnp.float32

    def gain(k, shape):
        return (1.0 + 0.02 * jax.random.normal(k, shape)).astype(f32)

    return {
        "x": jax.random.normal(ks[0], (BATCH, SEQ, D_MODEL), f32),
        "c": jax.random.normal(ks[1], (BATCH, D_MODEL), f32),
        "w_ada": jax.random.normal(ks[2], (DEPTH, D_MODEL, N_MOD * D_MODEL), f32) * (0.5 * D_MODEL ** -0.5),
        "b_ada": jax.random.normal(ks[3], (DEPTH, N_MOD * D_MODEL), f32) * 0.02,
        "pre_w_mix": gain(ks[4], (DEPTH, D_MODEL)),
        "w_in": jax.random.normal(ks[5], (DEPTH, D_MODEL, IN_COLS), f32) * D_MODEL ** -0.5,
        "attn_sinks": jax.random.normal(ks[6], (DEPTH, ATT_Q_HEADS), f32) * 0.5,
        "attn_out_w": gain(ks[7], (DEPTH, ATT_WIDTH)),
        "lb_table": jax.random.normal(ks[8], (DEPTH + 1, HG_WIDTH), f32) * 0.1,
        "hg_norm_w": gain(ks[9], (DEPTH, HG_HEAD_DIM)),
        "w_out": jax.random.normal(ks[10], (DEPTH, MIX_WIDTH, D_MODEL), f32) * MIX_WIDTH ** -0.5,
        "post_w_mix": gain(ks[11], (DEPTH, D_MODEL)),
        "pre_w_mlp": gain(ks[12], (DEPTH, D_MODEL)),
        "w_up": jax.random.normal(ks[13], (DEPTH, D_MODEL, D_FF), f32) * D_MODEL ** -0.5,
        "w_down": jax.random.normal(ks[14], (DEPTH, D_FF, D_MODEL), f32) * D_FF ** -0.5,
        "post_w_mlp": gain(ks[15], (DEPTH, D_MODEL)),
    }


def _fwd_reference(x, c, w_ada, b_ada, pre_w_mix, w_in, attn_sinks, attn_out_w, lb_table,
              hg_norm_w, w_out, post_w_mix, pre_w_mlp, w_up, w_down, post_w_mlp):
    B, T, _ = x.shape
    lb_p = jax.nn.softmax(lb_table.astype(jnp.float32), axis=0)
    lower_bounds = jnp.cumsum(lb_p, axis=0) - lb_p[0:1]
    c_act = jax.nn.silu(c)

    for l in range(DEPTH):
        mod = c_act @ w_ada[l] + b_ada[l]
        sh1, sc1, g1, sh2, sc2, g2 = [m[:, None, :] for m in jnp.split(mod, N_MOD, axis=-1)]

        h = rmsnorm(x, pre_w_mix[l]) * (1.0 + sc1) + sh1
        proj = h @ w_in[l]
        aq, ak, av, hq, hf, hi, hg = jnp.split(proj, _SPLITS, axis=-1)

        aq = partial_rope(aq.reshape(B, T, ATT_Q_HEADS, ATT_HEAD_DIM))
        ak = partial_rope(ak.reshape(B, T, ATT_KV_HEADS, ATT_HEAD_DIM))
        av = av.reshape(B, T, ATT_KV_HEADS, ATT_HEAD_DIM)
        attn = sliding_window_sink_attention(aq, ak, av, attn_sinks[l])
        attn = rmsnorm(attn, attn_out_w[l])

        lb = lower_bounds[l + 1].reshape(HG_HEADS, HG_HEAD_DIM)
        f = lb + (1.0 - lb) * jax.nn.sigmoid(hf.reshape(B, T, HG_HEADS, HG_HEAD_DIM).astype(jnp.float32))
        hq4 = jax.nn.silu(hq.reshape(B, T, HG_HEADS, HG_HEAD_DIM))
        hv4 = hi.reshape(B, T, HG_HEADS, HG_HEAD_DIM)
        rec = hgrn2_chunkwise(hq4, 1.0 - f, hv4, jnp.log(f)).astype(x.dtype)
        rec = rmsnorm(rec, hg_norm_w[l]) * jax.nn.silu(hg.reshape(B, T, HG_HEADS, HG_HEAD_DIM))
        rec = rec.reshape(B, T, HG_WIDTH)

        mix = jnp.concatenate([attn, rec], axis=-1) @ w_out[l]
        x = x + g1 * rmsnorm(mix, post_w_mix[l])

        h = rmsnorm(x, pre_w_mlp[l]) * (1.0 + sc2) + sh2
        u = jnp.square(jax.nn.relu(h @ w_up[l]))
        x = x + g2 * rmsnorm(u @ w_down[l], post_w_mlp[l])
    return x


import jax as _jax
import jax.numpy as _jnp

TWIN_FORMAT = 'train_step'
FWD_PARAMS = ['x', 'c', 'w_ada', 'b_ada', 'pre_w_mix', 'w_in', 'attn_sinks', 'attn_out_w', 'lb_table', 'hg_norm_w', 'w_out', 'post_w_mix', 'pre_w_mlp', 'w_up', 'w_down', 'post_w_mlp']
TWIN_WEIGHTS = ['w_ada', 'b_ada', 'pre_w_mix', 'w_in', 'attn_sinks', 'attn_out_w', 'lb_table', 'hg_norm_w', 'w_out', 'post_w_mix', 'pre_w_mlp', 'w_up', 'w_down', 'post_w_mlp']
TWIN_DIFF_INPUT = 'x'
TWIN_INPUTS = ['x', 'c', 'w_ada', 'b_ada', 'pre_w_mix', 'w_in', 'attn_sinks', 'attn_out_w', 'lb_table', 'hg_norm_w', 'w_out', 'post_w_mix', 'pre_w_mlp', 'w_up', 'w_down', 'post_w_mlp', 'loss_target', 'm_w_ada', 'm_b_ada', 'm_pre_w_mix', 'm_w_in', 'm_attn_sinks', 'm_attn_out_w', 'm_lb_table', 'm_hg_norm_w', 'm_w_out', 'm_post_w_mix', 'm_pre_w_mlp', 'm_w_up', 'm_w_down', 'm_post_w_mlp', 'v_w_ada', 'v_b_ada', 'v_pre_w_mix', 'v_w_in', 'v_attn_sinks', 'v_attn_out_w', 'v_lb_table', 'v_hg_norm_w', 'v_w_out', 'v_post_w_mix', 'v_pre_w_mlp', 'v_w_up', 'v_w_down', 'v_post_w_mlp']
TWIN_OUTPUTS = ['loss', 'grad_x', 'grad_w_ada', 'grad_b_ada', 'grad_pre_w_mix', 'grad_w_in', 'grad_attn_sinks', 'grad_attn_out_w', 'grad_lb_table', 'grad_hg_norm_w', 'grad_w_out', 'grad_post_w_mix', 'grad_pre_w_mlp', 'grad_w_up', 'grad_w_down', 'grad_post_w_mlp', 'delta_w_ada', 'delta_b_ada', 'delta_pre_w_mix', 'delta_w_in', 'delta_attn_sinks', 'delta_attn_out_w', 'delta_lb_table', 'delta_hg_norm_w', 'delta_w_out', 'delta_post_w_mix', 'delta_pre_w_mlp', 'delta_w_up', 'delta_w_down', 'delta_post_w_mlp', 'new_m_w_ada', 'new_m_b_ada', 'new_m_pre_w_mix', 'new_m_w_in', 'new_m_attn_sinks', 'new_m_attn_out_w', 'new_m_lb_table', 'new_m_hg_norm_w', 'new_m_w_out', 'new_m_post_w_mix', 'new_m_pre_w_mlp', 'new_m_w_up', 'new_m_w_down', 'new_m_post_w_mlp', 'new_v_w_ada', 'new_v_b_ada', 'new_v_pre_w_mix', 'new_v_w_in', 'new_v_attn_sinks', 'new_v_attn_out_w', 'new_v_lb_table', 'new_v_hg_norm_w', 'new_v_w_out', 'new_v_post_w_mix', 'new_v_pre_w_mlp', 'new_v_w_up', 'new_v_w_down', 'new_v_post_w_mlp']
TWIN_LEAF_KINDS = {'loss': 'loss', 'grad_x': 'grad_x', 'grad_w_ada': 'grad_w', 'grad_b_ada': 'grad_w', 'grad_pre_w_mix': 'grad_w', 'grad_w_in': 'grad_w', 'grad_attn_sinks': 'grad_w', 'grad_attn_out_w': 'grad_w', 'grad_lb_table': 'grad_w', 'grad_hg_norm_w': 'grad_w', 'grad_w_out': 'grad_w', 'grad_post_w_mix': 'grad_w', 'grad_pre_w_mlp': 'grad_w', 'grad_w_up': 'grad_w', 'grad_w_down': 'grad_w', 'grad_post_w_mlp': 'grad_w', 'delta_w_ada': 'delta_w', 'delta_b_ada': 'delta_w', 'delta_pre_w_mix': 'delta_w', 'delta_w_in': 'delta_w', 'delta_attn_sinks': 'delta_w', 'delta_attn_out_w': 'delta_w', 'delta_lb_table': 'delta_w', 'delta_hg_norm_w': 'delta_w', 'delta_w_out': 'delta_w', 'delta_post_w_mix': 'delta_w', 'delta_pre_w_mlp': 'delta_w', 'delta_w_up': 'delta_w', 'delta_w_down': 'delta_w', 'delta_post_w_mlp': 'delta_w', 'new_m_w_ada': 'new_m', 'new_m_b_ada': 'new_m', 'new_m_pre_w_mix': 'new_m', 'new_m_w_in': 'new_m', 'new_m_attn_sinks': 'new_m', 'new_m_attn_out_w': 'new_m', 'new_m_lb_table': 'new_m', 'new_m_hg_norm_w': 'new_m', 'new_m_w_out': 'new_m', 'new_m_post_w_mix': 'new_m', 'new_m_pre_w_mlp': 'new_m', 'new_m_w_up': 'new_m', 'new_m_w_down': 'new_m', 'new_m_post_w_mlp': 'new_m', 'new_v_w_ada': 'new_v', 'new_v_b_ada': 'new_v', 'new_v_pre_w_mix': 'new_v', 'new_v_w_in': 'new_v', 'new_v_attn_sinks': 'new_v', 'new_v_attn_out_w': 'new_v', 'new_v_lb_table': 'new_v', 'new_v_hg_norm_w': 'new_v', 'new_v_w_out': 'new_v', 'new_v_post_w_mix': 'new_v', 'new_v_pre_w_mlp': 'new_v', 'new_v_w_up': 'new_v', 'new_v_w_down': 'new_v', 'new_v_post_w_mlp': 'new_v'}


def _forward(args):
    return _fwd_reference(*[args[k] for k in FWD_PARAMS])


def _output_shape():
    out = _jax.eval_shape(lambda: _forward(_fwd_setup_inputs(0)))
    return out.shape, out.dtype

N_MICROBATCH = 1
ADAM_LR = 0.001
ADAM_B1 = 0.9
ADAM_B2 = 0.999
ADAM_EPS = 1e-08
ADAM_WD = 0.01
ADAM_STEP = 10
PER_EXAMPLE_BATCH_AXIS = {'x': 0, 'c': 0, 'loss_target': 0}
SHARED_INPUTS = []
_WEIGHT_DTYPES = {'w_ada': _jnp.float32, 'b_ada': _jnp.float32, 'pre_w_mix': _jnp.float32, 'w_in': _jnp.float32, 'attn_sinks': _jnp.float32, 'attn_out_w': _jnp.float32, 'lb_table': _jnp.float32, 'hg_norm_w': _jnp.float32, 'w_out': _jnp.float32, 'post_w_mix': _jnp.float32, 'pre_w_mlp': _jnp.float32, 'w_up': _jnp.float32, 'w_down': _jnp.float32, 'post_w_mlp': _jnp.float32}
MOMENT_SCALE = {'w_ada': 2.430596e+00, 'b_ada': 4.353342e+00, 'pre_w_mix': 1.269794e-01, 'w_in': 8.008644e-01, 'attn_sinks': 4.809353e-02, 'attn_out_w': 1.779079e+00, 'lb_table': 7.703094e-03, 'hg_norm_w': 4.761111e-01, 'w_out': 1.362221e+00, 'post_w_mix': 4.610027e+00, 'pre_w_mlp': 2.270372e-01, 'w_up': 1.521867e-01, 'w_down': 6.479562e-01, 'post_w_mlp': 3.922261e+00}


def _to_microbatches(a, axis):
    t = _jnp.moveaxis(a, axis, 0)
    t = t.reshape((N_MICROBATCH, t.shape[0] // N_MICROBATCH) + t.shape[1:])
    return _jnp.moveaxis(t, 1, axis + 1)


def setup_inputs(seed: int = 0) -> dict:
    inp = _fwd_setup_inputs(seed)
    key = _jax.random.fold_in(_jax.random.key(seed), 7919)
    shape, _ = _output_shape()
    out = dict(inp)
    out["loss_target"] = _jax.random.normal(_jax.random.fold_in(key, 0), shape, _jnp.float32)
    for i, name in enumerate(TWIN_WEIGHTS):
        w = inp[name].astype(_jnp.float32)
        if MOMENT_SCALE is None:
            s = _jnp.sqrt(_jnp.mean(_jnp.square(w)) + 1e-30)
        else:
            s = MOMENT_SCALE[name]
        km, kv = _jax.random.split(_jax.random.fold_in(key, i + 1))
        out[name] = w
        out["m_" + name] = s * _jax.random.normal(km, w.shape, _jnp.float32)
        out["v_" + name] = (s * s) * _jax.random.uniform(kv, w.shape, _jnp.float32, 0.5, 1.5)
    if N_MICROBATCH > 1:
        for name, axis in PER_EXAMPLE_BATCH_AXIS.items():
            out[name] = _to_microbatches(out[name], axis)
    return {'x': out['x'], 'c': out['c'], 'w_ada': out['w_ada'], 'b_ada': out['b_ada'], 'pre_w_mix': out['pre_w_mix'], 'w_in': out['w_in'], 'attn_sinks': out['attn_sinks'], 'attn_out_w': out['attn_out_w'], 'lb_table': out['lb_table'], 'hg_norm_w': out['hg_norm_w'], 'w_out': out['w_out'], 'post_w_mix': out['post_w_mix'], 'pre_w_mlp': out['pre_w_mlp'], 'w_up': out['w_up'], 'w_down': out['w_down'], 'post_w_mlp': out['post_w_mlp'], 'loss_target': out['loss_target'], 'm_w_ada': out['m_w_ada'], 'm_b_ada': out['m_b_ada'], 'm_pre_w_mix': out['m_pre_w_mix'], 'm_w_in': out['m_w_in'], 'm_attn_sinks': out['m_attn_sinks'], 'm_attn_out_w': out['m_attn_out_w'], 'm_lb_table': out['m_lb_table'], 'm_hg_norm_w': out['m_hg_norm_w'], 'm_w_out': out['m_w_out'], 'm_post_w_mix': out['m_post_w_mix'], 'm_pre_w_mlp': out['m_pre_w_mlp'], 'm_w_up': out['m_w_up'], 'm_w_down': out['m_w_down'], 'm_post_w_mlp': out['m_post_w_mlp'], 'v_w_ada': out['v_w_ada'], 'v_b_ada': out['v_b_ada'], 'v_pre_w_mix': out['v_pre_w_mix'], 'v_w_in': out['v_w_in'], 'v_attn_sinks': out['v_attn_sinks'], 'v_attn_out_w': out['v_attn_out_w'], 'v_lb_table': out['v_lb_table'], 'v_hg_norm_w': out['v_hg_norm_w'], 'v_w_out': out['v_w_out'], 'v_post_w_mix': out['v_post_w_mix'], 'v_pre_w_mlp': out['v_pre_w_mlp'], 'v_w_up': out['v_w_up'], 'v_w_down': out['v_w_down'], 'v_post_w_mlp': out['v_post_w_mlp']}


def _loss(weights, diff, rest, loss_target):
    with _jax.named_scope("forward"):
        args = {**rest, TWIN_DIFF_INPUT: diff, **{k: w.astype(_WEIGHT_DTYPES[k]) for k, w in weights.items()}}
        y = _forward(args)
    with _jax.named_scope("loss_head"):
        err = _jnp.square(y.astype(_jnp.float32) - loss_target)
        return 0.5 * _jnp.sum(_jnp.mean(err, axis=-1)) if err.ndim else 0.5 * err


def _adamw(w, g, m, v):
    m = ADAM_B1 * m + (1.0 - ADAM_B1) * g
    v = ADAM_B2 * v + (1.0 - ADAM_B2) * _jnp.square(g)
    m_hat = m / (1.0 - ADAM_B1 ** ADAM_STEP)
    v_hat = v / (1.0 - ADAM_B2 ** ADAM_STEP)
    delta = -ADAM_LR * (m_hat / (_jnp.sqrt(v_hat) + ADAM_EPS) + ADAM_WD * w)
    return delta, m, v


def reference(x, c, w_ada, b_ada, pre_w_mix, w_in, attn_sinks, attn_out_w, lb_table, hg_norm_w, w_out, post_w_mix, pre_w_mlp, w_up, w_down, post_w_mlp, loss_target, m_w_ada, m_b_ada, m_pre_w_mix, m_w_in, m_attn_sinks, m_attn_out_w, m_lb_table, m_hg_norm_w, m_w_out, m_post_w_mix, m_pre_w_mlp, m_w_up, m_w_down, m_post_w_mlp, v_w_ada, v_b_ada, v_pre_w_mix, v_w_in, v_attn_sinks, v_attn_out_w, v_lb_table, v_hg_norm_w, v_w_out, v_post_w_mix, v_pre_w_mlp, v_w_up, v_w_down, v_post_w_mlp):
    given = dict(x=x, c=c, w_ada=w_ada, b_ada=b_ada, pre_w_mix=pre_w_mix, w_in=w_in, attn_sinks=attn_sinks, attn_out_w=attn_out_w, lb_table=lb_table, hg_norm_w=hg_norm_w, w_out=w_out, post_w_mix=post_w_mix, pre_w_mlp=pre_w_mlp, w_up=w_up, w_down=w_down, post_w_mlp=post_w_mlp, loss_target=loss_target, m_w_ada=m_w_ada, m_b_ada=m_b_ada, m_pre_w_mix=m_pre_w_mix, m_w_in=m_w_in, m_attn_sinks=m_attn_sinks, m_attn_out_w=m_attn_out_w, m_lb_table=m_lb_table, m_hg_norm_w=m_hg_norm_w, m_w_out=m_w_out, m_post_w_mix=m_post_w_mix, m_pre_w_mlp=m_pre_w_mlp, m_w_up=m_w_up, m_w_down=m_w_down, m_post_w_mlp=m_post_w_mlp, v_w_ada=v_w_ada, v_b_ada=v_b_ada, v_pre_w_mix=v_pre_w_mix, v_w_in=v_w_in, v_attn_sinks=v_attn_sinks, v_attn_out_w=v_attn_out_w, v_lb_table=v_lb_table, v_hg_norm_w=v_hg_norm_w, v_w_out=v_w_out, v_post_w_mix=v_post_w_mix, v_pre_w_mlp=v_pre_w_mlp, v_w_up=v_w_up, v_w_down=v_w_down, v_post_w_mlp=v_post_w_mlp)
    weights = {n: given[n] for n in TWIN_WEIGHTS}
    shared = {n: given[n] for n in SHARED_INPUTS}
    per_example = {n: given[n] for n in ['x', 'c']}
    grad_fn = _jax.value_and_grad(_loss, argnums=(0, 1))

    def one_microbatch(ex, loss_target):
        ex = dict(ex)
        diff = ex.pop(TWIN_DIFF_INPUT)
        return grad_fn(weights, diff, {**shared, **ex}, loss_target)

    if N_MICROBATCH == 1:
        loss, (grad_w, grad_x) = one_microbatch(per_example, given["loss_target"])
    else:
        def body(carry, xs):
            loss_sum, grad_sum = carry
            l_k, (gw_k, gx_k) = one_microbatch(xs[0], xs[1])
            with _jax.named_scope("update"):
                return (loss_sum + l_k, _jax.tree.map(_jnp.add, grad_sum, gw_k)), gx_k

        init = (_jnp.zeros((), _jnp.float32), _jax.tree.map(_jnp.zeros_like, weights))
        (loss, grad_w), grad_x = _jax.lax.scan(body, init, (per_example, given["loss_target"]))
    with _jax.named_scope("update"):
        delta_w, new_m, new_v = {}, {}, {}
        for n in TWIN_WEIGHTS:
            delta_w[n], new_m[n], new_v[n] = _adamw(weights[n], grad_w[n], given["m_" + n], given["v_" + n])
    return (loss, grad_x, *[grad_w[n] for n in TWIN_WEIGHTS], *[delta_w[n] for n in TWIN_WEIGHTS],
            *[new_m[n] for n in TWIN_WEIGHTS], *[new_v[n] for n in TWIN_WEIGHTS])
```

```python
import functools

import jax
import jax.numpy as jnp
from jax import lax
from jax.experimental import pallas as pl
from jax.experimental.pallas import tpu as pltpu

F32 = jnp.float32
BF16 = jnp.bfloat16

D_MODEL = 1024
ATT_WIDTH = 512
ATT_HEAD_DIM = 64
ATT_Q_HEADS = 8
ATT_KV_HEADS = 2
ATT_GROUP = ATT_Q_HEADS // ATT_KV_HEADS
ATT_KV_COLS = ATT_KV_HEADS * ATT_HEAD_DIM
WINDOW = 128
ROPE_DIM = 16
ROPE_THETA = 500000.0
HG_WIDTH = 512
HG_HEAD_DIM = 128
HG_HEADS = 4
HG_CHUNK = 32
IN_COLS = ATT_WIDTH + 2 * ATT_KV_COLS + 4 * HG_WIDTH
ATT_COLS = ATT_WIDTH + 2 * ATT_KV_COLS
D_FF = 4 * D_MODEL
N_MOD = 6
EPS = 1e-6
ATT_SCALE = ATT_HEAD_DIM ** -0.5

ADAM_LR = 0.001
ADAM_B1 = 0.9
ADAM_B2 = 0.999
ADAM_EPS = 1e-08
ADAM_WD = 0.01
ADAM_STEP = 10

N_CHIPS = 4
N_DEV = 8
LANE = 128
VMEM_LIMIT = 48 * 1024 * 1024
MESH = pl.DeviceIdType.MESH

NT_DIMS = (((1,), (1,)), ((), ()))
TN_DIMS = (((0,), (0,)), ((), ()))


def _sds(shape, dtype):
    return jax.ShapeDtypeStruct(tuple(shape), dtype)


def _params(*sem):
    return pltpu.CompilerParams(dimension_semantics=sem, vmem_limit_bytes=VMEM_LIMIT)


def _sigmoid(x):
    return 1.0 / (1.0 + jnp.exp(-x))


def _dot(a, b, dims=None):
    a, b = a.astype(BF16), b.astype(BF16)
    if dims is None:
        return jnp.dot(a, b, preferred_element_type=F32)
    return lax.dot_general(a, b, dims, preferred_element_type=F32)


def _rms_fwd(x, w):
    rstd = lax.rsqrt(jnp.mean(x * x, axis=-1, keepdims=True) + EPS)
    xh = x * rstd
    return xh * w, xh, rstd


def _rms_bwd(dy, xh, rstd, w):
    dxh = dy * w
    dx = rstd * (dxh - xh * jnp.mean(dxh * xh, axis=-1, keepdims=True))
    return dx, dy * xh


def _colsum(x):
    return jnp.sum(x, axis=0, keepdims=True)


def _mm(a, b, *, name, out_dtype, trans_b=False, tm=512, tn=None, nk=1, a_fn=None, extra=(), epi=None,
        b_spec=None, n_out=None, k_total=None):
    m_total = a.shape[0]
    k_total = a.shape[1] if k_total is None else k_total
    tk = k_total // nk
    if n_out is None:
        n_out = b.shape[0] if trans_b else b.shape[1]
    tn = n_out if tn is None else tn
    grid = (m_total // tm, n_out // tn, nk)
    dims = NT_DIMS if trans_b else None

    def body(*refs):
        a_ref, b_ref = refs[0], refs[1]
        extra_refs = refs[2:2 + len(extra)]
        o_ref = refs[2 + len(extra)]
        av = a_ref[...]
        if a_fn is not None:
            av = a_fn(av.astype(F32))
        part = _dot(av, b_ref[...], dims)

        def finish(acc):
            if epi is not None:
                acc = epi(acc, *[r[...] for r in extra_refs])
            o_ref[...] = acc.astype(out_dtype)

        if nk == 1:
            finish(part)
        else:
            acc_ref = refs[-1]
            k = pl.program_id(2)

            @pl.when(k == 0)
            def _():
                acc_ref[...] = part

            @pl.when(k > 0)
            def _():
                acc_ref[...] += part

            @pl.when(k == nk - 1)
            def _():
                finish(acc_ref[...])

    if b_spec is None:
        if trans_b:
            b_spec = pl.BlockSpec((tn, tk), lambda i, j, k: (j, k))
        else:
            b_spec = pl.BlockSpec((tk, tn), lambda i, j, k: (k, j))
    in_specs = [pl.BlockSpec((tm, tk), lambda i, j, k: (i, k)), b_spec]
    in_specs += [pl.BlockSpec((tm, tn), lambda i, j, k: (i, j)) for _ in extra]
    return pl.pallas_call(
        body, name=name, grid=grid, in_specs=in_specs,
        out_specs=pl.BlockSpec((tm, tn), lambda i, j, k: (i, j)),
        out_shape=_sds((m_total, n_out), out_dtype),
        scratch_shapes=[pltpu.VMEM((tm, tn), F32)] if nk > 1 else [],
        compiler_params=_params("parallel", "parallel", "arbitrary"),
    )(a, b, *extra)


def _mm_tn(a, b, *, name, tk, tn, tm=512, a_fn=None, out_shape=None, out_spec=None):
    m_total, k_total = a.shape
    n_total = b.shape[1]
    nm = m_total // tm
    grid = (k_total // tk, n_total // tn, nm)

    def body(a_ref, b_ref, o_ref):
        av = a_ref[...]
        if a_fn is not None:
            av = a_fn(av.astype(F32))
        part = _dot(av, b_ref[...], TN_DIMS)
        m = pl.program_id(2)

        @pl.when(m == 0)
        def _():
            o_ref[...] = part

        @pl.when(m > 0)
        def _():
            o_ref[...] += part

    if out_shape is None:
        out_shape = _sds((k_total, n_total), F32)
        out_spec = pl.BlockSpec((tk, tn), lambda i, j, m: (i, j))
    return pl.pallas_call(
        body, name=name, grid=grid,
        in_specs=[pl.BlockSpec((tm, tk), lambda i, j, m: (m, i)), pl.BlockSpec((tm, tn), lambda i, j, m: (m, j))],
        out_specs=out_spec, out_shape=out_shape,
        compiler_params=_params("parallel", "parallel", "arbitrary"),
    )(a, b)


def _ada_fwd(c_all, w_shard, b_shard):
    nb, ncol = c_all.shape[0], w_shard.shape[1]
    tn = 512

    def body(c_ref, w_ref, b_ref, o_ref):
        c = c_ref[...]
        o_ref[...] = _dot(c * _sigmoid(c), w_ref[...]) + b_ref[...]

    return pl.pallas_call(
        body, name="ada_fwd", grid=(ncol // tn,),
        in_specs=[pl.BlockSpec((nb, D_MODEL), lambda j: (0, 0)), pl.BlockSpec((D_MODEL, tn), lambda j: (0, j)),
                  pl.BlockSpec((1, tn), lambda j: (0, j))],
        out_specs=pl.BlockSpec((nb, tn), lambda j: (0, j)), out_shape=_sds((nb, ncol), F32),
        compiler_params=_params("parallel"),
    )(c_all, w_shard, b_shard)


def _adamw_math(g, w, m, v):
    m = ADAM_B1 * m + (1.0 - ADAM_B1) * g
    v = ADAM_B2 * v + (1.0 - ADAM_B2) * (g * g)
    m_hat = m / (1.0 - ADAM_B1 ** ADAM_STEP)
    v_hat = v / (1.0 - ADAM_B2 ** ADAM_STEP)
    delta = -ADAM_LR * (m_hat / (jnp.sqrt(v_hat) + ADAM_EPS) + ADAM_WD * w)
    return delta, m, v


def _ada_bwd_adamw(c_all, dmod_cols, w, m, v):
    nb, ncol = dmod_cols.shape
    tn = 256

    def body(c_ref, d_ref, w_ref, m_ref, v_ref, g_ref, dl_ref, nm_ref, nv_ref):
        c = c_ref[...]
        g = _dot(c * _sigmoid(c), d_ref[...], TN_DIMS)
        g_ref[...] = g
        dl_ref[...], nm_ref[...], nv_ref[...] = _adamw_math(g, w_ref[...], m_ref[...], v_ref[...])

    col = pl.BlockSpec((D_MODEL, tn), lambda j: (0, j))
    shp = _sds((D_MODEL, ncol), F32)
    return pl.pallas_call(
        body, name="ada_bwd_adamw", grid=(ncol // tn,),
        in_specs=[pl.BlockSpec((nb, D_MODEL), lambda j: (0, 0)), pl.BlockSpec((nb, tn), lambda j: (0, j)), col, col, col],
        out_specs=[col, col, col, col], out_shape=[shp, shp, shp, shp],
        compiler_params=_params("parallel"),
    )(c_all, dmod_cols, w, m, v)


def _adamw(g, w, m, v, *, name):
    rows, cols = g.shape
    tr = min(rows, 256)

    def body(g_ref, w_ref, m_ref, v_ref, dl_ref, nm_ref, nv_ref):
        dl_ref[...], nm_ref[...], nv_ref[...] = _adamw_math(g_ref[...], w_ref[...], m_ref[...], v_ref[...])

    blk = pl.BlockSpec((tr, cols), lambda i: (i, 0))
    shp = _sds((rows, cols), F32)
    return pl.pallas_call(
        body, name=name, grid=(rows // tr,), in_specs=[blk] * 4, out_specs=[blk] * 3, out_shape=[shp] * 3,
        compiler_params=_params("parallel"),
    )(g, w, m, v)


def _tok_spec(tm, width=D_MODEL):
    return pl.BlockSpec((None, tm, width), lambda b, i: (b, i, 0))


def _row_spec(width=D_MODEL):
    return pl.BlockSpec((None, 1, width), lambda b, i: (b, 0, 0))


def _vec_spec(width=D_MODEL):
    return pl.BlockSpec((1, width), lambda b, i: (0, 0))


def _prenorm1(x, w, sc, sh, tm=512):
    bsz, seq, _ = x.shape

    def body(x_ref, w_ref, sc_ref, sh_ref, h_ref):
        y, _, _ = _rms_fwd(x_ref[...], w_ref[...])
        h_ref[...] = (y * (1.0 + sc_ref[...]) + sh_ref[...]).astype(BF16)

    return pl.pallas_call(
        body, name="prenorm1", grid=(bsz, seq // tm),
        in_specs=[_tok_spec(tm), _vec_spec(), _row_spec(), _row_spec()],
        out_specs=_tok_spec(tm), out_shape=_sds(x.shape, BF16),
        compiler_params=_params("parallel", "parallel"),
    )(x, w, sc, sh)


def _rope_tables(seq):
    half = ROPE_DIM // 2
    inv_freq = ROPE_THETA ** (-jnp.arange(0, ROPE_DIM, 2, dtype=F32) / ROPE_DIM)
    ang = jnp.arange(seq, dtype=F32)[:, None] * inv_freq[None, :]
    cos, sin = jnp.cos(ang), jnp.sin(ang)
    rest = ATT_HEAD_DIM - ROPE_DIM
    ones, zeros, zh = jnp.ones((seq, rest), F32), jnp.zeros((seq, rest), F32), jnp.zeros((seq, half), F32)
    reps = LANE // ATT_HEAD_DIM
    t_cos = jnp.tile(jnp.concatenate([cos, cos, ones], axis=1), (1, reps))
    t_up = jnp.tile(jnp.concatenate([zh, sin, zeros], axis=1), (1, reps))
    t_dn = jnp.tile(jnp.concatenate([-sin, zh, zeros], axis=1), (1, reps))
    return t_cos, t_up, t_dn


def _rope_fwd(proj, tables, tm=256):
    bsz, seq, _ = proj.shape
    half = ROPE_DIM // 2

    def body(p_ref, c_ref, u_ref, d_ref, q_ref, k_ref, v_ref):
        c, u, d = c_ref[...], u_ref[...], d_ref[...]

        def rope(x):
            return x * c + pltpu.roll(x, half, 1) * u + pltpu.roll(x, LANE - half, 1) * d

        for s in range(ATT_WIDTH // LANE):
            q_ref[:, s * LANE:(s + 1) * LANE] = rope(p_ref[:, s * LANE:(s + 1) * LANE]).astype(BF16)
        k_ref[...] = rope(p_ref[:, ATT_WIDTH:ATT_WIDTH + LANE]).astype(BF16)
        v_ref[...] = p_ref[:, ATT_WIDTH + LANE:ATT_COLS].astype(BF16)

    tab = pl.BlockSpec((tm, LANE), lambda b, i: (i, 0))
    return pl.pallas_call(
        body, name="rope_fwd", grid=(bsz, seq // tm),
        in_specs=[_tok_spec(tm, ATT_COLS), tab, tab, tab],
        out_specs=[_tok_spec(tm, ATT_WIDTH), _tok_spec(tm, LANE), _tok_spec(tm, LANE)],
        out_shape=[_sds((bsz, seq, ATT_WIDTH), BF16), _sds((bsz, seq, LANE), BF16), _sds((bsz, seq, LANE), BF16)],
        compiler_params=_params("parallel", "parallel"),
    )(proj, *tables)


def _band_masks(i):
    row = lax.broadcasted_iota(jnp.int32, (WINDOW, WINDOW), 0)
    col = lax.broadcasted_iota(jnp.int32, (WINDOW, WINDOW), 1)
    return col <= row, jnp.logical_and(col > row, i > 0)


def _prev_spec(width):
    return pl.BlockSpec((None, WINDOW, width), lambda b, i: (b, jnp.maximum(i - 1, 0), 0))


def _attn_fwd(qr, kr, vb, sinks, w_norm):
    bsz, seq, _ = qr.shape
    nblk = seq // WINDOW
    neg = float(jnp.finfo(jnp.float32).min)

    def body(sink_ref, q_ref, kc_ref, kp_ref, vc_ref, vp_ref, w_ref, raw_ref, an_ref, l_ref):
        mask_c, mask_p = _band_masks(pl.program_id(1))
        for h in range(ATT_Q_HEADS):
            g = h // ATT_GROUP
            hs = slice(h * ATT_HEAD_DIM, (h + 1) * ATT_HEAD_DIM)
            gs = slice(g * ATT_HEAD_DIM, (g + 1) * ATT_HEAD_DIM)
            q = q_ref[:, hs]
            sink = sink_ref[0, h]
            sc = jnp.where(mask_c, _dot(q, kc_ref[:, gs], NT_DIMS) * ATT_SCALE, neg)
            sp = jnp.where(mask_p, _dot(q, kp_ref[:, gs], NT_DIMS) * ATT_SCALE, neg)
            m = jnp.maximum(jnp.maximum(jnp.max(sc, axis=-1, keepdims=True), jnp.max(sp, axis=-1, keepdims=True)), sink)
            pc = jnp.where(mask_c, jnp.exp(sc - m), 0.0)
            pp = jnp.where(mask_p, jnp.exp(sp - m), 0.0)
            den = jnp.sum(pc, axis=-1, keepdims=True) + jnp.sum(pp, axis=-1, keepdims=True) + jnp.exp(sink - m)
            raw_ref[:, hs] = _dot(pc / den, vc_ref[:, gs]) + _dot(pp / den, vp_ref[:, gs])
            l_ref[:, h:h + 1] = m + jnp.log(den)
        y, _, _ = _rms_fwd(raw_ref[...], w_ref[...])
        an_ref[...] = y.astype(BF16)

    cur = lambda width: pl.BlockSpec((None, WINDOW, width), lambda b, i: (b, i, 0))
    return pl.pallas_call(
        body, name="attn_fwd", grid=(bsz, nblk),
        in_specs=[pl.BlockSpec(memory_space=pltpu.SMEM), cur(ATT_WIDTH), cur(LANE), _prev_spec(LANE), cur(LANE), _prev_spec(LANE),
                  _vec_spec(ATT_WIDTH)],
        out_specs=[cur(ATT_WIDTH), cur(ATT_WIDTH), cur(ATT_Q_HEADS)],
        out_shape=[_sds((bsz, seq, ATT_WIDTH), F32), _sds((bsz, seq, ATT_WIDTH), BF16), _sds((bsz, seq, ATT_Q_HEADS), F32)],
        compiler_params=_params("parallel", "parallel"),
    )(sinks, qr, kr, kr, vb, vb, w_norm)


HG_Q0 = ATT_COLS // LANE
HG_F0 = HG_Q0 + HG_HEADS
HG_I0 = HG_F0 + HG_HEADS
HG_G0 = HG_I0 + HG_HEADS
HG_TOK = 256
HG_NCH = HG_TOK // HG_CHUNK


def _chunk_consts():
    row = lax.broadcasted_iota(jnp.int32, (HG_CHUNK, HG_CHUNK), 0)
    col = lax.broadcasted_iota(jnp.int32, (HG_CHUNK, HG_CHUNK), 1)
    return row >= col, row <= col


def _cumsum_rows(tri_f32, x):
    return jnp.dot(tri_f32, x, precision=lax.Precision.HIGHEST, preferred_element_type=F32)


def _hgrn_gates(tbl, hf, hq):
    lb = _sigmoid(tbl[1:2] - tbl[0:1])
    sig = _sigmoid(hf)
    f = lb + (1.0 - lb) * sig
    sq = _sigmoid(hq)
    return lb, sig, f, sq


def _hgrn_fwd(proj, lb_table, norm_w):
    bsz, seq, _ = proj.shape
    nstep = seq // HG_TOK

    def body(tbl_ref, nw_ref, q_ref, f_ref, i_ref, g_ref, o_ref, rec_ref, st_ref, s_scr):
        @pl.when(pl.program_id(2) == 0)
        def _():
            s_scr[...] = jnp.zeros_like(s_scr)

        lower, _ = _chunk_consts()
        tri = lower.astype(F32)
        nw = nw_ref[...]
        for j in range(HG_NCH):
            sl = slice(j * HG_CHUNK, (j + 1) * HG_CHUNK)
            hq, v = q_ref[sl, :], i_ref[sl, :]
            _, _, f, sq = _hgrn_gates(tbl_ref[...], f_ref[sl, :], hq)
            q, k = hq * sq, 1.0 - f
            b = _cumsum_rows(tri, jnp.log(f))
            bl = b[HG_CHUNK - 1:HG_CHUNK, :]
            qd = q * jnp.exp(b)
            kd = k * jnp.exp(-b)
            ku = k * jnp.exp(bl - b)
            a = jnp.where(lower, _dot(qd, kd, NT_DIMS), 0.0)
            st = s_scr[...]
            st_ref[j] = st
            o = _dot(a, v) + _dot(qd, st, NT_DIMS)
            s_scr[...] = st * jnp.exp(bl) + _dot(v, ku, TN_DIMS)
            o_ref[sl, :] = o
            y, _, _ = _rms_fwd(o, nw)
            hg = g_ref[sl, :]
            rec_ref[sl, :] = (y * (hg * _sigmoid(hg))).astype(BF16)

    slab = lambda first: pl.BlockSpec((None, HG_TOK, LANE), lambda b, h, t: (b, t, first + h))
    head_out = pl.BlockSpec((None, HG_TOK, LANE), lambda b, h, t: (b, t, h))
    return pl.pallas_call(
        body, name="hgrn_fwd", grid=(bsz, HG_HEADS, nstep),
        in_specs=[pl.BlockSpec((2, LANE), lambda b, h, t: (0, h)), pl.BlockSpec((1, LANE), lambda b, h, t: (0, 0)),
                  slab(HG_Q0), slab(HG_F0), slab(HG_I0), slab(HG_G0)],
        out_specs=[head_out, head_out,
                   pl.BlockSpec((None, None, HG_NCH, LANE, LANE), lambda b, h, t: (b, h, t, 0, 0))],
        out_shape=[_sds((bsz, seq, HG_WIDTH), F32), _sds((bsz, seq, HG_WIDTH), BF16),
                   _sds((bsz, HG_HEADS, seq // HG_CHUNK, LANE, LANE), F32)],
        scratch_shapes=[pltpu.VMEM((LANE, LANE), F32)],
        compiler_params=_params("parallel", "parallel", "arbitrary"),
    )(lb_table, norm_w, proj, proj, proj, proj)


def _mid_fwd(x, mix, post_w, g1, pre_w, sc2, sh2, tm=512):
    bsz, seq, _ = x.shape

    def body(x_ref, mix_ref, pw_ref, g1_ref, w2_ref, sc_ref, sh_ref, x1_ref, h2_ref):
        n1, _, _ = _rms_fwd(mix_ref[...], pw_ref[...])
        x1 = x_ref[...] + g1_ref[...] * n1
        x1_ref[...] = x1
        y2, _, _ = _rms_fwd(x1, w2_ref[...])
        h2_ref[...] = (y2 * (1.0 + sc_ref[...]) + sh_ref[...]).astype(BF16)

    return pl.pallas_call(
        body, name="mid_fwd", grid=(bsz, seq // tm),
        in_specs=[_tok_spec(tm), _tok_spec(tm), _vec_spec(), _row_spec(), _vec_spec(), _row_spec(), _row_spec()],
        out_specs=[_tok_spec(tm), _tok_spec(tm)], out_shape=[_sds(x.shape, F32), _sds(x.shape, BF16)],
        compiler_params=_params("parallel", "parallel"),
    )(x, mix, post_w, g1, pre_w, sc2, sh2)


def _acc_out(ref, first, value):
    @pl.when(first)
    def _():
        ref[...] = value

    @pl.when(jnp.logical_not(first))
    def _():
        ref[...] += value


def _loss_bwd(x1, down, post_w, g2, target, tm=512):
    bsz, seq, _ = x1.shape

    def body(x1_ref, d_ref, w_ref, g2_ref, t_ref, loss_ref, dy_ref, dd_ref, dg2_ref, dw_ref):
        b, i = pl.program_id(0), pl.program_id(1)
        w, g2v = w_ref[...], g2_ref[...]
        n2, dh, rstd = _rms_fwd(d_ref[...], w)
        err = x1_ref[...] + g2v * n2 - t_ref[...]
        part = (0.5 / D_MODEL) * jnp.sum(jnp.sum(err * err, axis=-1, keepdims=True), axis=0, keepdims=True)
        _acc_out(loss_ref, jnp.logical_and(b == 0, i == 0), jnp.broadcast_to(part, (1, LANE)))
        dy = err * (1.0 / D_MODEL)
        dy_ref[...] = dy
        _acc_out(dg2_ref, i == 0, _colsum(dy * n2))
        dd, dw_rows = _rms_bwd(dy * g2v, dh, rstd, w)
        dd_ref[...] = dd.astype(BF16)
        _acc_out(dw_ref, jnp.logical_and(b == 0, i == 0), _colsum(dw_rows))

    return pl.pallas_call(
        body, name="loss_bwd", grid=(bsz, seq // tm),
        in_specs=[_tok_spec(tm), _tok_spec(tm), _vec_spec(), _row_spec(), _tok_spec(tm)],
        out_specs=[_vec_spec(LANE), _tok_spec(tm), _tok_spec(tm), _row_spec(), _vec_spec()],
        out_shape=[_sds((1, LANE), F32), _sds(x1.shape, F32), _sds(x1.shape, BF16), _sds((bsz, 1, D_MODEL), F32),
                   _sds((1, D_MODEL), F32)],
        compiler_params=_params("arbitrary", "arbitrary"),
    )(x1, down, post_w, g2, target)


def _mid_bwd(dh2, dy, x1, mix, pre_w, sc2, post_w, g1, tm=512):
    bsz, seq, _ = x1.shape

    def body(dh2_ref, dy_ref, x1_ref, mix_ref, w2_ref, sc_ref, pw_ref, g1_ref,
             dx1_ref, dmix_ref, dsc_ref, dsh_ref, dg1_ref, dw2_ref, dpw_ref):
        b, i = pl.program_id(0), pl.program_id(1)
        first = jnp.logical_and(b == 0, i == 0)
        w2, pw = w2_ref[...], pw_ref[...]
        dh2v = dh2_ref[...]
        y2, xh2, rstd2 = _rms_fwd(x1_ref[...], w2)
        _acc_out(dsh_ref, i == 0, _colsum(dh2v))
        _acc_out(dsc_ref, i == 0, _colsum(dh2v * y2))
        dx1n, dw_rows = _rms_bwd(dh2v * (1.0 + sc_ref[...]), xh2, rstd2, w2)
        _acc_out(dw2_ref, first, _colsum(dw_rows))
        dx1 = dy_ref[...] + dx1n
        dx1_ref[...] = dx1
        n1, mh, rstd1 = _rms_fwd(mix_ref[...], pw)
        _acc_out(dg1_ref, i == 0, _colsum(dx1 * n1))
        dmix, dpw_rows = _rms_bwd(dx1 * g1_ref[...], mh, rstd1, pw)
        dmix_ref[...] = dmix.astype(BF16)
        _acc_out(dpw_ref, first, _colsum(dpw_rows))

    row_shape = _sds((bsz, 1, D_MODEL), F32)
    vec_shape = _sds((1, D_MODEL), F32)
    return pl.pallas_call(
        body, name="mid_bwd", grid=(bsz, seq // tm),
        in_specs=[_tok_spec(tm), _tok_spec(tm), _tok_spec(tm), _tok_spec(tm), _vec_spec(), _row_spec(), _vec_spec(), _row_spec()],
        out_specs=[_tok_spec(tm), _tok_spec(tm), _row_spec(), _row_spec(), _row_spec(), _vec_spec(), _vec_spec()],
        out_shape=[_sds(x1.shape, F32), _sds(x1.shape, BF16), row_shape, row_shape, row_shape, vec_shape, vec_shape],
        compiler_params=_params("arbitrary", "arbitrary"),
    )(dh2, dy, x1, mix, pre_w, sc2, post_w, g1)


def _norm1_bwd(dh1, dx1, x, pre_w, sc1, tm=512):
    bsz, seq, _ = x.shape

    def body(dh_ref, dx1_ref, x_ref, w_ref, sc_ref, gx_ref, dsc_ref, dsh_ref, dw_ref):
        b, i = pl.program_id(0), pl.program_id(1)
        w = w_ref[...]
        dh = dh_ref[...]
        y, xh, rstd = _rms_fwd(x_ref[...], w)
        _acc_out(dsh_ref, i == 0, _colsum(dh))
        _acc_out(dsc_ref, i == 0, _colsum(dh * y))
        dx, dw_rows = _rms_bwd(dh * (1.0 + sc_ref[...]), xh, rstd, w)
        _acc_out(dw_ref, jnp.logical_and(b == 0, i == 0), _colsum(dw_rows))
        gx_ref[...] = dx1_ref[...] + dx

    row_shape = _sds((bsz, 1, D_MODEL), F32)
    return pl.pallas_call(
        body, name="norm1_bwd", grid=(bsz, seq // tm),
        in_specs=[_tok_spec(tm), _tok_spec(tm), _tok_spec(tm), _vec_spec(), _row_spec()],
        out_specs=[_tok_spec(tm), _row_spec(), _row_spec(), _vec_spec()],
        out_shape=[_sds(x.shape, F32), row_shape, row_shape, _sds((1, D_MODEL), F32)],
        compiler_params=_params("arbitrary", "arbitrary"),
    )(dh1, dx1, x, pre_w, sc1)


def _hgrn_bwd(dcat, proj, o_raw, states, lb_table, norm_w):
    bsz, seq, _ = proj.shape
    nstep = seq // HG_TOK
    rec0 = ATT_WIDTH // LANE

    def body(tbl_ref, nw_ref, dr_ref, q_ref, f_ref, i_ref, g_ref, o_ref, st_ref,
             dq_ref, df_ref, di_ref, dg_ref, dlb_ref, dnw_ref, ds_scr):
        h, b, t = pl.program_id(0), pl.program_id(1), pl.program_id(2)

        @pl.when(t == 0)
        def _():
            ds_scr[...] = jnp.zeros_like(ds_scr)

        lower, upper = _chunk_consts()
        tri_lo, tri_up = lower.astype(F32), upper.astype(F32)
        last_row = lax.broadcasted_iota(jnp.int32, (HG_CHUNK, LANE), 0) == HG_CHUNK - 1
        nw = nw_ref[...]
        dlb_acc = jnp.zeros((1, LANE), F32)
        dnw_acc = jnp.zeros((1, LANE), F32)
        for j in reversed(range(HG_NCH)):
            sl = slice(j * HG_CHUNK, (j + 1) * HG_CHUNK)
            hq, v, hg = q_ref[sl, :], i_ref[sl, :], g_ref[sl, :]
            lb, sig, f, sq = _hgrn_gates(tbl_ref[...], f_ref[sl, :], hq)
            q, k = hq * sq, 1.0 - f
            b_cum = _cumsum_rows(tri_lo, jnp.log(f))
            bl = b_cum[HG_CHUNK - 1:HG_CHUNK, :]
            e_b, e_nb, e_bl, e_rem = jnp.exp(b_cum), jnp.exp(-b_cum), jnp.exp(bl), jnp.exp(bl - b_cum)
            qd, kd, ku = q * e_b, k * e_nb, k * e_rem
            st = st_ref[j]
            y, on, rstd = _rms_fwd(o_ref[sl, :], nw)
            sg = _sigmoid(hg)
            dr = dr_ref[sl, :]
            dg_ref[sl, :] = (dr * y * (sg * (1.0 + hg * (1.0 - sg)))).astype(BF16)
            do, dnw_rows = _rms_bwd(dr * (hg * sg), on, rstd, nw)
            dnw_acc += _colsum(dnw_rows)
            dsp = ds_scr[...]
            at = jnp.where(upper, _dot(kd, qd, NT_DIMS), 0.0)
            da = jnp.where(lower, _dot(do, v, NT_DIMS), 0.0)
            dat = jnp.where(upper, _dot(v, do, NT_DIMS), 0.0)
            dv = _dot(at, do) + _dot(ku, dsp, NT_DIMS)
            dqd = _dot(da, kd) + _dot(do, st)
            dkd = _dot(dat, qd)
            dku = _dot(v, dsp)
            dbl = _colsum(st * dsp) * e_bl + _colsum(dku * ku)
            ds_scr[...] = _dot(do, qd, TN_DIMS) + dsp * e_bl
            dk = dkd * e_nb + dku * e_rem
            db = dqd * qd - dkd * kd - dku * ku
            db = db + jnp.where(last_row, dbl, 0.0)
            dlogf = _cumsum_rows(tri_up, db)
            dfv = dlogf / f - dk
            df_ref[sl, :] = (dfv * (1.0 - lb) * sig * (1.0 - sig)).astype(BF16)
            dlb_acc += _colsum(dfv * (1.0 - sig))
            dq_ref[sl, :] = (dqd * e_b * (sq * (1.0 + hq * (1.0 - sq)))).astype(BF16)
            di_ref[sl, :] = dv.astype(BF16)
        _acc_out(dlb_ref, jnp.logical_and(b == 0, t == 0), dlb_acc)
        _acc_out(dnw_ref, jnp.logical_and(h == 0, jnp.logical_and(b == 0, t == 0)), dnw_acc)

    rev = lambda t: nstep - 1 - t
    slab = lambda first: pl.BlockSpec((None, HG_TOK, LANE), lambda h, b, t: (b, rev(t), first + h))
    head = pl.BlockSpec((None, HG_TOK, LANE), lambda h, b, t: (b, rev(t), h))
    grad_shape = _sds((bsz, seq, HG_WIDTH), BF16)
    return pl.pallas_call(
        body, name="hgrn_bwd", grid=(HG_HEADS, bsz, nstep),
        in_specs=[pl.BlockSpec((2, LANE), lambda h, b, t: (0, h)), pl.BlockSpec((1, LANE), lambda h, b, t: (0, 0)),
                  slab(rec0), slab(HG_Q0), slab(HG_F0), slab(HG_I0), slab(HG_G0), head,
                  pl.BlockSpec((None, None, HG_NCH, LANE, LANE), lambda h, b, t: (b, h, rev(t), 0, 0))],
        out_specs=[head, head, head, head, pl.BlockSpec((1, LANE), lambda h, b, t: (0, h)),
                   pl.BlockSpec((1, LANE), lambda h, b, t: (0, 0))],
        out_shape=[grad_shape, grad_shape, grad_shape, grad_shape, _sds((1, HG_WIDTH), F32), _sds((1, LANE), F32)],
        scratch_shapes=[pltpu.VMEM((LANE, LANE), F32)],
        compiler_params=_params("arbitrary", "arbitrary", "arbitrary"),
    )(lb_table, norm_w, dcat, proj, proj, proj, proj, o_raw, states)


def _attn_bwd(dcat, raw, w_norm, qr, kr, vb, lse, sinks):
    bsz, seq, _ = qr.shape
    nblk = seq // WINDOW

    def body(sink_ref, da_ref, raw_ref, w_ref, q_ref, kc_ref, kp_ref, vc_ref, vp_ref, l_ref,
             dq_ref, dkd_ref, dkp_ref, dvd_ref, dvp_ref, dw_ref, dsink_ref):
        b, i = pl.program_id(0), pl.program_id(1)
        first = jnp.logical_and(b == 0, i == 0)
        mask_c, mask_p = _band_masks(i)
        raw_v = raw_ref[...]
        w = w_ref[...]
        _, on, rstd = _rms_fwd(raw_v, w)
        do_all, dw_rows = _rms_bwd(da_ref[...], on, rstd, w)
        _acc_out(dw_ref, first, _colsum(dw_rows))
        lane8 = lax.broadcasted_iota(jnp.int32, (1, ATT_Q_HEADS), 1)
        dsink = jnp.zeros((1, ATT_Q_HEADS), F32)
        for g in range(ATT_KV_HEADS):
            gs = slice(g * ATT_HEAD_DIM, (g + 1) * ATT_HEAD_DIM)
            kc, kp, vc, vp = kc_ref[:, gs], kp_ref[:, gs], vc_ref[:, gs], vp_ref[:, gs]
            dkc = jnp.zeros((WINDOW, ATT_HEAD_DIM), F32)
            dkp, dvc, dvp = dkc, dkc, dkc
            for hh in range(ATT_GROUP):
                h = g * ATT_GROUP + hh
                hs = slice(h * ATT_HEAD_DIM, (h + 1) * ATT_HEAD_DIM)
                q = q_ref[:, hs]
                doh = do_all[:, hs]
                dsum = jnp.sum(doh * raw_v[:, hs], axis=-1, keepdims=True)
                lse_h = l_ref[:, h:h + 1]
                pc = jnp.where(mask_c, jnp.exp(_dot(q, kc, NT_DIMS) * ATT_SCALE - lse_h), 0.0)
                pp = jnp.where(mask_p, jnp.exp(_dot(q, kp, NT_DIMS) * ATT_SCALE - lse_h), 0.0)
                p_sink = jnp.exp(sink_ref[0, h] - lse_h)
                dsink = dsink - jnp.where(lane8 == h, jnp.sum(p_sink * dsum, axis=0, keepdims=True), 0.0)
                dsc = pc * (_dot(doh, vc, NT_DIMS) - dsum) * ATT_SCALE
                dsp = pp * (_dot(doh, vp, NT_DIMS) - dsum) * ATT_SCALE
                dq_ref[:, hs] = _dot(dsc, kc) + _dot(dsp, kp)
                dkc += _dot(dsc, q, TN_DIMS)
                dkp += _dot(dsp, q, TN_DIMS)
                dvc += _dot(pc, doh, TN_DIMS)
                dvp += _dot(pp, doh, TN_DIMS)
            dkd_ref[:, gs], dkp_ref[:, gs], dvd_ref[:, gs], dvp_ref[:, gs] = dkc, dkp, dvc, dvp
        _acc_out(dsink_ref, first, dsink)

    cur = lambda width: pl.BlockSpec((None, WINDOW, width), lambda b, i: (b, i, 0))
    kv_shape = _sds((bsz, seq, LANE), F32)
    return pl.pallas_call(
        body, name="attn_bwd", grid=(bsz, nblk),
        in_specs=[pl.BlockSpec(memory_space=pltpu.SMEM), cur(ATT_WIDTH), cur(ATT_WIDTH), _vec_spec(ATT_WIDTH), cur(ATT_WIDTH),
                  cur(LANE), _prev_spec(LANE), cur(LANE), _prev_spec(LANE), cur(ATT_Q_HEADS)],
        out_specs=[cur(ATT_WIDTH), cur(LANE), cur(LANE), cur(LANE), cur(LANE), _vec_spec(ATT_WIDTH), _vec_spec(ATT_Q_HEADS)],
        out_shape=[_sds((bsz, seq, ATT_WIDTH), F32), kv_shape, kv_shape, kv_shape, kv_shape, _sds((1, ATT_WIDTH), F32),
                   _sds((1, ATT_Q_HEADS), F32)],
        compiler_params=_params("arbitrary", "arbitrary"),
    )(sinks, dcat, raw, w_norm, qr, kr, kr, vb, vb, lse)


def _rope_bwd(dq, dkd, dkp, dvd, dvp, tables):
    bsz, seq, _ = dq.shape
    nblk = seq // WINDOW
    half = ROPE_DIM // 2

    def body(dq_ref, dkd_ref, dkp_ref, dvd_ref, dvp_ref, c_ref, u_ref, d_ref, o_ref):
        c, u, d = c_ref[...], u_ref[...], d_ref[...]
        has_next = pl.program_id(1) < nblk - 1

        def unrope(g):
            return g * c + pltpu.roll(g * u, LANE - half, 1) + pltpu.roll(g * d, half, 1)

        for s in range(ATT_WIDTH // LANE):
            o_ref[:, s * LANE:(s + 1) * LANE] = unrope(dq_ref[:, s * LANE:(s + 1) * LANE]).astype(BF16)
        dk = dkd_ref[...] + jnp.where(has_next, dkp_ref[...], 0.0)
        o_ref[:, ATT_WIDTH:ATT_WIDTH + LANE] = unrope(dk).astype(BF16)
        o_ref[:, ATT_WIDTH + LANE:ATT_COLS] = (dvd_ref[...] + jnp.where(has_next, dvp_ref[...], 0.0)).astype(BF16)

    cur = lambda width: pl.BlockSpec((None, WINDOW, width), lambda b, i: (b, i, 0))
    nxt = pl.BlockSpec((None, WINDOW, LANE), lambda b, i: (b, jnp.minimum(i + 1, nblk - 1), 0))
    tab = pl.BlockSpec((WINDOW, LANE), lambda b, i: (i, 0))
    return pl.pallas_call(
        body, name="rope_bwd", grid=(bsz, nblk),
        in_specs=[cur(ATT_WIDTH), cur(LANE), nxt, cur(LANE), nxt, tab, tab, tab],
        out_specs=cur(ATT_COLS), out_shape=_sds((bsz, seq, ATT_COLS), BF16),
        compiler_params=_params("parallel", "parallel"),
    )(dq, dkd, dkp, dvd, dvp, *tables)


def _local_step(x, mods, target, w_in, w_up4, w_out, w_down, pre_w_mix, attn_sinks, attn_out_w, lb_table, hg_norm_w,
                post_w_mix, pre_w_mlp, post_w_mlp):
    bsz, seq, _ = x.shape
    ntok = bsz * seq
    sh1, sc1, g1, sh2, sc2, g2 = mods
    flat = lambda a: a.reshape(ntok, a.shape[-1])
    unflat = lambda a: a.reshape(bsz, seq, a.shape[-1])
    tables = _rope_tables(seq)

    h1 = _prenorm1(x, pre_w_mix, sc1, sh1)
    proj = unflat(_mm(flat(h1), w_in, name="in_proj", out_dtype=F32))
    qr, kr, vb = _rope_fwd(proj, tables)
    attn_raw, attn_n, lse = _attn_fwd(qr, kr, vb, attn_sinks, attn_out_w)
    o_raw, rec, states = _hgrn_fwd(proj, lb_table, hg_norm_w)
    cat = jnp.concatenate([attn_n, rec], axis=-1)
    mix = unflat(_mm(flat(cat), w_out, name="out_proj", out_dtype=F32))
    x1, h2 = _mid_fwd(x, mix, post_w_mix, g1, pre_w_mlp, sc2, sh2)
    up_spec = pl.BlockSpec((None, D_MODEL, D_MODEL), lambda i, j, k: (j, 0, 0))
    r = _mm(flat(h2), w_up4, name="up_proj", out_dtype=BF16, tn=D_MODEL, n_out=D_FF, b_spec=up_spec,
            epi=lambda acc: jnp.maximum(acc, 0.0))
    square = lambda t: t * t
    down = unflat(_mm(r, w_down, name="down_proj", out_dtype=F32, a_fn=square))
    loss_row, dy, dd, dg2, d_post_mlp = _loss_bwd(x1, down, post_w_mlp, g2, target)

    dpre = _mm(flat(dd), w_down, name="down_bwd", out_dtype=BF16, trans_b=True, tn=D_MODEL, extra=(r,),
               epi=lambda acc, rt: acc * (2.0 * rt.astype(F32)))
    half_rows = D_MODEL // 2
    g_down = _mm_tn(r, flat(dd), name="down_wgrad", tk=half_rows, tn=D_MODEL, a_fn=square,
                    out_shape=_sds((2, N_CHIPS, half_rows, D_MODEL), F32),
                    out_spec=pl.BlockSpec((None, None, half_rows, D_MODEL), lambda i, j, m: (i % 2, i // 2, 0, 0)))
    up_t_spec = pl.BlockSpec((None, D_MODEL, D_MODEL), lambda i, j, k: (k, 0, 0))
    dh2 = unflat(_mm(dpre, w_up4, name="up_bwd", out_dtype=F32, trans_b=True, nk=N_CHIPS, n_out=D_MODEL, b_spec=up_t_spec,
                     k_total=D_FF))
    g_up = _mm_tn(flat(h2), dpre, name="up_wgrad", tk=half_rows, tn=D_MODEL,
                  out_shape=_sds((2, N_CHIPS, half_rows, D_MODEL), F32),
                  out_spec=pl.BlockSpec((None, None, half_rows, D_MODEL), lambda i, j, m: (i, j, 0, 0)))
    dx1, dmix, dsc2, dsh2, dg1, d_pre_mlp, d_post_mix = _mid_bwd(dh2, dy, x1, mix, pre_w_mlp, sc2, post_w_mix, g1)

    dcat = unflat(_mm(flat(dmix), w_out, name="out_bwd", out_dtype=F32, trans_b=True))
    out_rows = D_MODEL // N_CHIPS
    g_out = _mm_tn(flat(cat), flat(dmix), name="out_wgrad", tk=out_rows, tn=half_rows,
                   out_shape=_sds((2, N_CHIPS, out_rows, half_rows), F32),
                   out_spec=pl.BlockSpec((None, None, out_rows, half_rows), lambda i, j, m: (j, i, 0, 0)))
    dhq, dhf, dhi, dhg, d_lb, d_hg_norm = _hgrn_bwd(dcat, proj, o_raw, states, lb_table, hg_norm_w)
    dq, dkd, dkp, dvd, dvp, d_attn_out, d_sinks = _attn_bwd(dcat, attn_raw, attn_out_w, qr, kr, vb, lse, attn_sinks)
    dproj_a = _rope_bwd(dq, dkd, dkp, dvd, dvp, tables)
    dproj = flat(jnp.concatenate([dproj_a, dhq, dhf, dhi, dhg], axis=-1))
    dh1 = unflat(_mm(dproj, w_in, name="in_bwd", out_dtype=F32, trans_b=True))
    g_in = _mm_tn(flat(h1), dproj, name="in_wgrad", tk=D_MODEL, tn=IN_COLS // 2)
    grad_x, dsc1, dsh1, d_pre_mix = _norm1_bwd(dh1, dx1, x, pre_w_mix, sc1)

    in_cols = IN_COLS // N_CHIPS
    g_in = g_in.reshape(2, half_rows, N_CHIPS, in_cols).transpose(0, 2, 1, 3)
    dmod = jnp.concatenate([dsh1, dsc1, dg1, dsh2, dsc2, dg2], axis=-1).reshape(bsz, N_MOD * D_MODEL)
    small = dict(pre_w_mix=d_pre_mix, post_w_mix=d_post_mix, pre_w_mlp=d_pre_mlp, post_w_mlp=d_post_mlp,
                 attn_out_w=d_attn_out, hg_norm_w=d_hg_norm, attn_sinks=d_sinks, lb=d_lb)
    return loss_row[0, 0], grad_x, (g_in, g_up, g_out, g_down), dmod, small


HBM_SPEC = pl.BlockSpec(memory_space=pltpu.HBM)


def _mesh_pos():
    return lax.axis_index("x"), lax.axis_index("y"), lax.axis_index("c")


def _allgather8(arrays, name):
    n = len(arrays)

    def body(*refs):
        ins, outs = refs[:n], refs[n:2 * n]
        send_sems, recv_sems, local_sems = refs[2 * n:]
        x, y, c = _mesh_pos()
        me, sibling = (x, y, c), (x, y, 1 - c)
        chips = [(1 - x, y), (x, 1 - y), (1 - x, 1 - y)]

        def copy(a, k, block, to, src=None):
            dst = outs[a].at[4 * block[0] + 2 * block[1] + block[2]]
            return pltpu.make_async_remote_copy(
                src_ref=dst if src is None else src, dst_ref=dst, send_sem=send_sems.at[7 * a + k],
                recv_sem=recv_sems.at[7 * a + k], device_id=to, device_id_type=MESH)

        mine = [pltpu.make_async_copy(ins[a], outs[a].at[4 * x + 2 * y + c], local_sems.at[a]) for a in range(n)]
        for cp in mine:
            cp.start()
        first = []
        for a in range(n):
            first.append(copy(a, 0, me, sibling, src=ins[a]))
            first += [copy(a, 1 + j, me, (*chip, c), src=ins[a]) for j, chip in enumerate(chips)]
        for cp in first:
            cp.start()
        passed = []
        for j, chip in enumerate(chips):
            for a in range(n):
                copy(a, 1 + j, (*chip, c), me).wait_recv()
                fwd = copy(a, 4 + j, (*chip, c), sibling)
                fwd.start()
                passed.append(fwd)
        for a in range(n):
            copy(a, 0, sibling, me).wait_recv()
            for j, chip in enumerate(chips):
                copy(a, 4 + j, (*chip, 1 - c), me).wait_recv()
        for cp in first + passed:
            cp.wait_send()
        for cp in mine:
            cp.wait()

    return pl.pallas_call(
        body, name=name, in_specs=[HBM_SPEC] * n, out_specs=[HBM_SPEC] * n,
        out_shape=[_sds((N_DEV,) + a.shape, a.dtype) for a in arrays],
        scratch_shapes=[pltpu.SemaphoreType.DMA((7 * n,)), pltpu.SemaphoreType.DMA((7 * n,)), pltpu.SemaphoreType.DMA((n,))],
    )(*arrays)


def _pair_send_other_half(arrays, name):
    n = len(arrays)

    def body(*refs):
        ins, outs = refs[:n], refs[n:2 * n]
        send_sems, recv_sems = refs[2 * n:]
        x, y, c = _mesh_pos()
        copies = [pltpu.make_async_remote_copy(
            src_ref=ins[a].at[1 - c], dst_ref=outs[a], send_sem=send_sems.at[a], recv_sem=recv_sems.at[a],
            device_id=(x, y, 1 - c), device_id_type=MESH) for a in range(n)]
        for cp in copies:
            cp.start()
        for cp in copies:
            cp.wait()

    return pl.pallas_call(
        body, name=name, in_specs=[HBM_SPEC] * n, out_specs=[HBM_SPEC] * n,
        out_shape=[_sds(a.shape[1:], a.dtype) for a in arrays],
        scratch_shapes=[pltpu.SemaphoreType.DMA((n,)), pltpu.SemaphoreType.DMA((n,))],
    )(*arrays)


def _chip_exchange(arrays, name):
    n = len(arrays)

    def body(*refs):
        ins, outs = refs[:n], refs[n:2 * n]
        send_sems, recv_sems, local_sems = refs[2 * n:]
        x, y, c = _mesh_pos()
        my_chip = 2 * x + y
        chips = [(1 - x, y), (x, 1 - y), (1 - x, 1 - y)]
        mine = [pltpu.make_async_copy(ins[a].at[my_chip], outs[a].at[my_chip], local_sems.at[a]) for a in range(n)]
        for cp in mine:
            cp.start()
        sends = []
        for a in range(n):
            for j, chip in enumerate(chips):
                sends.append(pltpu.make_async_remote_copy(
                    src_ref=ins[a].at[2 * chip[0] + chip[1]], dst_ref=outs[a].at[my_chip], send_sem=send_sems.at[3 * a + j],
                    recv_sem=recv_sems.at[3 * a + j], device_id=(*chip, c), device_id_type=MESH))
        for cp in sends:
            cp.start()
        for a in range(n):
            for j, chip in enumerate(chips):
                landed = outs[a].at[2 * chip[0] + chip[1]]
                pltpu.make_async_remote_copy(
                    src_ref=landed, dst_ref=landed, send_sem=send_sems.at[3 * a + j], recv_sem=recv_sems.at[3 * a + j],
                    device_id=(*chip, c), device_id_type=MESH).wait_recv()
        for cp in sends:
            cp.wait_send()
        for cp in mine:
            cp.wait()

    return pl.pallas_call(
        body, name=name, in_specs=[HBM_SPEC] * n, out_specs=[HBM_SPEC] * n,
        out_shape=[_sds(a.shape, a.dtype) for a in arrays],
        scratch_shapes=[pltpu.SemaphoreType.DMA((3 * n,)), pltpu.SemaphoreType.DMA((3 * n,)), pltpu.SemaphoreType.DMA((n,))],
    )(*arrays)


def _pair_allgather(arrays, name):
    n = len(arrays)

    def body(*refs):
        ins, outs = refs[:n], refs[n:2 * n]
        send_sems, recv_sems, local_sems = refs[2 * n:]
        x, y, c = _mesh_pos()
        mine = [pltpu.make_async_copy(ins[a], outs[a].at[c], local_sems.at[a]) for a in range(n)]
        copies = [pltpu.make_async_remote_copy(
            src_ref=ins[a], dst_ref=outs[a].at[c], send_sem=send_sems.at[a], recv_sem=recv_sems.at[a],
            device_id=(x, y, 1 - c), device_id_type=MESH) for a in range(n)]
        for cp in mine + copies:
            cp.start()
        for a in range(n):
            landed = outs[a].at[1 - c]
            pltpu.make_async_remote_copy(
                src_ref=landed, dst_ref=landed, send_sem=send_sems.at[a], recv_sem=recv_sems.at[a],
                device_id=(x, y, 1 - c), device_id_type=MESH).wait_recv()
        for cp in copies:
            cp.wait_send()
        for cp in mine:
            cp.wait()

    return pl.pallas_call(
        body, name=name, in_specs=[HBM_SPEC] * n, out_specs=[HBM_SPEC] * n,
        out_shape=[_sds((2,) + a.shape, a.dtype) for a in arrays],
        scratch_shapes=[pltpu.SemaphoreType.DMA((n,)), pltpu.SemaphoreType.DMA((n,)), pltpu.SemaphoreType.DMA((n,))],
    )(*arrays)


def _pair_sum(g, q, core, name):
    _, nblk, rows, cols = g.shape
    tr = min(rows, 256)

    def body(core_ref, g_ref, q_ref, o_ref):
        o_ref[...] = (g_ref[...] + q_ref[...]).astype(BF16)

    blk = pl.BlockSpec((None, tr, cols), lambda k, i, core_ref: (k, i, 0))
    return pl.pallas_call(
        body, name=name,
        grid_spec=pltpu.PrefetchScalarGridSpec(
            num_scalar_prefetch=1, grid=(nblk, rows // tr),
            in_specs=[pl.BlockSpec((None, None, tr, cols), lambda k, i, core_ref: (core_ref[0], k, i, 0)), blk],
            out_specs=blk),
        out_shape=_sds((nblk, rows, cols), BF16), compiler_params=_params("parallel", "parallel"),
    )(core, g, q)


def _sum_chips(xs, name):
    nblk, rows, cols = xs.shape
    tr = min(rows, 256)

    def body(x_ref, o_ref):
        acc = x_ref[0].astype(F32)
        for j in range(1, nblk):
            acc = acc + x_ref[j].astype(F32)
        o_ref[...] = acc

    return pl.pallas_call(
        body, name=name, grid=(rows // tr,),
        in_specs=[pl.BlockSpec((nblk, tr, cols), lambda i: (0, i, 0))], out_specs=pl.BlockSpec((tr, cols), lambda i: (i, 0)),
        out_shape=_sds((rows, cols), F32), compiler_params=_params("parallel"),
    )(xs)


SMALL_ROWS = 120
PACK_ROWS = 160


def _rows(a, nrows):
    flat = a.reshape(-1)
    return jnp.pad(flat, (0, nrows * LANE - flat.shape[0])).reshape(nrows, LANE)


def _pack_small(b_ada, pre_w_mix, post_w_mix, pre_w_mlp, post_w_mlp, attn_out_w, hg_norm_w, attn_sinks, lb_table):
    return jnp.concatenate([
        _rows(b_ada, 48), _rows(pre_w_mix, 8), _rows(post_w_mix, 8), _rows(pre_w_mlp, 8), _rows(post_w_mlp, 8),
        _rows(attn_out_w, 8), _rows(hg_norm_w, 8), _rows(attn_sinks, 8), _rows(lb_table[0], 8), _rows(lb_table[1], 8)], axis=0)


def _unpack_small(p):
    vec = lambda lo, n: p[lo:lo + n // LANE].reshape(1, n)
    lb = jnp.stack([p[104:108].reshape(HG_WIDTH), p[112:116].reshape(HG_WIDTH)])
    return dict(b_ada=vec(0, N_MOD * D_MODEL), pre_w_mix=vec(48, D_MODEL), post_w_mix=vec(56, D_MODEL), pre_w_mlp=vec(64, D_MODEL),
                post_w_mlp=vec(72, D_MODEL), attn_out_w=vec(80, ATT_WIDTH), hg_norm_w=p[88:89], attn_sinks=p[96:97, :ATT_Q_HEADS],
                lb_table=lb)


def _small_update(packs, w, m, v):
    def body(p_ref, w_ref, m_ref, v_ref, g_ref, dl_ref, nm_ref, nv_ref):
        tot = p_ref[0]
        for d in range(1, N_DEV):
            tot = tot + p_ref[d]
        wv = w_ref[...]
        p1 = _sigmoid(wv[112:120] - wv[104:112])
        s = tot[152:160] * p1 * (1.0 - p1)
        g = jnp.concatenate([tot[0:48] + tot[48:96], tot[96:152], -s, s], axis=0)
        g_ref[...] = g
        dl_ref[...], nm_ref[...], nv_ref[...] = _adamw_math(g, wv, m_ref[...], v_ref[...])

    shp = _sds((SMALL_ROWS, LANE), F32)
    return pl.pallas_call(body, name="small_update", out_shape=[shp] * 4, compiler_params=_params())(packs, w, m, v)


def kernel(x, c, w_ada, b_ada, pre_w_mix, w_in, attn_sinks, attn_out_w, lb_table, hg_norm_w, w_out, post_w_mix, pre_w_mlp, w_up, w_down, post_w_mlp, loss_target, m_w_ada, m_b_ada, m_pre_w_mix, m_w_in, m_attn_sinks, m_attn_out_w, m_lb_table, m_hg_norm_w, m_w_out, m_post_w_mix, m_pre_w_mlp, m_w_up, m_w_down, m_post_w_mlp, v_w_ada, v_b_ada, v_pre_w_mix, v_w_in, v_attn_sinks, v_attn_out_w, v_lb_table, v_hg_norm_w, v_w_out, v_post_w_mix, v_pre_w_mlp, v_w_up, v_w_down, v_post_w_mlp):
    xi, yi, ci = _mesh_pos()
    chip = 2 * xi + yi
    dev = 2 * chip + ci
    bsz = x.shape[0]
    ada_cols = w_ada.shape[2]

    def row_half(w):
        rows = w.shape[1] // 2
        return lax.dynamic_slice_in_dim(w[0], ci * rows, rows, axis=0).astype(BF16)

    c_g, in_g, up_g, out_g, down_g = _allgather8([c, row_half(w_in), row_half(w_up), row_half(w_out), row_half(w_down)],
                                                 "gather_weights")
    c_all = c_g.reshape(N_DEV * bsz, D_MODEL)
    w_in_full = in_g.reshape(N_CHIPS, D_MODEL, IN_COLS // N_CHIPS).transpose(1, 0, 2).reshape(D_MODEL, IN_COLS)
    w_up4 = up_g.reshape(N_CHIPS, D_MODEL, D_MODEL)
    w_out_full = out_g.reshape(D_MODEL, D_MODEL)
    w_down_full = down_g.reshape(D_FF, D_MODEL)

    b_cols = lax.dynamic_slice_in_dim(b_ada, chip * ada_cols, ada_cols, axis=1)
    mod_part = _ada_fwd(c_all, w_ada[0], b_cols)
    half_rows = mod_part.shape[0] // 2
    (mod_g,) = _allgather8([lax.dynamic_slice_in_dim(mod_part, ci * half_rows, half_rows, axis=0)], "gather_mod")
    mod_all = mod_g.reshape(N_CHIPS, 2, half_rows, ada_cols).transpose(1, 2, 0, 3).reshape(N_DEV * bsz, N_MOD * D_MODEL)
    mod = lax.dynamic_slice_in_dim(mod_all, dev * bsz, bsz, axis=0)
    mods = [mod[:, i * D_MODEL:(i + 1) * D_MODEL].reshape(bsz, 1, D_MODEL) for i in range(N_MOD)]

    loss_part, grad_x, big_grads, dmod, small = _local_step(
        x, mods, loss_target, w_in_full, w_up4, w_out_full, w_down_full, pre_w_mix, attn_sinks, attn_out_w, lb_table,
        hg_norm_w, post_w_mix, pre_w_mlp, post_w_mlp)
    loss = lax.psum(loss_part, ("x", "y", "c"))

    pack = jnp.concatenate([
        _rows(dmod, 96), _rows(small["pre_w_mix"], 8), _rows(small["post_w_mix"], 8), _rows(small["pre_w_mlp"], 8),
        _rows(small["post_w_mlp"], 8), _rows(small["attn_out_w"], 8), _rows(small["hg_norm_w"], 8), _rows(small["attn_sinks"], 8),
        _rows(small["lb"], 8)], axis=0)
    (packs,) = _allgather8([pack], "gather_small")
    small_args = lambda pre: (pre["b_ada"], pre["pre_w_mix"], pre["post_w_mix"], pre["pre_w_mlp"], pre["post_w_mlp"],
                              pre["attn_out_w"], pre["hg_norm_w"], pre["attn_sinks"], pre["lb_table"])
    w_small = dict(b_ada=b_ada, pre_w_mix=pre_w_mix, post_w_mix=post_w_mix, pre_w_mlp=pre_w_mlp, post_w_mlp=post_w_mlp,
                   attn_out_w=attn_out_w, hg_norm_w=hg_norm_w, attn_sinks=attn_sinks, lb_table=lb_table)
    m_small = dict(b_ada=m_b_ada, pre_w_mix=m_pre_w_mix, post_w_mix=m_post_w_mix, pre_w_mlp=m_pre_w_mlp, post_w_mlp=m_post_w_mlp,
                   attn_out_w=m_attn_out_w, hg_norm_w=m_hg_norm_w, attn_sinks=m_attn_sinks, lb_table=m_lb_table)
    v_small = dict(b_ada=v_b_ada, pre_w_mix=v_pre_w_mix, post_w_mix=v_post_w_mix, pre_w_mlp=v_pre_w_mlp, post_w_mlp=v_post_w_mlp,
                   attn_out_w=v_attn_out_w, hg_norm_w=v_hg_norm_w, attn_sinks=v_attn_sinks, lb_table=v_lb_table)
    small_out = [_unpack_small(p) for p in _small_update(packs, _pack_small(*small_args(w_small)), _pack_small(*small_args(m_small)),
                                                         _pack_small(*small_args(v_small)))]

    dmod_all = packs[:, :96, :].reshape(N_DEV * bsz, N_MOD * D_MODEL)
    dmod_cols = lax.dynamic_slice_in_dim(dmod_all, chip * ada_cols, ada_cols, axis=1)
    ada_out = _ada_bwd_adamw(c_all, dmod_cols, w_ada[0], m_w_ada[0], v_w_ada[0])

    core = jnp.reshape(ci, (1,)).astype(jnp.int32)
    names = ("in", "up", "out", "down")
    others = _pair_send_other_half(list(big_grads), "pair_reduce_send")
    pair_sums = [_pair_sum(g, q, core, f"pair_sum_{nm}") for g, q, nm in zip(big_grads, others, names)]
    landed = _chip_exchange(pair_sums, "chip_exchange")
    halves = [_sum_chips(xs, f"sum_chips_{nm}") for xs, nm in zip(landed, names)]
    g_in2, g_up2, g_out2, g_down2 = _pair_allgather(halves, "pair_allgather")
    g_w_in = g_in2.reshape(w_in.shape[1:])
    g_w_up = g_up2.reshape(w_up.shape[1:])
    g_w_out = g_out2.transpose(1, 0, 2).reshape(w_out.shape[1:])
    g_w_down = g_down2.reshape(w_down.shape[1:])
    big = dict(
        w_in=(g_w_in,) + tuple(_adamw(g_w_in, w_in[0], m_w_in[0], v_w_in[0], name="adamw_in")),
        w_up=(g_w_up,) + tuple(_adamw(g_w_up, w_up[0], m_w_up[0], v_w_up[0], name="adamw_up")),
        w_out=(g_w_out,) + tuple(_adamw(g_w_out, w_out[0], m_w_out[0], v_w_out[0], name="adamw_out")),
        w_down=(g_w_down,) + tuple(_adamw(g_w_down, w_down[0], m_w_down[0], v_w_down[0], name="adamw_down")),
        w_ada=tuple(ada_out),
    )
    order = ("w_ada", "b_ada", "pre_w_mix", "w_in", "attn_sinks", "attn_out_w", "lb_table", "hg_norm_w", "w_out", "post_w_mix",
             "pre_w_mlp", "w_up", "w_down", "post_w_mlp")
    outs = [loss, grad_x]
    for kind in range(4):
        for nm in order:
            outs.append(big[nm][kind][None] if nm in big else small_out[kind][nm])
    return tuple(outs)
```

```python
import functools

import jax
import jax.numpy as jnp
from jax import lax
from jax.experimental import pallas as pl
from jax.experimental.pallas import tpu as pltpu

F32 = jnp.float32
BF16 = jnp.bfloat16

D_MODEL = 1024
ATT_WIDTH = 512
ATT_HEAD_DIM = 64
ATT_Q_HEADS = 8
ATT_KV_HEADS = 2
ATT_GROUP = ATT_Q_HEADS // ATT_KV_HEADS
ATT_KV_COLS = ATT_KV_HEADS * ATT_HEAD_DIM
WINDOW = 128
ROPE_DIM = 16
ROPE_THETA = 500000.0
HG_WIDTH = 512
HG_HEAD_DIM = 128
HG_HEADS = 4
HG_CHUNK = 32
IN_COLS = ATT_WIDTH + 2 * ATT_KV_COLS + 4 * HG_WIDTH
ATT_COLS = ATT_WIDTH + 2 * ATT_KV_COLS
D_FF = 4 * D_MODEL
N_MOD = 6
EPS = 1e-6
ATT_SCALE = ATT_HEAD_DIM ** -0.5

ADAM_LR = 0.001
ADAM_B1 = 0.9
ADAM_B2 = 0.999
ADAM_EPS = 1e-08
ADAM_WD = 0.01
ADAM_STEP = 10

N_CHIPS = 4
N_DEV = 8
LANE = 128
VMEM_LIMIT = 48 * 1024 * 1024
MESH = pl.DeviceIdType.MESH

NT_DIMS = (((1,), (1,)), ((), ()))
TN_DIMS = (((0,), (0,)), ((), ()))


def _sds(shape, dtype):
    return jax.ShapeDtypeStruct(tuple(shape), dtype)


def _params(*sem):
    return pltpu.CompilerParams(dimension_semantics=sem, vmem_limit_bytes=VMEM_LIMIT)


def _sigmoid(x):
    return 1.0 / (1.0 + jnp.exp(-x))


def _dot(a, b, dims=None):
    a, b = a.astype(BF16), b.astype(BF16)
    if dims is None:
        return jnp.dot(a, b, preferred_element_type=F32)
    return lax.dot_general(a, b, dims, preferred_element_type=F32)


def _rms_fwd(x, w):
    rstd = lax.rsqrt(jnp.mean(x * x, axis=-1, keepdims=True) + EPS)
    xh = x * rstd
    return xh * w, xh, rstd


def _rms_bwd(dy, xh, rstd, w):
    dxh = dy * w
    dx = rstd * (dxh - xh * jnp.mean(dxh * xh, axis=-1, keepdims=True))
    return dx, dy * xh


def _colsum(x):
    return jnp.sum(x, axis=0, keepdims=True)


def _mm(a, b, *, name, out_dtype, trans_b=False, tm=512, tn=None, nk=1, a_fn=None, extra=(), epi=None,
        b_spec=None, n_out=None, k_total=None):
    m_total = a.shape[0]
    k_total = a.shape[1] if k_total is None else k_total
    tk = k_total // nk
    if n_out is None:
        n_out = b.shape[0] if trans_b else b.shape[1]
    tn = n_out if tn is None else tn
    grid = (m_total // tm, n_out // tn, nk)
    dims = NT_DIMS if trans_b else None

    def body(*refs):
        a_ref, b_ref = refs[0], refs[1]
        extra_refs = refs[2:2 + len(extra)]
        o_ref = refs[2 + len(extra)]
        av = a_ref[...]
        if a_fn is not None:
            av = a_fn(av.astype(F32))
        part = _dot(av, b_ref[...], dims)

        def finish(acc):
            if epi is not None:
                acc = epi(acc, *[r[...] for r in extra_refs])
            o_ref[...] = acc.astype(out_dtype)

        if nk == 1:
            finish(part)
        else:
            acc_ref = refs[-1]
            k = pl.program_id(2)

            @pl.when(k == 0)
            def _():
                acc_ref[...] = part

            @pl.when(k > 0)
            def _():
                acc_ref[...] += part

            @pl.when(k == nk - 1)
            def _():
                finish(acc_ref[...])

    if b_spec is None:
        if trans_b:
            b_spec = pl.BlockSpec((tn, tk), lambda i, j, k: (j, k))
        else:
            b_spec = pl.BlockSpec((tk, tn), lambda i, j, k: (k, j))
    in_specs = [pl.BlockSpec((tm, tk), lambda i, j, k: (i, k)), b_spec]
    in_specs += [pl.BlockSpec((tm, tn), lambda i, j, k: (i, j)) for _ in extra]
    return pl.pallas_call(
        body, name=name, grid=grid, in_specs=in_specs,
        out_specs=pl.BlockSpec((tm, tn), lambda i, j, k: (i, j)),
        out_shape=_sds((m_total, n_out), out_dtype),
        scratch_shapes=[pltpu.VMEM((tm, tn), F32)] if nk > 1 else [],
        compiler_params=_params("parallel", "parallel", "arbitrary"),
    )(a, b, *extra)


def _mm_tn(a, b, *, name, tk, tn, tm=512, a_fn=None, out_shape=None, out_spec=None):
    m_total, k_total = a.shape
    n_total = b.shape[1]
    nm = m_total // tm
    grid = (k_total // tk, n_total // tn, nm)

    def body(a_ref, b_ref, o_ref):
        av = a_ref[...]
        if a_fn is not None:
            av = a_fn(av.astype(F32))
        part = _dot(av, b_ref[...], TN_DIMS)
        m = pl.program_id(2)

        @pl.when(m == 0)
        def _():
            o_ref[...] = part

        @pl.when(m > 0)
        def _():
            o_ref[...] += part

    if out_shape is None:
        out_shape = _sds((k_total, n_total), F32)
        out_spec = pl.BlockSpec((tk, tn), lambda i, j, m: (i, j))
    return pl.pallas_call(
        body, name=name, grid=grid,
        in_specs=[pl.BlockSpec((tm, tk), lambda i, j, m: (m, i)), pl.BlockSpec((tm, tn), lambda i, j, m: (m, j))],
        out_specs=out_spec, out_shape=out_shape,
        compiler_params=_params("parallel", "parallel", "arbitrary"),
    )(a, b)


def _ada_fwd(c_all, w_shard, b_shard):
    nb, ncol = c_all.shape[0], w_shard.shape[1]
    tn = 512

    def body(c_ref, w_ref, b_ref, o_ref):
        c = c_ref[...]
        o_ref[...] = _dot(c * _sigmoid(c), w_ref[...]) + b_ref[...]

    return pl.pallas_call(
        body, name="ada_fwd", grid=(ncol // tn,),
        in_specs=[pl.BlockSpec((nb, D_MODEL), lambda j: (0, 0)), pl.BlockSpec((D_MODEL, tn), lambda j: (0, j)),
                  pl.BlockSpec((1, tn), lambda j: (0, j))],
        out_specs=pl.BlockSpec((nb, tn), lambda j: (0, j)), out_shape=_sds((nb, ncol), F32),
        compiler_params=_params("parallel"),
    )(c_all, w_shard, b_shard)


def _adamw_math(g, w, m, v):
    m = ADAM_B1 * m + (1.0 - ADAM_B1) * g
    v = ADAM_B2 * v + (1.0 - ADAM_B2) * (g * g)
    m_hat = m / (1.0 - ADAM_B1 ** ADAM_STEP)
    v_hat = v / (1.0 - ADAM_B2 ** ADAM_STEP)
    delta = -ADAM_LR * (m_hat / (jnp.sqrt(v_hat) + ADAM_EPS) + ADAM_WD * w)
    return delta, m, v


def _ada_bwd_adamw(c_all, dmod_cols, w, m, v):
    nb, ncol = dmod_cols.shape
    tn = 256

    def body(c_ref, d_ref, w_ref, m_ref, v_ref, g_ref, dl_ref, nm_ref, nv_ref):
        c = c_ref[...]
        g = _dot(c * _sigmoid(c), d_ref[...], TN_DIMS)
        g_ref[...] = g
        dl_ref[...], nm_ref[...], nv_ref[...] = _adamw_math(g, w_ref[...], m_ref[...], v_ref[...])

    col = pl.BlockSpec((D_MODEL, tn), lambda j: (0, j))
    shp = _sds((D_MODEL, ncol), F32)
    return pl.pallas_call(
        body, name="ada_bwd_adamw", grid=(ncol // tn,),
        in_specs=[pl.BlockSpec((nb, D_MODEL), lambda j: (0, 0)), pl.BlockSpec((nb, tn), lambda j: (0, j)), col, col, col],
        out_specs=[col, col, col, col], out_shape=[shp, shp, shp, shp],
        compiler_params=_params("parallel"),
    )(c_all, dmod_cols, w, m, v)


def _adamw_halves(own, theirs, core, w, m, v, *, axis, name):
    r2, c2 = own.shape
    tr = min(r2, 256)
    nt = r2 // tr

    def body(core_ref, own_ref, their_ref, w_ref, m_ref, v_ref, g_ref, dl_ref, nm_ref, nv_ref):
        g = jnp.where(pl.program_id(0) == core_ref[0], own_ref[...], their_ref[...])
        g_ref[...] = g
        dl_ref[...], nm_ref[...], nv_ref[...] = _adamw_math(g, w_ref[...], m_ref[...], v_ref[...])

    if axis == 0:
        full = pl.BlockSpec((tr, c2), lambda h, i, core_ref: (h * nt + i, 0))
    else:
        full = pl.BlockSpec((tr, c2), lambda h, i, core_ref: (i, h))
    half = pl.BlockSpec((tr, c2), lambda h, i, core_ref: (i, 0))
    shp = _sds(w.shape, F32)
    return pl.pallas_call(
        body, name=name,
        grid_spec=pltpu.PrefetchScalarGridSpec(num_scalar_prefetch=1, grid=(2, nt), in_specs=[half, half, full, full, full],
                                               out_specs=[full] * 4),
        out_shape=[shp] * 4, compiler_params=_params("parallel", "parallel"),
    )(core, own, theirs, w, m, v)


def _tok_spec(tm, width=D_MODEL):
    return pl.BlockSpec((None, tm, width), lambda b, i: (b, i, 0))


def _row_spec(width=D_MODEL):
    return pl.BlockSpec((None, 1, width), lambda b, i: (b, 0, 0))


def _vec_spec(width=D_MODEL):
    return pl.BlockSpec((1, width), lambda b, i: (0, 0))


def _prenorm1(x, w, sc, sh, tm=512):
    bsz, seq, _ = x.shape

    def body(x_ref, w_ref, sc_ref, sh_ref, h_ref):
        y, _, _ = _rms_fwd(x_ref[...], w_ref[...])
        h_ref[...] = (y * (1.0 + sc_ref[...]) + sh_ref[...]).astype(BF16)

    return pl.pallas_call(
        body, name="prenorm1", grid=(bsz, seq // tm),
        in_specs=[_tok_spec(tm), _vec_spec(), _row_spec(), _row_spec()],
        out_specs=_tok_spec(tm), out_shape=_sds(x.shape, BF16),
        compiler_params=_params("parallel", "parallel"),
    )(x, w, sc, sh)


def _rope_tables(seq):
    half = ROPE_DIM // 2
    inv_freq = ROPE_THETA ** (-jnp.arange(0, ROPE_DIM, 2, dtype=F32) / ROPE_DIM)
    ang = jnp.arange(seq, dtype=F32)[:, None] * inv_freq[None, :]
    cos, sin = jnp.cos(ang), jnp.sin(ang)
    rest = ATT_HEAD_DIM - ROPE_DIM
    ones, zeros, zh = jnp.ones((seq, rest), F32), jnp.zeros((seq, rest), F32), jnp.zeros((seq, half), F32)
    reps = LANE // ATT_HEAD_DIM
    t_cos = jnp.tile(jnp.concatenate([cos, cos, ones], axis=1), (1, reps))
    t_up = jnp.tile(jnp.concatenate([zh, sin, zeros], axis=1), (1, reps))
    t_dn = jnp.tile(jnp.concatenate([-sin, zh, zeros], axis=1), (1, reps))
    return t_cos, t_up, t_dn


def _rope_fwd(proj, tables, tm=256):
    bsz, seq, _ = proj.shape
    half = ROPE_DIM // 2

    def body(p_ref, c_ref, u_ref, d_ref, q_ref, k_ref, v_ref):
        c, u, d = c_ref[...], u_ref[...], d_ref[...]

        def rope(x):
            return x * c + pltpu.roll(x, half, 1) * u + pltpu.roll(x, LANE - half, 1) * d

        for s in range(ATT_WIDTH // LANE):
            q_ref[:, s * LANE:(s + 1) * LANE] = rope(p_ref[:, s * LANE:(s + 1) * LANE]).astype(BF16)
        k_ref[...] = rope(p_ref[:, ATT_WIDTH:ATT_WIDTH + LANE]).astype(BF16)
        v_ref[...] = p_ref[:, ATT_WIDTH + LANE:ATT_COLS].astype(BF16)

    tab = pl.BlockSpec((tm, LANE), lambda b, i: (i, 0))
    return pl.pallas_call(
        body, name="rope_fwd", grid=(bsz, seq // tm),
        in_specs=[_tok_spec(tm, ATT_COLS), tab, tab, tab],
        out_specs=[_tok_spec(tm, ATT_WIDTH), _tok_spec(tm, LANE), _tok_spec(tm, LANE)],
        out_shape=[_sds((bsz, seq, ATT_WIDTH), BF16), _sds((bsz, seq, LANE), BF16), _sds((bsz, seq, LANE), BF16)],
        compiler_params=_params("parallel", "parallel"),
    )(proj, *tables)


def _band_masks(i):
    row = lax.broadcasted_iota(jnp.int32, (WINDOW, WINDOW), 0)
    col = lax.broadcasted_iota(jnp.int32, (WINDOW, WINDOW), 1)
    return col <= row, jnp.logical_and(col > row, i > 0)


def _prev_spec(width):
    return pl.BlockSpec((None, WINDOW, width), lambda b, i: (b, jnp.maximum(i - 1, 0), 0))


def _attn_fwd(qr, kr, vb, sinks, w_norm):
    bsz, seq, _ = qr.shape
    nblk = seq // WINDOW
    neg = float(jnp.finfo(jnp.float32).min)

    def body(sink_ref, q_ref, kc_ref, kp_ref, vc_ref, vp_ref, w_ref, raw_ref, an_ref, l_ref):
        mask_c, mask_p = _band_masks(pl.program_id(1))
        for h in range(ATT_Q_HEADS):
            g = h // ATT_GROUP
            hs = slice(h * ATT_HEAD_DIM, (h + 1) * ATT_HEAD_DIM)
            gs = slice(g * ATT_HEAD_DIM, (g + 1) * ATT_HEAD_DIM)
            q = q_ref[:, hs]
            sink = sink_ref[0, h]
            sc = jnp.where(mask_c, _dot(q, kc_ref[:, gs], NT_DIMS) * ATT_SCALE, neg)
            sp = jnp.where(mask_p, _dot(q, kp_ref[:, gs], NT_DIMS) * ATT_SCALE, neg)
            m = jnp.maximum(jnp.maximum(jnp.max(sc, axis=-1, keepdims=True), jnp.max(sp, axis=-1, keepdims=True)), sink)
            pc = jnp.where(mask_c, jnp.exp(sc - m), 0.0)
            pp = jnp.where(mask_p, jnp.exp(sp - m), 0.0)
            den = jnp.sum(pc, axis=-1, keepdims=True) + jnp.sum(pp, axis=-1, keepdims=True) + jnp.exp(sink - m)
            raw_ref[:, hs] = _dot(pc / den, vc_ref[:, gs]) + _dot(pp / den, vp_ref[:, gs])
            l_ref[:, h:h + 1] = m + jnp.log(den)
        y, _, _ = _rms_fwd(raw_ref[...], w_ref[...])
        an_ref[...] = y.astype(BF16)

    cur = lambda width: pl.BlockSpec((None, WINDOW, width), lambda b, i: (b, i, 0))
    return pl.pallas_call(
        body, name="attn_fwd", grid=(bsz, nblk),
        in_specs=[pl.BlockSpec(memory_space=pltpu.SMEM), cur(ATT_WIDTH), cur(LANE), _prev_spec(LANE), cur(LANE), _prev_spec(LANE),
                  _vec_spec(ATT_WIDTH)],
        out_specs=[cur(ATT_WIDTH), cur(ATT_WIDTH), cur(ATT_Q_HEADS)],
        out_shape=[_sds((bsz, seq, ATT_WIDTH), F32), _sds((bsz, seq, ATT_WIDTH), BF16), _sds((bsz, seq, ATT_Q_HEADS), F32)],
        compiler_params=_params("parallel", "parallel"),
    )(sinks, qr, kr, kr, vb, vb, w_norm)


HG_Q0 = ATT_COLS // LANE
HG_F0 = HG_Q0 + HG_HEADS
HG_I0 = HG_F0 + HG_HEADS
HG_G0 = HG_I0 + HG_HEADS
HG_TOK = 256
HG_NCH = HG_TOK // HG_CHUNK


def _chunk_consts():
    row = lax.broadcasted_iota(jnp.int32, (HG_CHUNK, HG_CHUNK), 0)
    col = lax.broadcasted_iota(jnp.int32, (HG_CHUNK, HG_CHUNK), 1)
    return row >= col, row <= col


def _cumsum_rows(tri_f32, x):
    return jnp.dot(tri_f32, x, precision=lax.Precision.HIGHEST, preferred_element_type=F32)


def _hgrn_gates(tbl, hf, hq):
    lb = _sigmoid(tbl[1:2] - tbl[0:1])
    sig = _sigmoid(hf)
    f = lb + (1.0 - lb) * sig
    sq = _sigmoid(hq)
    return lb, sig, f, sq


def _hgrn_fwd(proj, lb_table, norm_w):
    bsz, seq, _ = proj.shape
    nstep = seq // HG_TOK

    def body(tbl_ref, nw_ref, q_ref, f_ref, i_ref, g_ref, o_ref, rec_ref, st_ref, s_scr):
        @pl.when(pl.program_id(2) == 0)
        def _():
            s_scr[...] = jnp.zeros_like(s_scr)

        lower, _ = _chunk_consts()
        tri = lower.astype(F32)
        nw = nw_ref[...]
        for j in range(HG_NCH):
            sl = slice(j * HG_CHUNK, (j + 1) * HG_CHUNK)
            hq, v = q_ref[sl, :], i_ref[sl, :]
            _, _, f, sq = _hgrn_gates(tbl_ref[...], f_ref[sl, :], hq)
            q, k = hq * sq, 1.0 - f
            b = _cumsum_rows(tri, jnp.log(f))
            bl = b[HG_CHUNK - 1:HG_CHUNK, :]
            qd = q * jnp.exp(b)
            kd = k * jnp.exp(-b)
            ku = k * jnp.exp(bl - b)
            a = jnp.where(lower, _dot(qd, kd, NT_DIMS), 0.0)
            st = s_scr[...]
            st_ref[j] = st
            o = _dot(a, v) + _dot(qd, st, NT_DIMS)
            s_scr[...] = st * jnp.exp(bl) + _dot(v, ku, TN_DIMS)
            o_ref[sl, :] = o
            y, _, _ = _rms_fwd(o, nw)
            hg = g_ref[sl, :]
            rec_ref[sl, :] = (y * (hg * _sigmoid(hg))).astype(BF16)

    slab = lambda first: pl.BlockSpec((None, HG_TOK, LANE), lambda b, h, t: (b, t, first + h))
    head_out = pl.BlockSpec((None, HG_TOK, LANE), lambda b, h, t: (b, t, h))
    return pl.pallas_call(
        body, name="hgrn_fwd", grid=(bsz, HG_HEADS, nstep),
        in_specs=[pl.BlockSpec((2, LANE), lambda b, h, t: (0, h)), pl.BlockSpec((1, LANE), lambda b, h, t: (0, 0)),
                  slab(HG_Q0), slab(HG_F0), slab(HG_I0), slab(HG_G0)],
        out_specs=[head_out, head_out,
                   pl.BlockSpec((None, None, HG_NCH, LANE, LANE), lambda b, h, t: (b, h, t, 0, 0))],
        out_shape=[_sds((bsz, seq, HG_WIDTH), F32), _sds((bsz, seq, HG_WIDTH), BF16),
                   _sds((bsz, HG_HEADS, seq // HG_CHUNK, LANE, LANE), F32)],
        scratch_shapes=[pltpu.VMEM((LANE, LANE), F32)],
        compiler_params=_params("parallel", "parallel", "arbitrary"),
    )(lb_table, norm_w, proj, proj, proj, proj)


def _mid_fwd(x, mix, post_w, g1, pre_w, sc2, sh2, tm=512):
    bsz, seq, _ = x.shape

    def body(x_ref, mix_ref, pw_ref, g1_ref, w2_ref, sc_ref, sh_ref, x1_ref, h2_ref):
        n1, _, _ = _rms_fwd(mix_ref[...], pw_ref[...])
        x1 = x_ref[...] + g1_ref[...] * n1
        x1_ref[...] = x1
        y2, _, _ = _rms_fwd(x1, w2_ref[...])
        h2_ref[...] = (y2 * (1.0 + sc_ref[...]) + sh_ref[...]).astype(BF16)

    return pl.pallas_call(
        body, name="mid_fwd", grid=(bsz, seq // tm),
        in_specs=[_tok_spec(tm), _tok_spec(tm), _vec_spec(), _row_spec(), _vec_spec(), _row_spec(), _row_spec()],
        out_specs=[_tok_spec(tm), _tok_spec(tm)], out_shape=[_sds(x.shape, F32), _sds(x.shape, BF16)],
        compiler_params=_params("parallel", "parallel"),
    )(x, mix, post_w, g1, pre_w, sc2, sh2)


def _acc_out(ref, first, value):
    @pl.when(first)
    def _():
        ref[...] = value

    @pl.when(jnp.logical_not(first))
    def _():
        ref[...] += value


def _loss_bwd(x1, down, post_w, g2, target, tm=512):
    bsz, seq, _ = x1.shape

    def body(x1_ref, d_ref, w_ref, g2_ref, t_ref, loss_ref, dy_ref, dd_ref, dg2_ref, dw_ref):
        b, i = pl.program_id(0), pl.program_id(1)
        w, g2v = w_ref[...], g2_ref[...]
        n2, dh, rstd = _rms_fwd(d_ref[...], w)
        err = x1_ref[...] + g2v * n2 - t_ref[...]
        part = (0.5 / D_MODEL) * jnp.sum(jnp.sum(err * err, axis=-1, keepdims=True), axis=0, keepdims=True)
        _acc_out(loss_ref, jnp.logical_and(b == 0, i == 0), jnp.broadcast_to(part, (1, LANE)))
        dy = err * (1.0 / D_MODEL)
        dy_ref[...] = dy
        _acc_out(dg2_ref, i == 0, _colsum(dy * n2))
        dd, dw_rows = _rms_bwd(dy * g2v, dh, rstd, w)
        dd_ref[...] = dd.astype(BF16)
        _acc_out(dw_ref, jnp.logical_and(b == 0, i == 0), _colsum(dw_rows))

    return pl.pallas_call(
        body, name="loss_bwd", grid=(bsz, seq // tm),
        in_specs=[_tok_spec(tm), _tok_spec(tm), _vec_spec(), _row_spec(), _tok_spec(tm)],
        out_specs=[_vec_spec(LANE), _tok_spec(tm), _tok_spec(tm), _row_spec(), _vec_spec()],
        out_shape=[_sds((1, LANE), F32), _sds(x1.shape, F32), _sds(x1.shape, BF16), _sds((bsz, 1, D_MODEL), F32),
                   _sds((1, D_MODEL), F32)],
        compiler_params=_params("arbitrary", "arbitrary"),
    )(x1, down, post_w, g2, target)


def _mid_bwd(dh2, dy, x1, mix, pre_w, sc2, post_w, g1, tm=512):
    bsz, seq, _ = x1.shape

    def body(dh2_ref, dy_ref, x1_ref, mix_ref, w2_ref, sc_ref, pw_ref, g1_ref,
             dx1_ref, dmix_ref, dsc_ref, dsh_ref, dg1_ref, dw2_ref, dpw_ref):
        b, i = pl.program_id(0), pl.program_id(1)
        first = jnp.logical_and(b == 0, i == 0)
        w2, pw = w2_ref[...], pw_ref[...]
        dh2v = dh2_ref[...]
        y2, xh2, rstd2 = _rms_fwd(x1_ref[...], w2)
        _acc_out(dsh_ref, i == 0, _colsum(dh2v))
        _acc_out(dsc_ref, i == 0, _colsum(dh2v * y2))
        dx1n, dw_rows = _rms_bwd(dh2v * (1.0 + sc_ref[...]), xh2, rstd2, w2)
        _acc_out(dw2_ref, first, _colsum(dw_rows))
        dx1 = dy_ref[...] + dx1n
        dx1_ref[...] = dx1
        n1, mh, rstd1 = _rms_fwd(mix_ref[...], pw)
        _acc_out(dg1_ref, i == 0, _colsum(dx1 * n1))
        dmix, dpw_rows = _rms_bwd(dx1 * g1_ref[...], mh, rstd1, pw)
        dmix_ref[...] = dmix.astype(BF16)
        _acc_out(dpw_ref, first, _colsum(dpw_rows))

    row_shape = _sds((bsz, 1, D_MODEL), F32)
    vec_shape = _sds((1, D_MODEL), F32)
    return pl.pallas_call(
        body, name="mid_bwd", grid=(bsz, seq // tm),
        in_specs=[_tok_spec(tm), _tok_spec(tm), _tok_spec(tm), _tok_spec(tm), _vec_spec(), _row_spec(), _vec_spec(), _row_spec()],
        out_specs=[_tok_spec(tm), _tok_spec(tm), _row_spec(), _row_spec(), _row_spec(), _vec_spec(), _vec_spec()],
        out_shape=[_sds(x1.shape, F32), _sds(x1.shape, BF16), row_shape, row_shape, row_shape, vec_shape, vec_shape],
        compiler_params=_params("arbitrary", "arbitrary"),
    )(dh2, dy, x1, mix, pre_w, sc2, post_w, g1)


def _norm1_bwd(dh1, dx1, x, pre_w, sc1, tm=512):
    bsz, seq, _ = x.shape

    def body(dh_ref, dx1_ref, x_ref, w_ref, sc_ref, gx_ref, dsc_ref, dsh_ref, dw_ref):
        b, i = pl.program_id(0), pl.program_id(1)
        w = w_ref[...]
        dh = dh_ref[...]
        y, xh, rstd = _rms_fwd(x_ref[...], w)
        _acc_out(dsh_ref, i == 0, _colsum(dh))
        _acc_out(dsc_ref, i == 0, _colsum(dh * y))
        dx, dw_rows = _rms_bwd(dh * (1.0 + sc_ref[...]), xh, rstd, w)
        _acc_out(dw_ref, jnp.logical_and(b == 0, i == 0), _colsum(dw_rows))
        gx_ref[...] = dx1_ref[...] + dx

    row_shape = _sds((bsz, 1, D_MODEL), F32)
    return pl.pallas_call(
        body, name="norm1_bwd", grid=(bsz, seq // tm),
        in_specs=[_tok_spec(tm), _tok_spec(tm), _tok_spec(tm), _vec_spec(), _row_spec()],
        out_specs=[_tok_spec(tm), _row_spec(), _row_spec(), _vec_spec()],
        out_shape=[_sds(x.shape, F32), row_shape, row_shape, _sds((1, D_MODEL), F32)],
        compiler_params=_params("arbitrary", "arbitrary"),
    )(dh1, dx1, x, pre_w, sc1)


def _hgrn_bwd(dcat, proj, o_raw, states, lb_table, norm_w):
    bsz, seq, _ = proj.shape
    nstep = seq // HG_TOK
    rec0 = ATT_WIDTH // LANE

    def body(tbl_ref, nw_ref, dr_ref, q_ref, f_ref, i_ref, g_ref, o_ref, st_ref,
             dq_ref, df_ref, di_ref, dg_ref, dlb_ref, dnw_ref, ds_scr):
        h, b, t = pl.program_id(0), pl.program_id(1), pl.program_id(2)

        @pl.when(t == 0)
        def _():
            ds_scr[...] = jnp.zeros_like(ds_scr)

        lower, upper = _chunk_consts()
        tri_lo, tri_up = lower.astype(F32), upper.astype(F32)
        last_row = lax.broadcasted_iota(jnp.int32, (HG_CHUNK, LANE), 0) == HG_CHUNK - 1
        nw = nw_ref[...]
        dlb_acc = jnp.zeros((1, LANE), F32)
        dnw_acc = jnp.zeros((1, LANE), F32)
        for j in reversed(range(HG_NCH)):
            sl = slice(j * HG_CHUNK, (j + 1) * HG_CHUNK)
            hq, v, hg = q_ref[sl, :], i_ref[sl, :], g_ref[sl, :]
            lb, sig, f, sq = _hgrn_gates(tbl_ref[...], f_ref[sl, :], hq)
            q, k = hq * sq, 1.0 - f
            b_cum = _cumsum_rows(tri_lo, jnp.log(f))
            bl = b_cum[HG_CHUNK - 1:HG_CHUNK, :]
            e_b, e_nb, e_bl, e_rem = jnp.exp(b_cum), jnp.exp(-b_cum), jnp.exp(bl), jnp.exp(bl - b_cum)
            qd, kd, ku = q * e_b, k * e_nb, k * e_rem
            st = st_ref[j]
            y, on, rstd = _rms_fwd(o_ref[sl, :], nw)
            sg = _sigmoid(hg)
            dr = dr_ref[sl, :]
            dg_ref[sl, :] = (dr * y * (sg * (1.0 + hg * (1.0 - sg)))).astype(BF16)
            do, dnw_rows = _rms_bwd(dr * (hg * sg), on, rstd, nw)
            dnw_acc += _colsum(dnw_rows)
            dsp = ds_scr[...]
            at = jnp.where(upper, _dot(kd, qd, NT_DIMS), 0.0)
            da = jnp.where(lower, _dot(do, v, NT_DIMS), 0.0)
            dat = jnp.where(upper, _dot(v, do, NT_DIMS), 0.0)
            dv = _dot(at, do) + _dot(ku, dsp, NT_DIMS)
            dqd = _dot(da, kd) + _dot(do, st)
            dkd = _dot(dat, qd)
            dku = _dot(v, dsp)
            dbl = _colsum(st * dsp) * e_bl + _colsum(dku * ku)
            ds_scr[...] = _dot(do, qd, TN_DIMS) + dsp * e_bl
            dk = dkd * e_nb + dku * e_rem
            db = dqd * qd - dkd * kd - dku * ku
            db = db + jnp.where(last_row, dbl, 0.0)
            dlogf = _cumsum_rows(tri_up, db)
            dfv = dlogf / f - dk
            df_ref[sl, :] = (dfv * (1.0 - lb) * sig * (1.0 - sig)).astype(BF16)
            dlb_acc += _colsum(dfv * (1.0 - sig))
            dq_ref[sl, :] = (dqd * e_b * (sq * (1.0 + hq * (1.0 - sq)))).astype(BF16)
            di_ref[sl, :] = dv.astype(BF16)
        _acc_out(dlb_ref, jnp.logical_and(b == 0, t == 0), dlb_acc)
        _acc_out(dnw_ref, jnp.logical_and(h == 0, jnp.logical_and(b == 0, t == 0)), dnw_acc)

    rev = lambda t: nstep - 1 - t
    slab = lambda first: pl.BlockSpec((None, HG_TOK, LANE), lambda h, b, t: (b, rev(t), first + h))
    head = pl.BlockSpec((None, HG_TOK, LANE), lambda h, b, t: (b, rev(t), h))
    grad_shape = _sds((bsz, seq, HG_WIDTH), BF16)
    return pl.pallas_call(
        body, name="hgrn_bwd", grid=(HG_HEADS, bsz, nstep),
        in_specs=[pl.BlockSpec((2, LANE), lambda h, b, t: (0, h)), pl.BlockSpec((1, LANE), lambda h, b, t: (0, 0)),
                  slab(rec0), slab(HG_Q0), slab(HG_F0), slab(HG_I0), slab(HG_G0), head,
                  pl.BlockSpec((None, None, HG_NCH, LANE, LANE), lambda h, b, t: (b, h, rev(t), 0, 0))],
        out_specs=[head, head, head, head, pl.BlockSpec((1, LANE), lambda h, b, t: (0, h)),
                   pl.BlockSpec((1, LANE), lambda h, b, t: (0, 0))],
        out_shape=[grad_shape, grad_shape, grad_shape, grad_shape, _sds((1, HG_WIDTH), F32), _sds((1, LANE), F32)],
        scratch_shapes=[pltpu.VMEM((LANE, LANE), F32)],
        compiler_params=_params("arbitrary", "arbitrary", "arbitrary"),
    )(lb_table, norm_w, dcat, proj, proj, proj, proj, o_raw, states)


def _attn_bwd(dcat, raw, w_norm, qr, kr, vb, lse, sinks):
    bsz, seq, _ = qr.shape
    nblk = seq // WINDOW

    def body(sink_ref, da_ref, raw_ref, w_ref, q_ref, kc_ref, kp_ref, vc_ref, vp_ref, l_ref,
             dq_ref, dkd_ref, dkp_ref, dvd_ref, dvp_ref, dw_ref, dsink_ref):
        b, i = pl.program_id(0), pl.program_id(1)
        first = jnp.logical_and(b == 0, i == 0)
        mask_c, mask_p = _band_masks(i)
        raw_v = raw_ref[...]
        w = w_ref[...]
        _, on, rstd = _rms_fwd(raw_v, w)
        do_all, dw_rows = _rms_bwd(da_ref[...], on, rstd, w)
        _acc_out(dw_ref, first, _colsum(dw_rows))
        lane8 = lax.broadcasted_iota(jnp.int32, (1, ATT_Q_HEADS), 1)
        dsink = jnp.zeros((1, ATT_Q_HEADS), F32)
        for g in range(ATT_KV_HEADS):
            gs = slice(g * ATT_HEAD_DIM, (g + 1) * ATT_HEAD_DIM)
            kc, kp, vc, vp = kc_ref[:, gs], kp_ref[:, gs], vc_ref[:, gs], vp_ref[:, gs]
            dkc = jnp.zeros((WINDOW, ATT_HEAD_DIM), F32)
            dkp, dvc, dvp = dkc, dkc, dkc
            for hh in range(ATT_GROUP):
                h = g * ATT_GROUP + hh
                hs = slice(h * ATT_HEAD_DIM, (h + 1) * ATT_HEAD_DIM)
                q = q_ref[:, hs]
                doh = do_all[:, hs]
                dsum = jnp.sum(doh * raw_v[:, hs], axis=-1, keepdims=True)
                lse_h = l_ref[:, h:h + 1]
                pc = jnp.where(mask_c, jnp.exp(_dot(q, kc, NT_DIMS) * ATT_SCALE - lse_h), 0.0)
                pp = jnp.where(mask_p, jnp.exp(_dot(q, kp, NT_DIMS) * ATT_SCALE - lse_h), 0.0)
                p_sink = jnp.exp(sink_ref[0, h] - lse_h)
                dsink = dsink - jnp.where(lane8 == h, jnp.sum(p_sink * dsum, axis=0, keepdims=True), 0.0)
                dsc = pc * (_dot(doh, vc, NT_DIMS) - dsum) * ATT_SCALE
                dsp = pp * (_dot(doh, vp, NT_DIMS) - dsum) * ATT_SCALE
                dq_ref[:, hs] = _dot(dsc, kc) + _dot(dsp, kp)
                dkc += _dot(dsc, q, TN_DIMS)
                dkp += _dot(dsp, q, TN_DIMS)
                dvc += _dot(pc, doh, TN_DIMS)
                dvp += _dot(pp, doh, TN_DIMS)
            dkd_ref[:, gs], dkp_ref[:, gs], dvd_ref[:, gs], dvp_ref[:, gs] = dkc, dkp, dvc, dvp
        _acc_out(dsink_ref, first, dsink)

    cur = lambda width: pl.BlockSpec((None, WINDOW, width), lambda b, i: (b, i, 0))
    kv_shape = _sds((bsz, seq, LANE), F32)
    return pl.pallas_call(
        body, name="attn_bwd", grid=(bsz, nblk),
        in_specs=[pl.BlockSpec(memory_space=pltpu.SMEM), cur(ATT_WIDTH), cur(ATT_WIDTH), _vec_spec(ATT_WIDTH), cur(ATT_WIDTH),
                  cur(LANE), _prev_spec(LANE), cur(LANE), _prev_spec(LANE), cur(ATT_Q_HEADS)],
        out_specs=[cur(ATT_WIDTH), cur(LANE), cur(LANE), cur(LANE), cur(LANE), _vec_spec(ATT_WIDTH), _vec_spec(ATT_Q_HEADS)],
        out_shape=[_sds((bsz, seq, ATT_WIDTH), F32), kv_shape, kv_shape, kv_shape, kv_shape, _sds((1, ATT_WIDTH), F32),
                   _sds((1, ATT_Q_HEADS), F32)],
        compiler_params=_params("arbitrary", "arbitrary"),
    )(sinks, dcat, raw, w_norm, qr, kr, kr, vb, vb, lse)


def _rope_bwd(dq, dkd, dkp, dvd, dvp, tables):
    bsz, seq, _ = dq.shape
    nblk = seq // WINDOW
    half = ROPE_DIM // 2

    def body(dq_ref, dkd_ref, dkp_ref, dvd_ref, dvp_ref, c_ref, u_ref, d_ref, o_ref):
        c, u, d = c_ref[...], u_ref[...], d_ref[...]
        has_next = pl.program_id(1) < nblk - 1

        def unrope(g):
            return g * c + pltpu.roll(g * u, LANE - half, 1) + pltpu.roll(g * d, half, 1)

        for s in range(ATT_WIDTH // LANE):
            o_ref[:, s * LANE:(s + 1) * LANE] = unrope(dq_ref[:, s * LANE:(s + 1) * LANE]).astype(BF16)
        dk = dkd_ref[...] + jnp.where(has_next, dkp_ref[...], 0.0)
        o_ref[:, ATT_WIDTH:ATT_WIDTH + LANE] = unrope(dk).astype(BF16)
        o_ref[:, ATT_WIDTH + LANE:ATT_COLS] = (dvd_ref[...] + jnp.where(has_next, dvp_ref[...], 0.0)).astype(BF16)

    cur = lambda width: pl.BlockSpec((None, WINDOW, width), lambda b, i: (b, i, 0))
    nxt = pl.BlockSpec((None, WINDOW, LANE), lambda b, i: (b, jnp.minimum(i + 1, nblk - 1), 0))
    tab = pl.BlockSpec((WINDOW, LANE), lambda b, i: (i, 0))
    return pl.pallas_call(
        body, name="rope_bwd", grid=(bsz, nblk),
        in_specs=[cur(ATT_WIDTH), cur(LANE), nxt, cur(LANE), nxt, tab, tab, tab],
        out_specs=cur(ATT_COLS), out_shape=_sds((bsz, seq, ATT_COLS), BF16),
        compiler_params=_params("parallel", "parallel"),
    )(dq, dkd, dkp, dvd, dvp, *tables)


def _local_step(x, mods, target, w_in, w_up4, w_out, w_down, pre_w_mix, attn_sinks, attn_out_w, lb_table, hg_norm_w,
                post_w_mix, pre_w_mlp, post_w_mlp):
    bsz, seq, _ = x.shape
    ntok = bsz * seq
    sh1, sc1, g1, sh2, sc2, g2 = mods
    flat = lambda a: a.reshape(ntok, a.shape[-1])
    unflat = lambda a: a.reshape(bsz, seq, a.shape[-1])
    tables = _rope_tables(seq)

    h1 = _prenorm1(x, pre_w_mix, sc1, sh1)
    proj = unflat(_mm(flat(h1), w_in, name="in_proj", out_dtype=F32))
    qr, kr, vb = _rope_fwd(proj, tables)
    attn_raw, attn_n, lse = _attn_fwd(qr, kr, vb, attn_sinks, attn_out_w)
    o_raw, rec, states = _hgrn_fwd(proj, lb_table, hg_norm_w)
    cat = jnp.concatenate([attn_n, rec], axis=-1)
    mix = unflat(_mm(flat(cat), w_out, name="out_proj", out_dtype=F32))
    x1, h2 = _mid_fwd(x, mix, post_w_mix, g1, pre_w_mlp, sc2, sh2)
    up_spec = pl.BlockSpec((None, D_MODEL, D_MODEL), lambda i, j, k: (j, 0, 0))
    r = _mm(flat(h2), w_up4, name="up_proj", out_dtype=BF16, tn=D_MODEL, n_out=D_FF, b_spec=up_spec,
            epi=lambda acc: jnp.maximum(acc, 0.0))
    square = lambda t: t * t
    down = unflat(_mm(r, w_down, name="down_proj", out_dtype=F32, a_fn=square))
    loss_row, dy, dd, dg2, d_post_mlp = _loss_bwd(x1, down, post_w_mlp, g2, target)

    dpre = _mm(flat(dd), w_down, name="down_bwd", out_dtype=BF16, trans_b=True, tn=D_MODEL, extra=(r,),
               epi=lambda acc, rt: acc * (2.0 * rt.astype(F32)))
    half_rows = D_MODEL // 2
    g_down = _mm_tn(r, flat(dd), name="down_wgrad", tk=half_rows, tn=D_MODEL, a_fn=square,
                    out_shape=_sds((2, N_CHIPS, half_rows, D_MODEL), F32),
                    out_spec=pl.BlockSpec((None, None, half_rows, D_MODEL), lambda i, j, m: (i % 2, i // 2, 0, 0)))
    up_t_spec = pl.BlockSpec((None, D_MODEL, D_MODEL), lambda i, j, k: (k, 0, 0))
    dh2 = unflat(_mm(dpre, w_up4, name="up_bwd", out_dtype=F32, trans_b=True, nk=N_CHIPS, n_out=D_MODEL, b_spec=up_t_spec,
                     k_total=D_FF))
    g_up = _mm_tn(flat(h2), dpre, name="up_wgrad", tk=half_rows, tn=D_MODEL,
                  out_shape=_sds((2, N_CHIPS, half_rows, D_MODEL), F32),
                  out_spec=pl.BlockSpec((None, None, half_rows, D_MODEL), lambda i, j, m: (i, j, 0, 0)))
    dx1, dmix, dsc2, dsh2, dg1, d_pre_mlp, d_post_mix = _mid_bwd(dh2, dy, x1, mix, pre_w_mlp, sc2, post_w_mix, g1)

    dcat = unflat(_mm(flat(dmix), w_out, name="out_bwd", out_dtype=F32, trans_b=True))
    out_rows = D_MODEL // N_CHIPS
    g_out = _mm_tn(flat(cat), flat(dmix), name="out_wgrad", tk=out_rows, tn=half_rows,
                   out_shape=_sds((2, N_CHIPS, out_rows, half_rows), F32),
                   out_spec=pl.BlockSpec((None, None, out_rows, half_rows), lambda i, j, m: (j, i, 0, 0)))
    dhq, dhf, dhi, dhg, d_lb, d_hg_norm = _hgrn_bwd(dcat, proj, o_raw, states, lb_table, hg_norm_w)
    dq, dkd, dkp, dvd, dvp, d_attn_out, d_sinks = _attn_bwd(dcat, attn_raw, attn_out_w, qr, kr, vb, lse, attn_sinks)
    dproj_a = _rope_bwd(dq, dkd, dkp, dvd, dvp, tables)
    dproj = flat(jnp.concatenate([dproj_a, dhq, dhf, dhi, dhg], axis=-1))
    dh1 = unflat(_mm(dproj, w_in, name="in_bwd", out_dtype=F32, trans_b=True))
    g_in = _mm_tn(flat(h1), dproj, name="in_wgrad", tk=D_MODEL, tn=IN_COLS // 2)
    grad_x, dsc1, dsh1, d_pre_mix = _norm1_bwd(dh1, dx1, x, pre_w_mix, sc1)

    in_cols = IN_COLS // N_CHIPS
    g_in = g_in.reshape(2, half_rows, N_CHIPS, in_cols).transpose(0, 2, 1, 3)
    dmod = jnp.concatenate([dsh1, dsc1, dg1, dsh2, dsc2, dg2], axis=-1).reshape(bsz, N_MOD * D_MODEL)
    small = dict(pre_w_mix=d_pre_mix, post_w_mix=d_post_mix, pre_w_mlp=d_pre_mlp, post_w_mlp=d_post_mlp,
                 attn_out_w=d_attn_out, hg_norm_w=d_hg_norm, attn_sinks=d_sinks, lb=d_lb)
    return loss_row[0, 0], grad_x, (g_in, g_up, g_out, g_down), dmod, small


HBM_SPEC = pl.BlockSpec(memory_space=pltpu.HBM)


def _mesh_pos():
    return lax.axis_index("x"), lax.axis_index("y"), lax.axis_index("c")


def _allgather8(arrays, name):
    n = len(arrays)

    def body(*refs):
        ins, outs = refs[:n], refs[n:2 * n]
        send_sems, recv_sems, local_sems = refs[2 * n:]
        x, y, c = _mesh_pos()
        me, sibling = (x, y, c), (x, y, 1 - c)
        chips = [(1 - x, y), (x, 1 - y), (1 - x, 1 - y)]

        def copy(a, k, block, to, src=None):
            dst = outs[a].at[4 * block[0] + 2 * block[1] + block[2]]
            return pltpu.make_async_remote_copy(
                src_ref=dst if src is None else src, dst_ref=dst, send_sem=send_sems.at[7 * a + k],
                recv_sem=recv_sems.at[7 * a + k], device_id=to, device_id_type=MESH)

        mine = [pltpu.make_async_copy(ins[a], outs[a].at[4 * x + 2 * y + c], local_sems.at[a]) for a in range(n)]
        for cp in mine:
            cp.start()
        first = []
        for a in range(n):
            first.append(copy(a, 0, me, sibling, src=ins[a]))
            first += [copy(a, 1 + j, me, (*chip, c), src=ins[a]) for j, chip in enumerate(chips)]
        for cp in first:
            cp.start()
        passed = []
        for j, chip in enumerate(chips):
            for a in range(n):
                copy(a, 1 + j, (*chip, c), me).wait_recv()
                fwd = copy(a, 4 + j, (*chip, c), sibling)
                fwd.start()
                passed.append(fwd)
        for a in range(n):
            copy(a, 0, sibling, me).wait_recv()
            for j, chip in enumerate(chips):
                copy(a, 4 + j, (*chip, 1 - c), me).wait_recv()
        for cp in first + passed:
            cp.wait_send()
        for cp in mine:
            cp.wait()

    return pl.pallas_call(
        body, name=name, in_specs=[HBM_SPEC] * n, out_specs=[HBM_SPEC] * n,
        out_shape=[_sds((N_DEV,) + a.shape, a.dtype) for a in arrays],
        scratch_shapes=[pltpu.SemaphoreType.DMA((7 * n,)), pltpu.SemaphoreType.DMA((7 * n,)), pltpu.SemaphoreType.DMA((n,))],
    )(*arrays)


def _pair_send_other_half(arrays, name):
    n = len(arrays)

    def body(*refs):
        ins, outs = refs[:n], refs[n:2 * n]
        send_sems, recv_sems = refs[2 * n:]
        x, y, c = _mesh_pos()
        copies = [pltpu.make_async_remote_copy(
            src_ref=ins[a].at[1 - c], dst_ref=outs[a], send_sem=send_sems.at[a], recv_sem=recv_sems.at[a],
            device_id=(x, y, 1 - c), device_id_type=MESH) for a in range(n)]
        for cp in copies:
            cp.start()
        for cp in copies:
            cp.wait()

    return pl.pallas_call(
        body, name=name, in_specs=[HBM_SPEC] * n, out_specs=[HBM_SPEC] * n,
        out_shape=[_sds(a.shape[1:], a.dtype) for a in arrays],
        scratch_shapes=[pltpu.SemaphoreType.DMA((n,)), pltpu.SemaphoreType.DMA((n,))],
    )(*arrays)


def _chip_exchange(arrays, name):
    n = len(arrays)

    def body(*refs):
        ins, outs = refs[:n], refs[n:2 * n]
        send_sems, recv_sems, local_sems = refs[2 * n:]
        x, y, c = _mesh_pos()
        my_chip = 2 * x + y
        chips = [(1 - x, y), (x, 1 - y), (1 - x, 1 - y)]
        mine = [pltpu.make_async_copy(ins[a].at[my_chip], outs[a].at[my_chip], local_sems.at[a]) for a in range(n)]
        for cp in mine:
            cp.start()
        sends = []
        for a in range(n):
            for j, chip in enumerate(chips):
                sends.append(pltpu.make_async_remote_copy(
                    src_ref=ins[a].at[2 * chip[0] + chip[1]], dst_ref=outs[a].at[my_chip], send_sem=send_sems.at[3 * a + j],
                    recv_sem=recv_sems.at[3 * a + j], device_id=(*chip, c), device_id_type=MESH))
        for cp in sends:
            cp.start()
        for a in range(n):
            for j, chip in enumerate(chips):
                landed = outs[a].at[2 * chip[0] + chip[1]]
                pltpu.make_async_remote_copy(
                    src_ref=landed, dst_ref=landed, send_sem=send_sems.at[3 * a + j], recv_sem=recv_sems.at[3 * a + j],
                    device_id=(*chip, c), device_id_type=MESH).wait_recv()
        for cp in sends:
            cp.wait_send()
        for cp in mine:
            cp.wait()

    return pl.pallas_call(
        body, name=name, in_specs=[HBM_SPEC] * n, out_specs=[HBM_SPEC] * n,
        out_shape=[_sds(a.shape, a.dtype) for a in arrays],
        scratch_shapes=[pltpu.SemaphoreType.DMA((3 * n,)), pltpu.SemaphoreType.DMA((3 * n,)), pltpu.SemaphoreType.DMA((n,))],
    )(*arrays)


def _pair_swap(arrays, name):
    n = len(arrays)

    def body(*refs):
        ins, outs = refs[:n], refs[n:2 * n]
        send_sems, recv_sems = refs[2 * n:]
        x, y, c = _mesh_pos()
        copies = [pltpu.make_async_remote_copy(
            src_ref=ins[a], dst_ref=outs[a], send_sem=send_sems.at[a], recv_sem=recv_sems.at[a],
            device_id=(x, y, 1 - c), device_id_type=MESH) for a in range(n)]
        for cp in copies:
            cp.start()
        for cp in copies:
            cp.wait()

    return pl.pallas_call(
        body, name=name, in_specs=[HBM_SPEC] * n, out_specs=[HBM_SPEC] * n,
        out_shape=[_sds(a.shape, a.dtype) for a in arrays],
        scratch_shapes=[pltpu.SemaphoreType.DMA((n,)), pltpu.SemaphoreType.DMA((n,))],
    )(*arrays)


def _pair_sum(g, q, core, name):
    _, nblk, rows, cols = g.shape
    tr = min(rows, 256)

    def body(core_ref, g_ref, q_ref, o_ref):
        o_ref[...] = (g_ref[...] + q_ref[...]).astype(BF16)

    blk = pl.BlockSpec((None, tr, cols), lambda k, i, core_ref: (k, i, 0))
    return pl.pallas_call(
        body, name=name,
        grid_spec=pltpu.PrefetchScalarGridSpec(
            num_scalar_prefetch=1, grid=(nblk, rows // tr),
            in_specs=[pl.BlockSpec((None, None, tr, cols), lambda k, i, core_ref: (core_ref[0], k, i, 0)), blk],
            out_specs=blk),
        out_shape=_sds((nblk, rows, cols), BF16), compiler_params=_params("parallel", "parallel"),
    )(core, g, q)


def _sum_chips(xs, name):
    nblk, rows, cols = xs.shape
    tr = min(rows, 256)

    def body(x_ref, o_ref):
        acc = x_ref[0].astype(F32)
        for j in range(1, nblk):
            acc = acc + x_ref[j].astype(F32)
        o_ref[...] = acc

    return pl.pallas_call(
        body, name=name, grid=(rows // tr,),
        in_specs=[pl.BlockSpec((nblk, tr, cols), lambda i: (0, i, 0))], out_specs=pl.BlockSpec((tr, cols), lambda i: (i, 0)),
        out_shape=_sds((rows, cols), F32), compiler_params=_params("parallel"),
    )(xs)


SMALL_ROWS = 120
PACK_ROWS = 160


def _rows(a, nrows):
    flat = a.reshape(-1)
    return jnp.pad(flat, (0, nrows * LANE - flat.shape[0])).reshape(nrows, LANE)


def _pack_small(b_ada, pre_w_mix, post_w_mix, pre_w_mlp, post_w_mlp, attn_out_w, hg_norm_w, attn_sinks, lb_table):
    return jnp.concatenate([
        _rows(b_ada, 48), _rows(pre_w_mix, 8), _rows(post_w_mix, 8), _rows(pre_w_mlp, 8), _rows(post_w_mlp, 8),
        _rows(attn_out_w, 8), _rows(hg_norm_w, 8), _rows(attn_sinks, 8), _rows(lb_table[0], 8), _rows(lb_table[1], 8)], axis=0)


def _unpack_small(p):
    vec = lambda lo, n: p[lo:lo + n // LANE].reshape(1, n)
    lb = jnp.stack([p[104:108].reshape(HG_WIDTH), p[112:116].reshape(HG_WIDTH)])
    return dict(b_ada=vec(0, N_MOD * D_MODEL), pre_w_mix=vec(48, D_MODEL), post_w_mix=vec(56, D_MODEL), pre_w_mlp=vec(64, D_MODEL),
                post_w_mlp=vec(72, D_MODEL), attn_out_w=vec(80, ATT_WIDTH), hg_norm_w=p[88:89], attn_sinks=p[96:97, :ATT_Q_HEADS],
                lb_table=lb)


def _small_update(packs, w, m, v):
    def body(p_ref, w_ref, m_ref, v_ref, g_ref, dl_ref, nm_ref, nv_ref):
        tot = p_ref[0]
        for d in range(1, N_DEV):
            tot = tot + p_ref[d]
        wv = w_ref[...]
        p1 = _sigmoid(wv[112:120] - wv[104:112])
        s = tot[152:160] * p1 * (1.0 - p1)
        g = jnp.concatenate([tot[0:48] + tot[48:96], tot[96:152], -s, s], axis=0)
        g_ref[...] = g
        dl_ref[...], nm_ref[...], nv_ref[...] = _adamw_math(g, wv, m_ref[...], v_ref[...])

    shp = _sds((SMALL_ROWS, LANE), F32)
    return pl.pallas_call(body, name="small_update", out_shape=[shp] * 4, compiler_params=_params())(packs, w, m, v)


def kernel(x, c, w_ada, b_ada, pre_w_mix, w_in, attn_sinks, attn_out_w, lb_table, hg_norm_w, w_out, post_w_mix, pre_w_mlp, w_up, w_down, post_w_mlp, loss_target, m_w_ada, m_b_ada, m_pre_w_mix, m_w_in, m_attn_sinks, m_attn_out_w, m_lb_table, m_hg_norm_w, m_w_out, m_post_w_mix, m_pre_w_mlp, m_w_up, m_w_down, m_post_w_mlp, v_w_ada, v_b_ada, v_pre_w_mix, v_w_in, v_attn_sinks, v_attn_out_w, v_lb_table, v_hg_norm_w, v_w_out, v_post_w_mix, v_pre_w_mlp, v_w_up, v_w_down, v_post_w_mlp):
    xi, yi, ci = _mesh_pos()
    chip = 2 * xi + yi
    dev = 2 * chip + ci
    bsz = x.shape[0]
    ada_cols = w_ada.shape[2]

    def row_half(w):
        rows = w.shape[1] // 2
        return lax.dynamic_slice_in_dim(w[0], ci * rows, rows, axis=0).astype(BF16)

    c_g, in_g, up_g, out_g, down_g = _allgather8([c, row_half(w_in), row_half(w_up), row_half(w_out), row_half(w_down)],
                                                 "gather_weights")
    c_all = c_g.reshape(N_DEV * bsz, D_MODEL)
    w_in_full = in_g.reshape(N_CHIPS, D_MODEL, IN_COLS // N_CHIPS).transpose(1, 0, 2).reshape(D_MODEL, IN_COLS)
    w_up4 = up_g.reshape(N_CHIPS, D_MODEL, D_MODEL)
    w_out_full = out_g.reshape(D_MODEL, D_MODEL)
    w_down_full = down_g.reshape(D_FF, D_MODEL)

    b_cols = lax.dynamic_slice_in_dim(b_ada, chip * ada_cols, ada_cols, axis=1)
    mod_part = _ada_fwd(c_all, w_ada[0], b_cols)
    half_rows = mod_part.shape[0] // 2
    (mod_g,) = _allgather8([lax.dynamic_slice_in_dim(mod_part, ci * half_rows, half_rows, axis=0)], "gather_mod")
    mod_all = mod_g.reshape(N_CHIPS, 2, half_rows, ada_cols).transpose(1, 2, 0, 3).reshape(N_DEV * bsz, N_MOD * D_MODEL)
    mod = lax.dynamic_slice_in_dim(mod_all, dev * bsz, bsz, axis=0)
    mods = [mod[:, i * D_MODEL:(i + 1) * D_MODEL].reshape(bsz, 1, D_MODEL) for i in range(N_MOD)]

    loss_part, grad_x, big_grads, dmod, small = _local_step(
        x, mods, loss_target, w_in_full, w_up4, w_out_full, w_down_full, pre_w_mix, attn_sinks, attn_out_w, lb_table,
        hg_norm_w, post_w_mix, pre_w_mlp, post_w_mlp)
    loss = lax.psum(loss_part, ("x", "y", "c"))

    pack = jnp.concatenate([
        _rows(dmod, 96), _rows(small["pre_w_mix"], 8), _rows(small["post_w_mix"], 8), _rows(small["pre_w_mlp"], 8),
        _rows(small["post_w_mlp"], 8), _rows(small["attn_out_w"], 8), _rows(small["hg_norm_w"], 8), _rows(small["attn_sinks"], 8),
        _rows(small["lb"], 8)], axis=0)
    (packs,) = _allgather8([pack], "gather_small")
    small_args = lambda pre: (pre["b_ada"], pre["pre_w_mix"], pre["post_w_mix"], pre["pre_w_mlp"], pre["post_w_mlp"],
                              pre["attn_out_w"], pre["hg_norm_w"], pre["attn_sinks"], pre["lb_table"])
    w_small = dict(b_ada=b_ada, pre_w_mix=pre_w_mix, post_w_mix=post_w_mix, pre_w_mlp=pre_w_mlp, post_w_mlp=post_w_mlp,
                   attn_out_w=attn_out_w, hg_norm_w=hg_norm_w, attn_sinks=attn_sinks, lb_table=lb_table)
    m_small = dict(b_ada=m_b_ada, pre_w_mix=m_pre_w_mix, post_w_mix=m_post_w_mix, pre_w_mlp=m_pre_w_mlp, post_w_mlp=m_post_w_mlp,
                   attn_out_w=m_attn_out_w, hg_norm_w=m_hg_norm_w, attn_sinks=m_attn_sinks, lb_table=m_lb_table)
    v_small = dict(b_ada=v_b_ada, pre_w_mix=v_pre_w_mix, post_w_mix=v_post_w_mix, pre_w_mlp=v_pre_w_mlp, post_w_mlp=v_post_w_mlp,
                   attn_out_w=v_attn_out_w, hg_norm_w=v_hg_norm_w, attn_sinks=v_attn_sinks, lb_table=v_lb_table)
    small_out = [_unpack_small(p) for p in _small_update(packs, _pack_small(*small_args(w_small)), _pack_small(*small_args(m_small)),
                                                         _pack_small(*small_args(v_small)))]

    dmod_all = packs[:, :96, :].reshape(N_DEV * bsz, N_MOD * D_MODEL)
    dmod_cols = lax.dynamic_slice_in_dim(dmod_all, chip * ada_cols, ada_cols, axis=1)
    ada_out = _ada_bwd_adamw(c_all, dmod_cols, w_ada[0], m_w_ada[0], v_w_ada[0])

    core = jnp.reshape(ci, (1,)).astype(jnp.int32)
    names = ("in", "up", "out", "down")
    others = _pair_send_other_half(list(big_grads), "pair_reduce_send")
    pair_sums = [_pair_sum(g, q, core, f"pair_sum_{nm}") for g, q, nm in zip(big_grads, others, names)]
    landed = _chip_exchange(pair_sums, "chip_exchange")
    halves = [_sum_chips(xs, f"sum_chips_{nm}") for xs, nm in zip(landed, names)]
    theirs = _pair_swap(halves, "pair_swap")
    big = dict(
        w_in=tuple(_adamw_halves(halves[0], theirs[0], core, w_in[0], m_w_in[0], v_w_in[0], axis=0, name="adamw_in")),
        w_up=tuple(_adamw_halves(halves[1], theirs[1], core, w_up[0], m_w_up[0], v_w_up[0], axis=0, name="adamw_up")),
        w_out=tuple(_adamw_halves(halves[2], theirs[2], core, w_out[0], m_w_out[0], v_w_out[0], axis=1, name="adamw_out")),
        w_down=tuple(_adamw_halves(halves[3], theirs[3], core, w_down[0], m_w_down[0], v_w_down[0], axis=0, name="adamw_down")),
        w_ada=tuple(ada_out),
    )
    order = ("w_ada", "b_ada", "pre_w_mix", "w_in", "attn_sinks", "attn_out_w", "lb_table", "hg_norm_w", "w_out", "post_w_mix",
             "pre_w_mlp", "w_up", "w_down", "post_w_mlp")
    outs = [loss, grad_x]
    for kind in range(4):
        for nm in order:
            outs.append(big[nm][kind][None] if nm in big else small_out[kind][nm])
    return tuple(outs)
```

```python
import functools

import jax
import jax.numpy as jnp
from jax import lax
from jax.experimental import pallas as pl
from jax.experimental.pallas import tpu as pltpu

F32 = jnp.float32
BF16 = jnp.bfloat16

D_MODEL = 1024
ATT_WIDTH = 512
ATT_HEAD_DIM = 64
ATT_Q_HEADS = 8
ATT_KV_HEADS = 2
ATT_GROUP = ATT_Q_HEADS // ATT_KV_HEADS
ATT_KV_COLS = ATT_KV_HEADS * ATT_HEAD_DIM
WINDOW = 128
ROPE_DIM = 16
ROPE_THETA = 500000.0
HG_WIDTH = 512
HG_HEAD_DIM = 128
HG_HEADS = 4
HG_CHUNK = 32
IN_COLS = ATT_WIDTH + 2 * ATT_KV_COLS + 4 * HG_WIDTH
ATT_COLS = ATT_WIDTH + 2 * ATT_KV_COLS
D_FF = 4 * D_MODEL
N_MOD = 6
EPS = 1e-6
ATT_SCALE = ATT_HEAD_DIM ** -0.5

ADAM_LR = 0.001
ADAM_B1 = 0.9
ADAM_B2 = 0.999
ADAM_EPS = 1e-08
ADAM_WD = 0.01
ADAM_STEP = 10

N_CHIPS = 4
N_DEV = 8
LANE = 128
VMEM_LIMIT = 48 * 1024 * 1024
MESH = pl.DeviceIdType.MESH

NT_DIMS = (((1,), (1,)), ((), ()))
TN_DIMS = (((0,), (0,)), ((), ()))


def _sds(shape, dtype):
    return jax.ShapeDtypeStruct(tuple(shape), dtype)


def _params(*sem):
    return pltpu.CompilerParams(dimension_semantics=sem, vmem_limit_bytes=VMEM_LIMIT)


def _sigmoid(x):
    return 1.0 / (1.0 + jnp.exp(-x))


def _dot(a, b, dims=None):
    a, b = a.astype(BF16), b.astype(BF16)
    if dims is None:
        return jnp.dot(a, b, preferred_element_type=F32)
    return lax.dot_general(a, b, dims, preferred_element_type=F32)


def _rms_fwd(x, w):
    rstd = lax.rsqrt(jnp.mean(x * x, axis=-1, keepdims=True) + EPS)
    xh = x * rstd
    return xh * w, xh, rstd


def _rms_bwd(dy, xh, rstd, w):
    dxh = dy * w
    dx = rstd * (dxh - xh * jnp.mean(dxh * xh, axis=-1, keepdims=True))
    return dx, dy * xh


def _colsum(x):
    return jnp.sum(x, axis=0, keepdims=True)


HBM_SPEC = pl.BlockSpec(memory_space=pltpu.HBM)


def _mesh_pos():
    return lax.axis_index("x"), lax.axis_index("y"), lax.axis_index("c")


class _Comm:
    def __init__(self, ins, outs, sems, start, finish, aliases=()):
        self.ins, self.outs, self.sems = list(ins), list(outs), list(sems)
        self.start, self.finish, self.aliases = start, finish, tuple(aliases)


def _call(body, args, *, name, grid, in_specs, out_specs, out_shape, sem, scratch_shapes=(), comms=()):
    scratch_shapes = list(scratch_shapes)
    if not comms:
        return pl.pallas_call(body, name=name, grid=grid, in_specs=in_specs, out_specs=out_specs, out_shape=out_shape,
                              scratch_shapes=scratch_shapes, compiler_params=_params(*sem))(*args)
    single = not isinstance(out_shape, (list, tuple))
    out_specs_l = [out_specs] if single else list(out_specs)
    out_shape_l = [out_shape] if single else list(out_shape)
    n_in, n_out, n_scr = len(in_specs), len(out_shape_l), len(scratch_shapes)
    n_ci = [len(cm.ins) for cm in comms]
    n_co = [len(cm.outs) for cm in comms]
    n_cs = [len(cm.sems) for cm in comms]
    aliases = {}
    for k, cm in enumerate(comms):
        for i, o in cm.aliases:
            aliases[n_in + sum(n_ci[:k]) + i] = n_out + sum(n_co[:k]) + o

    def fused(*refs):
        pos = [0]

        def take(n):
            part = refs[pos[0]:pos[0] + n]
            pos[0] += n
            return part

        ins = take(n_in)
        c_ins = [take(n) for n in n_ci]
        outs = take(n_out)
        c_outs = [take(n) for n in n_co]
        scr = take(n_scr)
        c_sems = [take(n) for n in n_cs]
        first, last = True, True
        for d, size in enumerate(grid):
            first = jnp.logical_and(first, pl.program_id(d) == 0)
            last = jnp.logical_and(last, pl.program_id(d) == size - 1)

        def run(which):
            for cm, ci, co, cs in zip(comms, c_ins, c_outs, c_sems):
                getattr(cm, which)(ci, co, cs)

        if grid:
            pl.when(first)(lambda: run("start"))
        else:
            run("start")
        body(*ins, *outs, *scr)
        if grid:
            pl.when(last)(lambda: run("finish"))
        else:
            run("finish")

    res = pl.pallas_call(
        fused, name=name, grid=grid, in_specs=list(in_specs) + [HBM_SPEC] * sum(n_ci),
        out_specs=out_specs_l + [HBM_SPEC] * sum(n_co), out_shape=out_shape_l + [s for cm in comms for s in cm.outs],
        input_output_aliases=aliases, scratch_shapes=scratch_shapes + [s for cm in comms for s in cm.sems],
        compiler_params=_params(*["arbitrary"] * len(grid)),
    )(*args, *[a for cm in comms for a in cm.ins])
    main = res[:n_out]
    extra, at = [], n_out
    for n in n_co:
        extra.append(list(res[at:at + n]))
        at += n
    return (main[0] if single else list(main)), extra


def _mm(a, b, *, name, out_dtype, trans_b=False, tm=512, tn=None, nk=1, a_fn=None, extra=(), epi=None,
        b_spec=None, n_out=None, k_total=None, comms=()):
    m_total = a.shape[0]
    k_total = a.shape[1] if k_total is None else k_total
    tk = k_total // nk
    if n_out is None:
        n_out = b.shape[0] if trans_b else b.shape[1]
    tn = n_out if tn is None else tn
    grid = (m_total // tm, n_out // tn, nk)
    dims = NT_DIMS if trans_b else None

    def body(*refs):
        a_ref, b_ref = refs[0], refs[1]
        extra_refs = refs[2:2 + len(extra)]
        o_ref = refs[2 + len(extra)]
        av = a_ref[...]
        if a_fn is not None:
            av = a_fn(av.astype(F32))
        part = _dot(av, b_ref[...], dims)

        def finish(acc):
            if epi is not None:
                acc = epi(acc, *[r[...] for r in extra_refs])
            o_ref[...] = acc.astype(out_dtype)

        if nk == 1:
            finish(part)
        else:
            acc_ref = refs[-1]
            k = pl.program_id(2)

            @pl.when(k == 0)
            def _():
                acc_ref[...] = part

            @pl.when(k > 0)
            def _():
                acc_ref[...] += part

            @pl.when(k == nk - 1)
            def _():
                finish(acc_ref[...])

    if b_spec is None:
        if trans_b:
            b_spec = pl.BlockSpec((tn, tk), lambda i, j, k: (j, k))
        else:
            b_spec = pl.BlockSpec((tk, tn), lambda i, j, k: (k, j))
    in_specs = [pl.BlockSpec((tm, tk), lambda i, j, k: (i, k)), b_spec]
    in_specs += [pl.BlockSpec((tm, tn), lambda i, j, k: (i, j)) for _ in extra]
    return _call(
        body, (a, b, *extra), name=name, grid=grid, in_specs=in_specs,
        out_specs=pl.BlockSpec((tm, tn), lambda i, j, k: (i, j)),
        out_shape=_sds((m_total, n_out), out_dtype),
        scratch_shapes=[pltpu.VMEM((tm, tn), F32)] if nk > 1 else [],
        sem=("parallel", "parallel", "arbitrary"), comms=comms)


def _mm_tn(a, b, *, name, tk, tn, tm=512, a_fn=None, out_shape=None, out_spec=None):
    m_total, k_total = a.shape
    n_total = b.shape[1]
    nm = m_total // tm
    grid = (k_total // tk, n_total // tn, nm)

    def body(a_ref, b_ref, o_ref):
        av = a_ref[...]
        if a_fn is not None:
            av = a_fn(av.astype(F32))
        part = _dot(av, b_ref[...], TN_DIMS)
        m = pl.program_id(2)

        @pl.when(m == 0)
        def _():
            o_ref[...] = part

        @pl.when(m > 0)
        def _():
            o_ref[...] += part

    if out_shape is None:
        out_shape = _sds((k_total, n_total), F32)
        out_spec = pl.BlockSpec((tk, tn), lambda i, j, m: (i, j))
    return pl.pallas_call(
        body, name=name, grid=grid,
        in_specs=[pl.BlockSpec((tm, tk), lambda i, j, m: (m, i)), pl.BlockSpec((tm, tn), lambda i, j, m: (m, j))],
        out_specs=out_spec, out_shape=out_shape,
        compiler_params=_params("parallel", "parallel", "arbitrary"),
    )(a, b)


def _ada_fwd(c_all, w_shard, b_shard):
    nb, ncol = c_all.shape[0], w_shard.shape[1]
    tn = 512

    def body(c_ref, w_ref, b_ref, o_ref):
        c = c_ref[...]
        o_ref[...] = _dot(c * _sigmoid(c), w_ref[...]) + b_ref[...]

    return pl.pallas_call(
        body, name="ada_fwd", grid=(ncol // tn,),
        in_specs=[pl.BlockSpec((nb, D_MODEL), lambda j: (0, 0)), pl.BlockSpec((D_MODEL, tn), lambda j: (0, j)),
                  pl.BlockSpec((1, tn), lambda j: (0, j))],
        out_specs=pl.BlockSpec((nb, tn), lambda j: (0, j)), out_shape=_sds((nb, ncol), F32),
        compiler_params=_params("parallel"),
    )(c_all, w_shard, b_shard)


def _adamw_math(g, w, m, v):
    m = ADAM_B1 * m + (1.0 - ADAM_B1) * g
    v = ADAM_B2 * v + (1.0 - ADAM_B2) * (g * g)
    m_hat = m / (1.0 - ADAM_B1 ** ADAM_STEP)
    v_hat = v / (1.0 - ADAM_B2 ** ADAM_STEP)
    delta = -ADAM_LR * (m_hat / (jnp.sqrt(v_hat) + ADAM_EPS) + ADAM_WD * w)
    return delta, m, v


def _ada_bwd_adamw(c_all, dmod_cols, w, m, v):
    nb, ncol = dmod_cols.shape
    tn = 256

    def body(c_ref, d_ref, w_ref, m_ref, v_ref, g_ref, dl_ref, nm_ref, nv_ref):
        c = c_ref[...]
        g = _dot(c * _sigmoid(c), d_ref[...], TN_DIMS)
        g_ref[...] = g
        dl_ref[...], nm_ref[...], nv_ref[...] = _adamw_math(g, w_ref[...], m_ref[...], v_ref[...])

    col = pl.BlockSpec((D_MODEL, tn), lambda j: (0, j))
    shp = _sds((D_MODEL, ncol), F32)
    return pl.pallas_call(
        body, name="ada_bwd_adamw", grid=(ncol // tn,),
        in_specs=[pl.BlockSpec((nb, D_MODEL), lambda j: (0, 0)), pl.BlockSpec((nb, tn), lambda j: (0, j)), col, col, col],
        out_specs=[col, col, col, col], out_shape=[shp, shp, shp, shp],
        compiler_params=_params("parallel"),
    )(c_all, dmod_cols, w, m, v)


def _adamw_halves(own, theirs, core, w, m, v, *, axis, name):
    r2, c2 = own.shape
    tr = min(r2, 256)
    nt = r2 // tr

    def body(core_ref, own_ref, their_ref, w_ref, m_ref, v_ref, g_ref, dl_ref, nm_ref, nv_ref):
        g = jnp.where(pl.program_id(0) == core_ref[0], own_ref[...], their_ref[...])
        g_ref[...] = g
        dl_ref[...], nm_ref[...], nv_ref[...] = _adamw_math(g, w_ref[...], m_ref[...], v_ref[...])

    if axis == 0:
        full = pl.BlockSpec((tr, c2), lambda h, i, core_ref: (h * nt + i, 0))
    else:
        full = pl.BlockSpec((tr, c2), lambda h, i, core_ref: (i, h))
    half = pl.BlockSpec((tr, c2), lambda h, i, core_ref: (i, 0))
    shp = _sds(w.shape, F32)
    return pl.pallas_call(
        body, name=name,
        grid_spec=pltpu.PrefetchScalarGridSpec(num_scalar_prefetch=1, grid=(2, nt), in_specs=[half, half, full, full, full],
                                               out_specs=[full] * 4),
        out_shape=[shp] * 4, compiler_params=_params("parallel", "parallel"),
    )(core, own, theirs, w, m, v)


def _tok_spec(tm, width=D_MODEL):
    return pl.BlockSpec((None, tm, width), lambda b, i: (b, i, 0))


def _row_spec(width=D_MODEL):
    return pl.BlockSpec((None, 1, width), lambda b, i: (b, 0, 0))


def _vec_spec(width=D_MODEL):
    return pl.BlockSpec((1, width), lambda b, i: (0, 0))


def _prenorm1(x, w, sc, sh, tm=512):
    bsz, seq, _ = x.shape

    def body(x_ref, w_ref, sc_ref, sh_ref, h_ref):
        y, _, _ = _rms_fwd(x_ref[...], w_ref[...])
        h_ref[...] = (y * (1.0 + sc_ref[...]) + sh_ref[...]).astype(BF16)

    return pl.pallas_call(
        body, name="prenorm1", grid=(bsz, seq // tm),
        in_specs=[_tok_spec(tm), _vec_spec(), _row_spec(), _row_spec()],
        out_specs=_tok_spec(tm), out_shape=_sds(x.shape, BF16),
        compiler_params=_params("parallel", "parallel"),
    )(x, w, sc, sh)


def _rope_tables(seq):
    half = ROPE_DIM // 2
    inv_freq = ROPE_THETA ** (-jnp.arange(0, ROPE_DIM, 2, dtype=F32) / ROPE_DIM)
    ang = jnp.arange(seq, dtype=F32)[:, None] * inv_freq[None, :]
    cos, sin = jnp.cos(ang), jnp.sin(ang)
    rest = ATT_HEAD_DIM - ROPE_DIM
    ones, zeros, zh = jnp.ones((seq, rest), F32), jnp.zeros((seq, rest), F32), jnp.zeros((seq, half), F32)
    reps = LANE // ATT_HEAD_DIM
    t_cos = jnp.tile(jnp.concatenate([cos, cos, ones], axis=1), (1, reps))
    t_up = jnp.tile(jnp.concatenate([zh, sin, zeros], axis=1), (1, reps))
    t_dn = jnp.tile(jnp.concatenate([-sin, zh, zeros], axis=1), (1, reps))
    return t_cos, t_up, t_dn


def _rope_fwd(proj, tables, tm=256):
    bsz, seq, _ = proj.shape
    half = ROPE_DIM // 2

    def body(p_ref, c_ref, u_ref, d_ref, q_ref, k_ref, v_ref):
        c, u, d = c_ref[...], u_ref[...], d_ref[...]

        def rope(x):
            return x * c + pltpu.roll(x, half, 1) * u + pltpu.roll(x, LANE - half, 1) * d

        for s in range(ATT_WIDTH // LANE):
            q_ref[:, s * LANE:(s + 1) * LANE] = rope(p_ref[:, s * LANE:(s + 1) * LANE]).astype(BF16)
        k_ref[...] = rope(p_ref[:, ATT_WIDTH:ATT_WIDTH + LANE]).astype(BF16)
        v_ref[...] = p_ref[:, ATT_WIDTH + LANE:ATT_COLS].astype(BF16)

    tab = pl.BlockSpec((tm, LANE), lambda b, i: (i, 0))
    return pl.pallas_call(
        body, name="rope_fwd", grid=(bsz, seq // tm),
        in_specs=[_tok_spec(tm, ATT_COLS), tab, tab, tab],
        out_specs=[_tok_spec(tm, ATT_WIDTH), _tok_spec(tm, LANE), _tok_spec(tm, LANE)],
        out_shape=[_sds((bsz, seq, ATT_WIDTH), BF16), _sds((bsz, seq, LANE), BF16), _sds((bsz, seq, LANE), BF16)],
        compiler_params=_params("parallel", "parallel"),
    )(proj, *tables)


def _band_masks(i):
    row = lax.broadcasted_iota(jnp.int32, (WINDOW, WINDOW), 0)
    col = lax.broadcasted_iota(jnp.int32, (WINDOW, WINDOW), 1)
    return col <= row, jnp.logical_and(col > row, i > 0)


def _prev_spec(width):
    return pl.BlockSpec((None, WINDOW, width), lambda b, i: (b, jnp.maximum(i - 1, 0), 0))


def _attn_fwd(qr, kr, vb, sinks, w_norm, comms=()):
    bsz, seq, _ = qr.shape
    nblk = seq // WINDOW
    neg = float(jnp.finfo(jnp.float32).min)

    def body(sink_ref, q_ref, kc_ref, kp_ref, vc_ref, vp_ref, w_ref, raw_ref, an_ref, l_ref):
        mask_c, mask_p = _band_masks(pl.program_id(1))
        for h in range(ATT_Q_HEADS):
            g = h // ATT_GROUP
            hs = slice(h * ATT_HEAD_DIM, (h + 1) * ATT_HEAD_DIM)
            gs = slice(g * ATT_HEAD_DIM, (g + 1) * ATT_HEAD_DIM)
            q = q_ref[:, hs]
            sink = sink_ref[0, h]
            sc = jnp.where(mask_c, _dot(q, kc_ref[:, gs], NT_DIMS) * ATT_SCALE, neg)
            sp = jnp.where(mask_p, _dot(q, kp_ref[:, gs], NT_DIMS) * ATT_SCALE, neg)
            m = jnp.maximum(jnp.maximum(jnp.max(sc, axis=-1, keepdims=True), jnp.max(sp, axis=-1, keepdims=True)), sink)
            pc = jnp.where(mask_c, jnp.exp(sc - m), 0.0)
            pp = jnp.where(mask_p, jnp.exp(sp - m), 0.0)
            den = jnp.sum(pc, axis=-1, keepdims=True) + jnp.sum(pp, axis=-1, keepdims=True) + jnp.exp(sink - m)
            raw_ref[:, hs] = _dot(pc / den, vc_ref[:, gs]) + _dot(pp / den, vp_ref[:, gs])
            l_ref[:, h:h + 1] = m + jnp.log(den)
        y, _, _ = _rms_fwd(raw_ref[...], w_ref[...])
        an_ref[...] = y.astype(BF16)

    cur = lambda width: pl.BlockSpec((None, WINDOW, width), lambda b, i: (b, i, 0))
    return _call(
        body, (sinks, qr, kr, kr, vb, vb, w_norm), name="attn_fwd", grid=(bsz, nblk),
        in_specs=[pl.BlockSpec(memory_space=pltpu.SMEM), cur(ATT_WIDTH), cur(LANE), _prev_spec(LANE), cur(LANE), _prev_spec(LANE),
                  _vec_spec(ATT_WIDTH)],
        out_specs=[cur(ATT_WIDTH), cur(ATT_WIDTH), cur(ATT_Q_HEADS)],
        out_shape=[_sds((bsz, seq, ATT_WIDTH), F32), _sds((bsz, seq, ATT_WIDTH), BF16), _sds((bsz, seq, ATT_Q_HEADS), F32)],
        sem=("parallel", "parallel"), comms=comms)


HG_Q0 = ATT_COLS // LANE
HG_F0 = HG_Q0 + HG_HEADS
HG_I0 = HG_F0 + HG_HEADS
HG_G0 = HG_I0 + HG_HEADS
HG_TOK = 256
HG_NCH = HG_TOK // HG_CHUNK


def _chunk_consts():
    row = lax.broadcasted_iota(jnp.int32, (HG_CHUNK, HG_CHUNK), 0)
    col = lax.broadcasted_iota(jnp.int32, (HG_CHUNK, HG_CHUNK), 1)
    return row >= col, row <= col


def _cumsum_rows(tri_f32, x):
    return jnp.dot(tri_f32, x, precision=lax.Precision.HIGHEST, preferred_element_type=F32)


def _hgrn_gates(tbl, hf, hq):
    lb = _sigmoid(tbl[1:2] - tbl[0:1])
    sig = _sigmoid(hf)
    f = lb + (1.0 - lb) * sig
    sq = _sigmoid(hq)
    return lb, sig, f, sq


def _hgrn_fwd(proj, lb_table, norm_w, comms=()):
    bsz, seq, _ = proj.shape
    nstep = seq // HG_TOK

    def body(tbl_ref, nw_ref, q_ref, f_ref, i_ref, g_ref, o_ref, rec_ref, st_ref, s_scr):
        @pl.when(pl.program_id(2) == 0)
        def _():
            s_scr[...] = jnp.zeros_like(s_scr)

        lower, _ = _chunk_consts()
        tri = lower.astype(F32)
        nw = nw_ref[...]
        for j in range(HG_NCH):
            sl = slice(j * HG_CHUNK, (j + 1) * HG_CHUNK)
            hq, v = q_ref[sl, :], i_ref[sl, :]
            _, _, f, sq = _hgrn_gates(tbl_ref[...], f_ref[sl, :], hq)
            q, k = hq * sq, 1.0 - f
            b = _cumsum_rows(tri, jnp.log(f))
            bl = b[HG_CHUNK - 1:HG_CHUNK, :]
            qd = q * jnp.exp(b)
            kd = k * jnp.exp(-b)
            ku = k * jnp.exp(bl - b)
            a = jnp.where(lower, _dot(qd, kd, NT_DIMS), 0.0)
            st = s_scr[...]
            st_ref[j] = st
            o = _dot(a, v) + _dot(qd, st, NT_DIMS)
            s_scr[...] = st * jnp.exp(bl) + _dot(v, ku, TN_DIMS)
            o_ref[sl, :] = o
            y, _, _ = _rms_fwd(o, nw)
            hg = g_ref[sl, :]
            rec_ref[sl, :] = (y * (hg * _sigmoid(hg))).astype(BF16)

    slab = lambda first: pl.BlockSpec((None, HG_TOK, LANE), lambda b, h, t: (b, t, first + h))
    head_out = pl.BlockSpec((None, HG_TOK, LANE), lambda b, h, t: (b, t, h))
    return _call(
        body, (lb_table, norm_w, proj, proj, proj, proj), name="hgrn_fwd", grid=(bsz, HG_HEADS, nstep),
        in_specs=[pl.BlockSpec((2, LANE), lambda b, h, t: (0, h)), pl.BlockSpec((1, LANE), lambda b, h, t: (0, 0)),
                  slab(HG_Q0), slab(HG_F0), slab(HG_I0), slab(HG_G0)],
        out_specs=[head_out, head_out,
                   pl.BlockSpec((None, None, HG_NCH, LANE, LANE), lambda b, h, t: (b, h, t, 0, 0))],
        out_shape=[_sds((bsz, seq, HG_WIDTH), F32), _sds((bsz, seq, HG_WIDTH), BF16),
                   _sds((bsz, HG_HEADS, seq // HG_CHUNK, LANE, LANE), F32)],
        scratch_shapes=[pltpu.VMEM((LANE, LANE), F32)],
        sem=("parallel", "parallel", "arbitrary"), comms=comms)


def _mid_fwd(x, mix, post_w, g1, pre_w, sc2, sh2, tm=512):
    bsz, seq, _ = x.shape

    def body(x_ref, mix_ref, pw_ref, g1_ref, w2_ref, sc_ref, sh_ref, x1_ref, h2_ref):
        n1, _, _ = _rms_fwd(mix_ref[...], pw_ref[...])
        x1 = x_ref[...] + g1_ref[...] * n1
        x1_ref[...] = x1
        y2, _, _ = _rms_fwd(x1, w2_ref[...])
        h2_ref[...] = (y2 * (1.0 + sc_ref[...]) + sh_ref[...]).astype(BF16)

    return pl.pallas_call(
        body, name="mid_fwd", grid=(bsz, seq // tm),
        in_specs=[_tok_spec(tm), _tok_spec(tm), _vec_spec(), _row_spec(), _vec_spec(), _row_spec(), _row_spec()],
        out_specs=[_tok_spec(tm), _tok_spec(tm)], out_shape=[_sds(x.shape, F32), _sds(x.shape, BF16)],
        compiler_params=_params("parallel", "parallel"),
    )(x, mix, post_w, g1, pre_w, sc2, sh2)


def _acc_out(ref, first, value):
    @pl.when(first)
    def _():
        ref[...] = value

    @pl.when(jnp.logical_not(first))
    def _():
        ref[...] += value


def _loss_bwd(x1, down, post_w, g2, target, tm=512):
    bsz, seq, _ = x1.shape

    def body(x1_ref, d_ref, w_ref, g2_ref, t_ref, loss_ref, dy_ref, dd_ref, dg2_ref, dw_ref):
        b, i = pl.program_id(0), pl.program_id(1)
        w, g2v = w_ref[...], g2_ref[...]
        n2, dh, rstd = _rms_fwd(d_ref[...], w)
        err = x1_ref[...] + g2v * n2 - t_ref[...]
        part = (0.5 / D_MODEL) * jnp.sum(jnp.sum(err * err, axis=-1, keepdims=True), axis=0, keepdims=True)
        _acc_out(loss_ref, jnp.logical_and(b == 0, i == 0), jnp.broadcast_to(part, (1, LANE)))
        dy = err * (1.0 / D_MODEL)
        dy_ref[...] = dy
        _acc_out(dg2_ref, i == 0, _colsum(dy * n2))
        dd, dw_rows = _rms_bwd(dy * g2v, dh, rstd, w)
        dd_ref[...] = dd.astype(BF16)
        _acc_out(dw_ref, jnp.logical_and(b == 0, i == 0), _colsum(dw_rows))

    return pl.pallas_call(
        body, name="loss_bwd", grid=(bsz, seq // tm),
        in_specs=[_tok_spec(tm), _tok_spec(tm), _vec_spec(), _row_spec(), _tok_spec(tm)],
        out_specs=[_vec_spec(LANE), _tok_spec(tm), _tok_spec(tm), _row_spec(), _vec_spec()],
        out_shape=[_sds((1, LANE), F32), _sds(x1.shape, F32), _sds(x1.shape, BF16), _sds((bsz, 1, D_MODEL), F32),
                   _sds((1, D_MODEL), F32)],
        compiler_params=_params("arbitrary", "arbitrary"),
    )(x1, down, post_w, g2, target)


def _mid_bwd(dh2, dy, x1, mix, pre_w, sc2, post_w, g1, tm=512, comms=()):
    bsz, seq, _ = x1.shape

    def body(dh2_ref, dy_ref, x1_ref, mix_ref, w2_ref, sc_ref, pw_ref, g1_ref,
             dx1_ref, dmix_ref, dsc_ref, dsh_ref, dg1_ref, dw2_ref, dpw_ref):
        b, i = pl.program_id(0), pl.program_id(1)
        first = jnp.logical_and(b == 0, i == 0)
        w2, pw = w2_ref[...], pw_ref[...]
        dh2v = dh2_ref[...]
        y2, xh2, rstd2 = _rms_fwd(x1_ref[...], w2)
        _acc_out(dsh_ref, i == 0, _colsum(dh2v))
        _acc_out(dsc_ref, i == 0, _colsum(dh2v * y2))
        dx1n, dw_rows = _rms_bwd(dh2v * (1.0 + sc_ref[...]), xh2, rstd2, w2)
        _acc_out(dw2_ref, first, _colsum(dw_rows))
        dx1 = dy_ref[...] + dx1n
        dx1_ref[...] = dx1
        n1, mh, rstd1 = _rms_fwd(mix_ref[...], pw)
        _acc_out(dg1_ref, i == 0, _colsum(dx1 * n1))
        dmix, dpw_rows = _rms_bwd(dx1 * g1_ref[...], mh, rstd1, pw)
        dmix_ref[...] = dmix.astype(BF16)
        _acc_out(dpw_ref, first, _colsum(dpw_rows))

    row_shape = _sds((bsz, 1, D_MODEL), F32)
    vec_shape = _sds((1, D_MODEL), F32)
    return _call(
        body, (dh2, dy, x1, mix, pre_w, sc2, post_w, g1), name="mid_bwd", grid=(bsz, seq // tm),
        in_specs=[_tok_spec(tm), _tok_spec(tm), _tok_spec(tm), _tok_spec(tm), _vec_spec(), _row_spec(), _vec_spec(), _row_spec()],
        out_specs=[_tok_spec(tm), _tok_spec(tm), _row_spec(), _row_spec(), _row_spec(), _vec_spec(), _vec_spec()],
        out_shape=[_sds(x1.shape, F32), _sds(x1.shape, BF16), row_shape, row_shape, row_shape, vec_shape, vec_shape],
        sem=("arbitrary", "arbitrary"), comms=comms)


def _norm1_bwd(dh1, dx1, x, pre_w, sc1, tm=512):
    bsz, seq, _ = x.shape

    def body(dh_ref, dx1_ref, x_ref, w_ref, sc_ref, gx_ref, dsc_ref, dsh_ref, dw_ref):
        b, i = pl.program_id(0), pl.program_id(1)
        w = w_ref[...]
        dh = dh_ref[...]
        y, xh, rstd = _rms_fwd(x_ref[...], w)
        _acc_out(dsh_ref, i == 0, _colsum(dh))
        _acc_out(dsc_ref, i == 0, _colsum(dh * y))
        dx, dw_rows = _rms_bwd(dh * (1.0 + sc_ref[...]), xh, rstd, w)
        _acc_out(dw_ref, jnp.logical_and(b == 0, i == 0), _colsum(dw_rows))
        gx_ref[...] = dx1_ref[...] + dx

    row_shape = _sds((bsz, 1, D_MODEL), F32)
    return pl.pallas_call(
        body, name="norm1_bwd", grid=(bsz, seq // tm),
        in_specs=[_tok_spec(tm), _tok_spec(tm), _tok_spec(tm), _vec_spec(), _row_spec()],
        out_specs=[_tok_spec(tm), _row_spec(), _row_spec(), _vec_spec()],
        out_shape=[_sds(x.shape, F32), row_shape, row_shape, _sds((1, D_MODEL), F32)],
        compiler_params=_params("arbitrary", "arbitrary"),
    )(dh1, dx1, x, pre_w, sc1)


def _hgrn_bwd(dcat, proj, o_raw, states, lb_table, norm_w, comms=()):
    bsz, seq, _ = proj.shape
    nstep = seq // HG_TOK
    rec0 = ATT_WIDTH // LANE

    def body(tbl_ref, nw_ref, dr_ref, q_ref, f_ref, i_ref, g_ref, o_ref, st_ref,
             dq_ref, df_ref, di_ref, dg_ref, dlb_ref, dnw_ref, ds_scr):
        h, b, t = pl.program_id(0), pl.program_id(1), pl.program_id(2)

        @pl.when(t == 0)
        def _():
            ds_scr[...] = jnp.zeros_like(ds_scr)

        lower, upper = _chunk_consts()
        tri_lo, tri_up = lower.astype(F32), upper.astype(F32)
        last_row = lax.broadcasted_iota(jnp.int32, (HG_CHUNK, LANE), 0) == HG_CHUNK - 1
        nw = nw_ref[...]
        dlb_acc = jnp.zeros((1, LANE), F32)
        dnw_acc = jnp.zeros((1, LANE), F32)
        for j in reversed(range(HG_NCH)):
            sl = slice(j * HG_CHUNK, (j + 1) * HG_CHUNK)
            hq, v, hg = q_ref[sl, :], i_ref[sl, :], g_ref[sl, :]
            lb, sig, f, sq = _hgrn_gates(tbl_ref[...], f_ref[sl, :], hq)
            q, k = hq * sq, 1.0 - f
            b_cum = _cumsum_rows(tri_lo, jnp.log(f))
            bl = b_cum[HG_CHUNK - 1:HG_CHUNK, :]
            e_b, e_nb, e_bl, e_rem = jnp.exp(b_cum), jnp.exp(-b_cum), jnp.exp(bl), jnp.exp(bl - b_cum)
            qd, kd, ku = q * e_b, k * e_nb, k * e_rem
            st = st_ref[j]
            y, on, rstd = _rms_fwd(o_ref[sl, :], nw)
            sg = _sigmoid(hg)
            dr = dr_ref[sl, :]
            dg_ref[sl, :] = (dr * y * (sg * (1.0 + hg * (1.0 - sg)))).astype(BF16)
            do, dnw_rows = _rms_bwd(dr * (hg * sg), on, rstd, nw)
            dnw_acc += _colsum(dnw_rows)
            dsp = ds_scr[...]
            at = jnp.where(upper, _dot(kd, qd, NT_DIMS), 0.0)
            da = jnp.where(lower, _dot(do, v, NT_DIMS), 0.0)
            dat = jnp.where(upper, _dot(v, do, NT_DIMS), 0.0)
            dv = _dot(at, do) + _dot(ku, dsp, NT_DIMS)
            dqd = _dot(da, kd) + _dot(do, st)
            dkd = _dot(dat, qd)
            dku = _dot(v, dsp)
            dbl = _colsum(st * dsp) * e_bl + _colsum(dku * ku)
            ds_scr[...] = _dot(do, qd, TN_DIMS) + dsp * e_bl
            dk = dkd * e_nb + dku * e_rem
            db = dqd * qd - dkd * kd - dku * ku
            db = db + jnp.where(last_row, dbl, 0.0)
            dlogf = _cumsum_rows(tri_up, db)
            dfv = dlogf / f - dk
            df_ref[sl, :] = (dfv * (1.0 - lb) * sig * (1.0 - sig)).astype(BF16)
            dlb_acc += _colsum(dfv * (1.0 - sig))
            dq_ref[sl, :] = (dqd * e_b * (sq * (1.0 + hq * (1.0 - sq)))).astype(BF16)
            di_ref[sl, :] = dv.astype(BF16)
        _acc_out(dlb_ref, jnp.logical_and(b == 0, t == 0), dlb_acc)
        _acc_out(dnw_ref, jnp.logical_and(h == 0, jnp.logical_and(b == 0, t == 0)), dnw_acc)

    rev = lambda t: nstep - 1 - t
    slab = lambda first: pl.BlockSpec((None, HG_TOK, LANE), lambda h, b, t: (b, rev(t), first + h))
    head = pl.BlockSpec((None, HG_TOK, LANE), lambda h, b, t: (b, rev(t), h))
    grad_shape = _sds((bsz, seq, HG_WIDTH), BF16)
    return _call(
        body, (lb_table, norm_w, dcat, proj, proj, proj, proj, o_raw, states), name="hgrn_bwd", grid=(HG_HEADS, bsz, nstep),
        in_specs=[pl.BlockSpec((2, LANE), lambda h, b, t: (0, h)), pl.BlockSpec((1, LANE), lambda h, b, t: (0, 0)),
                  slab(rec0), slab(HG_Q0), slab(HG_F0), slab(HG_I0), slab(HG_G0), head,
                  pl.BlockSpec((None, None, HG_NCH, LANE, LANE), lambda h, b, t: (b, h, rev(t), 0, 0))],
        out_specs=[head, head, head, head, pl.BlockSpec((1, LANE), lambda h, b, t: (0, h)),
                   pl.BlockSpec((1, LANE), lambda h, b, t: (0, 0))],
        out_shape=[grad_shape, grad_shape, grad_shape, grad_shape, _sds((1, HG_WIDTH), F32), _sds((1, LANE), F32)],
        scratch_shapes=[pltpu.VMEM((LANE, LANE), F32)],
        sem=("arbitrary", "arbitrary", "arbitrary"), comms=comms)


def _attn_bwd(dcat, raw, w_norm, qr, kr, vb, lse, sinks, comms=()):
    bsz, seq, _ = qr.shape
    nblk = seq // WINDOW

    def body(sink_ref, da_ref, raw_ref, w_ref, q_ref, kc_ref, kp_ref, vc_ref, vp_ref, l_ref,
             dq_ref, dkd_ref, dkp_ref, dvd_ref, dvp_ref, dw_ref, dsink_ref):
        b, i = pl.program_id(0), pl.program_id(1)
        first = jnp.logical_and(b == 0, i == 0)
        mask_c, mask_p = _band_masks(i)
        raw_v = raw_ref[...]
        w = w_ref[...]
        _, on, rstd = _rms_fwd(raw_v, w)
        do_all, dw_rows = _rms_bwd(da_ref[...], on, rstd, w)
        _acc_out(dw_ref, first, _colsum(dw_rows))
        lane8 = lax.broadcasted_iota(jnp.int32, (1, ATT_Q_HEADS), 1)
        dsink = jnp.zeros((1, ATT_Q_HEADS), F32)
        for g in range(ATT_KV_HEADS):
            gs = slice(g * ATT_HEAD_DIM, (g + 1) * ATT_HEAD_DIM)
            kc, kp, vc, vp = kc_ref[:, gs], kp_ref[:, gs], vc_ref[:, gs], vp_ref[:, gs]
            dkc = jnp.zeros((WINDOW, ATT_HEAD_DIM), F32)
            dkp, dvc, dvp = dkc, dkc, dkc
            for hh in range(ATT_GROUP):
                h = g * ATT_GROUP + hh
                hs = slice(h * ATT_HEAD_DIM, (h + 1) * ATT_HEAD_DIM)
                q = q_ref[:, hs]
                doh = do_all[:, hs]
                dsum = jnp.sum(doh * raw_v[:, hs], axis=-1, keepdims=True)
                lse_h = l_ref[:, h:h + 1]
                pc = jnp.where(mask_c, jnp.exp(_dot(q, kc, NT_DIMS) * ATT_SCALE - lse_h), 0.0)
                pp = jnp.where(mask_p, jnp.exp(_dot(q, kp, NT_DIMS) * ATT_SCALE - lse_h), 0.0)
                p_sink = jnp.exp(sink_ref[0, h] - lse_h)
                dsink = dsink - jnp.where(lane8 == h, jnp.sum(p_sink * dsum, axis=0, keepdims=True), 0.0)
                dsc = pc * (_dot(doh, vc, NT_DIMS) - dsum) * ATT_SCALE
                dsp = pp * (_dot(doh, vp, NT_DIMS) - dsum) * ATT_SCALE
                dq_ref[:, hs] = _dot(dsc, kc) + _dot(dsp, kp)
                dkc += _dot(dsc, q, TN_DIMS)
                dkp += _dot(dsp, q, TN_DIMS)
                dvc += _dot(pc, doh, TN_DIMS)
                dvp += _dot(pp, doh, TN_DIMS)
            dkd_ref[:, gs], dkp_ref[:, gs], dvd_ref[:, gs], dvp_ref[:, gs] = dkc, dkp, dvc, dvp
        _acc_out(dsink_ref, first, dsink)

    cur = lambda width: pl.BlockSpec((None, WINDOW, width), lambda b, i: (b, i, 0))
    kv_shape = _sds((bsz, seq, LANE), F32)
    return _call(
        body, (sinks, dcat, raw, w_norm, qr, kr, kr, vb, vb, lse), name="attn_bwd", grid=(bsz, nblk),
        in_specs=[pl.BlockSpec(memory_space=pltpu.SMEM), cur(ATT_WIDTH), cur(ATT_WIDTH), _vec_spec(ATT_WIDTH), cur(ATT_WIDTH),
                  cur(LANE), _prev_spec(LANE), cur(LANE), _prev_spec(LANE), cur(ATT_Q_HEADS)],
        out_specs=[cur(ATT_WIDTH), cur(LANE), cur(LANE), cur(LANE), cur(LANE), _vec_spec(ATT_WIDTH), _vec_spec(ATT_Q_HEADS)],
        out_shape=[_sds((bsz, seq, ATT_WIDTH), F32), kv_shape, kv_shape, kv_shape, kv_shape, _sds((1, ATT_WIDTH), F32),
                   _sds((1, ATT_Q_HEADS), F32)],
        sem=("arbitrary", "arbitrary"), comms=comms)


def _rope_bwd(dq, dkd, dkp, dvd, dvp, tables):
    bsz, seq, _ = dq.shape
    nblk = seq // WINDOW
    half = ROPE_DIM // 2

    def body(dq_ref, dkd_ref, dkp_ref, dvd_ref, dvp_ref, c_ref, u_ref, d_ref, o_ref):
        c, u, d = c_ref[...], u_ref[...], d_ref[...]
        has_next = pl.program_id(1) < nblk - 1

        def unrope(g):
            return g * c + pltpu.roll(g * u, LANE - half, 1) + pltpu.roll(g * d, half, 1)

        for s in range(ATT_WIDTH // LANE):
            o_ref[:, s * LANE:(s + 1) * LANE] = unrope(dq_ref[:, s * LANE:(s + 1) * LANE]).astype(BF16)
        dk = dkd_ref[...] + jnp.where(has_next, dkp_ref[...], 0.0)
        o_ref[:, ATT_WIDTH:ATT_WIDTH + LANE] = unrope(dk).astype(BF16)
        o_ref[:, ATT_WIDTH + LANE:ATT_COLS] = (dvd_ref[...] + jnp.where(has_next, dvp_ref[...], 0.0)).astype(BF16)

    cur = lambda width: pl.BlockSpec((None, WINDOW, width), lambda b, i: (b, i, 0))
    nxt = pl.BlockSpec((None, WINDOW, LANE), lambda b, i: (b, jnp.minimum(i + 1, nblk - 1), 0))
    tab = pl.BlockSpec((WINDOW, LANE), lambda b, i: (i, 0))
    return pl.pallas_call(
        body, name="rope_bwd", grid=(bsz, nblk),
        in_specs=[cur(ATT_WIDTH), cur(LANE), nxt, cur(LANE), nxt, tab, tab, tab],
        out_specs=cur(ATT_COLS), out_shape=_sds((bsz, seq, ATT_COLS), BF16),
        compiler_params=_params("parallel", "parallel"),
    )(dq, dkd, dkp, dvd, dvp, *tables)


def _other_chips(x, y):
    return [(1 - x, y), (x, 1 - y), (1 - x, 1 - y)]


def _sem_pair(n):
    return [pltpu.SemaphoreType.DMA((n,)), pltpu.SemaphoreType.DMA((n,))]


def _plan_chip_gather(blocks, bufs):
    n = len(blocks)

    def copies(ins, outs, sems):
        x, y, c = _mesh_pos()
        sends, lands = [], []
        for a in range(n):
            for j, chip in enumerate(_other_chips(x, y)):
                k = 3 * a + j
                sends.append(pltpu.make_async_remote_copy(
                    src_ref=ins[a], dst_ref=outs[a].at[4 * x + 2 * y + c], send_sem=sems[0].at[k], recv_sem=sems[1].at[k],
                    device_id=(*chip, c), device_id_type=MESH))
                slot = outs[a].at[4 * chip[0] + 2 * chip[1] + c]
                lands.append(pltpu.make_async_remote_copy(
                    src_ref=slot, dst_ref=slot, send_sem=sems[0].at[k], recv_sem=sems[1].at[k],
                    device_id=(*chip, c), device_id_type=MESH))
        return sends, lands

    def start(ins, outs, sems):
        for cp in copies(ins, outs, sems)[0]:
            cp.start()

    def finish(ins, outs, sems):
        sends, lands = copies(ins, outs, sems)
        for cp in lands:
            cp.wait_recv()
        for cp in sends:
            cp.wait_send()

    return _Comm(list(blocks) + list(bufs), [_sds(b.shape, b.dtype) for b in bufs], _sem_pair(3 * n), start, finish,
                 aliases=[(n + a, a) for a in range(n)])


def _plan_pair_forward(bufs):
    n = len(bufs)

    def copies(outs, sems):
        x, y, c = _mesh_pos()
        sends, lands = [], []
        for a in range(n):
            for j, chip in enumerate(_other_chips(x, y)):
                k = 3 * a + j
                slot = outs[a].at[4 * chip[0] + 2 * chip[1] + c]
                sends.append(pltpu.make_async_remote_copy(
                    src_ref=slot, dst_ref=slot, send_sem=sems[0].at[k], recv_sem=sems[1].at[k],
                    device_id=(x, y, 1 - c), device_id_type=MESH))
                theirs = outs[a].at[4 * chip[0] + 2 * chip[1] + 1 - c]
                lands.append(pltpu.make_async_remote_copy(
                    src_ref=theirs, dst_ref=theirs, send_sem=sems[0].at[k], recv_sem=sems[1].at[k],
                    device_id=(x, y, 1 - c), device_id_type=MESH))
        return sends, lands

    def start(ins, outs, sems):
        for cp in copies(outs, sems)[0]:
            cp.start()

    def finish(ins, outs, sems):
        sends, lands = copies(outs, sems)
        for cp in lands:
            cp.wait_recv()
        for cp in sends:
            cp.wait_send()

    return _Comm(list(bufs), [_sds(b.shape, b.dtype) for b in bufs], _sem_pair(3 * n), start, finish,
                 aliases=[(a, a) for a in range(n)])


def _plan_pair(arrays, other_half):
    n = len(arrays)

    def copies(ins, outs, sems):
        x, y, c = _mesh_pos()
        return [pltpu.make_async_remote_copy(
            src_ref=ins[a].at[1 - c] if other_half else ins[a], dst_ref=outs[a], send_sem=sems[0].at[a], recv_sem=sems[1].at[a],
            device_id=(x, y, 1 - c), device_id_type=MESH) for a in range(n)]

    def start(ins, outs, sems):
        for cp in copies(ins, outs, sems):
            cp.start()

    def finish(ins, outs, sems):
        for cp in copies(ins, outs, sems):
            cp.wait()

    shapes = [_sds(a.shape[1:] if other_half else a.shape, a.dtype) for a in arrays]
    return _Comm(list(arrays), shapes, _sem_pair(n), start, finish)


def _plan_chip_exchange(arrays):
    n = len(arrays)

    def copies(ins, outs, sems):
        x, y, c = _mesh_pos()
        sends, lands = [], []
        for a in range(n):
            for j, chip in enumerate(_other_chips(x, y)):
                k = 3 * a + j
                sends.append(pltpu.make_async_remote_copy(
                    src_ref=ins[a].at[2 * chip[0] + chip[1]], dst_ref=outs[a].at[2 * x + y], send_sem=sems[0].at[k],
                    recv_sem=sems[1].at[k], device_id=(*chip, c), device_id_type=MESH))
                slot = outs[a].at[2 * chip[0] + chip[1]]
                lands.append(pltpu.make_async_remote_copy(
                    src_ref=slot, dst_ref=slot, send_sem=sems[0].at[k], recv_sem=sems[1].at[k],
                    device_id=(*chip, c), device_id_type=MESH))
        return sends, lands

    def start(ins, outs, sems):
        for cp in copies(ins, outs, sems)[0]:
            cp.start()

    def finish(ins, outs, sems):
        sends, lands = copies(ins, outs, sems)
        for cp in lands:
            cp.wait_recv()
        for cp in sends:
            cp.wait_send()

    return _Comm(list(arrays), [_sds(a.shape, a.dtype) for a in arrays], _sem_pair(3 * n), start, finish)


def _comm_only(comms, name):
    return _call(lambda: None, (), name=name, grid=(), in_specs=[], out_specs=[], out_shape=[], sem=(), comms=comms)[1]


def _allgather8(arrays, name):
    return _comm_only([_plan_allgather8(arrays)], name)[0]


def _plan_allgather8(arrays):
    n = len(arrays)

    def parts(ins, outs, sems):
        send_sems, recv_sems, local_sems = sems
        x, y, c = _mesh_pos()
        me, sibling = (x, y, c), (x, y, 1 - c)
        chips = _other_chips(x, y)

        def copy(a, k, block, to, src=None):
            dst = outs[a].at[4 * block[0] + 2 * block[1] + block[2]]
            return pltpu.make_async_remote_copy(
                src_ref=dst if src is None else src, dst_ref=dst, send_sem=send_sems.at[7 * a + k],
                recv_sem=recv_sems.at[7 * a + k], device_id=to, device_id_type=MESH)

        mine = [pltpu.make_async_copy(ins[a], outs[a].at[4 * x + 2 * y + c], local_sems.at[a]) for a in range(n)]
        first = []
        for a in range(n):
            first.append(copy(a, 0, me, sibling, src=ins[a]))
            first += [copy(a, 1 + j, me, (*chip, c), src=ins[a]) for j, chip in enumerate(chips)]
        return copy, mine, first, me, sibling, chips, c

    def start(ins, outs, sems):
        _, mine, first, *_ = parts(ins, outs, sems)
        for cp in mine + first:
            cp.start()

    def finish(ins, outs, sems):
        copy, mine, first, me, sibling, chips, c = parts(ins, outs, sems)
        passed = []
        for j, chip in enumerate(chips):
            for a in range(n):
                copy(a, 1 + j, (*chip, c), me).wait_recv()
                fwd = copy(a, 4 + j, (*chip, c), sibling)
                fwd.start()
                passed.append(fwd)
        for a in range(n):
            copy(a, 0, sibling, me).wait_recv()
            for j, chip in enumerate(chips):
                copy(a, 4 + j, (*chip, 1 - c), me).wait_recv()
        for cp in first + passed:
            cp.wait_send()
        for cp in mine:
            cp.wait()

    sems = [pltpu.SemaphoreType.DMA((7 * n,)), pltpu.SemaphoreType.DMA((7 * n,)), pltpu.SemaphoreType.DMA((n,))]
    return _Comm(list(arrays), [_sds((N_DEV,) + a.shape, a.dtype) for a in arrays], sems, start, finish)


def _pair_sum(g, q, core, name):
    _, nblk, rows, cols = g.shape
    tr = min(rows, 256)

    def body(core_ref, g_ref, q_ref, o_ref):
        o_ref[...] = (g_ref[...] + q_ref[...]).astype(BF16)

    blk = pl.BlockSpec((None, tr, cols), lambda k, i, core_ref: (k, i, 0))
    return pl.pallas_call(
        body, name=name,
        grid_spec=pltpu.PrefetchScalarGridSpec(
            num_scalar_prefetch=1, grid=(nblk, rows // tr),
            in_specs=[pl.BlockSpec((None, None, tr, cols), lambda k, i, core_ref: (core_ref[0], k, i, 0)), blk],
            out_specs=blk),
        out_shape=_sds((nblk, rows, cols), BF16), compiler_params=_params("parallel", "parallel"),
    )(core, g, q)


def _sum_chips(own, landed, chip, name):
    _, rows, cols = own.shape
    tr = min(rows, 256)

    def body(chip_ref, own_ref, a_ref, b_ref, c_ref, o_ref):
        acc = own_ref[...].astype(F32) + a_ref[...].astype(F32)
        o_ref[...] = (acc + b_ref[...].astype(F32)) + c_ref[...].astype(F32)

    blk = lambda flip: pl.BlockSpec((None, tr, cols), lambda i, chip_ref: (jnp.bitwise_xor(chip_ref[0], flip), i, 0))
    return pl.pallas_call(
        body, name=name,
        grid_spec=pltpu.PrefetchScalarGridSpec(num_scalar_prefetch=1, grid=(rows // tr,), in_specs=[blk(0), blk(1), blk(2), blk(3)],
                                               out_specs=pl.BlockSpec((tr, cols), lambda i, chip_ref: (i, 0))),
        out_shape=_sds((rows, cols), F32), compiler_params=_params("parallel"),
    )(chip, own, landed, landed, landed)


SMALL_ROWS = 120
PACK_ROWS = 160


def _rows(a, nrows):
    flat = a.reshape(-1)
    return jnp.pad(flat, (0, nrows * LANE - flat.shape[0])).reshape(nrows, LANE)


def _pack_small(b_ada, pre_w_mix, post_w_mix, pre_w_mlp, post_w_mlp, attn_out_w, hg_norm_w, attn_sinks, lb_table):
    return jnp.concatenate([
        _rows(b_ada, 48), _rows(pre_w_mix, 8), _rows(post_w_mix, 8), _rows(pre_w_mlp, 8), _rows(post_w_mlp, 8),
        _rows(attn_out_w, 8), _rows(hg_norm_w, 8), _rows(attn_sinks, 8), _rows(lb_table[0], 8), _rows(lb_table[1], 8)], axis=0)


def _unpack_small(p):
    vec = lambda lo, n: p[lo:lo + n // LANE].reshape(1, n)
    lb = jnp.stack([p[104:108].reshape(HG_WIDTH), p[112:116].reshape(HG_WIDTH)])
    return dict(b_ada=vec(0, N_MOD * D_MODEL), pre_w_mix=vec(48, D_MODEL), post_w_mix=vec(56, D_MODEL), pre_w_mlp=vec(64, D_MODEL),
                post_w_mlp=vec(72, D_MODEL), attn_out_w=vec(80, ATT_WIDTH), hg_norm_w=p[88:89], attn_sinks=p[96:97, :ATT_Q_HEADS],
                lb_table=lb)


def _small_update(packs, w, m, v):
    def body(p_ref, w_ref, m_ref, v_ref, g_ref, dl_ref, nm_ref, nv_ref):
        tot = p_ref[0]
        for d in range(1, N_DEV):
            tot = tot + p_ref[d]
        wv = w_ref[...]
        p1 = _sigmoid(wv[112:120] - wv[104:112])
        s = tot[152:160] * p1 * (1.0 - p1)
        g = jnp.concatenate([tot[0:48] + tot[48:96], tot[96:152], -s, s], axis=0)
        g_ref[...] = g
        dl_ref[...], nm_ref[...], nv_ref[...] = _adamw_math(g, wv, m_ref[...], v_ref[...])

    shp = _sds((SMALL_ROWS, LANE), F32)
    return pl.pallas_call(body, name="small_update", out_shape=[shp] * 4, compiler_params=_params())(packs, w, m, v)


def kernel(x, c, w_ada, b_ada, pre_w_mix, w_in, attn_sinks, attn_out_w, lb_table, hg_norm_w, w_out, post_w_mix, pre_w_mlp, w_up, w_down, post_w_mlp, loss_target, m_w_ada, m_b_ada, m_pre_w_mix, m_w_in, m_attn_sinks, m_attn_out_w, m_lb_table, m_hg_norm_w, m_w_out, m_post_w_mix, m_pre_w_mlp, m_w_up, m_w_down, m_post_w_mlp, v_w_ada, v_b_ada, v_pre_w_mix, v_w_in, v_attn_sinks, v_attn_out_w, v_lb_table, v_hg_norm_w, v_w_out, v_post_w_mix, v_pre_w_mlp, v_w_up, v_w_down, v_post_w_mlp):
    xi, yi, ci = _mesh_pos()
    chip = 2 * xi + yi
    dev = 2 * chip + ci
    bsz, seq, _ = x.shape
    ntok = bsz * seq
    ada_cols = w_ada.shape[2]
    core = jnp.reshape(ci, (1,)).astype(jnp.int32)
    chip_idx = jnp.reshape(chip, (1,)).astype(jnp.int32)
    flat = lambda a: a.reshape(ntok, a.shape[-1])
    unflat = lambda a: a.reshape(bsz, seq, a.shape[-1])
    tables = _rope_tables(seq)

    def row_half(w):
        rows = w.shape[1] // 2
        return lax.dynamic_slice_in_dim(w[0], ci * rows, rows, axis=0).astype(BF16)

    def gather_buffer(w):
        rows, cols = w.shape[1] // 2, w.shape[2]
        own = w[0].astype(BF16).reshape(2, rows, cols)
        return lax.dynamic_update_slice(jnp.zeros((N_DEV, rows, cols), BF16), own, (2 * chip, 0, 0))

    c_g, in_g = _allgather8([c, row_half(w_in)], "gather_first")
    c_all = c_g.reshape(N_DEV * bsz, D_MODEL)
    w_in_full = in_g.reshape(N_CHIPS, D_MODEL, IN_COLS // N_CHIPS).transpose(1, 0, 2).reshape(D_MODEL, IN_COLS)

    b_cols = lax.dynamic_slice_in_dim(b_ada, chip * ada_cols, ada_cols, axis=1)
    mod_part = _ada_fwd(c_all, w_ada[0], b_cols)
    half_rows = mod_part.shape[0] // 2
    (mod_g,) = _allgather8([lax.dynamic_slice_in_dim(mod_part, ci * half_rows, half_rows, axis=0)], "gather_mod")
    mod_all = mod_g.reshape(N_CHIPS, 2, half_rows, ada_cols).transpose(1, 2, 0, 3).reshape(N_DEV * bsz, N_MOD * D_MODEL)
    mod = lax.dynamic_slice_in_dim(mod_all, dev * bsz, bsz, axis=0)
    sh1, sc1, g1, sh2, sc2, g2 = [mod[:, i * D_MODEL:(i + 1) * D_MODEL].reshape(bsz, 1, D_MODEL) for i in range(N_MOD)]

    h1 = _prenorm1(x, pre_w_mix, sc1, sh1)
    proj = unflat(_mm(flat(h1), w_in_full, name="in_proj", out_dtype=F32))
    qr, kr, vb = _rope_fwd(proj, tables)
    blocks = [row_half(w_out), row_half(w_up), row_half(w_down)]
    bufs = [gather_buffer(w_out), gather_buffer(w_up), gather_buffer(w_down)]
    (attn_raw, attn_n, lse), (bufs,) = _attn_fwd(qr, kr, vb, attn_sinks, attn_out_w, comms=[_plan_chip_gather(blocks, bufs)])
    (o_raw, rec, states), (bufs,) = _hgrn_fwd(proj, lb_table, hg_norm_w, comms=[_plan_pair_forward(bufs)])
    w_out_full = bufs[0].reshape(D_MODEL, D_MODEL)
    w_up4 = bufs[1].reshape(N_CHIPS, D_MODEL, D_MODEL)
    w_down_full = bufs[2].reshape(D_FF, D_MODEL)
    cat = jnp.concatenate([attn_n, rec], axis=-1)
    mix = unflat(_mm(flat(cat), w_out_full, name="out_proj", out_dtype=F32))
    x1, h2 = _mid_fwd(x, mix, post_w_mix, g1, pre_w_mlp, sc2, sh2)
    up_spec = pl.BlockSpec((None, D_MODEL, D_MODEL), lambda i, j, k: (j, 0, 0))
    r = _mm(flat(h2), w_up4, name="up_proj", out_dtype=BF16, tn=D_MODEL, n_out=D_FF, b_spec=up_spec,
            epi=lambda acc: jnp.maximum(acc, 0.0))
    square = lambda t: t * t
    down = unflat(_mm(r, w_down_full, name="down_proj", out_dtype=F32, a_fn=square))
    loss_row, dy, dd, dg2, d_post_mlp = _loss_bwd(x1, down, post_w_mlp, g2, loss_target)
    loss = lax.psum(loss_row[0, 0], ("x", "y", "c"))

    dpre = _mm(flat(dd), w_down_full, name="down_bwd", out_dtype=BF16, trans_b=True, tn=D_MODEL, extra=(r,),
               epi=lambda acc, rt: acc * (2.0 * rt.astype(F32)))
    half_rows = D_MODEL // 2
    g_down = _mm_tn(r, flat(dd), name="down_wgrad", tk=half_rows, tn=D_MODEL, a_fn=square,
                    out_shape=_sds((2, N_CHIPS, half_rows, D_MODEL), F32),
                    out_spec=pl.BlockSpec((None, None, half_rows, D_MODEL), lambda i, j, m: (i % 2, i // 2, 0, 0)))
    up_t_spec = pl.BlockSpec((None, D_MODEL, D_MODEL), lambda i, j, k: (k, 0, 0))
    dh2, ((q_down,),) = _mm(dpre, w_up4, name="up_bwd", out_dtype=F32, trans_b=True, nk=N_CHIPS, n_out=D_MODEL, b_spec=up_t_spec,
                            k_total=D_FF, comms=[_plan_pair([g_down], True)])
    g_up = _mm_tn(flat(h2), dpre, name="up_wgrad", tk=half_rows, tn=D_MODEL,
                  out_shape=_sds((2, N_CHIPS, half_rows, D_MODEL), F32),
                  out_spec=pl.BlockSpec((None, None, half_rows, D_MODEL), lambda i, j, m: (i, j, 0, 0)))
    s_down = _pair_sum(g_down, q_down, core, "pair_sum_down")
    (dx1, dmix, dsc2, dsh2, dg1, d_pre_mlp, d_post_mix), ((q_up,),) = _mid_bwd(
        unflat(dh2), dy, x1, mix, pre_w_mlp, sc2, post_w_mix, g1, comms=[_plan_pair([g_up], True)])
    s_up = _pair_sum(g_up, q_up, core, "pair_sum_up")

    dcat = unflat(_mm(flat(dmix), w_out_full, name="out_bwd", out_dtype=F32, trans_b=True))
    out_rows = D_MODEL // N_CHIPS
    g_out = _mm_tn(flat(cat), flat(dmix), name="out_wgrad", tk=out_rows, tn=half_rows,
                   out_shape=_sds((2, N_CHIPS, out_rows, half_rows), F32),
                   out_spec=pl.BlockSpec((None, None, out_rows, half_rows), lambda i, j, m: (j, i, 0, 0)))
    (dhq, dhf, dhi, dhg, d_lb, d_hg_norm), ((x_down, x_up), (q_out,)) = _hgrn_bwd(
        dcat, proj, o_raw, states, lb_table, hg_norm_w, comms=[_plan_chip_exchange([s_down, s_up]), _plan_pair([g_out], True)])
    half_down = _sum_chips(s_down, x_down, chip_idx, "sum_chips_down")
    half_up = _sum_chips(s_up, x_up, chip_idx, "sum_chips_up")
    s_out = _pair_sum(g_out, q_out, core, "pair_sum_out")
    (dq, dkd, dkp, dvd, dvp, d_attn_out, d_sinks), ((their_down, their_up), (x_out,)) = _attn_bwd(
        dcat, attn_raw, attn_out_w, qr, kr, vb, lse, attn_sinks,
        comms=[_plan_pair([half_down, half_up], False), _plan_chip_exchange([s_out])])
    half_out = _sum_chips(s_out, x_out, chip_idx, "sum_chips_out")
    dproj_a = _rope_bwd(dq, dkd, dkp, dvd, dvp, tables)
    dproj = flat(jnp.concatenate([dproj_a, dhq, dhf, dhi, dhg], axis=-1))
    g_in = _mm_tn(flat(h1), dproj, name="in_wgrad", tk=D_MODEL, tn=IN_COLS // 2)
    g_in = g_in.reshape(2, half_rows, N_CHIPS, IN_COLS // N_CHIPS).transpose(0, 2, 1, 3)
    dh1, ((q_in,), (their_out,)) = _mm(dproj, w_in_full, name="in_bwd", out_dtype=F32, trans_b=True,
                                       comms=[_plan_pair([g_in], True), _plan_pair([half_out], False)])
    s_in = _pair_sum(g_in, q_in, core, "pair_sum_in")
    grad_x, dsc1, dsh1, d_pre_mix = _norm1_bwd(unflat(dh1), dx1, x, pre_w_mix, sc1)

    dmod = jnp.concatenate([dsh1, dsc1, dg1, dsh2, dsc2, dg2], axis=-1).reshape(bsz, N_MOD * D_MODEL)
    pack = jnp.concatenate([
        _rows(dmod, 96), _rows(d_pre_mix, 8), _rows(d_post_mix, 8), _rows(d_pre_mlp, 8), _rows(d_post_mlp, 8),
        _rows(d_attn_out, 8), _rows(d_hg_norm, 8), _rows(d_sinks, 8), _rows(d_lb, 8)], axis=0)
    (packs,), (x_in,) = _comm_only([_plan_allgather8([pack]), _plan_chip_exchange([s_in])], "gather_small")
    half_in = _sum_chips(s_in, x_in, chip_idx, "sum_chips_in")
    ((their_in,),) = _comm_only([_plan_pair([half_in], False)], "pair_swap_in")
    small_args = lambda pre: (pre["b_ada"], pre["pre_w_mix"], pre["post_w_mix"], pre["pre_w_mlp"], pre["post_w_mlp"],
                              pre["attn_out_w"], pre["hg_norm_w"], pre["attn_sinks"], pre["lb_table"])
    w_small = dict(b_ada=b_ada, pre_w_mix=pre_w_mix, post_w_mix=post_w_mix, pre_w_mlp=pre_w_mlp, post_w_mlp=post_w_mlp,
                   attn_out_w=attn_out_w, hg_norm_w=hg_norm_w, attn_sinks=attn_sinks, lb_table=lb_table)
    m_small = dict(b_ada=m_b_ada, pre_w_mix=m_pre_w_mix, post_w_mix=m_post_w_mix, pre_w_mlp=m_pre_w_mlp, post_w_mlp=m_post_w_mlp,
                   attn_out_w=m_attn_out_w, hg_norm_w=m_hg_norm_w, attn_sinks=m_attn_sinks, lb_table=m_lb_table)
    v_small = dict(b_ada=v_b_ada, pre_w_mix=v_pre_w_mix, post_w_mix=v_post_w_mix, pre_w_mlp=v_pre_w_mlp, post_w_mlp=v_post_w_mlp,
                   attn_out_w=v_attn_out_w, hg_norm_w=v_hg_norm_w, attn_sinks=v_attn_sinks, lb_table=v_lb_table)
    small_out = [_unpack_small(p) for p in _small_update(packs, _pack_small(*small_args(w_small)), _pack_small(*small_args(m_small)),
                                                         _pack_small(*small_args(v_small)))]

    dmod_all = packs[:, :96, :].reshape(N_DEV * bsz, N_MOD * D_MODEL)
    dmod_cols = lax.dynamic_slice_in_dim(dmod_all, chip * ada_cols, ada_cols, axis=1)
    ada_out = _ada_bwd_adamw(c_all, dmod_cols, w_ada[0], m_w_ada[0], v_w_ada[0])

    big = dict(
        w_in=tuple(_adamw_halves(half_in, their_in, core, w_in[0], m_w_in[0], v_w_in[0], axis=0, name="adamw_in")),
        w_up=tuple(_adamw_halves(half_up, their_up, core, w_up[0], m_w_up[0], v_w_up[0], axis=0, name="adamw_up")),
        w_out=tuple(_adamw_halves(half_out, their_out, core, w_out[0], m_w_out[0], v_w_out[0], axis=1, name="adamw_out")),
        w_down=tuple(_adamw_halves(half_down, their_down, core, w_down[0], m_w_down[0], v_w_down[0], axis=0, name="adamw_down")),
        w_ada=tuple(ada_out),
    )
    order = ("w_ada", "b_ada", "pre_w_mix", "w_in", "attn_sinks", "attn_out_w", "lb_table", "hg_norm_w", "w_out", "post_w_mix",
             "pre_w_mlp", "w_up", "w_down", "post_w_mlp")
    outs = [loss, grad_x]
    for kind in range(4):
        for nm in order:
            outs.append(big[nm][kind][None] if nm in big else small_out[kind][nm])
    return tuple(outs)
```

```python
import functools

import jax
import jax.numpy as jnp
from jax import lax
from jax.experimental import pallas as pl
from jax.experimental.pallas import tpu as pltpu

F32 = jnp.float32
BF16 = jnp.bfloat16

D_MODEL = 1024
ATT_WIDTH = 512
ATT_HEAD_DIM = 64
ATT_Q_HEADS = 8
ATT_KV_HEADS = 2
ATT_GROUP = ATT_Q_HEADS // ATT_KV_HEADS
ATT_KV_COLS = ATT_KV_HEADS * ATT_HEAD_DIM
WINDOW = 128
ROPE_DIM = 16
ROPE_THETA = 500000.0
HG_WIDTH = 512
HG_HEAD_DIM = 128
HG_HEADS = 4
HG_CHUNK = 32
IN_COLS = ATT_WIDTH + 2 * ATT_KV_COLS + 4 * HG_WIDTH
ATT_COLS = ATT_WIDTH + 2 * ATT_KV_COLS
D_FF = 4 * D_MODEL
N_MOD = 6
EPS = 1e-6
ATT_SCALE = ATT_HEAD_DIM ** -0.5

ADAM_LR = 0.001
ADAM_B1 = 0.9
ADAM_B2 = 0.999
ADAM_EPS = 1e-08
ADAM_WD = 0.01
ADAM_STEP = 10

N_CHIPS = 4
N_DEV = 8
LANE = 128
VMEM_LIMIT = 48 * 1024 * 1024
MESH = pl.DeviceIdType.MESH

NT_DIMS = (((1,), (1,)), ((), ()))
TN_DIMS = (((0,), (0,)), ((), ()))


def _sds(shape, dtype):
    return jax.ShapeDtypeStruct(tuple(shape), dtype)


def _params(*sem):
    return pltpu.CompilerParams(dimension_semantics=sem, vmem_limit_bytes=VMEM_LIMIT)


def _sigmoid(x):
    return 1.0 / (1.0 + jnp.exp(-x))


def _dot(a, b, dims=None):
    a, b = a.astype(BF16), b.astype(BF16)
    if dims is None:
        return jnp.dot(a, b, preferred_element_type=F32)
    return lax.dot_general(a, b, dims, preferred_element_type=F32)


def _rms_fwd(x, w):
    rstd = lax.rsqrt(jnp.mean(x * x, axis=-1, keepdims=True) + EPS)
    xh = x * rstd
    return xh * w, xh, rstd


def _rms_bwd(dy, xh, rstd, w):
    dxh = dy * w
    dx = rstd * (dxh - xh * jnp.mean(dxh * xh, axis=-1, keepdims=True))
    return dx, dy * xh


def _colsum(x):
    return jnp.sum(x, axis=0, keepdims=True)


HBM_SPEC = pl.BlockSpec(memory_space=pltpu.HBM)


def _mesh_pos():
    return lax.axis_index("x"), lax.axis_index("y"), lax.axis_index("c")


class _Comm:
    def __init__(self, ins, outs, sems, start, finish, aliases=()):
        self.ins, self.outs, self.sems = list(ins), list(outs), list(sems)
        self.start, self.finish, self.aliases = start, finish, tuple(aliases)


def _call(body, args, *, name, grid, in_specs, out_specs, out_shape, sem, scratch_shapes=(), comms=()):
    scratch_shapes = list(scratch_shapes)
    if not comms:
        return pl.pallas_call(body, name=name, grid=grid, in_specs=in_specs, out_specs=out_specs, out_shape=out_shape,
                              scratch_shapes=scratch_shapes, compiler_params=_params(*sem))(*args)
    single = not isinstance(out_shape, (list, tuple))
    out_specs_l = [out_specs] if single else list(out_specs)
    out_shape_l = [out_shape] if single else list(out_shape)
    n_in, n_out, n_scr = len(in_specs), len(out_shape_l), len(scratch_shapes)
    n_ci = [len(cm.ins) for cm in comms]
    n_co = [len(cm.outs) for cm in comms]
    n_cs = [len(cm.sems) for cm in comms]
    aliases = {}
    for k, cm in enumerate(comms):
        for i, o in cm.aliases:
            aliases[n_in + sum(n_ci[:k]) + i] = n_out + sum(n_co[:k]) + o

    def fused(*refs):
        pos = [0]

        def take(n):
            part = refs[pos[0]:pos[0] + n]
            pos[0] += n
            return part

        ins = take(n_in)
        c_ins = [take(n) for n in n_ci]
        outs = take(n_out)
        c_outs = [take(n) for n in n_co]
        scr = take(n_scr)
        c_sems = [take(n) for n in n_cs]
        first, last = True, True
        for d, size in enumerate(grid):
            first = jnp.logical_and(first, pl.program_id(d) == 0)
            last = jnp.logical_and(last, pl.program_id(d) == size - 1)

        def run(which):
            for cm, ci, co, cs in zip(comms, c_ins, c_outs, c_sems):
                getattr(cm, which)(ci, co, cs)

        if grid:
            pl.when(first)(lambda: run("start"))
        else:
            run("start")
        body(*ins, *outs, *scr)
        if grid:
            pl.when(last)(lambda: run("finish"))
        else:
            run("finish")

    res = pl.pallas_call(
        fused, name=name, grid=grid, in_specs=list(in_specs) + [HBM_SPEC] * sum(n_ci),
        out_specs=out_specs_l + [HBM_SPEC] * sum(n_co), out_shape=out_shape_l + [s for cm in comms for s in cm.outs],
        input_output_aliases=aliases, scratch_shapes=scratch_shapes + [s for cm in comms for s in cm.sems],
        compiler_params=_params(*["arbitrary"] * len(grid)),
    )(*args, *[a for cm in comms for a in cm.ins])
    main = res[:n_out]
    extra, at = [], n_out
    for n in n_co:
        extra.append(list(res[at:at + n]))
        at += n
    return (main[0] if single else list(main)), extra


def _mm(a, b, *, name, out_dtype, trans_b=False, tm=512, tn=None, a_fn=None, extra=(), epi=None,
        b_spec=None, n_out=None, b_chunks=1, comms=()):
    m_total, k_total = a.shape
    if n_out is None:
        n_out = b.shape[0] if trans_b else b.shape[1]
    tn = n_out if tn is None else tn
    grid = (m_total // tm, n_out // tn)
    dims = NT_DIMS if trans_b else None
    kc = k_total // b_chunks

    def body(*refs):
        a_ref, b_ref = refs[0], refs[1]
        extra_refs = refs[2:2 + len(extra)]
        o_ref = refs[2 + len(extra)]
        if b_chunks == 1:
            av = a_ref[...]
            acc = _dot(av if a_fn is None else a_fn(av), b_ref[...], dims)
        else:
            acc = _dot(a_ref[:, 0:kc], b_ref[0], NT_DIMS)
            for k in range(1, b_chunks):
                acc = acc + _dot(a_ref[:, k * kc:(k + 1) * kc], b_ref[k], NT_DIMS)
        if epi is not None:
            acc = epi(acc, *[r[...] for r in extra_refs])
        o_ref[...] = acc.astype(out_dtype)

    if b_spec is None:
        if b_chunks > 1:
            b_spec = pl.BlockSpec((b_chunks, tn, kc), lambda i, j: (0, j, 0))
        elif trans_b:
            b_spec = pl.BlockSpec((tn, k_total), lambda i, j: (j, 0))
        else:
            b_spec = pl.BlockSpec((k_total, tn), lambda i, j: (0, j))
    in_specs = [pl.BlockSpec((tm, k_total), lambda i, j: (i, 0)), b_spec]
    in_specs += [pl.BlockSpec((tm, tn), lambda i, j: (i, j)) for _ in extra]
    return _call(
        body, (a, b, *extra), name=name, grid=grid, in_specs=in_specs,
        out_specs=pl.BlockSpec((tm, tn), lambda i, j: (i, j)),
        out_shape=_sds((m_total, n_out), out_dtype),
        sem=("parallel", "parallel"), comms=comms)


def _mm_tn(a, b, *, name, tk, tn, a_fn=None, out_shape=None, out_spec=None):
    m_total, k_total = a.shape
    n_total = b.shape[1]
    grid = (k_total // tk, n_total // tn)

    def body(a_ref, b_ref, o_ref):
        av = a_ref[...]
        part = _dot(av if a_fn is None else a_fn(av), b_ref[...], TN_DIMS)
        o_ref[...] = part.reshape(o_ref.shape)

    if out_shape is None:
        out_shape = _sds((k_total, n_total), F32)
        out_spec = pl.BlockSpec((tk, tn), lambda i, j: (i, j))
    return pl.pallas_call(
        body, name=name, grid=grid,
        in_specs=[pl.BlockSpec((m_total, tk), lambda i, j: (0, i)), pl.BlockSpec((m_total, tn), lambda i, j: (0, j))],
        out_specs=out_spec, out_shape=out_shape,
        compiler_params=_params("parallel", "parallel"),
    )(a, b)


def _ada_fwd(c_all, w_shard, b_shard):
    nb, ncol = c_all.shape[0], w_shard.shape[1]
    tn = 512

    def body(c_ref, w_ref, b_ref, o_ref):
        c = c_ref[...]
        o_ref[...] = _dot(c * _sigmoid(c), w_ref[...]) + b_ref[...]

    return pl.pallas_call(
        body, name="ada_fwd", grid=(ncol // tn,),
        in_specs=[pl.BlockSpec((nb, D_MODEL), lambda j: (0, 0)), pl.BlockSpec((D_MODEL, tn), lambda j: (0, j)),
                  pl.BlockSpec((1, tn), lambda j: (0, j))],
        out_specs=pl.BlockSpec((nb, tn), lambda j: (0, j)), out_shape=_sds((nb, ncol), F32),
        compiler_params=_params("parallel"),
    )(c_all, w_shard, b_shard)


def _adamw_math(g, w, m, v):
    m = ADAM_B1 * m + (1.0 - ADAM_B1) * g
    v = ADAM_B2 * v + (1.0 - ADAM_B2) * (g * g)
    m_hat = m / (1.0 - ADAM_B1 ** ADAM_STEP)
    v_hat = v / (1.0 - ADAM_B2 ** ADAM_STEP)
    delta = -ADAM_LR * (m_hat / (jnp.sqrt(v_hat) + ADAM_EPS) + ADAM_WD * w)
    return delta, m, v


def _ada_bwd_adamw(c_all, dmod_cols, w, m, v):
    nb, ncol = dmod_cols.shape
    tn = 256

    def body(c_ref, d_ref, w_ref, m_ref, v_ref, g_ref, dl_ref, nm_ref, nv_ref):
        c = c_ref[...]
        g = _dot(c * _sigmoid(c), d_ref[...], TN_DIMS)
        g_ref[...] = g
        dl_ref[...], nm_ref[...], nv_ref[...] = _adamw_math(g, w_ref[...], m_ref[...], v_ref[...])

    col = pl.BlockSpec((D_MODEL, tn), lambda j: (0, j))
    shp = _sds((D_MODEL, ncol), F32)
    return pl.pallas_call(
        body, name="ada_bwd_adamw", grid=(ncol // tn,),
        in_specs=[pl.BlockSpec((nb, D_MODEL), lambda j: (0, 0)), pl.BlockSpec((nb, tn), lambda j: (0, j)), col, col, col],
        out_specs=[col, col, col, col], out_shape=[shp, shp, shp, shp],
        compiler_params=_params("parallel"),
    )(c_all, dmod_cols, w, m, v)


def _adamw_halves(own, theirs, core, w, m, v, *, axis, name):
    r2, c2 = own.shape
    tr = min(r2, 256)
    nt = r2 // tr

    def body(core_ref, own_ref, their_ref, w_ref, m_ref, v_ref, g_ref, dl_ref, nm_ref, nv_ref):
        g = jnp.where(pl.program_id(0) == core_ref[0], own_ref[...], their_ref[...])
        g_ref[...] = g
        dl_ref[...], nm_ref[...], nv_ref[...] = _adamw_math(g, w_ref[...], m_ref[...], v_ref[...])

    if axis == 0:
        full = pl.BlockSpec((tr, c2), lambda h, i, core_ref: (h * nt + i, 0))
    else:
        full = pl.BlockSpec((tr, c2), lambda h, i, core_ref: (i, h))
    half = pl.BlockSpec((tr, c2), lambda h, i, core_ref: (i, 0))
    shp = _sds(w.shape, F32)
    return pl.pallas_call(
        body, name=name,
        grid_spec=pltpu.PrefetchScalarGridSpec(num_scalar_prefetch=1, grid=(2, nt), in_specs=[half, half, full, full, full],
                                               out_specs=[full] * 4),
        out_shape=[shp] * 4, compiler_params=_params("parallel", "parallel"),
    )(core, own, theirs, w, m, v)


def _tok_spec(tm, width=D_MODEL):
    return pl.BlockSpec((None, tm, width), lambda b, i: (b, i, 0))


def _row_spec(width=D_MODEL):
    return pl.BlockSpec((None, 1, width), lambda b, i: (b, 0, 0))


def _vec_spec(width=D_MODEL):
    return pl.BlockSpec((1, width), lambda b, i: (0, 0))


def _prenorm1(x, w, sc, sh, tm=512):
    bsz, seq, _ = x.shape

    def body(x_ref, w_ref, sc_ref, sh_ref, h_ref):
        y, _, _ = _rms_fwd(x_ref[...], w_ref[...])
        h_ref[...] = (y * (1.0 + sc_ref[...]) + sh_ref[...]).astype(BF16)

    return pl.pallas_call(
        body, name="prenorm1", grid=(bsz, seq // tm),
        in_specs=[_tok_spec(tm), _vec_spec(), _row_spec(), _row_spec()],
        out_specs=_tok_spec(tm), out_shape=_sds(x.shape, BF16),
        compiler_params=_params("parallel", "parallel"),
    )(x, w, sc, sh)


def _rope_tables(seq):
    half = ROPE_DIM // 2
    inv_freq = ROPE_THETA ** (-jnp.arange(0, ROPE_DIM, 2, dtype=F32) / ROPE_DIM)
    ang = jnp.arange(seq, dtype=F32)[:, None] * inv_freq[None, :]
    cos, sin = jnp.cos(ang), jnp.sin(ang)
    rest = ATT_HEAD_DIM - ROPE_DIM
    ones, zeros, zh = jnp.ones((seq, rest), F32), jnp.zeros((seq, rest), F32), jnp.zeros((seq, half), F32)
    reps = LANE // ATT_HEAD_DIM
    t_cos = jnp.tile(jnp.concatenate([cos, cos, ones], axis=1), (1, reps))
    t_up = jnp.tile(jnp.concatenate([zh, sin, zeros], axis=1), (1, reps))
    t_dn = jnp.tile(jnp.concatenate([-sin, zh, zeros], axis=1), (1, reps))
    return t_cos, t_up, t_dn


def _rope_fwd(proj, tables, tm=256):
    bsz, seq, _ = proj.shape
    half = ROPE_DIM // 2

    def body(p_ref, c_ref, u_ref, d_ref, q_ref, k_ref, v_ref):
        c, u, d = c_ref[...], u_ref[...], d_ref[...]

        def rope(x):
            return x * c + pltpu.roll(x, half, 1) * u + pltpu.roll(x, LANE - half, 1) * d

        for s in range(ATT_WIDTH // LANE):
            q_ref[:, s * LANE:(s + 1) * LANE] = rope(p_ref[:, s * LANE:(s + 1) * LANE]).astype(BF16)
        k_ref[...] = rope(p_ref[:, ATT_WIDTH:ATT_WIDTH + LANE]).astype(BF16)
        v_ref[...] = p_ref[:, ATT_WIDTH + LANE:ATT_COLS].astype(BF16)

    tab = pl.BlockSpec((tm, LANE), lambda b, i: (i, 0))
    return pl.pallas_call(
        body, name="rope_fwd", grid=(bsz, seq // tm),
        in_specs=[_tok_spec(tm, ATT_COLS), tab, tab, tab],
        out_specs=[_tok_spec(tm, ATT_WIDTH), _tok_spec(tm, LANE), _tok_spec(tm, LANE)],
        out_shape=[_sds((bsz, seq, ATT_WIDTH), BF16), _sds((bsz, seq, LANE), BF16), _sds((bsz, seq, LANE), BF16)],
        compiler_params=_params("parallel", "parallel"),
    )(proj, *tables)


def _band_masks(i):
    row = lax.broadcasted_iota(jnp.int32, (WINDOW, WINDOW), 0)
    col = lax.broadcasted_iota(jnp.int32, (WINDOW, WINDOW), 1)
    return col <= row, jnp.logical_and(col > row, i > 0)


def _prev_spec(width):
    return pl.BlockSpec((None, WINDOW, width), lambda b, i: (b, jnp.maximum(i - 1, 0), 0))


def _attn_fwd(qr, kr, vb, sinks, w_norm, comms=()):
    bsz, seq, _ = qr.shape
    nblk = seq // WINDOW
    neg = float(jnp.finfo(jnp.float32).min)

    def body(sink_ref, q_ref, kc_ref, kp_ref, vc_ref, vp_ref, w_ref, raw_ref, an_ref, l_ref):
        mask_c, mask_p = _band_masks(pl.program_id(1))
        for h in range(ATT_Q_HEADS):
            g = h // ATT_GROUP
            hs = slice(h * ATT_HEAD_DIM, (h + 1) * ATT_HEAD_DIM)
            gs = slice(g * ATT_HEAD_DIM, (g + 1) * ATT_HEAD_DIM)
            q = q_ref[:, hs]
            sink = sink_ref[0, h]
            sc = jnp.where(mask_c, _dot(q, kc_ref[:, gs], NT_DIMS) * ATT_SCALE, neg)
            sp = jnp.where(mask_p, _dot(q, kp_ref[:, gs], NT_DIMS) * ATT_SCALE, neg)
            m = jnp.maximum(jnp.maximum(jnp.max(sc, axis=-1, keepdims=True), jnp.max(sp, axis=-1, keepdims=True)), sink)
            pc = jnp.where(mask_c, jnp.exp(sc - m), 0.0)
            pp = jnp.where(mask_p, jnp.exp(sp - m), 0.0)
            den = jnp.sum(pc, axis=-1, keepdims=True) + jnp.sum(pp, axis=-1, keepdims=True) + jnp.exp(sink - m)
            raw_ref[:, hs] = _dot(pc / den, vc_ref[:, gs]) + _dot(pp / den, vp_ref[:, gs])
            l_ref[:, h:h + 1] = m + jnp.log(den)
        y, _, _ = _rms_fwd(raw_ref[...], w_ref[...])
        an_ref[...] = y.astype(BF16)

    cur = lambda width: pl.BlockSpec((None, WINDOW, width), lambda b, i: (b, i, 0))
    return _call(
        body, (sinks, qr, kr, kr, vb, vb, w_norm), name="attn_fwd", grid=(bsz, nblk),
        in_specs=[pl.BlockSpec(memory_space=pltpu.SMEM), cur(ATT_WIDTH), cur(LANE), _prev_spec(LANE), cur(LANE), _prev_spec(LANE),
                  _vec_spec(ATT_WIDTH)],
        out_specs=[cur(ATT_WIDTH), cur(ATT_WIDTH), cur(ATT_Q_HEADS)],
        out_shape=[_sds((bsz, seq, ATT_WIDTH), F32), _sds((bsz, seq, ATT_WIDTH), BF16), _sds((bsz, seq, ATT_Q_HEADS), F32)],
        sem=("parallel", "parallel"), comms=comms)


HG_Q0 = ATT_COLS // LANE
HG_F0 = HG_Q0 + HG_HEADS
HG_I0 = HG_F0 + HG_HEADS
HG_G0 = HG_I0 + HG_HEADS
HG_TOK = 256
HG_NCH = HG_TOK // HG_CHUNK


def _chunk_consts():
    row = lax.broadcasted_iota(jnp.int32, (HG_CHUNK, HG_CHUNK), 0)
    col = lax.broadcasted_iota(jnp.int32, (HG_CHUNK, HG_CHUNK), 1)
    return row >= col, row <= col


def _cumsum_rows(tri_f32, x):
    return jnp.dot(tri_f32, x, precision=lax.Precision.HIGHEST, preferred_element_type=F32)


def _hgrn_gates(tbl, hf, hq):
    lb = _sigmoid(tbl[1:2] - tbl[0:1])
    sig = _sigmoid(hf)
    f = lb + (1.0 - lb) * sig
    sq = _sigmoid(hq)
    return lb, sig, f, sq


def _hgrn_fwd(proj, lb_table, norm_w, comms=()):
    bsz, seq, _ = proj.shape
    nstep = seq // HG_TOK

    def body(tbl_ref, nw_ref, q_ref, f_ref, i_ref, g_ref, o_ref, rec_ref, st_ref, s_scr):
        @pl.when(pl.program_id(2) == 0)
        def _():
            s_scr[...] = jnp.zeros_like(s_scr)

        lower, _ = _chunk_consts()
        tri = lower.astype(F32)
        nw = nw_ref[...]
        st = s_scr[...]
        o_parts, rec_parts = [], []
        for j in range(HG_NCH):
            sl = slice(j * HG_CHUNK, (j + 1) * HG_CHUNK)
            hq, v = q_ref[sl, :], i_ref[sl, :]
            _, _, f, sq = _hgrn_gates(tbl_ref[...], f_ref[sl, :], hq)
            q, k = hq * sq, 1.0 - f
            b = _cumsum_rows(tri, jnp.log(f))
            bl = b[HG_CHUNK - 1:HG_CHUNK, :]
            qd = q * jnp.exp(b)
            kd = k * jnp.exp(-b)
            ku = k * jnp.exp(bl - b)
            a = jnp.where(lower, _dot(qd, kd, NT_DIMS), 0.0)
            st_ref[j] = st
            o = _dot(a, v) + _dot(qd, st, NT_DIMS)
            st = st * jnp.exp(bl) + _dot(v, ku, TN_DIMS)
            y, _, _ = _rms_fwd(o, nw)
            hg = g_ref[sl, :]
            o_parts.append(o)
            rec_parts.append(y * (hg * _sigmoid(hg)))
        s_scr[...] = st
        o_ref[...] = jnp.concatenate(o_parts, axis=0)
        rec_ref[...] = jnp.concatenate(rec_parts, axis=0).astype(BF16)

    slab = lambda first: pl.BlockSpec((None, HG_TOK, LANE), lambda b, h, t: (b, t, first + h))
    head_out = pl.BlockSpec((None, HG_TOK, LANE), lambda b, h, t: (b, t, h))
    return _call(
        body, (lb_table, norm_w, proj, proj, proj, proj), name="hgrn_fwd", grid=(bsz, HG_HEADS, nstep),
        in_specs=[pl.BlockSpec((2, LANE), lambda b, h, t: (0, h)), pl.BlockSpec((1, LANE), lambda b, h, t: (0, 0)),
                  slab(HG_Q0), slab(HG_F0), slab(HG_I0), slab(HG_G0)],
        out_specs=[head_out, head_out,
                   pl.BlockSpec((None, None, HG_NCH, LANE, LANE), lambda b, h, t: (b, h, t, 0, 0))],
        out_shape=[_sds((bsz, seq, HG_WIDTH), F32), _sds((bsz, seq, HG_WIDTH), BF16),
                   _sds((bsz, HG_HEADS, seq // HG_CHUNK, LANE, LANE), F32)],
        scratch_shapes=[pltpu.VMEM((LANE, LANE), F32)],
        sem=("parallel", "parallel", "arbitrary"), comms=comms)


def _mid_fwd(x, mix, post_w, g1, pre_w, sc2, sh2, tm=512):
    bsz, seq, _ = x.shape

    def body(x_ref, mix_ref, pw_ref, g1_ref, w2_ref, sc_ref, sh_ref, x1_ref, h2_ref):
        n1, _, _ = _rms_fwd(mix_ref[...], pw_ref[...])
        x1 = x_ref[...] + g1_ref[...] * n1
        x1_ref[...] = x1
        y2, _, _ = _rms_fwd(x1, w2_ref[...])
        h2_ref[...] = (y2 * (1.0 + sc_ref[...]) + sh_ref[...]).astype(BF16)

    return pl.pallas_call(
        body, name="mid_fwd", grid=(bsz, seq // tm),
        in_specs=[_tok_spec(tm), _tok_spec(tm), _vec_spec(), _row_spec(), _vec_spec(), _row_spec(), _row_spec()],
        out_specs=[_tok_spec(tm), _tok_spec(tm)], out_shape=[_sds(x.shape, F32), _sds(x.shape, BF16)],
        compiler_params=_params("parallel", "parallel"),
    )(x, mix, post_w, g1, pre_w, sc2, sh2)


def _acc_out(ref, first, value):
    @pl.when(first)
    def _():
        ref[...] = value

    @pl.when(jnp.logical_not(first))
    def _():
        ref[...] += value


def _loss_bwd(x1, down, post_w, g2, target, tm=512):
    bsz, seq, _ = x1.shape

    def body(x1_ref, d_ref, w_ref, g2_ref, t_ref, loss_ref, dy_ref, dd_ref, dg2_ref, dw_ref):
        b, i = pl.program_id(0), pl.program_id(1)
        w, g2v = w_ref[...], g2_ref[...]
        n2, dh, rstd = _rms_fwd(d_ref[...], w)
        err = x1_ref[...] + g2v * n2 - t_ref[...]
        part = (0.5 / D_MODEL) * jnp.sum(jnp.sum(err * err, axis=-1, keepdims=True), axis=0, keepdims=True)
        _acc_out(loss_ref, jnp.logical_and(b == 0, i == 0), jnp.broadcast_to(part, (1, LANE)))
        dy = err * (1.0 / D_MODEL)
        dy_ref[...] = dy
        _acc_out(dg2_ref, i == 0, _colsum(dy * n2))
        dd, dw_rows = _rms_bwd(dy * g2v, dh, rstd, w)
        dd_ref[...] = dd.astype(BF16)
        _acc_out(dw_ref, jnp.logical_and(b == 0, i == 0), _colsum(dw_rows))

    return pl.pallas_call(
        body, name="loss_bwd", grid=(bsz, seq // tm),
        in_specs=[_tok_spec(tm), _tok_spec(tm), _vec_spec(), _row_spec(), _tok_spec(tm)],
        out_specs=[_vec_spec(LANE), _tok_spec(tm), _tok_spec(tm), _row_spec(), _vec_spec()],
        out_shape=[_sds((1, LANE), F32), _sds(x1.shape, F32), _sds(x1.shape, BF16), _sds((bsz, 1, D_MODEL), F32),
                   _sds((1, D_MODEL), F32)],
        compiler_params=_params("arbitrary", "arbitrary"),
    )(x1, down, post_w, g2, target)


def _mid_bwd(dh2, dy, x1, mix, pre_w, sc2, post_w, g1, tm=512, comms=()):
    bsz, seq, _ = x1.shape

    def body(dh2_ref, dy_ref, x1_ref, mix_ref, w2_ref, sc_ref, pw_ref, g1_ref,
             dx1_ref, dmix_ref, dsc_ref, dsh_ref, dg1_ref, dw2_ref, dpw_ref):
        b, i = pl.program_id(0), pl.program_id(1)
        first = jnp.logical_and(b == 0, i == 0)
        w2, pw = w2_ref[...], pw_ref[...]
        dh2v = dh2_ref[...]
        y2, xh2, rstd2 = _rms_fwd(x1_ref[...], w2)
        _acc_out(dsh_ref, i == 0, _colsum(dh2v))
        _acc_out(dsc_ref, i == 0, _colsum(dh2v * y2))
        dx1n, dw_rows = _rms_bwd(dh2v * (1.0 + sc_ref[...]), xh2, rstd2, w2)
        _acc_out(dw2_ref, first, _colsum(dw_rows))
        dx1 = dy_ref[...] + dx1n
        dx1_ref[...] = dx1
        n1, mh, rstd1 = _rms_fwd(mix_ref[...], pw)
        _acc_out(dg1_ref, i == 0, _colsum(dx1 * n1))
        dmix, dpw_rows = _rms_bwd(dx1 * g1_ref[...], mh, rstd1, pw)
        dmix_ref[...] = dmix.astype(BF16)
        _acc_out(dpw_ref, first, _colsum(dpw_rows))

    row_shape = _sds((bsz, 1, D_MODEL), F32)
    vec_shape = _sds((1, D_MODEL), F32)
    return _call(
        body, (dh2, dy, x1, mix, pre_w, sc2, post_w, g1), name="mid_bwd", grid=(bsz, seq // tm),
        in_specs=[_tok_spec(tm), _tok_spec(tm), _tok_spec(tm), _tok_spec(tm), _vec_spec(), _row_spec(), _vec_spec(), _row_spec()],
        out_specs=[_tok_spec(tm), _tok_spec(tm), _row_spec(), _row_spec(), _row_spec(), _vec_spec(), _vec_spec()],
        out_shape=[_sds(x1.shape, F32), _sds(x1.shape, BF16), row_shape, row_shape, row_shape, vec_shape, vec_shape],
        sem=("arbitrary", "arbitrary"), comms=comms)


def _norm1_bwd(dh1, dx1, x, pre_w, sc1, tm=512):
    bsz, seq, _ = x.shape

    def body(dh_ref, dx1_ref, x_ref, w_ref, sc_ref, gx_ref, dsc_ref, dsh_ref, dw_ref):
        b, i = pl.program_id(0), pl.program_id(1)
        w = w_ref[...]
        dh = dh_ref[...]
        y, xh, rstd = _rms_fwd(x_ref[...], w)
        _acc_out(dsh_ref, i == 0, _colsum(dh))
        _acc_out(dsc_ref, i == 0, _colsum(dh * y))
        dx, dw_rows = _rms_bwd(dh * (1.0 + sc_ref[...]), xh, rstd, w)
        _acc_out(dw_ref, jnp.logical_and(b == 0, i == 0), _colsum(dw_rows))
        gx_ref[...] = dx1_ref[...] + dx

    row_shape = _sds((bsz, 1, D_MODEL), F32)
    return pl.pallas_call(
        body, name="norm1_bwd", grid=(bsz, seq // tm),
        in_specs=[_tok_spec(tm), _tok_spec(tm), _tok_spec(tm), _vec_spec(), _row_spec()],
        out_specs=[_tok_spec(tm), _row_spec(), _row_spec(), _vec_spec()],
        out_shape=[_sds(x.shape, F32), row_shape, row_shape, _sds((1, D_MODEL), F32)],
        compiler_params=_params("arbitrary", "arbitrary"),
    )(dh1, dx1, x, pre_w, sc1)


def _hgrn_bwd(dcat, proj, o_raw, states, lb_table, norm_w, comms=()):
    bsz, seq, _ = proj.shape
    nstep = seq // HG_TOK
    rec0 = ATT_WIDTH // LANE

    def body(tbl_ref, nw_ref, dr_ref, q_ref, f_ref, i_ref, g_ref, o_ref, st_ref,
             dq_ref, df_ref, di_ref, dg_ref, dlb_ref, dnw_ref, ds_scr):
        h, b, t = pl.program_id(0), pl.program_id(1), pl.program_id(2)

        @pl.when(t == 0)
        def _():
            ds_scr[...] = jnp.zeros_like(ds_scr)

        lower, upper = _chunk_consts()
        tri_lo, tri_up = lower.astype(F32), upper.astype(F32)
        last_row = lax.broadcasted_iota(jnp.int32, (HG_CHUNK, LANE), 0) == HG_CHUNK - 1
        nw = nw_ref[...]
        dlb_acc = jnp.zeros((1, LANE), F32)
        dnw_acc = jnp.zeros((1, LANE), F32)
        for j in reversed(range(HG_NCH)):
            sl = slice(j * HG_CHUNK, (j + 1) * HG_CHUNK)
            hq, v, hg = q_ref[sl, :], i_ref[sl, :], g_ref[sl, :]
            lb, sig, f, sq = _hgrn_gates(tbl_ref[...], f_ref[sl, :], hq)
            q, k = hq * sq, 1.0 - f
            b_cum = _cumsum_rows(tri_lo, jnp.log(f))
            bl = b_cum[HG_CHUNK - 1:HG_CHUNK, :]
            e_b, e_nb, e_bl, e_rem = jnp.exp(b_cum), jnp.exp(-b_cum), jnp.exp(bl), jnp.exp(bl - b_cum)
            qd, kd, ku = q * e_b, k * e_nb, k * e_rem
            st = st_ref[j]
            y, on, rstd = _rms_fwd(o_ref[sl, :], nw)
            sg = _sigmoid(hg)
            dr = dr_ref[sl, :]
            dg_ref[sl, :] = (dr * y * (sg * (1.0 + hg * (1.0 - sg)))).astype(BF16)
            do, dnw_rows = _rms_bwd(dr * (hg * sg), on, rstd, nw)
            dnw_acc += _colsum(dnw_rows)
            dsp = ds_scr[...]
            at = jnp.where(upper, _dot(kd, qd, NT_DIMS), 0.0)
            da = jnp.where(lower, _dot(do, v, NT_DIMS), 0.0)
            dat = jnp.where(upper, _dot(v, do, NT_DIMS), 0.0)
            dv = _dot(at, do) + _dot(ku, dsp, NT_DIMS)
            dqd = _dot(da, kd) + _dot(do, st)
            dkd = _dot(dat, qd)
            dku = _dot(v, dsp)
            dbl = _colsum(st * dsp) * e_bl + _colsum(dku * ku)
            ds_scr[...] = _dot(do, qd, TN_DIMS) + dsp * e_bl
            dk = dkd * e_nb + dku * e_rem
            db = dqd * qd - dkd * kd - dku * ku
            db = db + jnp.where(last_row, dbl, 0.0)
            dlogf = _cumsum_rows(tri_up, db)
            dfv = dlogf / f - dk
            df_ref[sl, :] = (dfv * (1.0 - lb) * sig * (1.0 - sig)).astype(BF16)
            dlb_acc += _colsum(dfv * (1.0 - sig))
            dq_ref[sl, :] = (dqd * e_b * (sq * (1.0 + hq * (1.0 - sq)))).astype(BF16)
            di_ref[sl, :] = dv.astype(BF16)
        _acc_out(dlb_ref, jnp.logical_and(b == 0, t == 0), dlb_acc)
        _acc_out(dnw_ref, jnp.logical_and(h == 0, jnp.logical_and(b == 0, t == 0)), dnw_acc)

    rev = lambda t: nstep - 1 - t
    slab = lambda first: pl.BlockSpec((None, HG_TOK, LANE), lambda h, b, t: (b, rev(t), first + h))
    head = pl.BlockSpec((None, HG_TOK, LANE), lambda h, b, t: (b, rev(t), h))
    grad_shape = _sds((bsz, seq, HG_WIDTH), BF16)
    return _call(
        body, (lb_table, norm_w, dcat, proj, proj, proj, proj, o_raw, states), name="hgrn_bwd", grid=(HG_HEADS, bsz, nstep),
        in_specs=[pl.BlockSpec((2, LANE), lambda h, b, t: (0, h)), pl.BlockSpec((1, LANE), lambda h, b, t: (0, 0)),
                  slab(rec0), slab(HG_Q0), slab(HG_F0), slab(HG_I0), slab(HG_G0), head,
                  pl.BlockSpec((None, None, HG_NCH, LANE, LANE), lambda h, b, t: (b, h, rev(t), 0, 0))],
        out_specs=[head, head, head, head, pl.BlockSpec((1, LANE), lambda h, b, t: (0, h)),
                   pl.BlockSpec((1, LANE), lambda h, b, t: (0, 0))],
        out_shape=[grad_shape, grad_shape, grad_shape, grad_shape, _sds((1, HG_WIDTH), F32), _sds((1, LANE), F32)],
        scratch_shapes=[pltpu.VMEM((LANE, LANE), F32)],
        sem=("arbitrary", "arbitrary", "arbitrary"), comms=comms)


def _attn_bwd(dcat, raw, w_norm, qr, kr, vb, lse, sinks, comms=()):
    bsz, seq, _ = qr.shape
    nblk = seq // WINDOW

    def body(sink_ref, da_ref, raw_ref, w_ref, q_ref, kc_ref, kp_ref, vc_ref, vp_ref, l_ref,
             dq_ref, dkd_ref, dkp_ref, dvd_ref, dvp_ref, dw_ref, dsink_ref):
        b, i = pl.program_id(0), pl.program_id(1)
        first = jnp.logical_and(b == 0, i == 0)
        mask_c, mask_p = _band_masks(i)
        raw_v = raw_ref[...]
        w = w_ref[...]
        _, on, rstd = _rms_fwd(raw_v, w)
        do_all, dw_rows = _rms_bwd(da_ref[...], on, rstd, w)
        _acc_out(dw_ref, first, _colsum(dw_rows))
        lane8 = lax.broadcasted_iota(jnp.int32, (1, ATT_Q_HEADS), 1)
        dsink = jnp.zeros((1, ATT_Q_HEADS), F32)
        for g in range(ATT_KV_HEADS):
            gs = slice(g * ATT_HEAD_DIM, (g + 1) * ATT_HEAD_DIM)
            kc, kp, vc, vp = kc_ref[:, gs], kp_ref[:, gs], vc_ref[:, gs], vp_ref[:, gs]
            dkc = jnp.zeros((WINDOW, ATT_HEAD_DIM), F32)
            dkp, dvc, dvp = dkc, dkc, dkc
            for hh in range(ATT_GROUP):
                h = g * ATT_GROUP + hh
                hs = slice(h * ATT_HEAD_DIM, (h + 1) * ATT_HEAD_DIM)
                q = q_ref[:, hs]
                doh = do_all[:, hs]
                dsum = jnp.sum(doh * raw_v[:, hs], axis=-1, keepdims=True)
                lse_h = l_ref[:, h:h + 1]
                pc = jnp.where(mask_c, jnp.exp(_dot(q, kc, NT_DIMS) * ATT_SCALE - lse_h), 0.0)
                pp = jnp.where(mask_p, jnp.exp(_dot(q, kp, NT_DIMS) * ATT_SCALE - lse_h), 0.0)
                p_sink = jnp.exp(sink_ref[0, h] - lse_h)
                dsink = dsink - jnp.where(lane8 == h, jnp.sum(p_sink * dsum, axis=0, keepdims=True), 0.0)
                dsc = pc * (_dot(doh, vc, NT_DIMS) - dsum) * ATT_SCALE
                dsp = pp * (_dot(doh, vp, NT_DIMS) - dsum) * ATT_SCALE
                dq_ref[:, hs] = _dot(dsc, kc) + _dot(dsp, kp)
                dkc += _dot(dsc, q, TN_DIMS)
                dkp += _dot(dsp, q, TN_DIMS)
                dvc += _dot(pc, doh, TN_DIMS)
                dvp += _dot(pp, doh, TN_DIMS)
            dkd_ref[:, gs], dkp_ref[:, gs], dvd_ref[:, gs], dvp_ref[:, gs] = dkc, dkp, dvc, dvp
        _acc_out(dsink_ref, first, dsink)

    cur = lambda width: pl.BlockSpec((None, WINDOW, width), lambda b, i: (b, i, 0))
    kv_shape = _sds((bsz, seq, LANE), F32)
    return _call(
        body, (sinks, dcat, raw, w_norm, qr, kr, kr, vb, vb, lse), name="attn_bwd", grid=(bsz, nblk),
        in_specs=[pl.BlockSpec(memory_space=pltpu.SMEM), cur(ATT_WIDTH), cur(ATT_WIDTH), _vec_spec(ATT_WIDTH), cur(ATT_WIDTH),
                  cur(LANE), _prev_spec(LANE), cur(LANE), _prev_spec(LANE), cur(ATT_Q_HEADS)],
        out_specs=[cur(ATT_WIDTH), cur(LANE), cur(LANE), cur(LANE), cur(LANE), _vec_spec(ATT_WIDTH), _vec_spec(ATT_Q_HEADS)],
        out_shape=[_sds((bsz, seq, ATT_WIDTH), F32), kv_shape, kv_shape, kv_shape, kv_shape, _sds((1, ATT_WIDTH), F32),
                   _sds((1, ATT_Q_HEADS), F32)],
        sem=("arbitrary", "arbitrary"), comms=comms)


def _rope_bwd(dq, dkd, dkp, dvd, dvp, tables):
    bsz, seq, _ = dq.shape
    nblk = seq // WINDOW
    half = ROPE_DIM // 2

    def body(dq_ref, dkd_ref, dkp_ref, dvd_ref, dvp_ref, c_ref, u_ref, d_ref, o_ref):
        c, u, d = c_ref[...], u_ref[...], d_ref[...]
        has_next = pl.program_id(1) < nblk - 1

        def unrope(g):
            return g * c + pltpu.roll(g * u, LANE - half, 1) + pltpu.roll(g * d, half, 1)

        for s in range(ATT_WIDTH // LANE):
            o_ref[:, s * LANE:(s + 1) * LANE] = unrope(dq_ref[:, s * LANE:(s + 1) * LANE]).astype(BF16)
        dk = dkd_ref[...] + jnp.where(has_next, dkp_ref[...], 0.0)
        o_ref[:, ATT_WIDTH:ATT_WIDTH + LANE] = unrope(dk).astype(BF16)
        o_ref[:, ATT_WIDTH + LANE:ATT_COLS] = (dvd_ref[...] + jnp.where(has_next, dvp_ref[...], 0.0)).astype(BF16)

    cur = lambda width: pl.BlockSpec((None, WINDOW, width), lambda b, i: (b, i, 0))
    nxt = pl.BlockSpec((None, WINDOW, LANE), lambda b, i: (b, jnp.minimum(i + 1, nblk - 1), 0))
    tab = pl.BlockSpec((WINDOW, LANE), lambda b, i: (i, 0))
    return pl.pallas_call(
        body, name="rope_bwd", grid=(bsz, nblk),
        in_specs=[cur(ATT_WIDTH), cur(LANE), nxt, cur(LANE), nxt, tab, tab, tab],
        out_specs=cur(ATT_COLS), out_shape=_sds((bsz, seq, ATT_COLS), BF16),
        compiler_params=_params("parallel", "parallel"),
    )(dq, dkd, dkp, dvd, dvp, *tables)


def _other_chips(x, y):
    return [(1 - x, y), (x, 1 - y), (1 - x, 1 - y)]


def _sem_pair(n):
    return [pltpu.SemaphoreType.DMA((n,)), pltpu.SemaphoreType.DMA((n,))]


def _plan_chip_gather(blocks, bufs):
    n = len(blocks)

    def copies(ins, outs, sems):
        x, y, c = _mesh_pos()
        sends, lands = [], []
        for a in range(n):
            for j, chip in enumerate(_other_chips(x, y)):
                k = 3 * a + j
                sends.append(pltpu.make_async_remote_copy(
                    src_ref=ins[a], dst_ref=outs[a].at[4 * x + 2 * y + c], send_sem=sems[0].at[k], recv_sem=sems[1].at[k],
                    device_id=(*chip, c), device_id_type=MESH))
                slot = outs[a].at[4 * chip[0] + 2 * chip[1] + c]
                lands.append(pltpu.make_async_remote_copy(
                    src_ref=slot, dst_ref=slot, send_sem=sems[0].at[k], recv_sem=sems[1].at[k],
                    device_id=(*chip, c), device_id_type=MESH))
        return sends, lands

    def start(ins, outs, sems):
        for cp in copies(ins, outs, sems)[0]:
            cp.start()

    def finish(ins, outs, sems):
        sends, lands = copies(ins, outs, sems)
        for cp in lands:
            cp.wait_recv()
        for cp in sends:
            cp.wait_send()

    return _Comm(list(blocks) + list(bufs), [_sds(b.shape, b.dtype) for b in bufs], _sem_pair(3 * n), start, finish,
                 aliases=[(n + a, a) for a in range(n)])


def _plan_pair_forward(bufs):
    n = len(bufs)

    def copies(outs, sems):
        x, y, c = _mesh_pos()
        sends, lands = [], []
        for a in range(n):
            for j, chip in enumerate(_other_chips(x, y)):
                k = 3 * a + j
                slot = outs[a].at[4 * chip[0] + 2 * chip[1] + c]
                sends.append(pltpu.make_async_remote_copy(
                    src_ref=slot, dst_ref=slot, send_sem=sems[0].at[k], recv_sem=sems[1].at[k],
                    device_id=(x, y, 1 - c), device_id_type=MESH))
                theirs = outs[a].at[4 * chip[0] + 2 * chip[1] + 1 - c]
                lands.append(pltpu.make_async_remote_copy(
                    src_ref=theirs, dst_ref=theirs, send_sem=sems[0].at[k], recv_sem=sems[1].at[k],
                    device_id=(x, y, 1 - c), device_id_type=MESH))
        return sends, lands

    def start(ins, outs, sems):
        for cp in copies(outs, sems)[0]:
            cp.start()

    def finish(ins, outs, sems):
        sends, lands = copies(outs, sems)
        for cp in lands:
            cp.wait_recv()
        for cp in sends:
            cp.wait_send()

    return _Comm(list(bufs), [_sds(b.shape, b.dtype) for b in bufs], _sem_pair(3 * n), start, finish,
                 aliases=[(a, a) for a in range(n)])


def _plan_pair(arrays, other_half):
    n = len(arrays)

    def copies(ins, outs, sems):
        x, y, c = _mesh_pos()
        return [pltpu.make_async_remote_copy(
            src_ref=ins[a].at[1 - c] if other_half else ins[a], dst_ref=outs[a], send_sem=sems[0].at[a], recv_sem=sems[1].at[a],
            device_id=(x, y, 1 - c), device_id_type=MESH) for a in range(n)]

    def start(ins, outs, sems):
        for cp in copies(ins, outs, sems):
            cp.start()

    def finish(ins, outs, sems):
        for cp in copies(ins, outs, sems):
            cp.wait()

    shapes = [_sds(a.shape[1:] if other_half else a.shape, a.dtype) for a in arrays]
    return _Comm(list(arrays), shapes, _sem_pair(n), start, finish)


def _plan_chip_exchange(arrays):
    n = len(arrays)

    def copies(ins, outs, sems):
        x, y, c = _mesh_pos()
        sends, lands = [], []
        for a in range(n):
            for j, chip in enumerate(_other_chips(x, y)):
                k = 3 * a + j
                sends.append(pltpu.make_async_remote_copy(
                    src_ref=ins[a].at[2 * chip[0] + chip[1]], dst_ref=outs[a].at[2 * x + y], send_sem=sems[0].at[k],
                    recv_sem=sems[1].at[k], device_id=(*chip, c), device_id_type=MESH))
                slot = outs[a].at[2 * chip[0] + chip[1]]
                lands.append(pltpu.make_async_remote_copy(
                    src_ref=slot, dst_ref=slot, send_sem=sems[0].at[k], recv_sem=sems[1].at[k],
                    device_id=(*chip, c), device_id_type=MESH))
        return sends, lands

    def start(ins, outs, sems):
        for cp in copies(ins, outs, sems)[0]:
            cp.start()

    def finish(ins, outs, sems):
        sends, lands = copies(ins, outs, sems)
        for cp in lands:
            cp.wait_recv()
        for cp in sends:
            cp.wait_send()

    return _Comm(list(arrays), [_sds(a.shape, a.dtype) for a in arrays], _sem_pair(3 * n), start, finish)


def _comm_only(comms, name):
    return _call(lambda: None, (), name=name, grid=(), in_specs=[], out_specs=[], out_shape=[], sem=(), comms=comms)[1]


def _allgather8(arrays, name):
    return _comm_only([_plan_allgather8(arrays)], name)[0]


def _plan_allgather8(arrays):
    n = len(arrays)

    def parts(ins, outs, sems):
        send_sems, recv_sems, local_sems = sems
        x, y, c = _mesh_pos()
        me, sibling = (x, y, c), (x, y, 1 - c)
        chips = _other_chips(x, y)

        def copy(a, k, block, to, src=None):
            dst = outs[a].at[4 * block[0] + 2 * block[1] + block[2]]
            return pltpu.make_async_remote_copy(
                src_ref=dst if src is None else src, dst_ref=dst, send_sem=send_sems.at[7 * a + k],
                recv_sem=recv_sems.at[7 * a + k], device_id=to, device_id_type=MESH)

        mine = [pltpu.make_async_copy(ins[a], outs[a].at[4 * x + 2 * y + c], local_sems.at[a]) for a in range(n)]
        first = []
        for a in range(n):
            first.append(copy(a, 0, me, sibling, src=ins[a]))
            first += [copy(a, 1 + j, me, (*chip, c), src=ins[a]) for j, chip in enumerate(chips)]
        return copy, mine, first, me, sibling, chips, c

    def start(ins, outs, sems):
        _, mine, first, *_ = parts(ins, outs, sems)
        for cp in mine + first:
            cp.start()

    def finish(ins, outs, sems):
        copy, mine, first, me, sibling, chips, c = parts(ins, outs, sems)
        passed = []
        for j, chip in enumerate(chips):
            for a in range(n):
                copy(a, 1 + j, (*chip, c), me).wait_recv()
                fwd = copy(a, 4 + j, (*chip, c), sibling)
                fwd.start()
                passed.append(fwd)
        for a in range(n):
            copy(a, 0, sibling, me).wait_recv()
            for j, chip in enumerate(chips):
                copy(a, 4 + j, (*chip, 1 - c), me).wait_recv()
        for cp in first + passed:
            cp.wait_send()
        for cp in mine:
            cp.wait()

    sems = [pltpu.SemaphoreType.DMA((7 * n,)), pltpu.SemaphoreType.DMA((7 * n,)), pltpu.SemaphoreType.DMA((n,))]
    return _Comm(list(arrays), [_sds((N_DEV,) + a.shape, a.dtype) for a in arrays], sems, start, finish)


def _pair_sum(g, q, core, name):
    _, nblk, rows, cols = g.shape
    tr = min(rows, 256)

    def body(core_ref, g_ref, q_ref, o_ref):
        o_ref[...] = (g_ref[...] + q_ref[...]).astype(BF16)

    blk = pl.BlockSpec((None, tr, cols), lambda k, i, core_ref: (k, i, 0))
    return pl.pallas_call(
        body, name=name,
        grid_spec=pltpu.PrefetchScalarGridSpec(
            num_scalar_prefetch=1, grid=(nblk, rows // tr),
            in_specs=[pl.BlockSpec((None, None, tr, cols), lambda k, i, core_ref: (core_ref[0], k, i, 0)), blk],
            out_specs=blk),
        out_shape=_sds((nblk, rows, cols), BF16), compiler_params=_params("parallel", "parallel"),
    )(core, g, q)


def _sum_chips(own, landed, chip, name):
    _, rows, cols = own.shape
    tr = min(rows, 256)

    def body(chip_ref, own_ref, a_ref, b_ref, c_ref, o_ref):
        acc = own_ref[...].astype(F32) + a_ref[...].astype(F32)
        o_ref[...] = (acc + b_ref[...].astype(F32)) + c_ref[...].astype(F32)

    blk = lambda flip: pl.BlockSpec((None, tr, cols), lambda i, chip_ref: (jnp.bitwise_xor(chip_ref[0], flip), i, 0))
    return pl.pallas_call(
        body, name=name,
        grid_spec=pltpu.PrefetchScalarGridSpec(num_scalar_prefetch=1, grid=(rows // tr,), in_specs=[blk(0), blk(1), blk(2), blk(3)],
                                               out_specs=pl.BlockSpec((tr, cols), lambda i, chip_ref: (i, 0))),
        out_shape=_sds((rows, cols), F32), compiler_params=_params("parallel"),
    )(chip, own, landed, landed, landed)


SMALL_ROWS = 120
PACK_ROWS = 160


def _rows(a, nrows):
    flat = a.reshape(-1)
    return jnp.pad(flat, (0, nrows * LANE - flat.shape[0])).reshape(nrows, LANE)


def _pack_small(b_ada, pre_w_mix, post_w_mix, pre_w_mlp, post_w_mlp, attn_out_w, hg_norm_w, attn_sinks, lb_table):
    return jnp.concatenate([
        _rows(b_ada, 48), _rows(pre_w_mix, 8), _rows(post_w_mix, 8), _rows(pre_w_mlp, 8), _rows(post_w_mlp, 8),
        _rows(attn_out_w, 8), _rows(hg_norm_w, 8), _rows(attn_sinks, 8), _rows(lb_table[0], 8), _rows(lb_table[1], 8)], axis=0)


def _unpack_small(p):
    vec = lambda lo, n: p[lo:lo + n // LANE].reshape(1, n)
    lb = jnp.stack([p[104:108].reshape(HG_WIDTH), p[112:116].reshape(HG_WIDTH)])
    return dict(b_ada=vec(0, N_MOD * D_MODEL), pre_w_mix=vec(48, D_MODEL), post_w_mix=vec(56, D_MODEL), pre_w_mlp=vec(64, D_MODEL),
                post_w_mlp=vec(72, D_MODEL), attn_out_w=vec(80, ATT_WIDTH), hg_norm_w=p[88:89], attn_sinks=p[96:97, :ATT_Q_HEADS],
                lb_table=lb)


def _small_update(packs, w, m, v):
    def body(p_ref, w_ref, m_ref, v_ref, g_ref, dl_ref, nm_ref, nv_ref):
        tot = p_ref[0]
        for d in range(1, N_DEV):
            tot = tot + p_ref[d]
        wv = w_ref[...]
        p1 = _sigmoid(wv[112:120] - wv[104:112])
        s = tot[152:160] * p1 * (1.0 - p1)
        g = jnp.concatenate([tot[0:48] + tot[48:96], tot[96:152], -s, s], axis=0)
        g_ref[...] = g
        dl_ref[...], nm_ref[...], nv_ref[...] = _adamw_math(g, wv, m_ref[...], v_ref[...])

    shp = _sds((SMALL_ROWS, LANE), F32)
    return pl.pallas_call(body, name="small_update", out_shape=[shp] * 4, compiler_params=_params())(packs, w, m, v)


def kernel(x, c, w_ada, b_ada, pre_w_mix, w_in, attn_sinks, attn_out_w, lb_table, hg_norm_w, w_out, post_w_mix, pre_w_mlp, w_up, w_down, post_w_mlp, loss_target, m_w_ada, m_b_ada, m_pre_w_mix, m_w_in, m_attn_sinks, m_attn_out_w, m_lb_table, m_hg_norm_w, m_w_out, m_post_w_mix, m_pre_w_mlp, m_w_up, m_w_down, m_post_w_mlp, v_w_ada, v_b_ada, v_pre_w_mix, v_w_in, v_attn_sinks, v_attn_out_w, v_lb_table, v_hg_norm_w, v_w_out, v_post_w_mix, v_pre_w_mlp, v_w_up, v_w_down, v_post_w_mlp):
    xi, yi, ci = _mesh_pos()
    chip = 2 * xi + yi
    dev = 2 * chip + ci
    bsz, seq, _ = x.shape
    ntok = bsz * seq
    ada_cols = w_ada.shape[2]
    core = jnp.reshape(ci, (1,)).astype(jnp.int32)
    chip_idx = jnp.reshape(chip, (1,)).astype(jnp.int32)
    flat = lambda a: a.reshape(ntok, a.shape[-1])
    unflat = lambda a: a.reshape(bsz, seq, a.shape[-1])
    tables = _rope_tables(seq)

    def row_half(w):
        rows = w.shape[1] // 2
        return lax.dynamic_slice_in_dim(w[0], ci * rows, rows, axis=0).astype(BF16)

    def gather_buffer(w):
        rows, cols = w.shape[1] // 2, w.shape[2]
        own = w[0].astype(BF16).reshape(2, rows, cols)
        return lax.dynamic_update_slice(jnp.zeros((N_DEV, rows, cols), BF16), own, (2 * chip, 0, 0))

    c_g, in_g = _allgather8([c, row_half(w_in)], "gather_first")
    c_all = c_g.reshape(N_DEV * bsz, D_MODEL)
    w_in_full = in_g.reshape(N_CHIPS, D_MODEL, IN_COLS // N_CHIPS).transpose(1, 0, 2).reshape(D_MODEL, IN_COLS)

    b_cols = lax.dynamic_slice_in_dim(b_ada, chip * ada_cols, ada_cols, axis=1)
    mod_part = _ada_fwd(c_all, w_ada[0], b_cols)
    half_rows = mod_part.shape[0] // 2
    (mod_g,) = _allgather8([lax.dynamic_slice_in_dim(mod_part, ci * half_rows, half_rows, axis=0)], "gather_mod")
    mod_all = mod_g.reshape(N_CHIPS, 2, half_rows, ada_cols).transpose(1, 2, 0, 3).reshape(N_DEV * bsz, N_MOD * D_MODEL)
    mod = lax.dynamic_slice_in_dim(mod_all, dev * bsz, bsz, axis=0)
    sh1, sc1, g1, sh2, sc2, g2 = [mod[:, i * D_MODEL:(i + 1) * D_MODEL].reshape(bsz, 1, D_MODEL) for i in range(N_MOD)]

    h1 = _prenorm1(x, pre_w_mix, sc1, sh1)
    proj = unflat(_mm(flat(h1), w_in_full, name="in_proj", out_dtype=F32))
    qr, kr, vb = _rope_fwd(proj, tables)
    blocks = [row_half(w_out), row_half(w_up), row_half(w_down)]
    bufs = [gather_buffer(w_out), gather_buffer(w_up), gather_buffer(w_down)]
    (attn_raw, attn_n, lse), (bufs,) = _attn_fwd(qr, kr, vb, attn_sinks, attn_out_w, comms=[_plan_chip_gather(blocks, bufs)])
    (o_raw, rec, states), (bufs,) = _hgrn_fwd(proj, lb_table, hg_norm_w, comms=[_plan_pair_forward(bufs)])
    w_out_full = bufs[0].reshape(D_MODEL, D_MODEL)
    w_up4 = bufs[1].reshape(N_CHIPS, D_MODEL, D_MODEL)
    w_down_full = bufs[2].reshape(D_FF, D_MODEL)
    cat = jnp.concatenate([attn_n, rec], axis=-1)
    mix = unflat(_mm(flat(cat), w_out_full, name="out_proj", out_dtype=F32))
    x1, h2 = _mid_fwd(x, mix, post_w_mix, g1, pre_w_mlp, sc2, sh2)
    big_tm = min(ntok, 1024)
    up_spec = pl.BlockSpec((None, D_MODEL, D_MODEL), lambda i, j: (j, 0, 0))
    r = _mm(flat(h2), w_up4, name="up_proj", out_dtype=BF16, tm=big_tm, tn=D_MODEL, n_out=D_FF, b_spec=up_spec,
            epi=lambda acc: jnp.maximum(acc, 0.0))
    square = lambda t: t * t
    down = unflat(_mm(r, w_down_full, name="down_proj", out_dtype=F32, a_fn=square))
    loss_row, dy, dd, dg2, d_post_mlp = _loss_bwd(x1, down, post_w_mlp, g2, loss_target)
    loss = lax.psum(loss_row[0, 0], ("x", "y", "c"))

    dpre = _mm(flat(dd), w_down_full, name="down_bwd", out_dtype=BF16, trans_b=True, tm=big_tm, tn=D_MODEL, extra=(r,),
               epi=lambda acc, rt: acc * (2.0 * rt.astype(F32)))
    half_rows = D_MODEL // 2
    g_down = _mm_tn(r, flat(dd), name="down_wgrad", tk=half_rows, tn=D_MODEL, a_fn=square,
                    out_shape=_sds((2, N_CHIPS, half_rows, D_MODEL), F32),
                    out_spec=pl.BlockSpec((None, None, half_rows, D_MODEL), lambda i, j: (i % 2, i // 2, 0, 0)))
    dh2, ((q_down,),) = _mm(dpre, w_up4, name="up_bwd", out_dtype=F32, trans_b=True, n_out=D_MODEL, b_chunks=N_CHIPS,
                            comms=[_plan_pair([g_down], True)])
    g_up = _mm_tn(flat(h2), dpre, name="up_wgrad", tk=D_MODEL, tn=half_rows,
                  out_shape=_sds((2, N_CHIPS, half_rows, D_MODEL), F32),
                  out_spec=pl.BlockSpec((2, None, half_rows, half_rows), lambda i, j: (0, j // 2, 0, j % 2)))
    s_down = _pair_sum(g_down, q_down, core, "pair_sum_down")
    (dx1, dmix, dsc2, dsh2, dg1, d_pre_mlp, d_post_mix), ((q_up,),) = _mid_bwd(
        unflat(dh2), dy, x1, mix, pre_w_mlp, sc2, post_w_mix, g1, comms=[_plan_pair([g_up], True)])
    s_up = _pair_sum(g_up, q_up, core, "pair_sum_up")

    dcat = unflat(_mm(flat(dmix), w_out_full, name="out_bwd", out_dtype=F32, trans_b=True))
    out_rows = D_MODEL // N_CHIPS
    g_out = _mm_tn(flat(cat), flat(dmix), name="out_wgrad", tk=out_rows, tn=half_rows,
                   out_shape=_sds((2, N_CHIPS, out_rows, half_rows), F32),
                   out_spec=pl.BlockSpec((None, None, out_rows, half_rows), lambda i, j: (j, i, 0, 0)))
    (dhq, dhf, dhi, dhg, d_lb, d_hg_norm), ((x_down, x_up), (q_out,)) = _hgrn_bwd(
        dcat, proj, o_raw, states, lb_table, hg_norm_w, comms=[_plan_chip_exchange([s_down, s_up]), _plan_pair([g_out], True)])
    half_down = _sum_chips(s_down, x_down, chip_idx, "sum_chips_down")
    half_up = _sum_chips(s_up, x_up, chip_idx, "sum_chips_up")
    s_out = _pair_sum(g_out, q_out, core, "pair_sum_out")
    (dq, dkd, dkp, dvd, dvp, d_attn_out, d_sinks), ((their_down, their_up), (x_out,)) = _attn_bwd(
        dcat, attn_raw, attn_out_w, qr, kr, vb, lse, attn_sinks,
        comms=[_plan_pair([half_down, half_up], False), _plan_chip_exchange([s_out])])
    half_out = _sum_chips(s_out, x_out, chip_idx, "sum_chips_out")
    dproj_a = _rope_bwd(dq, dkd, dkp, dvd, dvp, tables)
    dproj = flat(jnp.concatenate([dproj_a, dhq, dhf, dhi, dhg], axis=-1))
    g_in = _mm_tn(flat(h1), dproj, name="in_wgrad", tk=D_MODEL, tn=2 * LANE)
    g_in = g_in.reshape(2, half_rows, N_CHIPS, IN_COLS // N_CHIPS).transpose(0, 2, 1, 3)
    dh1, ((q_in,), (their_out,)) = _mm(dproj, w_in_full, name="in_bwd", out_dtype=F32, trans_b=True,
                                       comms=[_plan_pair([g_in], True), _plan_pair([half_out], False)])
    s_in = _pair_sum(g_in, q_in, core, "pair_sum_in")
    grad_x, dsc1, dsh1, d_pre_mix = _norm1_bwd(unflat(dh1), dx1, x, pre_w_mix, sc1)

    dmod = jnp.concatenate([dsh1, dsc1, dg1, dsh2, dsc2, dg2], axis=-1).reshape(bsz, N_MOD * D_MODEL)
    pack = jnp.concatenate([
        _rows(dmod, 96), _rows(d_pre_mix, 8), _rows(d_post_mix, 8), _rows(d_pre_mlp, 8), _rows(d_post_mlp, 8),
        _rows(d_attn_out, 8), _rows(d_hg_norm, 8), _rows(d_sinks, 8), _rows(d_lb, 8)], axis=0)
    (packs,), (x_in,) = _comm_only([_plan_allgather8([pack]), _plan_chip_exchange([s_in])], "gather_small")
    half_in = _sum_chips(s_in, x_in, chip_idx, "sum_chips_in")
    ((their_in,),) = _comm_only([_plan_pair([half_in], False)], "pair_swap_in")
    small_args = lambda pre: (pre["b_ada"], pre["pre_w_mix"], pre["post_w_mix"], pre["pre_w_mlp"], pre["post_w_mlp"],
                              pre["attn_out_w"], pre["hg_norm_w"], pre["attn_sinks"], pre["lb_table"])
    w_small = dict(b_ada=b_ada, pre_w_mix=pre_w_mix, post_w_mix=post_w_mix, pre_w_mlp=pre_w_mlp, post_w_mlp=post_w_mlp,
                   attn_out_w=attn_out_w, hg_norm_w=hg_norm_w, attn_sinks=attn_sinks, lb_table=lb_table)
    m_small = dict(b_ada=m_b_ada, pre_w_mix=m_pre_w_mix, post_w_mix=m_post_w_mix, pre_w_mlp=m_pre_w_mlp, post_w_mlp=m_post_w_mlp,
                   attn_out_w=m_attn_out_w, hg_norm_w=m_hg_norm_w, attn_sinks=m_attn_sinks, lb_table=m_lb_table)
    v_small = dict(b_ada=v_b_ada, pre_w_mix=v_pre_w_mix, post_w_mix=v_post_w_mix, pre_w_mlp=v_pre_w_mlp, post_w_mlp=v_post_w_mlp,
                   attn_out_w=v_attn_out_w, hg_norm_w=v_hg_norm_w, attn_sinks=v_attn_sinks, lb_table=v_lb_table)
    small_out = [_unpack_small(p) for p in _small_update(packs, _pack_small(*small_args(w_small)), _pack_small(*small_args(m_small)),
                                                         _pack_small(*small_args(v_small)))]

    dmod_all = packs[:, :96, :].reshape(N_DEV * bsz, N_MOD * D_MODEL)
    dmod_cols = lax.dynamic_slice_in_dim(dmod_all, chip * ada_cols, ada_cols, axis=1)
    ada_out = _ada_bwd_adamw(c_all, dmod_cols, w_ada[0], m_w_ada[0], v_w_ada[0])

    big = dict(
        w_in=tuple(_adamw_halves(half_in, their_in, core, w_in[0], m_w_in[0], v_w_in[0], axis=0, name="adamw_in")),
        w_up=tuple(_adamw_halves(half_up, their_up, core, w_up[0], m_w_up[0], v_w_up[0], axis=0, name="adamw_up")),
        w_out=tuple(_adamw_halves(half_out, their_out, core, w_out[0], m_w_out[0], v_w_out[0], axis=1, name="adamw_out")),
        w_down=tuple(_adamw_halves(half_down, their_down, core, w_down[0], m_w_down[0], v_w_down[0], axis=0, name="adamw_down")),
        w_ada=tuple(ada_out),
    )
    order = ("w_ada", "b_ada", "pre_w_mix", "w_in", "attn_sinks", "attn_out_w", "lb_table", "hg_norm_w", "w_out", "post_w_mix",
             "pre_w_mlp", "w_up", "w_down", "post_w_mlp")
    outs = [loss, grad_x]
    for kind in range(4):
        for nm in order:
            outs.append(big[nm][kind][None] if nm in big else small_out[kind][nm])
    return tuple(outs)
```

```python
import functools

import jax
import jax.numpy as jnp
from jax import lax
from jax.experimental import pallas as pl
from jax.experimental.pallas import tpu as pltpu

F32 = jnp.float32
BF16 = jnp.bfloat16

D_MODEL = 1024
ATT_WIDTH = 512
ATT_HEAD_DIM = 64
ATT_Q_HEADS = 8
ATT_KV_HEADS = 2
ATT_GROUP = ATT_Q_HEADS // ATT_KV_HEADS
ATT_KV_COLS = ATT_KV_HEADS * ATT_HEAD_DIM
WINDOW = 128
ROPE_DIM = 16
ROPE_THETA = 500000.0
HG_WIDTH = 512
HG_HEAD_DIM = 128
HG_HEADS = 4
HG_CHUNK = 32
IN_COLS = ATT_WIDTH + 2 * ATT_KV_COLS + 4 * HG_WIDTH
ATT_COLS = ATT_WIDTH + 2 * ATT_KV_COLS
D_FF = 4 * D_MODEL
N_MOD = 6
EPS = 1e-6
ATT_SCALE = ATT_HEAD_DIM ** -0.5

ADAM_LR = 0.001
ADAM_B1 = 0.9
ADAM_B2 = 0.999
ADAM_EPS = 1e-08
ADAM_WD = 0.01
ADAM_STEP = 10

N_CHIPS = 4
N_DEV = 8
LANE = 128
VMEM_LIMIT = 48 * 1024 * 1024
MESH = pl.DeviceIdType.MESH

NT_DIMS = (((1,), (1,)), ((), ()))
TN_DIMS = (((0,), (0,)), ((), ()))


def _sds(shape, dtype):
    return jax.ShapeDtypeStruct(tuple(shape), dtype)


def _params(*sem):
    return pltpu.CompilerParams(dimension_semantics=sem, vmem_limit_bytes=VMEM_LIMIT)


def _sigmoid(x):
    return 1.0 / (1.0 + jnp.exp(-x))


def _dot(a, b, dims=None):
    a, b = a.astype(BF16), b.astype(BF16)
    if dims is None:
        return jnp.dot(a, b, preferred_element_type=F32)
    return lax.dot_general(a, b, dims, preferred_element_type=F32)


def _rms_fwd(x, w):
    rstd = lax.rsqrt(jnp.mean(x * x, axis=-1, keepdims=True) + EPS)
    xh = x * rstd
    return xh * w, xh, rstd


def _rms_bwd(dy, xh, rstd, w):
    dxh = dy * w
    dx = rstd * (dxh - xh * jnp.mean(dxh * xh, axis=-1, keepdims=True))
    return dx, dy * xh


def _colsum(x):
    return jnp.sum(x, axis=0, keepdims=True)


HBM_SPEC = pl.BlockSpec(memory_space=pltpu.HBM)


def _mesh_pos():
    return lax.axis_index("x"), lax.axis_index("y"), lax.axis_index("c")


class _Comm:
    def __init__(self, ins, outs, sems, start, finish, aliases=()):
        self.ins, self.outs, self.sems = list(ins), list(outs), list(sems)
        self.start, self.finish, self.aliases = start, finish, tuple(aliases)


def _call(body, args, *, name, grid, in_specs, out_specs, out_shape, sem, scratch_shapes=(), comms=()):
    scratch_shapes = list(scratch_shapes)
    if not comms:
        return pl.pallas_call(body, name=name, grid=grid, in_specs=in_specs, out_specs=out_specs, out_shape=out_shape,
                              scratch_shapes=scratch_shapes, compiler_params=_params(*sem))(*args)
    single = not isinstance(out_shape, (list, tuple))
    out_specs_l = [out_specs] if single else list(out_specs)
    out_shape_l = [out_shape] if single else list(out_shape)
    n_in, n_out, n_scr = len(in_specs), len(out_shape_l), len(scratch_shapes)
    n_ci = [len(cm.ins) for cm in comms]
    n_co = [len(cm.outs) for cm in comms]
    n_cs = [len(cm.sems) for cm in comms]
    aliases = {}
    for k, cm in enumerate(comms):
        for i, o in cm.aliases:
            aliases[n_in + sum(n_ci[:k]) + i] = n_out + sum(n_co[:k]) + o

    def fused(*refs):
        pos = [0]

        def take(n):
            part = refs[pos[0]:pos[0] + n]
            pos[0] += n
            return part

        ins = take(n_in)
        c_ins = [take(n) for n in n_ci]
        outs = take(n_out)
        c_outs = [take(n) for n in n_co]
        scr = take(n_scr)
        c_sems = [take(n) for n in n_cs]
        first, last = True, True
        for d, size in enumerate(grid):
            first = jnp.logical_and(first, pl.program_id(d) == 0)
            last = jnp.logical_and(last, pl.program_id(d) == size - 1)

        def run(which):
            for cm, ci, co, cs in zip(comms, c_ins, c_outs, c_sems):
                getattr(cm, which)(ci, co, cs)

        if grid:
            pl.when(first)(lambda: run("start"))
        else:
            run("start")
        body(*ins, *outs, *scr)
        if grid:
            pl.when(last)(lambda: run("finish"))
        else:
            run("finish")

    res = pl.pallas_call(
        fused, name=name, grid=grid, in_specs=list(in_specs) + [HBM_SPEC] * sum(n_ci),
        out_specs=out_specs_l + [HBM_SPEC] * sum(n_co), out_shape=out_shape_l + [s for cm in comms for s in cm.outs],
        input_output_aliases=aliases, scratch_shapes=scratch_shapes + [s for cm in comms for s in cm.sems],
        compiler_params=_params(*["arbitrary"] * len(grid)),
    )(*args, *[a for cm in comms for a in cm.ins])
    main = res[:n_out]
    extra, at = [], n_out
    for n in n_co:
        extra.append(list(res[at:at + n]))
        at += n
    return (main[0] if single else list(main)), extra


def _mm(a, b, *, name, out_dtype, trans_b=False, tm=512, tn=None, a_fn=None, extra=(), epi=None,
        b_spec=None, n_out=None, b_chunks=1, comms=()):
    m_total, k_total = a.shape
    if n_out is None:
        n_out = b.shape[0] if trans_b else b.shape[1]
    tn = n_out if tn is None else tn
    grid = (m_total // tm, n_out // tn)
    dims = NT_DIMS if trans_b else None
    kc = k_total // b_chunks

    def body(*refs):
        a_ref, b_ref = refs[0], refs[1]
        extra_refs = refs[2:2 + len(extra)]
        o_ref = refs[2 + len(extra)]
        if b_chunks == 1:
            av = a_ref[...]
            acc = _dot(av if a_fn is None else a_fn(av), b_ref[...], dims)
        else:
            acc = _dot(a_ref[:, 0:kc], b_ref[0], NT_DIMS)
            for k in range(1, b_chunks):
                acc = acc + _dot(a_ref[:, k * kc:(k + 1) * kc], b_ref[k], NT_DIMS)
        if epi is not None:
            acc = epi(acc, *[r[...] for r in extra_refs])
        o_ref[...] = acc.astype(out_dtype)

    if b_spec is None:
        if b_chunks > 1:
            b_spec = pl.BlockSpec((b_chunks, tn, kc), lambda i, j: (0, j, 0))
        elif trans_b:
            b_spec = pl.BlockSpec((tn, k_total), lambda i, j: (j, 0))
        else:
            b_spec = pl.BlockSpec((k_total, tn), lambda i, j: (0, j))
    in_specs = [pl.BlockSpec((tm, k_total), lambda i, j: (i, 0)), b_spec]
    in_specs += [pl.BlockSpec((tm, tn), lambda i, j: (i, j)) for _ in extra]
    return _call(
        body, (a, b, *extra), name=name, grid=grid, in_specs=in_specs,
        out_specs=pl.BlockSpec((tm, tn), lambda i, j: (i, j)),
        out_shape=_sds((m_total, n_out), out_dtype),
        sem=("parallel", "parallel"), comms=comms)


def _mm_tn(a, b, *, name, tk, tn, a_fn=None, out_shape=None, out_spec=None):
    m_total, k_total = a.shape
    n_total = b.shape[1]
    grid = (k_total // tk, n_total // tn)

    def body(a_ref, b_ref, o_ref):
        av = a_ref[...]
        part = _dot(av if a_fn is None else a_fn(av), b_ref[...], TN_DIMS)
        o_ref[...] = part.reshape(o_ref.shape)

    if out_shape is None:
        out_shape = _sds((k_total, n_total), F32)
        out_spec = pl.BlockSpec((tk, tn), lambda i, j: (i, j))
    return pl.pallas_call(
        body, name=name, grid=grid,
        in_specs=[pl.BlockSpec((m_total, tk), lambda i, j: (0, i)), pl.BlockSpec((m_total, tn), lambda i, j: (0, j))],
        out_specs=out_spec, out_shape=out_shape,
        compiler_params=_params("parallel", "parallel"),
    )(a, b)


def _ada_fwd(c_all, w_shard, b_shard):
    nb, ncol = c_all.shape[0], w_shard.shape[1]
    tn = 512

    def body(c_ref, w_ref, b_ref, o_ref):
        c = c_ref[...]
        o_ref[...] = _dot(c * _sigmoid(c), w_ref[...]) + b_ref[...]

    return pl.pallas_call(
        body, name="ada_fwd", grid=(ncol // tn,),
        in_specs=[pl.BlockSpec((nb, D_MODEL), lambda j: (0, 0)), pl.BlockSpec((D_MODEL, tn), lambda j: (0, j)),
                  pl.BlockSpec((1, tn), lambda j: (0, j))],
        out_specs=pl.BlockSpec((nb, tn), lambda j: (0, j)), out_shape=_sds((nb, ncol), F32),
        compiler_params=_params("parallel"),
    )(c_all, w_shard, b_shard)


def _adamw_math(g, w, m, v):
    m = ADAM_B1 * m + (1.0 - ADAM_B1) * g
    v = ADAM_B2 * v + (1.0 - ADAM_B2) * (g * g)
    m_hat = m / (1.0 - ADAM_B1 ** ADAM_STEP)
    v_hat = v / (1.0 - ADAM_B2 ** ADAM_STEP)
    delta = -ADAM_LR * (m_hat / (jnp.sqrt(v_hat) + ADAM_EPS) + ADAM_WD * w)
    return delta, m, v


def _ada_bwd_adamw(c_all, dmod_cols, w, m, v):
    nb, ncol = dmod_cols.shape
    tn = 256

    def body(c_ref, d_ref, w_ref, m_ref, v_ref, g_ref, dl_ref, nm_ref, nv_ref):
        c = c_ref[...]
        g = _dot(c * _sigmoid(c), d_ref[...], TN_DIMS)
        g_ref[...] = g
        dl_ref[...], nm_ref[...], nv_ref[...] = _adamw_math(g, w_ref[...], m_ref[...], v_ref[...])

    col = pl.BlockSpec((D_MODEL, tn), lambda j: (0, j))
    shp = _sds((D_MODEL, ncol), F32)
    return pl.pallas_call(
        body, name="ada_bwd_adamw", grid=(ncol // tn,),
        in_specs=[pl.BlockSpec((nb, D_MODEL), lambda j: (0, 0)), pl.BlockSpec((nb, tn), lambda j: (0, j)), col, col, col],
        out_specs=[col, col, col, col], out_shape=[shp, shp, shp, shp],
        compiler_params=_params("parallel"),
    )(c_all, dmod_cols, w, m, v)


def _adamw_halves(own, theirs, core, w, m, v, *, axis, name):
    r2, c2 = own.shape
    tr = min(r2, 256)
    nt = r2 // tr

    def body(core_ref, own_ref, their_ref, w_ref, m_ref, v_ref, g_ref, dl_ref, nm_ref, nv_ref):
        g = jnp.where(pl.program_id(0) == core_ref[0], own_ref[...], their_ref[...])
        g_ref[...] = g
        dl_ref[...], nm_ref[...], nv_ref[...] = _adamw_math(g, w_ref[...], m_ref[...], v_ref[...])

    if axis == 0:
        full = pl.BlockSpec((tr, c2), lambda h, i, core_ref: (h * nt + i, 0))
    else:
        full = pl.BlockSpec((tr, c2), lambda h, i, core_ref: (i, h))
    half = pl.BlockSpec((tr, c2), lambda h, i, core_ref: (i, 0))
    shp = _sds(w.shape, F32)
    return pl.pallas_call(
        body, name=name,
        grid_spec=pltpu.PrefetchScalarGridSpec(num_scalar_prefetch=1, grid=(2, nt), in_specs=[half, half, full, full, full],
                                               out_specs=[full] * 4),
        out_shape=[shp] * 4, compiler_params=_params("parallel", "parallel"),
    )(core, own, theirs, w, m, v)


def _tok_spec(tm, width=D_MODEL):
    return pl.BlockSpec((None, tm, width), lambda b, i: (b, i, 0))


def _row_spec(width=D_MODEL):
    return pl.BlockSpec((None, 1, width), lambda b, i: (b, 0, 0))


def _vec_spec(width=D_MODEL):
    return pl.BlockSpec((1, width), lambda b, i: (0, 0))


def _prenorm1(x, w, sc, sh, tm=512):
    bsz, seq, _ = x.shape

    def body(x_ref, w_ref, sc_ref, sh_ref, h_ref):
        y, _, _ = _rms_fwd(x_ref[...], w_ref[...])
        h_ref[...] = (y * (1.0 + sc_ref[...]) + sh_ref[...]).astype(BF16)

    return pl.pallas_call(
        body, name="prenorm1", grid=(bsz, seq // tm),
        in_specs=[_tok_spec(tm), _vec_spec(), _row_spec(), _row_spec()],
        out_specs=_tok_spec(tm), out_shape=_sds(x.shape, BF16),
        compiler_params=_params("parallel", "parallel"),
    )(x, w, sc, sh)


def _rope_tables(seq):
    half = ROPE_DIM // 2
    inv_freq = ROPE_THETA ** (-jnp.arange(0, ROPE_DIM, 2, dtype=F32) / ROPE_DIM)
    ang = jnp.arange(seq, dtype=F32)[:, None] * inv_freq[None, :]
    cos, sin = jnp.cos(ang), jnp.sin(ang)
    rest = ATT_HEAD_DIM - ROPE_DIM
    ones, zeros, zh = jnp.ones((seq, rest), F32), jnp.zeros((seq, rest), F32), jnp.zeros((seq, half), F32)
    reps = LANE // ATT_HEAD_DIM
    t_cos = jnp.tile(jnp.concatenate([cos, cos, ones], axis=1), (1, reps))
    t_up = jnp.tile(jnp.concatenate([zh, sin, zeros], axis=1), (1, reps))
    t_dn = jnp.tile(jnp.concatenate([-sin, zh, zeros], axis=1), (1, reps))
    return t_cos, t_up, t_dn


def _rope_fwd(proj, tables, tm=256):
    bsz, seq, _ = proj.shape
    half = ROPE_DIM // 2

    def body(p_ref, c_ref, u_ref, d_ref, q_ref, k_ref, v_ref):
        c, u, d = c_ref[...], u_ref[...], d_ref[...]

        def rope(x):
            return x * c + pltpu.roll(x, half, 1) * u + pltpu.roll(x, LANE - half, 1) * d

        for s in range(ATT_WIDTH // LANE):
            q_ref[:, s * LANE:(s + 1) * LANE] = rope(p_ref[:, s * LANE:(s + 1) * LANE]).astype(BF16)
        k_ref[...] = rope(p_ref[:, ATT_WIDTH:ATT_WIDTH + LANE]).astype(BF16)
        v_ref[...] = p_ref[:, ATT_WIDTH + LANE:ATT_COLS].astype(BF16)

    tab = pl.BlockSpec((tm, LANE), lambda b, i: (i, 0))
    return pl.pallas_call(
        body, name="rope_fwd", grid=(bsz, seq // tm),
        in_specs=[_tok_spec(tm, ATT_COLS), tab, tab, tab],
        out_specs=[_tok_spec(tm, ATT_WIDTH), _tok_spec(tm, LANE), _tok_spec(tm, LANE)],
        out_shape=[_sds((bsz, seq, ATT_WIDTH), BF16), _sds((bsz, seq, LANE), BF16), _sds((bsz, seq, LANE), BF16)],
        compiler_params=_params("parallel", "parallel"),
    )(proj, *tables)


def _band_masks(i):
    row = lax.broadcasted_iota(jnp.int32, (WINDOW, WINDOW), 0)
    col = lax.broadcasted_iota(jnp.int32, (WINDOW, WINDOW), 1)
    return col <= row, jnp.logical_and(col > row, i > 0)


def _prev_spec(width):
    return pl.BlockSpec((None, WINDOW, width), lambda b, i: (b, jnp.maximum(i - 1, 0), 0))


def _attn_fwd(qr, kr, vb, sinks, w_norm, comms=()):
    bsz, seq, _ = qr.shape
    nblk = seq // WINDOW
    neg = float(jnp.finfo(jnp.float32).min)

    def body(sink_ref, q_ref, kc_ref, kp_ref, vc_ref, vp_ref, w_ref, raw_ref, an_ref, l_ref):
        mask_c, mask_p = _band_masks(pl.program_id(1))
        for h in range(ATT_Q_HEADS):
            g = h // ATT_GROUP
            hs = slice(h * ATT_HEAD_DIM, (h + 1) * ATT_HEAD_DIM)
            gs = slice(g * ATT_HEAD_DIM, (g + 1) * ATT_HEAD_DIM)
            q = q_ref[:, hs]
            sink = sink_ref[0, h]
            sc = jnp.where(mask_c, _dot(q, kc_ref[:, gs], NT_DIMS) * ATT_SCALE, neg)
            sp = jnp.where(mask_p, _dot(q, kp_ref[:, gs], NT_DIMS) * ATT_SCALE, neg)
            m = jnp.maximum(jnp.maximum(jnp.max(sc, axis=-1, keepdims=True), jnp.max(sp, axis=-1, keepdims=True)), sink)
            pc = jnp.where(mask_c, jnp.exp(sc - m), 0.0)
            pp = jnp.where(mask_p, jnp.exp(sp - m), 0.0)
            den = jnp.sum(pc, axis=-1, keepdims=True) + jnp.sum(pp, axis=-1, keepdims=True) + jnp.exp(sink - m)
            raw_ref[:, hs] = _dot(pc / den, vc_ref[:, gs]) + _dot(pp / den, vp_ref[:, gs])
            l_ref[:, h:h + 1] = m + jnp.log(den)
        y, _, _ = _rms_fwd(raw_ref[...], w_ref[...])
        an_ref[...] = y.astype(BF16)

    cur = lambda width: pl.BlockSpec((None, WINDOW, width), lambda b, i: (b, i, 0))
    return _call(
        body, (sinks, qr, kr, kr, vb, vb, w_norm), name="attn_fwd", grid=(bsz, nblk),
        in_specs=[pl.BlockSpec(memory_space=pltpu.SMEM), cur(ATT_WIDTH), cur(LANE), _prev_spec(LANE), cur(LANE), _prev_spec(LANE),
                  _vec_spec(ATT_WIDTH)],
        out_specs=[cur(ATT_WIDTH), cur(ATT_WIDTH), cur(ATT_Q_HEADS)],
        out_shape=[_sds((bsz, seq, ATT_WIDTH), F32), _sds((bsz, seq, ATT_WIDTH), BF16), _sds((bsz, seq, ATT_Q_HEADS), F32)],
        sem=("parallel", "parallel"), comms=comms)


HG_Q0 = ATT_COLS // LANE
HG_F0 = HG_Q0 + HG_HEADS
HG_I0 = HG_F0 + HG_HEADS
HG_G0 = HG_I0 + HG_HEADS
HG_TOK = 256
HG_NCH = HG_TOK // HG_CHUNK


def _block_masks():
    row = lax.broadcasted_iota(jnp.int32, (HG_TOK, HG_TOK), 0)
    col = lax.broadcasted_iota(jnp.int32, (HG_TOK, HG_TOK), 1)
    same = (row // HG_CHUNK) == (col // HG_CHUNK)
    return jnp.logical_and(same, col <= row), jnp.logical_and(same, col >= row)


def _row_in_chunk():
    return lax.broadcasted_iota(jnp.int32, (HG_TOK, LANE), 0) % HG_CHUNK


def _chunk_cumsum(x, reverse=False):
    ric = _row_in_chunk()
    shift = 1
    while shift < HG_CHUNK:
        if reverse:
            x = x + jnp.where(ric < HG_CHUNK - shift, pltpu.roll(x, HG_TOK - shift, 0), 0.0)
        else:
            x = x + jnp.where(ric >= shift, pltpu.roll(x, shift, 0), 0.0)
        shift *= 2
    return x


def _chunk_rows(rows):
    stacked = jnp.concatenate([r[None] for r in rows], axis=0)
    return jnp.broadcast_to(stacked, (HG_NCH, HG_CHUNK, LANE)).reshape(HG_TOK, LANE)


def _chunk_slices(x):
    return [x[j * HG_CHUNK:(j + 1) * HG_CHUNK] for j in range(HG_NCH)]


def _hgrn_common(tbl, hf, hq):
    lb = _sigmoid(tbl[1:2] - tbl[0:1])
    sig = _sigmoid(hf)
    f = lb + (1.0 - lb) * sig
    sq = _sigmoid(hq)
    q, k = hq * sq, 1.0 - f
    b = _chunk_cumsum(jnp.log(f))
    last = [b[(j + 1) * HG_CHUNK - 1:(j + 1) * HG_CHUNK] for j in range(HG_NCH)]
    bl = _chunk_rows(last)
    e_b, e_nb, e_rem = jnp.exp(b), jnp.exp(-b), jnp.exp(bl - b)
    e_last = [jnp.exp(r) for r in last]
    return dict(lb=lb, sig=sig, f=f, sq=sq, q=q, k=k, e_b=e_b, e_nb=e_nb, e_rem=e_rem, e_last=e_last,
                qd=q * e_b, kd=k * e_nb, ku=k * e_rem)


def _hgrn_fwd(proj, lb_table, norm_w, comms=()):
    bsz, seq, _ = proj.shape
    nstep = seq // HG_TOK

    def body(tbl_ref, nw_ref, q_ref, f_ref, i_ref, g_ref, o_ref, rec_ref, st_ref, s_scr):
        @pl.when(pl.program_id(2) == 0)
        def _():
            s_scr[...] = jnp.zeros_like(s_scr)

        v, hg = i_ref[...], g_ref[...]
        t = _hgrn_common(tbl_ref[...], f_ref[...], q_ref[...])
        lower, _ = _block_masks()
        a = jnp.where(lower, _dot(t["qd"], t["kd"], NT_DIMS), 0.0)
        o_intra = _dot(a, v)
        v_c, ku_c, qd_c = _chunk_slices(v.astype(BF16)), _chunk_slices(t["ku"].astype(BF16)), _chunk_slices(t["qd"].astype(BF16))
        updates = [_dot(v_c[j], ku_c[j], TN_DIMS) for j in range(HG_NCH)]
        st = s_scr[...]
        states = []
        for j in range(HG_NCH):
            states.append(st)
            st = st * t["e_last"][j] + updates[j]
        s_scr[...] = st
        o = o_intra + jnp.concatenate([_dot(qd_c[j], states[j], NT_DIMS) for j in range(HG_NCH)], axis=0)
        for j in range(HG_NCH):
            st_ref[j] = states[j]
        o_ref[...] = o
        y, _, _ = _rms_fwd(o, nw_ref[...])
        rec_ref[...] = (y * (hg * _sigmoid(hg))).astype(BF16)

    slab = lambda first: pl.BlockSpec((None, HG_TOK, LANE), lambda b, h, t: (b, t, first + h))
    head_out = pl.BlockSpec((None, HG_TOK, LANE), lambda b, h, t: (b, t, h))
    return _call(
        body, (lb_table, norm_w, proj, proj, proj, proj), name="hgrn_fwd", grid=(bsz, HG_HEADS, nstep),
        in_specs=[pl.BlockSpec((2, LANE), lambda b, h, t: (0, h)), pl.BlockSpec((1, LANE), lambda b, h, t: (0, 0)),
                  slab(HG_Q0), slab(HG_F0), slab(HG_I0), slab(HG_G0)],
        out_specs=[head_out, head_out,
                   pl.BlockSpec((None, None, HG_NCH, LANE, LANE), lambda b, h, t: (b, h, t, 0, 0))],
        out_shape=[_sds((bsz, seq, HG_WIDTH), F32), _sds((bsz, seq, HG_WIDTH), BF16),
                   _sds((bsz, HG_HEADS, seq // HG_CHUNK, LANE, LANE), F32)],
        scratch_shapes=[pltpu.VMEM((LANE, LANE), F32)],
        sem=("parallel", "parallel", "arbitrary"), comms=comms)


def _mid_fwd(x, mix, post_w, g1, pre_w, sc2, sh2, tm=512):
    bsz, seq, _ = x.shape

    def body(x_ref, mix_ref, pw_ref, g1_ref, w2_ref, sc_ref, sh_ref, x1_ref, h2_ref):
        n1, _, _ = _rms_fwd(mix_ref[...], pw_ref[...])
        x1 = x_ref[...] + g1_ref[...] * n1
        x1_ref[...] = x1
        y2, _, _ = _rms_fwd(x1, w2_ref[...])
        h2_ref[...] = (y2 * (1.0 + sc_ref[...]) + sh_ref[...]).astype(BF16)

    return pl.pallas_call(
        body, name="mid_fwd", grid=(bsz, seq // tm),
        in_specs=[_tok_spec(tm), _tok_spec(tm), _vec_spec(), _row_spec(), _vec_spec(), _row_spec(), _row_spec()],
        out_specs=[_tok_spec(tm), _tok_spec(tm)], out_shape=[_sds(x.shape, F32), _sds(x.shape, BF16)],
        compiler_params=_params("parallel", "parallel"),
    )(x, mix, post_w, g1, pre_w, sc2, sh2)


def _acc_out(ref, first, value):
    @pl.when(first)
    def _():
        ref[...] = value

    @pl.when(jnp.logical_not(first))
    def _():
        ref[...] += value


def _loss_bwd(x1, down, post_w, g2, target, tm=512):
    bsz, seq, _ = x1.shape

    def body(x1_ref, d_ref, w_ref, g2_ref, t_ref, loss_ref, dy_ref, dd_ref, dg2_ref, dw_ref):
        b, i = pl.program_id(0), pl.program_id(1)
        w, g2v = w_ref[...], g2_ref[...]
        n2, dh, rstd = _rms_fwd(d_ref[...], w)
        err = x1_ref[...] + g2v * n2 - t_ref[...]
        part = (0.5 / D_MODEL) * jnp.sum(jnp.sum(err * err, axis=-1, keepdims=True), axis=0, keepdims=True)
        _acc_out(loss_ref, jnp.logical_and(b == 0, i == 0), jnp.broadcast_to(part, (1, LANE)))
        dy = err * (1.0 / D_MODEL)
        dy_ref[...] = dy
        _acc_out(dg2_ref, i == 0, _colsum(dy * n2))
        dd, dw_rows = _rms_bwd(dy * g2v, dh, rstd, w)
        dd_ref[...] = dd.astype(BF16)
        _acc_out(dw_ref, jnp.logical_and(b == 0, i == 0), _colsum(dw_rows))

    return pl.pallas_call(
        body, name="loss_bwd", grid=(bsz, seq // tm),
        in_specs=[_tok_spec(tm), _tok_spec(tm), _vec_spec(), _row_spec(), _tok_spec(tm)],
        out_specs=[_vec_spec(LANE), _tok_spec(tm), _tok_spec(tm), _row_spec(), _vec_spec()],
        out_shape=[_sds((1, LANE), F32), _sds(x1.shape, F32), _sds(x1.shape, BF16), _sds((bsz, 1, D_MODEL), F32),
                   _sds((1, D_MODEL), F32)],
        compiler_params=_params("arbitrary", "arbitrary"),
    )(x1, down, post_w, g2, target)


def _mid_bwd(dh2, dy, x1, mix, pre_w, sc2, post_w, g1, tm=512, comms=()):
    bsz, seq, _ = x1.shape

    def body(dh2_ref, dy_ref, x1_ref, mix_ref, w2_ref, sc_ref, pw_ref, g1_ref,
             dx1_ref, dmix_ref, dsc_ref, dsh_ref, dg1_ref, dw2_ref, dpw_ref):
        b, i = pl.program_id(0), pl.program_id(1)
        first = jnp.logical_and(b == 0, i == 0)
        w2, pw = w2_ref[...], pw_ref[...]
        dh2v = dh2_ref[...]
        y2, xh2, rstd2 = _rms_fwd(x1_ref[...], w2)
        _acc_out(dsh_ref, i == 0, _colsum(dh2v))
        _acc_out(dsc_ref, i == 0, _colsum(dh2v * y2))
        dx1n, dw_rows = _rms_bwd(dh2v * (1.0 + sc_ref[...]), xh2, rstd2, w2)
        _acc_out(dw2_ref, first, _colsum(dw_rows))
        dx1 = dy_ref[...] + dx1n
        dx1_ref[...] = dx1
        n1, mh, rstd1 = _rms_fwd(mix_ref[...], pw)
        _acc_out(dg1_ref, i == 0, _colsum(dx1 * n1))
        dmix, dpw_rows = _rms_bwd(dx1 * g1_ref[...], mh, rstd1, pw)
        dmix_ref[...] = dmix.astype(BF16)
        _acc_out(dpw_ref, first, _colsum(dpw_rows))

    row_shape = _sds((bsz, 1, D_MODEL), F32)
    vec_shape = _sds((1, D_MODEL), F32)
    return _call(
        body, (dh2, dy, x1, mix, pre_w, sc2, post_w, g1), name="mid_bwd", grid=(bsz, seq // tm),
        in_specs=[_tok_spec(tm), _tok_spec(tm), _tok_spec(tm), _tok_spec(tm), _vec_spec(), _row_spec(), _vec_spec(), _row_spec()],
        out_specs=[_tok_spec(tm), _tok_spec(tm), _row_spec(), _row_spec(), _row_spec(), _vec_spec(), _vec_spec()],
        out_shape=[_sds(x1.shape, F32), _sds(x1.shape, BF16), row_shape, row_shape, row_shape, vec_shape, vec_shape],
        sem=("arbitrary", "arbitrary"), comms=comms)


def _norm1_bwd(dh1, dx1, x, pre_w, sc1, tm=512):
    bsz, seq, _ = x.shape

    def body(dh_ref, dx1_ref, x_ref, w_ref, sc_ref, gx_ref, dsc_ref, dsh_ref, dw_ref):
        b, i = pl.program_id(0), pl.program_id(1)
        w = w_ref[...]
        dh = dh_ref[...]
        y, xh, rstd = _rms_fwd(x_ref[...], w)
        _acc_out(dsh_ref, i == 0, _colsum(dh))
        _acc_out(dsc_ref, i == 0, _colsum(dh * y))
        dx, dw_rows = _rms_bwd(dh * (1.0 + sc_ref[...]), xh, rstd, w)
        _acc_out(dw_ref, jnp.logical_and(b == 0, i == 0), _colsum(dw_rows))
        gx_ref[...] = dx1_ref[...] + dx

    row_shape = _sds((bsz, 1, D_MODEL), F32)
    return pl.pallas_call(
        body, name="norm1_bwd", grid=(bsz, seq // tm),
        in_specs=[_tok_spec(tm), _tok_spec(tm), _tok_spec(tm), _vec_spec(), _row_spec()],
        out_specs=[_tok_spec(tm), _row_spec(), _row_spec(), _vec_spec()],
        out_shape=[_sds(x.shape, F32), row_shape, row_shape, _sds((1, D_MODEL), F32)],
        compiler_params=_params("arbitrary", "arbitrary"),
    )(dh1, dx1, x, pre_w, sc1)


def _hgrn_bwd(dcat, proj, o_raw, states, lb_table, norm_w, comms=()):
    bsz, seq, _ = proj.shape
    nstep = seq // HG_TOK
    rec0 = ATT_WIDTH // LANE

    def body(tbl_ref, nw_ref, dr_ref, q_ref, f_ref, i_ref, g_ref, o_ref, st_ref,
             dq_ref, df_ref, di_ref, dg_ref, dlb_ref, dnw_ref, ds_scr):
        h, b, t = pl.program_id(0), pl.program_id(1), pl.program_id(2)

        @pl.when(t == 0)
        def _():
            ds_scr[...] = jnp.zeros_like(ds_scr)

        hq, v, hg = q_ref[...], i_ref[...], g_ref[...]
        nw = nw_ref[...]
        c = _hgrn_common(tbl_ref[...], f_ref[...], hq)
        qd, kd, ku = c["qd"], c["kd"], c["ku"]
        y, on, rstd = _rms_fwd(o_ref[...], nw)
        sg = _sigmoid(hg)
        dr = dr_ref[...]
        dg_ref[...] = (dr * y * (sg * (1.0 + hg * (1.0 - sg)))).astype(BF16)
        do, dnw_rows = _rms_bwd(dr * (hg * sg), on, rstd, nw)
        lower, upper = _block_masks()
        at = jnp.where(upper, _dot(kd, qd, NT_DIMS), 0.0)
        da = jnp.where(lower, _dot(do, v, NT_DIMS), 0.0)
        dat = jnp.where(upper, _dot(v, do, NT_DIMS), 0.0)
        dv = _dot(at, do)
        dqd = _dot(da, kd)
        dkd = _dot(dat, qd)
        do_c, qd_c, v_c, ku_c = [_chunk_slices(z.astype(BF16)) for z in (do, qd, v, ku)]
        outer = [_dot(do_c[j], qd_c[j], TN_DIMS) for j in range(HG_NCH)]
        ds = ds_scr[...]
        ds_after = [None] * HG_NCH
        for j in reversed(range(HG_NCH)):
            ds_after[j] = ds
            ds = outer[j] + ds * c["e_last"][j]
        ds_scr[...] = ds
        states = [st_ref[j] for j in range(HG_NCH)]
        dv = dv + jnp.concatenate([_dot(ku_c[j], ds_after[j], NT_DIMS) for j in range(HG_NCH)], axis=0)
        dqd = dqd + jnp.concatenate([_dot(do_c[j], states[j]) for j in range(HG_NCH)], axis=0)
        dku = jnp.concatenate([_dot(v_c[j], ds_after[j]) for j in range(HG_NCH)], axis=0)
        dku_ku = dku * ku
        dbl = [_colsum(states[j] * ds_after[j]) * c["e_last"][j] + _colsum(dku_ku[j * HG_CHUNK:(j + 1) * HG_CHUNK])
               for j in range(HG_NCH)]
        dk = dkd * c["e_nb"] + dku * c["e_rem"]
        db = dqd * qd - dkd * kd - dku_ku + jnp.where(_row_in_chunk() == HG_CHUNK - 1, _chunk_rows(dbl), 0.0)
        dfv = _chunk_cumsum(db, reverse=True) / c["f"] - dk
        sig, sq = c["sig"], c["sq"]
        df_ref[...] = (dfv * (1.0 - c["lb"]) * sig * (1.0 - sig)).astype(BF16)
        dq_ref[...] = (dqd * c["e_b"] * (sq * (1.0 + hq * (1.0 - sq)))).astype(BF16)
        di_ref[...] = dv.astype(BF16)
        _acc_out(dlb_ref, jnp.logical_and(b == 0, t == 0), _colsum(dfv * (1.0 - sig)))
        _acc_out(dnw_ref, jnp.logical_and(h == 0, jnp.logical_and(b == 0, t == 0)), _colsum(dnw_rows))

    rev = lambda t: nstep - 1 - t
    slab = lambda first: pl.BlockSpec((None, HG_TOK, LANE), lambda h, b, t: (b, rev(t), first + h))
    head = pl.BlockSpec((None, HG_TOK, LANE), lambda h, b, t: (b, rev(t), h))
    grad_shape = _sds((bsz, seq, HG_WIDTH), BF16)
    return _call(
        body, (lb_table, norm_w, dcat, proj, proj, proj, proj, o_raw, states), name="hgrn_bwd", grid=(HG_HEADS, bsz, nstep),
        in_specs=[pl.BlockSpec((2, LANE), lambda h, b, t: (0, h)), pl.BlockSpec((1, LANE), lambda h, b, t: (0, 0)),
                  slab(rec0), slab(HG_Q0), slab(HG_F0), slab(HG_I0), slab(HG_G0), head,
                  pl.BlockSpec((None, None, HG_NCH, LANE, LANE), lambda h, b, t: (b, h, rev(t), 0, 0))],
        out_specs=[head, head, head, head, pl.BlockSpec((1, LANE), lambda h, b, t: (0, h)),
                   pl.BlockSpec((1, LANE), lambda h, b, t: (0, 0))],
        out_shape=[grad_shape, grad_shape, grad_shape, grad_shape, _sds((1, HG_WIDTH), F32), _sds((1, LANE), F32)],
        scratch_shapes=[pltpu.VMEM((LANE, LANE), F32)],
        sem=("arbitrary", "arbitrary", "arbitrary"), comms=comms)


def _attn_bwd(dcat, raw, w_norm, qr, kr, vb, lse, sinks, comms=()):
    bsz, seq, _ = qr.shape
    nblk = seq // WINDOW

    def body(sink_ref, da_ref, raw_ref, w_ref, q_ref, kc_ref, kp_ref, vc_ref, vp_ref, l_ref,
             dq_ref, dkd_ref, dkp_ref, dvd_ref, dvp_ref, dw_ref, dsink_ref):
        b, i = pl.program_id(0), pl.program_id(1)
        first = jnp.logical_and(b == 0, i == 0)
        mask_c, mask_p = _band_masks(i)
        raw_v = raw_ref[...]
        w = w_ref[...]
        _, on, rstd = _rms_fwd(raw_v, w)
        do_all, dw_rows = _rms_bwd(da_ref[...], on, rstd, w)
        _acc_out(dw_ref, first, _colsum(dw_rows))
        lane8 = lax.broadcasted_iota(jnp.int32, (1, ATT_Q_HEADS), 1)
        dsink = jnp.zeros((1, ATT_Q_HEADS), F32)
        for g in range(ATT_KV_HEADS):
            gs = slice(g * ATT_HEAD_DIM, (g + 1) * ATT_HEAD_DIM)
            kc, kp, vc, vp = kc_ref[:, gs], kp_ref[:, gs], vc_ref[:, gs], vp_ref[:, gs]
            dkc = jnp.zeros((WINDOW, ATT_HEAD_DIM), F32)
            dkp, dvc, dvp = dkc, dkc, dkc
            for hh in range(ATT_GROUP):
                h = g * ATT_GROUP + hh
                hs = slice(h * ATT_HEAD_DIM, (h + 1) * ATT_HEAD_DIM)
                q = q_ref[:, hs]
                doh = do_all[:, hs]
                dsum = jnp.sum(doh * raw_v[:, hs], axis=-1, keepdims=True)
                lse_h = l_ref[:, h:h + 1]
                pc = jnp.where(mask_c, jnp.exp(_dot(q, kc, NT_DIMS) * ATT_SCALE - lse_h), 0.0)
                pp = jnp.where(mask_p, jnp.exp(_dot(q, kp, NT_DIMS) * ATT_SCALE - lse_h), 0.0)
                p_sink = jnp.exp(sink_ref[0, h] - lse_h)
                dsink = dsink - jnp.where(lane8 == h, jnp.sum(p_sink * dsum, axis=0, keepdims=True), 0.0)
                dsc = pc * (_dot(doh, vc, NT_DIMS) - dsum) * ATT_SCALE
                dsp = pp * (_dot(doh, vp, NT_DIMS) - dsum) * ATT_SCALE
                dq_ref[:, hs] = _dot(dsc, kc) + _dot(dsp, kp)
                dkc += _dot(dsc, q, TN_DIMS)
                dkp += _dot(dsp, q, TN_DIMS)
                dvc += _dot(pc, doh, TN_DIMS)
                dvp += _dot(pp, doh, TN_DIMS)
            dkd_ref[:, gs], dkp_ref[:, gs], dvd_ref[:, gs], dvp_ref[:, gs] = dkc, dkp, dvc, dvp
        _acc_out(dsink_ref, first, dsink)

    cur = lambda width: pl.BlockSpec((None, WINDOW, width), lambda b, i: (b, i, 0))
    kv_shape = _sds((bsz, seq, LANE), F32)
    return _call(
        body, (sinks, dcat, raw, w_norm, qr, kr, kr, vb, vb, lse), name="attn_bwd", grid=(bsz, nblk),
        in_specs=[pl.BlockSpec(memory_space=pltpu.SMEM), cur(ATT_WIDTH), cur(ATT_WIDTH), _vec_spec(ATT_WIDTH), cur(ATT_WIDTH),
                  cur(LANE), _prev_spec(LANE), cur(LANE), _prev_spec(LANE), cur(ATT_Q_HEADS)],
        out_specs=[cur(ATT_WIDTH), cur(LANE), cur(LANE), cur(LANE), cur(LANE), _vec_spec(ATT_WIDTH), _vec_spec(ATT_Q_HEADS)],
        out_shape=[_sds((bsz, seq, ATT_WIDTH), F32), kv_shape, kv_shape, kv_shape, kv_shape, _sds((1, ATT_WIDTH), F32),
                   _sds((1, ATT_Q_HEADS), F32)],
        sem=("arbitrary", "arbitrary"), comms=comms)


def _rope_bwd(dq, dkd, dkp, dvd, dvp, tables):
    bsz, seq, _ = dq.shape
    nblk = seq // WINDOW
    half = ROPE_DIM // 2

    def body(dq_ref, dkd_ref, dkp_ref, dvd_ref, dvp_ref, c_ref, u_ref, d_ref, o_ref):
        c, u, d = c_ref[...], u_ref[...], d_ref[...]
        has_next = pl.program_id(1) < nblk - 1

        def unrope(g):
            return g * c + pltpu.roll(g * u, LANE - half, 1) + pltpu.roll(g * d, half, 1)

        for s in range(ATT_WIDTH // LANE):
            o_ref[:, s * LANE:(s + 1) * LANE] = unrope(dq_ref[:, s * LANE:(s + 1) * LANE]).astype(BF16)
        dk = dkd_ref[...] + jnp.where(has_next, dkp_ref[...], 0.0)
        o_ref[:, ATT_WIDTH:ATT_WIDTH + LANE] = unrope(dk).astype(BF16)
        o_ref[:, ATT_WIDTH + LANE:ATT_COLS] = (dvd_ref[...] + jnp.where(has_next, dvp_ref[...], 0.0)).astype(BF16)

    cur = lambda width: pl.BlockSpec((None, WINDOW, width), lambda b, i: (b, i, 0))
    nxt = pl.BlockSpec((None, WINDOW, LANE), lambda b, i: (b, jnp.minimum(i + 1, nblk - 1), 0))
    tab = pl.BlockSpec((WINDOW, LANE), lambda b, i: (i, 0))
    return pl.pallas_call(
        body, name="rope_bwd", grid=(bsz, nblk),
        in_specs=[cur(ATT_WIDTH), cur(LANE), nxt, cur(LANE), nxt, tab, tab, tab],
        out_specs=cur(ATT_COLS), out_shape=_sds((bsz, seq, ATT_COLS), BF16),
        compiler_params=_params("parallel", "parallel"),
    )(dq, dkd, dkp, dvd, dvp, *tables)


def _other_chips(x, y):
    return [(1 - x, y), (x, 1 - y), (1 - x, 1 - y)]


def _sem_pair(n):
    return [pltpu.SemaphoreType.DMA((n,)), pltpu.SemaphoreType.DMA((n,))]


def _plan_chip_gather(blocks, bufs):
    n = len(blocks)

    def copies(ins, outs, sems):
        x, y, c = _mesh_pos()
        sends, lands = [], []
        for a in range(n):
            for j, chip in enumerate(_other_chips(x, y)):
                k = 3 * a + j
                sends.append(pltpu.make_async_remote_copy(
                    src_ref=ins[a], dst_ref=outs[a].at[4 * x + 2 * y + c], send_sem=sems[0].at[k], recv_sem=sems[1].at[k],
                    device_id=(*chip, c), device_id_type=MESH))
                slot = outs[a].at[4 * chip[0] + 2 * chip[1] + c]
                lands.append(pltpu.make_async_remote_copy(
                    src_ref=slot, dst_ref=slot, send_sem=sems[0].at[k], recv_sem=sems[1].at[k],
                    device_id=(*chip, c), device_id_type=MESH))
        return sends, lands

    def start(ins, outs, sems):
        for cp in copies(ins, outs, sems)[0]:
            cp.start()

    def finish(ins, outs, sems):
        sends, lands = copies(ins, outs, sems)
        for cp in lands:
            cp.wait_recv()
        for cp in sends:
            cp.wait_send()

    return _Comm(list(blocks) + list(bufs), [_sds(b.shape, b.dtype) for b in bufs], _sem_pair(3 * n), start, finish,
                 aliases=[(n + a, a) for a in range(n)])


def _plan_pair_forward(bufs):
    n = len(bufs)

    def copies(outs, sems):
        x, y, c = _mesh_pos()
        sends, lands = [], []
        for a in range(n):
            for j, chip in enumerate(_other_chips(x, y)):
                k = 3 * a + j
                slot = outs[a].at[4 * chip[0] + 2 * chip[1] + c]
                sends.append(pltpu.make_async_remote_copy(
                    src_ref=slot, dst_ref=slot, send_sem=sems[0].at[k], recv_sem=sems[1].at[k],
                    device_id=(x, y, 1 - c), device_id_type=MESH))
                theirs = outs[a].at[4 * chip[0] + 2 * chip[1] + 1 - c]
                lands.append(pltpu.make_async_remote_copy(
                    src_ref=theirs, dst_ref=theirs, send_sem=sems[0].at[k], recv_sem=sems[1].at[k],
                    device_id=(x, y, 1 - c), device_id_type=MESH))
        return sends, lands

    def start(ins, outs, sems):
        for cp in copies(outs, sems)[0]:
            cp.start()

    def finish(ins, outs, sems):
        sends, lands = copies(outs, sems)
        for cp in lands:
            cp.wait_recv()
        for cp in sends:
            cp.wait_send()

    return _Comm(list(bufs), [_sds(b.shape, b.dtype) for b in bufs], _sem_pair(3 * n), start, finish,
                 aliases=[(a, a) for a in range(n)])


def _plan_pair(arrays, other_half):
    n = len(arrays)

    def copies(ins, outs, sems):
        x, y, c = _mesh_pos()
        return [pltpu.make_async_remote_copy(
            src_ref=ins[a].at[1 - c] if other_half else ins[a], dst_ref=outs[a], send_sem=sems[0].at[a], recv_sem=sems[1].at[a],
            device_id=(x, y, 1 - c), device_id_type=MESH) for a in range(n)]

    def start(ins, outs, sems):
        for cp in copies(ins, outs, sems):
            cp.start()

    def finish(ins, outs, sems):
        for cp in copies(ins, outs, sems):
            cp.wait()

    shapes = [_sds(a.shape[1:] if other_half else a.shape, a.dtype) for a in arrays]
    return _Comm(list(arrays), shapes, _sem_pair(n), start, finish)


def _plan_chip_exchange(arrays):
    n = len(arrays)

    def copies(ins, outs, sems):
        x, y, c = _mesh_pos()
        sends, lands = [], []
        for a in range(n):
            for j, chip in enumerate(_other_chips(x, y)):
                k = 3 * a + j
                sends.append(pltpu.make_async_remote_copy(
                    src_ref=ins[a].at[2 * chip[0] + chip[1]], dst_ref=outs[a].at[2 * x + y], send_sem=sems[0].at[k],
                    recv_sem=sems[1].at[k], device_id=(*chip, c), device_id_type=MESH))
                slot = outs[a].at[2 * chip[0] + chip[1]]
                lands.append(pltpu.make_async_remote_copy(
                    src_ref=slot, dst_ref=slot, send_sem=sems[0].at[k], recv_sem=sems[1].at[k],
                    device_id=(*chip, c), device_id_type=MESH))
        return sends, lands

    def start(ins, outs, sems):
        for cp in copies(ins, outs, sems)[0]:
            cp.start()

    def finish(ins, outs, sems):
        sends, lands = copies(ins, outs, sems)
        for cp in lands:
            cp.wait_recv()
        for cp in sends:
            cp.wait_send()

    return _Comm(list(arrays), [_sds(a.shape, a.dtype) for a in arrays], _sem_pair(3 * n), start, finish)


def _comm_only(comms, name):
    return _call(lambda: None, (), name=name, grid=(), in_specs=[], out_specs=[], out_shape=[], sem=(), comms=comms)[1]


def _allgather8(arrays, name):
    return _comm_only([_plan_allgather8(arrays)], name)[0]


def _plan_allgather8(arrays):
    n = len(arrays)

    def parts(ins, outs, sems):
        send_sems, recv_sems, local_sems = sems
        x, y, c = _mesh_pos()
        me, sibling = (x, y, c), (x, y, 1 - c)
        chips = _other_chips(x, y)

        def copy(a, k, block, to, src=None):
            dst = outs[a].at[4 * block[0] + 2 * block[1] + block[2]]
            return pltpu.make_async_remote_copy(
                src_ref=dst if src is None else src, dst_ref=dst, send_sem=send_sems.at[7 * a + k],
                recv_sem=recv_sems.at[7 * a + k], device_id=to, device_id_type=MESH)

        mine = [pltpu.make_async_copy(ins[a], outs[a].at[4 * x + 2 * y + c], local_sems.at[a]) for a in range(n)]
        first = []
        for a in range(n):
            first.append(copy(a, 0, me, sibling, src=ins[a]))
            first += [copy(a, 1 + j, me, (*chip, c), src=ins[a]) for j, chip in enumerate(chips)]
        return copy, mine, first, me, sibling, chips, c

    def start(ins, outs, sems):
        _, mine, first, *_ = parts(ins, outs, sems)
        for cp in mine + first:
            cp.start()

    def finish(ins, outs, sems):
        copy, mine, first, me, sibling, chips, c = parts(ins, outs, sems)
        passed = []
        for j, chip in enumerate(chips):
            for a in range(n):
                copy(a, 1 + j, (*chip, c), me).wait_recv()
                fwd = copy(a, 4 + j, (*chip, c), sibling)
                fwd.start()
                passed.append(fwd)
        for a in range(n):
            copy(a, 0, sibling, me).wait_recv()
            for j, chip in enumerate(chips):
                copy(a, 4 + j, (*chip, 1 - c), me).wait_recv()
        for cp in first + passed:
            cp.wait_send()
        for cp in mine:
            cp.wait()

    sems = [pltpu.SemaphoreType.DMA((7 * n,)), pltpu.SemaphoreType.DMA((7 * n,)), pltpu.SemaphoreType.DMA((n,))]
    return _Comm(list(arrays), [_sds((N_DEV,) + a.shape, a.dtype) for a in arrays], sems, start, finish)


def _pair_sum(g, q, core, name):
    _, nblk, rows, cols = g.shape
    tr = min(rows, 256)

    def body(core_ref, g_ref, q_ref, o_ref):
        o_ref[...] = (g_ref[...] + q_ref[...]).astype(BF16)

    blk = pl.BlockSpec((None, tr, cols), lambda k, i, core_ref: (k, i, 0))
    return pl.pallas_call(
        body, name=name,
        grid_spec=pltpu.PrefetchScalarGridSpec(
            num_scalar_prefetch=1, grid=(nblk, rows // tr),
            in_specs=[pl.BlockSpec((None, None, tr, cols), lambda k, i, core_ref: (core_ref[0], k, i, 0)), blk],
            out_specs=blk),
        out_shape=_sds((nblk, rows, cols), BF16), compiler_params=_params("parallel", "parallel"),
    )(core, g, q)


def _sum_chips(own, landed, chip, name):
    _, rows, cols = own.shape
    tr = min(rows, 256)

    def body(chip_ref, own_ref, a_ref, b_ref, c_ref, o_ref):
        acc = own_ref[...].astype(F32) + a_ref[...].astype(F32)
        o_ref[...] = (acc + b_ref[...].astype(F32)) + c_ref[...].astype(F32)

    blk = lambda flip: pl.BlockSpec((None, tr, cols), lambda i, chip_ref: (jnp.bitwise_xor(chip_ref[0], flip), i, 0))
    return pl.pallas_call(
        body, name=name,
        grid_spec=pltpu.PrefetchScalarGridSpec(num_scalar_prefetch=1, grid=(rows // tr,), in_specs=[blk(0), blk(1), blk(2), blk(3)],
                                               out_specs=pl.BlockSpec((tr, cols), lambda i, chip_ref: (i, 0))),
        out_shape=_sds((rows, cols), F32), compiler_params=_params("parallel"),
    )(chip, own, landed, landed, landed)


SMALL_ROWS = 120
PACK_ROWS = 160


def _rows(a, nrows):
    flat = a.reshape(-1)
    return jnp.pad(flat, (0, nrows * LANE - flat.shape[0])).reshape(nrows, LANE)


def _pack_small(b_ada, pre_w_mix, post_w_mix, pre_w_mlp, post_w_mlp, attn_out_w, hg_norm_w, attn_sinks, lb_table):
    return jnp.concatenate([
        _rows(b_ada, 48), _rows(pre_w_mix, 8), _rows(post_w_mix, 8), _rows(pre_w_mlp, 8), _rows(post_w_mlp, 8),
        _rows(attn_out_w, 8), _rows(hg_norm_w, 8), _rows(attn_sinks, 8), _rows(lb_table[0], 8), _rows(lb_table[1], 8)], axis=0)


def _unpack_small(p):
    vec = lambda lo, n: p[lo:lo + n // LANE].reshape(1, n)
    lb = jnp.stack([p[104:108].reshape(HG_WIDTH), p[112:116].reshape(HG_WIDTH)])
    return dict(b_ada=vec(0, N_MOD * D_MODEL), pre_w_mix=vec(48, D_MODEL), post_w_mix=vec(56, D_MODEL), pre_w_mlp=vec(64, D_MODEL),
                post_w_mlp=vec(72, D_MODEL), attn_out_w=vec(80, ATT_WIDTH), hg_norm_w=p[88:89], attn_sinks=p[96:97, :ATT_Q_HEADS],
                lb_table=lb)


def _small_update(packs, w, m, v):
    def body(p_ref, w_ref, m_ref, v_ref, g_ref, dl_ref, nm_ref, nv_ref):
        tot = p_ref[0]
        for d in range(1, N_DEV):
            tot = tot + p_ref[d]
        wv = w_ref[...]
        p1 = _sigmoid(wv[112:120] - wv[104:112])
        s = tot[152:160] * p1 * (1.0 - p1)
        g = jnp.concatenate([tot[0:48] + tot[48:96], tot[96:152], -s, s], axis=0)
        g_ref[...] = g
        dl_ref[...], nm_ref[...], nv_ref[...] = _adamw_math(g, wv, m_ref[...], v_ref[...])

    shp = _sds((SMALL_ROWS, LANE), F32)
    return pl.pallas_call(body, name="small_update", out_shape=[shp] * 4, compiler_params=_params())(packs, w, m, v)


def kernel(x, c, w_ada, b_ada, pre_w_mix, w_in, attn_sinks, attn_out_w, lb_table, hg_norm_w, w_out, post_w_mix, pre_w_mlp, w_up, w_down, post_w_mlp, loss_target, m_w_ada, m_b_ada, m_pre_w_mix, m_w_in, m_attn_sinks, m_attn_out_w, m_lb_table, m_hg_norm_w, m_w_out, m_post_w_mix, m_pre_w_mlp, m_w_up, m_w_down, m_post_w_mlp, v_w_ada, v_b_ada, v_pre_w_mix, v_w_in, v_attn_sinks, v_attn_out_w, v_lb_table, v_hg_norm_w, v_w_out, v_post_w_mix, v_pre_w_mlp, v_w_up, v_w_down, v_post_w_mlp):
    xi, yi, ci = _mesh_pos()
    chip = 2 * xi + yi
    dev = 2 * chip + ci
    bsz, seq, _ = x.shape
    ntok = bsz * seq
    ada_cols = w_ada.shape[2]
    core = jnp.reshape(ci, (1,)).astype(jnp.int32)
    chip_idx = jnp.reshape(chip, (1,)).astype(jnp.int32)
    flat = lambda a: a.reshape(ntok, a.shape[-1])
    unflat = lambda a: a.reshape(bsz, seq, a.shape[-1])
    tables = _rope_tables(seq)

    def row_half(w):
        rows = w.shape[1] // 2
        return lax.dynamic_slice_in_dim(w[0], ci * rows, rows, axis=0).astype(BF16)

    def gather_buffer(w):
        rows, cols = w.shape[1] // 2, w.shape[2]
        own = w[0].astype(BF16).reshape(2, rows, cols)
        return lax.dynamic_update_slice(jnp.zeros((N_DEV, rows, cols), BF16), own, (2 * chip, 0, 0))

    c_g, in_g = _allgather8([c, row_half(w_in)], "gather_first")
    c_all = c_g.reshape(N_DEV * bsz, D_MODEL)
    w_in_full = in_g.reshape(N_CHIPS, D_MODEL, IN_COLS // N_CHIPS).transpose(1, 0, 2).reshape(D_MODEL, IN_COLS)

    b_cols = lax.dynamic_slice_in_dim(b_ada, chip * ada_cols, ada_cols, axis=1)
    mod_part = _ada_fwd(c_all, w_ada[0], b_cols)
    half_rows = mod_part.shape[0] // 2
    (mod_g,) = _allgather8([lax.dynamic_slice_in_dim(mod_part, ci * half_rows, half_rows, axis=0)], "gather_mod")
    mod_all = mod_g.reshape(N_CHIPS, 2, half_rows, ada_cols).transpose(1, 2, 0, 3).reshape(N_DEV * bsz, N_MOD * D_MODEL)
    mod = lax.dynamic_slice_in_dim(mod_all, dev * bsz, bsz, axis=0)
    sh1, sc1, g1, sh2, sc2, g2 = [mod[:, i * D_MODEL:(i + 1) * D_MODEL].reshape(bsz, 1, D_MODEL) for i in range(N_MOD)]

    h1 = _prenorm1(x, pre_w_mix, sc1, sh1)
    proj = unflat(_mm(flat(h1), w_in_full, name="in_proj", out_dtype=F32))
    qr, kr, vb = _rope_fwd(proj, tables)
    blocks = [row_half(w_out), row_half(w_up), row_half(w_down)]
    bufs = [gather_buffer(w_out), gather_buffer(w_up), gather_buffer(w_down)]
    (attn_raw, attn_n, lse), (bufs,) = _attn_fwd(qr, kr, vb, attn_sinks, attn_out_w, comms=[_plan_chip_gather(blocks, bufs)])
    (o_raw, rec, states), (bufs,) = _hgrn_fwd(proj, lb_table, hg_norm_w, comms=[_plan_pair_forward(bufs)])
    w_out_full = bufs[0].reshape(D_MODEL, D_MODEL)
    w_up4 = bufs[1].reshape(N_CHIPS, D_MODEL, D_MODEL)
    w_down_full = bufs[2].reshape(D_FF, D_MODEL)
    cat = jnp.concatenate([attn_n, rec], axis=-1)
    mix = unflat(_mm(flat(cat), w_out_full, name="out_proj", out_dtype=F32))
    x1, h2 = _mid_fwd(x, mix, post_w_mix, g1, pre_w_mlp, sc2, sh2)
    big_tm = min(ntok, 1024)
    up_spec = pl.BlockSpec((None, D_MODEL, D_MODEL), lambda i, j: (j, 0, 0))
    r = _mm(flat(h2), w_up4, name="up_proj", out_dtype=BF16, tm=big_tm, tn=D_MODEL, n_out=D_FF, b_spec=up_spec,
            epi=lambda acc: jnp.maximum(acc, 0.0))
    square = lambda t: t * t
    down = unflat(_mm(r, w_down_full, name="down_proj", out_dtype=F32, a_fn=square))
    loss_row, dy, dd, dg2, d_post_mlp = _loss_bwd(x1, down, post_w_mlp, g2, loss_target)
    loss = lax.psum(loss_row[0, 0], ("x", "y", "c"))

    dpre = _mm(flat(dd), w_down_full, name="down_bwd", out_dtype=BF16, trans_b=True, tm=big_tm, tn=D_MODEL, extra=(r,),
               epi=lambda acc, rt: acc * (2.0 * rt.astype(F32)))
    half_rows = D_MODEL // 2
    g_down = _mm_tn(r, flat(dd), name="down_wgrad", tk=half_rows, tn=D_MODEL, a_fn=square,
                    out_shape=_sds((2, N_CHIPS, half_rows, D_MODEL), F32),
                    out_spec=pl.BlockSpec((None, None, half_rows, D_MODEL), lambda i, j: (i % 2, i // 2, 0, 0)))
    dh2, ((q_down,),) = _mm(dpre, w_up4, name="up_bwd", out_dtype=F32, trans_b=True, n_out=D_MODEL, b_chunks=N_CHIPS,
                            comms=[_plan_pair([g_down], True)])
    g_up = _mm_tn(flat(h2), dpre, name="up_wgrad", tk=D_MODEL, tn=half_rows,
                  out_shape=_sds((2, N_CHIPS, half_rows, D_MODEL), F32),
                  out_spec=pl.BlockSpec((2, None, half_rows, half_rows), lambda i, j: (0, j // 2, 0, j % 2)))
    s_down = _pair_sum(g_down, q_down, core, "pair_sum_down")
    (dx1, dmix, dsc2, dsh2, dg1, d_pre_mlp, d_post_mix), ((q_up,),) = _mid_bwd(
        unflat(dh2), dy, x1, mix, pre_w_mlp, sc2, post_w_mix, g1, comms=[_plan_pair([g_up], True)])
    s_up = _pair_sum(g_up, q_up, core, "pair_sum_up")

    dcat = unflat(_mm(flat(dmix), w_out_full, name="out_bwd", out_dtype=F32, trans_b=True))
    out_rows = D_MODEL // N_CHIPS
    g_out = _mm_tn(flat(cat), flat(dmix), name="out_wgrad", tk=out_rows, tn=half_rows,
                   out_shape=_sds((2, N_CHIPS, out_rows, half_rows), F32),
                   out_spec=pl.BlockSpec((None, None, out_rows, half_rows), lambda i, j: (j, i, 0, 0)))
    (dhq, dhf, dhi, dhg, d_lb, d_hg_norm), ((x_down, x_up), (q_out,)) = _hgrn_bwd(
        dcat, proj, o_raw, states, lb_table, hg_norm_w, comms=[_plan_chip_exchange([s_down, s_up]), _plan_pair([g_out], True)])
    half_down = _sum_chips(s_down, x_down, chip_idx, "sum_chips_down")
    half_up = _sum_chips(s_up, x_up, chip_idx, "sum_chips_up")
    s_out = _pair_sum(g_out, q_out, core, "pair_sum_out")
    (dq, dkd, dkp, dvd, dvp, d_attn_out, d_sinks), ((their_down, their_up), (x_out,)) = _attn_bwd(
        dcat, attn_raw, attn_out_w, qr, kr, vb, lse, attn_sinks,
        comms=[_plan_pair([half_down, half_up], False), _plan_chip_exchange([s_out])])
    half_out = _sum_chips(s_out, x_out, chip_idx, "sum_chips_out")
    dproj_a = _rope_bwd(dq, dkd, dkp, dvd, dvp, tables)
    dproj = flat(jnp.concatenate([dproj_a, dhq, dhf, dhi, dhg], axis=-1))
    g_in = _mm_tn(flat(h1), dproj, name="in_wgrad", tk=D_MODEL, tn=2 * LANE)
    g_in = g_in.reshape(2, half_rows, N_CHIPS, IN_COLS // N_CHIPS).transpose(0, 2, 1, 3)
    dh1, ((q_in,), (their_out,)) = _mm(dproj, w_in_full, name="in_bwd", out_dtype=F32, trans_b=True,
                                       comms=[_plan_pair([g_in], True), _plan_pair([half_out], False)])
    s_in = _pair_sum(g_in, q_in, core, "pair_sum_in")
    grad_x, dsc1, dsh1, d_pre_mix = _norm1_bwd(unflat(dh1), dx1, x, pre_w_mix, sc1)

    dmod = jnp.concatenate([dsh1, dsc1, dg1, dsh2, dsc2, dg2], axis=-1).reshape(bsz, N_MOD * D_MODEL)
    pack = jnp.concatenate([
        _rows(dmod, 96), _rows(d_pre_mix, 8), _rows(d_post_mix, 8), _rows(d_pre_mlp, 8), _rows(d_post_mlp, 8),
        _rows(d_attn_out, 8), _rows(d_hg_norm, 8), _rows(d_sinks, 8), _rows(d_lb, 8)], axis=0)
    (packs,), (x_in,) = _comm_only([_plan_allgather8([pack]), _plan_chip_exchange([s_in])], "gather_small")
    half_in = _sum_chips(s_in, x_in, chip_idx, "sum_chips_in")
    ((their_in,),) = _comm_only([_plan_pair([half_in], False)], "pair_swap_in")
    small_args = lambda pre: (pre["b_ada"], pre["pre_w_mix"], pre["post_w_mix"], pre["pre_w_mlp"], pre["post_w_mlp"],
                              pre["attn_out_w"], pre["hg_norm_w"], pre["attn_sinks"], pre["lb_table"])
    w_small = dict(b_ada=b_ada, pre_w_mix=pre_w_mix, post_w_mix=post_w_mix, pre_w_mlp=pre_w_mlp, post_w_mlp=post_w_mlp,
                   attn_out_w=attn_out_w, hg_norm_w=hg_norm_w, attn_sinks=attn_sinks, lb_table=lb_table)
    m_small = dict(b_ada=m_b_ada, pre_w_mix=m_pre_w_mix, post_w_mix=m_post_w_mix, pre_w_mlp=m_pre_w_mlp, post_w_mlp=m_post_w_mlp,
                   attn_out_w=m_attn_out_w, hg_norm_w=m_hg_norm_w, attn_sinks=m_attn_sinks, lb_table=m_lb_table)
    v_small = dict(b_ada=v_b_ada, pre_w_mix=v_pre_w_mix, post_w_mix=v_post_w_mix, pre_w_mlp=v_pre_w_mlp, post_w_mlp=v_post_w_mlp,
                   attn_out_w=v_attn_out_w, hg_norm_w=v_hg_norm_w, attn_sinks=v_attn_sinks, lb_table=v_lb_table)
    small_out = [_unpack_small(p) for p in _small_update(packs, _pack_small(*small_args(w_small)), _pack_small(*small_args(m_small)),
                                                         _pack_small(*small_args(v_small)))]

    dmod_all = packs[:, :96, :].reshape(N_DEV * bsz, N_MOD * D_MODEL)
    dmod_cols = lax.dynamic_slice_in_dim(dmod_all, chip * ada_cols, ada_cols, axis=1)
    ada_out = _ada_bwd_adamw(c_all, dmod_cols, w_ada[0], m_w_ada[0], v_w_ada[0])

    big = dict(
        w_in=tuple(_adamw_halves(half_in, their_in, core, w_in[0], m_w_in[0], v_w_in[0], axis=0, name="adamw_in")),
        w_up=tuple(_adamw_halves(half_up, their_up, core, w_up[0], m_w_up[0], v_w_up[0], axis=0, name="adamw_up")),
        w_out=tuple(_adamw_halves(half_out, their_out, core, w_out[0], m_w_out[0], v_w_out[0], axis=1, name="adamw_out")),
        w_down=tuple(_adamw_halves(half_down, their_down, core, w_down[0], m_w_down[0], v_w_down[0], axis=0, name="adamw_down")),
        w_ada=tuple(ada_out),
    )
    order = ("w_ada", "b_ada", "pre_w_mix", "w_in", "attn_sinks", "attn_out_w", "lb_table", "hg_norm_w", "w_out", "post_w_mix",
             "pre_w_mlp", "w_up", "w_down", "post_w_mlp")
    outs = [loss, grad_x]
    for kind in range(4):
        for nm in order:
            outs.append(big[nm][kind][None] if nm in big else small_out[kind][nm])
    return tuple(outs)
```

```python
import functools

import jax
import jax.numpy as jnp
from jax import lax
from jax.experimental import pallas as pl
from jax.experimental.pallas import tpu as pltpu

F32 = jnp.float32
BF16 = jnp.bfloat16

D_MODEL = 1024
ATT_WIDTH = 512
ATT_HEAD_DIM = 64
ATT_Q_HEADS = 8
ATT_KV_HEADS = 2
ATT_GROUP = ATT_Q_HEADS // ATT_KV_HEADS
ATT_KV_COLS = ATT_KV_HEADS * ATT_HEAD_DIM
WINDOW = 128
ROPE_DIM = 16
ROPE_THETA = 500000.0
HG_WIDTH = 512
HG_HEAD_DIM = 128
HG_HEADS = 4
HG_CHUNK = 32
IN_COLS = ATT_WIDTH + 2 * ATT_KV_COLS + 4 * HG_WIDTH
ATT_COLS = ATT_WIDTH + 2 * ATT_KV_COLS
D_FF = 4 * D_MODEL
N_MOD = 6
EPS = 1e-6
ATT_SCALE = ATT_HEAD_DIM ** -0.5

ADAM_LR = 0.001
ADAM_B1 = 0.9
ADAM_B2 = 0.999
ADAM_EPS = 1e-08
ADAM_WD = 0.01
ADAM_STEP = 10

N_CHIPS = 4
N_DEV = 8
LANE = 128
VMEM_LIMIT = 48 * 1024 * 1024
MESH = pl.DeviceIdType.MESH

NT_DIMS = (((1,), (1,)), ((), ()))
TN_DIMS = (((0,), (0,)), ((), ()))


def _sds(shape, dtype):
    return jax.ShapeDtypeStruct(tuple(shape), dtype)


def _params(*sem):
    return pltpu.CompilerParams(dimension_semantics=sem, vmem_limit_bytes=VMEM_LIMIT)


def _sigmoid(x):
    return 1.0 / (1.0 + jnp.exp(-x))


def _dot(a, b, dims=None):
    a, b = a.astype(BF16), b.astype(BF16)
    if dims is None:
        return jnp.dot(a, b, preferred_element_type=F32)
    return lax.dot_general(a, b, dims, preferred_element_type=F32)


def _rms_fwd(x, w):
    rstd = lax.rsqrt(jnp.mean(x * x, axis=-1, keepdims=True) + EPS)
    xh = x * rstd
    return xh * w, xh, rstd


def _rms_bwd(dy, xh, rstd, w):
    dxh = dy * w
    dx = rstd * (dxh - xh * jnp.mean(dxh * xh, axis=-1, keepdims=True))
    return dx, dy * xh


def _colsum(x):
    return jnp.sum(x, axis=0, keepdims=True)


HBM_SPEC = pl.BlockSpec(memory_space=pltpu.HBM)


def _mesh_pos():
    return lax.axis_index("x"), lax.axis_index("y"), lax.axis_index("c")


class _Comm:
    def __init__(self, ins, outs, sems, start, finish, aliases=()):
        self.ins, self.outs, self.sems = list(ins), list(outs), list(sems)
        self.start, self.finish, self.aliases = start, finish, tuple(aliases)


def _call(body, args, *, name, grid, in_specs, out_specs, out_shape, sem, scratch_shapes=(), comms=()):
    scratch_shapes = list(scratch_shapes)
    if not comms:
        return pl.pallas_call(body, name=name, grid=grid, in_specs=in_specs, out_specs=out_specs, out_shape=out_shape,
                              scratch_shapes=scratch_shapes, compiler_params=_params(*sem))(*args)
    single = not isinstance(out_shape, (list, tuple))
    out_specs_l = [out_specs] if single else list(out_specs)
    out_shape_l = [out_shape] if single else list(out_shape)
    n_in, n_out, n_scr = len(in_specs), len(out_shape_l), len(scratch_shapes)
    n_ci = [len(cm.ins) for cm in comms]
    n_co = [len(cm.outs) for cm in comms]
    n_cs = [len(cm.sems) for cm in comms]
    aliases = {}
    for k, cm in enumerate(comms):
        for i, o in cm.aliases:
            aliases[n_in + sum(n_ci[:k]) + i] = n_out + sum(n_co[:k]) + o

    def fused(*refs):
        pos = [0]

        def take(n):
            part = refs[pos[0]:pos[0] + n]
            pos[0] += n
            return part

        ins = take(n_in)
        c_ins = [take(n) for n in n_ci]
        outs = take(n_out)
        c_outs = [take(n) for n in n_co]
        scr = take(n_scr)
        c_sems = [take(n) for n in n_cs]
        first, last = True, True
        for d, size in enumerate(grid):
            first = jnp.logical_and(first, pl.program_id(d) == 0)
            last = jnp.logical_and(last, pl.program_id(d) == size - 1)

        def run(which):
            for cm, ci, co, cs in zip(comms, c_ins, c_outs, c_sems):
                getattr(cm, which)(ci, co, cs)

        if grid:
            pl.when(first)(lambda: run("start"))
        else:
            run("start")
        body(*ins, *outs, *scr)
        if grid:
            pl.when(last)(lambda: run("finish"))
        else:
            run("finish")

    res = pl.pallas_call(
        fused, name=name, grid=grid, in_specs=list(in_specs) + [HBM_SPEC] * sum(n_ci),
        out_specs=out_specs_l + [HBM_SPEC] * sum(n_co), out_shape=out_shape_l + [s for cm in comms for s in cm.outs],
        input_output_aliases=aliases, scratch_shapes=scratch_shapes + [s for cm in comms for s in cm.sems],
        compiler_params=_params(*["arbitrary"] * len(grid)),
    )(*args, *[a for cm in comms for a in cm.ins])
    main = res[:n_out]
    extra, at = [], n_out
    for n in n_co:
        extra.append(list(res[at:at + n]))
        at += n
    return (main[0] if single else list(main)), extra


def _mm(a, b, *, name, out_dtype, trans_b=False, tm=512, tn=None, a_fn=None, extra=(), epi=None,
        b_spec=None, n_out=None, b_chunks=1, comms=()):
    m_total, k_total = a.shape
    if n_out is None:
        n_out = b.shape[0] if trans_b else b.shape[1]
    tn = n_out if tn is None else tn
    grid = (m_total // tm, n_out // tn)
    dims = NT_DIMS if trans_b else None
    kc = k_total // b_chunks

    def body(*refs):
        a_ref, b_ref = refs[0], refs[1]
        extra_refs = refs[2:2 + len(extra)]
        o_ref = refs[2 + len(extra)]
        if b_chunks == 1:
            av = a_ref[...]
            acc = _dot(av if a_fn is None else a_fn(av), b_ref[...], dims)
        else:
            acc = _dot(a_ref[:, 0:kc], b_ref[0], NT_DIMS)
            for k in range(1, b_chunks):
                acc = acc + _dot(a_ref[:, k * kc:(k + 1) * kc], b_ref[k], NT_DIMS)
        if epi is not None:
            acc = epi(acc, *[r[...] for r in extra_refs])
        o_ref[...] = acc.astype(out_dtype)

    if b_spec is None:
        if b_chunks > 1:
            b_spec = pl.BlockSpec((b_chunks, tn, kc), lambda i, j: (0, j, 0))
        elif trans_b:
            b_spec = pl.BlockSpec((tn, k_total), lambda i, j: (j, 0))
        else:
            b_spec = pl.BlockSpec((k_total, tn), lambda i, j: (0, j))
    in_specs = [pl.BlockSpec((tm, k_total), lambda i, j: (i, 0)), b_spec]
    in_specs += [pl.BlockSpec((tm, tn), lambda i, j: (i, j)) for _ in extra]
    return _call(
        body, (a, b, *extra), name=name, grid=grid, in_specs=in_specs,
        out_specs=pl.BlockSpec((tm, tn), lambda i, j: (i, j)),
        out_shape=_sds((m_total, n_out), out_dtype),
        sem=("parallel", "parallel"), comms=comms)


def _mm_tn(a, b, *, name, tk, tn, a_fn=None, out_shape=None, out_spec=None):
    m_total, k_total = a.shape
    n_total = b.shape[1]
    grid = (k_total // tk, n_total // tn)

    def body(a_ref, b_ref, o_ref):
        av = a_ref[...]
        part = _dot(av if a_fn is None else a_fn(av), b_ref[...], TN_DIMS)
        o_ref[...] = part.reshape(o_ref.shape)

    if out_shape is None:
        out_shape = _sds((k_total, n_total), F32)
        out_spec = pl.BlockSpec((tk, tn), lambda i, j: (i, j))
    return pl.pallas_call(
        body, name=name, grid=grid,
        in_specs=[pl.BlockSpec((m_total, tk), lambda i, j: (0, i)), pl.BlockSpec((m_total, tn), lambda i, j: (0, j))],
        out_specs=out_spec, out_shape=out_shape,
        compiler_params=_params("parallel", "parallel"),
    )(a, b)


def _ada_fwd(c_all, w_shard, b_shard):
    nb, ncol = c_all.shape[0], w_shard.shape[1]
    tn = 512

    def body(c_ref, w_ref, b_ref, o_ref):
        c = c_ref[...]
        o_ref[...] = _dot(c * _sigmoid(c), w_ref[...]) + b_ref[...]

    return pl.pallas_call(
        body, name="ada_fwd", grid=(ncol // tn,),
        in_specs=[pl.BlockSpec((nb, D_MODEL), lambda j: (0, 0)), pl.BlockSpec((D_MODEL, tn), lambda j: (0, j)),
                  pl.BlockSpec((1, tn), lambda j: (0, j))],
        out_specs=pl.BlockSpec((nb, tn), lambda j: (0, j)), out_shape=_sds((nb, ncol), F32),
        compiler_params=_params("parallel"),
    )(c_all, w_shard, b_shard)


def _adamw_math(g, w, m, v):
    m = ADAM_B1 * m + (1.0 - ADAM_B1) * g
    v = ADAM_B2 * v + (1.0 - ADAM_B2) * (g * g)
    m_hat = m / (1.0 - ADAM_B1 ** ADAM_STEP)
    v_hat = v / (1.0 - ADAM_B2 ** ADAM_STEP)
    delta = -ADAM_LR * (m_hat / (jnp.sqrt(v_hat) + ADAM_EPS) + ADAM_WD * w)
    return delta, m, v


def _ada_bwd_adamw(c_all, dmod_cols, w, m, v):
    nb, ncol = dmod_cols.shape
    tn = 256

    def body(c_ref, d_ref, w_ref, m_ref, v_ref, g_ref, dl_ref, nm_ref, nv_ref):
        c = c_ref[...]
        g = _dot(c * _sigmoid(c), d_ref[...], TN_DIMS)
        g_ref[...] = g
        dl_ref[...], nm_ref[...], nv_ref[...] = _adamw_math(g, w_ref[...], m_ref[...], v_ref[...])

    col = pl.BlockSpec((D_MODEL, tn), lambda j: (0, j))
    shp = _sds((D_MODEL, ncol), F32)
    return pl.pallas_call(
        body, name="ada_bwd_adamw", grid=(ncol // tn,),
        in_specs=[pl.BlockSpec((nb, D_MODEL), lambda j: (0, 0)), pl.BlockSpec((nb, tn), lambda j: (0, j)), col, col, col],
        out_specs=[col, col, col, col], out_shape=[shp, shp, shp, shp],
        compiler_params=_params("parallel"),
    )(c_all, dmod_cols, w, m, v)


def _adamw_halves(own, theirs, core, w, m, v, *, axis, name):
    r2, c2 = own.shape
    tr = min(r2, 256)
    nt = r2 // tr

    def body(core_ref, own_ref, their_ref, w_ref, m_ref, v_ref, g_ref, dl_ref, nm_ref, nv_ref):
        g = jnp.where(pl.program_id(0) == core_ref[0], own_ref[...], their_ref[...])
        g_ref[...] = g
        dl_ref[...], nm_ref[...], nv_ref[...] = _adamw_math(g, w_ref[...], m_ref[...], v_ref[...])

    if axis == 0:
        full = pl.BlockSpec((tr, c2), lambda h, i, core_ref: (h * nt + i, 0))
    else:
        full = pl.BlockSpec((tr, c2), lambda h, i, core_ref: (i, h))
    half = pl.BlockSpec((tr, c2), lambda h, i, core_ref: (i, 0))
    shp = _sds(w.shape, F32)
    return pl.pallas_call(
        body, name=name,
        grid_spec=pltpu.PrefetchScalarGridSpec(num_scalar_prefetch=1, grid=(2, nt), in_specs=[half, half, full, full, full],
                                               out_specs=[full] * 4),
        out_shape=[shp] * 4, compiler_params=_params("parallel", "parallel"),
    )(core, own, theirs, w, m, v)


def _tok_spec(tm, width=D_MODEL):
    return pl.BlockSpec((None, tm, width), lambda b, i: (b, i, 0))


def _row_spec(width=D_MODEL):
    return pl.BlockSpec((None, 1, width), lambda b, i: (b, 0, 0))


def _vec_spec(width=D_MODEL):
    return pl.BlockSpec((1, width), lambda b, i: (0, 0))


def _prenorm1(x, w, sc, sh, tm=512):
    bsz, seq, _ = x.shape

    def body(x_ref, w_ref, sc_ref, sh_ref, h_ref):
        y, _, _ = _rms_fwd(x_ref[...], w_ref[...])
        h_ref[...] = (y * (1.0 + sc_ref[...]) + sh_ref[...]).astype(BF16)

    return pl.pallas_call(
        body, name="prenorm1", grid=(bsz, seq // tm),
        in_specs=[_tok_spec(tm), _vec_spec(), _row_spec(), _row_spec()],
        out_specs=_tok_spec(tm), out_shape=_sds(x.shape, BF16),
        compiler_params=_params("parallel", "parallel"),
    )(x, w, sc, sh)


def _rope_tables(seq):
    half = ROPE_DIM // 2
    inv_freq = ROPE_THETA ** (-jnp.arange(0, ROPE_DIM, 2, dtype=F32) / ROPE_DIM)
    ang = jnp.arange(seq, dtype=F32)[:, None] * inv_freq[None, :]
    cos, sin = jnp.cos(ang), jnp.sin(ang)
    rest = ATT_HEAD_DIM - ROPE_DIM
    ones, zeros, zh = jnp.ones((seq, rest), F32), jnp.zeros((seq, rest), F32), jnp.zeros((seq, half), F32)
    reps = LANE // ATT_HEAD_DIM
    t_cos = jnp.tile(jnp.concatenate([cos, cos, ones], axis=1), (1, reps))
    t_up = jnp.tile(jnp.concatenate([zh, sin, zeros], axis=1), (1, reps))
    t_dn = jnp.tile(jnp.concatenate([-sin, zh, zeros], axis=1), (1, reps))
    return t_cos, t_up, t_dn


GROUP_ROWS = ATT_GROUP * WINDOW


def _rope_fwd(proj, tables):
    bsz, seq, _ = proj.shape
    nblk = seq // WINDOW
    half = ROPE_DIM // 2
    tm = 2 * WINDOW

    def body(p_ref, c_ref, u_ref, d_ref, q_ref, k_ref, v_ref):
        c, u, d = c_ref[...], u_ref[...], d_ref[...]

        def rope(x):
            return (x * c + pltpu.roll(x, half, 1) * u + pltpu.roll(x, LANE - half, 1) * d).astype(BF16)

        heads_per_slab = LANE // ATT_HEAD_DIM
        for s in range(ATT_WIDTH // LANE):
            slab = rope(p_ref[:, s * LANE:(s + 1) * LANE])
            for part in range(heads_per_slab):
                g, hh = divmod(s * heads_per_slab + part, ATT_GROUP)
                piece = slab[:, part * ATT_HEAD_DIM:(part + 1) * ATT_HEAD_DIM]
                for blk in range(tm // WINDOW):
                    q_ref[blk, g, hh * WINDOW:(hh + 1) * WINDOW, :] = piece[blk * WINDOW:(blk + 1) * WINDOW]
        rk = rope(p_ref[:, ATT_WIDTH:ATT_WIDTH + LANE])
        vv = p_ref[:, ATT_WIDTH + LANE:ATT_COLS].astype(BF16)
        for g in range(ATT_KV_HEADS):
            k_ref[g] = rk[:, g * ATT_HEAD_DIM:(g + 1) * ATT_HEAD_DIM]
            v_ref[g] = vv[:, g * ATT_HEAD_DIM:(g + 1) * ATT_HEAD_DIM]

    tab = pl.BlockSpec((tm, LANE), lambda b, i: (i, 0))
    kv_spec = pl.BlockSpec((None, ATT_KV_HEADS, tm, ATT_HEAD_DIM), lambda b, i: (b, 0, i, 0))
    kv_shape = _sds((bsz, ATT_KV_HEADS, seq, ATT_HEAD_DIM), BF16)
    return pl.pallas_call(
        body, name="rope_fwd", grid=(bsz, seq // tm),
        in_specs=[_tok_spec(tm, ATT_COLS), tab, tab, tab],
        out_specs=[pl.BlockSpec((None, tm // WINDOW, ATT_KV_HEADS, GROUP_ROWS, ATT_HEAD_DIM), lambda b, i: (b, i, 0, 0, 0)),
                   kv_spec, kv_spec],
        out_shape=[_sds((bsz, nblk, ATT_KV_HEADS, GROUP_ROWS, ATT_HEAD_DIM), BF16), kv_shape, kv_shape],
        compiler_params=_params("parallel", "parallel"),
    )(proj, *tables)


def _band_mask(i):
    row = lax.broadcasted_iota(jnp.int32, (GROUP_ROWS, 2 * WINDOW), 0) % WINDOW
    col = lax.broadcasted_iota(jnp.int32, (GROUP_ROWS, 2 * WINDOW), 1)
    prev = jnp.logical_and(jnp.logical_and(col < WINDOW, col > row), i > 0)
    return jnp.logical_or(prev, jnp.logical_and(col >= WINDOW, col - WINDOW <= row))


def _sink_column(sink_ref, g):
    head = lax.broadcasted_iota(jnp.int32, (GROUP_ROWS, 1), 0) // WINDOW
    col = jnp.full((GROUP_ROWS, 1), sink_ref[0, g * ATT_GROUP], F32)
    for hh in range(1, ATT_GROUP):
        col = jnp.where(head == hh, sink_ref[0, g * ATT_GROUP + hh], col)
    return col


def _attn_specs():
    q_spec = pl.BlockSpec((None, None, ATT_KV_HEADS, GROUP_ROWS, ATT_HEAD_DIM), lambda b, i: (b, i, 0, 0, 0))
    kv_cur = pl.BlockSpec((None, ATT_KV_HEADS, WINDOW, ATT_HEAD_DIM), lambda b, i: (b, 0, i, 0))
    kv_prev = pl.BlockSpec((None, ATT_KV_HEADS, WINDOW, ATT_HEAD_DIM), lambda b, i: (b, 0, jnp.maximum(i - 1, 0), 0))
    return q_spec, kv_cur, kv_prev


def _attn_fwd(qh, kh, vh, sinks, w_norm, comms=()):
    bsz, nblk = qh.shape[0], qh.shape[1]
    seq = nblk * WINDOW
    neg = float(jnp.finfo(jnp.float32).min)

    def body(sink_ref, q_ref, kc_ref, kp_ref, vc_ref, vp_ref, w_ref, raw_ref, an_ref, l_ref):
        mask = _band_mask(pl.program_id(1))
        for g in range(ATT_KV_HEADS):
            keys = jnp.concatenate([kp_ref[g], kc_ref[g]], axis=0)
            vals = jnp.concatenate([vp_ref[g], vc_ref[g]], axis=0)
            sink = _sink_column(sink_ref, g)
            s = jnp.where(mask, _dot(q_ref[g], keys, NT_DIMS) * ATT_SCALE, neg)
            m = jnp.maximum(jnp.max(s, axis=-1, keepdims=True), sink)
            p = jnp.where(mask, jnp.exp(s - m), 0.0)
            den = jnp.sum(p, axis=-1, keepdims=True) + jnp.exp(sink - m)
            o = _dot(p / den, vals)
            lse = m + jnp.log(den)
            for hh in range(ATT_GROUP):
                h = g * ATT_GROUP + hh
                raw_ref[:, h * ATT_HEAD_DIM:(h + 1) * ATT_HEAD_DIM] = o[hh * WINDOW:(hh + 1) * WINDOW]
                l_ref[:, h:h + 1] = lse[hh * WINDOW:(hh + 1) * WINDOW]
        y, _, _ = _rms_fwd(raw_ref[...], w_ref[...])
        an_ref[...] = y.astype(BF16)

    cur = lambda width: pl.BlockSpec((None, WINDOW, width), lambda b, i: (b, i, 0))
    q_spec, kv_cur, kv_prev = _attn_specs()
    return _call(
        body, (sinks, qh, kh, kh, vh, vh, w_norm), name="attn_fwd", grid=(bsz, nblk),
        in_specs=[pl.BlockSpec(memory_space=pltpu.SMEM), q_spec, kv_cur, kv_prev, kv_cur, kv_prev, _vec_spec(ATT_WIDTH)],
        out_specs=[cur(ATT_WIDTH), cur(ATT_WIDTH), cur(ATT_Q_HEADS)],
        out_shape=[_sds((bsz, seq, ATT_WIDTH), F32), _sds((bsz, seq, ATT_WIDTH), BF16), _sds((bsz, seq, ATT_Q_HEADS), F32)],
        sem=("parallel", "parallel"), comms=comms)


HG_Q0 = ATT_COLS // LANE
HG_F0 = HG_Q0 + HG_HEADS
HG_I0 = HG_F0 + HG_HEADS
HG_G0 = HG_I0 + HG_HEADS
HG_TOK = 256
HG_NCH = HG_TOK // HG_CHUNK


def _block_masks():
    row = lax.broadcasted_iota(jnp.int32, (HG_TOK, HG_TOK), 0)
    col = lax.broadcasted_iota(jnp.int32, (HG_TOK, HG_TOK), 1)
    same = (row // HG_CHUNK) == (col // HG_CHUNK)
    return jnp.logical_and(same, col <= row), jnp.logical_and(same, col >= row)


def _row_in_chunk():
    return lax.broadcasted_iota(jnp.int32, (HG_TOK, LANE), 0) % HG_CHUNK


def _chunk_cumsum(x, reverse=False):
    ric = _row_in_chunk()
    shift = 1
    while shift < HG_CHUNK:
        if reverse:
            x = x + jnp.where(ric < HG_CHUNK - shift, pltpu.roll(x, HG_TOK - shift, 0), 0.0)
        else:
            x = x + jnp.where(ric >= shift, pltpu.roll(x, shift, 0), 0.0)
        shift *= 2
    return x


def _chunk_rows(rows):
    stacked = jnp.concatenate([r[None] for r in rows], axis=0)
    return jnp.broadcast_to(stacked, (HG_NCH, HG_CHUNK, LANE)).reshape(HG_TOK, LANE)


def _chunk_slices(x):
    return [x[j * HG_CHUNK:(j + 1) * HG_CHUNK] for j in range(HG_NCH)]


def _hgrn_common(tbl, hf, hq):
    lb = _sigmoid(tbl[1:2] - tbl[0:1])
    sig = _sigmoid(hf)
    f = lb + (1.0 - lb) * sig
    sq = _sigmoid(hq)
    q, k = hq * sq, 1.0 - f
    b = _chunk_cumsum(jnp.log(f))
    last = [b[(j + 1) * HG_CHUNK - 1:(j + 1) * HG_CHUNK] for j in range(HG_NCH)]
    bl = _chunk_rows(last)
    e_b, e_nb, e_rem = jnp.exp(b), jnp.exp(-b), jnp.exp(bl - b)
    e_last = [jnp.exp(r) for r in last]
    return dict(lb=lb, sig=sig, f=f, sq=sq, q=q, k=k, e_b=e_b, e_nb=e_nb, e_rem=e_rem, e_last=e_last,
                qd=q * e_b, kd=k * e_nb, ku=k * e_rem)


def _hgrn_fwd(proj, lb_table, norm_w, comms=()):
    bsz, seq, _ = proj.shape
    nstep = seq // HG_TOK

    def body(tbl_ref, nw_ref, q_ref, f_ref, i_ref, g_ref, o_ref, rec_ref, st_ref, s_scr):
        @pl.when(pl.program_id(2) == 0)
        def _():
            s_scr[...] = jnp.zeros_like(s_scr)

        v, hg = i_ref[...], g_ref[...]
        t = _hgrn_common(tbl_ref[...], f_ref[...], q_ref[...])
        lower, _ = _block_masks()
        a = jnp.where(lower, _dot(t["qd"], t["kd"], NT_DIMS), 0.0)
        o_intra = _dot(a, v)
        v_c, ku_c, qd_c = _chunk_slices(v.astype(BF16)), _chunk_slices(t["ku"].astype(BF16)), _chunk_slices(t["qd"].astype(BF16))
        updates = [_dot(v_c[j], ku_c[j], TN_DIMS) for j in range(HG_NCH)]
        st = s_scr[...]
        states = []
        for j in range(HG_NCH):
            states.append(st)
            st = st * t["e_last"][j] + updates[j]
        s_scr[...] = st
        o = o_intra + jnp.concatenate([_dot(qd_c[j], states[j], NT_DIMS) for j in range(HG_NCH)], axis=0)
        for j in range(HG_NCH):
            st_ref[j] = states[j]
        o_ref[...] = o
        y, _, _ = _rms_fwd(o, nw_ref[...])
        rec_ref[...] = (y * (hg * _sigmoid(hg))).astype(BF16)

    slab = lambda first: pl.BlockSpec((None, HG_TOK, LANE), lambda b, h, t: (b, t, first + h))
    head_out = pl.BlockSpec((None, HG_TOK, LANE), lambda b, h, t: (b, t, h))
    return _call(
        body, (lb_table, norm_w, proj, proj, proj, proj), name="hgrn_fwd", grid=(bsz, HG_HEADS, nstep),
        in_specs=[pl.BlockSpec((2, LANE), lambda b, h, t: (0, h)), pl.BlockSpec((1, LANE), lambda b, h, t: (0, 0)),
                  slab(HG_Q0), slab(HG_F0), slab(HG_I0), slab(HG_G0)],
        out_specs=[head_out, head_out,
                   pl.BlockSpec((None, None, HG_NCH, LANE, LANE), lambda b, h, t: (b, h, t, 0, 0))],
        out_shape=[_sds((bsz, seq, HG_WIDTH), F32), _sds((bsz, seq, HG_WIDTH), BF16),
                   _sds((bsz, HG_HEADS, seq // HG_CHUNK, LANE, LANE), F32)],
        scratch_shapes=[pltpu.VMEM((LANE, LANE), F32)],
        sem=("parallel", "parallel", "arbitrary"), comms=comms)


def _mid_fwd(x, mix, post_w, g1, pre_w, sc2, sh2, tm=512):
    bsz, seq, _ = x.shape

    def body(x_ref, mix_ref, pw_ref, g1_ref, w2_ref, sc_ref, sh_ref, x1_ref, h2_ref):
        n1, _, _ = _rms_fwd(mix_ref[...], pw_ref[...])
        x1 = x_ref[...] + g1_ref[...] * n1
        x1_ref[...] = x1
        y2, _, _ = _rms_fwd(x1, w2_ref[...])
        h2_ref[...] = (y2 * (1.0 + sc_ref[...]) + sh_ref[...]).astype(BF16)

    return pl.pallas_call(
        body, name="mid_fwd", grid=(bsz, seq // tm),
        in_specs=[_tok_spec(tm), _tok_spec(tm), _vec_spec(), _row_spec(), _vec_spec(), _row_spec(), _row_spec()],
        out_specs=[_tok_spec(tm), _tok_spec(tm)], out_shape=[_sds(x.shape, F32), _sds(x.shape, BF16)],
        compiler_params=_params("parallel", "parallel"),
    )(x, mix, post_w, g1, pre_w, sc2, sh2)


def _acc_out(ref, first, value):
    @pl.when(first)
    def _():
        ref[...] = value

    @pl.when(jnp.logical_not(first))
    def _():
        ref[...] += value


def _loss_bwd(x1, down, post_w, g2, target, tm=512):
    bsz, seq, _ = x1.shape

    def body(x1_ref, d_ref, w_ref, g2_ref, t_ref, loss_ref, dy_ref, dd_ref, dg2_ref, dw_ref):
        b, i = pl.program_id(0), pl.program_id(1)
        w, g2v = w_ref[...], g2_ref[...]
        n2, dh, rstd = _rms_fwd(d_ref[...], w)
        err = x1_ref[...] + g2v * n2 - t_ref[...]
        part = (0.5 / D_MODEL) * jnp.sum(jnp.sum(err * err, axis=-1, keepdims=True), axis=0, keepdims=True)
        _acc_out(loss_ref, jnp.logical_and(b == 0, i == 0), jnp.broadcast_to(part, (1, LANE)))
        dy = err * (1.0 / D_MODEL)
        dy_ref[...] = dy
        _acc_out(dg2_ref, i == 0, _colsum(dy * n2))
        dd, dw_rows = _rms_bwd(dy * g2v, dh, rstd, w)
        dd_ref[...] = dd.astype(BF16)
        _acc_out(dw_ref, jnp.logical_and(b == 0, i == 0), _colsum(dw_rows))

    return pl.pallas_call(
        body, name="loss_bwd", grid=(bsz, seq // tm),
        in_specs=[_tok_spec(tm), _tok_spec(tm), _vec_spec(), _row_spec(), _tok_spec(tm)],
        out_specs=[_vec_spec(LANE), _tok_spec(tm), _tok_spec(tm), _row_spec(), _vec_spec()],
        out_shape=[_sds((1, LANE), F32), _sds(x1.shape, F32), _sds(x1.shape, BF16), _sds((bsz, 1, D_MODEL), F32),
                   _sds((1, D_MODEL), F32)],
        compiler_params=_params("arbitrary", "arbitrary"),
    )(x1, down, post_w, g2, target)


def _mid_bwd(dh2, dy, x1, mix, pre_w, sc2, post_w, g1, tm=512, comms=()):
    bsz, seq, _ = x1.shape

    def body(dh2_ref, dy_ref, x1_ref, mix_ref, w2_ref, sc_ref, pw_ref, g1_ref,
             dx1_ref, dmix_ref, dsc_ref, dsh_ref, dg1_ref, dw2_ref, dpw_ref):
        b, i = pl.program_id(0), pl.program_id(1)
        first = jnp.logical_and(b == 0, i == 0)
        w2, pw = w2_ref[...], pw_ref[...]
        dh2v = dh2_ref[...]
        y2, xh2, rstd2 = _rms_fwd(x1_ref[...], w2)
        _acc_out(dsh_ref, i == 0, _colsum(dh2v))
        _acc_out(dsc_ref, i == 0, _colsum(dh2v * y2))
        dx1n, dw_rows = _rms_bwd(dh2v * (1.0 + sc_ref[...]), xh2, rstd2, w2)
        _acc_out(dw2_ref, first, _colsum(dw_rows))
        dx1 = dy_ref[...] + dx1n
        dx1_ref[...] = dx1
        n1, mh, rstd1 = _rms_fwd(mix_ref[...], pw)
        _acc_out(dg1_ref, i == 0, _colsum(dx1 * n1))
        dmix, dpw_rows = _rms_bwd(dx1 * g1_ref[...], mh, rstd1, pw)
        dmix_ref[...] = dmix.astype(BF16)
        _acc_out(dpw_ref, first, _colsum(dpw_rows))

    row_shape = _sds((bsz, 1, D_MODEL), F32)
    vec_shape = _sds((1, D_MODEL), F32)
    return _call(
        body, (dh2, dy, x1, mix, pre_w, sc2, post_w, g1), name="mid_bwd", grid=(bsz, seq // tm),
        in_specs=[_tok_spec(tm), _tok_spec(tm), _tok_spec(tm), _tok_spec(tm), _vec_spec(), _row_spec(), _vec_spec(), _row_spec()],
        out_specs=[_tok_spec(tm), _tok_spec(tm), _row_spec(), _row_spec(), _row_spec(), _vec_spec(), _vec_spec()],
        out_shape=[_sds(x1.shape, F32), _sds(x1.shape, BF16), row_shape, row_shape, row_shape, vec_shape, vec_shape],
        sem=("arbitrary", "arbitrary"), comms=comms)


def _norm1_bwd(dh1, dx1, x, pre_w, sc1, tm=512):
    bsz, seq, _ = x.shape

    def body(dh_ref, dx1_ref, x_ref, w_ref, sc_ref, gx_ref, dsc_ref, dsh_ref, dw_ref):
        b, i = pl.program_id(0), pl.program_id(1)
        w = w_ref[...]
        dh = dh_ref[...]
        y, xh, rstd = _rms_fwd(x_ref[...], w)
        _acc_out(dsh_ref, i == 0, _colsum(dh))
        _acc_out(dsc_ref, i == 0, _colsum(dh * y))
        dx, dw_rows = _rms_bwd(dh * (1.0 + sc_ref[...]), xh, rstd, w)
        _acc_out(dw_ref, jnp.logical_and(b == 0, i == 0), _colsum(dw_rows))
        gx_ref[...] = dx1_ref[...] + dx

    row_shape = _sds((bsz, 1, D_MODEL), F32)
    return pl.pallas_call(
        body, name="norm1_bwd", grid=(bsz, seq // tm),
        in_specs=[_tok_spec(tm), _tok_spec(tm), _tok_spec(tm), _vec_spec(), _row_spec()],
        out_specs=[_tok_spec(tm), _row_spec(), _row_spec(), _vec_spec()],
        out_shape=[_sds(x.shape, F32), row_shape, row_shape, _sds((1, D_MODEL), F32)],
        compiler_params=_params("arbitrary", "arbitrary"),
    )(dh1, dx1, x, pre_w, sc1)


def _hgrn_bwd(dcat, proj, o_raw, states, lb_table, norm_w, comms=()):
    bsz, seq, _ = proj.shape
    nstep = seq // HG_TOK
    rec0 = ATT_WIDTH // LANE

    def body(tbl_ref, nw_ref, dr_ref, q_ref, f_ref, i_ref, g_ref, o_ref, st_ref,
             dq_ref, df_ref, di_ref, dg_ref, dlb_ref, dnw_ref, ds_scr):
        h, b, t = pl.program_id(0), pl.program_id(1), pl.program_id(2)

        @pl.when(t == 0)
        def _():
            ds_scr[...] = jnp.zeros_like(ds_scr)

        hq, v, hg = q_ref[...], i_ref[...], g_ref[...]
        nw = nw_ref[...]
        c = _hgrn_common(tbl_ref[...], f_ref[...], hq)
        qd, kd, ku = c["qd"], c["kd"], c["ku"]
        y, on, rstd = _rms_fwd(o_ref[...], nw)
        sg = _sigmoid(hg)
        dr = dr_ref[...]
        dg_ref[...] = (dr * y * (sg * (1.0 + hg * (1.0 - sg)))).astype(BF16)
        do, dnw_rows = _rms_bwd(dr * (hg * sg), on, rstd, nw)
        lower, upper = _block_masks()
        at = jnp.where(upper, _dot(kd, qd, NT_DIMS), 0.0)
        da = jnp.where(lower, _dot(do, v, NT_DIMS), 0.0)
        dat = jnp.where(upper, _dot(v, do, NT_DIMS), 0.0)
        dv = _dot(at, do)
        dqd = _dot(da, kd)
        dkd = _dot(dat, qd)
        do_c, qd_c, v_c, ku_c = [_chunk_slices(z.astype(BF16)) for z in (do, qd, v, ku)]
        outer = [_dot(do_c[j], qd_c[j], TN_DIMS) for j in range(HG_NCH)]
        ds = ds_scr[...]
        ds_after = [None] * HG_NCH
        for j in reversed(range(HG_NCH)):
            ds_after[j] = ds
            ds = outer[j] + ds * c["e_last"][j]
        ds_scr[...] = ds
        states = [st_ref[j] for j in range(HG_NCH)]
        dv = dv + jnp.concatenate([_dot(ku_c[j], ds_after[j], NT_DIMS) for j in range(HG_NCH)], axis=0)
        dqd = dqd + jnp.concatenate([_dot(do_c[j], states[j]) for j in range(HG_NCH)], axis=0)
        dku = jnp.concatenate([_dot(v_c[j], ds_after[j]) for j in range(HG_NCH)], axis=0)
        dku_ku = dku * ku
        dbl = [_colsum(states[j] * ds_after[j]) * c["e_last"][j] + _colsum(dku_ku[j * HG_CHUNK:(j + 1) * HG_CHUNK])
               for j in range(HG_NCH)]
        dk = dkd * c["e_nb"] + dku * c["e_rem"]
        db = dqd * qd - dkd * kd - dku_ku + jnp.where(_row_in_chunk() == HG_CHUNK - 1, _chunk_rows(dbl), 0.0)
        dfv = _chunk_cumsum(db, reverse=True) / c["f"] - dk
        sig, sq = c["sig"], c["sq"]
        df_ref[...] = (dfv * (1.0 - c["lb"]) * sig * (1.0 - sig)).astype(BF16)
        dq_ref[...] = (dqd * c["e_b"] * (sq * (1.0 + hq * (1.0 - sq)))).astype(BF16)
        di_ref[...] = dv.astype(BF16)
        _acc_out(dlb_ref, jnp.logical_and(b == 0, t == 0), _colsum(dfv * (1.0 - sig)))
        _acc_out(dnw_ref, jnp.logical_and(h == 0, jnp.logical_and(b == 0, t == 0)), _colsum(dnw_rows))

    rev = lambda t: nstep - 1 - t
    slab = lambda first: pl.BlockSpec((None, HG_TOK, LANE), lambda h, b, t: (b, rev(t), first + h))
    head = pl.BlockSpec((None, HG_TOK, LANE), lambda h, b, t: (b, rev(t), h))
    grad_shape = _sds((bsz, seq, HG_WIDTH), BF16)
    return _call(
        body, (lb_table, norm_w, dcat, proj, proj, proj, proj, o_raw, states), name="hgrn_bwd", grid=(HG_HEADS, bsz, nstep),
        in_specs=[pl.BlockSpec((2, LANE), lambda h, b, t: (0, h)), pl.BlockSpec((1, LANE), lambda h, b, t: (0, 0)),
                  slab(rec0), slab(HG_Q0), slab(HG_F0), slab(HG_I0), slab(HG_G0), head,
                  pl.BlockSpec((None, None, HG_NCH, LANE, LANE), lambda h, b, t: (b, h, rev(t), 0, 0))],
        out_specs=[head, head, head, head, pl.BlockSpec((1, LANE), lambda h, b, t: (0, h)),
                   pl.BlockSpec((1, LANE), lambda h, b, t: (0, 0))],
        out_shape=[grad_shape, grad_shape, grad_shape, grad_shape, _sds((1, HG_WIDTH), F32), _sds((1, LANE), F32)],
        scratch_shapes=[pltpu.VMEM((LANE, LANE), F32)],
        sem=("arbitrary", "arbitrary", "arbitrary"), comms=comms)


def _attn_bwd(dcat, raw, w_norm, qh, kh, vh, lse, sinks, comms=()):
    bsz, nblk = qh.shape[0], qh.shape[1]
    seq = nblk * WINDOW

    def body(sink_ref, da_ref, raw_ref, w_ref, q_ref, kc_ref, kp_ref, vc_ref, vp_ref, l_ref,
             dq_ref, dkd_ref, dkp_ref, dvd_ref, dvp_ref, dw_ref, dsink_ref):
        b, i = pl.program_id(0), pl.program_id(1)
        first = jnp.logical_and(b == 0, i == 0)
        mask = _band_mask(i)
        raw_v = raw_ref[...]
        w = w_ref[...]
        _, on, rstd = _rms_fwd(raw_v, w)
        do_all, dw_rows = _rms_bwd(da_ref[...], on, rstd, w)
        _acc_out(dw_ref, first, _colsum(dw_rows))
        lane8 = lax.broadcasted_iota(jnp.int32, (1, ATT_Q_HEADS), 1)
        dsink = jnp.zeros((1, ATT_Q_HEADS), F32)
        for g in range(ATT_KV_HEADS):
            gs = slice(g * ATT_HEAD_DIM, (g + 1) * ATT_HEAD_DIM)
            heads = [slice((g * ATT_GROUP + hh) * ATT_HEAD_DIM, (g * ATT_GROUP + hh + 1) * ATT_HEAD_DIM) for hh in range(ATT_GROUP)]
            q = q_ref[g]
            keys = jnp.concatenate([kp_ref[g], kc_ref[g]], axis=0)
            vals = jnp.concatenate([vp_ref[g], vc_ref[g]], axis=0)
            do_g = jnp.concatenate([do_all[:, hs] for hs in heads], axis=0)
            dsum = jnp.concatenate([jnp.sum(do_all[:, hs] * raw_v[:, hs], axis=-1, keepdims=True) for hs in heads], axis=0)
            lse_g = jnp.concatenate([l_ref[:, g * ATT_GROUP + hh:g * ATT_GROUP + hh + 1] for hh in range(ATT_GROUP)], axis=0)
            p = jnp.where(mask, jnp.exp(_dot(q, keys, NT_DIMS) * ATT_SCALE - lse_g), 0.0)
            sink_part = jnp.exp(_sink_column(sink_ref, g) - lse_g) * dsum
            for hh in range(ATT_GROUP):
                head_sum = jnp.sum(sink_part[hh * WINDOW:(hh + 1) * WINDOW], axis=0, keepdims=True)
                dsink = dsink - jnp.where(lane8 == g * ATT_GROUP + hh, head_sum, 0.0)
            ds = p * (_dot(do_g, vals, NT_DIMS) - dsum) * ATT_SCALE
            dq_g = _dot(ds, keys)
            for hh, hs in enumerate(heads):
                dq_ref[:, hs] = dq_g[hh * WINDOW:(hh + 1) * WINDOW]
            dk_g = _dot(ds, q, TN_DIMS)
            dv_g = _dot(p, do_g, TN_DIMS)
            dkp_ref[:, gs], dkd_ref[:, gs] = dk_g[:WINDOW], dk_g[WINDOW:]
            dvp_ref[:, gs], dvd_ref[:, gs] = dv_g[:WINDOW], dv_g[WINDOW:]
        _acc_out(dsink_ref, first, dsink)

    cur = lambda width: pl.BlockSpec((None, WINDOW, width), lambda b, i: (b, i, 0))
    q_spec, kv_cur, kv_prev = _attn_specs()
    kv_shape = _sds((bsz, seq, LANE), F32)
    return _call(
        body, (sinks, dcat, raw, w_norm, qh, kh, kh, vh, vh, lse), name="attn_bwd", grid=(bsz, nblk),
        in_specs=[pl.BlockSpec(memory_space=pltpu.SMEM), cur(ATT_WIDTH), cur(ATT_WIDTH), _vec_spec(ATT_WIDTH), q_spec,
                  kv_cur, kv_prev, kv_cur, kv_prev, cur(ATT_Q_HEADS)],
        out_specs=[cur(ATT_WIDTH), cur(LANE), cur(LANE), cur(LANE), cur(LANE), _vec_spec(ATT_WIDTH), _vec_spec(ATT_Q_HEADS)],
        out_shape=[_sds((bsz, seq, ATT_WIDTH), F32), kv_shape, kv_shape, kv_shape, kv_shape, _sds((1, ATT_WIDTH), F32),
                   _sds((1, ATT_Q_HEADS), F32)],
        sem=("arbitrary", "arbitrary"), comms=comms)


def _rope_bwd(dq, dkd, dkp, dvd, dvp, tables):
    bsz, seq, _ = dq.shape
    nblk = seq // WINDOW
    half = ROPE_DIM // 2

    def body(dq_ref, dkd_ref, dkp_ref, dvd_ref, dvp_ref, c_ref, u_ref, d_ref, o_ref):
        c, u, d = c_ref[...], u_ref[...], d_ref[...]
        has_next = pl.program_id(1) < nblk - 1

        def unrope(g):
            return g * c + pltpu.roll(g * u, LANE - half, 1) + pltpu.roll(g * d, half, 1)

        for s in range(ATT_WIDTH // LANE):
            o_ref[:, s * LANE:(s + 1) * LANE] = unrope(dq_ref[:, s * LANE:(s + 1) * LANE]).astype(BF16)
        dk = dkd_ref[...] + jnp.where(has_next, dkp_ref[...], 0.0)
        o_ref[:, ATT_WIDTH:ATT_WIDTH + LANE] = unrope(dk).astype(BF16)
        o_ref[:, ATT_WIDTH + LANE:ATT_COLS] = (dvd_ref[...] + jnp.where(has_next, dvp_ref[...], 0.0)).astype(BF16)

    cur = lambda width: pl.BlockSpec((None, WINDOW, width), lambda b, i: (b, i, 0))
    nxt = pl.BlockSpec((None, WINDOW, LANE), lambda b, i: (b, jnp.minimum(i + 1, nblk - 1), 0))
    tab = pl.BlockSpec((WINDOW, LANE), lambda b, i: (i, 0))
    return pl.pallas_call(
        body, name="rope_bwd", grid=(bsz, nblk),
        in_specs=[cur(ATT_WIDTH), cur(LANE), nxt, cur(LANE), nxt, tab, tab, tab],
        out_specs=cur(ATT_COLS), out_shape=_sds((bsz, seq, ATT_COLS), BF16),
        compiler_params=_params("parallel", "parallel"),
    )(dq, dkd, dkp, dvd, dvp, *tables)


def _other_chips(x, y):
    return [(1 - x, y), (x, 1 - y), (1 - x, 1 - y)]


def _sem_pair(n):
    return [pltpu.SemaphoreType.DMA((n,)), pltpu.SemaphoreType.DMA((n,))]


def _plan_chip_gather(blocks, bufs):
    n = len(blocks)

    def copies(ins, outs, sems):
        x, y, c = _mesh_pos()
        sends, lands = [], []
        for a in range(n):
            for j, chip in enumerate(_other_chips(x, y)):
                k = 3 * a + j
                sends.append(pltpu.make_async_remote_copy(
                    src_ref=ins[a], dst_ref=outs[a].at[4 * x + 2 * y + c], send_sem=sems[0].at[k], recv_sem=sems[1].at[k],
                    device_id=(*chip, c), device_id_type=MESH))
                slot = outs[a].at[4 * chip[0] + 2 * chip[1] + c]
                lands.append(pltpu.make_async_remote_copy(
                    src_ref=slot, dst_ref=slot, send_sem=sems[0].at[k], recv_sem=sems[1].at[k],
                    device_id=(*chip, c), device_id_type=MESH))
        return sends, lands

    def start(ins, outs, sems):
        for cp in copies(ins, outs, sems)[0]:
            cp.start()

    def finish(ins, outs, sems):
        sends, lands = copies(ins, outs, sems)
        for cp in lands:
            cp.wait_recv()
        for cp in sends:
            cp.wait_send()

    return _Comm(list(blocks) + list(bufs), [_sds(b.shape, b.dtype) for b in bufs], _sem_pair(3 * n), start, finish,
                 aliases=[(n + a, a) for a in range(n)])


def _plan_pair_forward(bufs):
    n = len(bufs)

    def copies(outs, sems):
        x, y, c = _mesh_pos()
        sends, lands = [], []
        for a in range(n):
            for j, chip in enumerate(_other_chips(x, y)):
                k = 3 * a + j
                slot = outs[a].at[4 * chip[0] + 2 * chip[1] + c]
                sends.append(pltpu.make_async_remote_copy(
                    src_ref=slot, dst_ref=slot, send_sem=sems[0].at[k], recv_sem=sems[1].at[k],
                    device_id=(x, y, 1 - c), device_id_type=MESH))
                theirs = outs[a].at[4 * chip[0] + 2 * chip[1] + 1 - c]
                lands.append(pltpu.make_async_remote_copy(
                    src_ref=theirs, dst_ref=theirs, send_sem=sems[0].at[k], recv_sem=sems[1].at[k],
                    device_id=(x, y, 1 - c), device_id_type=MESH))
        return sends, lands

    def start(ins, outs, sems):
        for cp in copies(outs, sems)[0]:
            cp.start()

    def finish(ins, outs, sems):
        sends, lands = copies(outs, sems)
        for cp in lands:
            cp.wait_recv()
        for cp in sends:
            cp.wait_send()

    return _Comm(list(bufs), [_sds(b.shape, b.dtype) for b in bufs], _sem_pair(3 * n), start, finish,
                 aliases=[(a, a) for a in range(n)])


def _plan_pair(arrays, other_half):
    n = len(arrays)

    def copies(ins, outs, sems):
        x, y, c = _mesh_pos()
        return [pltpu.make_async_remote_copy(
            src_ref=ins[a].at[1 - c] if other_half else ins[a], dst_ref=outs[a], send_sem=sems[0].at[a], recv_sem=sems[1].at[a],
            device_id=(x, y, 1 - c), device_id_type=MESH) for a in range(n)]

    def start(ins, outs, sems):
        for cp in copies(ins, outs, sems):
            cp.start()

    def finish(ins, outs, sems):
        for cp in copies(ins, outs, sems):
            cp.wait()

    shapes = [_sds(a.shape[1:] if other_half else a.shape, a.dtype) for a in arrays]
    return _Comm(list(arrays), shapes, _sem_pair(n), start, finish)


def _plan_chip_exchange(arrays):
    n = len(arrays)

    def copies(ins, outs, sems):
        x, y, c = _mesh_pos()
        sends, lands = [], []
        for a in range(n):
            for j, chip in enumerate(_other_chips(x, y)):
                k = 3 * a + j
                sends.append(pltpu.make_async_remote_copy(
                    src_ref=ins[a].at[2 * chip[0] + chip[1]], dst_ref=outs[a].at[2 * x + y], send_sem=sems[0].at[k],
                    recv_sem=sems[1].at[k], device_id=(*chip, c), device_id_type=MESH))
                slot = outs[a].at[2 * chip[0] + chip[1]]
                lands.append(pltpu.make_async_remote_copy(
                    src_ref=slot, dst_ref=slot, send_sem=sems[0].at[k], recv_sem=sems[1].at[k],
                    device_id=(*chip, c), device_id_type=MESH))
        return sends, lands

    def start(ins, outs, sems):
        for cp in copies(ins, outs, sems)[0]:
            cp.start()

    def finish(ins, outs, sems):
        sends, lands = copies(ins, outs, sems)
        for cp in lands:
            cp.wait_recv()
        for cp in sends:
            cp.wait_send()

    return _Comm(list(arrays), [_sds(a.shape, a.dtype) for a in arrays], _sem_pair(3 * n), start, finish)


def _comm_only(comms, name):
    return _call(lambda: None, (), name=name, grid=(), in_specs=[], out_specs=[], out_shape=[], sem=(), comms=comms)[1]


def _allgather8(arrays, name):
    return _comm_only([_plan_allgather8(arrays)], name)[0]


def _plan_allgather8(arrays):
    n = len(arrays)

    def parts(ins, outs, sems):
        send_sems, recv_sems, local_sems = sems
        x, y, c = _mesh_pos()
        me, sibling = (x, y, c), (x, y, 1 - c)
        chips = _other_chips(x, y)

        def copy(a, k, block, to, src=None):
            dst = outs[a].at[4 * block[0] + 2 * block[1] + block[2]]
            return pltpu.make_async_remote_copy(
                src_ref=dst if src is None else src, dst_ref=dst, send_sem=send_sems.at[7 * a + k],
                recv_sem=recv_sems.at[7 * a + k], device_id=to, device_id_type=MESH)

        mine = [pltpu.make_async_copy(ins[a], outs[a].at[4 * x + 2 * y + c], local_sems.at[a]) for a in range(n)]
        first = []
        for a in range(n):
            first.append(copy(a, 0, me, sibling, src=ins[a]))
            first += [copy(a, 1 + j, me, (*chip, c), src=ins[a]) for j, chip in enumerate(chips)]
        return copy, mine, first, me, sibling, chips, c

    def start(ins, outs, sems):
        _, mine, first, *_ = parts(ins, outs, sems)
        for cp in mine + first:
            cp.start()

    def finish(ins, outs, sems):
        copy, mine, first, me, sibling, chips, c = parts(ins, outs, sems)
        passed = []
        for j, chip in enumerate(chips):
            for a in range(n):
                copy(a, 1 + j, (*chip, c), me).wait_recv()
                fwd = copy(a, 4 + j, (*chip, c), sibling)
                fwd.start()
                passed.append(fwd)
        for a in range(n):
            copy(a, 0, sibling, me).wait_recv()
            for j, chip in enumerate(chips):
                copy(a, 4 + j, (*chip, 1 - c), me).wait_recv()
        for cp in first + passed:
            cp.wait_send()
        for cp in mine:
            cp.wait()

    sems = [pltpu.SemaphoreType.DMA((7 * n,)), pltpu.SemaphoreType.DMA((7 * n,)), pltpu.SemaphoreType.DMA((n,))]
    return _Comm(list(arrays), [_sds((N_DEV,) + a.shape, a.dtype) for a in arrays], sems, start, finish)


def _pair_sum(g, q, core, name):
    _, nblk, rows, cols = g.shape
    tr = min(rows, 256)

    def body(core_ref, g_ref, q_ref, o_ref):
        o_ref[...] = (g_ref[...] + q_ref[...]).astype(BF16)

    blk = pl.BlockSpec((None, tr, cols), lambda k, i, core_ref: (k, i, 0))
    return pl.pallas_call(
        body, name=name,
        grid_spec=pltpu.PrefetchScalarGridSpec(
            num_scalar_prefetch=1, grid=(nblk, rows // tr),
            in_specs=[pl.BlockSpec((None, None, tr, cols), lambda k, i, core_ref: (core_ref[0], k, i, 0)), blk],
            out_specs=blk),
        out_shape=_sds((nblk, rows, cols), BF16), compiler_params=_params("parallel", "parallel"),
    )(core, g, q)


def _sum_chips(own, landed, chip, name):
    _, rows, cols = own.shape
    tr = min(rows, 256)

    def body(chip_ref, own_ref, a_ref, b_ref, c_ref, o_ref):
        acc = own_ref[...].astype(F32) + a_ref[...].astype(F32)
        o_ref[...] = (acc + b_ref[...].astype(F32)) + c_ref[...].astype(F32)

    blk = lambda flip: pl.BlockSpec((None, tr, cols), lambda i, chip_ref: (jnp.bitwise_xor(chip_ref[0], flip), i, 0))
    return pl.pallas_call(
        body, name=name,
        grid_spec=pltpu.PrefetchScalarGridSpec(num_scalar_prefetch=1, grid=(rows // tr,), in_specs=[blk(0), blk(1), blk(2), blk(3)],
                                               out_specs=pl.BlockSpec((tr, cols), lambda i, chip_ref: (i, 0))),
        out_shape=_sds((rows, cols), F32), compiler_params=_params("parallel"),
    )(chip, own, landed, landed, landed)


SMALL_ROWS = 120
PACK_ROWS = 160


def _rows(a, nrows):
    flat = a.reshape(-1)
    return jnp.pad(flat, (0, nrows * LANE - flat.shape[0])).reshape(nrows, LANE)


def _pack_small(b_ada, pre_w_mix, post_w_mix, pre_w_mlp, post_w_mlp, attn_out_w, hg_norm_w, attn_sinks, lb_table):
    return jnp.concatenate([
        _rows(b_ada, 48), _rows(pre_w_mix, 8), _rows(post_w_mix, 8), _rows(pre_w_mlp, 8), _rows(post_w_mlp, 8),
        _rows(attn_out_w, 8), _rows(hg_norm_w, 8), _rows(attn_sinks, 8), _rows(lb_table[0], 8), _rows(lb_table[1], 8)], axis=0)


def _unpack_small(p):
    vec = lambda lo, n: p[lo:lo + n // LANE].reshape(1, n)
    lb = jnp.stack([p[104:108].reshape(HG_WIDTH), p[112:116].reshape(HG_WIDTH)])
    return dict(b_ada=vec(0, N_MOD * D_MODEL), pre_w_mix=vec(48, D_MODEL), post_w_mix=vec(56, D_MODEL), pre_w_mlp=vec(64, D_MODEL),
                post_w_mlp=vec(72, D_MODEL), attn_out_w=vec(80, ATT_WIDTH), hg_norm_w=p[88:89], attn_sinks=p[96:97, :ATT_Q_HEADS],
                lb_table=lb)


def _small_update(packs, w, m, v):
    def body(p_ref, w_ref, m_ref, v_ref, g_ref, dl_ref, nm_ref, nv_ref):
        tot = p_ref[0]
        for d in range(1, N_DEV):
            tot = tot + p_ref[d]
        wv = w_ref[...]
        p1 = _sigmoid(wv[112:120] - wv[104:112])
        s = tot[152:160] * p1 * (1.0 - p1)
        g = jnp.concatenate([tot[0:48] + tot[48:96], tot[96:152], -s, s], axis=0)
        g_ref[...] = g
        dl_ref[...], nm_ref[...], nv_ref[...] = _adamw_math(g, wv, m_ref[...], v_ref[...])

    shp = _sds((SMALL_ROWS, LANE), F32)
    return pl.pallas_call(body, name="small_update", out_shape=[shp] * 4, compiler_params=_params())(packs, w, m, v)


def kernel(x, c, w_ada, b_ada, pre_w_mix, w_in, attn_sinks, attn_out_w, lb_table, hg_norm_w, w_out, post_w_mix, pre_w_mlp, w_up, w_down, post_w_mlp, loss_target, m_w_ada, m_b_ada, m_pre_w_mix, m_w_in, m_attn_sinks, m_attn_out_w, m_lb_table, m_hg_norm_w, m_w_out, m_post_w_mix, m_pre_w_mlp, m_w_up, m_w_down, m_post_w_mlp, v_w_ada, v_b_ada, v_pre_w_mix, v_w_in, v_attn_sinks, v_attn_out_w, v_lb_table, v_hg_norm_w, v_w_out, v_post_w_mix, v_pre_w_mlp, v_w_up, v_w_down, v_post_w_mlp):
    xi, yi, ci = _mesh_pos()
    chip = 2 * xi + yi
    dev = 2 * chip + ci
    bsz, seq, _ = x.shape
    ntok = bsz * seq
    ada_cols = w_ada.shape[2]
    core = jnp.reshape(ci, (1,)).astype(jnp.int32)
    chip_idx = jnp.reshape(chip, (1,)).astype(jnp.int32)
    flat = lambda a: a.reshape(ntok, a.shape[-1])
    unflat = lambda a: a.reshape(bsz, seq, a.shape[-1])
    tables = _rope_tables(seq)

    def row_half(w):
        rows = w.shape[1] // 2
        return lax.dynamic_slice_in_dim(w[0], ci * rows, rows, axis=0).astype(BF16)

    def gather_buffer(w):
        rows, cols = w.shape[1] // 2, w.shape[2]
        own = w[0].astype(BF16).reshape(2, rows, cols)
        return lax.dynamic_update_slice(jnp.zeros((N_DEV, rows, cols), BF16), own, (2 * chip, 0, 0))

    c_g, in_g = _allgather8([c, row_half(w_in)], "gather_first")
    c_all = c_g.reshape(N_DEV * bsz, D_MODEL)
    w_in_full = in_g.reshape(N_CHIPS, D_MODEL, IN_COLS // N_CHIPS).transpose(1, 0, 2).reshape(D_MODEL, IN_COLS)

    b_cols = lax.dynamic_slice_in_dim(b_ada, chip * ada_cols, ada_cols, axis=1)
    mod_part = _ada_fwd(c_all, w_ada[0], b_cols)
    half_rows = mod_part.shape[0] // 2
    (mod_g,) = _allgather8([lax.dynamic_slice_in_dim(mod_part, ci * half_rows, half_rows, axis=0)], "gather_mod")
    mod_all = mod_g.reshape(N_CHIPS, 2, half_rows, ada_cols).transpose(1, 2, 0, 3).reshape(N_DEV * bsz, N_MOD * D_MODEL)
    mod = lax.dynamic_slice_in_dim(mod_all, dev * bsz, bsz, axis=0)
    sh1, sc1, g1, sh2, sc2, g2 = [mod[:, i * D_MODEL:(i + 1) * D_MODEL].reshape(bsz, 1, D_MODEL) for i in range(N_MOD)]

    h1 = _prenorm1(x, pre_w_mix, sc1, sh1)
    proj, ((out_g,),) = _mm(flat(h1), w_in_full, name="in_proj", out_dtype=F32,
                            comms=[_plan_chip_gather([row_half(w_out)], [gather_buffer(w_out)])])
    proj = unflat(proj)
    qh, kh, vh = _rope_fwd(proj, tables)
    (attn_raw, attn_n, lse), ((up_g,), (out_g,)) = _attn_fwd(
        qh, kh, vh, attn_sinks, attn_out_w,
        comms=[_plan_chip_gather([row_half(w_up)], [gather_buffer(w_up)]), _plan_pair_forward([out_g])])
    (o_raw, rec, states), ((down_g,), (up_g,)) = _hgrn_fwd(
        proj, lb_table, hg_norm_w,
        comms=[_plan_chip_gather([row_half(w_down)], [gather_buffer(w_down)]), _plan_pair_forward([up_g])])
    w_out_full = out_g.reshape(D_MODEL, D_MODEL)
    w_up4 = up_g.reshape(N_CHIPS, D_MODEL, D_MODEL)
    cat = jnp.concatenate([attn_n, rec], axis=-1)
    mix = unflat(_mm(flat(cat), w_out_full, name="out_proj", out_dtype=F32))
    x1, h2 = _mid_fwd(x, mix, post_w_mix, g1, pre_w_mlp, sc2, sh2)
    big_tm = min(ntok, 1024)
    up_spec = pl.BlockSpec((None, D_MODEL, D_MODEL), lambda i, j: (j, 0, 0))
    r, ((down_g,),) = _mm(flat(h2), w_up4, name="up_proj", out_dtype=BF16, tm=big_tm, tn=D_MODEL, n_out=D_FF, b_spec=up_spec,
                          epi=lambda acc: jnp.maximum(acc, 0.0), comms=[_plan_pair_forward([down_g])])
    w_down_full = down_g.reshape(D_FF, D_MODEL)
    square = lambda t: t * t
    down = unflat(_mm(r, w_down_full, name="down_proj", out_dtype=F32, a_fn=square))
    loss_row, dy, dd, dg2, d_post_mlp = _loss_bwd(x1, down, post_w_mlp, g2, loss_target)
    loss = lax.psum(loss_row[0, 0], ("x", "y", "c"))

    dpre = _mm(flat(dd), w_down_full, name="down_bwd", out_dtype=BF16, trans_b=True, tm=big_tm, tn=D_MODEL, extra=(r,),
               epi=lambda acc, rt: acc * (2.0 * rt.astype(F32)))
    half_rows = D_MODEL // 2
    g_down = _mm_tn(r, flat(dd), name="down_wgrad", tk=half_rows, tn=D_MODEL, a_fn=square,
                    out_shape=_sds((2, N_CHIPS, half_rows, D_MODEL), F32),
                    out_spec=pl.BlockSpec((None, None, half_rows, D_MODEL), lambda i, j: (i % 2, i // 2, 0, 0)))
    dh2, ((q_down,),) = _mm(dpre, w_up4, name="up_bwd", out_dtype=F32, trans_b=True, n_out=D_MODEL, b_chunks=N_CHIPS,
                            comms=[_plan_pair([g_down], True)])
    g_up = _mm_tn(flat(h2), dpre, name="up_wgrad", tk=D_MODEL, tn=half_rows,
                  out_shape=_sds((2, N_CHIPS, half_rows, D_MODEL), F32),
                  out_spec=pl.BlockSpec((2, None, half_rows, half_rows), lambda i, j: (0, j // 2, 0, j % 2)))
    s_down = _pair_sum(g_down, q_down, core, "pair_sum_down")
    (dx1, dmix, dsc2, dsh2, dg1, d_pre_mlp, d_post_mix), ((q_up,),) = _mid_bwd(
        unflat(dh2), dy, x1, mix, pre_w_mlp, sc2, post_w_mix, g1, comms=[_plan_pair([g_up], True)])
    s_up = _pair_sum(g_up, q_up, core, "pair_sum_up")

    dcat = unflat(_mm(flat(dmix), w_out_full, name="out_bwd", out_dtype=F32, trans_b=True))
    out_rows = D_MODEL // N_CHIPS
    g_out = _mm_tn(flat(cat), flat(dmix), name="out_wgrad", tk=out_rows, tn=half_rows,
                   out_shape=_sds((2, N_CHIPS, out_rows, half_rows), F32),
                   out_spec=pl.BlockSpec((None, None, out_rows, half_rows), lambda i, j: (j, i, 0, 0)))
    (dhq, dhf, dhi, dhg, d_lb, d_hg_norm), ((x_down, x_up), (q_out,)) = _hgrn_bwd(
        dcat, proj, o_raw, states, lb_table, hg_norm_w, comms=[_plan_chip_exchange([s_down, s_up]), _plan_pair([g_out], True)])
    half_down = _sum_chips(s_down, x_down, chip_idx, "sum_chips_down")
    half_up = _sum_chips(s_up, x_up, chip_idx, "sum_chips_up")
    s_out = _pair_sum(g_out, q_out, core, "pair_sum_out")
    (dq, dkd, dkp, dvd, dvp, d_attn_out, d_sinks), ((their_down, their_up), (x_out,)) = _attn_bwd(
        dcat, attn_raw, attn_out_w, qh, kh, vh, lse, attn_sinks,
        comms=[_plan_pair([half_down, half_up], False), _plan_chip_exchange([s_out])])
    half_out = _sum_chips(s_out, x_out, chip_idx, "sum_chips_out")
    dproj_a = _rope_bwd(dq, dkd, dkp, dvd, dvp, tables)
    dproj = flat(jnp.concatenate([dproj_a, dhq, dhf, dhi, dhg], axis=-1))
    g_in = _mm_tn(flat(h1), dproj, name="in_wgrad", tk=D_MODEL, tn=2 * LANE)
    g_in = g_in.reshape(2, half_rows, N_CHIPS, IN_COLS // N_CHIPS).transpose(0, 2, 1, 3)
    dh1, ((q_in,), (their_out,)) = _mm(dproj, w_in_full, name="in_bwd", out_dtype=F32, trans_b=True,
                                       comms=[_plan_pair([g_in], True), _plan_pair([half_out], False)])
    s_in = _pair_sum(g_in, q_in, core, "pair_sum_in")
    grad_x, dsc1, dsh1, d_pre_mix = _norm1_bwd(unflat(dh1), dx1, x, pre_w_mix, sc1)

    dmod = jnp.concatenate([dsh1, dsc1, dg1, dsh2, dsc2, dg2], axis=-1).reshape(bsz, N_MOD * D_MODEL)
    pack = jnp.concatenate([
        _rows(dmod, 96), _rows(d_pre_mix, 8), _rows(d_post_mix, 8), _rows(d_pre_mlp, 8), _rows(d_post_mlp, 8),
        _rows(d_attn_out, 8), _rows(d_hg_norm, 8), _rows(d_sinks, 8), _rows(d_lb, 8)], axis=0)
    (packs,), (x_in,) = _comm_only([_plan_allgather8([pack]), _plan_chip_exchange([s_in])], "gather_small")
    half_in = _sum_chips(s_in, x_in, chip_idx, "sum_chips_in")
    ((their_in,),) = _comm_only([_plan_pair([half_in], False)], "pair_swap_in")
    small_args = lambda pre: (pre["b_ada"], pre["pre_w_mix"], pre["post_w_mix"], pre["pre_w_mlp"], pre["post_w_mlp"],
                              pre["attn_out_w"], pre["hg_norm_w"], pre["attn_sinks"], pre["lb_table"])
    w_small = dict(b_ada=b_ada, pre_w_mix=pre_w_mix, post_w_mix=post_w_mix, pre_w_mlp=pre_w_mlp, post_w_mlp=post_w_mlp,
                   attn_out_w=attn_out_w, hg_norm_w=hg_norm_w, attn_sinks=attn_sinks, lb_table=lb_table)
    m_small = dict(b_ada=m_b_ada, pre_w_mix=m_pre_w_mix, post_w_mix=m_post_w_mix, pre_w_mlp=m_pre_w_mlp, post_w_mlp=m_post_w_mlp,
                   attn_out_w=m_attn_out_w, hg_norm_w=m_hg_norm_w, attn_sinks=m_attn_sinks, lb_table=m_lb_table)
    v_small = dict(b_ada=v_b_ada, pre_w_mix=v_pre_w_mix, post_w_mix=v_post_w_mix, pre_w_mlp=v_pre_w_mlp, post_w_mlp=v_post_w_mlp,
                   attn_out_w=v_attn_out_w, hg_norm_w=v_hg_norm_w, attn_sinks=v_attn_sinks, lb_table=v_lb_table)
    small_out = [_unpack_small(p) for p in _small_update(packs, _pack_small(*small_args(w_small)), _pack_small(*small_args(m_small)),
                                                         _pack_small(*small_args(v_small)))]

    dmod_all = packs[:, :96, :].reshape(N_DEV * bsz, N_MOD * D_MODEL)
    dmod_cols = lax.dynamic_slice_in_dim(dmod_all, chip * ada_cols, ada_cols, axis=1)
    ada_out = _ada_bwd_adamw(c_all, dmod_cols, w_ada[0], m_w_ada[0], v_w_ada[0])

    big = dict(
        w_in=tuple(_adamw_halves(half_in, their_in, core, w_in[0], m_w_in[0], v_w_in[0], axis=0, name="adamw_in")),
        w_up=tuple(_adamw_halves(half_up, their_up, core, w_up[0], m_w_up[0], v_w_up[0], axis=0, name="adamw_up")),
        w_out=tuple(_adamw_halves(half_out, their_out, core, w_out[0], m_w_out[0], v_w_out[0], axis=1, name="adamw_out")),
        w_down=tuple(_adamw_halves(half_down, their_down, core, w_down[0], m_w_down[0], v_w_down[0], axis=0, name="adamw_down")),
        w_ada=tuple(ada_out),
    )
    order = ("w_ada", "b_ada", "pre_w_mix", "w_in", "attn_sinks", "attn_out_w", "lb_table", "hg_norm_w", "w_out", "post_w_mix",
             "pre_w_mlp", "w_up", "w_down", "post_w_mlp")
    outs = [loss, grad_x]
    for kind in range(4):
        for nm in order:
            outs.append(big[nm][kind][None] if nm in big else small_out[kind][nm])
    return tuple(outs)
```

```python
import functools

import jax
import jax.numpy as jnp
from jax import lax
from jax.experimental import pallas as pl
from jax.experimental.pallas import tpu as pltpu

F32 = jnp.float32
BF16 = jnp.bfloat16

D_MODEL = 1024
ATT_WIDTH = 512
ATT_HEAD_DIM = 64
ATT_Q_HEADS = 8
ATT_KV_HEADS = 2
ATT_GROUP = ATT_Q_HEADS // ATT_KV_HEADS
ATT_KV_COLS = ATT_KV_HEADS * ATT_HEAD_DIM
WINDOW = 128
ROPE_DIM = 16
ROPE_THETA = 500000.0
HG_WIDTH = 512
MIX_WIDTH = ATT_WIDTH + HG_WIDTH
HG_HEAD_DIM = 128
HG_HEADS = 4
HG_CHUNK = 32
IN_COLS = ATT_WIDTH + 2 * ATT_KV_COLS + 4 * HG_WIDTH
ATT_COLS = ATT_WIDTH + 2 * ATT_KV_COLS
D_FF = 4 * D_MODEL
N_MOD = 6
EPS = 1e-6
ATT_SCALE = ATT_HEAD_DIM ** -0.5

ADAM_LR = 0.001
ADAM_B1 = 0.9
ADAM_B2 = 0.999
ADAM_EPS = 1e-08
ADAM_WD = 0.01
ADAM_STEP = 10

N_CHIPS = 4
N_DEV = 8
LANE = 128
VMEM_LIMIT = 48 * 1024 * 1024
MESH = pl.DeviceIdType.MESH

NT_DIMS = (((1,), (1,)), ((), ()))
TN_DIMS = (((0,), (0,)), ((), ()))


def _sds(shape, dtype):
    return jax.ShapeDtypeStruct(tuple(shape), dtype)


def _params(*sem):
    return pltpu.CompilerParams(dimension_semantics=sem, vmem_limit_bytes=VMEM_LIMIT)


def _sigmoid(x):
    return 1.0 / (1.0 + jnp.exp(-x))


def _dot(a, b, dims=None):
    a, b = a.astype(BF16), b.astype(BF16)
    if dims is None:
        return jnp.dot(a, b, preferred_element_type=F32)
    return lax.dot_general(a, b, dims, preferred_element_type=F32)


def _rms_fwd(x, w):
    rstd = lax.rsqrt(jnp.mean(x * x, axis=-1, keepdims=True) + EPS)
    xh = x * rstd
    return xh * w, xh, rstd


def _rms_bwd(dy, xh, rstd, w):
    dxh = dy * w
    dx = rstd * (dxh - xh * jnp.mean(dxh * xh, axis=-1, keepdims=True))
    return dx, dy * xh


def _colsum(x):
    return jnp.sum(x, axis=0, keepdims=True)


def _row_tile(rows, cap=256):
    return max(t for t in range(16, cap + 1, 16) if rows % t == 0)


HBM_SPEC = pl.BlockSpec(memory_space=pltpu.HBM)


def _mesh_pos():
    return lax.axis_index("x"), lax.axis_index("y"), lax.axis_index("c")


class _Comm:
    def __init__(self, ins, outs, sems, start, finish, aliases=()):
        self.ins, self.outs, self.sems = list(ins), list(outs), list(sems)
        self.start, self.finish, self.aliases = start, finish, tuple(aliases)


def _call(body, args, *, name, grid, in_specs, out_specs, out_shape, sem, scratch_shapes=(), comms=(), aliases=None):
    scratch_shapes = list(scratch_shapes)
    if not comms:
        return pl.pallas_call(body, name=name, grid=grid, in_specs=in_specs, out_specs=out_specs, out_shape=out_shape,
                              input_output_aliases=dict(aliases or {}), scratch_shapes=scratch_shapes,
                              compiler_params=_params(*sem))(*args)
    single = not isinstance(out_shape, (list, tuple))
    out_specs_l = [out_specs] if single else list(out_specs)
    out_shape_l = [out_shape] if single else list(out_shape)
    n_in, n_out, n_scr = len(in_specs), len(out_shape_l), len(scratch_shapes)
    n_ci = [len(cm.ins) for cm in comms]
    n_co = [len(cm.outs) for cm in comms]
    n_cs = [len(cm.sems) for cm in comms]
    aliases = dict(aliases or {})
    for k, cm in enumerate(comms):
        for i, o in cm.aliases:
            aliases[n_in + sum(n_ci[:k]) + i] = n_out + sum(n_co[:k]) + o

    def fused(*refs):
        pos = [0]

        def take(n):
            part = refs[pos[0]:pos[0] + n]
            pos[0] += n
            return part

        ins = take(n_in)
        c_ins = [take(n) for n in n_ci]
        outs = take(n_out)
        c_outs = [take(n) for n in n_co]
        scr = take(n_scr)
        c_sems = [take(n) for n in n_cs]
        first, last = True, True
        for d, size in enumerate(grid):
            first = jnp.logical_and(first, pl.program_id(d) == 0)
            last = jnp.logical_and(last, pl.program_id(d) == size - 1)

        def run(which):
            for cm, ci, co, cs in zip(comms, c_ins, c_outs, c_sems):
                getattr(cm, which)(ci, co, cs)

        if grid:
            pl.when(first)(lambda: run("start"))
        else:
            run("start")
        body(*ins, *outs, *scr)
        if grid:
            pl.when(last)(lambda: run("finish"))
        else:
            run("finish")

    res = pl.pallas_call(
        fused, name=name, grid=grid, in_specs=list(in_specs) + [HBM_SPEC] * sum(n_ci),
        out_specs=out_specs_l + [HBM_SPEC] * sum(n_co), out_shape=out_shape_l + [s for cm in comms for s in cm.outs],
        input_output_aliases=aliases, scratch_shapes=scratch_shapes + [s for cm in comms for s in cm.sems],
        compiler_params=_params(*["arbitrary"] * len(grid)),
    )(*args, *[a for cm in comms for a in cm.ins])
    main = res[:n_out]
    extra, at = [], n_out
    for n in n_co:
        extra.append(list(res[at:at + n]))
        at += n
    return (main[0] if single else list(main)), extra


def _mm(a, b, *, name, out_dtype, trans_b=False, tm=512, tn=None, a_fn=None, extra=(), epi=None,
        b_spec=None, n_out=None, b_chunks=1, comms=()):
    m_total, k_total = a.shape
    if n_out is None:
        n_out = b.shape[0] if trans_b else b.shape[1]
    tn = n_out if tn is None else tn
    grid = (m_total // tm, n_out // tn)
    dims = NT_DIMS if trans_b else None
    kc = k_total // b_chunks

    def body(*refs):
        a_ref, b_ref = refs[0], refs[1]
        extra_refs = refs[2:2 + len(extra)]
        o_ref = refs[2 + len(extra)]
        if b_chunks == 1:
            av = a_ref[...]
            acc = _dot(av if a_fn is None else a_fn(av), b_ref[...], dims)
        else:
            acc = _dot(a_ref[:, 0:kc], b_ref[0], NT_DIMS)
            for k in range(1, b_chunks):
                acc = acc + _dot(a_ref[:, k * kc:(k + 1) * kc], b_ref[k], NT_DIMS)
        if epi is not None:
            acc = epi(acc, *[r[...] for r in extra_refs])
        o_ref[...] = acc.astype(out_dtype)

    if b_spec is None:
        if b_chunks > 1:
            b_spec = pl.BlockSpec((b_chunks, tn, kc), lambda i, j: (0, j, 0))
        elif trans_b:
            b_spec = pl.BlockSpec((tn, k_total), lambda i, j: (j, 0))
        else:
            b_spec = pl.BlockSpec((k_total, tn), lambda i, j: (0, j))
    in_specs = [pl.BlockSpec((tm, k_total), lambda i, j: (i, 0)), b_spec]
    in_specs += [pl.BlockSpec((tm, tn), lambda i, j: (i, j)) for _ in extra]
    return _call(
        body, (a, b, *extra), name=name, grid=grid, in_specs=in_specs,
        out_specs=pl.BlockSpec((tm, tn), lambda i, j: (i, j)),
        out_shape=_sds((m_total, n_out), out_dtype),
        sem=("parallel", "parallel"), comms=comms)


def _mm_tn(a, b, *, name, tk, tn, a_fn=None, out_shape=None, out_spec=None):
    m_total, k_total = a.shape
    n_total = b.shape[1]
    grid = (k_total // tk, n_total // tn)

    def body(a_ref, b_ref, o_ref):
        av = a_ref[...]
        part = _dot(av if a_fn is None else a_fn(av), b_ref[...], TN_DIMS)
        o_ref[...] = part.reshape(o_ref.shape)

    if out_shape is None:
        out_shape = _sds((k_total, n_total), F32)
        out_spec = pl.BlockSpec((tk, tn), lambda i, j: (i, j))
    return pl.pallas_call(
        body, name=name, grid=grid,
        in_specs=[pl.BlockSpec((m_total, tk), lambda i, j: (0, i)), pl.BlockSpec((m_total, tn), lambda i, j: (0, j))],
        out_specs=out_spec, out_shape=out_shape,
        compiler_params=_params("parallel", "parallel"),
    )(a, b)


def _ada_fwd(c_all, w_shard, b_shard):
    nb, ncol = c_all.shape[0], w_shard.shape[1]
    tn = 512

    def body(c_ref, w_ref, b_ref, o_ref):
        c = c_ref[...]
        o_ref[...] = _dot(c * _sigmoid(c), w_ref[...]) + b_ref[...]

    return pl.pallas_call(
        body, name="ada_fwd", grid=(ncol // tn,),
        in_specs=[pl.BlockSpec((nb, D_MODEL), lambda j: (0, 0)), pl.BlockSpec((D_MODEL, tn), lambda j: (0, j)),
                  pl.BlockSpec((1, tn), lambda j: (0, j))],
        out_specs=pl.BlockSpec((nb, tn), lambda j: (0, j)), out_shape=_sds((nb, ncol), F32),
        compiler_params=_params("parallel"),
    )(c_all, w_shard, b_shard)


def _adamw_math(g, w, m, v):
    m = ADAM_B1 * m + (1.0 - ADAM_B1) * g
    v = ADAM_B2 * v + (1.0 - ADAM_B2) * (g * g)
    m_hat = m / (1.0 - ADAM_B1 ** ADAM_STEP)
    v_hat = v / (1.0 - ADAM_B2 ** ADAM_STEP)
    delta = -ADAM_LR * (m_hat / (jnp.sqrt(v_hat) + ADAM_EPS) + ADAM_WD * w)
    return delta, m, v


def _ada_bwd_adamw(c_all, dmod_cols, w, m, v):
    nb, ncol = dmod_cols.shape
    tn = 256

    def body(c_ref, d_ref, w_ref, m_ref, v_ref, g_ref, dl_ref, nm_ref, nv_ref):
        c = c_ref[...]
        g = _dot(c * _sigmoid(c), d_ref[...], TN_DIMS)
        g_ref[...] = g
        dl_ref[...], nm_ref[...], nv_ref[...] = _adamw_math(g, w_ref[...], m_ref[...], v_ref[...])

    col = pl.BlockSpec((D_MODEL, tn), lambda j: (0, j))
    shp = _sds((D_MODEL, ncol), F32)
    return pl.pallas_call(
        body, name="ada_bwd_adamw", grid=(ncol // tn,),
        in_specs=[pl.BlockSpec((nb, D_MODEL), lambda j: (0, 0)), pl.BlockSpec((nb, tn), lambda j: (0, j)), col, col, col],
        out_specs=[col, col, col, col], out_shape=[shp, shp, shp, shp],
        compiler_params=_params("parallel"),
    )(c_all, dmod_cols, w, m, v)


def _adamw_halves(own, theirs, core, w, m, v, *, axis, name):
    r2, c2 = own.shape
    tr = _row_tile(r2)
    nt = r2 // tr

    def body(core_ref, own_ref, their_ref, w_ref, m_ref, v_ref, g_ref, dl_ref, nm_ref, nv_ref):
        g = jnp.where(pl.program_id(0) == core_ref[0], own_ref[...], their_ref[...])
        g_ref[...] = g
        dl_ref[...], nm_ref[...], nv_ref[...] = _adamw_math(g, w_ref[...], m_ref[...], v_ref[...])

    if axis == 0:
        full = pl.BlockSpec((tr, c2), lambda h, i, core_ref: (h * nt + i, 0))
    else:
        full = pl.BlockSpec((tr, c2), lambda h, i, core_ref: (i, h))
    half = pl.BlockSpec((tr, c2), lambda h, i, core_ref: (i, 0))
    shp = _sds(w.shape, F32)
    return pl.pallas_call(
        body, name=name,
        grid_spec=pltpu.PrefetchScalarGridSpec(num_scalar_prefetch=1, grid=(2, nt), in_specs=[half, half, full, full, full],
                                               out_specs=[full] * 4),
        out_shape=[shp] * 4, compiler_params=_params("parallel", "parallel"),
    )(core, own, theirs, w, m, v)


def _tok_spec(tm, width=D_MODEL):
    return pl.BlockSpec((None, tm, width), lambda b, i: (b, i, 0))


def _row_spec(width=D_MODEL):
    return pl.BlockSpec((None, 1, width), lambda b, i: (b, 0, 0))


def _vec_spec(width=D_MODEL):
    return pl.BlockSpec((1, width), lambda b, i: (0, 0))


def _prenorm1(x, w, sc, sh, tm=512):
    bsz, seq, _ = x.shape

    def body(x_ref, w_ref, sc_ref, sh_ref, h_ref):
        y, _, _ = _rms_fwd(x_ref[...], w_ref[...])
        h_ref[...] = (y * (1.0 + sc_ref[...]) + sh_ref[...]).astype(BF16)

    return pl.pallas_call(
        body, name="prenorm1", grid=(bsz, seq // tm),
        in_specs=[_tok_spec(tm), _vec_spec(), _row_spec(), _row_spec()],
        out_specs=_tok_spec(tm), out_shape=_sds(x.shape, BF16),
        compiler_params=_params("parallel", "parallel"),
    )(x, w, sc, sh)


def _rope_tables(seq):
    half = ROPE_DIM // 2
    inv_freq = ROPE_THETA ** (-jnp.arange(0, ROPE_DIM, 2, dtype=F32) / ROPE_DIM)
    ang = jnp.arange(seq, dtype=F32)[:, None] * inv_freq[None, :]
    cos, sin = jnp.cos(ang), jnp.sin(ang)
    rest = ATT_HEAD_DIM - ROPE_DIM
    ones, zeros, zh = jnp.ones((seq, rest), F32), jnp.zeros((seq, rest), F32), jnp.zeros((seq, half), F32)
    reps = LANE // ATT_HEAD_DIM
    t_cos = jnp.tile(jnp.concatenate([cos, cos, ones], axis=1), (1, reps))
    t_up = jnp.tile(jnp.concatenate([zh, sin, zeros], axis=1), (1, reps))
    t_dn = jnp.tile(jnp.concatenate([-sin, zh, zeros], axis=1), (1, reps))
    return t_cos, t_up, t_dn


GROUP_ROWS = ATT_GROUP * WINDOW


def _rope_fwd(proj, tables):
    bsz, seq, _ = proj.shape
    nblk = seq // WINDOW
    half = ROPE_DIM // 2
    tm = 2 * WINDOW

    def body(p_ref, c_ref, u_ref, d_ref, q_ref, k_ref, v_ref):
        c, u, d = c_ref[...], u_ref[...], d_ref[...]

        def rope(x):
            return (x * c + pltpu.roll(x, half, 1) * u + pltpu.roll(x, LANE - half, 1) * d).astype(BF16)

        heads_per_slab = LANE // ATT_HEAD_DIM
        for s in range(ATT_WIDTH // LANE):
            slab = rope(p_ref[:, s * LANE:(s + 1) * LANE])
            for part in range(heads_per_slab):
                g, hh = divmod(s * heads_per_slab + part, ATT_GROUP)
                piece = slab[:, part * ATT_HEAD_DIM:(part + 1) * ATT_HEAD_DIM]
                for blk in range(tm // WINDOW):
                    q_ref[blk, g, hh * WINDOW:(hh + 1) * WINDOW, :] = piece[blk * WINDOW:(blk + 1) * WINDOW]
        rk = rope(p_ref[:, ATT_WIDTH:ATT_WIDTH + LANE])
        vv = p_ref[:, ATT_WIDTH + LANE:ATT_COLS].astype(BF16)
        for g in range(ATT_KV_HEADS):
            k_ref[g] = rk[:, g * ATT_HEAD_DIM:(g + 1) * ATT_HEAD_DIM]
            v_ref[g] = vv[:, g * ATT_HEAD_DIM:(g + 1) * ATT_HEAD_DIM]

    tab = pl.BlockSpec((tm, LANE), lambda b, i: (i, 0))
    kv_spec = pl.BlockSpec((None, ATT_KV_HEADS, tm, ATT_HEAD_DIM), lambda b, i: (b, 0, i, 0))
    kv_shape = _sds((bsz, ATT_KV_HEADS, seq, ATT_HEAD_DIM), BF16)
    return pl.pallas_call(
        body, name="rope_fwd", grid=(bsz, seq // tm),
        in_specs=[_tok_spec(tm, ATT_COLS), tab, tab, tab],
        out_specs=[pl.BlockSpec((None, tm // WINDOW, ATT_KV_HEADS, GROUP_ROWS, ATT_HEAD_DIM), lambda b, i: (b, i, 0, 0, 0)),
                   kv_spec, kv_spec],
        out_shape=[_sds((bsz, nblk, ATT_KV_HEADS, GROUP_ROWS, ATT_HEAD_DIM), BF16), kv_shape, kv_shape],
        compiler_params=_params("parallel", "parallel"),
    )(proj, *tables)


def _band_mask(i):
    row = lax.broadcasted_iota(jnp.int32, (GROUP_ROWS, 2 * WINDOW), 0) % WINDOW
    col = lax.broadcasted_iota(jnp.int32, (GROUP_ROWS, 2 * WINDOW), 1)
    prev = jnp.logical_and(jnp.logical_and(col < WINDOW, col > row), i > 0)
    return jnp.logical_or(prev, jnp.logical_and(col >= WINDOW, col - WINDOW <= row))


def _sink_column(sink_ref, g):
    head = lax.broadcasted_iota(jnp.int32, (GROUP_ROWS, 1), 0) // WINDOW
    col = jnp.full((GROUP_ROWS, 1), sink_ref[0, g * ATT_GROUP], F32)
    for hh in range(1, ATT_GROUP):
        col = jnp.where(head == hh, sink_ref[0, g * ATT_GROUP + hh], col)
    return col


def _attn_specs():
    q_spec = pl.BlockSpec((None, None, ATT_KV_HEADS, GROUP_ROWS, ATT_HEAD_DIM), lambda b, i: (b, i, 0, 0, 0))
    kv_cur = pl.BlockSpec((None, ATT_KV_HEADS, WINDOW, ATT_HEAD_DIM), lambda b, i: (b, 0, i, 0))
    kv_prev = pl.BlockSpec((None, ATT_KV_HEADS, WINDOW, ATT_HEAD_DIM), lambda b, i: (b, 0, jnp.maximum(i - 1, 0), 0))
    return q_spec, kv_cur, kv_prev


def _attn_fwd(qh, kh, vh, sinks, w_norm, comms=()):
    bsz, nblk = qh.shape[0], qh.shape[1]
    seq = nblk * WINDOW
    neg = float(jnp.finfo(jnp.float32).min)

    def body(sink_ref, q_ref, kc_ref, kp_ref, vc_ref, vp_ref, w_ref, raw_ref, an_ref, l_ref):
        mask = _band_mask(pl.program_id(1))
        for g in range(ATT_KV_HEADS):
            keys = jnp.concatenate([kp_ref[g], kc_ref[g]], axis=0)
            vals = jnp.concatenate([vp_ref[g], vc_ref[g]], axis=0)
            sink = _sink_column(sink_ref, g)
            s = jnp.where(mask, _dot(q_ref[g], keys, NT_DIMS) * ATT_SCALE, neg)
            m = jnp.maximum(jnp.max(s, axis=-1, keepdims=True), sink)
            p = jnp.where(mask, jnp.exp(s - m), 0.0)
            den = jnp.sum(p, axis=-1, keepdims=True) + jnp.exp(sink - m)
            o = _dot(p / den, vals)
            lse = m + jnp.log(den)
            for hh in range(ATT_GROUP):
                h = g * ATT_GROUP + hh
                raw_ref[:, h * ATT_HEAD_DIM:(h + 1) * ATT_HEAD_DIM] = o[hh * WINDOW:(hh + 1) * WINDOW]
                l_ref[:, h:h + 1] = lse[hh * WINDOW:(hh + 1) * WINDOW]
        y, _, _ = _rms_fwd(raw_ref[...], w_ref[...])
        an_ref[...] = y.astype(BF16)

    cur = lambda width: pl.BlockSpec((None, WINDOW, width), lambda b, i: (b, i, 0))
    q_spec, kv_cur, kv_prev = _attn_specs()
    return _call(
        body, (sinks, qh, kh, kh, vh, vh, w_norm), name="attn_fwd", grid=(bsz, nblk),
        in_specs=[pl.BlockSpec(memory_space=pltpu.SMEM), q_spec, kv_cur, kv_prev, kv_cur, kv_prev, _vec_spec(ATT_WIDTH)],
        out_specs=[cur(ATT_WIDTH), cur(ATT_WIDTH), cur(ATT_Q_HEADS)],
        out_shape=[_sds((bsz, seq, ATT_WIDTH), F32), _sds((bsz, seq, MIX_WIDTH), BF16), _sds((bsz, seq, ATT_Q_HEADS), F32)],
        sem=("parallel", "parallel"), comms=comms)


HG_Q0 = ATT_COLS // LANE
HG_F0 = HG_Q0 + HG_HEADS
HG_I0 = HG_F0 + HG_HEADS
HG_G0 = HG_I0 + HG_HEADS
HG_TOK = 256
HG_NCH = HG_TOK // HG_CHUNK


def _block_masks():
    row = lax.broadcasted_iota(jnp.int32, (HG_TOK, HG_TOK), 0)
    col = lax.broadcasted_iota(jnp.int32, (HG_TOK, HG_TOK), 1)
    same = (row // HG_CHUNK) == (col // HG_CHUNK)
    return jnp.logical_and(same, col <= row), jnp.logical_and(same, col >= row)


def _row_in_chunk():
    return lax.broadcasted_iota(jnp.int32, (HG_TOK, LANE), 0) % HG_CHUNK


def _chunk_cumsum(x, reverse=False):
    ric = _row_in_chunk()
    shift = 1
    while shift < HG_CHUNK:
        if reverse:
            x = x + jnp.where(ric < HG_CHUNK - shift, pltpu.roll(x, HG_TOK - shift, 0), 0.0)
        else:
            x = x + jnp.where(ric >= shift, pltpu.roll(x, shift, 0), 0.0)
        shift *= 2
    return x


def _chunk_rows(rows):
    stacked = jnp.concatenate([r[None] for r in rows], axis=0)
    return jnp.broadcast_to(stacked, (HG_NCH, HG_CHUNK, LANE)).reshape(HG_TOK, LANE)


def _chunk_slices(x):
    return [x[j * HG_CHUNK:(j + 1) * HG_CHUNK] for j in range(HG_NCH)]


def _hgrn_common(tbl, hf, hq):
    lb = _sigmoid(tbl[1:2] - tbl[0:1])
    sig = _sigmoid(hf)
    f = lb + (1.0 - lb) * sig
    sq = _sigmoid(hq)
    q, k = hq * sq, 1.0 - f
    b = _chunk_cumsum(jnp.log(f))
    last = [b[(j + 1) * HG_CHUNK - 1:(j + 1) * HG_CHUNK] for j in range(HG_NCH)]
    bl = _chunk_rows(last)
    e_b, e_nb, e_rem = jnp.exp(b), jnp.exp(-b), jnp.exp(bl - b)
    e_last = [jnp.exp(r) for r in last]
    return dict(lb=lb, sig=sig, f=f, sq=sq, q=q, k=k, e_b=e_b, e_nb=e_nb, e_rem=e_rem, e_last=e_last,
                qd=q * e_b, kd=k * e_nb, ku=k * e_rem)


def _hgrn_fwd(proj, lb_table, norm_w, mix_in, comms=()):
    bsz, seq, _ = proj.shape
    nstep = seq // HG_TOK

    def body(tbl_ref, nw_ref, q_ref, f_ref, i_ref, g_ref, mix_ref, o_ref, rec_ref, st_ref, s_scr):
        @pl.when(pl.program_id(2) == 0)
        def _():
            s_scr[...] = jnp.zeros_like(s_scr)

        v, hg = i_ref[...], g_ref[...]
        t = _hgrn_common(tbl_ref[...], f_ref[...], q_ref[...])
        lower, _ = _block_masks()
        a = jnp.where(lower, _dot(t["qd"], t["kd"], NT_DIMS), 0.0)
        o_intra = _dot(a, v)
        v_c, ku_c, qd_c = _chunk_slices(v.astype(BF16)), _chunk_slices(t["ku"].astype(BF16)), _chunk_slices(t["qd"].astype(BF16))
        updates = [_dot(v_c[j], ku_c[j], TN_DIMS) for j in range(HG_NCH)]
        st = s_scr[...]
        states = []
        for j in range(HG_NCH):
            states.append(st)
            st = st * t["e_last"][j] + updates[j]
        s_scr[...] = st
        o = o_intra + jnp.concatenate([_dot(qd_c[j], states[j], NT_DIMS) for j in range(HG_NCH)], axis=0)
        for j in range(HG_NCH):
            st_ref[j] = states[j]
        o_ref[...] = o
        y, _, _ = _rms_fwd(o, nw_ref[...])
        rec_ref[...] = (y * (hg * _sigmoid(hg))).astype(BF16)

    slab = lambda first: pl.BlockSpec((None, HG_TOK, LANE), lambda b, h, t: (b, t, first + h))
    head_out = pl.BlockSpec((None, HG_TOK, LANE), lambda b, h, t: (b, t, h))
    mix_out = pl.BlockSpec((None, HG_TOK, LANE), lambda b, h, t: (b, t, ATT_WIDTH // LANE + h))
    return _call(
        body, (lb_table, norm_w, proj, proj, proj, proj, mix_in), name="hgrn_fwd", grid=(bsz, HG_HEADS, nstep),
        in_specs=[pl.BlockSpec((2, LANE), lambda b, h, t: (0, h)), pl.BlockSpec((1, LANE), lambda b, h, t: (0, 0)),
                  slab(HG_Q0), slab(HG_F0), slab(HG_I0), slab(HG_G0), pl.BlockSpec(memory_space=pl.ANY)],
        out_specs=[head_out, mix_out,
                   pl.BlockSpec((None, None, HG_NCH, LANE, LANE), lambda b, h, t: (b, h, t, 0, 0))],
        out_shape=[_sds((bsz, seq, HG_WIDTH), F32), _sds(mix_in.shape, BF16),
                   _sds((bsz, HG_HEADS, seq // HG_CHUNK, LANE, LANE), F32)],
        scratch_shapes=[pltpu.VMEM((LANE, LANE), F32)],
        sem=("parallel", "parallel", "arbitrary"), comms=comms, aliases={6: 1})


def _mid_fwd(x, mix, post_w, g1, pre_w, sc2, sh2, tm=512):
    bsz, seq, _ = x.shape

    def body(x_ref, mix_ref, pw_ref, g1_ref, w2_ref, sc_ref, sh_ref, x1_ref, h2_ref):
        n1, _, _ = _rms_fwd(mix_ref[...], pw_ref[...])
        x1 = x_ref[...] + g1_ref[...] * n1
        x1_ref[...] = x1
        y2, _, _ = _rms_fwd(x1, w2_ref[...])
        h2_ref[...] = (y2 * (1.0 + sc_ref[...]) + sh_ref[...]).astype(BF16)

    return pl.pallas_call(
        body, name="mid_fwd", grid=(bsz, seq // tm),
        in_specs=[_tok_spec(tm), _tok_spec(tm), _vec_spec(), _row_spec(), _vec_spec(), _row_spec(), _row_spec()],
        out_specs=[_tok_spec(tm), _tok_spec(tm)], out_shape=[_sds(x.shape, F32), _sds(x.shape, BF16)],
        compiler_params=_params("parallel", "parallel"),
    )(x, mix, post_w, g1, pre_w, sc2, sh2)


def _acc_out(ref, first, value):
    @pl.when(first)
    def _():
        ref[...] = value

    @pl.when(jnp.logical_not(first))
    def _():
        ref[...] += value


def _loss_bwd(x1, down, post_w, g2, target, tm=512):
    bsz, seq, _ = x1.shape

    def body(x1_ref, d_ref, w_ref, g2_ref, t_ref, loss_ref, dy_ref, dd_ref, dg2_ref, dw_ref):
        b, i = pl.program_id(0), pl.program_id(1)
        w, g2v = w_ref[...], g2_ref[...]
        n2, dh, rstd = _rms_fwd(d_ref[...], w)
        err = x1_ref[...] + g2v * n2 - t_ref[...]
        part = (0.5 / D_MODEL) * jnp.sum(jnp.sum(err * err, axis=-1, keepdims=True), axis=0, keepdims=True)
        _acc_out(loss_ref, jnp.logical_and(b == 0, i == 0), jnp.broadcast_to(part, (1, LANE)))
        dy = err * (1.0 / D_MODEL)
        dy_ref[...] = dy
        _acc_out(dg2_ref, i == 0, _colsum(dy * n2))
        dd, dw_rows = _rms_bwd(dy * g2v, dh, rstd, w)
        dd_ref[...] = dd.astype(BF16)
        _acc_out(dw_ref, jnp.logical_and(b == 0, i == 0), _colsum(dw_rows))

    return pl.pallas_call(
        body, name="loss_bwd", grid=(bsz, seq // tm),
        in_specs=[_tok_spec(tm), _tok_spec(tm), _vec_spec(), _row_spec(), _tok_spec(tm)],
        out_specs=[_vec_spec(LANE), _tok_spec(tm), _tok_spec(tm), _row_spec(), _vec_spec()],
        out_shape=[_sds((1, LANE), F32), _sds(x1.shape, F32), _sds(x1.shape, BF16), _sds((bsz, 1, D_MODEL), F32),
                   _sds((1, D_MODEL), F32)],
        compiler_params=_params("arbitrary", "arbitrary"),
    )(x1, down, post_w, g2, target)


def _mid_bwd(dh2, dy, x1, mix, pre_w, sc2, post_w, g1, tm=512, comms=()):
    bsz, seq, _ = x1.shape

    def body(dh2_ref, dy_ref, x1_ref, mix_ref, w2_ref, sc_ref, pw_ref, g1_ref,
             dx1_ref, dmix_ref, dsc_ref, dsh_ref, dg1_ref, dw2_ref, dpw_ref):
        b, i = pl.program_id(0), pl.program_id(1)
        first = jnp.logical_and(b == 0, i == 0)
        w2, pw = w2_ref[...], pw_ref[...]
        dh2v = dh2_ref[...]
        y2, xh2, rstd2 = _rms_fwd(x1_ref[...], w2)
        _acc_out(dsh_ref, i == 0, _colsum(dh2v))
        _acc_out(dsc_ref, i == 0, _colsum(dh2v * y2))
        dx1n, dw_rows = _rms_bwd(dh2v * (1.0 + sc_ref[...]), xh2, rstd2, w2)
        _acc_out(dw2_ref, first, _colsum(dw_rows))
        dx1 = dy_ref[...] + dx1n
        dx1_ref[...] = dx1
        n1, mh, rstd1 = _rms_fwd(mix_ref[...], pw)
        _acc_out(dg1_ref, i == 0, _colsum(dx1 * n1))
        dmix, dpw_rows = _rms_bwd(dx1 * g1_ref[...], mh, rstd1, pw)
        dmix_ref[...] = dmix.astype(BF16)
        _acc_out(dpw_ref, first, _colsum(dpw_rows))

    row_shape = _sds((bsz, 1, D_MODEL), F32)
    vec_shape = _sds((1, D_MODEL), F32)
    return _call(
        body, (dh2, dy, x1, mix, pre_w, sc2, post_w, g1), name="mid_bwd", grid=(bsz, seq // tm),
        in_specs=[_tok_spec(tm), _tok_spec(tm), _tok_spec(tm), _tok_spec(tm), _vec_spec(), _row_spec(), _vec_spec(), _row_spec()],
        out_specs=[_tok_spec(tm), _tok_spec(tm), _row_spec(), _row_spec(), _row_spec(), _vec_spec(), _vec_spec()],
        out_shape=[_sds(x1.shape, F32), _sds(x1.shape, BF16), row_shape, row_shape, row_shape, vec_shape, vec_shape],
        sem=("arbitrary", "arbitrary"), comms=comms)


def _norm1_bwd(dh1, dx1, x, pre_w, sc1, tm=512):
    bsz, seq, _ = x.shape

    def body(dh_ref, dx1_ref, x_ref, w_ref, sc_ref, gx_ref, dsc_ref, dsh_ref, dw_ref):
        b, i = pl.program_id(0), pl.program_id(1)
        w = w_ref[...]
        dh = dh_ref[...]
        y, xh, rstd = _rms_fwd(x_ref[...], w)
        _acc_out(dsh_ref, i == 0, _colsum(dh))
        _acc_out(dsc_ref, i == 0, _colsum(dh * y))
        dx, dw_rows = _rms_bwd(dh * (1.0 + sc_ref[...]), xh, rstd, w)
        _acc_out(dw_ref, jnp.logical_and(b == 0, i == 0), _colsum(dw_rows))
        gx_ref[...] = dx1_ref[...] + dx

    row_shape = _sds((bsz, 1, D_MODEL), F32)
    return pl.pallas_call(
        body, name="norm1_bwd", grid=(bsz, seq // tm),
        in_specs=[_tok_spec(tm), _tok_spec(tm), _tok_spec(tm), _vec_spec(), _row_spec()],
        out_specs=[_tok_spec(tm), _row_spec(), _row_spec(), _vec_spec()],
        out_shape=[_sds(x.shape, F32), row_shape, row_shape, _sds((1, D_MODEL), F32)],
        compiler_params=_params("arbitrary", "arbitrary"),
    )(dh1, dx1, x, pre_w, sc1)


def _hgrn_bwd(dcat, proj, o_raw, states, lb_table, norm_w, comms=()):
    bsz, seq, _ = proj.shape
    nstep = seq // HG_TOK
    rec0 = ATT_WIDTH // LANE

    def body(tbl_ref, nw_ref, dr_ref, q_ref, f_ref, i_ref, g_ref, o_ref, st_ref,
             dq_ref, df_ref, di_ref, dg_ref, dlb_ref, dnw_ref, ds_scr):
        h, b, t = pl.program_id(0), pl.program_id(1), pl.program_id(2)

        @pl.when(t == 0)
        def _():
            ds_scr[...] = jnp.zeros_like(ds_scr)

        hq, v, hg = q_ref[...], i_ref[...], g_ref[...]
        nw = nw_ref[...]
        c = _hgrn_common(tbl_ref[...], f_ref[...], hq)
        qd, kd, ku = c["qd"], c["kd"], c["ku"]
        y, on, rstd = _rms_fwd(o_ref[...], nw)
        sg = _sigmoid(hg)
        dr = dr_ref[...]
        dg_ref[...] = (dr * y * (sg * (1.0 + hg * (1.0 - sg)))).astype(BF16)
        do, dnw_rows = _rms_bwd(dr * (hg * sg), on, rstd, nw)
        lower, upper = _block_masks()
        at = jnp.where(upper, _dot(kd, qd, NT_DIMS), 0.0)
        da = jnp.where(lower, _dot(do, v, NT_DIMS), 0.0)
        dat = jnp.where(upper, _dot(v, do, NT_DIMS), 0.0)
        dv = _dot(at, do)
        dqd = _dot(da, kd)
        dkd = _dot(dat, qd)
        do_c, qd_c, v_c, ku_c = [_chunk_slices(z.astype(BF16)) for z in (do, qd, v, ku)]
        outer = [_dot(do_c[j], qd_c[j], TN_DIMS) for j in range(HG_NCH)]
        ds = ds_scr[...]
        ds_after = [None] * HG_NCH
        for j in reversed(range(HG_NCH)):
            ds_after[j] = ds
            ds = outer[j] + ds * c["e_last"][j]
        ds_scr[...] = ds
        states = [st_ref[j] for j in range(HG_NCH)]
        dv = dv + jnp.concatenate([_dot(ku_c[j], ds_after[j], NT_DIMS) for j in range(HG_NCH)], axis=0)
        dqd = dqd + jnp.concatenate([_dot(do_c[j], states[j]) for j in range(HG_NCH)], axis=0)
        dku = jnp.concatenate([_dot(v_c[j], ds_after[j]) for j in range(HG_NCH)], axis=0)
        dku_ku = dku * ku
        dbl = [_colsum(states[j] * ds_after[j]) * c["e_last"][j] + _colsum(dku_ku[j * HG_CHUNK:(j + 1) * HG_CHUNK])
               for j in range(HG_NCH)]
        dk = dkd * c["e_nb"] + dku * c["e_rem"]
        db = dqd * qd - dkd * kd - dku_ku + jnp.where(_row_in_chunk() == HG_CHUNK - 1, _chunk_rows(dbl), 0.0)
        dfv = _chunk_cumsum(db, reverse=True) / c["f"] - dk
        sig, sq = c["sig"], c["sq"]
        df_ref[...] = (dfv * (1.0 - c["lb"]) * sig * (1.0 - sig)).astype(BF16)
        dq_ref[...] = (dqd * c["e_b"] * (sq * (1.0 + hq * (1.0 - sq)))).astype(BF16)
        di_ref[...] = dv.astype(BF16)
        _acc_out(dlb_ref, jnp.logical_and(b == 0, t == 0), _colsum(dfv * (1.0 - sig)))
        _acc_out(dnw_ref, jnp.logical_and(h == 0, jnp.logical_and(b == 0, t == 0)), _colsum(dnw_rows))

    rev = lambda t: nstep - 1 - t
    slab = lambda first: pl.BlockSpec((None, HG_TOK, LANE), lambda h, b, t: (b, rev(t), first + h))
    head = pl.BlockSpec((None, HG_TOK, LANE), lambda h, b, t: (b, rev(t), h))
    grad_shape = _sds((bsz, seq, HG_WIDTH), BF16)
    return _call(
        body, (lb_table, norm_w, dcat, proj, proj, proj, proj, o_raw, states), name="hgrn_bwd", grid=(HG_HEADS, bsz, nstep),
        in_specs=[pl.BlockSpec((2, LANE), lambda h, b, t: (0, h)), pl.BlockSpec((1, LANE), lambda h, b, t: (0, 0)),
                  slab(rec0), slab(HG_Q0), slab(HG_F0), slab(HG_I0), slab(HG_G0), head,
                  pl.BlockSpec((None, None, HG_NCH, LANE, LANE), lambda h, b, t: (b, h, rev(t), 0, 0))],
        out_specs=[head, head, head, head, pl.BlockSpec((1, LANE), lambda h, b, t: (0, h)),
                   pl.BlockSpec((1, LANE), lambda h, b, t: (0, 0))],
        out_shape=[grad_shape, grad_shape, grad_shape, grad_shape, _sds((1, HG_WIDTH), F32), _sds((1, LANE), F32)],
        scratch_shapes=[pltpu.VMEM((LANE, LANE), F32)],
        sem=("arbitrary", "arbitrary", "arbitrary"), comms=comms)


def _attn_bwd(dcat, raw, w_norm, qh, kh, vh, lse, sinks, comms=()):
    bsz, nblk = qh.shape[0], qh.shape[1]
    seq = nblk * WINDOW

    def body(sink_ref, da_ref, raw_ref, w_ref, q_ref, kc_ref, kp_ref, vc_ref, vp_ref, l_ref,
             dq_ref, dkd_ref, dkp_ref, dvd_ref, dvp_ref, dw_ref, dsink_ref):
        b, i = pl.program_id(0), pl.program_id(1)
        first = jnp.logical_and(b == 0, i == 0)
        mask = _band_mask(i)
        raw_v = raw_ref[...]
        w = w_ref[...]
        _, on, rstd = _rms_fwd(raw_v, w)
        do_all, dw_rows = _rms_bwd(da_ref[...], on, rstd, w)
        _acc_out(dw_ref, first, _colsum(dw_rows))
        lane8 = lax.broadcasted_iota(jnp.int32, (1, ATT_Q_HEADS), 1)
        dsink = jnp.zeros((1, ATT_Q_HEADS), F32)
        for g in range(ATT_KV_HEADS):
            gs = slice(g * ATT_HEAD_DIM, (g + 1) * ATT_HEAD_DIM)
            heads = [slice((g * ATT_GROUP + hh) * ATT_HEAD_DIM, (g * ATT_GROUP + hh + 1) * ATT_HEAD_DIM) for hh in range(ATT_GROUP)]
            q = q_ref[g]
            keys = jnp.concatenate([kp_ref[g], kc_ref[g]], axis=0)
            vals = jnp.concatenate([vp_ref[g], vc_ref[g]], axis=0)
            do_g = jnp.concatenate([do_all[:, hs] for hs in heads], axis=0)
            dsum = jnp.concatenate([jnp.sum(do_all[:, hs] * raw_v[:, hs], axis=-1, keepdims=True) for hs in heads], axis=0)
            lse_g = jnp.concatenate([l_ref[:, g * ATT_GROUP + hh:g * ATT_GROUP + hh + 1] for hh in range(ATT_GROUP)], axis=0)
            p = jnp.where(mask, jnp.exp(_dot(q, keys, NT_DIMS) * ATT_SCALE - lse_g), 0.0)
            sink_part = jnp.exp(_sink_column(sink_ref, g) - lse_g) * dsum
            for hh in range(ATT_GROUP):
                head_sum = jnp.sum(sink_part[hh * WINDOW:(hh + 1) * WINDOW], axis=0, keepdims=True)
                dsink = dsink - jnp.where(lane8 == g * ATT_GROUP + hh, head_sum, 0.0)
            ds = p * (_dot(do_g, vals, NT_DIMS) - dsum) * ATT_SCALE
            dq_g = _dot(ds, keys)
            for hh, hs in enumerate(heads):
                dq_ref[:, hs] = dq_g[hh * WINDOW:(hh + 1) * WINDOW]
            dk_g = _dot(ds, q, TN_DIMS)
            dv_g = _dot(p, do_g, TN_DIMS)
            dkp_ref[:, gs], dkd_ref[:, gs] = dk_g[:WINDOW], dk_g[WINDOW:]
            dvp_ref[:, gs], dvd_ref[:, gs] = dv_g[:WINDOW], dv_g[WINDOW:]
        _acc_out(dsink_ref, first, dsink)

    cur = lambda width: pl.BlockSpec((None, WINDOW, width), lambda b, i: (b, i, 0))
    q_spec, kv_cur, kv_prev = _attn_specs()
    kv_shape = _sds((bsz, seq, LANE), F32)
    return _call(
        body, (sinks, dcat, raw, w_norm, qh, kh, kh, vh, vh, lse), name="attn_bwd", grid=(bsz, nblk),
        in_specs=[pl.BlockSpec(memory_space=pltpu.SMEM), cur(ATT_WIDTH), cur(ATT_WIDTH), _vec_spec(ATT_WIDTH), q_spec,
                  kv_cur, kv_prev, kv_cur, kv_prev, cur(ATT_Q_HEADS)],
        out_specs=[cur(ATT_WIDTH), cur(LANE), cur(LANE), cur(LANE), cur(LANE), _vec_spec(ATT_WIDTH), _vec_spec(ATT_Q_HEADS)],
        out_shape=[_sds((bsz, seq, ATT_WIDTH), F32), kv_shape, kv_shape, kv_shape, kv_shape, _sds((1, ATT_WIDTH), F32),
                   _sds((1, ATT_Q_HEADS), F32)],
        sem=("arbitrary", "arbitrary"), comms=comms)


def _rope_bwd(dq, dkd, dkp, dvd, dvp, tables):
    bsz, seq, _ = dq.shape
    nblk = seq // WINDOW
    half = ROPE_DIM // 2

    def body(dq_ref, dkd_ref, dkp_ref, dvd_ref, dvp_ref, c_ref, u_ref, d_ref, o_ref):
        c, u, d = c_ref[...], u_ref[...], d_ref[...]
        has_next = pl.program_id(1) < nblk - 1

        def unrope(g):
            return g * c + pltpu.roll(g * u, LANE - half, 1) + pltpu.roll(g * d, half, 1)

        for s in range(ATT_WIDTH // LANE):
            o_ref[:, s * LANE:(s + 1) * LANE] = unrope(dq_ref[:, s * LANE:(s + 1) * LANE]).astype(BF16)
        dk = dkd_ref[...] + jnp.where(has_next, dkp_ref[...], 0.0)
        o_ref[:, ATT_WIDTH:ATT_WIDTH + LANE] = unrope(dk).astype(BF16)
        o_ref[:, ATT_WIDTH + LANE:ATT_COLS] = (dvd_ref[...] + jnp.where(has_next, dvp_ref[...], 0.0)).astype(BF16)

    cur = lambda width: pl.BlockSpec((None, WINDOW, width), lambda b, i: (b, i, 0))
    nxt = pl.BlockSpec((None, WINDOW, LANE), lambda b, i: (b, jnp.minimum(i + 1, nblk - 1), 0))
    tab = pl.BlockSpec((WINDOW, LANE), lambda b, i: (i, 0))
    return pl.pallas_call(
        body, name="rope_bwd", grid=(bsz, nblk),
        in_specs=[cur(ATT_WIDTH), cur(LANE), nxt, cur(LANE), nxt, tab, tab, tab],
        out_specs=cur(ATT_COLS), out_shape=_sds((bsz, seq, ATT_COLS), BF16),
        compiler_params=_params("parallel", "parallel"),
    )(dq, dkd, dkp, dvd, dvp, *tables)


def _other_chips(x, y):
    return [(1 - x, y), (x, 1 - y), (1 - x, 1 - y)]


def _sem_pair(n):
    return [pltpu.SemaphoreType.DMA((n,)), pltpu.SemaphoreType.DMA((n,))]


def _plan_chip_gather(blocks, bufs):
    n = len(blocks)

    def copies(ins, outs, sems):
        x, y, c = _mesh_pos()
        sends, lands = [], []
        for a in range(n):
            for j, chip in enumerate(_other_chips(x, y)):
                k = 3 * a + j
                sends.append(pltpu.make_async_remote_copy(
                    src_ref=ins[a], dst_ref=outs[a].at[4 * x + 2 * y + c], send_sem=sems[0].at[k], recv_sem=sems[1].at[k],
                    device_id=(*chip, c), device_id_type=MESH))
                slot = outs[a].at[4 * chip[0] + 2 * chip[1] + c]
                lands.append(pltpu.make_async_remote_copy(
                    src_ref=slot, dst_ref=slot, send_sem=sems[0].at[k], recv_sem=sems[1].at[k],
                    device_id=(*chip, c), device_id_type=MESH))
        return sends, lands

    def start(ins, outs, sems):
        for cp in copies(ins, outs, sems)[0]:
            cp.start()

    def finish(ins, outs, sems):
        sends, lands = copies(ins, outs, sems)
        for cp in lands:
            cp.wait_recv()
        for cp in sends:
            cp.wait_send()

    return _Comm(list(blocks) + list(bufs), [_sds(b.shape, b.dtype) for b in bufs], _sem_pair(3 * n), start, finish,
                 aliases=[(n + a, a) for a in range(n)])


def _plan_pair_forward(bufs):
    n = len(bufs)

    def copies(outs, sems):
        x, y, c = _mesh_pos()
        sends, lands = [], []
        for a in range(n):
            for j, chip in enumerate(_other_chips(x, y)):
                k = 3 * a + j
                slot = outs[a].at[4 * chip[0] + 2 * chip[1] + c]
                sends.append(pltpu.make_async_remote_copy(
                    src_ref=slot, dst_ref=slot, send_sem=sems[0].at[k], recv_sem=sems[1].at[k],
                    device_id=(x, y, 1 - c), device_id_type=MESH))
                theirs = outs[a].at[4 * chip[0] + 2 * chip[1] + 1 - c]
                lands.append(pltpu.make_async_remote_copy(
                    src_ref=theirs, dst_ref=theirs, send_sem=sems[0].at[k], recv_sem=sems[1].at[k],
                    device_id=(x, y, 1 - c), device_id_type=MESH))
        return sends, lands

    def start(ins, outs, sems):
        for cp in copies(outs, sems)[0]:
            cp.start()

    def finish(ins, outs, sems):
        sends, lands = copies(outs, sems)
        for cp in lands:
            cp.wait_recv()
        for cp in sends:
            cp.wait_send()

    return _Comm(list(bufs), [_sds(b.shape, b.dtype) for b in bufs], _sem_pair(3 * n), start, finish,
                 aliases=[(a, a) for a in range(n)])


def _plan_pair(arrays, other_half):
    n = len(arrays)
    per = N_CHIPS if other_half == "chip_major" else 1

    def copies(ins, outs, sems):
        x, y, c = _mesh_pos()
        out = []
        for a in range(n):
            for k in range(per):
                if other_half == "chip_major":
                    src, dst = ins[a].at[k, 1 - c], outs[a].at[k]
                else:
                    src, dst = (ins[a].at[1 - c] if other_half else ins[a]), outs[a]
                out.append(pltpu.make_async_remote_copy(
                    src_ref=src, dst_ref=dst, send_sem=sems[0].at[per * a + k], recv_sem=sems[1].at[per * a + k],
                    device_id=(x, y, 1 - c), device_id_type=MESH))
        return out

    def start(ins, outs, sems):
        for cp in copies(ins, outs, sems):
            cp.start()

    def finish(ins, outs, sems):
        for cp in copies(ins, outs, sems):
            cp.wait()

    if other_half == "chip_major":
        shapes = [_sds((a.shape[0],) + a.shape[2:], a.dtype) for a in arrays]
    else:
        shapes = [_sds(a.shape[1:] if other_half else a.shape, a.dtype) for a in arrays]
    return _Comm(list(arrays), shapes, _sem_pair(per * n), start, finish)


def _plan_chip_exchange(arrays):
    n = len(arrays)

    def copies(ins, outs, sems):
        x, y, c = _mesh_pos()
        sends, lands = [], []
        for a in range(n):
            for j, chip in enumerate(_other_chips(x, y)):
                k = 3 * a + j
                sends.append(pltpu.make_async_remote_copy(
                    src_ref=ins[a].at[2 * chip[0] + chip[1]], dst_ref=outs[a].at[2 * x + y], send_sem=sems[0].at[k],
                    recv_sem=sems[1].at[k], device_id=(*chip, c), device_id_type=MESH))
                slot = outs[a].at[2 * chip[0] + chip[1]]
                lands.append(pltpu.make_async_remote_copy(
                    src_ref=slot, dst_ref=slot, send_sem=sems[0].at[k], recv_sem=sems[1].at[k],
                    device_id=(*chip, c), device_id_type=MESH))
        return sends, lands

    def start(ins, outs, sems):
        for cp in copies(ins, outs, sems)[0]:
            cp.start()

    def finish(ins, outs, sems):
        sends, lands = copies(ins, outs, sems)
        for cp in lands:
            cp.wait_recv()
        for cp in sends:
            cp.wait_send()

    return _Comm(list(arrays), [_sds(a.shape, a.dtype) for a in arrays], _sem_pair(3 * n), start, finish)


def _comm_only(comms, name):
    return _call(lambda: None, (), name=name, grid=(), in_specs=[], out_specs=[], out_shape=[], sem=(), comms=comms)[1]


def _allgather8(arrays, name):
    return _comm_only([_plan_allgather8(arrays)], name)[0]


def _plan_allgather8(arrays):
    n = len(arrays)

    def parts(ins, outs, sems):
        send_sems, recv_sems, local_sems = sems
        x, y, c = _mesh_pos()
        me, sibling = (x, y, c), (x, y, 1 - c)
        chips = _other_chips(x, y)

        def copy(a, k, block, to, src=None):
            dst = outs[a].at[4 * block[0] + 2 * block[1] + block[2]]
            return pltpu.make_async_remote_copy(
                src_ref=dst if src is None else src, dst_ref=dst, send_sem=send_sems.at[7 * a + k],
                recv_sem=recv_sems.at[7 * a + k], device_id=to, device_id_type=MESH)

        mine = [pltpu.make_async_copy(ins[a], outs[a].at[4 * x + 2 * y + c], local_sems.at[a]) for a in range(n)]
        first = []
        for a in range(n):
            first.append(copy(a, 0, me, sibling, src=ins[a]))
            first += [copy(a, 1 + j, me, (*chip, c), src=ins[a]) for j, chip in enumerate(chips)]
        return copy, mine, first, me, sibling, chips, c

    def start(ins, outs, sems):
        _, mine, first, *_ = parts(ins, outs, sems)
        for cp in mine + first:
            cp.start()

    def finish(ins, outs, sems):
        copy, mine, first, me, sibling, chips, c = parts(ins, outs, sems)
        passed = []
        for j, chip in enumerate(chips):
            for a in range(n):
                copy(a, 1 + j, (*chip, c), me).wait_recv()
                fwd = copy(a, 4 + j, (*chip, c), sibling)
                fwd.start()
                passed.append(fwd)
        for a in range(n):
            copy(a, 0, sibling, me).wait_recv()
            for j, chip in enumerate(chips):
                copy(a, 4 + j, (*chip, 1 - c), me).wait_recv()
        for cp in first + passed:
            cp.wait_send()
        for cp in mine:
            cp.wait()

    sems = [pltpu.SemaphoreType.DMA((7 * n,)), pltpu.SemaphoreType.DMA((7 * n,)), pltpu.SemaphoreType.DMA((n,))]
    return _Comm(list(arrays), [_sds((N_DEV,) + a.shape, a.dtype) for a in arrays], sems, start, finish)


def _pair_sum(g, q, core, name, chip_major=False):
    rows, cols = g.shape[2:]
    tr = _row_tile(rows)

    def body(core_ref, g_ref, q_ref, o_ref):
        o_ref[...] = (g_ref[...] + q_ref[...]).astype(BF16)

    blk = pl.BlockSpec((None, tr, cols), lambda k, i, core_ref: (k, i, 0))
    if chip_major:
        own = pl.BlockSpec((None, None, tr, cols), lambda k, i, core_ref: (k, core_ref[0], i, 0))
    else:
        own = pl.BlockSpec((None, None, tr, cols), lambda k, i, core_ref: (core_ref[0], k, i, 0))
    return pl.pallas_call(
        body, name=name,
        grid_spec=pltpu.PrefetchScalarGridSpec(num_scalar_prefetch=1, grid=(N_CHIPS, rows // tr), in_specs=[own, blk], out_specs=blk),
        out_shape=_sds((N_CHIPS, rows, cols), BF16), compiler_params=_params("parallel", "parallel"),
    )(core, g, q)


def _sum_chips(own, landed, chip, name):
    _, rows, cols = own.shape
    tr = _row_tile(rows)

    def body(chip_ref, own_ref, a_ref, b_ref, c_ref, o_ref):
        acc = own_ref[...].astype(F32) + a_ref[...].astype(F32)
        o_ref[...] = (acc + b_ref[...].astype(F32)) + c_ref[...].astype(F32)

    blk = lambda flip: pl.BlockSpec((None, tr, cols), lambda i, chip_ref: (jnp.bitwise_xor(chip_ref[0], flip), i, 0))
    return pl.pallas_call(
        body, name=name,
        grid_spec=pltpu.PrefetchScalarGridSpec(num_scalar_prefetch=1, grid=(rows // tr,), in_specs=[blk(0), blk(1), blk(2), blk(3)],
                                               out_specs=pl.BlockSpec((tr, cols), lambda i, chip_ref: (i, 0))),
        out_shape=_sds((rows, cols), F32), compiler_params=_params("parallel"),
    )(chip, own, landed, landed, landed)


SMALL_ROWS = 120
PACK_ROWS = 168


def _rows(a, nrows):
    flat = a.reshape(-1)
    return jnp.pad(flat, (0, nrows * LANE - flat.shape[0])).reshape(nrows, LANE)


def _pack_small(b_ada, pre_w_mix, post_w_mix, pre_w_mlp, post_w_mlp, attn_out_w, hg_norm_w, attn_sinks, lb_table):
    return jnp.concatenate([
        _rows(b_ada, 48), _rows(pre_w_mix, 8), _rows(post_w_mix, 8), _rows(pre_w_mlp, 8), _rows(post_w_mlp, 8),
        _rows(attn_out_w, 8), _rows(hg_norm_w, 8), _rows(attn_sinks, 8), _rows(lb_table[0], 8), _rows(lb_table[1], 8)], axis=0)


def _unpack_small(p):
    vec = lambda lo, n: p[lo:lo + n // LANE].reshape(1, n)
    lb = jnp.stack([p[104:108].reshape(HG_WIDTH), p[112:116].reshape(HG_WIDTH)])
    return dict(b_ada=vec(0, N_MOD * D_MODEL), pre_w_mix=vec(48, D_MODEL), post_w_mix=vec(56, D_MODEL), pre_w_mlp=vec(64, D_MODEL),
                post_w_mlp=vec(72, D_MODEL), attn_out_w=vec(80, ATT_WIDTH), hg_norm_w=p[88:89], attn_sinks=p[96:97, :ATT_Q_HEADS],
                lb_table=lb)


def _small_update(packs, w, m, v):
    def body(p_ref, w_ref, m_ref, v_ref, g_ref, dl_ref, nm_ref, nv_ref, loss_ref):
        tot = p_ref[0]
        for d in range(1, N_DEV):
            tot = tot + p_ref[d]
        wv = w_ref[...]
        p1 = _sigmoid(wv[112:120] - wv[104:112])
        s = tot[152:160] * p1 * (1.0 - p1)
        g = jnp.concatenate([tot[0:48] + tot[48:96], tot[96:152], -s, s], axis=0)
        g_ref[...] = g
        dl_ref[...], nm_ref[...], nv_ref[...] = _adamw_math(g, wv, m_ref[...], v_ref[...])
        loss_ref[...] = tot[160:168]

    shp = _sds((SMALL_ROWS, LANE), F32)
    return pl.pallas_call(body, name="small_update", out_shape=[shp] * 4 + [_sds((8, LANE), F32)],
                          compiler_params=_params())(packs, w, m, v)


def kernel(x, c, w_ada, b_ada, pre_w_mix, w_in, attn_sinks, attn_out_w, lb_table, hg_norm_w, w_out, post_w_mix, pre_w_mlp, w_up, w_down, post_w_mlp, loss_target, m_w_ada, m_b_ada, m_pre_w_mix, m_w_in, m_attn_sinks, m_attn_out_w, m_lb_table, m_hg_norm_w, m_w_out, m_post_w_mix, m_pre_w_mlp, m_w_up, m_w_down, m_post_w_mlp, v_w_ada, v_b_ada, v_pre_w_mix, v_w_in, v_attn_sinks, v_attn_out_w, v_lb_table, v_hg_norm_w, v_w_out, v_post_w_mix, v_pre_w_mlp, v_w_up, v_w_down, v_post_w_mlp):
    xi, yi, ci = _mesh_pos()
    chip = 2 * xi + yi
    dev = 2 * chip + ci
    bsz, seq, _ = x.shape
    ntok = bsz * seq
    ada_cols = w_ada.shape[2]
    core = jnp.reshape(ci, (1,)).astype(jnp.int32)
    chip_idx = jnp.reshape(chip, (1,)).astype(jnp.int32)
    flat = lambda a: a.reshape(ntok, a.shape[-1])
    unflat = lambda a: a.reshape(bsz, seq, a.shape[-1])
    tables = _rope_tables(seq)

    def row_half(w):
        rows = w.shape[1] // 2
        return lax.dynamic_slice_in_dim(w[0], ci * rows, rows, axis=0).astype(BF16)

    def gather_buffer(w):
        rows, cols = w.shape[1] // 2, w.shape[2]
        own = w[0].astype(BF16).reshape(2, rows, cols)
        return lax.dynamic_update_slice(jnp.zeros((N_DEV, rows, cols), BF16), own, (2 * chip, 0, 0))

    w_in_t, m_in_t, v_in_t = [jnp.transpose(a[0])[None] for a in (w_in, m_w_in, v_w_in)]
    c_g, in_g = _allgather8([c, row_half(w_in_t)], "gather_first")
    c_all = c_g.reshape(N_DEV * bsz, D_MODEL)
    w_in_full = in_g.reshape(IN_COLS, D_MODEL)

    b_cols = lax.dynamic_slice_in_dim(b_ada, chip * ada_cols, ada_cols, axis=1)
    mod_part = _ada_fwd(c_all, w_ada[0], b_cols)
    half_rows = mod_part.shape[0] // 2
    (mod_g,) = _allgather8([lax.dynamic_slice_in_dim(mod_part, ci * half_rows, half_rows, axis=0)], "gather_mod")
    mod_all = mod_g.reshape(N_CHIPS, 2, half_rows, ada_cols).transpose(1, 2, 0, 3).reshape(N_DEV * bsz, N_MOD * D_MODEL)
    mod = lax.dynamic_slice_in_dim(mod_all, dev * bsz, bsz, axis=0)
    sh1, sc1, g1, sh2, sc2, g2 = [mod[:, i * D_MODEL:(i + 1) * D_MODEL].reshape(bsz, 1, D_MODEL) for i in range(N_MOD)]

    h1 = _prenorm1(x, pre_w_mix, sc1, sh1)
    proj, ((out_g,),) = _mm(flat(h1), w_in_full, name="in_proj", out_dtype=F32, trans_b=True,
                            comms=[_plan_chip_gather([row_half(w_out)], [gather_buffer(w_out)])])
    proj = unflat(proj)
    qh, kh, vh = _rope_fwd(proj, tables)
    (attn_raw, cat, lse), ((up_g,), (out_g,)) = _attn_fwd(
        qh, kh, vh, attn_sinks, attn_out_w,
        comms=[_plan_chip_gather([row_half(w_up)], [gather_buffer(w_up)]), _plan_pair_forward([out_g])])
    (o_raw, cat, states), ((down_g,), (up_g,)) = _hgrn_fwd(
        proj, lb_table, hg_norm_w, cat,
        comms=[_plan_chip_gather([row_half(w_down)], [gather_buffer(w_down)]), _plan_pair_forward([up_g])])
    w_out_full = out_g.reshape(D_MODEL, D_MODEL)
    w_up4 = up_g.reshape(N_CHIPS, D_MODEL, D_MODEL)
    mix = unflat(_mm(flat(cat), w_out_full, name="out_proj", out_dtype=F32))
    x1, h2 = _mid_fwd(x, mix, post_w_mix, g1, pre_w_mlp, sc2, sh2)
    big_tm = min(ntok, 1024)
    up_spec = pl.BlockSpec((None, D_MODEL, D_MODEL), lambda i, j: (j, 0, 0))
    r, ((down_g,),) = _mm(flat(h2), w_up4, name="up_proj", out_dtype=BF16, tm=big_tm, tn=D_MODEL, n_out=D_FF, b_spec=up_spec,
                          epi=lambda acc: jnp.maximum(acc, 0.0), comms=[_plan_pair_forward([down_g])])
    w_down_full = down_g.reshape(D_FF, D_MODEL)
    square = lambda t: t * t
    down = unflat(_mm(r, w_down_full, name="down_proj", out_dtype=F32, a_fn=square))
    loss_row, dy, dd, dg2, d_post_mlp = _loss_bwd(x1, down, post_w_mlp, g2, loss_target)

    dpre = _mm(flat(dd), w_down_full, name="down_bwd", out_dtype=BF16, trans_b=True, tm=big_tm, tn=D_MODEL, extra=(r,),
               epi=lambda acc, rt: acc * (2.0 * rt.astype(F32)))
    half_rows = D_MODEL // 2
    g_down = _mm_tn(r, flat(dd), name="down_wgrad", tk=half_rows, tn=D_MODEL, a_fn=square,
                    out_shape=_sds((2, N_CHIPS, half_rows, D_MODEL), F32),
                    out_spec=pl.BlockSpec((None, None, half_rows, D_MODEL), lambda i, j: (i % 2, i // 2, 0, 0)))
    dh2, ((q_down,),) = _mm(dpre, w_up4, name="up_bwd", out_dtype=F32, trans_b=True, n_out=D_MODEL, b_chunks=N_CHIPS,
                            comms=[_plan_pair([g_down], True)])
    g_up = _mm_tn(flat(h2), dpre, name="up_wgrad", tk=D_MODEL, tn=half_rows,
                  out_shape=_sds((2, N_CHIPS, half_rows, D_MODEL), F32),
                  out_spec=pl.BlockSpec((2, None, half_rows, half_rows), lambda i, j: (0, j // 2, 0, j % 2)))
    s_down = _pair_sum(g_down, q_down, core, "pair_sum_down")
    (dx1, dmix, dsc2, dsh2, dg1, d_pre_mlp, d_post_mix), ((q_up,),) = _mid_bwd(
        unflat(dh2), dy, x1, mix, pre_w_mlp, sc2, post_w_mix, g1, comms=[_plan_pair([g_up], True)])
    s_up = _pair_sum(g_up, q_up, core, "pair_sum_up")

    dcat = unflat(_mm(flat(dmix), w_out_full, name="out_bwd", out_dtype=F32, trans_b=True))
    out_rows = D_MODEL // N_CHIPS
    g_out = _mm_tn(flat(cat), flat(dmix), name="out_wgrad", tk=out_rows, tn=half_rows,
                   out_shape=_sds((2, N_CHIPS, out_rows, half_rows), F32),
                   out_spec=pl.BlockSpec((None, None, out_rows, half_rows), lambda i, j: (j, i, 0, 0)))
    (dhq, dhf, dhi, dhg, d_lb, d_hg_norm), ((x_down, x_up), (q_out,)) = _hgrn_bwd(
        dcat, proj, o_raw, states, lb_table, hg_norm_w, comms=[_plan_chip_exchange([s_down, s_up]), _plan_pair([g_out], True)])
    half_down = _sum_chips(s_down, x_down, chip_idx, "sum_chips_down")
    half_up = _sum_chips(s_up, x_up, chip_idx, "sum_chips_up")
    s_out = _pair_sum(g_out, q_out, core, "pair_sum_out")
    (dq, dkd, dkp, dvd, dvp, d_attn_out, d_sinks), ((their_down, their_up), (x_out,)) = _attn_bwd(
        dcat, attn_raw, attn_out_w, qh, kh, vh, lse, attn_sinks,
        comms=[_plan_pair([half_down, half_up], False), _plan_chip_exchange([s_out])])
    half_out = _sum_chips(s_out, x_out, chip_idx, "sum_chips_out")
    dproj_a = _rope_bwd(dq, dkd, dkp, dvd, dvp, tables)
    dproj = flat(jnp.concatenate([dproj_a, dhq, dhf, dhi, dhg], axis=-1))
    in_rows = IN_COLS // N_CHIPS // 2
    g_in = _mm_tn(dproj, flat(h1), name="in_wgrad", tk=2 * LANE, tn=D_MODEL).reshape(N_CHIPS, 2, in_rows, D_MODEL)
    dh1, ((q_in,), (their_out,)) = _mm(dproj, w_in_full, name="in_bwd", out_dtype=F32,
                                       comms=[_plan_pair([g_in], "chip_major"), _plan_pair([half_out], False)])
    s_in = _pair_sum(g_in, q_in, core, "pair_sum_in", chip_major=True)
    grad_x, dsc1, dsh1, d_pre_mix = _norm1_bwd(unflat(dh1), dx1, x, pre_w_mix, sc1)

    dmod = jnp.concatenate([dsh1, dsc1, dg1, dsh2, dsc2, dg2], axis=-1).reshape(bsz, N_MOD * D_MODEL)
    pack = jnp.concatenate([
        _rows(dmod, 96), _rows(d_pre_mix, 8), _rows(d_post_mix, 8), _rows(d_pre_mlp, 8), _rows(d_post_mlp, 8),
        _rows(d_attn_out, 8), _rows(d_hg_norm, 8), _rows(d_sinks, 8), _rows(d_lb, 8), _rows(loss_row, 8)], axis=0)
    (packs,), (x_in,) = _comm_only([_plan_allgather8([pack]), _plan_chip_exchange([s_in])], "gather_small")
    half_in = _sum_chips(s_in, x_in, chip_idx, "sum_chips_in")
    ((their_in,),) = _comm_only([_plan_pair([half_in], False)], "pair_swap_in")
    small_args = lambda pre: (pre["b_ada"], pre["pre_w_mix"], pre["post_w_mix"], pre["pre_w_mlp"], pre["post_w_mlp"],
                              pre["attn_out_w"], pre["hg_norm_w"], pre["attn_sinks"], pre["lb_table"])
    w_small = dict(b_ada=b_ada, pre_w_mix=pre_w_mix, post_w_mix=post_w_mix, pre_w_mlp=pre_w_mlp, post_w_mlp=post_w_mlp,
                   attn_out_w=attn_out_w, hg_norm_w=hg_norm_w, attn_sinks=attn_sinks, lb_table=lb_table)
    m_small = dict(b_ada=m_b_ada, pre_w_mix=m_pre_w_mix, post_w_mix=m_post_w_mix, pre_w_mlp=m_pre_w_mlp, post_w_mlp=m_post_w_mlp,
                   attn_out_w=m_attn_out_w, hg_norm_w=m_hg_norm_w, attn_sinks=m_attn_sinks, lb_table=m_lb_table)
    v_small = dict(b_ada=v_b_ada, pre_w_mix=v_pre_w_mix, post_w_mix=v_post_w_mix, pre_w_mlp=v_pre_w_mlp, post_w_mlp=v_post_w_mlp,
                   attn_out_w=v_attn_out_w, hg_norm_w=v_hg_norm_w, attn_sinks=v_attn_sinks, lb_table=v_lb_table)
    *small_packed, loss_rows = _small_update(packs, _pack_small(*small_args(w_small)), _pack_small(*small_args(m_small)),
                                             _pack_small(*small_args(v_small)))
    small_out = [_unpack_small(p) for p in small_packed]
    loss = loss_rows[0, 0]

    dmod_all = packs[:, :96, :].reshape(N_DEV * bsz, N_MOD * D_MODEL)
    dmod_cols = lax.dynamic_slice_in_dim(dmod_all, chip * ada_cols, ada_cols, axis=1)
    ada_out = _ada_bwd_adamw(c_all, dmod_cols, w_ada[0], m_w_ada[0], v_w_ada[0])

    big = dict(
        w_in=tuple(jnp.transpose(a) for a in _adamw_halves(half_in, their_in, core, w_in_t[0], m_in_t[0], v_in_t[0], axis=0,
                                                           name="adamw_in")),
        w_up=tuple(_adamw_halves(half_up, their_up, core, w_up[0], m_w_up[0], v_w_up[0], axis=0, name="adamw_up")),
        w_out=tuple(_adamw_halves(half_out, their_out, core, w_out[0], m_w_out[0], v_w_out[0], axis=1, name="adamw_out")),
        w_down=tuple(_adamw_halves(half_down, their_down, core, w_down[0], m_w_down[0], v_w_down[0], axis=0, name="adamw_down")),
        w_ada=tuple(ada_out),
    )
    order = ("w_ada", "b_ada", "pre_w_mix", "w_in", "attn_sinks", "attn_out_w", "lb_table", "hg_norm_w", "w_out", "post_w_mix",
             "pre_w_mlp", "w_up", "w_down", "post_w_mlp")
    outs = [loss, grad_x]
    for kind in range(4):
        for nm in order:
            outs.append(big[nm][kind][None] if nm in big else small_out[kind][nm])
    return tuple(outs)
```

```python
import functools

import jax
import jax.numpy as jnp
from jax import lax
from jax.experimental import pallas as pl
from jax.experimental.pallas import tpu as pltpu

F32 = jnp.float32
BF16 = jnp.bfloat16

D_MODEL = 1024
ATT_WIDTH = 512
ATT_HEAD_DIM = 64
ATT_Q_HEADS = 8
ATT_KV_HEADS = 2
ATT_GROUP = ATT_Q_HEADS // ATT_KV_HEADS
ATT_KV_COLS = ATT_KV_HEADS * ATT_HEAD_DIM
WINDOW = 128
ROPE_DIM = 16
ROPE_THETA = 500000.0
HG_WIDTH = 512
MIX_WIDTH = ATT_WIDTH + HG_WIDTH
HG_HEAD_DIM = 128
HG_HEADS = 4
HG_CHUNK = 32
IN_COLS = ATT_WIDTH + 2 * ATT_KV_COLS + 4 * HG_WIDTH
ATT_COLS = ATT_WIDTH + 2 * ATT_KV_COLS
D_FF = 4 * D_MODEL
N_MOD = 6
EPS = 1e-6
ATT_SCALE = ATT_HEAD_DIM ** -0.5

ADAM_LR = 0.001
ADAM_B1 = 0.9
ADAM_B2 = 0.999
ADAM_EPS = 1e-08
ADAM_WD = 0.01
ADAM_STEP = 10

N_CHIPS = 4
N_DEV = 8
LANE = 128
VMEM_LIMIT = 48 * 1024 * 1024
MESH = pl.DeviceIdType.MESH

NT_DIMS = (((1,), (1,)), ((), ()))
TN_DIMS = (((0,), (0,)), ((), ()))


def _sds(shape, dtype):
    return jax.ShapeDtypeStruct(tuple(shape), dtype)


def _params(*sem):
    return pltpu.CompilerParams(dimension_semantics=sem, vmem_limit_bytes=VMEM_LIMIT)


def _sigmoid(x):
    return 1.0 / (1.0 + jnp.exp(-x))


def _dot(a, b, dims=None):
    a, b = a.astype(BF16), b.astype(BF16)
    if dims is None:
        return jnp.dot(a, b, preferred_element_type=F32)
    return lax.dot_general(a, b, dims, preferred_element_type=F32)


def _rms_fwd(x, w):
    rstd = lax.rsqrt(jnp.mean(x * x, axis=-1, keepdims=True) + EPS)
    xh = x * rstd
    return xh * w, xh, rstd


def _rms_bwd(dy, xh, rstd, w):
    dxh = dy * w
    dx = rstd * (dxh - xh * jnp.mean(dxh * xh, axis=-1, keepdims=True))
    return dx, dy * xh


def _colsum(x):
    return jnp.sum(x, axis=0, keepdims=True)


def _row_tile(rows, cap=256):
    return max(t for t in range(16, cap + 1, 16) if rows % t == 0)


HBM_SPEC = pl.BlockSpec(memory_space=pltpu.HBM)


def _mesh_pos():
    return lax.axis_index("x"), lax.axis_index("y"), lax.axis_index("c")


class _Comm:
    def __init__(self, ins, outs, sems, start, finish, aliases=()):
        self.ins, self.outs, self.sems = list(ins), list(outs), list(sems)
        self.start, self.finish, self.aliases = start, finish, tuple(aliases)


def _call(body, args, *, name, grid, in_specs, out_specs, out_shape, sem, scratch_shapes=(), comms=(), aliases=None):
    scratch_shapes = list(scratch_shapes)
    if not comms:
        return pl.pallas_call(body, name=name, grid=grid, in_specs=in_specs, out_specs=out_specs, out_shape=out_shape,
                              input_output_aliases=dict(aliases or {}), scratch_shapes=scratch_shapes,
                              compiler_params=_params(*sem))(*args)
    single = not isinstance(out_shape, (list, tuple))
    out_specs_l = [out_specs] if single else list(out_specs)
    out_shape_l = [out_shape] if single else list(out_shape)
    n_in, n_out, n_scr = len(in_specs), len(out_shape_l), len(scratch_shapes)
    n_ci = [len(cm.ins) for cm in comms]
    n_co = [len(cm.outs) for cm in comms]
    n_cs = [len(cm.sems) for cm in comms]
    aliases = dict(aliases or {})
    for k, cm in enumerate(comms):
        for i, o in cm.aliases:
            aliases[n_in + sum(n_ci[:k]) + i] = n_out + sum(n_co[:k]) + o

    def fused(*refs):
        pos = [0]

        def take(n):
            part = refs[pos[0]:pos[0] + n]
            pos[0] += n
            return part

        ins = take(n_in)
        c_ins = [take(n) for n in n_ci]
        outs = take(n_out)
        c_outs = [take(n) for n in n_co]
        scr = take(n_scr)
        c_sems = [take(n) for n in n_cs]
        first, last = True, True
        for d, size in enumerate(grid):
            first = jnp.logical_and(first, pl.program_id(d) == 0)
            last = jnp.logical_and(last, pl.program_id(d) == size - 1)

        def run(which):
            for cm, ci, co, cs in zip(comms, c_ins, c_outs, c_sems):
                getattr(cm, which)(ci, co, cs)

        if grid:
            pl.when(first)(lambda: run("start"))
        else:
            run("start")
        body(*ins, *outs, *scr)
        if grid:
            pl.when(last)(lambda: run("finish"))
        else:
            run("finish")

    res = pl.pallas_call(
        fused, name=name, grid=grid, in_specs=list(in_specs) + [HBM_SPEC] * sum(n_ci),
        out_specs=out_specs_l + [HBM_SPEC] * sum(n_co), out_shape=out_shape_l + [s for cm in comms for s in cm.outs],
        input_output_aliases=aliases, scratch_shapes=scratch_shapes + [s for cm in comms for s in cm.sems],
        compiler_params=_params(*["arbitrary"] * len(grid)),
    )(*args, *[a for cm in comms for a in cm.ins])
    main = res[:n_out]
    extra, at = [], n_out
    for n in n_co:
        extra.append(list(res[at:at + n]))
        at += n
    return (main[0] if single else list(main)), extra


def _mm(a, b, *, name, out_dtype, trans_b=False, tm=512, tn=None, a_fn=None, extra=(), epi=None,
        b_spec=None, n_out=None, b_chunks=1, comms=()):
    m_total, k_total = a.shape
    if n_out is None:
        n_out = b.shape[0] if trans_b else b.shape[1]
    tn = n_out if tn is None else tn
    grid = (m_total // tm, n_out // tn)
    dims = NT_DIMS if trans_b else None
    kc = k_total // b_chunks

    def body(*refs):
        a_ref, b_ref = refs[0], refs[1]
        extra_refs = refs[2:2 + len(extra)]
        o_ref = refs[2 + len(extra)]
        if b_chunks == 1:
            av = a_ref[...]
            acc = _dot(av if a_fn is None else a_fn(av), b_ref[...], dims)
        else:
            acc = _dot(a_ref[:, 0:kc], b_ref[0], NT_DIMS)
            for k in range(1, b_chunks):
                acc = acc + _dot(a_ref[:, k * kc:(k + 1) * kc], b_ref[k], NT_DIMS)
        if epi is not None:
            acc = epi(acc, *[r[...] for r in extra_refs])
        o_ref[...] = acc.astype(out_dtype)

    if b_spec is None:
        if b_chunks > 1:
            b_spec = pl.BlockSpec((b_chunks, tn, kc), lambda i, j: (0, j, 0))
        elif trans_b:
            b_spec = pl.BlockSpec((tn, k_total), lambda i, j: (j, 0))
        else:
            b_spec = pl.BlockSpec((k_total, tn), lambda i, j: (0, j))
    in_specs = [pl.BlockSpec((tm, k_total), lambda i, j: (i, 0)), b_spec]
    in_specs += [pl.BlockSpec((tm, tn), lambda i, j: (i, j)) for _ in extra]
    return _call(
        body, (a, b, *extra), name=name, grid=grid, in_specs=in_specs,
        out_specs=pl.BlockSpec((tm, tn), lambda i, j: (i, j)),
        out_shape=_sds((m_total, n_out), out_dtype),
        sem=("parallel", "parallel"), comms=comms)


def _mm_tn(a, b, *, name, tk, tn, a_fn=None, out_shape=None, out_spec=None):
    m_total, k_total = a.shape
    n_total = b.shape[1]
    grid = (k_total // tk, n_total // tn)

    def body(a_ref, b_ref, o_ref):
        av = a_ref[...]
        part = _dot(av if a_fn is None else a_fn(av), b_ref[...], TN_DIMS)
        o_ref[...] = part.reshape(o_ref.shape)

    if out_shape is None:
        out_shape = _sds((k_total, n_total), F32)
        out_spec = pl.BlockSpec((tk, tn), lambda i, j: (i, j))
    return pl.pallas_call(
        body, name=name, grid=grid,
        in_specs=[pl.BlockSpec((m_total, tk), lambda i, j: (0, i)), pl.BlockSpec((m_total, tn), lambda i, j: (0, j))],
        out_specs=out_spec, out_shape=out_shape,
        compiler_params=_params("parallel", "parallel"),
    )(a, b)


def _ada_fwd(c_all, w_shard, b_shard):
    nb, ncol = c_all.shape[0], w_shard.shape[1]
    tn = 512

    def body(c_ref, w_ref, b_ref, o_ref):
        c = c_ref[...]
        o_ref[...] = _dot(c * _sigmoid(c), w_ref[...]) + b_ref[...]

    return pl.pallas_call(
        body, name="ada_fwd", grid=(ncol // tn,),
        in_specs=[pl.BlockSpec((nb, D_MODEL), lambda j: (0, 0)), pl.BlockSpec((D_MODEL, tn), lambda j: (0, j)),
                  pl.BlockSpec((1, tn), lambda j: (0, j))],
        out_specs=pl.BlockSpec((nb, tn), lambda j: (0, j)), out_shape=_sds((nb, ncol), F32),
        compiler_params=_params("parallel"),
    )(c_all, w_shard, b_shard)


def _adamw_math(g, w, m, v):
    m = ADAM_B1 * m + (1.0 - ADAM_B1) * g
    v = ADAM_B2 * v + (1.0 - ADAM_B2) * (g * g)
    m_hat = m / (1.0 - ADAM_B1 ** ADAM_STEP)
    v_hat = v / (1.0 - ADAM_B2 ** ADAM_STEP)
    delta = -ADAM_LR * (m_hat / (jnp.sqrt(v_hat) + ADAM_EPS) + ADAM_WD * w)
    return delta, m, v


def _ada_bwd_adamw(c_all, dmod_cols, w, m, v):
    nb, ncol = dmod_cols.shape
    tn = 256

    def body(c_ref, d_ref, w_ref, m_ref, v_ref, g_ref, dl_ref, nm_ref, nv_ref):
        c = c_ref[...]
        g = _dot(c * _sigmoid(c), d_ref[...], TN_DIMS)
        g_ref[...] = g
        dl_ref[...], nm_ref[...], nv_ref[...] = _adamw_math(g, w_ref[...], m_ref[...], v_ref[...])

    col = pl.BlockSpec((D_MODEL, tn), lambda j: (0, j))
    shp = _sds((D_MODEL, ncol), F32)
    return pl.pallas_call(
        body, name="ada_bwd_adamw", grid=(ncol // tn,),
        in_specs=[pl.BlockSpec((nb, D_MODEL), lambda j: (0, 0)), pl.BlockSpec((nb, tn), lambda j: (0, j)), col, col, col],
        out_specs=[col, col, col, col], out_shape=[shp, shp, shp, shp],
        compiler_params=_params("parallel"),
    )(c_all, dmod_cols, w, m, v)


def _adamw_halves(own, theirs, core, w, m, v, *, axis, name):
    r2, c2 = own.shape
    tr = _row_tile(r2)
    nt = r2 // tr

    def body(core_ref, own_ref, their_ref, w_ref, m_ref, v_ref, g_ref, dl_ref, nm_ref, nv_ref):
        g = jnp.where(pl.program_id(0) == core_ref[0], own_ref[...], their_ref[...])
        g_ref[...] = g
        dl_ref[...], nm_ref[...], nv_ref[...] = _adamw_math(g, w_ref[...], m_ref[...], v_ref[...])

    if axis == 0:
        full = pl.BlockSpec((tr, c2), lambda h, i, core_ref: (h * nt + i, 0))
    else:
        full = pl.BlockSpec((tr, c2), lambda h, i, core_ref: (i, h))
    half = pl.BlockSpec((tr, c2), lambda h, i, core_ref: (i, 0))
    shp = _sds(w.shape, F32)
    return pl.pallas_call(
        body, name=name,
        grid_spec=pltpu.PrefetchScalarGridSpec(num_scalar_prefetch=1, grid=(2, nt), in_specs=[half, half, full, full, full],
                                               out_specs=[full] * 4),
        out_shape=[shp] * 4, compiler_params=_params("parallel", "parallel"),
    )(core, own, theirs, w, m, v)


def _tok_spec(tm, width=D_MODEL):
    return pl.BlockSpec((None, tm, width), lambda b, i: (b, i, 0))


def _row_spec(width=D_MODEL):
    return pl.BlockSpec((None, 1, width), lambda b, i: (b, 0, 0))


def _vec_spec(width=D_MODEL):
    return pl.BlockSpec((1, width), lambda b, i: (0, 0))


def _mm_rows(a, b, *, name, tm, extra, extra_specs, out_specs, out_shape, epi, pro=None, trans_b=False, b_chunks=1, comms=()):
    bsz, seq, k_total = a.shape
    kc = k_total // b_chunks
    dims = NT_DIMS if trans_b else None

    def body(*refs):
        a_ref, b_ref = refs[0], refs[1]
        ex, outs = refs[2:2 + len(extra)], refs[2 + len(extra):]
        if b_chunks == 1:
            acc = _dot(a_ref[...] if pro is None else pro(a_ref, ex, outs), b_ref[...], dims)
        else:
            acc = _dot(a_ref[:, 0:kc], b_ref[0], NT_DIMS)
            for k in range(1, b_chunks):
                acc = acc + _dot(a_ref[:, k * kc:(k + 1) * kc], b_ref[k], NT_DIMS)
        epi(acc, ex, outs)

    b_spec = pl.BlockSpec(b.shape, lambda bb, i: (0,) * b.ndim)
    return _call(
        body, (a, b, *extra), name=name, grid=(bsz, seq // tm), in_specs=[_tok_spec(tm, k_total), b_spec, *extra_specs],
        out_specs=out_specs, out_shape=out_shape, sem=("arbitrary", "arbitrary"), comms=comms)


def _in_proj_fused(x, w, sc, sh, w_in_t, comms=()):
    tm = 512

    def pro(x_ref, ex, outs):
        y, _, _ = _rms_fwd(x_ref[...], ex[0][...])
        h = (y * (1.0 + ex[1][...]) + ex[2][...]).astype(BF16)
        outs[0][...] = h
        return h

    def epi(acc, ex, outs):
        outs[1][...] = acc

    cols = w_in_t.shape[0]
    return _mm_rows(x, w_in_t, name="in_proj", tm=tm, extra=(w, sc, sh), extra_specs=[_vec_spec(), _row_spec(), _row_spec()],
                    out_specs=[_tok_spec(tm), _tok_spec(tm, cols)],
                    out_shape=[_sds(x.shape, BF16), _sds(x.shape[:2] + (cols,), F32)], pro=pro, epi=epi, trans_b=True, comms=comms)


def _rope_tables(seq):
    half = ROPE_DIM // 2
    inv_freq = ROPE_THETA ** (-jnp.arange(0, ROPE_DIM, 2, dtype=F32) / ROPE_DIM)
    ang = jnp.arange(seq, dtype=F32)[:, None] * inv_freq[None, :]
    cos, sin = jnp.cos(ang), jnp.sin(ang)
    rest = ATT_HEAD_DIM - ROPE_DIM
    ones, zeros, zh = jnp.ones((seq, rest), F32), jnp.zeros((seq, rest), F32), jnp.zeros((seq, half), F32)
    reps = LANE // ATT_HEAD_DIM
    t_cos = jnp.tile(jnp.concatenate([cos, cos, ones], axis=1), (1, reps))
    t_up = jnp.tile(jnp.concatenate([zh, sin, zeros], axis=1), (1, reps))
    t_dn = jnp.tile(jnp.concatenate([-sin, zh, zeros], axis=1), (1, reps))
    return t_cos, t_up, t_dn


GROUP_ROWS = ATT_GROUP * WINDOW


def _rope_fwd(proj, tables):
    bsz, seq, _ = proj.shape
    nblk = seq // WINDOW
    half = ROPE_DIM // 2
    tm = 2 * WINDOW

    def body(p_ref, c_ref, u_ref, d_ref, q_ref, k_ref, v_ref):
        c, u, d = c_ref[...], u_ref[...], d_ref[...]

        def rope(x):
            return (x * c + pltpu.roll(x, half, 1) * u + pltpu.roll(x, LANE - half, 1) * d).astype(BF16)

        heads_per_slab = LANE // ATT_HEAD_DIM
        for s in range(ATT_WIDTH // LANE):
            slab = rope(p_ref[:, s * LANE:(s + 1) * LANE])
            for part in range(heads_per_slab):
                g, hh = divmod(s * heads_per_slab + part, ATT_GROUP)
                piece = slab[:, part * ATT_HEAD_DIM:(part + 1) * ATT_HEAD_DIM]
                for blk in range(tm // WINDOW):
                    q_ref[blk, g, hh * WINDOW:(hh + 1) * WINDOW, :] = piece[blk * WINDOW:(blk + 1) * WINDOW]
        rk = rope(p_ref[:, ATT_WIDTH:ATT_WIDTH + LANE])
        vv = p_ref[:, ATT_WIDTH + LANE:ATT_COLS].astype(BF16)
        for g in range(ATT_KV_HEADS):
            k_ref[g] = rk[:, g * ATT_HEAD_DIM:(g + 1) * ATT_HEAD_DIM]
            v_ref[g] = vv[:, g * ATT_HEAD_DIM:(g + 1) * ATT_HEAD_DIM]

    tab = pl.BlockSpec((tm, LANE), lambda b, i: (i, 0))
    kv_spec = pl.BlockSpec((None, ATT_KV_HEADS, tm, ATT_HEAD_DIM), lambda b, i: (b, 0, i, 0))
    kv_shape = _sds((bsz, ATT_KV_HEADS, seq, ATT_HEAD_DIM), BF16)
    return pl.pallas_call(
        body, name="rope_fwd", grid=(bsz, seq // tm),
        in_specs=[_tok_spec(tm, ATT_COLS), tab, tab, tab],
        out_specs=[pl.BlockSpec((None, tm // WINDOW, ATT_KV_HEADS, GROUP_ROWS, ATT_HEAD_DIM), lambda b, i: (b, i, 0, 0, 0)),
                   kv_spec, kv_spec],
        out_shape=[_sds((bsz, nblk, ATT_KV_HEADS, GROUP_ROWS, ATT_HEAD_DIM), BF16), kv_shape, kv_shape],
        compiler_params=_params("parallel", "parallel"),
    )(proj, *tables)


def _band_mask(i):
    row = lax.broadcasted_iota(jnp.int32, (GROUP_ROWS, 2 * WINDOW), 0) % WINDOW
    col = lax.broadcasted_iota(jnp.int32, (GROUP_ROWS, 2 * WINDOW), 1)
    prev = jnp.logical_and(jnp.logical_and(col < WINDOW, col > row), i > 0)
    return jnp.logical_or(prev, jnp.logical_and(col >= WINDOW, col - WINDOW <= row))


def _sink_column(sink_ref, g):
    head = lax.broadcasted_iota(jnp.int32, (GROUP_ROWS, 1), 0) // WINDOW
    col = jnp.full((GROUP_ROWS, 1), sink_ref[0, g * ATT_GROUP], F32)
    for hh in range(1, ATT_GROUP):
        col = jnp.where(head == hh, sink_ref[0, g * ATT_GROUP + hh], col)
    return col


def _attn_specs():
    q_spec = pl.BlockSpec((None, None, ATT_KV_HEADS, GROUP_ROWS, ATT_HEAD_DIM), lambda b, i: (b, i, 0, 0, 0))
    kv_cur = pl.BlockSpec((None, ATT_KV_HEADS, WINDOW, ATT_HEAD_DIM), lambda b, i: (b, 0, i, 0))
    kv_prev = pl.BlockSpec((None, ATT_KV_HEADS, WINDOW, ATT_HEAD_DIM), lambda b, i: (b, 0, jnp.maximum(i - 1, 0), 0))
    return q_spec, kv_cur, kv_prev


def _attn_fwd(qh, kh, vh, sinks, w_norm, comms=()):
    bsz, nblk = qh.shape[0], qh.shape[1]
    seq = nblk * WINDOW
    neg = float(jnp.finfo(jnp.float32).min)

    def body(sink_ref, q_ref, kc_ref, kp_ref, vc_ref, vp_ref, w_ref, raw_ref, an_ref, l_ref):
        mask = _band_mask(pl.program_id(1))
        for g in range(ATT_KV_HEADS):
            keys = jnp.concatenate([kp_ref[g], kc_ref[g]], axis=0)
            vals = jnp.concatenate([vp_ref[g], vc_ref[g]], axis=0)
            sink = _sink_column(sink_ref, g)
            s = jnp.where(mask, _dot(q_ref[g], keys, NT_DIMS) * ATT_SCALE, neg)
            m = jnp.maximum(jnp.max(s, axis=-1, keepdims=True), sink)
            p = jnp.where(mask, jnp.exp(s - m), 0.0)
            den = jnp.sum(p, axis=-1, keepdims=True) + jnp.exp(sink - m)
            o = _dot(p / den, vals)
            lse = m + jnp.log(den)
            for hh in range(ATT_GROUP):
                h = g * ATT_GROUP + hh
                raw_ref[:, h * ATT_HEAD_DIM:(h + 1) * ATT_HEAD_DIM] = o[hh * WINDOW:(hh + 1) * WINDOW]
                l_ref[:, h:h + 1] = lse[hh * WINDOW:(hh + 1) * WINDOW]
        y, _, _ = _rms_fwd(raw_ref[...], w_ref[...])
        an_ref[...] = y.astype(BF16)

    cur = lambda width: pl.BlockSpec((None, WINDOW, width), lambda b, i: (b, i, 0))
    q_spec, kv_cur, kv_prev = _attn_specs()
    return _call(
        body, (sinks, qh, kh, kh, vh, vh, w_norm), name="attn_fwd", grid=(bsz, nblk),
        in_specs=[pl.BlockSpec(memory_space=pltpu.SMEM), q_spec, kv_cur, kv_prev, kv_cur, kv_prev, _vec_spec(ATT_WIDTH)],
        out_specs=[cur(ATT_WIDTH), cur(ATT_WIDTH), cur(ATT_Q_HEADS)],
        out_shape=[_sds((bsz, seq, ATT_WIDTH), F32), _sds((bsz, seq, MIX_WIDTH), BF16), _sds((bsz, seq, ATT_Q_HEADS), F32)],
        sem=("parallel", "parallel"), comms=comms)


HG_Q0 = ATT_COLS // LANE
HG_F0 = HG_Q0 + HG_HEADS
HG_I0 = HG_F0 + HG_HEADS
HG_G0 = HG_I0 + HG_HEADS
HG_TOK = 256
HG_NCH = HG_TOK // HG_CHUNK


def _block_masks():
    row = lax.broadcasted_iota(jnp.int32, (HG_TOK, HG_TOK), 0)
    col = lax.broadcasted_iota(jnp.int32, (HG_TOK, HG_TOK), 1)
    same = (row // HG_CHUNK) == (col // HG_CHUNK)
    return jnp.logical_and(same, col <= row), jnp.logical_and(same, col >= row)


def _row_in_chunk():
    return lax.broadcasted_iota(jnp.int32, (HG_TOK, LANE), 0) % HG_CHUNK


def _chunk_cumsum(x, reverse=False):
    ric = _row_in_chunk()
    shift = 1
    while shift < HG_CHUNK:
        if reverse:
            x = x + jnp.where(ric < HG_CHUNK - shift, pltpu.roll(x, HG_TOK - shift, 0), 0.0)
        else:
            x = x + jnp.where(ric >= shift, pltpu.roll(x, shift, 0), 0.0)
        shift *= 2
    return x


def _chunk_rows(rows):
    stacked = jnp.concatenate([r[None] for r in rows], axis=0)
    return jnp.broadcast_to(stacked, (HG_NCH, HG_CHUNK, LANE)).reshape(HG_TOK, LANE)


def _chunk_slices(x):
    return [x[j * HG_CHUNK:(j + 1) * HG_CHUNK] for j in range(HG_NCH)]


def _hgrn_common(tbl, hf, hq):
    lb = _sigmoid(tbl[1:2] - tbl[0:1])
    sig = _sigmoid(hf)
    f = lb + (1.0 - lb) * sig
    sq = _sigmoid(hq)
    q, k = hq * sq, 1.0 - f
    b = _chunk_cumsum(jnp.log(f))
    last = [b[(j + 1) * HG_CHUNK - 1:(j + 1) * HG_CHUNK] for j in range(HG_NCH)]
    bl = _chunk_rows(last)
    e_b, e_nb, e_rem = jnp.exp(b), jnp.exp(-b), jnp.exp(bl - b)
    e_last = [jnp.exp(r) for r in last]
    return dict(lb=lb, sig=sig, f=f, sq=sq, q=q, k=k, e_b=e_b, e_nb=e_nb, e_rem=e_rem, e_last=e_last,
                qd=q * e_b, kd=k * e_nb, ku=k * e_rem)


def _hgrn_fwd(proj, lb_table, norm_w, mix_in, comms=()):
    bsz, seq, _ = proj.shape
    nstep = seq // HG_TOK

    def body(tbl_ref, nw_ref, q_ref, f_ref, i_ref, g_ref, mix_ref, o_ref, rec_ref, st_ref, s_scr):
        @pl.when(pl.program_id(2) == 0)
        def _():
            s_scr[...] = jnp.zeros_like(s_scr)

        v, hg = i_ref[...], g_ref[...]
        t = _hgrn_common(tbl_ref[...], f_ref[...], q_ref[...])
        lower, _ = _block_masks()
        a = jnp.where(lower, _dot(t["qd"], t["kd"], NT_DIMS), 0.0)
        o_intra = _dot(a, v)
        v_c, ku_c, qd_c = _chunk_slices(v.astype(BF16)), _chunk_slices(t["ku"].astype(BF16)), _chunk_slices(t["qd"].astype(BF16))
        updates = [_dot(v_c[j], ku_c[j], TN_DIMS) for j in range(HG_NCH)]
        st = s_scr[...]
        states = []
        for j in range(HG_NCH):
            states.append(st)
            st = st * t["e_last"][j] + updates[j]
        s_scr[...] = st
        o = o_intra + jnp.concatenate([_dot(qd_c[j], states[j], NT_DIMS) for j in range(HG_NCH)], axis=0)
        for j in range(HG_NCH):
            st_ref[j] = states[j]
        o_ref[...] = o
        y, _, _ = _rms_fwd(o, nw_ref[...])
        rec_ref[...] = (y * (hg * _sigmoid(hg))).astype(BF16)

    slab = lambda first: pl.BlockSpec((None, HG_TOK, LANE), lambda b, h, t: (b, t, first + h))
    head_out = pl.BlockSpec((None, HG_TOK, LANE), lambda b, h, t: (b, t, h))
    mix_out = pl.BlockSpec((None, HG_TOK, LANE), lambda b, h, t: (b, t, ATT_WIDTH // LANE + h))
    return _call(
        body, (lb_table, norm_w, proj, proj, proj, proj, mix_in), name="hgrn_fwd", grid=(bsz, HG_HEADS, nstep),
        in_specs=[pl.BlockSpec((2, LANE), lambda b, h, t: (0, h)), pl.BlockSpec((1, LANE), lambda b, h, t: (0, 0)),
                  slab(HG_Q0), slab(HG_F0), slab(HG_I0), slab(HG_G0), pl.BlockSpec(memory_space=pl.ANY)],
        out_specs=[head_out, mix_out,
                   pl.BlockSpec((None, None, HG_NCH, LANE, LANE), lambda b, h, t: (b, h, t, 0, 0))],
        out_shape=[_sds((bsz, seq, HG_WIDTH), F32), _sds(mix_in.shape, BF16),
                   _sds((bsz, HG_HEADS, seq // HG_CHUNK, LANE, LANE), F32)],
        scratch_shapes=[pltpu.VMEM((LANE, LANE), F32)],
        sem=("parallel", "parallel", "arbitrary"), comms=comms, aliases={6: 1})


def _out_proj_fused(cat, w_out, x, post_w, g1, pre_w, sc2, sh2):
    tm = 512

    def epi(mix, ex, outs):
        x_ref, pw_ref, g1_ref, w2_ref, sc_ref, sh_ref = ex
        outs[0][...] = mix
        n1, _, _ = _rms_fwd(mix, pw_ref[...])
        x1 = x_ref[...] + g1_ref[...] * n1
        outs[1][...] = x1
        y2, _, _ = _rms_fwd(x1, w2_ref[...])
        outs[2][...] = (y2 * (1.0 + sc_ref[...]) + sh_ref[...]).astype(BF16)

    return _mm_rows(cat, w_out, name="out_proj", tm=tm, extra=(x, post_w, g1, pre_w, sc2, sh2),
                    extra_specs=[_tok_spec(tm), _vec_spec(), _row_spec(), _vec_spec(), _row_spec(), _row_spec()],
                    out_specs=[_tok_spec(tm), _tok_spec(tm), _tok_spec(tm)],
                    out_shape=[_sds(x.shape, F32), _sds(x.shape, F32), _sds(x.shape, BF16)], epi=epi)


def _acc_out(ref, first, value):
    @pl.when(first)
    def _():
        ref[...] = value

    @pl.when(jnp.logical_not(first))
    def _():
        ref[...] += value


def _down_proj_fused(r, w_down, x1, post_w, g2, target):
    tm = 256
    bsz = x1.shape[0]

    def pro(r_ref, ex, outs):
        rv = r_ref[...]
        return rv * rv

    def epi(down, ex, outs):
        x1_ref, w_ref, g2_ref, t_ref = ex
        loss_ref, dy_ref, dd_ref, dg2_ref, dw_ref = outs
        b, i = pl.program_id(0), pl.program_id(1)
        w, g2v = w_ref[...], g2_ref[...]
        n2, dh, rstd = _rms_fwd(down, w)
        err = x1_ref[...] + g2v * n2 - t_ref[...]
        part = (0.5 / D_MODEL) * jnp.sum(jnp.sum(err * err, axis=-1, keepdims=True), axis=0, keepdims=True)
        _acc_out(loss_ref, jnp.logical_and(b == 0, i == 0), jnp.broadcast_to(part, (1, LANE)))
        dy = err * (1.0 / D_MODEL)
        dy_ref[...] = dy
        _acc_out(dg2_ref, i == 0, _colsum(dy * n2))
        dd, dw_rows = _rms_bwd(dy * g2v, dh, rstd, w)
        dd_ref[...] = dd.astype(BF16)
        _acc_out(dw_ref, jnp.logical_and(b == 0, i == 0), _colsum(dw_rows))

    return _mm_rows(r, w_down, name="down_proj", tm=tm, extra=(x1, post_w, g2, target),
                    extra_specs=[_tok_spec(tm), _vec_spec(), _row_spec(), _tok_spec(tm)],
                    out_specs=[_vec_spec(LANE), _tok_spec(tm), _tok_spec(tm), _row_spec(), _vec_spec()],
                    out_shape=[_sds((1, LANE), F32), _sds(x1.shape, F32), _sds(x1.shape, BF16), _sds((bsz, 1, D_MODEL), F32),
                               _sds((1, D_MODEL), F32)], pro=pro, epi=epi)


def _up_bwd_fused(dpre, w_up4, dy, x1, mix, pre_w, sc2, post_w, g1, comms=()):
    tm = 256
    bsz = x1.shape[0]

    def epi(dh2v, ex, outs):
        dy_ref, x1_ref, mix_ref, w2_ref, sc_ref, pw_ref, g1_ref = ex
        dx1_ref, dmix_ref, dsc_ref, dsh_ref, dg1_ref, dw2_ref, dpw_ref = outs
        b, i = pl.program_id(0), pl.program_id(1)
        first = jnp.logical_and(b == 0, i == 0)
        w2, pw = w2_ref[...], pw_ref[...]
        y2, xh2, rstd2 = _rms_fwd(x1_ref[...], w2)
        _acc_out(dsh_ref, i == 0, _colsum(dh2v))
        _acc_out(dsc_ref, i == 0, _colsum(dh2v * y2))
        dx1n, dw_rows = _rms_bwd(dh2v * (1.0 + sc_ref[...]), xh2, rstd2, w2)
        _acc_out(dw2_ref, first, _colsum(dw_rows))
        dx1 = dy_ref[...] + dx1n
        dx1_ref[...] = dx1
        n1, mh, rstd1 = _rms_fwd(mix_ref[...], pw)
        _acc_out(dg1_ref, i == 0, _colsum(dx1 * n1))
        dmix, dpw_rows = _rms_bwd(dx1 * g1_ref[...], mh, rstd1, pw)
        dmix_ref[...] = dmix.astype(BF16)
        _acc_out(dpw_ref, first, _colsum(dpw_rows))

    row_shape = _sds((bsz, 1, D_MODEL), F32)
    vec_shape = _sds((1, D_MODEL), F32)
    return _mm_rows(dpre, w_up4, name="up_bwd", tm=tm, extra=(dy, x1, mix, pre_w, sc2, post_w, g1),
                    extra_specs=[_tok_spec(tm), _tok_spec(tm), _tok_spec(tm), _vec_spec(), _row_spec(), _vec_spec(), _row_spec()],
                    out_specs=[_tok_spec(tm), _tok_spec(tm), _row_spec(), _row_spec(), _row_spec(), _vec_spec(), _vec_spec()],
                    out_shape=[_sds(x1.shape, F32), _sds(x1.shape, BF16), row_shape, row_shape, row_shape, vec_shape, vec_shape],
                    epi=epi, b_chunks=w_up4.shape[0], comms=comms)


def _norm1_bwd(dh1, dx1, x, pre_w, sc1, tm=512):
    bsz, seq, _ = x.shape

    def body(dh_ref, dx1_ref, x_ref, w_ref, sc_ref, gx_ref, dsc_ref, dsh_ref, dw_ref):
        b, i = pl.program_id(0), pl.program_id(1)
        w = w_ref[...]
        dh = dh_ref[...]
        y, xh, rstd = _rms_fwd(x_ref[...], w)
        _acc_out(dsh_ref, i == 0, _colsum(dh))
        _acc_out(dsc_ref, i == 0, _colsum(dh * y))
        dx, dw_rows = _rms_bwd(dh * (1.0 + sc_ref[...]), xh, rstd, w)
        _acc_out(dw_ref, jnp.logical_and(b == 0, i == 0), _colsum(dw_rows))
        gx_ref[...] = dx1_ref[...] + dx

    row_shape = _sds((bsz, 1, D_MODEL), F32)
    return pl.pallas_call(
        body, name="norm1_bwd", grid=(bsz, seq // tm),
        in_specs=[_tok_spec(tm), _tok_spec(tm), _tok_spec(tm), _vec_spec(), _row_spec()],
        out_specs=[_tok_spec(tm), _row_spec(), _row_spec(), _vec_spec()],
        out_shape=[_sds(x.shape, F32), row_shape, row_shape, _sds((1, D_MODEL), F32)],
        compiler_params=_params("arbitrary", "arbitrary"),
    )(dh1, dx1, x, pre_w, sc1)


def _hgrn_bwd(dcat, proj, o_raw, states, lb_table, norm_w, comms=()):
    bsz, seq, _ = proj.shape
    nstep = seq // HG_TOK
    rec0 = ATT_WIDTH // LANE

    def body(tbl_ref, nw_ref, dr_ref, q_ref, f_ref, i_ref, g_ref, o_ref, st_ref,
             dq_ref, df_ref, di_ref, dg_ref, dlb_ref, dnw_ref, ds_scr):
        h, b, t = pl.program_id(0), pl.program_id(1), pl.program_id(2)

        @pl.when(t == 0)
        def _():
            ds_scr[...] = jnp.zeros_like(ds_scr)

        hq, v, hg = q_ref[...], i_ref[...], g_ref[...]
        nw = nw_ref[...]
        c = _hgrn_common(tbl_ref[...], f_ref[...], hq)
        qd, kd, ku = c["qd"], c["kd"], c["ku"]
        y, on, rstd = _rms_fwd(o_ref[...], nw)
        sg = _sigmoid(hg)
        dr = dr_ref[...]
        dg_ref[...] = (dr * y * (sg * (1.0 + hg * (1.0 - sg)))).astype(BF16)
        do, dnw_rows = _rms_bwd(dr * (hg * sg), on, rstd, nw)
        lower, upper = _block_masks()
        at = jnp.where(upper, _dot(kd, qd, NT_DIMS), 0.0)
        da = jnp.where(lower, _dot(do, v, NT_DIMS), 0.0)
        dat = jnp.where(upper, _dot(v, do, NT_DIMS), 0.0)
        dv = _dot(at, do)
        dqd = _dot(da, kd)
        dkd = _dot(dat, qd)
        do_c, qd_c, v_c, ku_c = [_chunk_slices(z.astype(BF16)) for z in (do, qd, v, ku)]
        outer = [_dot(do_c[j], qd_c[j], TN_DIMS) for j in range(HG_NCH)]
        ds = ds_scr[...]
        ds_after = [None] * HG_NCH
        for j in reversed(range(HG_NCH)):
            ds_after[j] = ds
            ds = outer[j] + ds * c["e_last"][j]
        ds_scr[...] = ds
        states = [st_ref[j] for j in range(HG_NCH)]
        dv = dv + jnp.concatenate([_dot(ku_c[j], ds_after[j], NT_DIMS) for j in range(HG_NCH)], axis=0)
        dqd = dqd + jnp.concatenate([_dot(do_c[j], states[j]) for j in range(HG_NCH)], axis=0)
        dku = jnp.concatenate([_dot(v_c[j], ds_after[j]) for j in range(HG_NCH)], axis=0)
        dku_ku = dku * ku
        dbl = [_colsum(states[j] * ds_after[j]) * c["e_last"][j] + _colsum(dku_ku[j * HG_CHUNK:(j + 1) * HG_CHUNK])
               for j in range(HG_NCH)]
        dk = dkd * c["e_nb"] + dku * c["e_rem"]
        db = dqd * qd - dkd * kd - dku_ku + jnp.where(_row_in_chunk() == HG_CHUNK - 1, _chunk_rows(dbl), 0.0)
        dfv = _chunk_cumsum(db, reverse=True) / c["f"] - dk
        sig, sq = c["sig"], c["sq"]
        df_ref[...] = (dfv * (1.0 - c["lb"]) * sig * (1.0 - sig)).astype(BF16)
        dq_ref[...] = (dqd * c["e_b"] * (sq * (1.0 + hq * (1.0 - sq)))).astype(BF16)
        di_ref[...] = dv.astype(BF16)
        _acc_out(dlb_ref, jnp.logical_and(b == 0, t == 0), _colsum(dfv * (1.0 - sig)))
        _acc_out(dnw_ref, jnp.logical_and(h == 0, jnp.logical_and(b == 0, t == 0)), _colsum(dnw_rows))

    rev = lambda t: nstep - 1 - t
    slab = lambda first: pl.BlockSpec((None, HG_TOK, LANE), lambda h, b, t: (b, rev(t), first + h))
    head = pl.BlockSpec((None, HG_TOK, LANE), lambda h, b, t: (b, rev(t), h))
    grad_shape = _sds((bsz, seq, HG_WIDTH), BF16)
    return _call(
        body, (lb_table, norm_w, dcat, proj, proj, proj, proj, o_raw, states), name="hgrn_bwd", grid=(HG_HEADS, bsz, nstep),
        in_specs=[pl.BlockSpec((2, LANE), lambda h, b, t: (0, h)), pl.BlockSpec((1, LANE), lambda h, b, t: (0, 0)),
                  slab(rec0), slab(HG_Q0), slab(HG_F0), slab(HG_I0), slab(HG_G0), head,
                  pl.BlockSpec((None, None, HG_NCH, LANE, LANE), lambda h, b, t: (b, h, rev(t), 0, 0))],
        out_specs=[head, head, head, head, pl.BlockSpec((1, LANE), lambda h, b, t: (0, h)),
                   pl.BlockSpec((1, LANE), lambda h, b, t: (0, 0))],
        out_shape=[grad_shape, grad_shape, grad_shape, grad_shape, _sds((1, HG_WIDTH), F32), _sds((1, LANE), F32)],
        scratch_shapes=[pltpu.VMEM((LANE, LANE), F32)],
        sem=("arbitrary", "arbitrary", "arbitrary"), comms=comms)


def _attn_bwd(dcat, raw, w_norm, qh, kh, vh, lse, sinks, comms=()):
    bsz, nblk = qh.shape[0], qh.shape[1]
    seq = nblk * WINDOW

    def body(sink_ref, da_ref, raw_ref, w_ref, q_ref, kc_ref, kp_ref, vc_ref, vp_ref, l_ref,
             dq_ref, dkd_ref, dkp_ref, dvd_ref, dvp_ref, dw_ref, dsink_ref):
        b, i = pl.program_id(0), pl.program_id(1)
        first = jnp.logical_and(b == 0, i == 0)
        mask = _band_mask(i)
        raw_v = raw_ref[...]
        w = w_ref[...]
        _, on, rstd = _rms_fwd(raw_v, w)
        do_all, dw_rows = _rms_bwd(da_ref[...], on, rstd, w)
        _acc_out(dw_ref, first, _colsum(dw_rows))
        lane8 = lax.broadcasted_iota(jnp.int32, (1, ATT_Q_HEADS), 1)
        dsink = jnp.zeros((1, ATT_Q_HEADS), F32)
        for g in range(ATT_KV_HEADS):
            gs = slice(g * ATT_HEAD_DIM, (g + 1) * ATT_HEAD_DIM)
            heads = [slice((g * ATT_GROUP + hh) * ATT_HEAD_DIM, (g * ATT_GROUP + hh + 1) * ATT_HEAD_DIM) for hh in range(ATT_GROUP)]
            q = q_ref[g]
            keys = jnp.concatenate([kp_ref[g], kc_ref[g]], axis=0)
            vals = jnp.concatenate([vp_ref[g], vc_ref[g]], axis=0)
            do_g = jnp.concatenate([do_all[:, hs] for hs in heads], axis=0)
            dsum = jnp.concatenate([jnp.sum(do_all[:, hs] * raw_v[:, hs], axis=-1, keepdims=True) for hs in heads], axis=0)
            lse_g = jnp.concatenate([l_ref[:, g * ATT_GROUP + hh:g * ATT_GROUP + hh + 1] for hh in range(ATT_GROUP)], axis=0)
            p = jnp.where(mask, jnp.exp(_dot(q, keys, NT_DIMS) * ATT_SCALE - lse_g), 0.0)
            sink_part = jnp.exp(_sink_column(sink_ref, g) - lse_g) * dsum
            for hh in range(ATT_GROUP):
                head_sum = jnp.sum(sink_part[hh * WINDOW:(hh + 1) * WINDOW], axis=0, keepdims=True)
                dsink = dsink - jnp.where(lane8 == g * ATT_GROUP + hh, head_sum, 0.0)
            ds = p * (_dot(do_g, vals, NT_DIMS) - dsum) * ATT_SCALE
            dq_g = _dot(ds, keys)
            for hh, hs in enumerate(heads):
                dq_ref[:, hs] = dq_g[hh * WINDOW:(hh + 1) * WINDOW]
            dk_g = _dot(ds, q, TN_DIMS)
            dv_g = _dot(p, do_g, TN_DIMS)
            dkp_ref[:, gs], dkd_ref[:, gs] = dk_g[:WINDOW], dk_g[WINDOW:]
            dvp_ref[:, gs], dvd_ref[:, gs] = dv_g[:WINDOW], dv_g[WINDOW:]
        _acc_out(dsink_ref, first, dsink)

    cur = lambda width: pl.BlockSpec((None, WINDOW, width), lambda b, i: (b, i, 0))
    q_spec, kv_cur, kv_prev = _attn_specs()
    kv_shape = _sds((bsz, seq, LANE), F32)
    return _call(
        body, (sinks, dcat, raw, w_norm, qh, kh, kh, vh, vh, lse), name="attn_bwd", grid=(bsz, nblk),
        in_specs=[pl.BlockSpec(memory_space=pltpu.SMEM), cur(ATT_WIDTH), cur(ATT_WIDTH), _vec_spec(ATT_WIDTH), q_spec,
                  kv_cur, kv_prev, kv_cur, kv_prev, cur(ATT_Q_HEADS)],
        out_specs=[cur(ATT_WIDTH), cur(LANE), cur(LANE), cur(LANE), cur(LANE), _vec_spec(ATT_WIDTH), _vec_spec(ATT_Q_HEADS)],
        out_shape=[_sds((bsz, seq, ATT_WIDTH), F32), kv_shape, kv_shape, kv_shape, kv_shape, _sds((1, ATT_WIDTH), F32),
                   _sds((1, ATT_Q_HEADS), F32)],
        sem=("arbitrary", "arbitrary"), comms=comms)


def _rope_bwd(dq, dkd, dkp, dvd, dvp, tables):
    bsz, seq, _ = dq.shape
    nblk = seq // WINDOW
    half = ROPE_DIM // 2

    def body(dq_ref, dkd_ref, dkp_ref, dvd_ref, dvp_ref, c_ref, u_ref, d_ref, o_ref):
        c, u, d = c_ref[...], u_ref[...], d_ref[...]
        has_next = pl.program_id(1) < nblk - 1

        def unrope(g):
            return g * c + pltpu.roll(g * u, LANE - half, 1) + pltpu.roll(g * d, half, 1)

        for s in range(ATT_WIDTH // LANE):
            o_ref[:, s * LANE:(s + 1) * LANE] = unrope(dq_ref[:, s * LANE:(s + 1) * LANE]).astype(BF16)
        dk = dkd_ref[...] + jnp.where(has_next, dkp_ref[...], 0.0)
        o_ref[:, ATT_WIDTH:ATT_WIDTH + LANE] = unrope(dk).astype(BF16)
        o_ref[:, ATT_WIDTH + LANE:ATT_COLS] = (dvd_ref[...] + jnp.where(has_next, dvp_ref[...], 0.0)).astype(BF16)

    cur = lambda width: pl.BlockSpec((None, WINDOW, width), lambda b, i: (b, i, 0))
    nxt = pl.BlockSpec((None, WINDOW, LANE), lambda b, i: (b, jnp.minimum(i + 1, nblk - 1), 0))
    tab = pl.BlockSpec((WINDOW, LANE), lambda b, i: (i, 0))
    return pl.pallas_call(
        body, name="rope_bwd", grid=(bsz, nblk),
        in_specs=[cur(ATT_WIDTH), cur(LANE), nxt, cur(LANE), nxt, tab, tab, tab],
        out_specs=cur(ATT_COLS), out_shape=_sds((bsz, seq, ATT_COLS), BF16),
        compiler_params=_params("parallel", "parallel"),
    )(dq, dkd, dkp, dvd, dvp, *tables)


def _other_chips(x, y):
    return [(1 - x, y), (x, 1 - y), (1 - x, 1 - y)]


def _sem_pair(n):
    return [pltpu.SemaphoreType.DMA((n,)), pltpu.SemaphoreType.DMA((n,))]


def _plan_chip_gather(blocks, bufs):
    n = len(blocks)

    def copies(ins, outs, sems):
        x, y, c = _mesh_pos()
        sends, lands = [], []
        for a in range(n):
            for j, chip in enumerate(_other_chips(x, y)):
                k = 3 * a + j
                sends.append(pltpu.make_async_remote_copy(
                    src_ref=ins[a], dst_ref=outs[a].at[4 * x + 2 * y + c], send_sem=sems[0].at[k], recv_sem=sems[1].at[k],
                    device_id=(*chip, c), device_id_type=MESH))
                slot = outs[a].at[4 * chip[0] + 2 * chip[1] + c]
                lands.append(pltpu.make_async_remote_copy(
                    src_ref=slot, dst_ref=slot, send_sem=sems[0].at[k], recv_sem=sems[1].at[k],
                    device_id=(*chip, c), device_id_type=MESH))
        return sends, lands

    def start(ins, outs, sems):
        for cp in copies(ins, outs, sems)[0]:
            cp.start()

    def finish(ins, outs, sems):
        sends, lands = copies(ins, outs, sems)
        for cp in lands:
            cp.wait_recv()
        for cp in sends:
            cp.wait_send()

    return _Comm(list(blocks) + list(bufs), [_sds(b.shape, b.dtype) for b in bufs], _sem_pair(3 * n), start, finish,
                 aliases=[(n + a, a) for a in range(n)])


def _plan_pair_forward(bufs):
    n = len(bufs)

    def copies(outs, sems):
        x, y, c = _mesh_pos()
        sends, lands = [], []
        for a in range(n):
            for j, chip in enumerate(_other_chips(x, y)):
                k = 3 * a + j
                slot = outs[a].at[4 * chip[0] + 2 * chip[1] + c]
                sends.append(pltpu.make_async_remote_copy(
                    src_ref=slot, dst_ref=slot, send_sem=sems[0].at[k], recv_sem=sems[1].at[k],
                    device_id=(x, y, 1 - c), device_id_type=MESH))
                theirs = outs[a].at[4 * chip[0] + 2 * chip[1] + 1 - c]
                lands.append(pltpu.make_async_remote_copy(
                    src_ref=theirs, dst_ref=theirs, send_sem=sems[0].at[k], recv_sem=sems[1].at[k],
                    device_id=(x, y, 1 - c), device_id_type=MESH))
        return sends, lands

    def start(ins, outs, sems):
        for cp in copies(outs, sems)[0]:
            cp.start()

    def finish(ins, outs, sems):
        sends, lands = copies(outs, sems)
        for cp in lands:
            cp.wait_recv()
        for cp in sends:
            cp.wait_send()

    return _Comm(list(bufs), [_sds(b.shape, b.dtype) for b in bufs], _sem_pair(3 * n), start, finish,
                 aliases=[(a, a) for a in range(n)])


def _plan_pair(arrays, other_half):
    n = len(arrays)
    per = N_CHIPS if other_half == "chip_major" else 1

    def copies(ins, outs, sems):
        x, y, c = _mesh_pos()
        out = []
        for a in range(n):
            for k in range(per):
                if other_half == "chip_major":
                    src, dst = ins[a].at[k, 1 - c], outs[a].at[k]
                else:
                    src, dst = (ins[a].at[1 - c] if other_half else ins[a]), outs[a]
                out.append(pltpu.make_async_remote_copy(
                    src_ref=src, dst_ref=dst, send_sem=sems[0].at[per * a + k], recv_sem=sems[1].at[per * a + k],
                    device_id=(x, y, 1 - c), device_id_type=MESH))
        return out

    def start(ins, outs, sems):
        for cp in copies(ins, outs, sems):
            cp.start()

    def finish(ins, outs, sems):
        for cp in copies(ins, outs, sems):
            cp.wait()

    if other_half == "chip_major":
        shapes = [_sds((a.shape[0],) + a.shape[2:], a.dtype) for a in arrays]
    else:
        shapes = [_sds(a.shape[1:] if other_half else a.shape, a.dtype) for a in arrays]
    return _Comm(list(arrays), shapes, _sem_pair(per * n), start, finish)


def _plan_chip_exchange(arrays):
    n = len(arrays)

    def copies(ins, outs, sems):
        x, y, c = _mesh_pos()
        sends, lands = [], []
        for a in range(n):
            for j, chip in enumerate(_other_chips(x, y)):
                k = 3 * a + j
                sends.append(pltpu.make_async_remote_copy(
                    src_ref=ins[a].at[2 * chip[0] + chip[1]], dst_ref=outs[a].at[2 * x + y], send_sem=sems[0].at[k],
                    recv_sem=sems[1].at[k], device_id=(*chip, c), device_id_type=MESH))
                slot = outs[a].at[2 * chip[0] + chip[1]]
                lands.append(pltpu.make_async_remote_copy(
                    src_ref=slot, dst_ref=slot, send_sem=sems[0].at[k], recv_sem=sems[1].at[k],
                    device_id=(*chip, c), device_id_type=MESH))
        return sends, lands

    def start(ins, outs, sems):
        for cp in copies(ins, outs, sems)[0]:
            cp.start()

    def finish(ins, outs, sems):
        sends, lands = copies(ins, outs, sems)
        for cp in lands:
            cp.wait_recv()
        for cp in sends:
            cp.wait_send()

    return _Comm(list(arrays), [_sds(a.shape, a.dtype) for a in arrays], _sem_pair(3 * n), start, finish)


def _comm_only(comms, name):
    return _call(lambda: None, (), name=name, grid=(), in_specs=[], out_specs=[], out_shape=[], sem=(), comms=comms)[1]


def _allgather8(arrays, name):
    return _comm_only([_plan_allgather8(arrays)], name)[0]


def _plan_allgather8(arrays):
    n = len(arrays)

    def parts(ins, outs, sems):
        send_sems, recv_sems, local_sems = sems
        x, y, c = _mesh_pos()
        me, sibling = (x, y, c), (x, y, 1 - c)
        chips = _other_chips(x, y)

        def copy(a, k, block, to, src=None):
            dst = outs[a].at[4 * block[0] + 2 * block[1] + block[2]]
            return pltpu.make_async_remote_copy(
                src_ref=dst if src is None else src, dst_ref=dst, send_sem=send_sems.at[7 * a + k],
                recv_sem=recv_sems.at[7 * a + k], device_id=to, device_id_type=MESH)

        mine = [pltpu.make_async_copy(ins[a], outs[a].at[4 * x + 2 * y + c], local_sems.at[a]) for a in range(n)]
        first = []
        for a in range(n):
            first.append(copy(a, 0, me, sibling, src=ins[a]))
            first += [copy(a, 1 + j, me, (*chip, c), src=ins[a]) for j, chip in enumerate(chips)]
        return copy, mine, first, me, sibling, chips, c

    def start(ins, outs, sems):
        _, mine, first, *_ = parts(ins, outs, sems)
        for cp in mine + first:
            cp.start()

    def finish(ins, outs, sems):
        copy, mine, first, me, sibling, chips, c = parts(ins, outs, sems)
        passed = []
        for j, chip in enumerate(chips):
            for a in range(n):
                copy(a, 1 + j, (*chip, c), me).wait_recv()
                fwd = copy(a, 4 + j, (*chip, c), sibling)
                fwd.start()
                passed.append(fwd)
        for a in range(n):
            copy(a, 0, sibling, me).wait_recv()
            for j, chip in enumerate(chips):
                copy(a, 4 + j, (*chip, 1 - c), me).wait_recv()
        for cp in first + passed:
            cp.wait_send()
        for cp in mine:
            cp.wait()

    sems = [pltpu.SemaphoreType.DMA((7 * n,)), pltpu.SemaphoreType.DMA((7 * n,)), pltpu.SemaphoreType.DMA((n,))]
    return _Comm(list(arrays), [_sds((N_DEV,) + a.shape, a.dtype) for a in arrays], sems, start, finish)


def _pair_sum(g, q, core, name, chip_major=False):
    rows, cols = g.shape[2:]
    tr = _row_tile(rows)

    def body(core_ref, g_ref, q_ref, o_ref):
        o_ref[...] = (g_ref[...] + q_ref[...]).astype(BF16)

    blk = pl.BlockSpec((None, tr, cols), lambda k, i, core_ref: (k, i, 0))
    if chip_major:
        own = pl.BlockSpec((None, None, tr, cols), lambda k, i, core_ref: (k, core_ref[0], i, 0))
    else:
        own = pl.BlockSpec((None, None, tr, cols), lambda k, i, core_ref: (core_ref[0], k, i, 0))
    return pl.pallas_call(
        body, name=name,
        grid_spec=pltpu.PrefetchScalarGridSpec(num_scalar_prefetch=1, grid=(N_CHIPS, rows // tr), in_specs=[own, blk], out_specs=blk),
        out_shape=_sds((N_CHIPS, rows, cols), BF16), compiler_params=_params("parallel", "parallel"),
    )(core, g, q)


def _sum_chips(own, landed, chip, name):
    _, rows, cols = own.shape
    tr = _row_tile(rows)

    def body(chip_ref, own_ref, a_ref, b_ref, c_ref, o_ref):
        acc = own_ref[...].astype(F32) + a_ref[...].astype(F32)
        o_ref[...] = (acc + b_ref[...].astype(F32)) + c_ref[...].astype(F32)

    blk = lambda flip: pl.BlockSpec((None, tr, cols), lambda i, chip_ref: (jnp.bitwise_xor(chip_ref[0], flip), i, 0))
    return pl.pallas_call(
        body, name=name,
        grid_spec=pltpu.PrefetchScalarGridSpec(num_scalar_prefetch=1, grid=(rows // tr,), in_specs=[blk(0), blk(1), blk(2), blk(3)],
                                               out_specs=pl.BlockSpec((tr, cols), lambda i, chip_ref: (i, 0))),
        out_shape=_sds((rows, cols), F32), compiler_params=_params("parallel"),
    )(chip, own, landed, landed, landed)


SMALL_ROWS = 120
PACK_ROWS = 168


def _rows(a, nrows):
    flat = a.reshape(-1)
    return jnp.pad(flat, (0, nrows * LANE - flat.shape[0])).reshape(nrows, LANE)


def _pack_small(b_ada, pre_w_mix, post_w_mix, pre_w_mlp, post_w_mlp, attn_out_w, hg_norm_w, attn_sinks, lb_table):
    return jnp.concatenate([
        _rows(b_ada, 48), _rows(pre_w_mix, 8), _rows(post_w_mix, 8), _rows(pre_w_mlp, 8), _rows(post_w_mlp, 8),
        _rows(attn_out_w, 8), _rows(hg_norm_w, 8), _rows(attn_sinks, 8), _rows(lb_table[0], 8), _rows(lb_table[1], 8)], axis=0)


def _unpack_small(p):
    vec = lambda lo, n: p[lo:lo + n // LANE].reshape(1, n)
    lb = jnp.stack([p[104:108].reshape(HG_WIDTH), p[112:116].reshape(HG_WIDTH)])
    return dict(b_ada=vec(0, N_MOD * D_MODEL), pre_w_mix=vec(48, D_MODEL), post_w_mix=vec(56, D_MODEL), pre_w_mlp=vec(64, D_MODEL),
                post_w_mlp=vec(72, D_MODEL), attn_out_w=vec(80, ATT_WIDTH), hg_norm_w=p[88:89], attn_sinks=p[96:97, :ATT_Q_HEADS],
                lb_table=lb)


def _small_update(packs, w, m, v):
    def body(p_ref, w_ref, m_ref, v_ref, g_ref, dl_ref, nm_ref, nv_ref, loss_ref):
        tot = p_ref[0]
        for d in range(1, N_DEV):
            tot = tot + p_ref[d]
        wv = w_ref[...]
        p1 = _sigmoid(wv[112:120] - wv[104:112])
        s = tot[152:160] * p1 * (1.0 - p1)
        g = jnp.concatenate([tot[0:48] + tot[48:96], tot[96:152], -s, s], axis=0)
        g_ref[...] = g
        dl_ref[...], nm_ref[...], nv_ref[...] = _adamw_math(g, wv, m_ref[...], v_ref[...])
        loss_ref[...] = tot[160:168]

    shp = _sds((SMALL_ROWS, LANE), F32)
    return pl.pallas_call(body, name="small_update", out_shape=[shp] * 4 + [_sds((8, LANE), F32)],
                          compiler_params=_params())(packs, w, m, v)


def kernel(x, c, w_ada, b_ada, pre_w_mix, w_in, attn_sinks, attn_out_w, lb_table, hg_norm_w, w_out, post_w_mix, pre_w_mlp, w_up, w_down, post_w_mlp, loss_target, m_w_ada, m_b_ada, m_pre_w_mix, m_w_in, m_attn_sinks, m_attn_out_w, m_lb_table, m_hg_norm_w, m_w_out, m_post_w_mix, m_pre_w_mlp, m_w_up, m_w_down, m_post_w_mlp, v_w_ada, v_b_ada, v_pre_w_mix, v_w_in, v_attn_sinks, v_attn_out_w, v_lb_table, v_hg_norm_w, v_w_out, v_post_w_mix, v_pre_w_mlp, v_w_up, v_w_down, v_post_w_mlp):
    xi, yi, ci = _mesh_pos()
    chip = 2 * xi + yi
    dev = 2 * chip + ci
    bsz, seq, _ = x.shape
    ntok = bsz * seq
    ada_cols = w_ada.shape[2]
    core = jnp.reshape(ci, (1,)).astype(jnp.int32)
    chip_idx = jnp.reshape(chip, (1,)).astype(jnp.int32)
    flat = lambda a: a.reshape(ntok, a.shape[-1])
    unflat = lambda a: a.reshape(bsz, seq, a.shape[-1])
    tables = _rope_tables(seq)

    def row_half(w):
        rows = w.shape[1] // 2
        return lax.dynamic_slice_in_dim(w[0], ci * rows, rows, axis=0).astype(BF16)

    def gather_buffer(w):
        rows, cols = w.shape[1] // 2, w.shape[2]
        own = w[0].astype(BF16).reshape(2, rows, cols)
        return lax.dynamic_update_slice(jnp.zeros((N_DEV, rows, cols), BF16), own, (2 * chip, 0, 0))

    w_in_t, m_in_t, v_in_t = [jnp.transpose(a[0])[None] for a in (w_in, m_w_in, v_w_in)]
    c_g, in_g = _allgather8([c, row_half(w_in_t)], "gather_first")
    c_all = c_g.reshape(N_DEV * bsz, D_MODEL)
    w_in_full = in_g.reshape(IN_COLS, D_MODEL)

    b_cols = lax.dynamic_slice_in_dim(b_ada, chip * ada_cols, ada_cols, axis=1)
    mod_part = _ada_fwd(c_all, w_ada[0], b_cols)
    half_rows = mod_part.shape[0] // 2
    (mod_g,) = _allgather8([lax.dynamic_slice_in_dim(mod_part, ci * half_rows, half_rows, axis=0)], "gather_mod")
    mod_all = mod_g.reshape(N_CHIPS, 2, half_rows, ada_cols).transpose(1, 2, 0, 3).reshape(N_DEV * bsz, N_MOD * D_MODEL)
    mod = lax.dynamic_slice_in_dim(mod_all, dev * bsz, bsz, axis=0)
    sh1, sc1, g1, sh2, sc2, g2 = [mod[:, i * D_MODEL:(i + 1) * D_MODEL].reshape(bsz, 1, D_MODEL) for i in range(N_MOD)]

    (h1, proj), ((out_g,),) = _in_proj_fused(x, pre_w_mix, sc1, sh1, w_in_full,
                                             comms=[_plan_chip_gather([row_half(w_out)], [gather_buffer(w_out)])])
    qh, kh, vh = _rope_fwd(proj, tables)
    (attn_raw, cat, lse), ((up_g,), (out_g,)) = _attn_fwd(
        qh, kh, vh, attn_sinks, attn_out_w,
        comms=[_plan_chip_gather([row_half(w_up)], [gather_buffer(w_up)]), _plan_pair_forward([out_g])])
    (o_raw, cat, states), ((down_g,), (up_g,)) = _hgrn_fwd(
        proj, lb_table, hg_norm_w, cat,
        comms=[_plan_chip_gather([row_half(w_down)], [gather_buffer(w_down)]), _plan_pair_forward([up_g])])
    w_out_full = out_g.reshape(D_MODEL, D_MODEL)
    w_up4 = up_g.reshape(N_CHIPS, D_MODEL, D_MODEL)
    mix, x1, h2 = _out_proj_fused(cat, w_out_full, x, post_w_mix, g1, pre_w_mlp, sc2, sh2)
    big_tm = min(ntok, 1024)
    up_spec = pl.BlockSpec((None, D_MODEL, D_MODEL), lambda i, j: (j, 0, 0))
    r, ((down_g,),) = _mm(flat(h2), w_up4, name="up_proj", out_dtype=BF16, tm=big_tm, tn=D_MODEL, n_out=D_FF, b_spec=up_spec,
                          epi=lambda acc: jnp.maximum(acc, 0.0), comms=[_plan_pair_forward([down_g])])
    w_down_full = down_g.reshape(D_FF, D_MODEL)
    square = lambda t: t * t
    loss_row, dy, dd, dg2, d_post_mlp = _down_proj_fused(unflat(r), w_down_full, x1, post_w_mlp, g2, loss_target)

    dpre = _mm(flat(dd), w_down_full, name="down_bwd", out_dtype=BF16, trans_b=True, tm=big_tm, tn=D_MODEL, extra=(r,),
               epi=lambda acc, rt: acc * (2.0 * rt.astype(F32)))
    half_rows = D_MODEL // 2
    g_down = _mm_tn(r, flat(dd), name="down_wgrad", tk=half_rows, tn=D_MODEL, a_fn=square,
                    out_shape=_sds((2, N_CHIPS, half_rows, D_MODEL), F32),
                    out_spec=pl.BlockSpec((None, None, half_rows, D_MODEL), lambda i, j: (i % 2, i // 2, 0, 0)))
    (dx1, dmix, dsc2, dsh2, dg1, d_pre_mlp, d_post_mix), ((q_down,),) = _up_bwd_fused(
        unflat(dpre), w_up4, dy, x1, mix, pre_w_mlp, sc2, post_w_mix, g1, comms=[_plan_pair([g_down], True)])
    g_up = _mm_tn(flat(h2), dpre, name="up_wgrad", tk=D_MODEL, tn=half_rows,
                  out_shape=_sds((2, N_CHIPS, half_rows, D_MODEL), F32),
                  out_spec=pl.BlockSpec((2, None, half_rows, half_rows), lambda i, j: (0, j // 2, 0, j % 2)))
    s_down = _pair_sum(g_down, q_down, core, "pair_sum_down")

    dcat, ((q_up,),) = _mm(flat(dmix), w_out_full, name="out_bwd", out_dtype=F32, trans_b=True, comms=[_plan_pair([g_up], True)])
    dcat = unflat(dcat)
    s_up = _pair_sum(g_up, q_up, core, "pair_sum_up")
    out_rows = D_MODEL // N_CHIPS
    g_out = _mm_tn(flat(cat), flat(dmix), name="out_wgrad", tk=out_rows, tn=half_rows,
                   out_shape=_sds((2, N_CHIPS, out_rows, half_rows), F32),
                   out_spec=pl.BlockSpec((None, None, out_rows, half_rows), lambda i, j: (j, i, 0, 0)))
    (dhq, dhf, dhi, dhg, d_lb, d_hg_norm), ((x_down, x_up), (q_out,)) = _hgrn_bwd(
        dcat, proj, o_raw, states, lb_table, hg_norm_w, comms=[_plan_chip_exchange([s_down, s_up]), _plan_pair([g_out], True)])
    half_down = _sum_chips(s_down, x_down, chip_idx, "sum_chips_down")
    half_up = _sum_chips(s_up, x_up, chip_idx, "sum_chips_up")
    s_out = _pair_sum(g_out, q_out, core, "pair_sum_out")
    (dq, dkd, dkp, dvd, dvp, d_attn_out, d_sinks), ((their_down, their_up), (x_out,)) = _attn_bwd(
        dcat, attn_raw, attn_out_w, qh, kh, vh, lse, attn_sinks,
        comms=[_plan_pair([half_down, half_up], False), _plan_chip_exchange([s_out])])
    half_out = _sum_chips(s_out, x_out, chip_idx, "sum_chips_out")
    dproj_a = _rope_bwd(dq, dkd, dkp, dvd, dvp, tables)
    dproj = flat(jnp.concatenate([dproj_a, dhq, dhf, dhi, dhg], axis=-1))
    in_rows = IN_COLS // N_CHIPS // 2
    g_in = _mm_tn(dproj, flat(h1), name="in_wgrad", tk=2 * LANE, tn=D_MODEL).reshape(N_CHIPS, 2, in_rows, D_MODEL)
    dh1, ((q_in,), (their_out,)) = _mm(dproj, w_in_full, name="in_bwd", out_dtype=F32,
                                       comms=[_plan_pair([g_in], "chip_major"), _plan_pair([half_out], False)])
    s_in = _pair_sum(g_in, q_in, core, "pair_sum_in", chip_major=True)
    grad_x, dsc1, dsh1, d_pre_mix = _norm1_bwd(unflat(dh1), dx1, x, pre_w_mix, sc1)

    dmod = jnp.concatenate([dsh1, dsc1, dg1, dsh2, dsc2, dg2], axis=-1).reshape(bsz, N_MOD * D_MODEL)
    pack = jnp.concatenate([
        _rows(dmod, 96), _rows(d_pre_mix, 8), _rows(d_post_mix, 8), _rows(d_pre_mlp, 8), _rows(d_post_mlp, 8),
        _rows(d_attn_out, 8), _rows(d_hg_norm, 8), _rows(d_sinks, 8), _rows(d_lb, 8), _rows(loss_row, 8)], axis=0)
    (packs,), (x_in,) = _comm_only([_plan_allgather8([pack]), _plan_chip_exchange([s_in])], "gather_small")
    half_in = _sum_chips(s_in, x_in, chip_idx, "sum_chips_in")
    ((their_in,),) = _comm_only([_plan_pair([half_in], False)], "pair_swap_in")
    small_args = lambda pre: (pre["b_ada"], pre["pre_w_mix"], pre["post_w_mix"], pre["pre_w_mlp"], pre["post_w_mlp"],
                              pre["attn_out_w"], pre["hg_norm_w"], pre["attn_sinks"], pre["lb_table"])
    w_small = dict(b_ada=b_ada, pre_w_mix=pre_w_mix, post_w_mix=post_w_mix, pre_w_mlp=pre_w_mlp, post_w_mlp=post_w_mlp,
                   attn_out_w=attn_out_w, hg_norm_w=hg_norm_w, attn_sinks=attn_sinks, lb_table=lb_table)
    m_small = dict(b_ada=m_b_ada, pre_w_mix=m_pre_w_mix, post_w_mix=m_post_w_mix, pre_w_mlp=m_pre_w_mlp, post_w_mlp=m_post_w_mlp,
                   attn_out_w=m_attn_out_w, hg_norm_w=m_hg_norm_w, attn_sinks=m_attn_sinks, lb_table=m_lb_table)
    v_small = dict(b_ada=v_b_ada, pre_w_mix=v_pre_w_mix, post_w_mix=v_post_w_mix, pre_w_mlp=v_pre_w_mlp, post_w_mlp=v_post_w_mlp,
                   attn_out_w=v_attn_out_w, hg_norm_w=v_hg_norm_w, attn_sinks=v_attn_sinks, lb_table=v_lb_table)
    *small_packed, loss_rows = _small_update(packs, _pack_small(*small_args(w_small)), _pack_small(*small_args(m_small)),
                                             _pack_small(*small_args(v_small)))
    small_out = [_unpack_small(p) for p in small_packed]
    loss = loss_rows[0, 0]

    dmod_all = packs[:, :96, :].reshape(N_DEV * bsz, N_MOD * D_MODEL)
    dmod_cols = lax.dynamic_slice_in_dim(dmod_all, chip * ada_cols, ada_cols, axis=1)
    ada_out = _ada_bwd_adamw(c_all, dmod_cols, w_ada[0], m_w_ada[0], v_w_ada[0])

    big = dict(
        w_in=tuple(jnp.transpose(a) for a in _adamw_halves(half_in, their_in, core, w_in_t[0], m_in_t[0], v_in_t[0], axis=0,
                                                           name="adamw_in")),
        w_up=tuple(_adamw_halves(half_up, their_up, core, w_up[0], m_w_up[0], v_w_up[0], axis=0, name="adamw_up")),
        w_out=tuple(_adamw_halves(half_out, their_out, core, w_out[0], m_w_out[0], v_w_out[0], axis=1, name="adamw_out")),
        w_down=tuple(_adamw_halves(half_down, their_down, core, w_down[0], m_w_down[0], v_w_down[0], axis=0, name="adamw_down")),
        w_ada=tuple(ada_out),
    )
    order = ("w_ada", "b_ada", "pre_w_mix", "w_in", "attn_sinks", "attn_out_w", "lb_table", "hg_norm_w", "w_out", "post_w_mix",
             "pre_w_mlp", "w_up", "w_down", "post_w_mlp")
    outs = [loss, grad_x]
    for kind in range(4):
        for nm in order:
            outs.append(big[nm][kind][None] if nm in big else small_out[kind][nm])
    return tuple(outs)
```

```python
import functools

import jax
import jax.numpy as jnp
from jax import lax
from jax.experimental import pallas as pl
from jax.experimental.pallas import tpu as pltpu

F32 = jnp.float32
BF16 = jnp.bfloat16

D_MODEL = 1024
ATT_WIDTH = 512
ATT_HEAD_DIM = 64
ATT_Q_HEADS = 8
ATT_KV_HEADS = 2
ATT_GROUP = ATT_Q_HEADS // ATT_KV_HEADS
ATT_KV_COLS = ATT_KV_HEADS * ATT_HEAD_DIM
WINDOW = 128
ROPE_DIM = 16
ROPE_THETA = 500000.0
HG_WIDTH = 512
MIX_WIDTH = ATT_WIDTH + HG_WIDTH
HG_HEAD_DIM = 128
HG_HEADS = 4
HG_CHUNK = 32
IN_COLS = ATT_WIDTH + 2 * ATT_KV_COLS + 4 * HG_WIDTH
ATT_COLS = ATT_WIDTH + 2 * ATT_KV_COLS
D_FF = 4 * D_MODEL
N_MOD = 6
EPS = 1e-6
ATT_SCALE = ATT_HEAD_DIM ** -0.5

ADAM_LR = 0.001
ADAM_B1 = 0.9
ADAM_B2 = 0.999
ADAM_EPS = 1e-08
ADAM_WD = 0.01
ADAM_STEP = 10

N_CHIPS = 4
N_DEV = 8
LANE = 128
VMEM_LIMIT = 48 * 1024 * 1024
MESH = pl.DeviceIdType.MESH

NT_DIMS = (((1,), (1,)), ((), ()))
TN_DIMS = (((0,), (0,)), ((), ()))


def _sds(shape, dtype):
    return jax.ShapeDtypeStruct(tuple(shape), dtype)


def _params(*sem):
    return pltpu.CompilerParams(dimension_semantics=sem, vmem_limit_bytes=VMEM_LIMIT)


def _sigmoid(x):
    return 1.0 / (1.0 + jnp.exp(-x))


def _dot(a, b, dims=None):
    a, b = a.astype(BF16), b.astype(BF16)
    if dims is None:
        return jnp.dot(a, b, preferred_element_type=F32)
    return lax.dot_general(a, b, dims, preferred_element_type=F32)


def _rms_fwd(x, w):
    rstd = lax.rsqrt(jnp.mean(x * x, axis=-1, keepdims=True) + EPS)
    xh = x * rstd
    return xh * w, xh, rstd


def _rms_bwd(dy, xh, rstd, w):
    dxh = dy * w
    dx = rstd * (dxh - xh * jnp.mean(dxh * xh, axis=-1, keepdims=True))
    return dx, dy * xh


def _colsum(x):
    return jnp.sum(x, axis=0, keepdims=True)


def _row_tile(rows, cap=256):
    return max(t for t in range(16, cap + 1, 16) if rows % t == 0)


HBM_SPEC = pl.BlockSpec(memory_space=pltpu.HBM)


def _mesh_pos():
    return lax.axis_index("x"), lax.axis_index("y"), lax.axis_index("c")


class _Comm:
    def __init__(self, ins, outs, sems, start, finish, aliases=()):
        self.ins, self.outs, self.sems = list(ins), list(outs), list(sems)
        self.start, self.finish, self.aliases = start, finish, tuple(aliases)


def _call(body, args, *, name, grid, in_specs, out_specs, out_shape, sem, scratch_shapes=(), comms=(), aliases=None):
    scratch_shapes = list(scratch_shapes)
    if not comms:
        return pl.pallas_call(body, name=name, grid=grid, in_specs=in_specs, out_specs=out_specs, out_shape=out_shape,
                              input_output_aliases=dict(aliases or {}), scratch_shapes=scratch_shapes,
                              compiler_params=_params(*sem))(*args)
    single = not isinstance(out_shape, (list, tuple))
    out_specs_l = [out_specs] if single else list(out_specs)
    out_shape_l = [out_shape] if single else list(out_shape)
    n_in, n_out, n_scr = len(in_specs), len(out_shape_l), len(scratch_shapes)
    n_ci = [len(cm.ins) for cm in comms]
    n_co = [len(cm.outs) for cm in comms]
    n_cs = [len(cm.sems) for cm in comms]
    aliases = dict(aliases or {})
    for k, cm in enumerate(comms):
        for i, o in cm.aliases:
            aliases[n_in + sum(n_ci[:k]) + i] = n_out + sum(n_co[:k]) + o

    def fused(*refs):
        pos = [0]

        def take(n):
            part = refs[pos[0]:pos[0] + n]
            pos[0] += n
            return part

        ins = take(n_in)
        c_ins = [take(n) for n in n_ci]
        outs = take(n_out)
        c_outs = [take(n) for n in n_co]
        scr = take(n_scr)
        c_sems = [take(n) for n in n_cs]
        first, last = True, True
        for d, size in enumerate(grid):
            first = jnp.logical_and(first, pl.program_id(d) == 0)
            last = jnp.logical_and(last, pl.program_id(d) == size - 1)

        def run(which):
            for cm, ci, co, cs in zip(comms, c_ins, c_outs, c_sems):
                getattr(cm, which)(ci, co, cs)

        if grid:
            pl.when(first)(lambda: run("start"))
        else:
            run("start")
        body(*ins, *outs, *scr)
        if grid:
            pl.when(last)(lambda: run("finish"))
        else:
            run("finish")

    res = pl.pallas_call(
        fused, name=name, grid=grid, in_specs=list(in_specs) + [HBM_SPEC] * sum(n_ci),
        out_specs=out_specs_l + [HBM_SPEC] * sum(n_co), out_shape=out_shape_l + [s for cm in comms for s in cm.outs],
        input_output_aliases=aliases, scratch_shapes=scratch_shapes + [s for cm in comms for s in cm.sems],
        compiler_params=_params(*["arbitrary"] * len(grid)),
    )(*args, *[a for cm in comms for a in cm.ins])
    main = res[:n_out]
    extra, at = [], n_out
    for n in n_co:
        extra.append(list(res[at:at + n]))
        at += n
    return (main[0] if single else list(main)), extra


def _mm(a, b, *, name, out_dtype, trans_b=False, tm=512, tn=None, a_fn=None, extra=(), epi=None,
        b_spec=None, n_out=None, b_chunks=1, comms=()):
    m_total, k_total = a.shape
    if n_out is None:
        n_out = b.shape[0] if trans_b else b.shape[1]
    tn = n_out if tn is None else tn
    grid = (m_total // tm, n_out // tn)
    dims = NT_DIMS if trans_b else None
    kc = k_total // b_chunks

    def body(*refs):
        a_ref, b_ref = refs[0], refs[1]
        extra_refs = refs[2:2 + len(extra)]
        o_ref = refs[2 + len(extra)]
        if b_chunks == 1:
            av = a_ref[...]
            acc = _dot(av if a_fn is None else a_fn(av), b_ref[...], dims)
        else:
            acc = _dot(a_ref[:, 0:kc], b_ref[0], NT_DIMS)
            for k in range(1, b_chunks):
                acc = acc + _dot(a_ref[:, k * kc:(k + 1) * kc], b_ref[k], NT_DIMS)
        if epi is not None:
            acc = epi(acc, *[r[...] for r in extra_refs])
        o_ref[...] = acc.astype(out_dtype)

    if b_spec is None:
        if b_chunks > 1:
            b_spec = pl.BlockSpec((b_chunks, tn, kc), lambda i, j: (0, j, 0))
        elif trans_b:
            b_spec = pl.BlockSpec((tn, k_total), lambda i, j: (j, 0))
        else:
            b_spec = pl.BlockSpec((k_total, tn), lambda i, j: (0, j))
    in_specs = [pl.BlockSpec((tm, k_total), lambda i, j: (i, 0)), b_spec]
    in_specs += [pl.BlockSpec((tm, tn), lambda i, j: (i, j)) for _ in extra]
    return _call(
        body, (a, b, *extra), name=name, grid=grid, in_specs=in_specs,
        out_specs=pl.BlockSpec((tm, tn), lambda i, j: (i, j)),
        out_shape=_sds((m_total, n_out), out_dtype),
        sem=("parallel", "parallel"), comms=comms)


def _mm_tn(a, b, *, name, tk, tn, a_fn=None, out_shape=None, out_spec=None):
    m_total, k_total = a.shape
    n_total = b.shape[1]
    grid = (k_total // tk, n_total // tn)

    def body(a_ref, b_ref, o_ref):
        av = a_ref[...]
        part = _dot(av if a_fn is None else a_fn(av), b_ref[...], TN_DIMS)
        o_ref[...] = part.reshape(o_ref.shape)

    if out_shape is None:
        out_shape = _sds((k_total, n_total), F32)
        out_spec = pl.BlockSpec((tk, tn), lambda i, j: (i, j))
    return pl.pallas_call(
        body, name=name, grid=grid,
        in_specs=[pl.BlockSpec((m_total, tk), lambda i, j: (0, i)), pl.BlockSpec((m_total, tn), lambda i, j: (0, j))],
        out_specs=out_spec, out_shape=out_shape,
        compiler_params=_params("parallel", "parallel"),
    )(a, b)


def _ada_fwd(c_all, w_shard, b_shard):
    nb, ncol = c_all.shape[0], w_shard.shape[1]
    tn = 512

    def body(c_ref, w_ref, b_ref, o_ref):
        c = c_ref[...]
        o_ref[...] = _dot(c * _sigmoid(c), w_ref[...]) + b_ref[...]

    return pl.pallas_call(
        body, name="ada_fwd", grid=(ncol // tn,),
        in_specs=[pl.BlockSpec((nb, D_MODEL), lambda j: (0, 0)), pl.BlockSpec((D_MODEL, tn), lambda j: (0, j)),
                  pl.BlockSpec((1, tn), lambda j: (0, j))],
        out_specs=pl.BlockSpec((nb, tn), lambda j: (0, j)), out_shape=_sds((nb, ncol), F32),
        compiler_params=_params("parallel"),
    )(c_all, w_shard, b_shard)


def _adamw_math(g, w, m, v):
    m = ADAM_B1 * m + (1.0 - ADAM_B1) * g
    v = ADAM_B2 * v + (1.0 - ADAM_B2) * (g * g)
    m_hat = m / (1.0 - ADAM_B1 ** ADAM_STEP)
    v_hat = v / (1.0 - ADAM_B2 ** ADAM_STEP)
    delta = -ADAM_LR * (m_hat / (jnp.sqrt(v_hat) + ADAM_EPS) + ADAM_WD * w)
    return delta, m, v


def _ada_bwd_adamw(c_all, dmod_cols, w, m, v):
    nb, ncol = dmod_cols.shape
    tn = 256

    def body(c_ref, d_ref, w_ref, m_ref, v_ref, g_ref, dl_ref, nm_ref, nv_ref):
        c = c_ref[...]
        g = _dot(c * _sigmoid(c), d_ref[...], TN_DIMS)
        g_ref[...] = g
        dl_ref[...], nm_ref[...], nv_ref[...] = _adamw_math(g, w_ref[...], m_ref[...], v_ref[...])

    col = pl.BlockSpec((D_MODEL, tn), lambda j: (0, j))
    shp = _sds((D_MODEL, ncol), F32)
    return pl.pallas_call(
        body, name="ada_bwd_adamw", grid=(ncol // tn,),
        in_specs=[pl.BlockSpec((nb, D_MODEL), lambda j: (0, 0)), pl.BlockSpec((nb, tn), lambda j: (0, j)), col, col, col],
        out_specs=[col, col, col, col], out_shape=[shp, shp, shp, shp],
        compiler_params=_params("parallel"),
    )(c_all, dmod_cols, w, m, v)


def _adamw_halves(own, theirs, core, w, m, v, *, axis, name):
    r2, c2 = own.shape
    tr = _row_tile(r2)
    nt = r2 // tr

    def body(core_ref, own_ref, their_ref, w_ref, m_ref, v_ref, g_ref, dl_ref, nm_ref, nv_ref):
        g = jnp.where(pl.program_id(0) == core_ref[0], own_ref[...], their_ref[...])
        g_ref[...] = g
        dl_ref[...], nm_ref[...], nv_ref[...] = _adamw_math(g, w_ref[...], m_ref[...], v_ref[...])

    if axis == 0:
        full = pl.BlockSpec((tr, c2), lambda h, i, core_ref: (h * nt + i, 0))
    else:
        full = pl.BlockSpec((tr, c2), lambda h, i, core_ref: (i, h))
    half = pl.BlockSpec((tr, c2), lambda h, i, core_ref: (i, 0))
    shp = _sds(w.shape, F32)
    return pl.pallas_call(
        body, name=name,
        grid_spec=pltpu.PrefetchScalarGridSpec(num_scalar_prefetch=1, grid=(2, nt), in_specs=[half, half, full, full, full],
                                               out_specs=[full] * 4),
        out_shape=[shp] * 4, compiler_params=_params("parallel", "parallel"),
    )(core, own, theirs, w, m, v)


def _tok_spec(tm, width=D_MODEL):
    return pl.BlockSpec((None, tm, width), lambda b, i: (b, i, 0))


def _row_spec(width=D_MODEL):
    return pl.BlockSpec((None, 1, width), lambda b, i: (b, 0, 0))


def _vec_spec(width=D_MODEL):
    return pl.BlockSpec((1, width), lambda b, i: (0, 0))


def _mm_rows(a, b, *, name, tm, extra, extra_specs, out_specs, out_shape, epi, pro=None, trans_b=False, b_chunks=1, comms=()):
    bsz, seq, k_total = a.shape
    kc = k_total // b_chunks
    dims = NT_DIMS if trans_b else None

    def body(*refs):
        a_ref, b_ref = refs[0], refs[1]
        ex, outs = refs[2:2 + len(extra)], refs[2 + len(extra):]
        if b_chunks == 1:
            acc = _dot(a_ref[...] if pro is None else pro(a_ref, ex, outs), b_ref[...], dims)
        else:
            acc = _dot(a_ref[:, 0:kc], b_ref[0], NT_DIMS)
            for k in range(1, b_chunks):
                acc = acc + _dot(a_ref[:, k * kc:(k + 1) * kc], b_ref[k], NT_DIMS)
        epi(acc, ex, outs)

    b_spec = pl.BlockSpec(b.shape, lambda bb, i: (0,) * b.ndim)
    return _call(
        body, (a, b, *extra), name=name, grid=(bsz, seq // tm), in_specs=[_tok_spec(tm, k_total), b_spec, *extra_specs],
        out_specs=out_specs, out_shape=out_shape, sem=("arbitrary", "arbitrary"), comms=comms)


def _in_proj_fused(x, w, sc, sh, w_in_t, comms=()):
    tm = 512

    def pro(x_ref, ex, outs):
        y, _, _ = _rms_fwd(x_ref[...], ex[0][...])
        h = (y * (1.0 + ex[1][...]) + ex[2][...]).astype(BF16)
        outs[0][...] = h
        return h

    def epi(acc, ex, outs):
        outs[1][...] = acc

    cols = w_in_t.shape[0]
    return _mm_rows(x, w_in_t, name="in_proj", tm=tm, extra=(w, sc, sh), extra_specs=[_vec_spec(), _row_spec(), _row_spec()],
                    out_specs=[_tok_spec(tm), _tok_spec(tm, cols)],
                    out_shape=[_sds(x.shape, BF16), _sds(x.shape[:2] + (cols,), F32)], pro=pro, epi=epi, trans_b=True, comms=comms)


def _rope_tables(seq):
    half = ROPE_DIM // 2
    inv_freq = ROPE_THETA ** (-jnp.arange(0, ROPE_DIM, 2, dtype=F32) / ROPE_DIM)
    ang = jnp.arange(seq, dtype=F32)[:, None] * inv_freq[None, :]
    cos, sin = jnp.cos(ang), jnp.sin(ang)
    rest = ATT_HEAD_DIM - ROPE_DIM
    ones, zeros, zh = jnp.ones((seq, rest), F32), jnp.zeros((seq, rest), F32), jnp.zeros((seq, half), F32)
    reps = LANE // ATT_HEAD_DIM
    t_cos = jnp.tile(jnp.concatenate([cos, cos, ones], axis=1), (1, reps))
    t_up = jnp.tile(jnp.concatenate([zh, sin, zeros], axis=1), (1, reps))
    t_dn = jnp.tile(jnp.concatenate([-sin, zh, zeros], axis=1), (1, reps))
    return t_cos, t_up, t_dn


GROUP_ROWS = ATT_GROUP * WINDOW


def _rope_fwd(proj, tables):
    bsz, seq, _ = proj.shape
    nblk = seq // WINDOW
    half = ROPE_DIM // 2
    tm = 2 * WINDOW

    def body(p_ref, c_ref, u_ref, d_ref, q_ref, k_ref, v_ref):
        c, u, d = c_ref[...], u_ref[...], d_ref[...]

        def rope(x):
            return (x * c + pltpu.roll(x, half, 1) * u + pltpu.roll(x, LANE - half, 1) * d).astype(BF16)

        heads_per_slab = LANE // ATT_HEAD_DIM
        for s in range(ATT_WIDTH // LANE):
            slab = rope(p_ref[:, s * LANE:(s + 1) * LANE])
            for part in range(heads_per_slab):
                g, hh = divmod(s * heads_per_slab + part, ATT_GROUP)
                piece = slab[:, part * ATT_HEAD_DIM:(part + 1) * ATT_HEAD_DIM]
                for blk in range(tm // WINDOW):
                    q_ref[blk, g, hh * WINDOW:(hh + 1) * WINDOW, :] = piece[blk * WINDOW:(blk + 1) * WINDOW]
        rk = rope(p_ref[:, ATT_WIDTH:ATT_WIDTH + LANE])
        vv = p_ref[:, ATT_WIDTH + LANE:ATT_COLS].astype(BF16)
        for g in range(ATT_KV_HEADS):
            k_ref[g] = rk[:, g * ATT_HEAD_DIM:(g + 1) * ATT_HEAD_DIM]
            v_ref[g] = vv[:, g * ATT_HEAD_DIM:(g + 1) * ATT_HEAD_DIM]

    tab = pl.BlockSpec((tm, LANE), lambda b, i: (i, 0))
    kv_spec = pl.BlockSpec((None, ATT_KV_HEADS, tm, ATT_HEAD_DIM), lambda b, i: (b, 0, i, 0))
    kv_shape = _sds((bsz, ATT_KV_HEADS, seq, ATT_HEAD_DIM), BF16)
    return pl.pallas_call(
        body, name="rope_fwd", grid=(bsz, seq // tm),
        in_specs=[_tok_spec(tm, ATT_COLS), tab, tab, tab],
        out_specs=[pl.BlockSpec((None, tm // WINDOW, ATT_KV_HEADS, GROUP_ROWS, ATT_HEAD_DIM), lambda b, i: (b, i, 0, 0, 0)),
                   kv_spec, kv_spec],
        out_shape=[_sds((bsz, nblk, ATT_KV_HEADS, GROUP_ROWS, ATT_HEAD_DIM), BF16), kv_shape, kv_shape],
        compiler_params=_params("parallel", "parallel"),
    )(proj, *tables)


def _band_mask(i):
    row = lax.broadcasted_iota(jnp.int32, (GROUP_ROWS, 2 * WINDOW), 0) % WINDOW
    col = lax.broadcasted_iota(jnp.int32, (GROUP_ROWS, 2 * WINDOW), 1)
    prev = jnp.logical_and(jnp.logical_and(col < WINDOW, col > row), i > 0)
    return jnp.logical_or(prev, jnp.logical_and(col >= WINDOW, col - WINDOW <= row))


def _sink_column(sink_ref, g):
    head = lax.broadcasted_iota(jnp.int32, (GROUP_ROWS, 1), 0) // WINDOW
    col = jnp.full((GROUP_ROWS, 1), sink_ref[0, g * ATT_GROUP], F32)
    for hh in range(1, ATT_GROUP):
        col = jnp.where(head == hh, sink_ref[0, g * ATT_GROUP + hh], col)
    return col


def _attn_specs():
    q_spec = pl.BlockSpec((None, None, ATT_KV_HEADS, GROUP_ROWS, ATT_HEAD_DIM), lambda b, i: (b, i, 0, 0, 0))
    kv_cur = pl.BlockSpec((None, ATT_KV_HEADS, WINDOW, ATT_HEAD_DIM), lambda b, i: (b, 0, i, 0))
    kv_prev = pl.BlockSpec((None, ATT_KV_HEADS, WINDOW, ATT_HEAD_DIM), lambda b, i: (b, 0, jnp.maximum(i - 1, 0), 0))
    return q_spec, kv_cur, kv_prev


def _attn_fwd(qh, kh, vh, sinks, w_norm, comms=()):
    bsz, nblk = qh.shape[0], qh.shape[1]
    seq = nblk * WINDOW
    neg = float(jnp.finfo(jnp.float32).min)

    def body(sink_ref, q_ref, kc_ref, kp_ref, vc_ref, vp_ref, w_ref, raw_ref, an_ref, l_ref):
        mask = _band_mask(pl.program_id(1))
        for g in range(ATT_KV_HEADS):
            keys = jnp.concatenate([kp_ref[g], kc_ref[g]], axis=0)
            vals = jnp.concatenate([vp_ref[g], vc_ref[g]], axis=0)
            sink = _sink_column(sink_ref, g)
            s = jnp.where(mask, _dot(q_ref[g], keys, NT_DIMS) * ATT_SCALE, neg)
            m = jnp.maximum(jnp.max(s, axis=-1, keepdims=True), sink)
            p = jnp.where(mask, jnp.exp(s - m), 0.0)
            den = jnp.sum(p, axis=-1, keepdims=True) + jnp.exp(sink - m)
            o = _dot(p / den, vals)
            lse = m + jnp.log(den)
            for hh in range(ATT_GROUP):
                h = g * ATT_GROUP + hh
                raw_ref[:, h * ATT_HEAD_DIM:(h + 1) * ATT_HEAD_DIM] = o[hh * WINDOW:(hh + 1) * WINDOW]
                l_ref[:, h:h + 1] = lse[hh * WINDOW:(hh + 1) * WINDOW]
        y, _, _ = _rms_fwd(raw_ref[...], w_ref[...])
        an_ref[...] = y.astype(BF16)

    cur = lambda width: pl.BlockSpec((None, WINDOW, width), lambda b, i: (b, i, 0))
    q_spec, kv_cur, kv_prev = _attn_specs()
    return _call(
        body, (sinks, qh, kh, kh, vh, vh, w_norm), name="attn_fwd", grid=(bsz, nblk),
        in_specs=[pl.BlockSpec(memory_space=pltpu.SMEM), q_spec, kv_cur, kv_prev, kv_cur, kv_prev, _vec_spec(ATT_WIDTH)],
        out_specs=[cur(ATT_WIDTH), cur(ATT_WIDTH), cur(ATT_Q_HEADS)],
        out_shape=[_sds((bsz, seq, ATT_WIDTH), F32), _sds((bsz, seq, MIX_WIDTH), BF16), _sds((bsz, seq, ATT_Q_HEADS), F32)],
        sem=("parallel", "parallel"), comms=comms)


HG_Q0 = ATT_COLS // LANE
HG_F0 = HG_Q0 + HG_HEADS
HG_I0 = HG_F0 + HG_HEADS
HG_G0 = HG_I0 + HG_HEADS
HG_TOK = 256
HG_NCH = HG_TOK // HG_CHUNK
HG_HPS = 2


def _block_masks():
    row = lax.broadcasted_iota(jnp.int32, (HG_TOK, HG_TOK), 0)
    col = lax.broadcasted_iota(jnp.int32, (HG_TOK, HG_TOK), 1)
    same = (row // HG_CHUNK) == (col // HG_CHUNK)
    return jnp.logical_and(same, col <= row), jnp.logical_and(same, col >= row)


def _row_in_chunk():
    return lax.broadcasted_iota(jnp.int32, (HG_TOK, LANE), 0) % HG_CHUNK


def _chunk_cumsum(x, reverse=False):
    ric = _row_in_chunk()
    shift = 1
    while shift < HG_CHUNK:
        if reverse:
            x = x + jnp.where(ric < HG_CHUNK - shift, pltpu.roll(x, HG_TOK - shift, 0), 0.0)
        else:
            x = x + jnp.where(ric >= shift, pltpu.roll(x, shift, 0), 0.0)
        shift *= 2
    return x


def _chunk_rows(rows):
    stacked = jnp.concatenate([r[None] for r in rows], axis=0)
    return jnp.broadcast_to(stacked, (HG_NCH, HG_CHUNK, LANE)).reshape(HG_TOK, LANE)


def _chunk_slices(x):
    return [x[j * HG_CHUNK:(j + 1) * HG_CHUNK] for j in range(HG_NCH)]


def _hgrn_common(tbl, hf, hq):
    lb = _sigmoid(tbl[1:2] - tbl[0:1])
    sig = _sigmoid(hf)
    f = lb + (1.0 - lb) * sig
    sq = _sigmoid(hq)
    q, k = hq * sq, 1.0 - f
    b = _chunk_cumsum(jnp.log(f))
    last = [b[(j + 1) * HG_CHUNK - 1:(j + 1) * HG_CHUNK] for j in range(HG_NCH)]
    bl = _chunk_rows(last)
    e_b, e_nb, e_rem = jnp.exp(b), jnp.exp(-b), jnp.exp(bl - b)
    e_last = [jnp.exp(r) for r in last]
    return dict(lb=lb, sig=sig, f=f, sq=sq, q=q, k=k, e_b=e_b, e_nb=e_nb, e_rem=e_rem, e_last=e_last,
                qd=q * e_b, kd=k * e_nb, ku=k * e_rem)


def _hgrn_fwd(proj, lb_table, norm_w, mix_in, comms=()):
    bsz, seq, _ = proj.shape
    nstep = seq // HG_TOK

    def body(tbl_ref, nw_ref, q_ref, f_ref, i_ref, g_ref, mix_ref, o_ref, rec_ref, st_ref, s_scr):
        @pl.when(pl.program_id(2) == 0)
        def _():
            s_scr[...] = jnp.zeros_like(s_scr)

        lower, _ = _block_masks()
        for hp in range(HG_HPS):
            ls = slice(hp * LANE, (hp + 1) * LANE)
            v, hg = i_ref[:, ls], g_ref[:, ls]
            t = _hgrn_common(tbl_ref[:, ls], f_ref[:, ls], q_ref[:, ls])
            a = jnp.where(lower, _dot(t["qd"], t["kd"], NT_DIMS), 0.0)
            o_intra = _dot(a, v)
            v_c, ku_c, qd_c = [_chunk_slices(z.astype(BF16)) for z in (v, t["ku"], t["qd"])]
            updates = [_dot(v_c[j], ku_c[j], TN_DIMS) for j in range(HG_NCH)]
            st = s_scr[hp]
            states = []
            for j in range(HG_NCH):
                states.append(st)
                st = st * t["e_last"][j] + updates[j]
            s_scr[hp] = st
            o = o_intra + jnp.concatenate([_dot(qd_c[j], states[j], NT_DIMS) for j in range(HG_NCH)], axis=0)
            for j in range(HG_NCH):
                st_ref[hp, j] = states[j]
            o_ref[:, ls] = o
            y, _, _ = _rms_fwd(o, nw_ref[...])
            rec_ref[:, ls] = (y * (hg * _sigmoid(hg))).astype(BF16)

    width = HG_HPS * LANE
    slab = lambda first: pl.BlockSpec((None, HG_TOK, width), lambda b, h, t: (b, t, first // HG_HPS + h))
    head_out = pl.BlockSpec((None, HG_TOK, width), lambda b, h, t: (b, t, h))
    mix_out = pl.BlockSpec((None, HG_TOK, width), lambda b, h, t: (b, t, ATT_WIDTH // width + h))
    return _call(
        body, (lb_table, norm_w, proj, proj, proj, proj, mix_in), name="hgrn_fwd", grid=(bsz, HG_HEADS // HG_HPS, nstep),
        in_specs=[pl.BlockSpec((2, width), lambda b, h, t: (0, h)), pl.BlockSpec((1, LANE), lambda b, h, t: (0, 0)),
                  slab(HG_Q0), slab(HG_F0), slab(HG_I0), slab(HG_G0), pl.BlockSpec(memory_space=pl.ANY)],
        out_specs=[head_out, mix_out,
                   pl.BlockSpec((None, HG_HPS, HG_NCH, LANE, LANE), lambda b, h, t: (b, h, t, 0, 0))],
        out_shape=[_sds((bsz, seq, HG_WIDTH), F32), _sds(mix_in.shape, BF16),
                   _sds((bsz, HG_HEADS, seq // HG_CHUNK, LANE, LANE), F32)],
        scratch_shapes=[pltpu.VMEM((HG_HPS, LANE, LANE), F32)],
        sem=("parallel", "parallel", "arbitrary"), comms=comms, aliases={6: 1})


def _out_proj_fused(cat, w_out, x, post_w, g1, pre_w, sc2, sh2):
    tm = 512

    def epi(mix, ex, outs):
        x_ref, pw_ref, g1_ref, w2_ref, sc_ref, sh_ref = ex
        outs[0][...] = mix
        n1, _, _ = _rms_fwd(mix, pw_ref[...])
        x1 = x_ref[...] + g1_ref[...] * n1
        outs[1][...] = x1
        y2, _, _ = _rms_fwd(x1, w2_ref[...])
        outs[2][...] = (y2 * (1.0 + sc_ref[...]) + sh_ref[...]).astype(BF16)

    return _mm_rows(cat, w_out, name="out_proj", tm=tm, extra=(x, post_w, g1, pre_w, sc2, sh2),
                    extra_specs=[_tok_spec(tm), _vec_spec(), _row_spec(), _vec_spec(), _row_spec(), _row_spec()],
                    out_specs=[_tok_spec(tm), _tok_spec(tm), _tok_spec(tm)],
                    out_shape=[_sds(x.shape, F32), _sds(x.shape, F32), _sds(x.shape, BF16)], epi=epi)


def _acc_out(ref, first, value):
    @pl.when(first)
    def _():
        ref[...] = value

    @pl.when(jnp.logical_not(first))
    def _():
        ref[...] += value


def _down_proj_fused(r, w_down, x1, post_w, g2, target):
    tm = 256
    bsz = x1.shape[0]

    def pro(r_ref, ex, outs):
        rv = r_ref[...]
        return rv * rv

    def epi(down, ex, outs):
        x1_ref, w_ref, g2_ref, t_ref = ex
        loss_ref, dy_ref, dd_ref, dg2_ref, dw_ref = outs
        b, i = pl.program_id(0), pl.program_id(1)
        w, g2v = w_ref[...], g2_ref[...]
        n2, dh, rstd = _rms_fwd(down, w)
        err = x1_ref[...] + g2v * n2 - t_ref[...]
        part = (0.5 / D_MODEL) * jnp.sum(jnp.sum(err * err, axis=-1, keepdims=True), axis=0, keepdims=True)
        _acc_out(loss_ref, jnp.logical_and(b == 0, i == 0), jnp.broadcast_to(part, (1, LANE)))
        dy = err * (1.0 / D_MODEL)
        dy_ref[...] = dy
        _acc_out(dg2_ref, i == 0, _colsum(dy * n2))
        dd, dw_rows = _rms_bwd(dy * g2v, dh, rstd, w)
        dd_ref[...] = dd.astype(BF16)
        _acc_out(dw_ref, jnp.logical_and(b == 0, i == 0), _colsum(dw_rows))

    return _mm_rows(r, w_down, name="down_proj", tm=tm, extra=(x1, post_w, g2, target),
                    extra_specs=[_tok_spec(tm), _vec_spec(), _row_spec(), _tok_spec(tm)],
                    out_specs=[_vec_spec(LANE), _tok_spec(tm), _tok_spec(tm), _row_spec(), _vec_spec()],
                    out_shape=[_sds((1, LANE), F32), _sds(x1.shape, F32), _sds(x1.shape, BF16), _sds((bsz, 1, D_MODEL), F32),
                               _sds((1, D_MODEL), F32)], pro=pro, epi=epi)


def _up_bwd_fused(dpre, w_up4, dy, x1, mix, pre_w, sc2, post_w, g1, comms=()):
    tm = 256
    bsz = x1.shape[0]

    def epi(dh2v, ex, outs):
        dy_ref, x1_ref, mix_ref, w2_ref, sc_ref, pw_ref, g1_ref = ex
        dx1_ref, dmix_ref, dsc_ref, dsh_ref, dg1_ref, dw2_ref, dpw_ref = outs
        b, i = pl.program_id(0), pl.program_id(1)
        first = jnp.logical_and(b == 0, i == 0)
        w2, pw = w2_ref[...], pw_ref[...]
        y2, xh2, rstd2 = _rms_fwd(x1_ref[...], w2)
        _acc_out(dsh_ref, i == 0, _colsum(dh2v))
        _acc_out(dsc_ref, i == 0, _colsum(dh2v * y2))
        dx1n, dw_rows = _rms_bwd(dh2v * (1.0 + sc_ref[...]), xh2, rstd2, w2)
        _acc_out(dw2_ref, first, _colsum(dw_rows))
        dx1 = dy_ref[...] + dx1n
        dx1_ref[...] = dx1
        n1, mh, rstd1 = _rms_fwd(mix_ref[...], pw)
        _acc_out(dg1_ref, i == 0, _colsum(dx1 * n1))
        dmix, dpw_rows = _rms_bwd(dx1 * g1_ref[...], mh, rstd1, pw)
        dmix_ref[...] = dmix.astype(BF16)
        _acc_out(dpw_ref, first, _colsum(dpw_rows))

    row_shape = _sds((bsz, 1, D_MODEL), F32)
    vec_shape = _sds((1, D_MODEL), F32)
    return _mm_rows(dpre, w_up4, name="up_bwd", tm=tm, extra=(dy, x1, mix, pre_w, sc2, post_w, g1),
                    extra_specs=[_tok_spec(tm), _tok_spec(tm), _tok_spec(tm), _vec_spec(), _row_spec(), _vec_spec(), _row_spec()],
                    out_specs=[_tok_spec(tm), _tok_spec(tm), _row_spec(), _row_spec(), _row_spec(), _vec_spec(), _vec_spec()],
                    out_shape=[_sds(x1.shape, F32), _sds(x1.shape, BF16), row_shape, row_shape, row_shape, vec_shape, vec_shape],
                    epi=epi, b_chunks=w_up4.shape[0], comms=comms)


def _in_bwd_fused(dproj, w_in_t, dx1, x, pre_w, sc1, comms=()):
    tm = 512
    bsz = x.shape[0]

    def epi(dh, ex, outs):
        dx1_ref, x_ref, w_ref, sc_ref = ex
        gx_ref, dsc_ref, dsh_ref, dw_ref = outs
        b, i = pl.program_id(0), pl.program_id(1)
        w = w_ref[...]
        y, xh, rstd = _rms_fwd(x_ref[...], w)
        _acc_out(dsh_ref, i == 0, _colsum(dh))
        _acc_out(dsc_ref, i == 0, _colsum(dh * y))
        dx, dw_rows = _rms_bwd(dh * (1.0 + sc_ref[...]), xh, rstd, w)
        _acc_out(dw_ref, jnp.logical_and(b == 0, i == 0), _colsum(dw_rows))
        gx_ref[...] = dx1_ref[...] + dx

    row_shape = _sds((bsz, 1, D_MODEL), F32)
    return _mm_rows(dproj, w_in_t, name="in_bwd", tm=tm, extra=(dx1, x, pre_w, sc1),
                    extra_specs=[_tok_spec(tm), _tok_spec(tm), _vec_spec(), _row_spec()],
                    out_specs=[_tok_spec(tm), _row_spec(), _row_spec(), _vec_spec()],
                    out_shape=[_sds(x.shape, F32), row_shape, row_shape, _sds((1, D_MODEL), F32)], epi=epi, comms=comms)


def _hgrn_bwd(dcat, proj, o_raw, states, lb_table, norm_w, comms=()):
    bsz, seq, _ = proj.shape
    nstep = seq // HG_TOK
    rec0 = ATT_WIDTH // LANE

    def body(tbl_ref, nw_ref, dr_ref, q_ref, f_ref, i_ref, g_ref, o_ref, st_ref,
             dq_ref, df_ref, di_ref, dg_ref, dlb_ref, dnw_ref, ds_scr):
        h, b, t = pl.program_id(0), pl.program_id(1), pl.program_id(2)

        @pl.when(t == 0)
        def _():
            ds_scr[...] = jnp.zeros_like(ds_scr)

        lower, upper = _block_masks()
        dlb_parts = []
        dnw_acc = jnp.zeros((1, LANE), F32)
        for hp in range(HG_HPS):
            ls = slice(hp * LANE, (hp + 1) * LANE)
            hq, v, hg = q_ref[:, ls], i_ref[:, ls], g_ref[:, ls]
            nw = nw_ref[...]
            c = _hgrn_common(tbl_ref[:, ls], f_ref[:, ls], hq)
            qd, kd, ku = c["qd"], c["kd"], c["ku"]
            y, on, rstd = _rms_fwd(o_ref[:, ls], nw)
            sg = _sigmoid(hg)
            dr = dr_ref[:, ls]
            dg_ref[:, ls] = (dr * y * (sg * (1.0 + hg * (1.0 - sg)))).astype(BF16)
            do, dnw_rows = _rms_bwd(dr * (hg * sg), on, rstd, nw)
            at = jnp.where(upper, _dot(kd, qd, NT_DIMS), 0.0)
            da = jnp.where(lower, _dot(do, v, NT_DIMS), 0.0)
            dat = jnp.where(upper, _dot(v, do, NT_DIMS), 0.0)
            dv = _dot(at, do)
            dqd = _dot(da, kd)
            dkd = _dot(dat, qd)
            do_c, qd_c, v_c, ku_c = [_chunk_slices(z.astype(BF16)) for z in (do, qd, v, ku)]
            outer = [_dot(do_c[j], qd_c[j], TN_DIMS) for j in range(HG_NCH)]
            ds = ds_scr[hp]
            ds_after = [None] * HG_NCH
            for j in reversed(range(HG_NCH)):
                ds_after[j] = ds
                ds = outer[j] + ds * c["e_last"][j]
            ds_scr[hp] = ds
            states = [st_ref[hp, j] for j in range(HG_NCH)]
            dv = dv + jnp.concatenate([_dot(ku_c[j], ds_after[j], NT_DIMS) for j in range(HG_NCH)], axis=0)
            dqd = dqd + jnp.concatenate([_dot(do_c[j], states[j]) for j in range(HG_NCH)], axis=0)
            dku = jnp.concatenate([_dot(v_c[j], ds_after[j]) for j in range(HG_NCH)], axis=0)
            dku_ku = dku * ku
            dbl = [_colsum(states[j] * ds_after[j]) * c["e_last"][j] + _colsum(dku_ku[j * HG_CHUNK:(j + 1) * HG_CHUNK])
                   for j in range(HG_NCH)]
            dk = dkd * c["e_nb"] + dku * c["e_rem"]
            db = dqd * qd - dkd * kd - dku_ku + jnp.where(_row_in_chunk() == HG_CHUNK - 1, _chunk_rows(dbl), 0.0)
            dfv = _chunk_cumsum(db, reverse=True) / c["f"] - dk
            sig, sq = c["sig"], c["sq"]
            df_ref[:, ls] = (dfv * (1.0 - c["lb"]) * sig * (1.0 - sig)).astype(BF16)
            dq_ref[:, ls] = (dqd * c["e_b"] * (sq * (1.0 + hq * (1.0 - sq)))).astype(BF16)
            di_ref[:, ls] = dv.astype(BF16)
            dlb_parts.append(_colsum(dfv * (1.0 - sig)))
            dnw_acc = dnw_acc + _colsum(dnw_rows)
        _acc_out(dlb_ref, jnp.logical_and(b == 0, t == 0), jnp.concatenate(dlb_parts, axis=1))
        _acc_out(dnw_ref, jnp.logical_and(h == 0, jnp.logical_and(b == 0, t == 0)), dnw_acc)

    rev = lambda t: nstep - 1 - t
    width = HG_HPS * LANE
    slab = lambda first: pl.BlockSpec((None, HG_TOK, width), lambda h, b, t: (b, rev(t), first // HG_HPS + h))
    head = pl.BlockSpec((None, HG_TOK, width), lambda h, b, t: (b, rev(t), h))
    grad_shape = _sds((bsz, seq, HG_WIDTH), BF16)
    return _call(
        body, (lb_table, norm_w, dcat, proj, proj, proj, proj, o_raw, states), name="hgrn_bwd",
        grid=(HG_HEADS // HG_HPS, bsz, nstep),
        in_specs=[pl.BlockSpec((2, width), lambda h, b, t: (0, h)), pl.BlockSpec((1, LANE), lambda h, b, t: (0, 0)),
                  slab(rec0), slab(HG_Q0), slab(HG_F0), slab(HG_I0), slab(HG_G0), head,
                  pl.BlockSpec((None, HG_HPS, HG_NCH, LANE, LANE), lambda h, b, t: (b, h, rev(t), 0, 0))],
        out_specs=[head, head, head, head, pl.BlockSpec((1, width), lambda h, b, t: (0, h)),
                   pl.BlockSpec((1, LANE), lambda h, b, t: (0, 0))],
        out_shape=[grad_shape, grad_shape, grad_shape, grad_shape, _sds((1, HG_WIDTH), F32), _sds((1, LANE), F32)],
        scratch_shapes=[pltpu.VMEM((HG_HPS, LANE, LANE), F32)],
        sem=("arbitrary", "arbitrary", "arbitrary"), comms=comms)


def _attn_bwd(dcat, raw, w_norm, qh, kh, vh, lse, sinks, comms=()):
    bsz, nblk = qh.shape[0], qh.shape[1]
    seq = nblk * WINDOW

    def body(sink_ref, da_ref, raw_ref, w_ref, q_ref, kc_ref, kp_ref, vc_ref, vp_ref, l_ref,
             dq_ref, dkd_ref, dkp_ref, dvd_ref, dvp_ref, dw_ref, dsink_ref):
        b, i = pl.program_id(0), pl.program_id(1)
        first = jnp.logical_and(b == 0, i == 0)
        mask = _band_mask(i)
        raw_v = raw_ref[...]
        w = w_ref[...]
        _, on, rstd = _rms_fwd(raw_v, w)
        do_all, dw_rows = _rms_bwd(da_ref[...], on, rstd, w)
        _acc_out(dw_ref, first, _colsum(dw_rows))
        lane8 = lax.broadcasted_iota(jnp.int32, (1, ATT_Q_HEADS), 1)
        dsink = jnp.zeros((1, ATT_Q_HEADS), F32)
        for g in range(ATT_KV_HEADS):
            gs = slice(g * ATT_HEAD_DIM, (g + 1) * ATT_HEAD_DIM)
            heads = [slice((g * ATT_GROUP + hh) * ATT_HEAD_DIM, (g * ATT_GROUP + hh + 1) * ATT_HEAD_DIM) for hh in range(ATT_GROUP)]
            q = q_ref[g]
            keys = jnp.concatenate([kp_ref[g], kc_ref[g]], axis=0)
            vals = jnp.concatenate([vp_ref[g], vc_ref[g]], axis=0)
            do_g = jnp.concatenate([do_all[:, hs] for hs in heads], axis=0)
            dsum = jnp.concatenate([jnp.sum(do_all[:, hs] * raw_v[:, hs], axis=-1, keepdims=True) for hs in heads], axis=0)
            lse_g = jnp.concatenate([l_ref[:, g * ATT_GROUP + hh:g * ATT_GROUP + hh + 1] for hh in range(ATT_GROUP)], axis=0)
            p = jnp.where(mask, jnp.exp(_dot(q, keys, NT_DIMS) * ATT_SCALE - lse_g), 0.0)
            sink_part = jnp.exp(_sink_column(sink_ref, g) - lse_g) * dsum
            for hh in range(ATT_GROUP):
                head_sum = jnp.sum(sink_part[hh * WINDOW:(hh + 1) * WINDOW], axis=0, keepdims=True)
                dsink = dsink - jnp.where(lane8 == g * ATT_GROUP + hh, head_sum, 0.0)
            ds = p * (_dot(do_g, vals, NT_DIMS) - dsum) * ATT_SCALE
            dq_g = _dot(ds, keys)
            for hh, hs in enumerate(heads):
                dq_ref[:, hs] = dq_g[hh * WINDOW:(hh + 1) * WINDOW]
            dk_g = _dot(ds, q, TN_DIMS)
            dv_g = _dot(p, do_g, TN_DIMS)
            dkp_ref[:, gs], dkd_ref[:, gs] = dk_g[:WINDOW], dk_g[WINDOW:]
            dvp_ref[:, gs], dvd_ref[:, gs] = dv_g[:WINDOW], dv_g[WINDOW:]
        _acc_out(dsink_ref, first, dsink)

    cur = lambda width: pl.BlockSpec((None, WINDOW, width), lambda b, i: (b, i, 0))
    q_spec, kv_cur, kv_prev = _attn_specs()
    kv_shape = _sds((bsz, seq, LANE), F32)
    return _call(
        body, (sinks, dcat, raw, w_norm, qh, kh, kh, vh, vh, lse), name="attn_bwd", grid=(bsz, nblk),
        in_specs=[pl.BlockSpec(memory_space=pltpu.SMEM), cur(ATT_WIDTH), cur(ATT_WIDTH), _vec_spec(ATT_WIDTH), q_spec,
                  kv_cur, kv_prev, kv_cur, kv_prev, cur(ATT_Q_HEADS)],
        out_specs=[cur(ATT_WIDTH), cur(LANE), cur(LANE), cur(LANE), cur(LANE), _vec_spec(ATT_WIDTH), _vec_spec(ATT_Q_HEADS)],
        out_shape=[_sds((bsz, seq, ATT_WIDTH), F32), kv_shape, kv_shape, kv_shape, kv_shape, _sds((1, ATT_WIDTH), F32),
                   _sds((1, ATT_Q_HEADS), F32)],
        sem=("arbitrary", "arbitrary"), comms=comms)


def _rope_bwd(dq, dkd, dkp, dvd, dvp, tables):
    bsz, seq, _ = dq.shape
    nblk = seq // WINDOW
    half = ROPE_DIM // 2

    def body(dq_ref, dkd_ref, dkp_ref, dvd_ref, dvp_ref, c_ref, u_ref, d_ref, o_ref):
        c, u, d = c_ref[...], u_ref[...], d_ref[...]
        has_next = pl.program_id(1) < nblk - 1

        def unrope(g):
            return g * c + pltpu.roll(g * u, LANE - half, 1) + pltpu.roll(g * d, half, 1)

        for s in range(ATT_WIDTH // LANE):
            o_ref[:, s * LANE:(s + 1) * LANE] = unrope(dq_ref[:, s * LANE:(s + 1) * LANE]).astype(BF16)
        dk = dkd_ref[...] + jnp.where(has_next, dkp_ref[...], 0.0)
        o_ref[:, ATT_WIDTH:ATT_WIDTH + LANE] = unrope(dk).astype(BF16)
        o_ref[:, ATT_WIDTH + LANE:ATT_COLS] = (dvd_ref[...] + jnp.where(has_next, dvp_ref[...], 0.0)).astype(BF16)

    cur = lambda width: pl.BlockSpec((None, WINDOW, width), lambda b, i: (b, i, 0))
    nxt = pl.BlockSpec((None, WINDOW, LANE), lambda b, i: (b, jnp.minimum(i + 1, nblk - 1), 0))
    tab = pl.BlockSpec((WINDOW, LANE), lambda b, i: (i, 0))
    return pl.pallas_call(
        body, name="rope_bwd", grid=(bsz, nblk),
        in_specs=[cur(ATT_WIDTH), cur(LANE), nxt, cur(LANE), nxt, tab, tab, tab],
        out_specs=cur(ATT_COLS), out_shape=_sds((bsz, seq, ATT_COLS), BF16),
        compiler_params=_params("parallel", "parallel"),
    )(dq, dkd, dkp, dvd, dvp, *tables)


def _other_chips(x, y):
    return [(1 - x, y), (x, 1 - y), (1 - x, 1 - y)]


def _sem_pair(n):
    return [pltpu.SemaphoreType.DMA((n,)), pltpu.SemaphoreType.DMA((n,))]


def _plan_chip_gather(blocks, bufs):
    n = len(blocks)

    def copies(ins, outs, sems):
        x, y, c = _mesh_pos()
        sends, lands = [], []
        for a in range(n):
            for j, chip in enumerate(_other_chips(x, y)):
                k = 3 * a + j
                sends.append(pltpu.make_async_remote_copy(
                    src_ref=ins[a], dst_ref=outs[a].at[4 * x + 2 * y + c], send_sem=sems[0].at[k], recv_sem=sems[1].at[k],
                    device_id=(*chip, c), device_id_type=MESH))
                slot = outs[a].at[4 * chip[0] + 2 * chip[1] + c]
                lands.append(pltpu.make_async_remote_copy(
                    src_ref=slot, dst_ref=slot, send_sem=sems[0].at[k], recv_sem=sems[1].at[k],
                    device_id=(*chip, c), device_id_type=MESH))
        return sends, lands

    def start(ins, outs, sems):
        for cp in copies(ins, outs, sems)[0]:
            cp.start()

    def finish(ins, outs, sems):
        sends, lands = copies(ins, outs, sems)
        for cp in lands:
            cp.wait_recv()
        for cp in sends:
            cp.wait_send()

    return _Comm(list(blocks) + list(bufs), [_sds(b.shape, b.dtype) for b in bufs], _sem_pair(3 * n), start, finish,
                 aliases=[(n + a, a) for a in range(n)])


def _plan_pair_forward(bufs):
    n = len(bufs)

    def copies(outs, sems):
        x, y, c = _mesh_pos()
        sends, lands = [], []
        for a in range(n):
            for j, chip in enumerate(_other_chips(x, y)):
                k = 3 * a + j
                slot = outs[a].at[4 * chip[0] + 2 * chip[1] + c]
                sends.append(pltpu.make_async_remote_copy(
                    src_ref=slot, dst_ref=slot, send_sem=sems[0].at[k], recv_sem=sems[1].at[k],
                    device_id=(x, y, 1 - c), device_id_type=MESH))
                theirs = outs[a].at[4 * chip[0] + 2 * chip[1] + 1 - c]
                lands.append(pltpu.make_async_remote_copy(
                    src_ref=theirs, dst_ref=theirs, send_sem=sems[0].at[k], recv_sem=sems[1].at[k],
                    device_id=(x, y, 1 - c), device_id_type=MESH))
        return sends, lands

    def start(ins, outs, sems):
        for cp in copies(outs, sems)[0]:
            cp.start()

    def finish(ins, outs, sems):
        sends, lands = copies(outs, sems)
        for cp in lands:
            cp.wait_recv()
        for cp in sends:
            cp.wait_send()

    return _Comm(list(bufs), [_sds(b.shape, b.dtype) for b in bufs], _sem_pair(3 * n), start, finish,
                 aliases=[(a, a) for a in range(n)])


def _plan_pair(arrays, other_half):
    n = len(arrays)
    per = N_CHIPS if other_half == "chip_major" else 1

    def copies(ins, outs, sems):
        x, y, c = _mesh_pos()
        out = []
        for a in range(n):
            for k in range(per):
                if other_half == "chip_major":
                    src, dst = ins[a].at[k, 1 - c], outs[a].at[k]
                else:
                    src, dst = (ins[a].at[1 - c] if other_half else ins[a]), outs[a]
                out.append(pltpu.make_async_remote_copy(
                    src_ref=src, dst_ref=dst, send_sem=sems[0].at[per * a + k], recv_sem=sems[1].at[per * a + k],
                    device_id=(x, y, 1 - c), device_id_type=MESH))
        return out

    def start(ins, outs, sems):
        for cp in copies(ins, outs, sems):
            cp.start()

    def finish(ins, outs, sems):
        for cp in copies(ins, outs, sems):
            cp.wait()

    if other_half == "chip_major":
        shapes = [_sds((a.shape[0],) + a.shape[2:], a.dtype) for a in arrays]
    else:
        shapes = [_sds(a.shape[1:] if other_half else a.shape, a.dtype) for a in arrays]
    return _Comm(list(arrays), shapes, _sem_pair(per * n), start, finish)


def _plan_chip_exchange(arrays):
    n = len(arrays)

    def copies(ins, outs, sems):
        x, y, c = _mesh_pos()
        sends, lands = [], []
        for a in range(n):
            for j, chip in enumerate(_other_chips(x, y)):
                k = 3 * a + j
                sends.append(pltpu.make_async_remote_copy(
                    src_ref=ins[a].at[2 * chip[0] + chip[1]], dst_ref=outs[a].at[2 * x + y], send_sem=sems[0].at[k],
                    recv_sem=sems[1].at[k], device_id=(*chip, c), device_id_type=MESH))
                slot = outs[a].at[2 * chip[0] + chip[1]]
                lands.append(pltpu.make_async_remote_copy(
                    src_ref=slot, dst_ref=slot, send_sem=sems[0].at[k], recv_sem=sems[1].at[k],
                    device_id=(*chip, c), device_id_type=MESH))
        return sends, lands

    def start(ins, outs, sems):
        for cp in copies(ins, outs, sems)[0]:
            cp.start()

    def finish(ins, outs, sems):
        sends, lands = copies(ins, outs, sems)
        for cp in lands:
            cp.wait_recv()
        for cp in sends:
            cp.wait_send()

    return _Comm(list(arrays), [_sds(a.shape, a.dtype) for a in arrays], _sem_pair(3 * n), start, finish)


def _comm_only(comms, name):
    return _call(lambda: None, (), name=name, grid=(), in_specs=[], out_specs=[], out_shape=[], sem=(), comms=comms)[1]


def _allgather8(arrays, name):
    return _comm_only([_plan_allgather8(arrays)], name)[0]


def _plan_allgather8(arrays):
    n = len(arrays)

    def parts(ins, outs, sems):
        send_sems, recv_sems, local_sems = sems
        x, y, c = _mesh_pos()
        me, sibling = (x, y, c), (x, y, 1 - c)
        chips = _other_chips(x, y)

        def copy(a, k, block, to, src=None):
            dst = outs[a].at[4 * block[0] + 2 * block[1] + block[2]]
            return pltpu.make_async_remote_copy(
                src_ref=dst if src is None else src, dst_ref=dst, send_sem=send_sems.at[7 * a + k],
                recv_sem=recv_sems.at[7 * a + k], device_id=to, device_id_type=MESH)

        mine = [pltpu.make_async_copy(ins[a], outs[a].at[4 * x + 2 * y + c], local_sems.at[a]) for a in range(n)]
        first = []
        for a in range(n):
            first.append(copy(a, 0, me, sibling, src=ins[a]))
            first += [copy(a, 1 + j, me, (*chip, c), src=ins[a]) for j, chip in enumerate(chips)]
        return copy, mine, first, me, sibling, chips, c

    def start(ins, outs, sems):
        _, mine, first, *_ = parts(ins, outs, sems)
        for cp in mine + first:
            cp.start()

    def finish(ins, outs, sems):
        copy, mine, first, me, sibling, chips, c = parts(ins, outs, sems)
        passed = []
        for j, chip in enumerate(chips):
            for a in range(n):
                copy(a, 1 + j, (*chip, c), me).wait_recv()
                fwd = copy(a, 4 + j, (*chip, c), sibling)
                fwd.start()
                passed.append(fwd)
        for a in range(n):
            copy(a, 0, sibling, me).wait_recv()
            for j, chip in enumerate(chips):
                copy(a, 4 + j, (*chip, 1 - c), me).wait_recv()
        for cp in first + passed:
            cp.wait_send()
        for cp in mine:
            cp.wait()

    sems = [pltpu.SemaphoreType.DMA((7 * n,)), pltpu.SemaphoreType.DMA((7 * n,)), pltpu.SemaphoreType.DMA((n,))]
    return _Comm(list(arrays), [_sds((N_DEV,) + a.shape, a.dtype) for a in arrays], sems, start, finish)


def _pair_sum(g, q, core, name, chip_major=False):
    rows, cols = g.shape[2:]
    tr = _row_tile(rows)

    def body(core_ref, g_ref, q_ref, o_ref):
        o_ref[...] = (g_ref[...] + q_ref[...]).astype(BF16)

    blk = pl.BlockSpec((None, tr, cols), lambda k, i, core_ref: (k, i, 0))
    if chip_major:
        own = pl.BlockSpec((None, None, tr, cols), lambda k, i, core_ref: (k, core_ref[0], i, 0))
    else:
        own = pl.BlockSpec((None, None, tr, cols), lambda k, i, core_ref: (core_ref[0], k, i, 0))
    return pl.pallas_call(
        body, name=name,
        grid_spec=pltpu.PrefetchScalarGridSpec(num_scalar_prefetch=1, grid=(N_CHIPS, rows // tr), in_specs=[own, blk], out_specs=blk),
        out_shape=_sds((N_CHIPS, rows, cols), BF16), compiler_params=_params("parallel", "parallel"),
    )(core, g, q)


def _sum_chips(own, landed, chip, name):
    _, rows, cols = own.shape
    tr = _row_tile(rows)

    def body(chip_ref, own_ref, a_ref, b_ref, c_ref, o_ref):
        acc = own_ref[...].astype(F32) + a_ref[...].astype(F32)
        o_ref[...] = (acc + b_ref[...].astype(F32)) + c_ref[...].astype(F32)

    blk = lambda flip: pl.BlockSpec((None, tr, cols), lambda i, chip_ref: (jnp.bitwise_xor(chip_ref[0], flip), i, 0))
    return pl.pallas_call(
        body, name=name,
        grid_spec=pltpu.PrefetchScalarGridSpec(num_scalar_prefetch=1, grid=(rows // tr,), in_specs=[blk(0), blk(1), blk(2), blk(3)],
                                               out_specs=pl.BlockSpec((tr, cols), lambda i, chip_ref: (i, 0))),
        out_shape=_sds((rows, cols), F32), compiler_params=_params("parallel"),
    )(chip, own, landed, landed, landed)


SMALL_ROWS = 120
PACK_ROWS = 168


def _rows(a, nrows):
    flat = a.reshape(-1)
    return jnp.pad(flat, (0, nrows * LANE - flat.shape[0])).reshape(nrows, LANE)


def _pack_small(b_ada, pre_w_mix, post_w_mix, pre_w_mlp, post_w_mlp, attn_out_w, hg_norm_w, attn_sinks, lb_table):
    return jnp.concatenate([
        _rows(b_ada, 48), _rows(pre_w_mix, 8), _rows(post_w_mix, 8), _rows(pre_w_mlp, 8), _rows(post_w_mlp, 8),
        _rows(attn_out_w, 8), _rows(hg_norm_w, 8), _rows(attn_sinks, 8), _rows(lb_table[0], 8), _rows(lb_table[1], 8)], axis=0)


def _unpack_small(p):
    vec = lambda lo, n: p[lo:lo + n // LANE].reshape(1, n)
    lb = jnp.stack([p[104:108].reshape(HG_WIDTH), p[112:116].reshape(HG_WIDTH)])
    return dict(b_ada=vec(0, N_MOD * D_MODEL), pre_w_mix=vec(48, D_MODEL), post_w_mix=vec(56, D_MODEL), pre_w_mlp=vec(64, D_MODEL),
                post_w_mlp=vec(72, D_MODEL), attn_out_w=vec(80, ATT_WIDTH), hg_norm_w=p[88:89], attn_sinks=p[96:97, :ATT_Q_HEADS],
                lb_table=lb)


def _small_update(packs, w, m, v):
    def body(p_ref, w_ref, m_ref, v_ref, g_ref, dl_ref, nm_ref, nv_ref, loss_ref):
        tot = p_ref[0]
        for d in range(1, N_DEV):
            tot = tot + p_ref[d]
        wv = w_ref[...]
        p1 = _sigmoid(wv[112:120] - wv[104:112])
        s = tot[152:160] * p1 * (1.0 - p1)
        g = jnp.concatenate([tot[0:48] + tot[48:96], tot[96:152], -s, s], axis=0)
        g_ref[...] = g
        dl_ref[...], nm_ref[...], nv_ref[...] = _adamw_math(g, wv, m_ref[...], v_ref[...])
        loss_ref[...] = tot[160:168]

    shp = _sds((SMALL_ROWS, LANE), F32)
    return pl.pallas_call(body, name="small_update", out_shape=[shp] * 4 + [_sds((8, LANE), F32)],
                          compiler_params=_params())(packs, w, m, v)


def kernel(x, c, w_ada, b_ada, pre_w_mix, w_in, attn_sinks, attn_out_w, lb_table, hg_norm_w, w_out, post_w_mix, pre_w_mlp, w_up, w_down, post_w_mlp, loss_target, m_w_ada, m_b_ada, m_pre_w_mix, m_w_in, m_attn_sinks, m_attn_out_w, m_lb_table, m_hg_norm_w, m_w_out, m_post_w_mix, m_pre_w_mlp, m_w_up, m_w_down, m_post_w_mlp, v_w_ada, v_b_ada, v_pre_w_mix, v_w_in, v_attn_sinks, v_attn_out_w, v_lb_table, v_hg_norm_w, v_w_out, v_post_w_mix, v_pre_w_mlp, v_w_up, v_w_down, v_post_w_mlp):
    xi, yi, ci = _mesh_pos()
    chip = 2 * xi + yi
    dev = 2 * chip + ci
    bsz, seq, _ = x.shape
    ntok = bsz * seq
    ada_cols = w_ada.shape[2]
    core = jnp.reshape(ci, (1,)).astype(jnp.int32)
    chip_idx = jnp.reshape(chip, (1,)).astype(jnp.int32)
    flat = lambda a: a.reshape(ntok, a.shape[-1])
    unflat = lambda a: a.reshape(bsz, seq, a.shape[-1])
    tables = _rope_tables(seq)

    def row_half(w):
        rows = w.shape[1] // 2
        return lax.dynamic_slice_in_dim(w[0], ci * rows, rows, axis=0).astype(BF16)

    def gather_buffer(w):
        rows, cols = w.shape[1] // 2, w.shape[2]
        own = w[0].astype(BF16).reshape(2, rows, cols)
        return lax.dynamic_update_slice(jnp.zeros((N_DEV, rows, cols), BF16), own, (2 * chip, 0, 0))

    w_in_t, m_in_t, v_in_t = [jnp.transpose(a[0])[None] for a in (w_in, m_w_in, v_w_in)]
    c_g, in_g = _allgather8([c, row_half(w_in_t)], "gather_first")
    c_all = c_g.reshape(N_DEV * bsz, D_MODEL)
    w_in_full = in_g.reshape(IN_COLS, D_MODEL)

    b_cols = lax.dynamic_slice_in_dim(b_ada, chip * ada_cols, ada_cols, axis=1)
    mod_part = _ada_fwd(c_all, w_ada[0], b_cols)
    half_rows = mod_part.shape[0] // 2
    (mod_g,) = _allgather8([lax.dynamic_slice_in_dim(mod_part, ci * half_rows, half_rows, axis=0)], "gather_mod")
    mod_all = mod_g.reshape(N_CHIPS, 2, half_rows, ada_cols).transpose(1, 2, 0, 3).reshape(N_DEV * bsz, N_MOD * D_MODEL)
    mod = lax.dynamic_slice_in_dim(mod_all, dev * bsz, bsz, axis=0)
    sh1, sc1, g1, sh2, sc2, g2 = [mod[:, i * D_MODEL:(i + 1) * D_MODEL].reshape(bsz, 1, D_MODEL) for i in range(N_MOD)]

    (h1, proj), ((out_g,),) = _in_proj_fused(x, pre_w_mix, sc1, sh1, w_in_full,
                                             comms=[_plan_chip_gather([row_half(w_out)], [gather_buffer(w_out)])])
    qh, kh, vh = _rope_fwd(proj, tables)
    (attn_raw, cat, lse), ((up_g,), (out_g,)) = _attn_fwd(
        qh, kh, vh, attn_sinks, attn_out_w,
        comms=[_plan_chip_gather([row_half(w_up)], [gather_buffer(w_up)]), _plan_pair_forward([out_g])])
    (o_raw, cat, states), ((down_g,), (up_g,)) = _hgrn_fwd(
        proj, lb_table, hg_norm_w, cat,
        comms=[_plan_chip_gather([row_half(w_down)], [gather_buffer(w_down)]), _plan_pair_forward([up_g])])
    w_out_full = out_g.reshape(D_MODEL, D_MODEL)
    w_up4 = up_g.reshape(N_CHIPS, D_MODEL, D_MODEL)
    mix, x1, h2 = _out_proj_fused(cat, w_out_full, x, post_w_mix, g1, pre_w_mlp, sc2, sh2)
    big_tm = min(ntok, 2048)
    up_spec = pl.BlockSpec((None, D_MODEL, D_MODEL), lambda i, j: (j, 0, 0))
    r, ((down_g,),) = _mm(flat(h2), w_up4, name="up_proj", out_dtype=BF16, tm=big_tm, tn=D_MODEL, n_out=D_FF, b_spec=up_spec,
                          epi=lambda acc: jnp.maximum(acc, 0.0), comms=[_plan_pair_forward([down_g])])
    w_down_full = down_g.reshape(D_FF, D_MODEL)
    square = lambda t: t * t
    loss_row, dy, dd, dg2, d_post_mlp = _down_proj_fused(unflat(r), w_down_full, x1, post_w_mlp, g2, loss_target)

    dpre = _mm(flat(dd), w_down_full, name="down_bwd", out_dtype=BF16, trans_b=True, tm=big_tm, tn=D_MODEL, extra=(r,),
               epi=lambda acc, rt: acc * (2.0 * rt.astype(F32)))
    half_rows = D_MODEL // 2
    g_down = _mm_tn(r, flat(dd), name="down_wgrad", tk=half_rows, tn=D_MODEL, a_fn=square,
                    out_shape=_sds((2, N_CHIPS, half_rows, D_MODEL), F32),
                    out_spec=pl.BlockSpec((None, None, half_rows, D_MODEL), lambda i, j: (i % 2, i // 2, 0, 0)))
    (dx1, dmix, dsc2, dsh2, dg1, d_pre_mlp, d_post_mix), ((q_down,),) = _up_bwd_fused(
        unflat(dpre), w_up4, dy, x1, mix, pre_w_mlp, sc2, post_w_mix, g1, comms=[_plan_pair([g_down], True)])
    g_up = _mm_tn(flat(h2), dpre, name="up_wgrad", tk=D_MODEL, tn=half_rows,
                  out_shape=_sds((2, N_CHIPS, half_rows, D_MODEL), F32),
                  out_spec=pl.BlockSpec((2, None, half_rows, half_rows), lambda i, j: (0, j // 2, 0, j % 2)))
    s_down = _pair_sum(g_down, q_down, core, "pair_sum_down")

    dcat, ((q_up,),) = _mm(flat(dmix), w_out_full, name="out_bwd", out_dtype=F32, trans_b=True, comms=[_plan_pair([g_up], True)])
    dcat = unflat(dcat)
    s_up = _pair_sum(g_up, q_up, core, "pair_sum_up")
    out_rows = D_MODEL // N_CHIPS
    g_out = _mm_tn(flat(cat), flat(dmix), name="out_wgrad", tk=out_rows, tn=half_rows,
                   out_shape=_sds((2, N_CHIPS, out_rows, half_rows), F32),
                   out_spec=pl.BlockSpec((None, None, out_rows, half_rows), lambda i, j: (j, i, 0, 0)))
    (dhq, dhf, dhi, dhg, d_lb, d_hg_norm), ((x_down, x_up), (q_out,)) = _hgrn_bwd(
        dcat, proj, o_raw, states, lb_table, hg_norm_w, comms=[_plan_chip_exchange([s_down, s_up]), _plan_pair([g_out], True)])
    half_down = _sum_chips(s_down, x_down, chip_idx, "sum_chips_down")
    half_up = _sum_chips(s_up, x_up, chip_idx, "sum_chips_up")
    s_out = _pair_sum(g_out, q_out, core, "pair_sum_out")
    (dq, dkd, dkp, dvd, dvp, d_attn_out, d_sinks), ((their_down, their_up), (x_out,)) = _attn_bwd(
        dcat, attn_raw, attn_out_w, qh, kh, vh, lse, attn_sinks,
        comms=[_plan_pair([half_down, half_up], False), _plan_chip_exchange([s_out])])
    half_out = _sum_chips(s_out, x_out, chip_idx, "sum_chips_out")
    dproj_a = _rope_bwd(dq, dkd, dkp, dvd, dvp, tables)
    dproj = jnp.concatenate([dproj_a, dhq, dhf, dhi, dhg], axis=-1)
    in_rows = IN_COLS // N_CHIPS // 2
    g_in = _mm_tn(flat(dproj), flat(h1), name="in_wgrad", tk=2 * LANE, tn=D_MODEL).reshape(N_CHIPS, 2, in_rows, D_MODEL)
    (grad_x, dsc1, dsh1, d_pre_mix), ((q_in,), (their_out,)) = _in_bwd_fused(
        dproj, w_in_full, dx1, x, pre_w_mix, sc1, comms=[_plan_pair([g_in], "chip_major"), _plan_pair([half_out], False)])
    s_in = _pair_sum(g_in, q_in, core, "pair_sum_in", chip_major=True)

    dmod = jnp.concatenate([dsh1, dsc1, dg1, dsh2, dsc2, dg2], axis=-1).reshape(bsz, N_MOD * D_MODEL)
    pack = jnp.concatenate([
        _rows(dmod, 96), _rows(d_pre_mix, 8), _rows(d_post_mix, 8), _rows(d_pre_mlp, 8), _rows(d_post_mlp, 8),
        _rows(d_attn_out, 8), _rows(d_hg_norm, 8), _rows(d_sinks, 8), _rows(d_lb, 8), _rows(loss_row, 8)], axis=0)
    (packs,), (x_in,) = _comm_only([_plan_allgather8([pack]), _plan_chip_exchange([s_in])], "gather_small")
    half_in = _sum_chips(s_in, x_in, chip_idx, "sum_chips_in")
    ((their_in,),) = _comm_only([_plan_pair([half_in], False)], "pair_swap_in")
    small_args = lambda pre: (pre["b_ada"], pre["pre_w_mix"], pre["post_w_mix"], pre["pre_w_mlp"], pre["post_w_mlp"],
                              pre["attn_out_w"], pre["hg_norm_w"], pre["attn_sinks"], pre["lb_table"])
    w_small = dict(b_ada=b_ada, pre_w_mix=pre_w_mix, post_w_mix=post_w_mix, pre_w_mlp=pre_w_mlp, post_w_mlp=post_w_mlp,
                   attn_out_w=attn_out_w, hg_norm_w=hg_norm_w, attn_sinks=attn_sinks, lb_table=lb_table)
    m_small = dict(b_ada=m_b_ada, pre_w_mix=m_pre_w_mix, post_w_mix=m_post_w_mix, pre_w_mlp=m_pre_w_mlp, post_w_mlp=m_post_w_mlp,
                   attn_out_w=m_attn_out_w, hg_norm_w=m_hg_norm_w, attn_sinks=m_attn_sinks, lb_table=m_lb_table)
    v_small = dict(b_ada=v_b_ada, pre_w_mix=v_pre_w_mix, post_w_mix=v_post_w_mix, pre_w_mlp=v_pre_w_mlp, post_w_mlp=v_post_w_mlp,
                   attn_out_w=v_attn_out_w, hg_norm_w=v_hg_norm_w, attn_sinks=v_attn_sinks, lb_table=v_lb_table)
    *small_packed, loss_rows = _small_update(packs, _pack_small(*small_args(w_small)), _pack_small(*small_args(m_small)),
                                             _pack_small(*small_args(v_small)))
    small_out = [_unpack_small(p) for p in small_packed]
    loss = loss_rows[0, 0]

    dmod_all = packs[:, :96, :].reshape(N_DEV * bsz, N_MOD * D_MODEL)
    dmod_cols = lax.dynamic_slice_in_dim(dmod_all, chip * ada_cols, ada_cols, axis=1)
    ada_out = _ada_bwd_adamw(c_all, dmod_cols, w_ada[0], m_w_ada[0], v_w_ada[0])

    big = dict(
        w_in=tuple(jnp.transpose(a) for a in _adamw_halves(half_in, their_in, core, w_in_t[0], m_in_t[0], v_in_t[0], axis=0,
                                                           name="adamw_in")),
        w_up=tuple(_adamw_halves(half_up, their_up, core, w_up[0], m_w_up[0], v_w_up[0], axis=0, name="adamw_up")),
        w_out=tuple(_adamw_halves(half_out, their_out, core, w_out[0], m_w_out[0], v_w_out[0], axis=1, name="adamw_out")),
        w_down=tuple(_adamw_halves(half_down, their_down, core, w_down[0], m_w_down[0], v_w_down[0], axis=0, name="adamw_down")),
        w_ada=tuple(ada_out),
    )
    order = ("w_ada", "b_ada", "pre_w_mix", "w_in", "attn_sinks", "attn_out_w", "lb_table", "hg_norm_w", "w_out", "post_w_mix",
             "pre_w_mlp", "w_up", "w_down", "post_w_mlp")
    outs = [loss, grad_x]
    for kind in range(4):
        for nm in order:
            outs.append(big[nm][kind][None] if nm in big else small_out[kind][nm])
    return tuple(outs)
```

```python
import functools

import jax
import jax.numpy as jnp
from jax import lax
from jax.experimental import pallas as pl
from jax.experimental.pallas import tpu as pltpu

F32 = jnp.float32
BF16 = jnp.bfloat16

D_MODEL = 1024
ATT_WIDTH = 512
ATT_HEAD_DIM = 64
ATT_Q_HEADS = 8
ATT_KV_HEADS = 2
ATT_GROUP = ATT_Q_HEADS // ATT_KV_HEADS
ATT_KV_COLS = ATT_KV_HEADS * ATT_HEAD_DIM
WINDOW = 128
ROPE_DIM = 16
ROPE_THETA = 500000.0
HG_WIDTH = 512
MIX_WIDTH = ATT_WIDTH + HG_WIDTH
HG_HEAD_DIM = 128
HG_HEADS = 4
HG_CHUNK = 32
IN_COLS = ATT_WIDTH + 2 * ATT_KV_COLS + 4 * HG_WIDTH
ATT_COLS = ATT_WIDTH + 2 * ATT_KV_COLS
D_FF = 4 * D_MODEL
N_MOD = 6
EPS = 1e-6
ATT_SCALE = ATT_HEAD_DIM ** -0.5

ADAM_LR = 0.001
ADAM_B1 = 0.9
ADAM_B2 = 0.999
ADAM_EPS = 1e-08
ADAM_WD = 0.01
ADAM_STEP = 10

N_CHIPS = 4
N_DEV = 8
LANE = 128
VMEM_LIMIT = 48 * 1024 * 1024
MESH = pl.DeviceIdType.MESH

NT_DIMS = (((1,), (1,)), ((), ()))
TN_DIMS = (((0,), (0,)), ((), ()))


def _sds(shape, dtype):
    return jax.ShapeDtypeStruct(tuple(shape), dtype)


def _params(*sem):
    return pltpu.CompilerParams(dimension_semantics=sem, vmem_limit_bytes=VMEM_LIMIT)


def _sigmoid(x):
    return 1.0 / (1.0 + jnp.exp(-x))


def _dot(a, b, dims=None):
    a, b = a.astype(BF16), b.astype(BF16)
    if dims is None:
        return jnp.dot(a, b, preferred_element_type=F32)
    return lax.dot_general(a, b, dims, preferred_element_type=F32)


def _rms_fwd(x, w):
    rstd = lax.rsqrt(jnp.mean(x * x, axis=-1, keepdims=True) + EPS)
    xh = x * rstd
    return xh * w, xh, rstd


def _rms_bwd(dy, xh, rstd, w):
    dxh = dy * w
    dx = rstd * (dxh - xh * jnp.mean(dxh * xh, axis=-1, keepdims=True))
    return dx, dy * xh


def _colsum(x):
    return jnp.sum(x, axis=0, keepdims=True)


def _row_tile(rows, cap=256):
    return max(t for t in range(16, cap + 1, 16) if rows % t == 0)


HBM_SPEC = pl.BlockSpec(memory_space=pltpu.HBM)


def _mesh_pos():
    return lax.axis_index("x"), lax.axis_index("y"), lax.axis_index("c")


class _Comm:
    def __init__(self, ins, outs, sems, start, finish, aliases=()):
        self.ins, self.outs, self.sems = list(ins), list(outs), list(sems)
        self.start, self.finish, self.aliases = start, finish, tuple(aliases)


def _call(body, args, *, name, grid, in_specs, out_specs, out_shape, sem, scratch_shapes=(), comms=(), aliases=None):
    scratch_shapes = list(scratch_shapes)
    if not comms:
        return pl.pallas_call(body, name=name, grid=grid, in_specs=in_specs, out_specs=out_specs, out_shape=out_shape,
                              input_output_aliases=dict(aliases or {}), scratch_shapes=scratch_shapes,
                              compiler_params=_params(*sem))(*args)
    single = not isinstance(out_shape, (list, tuple))
    out_specs_l = [out_specs] if single else list(out_specs)
    out_shape_l = [out_shape] if single else list(out_shape)
    n_in, n_out, n_scr = len(in_specs), len(out_shape_l), len(scratch_shapes)
    n_ci = [len(cm.ins) for cm in comms]
    n_co = [len(cm.outs) for cm in comms]
    n_cs = [len(cm.sems) for cm in comms]
    aliases = dict(aliases or {})
    for k, cm in enumerate(comms):
        for i, o in cm.aliases:
            aliases[n_in + sum(n_ci[:k]) + i] = n_out + sum(n_co[:k]) + o

    def fused(*refs):
        pos = [0]

        def take(n):
            part = refs[pos[0]:pos[0] + n]
            pos[0] += n
            return part

        ins = take(n_in)
        c_ins = [take(n) for n in n_ci]
        outs = take(n_out)
        c_outs = [take(n) for n in n_co]
        scr = take(n_scr)
        c_sems = [take(n) for n in n_cs]
        first, last = True, True
        for d, size in enumerate(grid):
            first = jnp.logical_and(first, pl.program_id(d) == 0)
            last = jnp.logical_and(last, pl.program_id(d) == size - 1)

        def run(which):
            for cm, ci, co, cs in zip(comms, c_ins, c_outs, c_sems):
                getattr(cm, which)(ci, co, cs)

        if grid:
            pl.when(first)(lambda: run("start"))
        else:
            run("start")
        body(*ins, *outs, *scr)
        if grid:
            pl.when(last)(lambda: run("finish"))
        else:
            run("finish")

    res = pl.pallas_call(
        fused, name=name, grid=grid, in_specs=list(in_specs) + [HBM_SPEC] * sum(n_ci),
        out_specs=out_specs_l + [HBM_SPEC] * sum(n_co), out_shape=out_shape_l + [s for cm in comms for s in cm.outs],
        input_output_aliases=aliases, scratch_shapes=scratch_shapes + [s for cm in comms for s in cm.sems],
        compiler_params=_params(*["arbitrary"] * len(grid)),
    )(*args, *[a for cm in comms for a in cm.ins])
    main = res[:n_out]
    extra, at = [], n_out
    for n in n_co:
        extra.append(list(res[at:at + n]))
        at += n
    return (main[0] if single else list(main)), extra


def _mm(a, b, *, name, out_dtype, trans_b=False, tm=512, tn=None, a_fn=None, extra=(), epi=None,
        b_spec=None, n_out=None, b_chunks=1, comms=()):
    m_total, k_total = a.shape
    if n_out is None:
        n_out = b.shape[0] if trans_b else b.shape[1]
    tn = n_out if tn is None else tn
    grid = (m_total // tm, n_out // tn)
    dims = NT_DIMS if trans_b else None
    kc = k_total // b_chunks

    def body(*refs):
        a_ref, b_ref = refs[0], refs[1]
        extra_refs = refs[2:2 + len(extra)]
        o_ref = refs[2 + len(extra)]
        if b_chunks == 1:
            av = a_ref[...]
            acc = _dot(av if a_fn is None else a_fn(av), b_ref[...], dims)
        else:
            acc = _dot(a_ref[:, 0:kc], b_ref[0], NT_DIMS)
            for k in range(1, b_chunks):
                acc = acc + _dot(a_ref[:, k * kc:(k + 1) * kc], b_ref[k], NT_DIMS)
        if epi is not None:
            acc = epi(acc, *[r[...] for r in extra_refs])
        o_ref[...] = acc.astype(out_dtype)

    if b_spec is None:
        if b_chunks > 1:
            b_spec = pl.BlockSpec((b_chunks, tn, kc), lambda i, j: (0, j, 0))
        elif trans_b:
            b_spec = pl.BlockSpec((tn, k_total), lambda i, j: (j, 0))
        else:
            b_spec = pl.BlockSpec((k_total, tn), lambda i, j: (0, j))
    in_specs = [pl.BlockSpec((tm, k_total), lambda i, j: (i, 0)), b_spec]
    in_specs += [pl.BlockSpec((tm, tn), lambda i, j: (i, j)) for _ in extra]
    return _call(
        body, (a, b, *extra), name=name, grid=grid, in_specs=in_specs,
        out_specs=pl.BlockSpec((tm, tn), lambda i, j: (i, j)),
        out_shape=_sds((m_total, n_out), out_dtype),
        sem=("parallel", "parallel"), comms=comms)


def _mm_tn(a, b, *, name, tk, tn, a_fn=None, out_shape=None, out_spec=None):
    m_total, k_total = a.shape
    n_total = b.shape[1]
    grid = (k_total // tk, n_total // tn)

    def body(a_ref, b_ref, o_ref):
        av = a_ref[...]
        part = _dot(av if a_fn is None else a_fn(av), b_ref[...], TN_DIMS)
        o_ref[...] = part.reshape(o_ref.shape)

    if out_shape is None:
        out_shape = _sds((k_total, n_total), F32)
        out_spec = pl.BlockSpec((tk, tn), lambda i, j: (i, j))
    return pl.pallas_call(
        body, name=name, grid=grid,
        in_specs=[pl.BlockSpec((m_total, tk), lambda i, j: (0, i)), pl.BlockSpec((m_total, tn), lambda i, j: (0, j))],
        out_specs=out_spec, out_shape=out_shape,
        compiler_params=_params("parallel", "parallel"),
    )(a, b)


def _ada_fwd(c_all, w_shard, b_shard):
    nb, ncol = c_all.shape[0], w_shard.shape[1]
    tn = 512

    def body(c_ref, w_ref, b_ref, o_ref):
        c = c_ref[...]
        o_ref[...] = _dot(c * _sigmoid(c), w_ref[...]) + b_ref[...]

    return pl.pallas_call(
        body, name="ada_fwd", grid=(ncol // tn,),
        in_specs=[pl.BlockSpec((nb, D_MODEL), lambda j: (0, 0)), pl.BlockSpec((D_MODEL, tn), lambda j: (0, j)),
                  pl.BlockSpec((1, tn), lambda j: (0, j))],
        out_specs=pl.BlockSpec((nb, tn), lambda j: (0, j)), out_shape=_sds((nb, ncol), F32),
        compiler_params=_params("parallel"),
    )(c_all, w_shard, b_shard)


def _adamw_math(g, w, m, v):
    m = ADAM_B1 * m + (1.0 - ADAM_B1) * g
    v = ADAM_B2 * v + (1.0 - ADAM_B2) * (g * g)
    m_hat = m / (1.0 - ADAM_B1 ** ADAM_STEP)
    v_hat = v / (1.0 - ADAM_B2 ** ADAM_STEP)
    delta = -ADAM_LR * (m_hat / (jnp.sqrt(v_hat) + ADAM_EPS) + ADAM_WD * w)
    return delta, m, v


def _ada_bwd_adamw(c_all, dmod_cols, w, m, v):
    nb, ncol = dmod_cols.shape
    tn = 256

    def body(c_ref, d_ref, w_ref, m_ref, v_ref, g_ref, dl_ref, nm_ref, nv_ref):
        c = c_ref[...]
        g = _dot(c * _sigmoid(c), d_ref[...], TN_DIMS)
        g_ref[...] = g
        dl_ref[...], nm_ref[...], nv_ref[...] = _adamw_math(g, w_ref[...], m_ref[...], v_ref[...])

    col = pl.BlockSpec((D_MODEL, tn), lambda j: (0, j))
    shp = _sds((D_MODEL, ncol), F32)
    return pl.pallas_call(
        body, name="ada_bwd_adamw", grid=(ncol // tn,),
        in_specs=[pl.BlockSpec((nb, D_MODEL), lambda j: (0, 0)), pl.BlockSpec((nb, tn), lambda j: (0, j)), col, col, col],
        out_specs=[col, col, col, col], out_shape=[shp, shp, shp, shp],
        compiler_params=_params("parallel"),
    )(c_all, dmod_cols, w, m, v)


def _adamw_halves(own, theirs, core, w, m, v, *, axis, name):
    r2, c2 = own.shape
    tr = _row_tile(r2)
    nt = r2 // tr

    def body(core_ref, own_ref, their_ref, w_ref, m_ref, v_ref, g_ref, dl_ref, nm_ref, nv_ref):
        g = jnp.where(pl.program_id(0) == core_ref[0], own_ref[...], their_ref[...])
        g_ref[...] = g
        dl_ref[...], nm_ref[...], nv_ref[...] = _adamw_math(g, w_ref[...], m_ref[...], v_ref[...])

    if axis == 0:
        full = pl.BlockSpec((tr, c2), lambda h, i, core_ref: (h * nt + i, 0))
    else:
        full = pl.BlockSpec((tr, c2), lambda h, i, core_ref: (i, h))
    half = pl.BlockSpec((tr, c2), lambda h, i, core_ref: (i, 0))
    shp = _sds(w.shape, F32)
    return pl.pallas_call(
        body, name=name,
        grid_spec=pltpu.PrefetchScalarGridSpec(num_scalar_prefetch=1, grid=(2, nt), in_specs=[half, half, full, full, full],
                                               out_specs=[full] * 4),
        out_shape=[shp] * 4, compiler_params=_params("parallel", "parallel"),
    )(core, own, theirs, w, m, v)


def _tok_spec(tm, width=D_MODEL):
    return pl.BlockSpec((None, tm, width), lambda b, i: (b, i, 0))


def _row_spec(width=D_MODEL):
    return pl.BlockSpec((None, 1, width), lambda b, i: (b, 0, 0))


def _vec_spec(width=D_MODEL):
    return pl.BlockSpec((1, width), lambda b, i: (0, 0))


def _mm_rows(a, b, *, name, tm, extra, extra_specs, out_specs, out_shape, epi, pro=None, trans_b=False, b_chunks=1, comms=()):
    bsz, seq, k_total = a.shape
    kc = k_total // b_chunks
    dims = NT_DIMS if trans_b else None

    def body(*refs):
        a_ref, b_ref = refs[0], refs[1]
        ex, outs = refs[2:2 + len(extra)], refs[2 + len(extra):]
        if b_chunks == 1:
            acc = _dot(a_ref[...] if pro is None else pro(a_ref, ex, outs), b_ref[...], dims)
        else:
            acc = _dot(a_ref[:, 0:kc], b_ref[0], NT_DIMS)
            for k in range(1, b_chunks):
                acc = acc + _dot(a_ref[:, k * kc:(k + 1) * kc], b_ref[k], NT_DIMS)
        epi(acc, ex, outs)

    b_spec = pl.BlockSpec(b.shape, lambda bb, i: (0,) * b.ndim)
    return _call(
        body, (a, b, *extra), name=name, grid=(bsz, seq // tm), in_specs=[_tok_spec(tm, k_total), b_spec, *extra_specs],
        out_specs=out_specs, out_shape=out_shape, sem=("arbitrary", "arbitrary"), comms=comms)


def _in_proj_fused(x, w, sc, sh, w_in_t, comms=()):
    tm = 512

    def pro(x_ref, ex, outs):
        y, _, _ = _rms_fwd(x_ref[...], ex[0][...])
        h = (y * (1.0 + ex[1][...]) + ex[2][...]).astype(BF16)
        outs[0][...] = h
        return h

    def epi(acc, ex, outs):
        outs[1][...] = acc

    cols = w_in_t.shape[0]
    return _mm_rows(x, w_in_t, name="in_proj", tm=tm, extra=(w, sc, sh), extra_specs=[_vec_spec(), _row_spec(), _row_spec()],
                    out_specs=[_tok_spec(tm), _tok_spec(tm, cols)],
                    out_shape=[_sds(x.shape, BF16), _sds(x.shape[:2] + (cols,), F32)], pro=pro, epi=epi, trans_b=True, comms=comms)


def _rope_tables(seq):
    half = ROPE_DIM // 2
    inv_freq = ROPE_THETA ** (-jnp.arange(0, ROPE_DIM, 2, dtype=F32) / ROPE_DIM)
    ang = jnp.arange(seq, dtype=F32)[:, None] * inv_freq[None, :]
    cos, sin = jnp.cos(ang), jnp.sin(ang)
    rest = ATT_HEAD_DIM - ROPE_DIM
    ones, zeros, zh = jnp.ones((seq, rest), F32), jnp.zeros((seq, rest), F32), jnp.zeros((seq, half), F32)
    reps = LANE // ATT_HEAD_DIM
    t_cos = jnp.tile(jnp.concatenate([cos, cos, ones], axis=1), (1, reps))
    t_up = jnp.tile(jnp.concatenate([zh, sin, zeros], axis=1), (1, reps))
    t_dn = jnp.tile(jnp.concatenate([-sin, zh, zeros], axis=1), (1, reps))
    return t_cos, t_up, t_dn


GROUP_ROWS = ATT_GROUP * WINDOW


def _rope_fwd(proj, tables):
    bsz, seq, _ = proj.shape
    nblk = seq // WINDOW
    half = ROPE_DIM // 2
    tm = 2 * WINDOW

    def body(p_ref, c_ref, u_ref, d_ref, q_ref, k_ref, v_ref):
        c, u, d = c_ref[...], u_ref[...], d_ref[...]

        def rope(x):
            return (x * c + pltpu.roll(x, half, 1) * u + pltpu.roll(x, LANE - half, 1) * d).astype(BF16)

        heads_per_slab = LANE // ATT_HEAD_DIM
        for s in range(ATT_WIDTH // LANE):
            slab = rope(p_ref[:, s * LANE:(s + 1) * LANE])
            for part in range(heads_per_slab):
                g, hh = divmod(s * heads_per_slab + part, ATT_GROUP)
                piece = slab[:, part * ATT_HEAD_DIM:(part + 1) * ATT_HEAD_DIM]
                for blk in range(tm // WINDOW):
                    q_ref[blk, g, hh * WINDOW:(hh + 1) * WINDOW, :] = piece[blk * WINDOW:(blk + 1) * WINDOW]
        rk = rope(p_ref[:, ATT_WIDTH:ATT_WIDTH + LANE])
        vv = p_ref[:, ATT_WIDTH + LANE:ATT_COLS].astype(BF16)
        for g in range(ATT_KV_HEADS):
            k_ref[g] = rk[:, g * ATT_HEAD_DIM:(g + 1) * ATT_HEAD_DIM]
            v_ref[g] = vv[:, g * ATT_HEAD_DIM:(g + 1) * ATT_HEAD_DIM]

    tab = pl.BlockSpec((tm, LANE), lambda b, i: (i, 0))
    kv_spec = pl.BlockSpec((None, ATT_KV_HEADS, tm, ATT_HEAD_DIM), lambda b, i: (b, 0, i, 0))
    kv_shape = _sds((bsz, ATT_KV_HEADS, seq, ATT_HEAD_DIM), BF16)
    return pl.pallas_call(
        body, name="rope_fwd", grid=(bsz, seq // tm),
        in_specs=[_tok_spec(tm, ATT_COLS), tab, tab, tab],
        out_specs=[pl.BlockSpec((None, tm // WINDOW, ATT_KV_HEADS, GROUP_ROWS, ATT_HEAD_DIM), lambda b, i: (b, i, 0, 0, 0)),
                   kv_spec, kv_spec],
        out_shape=[_sds((bsz, nblk, ATT_KV_HEADS, GROUP_ROWS, ATT_HEAD_DIM), BF16), kv_shape, kv_shape],
        compiler_params=_params("parallel", "parallel"),
    )(proj, *tables)


ATT_BPS = 2


def _band_mask(has_prev):
    row = lax.broadcasted_iota(jnp.int32, (GROUP_ROWS, 2 * WINDOW), 0) % WINDOW
    col = lax.broadcasted_iota(jnp.int32, (GROUP_ROWS, 2 * WINDOW), 1)
    prev = jnp.logical_and(jnp.logical_and(col < WINDOW, col > row), has_prev)
    return jnp.logical_or(prev, jnp.logical_and(col >= WINDOW, col - WINDOW <= row))


def _sink_column(sink_ref, g):
    head = lax.broadcasted_iota(jnp.int32, (GROUP_ROWS, 1), 0) // WINDOW
    col = jnp.full((GROUP_ROWS, 1), sink_ref[0, g * ATT_GROUP], F32)
    for hh in range(1, ATT_GROUP):
        col = jnp.where(head == hh, sink_ref[0, g * ATT_GROUP + hh], col)
    return col


def _attn_specs():
    q_spec = pl.BlockSpec((None, ATT_BPS, ATT_KV_HEADS, GROUP_ROWS, ATT_HEAD_DIM), lambda b, i: (b, i, 0, 0, 0))
    kv_cur = pl.BlockSpec((None, ATT_KV_HEADS, ATT_BPS * WINDOW, ATT_HEAD_DIM), lambda b, i: (b, 0, i, 0))
    kv_prev = pl.BlockSpec((None, ATT_KV_HEADS, WINDOW, ATT_HEAD_DIM), lambda b, i: (b, 0, jnp.maximum(ATT_BPS * i - 1, 0), 0))
    return q_spec, kv_cur, kv_prev


def _band(prev_ref, cur_ref, g, blk):
    own = cur_ref[g, blk * WINDOW:(blk + 1) * WINDOW]
    before = prev_ref[g] if blk == 0 else cur_ref[g, (blk - 1) * WINDOW:blk * WINDOW]
    return jnp.concatenate([before, own], axis=0)


def _attn_fwd(qh, kh, vh, sinks, w_norm, comms=()):
    bsz, nblk = qh.shape[0], qh.shape[1]
    seq = nblk * WINDOW
    rows = ATT_BPS * WINDOW
    neg = float(jnp.finfo(jnp.float32).min)

    def body(sink_ref, q_ref, kc_ref, kp_ref, vc_ref, vp_ref, w_ref, raw_ref, an_ref, l_ref):
        for blk in range(ATT_BPS):
            mask = _band_mask(True if blk else pl.program_id(1) > 0)
            for g in range(ATT_KV_HEADS):
                keys, vals = _band(kp_ref, kc_ref, g, blk), _band(vp_ref, vc_ref, g, blk)
                sink = _sink_column(sink_ref, g)
                s = jnp.where(mask, _dot(q_ref[blk, g], keys, NT_DIMS) * ATT_SCALE, neg)
                m = jnp.maximum(jnp.max(s, axis=-1, keepdims=True), sink)
                p = jnp.where(mask, jnp.exp(s - m), 0.0)
                den = jnp.sum(p, axis=-1, keepdims=True) + jnp.exp(sink - m)
                o = _dot(p / den, vals)
                lse = m + jnp.log(den)
                tok = slice(blk * WINDOW, (blk + 1) * WINDOW)
                for hh in range(ATT_GROUP):
                    h = g * ATT_GROUP + hh
                    raw_ref[tok, h * ATT_HEAD_DIM:(h + 1) * ATT_HEAD_DIM] = o[hh * WINDOW:(hh + 1) * WINDOW]
                    l_ref[tok, h:h + 1] = lse[hh * WINDOW:(hh + 1) * WINDOW]
        y, _, _ = _rms_fwd(raw_ref[...], w_ref[...])
        an_ref[...] = y.astype(BF16)

    cur = lambda width: pl.BlockSpec((None, rows, width), lambda b, i: (b, i, 0))
    q_spec, kv_cur, kv_prev = _attn_specs()
    return _call(
        body, (sinks, qh, kh, kh, vh, vh, w_norm), name="attn_fwd", grid=(bsz, nblk // ATT_BPS),
        in_specs=[pl.BlockSpec(memory_space=pltpu.SMEM), q_spec, kv_cur, kv_prev, kv_cur, kv_prev, _vec_spec(ATT_WIDTH)],
        out_specs=[cur(ATT_WIDTH), cur(ATT_WIDTH), cur(ATT_Q_HEADS)],
        out_shape=[_sds((bsz, seq, ATT_WIDTH), F32), _sds((bsz, seq, MIX_WIDTH), BF16), _sds((bsz, seq, ATT_Q_HEADS), F32)],
        sem=("parallel", "parallel"), comms=comms)


HG_Q0 = ATT_COLS // LANE
HG_F0 = HG_Q0 + HG_HEADS
HG_I0 = HG_F0 + HG_HEADS
HG_G0 = HG_I0 + HG_HEADS
HG_TOK = 256
HG_NCH = HG_TOK // HG_CHUNK
HG_HPS = 2


def _block_masks():
    row = lax.broadcasted_iota(jnp.int32, (HG_TOK, HG_TOK), 0)
    col = lax.broadcasted_iota(jnp.int32, (HG_TOK, HG_TOK), 1)
    same = (row // HG_CHUNK) == (col // HG_CHUNK)
    return jnp.logical_and(same, col <= row), jnp.logical_and(same, col >= row)


def _row_in_chunk():
    return lax.broadcasted_iota(jnp.int32, (HG_TOK, LANE), 0) % HG_CHUNK


def _chunk_cumsum(x, reverse=False):
    ric = _row_in_chunk()
    shift = 1
    while shift < HG_CHUNK:
        if reverse:
            x = x + jnp.where(ric < HG_CHUNK - shift, pltpu.roll(x, HG_TOK - shift, 0), 0.0)
        else:
            x = x + jnp.where(ric >= shift, pltpu.roll(x, shift, 0), 0.0)
        shift *= 2
    return x


def _chunk_rows(rows):
    stacked = jnp.concatenate([r[None] for r in rows], axis=0)
    return jnp.broadcast_to(stacked, (HG_NCH, HG_CHUNK, LANE)).reshape(HG_TOK, LANE)


def _chunk_slices(x):
    return [x[j * HG_CHUNK:(j + 1) * HG_CHUNK] for j in range(HG_NCH)]


def _hgrn_common(tbl, hf, hq):
    lb = _sigmoid(tbl[1:2] - tbl[0:1])
    sig = _sigmoid(hf)
    f = lb + (1.0 - lb) * sig
    sq = _sigmoid(hq)
    q, k = hq * sq, 1.0 - f
    b = _chunk_cumsum(jnp.log(f))
    last = [b[(j + 1) * HG_CHUNK - 1:(j + 1) * HG_CHUNK] for j in range(HG_NCH)]
    bl = _chunk_rows(last)
    e_b, e_nb, e_rem = jnp.exp(b), jnp.exp(-b), jnp.exp(bl - b)
    e_last = [jnp.exp(r) for r in last]
    return dict(lb=lb, sig=sig, f=f, sq=sq, q=q, k=k, e_b=e_b, e_nb=e_nb, e_rem=e_rem, e_last=e_last,
                qd=q * e_b, kd=k * e_nb, ku=k * e_rem)


def _hgrn_fwd(proj, lb_table, norm_w, mix_in, comms=()):
    bsz, seq, _ = proj.shape
    nstep = seq // HG_TOK

    def body(tbl_ref, nw_ref, q_ref, f_ref, i_ref, g_ref, mix_ref, o_ref, rec_ref, st_ref, s_scr):
        @pl.when(pl.program_id(2) == 0)
        def _():
            s_scr[...] = jnp.zeros_like(s_scr)

        lower, _ = _block_masks()
        for hp in range(HG_HPS):
            ls = slice(hp * LANE, (hp + 1) * LANE)
            v, hg = i_ref[:, ls], g_ref[:, ls]
            t = _hgrn_common(tbl_ref[:, ls], f_ref[:, ls], q_ref[:, ls])
            a = jnp.where(lower, _dot(t["qd"], t["kd"], NT_DIMS), 0.0)
            o_intra = _dot(a, v)
            v_c, ku_c, qd_c = [_chunk_slices(z.astype(BF16)) for z in (v, t["ku"], t["qd"])]
            updates = [_dot(v_c[j], ku_c[j], TN_DIMS) for j in range(HG_NCH)]
            st = s_scr[hp]
            states = []
            for j in range(HG_NCH):
                states.append(st)
                st = st * t["e_last"][j] + updates[j]
            s_scr[hp] = st
            o = o_intra + jnp.concatenate([_dot(qd_c[j], states[j], NT_DIMS) for j in range(HG_NCH)], axis=0)
            for j in range(HG_NCH):
                st_ref[hp, j] = states[j]
            o_ref[:, ls] = o
            y, _, _ = _rms_fwd(o, nw_ref[...])
            rec_ref[:, ls] = (y * (hg * _sigmoid(hg))).astype(BF16)

    width = HG_HPS * LANE
    slab = lambda first: pl.BlockSpec((None, HG_TOK, width), lambda b, h, t: (b, t, first // HG_HPS + h))
    head_out = pl.BlockSpec((None, HG_TOK, width), lambda b, h, t: (b, t, h))
    mix_out = pl.BlockSpec((None, HG_TOK, width), lambda b, h, t: (b, t, ATT_WIDTH // width + h))
    return _call(
        body, (lb_table, norm_w, proj, proj, proj, proj, mix_in), name="hgrn_fwd", grid=(bsz, HG_HEADS // HG_HPS, nstep),
        in_specs=[pl.BlockSpec((2, width), lambda b, h, t: (0, h)), pl.BlockSpec((1, LANE), lambda b, h, t: (0, 0)),
                  slab(HG_Q0), slab(HG_F0), slab(HG_I0), slab(HG_G0), pl.BlockSpec(memory_space=pl.ANY)],
        out_specs=[head_out, mix_out,
                   pl.BlockSpec((None, HG_HPS, HG_NCH, LANE, LANE), lambda b, h, t: (b, h, t, 0, 0))],
        out_shape=[_sds((bsz, seq, HG_WIDTH), F32), _sds(mix_in.shape, BF16),
                   _sds((bsz, HG_HEADS, seq // HG_CHUNK, LANE, LANE), F32)],
        scratch_shapes=[pltpu.VMEM((HG_HPS, LANE, LANE), F32)],
        sem=("parallel", "parallel", "arbitrary"), comms=comms, aliases={6: 1})


def _out_proj_fused(cat, w_out, x, post_w, g1, pre_w, sc2, sh2):
    tm = 512

    def epi(mix, ex, outs):
        x_ref, pw_ref, g1_ref, w2_ref, sc_ref, sh_ref = ex
        outs[0][...] = mix
        n1, _, _ = _rms_fwd(mix, pw_ref[...])
        x1 = x_ref[...] + g1_ref[...] * n1
        outs[1][...] = x1
        y2, _, _ = _rms_fwd(x1, w2_ref[...])
        outs[2][...] = (y2 * (1.0 + sc_ref[...]) + sh_ref[...]).astype(BF16)

    return _mm_rows(cat, w_out, name="out_proj", tm=tm, extra=(x, post_w, g1, pre_w, sc2, sh2),
                    extra_specs=[_tok_spec(tm), _vec_spec(), _row_spec(), _vec_spec(), _row_spec(), _row_spec()],
                    out_specs=[_tok_spec(tm), _tok_spec(tm), _tok_spec(tm)],
                    out_shape=[_sds(x.shape, F32), _sds(x.shape, F32), _sds(x.shape, BF16)], epi=epi)


def _acc_out(ref, first, value):
    @pl.when(first)
    def _():
        ref[...] = value

    @pl.when(jnp.logical_not(first))
    def _():
        ref[...] += value


def _down_proj_fused(r, w_down, x1, post_w, g2, target):
    tm = 256
    bsz = x1.shape[0]

    def pro(r_ref, ex, outs):
        rv = r_ref[...]
        return rv * rv

    def epi(down, ex, outs):
        x1_ref, w_ref, g2_ref, t_ref = ex
        loss_ref, dy_ref, dd_ref, dg2_ref, dw_ref = outs
        b, i = pl.program_id(0), pl.program_id(1)
        w, g2v = w_ref[...], g2_ref[...]
        n2, dh, rstd = _rms_fwd(down, w)
        err = x1_ref[...] + g2v * n2 - t_ref[...]
        part = (0.5 / D_MODEL) * jnp.sum(jnp.sum(err * err, axis=-1, keepdims=True), axis=0, keepdims=True)
        _acc_out(loss_ref, jnp.logical_and(b == 0, i == 0), jnp.broadcast_to(part, (1, LANE)))
        dy = err * (1.0 / D_MODEL)
        dy_ref[...] = dy
        _acc_out(dg2_ref, i == 0, _colsum(dy * n2))
        dd, dw_rows = _rms_bwd(dy * g2v, dh, rstd, w)
        dd_ref[...] = dd.astype(BF16)
        _acc_out(dw_ref, jnp.logical_and(b == 0, i == 0), _colsum(dw_rows))

    return _mm_rows(r, w_down, name="down_proj", tm=tm, extra=(x1, post_w, g2, target),
                    extra_specs=[_tok_spec(tm), _vec_spec(), _row_spec(), _tok_spec(tm)],
                    out_specs=[_vec_spec(LANE), _tok_spec(tm), _tok_spec(tm), _row_spec(), _vec_spec()],
                    out_shape=[_sds((1, LANE), F32), _sds(x1.shape, F32), _sds(x1.shape, BF16), _sds((bsz, 1, D_MODEL), F32),
                               _sds((1, D_MODEL), F32)], pro=pro, epi=epi)


def _up_bwd_fused(dpre, w_up4, dy, x1, mix, pre_w, sc2, post_w, g1, comms=()):
    tm = 256
    bsz = x1.shape[0]

    def epi(dh2v, ex, outs):
        dy_ref, x1_ref, mix_ref, w2_ref, sc_ref, pw_ref, g1_ref = ex
        dx1_ref, dmix_ref, dsc_ref, dsh_ref, dg1_ref, dw2_ref, dpw_ref = outs
        b, i = pl.program_id(0), pl.program_id(1)
        first = jnp.logical_and(b == 0, i == 0)
        w2, pw = w2_ref[...], pw_ref[...]
        y2, xh2, rstd2 = _rms_fwd(x1_ref[...], w2)
        _acc_out(dsh_ref, i == 0, _colsum(dh2v))
        _acc_out(dsc_ref, i == 0, _colsum(dh2v * y2))
        dx1n, dw_rows = _rms_bwd(dh2v * (1.0 + sc_ref[...]), xh2, rstd2, w2)
        _acc_out(dw2_ref, first, _colsum(dw_rows))
        dx1 = dy_ref[...] + dx1n
        dx1_ref[...] = dx1
        n1, mh, rstd1 = _rms_fwd(mix_ref[...], pw)
        _acc_out(dg1_ref, i == 0, _colsum(dx1 * n1))
        dmix, dpw_rows = _rms_bwd(dx1 * g1_ref[...], mh, rstd1, pw)
        dmix_ref[...] = dmix.astype(BF16)
        _acc_out(dpw_ref, first, _colsum(dpw_rows))

    row_shape = _sds((bsz, 1, D_MODEL), F32)
    vec_shape = _sds((1, D_MODEL), F32)
    return _mm_rows(dpre, w_up4, name="up_bwd", tm=tm, extra=(dy, x1, mix, pre_w, sc2, post_w, g1),
                    extra_specs=[_tok_spec(tm), _tok_spec(tm), _tok_spec(tm), _vec_spec(), _row_spec(), _vec_spec(), _row_spec()],
                    out_specs=[_tok_spec(tm), _tok_spec(tm), _row_spec(), _row_spec(), _row_spec(), _vec_spec(), _vec_spec()],
                    out_shape=[_sds(x1.shape, F32), _sds(x1.shape, BF16), row_shape, row_shape, row_shape, vec_shape, vec_shape],
                    epi=epi, b_chunks=w_up4.shape[0], comms=comms)


def _norm1_bwd(dh1, dx1, x, pre_w, sc1, tm=512):
    bsz, seq, _ = x.shape

    def body(dh_ref, dx1_ref, x_ref, w_ref, sc_ref, gx_ref, dsc_ref, dsh_ref, dw_ref):
        b, i = pl.program_id(0), pl.program_id(1)
        w = w_ref[...]
        dh = dh_ref[...]
        y, xh, rstd = _rms_fwd(x_ref[...], w)
        _acc_out(dsh_ref, i == 0, _colsum(dh))
        _acc_out(dsc_ref, i == 0, _colsum(dh * y))
        dx, dw_rows = _rms_bwd(dh * (1.0 + sc_ref[...]), xh, rstd, w)
        _acc_out(dw_ref, jnp.logical_and(b == 0, i == 0), _colsum(dw_rows))
        gx_ref[...] = dx1_ref[...] + dx

    row_shape = _sds((bsz, 1, D_MODEL), F32)
    return pl.pallas_call(
        body, name="norm1_bwd", grid=(bsz, seq // tm),
        in_specs=[_tok_spec(tm), _tok_spec(tm), _tok_spec(tm), _vec_spec(), _row_spec()],
        out_specs=[_tok_spec(tm), _row_spec(), _row_spec(), _vec_spec()],
        out_shape=[_sds(x.shape, F32), row_shape, row_shape, _sds((1, D_MODEL), F32)],
        compiler_params=_params("arbitrary", "arbitrary"),
    )(dh1, dx1, x, pre_w, sc1)


def _hgrn_bwd(dcat, proj, o_raw, states, lb_table, norm_w, comms=()):
    bsz, seq, _ = proj.shape
    nstep = seq // HG_TOK
    rec0 = ATT_WIDTH // LANE

    def body(tbl_ref, nw_ref, dr_ref, q_ref, f_ref, i_ref, g_ref, o_ref, st_ref,
             dq_ref, df_ref, di_ref, dg_ref, dlb_ref, dnw_ref, ds_scr):
        h, b, t = pl.program_id(0), pl.program_id(1), pl.program_id(2)

        @pl.when(t == 0)
        def _():
            ds_scr[...] = jnp.zeros_like(ds_scr)

        lower, upper = _block_masks()
        dlb_parts = []
        dnw_acc = jnp.zeros((1, LANE), F32)
        for hp in range(HG_HPS):
            ls = slice(hp * LANE, (hp + 1) * LANE)
            hq, v, hg = q_ref[:, ls], i_ref[:, ls], g_ref[:, ls]
            nw = nw_ref[...]
            c = _hgrn_common(tbl_ref[:, ls], f_ref[:, ls], hq)
            qd, kd, ku = c["qd"], c["kd"], c["ku"]
            y, on, rstd = _rms_fwd(o_ref[:, ls], nw)
            sg = _sigmoid(hg)
            dr = dr_ref[:, ls]
            dg_ref[:, ls] = (dr * y * (sg * (1.0 + hg * (1.0 - sg)))).astype(BF16)
            do, dnw_rows = _rms_bwd(dr * (hg * sg), on, rstd, nw)
            at = jnp.where(upper, _dot(kd, qd, NT_DIMS), 0.0)
            da = jnp.where(lower, _dot(do, v, NT_DIMS), 0.0)
            dat = jnp.where(upper, _dot(v, do, NT_DIMS), 0.0)
            dv = _dot(at, do)
            dqd = _dot(da, kd)
            dkd = _dot(dat, qd)
            do_c, qd_c, v_c, ku_c = [_chunk_slices(z.astype(BF16)) for z in (do, qd, v, ku)]
            outer = [_dot(do_c[j], qd_c[j], TN_DIMS) for j in range(HG_NCH)]
            ds = ds_scr[hp]
            ds_after = [None] * HG_NCH
            for j in reversed(range(HG_NCH)):
                ds_after[j] = ds
                ds = outer[j] + ds * c["e_last"][j]
            ds_scr[hp] = ds
            states = [st_ref[hp, j] for j in range(HG_NCH)]
            dv = dv + jnp.concatenate([_dot(ku_c[j], ds_after[j], NT_DIMS) for j in range(HG_NCH)], axis=0)
            dqd = dqd + jnp.concatenate([_dot(do_c[j], states[j]) for j in range(HG_NCH)], axis=0)
            dku = jnp.concatenate([_dot(v_c[j], ds_after[j]) for j in range(HG_NCH)], axis=0)
            dku_ku = dku * ku
            dbl = [_colsum(states[j] * ds_after[j]) * c["e_last"][j] + _colsum(dku_ku[j * HG_CHUNK:(j + 1) * HG_CHUNK])
                   for j in range(HG_NCH)]
            dk = dkd * c["e_nb"] + dku * c["e_rem"]
            db = dqd * qd - dkd * kd - dku_ku + jnp.where(_row_in_chunk() == HG_CHUNK - 1, _chunk_rows(dbl), 0.0)
            dfv = _chunk_cumsum(db, reverse=True) / c["f"] - dk
            sig, sq = c["sig"], c["sq"]
            df_ref[:, ls] = (dfv * (1.0 - c["lb"]) * sig * (1.0 - sig)).astype(BF16)
            dq_ref[:, ls] = (dqd * c["e_b"] * (sq * (1.0 + hq * (1.0 - sq)))).astype(BF16)
            di_ref[:, ls] = dv.astype(BF16)
            dlb_parts.append(_colsum(dfv * (1.0 - sig)))
            dnw_acc = dnw_acc + _colsum(dnw_rows)
        _acc_out(dlb_ref, jnp.logical_and(b == 0, t == 0), jnp.concatenate(dlb_parts, axis=1))
        _acc_out(dnw_ref, jnp.logical_and(h == 0, jnp.logical_and(b == 0, t == 0)), dnw_acc)

    rev = lambda t: nstep - 1 - t
    width = HG_HPS * LANE
    slab = lambda first: pl.BlockSpec((None, HG_TOK, width), lambda h, b, t: (b, rev(t), first // HG_HPS + h))
    head = pl.BlockSpec((None, HG_TOK, width), lambda h, b, t: (b, rev(t), h))
    grad_shape = _sds((bsz, seq, HG_WIDTH), BF16)
    return _call(
        body, (lb_table, norm_w, dcat, proj, proj, proj, proj, o_raw, states), name="hgrn_bwd",
        grid=(HG_HEADS // HG_HPS, bsz, nstep),
        in_specs=[pl.BlockSpec((2, width), lambda h, b, t: (0, h)), pl.BlockSpec((1, LANE), lambda h, b, t: (0, 0)),
                  slab(rec0), slab(HG_Q0), slab(HG_F0), slab(HG_I0), slab(HG_G0), head,
                  pl.BlockSpec((None, HG_HPS, HG_NCH, LANE, LANE), lambda h, b, t: (b, h, rev(t), 0, 0))],
        out_specs=[head, head, head, head, pl.BlockSpec((1, width), lambda h, b, t: (0, h)),
                   pl.BlockSpec((1, LANE), lambda h, b, t: (0, 0))],
        out_shape=[grad_shape, grad_shape, grad_shape, grad_shape, _sds((1, HG_WIDTH), F32), _sds((1, LANE), F32)],
        scratch_shapes=[pltpu.VMEM((HG_HPS, LANE, LANE), F32)],
        sem=("arbitrary", "arbitrary", "arbitrary"), comms=comms)


def _attn_bwd(dcat, raw, w_norm, qh, kh, vh, lse, sinks, comms=()):
    bsz, nblk = qh.shape[0], qh.shape[1]
    seq = nblk * WINDOW

    def body(sink_ref, da_ref, raw_ref, w_ref, q_ref, kc_ref, kp_ref, vc_ref, vp_ref, l_ref,
             dq_ref, dkd_ref, dkp_ref, dvd_ref, dvp_ref, dw_ref, dsink_ref):
        b, i = pl.program_id(0), pl.program_id(1)
        first = jnp.logical_and(b == 0, i == 0)
        w = w_ref[...]
        _, on, rstd = _rms_fwd(raw_ref[...], w)
        do_step, dw_rows = _rms_bwd(da_ref[...], on, rstd, w)
        _acc_out(dw_ref, first, _colsum(dw_rows))
        lane8 = lax.broadcasted_iota(jnp.int32, (1, ATT_Q_HEADS), 1)
        dsink = jnp.zeros((1, ATT_Q_HEADS), F32)
        for blk in range(ATT_BPS):
            tok = slice(blk * WINDOW, (blk + 1) * WINDOW)
            mask = _band_mask(True if blk else i > 0)
            raw_v, do_all = raw_ref[tok, :], do_step[tok]
            for g in range(ATT_KV_HEADS):
                gs = slice(g * ATT_HEAD_DIM, (g + 1) * ATT_HEAD_DIM)
                heads = [slice((g * ATT_GROUP + hh) * ATT_HEAD_DIM, (g * ATT_GROUP + hh + 1) * ATT_HEAD_DIM)
                         for hh in range(ATT_GROUP)]
                q = q_ref[blk, g]
                keys, vals = _band(kp_ref, kc_ref, g, blk), _band(vp_ref, vc_ref, g, blk)
                do_g = jnp.concatenate([do_all[:, hs] for hs in heads], axis=0)
                dsum = jnp.concatenate([jnp.sum(do_all[:, hs] * raw_v[:, hs], axis=-1, keepdims=True) for hs in heads], axis=0)
                lse_g = jnp.concatenate([l_ref[tok, g * ATT_GROUP + hh:g * ATT_GROUP + hh + 1] for hh in range(ATT_GROUP)], axis=0)
                p = jnp.where(mask, jnp.exp(_dot(q, keys, NT_DIMS) * ATT_SCALE - lse_g), 0.0)
                sink_part = jnp.exp(_sink_column(sink_ref, g) - lse_g) * dsum
                for hh in range(ATT_GROUP):
                    head_sum = jnp.sum(sink_part[hh * WINDOW:(hh + 1) * WINDOW], axis=0, keepdims=True)
                    dsink = dsink - jnp.where(lane8 == g * ATT_GROUP + hh, head_sum, 0.0)
                ds = p * (_dot(do_g, vals, NT_DIMS) - dsum) * ATT_SCALE
                dq_g = _dot(ds, keys)
                for hh, hs in enumerate(heads):
                    dq_ref[tok, hs] = dq_g[hh * WINDOW:(hh + 1) * WINDOW]
                dk_g = _dot(ds, q, TN_DIMS)
                dv_g = _dot(p, do_g, TN_DIMS)
                dkp_ref[tok, gs], dkd_ref[tok, gs] = dk_g[:WINDOW], dk_g[WINDOW:]
                dvp_ref[tok, gs], dvd_ref[tok, gs] = dv_g[:WINDOW], dv_g[WINDOW:]
        _acc_out(dsink_ref, first, dsink)

    rows = ATT_BPS * WINDOW
    cur = lambda width: pl.BlockSpec((None, rows, width), lambda b, i: (b, i, 0))
    q_spec, kv_cur, kv_prev = _attn_specs()
    kv_shape = _sds((bsz, seq, LANE), F32)
    return _call(
        body, (sinks, dcat, raw, w_norm, qh, kh, kh, vh, vh, lse), name="attn_bwd", grid=(bsz, nblk // ATT_BPS),
        in_specs=[pl.BlockSpec(memory_space=pltpu.SMEM), cur(ATT_WIDTH), cur(ATT_WIDTH), _vec_spec(ATT_WIDTH), q_spec,
                  kv_cur, kv_prev, kv_cur, kv_prev, cur(ATT_Q_HEADS)],
        out_specs=[cur(ATT_WIDTH), cur(LANE), cur(LANE), cur(LANE), cur(LANE), _vec_spec(ATT_WIDTH), _vec_spec(ATT_Q_HEADS)],
        out_shape=[_sds((bsz, seq, ATT_WIDTH), F32), kv_shape, kv_shape, kv_shape, kv_shape, _sds((1, ATT_WIDTH), F32),
                   _sds((1, ATT_Q_HEADS), F32)],
        sem=("arbitrary", "arbitrary"), comms=comms)


def _rope_bwd(dq, dkd, dkp, dvd, dvp, tables):
    bsz, seq, _ = dq.shape
    nblk = seq // WINDOW
    half = ROPE_DIM // 2

    def body(dq_ref, dkd_ref, dkp_ref, dvd_ref, dvp_ref, c_ref, u_ref, d_ref, o_ref):
        c, u, d = c_ref[...], u_ref[...], d_ref[...]
        has_next = pl.program_id(1) < nblk - 1

        def unrope(g):
            return g * c + pltpu.roll(g * u, LANE - half, 1) + pltpu.roll(g * d, half, 1)

        for s in range(ATT_WIDTH // LANE):
            o_ref[:, s * LANE:(s + 1) * LANE] = unrope(dq_ref[:, s * LANE:(s + 1) * LANE]).astype(BF16)
        dk = dkd_ref[...] + jnp.where(has_next, dkp_ref[...], 0.0)
        o_ref[:, ATT_WIDTH:ATT_WIDTH + LANE] = unrope(dk).astype(BF16)
        o_ref[:, ATT_WIDTH + LANE:ATT_COLS] = (dvd_ref[...] + jnp.where(has_next, dvp_ref[...], 0.0)).astype(BF16)

    cur = lambda width: pl.BlockSpec((None, WINDOW, width), lambda b, i: (b, i, 0))
    nxt = pl.BlockSpec((None, WINDOW, LANE), lambda b, i: (b, jnp.minimum(i + 1, nblk - 1), 0))
    tab = pl.BlockSpec((WINDOW, LANE), lambda b, i: (i, 0))
    return pl.pallas_call(
        body, name="rope_bwd", grid=(bsz, nblk),
        in_specs=[cur(ATT_WIDTH), cur(LANE), nxt, cur(LANE), nxt, tab, tab, tab],
        out_specs=cur(ATT_COLS), out_shape=_sds((bsz, seq, ATT_COLS), BF16),
        compiler_params=_params("parallel", "parallel"),
    )(dq, dkd, dkp, dvd, dvp, *tables)


def _other_chips(x, y):
    return [(1 - x, y), (x, 1 - y), (1 - x, 1 - y)]


def _sem_pair(n):
    return [pltpu.SemaphoreType.DMA((n,)), pltpu.SemaphoreType.DMA((n,))]


def _plan_chip_gather(blocks, bufs):
    n = len(blocks)

    def copies(ins, outs, sems):
        x, y, c = _mesh_pos()
        sends, lands = [], []
        for a in range(n):
            for j, chip in enumerate(_other_chips(x, y)):
                k = 3 * a + j
                sends.append(pltpu.make_async_remote_copy(
                    src_ref=ins[a], dst_ref=outs[a].at[4 * x + 2 * y + c], send_sem=sems[0].at[k], recv_sem=sems[1].at[k],
                    device_id=(*chip, c), device_id_type=MESH))
                slot = outs[a].at[4 * chip[0] + 2 * chip[1] + c]
                lands.append(pltpu.make_async_remote_copy(
                    src_ref=slot, dst_ref=slot, send_sem=sems[0].at[k], recv_sem=sems[1].at[k],
                    device_id=(*chip, c), device_id_type=MESH))
        return sends, lands

    def start(ins, outs, sems):
        for cp in copies(ins, outs, sems)[0]:
            cp.start()

    def finish(ins, outs, sems):
        sends, lands = copies(ins, outs, sems)
        for cp in lands:
            cp.wait_recv()
        for cp in sends:
            cp.wait_send()

    return _Comm(list(blocks) + list(bufs), [_sds(b.shape, b.dtype) for b in bufs], _sem_pair(3 * n), start, finish,
                 aliases=[(n + a, a) for a in range(n)])


def _plan_pair_forward(bufs):
    n = len(bufs)

    def copies(outs, sems):
        x, y, c = _mesh_pos()
        sends, lands = [], []
        for a in range(n):
            for j, chip in enumerate(_other_chips(x, y)):
                k = 3 * a + j
                slot = outs[a].at[4 * chip[0] + 2 * chip[1] + c]
                sends.append(pltpu.make_async_remote_copy(
                    src_ref=slot, dst_ref=slot, send_sem=sems[0].at[k], recv_sem=sems[1].at[k],
                    device_id=(x, y, 1 - c), device_id_type=MESH))
                theirs = outs[a].at[4 * chip[0] + 2 * chip[1] + 1 - c]
                lands.append(pltpu.make_async_remote_copy(
                    src_ref=theirs, dst_ref=theirs, send_sem=sems[0].at[k], recv_sem=sems[1].at[k],
                    device_id=(x, y, 1 - c), device_id_type=MESH))
        return sends, lands

    def start(ins, outs, sems):
        for cp in copies(outs, sems)[0]:
            cp.start()

    def finish(ins, outs, sems):
        sends, lands = copies(outs, sems)
        for cp in lands:
            cp.wait_recv()
        for cp in sends:
            cp.wait_send()

    return _Comm(list(bufs), [_sds(b.shape, b.dtype) for b in bufs], _sem_pair(3 * n), start, finish,
                 aliases=[(a, a) for a in range(n)])


def _plan_pair(arrays, other_half):
    n = len(arrays)
    per = N_CHIPS if other_half == "chip_major" else 1

    def copies(ins, outs, sems):
        x, y, c = _mesh_pos()
        out = []
        for a in range(n):
            for k in range(per):
                if other_half == "chip_major":
                    src, dst = ins[a].at[k, 1 - c], outs[a].at[k]
                else:
                    src, dst = (ins[a].at[1 - c] if other_half else ins[a]), outs[a]
                out.append(pltpu.make_async_remote_copy(
                    src_ref=src, dst_ref=dst, send_sem=sems[0].at[per * a + k], recv_sem=sems[1].at[per * a + k],
                    device_id=(x, y, 1 - c), device_id_type=MESH))
        return out

    def start(ins, outs, sems):
        for cp in copies(ins, outs, sems):
            cp.start()

    def finish(ins, outs, sems):
        for cp in copies(ins, outs, sems):
            cp.wait()

    if other_half == "chip_major":
        shapes = [_sds((a.shape[0],) + a.shape[2:], a.dtype) for a in arrays]
    else:
        shapes = [_sds(a.shape[1:] if other_half else a.shape, a.dtype) for a in arrays]
    return _Comm(list(arrays), shapes, _sem_pair(per * n), start, finish)


def _plan_chip_exchange(arrays):
    n = len(arrays)

    def copies(ins, outs, sems):
        x, y, c = _mesh_pos()
        sends, lands = [], []
        for a in range(n):
            for j, chip in enumerate(_other_chips(x, y)):
                k = 3 * a + j
                sends.append(pltpu.make_async_remote_copy(
                    src_ref=ins[a].at[2 * chip[0] + chip[1]], dst_ref=outs[a].at[2 * x + y], send_sem=sems[0].at[k],
                    recv_sem=sems[1].at[k], device_id=(*chip, c), device_id_type=MESH))
                slot = outs[a].at[2 * chip[0] + chip[1]]
                lands.append(pltpu.make_async_remote_copy(
                    src_ref=slot, dst_ref=slot, send_sem=sems[0].at[k], recv_sem=sems[1].at[k],
                    device_id=(*chip, c), device_id_type=MESH))
        return sends, lands

    def start(ins, outs, sems):
        for cp in copies(ins, outs, sems)[0]:
            cp.start()

    def finish(ins, outs, sems):
        sends, lands = copies(ins, outs, sems)
        for cp in lands:
            cp.wait_recv()
        for cp in sends:
            cp.wait_send()

    return _Comm(list(arrays), [_sds(a.shape, a.dtype) for a in arrays], _sem_pair(3 * n), start, finish)


def _comm_only(comms, name):
    return _call(lambda: None, (), name=name, grid=(), in_specs=[], out_specs=[], out_shape=[], sem=(), comms=comms)[1]


def _allgather8(arrays, name):
    return _comm_only([_plan_allgather8(arrays)], name)[0]


def _plan_allgather8(arrays):
    n = len(arrays)

    def parts(ins, outs, sems):
        send_sems, recv_sems, local_sems = sems
        x, y, c = _mesh_pos()
        me, sibling = (x, y, c), (x, y, 1 - c)
        chips = _other_chips(x, y)

        def copy(a, k, block, to, src=None):
            dst = outs[a].at[4 * block[0] + 2 * block[1] + block[2]]
            return pltpu.make_async_remote_copy(
                src_ref=dst if src is None else src, dst_ref=dst, send_sem=send_sems.at[7 * a + k],
                recv_sem=recv_sems.at[7 * a + k], device_id=to, device_id_type=MESH)

        mine = [pltpu.make_async_copy(ins[a], outs[a].at[4 * x + 2 * y + c], local_sems.at[a]) for a in range(n)]
        first = []
        for a in range(n):
            first.append(copy(a, 0, me, sibling, src=ins[a]))
            first += [copy(a, 1 + j, me, (*chip, c), src=ins[a]) for j, chip in enumerate(chips)]
        return copy, mine, first, me, sibling, chips, c

    def start(ins, outs, sems):
        _, mine, first, *_ = parts(ins, outs, sems)
        for cp in mine + first:
            cp.start()

    def finish(ins, outs, sems):
        copy, mine, first, me, sibling, chips, c = parts(ins, outs, sems)
        passed = []
        for j, chip in enumerate(chips):
            for a in range(n):
                copy(a, 1 + j, (*chip, c), me).wait_recv()
                fwd = copy(a, 4 + j, (*chip, c), sibling)
                fwd.start()
                passed.append(fwd)
        for a in range(n):
            copy(a, 0, sibling, me).wait_recv()
            for j, chip in enumerate(chips):
                copy(a, 4 + j, (*chip, 1 - c), me).wait_recv()
        for cp in first + passed:
            cp.wait_send()
        for cp in mine:
            cp.wait()

    sems = [pltpu.SemaphoreType.DMA((7 * n,)), pltpu.SemaphoreType.DMA((7 * n,)), pltpu.SemaphoreType.DMA((n,))]
    return _Comm(list(arrays), [_sds((N_DEV,) + a.shape, a.dtype) for a in arrays], sems, start, finish)


def _pair_sum(g, q, core, name, chip_major=False):
    rows, cols = g.shape[2:]
    tr = _row_tile(rows)

    def body(core_ref, g_ref, q_ref, o_ref):
        o_ref[...] = (g_ref[...] + q_ref[...]).astype(BF16)

    blk = pl.BlockSpec((None, tr, cols), lambda k, i, core_ref: (k, i, 0))
    if chip_major:
        own = pl.BlockSpec((None, None, tr, cols), lambda k, i, core_ref: (k, core_ref[0], i, 0))
    else:
        own = pl.BlockSpec((None, None, tr, cols), lambda k, i, core_ref: (core_ref[0], k, i, 0))
    return pl.pallas_call(
        body, name=name,
        grid_spec=pltpu.PrefetchScalarGridSpec(num_scalar_prefetch=1, grid=(N_CHIPS, rows // tr), in_specs=[own, blk], out_specs=blk),
        out_shape=_sds((N_CHIPS, rows, cols), BF16), compiler_params=_params("parallel", "parallel"),
    )(core, g, q)


def _sum_chips(own, landed, chip, name):
    _, rows, cols = own.shape
    tr = _row_tile(rows)

    def body(chip_ref, own_ref, a_ref, b_ref, c_ref, o_ref):
        acc = own_ref[...].astype(F32) + a_ref[...].astype(F32)
        o_ref[...] = (acc + b_ref[...].astype(F32)) + c_ref[...].astype(F32)

    blk = lambda flip: pl.BlockSpec((None, tr, cols), lambda i, chip_ref: (jnp.bitwise_xor(chip_ref[0], flip), i, 0))
    return pl.pallas_call(
        body, name=name,
        grid_spec=pltpu.PrefetchScalarGridSpec(num_scalar_prefetch=1, grid=(rows // tr,), in_specs=[blk(0), blk(1), blk(2), blk(3)],
                                               out_specs=pl.BlockSpec((tr, cols), lambda i, chip_ref: (i, 0))),
        out_shape=_sds((rows, cols), F32), compiler_params=_params("parallel"),
    )(chip, own, landed, landed, landed)


SMALL_ROWS = 120
PACK_ROWS = 168


def _rows(a, nrows):
    flat = a.reshape(-1)
    return jnp.pad(flat, (0, nrows * LANE - flat.shape[0])).reshape(nrows, LANE)


def _pack_small(b_ada, pre_w_mix, post_w_mix, pre_w_mlp, post_w_mlp, attn_out_w, hg_norm_w, attn_sinks, lb_table):
    return jnp.concatenate([
        _rows(b_ada, 48), _rows(pre_w_mix, 8), _rows(post_w_mix, 8), _rows(pre_w_mlp, 8), _rows(post_w_mlp, 8),
        _rows(attn_out_w, 8), _rows(hg_norm_w, 8), _rows(attn_sinks, 8), _rows(lb_table[0], 8), _rows(lb_table[1], 8)], axis=0)


def _unpack_small(p):
    vec = lambda lo, n: p[lo:lo + n // LANE].reshape(1, n)
    lb = jnp.stack([p[104:108].reshape(HG_WIDTH), p[112:116].reshape(HG_WIDTH)])
    return dict(b_ada=vec(0, N_MOD * D_MODEL), pre_w_mix=vec(48, D_MODEL), post_w_mix=vec(56, D_MODEL), pre_w_mlp=vec(64, D_MODEL),
                post_w_mlp=vec(72, D_MODEL), attn_out_w=vec(80, ATT_WIDTH), hg_norm_w=p[88:89], attn_sinks=p[96:97, :ATT_Q_HEADS],
                lb_table=lb)


def _small_update(packs, w, m, v):
    def body(p_ref, w_ref, m_ref, v_ref, g_ref, dl_ref, nm_ref, nv_ref, loss_ref):
        tot = p_ref[0]
        for d in range(1, N_DEV):
            tot = tot + p_ref[d]
        wv = w_ref[...]
        p1 = _sigmoid(wv[112:120] - wv[104:112])
        s = tot[152:160] * p1 * (1.0 - p1)
        g = jnp.concatenate([tot[0:48] + tot[48:96], tot[96:152], -s, s], axis=0)
        g_ref[...] = g
        dl_ref[...], nm_ref[...], nv_ref[...] = _adamw_math(g, wv, m_ref[...], v_ref[...])
        loss_ref[...] = tot[160:168]

    shp = _sds((SMALL_ROWS, LANE), F32)
    return pl.pallas_call(body, name="small_update", out_shape=[shp] * 4 + [_sds((8, LANE), F32)],
                          compiler_params=_params())(packs, w, m, v)


def kernel(x, c, w_ada, b_ada, pre_w_mix, w_in, attn_sinks, attn_out_w, lb_table, hg_norm_w, w_out, post_w_mix, pre_w_mlp, w_up, w_down, post_w_mlp, loss_target, m_w_ada, m_b_ada, m_pre_w_mix, m_w_in, m_attn_sinks, m_attn_out_w, m_lb_table, m_hg_norm_w, m_w_out, m_post_w_mix, m_pre_w_mlp, m_w_up, m_w_down, m_post_w_mlp, v_w_ada, v_b_ada, v_pre_w_mix, v_w_in, v_attn_sinks, v_attn_out_w, v_lb_table, v_hg_norm_w, v_w_out, v_post_w_mix, v_pre_w_mlp, v_w_up, v_w_down, v_post_w_mlp):
    xi, yi, ci = _mesh_pos()
    chip = 2 * xi + yi
    dev = 2 * chip + ci
    bsz, seq, _ = x.shape
    ntok = bsz * seq
    ada_cols = w_ada.shape[2]
    core = jnp.reshape(ci, (1,)).astype(jnp.int32)
    chip_idx = jnp.reshape(chip, (1,)).astype(jnp.int32)
    flat = lambda a: a.reshape(ntok, a.shape[-1])
    unflat = lambda a: a.reshape(bsz, seq, a.shape[-1])
    tables = _rope_tables(seq)

    def row_half(w):
        rows = w.shape[1] // 2
        return lax.dynamic_slice_in_dim(w[0], ci * rows, rows, axis=0).astype(BF16)

    def gather_buffer(w):
        rows, cols = w.shape[1] // 2, w.shape[2]
        own = w[0].astype(BF16).reshape(2, rows, cols)
        return lax.dynamic_update_slice(jnp.zeros((N_DEV, rows, cols), BF16), own, (2 * chip, 0, 0))

    w_in_t, m_in_t, v_in_t = [jnp.transpose(a[0])[None] for a in (w_in, m_w_in, v_w_in)]
    c_g, in_g = _allgather8([c, row_half(w_in_t)], "gather_first")
    c_all = c_g.reshape(N_DEV * bsz, D_MODEL)
    w_in_full = in_g.reshape(IN_COLS, D_MODEL)

    b_cols = lax.dynamic_slice_in_dim(b_ada, chip * ada_cols, ada_cols, axis=1)
    mod_part = _ada_fwd(c_all, w_ada[0], b_cols)
    half_rows = mod_part.shape[0] // 2
    (mod_g,) = _allgather8([lax.dynamic_slice_in_dim(mod_part, ci * half_rows, half_rows, axis=0)], "gather_mod")
    mod_all = mod_g.reshape(N_CHIPS, 2, half_rows, ada_cols).transpose(1, 2, 0, 3).reshape(N_DEV * bsz, N_MOD * D_MODEL)
    mod = lax.dynamic_slice_in_dim(mod_all, dev * bsz, bsz, axis=0)
    sh1, sc1, g1, sh2, sc2, g2 = [mod[:, i * D_MODEL:(i + 1) * D_MODEL].reshape(bsz, 1, D_MODEL) for i in range(N_MOD)]

    (h1, proj), ((out_g,),) = _in_proj_fused(x, pre_w_mix, sc1, sh1, w_in_full,
                                             comms=[_plan_chip_gather([row_half(w_out)], [gather_buffer(w_out)])])
    qh, kh, vh = _rope_fwd(proj, tables)
    (attn_raw, cat, lse), ((up_g,), (out_g,)) = _attn_fwd(
        qh, kh, vh, attn_sinks, attn_out_w,
        comms=[_plan_chip_gather([row_half(w_up)], [gather_buffer(w_up)]), _plan_pair_forward([out_g])])
    (o_raw, cat, states), ((down_g,), (up_g,)) = _hgrn_fwd(
        proj, lb_table, hg_norm_w, cat,
        comms=[_plan_chip_gather([row_half(w_down)], [gather_buffer(w_down)]), _plan_pair_forward([up_g])])
    w_out_full = out_g.reshape(D_MODEL, D_MODEL)
    w_up4 = up_g.reshape(N_CHIPS, D_MODEL, D_MODEL)
    mix, x1, h2 = _out_proj_fused(cat, w_out_full, x, post_w_mix, g1, pre_w_mlp, sc2, sh2)
    big_tm = min(ntok, 2048)
    up_spec = pl.BlockSpec((None, D_MODEL, D_MODEL), lambda i, j: (j, 0, 0))
    r, ((down_g,),) = _mm(flat(h2), w_up4, name="up_proj", out_dtype=BF16, tm=big_tm, tn=D_MODEL, n_out=D_FF, b_spec=up_spec,
                          epi=lambda acc: jnp.maximum(acc, 0.0), comms=[_plan_pair_forward([down_g])])
    w_down_full = down_g.reshape(D_FF, D_MODEL)
    square = lambda t: t * t
    loss_row, dy, dd, dg2, d_post_mlp = _down_proj_fused(unflat(r), w_down_full, x1, post_w_mlp, g2, loss_target)

    dpre = _mm(flat(dd), w_down_full, name="down_bwd", out_dtype=BF16, trans_b=True, tm=big_tm, tn=D_MODEL, extra=(r,),
               epi=lambda acc, rt: acc * (2.0 * rt.astype(F32)))
    half_rows = D_MODEL // 2
    g_down = _mm_tn(r, flat(dd), name="down_wgrad", tk=half_rows, tn=D_MODEL, a_fn=square,
                    out_shape=_sds((2, N_CHIPS, half_rows, D_MODEL), F32),
                    out_spec=pl.BlockSpec((None, None, half_rows, D_MODEL), lambda i, j: (i % 2, i // 2, 0, 0)))
    (dx1, dmix, dsc2, dsh2, dg1, d_pre_mlp, d_post_mix), ((q_down,),) = _up_bwd_fused(
        unflat(dpre), w_up4, dy, x1, mix, pre_w_mlp, sc2, post_w_mix, g1, comms=[_plan_pair([g_down], True)])
    g_up = _mm_tn(flat(h2), dpre, name="up_wgrad", tk=D_MODEL, tn=half_rows,
                  out_shape=_sds((2, N_CHIPS, half_rows, D_MODEL), F32),
                  out_spec=pl.BlockSpec((2, None, half_rows, half_rows), lambda i, j: (0, j // 2, 0, j % 2)))
    s_down = _pair_sum(g_down, q_down, core, "pair_sum_down")

    dcat, ((q_up,),) = _mm(flat(dmix), w_out_full, name="out_bwd", out_dtype=F32, trans_b=True, comms=[_plan_pair([g_up], True)])
    dcat = unflat(dcat)
    s_up = _pair_sum(g_up, q_up, core, "pair_sum_up")
    out_rows = D_MODEL // N_CHIPS
    g_out = _mm_tn(flat(cat), flat(dmix), name="out_wgrad", tk=out_rows, tn=half_rows,
                   out_shape=_sds((2, N_CHIPS, out_rows, half_rows), F32),
                   out_spec=pl.BlockSpec((None, None, out_rows, half_rows), lambda i, j: (j, i, 0, 0)))
    (dhq, dhf, dhi, dhg, d_lb, d_hg_norm), ((x_down,), (q_out,)) = _hgrn_bwd(
        dcat, proj, o_raw, states, lb_table, hg_norm_w, comms=[_plan_chip_exchange([s_down]), _plan_pair([g_out], True)])
    half_down = _sum_chips(s_down, x_down, chip_idx, "sum_chips_down")
    s_out = _pair_sum(g_out, q_out, core, "pair_sum_out")
    (dq, dkd, dkp, dvd, dvp, d_attn_out, d_sinks), ((their_down,), (x_up, x_out)) = _attn_bwd(
        dcat, attn_raw, attn_out_w, qh, kh, vh, lse, attn_sinks,
        comms=[_plan_pair([half_down], False), _plan_chip_exchange([s_up, s_out])])
    half_up = _sum_chips(s_up, x_up, chip_idx, "sum_chips_up")
    half_out = _sum_chips(s_out, x_out, chip_idx, "sum_chips_out")
    dproj_a = _rope_bwd(dq, dkd, dkp, dvd, dvp, tables)
    dproj = flat(jnp.concatenate([dproj_a, dhq, dhf, dhi, dhg], axis=-1))
    in_rows = IN_COLS // N_CHIPS // 2
    g_in = _mm_tn(dproj, flat(h1), name="in_wgrad", tk=2 * LANE, tn=D_MODEL).reshape(N_CHIPS, 2, in_rows, D_MODEL)
    dh1, ((q_in,), (their_up, their_out)) = _mm(
        dproj, w_in_full, name="in_bwd", out_dtype=F32,
        comms=[_plan_pair([g_in], "chip_major"), _plan_pair([half_up, half_out], False)])
    s_in = _pair_sum(g_in, q_in, core, "pair_sum_in", chip_major=True)
    grad_x, dsc1, dsh1, d_pre_mix = _norm1_bwd(unflat(dh1), dx1, x, pre_w_mix, sc1)

    dmod = jnp.concatenate([dsh1, dsc1, dg1, dsh2, dsc2, dg2], axis=-1).reshape(bsz, N_MOD * D_MODEL)
    pack = jnp.concatenate([
        _rows(dmod, 96), _rows(d_pre_mix, 8), _rows(d_post_mix, 8), _rows(d_pre_mlp, 8), _rows(d_post_mlp, 8),
        _rows(d_attn_out, 8), _rows(d_hg_norm, 8), _rows(d_sinks, 8), _rows(d_lb, 8), _rows(loss_row, 8)], axis=0)
    (packs,), (x_in,) = _comm_only([_plan_allgather8([pack]), _plan_chip_exchange([s_in])], "gather_small")
    half_in = _sum_chips(s_in, x_in, chip_idx, "sum_chips_in")
    ((their_in,),) = _comm_only([_plan_pair([half_in], False)], "pair_swap_in")
    small_args = lambda pre: (pre["b_ada"], pre["pre_w_mix"], pre["post_w_mix"], pre["pre_w_mlp"], pre["post_w_mlp"],
                              pre["attn_out_w"], pre["hg_norm_w"], pre["attn_sinks"], pre["lb_table"])
    w_small = dict(b_ada=b_ada, pre_w_mix=pre_w_mix, post_w_mix=post_w_mix, pre_w_mlp=pre_w_mlp, post_w_mlp=post_w_mlp,
                   attn_out_w=attn_out_w, hg_norm_w=hg_norm_w, attn_sinks=attn_sinks, lb_table=lb_table)
    m_small = dict(b_ada=m_b_ada, pre_w_mix=m_pre_w_mix, post_w_mix=m_post_w_mix, pre_w_mlp=m_pre_w_mlp, post_w_mlp=m_post_w_mlp,
                   attn_out_w=m_attn_out_w, hg_norm_w=m_hg_norm_w, attn_sinks=m_attn_sinks, lb_table=m_lb_table)
    v_small = dict(b_ada=v_b_ada, pre_w_mix=v_pre_w_mix, post_w_mix=v_post_w_mix, pre_w_mlp=v_pre_w_mlp, post_w_mlp=v_post_w_mlp,
                   attn_out_w=v_attn_out_w, hg_norm_w=v_hg_norm_w, attn_sinks=v_attn_sinks, lb_table=v_lb_table)
    *small_packed, loss_rows = _small_update(packs, _pack_small(*small_args(w_small)), _pack_small(*small_args(m_small)),
                                             _pack_small(*small_args(v_small)))
    small_out = [_unpack_small(p) for p in small_packed]
    loss = loss_rows[0, 0]

    dmod_all = packs[:, :96, :].reshape(N_DEV * bsz, N_MOD * D_MODEL)
    dmod_cols = lax.dynamic_slice_in_dim(dmod_all, chip * ada_cols, ada_cols, axis=1)
    ada_out = _ada_bwd_adamw(c_all, dmod_cols, w_ada[0], m_w_ada[0], v_w_ada[0])

    big = dict(
        w_in=tuple(jnp.transpose(a) for a in _adamw_halves(half_in, their_in, core, w_in_t[0], m_in_t[0], v_in_t[0], axis=0,
                                                           name="adamw_in")),
        w_up=tuple(_adamw_halves(half_up, their_up, core, w_up[0], m_w_up[0], v_w_up[0], axis=0, name="adamw_up")),
        w_out=tuple(_adamw_halves(half_out, their_out, core, w_out[0], m_w_out[0], v_w_out[0], axis=1, name="adamw_out")),
        w_down=tuple(_adamw_halves(half_down, their_down, core, w_down[0], m_w_down[0], v_w_down[0], axis=0, name="adamw_down")),
        w_ada=tuple(ada_out),
    )
    order = ("w_ada", "b_ada", "pre_w_mix", "w_in", "attn_sinks", "attn_out_w", "lb_table", "hg_norm_w", "w_out", "post_w_mix",
             "pre_w_mlp", "w_up", "w_down", "post_w_mlp")
    outs = [loss, grad_x]
    for kind in range(4):
        for nm in order:
            outs.append(big[nm][kind][None] if nm in big else small_out[kind][nm])
    return tuple(outs)
```

```python
import functools

import jax
import jax.numpy as jnp
from jax import lax
from jax.experimental import pallas as pl
from jax.experimental.pallas import tpu as pltpu

F32 = jnp.float32
BF16 = jnp.bfloat16

D_MODEL = 1024
ATT_WIDTH = 512
ATT_HEAD_DIM = 64
ATT_Q_HEADS = 8
ATT_KV_HEADS = 2
ATT_GROUP = ATT_Q_HEADS // ATT_KV_HEADS
ATT_KV_COLS = ATT_KV_HEADS * ATT_HEAD_DIM
WINDOW = 128
ROPE_DIM = 16
ROPE_THETA = 500000.0
HG_WIDTH = 512
MIX_WIDTH = ATT_WIDTH + HG_WIDTH
HG_HEAD_DIM = 128
HG_HEADS = 4
HG_CHUNK = 32
IN_COLS = ATT_WIDTH + 2 * ATT_KV_COLS + 4 * HG_WIDTH
ATT_COLS = ATT_WIDTH + 2 * ATT_KV_COLS
D_FF = 4 * D_MODEL
N_MOD = 6
EPS = 1e-6
ATT_SCALE = ATT_HEAD_DIM ** -0.5

ADAM_LR = 0.001
ADAM_B1 = 0.9
ADAM_B2 = 0.999
ADAM_EPS = 1e-08
ADAM_WD = 0.01
ADAM_STEP = 10

N_CHIPS = 4
N_DEV = 8
LANE = 128
VMEM_LIMIT = 48 * 1024 * 1024
MESH = pl.DeviceIdType.MESH

NT_DIMS = (((1,), (1,)), ((), ()))
TN_DIMS = (((0,), (0,)), ((), ()))


def _sds(shape, dtype):
    return jax.ShapeDtypeStruct(tuple(shape), dtype)


def _params(*sem):
    return pltpu.CompilerParams(dimension_semantics=sem, vmem_limit_bytes=VMEM_LIMIT)


def _sigmoid(x):
    return 1.0 / (1.0 + jnp.exp(-x))


def _dot(a, b, dims=None):
    a, b = a.astype(BF16), b.astype(BF16)
    if dims is None:
        return jnp.dot(a, b, preferred_element_type=F32)
    return lax.dot_general(a, b, dims, preferred_element_type=F32)


def _rms_fwd(x, w):
    rstd = lax.rsqrt(jnp.mean(x * x, axis=-1, keepdims=True) + EPS)
    xh = x * rstd
    return xh * w, xh, rstd


def _rms_bwd(dy, xh, rstd, w):
    dxh = dy * w
    dx = rstd * (dxh - xh * jnp.mean(dxh * xh, axis=-1, keepdims=True))
    return dx, dy * xh


def _colsum(x):
    return jnp.sum(x, axis=0, keepdims=True)


def _row_tile(rows, cap=256):
    return max(t for t in range(16, cap + 1, 16) if rows % t == 0)


HBM_SPEC = pl.BlockSpec(memory_space=pltpu.HBM)


def _mesh_pos():
    return lax.axis_index("x"), lax.axis_index("y"), lax.axis_index("c")


class _Comm:
    def __init__(self, ins, outs, sems, start, finish, aliases=()):
        self.ins, self.outs, self.sems = list(ins), list(outs), list(sems)
        self.start, self.finish, self.aliases = start, finish, tuple(aliases)


def _call(body, args, *, name, grid, in_specs, out_specs, out_shape, sem, scratch_shapes=(), comms=(), aliases=None):
    scratch_shapes = list(scratch_shapes)
    if not comms:
        return pl.pallas_call(body, name=name, grid=grid, in_specs=in_specs, out_specs=out_specs, out_shape=out_shape,
                              input_output_aliases=dict(aliases or {}), scratch_shapes=scratch_shapes,
                              compiler_params=_params(*sem))(*args)
    single = not isinstance(out_shape, (list, tuple))
    out_specs_l = [out_specs] if single else list(out_specs)
    out_shape_l = [out_shape] if single else list(out_shape)
    n_in, n_out, n_scr = len(in_specs), len(out_shape_l), len(scratch_shapes)
    n_ci = [len(cm.ins) for cm in comms]
    n_co = [len(cm.outs) for cm in comms]
    n_cs = [len(cm.sems) for cm in comms]
    aliases = dict(aliases or {})
    for k, cm in enumerate(comms):
        for i, o in cm.aliases:
            aliases[n_in + sum(n_ci[:k]) + i] = n_out + sum(n_co[:k]) + o

    def fused(*refs):
        pos = [0]

        def take(n):
            part = refs[pos[0]:pos[0] + n]
            pos[0] += n
            return part

        ins = take(n_in)
        c_ins = [take(n) for n in n_ci]
        outs = take(n_out)
        c_outs = [take(n) for n in n_co]
        scr = take(n_scr)
        c_sems = [take(n) for n in n_cs]
        first, last = True, True
        for d, size in enumerate(grid):
            first = jnp.logical_and(first, pl.program_id(d) == 0)
            last = jnp.logical_and(last, pl.program_id(d) == size - 1)

        def run(which):
            for cm, ci, co, cs in zip(comms, c_ins, c_outs, c_sems):
                getattr(cm, which)(ci, co, cs)

        if grid:
            pl.when(first)(lambda: run("start"))
        else:
            run("start")
        body(*ins, *outs, *scr)
        if grid:
            pl.when(last)(lambda: run("finish"))
        else:
            run("finish")

    res = pl.pallas_call(
        fused, name=name, grid=grid, in_specs=list(in_specs) + [HBM_SPEC] * sum(n_ci),
        out_specs=out_specs_l + [HBM_SPEC] * sum(n_co), out_shape=out_shape_l + [s for cm in comms for s in cm.outs],
        input_output_aliases=aliases, scratch_shapes=scratch_shapes + [s for cm in comms for s in cm.sems],
        compiler_params=_params(*["arbitrary"] * len(grid)),
    )(*args, *[a for cm in comms for a in cm.ins])
    main = res[:n_out]
    extra, at = [], n_out
    for n in n_co:
        extra.append(list(res[at:at + n]))
        at += n
    return (main[0] if single else list(main)), extra


def _mm(a, b, *, name, out_dtype, trans_b=False, tm=512, tn=None, a_fn=None, extra=(), epi=None,
        b_spec=None, n_out=None, b_chunks=1, comms=()):
    m_total, k_total = a.shape
    if n_out is None:
        n_out = b.shape[0] if trans_b else b.shape[1]
    tn = n_out if tn is None else tn
    grid = (m_total // tm, n_out // tn)
    dims = NT_DIMS if trans_b else None
    kc = k_total // b_chunks

    def body(*refs):
        a_ref, b_ref = refs[0], refs[1]
        extra_refs = refs[2:2 + len(extra)]
        o_ref = refs[2 + len(extra)]
        if b_chunks == 1:
            av = a_ref[...]
            acc = _dot(av if a_fn is None else a_fn(av), b_ref[...], dims)
        else:
            acc = _dot(a_ref[:, 0:kc], b_ref[0], NT_DIMS)
            for k in range(1, b_chunks):
                acc = acc + _dot(a_ref[:, k * kc:(k + 1) * kc], b_ref[k], NT_DIMS)
        if epi is not None:
            acc = epi(acc, *[r[...] for r in extra_refs])
        o_ref[...] = acc.astype(out_dtype)

    if b_spec is None:
        if b_chunks > 1:
            b_spec = pl.BlockSpec((b_chunks, tn, kc), lambda i, j: (0, j, 0))
        elif trans_b:
            b_spec = pl.BlockSpec((tn, k_total), lambda i, j: (j, 0))
        else:
            b_spec = pl.BlockSpec((k_total, tn), lambda i, j: (0, j))
    in_specs = [pl.BlockSpec((tm, k_total), lambda i, j: (i, 0)), b_spec]
    in_specs += [pl.BlockSpec((tm, tn), lambda i, j: (i, j)) for _ in extra]
    return _call(
        body, (a, b, *extra), name=name, grid=grid, in_specs=in_specs,
        out_specs=pl.BlockSpec((tm, tn), lambda i, j: (i, j)),
        out_shape=_sds((m_total, n_out), out_dtype),
        sem=("parallel", "parallel"), comms=comms)


def _mm_tn(a, b, *, name, tk, tn, a_fn=None, out_shape=None, out_spec=None):
    m_total, k_total = a.shape
    n_total = b.shape[1]
    grid = (k_total // tk, n_total // tn)

    def body(a_ref, b_ref, o_ref):
        av = a_ref[...]
        part = _dot(av if a_fn is None else a_fn(av), b_ref[...], TN_DIMS)
        o_ref[...] = part.reshape(o_ref.shape)

    if out_shape is None:
        out_shape = _sds((k_total, n_total), F32)
        out_spec = pl.BlockSpec((tk, tn), lambda i, j: (i, j))
    return pl.pallas_call(
        body, name=name, grid=grid,
        in_specs=[pl.BlockSpec((m_total, tk), lambda i, j: (0, i)), pl.BlockSpec((m_total, tn), lambda i, j: (0, j))],
        out_specs=out_spec, out_shape=out_shape,
        compiler_params=_params("parallel", "parallel"),
    )(a, b)


def _ada_fwd(c_all, w_shard, b_shard):
    nb, ncol = c_all.shape[0], w_shard.shape[1]
    tn = 512

    def body(c_ref, w_ref, b_ref, o_ref):
        c = c_ref[...]
        o_ref[...] = _dot(c * _sigmoid(c), w_ref[...]) + b_ref[...]

    return pl.pallas_call(
        body, name="ada_fwd", grid=(ncol // tn,),
        in_specs=[pl.BlockSpec((nb, D_MODEL), lambda j: (0, 0)), pl.BlockSpec((D_MODEL, tn), lambda j: (0, j)),
                  pl.BlockSpec((1, tn), lambda j: (0, j))],
        out_specs=pl.BlockSpec((nb, tn), lambda j: (0, j)), out_shape=_sds((nb, ncol), F32),
        compiler_params=_params("parallel"),
    )(c_all, w_shard, b_shard)


def _adamw_math(g, w, m, v):
    m = ADAM_B1 * m + (1.0 - ADAM_B1) * g
    v = ADAM_B2 * v + (1.0 - ADAM_B2) * (g * g)
    m_hat = m / (1.0 - ADAM_B1 ** ADAM_STEP)
    v_hat = v / (1.0 - ADAM_B2 ** ADAM_STEP)
    delta = -ADAM_LR * (m_hat / (jnp.sqrt(v_hat) + ADAM_EPS) + ADAM_WD * w)
    return delta, m, v


def _ada_bwd_adamw(c_all, dmod_cols, w, m, v):
    nb, ncol = dmod_cols.shape
    tn = 256

    def body(c_ref, d_ref, w_ref, m_ref, v_ref, g_ref, dl_ref, nm_ref, nv_ref):
        c = c_ref[...]
        g = _dot(c * _sigmoid(c), d_ref[...], TN_DIMS)
        g_ref[...] = g
        dl_ref[...], nm_ref[...], nv_ref[...] = _adamw_math(g, w_ref[...], m_ref[...], v_ref[...])

    col = pl.BlockSpec((D_MODEL, tn), lambda j: (0, j))
    shp = _sds((D_MODEL, ncol), F32)
    return pl.pallas_call(
        body, name="ada_bwd_adamw", grid=(ncol // tn,),
        in_specs=[pl.BlockSpec((nb, D_MODEL), lambda j: (0, 0)), pl.BlockSpec((nb, tn), lambda j: (0, j)), col, col, col],
        out_specs=[col, col, col, col], out_shape=[shp, shp, shp, shp],
        compiler_params=_params("parallel"),
    )(c_all, dmod_cols, w, m, v)


def _adamw_halves(own, theirs, core, w, m, v, *, axis, name):
    r2, c2 = own.shape
    tr = _row_tile(r2)
    nt = r2 // tr

    def body(core_ref, own_ref, their_ref, w_ref, m_ref, v_ref, g_ref, dl_ref, nm_ref, nv_ref):
        g = jnp.where(pl.program_id(0) == core_ref[0], own_ref[...], their_ref[...])
        g_ref[...] = g
        dl_ref[...], nm_ref[...], nv_ref[...] = _adamw_math(g, w_ref[...], m_ref[...], v_ref[...])

    if axis == 0:
        full = pl.BlockSpec((tr, c2), lambda h, i, core_ref: (h * nt + i, 0))
    else:
        full = pl.BlockSpec((tr, c2), lambda h, i, core_ref: (i, h))
    half = pl.BlockSpec((tr, c2), lambda h, i, core_ref: (i, 0))
    shp = _sds(w.shape, F32)
    return pl.pallas_call(
        body, name=name,
        grid_spec=pltpu.PrefetchScalarGridSpec(num_scalar_prefetch=1, grid=(2, nt), in_specs=[half, half, full, full, full],
                                               out_specs=[full] * 4),
        out_shape=[shp] * 4, compiler_params=_params("parallel", "parallel"),
    )(core, own, theirs, w, m, v)


def _tok_spec(tm, width=D_MODEL):
    return pl.BlockSpec((None, tm, width), lambda b, i: (b, i, 0))


def _row_spec(width=D_MODEL):
    return pl.BlockSpec((None, 1, width), lambda b, i: (b, 0, 0))


def _vec_spec(width=D_MODEL):
    return pl.BlockSpec((1, width), lambda b, i: (0, 0))


def _mm_rows(a, b, *, name, tm, extra, extra_specs, out_specs, out_shape, epi, pro=None, trans_b=False, b_chunks=1, comms=()):
    bsz, seq, k_total = a.shape
    kc = k_total // b_chunks
    dims = NT_DIMS if trans_b else None

    def body(*refs):
        a_ref, b_ref = refs[0], refs[1]
        ex, outs = refs[2:2 + len(extra)], refs[2 + len(extra):]
        if b_chunks == 1:
            acc = _dot(a_ref[...] if pro is None else pro(a_ref, ex, outs), b_ref[...], dims)
        else:
            acc = _dot(a_ref[:, 0:kc], b_ref[0], NT_DIMS)
            for k in range(1, b_chunks):
                acc = acc + _dot(a_ref[:, k * kc:(k + 1) * kc], b_ref[k], NT_DIMS)
        epi(acc, ex, outs)

    b_spec = pl.BlockSpec(b.shape, lambda bb, i: (0,) * b.ndim)
    return _call(
        body, (a, b, *extra), name=name, grid=(bsz, seq // tm), in_specs=[_tok_spec(tm, k_total), b_spec, *extra_specs],
        out_specs=out_specs, out_shape=out_shape, sem=("arbitrary", "arbitrary"), comms=comms)


def _in_proj_fused(x, w, sc, sh, w_in_t, comms=()):
    tm = 512

    def pro(x_ref, ex, outs):
        y, _, _ = _rms_fwd(x_ref[...], ex[0][...])
        h = (y * (1.0 + ex[1][...]) + ex[2][...]).astype(BF16)
        outs[0][...] = h
        return h

    def epi(acc, ex, outs):
        outs[1][...] = acc

    cols = w_in_t.shape[0]
    return _mm_rows(x, w_in_t, name="in_proj", tm=tm, extra=(w, sc, sh), extra_specs=[_vec_spec(), _row_spec(), _row_spec()],
                    out_specs=[_tok_spec(tm), _tok_spec(tm, cols)],
                    out_shape=[_sds(x.shape, BF16), _sds(x.shape[:2] + (cols,), F32)], pro=pro, epi=epi, trans_b=True, comms=comms)


def _rope_tables(seq):
    half = ROPE_DIM // 2
    inv_freq = ROPE_THETA ** (-jnp.arange(0, ROPE_DIM, 2, dtype=F32) / ROPE_DIM)
    ang = jnp.arange(seq, dtype=F32)[:, None] * inv_freq[None, :]
    cos, sin = jnp.cos(ang), jnp.sin(ang)
    rest = ATT_HEAD_DIM - ROPE_DIM
    ones, zeros, zh = jnp.ones((seq, rest), F32), jnp.zeros((seq, rest), F32), jnp.zeros((seq, half), F32)
    reps = LANE // ATT_HEAD_DIM
    t_cos = jnp.tile(jnp.concatenate([cos, cos, ones], axis=1), (1, reps))
    t_up = jnp.tile(jnp.concatenate([zh, sin, zeros], axis=1), (1, reps))
    t_dn = jnp.tile(jnp.concatenate([-sin, zh, zeros], axis=1), (1, reps))
    return t_cos, t_up, t_dn


GROUP_ROWS = ATT_GROUP * WINDOW


def _rope_fwd(proj, tables):
    bsz, seq, _ = proj.shape
    nblk = seq // WINDOW
    half = ROPE_DIM // 2
    tm = 2 * WINDOW

    def body(p_ref, c_ref, u_ref, d_ref, q_ref, k_ref, v_ref):
        c, u, d = c_ref[...], u_ref[...], d_ref[...]

        def rope(x):
            return (x * c + pltpu.roll(x, half, 1) * u + pltpu.roll(x, LANE - half, 1) * d).astype(BF16)

        heads_per_slab = LANE // ATT_HEAD_DIM
        for s in range(ATT_WIDTH // LANE):
            slab = rope(p_ref[:, s * LANE:(s + 1) * LANE])
            for part in range(heads_per_slab):
                g, hh = divmod(s * heads_per_slab + part, ATT_GROUP)
                piece = slab[:, part * ATT_HEAD_DIM:(part + 1) * ATT_HEAD_DIM]
                for blk in range(tm // WINDOW):
                    q_ref[blk, g, hh * WINDOW:(hh + 1) * WINDOW, :] = piece[blk * WINDOW:(blk + 1) * WINDOW]
        rk = rope(p_ref[:, ATT_WIDTH:ATT_WIDTH + LANE])
        vv = p_ref[:, ATT_WIDTH + LANE:ATT_COLS].astype(BF16)
        for g in range(ATT_KV_HEADS):
            k_ref[g] = rk[:, g * ATT_HEAD_DIM:(g + 1) * ATT_HEAD_DIM]
            v_ref[g] = vv[:, g * ATT_HEAD_DIM:(g + 1) * ATT_HEAD_DIM]

    tab = pl.BlockSpec((tm, LANE), lambda b, i: (i, 0))
    kv_spec = pl.BlockSpec((None, ATT_KV_HEADS, tm, ATT_HEAD_DIM), lambda b, i: (b, 0, i, 0))
    kv_shape = _sds((bsz, ATT_KV_HEADS, seq, ATT_HEAD_DIM), BF16)
    return pl.pallas_call(
        body, name="rope_fwd", grid=(bsz, seq // tm),
        in_specs=[_tok_spec(tm, ATT_COLS), tab, tab, tab],
        out_specs=[pl.BlockSpec((None, tm // WINDOW, ATT_KV_HEADS, GROUP_ROWS, ATT_HEAD_DIM), lambda b, i: (b, i, 0, 0, 0)),
                   kv_spec, kv_spec],
        out_shape=[_sds((bsz, nblk, ATT_KV_HEADS, GROUP_ROWS, ATT_HEAD_DIM), BF16), kv_shape, kv_shape],
        compiler_params=_params("parallel", "parallel"),
    )(proj, *tables)


ATT_BPS = 2


def _band_mask(has_prev):
    row = lax.broadcasted_iota(jnp.int32, (GROUP_ROWS, 2 * WINDOW), 0) % WINDOW
    col = lax.broadcasted_iota(jnp.int32, (GROUP_ROWS, 2 * WINDOW), 1)
    prev = jnp.logical_and(jnp.logical_and(col < WINDOW, col > row), has_prev)
    return jnp.logical_or(prev, jnp.logical_and(col >= WINDOW, col - WINDOW <= row))


def _sink_column(sink_ref, g):
    head = lax.broadcasted_iota(jnp.int32, (GROUP_ROWS, 1), 0) // WINDOW
    col = jnp.full((GROUP_ROWS, 1), sink_ref[0, g * ATT_GROUP], F32)
    for hh in range(1, ATT_GROUP):
        col = jnp.where(head == hh, sink_ref[0, g * ATT_GROUP + hh], col)
    return col


def _attn_specs():
    q_spec = pl.BlockSpec((None, ATT_BPS, ATT_KV_HEADS, GROUP_ROWS, ATT_HEAD_DIM), lambda b, i: (b, i, 0, 0, 0))
    kv_cur = pl.BlockSpec((None, ATT_KV_HEADS, ATT_BPS * WINDOW, ATT_HEAD_DIM), lambda b, i: (b, 0, i, 0))
    kv_prev = pl.BlockSpec((None, ATT_KV_HEADS, WINDOW, ATT_HEAD_DIM), lambda b, i: (b, 0, jnp.maximum(ATT_BPS * i - 1, 0), 0))
    return q_spec, kv_cur, kv_prev


def _band(prev_ref, cur_ref, g, blk):
    own = cur_ref[g, blk * WINDOW:(blk + 1) * WINDOW]
    before = prev_ref[g] if blk == 0 else cur_ref[g, (blk - 1) * WINDOW:blk * WINDOW]
    return jnp.concatenate([before, own], axis=0)


def _attn_fwd(qh, kh, vh, sinks, w_norm, comms=()):
    bsz, nblk = qh.shape[0], qh.shape[1]
    seq = nblk * WINDOW
    rows = ATT_BPS * WINDOW
    neg = float(jnp.finfo(jnp.float32).min)

    def body(sink_ref, q_ref, kc_ref, kp_ref, vc_ref, vp_ref, w_ref, raw_ref, an_ref, l_ref):
        for blk in range(ATT_BPS):
            mask = _band_mask(True if blk else pl.program_id(1) > 0)
            for g in range(ATT_KV_HEADS):
                keys, vals = _band(kp_ref, kc_ref, g, blk), _band(vp_ref, vc_ref, g, blk)
                sink = _sink_column(sink_ref, g)
                s = jnp.where(mask, _dot(q_ref[blk, g], keys, NT_DIMS) * ATT_SCALE, neg)
                m = jnp.maximum(jnp.max(s, axis=-1, keepdims=True), sink)
                p = jnp.where(mask, jnp.exp(s - m), 0.0)
                den = jnp.sum(p, axis=-1, keepdims=True) + jnp.exp(sink - m)
                o = _dot(p / den, vals)
                lse = m + jnp.log(den)
                tok = slice(blk * WINDOW, (blk + 1) * WINDOW)
                for hh in range(ATT_GROUP):
                    h = g * ATT_GROUP + hh
                    raw_ref[tok, h * ATT_HEAD_DIM:(h + 1) * ATT_HEAD_DIM] = o[hh * WINDOW:(hh + 1) * WINDOW]
                    l_ref[tok, h:h + 1] = lse[hh * WINDOW:(hh + 1) * WINDOW]
        y, _, _ = _rms_fwd(raw_ref[...], w_ref[...])
        an_ref[...] = y.astype(BF16)

    cur = lambda width: pl.BlockSpec((None, rows, width), lambda b, i: (b, i, 0))
    q_spec, kv_cur, kv_prev = _attn_specs()
    return _call(
        body, (sinks, qh, kh, kh, vh, vh, w_norm), name="attn_fwd", grid=(bsz, nblk // ATT_BPS),
        in_specs=[pl.BlockSpec(memory_space=pltpu.SMEM), q_spec, kv_cur, kv_prev, kv_cur, kv_prev, _vec_spec(ATT_WIDTH)],
        out_specs=[cur(ATT_WIDTH), cur(ATT_WIDTH), cur(ATT_Q_HEADS)],
        out_shape=[_sds((bsz, seq, ATT_WIDTH), F32), _sds((bsz, seq, MIX_WIDTH), BF16), _sds((bsz, seq, ATT_Q_HEADS), F32)],
        sem=("parallel", "parallel"), comms=comms)


HG_Q0 = ATT_COLS // LANE
HG_F0 = HG_Q0 + HG_HEADS
HG_I0 = HG_F0 + HG_HEADS
HG_G0 = HG_I0 + HG_HEADS
HG_TOK = 256
HG_NCH = HG_TOK // HG_CHUNK
HG_HPS = 2


def _block_masks():
    row = lax.broadcasted_iota(jnp.int32, (HG_TOK, HG_TOK), 0)
    col = lax.broadcasted_iota(jnp.int32, (HG_TOK, HG_TOK), 1)
    same = (row // HG_CHUNK) == (col // HG_CHUNK)
    return jnp.logical_and(same, col <= row), jnp.logical_and(same, col >= row)


def _row_in_chunk():
    return lax.broadcasted_iota(jnp.int32, (HG_TOK, LANE), 0) % HG_CHUNK


def _chunk_cumsum(x, reverse=False):
    ric = _row_in_chunk()
    shift = 1
    while shift < HG_CHUNK:
        if reverse:
            x = x + jnp.where(ric < HG_CHUNK - shift, pltpu.roll(x, HG_TOK - shift, 0), 0.0)
        else:
            x = x + jnp.where(ric >= shift, pltpu.roll(x, shift, 0), 0.0)
        shift *= 2
    return x


def _chunk_rows(rows):
    stacked = jnp.concatenate([r[None] for r in rows], axis=0)
    return jnp.broadcast_to(stacked, (HG_NCH, HG_CHUNK, LANE)).reshape(HG_TOK, LANE)


def _chunk_slices(x):
    return [x[j * HG_CHUNK:(j + 1) * HG_CHUNK] for j in range(HG_NCH)]


def _hgrn_common(tbl, hf, hq):
    lb = _sigmoid(tbl[1:2] - tbl[0:1])
    sig = _sigmoid(hf)
    f = lb + (1.0 - lb) * sig
    sq = _sigmoid(hq)
    q, k = hq * sq, 1.0 - f
    b = _chunk_cumsum(jnp.log(f))
    last = [b[(j + 1) * HG_CHUNK - 1:(j + 1) * HG_CHUNK] for j in range(HG_NCH)]
    bl = _chunk_rows(last)
    e_b, e_nb, e_rem = jnp.exp(b), jnp.exp(-b), jnp.exp(bl - b)
    e_last = [jnp.exp(r) for r in last]
    return dict(lb=lb, sig=sig, f=f, sq=sq, q=q, k=k, e_b=e_b, e_nb=e_nb, e_rem=e_rem, e_last=e_last,
                qd=q * e_b, kd=k * e_nb, ku=k * e_rem)


def _hgrn_fwd(proj, lb_table, norm_w, mix_in, comms=()):
    bsz, seq, _ = proj.shape
    nstep = seq // HG_TOK

    def body(tbl_ref, nw_ref, q_ref, f_ref, i_ref, g_ref, mix_ref, o_ref, rec_ref, st_ref, s_scr):
        @pl.when(pl.program_id(2) == 0)
        def _():
            s_scr[...] = jnp.zeros_like(s_scr)

        lower, _ = _block_masks()
        for hp in range(HG_HPS):
            ls = slice(hp * LANE, (hp + 1) * LANE)
            v, hg = i_ref[:, ls], g_ref[:, ls]
            t = _hgrn_common(tbl_ref[:, ls], f_ref[:, ls], q_ref[:, ls])
            a = jnp.where(lower, _dot(t["qd"], t["kd"], NT_DIMS), 0.0)
            o_intra = _dot(a, v)
            v_c, ku_c, qd_c = [_chunk_slices(z.astype(BF16)) for z in (v, t["ku"], t["qd"])]
            updates = [_dot(v_c[j], ku_c[j], TN_DIMS) for j in range(HG_NCH)]
            st = s_scr[hp]
            states = []
            for j in range(HG_NCH):
                states.append(st)
                st = st * t["e_last"][j] + updates[j]
            s_scr[hp] = st
            o = o_intra + jnp.concatenate([_dot(qd_c[j], states[j], NT_DIMS) for j in range(HG_NCH)], axis=0)
            for j in range(HG_NCH):
                st_ref[hp, j] = states[j]
            o_ref[:, ls] = o
            y, _, _ = _rms_fwd(o, nw_ref[...])
            rec_ref[:, ls] = (y * (hg * _sigmoid(hg))).astype(BF16)

    width = HG_HPS * LANE
    slab = lambda first: pl.BlockSpec((None, HG_TOK, width), lambda b, h, t: (b, t, first // HG_HPS + h))
    head_out = pl.BlockSpec((None, HG_TOK, width), lambda b, h, t: (b, t, h))
    mix_out = pl.BlockSpec((None, HG_TOK, width), lambda b, h, t: (b, t, ATT_WIDTH // width + h))
    return _call(
        body, (lb_table, norm_w, proj, proj, proj, proj, mix_in), name="hgrn_fwd", grid=(bsz, HG_HEADS // HG_HPS, nstep),
        in_specs=[pl.BlockSpec((2, width), lambda b, h, t: (0, h)), pl.BlockSpec((1, LANE), lambda b, h, t: (0, 0)),
                  slab(HG_Q0), slab(HG_F0), slab(HG_I0), slab(HG_G0), pl.BlockSpec(memory_space=pl.ANY)],
        out_specs=[head_out, mix_out,
                   pl.BlockSpec((None, HG_HPS, HG_NCH, LANE, LANE), lambda b, h, t: (b, h, t, 0, 0))],
        out_shape=[_sds((bsz, seq, HG_WIDTH), F32), _sds(mix_in.shape, BF16),
                   _sds((bsz, HG_HEADS, seq // HG_CHUNK, LANE, LANE), F32)],
        scratch_shapes=[pltpu.VMEM((HG_HPS, LANE, LANE), F32)],
        sem=("parallel", "parallel", "arbitrary"), comms=comms, aliases={6: 1})


def _out_proj_fused(cat, w_out, x, post_w, g1, pre_w, sc2, sh2):
    tm = 512

    def epi(mix, ex, outs):
        x_ref, pw_ref, g1_ref, w2_ref, sc_ref, sh_ref = ex
        outs[0][...] = mix
        n1, _, _ = _rms_fwd(mix, pw_ref[...])
        x1 = x_ref[...] + g1_ref[...] * n1
        outs[1][...] = x1
        y2, _, _ = _rms_fwd(x1, w2_ref[...])
        outs[2][...] = (y2 * (1.0 + sc_ref[...]) + sh_ref[...]).astype(BF16)

    return _mm_rows(cat, w_out, name="out_proj", tm=tm, extra=(x, post_w, g1, pre_w, sc2, sh2),
                    extra_specs=[_tok_spec(tm), _vec_spec(), _row_spec(), _vec_spec(), _row_spec(), _row_spec()],
                    out_specs=[_tok_spec(tm), _tok_spec(tm), _tok_spec(tm)],
                    out_shape=[_sds(x.shape, F32), _sds(x.shape, F32), _sds(x.shape, BF16)], epi=epi)


def _acc_out(ref, first, value):
    @pl.when(first)
    def _():
        ref[...] = value

    @pl.when(jnp.logical_not(first))
    def _():
        ref[...] += value


def _down_proj_fused(r, w_down, x1, post_w, g2, target):
    tm = 256
    bsz = x1.shape[0]

    def pro(r_ref, ex, outs):
        rv = r_ref[...]
        return rv * rv

    def epi(down, ex, outs):
        x1_ref, w_ref, g2_ref, t_ref = ex
        loss_ref, dy_ref, dd_ref, dg2_ref, dw_ref = outs
        b, i = pl.program_id(0), pl.program_id(1)
        w, g2v = w_ref[...], g2_ref[...]
        n2, dh, rstd = _rms_fwd(down, w)
        err = x1_ref[...] + g2v * n2 - t_ref[...]
        part = (0.5 / D_MODEL) * jnp.sum(jnp.sum(err * err, axis=-1, keepdims=True), axis=0, keepdims=True)
        _acc_out(loss_ref, jnp.logical_and(b == 0, i == 0), jnp.broadcast_to(part, (1, LANE)))
        dy = err * (1.0 / D_MODEL)
        dy_ref[...] = dy
        _acc_out(dg2_ref, i == 0, _colsum(dy * n2))
        dd, dw_rows = _rms_bwd(dy * g2v, dh, rstd, w)
        dd_ref[...] = dd.astype(BF16)
        _acc_out(dw_ref, jnp.logical_and(b == 0, i == 0), _colsum(dw_rows))

    return _mm_rows(r, w_down, name="down_proj", tm=tm, extra=(x1, post_w, g2, target),
                    extra_specs=[_tok_spec(tm), _vec_spec(), _row_spec(), _tok_spec(tm)],
                    out_specs=[_vec_spec(LANE), _tok_spec(tm), _tok_spec(tm), _row_spec(), _vec_spec()],
                    out_shape=[_sds((1, LANE), F32), _sds(x1.shape, F32), _sds(x1.shape, BF16), _sds((bsz, 1, D_MODEL), F32),
                               _sds((1, D_MODEL), F32)], pro=pro, epi=epi)


def _up_bwd_fused(dpre, w_up4, dy, x1, mix, pre_w, sc2, post_w, g1, comms=()):
    tm = 256
    bsz = x1.shape[0]

    def epi(dh2v, ex, outs):
        dy_ref, x1_ref, mix_ref, w2_ref, sc_ref, pw_ref, g1_ref = ex
        dx1_ref, dmix_ref, dsc_ref, dsh_ref, dg1_ref, dw2_ref, dpw_ref = outs
        b, i = pl.program_id(0), pl.program_id(1)
        first = jnp.logical_and(b == 0, i == 0)
        w2, pw = w2_ref[...], pw_ref[...]
        y2, xh2, rstd2 = _rms_fwd(x1_ref[...], w2)
        _acc_out(dsh_ref, i == 0, _colsum(dh2v))
        _acc_out(dsc_ref, i == 0, _colsum(dh2v * y2))
        dx1n, dw_rows = _rms_bwd(dh2v * (1.0 + sc_ref[...]), xh2, rstd2, w2)
        _acc_out(dw2_ref, first, _colsum(dw_rows))
        dx1 = dy_ref[...] + dx1n
        dx1_ref[...] = dx1
        n1, mh, rstd1 = _rms_fwd(mix_ref[...], pw)
        _acc_out(dg1_ref, i == 0, _colsum(dx1 * n1))
        dmix, dpw_rows = _rms_bwd(dx1 * g1_ref[...], mh, rstd1, pw)
        dmix_ref[...] = dmix.astype(BF16)
        _acc_out(dpw_ref, first, _colsum(dpw_rows))

    row_shape = _sds((bsz, 1, D_MODEL), F32)
    vec_shape = _sds((1, D_MODEL), F32)
    return _mm_rows(dpre, w_up4, name="up_bwd", tm=tm, extra=(dy, x1, mix, pre_w, sc2, post_w, g1),
                    extra_specs=[_tok_spec(tm), _tok_spec(tm), _tok_spec(tm), _vec_spec(), _row_spec(), _vec_spec(), _row_spec()],
                    out_specs=[_tok_spec(tm), _tok_spec(tm), _row_spec(), _row_spec(), _row_spec(), _vec_spec(), _vec_spec()],
                    out_shape=[_sds(x1.shape, F32), _sds(x1.shape, BF16), row_shape, row_shape, row_shape, vec_shape, vec_shape],
                    epi=epi, b_chunks=w_up4.shape[0], comms=comms)


def _norm1_bwd(dh1, dx1, x, pre_w, sc1, tm=512):
    bsz, seq, _ = x.shape

    def body(dh_ref, dx1_ref, x_ref, w_ref, sc_ref, gx_ref, dsc_ref, dsh_ref, dw_ref):
        b, i = pl.program_id(0), pl.program_id(1)
        w = w_ref[...]
        dh = dh_ref[...]
        y, xh, rstd = _rms_fwd(x_ref[...], w)
        _acc_out(dsh_ref, i == 0, _colsum(dh))
        _acc_out(dsc_ref, i == 0, _colsum(dh * y))
        dx, dw_rows = _rms_bwd(dh * (1.0 + sc_ref[...]), xh, rstd, w)
        _acc_out(dw_ref, jnp.logical_and(b == 0, i == 0), _colsum(dw_rows))
        gx_ref[...] = dx1_ref[...] + dx

    row_shape = _sds((bsz, 1, D_MODEL), F32)
    return pl.pallas_call(
        body, name="norm1_bwd", grid=(bsz, seq // tm),
        in_specs=[_tok_spec(tm), _tok_spec(tm), _tok_spec(tm), _vec_spec(), _row_spec()],
        out_specs=[_tok_spec(tm), _row_spec(), _row_spec(), _vec_spec()],
        out_shape=[_sds(x.shape, F32), row_shape, row_shape, _sds((1, D_MODEL), F32)],
        compiler_params=_params("arbitrary", "arbitrary"),
    )(dh1, dx1, x, pre_w, sc1)


def _hgrn_bwd(dcat, proj, o_raw, states, lb_table, norm_w, comms=()):
    bsz, seq, _ = proj.shape
    nstep = seq // HG_TOK
    rec0 = ATT_WIDTH // LANE

    def body(tbl_ref, nw_ref, dr_ref, q_ref, f_ref, i_ref, g_ref, o_ref, st_ref,
             dq_ref, df_ref, di_ref, dg_ref, dlb_ref, dnw_ref, ds_scr):
        h, b, t = pl.program_id(0), pl.program_id(1), pl.program_id(2)

        @pl.when(t == 0)
        def _():
            ds_scr[...] = jnp.zeros_like(ds_scr)

        lower, upper = _block_masks()
        dlb_parts = []
        dnw_acc = jnp.zeros((1, LANE), F32)
        for hp in range(HG_HPS):
            ls = slice(hp * LANE, (hp + 1) * LANE)
            hq, v, hg = q_ref[:, ls], i_ref[:, ls], g_ref[:, ls]
            nw = nw_ref[...]
            c = _hgrn_common(tbl_ref[:, ls], f_ref[:, ls], hq)
            qd, kd, ku = c["qd"], c["kd"], c["ku"]
            y, on, rstd = _rms_fwd(o_ref[:, ls], nw)
            sg = _sigmoid(hg)
            dr = dr_ref[:, ls]
            dg_ref[:, ls] = (dr * y * (sg * (1.0 + hg * (1.0 - sg)))).astype(BF16)
            do, dnw_rows = _rms_bwd(dr * (hg * sg), on, rstd, nw)
            at = jnp.where(upper, _dot(kd, qd, NT_DIMS), 0.0)
            da = jnp.where(lower, _dot(do, v, NT_DIMS), 0.0)
            dat = jnp.where(upper, _dot(v, do, NT_DIMS), 0.0)
            dv = _dot(at, do)
            dqd = _dot(da, kd)
            dkd = _dot(dat, qd)
            do_c, qd_c, v_c, ku_c = [_chunk_slices(z.astype(BF16)) for z in (do, qd, v, ku)]
            outer = [_dot(do_c[j], qd_c[j], TN_DIMS) for j in range(HG_NCH)]
            ds = ds_scr[hp]
            ds_after = [None] * HG_NCH
            for j in reversed(range(HG_NCH)):
                ds_after[j] = ds
                ds = outer[j] + ds * c["e_last"][j]
            ds_scr[hp] = ds
            states = [st_ref[hp, j] for j in range(HG_NCH)]
            dv = dv + jnp.concatenate([_dot(ku_c[j], ds_after[j], NT_DIMS) for j in range(HG_NCH)], axis=0)
            dqd = dqd + jnp.concatenate([_dot(do_c[j], states[j]) for j in range(HG_NCH)], axis=0)
            dku = jnp.concatenate([_dot(v_c[j], ds_after[j]) for j in range(HG_NCH)], axis=0)
            dku_ku = dku * ku
            dbl = [_colsum(states[j] * ds_after[j]) * c["e_last"][j] + _colsum(dku_ku[j * HG_CHUNK:(j + 1) * HG_CHUNK])
                   for j in range(HG_NCH)]
            dk = dkd * c["e_nb"] + dku * c["e_rem"]
            db = dqd * qd - dkd * kd - dku_ku + jnp.where(_row_in_chunk() == HG_CHUNK - 1, _chunk_rows(dbl), 0.0)
            dfv = _chunk_cumsum(db, reverse=True) / c["f"] - dk
            sig, sq = c["sig"], c["sq"]
            df_ref[:, ls] = (dfv * (1.0 - c["lb"]) * sig * (1.0 - sig)).astype(BF16)
            dq_ref[:, ls] = (dqd * c["e_b"] * (sq * (1.0 + hq * (1.0 - sq)))).astype(BF16)
            di_ref[:, ls] = dv.astype(BF16)
            dlb_parts.append(_colsum(dfv * (1.0 - sig)))
            dnw_acc = dnw_acc + _colsum(dnw_rows)
        _acc_out(dlb_ref, jnp.logical_and(b == 0, t == 0), jnp.concatenate(dlb_parts, axis=1))
        _acc_out(dnw_ref, jnp.logical_and(h == 0, jnp.logical_and(b == 0, t == 0)), dnw_acc)

    rev = lambda t: nstep - 1 - t
    width = HG_HPS * LANE
    slab = lambda first: pl.BlockSpec((None, HG_TOK, width), lambda h, b, t: (b, rev(t), first // HG_HPS + h))
    head = pl.BlockSpec((None, HG_TOK, width), lambda h, b, t: (b, rev(t), h))
    grad_shape = _sds((bsz, seq, HG_WIDTH), BF16)
    return _call(
        body, (lb_table, norm_w, dcat, proj, proj, proj, proj, o_raw, states), name="hgrn_bwd",
        grid=(HG_HEADS // HG_HPS, bsz, nstep),
        in_specs=[pl.BlockSpec((2, width), lambda h, b, t: (0, h)), pl.BlockSpec((1, LANE), lambda h, b, t: (0, 0)),
                  slab(rec0), slab(HG_Q0), slab(HG_F0), slab(HG_I0), slab(HG_G0), head,
                  pl.BlockSpec((None, HG_HPS, HG_NCH, LANE, LANE), lambda h, b, t: (b, h, rev(t), 0, 0))],
        out_specs=[head, head, head, head, pl.BlockSpec((1, width), lambda h, b, t: (0, h)),
                   pl.BlockSpec((1, LANE), lambda h, b, t: (0, 0))],
        out_shape=[grad_shape, grad_shape, grad_shape, grad_shape, _sds((1, HG_WIDTH), F32), _sds((1, LANE), F32)],
        scratch_shapes=[pltpu.VMEM((HG_HPS, LANE, LANE), F32)],
        sem=("arbitrary", "arbitrary", "arbitrary"), comms=comms)


def _attn_bwd(dcat, raw, w_norm, qh, kh, vh, lse, sinks, comms=()):
    bsz, nblk = qh.shape[0], qh.shape[1]
    seq = nblk * WINDOW

    def body(sink_ref, da_ref, raw_ref, w_ref, q_ref, kc_ref, kp_ref, vc_ref, vp_ref, l_ref,
             dq_ref, dkd_ref, dkp_ref, dvd_ref, dvp_ref, dw_ref, dsink_ref):
        b, i = pl.program_id(0), pl.program_id(1)
        first = jnp.logical_and(b == 0, i == 0)
        w = w_ref[...]
        _, on, rstd = _rms_fwd(raw_ref[...], w)
        do_step, dw_rows = _rms_bwd(da_ref[...], on, rstd, w)
        _acc_out(dw_ref, first, _colsum(dw_rows))
        lane8 = lax.broadcasted_iota(jnp.int32, (1, ATT_Q_HEADS), 1)
        dsink = jnp.zeros((1, ATT_Q_HEADS), F32)
        for blk in range(ATT_BPS):
            tok = slice(blk * WINDOW, (blk + 1) * WINDOW)
            mask = _band_mask(True if blk else i > 0)
            raw_v, do_all = raw_ref[tok, :], do_step[tok]
            for g in range(ATT_KV_HEADS):
                gs = slice(g * ATT_HEAD_DIM, (g + 1) * ATT_HEAD_DIM)
                heads = [slice((g * ATT_GROUP + hh) * ATT_HEAD_DIM, (g * ATT_GROUP + hh + 1) * ATT_HEAD_DIM)
                         for hh in range(ATT_GROUP)]
                q = q_ref[blk, g]
                keys, vals = _band(kp_ref, kc_ref, g, blk), _band(vp_ref, vc_ref, g, blk)
                do_g = jnp.concatenate([do_all[:, hs] for hs in heads], axis=0)
                dsum = jnp.concatenate([jnp.sum(do_all[:, hs] * raw_v[:, hs], axis=-1, keepdims=True) for hs in heads], axis=0)
                lse_g = jnp.concatenate([l_ref[tok, g * ATT_GROUP + hh:g * ATT_GROUP + hh + 1] for hh in range(ATT_GROUP)], axis=0)
                p = jnp.where(mask, jnp.exp(_dot(q, keys, NT_DIMS) * ATT_SCALE - lse_g), 0.0)
                sink_part = jnp.exp(_sink_column(sink_ref, g) - lse_g) * dsum
                for hh in range(ATT_GROUP):
                    head_sum = jnp.sum(sink_part[hh * WINDOW:(hh + 1) * WINDOW], axis=0, keepdims=True)
                    dsink = dsink - jnp.where(lane8 == g * ATT_GROUP + hh, head_sum, 0.0)
                ds = p * (_dot(do_g, vals, NT_DIMS) - dsum) * ATT_SCALE
                dq_g = _dot(ds, keys)
                for hh, hs in enumerate(heads):
                    dq_ref[tok, hs] = dq_g[hh * WINDOW:(hh + 1) * WINDOW]
                dk_g = _dot(ds, q, TN_DIMS)
                dv_g = _dot(p, do_g, TN_DIMS)
                dkp_ref[tok, gs], dkd_ref[tok, gs] = dk_g[:WINDOW], dk_g[WINDOW:]
                dvp_ref[tok, gs], dvd_ref[tok, gs] = dv_g[:WINDOW], dv_g[WINDOW:]
        _acc_out(dsink_ref, first, dsink)

    rows = ATT_BPS * WINDOW
    cur = lambda width: pl.BlockSpec((None, rows, width), lambda b, i: (b, i, 0))
    q_spec, kv_cur, kv_prev = _attn_specs()
    kv_shape = _sds((bsz, seq, LANE), F32)
    return _call(
        body, (sinks, dcat, raw, w_norm, qh, kh, kh, vh, vh, lse), name="attn_bwd", grid=(bsz, nblk // ATT_BPS),
        in_specs=[pl.BlockSpec(memory_space=pltpu.SMEM), cur(ATT_WIDTH), cur(ATT_WIDTH), _vec_spec(ATT_WIDTH), q_spec,
                  kv_cur, kv_prev, kv_cur, kv_prev, cur(ATT_Q_HEADS)],
        out_specs=[cur(ATT_WIDTH), cur(LANE), cur(LANE), cur(LANE), cur(LANE), _vec_spec(ATT_WIDTH), _vec_spec(ATT_Q_HEADS)],
        out_shape=[_sds((bsz, seq, ATT_WIDTH), F32), kv_shape, kv_shape, kv_shape, kv_shape, _sds((1, ATT_WIDTH), F32),
                   _sds((1, ATT_Q_HEADS), F32)],
        sem=("arbitrary", "arbitrary"), comms=comms)


def _rope_bwd(dq, dkd, dkp, dvd, dvp, tables):
    bsz, seq, _ = dq.shape
    nblk = seq // WINDOW
    half = ROPE_DIM // 2

    def body(dq_ref, dkd_ref, dkp_ref, dvd_ref, dvp_ref, c_ref, u_ref, d_ref, o_ref):
        c, u, d = c_ref[...], u_ref[...], d_ref[...]
        has_next = pl.program_id(1) < nblk - 1

        def unrope(g):
            return g * c + pltpu.roll(g * u, LANE - half, 1) + pltpu.roll(g * d, half, 1)

        for s in range(ATT_WIDTH // LANE):
            o_ref[:, s * LANE:(s + 1) * LANE] = unrope(dq_ref[:, s * LANE:(s + 1) * LANE]).astype(BF16)
        dk = dkd_ref[...] + jnp.where(has_next, dkp_ref[...], 0.0)
        o_ref[:, ATT_WIDTH:ATT_WIDTH + LANE] = unrope(dk).astype(BF16)
        o_ref[:, ATT_WIDTH + LANE:ATT_COLS] = (dvd_ref[...] + jnp.where(has_next, dvp_ref[...], 0.0)).astype(BF16)

    cur = lambda width: pl.BlockSpec((None, WINDOW, width), lambda b, i: (b, i, 0))
    nxt = pl.BlockSpec((None, WINDOW, LANE), lambda b, i: (b, jnp.minimum(i + 1, nblk - 1), 0))
    tab = pl.BlockSpec((WINDOW, LANE), lambda b, i: (i, 0))
    return pl.pallas_call(
        body, name="rope_bwd", grid=(bsz, nblk),
        in_specs=[cur(ATT_WIDTH), cur(LANE), nxt, cur(LANE), nxt, tab, tab, tab],
        out_specs=cur(ATT_COLS), out_shape=_sds((bsz, seq, ATT_COLS), BF16),
        compiler_params=_params("parallel", "parallel"),
    )(dq, dkd, dkp, dvd, dvp, *tables)


def _other_chips(x, y):
    return [(1 - x, y), (x, 1 - y), (1 - x, 1 - y)]


def _sem_pair(n):
    return [pltpu.SemaphoreType.DMA((n,)), pltpu.SemaphoreType.DMA((n,))]


def _rows_of(ref, rows):
    return ref if rows is None else ref.at[pl.ds(rows[0], rows[1])]


def _plan_chip_gather(blocks, bufs, rows=None, forward_rows=None):
    n = len(blocks)

    def copies(ins, outs, sems):
        x, y, c = _mesh_pos()
        sends, lands = [], []
        for a in range(n):
            for j, chip in enumerate(_other_chips(x, y)):
                k = 3 * a + j
                sends.append(pltpu.make_async_remote_copy(
                    src_ref=_rows_of(ins[a], rows), dst_ref=_rows_of(outs[a].at[4 * x + 2 * y + c], rows), send_sem=sems[0].at[k],
                    recv_sem=sems[1].at[k], device_id=(*chip, c), device_id_type=MESH))
                slot = _rows_of(outs[a].at[4 * chip[0] + 2 * chip[1] + c], rows)
                lands.append(pltpu.make_async_remote_copy(
                    src_ref=slot, dst_ref=slot, send_sem=sems[0].at[k], recv_sem=sems[1].at[k],
                    device_id=(*chip, c), device_id_type=MESH))
                if forward_rows is not None:
                    k = 3 * (n + a) + j
                    mine = _rows_of(outs[a].at[4 * chip[0] + 2 * chip[1] + c], forward_rows)
                    sends.append(pltpu.make_async_remote_copy(
                        src_ref=mine, dst_ref=mine, send_sem=sems[0].at[k], recv_sem=sems[1].at[k],
                        device_id=(x, y, 1 - c), device_id_type=MESH))
                    theirs = _rows_of(outs[a].at[4 * chip[0] + 2 * chip[1] + 1 - c], forward_rows)
                    lands.append(pltpu.make_async_remote_copy(
                        src_ref=theirs, dst_ref=theirs, send_sem=sems[0].at[k], recv_sem=sems[1].at[k],
                        device_id=(x, y, 1 - c), device_id_type=MESH))
        return sends, lands

    def start(ins, outs, sems):
        for cp in copies(ins, outs, sems)[0]:
            cp.start()

    def finish(ins, outs, sems):
        sends, lands = copies(ins, outs, sems)
        for cp in lands:
            cp.wait_recv()
        for cp in sends:
            cp.wait_send()

    n_sems = 3 * n * (2 if forward_rows is not None else 1)
    return _Comm(list(blocks) + list(bufs), [_sds(b.shape, b.dtype) for b in bufs], _sem_pair(n_sems), start, finish,
                 aliases=[(n + a, a) for a in range(n)])


def _plan_pair_forward(bufs, rows=None):
    n = len(bufs)

    def copies(outs, sems):
        x, y, c = _mesh_pos()
        sends, lands = [], []
        for a in range(n):
            for j, chip in enumerate(_other_chips(x, y)):
                k = 3 * a + j
                slot = _rows_of(outs[a].at[4 * chip[0] + 2 * chip[1] + c], rows)
                sends.append(pltpu.make_async_remote_copy(
                    src_ref=slot, dst_ref=slot, send_sem=sems[0].at[k], recv_sem=sems[1].at[k],
                    device_id=(x, y, 1 - c), device_id_type=MESH))
                theirs = _rows_of(outs[a].at[4 * chip[0] + 2 * chip[1] + 1 - c], rows)
                lands.append(pltpu.make_async_remote_copy(
                    src_ref=theirs, dst_ref=theirs, send_sem=sems[0].at[k], recv_sem=sems[1].at[k],
                    device_id=(x, y, 1 - c), device_id_type=MESH))
        return sends, lands

    def start(ins, outs, sems):
        for cp in copies(outs, sems)[0]:
            cp.start()

    def finish(ins, outs, sems):
        sends, lands = copies(outs, sems)
        for cp in lands:
            cp.wait_recv()
        for cp in sends:
            cp.wait_send()

    return _Comm(list(bufs), [_sds(b.shape, b.dtype) for b in bufs], _sem_pair(3 * n), start, finish,
                 aliases=[(a, a) for a in range(n)])


def _plan_pair(arrays, other_half):
    n = len(arrays)
    per = N_CHIPS if other_half == "chip_major" else 1

    def copies(ins, outs, sems):
        x, y, c = _mesh_pos()
        out = []
        for a in range(n):
            for k in range(per):
                if other_half == "chip_major":
                    src, dst = ins[a].at[k, 1 - c], outs[a].at[k]
                else:
                    src, dst = (ins[a].at[1 - c] if other_half else ins[a]), outs[a]
                out.append(pltpu.make_async_remote_copy(
                    src_ref=src, dst_ref=dst, send_sem=sems[0].at[per * a + k], recv_sem=sems[1].at[per * a + k],
                    device_id=(x, y, 1 - c), device_id_type=MESH))
        return out

    def start(ins, outs, sems):
        for cp in copies(ins, outs, sems):
            cp.start()

    def finish(ins, outs, sems):
        for cp in copies(ins, outs, sems):
            cp.wait()

    if other_half == "chip_major":
        shapes = [_sds((a.shape[0],) + a.shape[2:], a.dtype) for a in arrays]
    else:
        shapes = [_sds(a.shape[1:] if other_half else a.shape, a.dtype) for a in arrays]
    return _Comm(list(arrays), shapes, _sem_pair(per * n), start, finish)


def _plan_chip_exchange(arrays):
    n = len(arrays)

    def copies(ins, outs, sems):
        x, y, c = _mesh_pos()
        sends, lands = [], []
        for a in range(n):
            for j, chip in enumerate(_other_chips(x, y)):
                k = 3 * a + j
                sends.append(pltpu.make_async_remote_copy(
                    src_ref=ins[a].at[2 * chip[0] + chip[1]], dst_ref=outs[a].at[2 * x + y], send_sem=sems[0].at[k],
                    recv_sem=sems[1].at[k], device_id=(*chip, c), device_id_type=MESH))
                slot = outs[a].at[2 * chip[0] + chip[1]]
                lands.append(pltpu.make_async_remote_copy(
                    src_ref=slot, dst_ref=slot, send_sem=sems[0].at[k], recv_sem=sems[1].at[k],
                    device_id=(*chip, c), device_id_type=MESH))
        return sends, lands

    def start(ins, outs, sems):
        for cp in copies(ins, outs, sems)[0]:
            cp.start()

    def finish(ins, outs, sems):
        sends, lands = copies(ins, outs, sems)
        for cp in lands:
            cp.wait_recv()
        for cp in sends:
            cp.wait_send()

    return _Comm(list(arrays), [_sds(a.shape, a.dtype) for a in arrays], _sem_pair(3 * n), start, finish)


def _comm_only(comms, name):
    return _call(lambda: None, (), name=name, grid=(), in_specs=[], out_specs=[], out_shape=[], sem=(), comms=comms)[1]


def _allgather8(arrays, name):
    return _comm_only([_plan_allgather8(arrays)], name)[0]


def _plan_allgather8(arrays):
    n = len(arrays)

    def parts(ins, outs, sems):
        send_sems, recv_sems, local_sems = sems
        x, y, c = _mesh_pos()
        me, sibling = (x, y, c), (x, y, 1 - c)
        chips = _other_chips(x, y)

        def copy(a, k, block, to, src=None):
            dst = outs[a].at[4 * block[0] + 2 * block[1] + block[2]]
            return pltpu.make_async_remote_copy(
                src_ref=dst if src is None else src, dst_ref=dst, send_sem=send_sems.at[7 * a + k],
                recv_sem=recv_sems.at[7 * a + k], device_id=to, device_id_type=MESH)

        mine = [pltpu.make_async_copy(ins[a], outs[a].at[4 * x + 2 * y + c], local_sems.at[a]) for a in range(n)]
        first = []
        for a in range(n):
            first.append(copy(a, 0, me, sibling, src=ins[a]))
            first += [copy(a, 1 + j, me, (*chip, c), src=ins[a]) for j, chip in enumerate(chips)]
        return copy, mine, first, me, sibling, chips, c

    def start(ins, outs, sems):
        _, mine, first, *_ = parts(ins, outs, sems)
        for cp in mine + first:
            cp.start()

    def finish(ins, outs, sems):
        copy, mine, first, me, sibling, chips, c = parts(ins, outs, sems)
        passed = []
        for j, chip in enumerate(chips):
            for a in range(n):
                copy(a, 1 + j, (*chip, c), me).wait_recv()
                fwd = copy(a, 4 + j, (*chip, c), sibling)
                fwd.start()
                passed.append(fwd)
        for a in range(n):
            copy(a, 0, sibling, me).wait_recv()
            for j, chip in enumerate(chips):
                copy(a, 4 + j, (*chip, 1 - c), me).wait_recv()
        for cp in first + passed:
            cp.wait_send()
        for cp in mine:
            cp.wait()

    sems = [pltpu.SemaphoreType.DMA((7 * n,)), pltpu.SemaphoreType.DMA((7 * n,)), pltpu.SemaphoreType.DMA((n,))]
    return _Comm(list(arrays), [_sds((N_DEV,) + a.shape, a.dtype) for a in arrays], sems, start, finish)


def _pair_sum(g, q, core, name, chip_major=False):
    rows, cols = g.shape[2:]
    tr = _row_tile(rows)

    def body(core_ref, g_ref, q_ref, o_ref):
        o_ref[...] = (g_ref[...] + q_ref[...]).astype(BF16)

    blk = pl.BlockSpec((None, tr, cols), lambda k, i, core_ref: (k, i, 0))
    if chip_major:
        own = pl.BlockSpec((None, None, tr, cols), lambda k, i, core_ref: (k, core_ref[0], i, 0))
    else:
        own = pl.BlockSpec((None, None, tr, cols), lambda k, i, core_ref: (core_ref[0], k, i, 0))
    return pl.pallas_call(
        body, name=name,
        grid_spec=pltpu.PrefetchScalarGridSpec(num_scalar_prefetch=1, grid=(N_CHIPS, rows // tr), in_specs=[own, blk], out_specs=blk),
        out_shape=_sds((N_CHIPS, rows, cols), BF16), compiler_params=_params("parallel", "parallel"),
    )(core, g, q)


def _sum_chips(own, landed, chip, name):
    _, rows, cols = own.shape
    tr = _row_tile(rows)

    def body(chip_ref, own_ref, a_ref, b_ref, c_ref, o_ref):
        acc = own_ref[...].astype(F32) + a_ref[...].astype(F32)
        o_ref[...] = (acc + b_ref[...].astype(F32)) + c_ref[...].astype(F32)

    blk = lambda flip: pl.BlockSpec((None, tr, cols), lambda i, chip_ref: (jnp.bitwise_xor(chip_ref[0], flip), i, 0))
    return pl.pallas_call(
        body, name=name,
        grid_spec=pltpu.PrefetchScalarGridSpec(num_scalar_prefetch=1, grid=(rows // tr,), in_specs=[blk(0), blk(1), blk(2), blk(3)],
                                               out_specs=pl.BlockSpec((tr, cols), lambda i, chip_ref: (i, 0))),
        out_shape=_sds((rows, cols), F32), compiler_params=_params("parallel"),
    )(chip, own, landed, landed, landed)


SMALL_ROWS = 120
PACK_ROWS = 168


def _rows(a, nrows):
    flat = a.reshape(-1)
    return jnp.pad(flat, (0, nrows * LANE - flat.shape[0])).reshape(nrows, LANE)


def _pack_small(b_ada, pre_w_mix, post_w_mix, pre_w_mlp, post_w_mlp, attn_out_w, hg_norm_w, attn_sinks, lb_table):
    return jnp.concatenate([
        _rows(b_ada, 48), _rows(pre_w_mix, 8), _rows(post_w_mix, 8), _rows(pre_w_mlp, 8), _rows(post_w_mlp, 8),
        _rows(attn_out_w, 8), _rows(hg_norm_w, 8), _rows(attn_sinks, 8), _rows(lb_table[0], 8), _rows(lb_table[1], 8)], axis=0)


def _unpack_small(p):
    vec = lambda lo, n: p[lo:lo + n // LANE].reshape(1, n)
    lb = jnp.stack([p[104:108].reshape(HG_WIDTH), p[112:116].reshape(HG_WIDTH)])
    return dict(b_ada=vec(0, N_MOD * D_MODEL), pre_w_mix=vec(48, D_MODEL), post_w_mix=vec(56, D_MODEL), pre_w_mlp=vec(64, D_MODEL),
                post_w_mlp=vec(72, D_MODEL), attn_out_w=vec(80, ATT_WIDTH), hg_norm_w=p[88:89], attn_sinks=p[96:97, :ATT_Q_HEADS],
                lb_table=lb)


def _small_update(packs, w, m, v):
    def body(p_ref, w_ref, m_ref, v_ref, g_ref, dl_ref, nm_ref, nv_ref, loss_ref):
        tot = p_ref[0]
        for d in range(1, N_DEV):
            tot = tot + p_ref[d]
        wv = w_ref[...]
        p1 = _sigmoid(wv[112:120] - wv[104:112])
        s = tot[152:160] * p1 * (1.0 - p1)
        g = jnp.concatenate([tot[0:48] + tot[48:96], tot[96:152], -s, s], axis=0)
        g_ref[...] = g
        dl_ref[...], nm_ref[...], nv_ref[...] = _adamw_math(g, wv, m_ref[...], v_ref[...])
        loss_ref[...] = tot[160:168]

    shp = _sds((SMALL_ROWS, LANE), F32)
    return pl.pallas_call(body, name="small_update", out_shape=[shp] * 4 + [_sds((8, LANE), F32)],
                          compiler_params=_params())(packs, w, m, v)


def kernel(x, c, w_ada, b_ada, pre_w_mix, w_in, attn_sinks, attn_out_w, lb_table, hg_norm_w, w_out, post_w_mix, pre_w_mlp, w_up, w_down, post_w_mlp, loss_target, m_w_ada, m_b_ada, m_pre_w_mix, m_w_in, m_attn_sinks, m_attn_out_w, m_lb_table, m_hg_norm_w, m_w_out, m_post_w_mix, m_pre_w_mlp, m_w_up, m_w_down, m_post_w_mlp, v_w_ada, v_b_ada, v_pre_w_mix, v_w_in, v_attn_sinks, v_attn_out_w, v_lb_table, v_hg_norm_w, v_w_out, v_post_w_mix, v_pre_w_mlp, v_w_up, v_w_down, v_post_w_mlp):
    xi, yi, ci = _mesh_pos()
    chip = 2 * xi + yi
    dev = 2 * chip + ci
    bsz, seq, _ = x.shape
    ntok = bsz * seq
    ada_cols = w_ada.shape[2]
    core = jnp.reshape(ci, (1,)).astype(jnp.int32)
    chip_idx = jnp.reshape(chip, (1,)).astype(jnp.int32)
    flat = lambda a: a.reshape(ntok, a.shape[-1])
    unflat = lambda a: a.reshape(bsz, seq, a.shape[-1])
    tables = _rope_tables(seq)

    def row_half(w):
        rows = w.shape[1] // 2
        return lax.dynamic_slice_in_dim(w[0], ci * rows, rows, axis=0).astype(BF16)

    def gather_buffer(w):
        rows, cols = w.shape[1] // 2, w.shape[2]
        own = w[0].astype(BF16).reshape(2, rows, cols)
        return lax.dynamic_update_slice(jnp.zeros((N_DEV, rows, cols), BF16), own, (2 * chip, 0, 0))

    w_in_t, m_in_t, v_in_t = [jnp.transpose(a[0])[None] for a in (w_in, m_w_in, v_w_in)]
    c_g, in_g = _allgather8([c, row_half(w_in_t)], "gather_first")
    c_all = c_g.reshape(N_DEV * bsz, D_MODEL)
    w_in_full = in_g.reshape(IN_COLS, D_MODEL)

    b_cols = lax.dynamic_slice_in_dim(b_ada, chip * ada_cols, ada_cols, axis=1)
    mod_part = _ada_fwd(c_all, w_ada[0], b_cols)
    half_rows = mod_part.shape[0] // 2
    (mod_g,) = _allgather8([lax.dynamic_slice_in_dim(mod_part, ci * half_rows, half_rows, axis=0)], "gather_mod")
    mod_all = mod_g.reshape(N_CHIPS, 2, half_rows, ada_cols).transpose(1, 2, 0, 3).reshape(N_DEV * bsz, N_MOD * D_MODEL)
    mod = lax.dynamic_slice_in_dim(mod_all, dev * bsz, bsz, axis=0)
    sh1, sc1, g1, sh2, sc2, g2 = [mod[:, i * D_MODEL:(i + 1) * D_MODEL].reshape(bsz, 1, D_MODEL) for i in range(N_MOD)]

    up_rows = w_up.shape[1] // 4
    first_half, second_half = (0, up_rows), (up_rows, up_rows)
    (h1, proj), ((out_g,), (up_g,)) = _in_proj_fused(
        x, pre_w_mix, sc1, sh1, w_in_full,
        comms=[_plan_chip_gather([row_half(w_out)], [gather_buffer(w_out)]),
               _plan_chip_gather([row_half(w_up)], [gather_buffer(w_up)], rows=first_half)])
    qh, kh, vh = _rope_fwd(proj, tables)
    (attn_raw, cat, lse), ((up_g,), (out_g,)) = _attn_fwd(
        qh, kh, vh, attn_sinks, attn_out_w,
        comms=[_plan_chip_gather([row_half(w_up)], [up_g], rows=second_half, forward_rows=first_half), _plan_pair_forward([out_g])])
    (o_raw, cat, states), ((down_g,), (up_g,)) = _hgrn_fwd(
        proj, lb_table, hg_norm_w, cat,
        comms=[_plan_chip_gather([row_half(w_down)], [gather_buffer(w_down)]), _plan_pair_forward([up_g], rows=second_half)])
    w_out_full = out_g.reshape(D_MODEL, D_MODEL)
    w_up4 = up_g.reshape(N_CHIPS, D_MODEL, D_MODEL)
    mix, x1, h2 = _out_proj_fused(cat, w_out_full, x, post_w_mix, g1, pre_w_mlp, sc2, sh2)
    big_tm = min(ntok, 2048)
    up_spec = pl.BlockSpec((None, D_MODEL, D_MODEL), lambda i, j: (j, 0, 0))
    r, ((down_g,),) = _mm(flat(h2), w_up4, name="up_proj", out_dtype=BF16, tm=big_tm, tn=D_MODEL, n_out=D_FF, b_spec=up_spec,
                          epi=lambda acc: jnp.maximum(acc, 0.0), comms=[_plan_pair_forward([down_g])])
    w_down_full = down_g.reshape(D_FF, D_MODEL)
    square = lambda t: t * t
    loss_row, dy, dd, dg2, d_post_mlp = _down_proj_fused(unflat(r), w_down_full, x1, post_w_mlp, g2, loss_target)

    dpre = _mm(flat(dd), w_down_full, name="down_bwd", out_dtype=BF16, trans_b=True, tm=big_tm, tn=D_MODEL, extra=(r,),
               epi=lambda acc, rt: acc * (2.0 * rt.astype(F32)))
    half_rows = D_MODEL // 2
    g_down = _mm_tn(r, flat(dd), name="down_wgrad", tk=half_rows, tn=D_MODEL, a_fn=square,
                    out_shape=_sds((2, N_CHIPS, half_rows, D_MODEL), F32),
                    out_spec=pl.BlockSpec((None, None, half_rows, D_MODEL), lambda i, j: (i % 2, i // 2, 0, 0)))
    (dx1, dmix, dsc2, dsh2, dg1, d_pre_mlp, d_post_mix), ((q_down,),) = _up_bwd_fused(
        unflat(dpre), w_up4, dy, x1, mix, pre_w_mlp, sc2, post_w_mix, g1, comms=[_plan_pair([g_down], True)])
    g_up = _mm_tn(flat(h2), dpre, name="up_wgrad", tk=D_MODEL, tn=half_rows,
                  out_shape=_sds((2, N_CHIPS, half_rows, D_MODEL), F32),
                  out_spec=pl.BlockSpec((2, None, half_rows, half_rows), lambda i, j: (0, j // 2, 0, j % 2)))
    s_down = _pair_sum(g_down, q_down, core, "pair_sum_down")

    dcat, ((q_up,),) = _mm(flat(dmix), w_out_full, name="out_bwd", out_dtype=F32, trans_b=True, comms=[_plan_pair([g_up], True)])
    dcat = unflat(dcat)
    s_up = _pair_sum(g_up, q_up, core, "pair_sum_up")
    out_rows = D_MODEL // N_CHIPS
    g_out = _mm_tn(flat(cat), flat(dmix), name="out_wgrad", tk=out_rows, tn=half_rows,
                   out_shape=_sds((2, N_CHIPS, out_rows, half_rows), F32),
                   out_spec=pl.BlockSpec((None, None, out_rows, half_rows), lambda i, j: (j, i, 0, 0)))
    (dhq, dhf, dhi, dhg, d_lb, d_hg_norm), ((x_down,), (q_out,)) = _hgrn_bwd(
        dcat, proj, o_raw, states, lb_table, hg_norm_w, comms=[_plan_chip_exchange([s_down]), _plan_pair([g_out], True)])
    half_down = _sum_chips(s_down, x_down, chip_idx, "sum_chips_down")
    s_out = _pair_sum(g_out, q_out, core, "pair_sum_out")
    (dq, dkd, dkp, dvd, dvp, d_attn_out, d_sinks), ((their_down,), (x_up, x_out)) = _attn_bwd(
        dcat, attn_raw, attn_out_w, qh, kh, vh, lse, attn_sinks,
        comms=[_plan_pair([half_down], False), _plan_chip_exchange([s_up, s_out])])
    half_up = _sum_chips(s_up, x_up, chip_idx, "sum_chips_up")
    half_out = _sum_chips(s_out, x_out, chip_idx, "sum_chips_out")
    dproj_a = _rope_bwd(dq, dkd, dkp, dvd, dvp, tables)
    dproj = flat(jnp.concatenate([dproj_a, dhq, dhf, dhi, dhg], axis=-1))
    in_rows = IN_COLS // N_CHIPS // 2
    g_in = _mm_tn(dproj, flat(h1), name="in_wgrad", tk=2 * LANE, tn=D_MODEL).reshape(N_CHIPS, 2, in_rows, D_MODEL)
    dh1, ((q_in,), (their_up, their_out)) = _mm(
        dproj, w_in_full, name="in_bwd", out_dtype=F32,
        comms=[_plan_pair([g_in], "chip_major"), _plan_pair([half_up, half_out], False)])
    s_in = _pair_sum(g_in, q_in, core, "pair_sum_in", chip_major=True)
    grad_x, dsc1, dsh1, d_pre_mix = _norm1_bwd(unflat(dh1), dx1, x, pre_w_mix, sc1)

    dmod = jnp.concatenate([dsh1, dsc1, dg1, dsh2, dsc2, dg2], axis=-1).reshape(bsz, N_MOD * D_MODEL)
    pack = jnp.concatenate([
        _rows(dmod, 96), _rows(d_pre_mix, 8), _rows(d_post_mix, 8), _rows(d_pre_mlp, 8), _rows(d_post_mlp, 8),
        _rows(d_attn_out, 8), _rows(d_hg_norm, 8), _rows(d_sinks, 8), _rows(d_lb, 8), _rows(loss_row, 8)], axis=0)
    (packs,), (x_in,) = _comm_only([_plan_allgather8([pack]), _plan_chip_exchange([s_in])], "gather_small")
    half_in = _sum_chips(s_in, x_in, chip_idx, "sum_chips_in")
    ((their_in,),) = _comm_only([_plan_pair([half_in], False)], "pair_swap_in")
    small_args = lambda pre: (pre["b_ada"], pre["pre_w_mix"], pre["post_w_mix"], pre["pre_w_mlp"], pre["post_w_mlp"],
                              pre["attn_out_w"], pre["hg_norm_w"], pre["attn_sinks"], pre["lb_table"])
    w_small = dict(b_ada=b_ada, pre_w_mix=pre_w_mix, post_w_mix=post_w_mix, pre_w_mlp=pre_w_mlp, post_w_mlp=post_w_mlp,
                   attn_out_w=attn_out_w, hg_norm_w=hg_norm_w, attn_sinks=attn_sinks, lb_table=lb_table)
    m_small = dict(b_ada=m_b_ada, pre_w_mix=m_pre_w_mix, post_w_mix=m_post_w_mix, pre_w_mlp=m_pre_w_mlp, post_w_mlp=m_post_w_mlp,
                   attn_out_w=m_attn_out_w, hg_norm_w=m_hg_norm_w, attn_sinks=m_attn_sinks, lb_table=m_lb_table)
    v_small = dict(b_ada=v_b_ada, pre_w_mix=v_pre_w_mix, post_w_mix=v_post_w_mix, pre_w_mlp=v_pre_w_mlp, post_w_mlp=v_post_w_mlp,
                   attn_out_w=v_attn_out_w, hg_norm_w=v_hg_norm_w, attn_sinks=v_attn_sinks, lb_table=v_lb_table)
    *small_packed, loss_rows = _small_update(packs, _pack_small(*small_args(w_small)), _pack_small(*small_args(m_small)),
                                             _pack_small(*small_args(v_small)))
    small_out = [_unpack_small(p) for p in small_packed]
    loss = loss_rows[0, 0]

    dmod_all = packs[:, :96, :].reshape(N_DEV * bsz, N_MOD * D_MODEL)
    dmod_cols = lax.dynamic_slice_in_dim(dmod_all, chip * ada_cols, ada_cols, axis=1)
    ada_out = _ada_bwd_adamw(c_all, dmod_cols, w_ada[0], m_w_ada[0], v_w_ada[0])

    big = dict(
        w_in=tuple(jnp.transpose(a) for a in _adamw_halves(half_in, their_in, core, w_in_t[0], m_in_t[0], v_in_t[0], axis=0,
                                                           name="adamw_in")),
        w_up=tuple(_adamw_halves(half_up, their_up, core, w_up[0], m_w_up[0], v_w_up[0], axis=0, name="adamw_up")),
        w_out=tuple(_adamw_halves(half_out, their_out, core, w_out[0], m_w_out[0], v_w_out[0], axis=1, name="adamw_out")),
        w_down=tuple(_adamw_halves(half_down, their_down, core, w_down[0], m_w_down[0], v_w_down[0], axis=0, name="adamw_down")),
        w_ada=tuple(ada_out),
    )
    order = ("w_ada", "b_ada", "pre_w_mix", "w_in", "attn_sinks", "attn_out_w", "lb_table", "hg_norm_w", "w_out", "post_w_mix",
             "pre_w_mlp", "w_up", "w_down", "post_w_mlp")
    outs = [loss, grad_x]
    for kind in range(4):
        for nm in order:
            outs.append(big[nm][kind][None] if nm in big else small_out[kind][nm])
    return tuple(outs)
```

```python
import jax
import jax.numpy as jnp
from jax import lax
from jax.experimental import pallas as pl
from jax.experimental.pallas import tpu as pltpu

F32 = jnp.float32
BF16 = jnp.bfloat16

D_MODEL = 1024
ATT_WIDTH = 512
ATT_HEAD_DIM = 64
ATT_Q_HEADS = 8
ATT_KV_HEADS = 2
ATT_GROUP = ATT_Q_HEADS // ATT_KV_HEADS
ATT_KV_COLS = ATT_KV_HEADS * ATT_HEAD_DIM
WINDOW = 128
ROPE_DIM = 16
ROPE_THETA = 500000.0
HG_WIDTH = 512
MIX_WIDTH = ATT_WIDTH + HG_WIDTH
HG_HEAD_DIM = 128
HG_HEADS = 4
HG_CHUNK = 32
IN_COLS = ATT_WIDTH + 2 * ATT_KV_COLS + 4 * HG_WIDTH
ATT_COLS = ATT_WIDTH + 2 * ATT_KV_COLS
D_FF = 4 * D_MODEL
N_MOD = 6
EPS = 1e-6
ATT_SCALE = ATT_HEAD_DIM ** -0.5

ADAM_LR = 0.001
ADAM_B1 = 0.9
ADAM_B2 = 0.999
ADAM_EPS = 1e-08
ADAM_WD = 0.01
ADAM_STEP = 10

N_CHIPS = 4
N_DEV = 8
LANE = 128
VMEM_LIMIT = 48 * 1024 * 1024
MESH = pl.DeviceIdType.MESH

NT_DIMS = (((1,), (1,)), ((), ()))
TN_DIMS = (((0,), (0,)), ((), ()))


def _sds(shape, dtype):
    return jax.ShapeDtypeStruct(tuple(shape), dtype)


def _params(*sem):
    return pltpu.CompilerParams(dimension_semantics=sem, vmem_limit_bytes=VMEM_LIMIT)


def _sigmoid(x):
    return 1.0 / (1.0 + jnp.exp(-x))


def _dot(a, b, dims=None):
    a, b = a.astype(BF16), b.astype(BF16)
    if dims is None:
        return jnp.dot(a, b, preferred_element_type=F32)
    return lax.dot_general(a, b, dims, preferred_element_type=F32)


def _rms_fwd(x, w):
    rstd = lax.rsqrt(jnp.mean(x * x, axis=-1, keepdims=True) + EPS)
    xh = x * rstd
    return xh * w, xh, rstd


def _rms_bwd(dy, xh, rstd, w):
    dxh = dy * w
    dx = rstd * (dxh - xh * jnp.mean(dxh * xh, axis=-1, keepdims=True))
    return dx, dy * xh


def _colsum(x):
    return jnp.sum(x, axis=0, keepdims=True)


def _row_tile(rows, cap=256):
    return max(t for t in range(16, cap + 1, 16) if rows % t == 0)


HBM_SPEC = pl.BlockSpec(memory_space=pltpu.HBM)


def _mesh_pos():
    return lax.axis_index("x"), lax.axis_index("y"), lax.axis_index("c")


class _Comm:
    def __init__(self, ins, outs, sems, start, finish, aliases=()):
        self.ins, self.outs, self.sems = list(ins), list(outs), list(sems)
        self.start, self.finish, self.aliases = start, finish, tuple(aliases)


def _call(body, args, *, name, grid, in_specs, out_specs, out_shape, sem, scratch_shapes=(), comms=(), aliases=None):
    scratch_shapes = list(scratch_shapes)
    if not comms:
        return pl.pallas_call(body, name=name, grid=grid, in_specs=in_specs, out_specs=out_specs, out_shape=out_shape,
                              input_output_aliases=dict(aliases or {}), scratch_shapes=scratch_shapes,
                              compiler_params=_params(*sem))(*args)
    single = not isinstance(out_shape, (list, tuple))
    out_specs_l = [out_specs] if single else list(out_specs)
    out_shape_l = [out_shape] if single else list(out_shape)
    n_in, n_out, n_scr = len(in_specs), len(out_shape_l), len(scratch_shapes)
    n_ci = [len(cm.ins) for cm in comms]
    n_co = [len(cm.outs) for cm in comms]
    n_cs = [len(cm.sems) for cm in comms]
    aliases = dict(aliases or {})
    for k, cm in enumerate(comms):
        for i, o in cm.aliases:
            aliases[n_in + sum(n_ci[:k]) + i] = n_out + sum(n_co[:k]) + o

    def fused(*refs):
        pos = [0]

        def take(n):
            part = refs[pos[0]:pos[0] + n]
            pos[0] += n
            return part

        ins = take(n_in)
        c_ins = [take(n) for n in n_ci]
        outs = take(n_out)
        c_outs = [take(n) for n in n_co]
        scr = take(n_scr)
        c_sems = [take(n) for n in n_cs]
        first, last = True, True
        for d, size in enumerate(grid):
            first = jnp.logical_and(first, pl.program_id(d) == 0)
            last = jnp.logical_and(last, pl.program_id(d) == size - 1)

        def run(which):
            for cm, ci, co, cs in zip(comms, c_ins, c_outs, c_sems):
                getattr(cm, which)(ci, co, cs)

        if grid:
            pl.when(first)(lambda: run("start"))
        else:
            run("start")
        body(*ins, *outs, *scr)
        if grid:
            pl.when(last)(lambda: run("finish"))
        else:
            run("finish")

    res = pl.pallas_call(
        fused, name=name, grid=grid, in_specs=list(in_specs) + [HBM_SPEC] * sum(n_ci),
        out_specs=out_specs_l + [HBM_SPEC] * sum(n_co), out_shape=out_shape_l + [s for cm in comms for s in cm.outs],
        input_output_aliases=aliases, scratch_shapes=scratch_shapes + [s for cm in comms for s in cm.sems],
        compiler_params=_params(*["arbitrary"] * len(grid)),
    )(*args, *[a for cm in comms for a in cm.ins])
    main = res[:n_out]
    extra, at = [], n_out
    for n in n_co:
        extra.append(list(res[at:at + n]))
        at += n
    return (main[0] if single else list(main)), extra


def _mm(a, b, *, name, out_dtype, trans_b=False, tm=512, tn=None, extra=(), epi=None, b_spec=None, n_out=None, comms=()):
    m_total, k_total = a.shape
    if n_out is None:
        n_out = b.shape[0] if trans_b else b.shape[1]
    tn = n_out if tn is None else tn
    grid = (m_total // tm, n_out // tn)
    dims = NT_DIMS if trans_b else None

    def body(*refs):
        a_ref, b_ref = refs[0], refs[1]
        extra_refs = refs[2:2 + len(extra)]
        o_ref = refs[2 + len(extra)]
        acc = _dot(a_ref[...], b_ref[...], dims)
        if epi is not None:
            acc = epi(acc, *[r[...] for r in extra_refs])
        o_ref[...] = acc.astype(out_dtype)

    if b_spec is None:
        if trans_b:
            b_spec = pl.BlockSpec((tn, k_total), lambda i, j: (j, 0))
        else:
            b_spec = pl.BlockSpec((k_total, tn), lambda i, j: (0, j))
    in_specs = [pl.BlockSpec((tm, k_total), lambda i, j: (i, 0)), b_spec]
    in_specs += [pl.BlockSpec((tm, tn), lambda i, j: (i, j)) for _ in extra]
    return _call(
        body, (a, b, *extra), name=name, grid=grid, in_specs=in_specs,
        out_specs=pl.BlockSpec((tm, tn), lambda i, j: (i, j)),
        out_shape=_sds((m_total, n_out), out_dtype),
        sem=("parallel", "parallel"), comms=comms)


def _mm_tn(a, b, *, name, tk, tn, a_fn=None, out_shape=None, out_spec=None):
    m_total, k_total = a.shape
    n_total = b.shape[1]
    grid = (k_total // tk, n_total // tn)

    def body(a_ref, b_ref, o_ref):
        av = a_ref[...]
        part = _dot(av if a_fn is None else a_fn(av), b_ref[...], TN_DIMS)
        o_ref[...] = part.reshape(o_ref.shape)

    if out_shape is None:
        out_shape = _sds((k_total, n_total), F32)
        out_spec = pl.BlockSpec((tk, tn), lambda i, j: (i, j))
    return pl.pallas_call(
        body, name=name, grid=grid,
        in_specs=[pl.BlockSpec((m_total, tk), lambda i, j: (0, i)), pl.BlockSpec((m_total, tn), lambda i, j: (0, j))],
        out_specs=out_spec, out_shape=out_shape,
        compiler_params=_params("parallel", "parallel"),
    )(a, b)


def _ada_fwd(c_all, w_shard, b_shard):
    nb, ncol = c_all.shape[0], w_shard.shape[1]
    tn = 512

    def body(c_ref, w_ref, b_ref, o_ref):
        c = c_ref[...]
        o_ref[...] = _dot(c * _sigmoid(c), w_ref[...]) + b_ref[...]

    return pl.pallas_call(
        body, name="ada_fwd", grid=(ncol // tn,),
        in_specs=[pl.BlockSpec((nb, D_MODEL), lambda j: (0, 0)), pl.BlockSpec((D_MODEL, tn), lambda j: (0, j)),
                  pl.BlockSpec((1, tn), lambda j: (0, j))],
        out_specs=pl.BlockSpec((nb, tn), lambda j: (0, j)), out_shape=_sds((nb, ncol), F32),
        compiler_params=_params("parallel"),
    )(c_all, w_shard, b_shard)


def _adamw_math(g, w, m, v):
    m = ADAM_B1 * m + (1.0 - ADAM_B1) * g
    v = ADAM_B2 * v + (1.0 - ADAM_B2) * (g * g)
    m_hat = m / (1.0 - ADAM_B1 ** ADAM_STEP)
    v_hat = v / (1.0 - ADAM_B2 ** ADAM_STEP)
    delta = -ADAM_LR * (m_hat / (jnp.sqrt(v_hat) + ADAM_EPS) + ADAM_WD * w)
    return delta, m, v


def _ada_bwd_adamw(c_all, dmod_cols, w, m, v):
    nb, ncol = dmod_cols.shape
    tn = 256

    def body(c_ref, d_ref, w_ref, m_ref, v_ref, g_ref, dl_ref, nm_ref, nv_ref):
        c = c_ref[...]
        g = _dot(c * _sigmoid(c), d_ref[...], TN_DIMS)
        g_ref[...] = g
        dl_ref[...], nm_ref[...], nv_ref[...] = _adamw_math(g, w_ref[...], m_ref[...], v_ref[...])

    col = pl.BlockSpec((D_MODEL, tn), lambda j: (0, j))
    shp = _sds((D_MODEL, ncol), F32)
    return pl.pallas_call(
        body, name="ada_bwd_adamw", grid=(ncol // tn,),
        in_specs=[pl.BlockSpec((nb, D_MODEL), lambda j: (0, 0)), pl.BlockSpec((nb, tn), lambda j: (0, j)), col, col, col],
        out_specs=[col, col, col, col], out_shape=[shp, shp, shp, shp],
        compiler_params=_params("parallel"),
    )(c_all, dmod_cols, w, m, v)


def _adamw_halves(own, theirs, core, w, m, v, *, axis, name):
    r2, c2 = own.shape
    tr = _row_tile(r2)
    nt = r2 // tr

    def body(core_ref, own_ref, their_ref, w_ref, m_ref, v_ref, g_ref, dl_ref, nm_ref, nv_ref):
        g = jnp.where(pl.program_id(0) == core_ref[0], own_ref[...], their_ref[...])
        g_ref[...] = g
        dl_ref[...], nm_ref[...], nv_ref[...] = _adamw_math(g, w_ref[...], m_ref[...], v_ref[...])

    if axis == 0:
        full = pl.BlockSpec((tr, c2), lambda h, i, core_ref: (h * nt + i, 0))
    else:
        full = pl.BlockSpec((tr, c2), lambda h, i, core_ref: (i, h))
    half = pl.BlockSpec((tr, c2), lambda h, i, core_ref: (i, 0))
    shp = _sds(w.shape, F32)
    return pl.pallas_call(
        body, name=name,
        grid_spec=pltpu.PrefetchScalarGridSpec(num_scalar_prefetch=1, grid=(2, nt), in_specs=[half, half, full, full, full],
                                               out_specs=[full] * 4),
        out_shape=[shp] * 4, compiler_params=_params("parallel", "parallel"),
    )(core, own, theirs, w, m, v)


def _tok_spec(tm, width=D_MODEL):
    return pl.BlockSpec((None, tm, width), lambda b, i: (b, i, 0))


def _row_spec(width=D_MODEL):
    return pl.BlockSpec((None, 1, width), lambda b, i: (b, 0, 0))


def _vec_spec(width=D_MODEL):
    return pl.BlockSpec((1, width), lambda b, i: (0, 0))


def _mm_rows(a, b, *, name, tm, extra, extra_specs, out_specs, out_shape, epi, pro=None, trans_b=False, b_chunks=1, comms=()):
    bsz, seq, k_total = a.shape
    kc = k_total // b_chunks
    dims = NT_DIMS if trans_b else None

    def body(*refs):
        a_ref, b_ref = refs[0], refs[1]
        ex, outs = refs[2:2 + len(extra)], refs[2 + len(extra):]
        if b_chunks == 1:
            acc = _dot(a_ref[...] if pro is None else pro(a_ref, ex, outs), b_ref[...], dims)
        else:
            acc = _dot(a_ref[:, 0:kc], b_ref[0], NT_DIMS)
            for k in range(1, b_chunks):
                acc = acc + _dot(a_ref[:, k * kc:(k + 1) * kc], b_ref[k], NT_DIMS)
        epi(acc, ex, outs)

    b_spec = pl.BlockSpec(b.shape, lambda bb, i: (0,) * b.ndim)
    return _call(
        body, (a, b, *extra), name=name, grid=(bsz, seq // tm), in_specs=[_tok_spec(tm, k_total), b_spec, *extra_specs],
        out_specs=out_specs, out_shape=out_shape, sem=("arbitrary", "arbitrary"), comms=comms)


def _in_proj_fused(x, w, sc, sh, w_in_t, tables, comms=()):
    tm = 512
    bsz, seq, _ = x.shape
    half = ROPE_DIM // 2
    heads_per_slab = LANE // ATT_HEAD_DIM

    def pro(x_ref, ex, outs):
        y, _, _ = _rms_fwd(x_ref[...], ex[0][...])
        h = (y * (1.0 + ex[1][...]) + ex[2][...]).astype(BF16)
        outs[0][...] = h
        return h

    def epi(acc, ex, outs):
        c, u, d = ex[3][...], ex[4][...], ex[5][...]
        _, proj_ref, q_ref, k_ref, v_ref = outs
        proj_ref[...] = acc

        def rope(z):
            return (z * c + pltpu.roll(z, half, 1) * u + pltpu.roll(z, LANE - half, 1) * d).astype(BF16)

        for s in range(ATT_WIDTH // LANE):
            slab = rope(acc[:, s * LANE:(s + 1) * LANE])
            for part in range(heads_per_slab):
                g, hh = divmod(s * heads_per_slab + part, ATT_GROUP)
                piece = slab[:, part * ATT_HEAD_DIM:(part + 1) * ATT_HEAD_DIM]
                for blk in range(tm // WINDOW):
                    q_ref[blk, g, hh * WINDOW:(hh + 1) * WINDOW, :] = piece[blk * WINDOW:(blk + 1) * WINDOW]
        rk = rope(acc[:, ATT_WIDTH:ATT_WIDTH + LANE])
        vv = acc[:, ATT_WIDTH + LANE:ATT_COLS].astype(BF16)
        for g in range(ATT_KV_HEADS):
            k_ref[g] = rk[:, g * ATT_HEAD_DIM:(g + 1) * ATT_HEAD_DIM]
            v_ref[g] = vv[:, g * ATT_HEAD_DIM:(g + 1) * ATT_HEAD_DIM]

    cols = w_in_t.shape[0]
    tab = pl.BlockSpec((tm, LANE), lambda b, i: (i, 0))
    kv_spec = pl.BlockSpec((None, ATT_KV_HEADS, tm, ATT_HEAD_DIM), lambda b, i: (b, 0, i, 0))
    kv_shape = _sds((bsz, ATT_KV_HEADS, seq, ATT_HEAD_DIM), BF16)
    q_spec = pl.BlockSpec((None, tm // WINDOW, ATT_KV_HEADS, GROUP_ROWS, ATT_HEAD_DIM), lambda b, i: (b, i, 0, 0, 0))
    return _mm_rows(x, w_in_t, name="in_proj", tm=tm, extra=(w, sc, sh, *tables),
                    extra_specs=[_vec_spec(), _row_spec(), _row_spec(), tab, tab, tab],
                    out_specs=[_tok_spec(tm), _tok_spec(tm, cols), q_spec, kv_spec, kv_spec],
                    out_shape=[_sds(x.shape, BF16), _sds((bsz, seq, cols), F32),
                               _sds((bsz, seq // WINDOW, ATT_KV_HEADS, GROUP_ROWS, ATT_HEAD_DIM), BF16), kv_shape, kv_shape],
                    pro=pro, epi=epi, trans_b=True, comms=comms)


def _rope_tables(seq):
    half = ROPE_DIM // 2
    inv_freq = ROPE_THETA ** (-jnp.arange(0, ROPE_DIM, 2, dtype=F32) / ROPE_DIM)
    ang = jnp.arange(seq, dtype=F32)[:, None] * inv_freq[None, :]
    cos, sin = jnp.cos(ang), jnp.sin(ang)
    rest = ATT_HEAD_DIM - ROPE_DIM
    ones, zeros, zh = jnp.ones((seq, rest), F32), jnp.zeros((seq, rest), F32), jnp.zeros((seq, half), F32)
    reps = LANE // ATT_HEAD_DIM
    t_cos = jnp.tile(jnp.concatenate([cos, cos, ones], axis=1), (1, reps))
    t_up = jnp.tile(jnp.concatenate([zh, sin, zeros], axis=1), (1, reps))
    t_dn = jnp.tile(jnp.concatenate([-sin, zh, zeros], axis=1), (1, reps))
    return t_cos, t_up, t_dn


GROUP_ROWS = ATT_GROUP * WINDOW


ATT_BPS = 2


def _band_mask(has_prev):
    row = lax.broadcasted_iota(jnp.int32, (GROUP_ROWS, 2 * WINDOW), 0) % WINDOW
    col = lax.broadcasted_iota(jnp.int32, (GROUP_ROWS, 2 * WINDOW), 1)
    prev = jnp.logical_and(jnp.logical_and(col < WINDOW, col > row), has_prev)
    return jnp.logical_or(prev, jnp.logical_and(col >= WINDOW, col - WINDOW <= row))


def _sink_column(sink_ref, g):
    head = lax.broadcasted_iota(jnp.int32, (GROUP_ROWS, 1), 0) // WINDOW
    col = jnp.full((GROUP_ROWS, 1), sink_ref[0, g * ATT_GROUP], F32)
    for hh in range(1, ATT_GROUP):
        col = jnp.where(head == hh, sink_ref[0, g * ATT_GROUP + hh], col)
    return col


def _attn_specs():
    q_spec = pl.BlockSpec((None, ATT_BPS, ATT_KV_HEADS, GROUP_ROWS, ATT_HEAD_DIM), lambda b, i: (b, i, 0, 0, 0))
    kv_cur = pl.BlockSpec((None, ATT_KV_HEADS, ATT_BPS * WINDOW, ATT_HEAD_DIM), lambda b, i: (b, 0, i, 0))
    kv_prev = pl.BlockSpec((None, ATT_KV_HEADS, WINDOW, ATT_HEAD_DIM), lambda b, i: (b, 0, jnp.maximum(ATT_BPS * i - 1, 0), 0))
    return q_spec, kv_cur, kv_prev


def _band(prev_ref, cur_ref, g, blk):
    own = cur_ref[g, blk * WINDOW:(blk + 1) * WINDOW]
    before = prev_ref[g] if blk == 0 else cur_ref[g, (blk - 1) * WINDOW:blk * WINDOW]
    return jnp.concatenate([before, own], axis=0)


def _attn_fwd(qh, kh, vh, sinks, w_norm, comms=()):
    bsz, nblk = qh.shape[0], qh.shape[1]
    seq = nblk * WINDOW
    rows = ATT_BPS * WINDOW
    neg = float(jnp.finfo(jnp.float32).min)

    def body(sink_ref, q_ref, kc_ref, kp_ref, vc_ref, vp_ref, w_ref, raw_ref, an_ref, l_ref):
        for blk in range(ATT_BPS):
            mask = _band_mask(True if blk else pl.program_id(1) > 0)
            for g in range(ATT_KV_HEADS):
                keys, vals = _band(kp_ref, kc_ref, g, blk), _band(vp_ref, vc_ref, g, blk)
                sink = _sink_column(sink_ref, g)
                s = jnp.where(mask, _dot(q_ref[blk, g], keys, NT_DIMS) * ATT_SCALE, neg)
                m = jnp.maximum(jnp.max(s, axis=-1, keepdims=True), sink)
                p = jnp.where(mask, jnp.exp(s - m), 0.0)
                den = jnp.sum(p, axis=-1, keepdims=True) + jnp.exp(sink - m)
                o = _dot(p / den, vals)
                lse = m + jnp.log(den)
                tok = slice(blk * WINDOW, (blk + 1) * WINDOW)
                for hh in range(ATT_GROUP):
                    h = g * ATT_GROUP + hh
                    raw_ref[tok, h * ATT_HEAD_DIM:(h + 1) * ATT_HEAD_DIM] = o[hh * WINDOW:(hh + 1) * WINDOW]
                    l_ref[tok, h:h + 1] = lse[hh * WINDOW:(hh + 1) * WINDOW]
        y, _, _ = _rms_fwd(raw_ref[...], w_ref[...])
        an_ref[...] = y.astype(BF16)

    cur = lambda width: pl.BlockSpec((None, rows, width), lambda b, i: (b, i, 0))
    q_spec, kv_cur, kv_prev = _attn_specs()
    return _call(
        body, (sinks, qh, kh, kh, vh, vh, w_norm), name="attn_fwd", grid=(bsz, nblk // ATT_BPS),
        in_specs=[pl.BlockSpec(memory_space=pltpu.SMEM), q_spec, kv_cur, kv_prev, kv_cur, kv_prev, _vec_spec(ATT_WIDTH)],
        out_specs=[cur(ATT_WIDTH), cur(ATT_WIDTH), cur(ATT_Q_HEADS)],
        out_shape=[_sds((bsz, seq, ATT_WIDTH), F32), _sds((bsz, seq, MIX_WIDTH), BF16), _sds((bsz, seq, ATT_Q_HEADS), F32)],
        sem=("parallel", "parallel"), comms=comms)


HG_Q0 = ATT_COLS // LANE
HG_F0 = HG_Q0 + HG_HEADS
HG_I0 = HG_F0 + HG_HEADS
HG_G0 = HG_I0 + HG_HEADS
HG_TOK = 256
HG_NCH = HG_TOK // HG_CHUNK
HG_HPS = 2


def _block_masks():
    row = lax.broadcasted_iota(jnp.int32, (HG_TOK, HG_TOK), 0)
    col = lax.broadcasted_iota(jnp.int32, (HG_TOK, HG_TOK), 1)
    same = (row // HG_CHUNK) == (col // HG_CHUNK)
    return jnp.logical_and(same, col <= row), jnp.logical_and(same, col >= row)


def _row_in_chunk():
    return lax.broadcasted_iota(jnp.int32, (HG_TOK, LANE), 0) % HG_CHUNK


def _chunk_cumsum(x, reverse=False):
    ric = _row_in_chunk()
    shift = 1
    while shift < HG_CHUNK:
        if reverse:
            x = x + jnp.where(ric < HG_CHUNK - shift, pltpu.roll(x, HG_TOK - shift, 0), 0.0)
        else:
            x = x + jnp.where(ric >= shift, pltpu.roll(x, shift, 0), 0.0)
        shift *= 2
    return x


def _chunk_rows(rows):
    stacked = jnp.concatenate([r[None] for r in rows], axis=0)
    return jnp.broadcast_to(stacked, (HG_NCH, HG_CHUNK, LANE)).reshape(HG_TOK, LANE)


def _chunk_slices(x):
    return [x[j * HG_CHUNK:(j + 1) * HG_CHUNK] for j in range(HG_NCH)]


def _hgrn_common(tbl, hf, hq):
    lb = _sigmoid(tbl[1:2] - tbl[0:1])
    sig = _sigmoid(hf)
    f = lb + (1.0 - lb) * sig
    sq = _sigmoid(hq)
    q, k = hq * sq, 1.0 - f
    b = _chunk_cumsum(jnp.log(f))
    last = [b[(j + 1) * HG_CHUNK - 1:(j + 1) * HG_CHUNK] for j in range(HG_NCH)]
    bl = _chunk_rows(last)
    e_b, e_nb, e_rem = jnp.exp(b), jnp.exp(-b), jnp.exp(bl - b)
    e_last = [jnp.exp(r) for r in last]
    return dict(lb=lb, sig=sig, f=f, sq=sq, q=q, k=k, e_b=e_b, e_nb=e_nb, e_rem=e_rem, e_last=e_last,
                qd=q * e_b, kd=k * e_nb, ku=k * e_rem)


def _hgrn_fwd(proj, lb_table, norm_w, mix_in, comms=()):
    bsz, seq, _ = proj.shape
    nstep = seq // HG_TOK

    def body(tbl_ref, nw_ref, q_ref, f_ref, i_ref, g_ref, mix_ref, o_ref, rec_ref, st_ref, s_scr):
        @pl.when(pl.program_id(2) == 0)
        def _():
            s_scr[...] = jnp.zeros_like(s_scr)

        lower, _ = _block_masks()
        for hp in range(HG_HPS):
            ls = slice(hp * LANE, (hp + 1) * LANE)
            v, hg = i_ref[:, ls], g_ref[:, ls]
            t = _hgrn_common(tbl_ref[:, ls], f_ref[:, ls], q_ref[:, ls])
            a = jnp.where(lower, _dot(t["qd"], t["kd"], NT_DIMS), 0.0)
            o_intra = _dot(a, v)
            v_c, ku_c, qd_c = [_chunk_slices(z.astype(BF16)) for z in (v, t["ku"], t["qd"])]
            updates = [_dot(v_c[j], ku_c[j], TN_DIMS) for j in range(HG_NCH)]
            st = s_scr[hp]
            states = []
            for j in range(HG_NCH):
                states.append(st)
                st = st * t["e_last"][j] + updates[j]
            s_scr[hp] = st
            o = o_intra + jnp.concatenate([_dot(qd_c[j], states[j], NT_DIMS) for j in range(HG_NCH)], axis=0)
            for j in range(HG_NCH):
                st_ref[hp, j] = states[j]
            o_ref[:, ls] = o
            y, _, _ = _rms_fwd(o, nw_ref[...])
            rec_ref[:, ls] = (y * (hg * _sigmoid(hg))).astype(BF16)

    width = HG_HPS * LANE
    slab = lambda first: pl.BlockSpec((None, HG_TOK, width), lambda b, h, t: (b, t, first // HG_HPS + h))
    head_out = pl.BlockSpec((None, HG_TOK, width), lambda b, h, t: (b, t, h))
    mix_out = pl.BlockSpec((None, HG_TOK, width), lambda b, h, t: (b, t, ATT_WIDTH // width + h))
    return _call(
        body, (lb_table, norm_w, proj, proj, proj, proj, mix_in), name="hgrn_fwd", grid=(bsz, HG_HEADS // HG_HPS, nstep),
        in_specs=[pl.BlockSpec((2, width), lambda b, h, t: (0, h)), pl.BlockSpec((1, LANE), lambda b, h, t: (0, 0)),
                  slab(HG_Q0), slab(HG_F0), slab(HG_I0), slab(HG_G0), pl.BlockSpec(memory_space=pl.ANY)],
        out_specs=[head_out, mix_out,
                   pl.BlockSpec((None, HG_HPS, HG_NCH, LANE, LANE), lambda b, h, t: (b, h, t, 0, 0))],
        out_shape=[_sds((bsz, seq, HG_WIDTH), F32), _sds(mix_in.shape, BF16),
                   _sds((bsz, HG_HEADS, seq // HG_CHUNK, LANE, LANE), F32)],
        scratch_shapes=[pltpu.VMEM((HG_HPS, LANE, LANE), F32)],
        sem=("parallel", "parallel", "arbitrary"), comms=comms, aliases={6: 1})


def _out_proj_fused(cat, w_out, x, post_w, g1, pre_w, sc2, sh2):
    tm = 512

    def epi(mix, ex, outs):
        x_ref, pw_ref, g1_ref, w2_ref, sc_ref, sh_ref = ex
        outs[0][...] = mix
        n1, _, _ = _rms_fwd(mix, pw_ref[...])
        x1 = x_ref[...] + g1_ref[...] * n1
        outs[1][...] = x1
        y2, _, _ = _rms_fwd(x1, w2_ref[...])
        outs[2][...] = (y2 * (1.0 + sc_ref[...]) + sh_ref[...]).astype(BF16)

    return _mm_rows(cat, w_out, name="out_proj", tm=tm, extra=(x, post_w, g1, pre_w, sc2, sh2),
                    extra_specs=[_tok_spec(tm), _vec_spec(), _row_spec(), _vec_spec(), _row_spec(), _row_spec()],
                    out_specs=[_tok_spec(tm), _tok_spec(tm), _tok_spec(tm)],
                    out_shape=[_sds(x.shape, F32), _sds(x.shape, F32), _sds(x.shape, BF16)], epi=epi)


def _acc_out(ref, first, value):
    @pl.when(first)
    def _():
        ref[...] = value

    @pl.when(jnp.logical_not(first))
    def _():
        ref[...] += value


def _down_proj_fused(r, w_down, x1, post_w, g2, target):
    tm = 256
    bsz = x1.shape[0]

    def pro(r_ref, ex, outs):
        rv = r_ref[...]
        return rv * rv

    def epi(down, ex, outs):
        x1_ref, w_ref, g2_ref, t_ref = ex
        loss_ref, dy_ref, dd_ref, dg2_ref, dw_ref = outs
        b, i = pl.program_id(0), pl.program_id(1)
        w, g2v = w_ref[...], g2_ref[...]
        n2, dh, rstd = _rms_fwd(down, w)
        err = x1_ref[...] + g2v * n2 - t_ref[...]
        part = (0.5 / D_MODEL) * jnp.sum(jnp.sum(err * err, axis=-1, keepdims=True), axis=0, keepdims=True)
        _acc_out(loss_ref, jnp.logical_and(b == 0, i == 0), jnp.broadcast_to(part, (1, LANE)))
        dy = err * (1.0 / D_MODEL)
        dy_ref[...] = dy
        _acc_out(dg2_ref, i == 0, _colsum(dy * n2))
        dd, dw_rows = _rms_bwd(dy * g2v, dh, rstd, w)
        dd_ref[...] = dd.astype(BF16)
        _acc_out(dw_ref, jnp.logical_and(b == 0, i == 0), _colsum(dw_rows))

    return _mm_rows(r, w_down, name="down_proj", tm=tm, extra=(x1, post_w, g2, target),
                    extra_specs=[_tok_spec(tm), _vec_spec(), _row_spec(), _tok_spec(tm)],
                    out_specs=[_vec_spec(LANE), _tok_spec(tm), _tok_spec(tm), _row_spec(), _vec_spec()],
                    out_shape=[_sds((1, LANE), F32), _sds(x1.shape, F32), _sds(x1.shape, BF16), _sds((bsz, 1, D_MODEL), F32),
                               _sds((1, D_MODEL), F32)], pro=pro, epi=epi)


def _up_bwd_fused(dpre, w_up4, dy, x1, mix, pre_w, sc2, post_w, g1, comms=()):
    tm = 256
    bsz = x1.shape[0]

    def epi(dh2v, ex, outs):
        dy_ref, x1_ref, mix_ref, w2_ref, sc_ref, pw_ref, g1_ref = ex
        dx1_ref, dmix_ref, dsc_ref, dsh_ref, dg1_ref, dw2_ref, dpw_ref = outs
        b, i = pl.program_id(0), pl.program_id(1)
        first = jnp.logical_and(b == 0, i == 0)
        w2, pw = w2_ref[...], pw_ref[...]
        y2, xh2, rstd2 = _rms_fwd(x1_ref[...], w2)
        _acc_out(dsh_ref, i == 0, _colsum(dh2v))
        _acc_out(dsc_ref, i == 0, _colsum(dh2v * y2))
        dx1n, dw_rows = _rms_bwd(dh2v * (1.0 + sc_ref[...]), xh2, rstd2, w2)
        _acc_out(dw2_ref, first, _colsum(dw_rows))
        dx1 = dy_ref[...] + dx1n
        dx1_ref[...] = dx1
        n1, mh, rstd1 = _rms_fwd(mix_ref[...], pw)
        _acc_out(dg1_ref, i == 0, _colsum(dx1 * n1))
        dmix, dpw_rows = _rms_bwd(dx1 * g1_ref[...], mh, rstd1, pw)
        dmix_ref[...] = dmix.astype(BF16)
        _acc_out(dpw_ref, first, _colsum(dpw_rows))

    row_shape = _sds((bsz, 1, D_MODEL), F32)
    vec_shape = _sds((1, D_MODEL), F32)
    return _mm_rows(dpre, w_up4, name="up_bwd", tm=tm, extra=(dy, x1, mix, pre_w, sc2, post_w, g1),
                    extra_specs=[_tok_spec(tm), _tok_spec(tm), _tok_spec(tm), _vec_spec(), _row_spec(), _vec_spec(), _row_spec()],
                    out_specs=[_tok_spec(tm), _tok_spec(tm), _row_spec(), _row_spec(), _row_spec(), _vec_spec(), _vec_spec()],
                    out_shape=[_sds(x1.shape, F32), _sds(x1.shape, BF16), row_shape, row_shape, row_shape, vec_shape, vec_shape],
                    epi=epi, b_chunks=w_up4.shape[0], comms=comms)


def _norm1_bwd(dh1, dx1, x, pre_w, sc1, tm=512):
    bsz, seq, _ = x.shape

    def body(dh_ref, dx1_ref, x_ref, w_ref, sc_ref, gx_ref, dsc_ref, dsh_ref, dw_ref):
        b, i = pl.program_id(0), pl.program_id(1)
        w = w_ref[...]
        dh = dh_ref[...]
        y, xh, rstd = _rms_fwd(x_ref[...], w)
        _acc_out(dsh_ref, i == 0, _colsum(dh))
        _acc_out(dsc_ref, i == 0, _colsum(dh * y))
        dx, dw_rows = _rms_bwd(dh * (1.0 + sc_ref[...]), xh, rstd, w)
        _acc_out(dw_ref, jnp.logical_and(b == 0, i == 0), _colsum(dw_rows))
        gx_ref[...] = dx1_ref[...] + dx

    row_shape = _sds((bsz, 1, D_MODEL), F32)
    return pl.pallas_call(
        body, name="norm1_bwd", grid=(bsz, seq // tm),
        in_specs=[_tok_spec(tm), _tok_spec(tm), _tok_spec(tm), _vec_spec(), _row_spec()],
        out_specs=[_tok_spec(tm), _row_spec(), _row_spec(), _vec_spec()],
        out_shape=[_sds(x.shape, F32), row_shape, row_shape, _sds((1, D_MODEL), F32)],
        compiler_params=_params("arbitrary", "arbitrary"),
    )(dh1, dx1, x, pre_w, sc1)


def _hgrn_bwd(dcat, proj, o_raw, states, lb_table, norm_w, comms=()):
    bsz, seq, _ = proj.shape
    nstep = seq // HG_TOK
    rec0 = ATT_WIDTH // LANE

    def body(tbl_ref, nw_ref, dr_ref, q_ref, f_ref, i_ref, g_ref, o_ref, st_ref,
             dq_ref, df_ref, di_ref, dg_ref, dlb_ref, dnw_ref, ds_scr):
        h, b, t = pl.program_id(0), pl.program_id(1), pl.program_id(2)

        @pl.when(t == 0)
        def _():
            ds_scr[...] = jnp.zeros_like(ds_scr)

        lower, upper = _block_masks()
        dlb_parts = []
        dnw_acc = jnp.zeros((1, LANE), F32)
        for hp in range(HG_HPS):
            ls = slice(hp * LANE, (hp + 1) * LANE)
            hq, v, hg = q_ref[:, ls], i_ref[:, ls], g_ref[:, ls]
            nw = nw_ref[...]
            c = _hgrn_common(tbl_ref[:, ls], f_ref[:, ls], hq)
            qd, kd, ku = c["qd"], c["kd"], c["ku"]
            y, on, rstd = _rms_fwd(o_ref[:, ls], nw)
            sg = _sigmoid(hg)
            dr = dr_ref[:, ls]
            dg_ref[:, ls] = (dr * y * (sg * (1.0 + hg * (1.0 - sg)))).astype(BF16)
            do, dnw_rows = _rms_bwd(dr * (hg * sg), on, rstd, nw)
            at = jnp.where(upper, _dot(kd, qd, NT_DIMS), 0.0)
            da = jnp.where(lower, _dot(do, v, NT_DIMS), 0.0)
            dat = jnp.where(upper, _dot(v, do, NT_DIMS), 0.0)
            dv = _dot(at, do)
            dqd = _dot(da, kd)
            dkd = _dot(dat, qd)
            do_c, qd_c, v_c, ku_c = [_chunk_slices(z.astype(BF16)) for z in (do, qd, v, ku)]
            outer = [_dot(do_c[j], qd_c[j], TN_DIMS) for j in range(HG_NCH)]
            ds = ds_scr[hp]
            ds_after = [None] * HG_NCH
            for j in reversed(range(HG_NCH)):
                ds_after[j] = ds
                ds = outer[j] + ds * c["e_last"][j]
            ds_scr[hp] = ds
            states = [st_ref[hp, j] for j in range(HG_NCH)]
            dv = dv + jnp.concatenate([_dot(ku_c[j], ds_after[j], NT_DIMS) for j in range(HG_NCH)], axis=0)
            dqd = dqd + jnp.concatenate([_dot(do_c[j], states[j]) for j in range(HG_NCH)], axis=0)
            dku = jnp.concatenate([_dot(v_c[j], ds_after[j]) for j in range(HG_NCH)], axis=0)
            dku_ku = dku * ku
            dbl = [_colsum(states[j] * ds_after[j]) * c["e_last"][j] + _colsum(dku_ku[j * HG_CHUNK:(j + 1) * HG_CHUNK])
                   for j in range(HG_NCH)]
            dk = dkd * c["e_nb"] + dku * c["e_rem"]
            db = dqd * qd - dkd * kd - dku_ku + jnp.where(_row_in_chunk() == HG_CHUNK - 1, _chunk_rows(dbl), 0.0)
            dfv = _chunk_cumsum(db, reverse=True) / c["f"] - dk
            sig, sq = c["sig"], c["sq"]
            df_ref[:, ls] = (dfv * (1.0 - c["lb"]) * sig * (1.0 - sig)).astype(BF16)
            dq_ref[:, ls] = (dqd * c["e_b"] * (sq * (1.0 + hq * (1.0 - sq)))).astype(BF16)
            di_ref[:, ls] = dv.astype(BF16)
            dlb_parts.append(_colsum(dfv * (1.0 - sig)))
            dnw_acc = dnw_acc + _colsum(dnw_rows)
        _acc_out(dlb_ref, jnp.logical_and(b == 0, t == 0), jnp.concatenate(dlb_parts, axis=1))
        _acc_out(dnw_ref, jnp.logical_and(h == 0, jnp.logical_and(b == 0, t == 0)), dnw_acc)

    rev = lambda t: nstep - 1 - t
    width = HG_HPS * LANE
    slab = lambda first: pl.BlockSpec((None, HG_TOK, width), lambda h, b, t: (b, rev(t), first // HG_HPS + h))
    head = pl.BlockSpec((None, HG_TOK, width), lambda h, b, t: (b, rev(t), h))
    grad_shape = _sds((bsz, seq, HG_WIDTH), BF16)
    return _call(
        body, (lb_table, norm_w, dcat, proj, proj, proj, proj, o_raw, states), name="hgrn_bwd",
        grid=(HG_HEADS // HG_HPS, bsz, nstep),
        in_specs=[pl.BlockSpec((2, width), lambda h, b, t: (0, h)), pl.BlockSpec((1, LANE), lambda h, b, t: (0, 0)),
                  slab(rec0), slab(HG_Q0), slab(HG_F0), slab(HG_I0), slab(HG_G0), head,
                  pl.BlockSpec((None, HG_HPS, HG_NCH, LANE, LANE), lambda h, b, t: (b, h, rev(t), 0, 0))],
        out_specs=[head, head, head, head, pl.BlockSpec((1, width), lambda h, b, t: (0, h)),
                   pl.BlockSpec((1, LANE), lambda h, b, t: (0, 0))],
        out_shape=[grad_shape, grad_shape, grad_shape, grad_shape, _sds((1, HG_WIDTH), F32), _sds((1, LANE), F32)],
        scratch_shapes=[pltpu.VMEM((HG_HPS, LANE, LANE), F32)],
        sem=("arbitrary", "arbitrary", "arbitrary"), comms=comms)


def _attn_bwd(dcat, raw, w_norm, qh, kh, vh, lse, sinks, comms=()):
    bsz, nblk = qh.shape[0], qh.shape[1]
    seq = nblk * WINDOW

    def body(sink_ref, da_ref, raw_ref, w_ref, q_ref, kc_ref, kp_ref, vc_ref, vp_ref, l_ref,
             dq_ref, dkd_ref, dkp_ref, dvd_ref, dvp_ref, dw_ref, dsink_ref):
        b, i = pl.program_id(0), pl.program_id(1)
        first = jnp.logical_and(b == 0, i == 0)
        w = w_ref[...]
        _, on, rstd = _rms_fwd(raw_ref[...], w)
        do_step, dw_rows = _rms_bwd(da_ref[...], on, rstd, w)
        _acc_out(dw_ref, first, _colsum(dw_rows))
        lane8 = lax.broadcasted_iota(jnp.int32, (1, ATT_Q_HEADS), 1)
        dsink = jnp.zeros((1, ATT_Q_HEADS), F32)
        for blk in range(ATT_BPS):
            tok = slice(blk * WINDOW, (blk + 1) * WINDOW)
            mask = _band_mask(True if blk else i > 0)
            raw_v, do_all = raw_ref[tok, :], do_step[tok]
            for g in range(ATT_KV_HEADS):
                gs = slice(g * ATT_HEAD_DIM, (g + 1) * ATT_HEAD_DIM)
                heads = [slice((g * ATT_GROUP + hh) * ATT_HEAD_DIM, (g * ATT_GROUP + hh + 1) * ATT_HEAD_DIM)
                         for hh in range(ATT_GROUP)]
                q = q_ref[blk, g]
                keys, vals = _band(kp_ref, kc_ref, g, blk), _band(vp_ref, vc_ref, g, blk)
                do_g = jnp.concatenate([do_all[:, hs] for hs in heads], axis=0)
                dsum = jnp.concatenate([jnp.sum(do_all[:, hs] * raw_v[:, hs], axis=-1, keepdims=True) for hs in heads], axis=0)
                lse_g = jnp.concatenate([l_ref[tok, g * ATT_GROUP + hh:g * ATT_GROUP + hh + 1] for hh in range(ATT_GROUP)], axis=0)
                p = jnp.where(mask, jnp.exp(_dot(q, keys, NT_DIMS) * ATT_SCALE - lse_g), 0.0)
                sink_part = jnp.exp(_sink_column(sink_ref, g) - lse_g) * dsum
                for hh in range(ATT_GROUP):
                    head_sum = jnp.sum(sink_part[hh * WINDOW:(hh + 1) * WINDOW], axis=0, keepdims=True)
                    dsink = dsink - jnp.where(lane8 == g * ATT_GROUP + hh, head_sum, 0.0)
                ds = p * (_dot(do_g, vals, NT_DIMS) - dsum) * ATT_SCALE
                dq_g = _dot(ds, keys)
                for hh, hs in enumerate(heads):
                    dq_ref[tok, hs] = dq_g[hh * WINDOW:(hh + 1) * WINDOW]
                dk_g = _dot(ds, q, TN_DIMS)
                dv_g = _dot(p, do_g, TN_DIMS)
                dkp_ref[tok, gs], dkd_ref[tok, gs] = dk_g[:WINDOW], dk_g[WINDOW:]
                dvp_ref[tok, gs], dvd_ref[tok, gs] = dv_g[:WINDOW], dv_g[WINDOW:]
        _acc_out(dsink_ref, first, dsink)

    rows = ATT_BPS * WINDOW
    cur = lambda width: pl.BlockSpec((None, rows, width), lambda b, i: (b, i, 0))
    q_spec, kv_cur, kv_prev = _attn_specs()
    kv_shape = _sds((bsz, seq, LANE), F32)
    return _call(
        body, (sinks, dcat, raw, w_norm, qh, kh, kh, vh, vh, lse), name="attn_bwd", grid=(bsz, nblk // ATT_BPS),
        in_specs=[pl.BlockSpec(memory_space=pltpu.SMEM), cur(ATT_WIDTH), cur(ATT_WIDTH), _vec_spec(ATT_WIDTH), q_spec,
                  kv_cur, kv_prev, kv_cur, kv_prev, cur(ATT_Q_HEADS)],
        out_specs=[cur(ATT_WIDTH), cur(LANE), cur(LANE), cur(LANE), cur(LANE), _vec_spec(ATT_WIDTH), _vec_spec(ATT_Q_HEADS)],
        out_shape=[_sds((bsz, seq, ATT_WIDTH), F32), kv_shape, kv_shape, kv_shape, kv_shape, _sds((1, ATT_WIDTH), F32),
                   _sds((1, ATT_Q_HEADS), F32)],
        sem=("arbitrary", "arbitrary"), comms=comms)


def _rope_bwd(dq, dkd, dkp, dvd, dvp, tables):
    bsz, seq, _ = dq.shape
    nblk = seq // WINDOW
    half = ROPE_DIM // 2

    def body(dq_ref, dkd_ref, dkp_ref, dvd_ref, dvp_ref, c_ref, u_ref, d_ref, o_ref):
        c, u, d = c_ref[...], u_ref[...], d_ref[...]
        has_next = pl.program_id(1) < nblk - 1

        def unrope(g):
            return g * c + pltpu.roll(g * u, LANE - half, 1) + pltpu.roll(g * d, half, 1)

        for s in range(ATT_WIDTH // LANE):
            o_ref[:, s * LANE:(s + 1) * LANE] = unrope(dq_ref[:, s * LANE:(s + 1) * LANE]).astype(BF16)
        dk = dkd_ref[...] + jnp.where(has_next, dkp_ref[...], 0.0)
        o_ref[:, ATT_WIDTH:ATT_WIDTH + LANE] = unrope(dk).astype(BF16)
        o_ref[:, ATT_WIDTH + LANE:ATT_COLS] = (dvd_ref[...] + jnp.where(has_next, dvp_ref[...], 0.0)).astype(BF16)

    cur = lambda width: pl.BlockSpec((None, WINDOW, width), lambda b, i: (b, i, 0))
    nxt = pl.BlockSpec((None, WINDOW, LANE), lambda b, i: (b, jnp.minimum(i + 1, nblk - 1), 0))
    tab = pl.BlockSpec((WINDOW, LANE), lambda b, i: (i, 0))
    return pl.pallas_call(
        body, name="rope_bwd", grid=(bsz, nblk),
        in_specs=[cur(ATT_WIDTH), cur(LANE), nxt, cur(LANE), nxt, tab, tab, tab],
        out_specs=cur(ATT_COLS), out_shape=_sds((bsz, seq, ATT_COLS), BF16),
        compiler_params=_params("parallel", "parallel"),
    )(dq, dkd, dkp, dvd, dvp, *tables)


def _other_chips(x, y):
    return [(1 - x, y), (x, 1 - y), (1 - x, 1 - y)]


def _sem_pair(n):
    return [pltpu.SemaphoreType.DMA((n,)), pltpu.SemaphoreType.DMA((n,))]


def _rows_of(ref, rows):
    return ref if rows is None else ref.at[pl.ds(rows[0], rows[1])]


def _plan_chip_gather(blocks, bufs, rows=None, forward_rows=None):
    n = len(blocks)

    def copies(ins, outs, sems):
        x, y, c = _mesh_pos()
        sends, lands = [], []
        for a in range(n):
            for j, chip in enumerate(_other_chips(x, y)):
                k = 3 * a + j
                sends.append(pltpu.make_async_remote_copy(
                    src_ref=_rows_of(ins[a], rows), dst_ref=_rows_of(outs[a].at[4 * x + 2 * y + c], rows), send_sem=sems[0].at[k],
                    recv_sem=sems[1].at[k], device_id=(*chip, c), device_id_type=MESH))
                slot = _rows_of(outs[a].at[4 * chip[0] + 2 * chip[1] + c], rows)
                lands.append(pltpu.make_async_remote_copy(
                    src_ref=slot, dst_ref=slot, send_sem=sems[0].at[k], recv_sem=sems[1].at[k],
                    device_id=(*chip, c), device_id_type=MESH))
                if forward_rows is not None:
                    k = 3 * (n + a) + j
                    mine = _rows_of(outs[a].at[4 * chip[0] + 2 * chip[1] + c], forward_rows)
                    sends.append(pltpu.make_async_remote_copy(
                        src_ref=mine, dst_ref=mine, send_sem=sems[0].at[k], recv_sem=sems[1].at[k],
                        device_id=(x, y, 1 - c), device_id_type=MESH))
                    theirs = _rows_of(outs[a].at[4 * chip[0] + 2 * chip[1] + 1 - c], forward_rows)
                    lands.append(pltpu.make_async_remote_copy(
                        src_ref=theirs, dst_ref=theirs, send_sem=sems[0].at[k], recv_sem=sems[1].at[k],
                        device_id=(x, y, 1 - c), device_id_type=MESH))
        return sends, lands

    def start(ins, outs, sems):
        for cp in copies(ins, outs, sems)[0]:
            cp.start()

    def finish(ins, outs, sems):
        sends, lands = copies(ins, outs, sems)
        for cp in lands:
            cp.wait_recv()
        for cp in sends:
            cp.wait_send()

    n_sems = 3 * n * (2 if forward_rows is not None else 1)
    return _Comm(list(blocks) + list(bufs), [_sds(b.shape, b.dtype) for b in bufs], _sem_pair(n_sems), start, finish,
                 aliases=[(n + a, a) for a in range(n)])


def _plan_pair_forward(bufs, rows=None):
    n = len(bufs)

    def copies(outs, sems):
        x, y, c = _mesh_pos()
        sends, lands = [], []
        for a in range(n):
            for j, chip in enumerate(_other_chips(x, y)):
                k = 3 * a + j
                slot = _rows_of(outs[a].at[4 * chip[0] + 2 * chip[1] + c], rows)
                sends.append(pltpu.make_async_remote_copy(
                    src_ref=slot, dst_ref=slot, send_sem=sems[0].at[k], recv_sem=sems[1].at[k],
                    device_id=(x, y, 1 - c), device_id_type=MESH))
                theirs = _rows_of(outs[a].at[4 * chip[0] + 2 * chip[1] + 1 - c], rows)
                lands.append(pltpu.make_async_remote_copy(
                    src_ref=theirs, dst_ref=theirs, send_sem=sems[0].at[k], recv_sem=sems[1].at[k],
                    device_id=(x, y, 1 - c), device_id_type=MESH))
        return sends, lands

    def start(ins, outs, sems):
        for cp in copies(outs, sems)[0]:
            cp.start()

    def finish(ins, outs, sems):
        sends, lands = copies(outs, sems)
        for cp in lands:
            cp.wait_recv()
        for cp in sends:
            cp.wait_send()

    return _Comm(list(bufs), [_sds(b.shape, b.dtype) for b in bufs], _sem_pair(3 * n), start, finish,
                 aliases=[(a, a) for a in range(n)])


def _plan_pair(arrays, other_half):
    n = len(arrays)
    per = N_CHIPS if other_half == "chip_major" else 1

    def copies(ins, outs, sems):
        x, y, c = _mesh_pos()
        out = []
        for a in range(n):
            for k in range(per):
                if other_half == "chip_major":
                    src, dst = ins[a].at[k, 1 - c], outs[a].at[k]
                else:
                    src, dst = (ins[a].at[1 - c] if other_half else ins[a]), outs[a]
                out.append(pltpu.make_async_remote_copy(
                    src_ref=src, dst_ref=dst, send_sem=sems[0].at[per * a + k], recv_sem=sems[1].at[per * a + k],
                    device_id=(x, y, 1 - c), device_id_type=MESH))
        return out

    def start(ins, outs, sems):
        for cp in copies(ins, outs, sems):
            cp.start()

    def finish(ins, outs, sems):
        for cp in copies(ins, outs, sems):
            cp.wait()

    if other_half == "chip_major":
        shapes = [_sds((a.shape[0],) + a.shape[2:], a.dtype) for a in arrays]
    else:
        shapes = [_sds(a.shape[1:] if other_half else a.shape, a.dtype) for a in arrays]
    return _Comm(list(arrays), shapes, _sem_pair(per * n), start, finish)


def _plan_chip_exchange(arrays):
    n = len(arrays)

    def copies(ins, outs, sems):
        x, y, c = _mesh_pos()
        sends, lands = [], []
        for a in range(n):
            for j, chip in enumerate(_other_chips(x, y)):
                k = 3 * a + j
                sends.append(pltpu.make_async_remote_copy(
                    src_ref=ins[a].at[2 * chip[0] + chip[1]], dst_ref=outs[a].at[2 * x + y], send_sem=sems[0].at[k],
                    recv_sem=sems[1].at[k], device_id=(*chip, c), device_id_type=MESH))
                slot = outs[a].at[2 * chip[0] + chip[1]]
                lands.append(pltpu.make_async_remote_copy(
                    src_ref=slot, dst_ref=slot, send_sem=sems[0].at[k], recv_sem=sems[1].at[k],
                    device_id=(*chip, c), device_id_type=MESH))
        return sends, lands

    def start(ins, outs, sems):
        for cp in copies(ins, outs, sems)[0]:
            cp.start()

    def finish(ins, outs, sems):
        sends, lands = copies(ins, outs, sems)
        for cp in lands:
            cp.wait_recv()
        for cp in sends:
            cp.wait_send()

    return _Comm(list(arrays), [_sds(a.shape, a.dtype) for a in arrays], _sem_pair(3 * n), start, finish)


def _comm_only(comms, name):
    return _call(lambda: None, (), name=name, grid=(), in_specs=[], out_specs=[], out_shape=[], sem=(), comms=comms)[1]


def _allgather8(arrays, name):
    return _comm_only([_plan_allgather8(arrays)], name)[0]


def _plan_allgather8(arrays):
    n = len(arrays)

    def parts(ins, outs, sems):
        send_sems, recv_sems, local_sems = sems
        x, y, c = _mesh_pos()
        me, sibling = (x, y, c), (x, y, 1 - c)
        chips = _other_chips(x, y)

        def copy(a, k, block, to, src=None):
            dst = outs[a].at[4 * block[0] + 2 * block[1] + block[2]]
            return pltpu.make_async_remote_copy(
                src_ref=dst if src is None else src, dst_ref=dst, send_sem=send_sems.at[7 * a + k],
                recv_sem=recv_sems.at[7 * a + k], device_id=to, device_id_type=MESH)

        mine = [pltpu.make_async_copy(ins[a], outs[a].at[4 * x + 2 * y + c], local_sems.at[a]) for a in range(n)]
        first = []
        for a in range(n):
            first.append(copy(a, 0, me, sibling, src=ins[a]))
            first += [copy(a, 1 + j, me, (*chip, c), src=ins[a]) for j, chip in enumerate(chips)]
        return copy, mine, first, me, sibling, chips, c

    def start(ins, outs, sems):
        _, mine, first, *_ = parts(ins, outs, sems)
        for cp in mine + first:
            cp.start()

    def finish(ins, outs, sems):
        copy, mine, first, me, sibling, chips, c = parts(ins, outs, sems)
        passed = []
        for j, chip in enumerate(chips):
            for a in range(n):
                copy(a, 1 + j, (*chip, c), me).wait_recv()
                fwd = copy(a, 4 + j, (*chip, c), sibling)
                fwd.start()
                passed.append(fwd)
        for a in range(n):
            copy(a, 0, sibling, me).wait_recv()
            for j, chip in enumerate(chips):
                copy(a, 4 + j, (*chip, 1 - c), me).wait_recv()
        for cp in first + passed:
            cp.wait_send()
        for cp in mine:
            cp.wait()

    sems = [pltpu.SemaphoreType.DMA((7 * n,)), pltpu.SemaphoreType.DMA((7 * n,)), pltpu.SemaphoreType.DMA((n,))]
    return _Comm(list(arrays), [_sds((N_DEV,) + a.shape, a.dtype) for a in arrays], sems, start, finish)


def _pair_sum(g, q, core, name, chip_major=False):
    rows, cols = g.shape[2:]
    tr = _row_tile(rows)

    def body(core_ref, g_ref, q_ref, o_ref):
        o_ref[...] = (g_ref[...] + q_ref[...]).astype(BF16)

    blk = pl.BlockSpec((None, tr, cols), lambda k, i, core_ref: (k, i, 0))
    if chip_major:
        own = pl.BlockSpec((None, None, tr, cols), lambda k, i, core_ref: (k, core_ref[0], i, 0))
    else:
        own = pl.BlockSpec((None, None, tr, cols), lambda k, i, core_ref: (core_ref[0], k, i, 0))
    return pl.pallas_call(
        body, name=name,
        grid_spec=pltpu.PrefetchScalarGridSpec(num_scalar_prefetch=1, grid=(N_CHIPS, rows // tr), in_specs=[own, blk], out_specs=blk),
        out_shape=_sds((N_CHIPS, rows, cols), BF16), compiler_params=_params("parallel", "parallel"),
    )(core, g, q)


def _sum_chips(own, landed, chip, name):
    _, rows, cols = own.shape
    tr = _row_tile(rows)

    def body(chip_ref, own_ref, a_ref, b_ref, c_ref, o_ref):
        acc = own_ref[...].astype(F32) + a_ref[...].astype(F32)
        o_ref[...] = (acc + b_ref[...].astype(F32)) + c_ref[...].astype(F32)

    blk = lambda flip: pl.BlockSpec((None, tr, cols), lambda i, chip_ref: (jnp.bitwise_xor(chip_ref[0], flip), i, 0))
    return pl.pallas_call(
        body, name=name,
        grid_spec=pltpu.PrefetchScalarGridSpec(num_scalar_prefetch=1, grid=(rows // tr,), in_specs=[blk(0), blk(1), blk(2), blk(3)],
                                               out_specs=pl.BlockSpec((tr, cols), lambda i, chip_ref: (i, 0))),
        out_shape=_sds((rows, cols), F32), compiler_params=_params("parallel"),
    )(chip, own, landed, landed, landed)


SUBLANES = 8


def _tile_rows(n_elems):
    return -(-n_elems // (SUBLANES * LANE)) * SUBLANES


SMALL_ITEMS = (("b_ada", N_MOD * D_MODEL), ("pre_w_mix", D_MODEL), ("post_w_mix", D_MODEL), ("pre_w_mlp", D_MODEL),
               ("post_w_mlp", D_MODEL), ("attn_out_w", ATT_WIDTH), ("hg_norm_w", HG_HEAD_DIM), ("attn_sinks", ATT_Q_HEADS),
               ("lb_0", HG_WIDTH), ("lb_1", HG_WIDTH))
SMALL_AT = {}
for _name, _size in SMALL_ITEMS:
    SMALL_AT[_name] = (sum(r for _, r in SMALL_AT.values()), _tile_rows(_size))
SMALL_ROWS = sum(r for _, r in SMALL_AT.values())
MOD_ROWS = SMALL_AT["b_ada"][1]
PLAIN_ROWS = SMALL_AT["lb_0"][0] - MOD_ROWS
LB_ROWS = SMALL_AT["lb_0"][1]


def _rows(a, nrows=None):
    flat = a.reshape(-1)
    nrows = _tile_rows(flat.shape[0]) if nrows is None else nrows
    return jnp.pad(flat, (0, nrows * LANE - flat.shape[0])).reshape(nrows, LANE)


def _pack_small(vals):
    vals = dict(vals, lb_0=vals["lb_table"][0], lb_1=vals["lb_table"][1])
    return jnp.concatenate([_rows(vals[name], SMALL_AT[name][1]) for name, _ in SMALL_ITEMS], axis=0)


def _unpack_small(p):
    def item(name, shape):
        first = SMALL_AT[name][0]
        size = shape[0] * shape[1]
        return p[first:first + SMALL_AT[name][1]].reshape(-1)[:size].reshape(shape)

    out = {name: item(name, (1, size)) for name, size in SMALL_ITEMS if not name.startswith("lb_")}
    out["lb_table"] = jnp.concatenate([item("lb_0", (1, HG_WIDTH)), item("lb_1", (1, HG_WIDTH))], axis=0)
    return out


def _pack_partials(dmod, plain, d_lb, loss_row):
    return jnp.concatenate([_rows(dmod, dmod.shape[0] * MOD_ROWS)] + [_rows(g) for g in plain] + [_rows(d_lb), _rows(loss_row)], axis=0)


def _small_update(packs, w, m, v, n_seq):
    mod_end = n_seq * MOD_ROWS
    lb_at = mod_end + PLAIN_ROWS
    t0, t1 = SMALL_AT["lb_0"][0], SMALL_AT["lb_1"][0]

    def body(p_ref, w_ref, m_ref, v_ref, g_ref, dl_ref, nm_ref, nv_ref, loss_ref):
        tot = p_ref[0]
        for d in range(1, N_DEV):
            tot = tot + p_ref[d]
        wv = w_ref[...]
        p1 = _sigmoid(wv[t1:t1 + LB_ROWS] - wv[t0:t0 + LB_ROWS])
        s = tot[lb_at:lb_at + LB_ROWS] * p1 * (1.0 - p1)
        g_bias = tot[0:MOD_ROWS]
        for q in range(1, n_seq):
            g_bias = g_bias + tot[q * MOD_ROWS:(q + 1) * MOD_ROWS]
        g = jnp.concatenate([g_bias, tot[mod_end:lb_at], -s, s], axis=0)
        g_ref[...] = g
        dl_ref[...], nm_ref[...], nv_ref[...] = _adamw_math(g, wv, m_ref[...], v_ref[...])
        loss_ref[...] = tot[lb_at + LB_ROWS:lb_at + LB_ROWS + SUBLANES]

    shp = _sds((SMALL_ROWS, LANE), F32)
    return pl.pallas_call(body, name="small_update", out_shape=[shp] * 4 + [_sds((SUBLANES, LANE), F32)],
                          compiler_params=_params())(packs, w, m, v)


def kernel(x, c, w_ada, b_ada, pre_w_mix, w_in, attn_sinks, attn_out_w, lb_table, hg_norm_w, w_out, post_w_mix, pre_w_mlp, w_up, w_down, post_w_mlp, loss_target, m_w_ada, m_b_ada, m_pre_w_mix, m_w_in, m_attn_sinks, m_attn_out_w, m_lb_table, m_hg_norm_w, m_w_out, m_post_w_mix, m_pre_w_mlp, m_w_up, m_w_down, m_post_w_mlp, v_w_ada, v_b_ada, v_pre_w_mix, v_w_in, v_attn_sinks, v_attn_out_w, v_lb_table, v_hg_norm_w, v_w_out, v_post_w_mix, v_pre_w_mlp, v_w_up, v_w_down, v_post_w_mlp):
    xi, yi, ci = _mesh_pos()
    chip = 2 * xi + yi
    dev = 2 * chip + ci
    bsz, seq, _ = x.shape
    ntok = bsz * seq
    ada_cols = w_ada.shape[2]
    core = jnp.reshape(ci, (1,)).astype(jnp.int32)
    chip_idx = jnp.reshape(chip, (1,)).astype(jnp.int32)
    flat = lambda a: a.reshape(ntok, a.shape[-1])
    unflat = lambda a: a.reshape(bsz, seq, a.shape[-1])
    tables = _rope_tables(seq)

    def row_half(w):
        rows = w.shape[1] // 2
        return lax.dynamic_slice_in_dim(w[0], ci * rows, rows, axis=0).astype(BF16)

    def gather_buffer(w):
        rows, cols = w.shape[1] // 2, w.shape[2]
        own = w[0].astype(BF16).reshape(2, rows, cols)
        return lax.dynamic_update_slice(jnp.zeros((N_DEV, rows, cols), BF16), own, (2 * chip, 0, 0))

    w_in_t, m_in_t, v_in_t = [jnp.transpose(a[0])[None] for a in (w_in, m_w_in, v_w_in)]
    c_g, in_g = _allgather8([c, row_half(w_in_t)], "gather_first")
    c_all = c_g.reshape(N_DEV * bsz, D_MODEL)
    w_in_full = in_g.reshape(IN_COLS, D_MODEL)

    b_cols = lax.dynamic_slice_in_dim(b_ada, chip * ada_cols, ada_cols, axis=1)
    mod_part = _ada_fwd(c_all, w_ada[0], b_cols)
    half_rows = mod_part.shape[0] // 2
    (mod_g,) = _allgather8([lax.dynamic_slice_in_dim(mod_part, ci * half_rows, half_rows, axis=0)], "gather_mod")
    mod_all = mod_g.reshape(N_CHIPS, 2, half_rows, ada_cols).transpose(1, 2, 0, 3).reshape(N_DEV * bsz, N_MOD * D_MODEL)
    mod = lax.dynamic_slice_in_dim(mod_all, dev * bsz, bsz, axis=0)
    sh1, sc1, g1, sh2, sc2, g2 = [mod[:, i * D_MODEL:(i + 1) * D_MODEL].reshape(bsz, 1, D_MODEL) for i in range(N_MOD)]

    up_rows = w_up.shape[1] // 4
    first_half, second_half = (0, up_rows), (up_rows, up_rows)
    (h1, proj, qh, kh, vh), ((out_g,), (up_g,)) = _in_proj_fused(
        x, pre_w_mix, sc1, sh1, w_in_full, tables,
        comms=[_plan_chip_gather([row_half(w_out)], [gather_buffer(w_out)]),
               _plan_chip_gather([row_half(w_up)], [gather_buffer(w_up)], rows=first_half)])
    (attn_raw, cat, lse), ((up_g,), (out_g,)) = _attn_fwd(
        qh, kh, vh, attn_sinks, attn_out_w,
        comms=[_plan_chip_gather([row_half(w_up)], [up_g], rows=second_half, forward_rows=first_half), _plan_pair_forward([out_g])])
    (o_raw, cat, states), ((down_g,), (up_g,)) = _hgrn_fwd(
        proj, lb_table, hg_norm_w, cat,
        comms=[_plan_chip_gather([row_half(w_down)], [gather_buffer(w_down)]), _plan_pair_forward([up_g], rows=second_half)])
    w_out_full = out_g.reshape(D_MODEL, D_MODEL)
    w_up4 = up_g.reshape(N_CHIPS, D_MODEL, D_MODEL)
    mix, x1, h2 = _out_proj_fused(cat, w_out_full, x, post_w_mix, g1, pre_w_mlp, sc2, sh2)
    big_tm = min(ntok, 2048)
    up_spec = pl.BlockSpec((None, D_MODEL, D_MODEL), lambda i, j: (j, 0, 0))
    r, ((down_g,),) = _mm(flat(h2), w_up4, name="up_proj", out_dtype=BF16, tm=big_tm, tn=D_MODEL, n_out=D_FF, b_spec=up_spec,
                          epi=lambda acc: jnp.maximum(acc, 0.0), comms=[_plan_pair_forward([down_g])])
    w_down_full = down_g.reshape(D_FF, D_MODEL)
    square = lambda t: t * t
    loss_row, dy, dd, dg2, d_post_mlp = _down_proj_fused(unflat(r), w_down_full, x1, post_w_mlp, g2, loss_target)

    dpre = _mm(flat(dd), w_down_full, name="down_bwd", out_dtype=BF16, trans_b=True, tm=big_tm, tn=D_MODEL, extra=(r,),
               epi=lambda acc, rt: acc * (2.0 * rt.astype(F32)))
    half_rows = D_MODEL // 2
    g_down = _mm_tn(r, flat(dd), name="down_wgrad", tk=half_rows, tn=D_MODEL, a_fn=square,
                    out_shape=_sds((2, N_CHIPS, half_rows, D_MODEL), F32),
                    out_spec=pl.BlockSpec((None, None, half_rows, D_MODEL), lambda i, j: (i % 2, i // 2, 0, 0)))
    (dx1, dmix, dsc2, dsh2, dg1, d_pre_mlp, d_post_mix), ((q_down,),) = _up_bwd_fused(
        unflat(dpre), w_up4, dy, x1, mix, pre_w_mlp, sc2, post_w_mix, g1, comms=[_plan_pair([g_down], True)])
    g_up = _mm_tn(flat(h2), dpre, name="up_wgrad", tk=D_MODEL, tn=half_rows,
                  out_shape=_sds((2, N_CHIPS, half_rows, D_MODEL), F32),
                  out_spec=pl.BlockSpec((2, None, half_rows, half_rows), lambda i, j: (0, j // 2, 0, j % 2)))
    s_down = _pair_sum(g_down, q_down, core, "pair_sum_down")

    dcat, ((q_up,),) = _mm(flat(dmix), w_out_full, name="out_bwd", out_dtype=F32, trans_b=True, comms=[_plan_pair([g_up], True)])
    dcat = unflat(dcat)
    s_up = _pair_sum(g_up, q_up, core, "pair_sum_up")
    out_rows = D_MODEL // N_CHIPS
    g_out = _mm_tn(flat(cat), flat(dmix), name="out_wgrad", tk=out_rows, tn=half_rows,
                   out_shape=_sds((2, N_CHIPS, out_rows, half_rows), F32),
                   out_spec=pl.BlockSpec((None, None, out_rows, half_rows), lambda i, j: (j, i, 0, 0)))
    (dhq, dhf, dhi, dhg, d_lb, d_hg_norm), ((x_down,), (q_out,)) = _hgrn_bwd(
        dcat, proj, o_raw, states, lb_table, hg_norm_w, comms=[_plan_chip_exchange([s_down]), _plan_pair([g_out], True)])
    half_down = _sum_chips(s_down, x_down, chip_idx, "sum_chips_down")
    s_out = _pair_sum(g_out, q_out, core, "pair_sum_out")
    (dq, dkd, dkp, dvd, dvp, d_attn_out, d_sinks), ((their_down,), (x_up, x_out)) = _attn_bwd(
        dcat, attn_raw, attn_out_w, qh, kh, vh, lse, attn_sinks,
        comms=[_plan_pair([half_down], False), _plan_chip_exchange([s_up, s_out])])
    half_up = _sum_chips(s_up, x_up, chip_idx, "sum_chips_up")
    half_out = _sum_chips(s_out, x_out, chip_idx, "sum_chips_out")
    dproj_a = _rope_bwd(dq, dkd, dkp, dvd, dvp, tables)
    dproj = flat(jnp.concatenate([dproj_a, dhq, dhf, dhi, dhg], axis=-1))
    in_rows = IN_COLS // N_CHIPS // 2
    g_in = _mm_tn(dproj, flat(h1), name="in_wgrad", tk=2 * LANE, tn=D_MODEL).reshape(N_CHIPS, 2, in_rows, D_MODEL)
    dh1, ((q_in,), (their_up, their_out)) = _mm(
        dproj, w_in_full, name="in_bwd", out_dtype=F32,
        comms=[_plan_pair([g_in], "chip_major"), _plan_pair([half_up, half_out], False)])
    s_in = _pair_sum(g_in, q_in, core, "pair_sum_in", chip_major=True)
    grad_x, dsc1, dsh1, d_pre_mix = _norm1_bwd(unflat(dh1), dx1, x, pre_w_mix, sc1)

    dmod = jnp.concatenate([dsh1, dsc1, dg1, dsh2, dsc2, dg2], axis=-1).reshape(bsz, N_MOD * D_MODEL)
    pack = _pack_partials(dmod, [d_pre_mix, d_post_mix, d_pre_mlp, d_post_mlp, d_attn_out, d_hg_norm, d_sinks], d_lb, loss_row)
    (packs,), (x_in,) = _comm_only([_plan_allgather8([pack]), _plan_chip_exchange([s_in])], "gather_small")
    half_in = _sum_chips(s_in, x_in, chip_idx, "sum_chips_in")
    ((their_in,),) = _comm_only([_plan_pair([half_in], False)], "pair_swap_in")
    w_small = dict(b_ada=b_ada, pre_w_mix=pre_w_mix, post_w_mix=post_w_mix, pre_w_mlp=pre_w_mlp, post_w_mlp=post_w_mlp,
                   attn_out_w=attn_out_w, hg_norm_w=hg_norm_w, attn_sinks=attn_sinks, lb_table=lb_table)
    m_small = dict(b_ada=m_b_ada, pre_w_mix=m_pre_w_mix, post_w_mix=m_post_w_mix, pre_w_mlp=m_pre_w_mlp, post_w_mlp=m_post_w_mlp,
                   attn_out_w=m_attn_out_w, hg_norm_w=m_hg_norm_w, attn_sinks=m_attn_sinks, lb_table=m_lb_table)
    v_small = dict(b_ada=v_b_ada, pre_w_mix=v_pre_w_mix, post_w_mix=v_post_w_mix, pre_w_mlp=v_pre_w_mlp, post_w_mlp=v_post_w_mlp,
                   attn_out_w=v_attn_out_w, hg_norm_w=v_hg_norm_w, attn_sinks=v_attn_sinks, lb_table=v_lb_table)
    *small_packed, loss_rows = _small_update(packs, _pack_small(w_small), _pack_small(m_small), _pack_small(v_small), bsz)
    small_out = [_unpack_small(p) for p in small_packed]
    loss = loss_rows[0, 0]

    dmod_all = packs[:, :bsz * MOD_ROWS, :].reshape(N_DEV * bsz, N_MOD * D_MODEL)
    dmod_cols = lax.dynamic_slice_in_dim(dmod_all, chip * ada_cols, ada_cols, axis=1)
    ada_out = _ada_bwd_adamw(c_all, dmod_cols, w_ada[0], m_w_ada[0], v_w_ada[0])

    big = dict(
        w_in=tuple(jnp.transpose(a) for a in _adamw_halves(half_in, their_in, core, w_in_t[0], m_in_t[0], v_in_t[0], axis=0,
                                                           name="adamw_in")),
        w_up=tuple(_adamw_halves(half_up, their_up, core, w_up[0], m_w_up[0], v_w_up[0], axis=0, name="adamw_up")),
        w_out=tuple(_adamw_halves(half_out, their_out, core, w_out[0], m_w_out[0], v_w_out[0], axis=1, name="adamw_out")),
        w_down=tuple(_adamw_halves(half_down, their_down, core, w_down[0], m_w_down[0], v_w_down[0], axis=0, name="adamw_down")),
        w_ada=tuple(ada_out),
    )
    order = ("w_ada", "b_ada", "pre_w_mix", "w_in", "attn_sinks", "attn_out_w", "lb_table", "hg_norm_w", "w_out", "post_w_mix",
             "pre_w_mlp", "w_up", "w_down", "post_w_mlp")
    outs = [loss, grad_x]
    for kind in range(4):
        for nm in order:
            outs.append(big[nm][kind][None] if nm in big else small_out[kind][nm])
    return tuple(outs)
```

```python
import jax
import jax.numpy as jnp
from jax import lax
from jax.experimental import pallas as pl
from jax.experimental.pallas import tpu as pltpu

F32 = jnp.float32
BF16 = jnp.bfloat16

D_MODEL = 1024
ATT_WIDTH = 512
ATT_HEAD_DIM = 64
ATT_Q_HEADS = 8
ATT_KV_HEADS = 2
ATT_GROUP = ATT_Q_HEADS // ATT_KV_HEADS
ATT_KV_COLS = ATT_KV_HEADS * ATT_HEAD_DIM
WINDOW = 128
ROPE_DIM = 16
ROPE_THETA = 500000.0
HG_WIDTH = 512
MIX_WIDTH = ATT_WIDTH + HG_WIDTH
HG_HEAD_DIM = 128
HG_HEADS = 4
HG_CHUNK = 32
IN_COLS = ATT_WIDTH + 2 * ATT_KV_COLS + 4 * HG_WIDTH
ATT_COLS = ATT_WIDTH + 2 * ATT_KV_COLS
D_FF = 4 * D_MODEL
N_MOD = 6
EPS = 1e-6
ATT_SCALE = ATT_HEAD_DIM ** -0.5

ADAM_LR = 0.001
ADAM_B1 = 0.9
ADAM_B2 = 0.999
ADAM_EPS = 1e-08
ADAM_WD = 0.01
ADAM_STEP = 10

N_CHIPS = 4
N_DEV = 8
LANE = 128
VMEM_LIMIT = 48 * 1024 * 1024
VMEM_LIMIT_BIG = 58 * 1024 * 1024
MESH = pl.DeviceIdType.MESH

NT_DIMS = (((1,), (1,)), ((), ()))
TN_DIMS = (((0,), (0,)), ((), ()))


def _sds(shape, dtype):
    return jax.ShapeDtypeStruct(tuple(shape), dtype)


def _params(*sem, vmem_limit=None):
    return pltpu.CompilerParams(dimension_semantics=sem, vmem_limit_bytes=VMEM_LIMIT if vmem_limit is None else vmem_limit)


def _sigmoid(x):
    return 1.0 / (1.0 + jnp.exp(-x))


def _dot(a, b, dims=None):
    a, b = a.astype(BF16), b.astype(BF16)
    if dims is None:
        return jnp.dot(a, b, preferred_element_type=F32)
    return lax.dot_general(a, b, dims, preferred_element_type=F32)


def _rms_fwd(x, w):
    rstd = lax.rsqrt(jnp.mean(x * x, axis=-1, keepdims=True) + EPS)
    xh = x * rstd
    return xh * w, xh, rstd


def _rms_bwd(dy, xh, rstd, w):
    dxh = dy * w
    dx = rstd * (dxh - xh * jnp.mean(dxh * xh, axis=-1, keepdims=True))
    return dx, dy * xh


def _colsum(x):
    return jnp.sum(x, axis=0, keepdims=True)


def _row_tile(rows, cap=256):
    return max(t for t in range(16, cap + 1, 16) if rows % t == 0)


HBM_SPEC = pl.BlockSpec(memory_space=pltpu.HBM)


def _mesh_pos():
    return lax.axis_index("x"), lax.axis_index("y"), lax.axis_index("c")


class _Comm:
    def __init__(self, ins, outs, sems, start, finish, aliases=()):
        self.ins, self.outs, self.sems = list(ins), list(outs), list(sems)
        self.start, self.finish, self.aliases = start, finish, tuple(aliases)


def _call(body, args, *, name, grid, in_specs, out_specs, out_shape, sem, scratch_shapes=(), comms=(), aliases=None,
          vmem_limit=None):
    scratch_shapes = list(scratch_shapes)
    if not comms:
        return pl.pallas_call(body, name=name, grid=grid, in_specs=in_specs, out_specs=out_specs, out_shape=out_shape,
                              input_output_aliases=dict(aliases or {}), scratch_shapes=scratch_shapes,
                              compiler_params=_params(*sem, vmem_limit=vmem_limit))(*args)
    single = not isinstance(out_shape, (list, tuple))
    out_specs_l = [out_specs] if single else list(out_specs)
    out_shape_l = [out_shape] if single else list(out_shape)
    n_in, n_out, n_scr = len(in_specs), len(out_shape_l), len(scratch_shapes)
    n_ci = [len(cm.ins) for cm in comms]
    n_co = [len(cm.outs) for cm in comms]
    n_cs = [len(cm.sems) for cm in comms]
    aliases = dict(aliases or {})
    for k, cm in enumerate(comms):
        for i, o in cm.aliases:
            aliases[n_in + sum(n_ci[:k]) + i] = n_out + sum(n_co[:k]) + o

    def fused(*refs):
        pos = [0]

        def take(n):
            part = refs[pos[0]:pos[0] + n]
            pos[0] += n
            return part

        ins = take(n_in)
        c_ins = [take(n) for n in n_ci]
        outs = take(n_out)
        c_outs = [take(n) for n in n_co]
        scr = take(n_scr)
        c_sems = [take(n) for n in n_cs]
        first, last = True, True
        for d, size in enumerate(grid):
            first = jnp.logical_and(first, pl.program_id(d) == 0)
            last = jnp.logical_and(last, pl.program_id(d) == size - 1)

        def run(which):
            for cm, ci, co, cs in zip(comms, c_ins, c_outs, c_sems):
                getattr(cm, which)(ci, co, cs)

        if grid:
            pl.when(first)(lambda: run("start"))
        else:
            run("start")
        body(*ins, *outs, *scr)
        if grid:
            pl.when(last)(lambda: run("finish"))
        else:
            run("finish")

    res = pl.pallas_call(
        fused, name=name, grid=grid, in_specs=list(in_specs) + [HBM_SPEC] * sum(n_ci),
        out_specs=out_specs_l + [HBM_SPEC] * sum(n_co), out_shape=out_shape_l + [s for cm in comms for s in cm.outs],
        input_output_aliases=aliases, scratch_shapes=scratch_shapes + [s for cm in comms for s in cm.sems],
        compiler_params=_params(*["arbitrary"] * len(grid), vmem_limit=vmem_limit),
    )(*args, *[a for cm in comms for a in cm.ins])
    main = res[:n_out]
    extra, at = [], n_out
    for n in n_co:
        extra.append(list(res[at:at + n]))
        at += n
    return (main[0] if single else list(main)), extra


def _mm(a, b, *, name, out_dtype, trans_b=False, tm=512, tn=None, extra=(), epi=None, b_spec=None, n_out=None, comms=()):
    m_total, k_total = a.shape
    if n_out is None:
        n_out = b.shape[0] if trans_b else b.shape[1]
    tn = n_out if tn is None else tn
    grid = (m_total // tm, n_out // tn)
    dims = NT_DIMS if trans_b else None

    def body(*refs):
        a_ref, b_ref = refs[0], refs[1]
        extra_refs = refs[2:2 + len(extra)]
        o_ref = refs[2 + len(extra)]
        acc = _dot(a_ref[...], b_ref[...], dims)
        if epi is not None:
            acc = epi(acc, *[r[...] for r in extra_refs])
        o_ref[...] = acc.astype(out_dtype)

    if b_spec is None:
        if trans_b:
            b_spec = pl.BlockSpec((tn, k_total), lambda i, j: (j, 0))
        else:
            b_spec = pl.BlockSpec((k_total, tn), lambda i, j: (0, j))
    in_specs = [pl.BlockSpec((tm, k_total), lambda i, j: (i, 0)), b_spec]
    in_specs += [pl.BlockSpec((tm, tn), lambda i, j: (i, j)) for _ in extra]
    return _call(
        body, (a, b, *extra), name=name, grid=grid, in_specs=in_specs,
        out_specs=pl.BlockSpec((tm, tn), lambda i, j: (i, j)),
        out_shape=_sds((m_total, n_out), out_dtype),
        sem=("parallel", "parallel"), comms=comms)


def _mm_tn(a, b, *, name, tk, tn, a_fn=None, out_shape=None, out_spec=None):
    m_total, k_total = a.shape
    n_total = b.shape[1]
    grid = (k_total // tk, n_total // tn)

    def body(a_ref, b_ref, o_ref):
        av = a_ref[...]
        part = _dot(av if a_fn is None else a_fn(av), b_ref[...], TN_DIMS)
        o_ref[...] = part.reshape(o_ref.shape)

    if out_shape is None:
        out_shape = _sds((k_total, n_total), F32)
        out_spec = pl.BlockSpec((tk, tn), lambda i, j: (i, j))
    return pl.pallas_call(
        body, name=name, grid=grid,
        in_specs=[pl.BlockSpec((m_total, tk), lambda i, j: (0, i)), pl.BlockSpec((m_total, tn), lambda i, j: (0, j))],
        out_specs=out_spec, out_shape=out_shape,
        compiler_params=_params("parallel", "parallel"),
    )(a, b)


def _ada_fwd(c_all, w_shard, b_shard):
    nb, ncol = c_all.shape[0], w_shard.shape[1]
    tn = 512

    def body(c_ref, w_ref, b_ref, o_ref):
        c = c_ref[...]
        o_ref[...] = _dot(c * _sigmoid(c), w_ref[...]) + b_ref[...]

    return pl.pallas_call(
        body, name="ada_fwd", grid=(ncol // tn,),
        in_specs=[pl.BlockSpec((nb, D_MODEL), lambda j: (0, 0)), pl.BlockSpec((D_MODEL, tn), lambda j: (0, j)),
                  pl.BlockSpec((1, tn), lambda j: (0, j))],
        out_specs=pl.BlockSpec((nb, tn), lambda j: (0, j)), out_shape=_sds((nb, ncol), F32),
        compiler_params=_params("parallel"),
    )(c_all, w_shard, b_shard)


def _adamw_math(g, w, m, v):
    m = ADAM_B1 * m + (1.0 - ADAM_B1) * g
    v = ADAM_B2 * v + (1.0 - ADAM_B2) * (g * g)
    m_hat = m / (1.0 - ADAM_B1 ** ADAM_STEP)
    v_hat = v / (1.0 - ADAM_B2 ** ADAM_STEP)
    delta = -ADAM_LR * (m_hat / (jnp.sqrt(v_hat) + ADAM_EPS) + ADAM_WD * w)
    return delta, m, v


def _ada_bwd_adamw(c_all, dmod_cols, w, m, v):
    nb, ncol = dmod_cols.shape
    tn = 256

    def body(c_ref, d_ref, w_ref, m_ref, v_ref, g_ref, dl_ref, nm_ref, nv_ref):
        c = c_ref[...]
        g = _dot(c * _sigmoid(c), d_ref[...], TN_DIMS)
        g_ref[...] = g
        dl_ref[...], nm_ref[...], nv_ref[...] = _adamw_math(g, w_ref[...], m_ref[...], v_ref[...])

    col = pl.BlockSpec((D_MODEL, tn), lambda j: (0, j))
    shp = _sds((D_MODEL, ncol), F32)
    return pl.pallas_call(
        body, name="ada_bwd_adamw", grid=(ncol // tn,),
        in_specs=[pl.BlockSpec((nb, D_MODEL), lambda j: (0, 0)), pl.BlockSpec((nb, tn), lambda j: (0, j)), col, col, col],
        out_specs=[col, col, col, col], out_shape=[shp, shp, shp, shp],
        compiler_params=_params("parallel"),
    )(c_all, dmod_cols, w, m, v)


def _adamw_halves(own, theirs, core, w, m, v, *, axis, name):
    r2, c2 = own.shape
    tr = _row_tile(r2)
    nt = r2 // tr

    def body(core_ref, own_ref, their_ref, w_ref, m_ref, v_ref, g_ref, dl_ref, nm_ref, nv_ref):
        g = jnp.where(pl.program_id(0) == core_ref[0], own_ref[...], their_ref[...])
        g_ref[...] = g
        dl_ref[...], nm_ref[...], nv_ref[...] = _adamw_math(g, w_ref[...], m_ref[...], v_ref[...])

    if axis == 0:
        full = pl.BlockSpec((tr, c2), lambda h, i, core_ref: (h * nt + i, 0))
    else:
        full = pl.BlockSpec((tr, c2), lambda h, i, core_ref: (i, h))
    half = pl.BlockSpec((tr, c2), lambda h, i, core_ref: (i, 0))
    shp = _sds(w.shape, F32)
    return pl.pallas_call(
        body, name=name,
        grid_spec=pltpu.PrefetchScalarGridSpec(num_scalar_prefetch=1, grid=(2, nt), in_specs=[half, half, full, full, full],
                                               out_specs=[full] * 4),
        out_shape=[shp] * 4, compiler_params=_params("parallel", "parallel"),
    )(core, own, theirs, w, m, v)


def _tok_spec(tm, width=D_MODEL):
    return pl.BlockSpec((None, tm, width), lambda b, i: (b, i, 0))


def _row_spec(width=D_MODEL):
    return pl.BlockSpec((None, 1, width), lambda b, i: (b, 0, 0))


def _vec_spec(width=D_MODEL):
    return pl.BlockSpec((1, width), lambda b, i: (0, 0))


class _RowsOf:
    def __init__(self, ref, first, count):
        self.ref, self.rows = ref, slice(first, first + count)

    def __getitem__(self, idx):
        return self.ref[self.rows, :]

    def __setitem__(self, idx, value):
        self.ref[self.rows, :] = value


def _mm_rows(a, b, *, name, tm, extra, extra_specs, out_specs, out_shape, epi, pro=None, trans_b=False, b_chunks=1, comms=(),
             parts=1, zero_per_seq=(), zero_once=(), vmem_limit=None):
    bsz, seq, k_total = a.shape
    kc = k_total // b_chunks
    dims = NT_DIMS if trans_b else None
    rows = tm // parts

    def body(*refs):
        a_ref, b_ref = refs[0], refs[1]
        ex, outs = refs[2:2 + len(extra)], refs[2 + len(extra):]
        if zero_per_seq:
            @pl.when(pl.program_id(1) == 0)
            def _():
                for k in zero_per_seq:
                    outs[k][...] = jnp.zeros_like(outs[k])
        if zero_once:
            @pl.when(jnp.logical_and(pl.program_id(0) == 0, pl.program_id(1) == 0))
            def _():
                for k in zero_once:
                    outs[k][...] = jnp.zeros_like(outs[k])

        def part_of(ref, p):
            tiled = len(ref.shape) == 2 and ref.shape[0] == tm
            return _RowsOf(ref, p * rows, rows) if tiled and parts > 1 else ref

        accs = []
        for p in range(parts):
            a_p, ex_p, outs_p = part_of(a_ref, p), [part_of(r, p) for r in ex], [part_of(r, p) for r in outs]
            if b_chunks == 1:
                accs.append(_dot(a_p[...] if pro is None else pro(a_p, ex_p, outs_p), b_ref[...], dims))
            else:
                acc = _dot(a_p[...][:, 0:kc], b_ref[0], NT_DIMS)
                for k in range(1, b_chunks):
                    acc = acc + _dot(a_p[...][:, k * kc:(k + 1) * kc], b_ref[k], NT_DIMS)
                accs.append(acc)
        for p in range(parts):
            epi(accs[p], [part_of(r, p) for r in ex], [part_of(r, p) for r in outs])

    b_spec = pl.BlockSpec(b.shape, lambda bb, i: (0,) * b.ndim)
    return _call(
        body, (a, b, *extra), name=name, grid=(bsz, seq // tm), in_specs=[_tok_spec(tm, k_total), b_spec, *extra_specs],
        out_specs=out_specs, out_shape=out_shape, sem=("arbitrary", "arbitrary"), comms=comms, vmem_limit=vmem_limit)


def _in_proj_fused(x, w, sc, sh, w_in_t, tables, comms=()):
    tm = 512
    bsz, seq, _ = x.shape
    half = ROPE_DIM // 2
    heads_per_slab = LANE // ATT_HEAD_DIM

    def pro(x_ref, ex, outs):
        y, _, _ = _rms_fwd(x_ref[...], ex[0][...])
        h = (y * (1.0 + ex[1][...]) + ex[2][...]).astype(BF16)
        outs[0][...] = h
        return h

    def epi(acc, ex, outs):
        c, u, d = ex[3][...], ex[4][...], ex[5][...]
        _, proj_ref, q_ref, k_ref, v_ref = outs
        proj_ref[...] = acc

        def rope(z):
            return (z * c + pltpu.roll(z, half, 1) * u + pltpu.roll(z, LANE - half, 1) * d).astype(BF16)

        for s in range(ATT_WIDTH // LANE):
            slab = rope(acc[:, s * LANE:(s + 1) * LANE])
            for part in range(heads_per_slab):
                g, hh = divmod(s * heads_per_slab + part, ATT_GROUP)
                piece = slab[:, part * ATT_HEAD_DIM:(part + 1) * ATT_HEAD_DIM]
                for blk in range(tm // WINDOW):
                    q_ref[blk, g, hh * WINDOW:(hh + 1) * WINDOW, :] = piece[blk * WINDOW:(blk + 1) * WINDOW]
        rk = rope(acc[:, ATT_WIDTH:ATT_WIDTH + LANE])
        vv = acc[:, ATT_WIDTH + LANE:ATT_COLS].astype(BF16)
        for g in range(ATT_KV_HEADS):
            k_ref[g] = rk[:, g * ATT_HEAD_DIM:(g + 1) * ATT_HEAD_DIM]
            v_ref[g] = vv[:, g * ATT_HEAD_DIM:(g + 1) * ATT_HEAD_DIM]

    cols = w_in_t.shape[0]
    tab = pl.BlockSpec((tm, LANE), lambda b, i: (i, 0))
    kv_spec = pl.BlockSpec((None, ATT_KV_HEADS, tm, ATT_HEAD_DIM), lambda b, i: (b, 0, i, 0))
    kv_shape = _sds((bsz, ATT_KV_HEADS, seq, ATT_HEAD_DIM), BF16)
    q_spec = pl.BlockSpec((None, tm // WINDOW, ATT_KV_HEADS, GROUP_ROWS, ATT_HEAD_DIM), lambda b, i: (b, i, 0, 0, 0))
    return _mm_rows(x, w_in_t, name="in_proj", tm=tm, extra=(w, sc, sh, *tables),
                    extra_specs=[_vec_spec(), _row_spec(), _row_spec(), tab, tab, tab],
                    out_specs=[_tok_spec(tm), _tok_spec(tm, cols), q_spec, kv_spec, kv_spec],
                    out_shape=[_sds(x.shape, BF16), _sds((bsz, seq, cols), F32),
                               _sds((bsz, seq // WINDOW, ATT_KV_HEADS, GROUP_ROWS, ATT_HEAD_DIM), BF16), kv_shape, kv_shape],
                    pro=pro, epi=epi, trans_b=True, comms=comms)


def _rope_tables(seq):
    half = ROPE_DIM // 2
    inv_freq = ROPE_THETA ** (-jnp.arange(0, ROPE_DIM, 2, dtype=F32) / ROPE_DIM)
    ang = jnp.arange(seq, dtype=F32)[:, None] * inv_freq[None, :]
    cos, sin = jnp.cos(ang), jnp.sin(ang)
    rest = ATT_HEAD_DIM - ROPE_DIM
    ones, zeros, zh = jnp.ones((seq, rest), F32), jnp.zeros((seq, rest), F32), jnp.zeros((seq, half), F32)
    reps = LANE // ATT_HEAD_DIM
    t_cos = jnp.tile(jnp.concatenate([cos, cos, ones], axis=1), (1, reps))
    t_up = jnp.tile(jnp.concatenate([zh, sin, zeros], axis=1), (1, reps))
    t_dn = jnp.tile(jnp.concatenate([-sin, zh, zeros], axis=1), (1, reps))
    return t_cos, t_up, t_dn


GROUP_ROWS = ATT_GROUP * WINDOW


ATT_BPS = 2


def _band_mask(has_prev):
    row = lax.broadcasted_iota(jnp.int32, (GROUP_ROWS, 2 * WINDOW), 0) % WINDOW
    col = lax.broadcasted_iota(jnp.int32, (GROUP_ROWS, 2 * WINDOW), 1)
    prev = jnp.logical_and(jnp.logical_and(col < WINDOW, col > row), has_prev)
    return jnp.logical_or(prev, jnp.logical_and(col >= WINDOW, col - WINDOW <= row))


def _sink_column(sink_ref, g):
    head = lax.broadcasted_iota(jnp.int32, (GROUP_ROWS, 1), 0) // WINDOW
    col = jnp.full((GROUP_ROWS, 1), sink_ref[0, g * ATT_GROUP], F32)
    for hh in range(1, ATT_GROUP):
        col = jnp.where(head == hh, sink_ref[0, g * ATT_GROUP + hh], col)
    return col


def _attn_specs():
    q_spec = pl.BlockSpec((None, ATT_BPS, ATT_KV_HEADS, GROUP_ROWS, ATT_HEAD_DIM), lambda b, i: (b, i, 0, 0, 0))
    kv_cur = pl.BlockSpec((None, ATT_KV_HEADS, ATT_BPS * WINDOW, ATT_HEAD_DIM), lambda b, i: (b, 0, i, 0))
    kv_prev = pl.BlockSpec((None, ATT_KV_HEADS, WINDOW, ATT_HEAD_DIM), lambda b, i: (b, 0, jnp.maximum(ATT_BPS * i - 1, 0), 0))
    return q_spec, kv_cur, kv_prev


def _band(prev_ref, cur_ref, g, blk):
    own = cur_ref[g, blk * WINDOW:(blk + 1) * WINDOW]
    before = prev_ref[g] if blk == 0 else cur_ref[g, (blk - 1) * WINDOW:blk * WINDOW]
    return jnp.concatenate([before, own], axis=0)


def _attn_fwd(qh, kh, vh, sinks, w_norm, comms=()):
    bsz, nblk = qh.shape[0], qh.shape[1]
    seq = nblk * WINDOW
    rows = ATT_BPS * WINDOW
    neg = float(jnp.finfo(jnp.float32).min)

    def body(sink_ref, q_ref, kc_ref, kp_ref, vc_ref, vp_ref, w_ref, raw_ref, an_ref, l_ref):
        for blk in range(ATT_BPS):
            mask = _band_mask(True if blk else pl.program_id(1) > 0)
            for g in range(ATT_KV_HEADS):
                keys, vals = _band(kp_ref, kc_ref, g, blk), _band(vp_ref, vc_ref, g, blk)
                sink = _sink_column(sink_ref, g)
                s = jnp.where(mask, _dot(q_ref[blk, g], keys, NT_DIMS) * ATT_SCALE, neg)
                m = jnp.maximum(jnp.max(s, axis=-1, keepdims=True), sink)
                p = jnp.where(mask, jnp.exp(s - m), 0.0)
                den = jnp.sum(p, axis=-1, keepdims=True) + jnp.exp(sink - m)
                o = _dot(p / den, vals)
                lse = m + jnp.log(den)
                tok = slice(blk * WINDOW, (blk + 1) * WINDOW)
                for hh in range(ATT_GROUP):
                    h = g * ATT_GROUP + hh
                    raw_ref[tok, h * ATT_HEAD_DIM:(h + 1) * ATT_HEAD_DIM] = o[hh * WINDOW:(hh + 1) * WINDOW]
                    l_ref[tok, h:h + 1] = lse[hh * WINDOW:(hh + 1) * WINDOW]
        y, _, _ = _rms_fwd(raw_ref[...], w_ref[...])
        an_ref[...] = y.astype(BF16)

    cur = lambda width: pl.BlockSpec((None, rows, width), lambda b, i: (b, i, 0))
    q_spec, kv_cur, kv_prev = _attn_specs()
    return _call(
        body, (sinks, qh, kh, kh, vh, vh, w_norm), name="attn_fwd", grid=(bsz, nblk // ATT_BPS),
        in_specs=[pl.BlockSpec(memory_space=pltpu.SMEM), q_spec, kv_cur, kv_prev, kv_cur, kv_prev, _vec_spec(ATT_WIDTH)],
        out_specs=[cur(ATT_WIDTH), cur(ATT_WIDTH), cur(ATT_Q_HEADS)],
        out_shape=[_sds((bsz, seq, ATT_WIDTH), F32), _sds((bsz, seq, MIX_WIDTH), BF16), _sds((bsz, seq, ATT_Q_HEADS), F32)],
        sem=("parallel", "parallel"), comms=comms)


HG_Q0 = ATT_COLS // LANE
HG_F0 = HG_Q0 + HG_HEADS
HG_I0 = HG_F0 + HG_HEADS
HG_G0 = HG_I0 + HG_HEADS
HG_TOK = 256
HG_NCH = HG_TOK // HG_CHUNK
HG_HPS = 2


def _block_masks():
    row = lax.broadcasted_iota(jnp.int32, (HG_TOK, HG_TOK), 0)
    col = lax.broadcasted_iota(jnp.int32, (HG_TOK, HG_TOK), 1)
    same = (row // HG_CHUNK) == (col // HG_CHUNK)
    return jnp.logical_and(same, col <= row), jnp.logical_and(same, col >= row)


def _row_in_chunk():
    return lax.broadcasted_iota(jnp.int32, (HG_TOK, LANE), 0) % HG_CHUNK


def _chunk_cumsum(x, reverse=False):
    ric = _row_in_chunk()
    shift = 1
    while shift < HG_CHUNK:
        if reverse:
            x = x + jnp.where(ric < HG_CHUNK - shift, pltpu.roll(x, HG_TOK - shift, 0), 0.0)
        else:
            x = x + jnp.where(ric >= shift, pltpu.roll(x, shift, 0), 0.0)
        shift *= 2
    return x


def _chunk_rows(rows):
    stacked = jnp.concatenate([r[None] for r in rows], axis=0)
    return jnp.broadcast_to(stacked, (HG_NCH, HG_CHUNK, LANE)).reshape(HG_TOK, LANE)


def _chunk_slices(x):
    return [x[j * HG_CHUNK:(j + 1) * HG_CHUNK] for j in range(HG_NCH)]


def _hgrn_common(tbl, hf, hq):
    lb = _sigmoid(tbl[1:2] - tbl[0:1])
    sig = _sigmoid(hf)
    f = lb + (1.0 - lb) * sig
    sq = _sigmoid(hq)
    q, k = hq * sq, 1.0 - f
    b = _chunk_cumsum(jnp.log(f))
    last = [b[(j + 1) * HG_CHUNK - 1:(j + 1) * HG_CHUNK] for j in range(HG_NCH)]
    bl = _chunk_rows(last)
    e_b, e_nb, e_rem = jnp.exp(b), jnp.exp(-b), jnp.exp(bl - b)
    e_last = [jnp.exp(r) for r in last]
    return dict(lb=lb, sig=sig, f=f, sq=sq, q=q, k=k, e_b=e_b, e_nb=e_nb, e_rem=e_rem, e_last=e_last,
                qd=q * e_b, kd=k * e_nb, ku=k * e_rem)


def _hgrn_fwd(proj, lb_table, norm_w, mix_in, comms=()):
    bsz, seq, _ = proj.shape
    nstep = seq // HG_TOK

    def body(tbl_ref, nw_ref, q_ref, f_ref, i_ref, g_ref, mix_ref, o_ref, rec_ref, st_ref, s_scr):
        @pl.when(pl.program_id(2) == 0)
        def _():
            s_scr[...] = jnp.zeros_like(s_scr)

        lower, _ = _block_masks()
        for hp in range(HG_HPS):
            ls = slice(hp * LANE, (hp + 1) * LANE)
            v, hg = i_ref[:, ls], g_ref[:, ls]
            t = _hgrn_common(tbl_ref[:, ls], f_ref[:, ls], q_ref[:, ls])
            a = jnp.where(lower, _dot(t["qd"], t["kd"], NT_DIMS), 0.0)
            o_intra = _dot(a, v)
            v_c, ku_c, qd_c = [_chunk_slices(z.astype(BF16)) for z in (v, t["ku"], t["qd"])]
            updates = [_dot(v_c[j], ku_c[j], TN_DIMS) for j in range(HG_NCH)]
            st = s_scr[hp]
            states = []
            for j in range(HG_NCH):
                states.append(st)
                st = st * t["e_last"][j] + updates[j]
            s_scr[hp] = st
            o = o_intra + jnp.concatenate([_dot(qd_c[j], states[j], NT_DIMS) for j in range(HG_NCH)], axis=0)
            for j in range(HG_NCH):
                st_ref[hp, j] = states[j]
            o_ref[:, ls] = o
            y, _, _ = _rms_fwd(o, nw_ref[...])
            rec_ref[:, ls] = (y * (hg * _sigmoid(hg))).astype(BF16)

    width = HG_HPS * LANE
    slab = lambda first: pl.BlockSpec((None, HG_TOK, width), lambda b, h, t: (b, t, first // HG_HPS + h))
    head_out = pl.BlockSpec((None, HG_TOK, width), lambda b, h, t: (b, t, h))
    mix_out = pl.BlockSpec((None, HG_TOK, width), lambda b, h, t: (b, t, ATT_WIDTH // width + h))
    return _call(
        body, (lb_table, norm_w, proj, proj, proj, proj, mix_in), name="hgrn_fwd", grid=(bsz, HG_HEADS // HG_HPS, nstep),
        in_specs=[pl.BlockSpec((2, width), lambda b, h, t: (0, h)), pl.BlockSpec((1, LANE), lambda b, h, t: (0, 0)),
                  slab(HG_Q0), slab(HG_F0), slab(HG_I0), slab(HG_G0), pl.BlockSpec(memory_space=pl.ANY)],
        out_specs=[head_out, mix_out,
                   pl.BlockSpec((None, HG_HPS, HG_NCH, LANE, LANE), lambda b, h, t: (b, h, t, 0, 0))],
        out_shape=[_sds((bsz, seq, HG_WIDTH), F32), _sds(mix_in.shape, BF16),
                   _sds((bsz, HG_HEADS, seq // HG_CHUNK, LANE, LANE), F32)],
        scratch_shapes=[pltpu.VMEM((HG_HPS, LANE, LANE), F32)],
        sem=("parallel", "parallel", "arbitrary"), comms=comms, aliases={6: 1})


def _out_proj_fused(cat, w_out, x, post_w, g1, pre_w, sc2, sh2):
    tm = 512

    def epi(mix, ex, outs):
        x_ref, pw_ref, g1_ref, w2_ref, sc_ref, sh_ref = ex
        outs[0][...] = mix
        n1, _, _ = _rms_fwd(mix, pw_ref[...])
        x1 = x_ref[...] + g1_ref[...] * n1
        outs[1][...] = x1
        y2, _, _ = _rms_fwd(x1, w2_ref[...])
        outs[2][...] = (y2 * (1.0 + sc_ref[...]) + sh_ref[...]).astype(BF16)

    return _mm_rows(cat, w_out, name="out_proj", tm=tm, extra=(x, post_w, g1, pre_w, sc2, sh2),
                    extra_specs=[_tok_spec(tm), _vec_spec(), _row_spec(), _vec_spec(), _row_spec(), _row_spec()],
                    out_specs=[_tok_spec(tm), _tok_spec(tm), _tok_spec(tm)],
                    out_shape=[_sds(x.shape, F32), _sds(x.shape, F32), _sds(x.shape, BF16)], epi=epi)


def _acc_out(ref, first, value):
    @pl.when(first)
    def _():
        ref[...] = value

    @pl.when(jnp.logical_not(first))
    def _():
        ref[...] += value


def _down_proj_fused(r, w_down, x1, post_w, g2, target):
    tm = 512
    bsz = x1.shape[0]

    def pro(r_ref, ex, outs):
        rv = r_ref[...]
        return rv * rv

    def epi(down, ex, outs):
        x1_ref, w_ref, g2_ref, t_ref = ex
        loss_ref, dy_ref, dd_ref, dg2_ref, dw_ref = outs
        w, g2v = w_ref[...], g2_ref[...]
        n2, dh, rstd = _rms_fwd(down, w)
        err = x1_ref[...] + g2v * n2 - t_ref[...]
        part = (0.5 / D_MODEL) * jnp.sum(jnp.sum(err * err, axis=-1, keepdims=True), axis=0, keepdims=True)
        loss_ref[...] += jnp.broadcast_to(part, (1, LANE))
        dy = err * (1.0 / D_MODEL)
        dy_ref[...] = dy
        dg2_ref[...] += _colsum(dy * n2)
        dd, dw_rows = _rms_bwd(dy * g2v, dh, rstd, w)
        dd_ref[...] = dd.astype(BF16)
        dw_ref[...] += _colsum(dw_rows)

    return _mm_rows(r, w_down, name="down_proj", tm=tm, extra=(x1, post_w, g2, target),
                    extra_specs=[_tok_spec(tm), _vec_spec(), _row_spec(), _tok_spec(tm)],
                    out_specs=[_vec_spec(LANE), _tok_spec(tm), _tok_spec(tm), _row_spec(), _vec_spec()],
                    out_shape=[_sds((1, LANE), F32), _sds(x1.shape, F32), _sds(x1.shape, BF16), _sds((bsz, 1, D_MODEL), F32),
                               _sds((1, D_MODEL), F32)], pro=pro, epi=epi, parts=2, zero_per_seq=(3,), zero_once=(0, 4),
                    vmem_limit=VMEM_LIMIT_BIG)


def _up_bwd_fused(dpre, w_up4, dy, x1, mix, pre_w, sc2, post_w, g1, comms=()):
    tm = 512
    bsz = x1.shape[0]

    def epi(dh2v, ex, outs):
        dy_ref, x1_ref, mix_ref, w2_ref, sc_ref, pw_ref, g1_ref = ex
        dx1_ref, dmix_ref, dsc_ref, dsh_ref, dg1_ref, dw2_ref, dpw_ref = outs
        w2, pw = w2_ref[...], pw_ref[...]
        y2, xh2, rstd2 = _rms_fwd(x1_ref[...], w2)
        dsh_ref[...] += _colsum(dh2v)
        dsc_ref[...] += _colsum(dh2v * y2)
        dx1n, dw_rows = _rms_bwd(dh2v * (1.0 + sc_ref[...]), xh2, rstd2, w2)
        dw2_ref[...] += _colsum(dw_rows)
        dx1 = dy_ref[...] + dx1n
        dx1_ref[...] = dx1
        n1, mh, rstd1 = _rms_fwd(mix_ref[...], pw)
        dg1_ref[...] += _colsum(dx1 * n1)
        dmix, dpw_rows = _rms_bwd(dx1 * g1_ref[...], mh, rstd1, pw)
        dmix_ref[...] = dmix.astype(BF16)
        dpw_ref[...] += _colsum(dpw_rows)

    row_shape = _sds((bsz, 1, D_MODEL), F32)
    vec_shape = _sds((1, D_MODEL), F32)
    return _mm_rows(dpre, w_up4, name="up_bwd", tm=tm, extra=(dy, x1, mix, pre_w, sc2, post_w, g1),
                    extra_specs=[_tok_spec(tm), _tok_spec(tm), _tok_spec(tm), _vec_spec(), _row_spec(), _vec_spec(), _row_spec()],
                    out_specs=[_tok_spec(tm), _tok_spec(tm), _row_spec(), _row_spec(), _row_spec(), _vec_spec(), _vec_spec()],
                    out_shape=[_sds(x1.shape, F32), _sds(x1.shape, BF16), row_shape, row_shape, row_shape, vec_shape, vec_shape],
                    epi=epi, b_chunks=w_up4.shape[0], comms=comms, parts=2, zero_per_seq=(2, 3, 4), zero_once=(5, 6),
                    vmem_limit=VMEM_LIMIT_BIG)


def _norm1_bwd(dh1, dx1, x, pre_w, sc1, tm=512):
    bsz, seq, _ = x.shape

    def body(dh_ref, dx1_ref, x_ref, w_ref, sc_ref, gx_ref, dsc_ref, dsh_ref, dw_ref):
        b, i = pl.program_id(0), pl.program_id(1)
        w = w_ref[...]
        dh = dh_ref[...]
        y, xh, rstd = _rms_fwd(x_ref[...], w)
        _acc_out(dsh_ref, i == 0, _colsum(dh))
        _acc_out(dsc_ref, i == 0, _colsum(dh * y))
        dx, dw_rows = _rms_bwd(dh * (1.0 + sc_ref[...]), xh, rstd, w)
        _acc_out(dw_ref, jnp.logical_and(b == 0, i == 0), _colsum(dw_rows))
        gx_ref[...] = dx1_ref[...] + dx

    row_shape = _sds((bsz, 1, D_MODEL), F32)
    return pl.pallas_call(
        body, name="norm1_bwd", grid=(bsz, seq // tm),
        in_specs=[_tok_spec(tm), _tok_spec(tm), _tok_spec(tm), _vec_spec(), _row_spec()],
        out_specs=[_tok_spec(tm), _row_spec(), _row_spec(), _vec_spec()],
        out_shape=[_sds(x.shape, F32), row_shape, row_shape, _sds((1, D_MODEL), F32)],
        compiler_params=_params("arbitrary", "arbitrary"),
    )(dh1, dx1, x, pre_w, sc1)


def _hgrn_bwd(dcat, proj, o_raw, states, lb_table, norm_w, comms=()):
    bsz, seq, _ = proj.shape
    nstep = seq // HG_TOK
    rec0 = ATT_WIDTH // LANE

    def body(tbl_ref, nw_ref, dr_ref, q_ref, f_ref, i_ref, g_ref, o_ref, st_ref,
             dq_ref, df_ref, di_ref, dg_ref, dlb_ref, dnw_ref, ds_scr):
        h, b, t = pl.program_id(0), pl.program_id(1), pl.program_id(2)

        @pl.when(t == 0)
        def _():
            ds_scr[...] = jnp.zeros_like(ds_scr)

        lower, upper = _block_masks()
        dlb_parts = []
        dnw_acc = jnp.zeros((1, LANE), F32)
        for hp in range(HG_HPS):
            ls = slice(hp * LANE, (hp + 1) * LANE)
            hq, v, hg = q_ref[:, ls], i_ref[:, ls], g_ref[:, ls]
            nw = nw_ref[...]
            c = _hgrn_common(tbl_ref[:, ls], f_ref[:, ls], hq)
            qd, kd, ku = c["qd"], c["kd"], c["ku"]
            y, on, rstd = _rms_fwd(o_ref[:, ls], nw)
            sg = _sigmoid(hg)
            dr = dr_ref[:, ls]
            dg_ref[:, ls] = (dr * y * (sg * (1.0 + hg * (1.0 - sg)))).astype(BF16)
            do, dnw_rows = _rms_bwd(dr * (hg * sg), on, rstd, nw)
            at = jnp.where(upper, _dot(kd, qd, NT_DIMS), 0.0)
            da = jnp.where(lower, _dot(do, v, NT_DIMS), 0.0)
            dat = jnp.where(upper, _dot(v, do, NT_DIMS), 0.0)
            dv = _dot(at, do)
            dqd = _dot(da, kd)
            dkd = _dot(dat, qd)
            do_c, qd_c, v_c, ku_c = [_chunk_slices(z.astype(BF16)) for z in (do, qd, v, ku)]
            outer = [_dot(do_c[j], qd_c[j], TN_DIMS) for j in range(HG_NCH)]
            ds = ds_scr[hp]
            ds_after = [None] * HG_NCH
            for j in reversed(range(HG_NCH)):
                ds_after[j] = ds
                ds = outer[j] + ds * c["e_last"][j]
            ds_scr[hp] = ds
            states = [st_ref[hp, j] for j in range(HG_NCH)]
            dv = dv + jnp.concatenate([_dot(ku_c[j], ds_after[j], NT_DIMS) for j in range(HG_NCH)], axis=0)
            dqd = dqd + jnp.concatenate([_dot(do_c[j], states[j]) for j in range(HG_NCH)], axis=0)
            dku = jnp.concatenate([_dot(v_c[j], ds_after[j]) for j in range(HG_NCH)], axis=0)
            dku_ku = dku * ku
            dbl = [_colsum(states[j] * ds_after[j]) * c["e_last"][j] + _colsum(dku_ku[j * HG_CHUNK:(j + 1) * HG_CHUNK])
                   for j in range(HG_NCH)]
            dk = dkd * c["e_nb"] + dku * c["e_rem"]
            db = dqd * qd - dkd * kd - dku_ku + jnp.where(_row_in_chunk() == HG_CHUNK - 1, _chunk_rows(dbl), 0.0)
            dfv = _chunk_cumsum(db, reverse=True) / c["f"] - dk
            sig, sq = c["sig"], c["sq"]
            df_ref[:, ls] = (dfv * (1.0 - c["lb"]) * sig * (1.0 - sig)).astype(BF16)
            dq_ref[:, ls] = (dqd * c["e_b"] * (sq * (1.0 + hq * (1.0 - sq)))).astype(BF16)
            di_ref[:, ls] = dv.astype(BF16)
            dlb_parts.append(_colsum(dfv * (1.0 - sig)))
            dnw_acc = dnw_acc + _colsum(dnw_rows)
        _acc_out(dlb_ref, jnp.logical_and(b == 0, t == 0), jnp.concatenate(dlb_parts, axis=1))
        _acc_out(dnw_ref, jnp.logical_and(h == 0, jnp.logical_and(b == 0, t == 0)), dnw_acc)

    rev = lambda t: nstep - 1 - t
    width = HG_HPS * LANE
    slab = lambda first: pl.BlockSpec((None, HG_TOK, width), lambda h, b, t: (b, rev(t), first // HG_HPS + h))
    head = pl.BlockSpec((None, HG_TOK, width), lambda h, b, t: (b, rev(t), h))
    grad_shape = _sds((bsz, seq, HG_WIDTH), BF16)
    return _call(
        body, (lb_table, norm_w, dcat, proj, proj, proj, proj, o_raw, states), name="hgrn_bwd",
        grid=(HG_HEADS // HG_HPS, bsz, nstep),
        in_specs=[pl.BlockSpec((2, width), lambda h, b, t: (0, h)), pl.BlockSpec((1, LANE), lambda h, b, t: (0, 0)),
                  slab(rec0), slab(HG_Q0), slab(HG_F0), slab(HG_I0), slab(HG_G0), head,
                  pl.BlockSpec((None, HG_HPS, HG_NCH, LANE, LANE), lambda h, b, t: (b, h, rev(t), 0, 0))],
        out_specs=[head, head, head, head, pl.BlockSpec((1, width), lambda h, b, t: (0, h)),
                   pl.BlockSpec((1, LANE), lambda h, b, t: (0, 0))],
        out_shape=[grad_shape, grad_shape, grad_shape, grad_shape, _sds((1, HG_WIDTH), F32), _sds((1, LANE), F32)],
        scratch_shapes=[pltpu.VMEM((HG_HPS, LANE, LANE), F32)],
        sem=("arbitrary", "arbitrary", "arbitrary"), comms=comms)


def _attn_bwd(dcat, raw, w_norm, qh, kh, vh, lse, sinks, comms=()):
    bsz, nblk = qh.shape[0], qh.shape[1]
    seq = nblk * WINDOW

    def body(sink_ref, da_ref, raw_ref, w_ref, q_ref, kc_ref, kp_ref, vc_ref, vp_ref, l_ref,
             dq_ref, dkd_ref, dkp_ref, dvd_ref, dvp_ref, dw_ref, dsink_ref):
        b, i = pl.program_id(0), pl.program_id(1)
        first = jnp.logical_and(b == 0, i == 0)
        w = w_ref[...]
        _, on, rstd = _rms_fwd(raw_ref[...], w)
        do_step, dw_rows = _rms_bwd(da_ref[...], on, rstd, w)
        _acc_out(dw_ref, first, _colsum(dw_rows))
        lane8 = lax.broadcasted_iota(jnp.int32, (1, ATT_Q_HEADS), 1)
        dsink = jnp.zeros((1, ATT_Q_HEADS), F32)
        for blk in range(ATT_BPS):
            tok = slice(blk * WINDOW, (blk + 1) * WINDOW)
            mask = _band_mask(True if blk else i > 0)
            raw_v, do_all = raw_ref[tok, :], do_step[tok]
            for g in range(ATT_KV_HEADS):
                gs = slice(g * ATT_HEAD_DIM, (g + 1) * ATT_HEAD_DIM)
                heads = [slice((g * ATT_GROUP + hh) * ATT_HEAD_DIM, (g * ATT_GROUP + hh + 1) * ATT_HEAD_DIM)
                         for hh in range(ATT_GROUP)]
                q = q_ref[blk, g]
                keys, vals = _band(kp_ref, kc_ref, g, blk), _band(vp_ref, vc_ref, g, blk)
                do_g = jnp.concatenate([do_all[:, hs] for hs in heads], axis=0)
                dsum = jnp.concatenate([jnp.sum(do_all[:, hs] * raw_v[:, hs], axis=-1, keepdims=True) for hs in heads], axis=0)
                lse_g = jnp.concatenate([l_ref[tok, g * ATT_GROUP + hh:g * ATT_GROUP + hh + 1] for hh in range(ATT_GROUP)], axis=0)
                p = jnp.where(mask, jnp.exp(_dot(q, keys, NT_DIMS) * ATT_SCALE - lse_g), 0.0)
                sink_part = jnp.exp(_sink_column(sink_ref, g) - lse_g) * dsum
                for hh in range(ATT_GROUP):
                    head_sum = jnp.sum(sink_part[hh * WINDOW:(hh + 1) * WINDOW], axis=0, keepdims=True)
                    dsink = dsink - jnp.where(lane8 == g * ATT_GROUP + hh, head_sum, 0.0)
                ds = p * (_dot(do_g, vals, NT_DIMS) - dsum) * ATT_SCALE
                dq_g = _dot(ds, keys)
                for hh, hs in enumerate(heads):
                    dq_ref[tok, hs] = dq_g[hh * WINDOW:(hh + 1) * WINDOW]
                dk_g = _dot(ds, q, TN_DIMS)
                dv_g = _dot(p, do_g, TN_DIMS)
                dkp_ref[tok, gs], dkd_ref[tok, gs] = dk_g[:WINDOW], dk_g[WINDOW:]
                dvp_ref[tok, gs], dvd_ref[tok, gs] = dv_g[:WINDOW], dv_g[WINDOW:]
        _acc_out(dsink_ref, first, dsink)

    rows = ATT_BPS * WINDOW
    cur = lambda width: pl.BlockSpec((None, rows, width), lambda b, i: (b, i, 0))
    q_spec, kv_cur, kv_prev = _attn_specs()
    kv_shape = _sds((bsz, seq, LANE), F32)
    return _call(
        body, (sinks, dcat, raw, w_norm, qh, kh, kh, vh, vh, lse), name="attn_bwd", grid=(bsz, nblk // ATT_BPS),
        in_specs=[pl.BlockSpec(memory_space=pltpu.SMEM), cur(ATT_WIDTH), cur(ATT_WIDTH), _vec_spec(ATT_WIDTH), q_spec,
                  kv_cur, kv_prev, kv_cur, kv_prev, cur(ATT_Q_HEADS)],
        out_specs=[cur(ATT_WIDTH), cur(LANE), cur(LANE), cur(LANE), cur(LANE), _vec_spec(ATT_WIDTH), _vec_spec(ATT_Q_HEADS)],
        out_shape=[_sds((bsz, seq, ATT_WIDTH), F32), kv_shape, kv_shape, kv_shape, kv_shape, _sds((1, ATT_WIDTH), F32),
                   _sds((1, ATT_Q_HEADS), F32)],
        sem=("arbitrary", "arbitrary"), comms=comms)


def _rope_bwd(dq, dkd, dkp, dvd, dvp, tables):
    bsz, seq, _ = dq.shape
    nblk = seq // WINDOW
    half = ROPE_DIM // 2

    def body(dq_ref, dkd_ref, dkp_ref, dvd_ref, dvp_ref, c_ref, u_ref, d_ref, o_ref):
        c, u, d = c_ref[...], u_ref[...], d_ref[...]
        has_next = pl.program_id(1) < nblk - 1

        def unrope(g):
            return g * c + pltpu.roll(g * u, LANE - half, 1) + pltpu.roll(g * d, half, 1)

        for s in range(ATT_WIDTH // LANE):
            o_ref[:, s * LANE:(s + 1) * LANE] = unrope(dq_ref[:, s * LANE:(s + 1) * LANE]).astype(BF16)
        dk = dkd_ref[...] + jnp.where(has_next, dkp_ref[...], 0.0)
        o_ref[:, ATT_WIDTH:ATT_WIDTH + LANE] = unrope(dk).astype(BF16)
        o_ref[:, ATT_WIDTH + LANE:ATT_COLS] = (dvd_ref[...] + jnp.where(has_next, dvp_ref[...], 0.0)).astype(BF16)

    cur = lambda width: pl.BlockSpec((None, WINDOW, width), lambda b, i: (b, i, 0))
    nxt = pl.BlockSpec((None, WINDOW, LANE), lambda b, i: (b, jnp.minimum(i + 1, nblk - 1), 0))
    tab = pl.BlockSpec((WINDOW, LANE), lambda b, i: (i, 0))
    return pl.pallas_call(
        body, name="rope_bwd", grid=(bsz, nblk),
        in_specs=[cur(ATT_WIDTH), cur(LANE), nxt, cur(LANE), nxt, tab, tab, tab],
        out_specs=cur(ATT_COLS), out_shape=_sds((bsz, seq, ATT_COLS), BF16),
        compiler_params=_params("parallel", "parallel"),
    )(dq, dkd, dkp, dvd, dvp, *tables)


def _other_chips(x, y):
    return [(1 - x, y), (x, 1 - y), (1 - x, 1 - y)]


def _sem_pair(n):
    return [pltpu.SemaphoreType.DMA((n,)), pltpu.SemaphoreType.DMA((n,))]


def _rows_of(ref, rows):
    return ref if rows is None else ref.at[pl.ds(rows[0], rows[1])]


def _plan_chip_gather(blocks, bufs, rows=None, forward_rows=None):
    n = len(blocks)

    def copies(ins, outs, sems):
        x, y, c = _mesh_pos()
        sends, lands = [], []
        for a in range(n):
            for j, chip in enumerate(_other_chips(x, y)):
                k = 3 * a + j
                sends.append(pltpu.make_async_remote_copy(
                    src_ref=_rows_of(ins[a], rows), dst_ref=_rows_of(outs[a].at[4 * x + 2 * y + c], rows), send_sem=sems[0].at[k],
                    recv_sem=sems[1].at[k], device_id=(*chip, c), device_id_type=MESH))
                slot = _rows_of(outs[a].at[4 * chip[0] + 2 * chip[1] + c], rows)
                lands.append(pltpu.make_async_remote_copy(
                    src_ref=slot, dst_ref=slot, send_sem=sems[0].at[k], recv_sem=sems[1].at[k],
                    device_id=(*chip, c), device_id_type=MESH))
                if forward_rows is not None:
                    k = 3 * (n + a) + j
                    mine = _rows_of(outs[a].at[4 * chip[0] + 2 * chip[1] + c], forward_rows)
                    sends.append(pltpu.make_async_remote_copy(
                        src_ref=mine, dst_ref=mine, send_sem=sems[0].at[k], recv_sem=sems[1].at[k],
                        device_id=(x, y, 1 - c), device_id_type=MESH))
                    theirs = _rows_of(outs[a].at[4 * chip[0] + 2 * chip[1] + 1 - c], forward_rows)
                    lands.append(pltpu.make_async_remote_copy(
                        src_ref=theirs, dst_ref=theirs, send_sem=sems[0].at[k], recv_sem=sems[1].at[k],
                        device_id=(x, y, 1 - c), device_id_type=MESH))
        return sends, lands

    def start(ins, outs, sems):
        for cp in copies(ins, outs, sems)[0]:
            cp.start()

    def finish(ins, outs, sems):
        sends, lands = copies(ins, outs, sems)
        for cp in lands:
            cp.wait_recv()
        for cp in sends:
            cp.wait_send()

    n_sems = 3 * n * (2 if forward_rows is not None else 1)
    return _Comm(list(blocks) + list(bufs), [_sds(b.shape, b.dtype) for b in bufs], _sem_pair(n_sems), start, finish,
                 aliases=[(n + a, a) for a in range(n)])


def _plan_pair_forward(bufs, rows=None):
    n = len(bufs)

    def copies(outs, sems):
        x, y, c = _mesh_pos()
        sends, lands = [], []
        for a in range(n):
            for j, chip in enumerate(_other_chips(x, y)):
                k = 3 * a + j
                slot = _rows_of(outs[a].at[4 * chip[0] + 2 * chip[1] + c], rows)
                sends.append(pltpu.make_async_remote_copy(
                    src_ref=slot, dst_ref=slot, send_sem=sems[0].at[k], recv_sem=sems[1].at[k],
                    device_id=(x, y, 1 - c), device_id_type=MESH))
                theirs = _rows_of(outs[a].at[4 * chip[0] + 2 * chip[1] + 1 - c], rows)
                lands.append(pltpu.make_async_remote_copy(
                    src_ref=theirs, dst_ref=theirs, send_sem=sems[0].at[k], recv_sem=sems[1].at[k],
                    device_id=(x, y, 1 - c), device_id_type=MESH))
        return sends, lands

    def start(ins, outs, sems):
        for cp in copies(outs, sems)[0]:
            cp.start()

    def finish(ins, outs, sems):
        sends, lands = copies(outs, sems)
        for cp in lands:
            cp.wait_recv()
        for cp in sends:
            cp.wait_send()

    return _Comm(list(bufs), [_sds(b.shape, b.dtype) for b in bufs], _sem_pair(3 * n), start, finish,
                 aliases=[(a, a) for a in range(n)])


def _plan_pair(arrays, other_half):
    n = len(arrays)
    per = N_CHIPS if other_half == "chip_major" else 1

    def copies(ins, outs, sems):
        x, y, c = _mesh_pos()
        out = []
        for a in range(n):
            for k in range(per):
                if other_half == "chip_major":
                    src, dst = ins[a].at[k, 1 - c], outs[a].at[k]
                else:
                    src, dst = (ins[a].at[1 - c] if other_half else ins[a]), outs[a]
                out.append(pltpu.make_async_remote_copy(
                    src_ref=src, dst_ref=dst, send_sem=sems[0].at[per * a + k], recv_sem=sems[1].at[per * a + k],
                    device_id=(x, y, 1 - c), device_id_type=MESH))
        return out

    def start(ins, outs, sems):
        for cp in copies(ins, outs, sems):
            cp.start()

    def finish(ins, outs, sems):
        for cp in copies(ins, outs, sems):
            cp.wait()

    if other_half == "chip_major":
        shapes = [_sds((a.shape[0],) + a.shape[2:], a.dtype) for a in arrays]
    else:
        shapes = [_sds(a.shape[1:] if other_half else a.shape, a.dtype) for a in arrays]
    return _Comm(list(arrays), shapes, _sem_pair(per * n), start, finish)


def _plan_chip_exchange(arrays):
    n = len(arrays)

    def copies(ins, outs, sems):
        x, y, c = _mesh_pos()
        sends, lands = [], []
        for a in range(n):
            for j, chip in enumerate(_other_chips(x, y)):
                k = 3 * a + j
                sends.append(pltpu.make_async_remote_copy(
                    src_ref=ins[a].at[2 * chip[0] + chip[1]], dst_ref=outs[a].at[2 * x + y], send_sem=sems[0].at[k],
                    recv_sem=sems[1].at[k], device_id=(*chip, c), device_id_type=MESH))
                slot = outs[a].at[2 * chip[0] + chip[1]]
                lands.append(pltpu.make_async_remote_copy(
                    src_ref=slot, dst_ref=slot, send_sem=sems[0].at[k], recv_sem=sems[1].at[k],
                    device_id=(*chip, c), device_id_type=MESH))
        return sends, lands

    def start(ins, outs, sems):
        for cp in copies(ins, outs, sems)[0]:
            cp.start()

    def finish(ins, outs, sems):
        sends, lands = copies(ins, outs, sems)
        for cp in lands:
            cp.wait_recv()
        for cp in sends:
            cp.wait_send()

    return _Comm(list(arrays), [_sds(a.shape, a.dtype) for a in arrays], _sem_pair(3 * n), start, finish)


def _comm_only(comms, name):
    return _call(lambda: None, (), name=name, grid=(), in_specs=[], out_specs=[], out_shape=[], sem=(), comms=comms)[1]


def _allgather8(arrays, name):
    return _comm_only([_plan_allgather8(arrays)], name)[0]


def _plan_allgather8(arrays):
    n = len(arrays)

    def parts(ins, outs, sems):
        send_sems, recv_sems, local_sems = sems
        x, y, c = _mesh_pos()
        me, sibling = (x, y, c), (x, y, 1 - c)
        chips = _other_chips(x, y)

        def copy(a, k, block, to, src=None):
            dst = outs[a].at[4 * block[0] + 2 * block[1] + block[2]]
            return pltpu.make_async_remote_copy(
                src_ref=dst if src is None else src, dst_ref=dst, send_sem=send_sems.at[7 * a + k],
                recv_sem=recv_sems.at[7 * a + k], device_id=to, device_id_type=MESH)

        mine = [pltpu.make_async_copy(ins[a], outs[a].at[4 * x + 2 * y + c], local_sems.at[a]) for a in range(n)]
        first = []
        for a in range(n):
            first.append(copy(a, 0, me, sibling, src=ins[a]))
            first += [copy(a, 1 + j, me, (*chip, c), src=ins[a]) for j, chip in enumerate(chips)]
        return copy, mine, first, me, sibling, chips, c

    def start(ins, outs, sems):
        _, mine, first, *_ = parts(ins, outs, sems)
        for cp in mine + first:
            cp.start()

    def finish(ins, outs, sems):
        copy, mine, first, me, sibling, chips, c = parts(ins, outs, sems)
        passed = []
        for j, chip in enumerate(chips):
            for a in range(n):
                copy(a, 1 + j, (*chip, c), me).wait_recv()
                fwd = copy(a, 4 + j, (*chip, c), sibling)
                fwd.start()
                passed.append(fwd)
        for a in range(n):
            copy(a, 0, sibling, me).wait_recv()
            for j, chip in enumerate(chips):
                copy(a, 4 + j, (*chip, 1 - c), me).wait_recv()
        for cp in first + passed:
            cp.wait_send()
        for cp in mine:
            cp.wait()

    sems = [pltpu.SemaphoreType.DMA((7 * n,)), pltpu.SemaphoreType.DMA((7 * n,)), pltpu.SemaphoreType.DMA((n,))]
    return _Comm(list(arrays), [_sds((N_DEV,) + a.shape, a.dtype) for a in arrays], sems, start, finish)


def _pair_sum(g, q, core, name, chip_major=False):
    rows, cols = g.shape[2:]
    tr = _row_tile(rows)

    def body(core_ref, g_ref, q_ref, o_ref):
        o_ref[...] = (g_ref[...] + q_ref[...]).astype(BF16)

    blk = pl.BlockSpec((None, tr, cols), lambda k, i, core_ref: (k, i, 0))
    if chip_major:
        own = pl.BlockSpec((None, None, tr, cols), lambda k, i, core_ref: (k, core_ref[0], i, 0))
    else:
        own = pl.BlockSpec((None, None, tr, cols), lambda k, i, core_ref: (core_ref[0], k, i, 0))
    return pl.pallas_call(
        body, name=name,
        grid_spec=pltpu.PrefetchScalarGridSpec(num_scalar_prefetch=1, grid=(N_CHIPS, rows // tr), in_specs=[own, blk], out_specs=blk),
        out_shape=_sds((N_CHIPS, rows, cols), BF16), compiler_params=_params("parallel", "parallel"),
    )(core, g, q)


def _sum_chips(own, landed, chip, name):
    _, rows, cols = own.shape
    tr = _row_tile(rows)

    def body(chip_ref, own_ref, a_ref, b_ref, c_ref, o_ref):
        acc = own_ref[...].astype(F32) + a_ref[...].astype(F32)
        o_ref[...] = (acc + b_ref[...].astype(F32)) + c_ref[...].astype(F32)

    blk = lambda flip: pl.BlockSpec((None, tr, cols), lambda i, chip_ref: (jnp.bitwise_xor(chip_ref[0], flip), i, 0))
    return pl.pallas_call(
        body, name=name,
        grid_spec=pltpu.PrefetchScalarGridSpec(num_scalar_prefetch=1, grid=(rows // tr,), in_specs=[blk(0), blk(1), blk(2), blk(3)],
                                               out_specs=pl.BlockSpec((tr, cols), lambda i, chip_ref: (i, 0))),
        out_shape=_sds((rows, cols), F32), compiler_params=_params("parallel"),
    )(chip, own, landed, landed, landed)


SUBLANES = 8


def _tile_rows(n_elems):
    return -(-n_elems // (SUBLANES * LANE)) * SUBLANES


SMALL_ITEMS = (("b_ada", N_MOD * D_MODEL), ("pre_w_mix", D_MODEL), ("post_w_mix", D_MODEL), ("pre_w_mlp", D_MODEL),
               ("post_w_mlp", D_MODEL), ("attn_out_w", ATT_WIDTH), ("hg_norm_w", HG_HEAD_DIM), ("attn_sinks", ATT_Q_HEADS),
               ("lb_0", HG_WIDTH), ("lb_1", HG_WIDTH))
SMALL_AT = {}
for _name, _size in SMALL_ITEMS:
    SMALL_AT[_name] = (sum(r for _, r in SMALL_AT.values()), _tile_rows(_size))
SMALL_ROWS = sum(r for _, r in SMALL_AT.values())
MOD_ROWS = SMALL_AT["b_ada"][1]
PLAIN_ROWS = SMALL_AT["lb_0"][0] - MOD_ROWS
LB_ROWS = SMALL_AT["lb_0"][1]


def _rows(a, nrows=None):
    flat = a.reshape(-1)
    nrows = _tile_rows(flat.shape[0]) if nrows is None else nrows
    return jnp.pad(flat, (0, nrows * LANE - flat.shape[0])).reshape(nrows, LANE)


def _pack_small(vals):
    vals = dict(vals, lb_0=vals["lb_table"][0], lb_1=vals["lb_table"][1])
    return jnp.concatenate([_rows(vals[name], SMALL_AT[name][1]) for name, _ in SMALL_ITEMS], axis=0)


def _unpack_small(p):
    def item(name, shape):
        first = SMALL_AT[name][0]
        size = shape[0] * shape[1]
        return p[first:first + SMALL_AT[name][1]].reshape(-1)[:size].reshape(shape)

    out = {name: item(name, (1, size)) for name, size in SMALL_ITEMS if not name.startswith("lb_")}
    out["lb_table"] = jnp.concatenate([item("lb_0", (1, HG_WIDTH)), item("lb_1", (1, HG_WIDTH))], axis=0)
    return out


def _pack_partials(dmod, plain, d_lb, loss_row):
    return jnp.concatenate([_rows(dmod, dmod.shape[0] * MOD_ROWS)] + [_rows(g) for g in plain] + [_rows(d_lb), _rows(loss_row)], axis=0)


def _small_update(packs, w, m, v, n_seq):
    mod_end = n_seq * MOD_ROWS
    lb_at = mod_end + PLAIN_ROWS
    t0, t1 = SMALL_AT["lb_0"][0], SMALL_AT["lb_1"][0]

    def body(p_ref, w_ref, m_ref, v_ref, g_ref, dl_ref, nm_ref, nv_ref, loss_ref):
        tot = p_ref[0]
        for d in range(1, N_DEV):
            tot = tot + p_ref[d]
        wv = w_ref[...]
        p1 = _sigmoid(wv[t1:t1 + LB_ROWS] - wv[t0:t0 + LB_ROWS])
        s = tot[lb_at:lb_at + LB_ROWS] * p1 * (1.0 - p1)
        g_bias = tot[0:MOD_ROWS]
        for q in range(1, n_seq):
            g_bias = g_bias + tot[q * MOD_ROWS:(q + 1) * MOD_ROWS]
        g = jnp.concatenate([g_bias, tot[mod_end:lb_at], -s, s], axis=0)
        g_ref[...] = g
        dl_ref[...], nm_ref[...], nv_ref[...] = _adamw_math(g, wv, m_ref[...], v_ref[...])
        loss_ref[...] = tot[lb_at + LB_ROWS:lb_at + LB_ROWS + SUBLANES]

    shp = _sds((SMALL_ROWS, LANE), F32)
    return pl.pallas_call(body, name="small_update", out_shape=[shp] * 4 + [_sds((SUBLANES, LANE), F32)],
                          compiler_params=_params())(packs, w, m, v)


def kernel(x, c, w_ada, b_ada, pre_w_mix, w_in, attn_sinks, attn_out_w, lb_table, hg_norm_w, w_out, post_w_mix, pre_w_mlp, w_up, w_down, post_w_mlp, loss_target, m_w_ada, m_b_ada, m_pre_w_mix, m_w_in, m_attn_sinks, m_attn_out_w, m_lb_table, m_hg_norm_w, m_w_out, m_post_w_mix, m_pre_w_mlp, m_w_up, m_w_down, m_post_w_mlp, v_w_ada, v_b_ada, v_pre_w_mix, v_w_in, v_attn_sinks, v_attn_out_w, v_lb_table, v_hg_norm_w, v_w_out, v_post_w_mix, v_pre_w_mlp, v_w_up, v_w_down, v_post_w_mlp):
    xi, yi, ci = _mesh_pos()
    chip = 2 * xi + yi
    dev = 2 * chip + ci
    bsz, seq, _ = x.shape
    ntok = bsz * seq
    ada_cols = w_ada.shape[2]
    core = jnp.reshape(ci, (1,)).astype(jnp.int32)
    chip_idx = jnp.reshape(chip, (1,)).astype(jnp.int32)
    flat = lambda a: a.reshape(ntok, a.shape[-1])
    unflat = lambda a: a.reshape(bsz, seq, a.shape[-1])
    tables = _rope_tables(seq)

    def row_half(w):
        rows = w.shape[1] // 2
        return lax.dynamic_slice_in_dim(w[0], ci * rows, rows, axis=0).astype(BF16)

    def gather_buffer(w):
        rows, cols = w.shape[1] // 2, w.shape[2]
        own = w[0].astype(BF16).reshape(2, rows, cols)
        return lax.dynamic_update_slice(jnp.zeros((N_DEV, rows, cols), BF16), own, (2 * chip, 0, 0))

    w_in_t, m_in_t, v_in_t = [jnp.transpose(a[0])[None] for a in (w_in, m_w_in, v_w_in)]
    c_g, in_g = _allgather8([c, row_half(w_in_t)], "gather_first")
    c_all = c_g.reshape(N_DEV * bsz, D_MODEL)
    w_in_full = in_g.reshape(IN_COLS, D_MODEL)

    b_cols = lax.dynamic_slice_in_dim(b_ada, chip * ada_cols, ada_cols, axis=1)
    mod_part = _ada_fwd(c_all, w_ada[0], b_cols)
    half_rows = mod_part.shape[0] // 2
    (mod_g,) = _allgather8([lax.dynamic_slice_in_dim(mod_part, ci * half_rows, half_rows, axis=0)], "gather_mod")
    mod_all = mod_g.reshape(N_CHIPS, 2, half_rows, ada_cols).transpose(1, 2, 0, 3).reshape(N_DEV * bsz, N_MOD * D_MODEL)
    mod = lax.dynamic_slice_in_dim(mod_all, dev * bsz, bsz, axis=0)
    sh1, sc1, g1, sh2, sc2, g2 = [mod[:, i * D_MODEL:(i + 1) * D_MODEL].reshape(bsz, 1, D_MODEL) for i in range(N_MOD)]

    up_rows = w_up.shape[1] // 4
    first_half, second_half = (0, up_rows), (up_rows, up_rows)
    (h1, proj, qh, kh, vh), ((out_g,), (up_g,)) = _in_proj_fused(
        x, pre_w_mix, sc1, sh1, w_in_full, tables,
        comms=[_plan_chip_gather([row_half(w_out)], [gather_buffer(w_out)]),
               _plan_chip_gather([row_half(w_up)], [gather_buffer(w_up)], rows=first_half)])
    (attn_raw, cat, lse), ((up_g,), (out_g,)) = _attn_fwd(
        qh, kh, vh, attn_sinks, attn_out_w,
        comms=[_plan_chip_gather([row_half(w_up)], [up_g], rows=second_half, forward_rows=first_half), _plan_pair_forward([out_g])])
    (o_raw, cat, states), ((down_g,), (up_g,)) = _hgrn_fwd(
        proj, lb_table, hg_norm_w, cat,
        comms=[_plan_chip_gather([row_half(w_down)], [gather_buffer(w_down)]), _plan_pair_forward([up_g], rows=second_half)])
    w_out_full = out_g.reshape(D_MODEL, D_MODEL)
    w_up4 = up_g.reshape(N_CHIPS, D_MODEL, D_MODEL)
    mix, x1, h2 = _out_proj_fused(cat, w_out_full, x, post_w_mix, g1, pre_w_mlp, sc2, sh2)
    big_tm = min(ntok, 2048)
    up_spec = pl.BlockSpec((None, D_MODEL, D_MODEL), lambda i, j: (j, 0, 0))
    r, ((down_g,),) = _mm(flat(h2), w_up4, name="up_proj", out_dtype=BF16, tm=big_tm, tn=D_MODEL, n_out=D_FF, b_spec=up_spec,
                          epi=lambda acc: jnp.maximum(acc, 0.0), comms=[_plan_pair_forward([down_g])])
    w_down_full = down_g.reshape(D_FF, D_MODEL)
    square = lambda t: t * t
    loss_row, dy, dd, dg2, d_post_mlp = _down_proj_fused(unflat(r), w_down_full, x1, post_w_mlp, g2, loss_target)

    dpre = _mm(flat(dd), w_down_full, name="down_bwd", out_dtype=BF16, trans_b=True, tm=big_tm, tn=D_MODEL, extra=(r,),
               epi=lambda acc, rt: acc * (2.0 * rt.astype(F32)))
    half_rows = D_MODEL // 2
    g_down = _mm_tn(r, flat(dd), name="down_wgrad", tk=half_rows, tn=D_MODEL, a_fn=square,
                    out_shape=_sds((2, N_CHIPS, half_rows, D_MODEL), F32),
                    out_spec=pl.BlockSpec((None, None, half_rows, D_MODEL), lambda i, j: (i % 2, i // 2, 0, 0)))
    (dx1, dmix, dsc2, dsh2, dg1, d_pre_mlp, d_post_mix), ((q_down,),) = _up_bwd_fused(
        unflat(dpre), w_up4, dy, x1, mix, pre_w_mlp, sc2, post_w_mix, g1, comms=[_plan_pair([g_down], True)])
    g_up = _mm_tn(flat(h2), dpre, name="up_wgrad", tk=D_MODEL, tn=half_rows,
                  out_shape=_sds((2, N_CHIPS, half_rows, D_MODEL), F32),
                  out_spec=pl.BlockSpec((2, None, half_rows, half_rows), lambda i, j: (0, j // 2, 0, j % 2)))
    s_down = _pair_sum(g_down, q_down, core, "pair_sum_down")

    dcat, ((q_up,),) = _mm(flat(dmix), w_out_full, name="out_bwd", out_dtype=F32, trans_b=True, comms=[_plan_pair([g_up], True)])
    dcat = unflat(dcat)
    s_up = _pair_sum(g_up, q_up, core, "pair_sum_up")
    out_rows = D_MODEL // N_CHIPS
    g_out = _mm_tn(flat(cat), flat(dmix), name="out_wgrad", tk=out_rows, tn=half_rows,
                   out_shape=_sds((2, N_CHIPS, out_rows, half_rows), F32),
                   out_spec=pl.BlockSpec((None, None, out_rows, half_rows), lambda i, j: (j, i, 0, 0)))
    (dhq, dhf, dhi, dhg, d_lb, d_hg_norm), ((x_down,), (q_out,)) = _hgrn_bwd(
        dcat, proj, o_raw, states, lb_table, hg_norm_w, comms=[_plan_chip_exchange([s_down]), _plan_pair([g_out], True)])
    half_down = _sum_chips(s_down, x_down, chip_idx, "sum_chips_down")
    s_out = _pair_sum(g_out, q_out, core, "pair_sum_out")
    (dq, dkd, dkp, dvd, dvp, d_attn_out, d_sinks), ((their_down,), (x_up, x_out)) = _attn_bwd(
        dcat, attn_raw, attn_out_w, qh, kh, vh, lse, attn_sinks,
        comms=[_plan_pair([half_down], False), _plan_chip_exchange([s_up, s_out])])
    half_up = _sum_chips(s_up, x_up, chip_idx, "sum_chips_up")
    half_out = _sum_chips(s_out, x_out, chip_idx, "sum_chips_out")
    dproj_a = _rope_bwd(dq, dkd, dkp, dvd, dvp, tables)
    dproj = flat(jnp.concatenate([dproj_a, dhq, dhf, dhi, dhg], axis=-1))
    in_rows = IN_COLS // N_CHIPS // 2
    g_in = _mm_tn(dproj, flat(h1), name="in_wgrad", tk=2 * LANE, tn=D_MODEL).reshape(N_CHIPS, 2, in_rows, D_MODEL)
    dh1, ((q_in,), (their_up, their_out)) = _mm(
        dproj, w_in_full, name="in_bwd", out_dtype=F32,
        comms=[_plan_pair([g_in], "chip_major"), _plan_pair([half_up, half_out], False)])
    s_in = _pair_sum(g_in, q_in, core, "pair_sum_in", chip_major=True)
    grad_x, dsc1, dsh1, d_pre_mix = _norm1_bwd(unflat(dh1), dx1, x, pre_w_mix, sc1)

    dmod = jnp.concatenate([dsh1, dsc1, dg1, dsh2, dsc2, dg2], axis=-1).reshape(bsz, N_MOD * D_MODEL)
    pack = _pack_partials(dmod, [d_pre_mix, d_post_mix, d_pre_mlp, d_post_mlp, d_attn_out, d_hg_norm, d_sinks], d_lb, loss_row)
    (packs,), (x_in,) = _comm_only([_plan_allgather8([pack]), _plan_chip_exchange([s_in])], "gather_small")
    half_in = _sum_chips(s_in, x_in, chip_idx, "sum_chips_in")
    ((their_in,),) = _comm_only([_plan_pair([half_in], False)], "pair_swap_in")
    w_small = dict(b_ada=b_ada, pre_w_mix=pre_w_mix, post_w_mix=post_w_mix, pre_w_mlp=pre_w_mlp, post_w_mlp=post_w_mlp,
                   attn_out_w=attn_out_w, hg_norm_w=hg_norm_w, attn_sinks=attn_sinks, lb_table=lb_table)
    m_small = dict(b_ada=m_b_ada, pre_w_mix=m_pre_w_mix, post_w_mix=m_post_w_mix, pre_w_mlp=m_pre_w_mlp, post_w_mlp=m_post_w_mlp,
                   attn_out_w=m_attn_out_w, hg_norm_w=m_hg_norm_w, attn_sinks=m_attn_sinks, lb_table=m_lb_table)
    v_small = dict(b_ada=v_b_ada, pre_w_mix=v_pre_w_mix, post_w_mix=v_post_w_mix, pre_w_mlp=v_pre_w_mlp, post_w_mlp=v_post_w_mlp,
                   attn_out_w=v_attn_out_w, hg_norm_w=v_hg_norm_w, attn_sinks=v_attn_sinks, lb_table=v_lb_table)
    *small_packed, loss_rows = _small_update(packs, _pack_small(w_small), _pack_small(m_small), _pack_small(v_small), bsz)
    small_out = [_unpack_small(p) for p in small_packed]
    loss = loss_rows[0, 0]

    dmod_all = packs[:, :bsz * MOD_ROWS, :].reshape(N_DEV * bsz, N_MOD * D_MODEL)
    dmod_cols = lax.dynamic_slice_in_dim(dmod_all, chip * ada_cols, ada_cols, axis=1)
    ada_out = _ada_bwd_adamw(c_all, dmod_cols, w_ada[0], m_w_ada[0], v_w_ada[0])

    big = dict(
        w_in=tuple(jnp.transpose(a) for a in _adamw_halves(half_in, their_in, core, w_in_t[0], m_in_t[0], v_in_t[0], axis=0,
                                                           name="adamw_in")),
        w_up=tuple(_adamw_halves(half_up, their_up, core, w_up[0], m_w_up[0], v_w_up[0], axis=0, name="adamw_up")),
        w_out=tuple(_adamw_halves(half_out, their_out, core, w_out[0], m_w_out[0], v_w_out[0], axis=1, name="adamw_out")),
        w_down=tuple(_adamw_halves(half_down, their_down, core, w_down[0], m_w_down[0], v_w_down[0], axis=0, name="adamw_down")),
        w_ada=tuple(ada_out),
    )
    order = ("w_ada", "b_ada", "pre_w_mix", "w_in", "attn_sinks", "attn_out_w", "lb_table", "hg_norm_w", "w_out", "post_w_mix",
             "pre_w_mlp", "w_up", "w_down", "post_w_mlp")
    outs = [loss, grad_x]
    for kind in range(4):
        for nm in order:
            outs.append(big[nm][kind][None] if nm in big else small_out[kind][nm])
    return tuple(outs)
```

```python
import jax
import jax.numpy as jnp
from jax import lax
from jax.experimental import pallas as pl
from jax.experimental.pallas import tpu as pltpu

F32 = jnp.float32
BF16 = jnp.bfloat16

D_MODEL = 1024
ATT_WIDTH = 512
ATT_HEAD_DIM = 64
ATT_Q_HEADS = 8
ATT_KV_HEADS = 2
ATT_GROUP = ATT_Q_HEADS // ATT_KV_HEADS
ATT_KV_COLS = ATT_KV_HEADS * ATT_HEAD_DIM
WINDOW = 128
ROPE_DIM = 16
ROPE_THETA = 500000.0
HG_WIDTH = 512
MIX_WIDTH = ATT_WIDTH + HG_WIDTH
HG_HEAD_DIM = 128
HG_HEADS = 4
HG_CHUNK = 32
IN_COLS = ATT_WIDTH + 2 * ATT_KV_COLS + 4 * HG_WIDTH
ATT_COLS = ATT_WIDTH + 2 * ATT_KV_COLS
D_FF = 4 * D_MODEL
N_MOD = 6
EPS = 1e-6
ATT_SCALE = ATT_HEAD_DIM ** -0.5

ADAM_LR = 0.001
ADAM_B1 = 0.9
ADAM_B2 = 0.999
ADAM_EPS = 1e-08
ADAM_WD = 0.01
ADAM_STEP = 10

N_CHIPS = 4
N_DEV = 8
LANE = 128
VMEM_LIMIT = 48 * 1024 * 1024
VMEM_LIMIT_BIG = 58 * 1024 * 1024
MESH = pl.DeviceIdType.MESH

NT_DIMS = (((1,), (1,)), ((), ()))
TN_DIMS = (((0,), (0,)), ((), ()))


def _sds(shape, dtype):
    return jax.ShapeDtypeStruct(tuple(shape), dtype)


def _params(*sem, vmem_limit=None):
    return pltpu.CompilerParams(dimension_semantics=sem, vmem_limit_bytes=VMEM_LIMIT if vmem_limit is None else vmem_limit)


def _sigmoid(x):
    return 1.0 / (1.0 + jnp.exp(-x))


def _dot(a, b, dims=None):
    a, b = a.astype(BF16), b.astype(BF16)
    if dims is None:
        return jnp.dot(a, b, preferred_element_type=F32)
    return lax.dot_general(a, b, dims, preferred_element_type=F32)


def _rms_fwd(x, w):
    rstd = lax.rsqrt(jnp.mean(x * x, axis=-1, keepdims=True) + EPS)
    xh = x * rstd
    return xh * w, xh, rstd


def _rms_bwd(dy, xh, rstd, w):
    dxh = dy * w
    dx = rstd * (dxh - xh * jnp.mean(dxh * xh, axis=-1, keepdims=True))
    return dx, dy * xh


def _colsum(x):
    return jnp.sum(x, axis=0, keepdims=True)


def _row_tile(rows, cap=256):
    return max(t for t in range(16, cap + 1, 16) if rows % t == 0)


HBM_SPEC = pl.BlockSpec(memory_space=pltpu.HBM)


def _mesh_pos():
    return lax.axis_index("x"), lax.axis_index("y"), lax.axis_index("c")


class _Comm:
    def __init__(self, ins, outs, sems, start, finish, aliases=()):
        self.ins, self.outs, self.sems = list(ins), list(outs), list(sems)
        self.start, self.finish, self.aliases = start, finish, tuple(aliases)


def _call(body, args, *, name, grid, in_specs, out_specs, out_shape, sem, scratch_shapes=(), comms=(), aliases=None,
          vmem_limit=None):
    scratch_shapes = list(scratch_shapes)
    if not comms:
        return pl.pallas_call(body, name=name, grid=grid, in_specs=in_specs, out_specs=out_specs, out_shape=out_shape,
                              input_output_aliases=dict(aliases or {}), scratch_shapes=scratch_shapes,
                              compiler_params=_params(*sem, vmem_limit=vmem_limit))(*args)
    single = not isinstance(out_shape, (list, tuple))
    out_specs_l = [out_specs] if single else list(out_specs)
    out_shape_l = [out_shape] if single else list(out_shape)
    n_in, n_out, n_scr = len(in_specs), len(out_shape_l), len(scratch_shapes)
    n_ci = [len(cm.ins) for cm in comms]
    n_co = [len(cm.outs) for cm in comms]
    n_cs = [len(cm.sems) for cm in comms]
    aliases = dict(aliases or {})
    for k, cm in enumerate(comms):
        for i, o in cm.aliases:
            aliases[n_in + sum(n_ci[:k]) + i] = n_out + sum(n_co[:k]) + o

    def fused(*refs):
        pos = [0]

        def take(n):
            part = refs[pos[0]:pos[0] + n]
            pos[0] += n
            return part

        ins = take(n_in)
        c_ins = [take(n) for n in n_ci]
        outs = take(n_out)
        c_outs = [take(n) for n in n_co]
        scr = take(n_scr)
        c_sems = [take(n) for n in n_cs]
        first, last = True, True
        for d, size in enumerate(grid):
            first = jnp.logical_and(first, pl.program_id(d) == 0)
            last = jnp.logical_and(last, pl.program_id(d) == size - 1)

        def run(which):
            for cm, ci, co, cs in zip(comms, c_ins, c_outs, c_sems):
                getattr(cm, which)(ci, co, cs)

        if grid:
            pl.when(first)(lambda: run("start"))
        else:
            run("start")
        body(*ins, *outs, *scr)
        if grid:
            pl.when(last)(lambda: run("finish"))
        else:
            run("finish")

    res = pl.pallas_call(
        fused, name=name, grid=grid, in_specs=list(in_specs) + [HBM_SPEC] * sum(n_ci),
        out_specs=out_specs_l + [HBM_SPEC] * sum(n_co), out_shape=out_shape_l + [s for cm in comms for s in cm.outs],
        input_output_aliases=aliases, scratch_shapes=scratch_shapes + [s for cm in comms for s in cm.sems],
        compiler_params=_params(*["arbitrary"] * len(grid), vmem_limit=vmem_limit),
    )(*args, *[a for cm in comms for a in cm.ins])
    main = res[:n_out]
    extra, at = [], n_out
    for n in n_co:
        extra.append(list(res[at:at + n]))
        at += n
    return (main[0] if single else list(main)), extra


def _mm(a, b, *, name, out_dtype, trans_b=False, tm=512, tn=None, extra=(), epi=None, b_spec=None, n_out=None, comms=()):
    m_total, k_total = a.shape
    if n_out is None:
        n_out = b.shape[0] if trans_b else b.shape[1]
    tn = n_out if tn is None else tn
    grid = (m_total // tm, n_out // tn)
    dims = NT_DIMS if trans_b else None

    def body(*refs):
        a_ref, b_ref = refs[0], refs[1]
        extra_refs = refs[2:2 + len(extra)]
        o_ref = refs[2 + len(extra)]
        acc = _dot(a_ref[...], b_ref[...], dims)
        if epi is not None:
            acc = epi(acc, *[r[...] for r in extra_refs])
        o_ref[...] = acc.astype(out_dtype)

    if b_spec is None:
        if trans_b:
            b_spec = pl.BlockSpec((tn, k_total), lambda i, j: (j, 0))
        else:
            b_spec = pl.BlockSpec((k_total, tn), lambda i, j: (0, j))
    in_specs = [pl.BlockSpec((tm, k_total), lambda i, j: (i, 0)), b_spec]
    in_specs += [pl.BlockSpec((tm, tn), lambda i, j: (i, j)) for _ in extra]
    return _call(
        body, (a, b, *extra), name=name, grid=grid, in_specs=in_specs,
        out_specs=pl.BlockSpec((tm, tn), lambda i, j: (i, j)),
        out_shape=_sds((m_total, n_out), out_dtype),
        sem=("parallel", "parallel"), comms=comms)


def _mm_tn(a, b, *, name, tk, tn, a_fn=None, out_shape=None, out_spec=None):
    m_total, k_total = a.shape
    n_total = b.shape[1]
    grid = (k_total // tk, n_total // tn)

    def body(a_ref, b_ref, o_ref):
        av = a_ref[...]
        part = _dot(av if a_fn is None else a_fn(av), b_ref[...], TN_DIMS)
        o_ref[...] = part.reshape(o_ref.shape)

    if out_shape is None:
        out_shape = _sds((k_total, n_total), F32)
        out_spec = pl.BlockSpec((tk, tn), lambda i, j: (i, j))
    return pl.pallas_call(
        body, name=name, grid=grid,
        in_specs=[pl.BlockSpec((m_total, tk), lambda i, j: (0, i)), pl.BlockSpec((m_total, tn), lambda i, j: (0, j))],
        out_specs=out_spec, out_shape=out_shape,
        compiler_params=_params("parallel", "parallel"),
    )(a, b)


def _ada_fwd(c_all, w_shard, b_shard):
    nb, ncol = c_all.shape[0], w_shard.shape[1]
    tn = 512

    def body(c_ref, w_ref, b_ref, o_ref):
        c = c_ref[...]
        o_ref[...] = _dot(c * _sigmoid(c), w_ref[...]) + b_ref[...]

    return pl.pallas_call(
        body, name="ada_fwd", grid=(ncol // tn,),
        in_specs=[pl.BlockSpec((nb, D_MODEL), lambda j: (0, 0)), pl.BlockSpec((D_MODEL, tn), lambda j: (0, j)),
                  pl.BlockSpec((1, tn), lambda j: (0, j))],
        out_specs=pl.BlockSpec((nb, tn), lambda j: (0, j)), out_shape=_sds((nb, ncol), F32),
        compiler_params=_params("parallel"),
    )(c_all, w_shard, b_shard)


def _adamw_math(g, w, m, v):
    m = ADAM_B1 * m + (1.0 - ADAM_B1) * g
    v = ADAM_B2 * v + (1.0 - ADAM_B2) * (g * g)
    m_hat = m / (1.0 - ADAM_B1 ** ADAM_STEP)
    v_hat = v / (1.0 - ADAM_B2 ** ADAM_STEP)
    delta = -ADAM_LR * (m_hat / (jnp.sqrt(v_hat) + ADAM_EPS) + ADAM_WD * w)
    return delta, m, v


def _ada_bwd_adamw(c_all, dmod_cols, w, m, v):
    nb, ncol = dmod_cols.shape
    tn = 256

    def body(c_ref, d_ref, w_ref, m_ref, v_ref, g_ref, dl_ref, nm_ref, nv_ref):
        c = c_ref[...]
        g = _dot(c * _sigmoid(c), d_ref[...], TN_DIMS)
        g_ref[...] = g
        dl_ref[...], nm_ref[...], nv_ref[...] = _adamw_math(g, w_ref[...], m_ref[...], v_ref[...])

    col = pl.BlockSpec((D_MODEL, tn), lambda j: (0, j))
    shp = _sds((D_MODEL, ncol), F32)
    return pl.pallas_call(
        body, name="ada_bwd_adamw", grid=(ncol // tn,),
        in_specs=[pl.BlockSpec((nb, D_MODEL), lambda j: (0, 0)), pl.BlockSpec((nb, tn), lambda j: (0, j)), col, col, col],
        out_specs=[col, col, col, col], out_shape=[shp, shp, shp, shp],
        compiler_params=_params("parallel"),
    )(c_all, dmod_cols, w, m, v)


def _adamw_halves(own, theirs, core, w, m, v, *, axis, name):
    r2, c2 = own.shape
    tr = _row_tile(r2)
    nt = r2 // tr

    def body(core_ref, own_ref, their_ref, w_ref, m_ref, v_ref, g_ref, dl_ref, nm_ref, nv_ref):
        g = jnp.where(pl.program_id(0) == core_ref[0], own_ref[...], their_ref[...])
        g_ref[...] = g
        dl_ref[...], nm_ref[...], nv_ref[...] = _adamw_math(g, w_ref[...], m_ref[...], v_ref[...])

    if axis == 0:
        full = pl.BlockSpec((tr, c2), lambda h, i, core_ref: (h * nt + i, 0))
    else:
        full = pl.BlockSpec((tr, c2), lambda h, i, core_ref: (i, h))
    half = pl.BlockSpec((tr, c2), lambda h, i, core_ref: (i, 0))
    shp = _sds(w.shape, F32)
    return pl.pallas_call(
        body, name=name,
        grid_spec=pltpu.PrefetchScalarGridSpec(num_scalar_prefetch=1, grid=(2, nt), in_specs=[half, half, full, full, full],
                                               out_specs=[full] * 4),
        out_shape=[shp] * 4, compiler_params=_params("parallel", "parallel"),
    )(core, own, theirs, w, m, v)


def _tok_spec(tm, width=D_MODEL):
    return pl.BlockSpec((None, tm, width), lambda b, i: (b, i, 0))


def _row_spec(width=D_MODEL):
    return pl.BlockSpec((None, 1, width), lambda b, i: (b, 0, 0))


def _vec_spec(width=D_MODEL):
    return pl.BlockSpec((1, width), lambda b, i: (0, 0))


class _RowsOf:
    def __init__(self, ref, first, count):
        self.ref, self.rows = ref, slice(first, first + count)

    def __getitem__(self, idx):
        return self.ref[self.rows, :]

    def __setitem__(self, idx, value):
        self.ref[self.rows, :] = value


def _mm_rows(a, b, *, name, tm, extra, extra_specs, out_specs, out_shape, epi, pro=None, trans_b=False, b_chunks=1, comms=(),
             parts=1, zero_per_seq=(), zero_once=(), vmem_limit=None):
    bsz, seq, k_total = a.shape
    kc = k_total // b_chunks
    dims = NT_DIMS if trans_b else None
    rows = tm // parts

    def body(*refs):
        a_ref, b_ref = refs[0], refs[1]
        ex, outs = refs[2:2 + len(extra)], refs[2 + len(extra):]
        if zero_per_seq:
            @pl.when(pl.program_id(1) == 0)
            def _():
                for k in zero_per_seq:
                    outs[k][...] = jnp.zeros_like(outs[k])
        if zero_once:
            @pl.when(jnp.logical_and(pl.program_id(0) == 0, pl.program_id(1) == 0))
            def _():
                for k in zero_once:
                    outs[k][...] = jnp.zeros_like(outs[k])

        def part_of(ref, p):
            tiled = len(ref.shape) == 2 and ref.shape[0] == tm
            return _RowsOf(ref, p * rows, rows) if tiled and parts > 1 else ref

        accs = []
        for p in range(parts):
            a_p, ex_p, outs_p = part_of(a_ref, p), [part_of(r, p) for r in ex], [part_of(r, p) for r in outs]
            if b_chunks == 1:
                accs.append(_dot(a_p[...] if pro is None else pro(a_p, ex_p, outs_p), b_ref[...], dims))
            else:
                acc = _dot(a_p[...][:, 0:kc], b_ref[0], NT_DIMS)
                for k in range(1, b_chunks):
                    acc = acc + _dot(a_p[...][:, k * kc:(k + 1) * kc], b_ref[k], NT_DIMS)
                accs.append(acc)
        for p in range(parts):
            epi(accs[p], [part_of(r, p) for r in ex], [part_of(r, p) for r in outs])

    b_spec = pl.BlockSpec(b.shape, lambda bb, i: (0,) * b.ndim)
    return _call(
        body, (a, b, *extra), name=name, grid=(bsz, seq // tm), in_specs=[_tok_spec(tm, k_total), b_spec, *extra_specs],
        out_specs=out_specs, out_shape=out_shape, sem=("arbitrary", "arbitrary"), comms=comms, vmem_limit=vmem_limit)


def _in_proj_fused(x, w, sc, sh, w_in_t, tables, comms=()):
    tm = 512
    bsz, seq, _ = x.shape
    half = ROPE_DIM // 2
    heads_per_slab = LANE // ATT_HEAD_DIM

    def pro(x_ref, ex, outs):
        y, _, _ = _rms_fwd(x_ref[...], ex[0][...])
        h = (y * (1.0 + ex[1][...]) + ex[2][...]).astype(BF16)
        outs[0][...] = h
        return h

    def epi(acc, ex, outs):
        c, u, d = ex[3][...], ex[4][...], ex[5][...]
        _, proj_ref, q_ref, k_ref, v_ref = outs
        proj_ref[...] = acc

        def rope(z):
            return (z * c + pltpu.roll(z, half, 1) * u + pltpu.roll(z, LANE - half, 1) * d).astype(BF16)

        for s in range(ATT_WIDTH // LANE):
            slab = rope(acc[:, s * LANE:(s + 1) * LANE])
            for part in range(heads_per_slab):
                g, hh = divmod(s * heads_per_slab + part, ATT_GROUP)
                piece = slab[:, part * ATT_HEAD_DIM:(part + 1) * ATT_HEAD_DIM]
                for blk in range(tm // WINDOW):
                    q_ref[blk, g, hh * WINDOW:(hh + 1) * WINDOW, :] = piece[blk * WINDOW:(blk + 1) * WINDOW]
        rk = rope(acc[:, ATT_WIDTH:ATT_WIDTH + LANE])
        vv = acc[:, ATT_WIDTH + LANE:ATT_COLS].astype(BF16)
        for g in range(ATT_KV_HEADS):
            k_ref[g] = rk[:, g * ATT_HEAD_DIM:(g + 1) * ATT_HEAD_DIM]
            v_ref[g] = vv[:, g * ATT_HEAD_DIM:(g + 1) * ATT_HEAD_DIM]

    cols = w_in_t.shape[0]
    tab = pl.BlockSpec((tm, LANE), lambda b, i: (i, 0))
    kv_spec = pl.BlockSpec((None, ATT_KV_HEADS, tm, ATT_HEAD_DIM), lambda b, i: (b, 0, i, 0))
    kv_shape = _sds((bsz, ATT_KV_HEADS, seq, ATT_HEAD_DIM), BF16)
    q_spec = pl.BlockSpec((None, tm // WINDOW, ATT_KV_HEADS, GROUP_ROWS, ATT_HEAD_DIM), lambda b, i: (b, i, 0, 0, 0))
    return _mm_rows(x, w_in_t, name="in_proj", tm=tm, extra=(w, sc, sh, *tables),
                    extra_specs=[_vec_spec(), _row_spec(), _row_spec(), tab, tab, tab],
                    out_specs=[_tok_spec(tm), _tok_spec(tm, cols), q_spec, kv_spec, kv_spec],
                    out_shape=[_sds(x.shape, BF16), _sds((bsz, seq, cols), F32),
                               _sds((bsz, seq // WINDOW, ATT_KV_HEADS, GROUP_ROWS, ATT_HEAD_DIM), BF16), kv_shape, kv_shape],
                    pro=pro, epi=epi, trans_b=True, comms=comms)


def _rope_tables(seq):
    half = ROPE_DIM // 2
    inv_freq = ROPE_THETA ** (-jnp.arange(0, ROPE_DIM, 2, dtype=F32) / ROPE_DIM)
    ang = jnp.arange(seq, dtype=F32)[:, None] * inv_freq[None, :]
    cos, sin = jnp.cos(ang), jnp.sin(ang)
    rest = ATT_HEAD_DIM - ROPE_DIM
    ones, zeros, zh = jnp.ones((seq, rest), F32), jnp.zeros((seq, rest), F32), jnp.zeros((seq, half), F32)
    reps = LANE // ATT_HEAD_DIM
    t_cos = jnp.tile(jnp.concatenate([cos, cos, ones], axis=1), (1, reps))
    t_up = jnp.tile(jnp.concatenate([zh, sin, zeros], axis=1), (1, reps))
    t_dn = jnp.tile(jnp.concatenate([-sin, zh, zeros], axis=1), (1, reps))
    return t_cos, t_up, t_dn


GROUP_ROWS = ATT_GROUP * WINDOW


ATT_BPS = 2


def _band_mask(has_prev):
    row = lax.broadcasted_iota(jnp.int32, (GROUP_ROWS, 2 * WINDOW), 0) % WINDOW
    col = lax.broadcasted_iota(jnp.int32, (GROUP_ROWS, 2 * WINDOW), 1)
    prev = jnp.logical_and(jnp.logical_and(col < WINDOW, col > row), has_prev)
    return jnp.logical_or(prev, jnp.logical_and(col >= WINDOW, col - WINDOW <= row))


def _sink_column(sink_ref, g):
    head = lax.broadcasted_iota(jnp.int32, (GROUP_ROWS, 1), 0) // WINDOW
    col = jnp.full((GROUP_ROWS, 1), sink_ref[0, g * ATT_GROUP], F32)
    for hh in range(1, ATT_GROUP):
        col = jnp.where(head == hh, sink_ref[0, g * ATT_GROUP + hh], col)
    return col


def _attn_specs():
    q_spec = pl.BlockSpec((None, ATT_BPS, ATT_KV_HEADS, GROUP_ROWS, ATT_HEAD_DIM), lambda b, i: (b, i, 0, 0, 0))
    kv_cur = pl.BlockSpec((None, ATT_KV_HEADS, ATT_BPS * WINDOW, ATT_HEAD_DIM), lambda b, i: (b, 0, i, 0))
    kv_prev = pl.BlockSpec((None, ATT_KV_HEADS, WINDOW, ATT_HEAD_DIM), lambda b, i: (b, 0, jnp.maximum(ATT_BPS * i - 1, 0), 0))
    return q_spec, kv_cur, kv_prev


def _band(prev_ref, cur_ref, g, blk):
    own = cur_ref[g, blk * WINDOW:(blk + 1) * WINDOW]
    before = prev_ref[g] if blk == 0 else cur_ref[g, (blk - 1) * WINDOW:blk * WINDOW]
    return jnp.concatenate([before, own], axis=0)


def _attn_fwd(qh, kh, vh, sinks, w_norm, comms=()):
    bsz, nblk = qh.shape[0], qh.shape[1]
    seq = nblk * WINDOW
    rows = ATT_BPS * WINDOW
    neg = float(jnp.finfo(jnp.float32).min)

    def body(sink_ref, q_ref, kc_ref, kp_ref, vc_ref, vp_ref, w_ref, raw_ref, an_ref, l_ref):
        for blk in range(ATT_BPS):
            mask = _band_mask(True if blk else pl.program_id(1) > 0)
            for g in range(ATT_KV_HEADS):
                keys, vals = _band(kp_ref, kc_ref, g, blk), _band(vp_ref, vc_ref, g, blk)
                sink = _sink_column(sink_ref, g)
                s = jnp.where(mask, _dot(q_ref[blk, g], keys, NT_DIMS) * ATT_SCALE, neg)
                m = jnp.maximum(jnp.max(s, axis=-1, keepdims=True), sink)
                p = jnp.where(mask, jnp.exp(s - m), 0.0)
                den = jnp.sum(p, axis=-1, keepdims=True) + jnp.exp(sink - m)
                o = _dot(p / den, vals)
                lse = m + jnp.log(den)
                tok = slice(blk * WINDOW, (blk + 1) * WINDOW)
                for hh in range(ATT_GROUP):
                    h = g * ATT_GROUP + hh
                    raw_ref[tok, h * ATT_HEAD_DIM:(h + 1) * ATT_HEAD_DIM] = o[hh * WINDOW:(hh + 1) * WINDOW]
                    l_ref[tok, h:h + 1] = lse[hh * WINDOW:(hh + 1) * WINDOW]
        y, _, _ = _rms_fwd(raw_ref[...], w_ref[...])
        an_ref[...] = y.astype(BF16)

    cur = lambda width: pl.BlockSpec((None, rows, width), lambda b, i: (b, i, 0))
    q_spec, kv_cur, kv_prev = _attn_specs()
    return _call(
        body, (sinks, qh, kh, kh, vh, vh, w_norm), name="attn_fwd", grid=(bsz, nblk // ATT_BPS),
        in_specs=[pl.BlockSpec(memory_space=pltpu.SMEM), q_spec, kv_cur, kv_prev, kv_cur, kv_prev, _vec_spec(ATT_WIDTH)],
        out_specs=[cur(ATT_WIDTH), cur(ATT_WIDTH), cur(ATT_Q_HEADS)],
        out_shape=[_sds((bsz, seq, ATT_WIDTH), F32), _sds((bsz, seq, MIX_WIDTH), BF16), _sds((bsz, seq, ATT_Q_HEADS), F32)],
        sem=("parallel", "parallel"), comms=comms)


HG_Q0 = ATT_COLS // LANE
HG_F0 = HG_Q0 + HG_HEADS
HG_I0 = HG_F0 + HG_HEADS
HG_G0 = HG_I0 + HG_HEADS
HG_TOK = 256
HG_NCH = HG_TOK // HG_CHUNK
HG_HPS = 2


def _block_masks():
    row = lax.broadcasted_iota(jnp.int32, (HG_TOK, HG_TOK), 0)
    col = lax.broadcasted_iota(jnp.int32, (HG_TOK, HG_TOK), 1)
    same = (row // HG_CHUNK) == (col // HG_CHUNK)
    return jnp.logical_and(same, col <= row), jnp.logical_and(same, col >= row)


def _row_in_chunk():
    return lax.broadcasted_iota(jnp.int32, (HG_TOK, LANE), 0) % HG_CHUNK


def _chunk_cumsum(x, reverse=False):
    ric = _row_in_chunk()
    shift = 1
    while shift < HG_CHUNK:
        if reverse:
            x = x + jnp.where(ric < HG_CHUNK - shift, pltpu.roll(x, HG_TOK - shift, 0), 0.0)
        else:
            x = x + jnp.where(ric >= shift, pltpu.roll(x, shift, 0), 0.0)
        shift *= 2
    return x


def _chunk_rows(rows):
    stacked = jnp.concatenate([r[None] for r in rows], axis=0)
    return jnp.broadcast_to(stacked, (HG_NCH, HG_CHUNK, LANE)).reshape(HG_TOK, LANE)


def _chunk_slices(x):
    return [x[j * HG_CHUNK:(j + 1) * HG_CHUNK] for j in range(HG_NCH)]


def _hgrn_common(tbl, hf, hq):
    lb = _sigmoid(tbl[1:2] - tbl[0:1])
    sig = _sigmoid(hf)
    f = lb + (1.0 - lb) * sig
    sq = _sigmoid(hq)
    q, k = hq * sq, 1.0 - f
    b = _chunk_cumsum(jnp.log(f))
    last = [b[(j + 1) * HG_CHUNK - 1:(j + 1) * HG_CHUNK] for j in range(HG_NCH)]
    bl = _chunk_rows(last)
    e_b, e_nb, e_rem = jnp.exp(b), jnp.exp(-b), jnp.exp(bl - b)
    e_last = [jnp.exp(r) for r in last]
    return dict(lb=lb, sig=sig, f=f, sq=sq, q=q, k=k, e_b=e_b, e_nb=e_nb, e_rem=e_rem, e_last=e_last,
                qd=q * e_b, kd=k * e_nb, ku=k * e_rem)


def _hgrn_fwd(proj, lb_table, norm_w, mix_in, comms=()):
    bsz, seq, _ = proj.shape
    nstep = seq // HG_TOK

    def body(tbl_ref, nw_ref, q_ref, f_ref, i_ref, g_ref, mix_ref, o_ref, rec_ref, st_ref, s_scr):
        @pl.when(pl.program_id(2) == 0)
        def _():
            s_scr[...] = jnp.zeros_like(s_scr)

        lower, _ = _block_masks()
        for hp in range(HG_HPS):
            ls = slice(hp * LANE, (hp + 1) * LANE)
            v, hg = i_ref[:, ls], g_ref[:, ls]
            t = _hgrn_common(tbl_ref[:, ls], f_ref[:, ls], q_ref[:, ls])
            a = jnp.where(lower, _dot(t["qd"], t["kd"], NT_DIMS), 0.0)
            o_intra = _dot(a, v)
            v_c, ku_c, qd_c = [_chunk_slices(z.astype(BF16)) for z in (v, t["ku"], t["qd"])]
            updates = [_dot(v_c[j], ku_c[j], TN_DIMS) for j in range(HG_NCH)]
            st = s_scr[hp]
            states = []
            for j in range(HG_NCH):
                states.append(st)
                st = st * t["e_last"][j] + updates[j]
            s_scr[hp] = st
            o = o_intra + jnp.concatenate([_dot(qd_c[j], states[j], NT_DIMS) for j in range(HG_NCH)], axis=0)
            for j in range(HG_NCH):
                st_ref[hp, j] = states[j]
            o_ref[:, ls] = o
            y, _, _ = _rms_fwd(o, nw_ref[...])
            rec_ref[:, ls] = (y * (hg * _sigmoid(hg))).astype(BF16)

    width = HG_HPS * LANE
    slab = lambda first: pl.BlockSpec((None, HG_TOK, width), lambda b, h, t: (b, t, first // HG_HPS + h))
    head_out = pl.BlockSpec((None, HG_TOK, width), lambda b, h, t: (b, t, h))
    mix_out = pl.BlockSpec((None, HG_TOK, width), lambda b, h, t: (b, t, ATT_WIDTH // width + h))
    return _call(
        body, (lb_table, norm_w, proj, proj, proj, proj, mix_in), name="hgrn_fwd", grid=(bsz, HG_HEADS // HG_HPS, nstep),
        in_specs=[pl.BlockSpec((2, width), lambda b, h, t: (0, h)), pl.BlockSpec((1, LANE), lambda b, h, t: (0, 0)),
                  slab(HG_Q0), slab(HG_F0), slab(HG_I0), slab(HG_G0), pl.BlockSpec(memory_space=pl.ANY)],
        out_specs=[head_out, mix_out,
                   pl.BlockSpec((None, HG_HPS, HG_NCH, LANE, LANE), lambda b, h, t: (b, h, t, 0, 0))],
        out_shape=[_sds((bsz, seq, HG_WIDTH), F32), _sds(mix_in.shape, BF16),
                   _sds((bsz, HG_HEADS, seq // HG_CHUNK, LANE, LANE), F32)],
        scratch_shapes=[pltpu.VMEM((HG_HPS, LANE, LANE), F32)],
        sem=("parallel", "parallel", "arbitrary"), comms=comms, aliases={6: 1})


def _out_proj_fused(cat, w_out, x, post_w, g1, pre_w, sc2, sh2):
    tm = 512

    def epi(mix, ex, outs):
        x_ref, pw_ref, g1_ref, w2_ref, sc_ref, sh_ref = ex
        outs[0][...] = mix
        n1, _, _ = _rms_fwd(mix, pw_ref[...])
        x1 = x_ref[...] + g1_ref[...] * n1
        outs[1][...] = x1
        y2, _, _ = _rms_fwd(x1, w2_ref[...])
        outs[2][...] = (y2 * (1.0 + sc_ref[...]) + sh_ref[...]).astype(BF16)

    return _mm_rows(cat, w_out, name="out_proj", tm=tm, extra=(x, post_w, g1, pre_w, sc2, sh2),
                    extra_specs=[_tok_spec(tm), _vec_spec(), _row_spec(), _vec_spec(), _row_spec(), _row_spec()],
                    out_specs=[_tok_spec(tm), _tok_spec(tm), _tok_spec(tm)],
                    out_shape=[_sds(x.shape, F32), _sds(x.shape, F32), _sds(x.shape, BF16)], epi=epi)


def _acc_out(ref, first, value):
    @pl.when(first)
    def _():
        ref[...] = value

    @pl.when(jnp.logical_not(first))
    def _():
        ref[...] += value


def _down_proj_fused(r, w_down, x1, post_w, g2, target):
    tm = 512
    bsz = x1.shape[0]

    def pro(r_ref, ex, outs):
        rv = r_ref[...]
        return rv * rv

    def epi(down, ex, outs):
        x1_ref, w_ref, g2_ref, t_ref = ex
        loss_ref, dy_ref, dd_ref, dg2_ref, dw_ref = outs
        w, g2v = w_ref[...], g2_ref[...]
        n2, dh, rstd = _rms_fwd(down, w)
        err = x1_ref[...] + g2v * n2 - t_ref[...]
        part = (0.5 / D_MODEL) * jnp.sum(jnp.sum(err * err, axis=-1, keepdims=True), axis=0, keepdims=True)
        loss_ref[...] += jnp.broadcast_to(part, (1, LANE))
        dy = err * (1.0 / D_MODEL)
        dy_ref[...] = dy
        dg2_ref[...] += _colsum(dy * n2)
        dd, dw_rows = _rms_bwd(dy * g2v, dh, rstd, w)
        dd_ref[...] = dd.astype(BF16)
        dw_ref[...] += _colsum(dw_rows)

    return _mm_rows(r, w_down, name="down_proj", tm=tm, extra=(x1, post_w, g2, target),
                    extra_specs=[_tok_spec(tm), _vec_spec(), _row_spec(), _tok_spec(tm)],
                    out_specs=[_vec_spec(LANE), _tok_spec(tm), _tok_spec(tm), _row_spec(), _vec_spec()],
                    out_shape=[_sds((1, LANE), F32), _sds(x1.shape, F32), _sds(x1.shape, BF16), _sds((bsz, 1, D_MODEL), F32),
                               _sds((1, D_MODEL), F32)], pro=pro, epi=epi, parts=2, zero_per_seq=(3,), zero_once=(0, 4),
                    vmem_limit=VMEM_LIMIT_BIG)


def _up_bwd_fused(dpre, w_up4, dy, x1, mix, pre_w, sc2, post_w, g1, comms=()):
    tm = 512
    bsz = x1.shape[0]

    def epi(dh2v, ex, outs):
        dy_ref, x1_ref, mix_ref, w2_ref, sc_ref, pw_ref, g1_ref = ex
        dx1_ref, dmix_ref, dsc_ref, dsh_ref, dg1_ref, dw2_ref, dpw_ref = outs
        w2, pw = w2_ref[...], pw_ref[...]
        y2, xh2, rstd2 = _rms_fwd(x1_ref[...], w2)
        dsh_ref[...] += _colsum(dh2v)
        dsc_ref[...] += _colsum(dh2v * y2)
        dx1n, dw_rows = _rms_bwd(dh2v * (1.0 + sc_ref[...]), xh2, rstd2, w2)
        dw2_ref[...] += _colsum(dw_rows)
        dx1 = dy_ref[...] + dx1n
        dx1_ref[...] = dx1
        n1, mh, rstd1 = _rms_fwd(mix_ref[...], pw)
        dg1_ref[...] += _colsum(dx1 * n1)
        dmix, dpw_rows = _rms_bwd(dx1 * g1_ref[...], mh, rstd1, pw)
        dmix_ref[...] = dmix.astype(BF16)
        dpw_ref[...] += _colsum(dpw_rows)

    row_shape = _sds((bsz, 1, D_MODEL), F32)
    vec_shape = _sds((1, D_MODEL), F32)
    return _mm_rows(dpre, w_up4, name="up_bwd", tm=tm, extra=(dy, x1, mix, pre_w, sc2, post_w, g1),
                    extra_specs=[_tok_spec(tm), _tok_spec(tm), _tok_spec(tm), _vec_spec(), _row_spec(), _vec_spec(), _row_spec()],
                    out_specs=[_tok_spec(tm), _tok_spec(tm), _row_spec(), _row_spec(), _row_spec(), _vec_spec(), _vec_spec()],
                    out_shape=[_sds(x1.shape, F32), _sds(x1.shape, BF16), row_shape, row_shape, row_shape, vec_shape, vec_shape],
                    epi=epi, b_chunks=w_up4.shape[0], comms=comms, parts=2, zero_per_seq=(2, 3, 4), zero_once=(5, 6),
                    vmem_limit=VMEM_LIMIT_BIG)


def _norm1_bwd(dh1, dx1, x, pre_w, sc1, tm=512):
    bsz, seq, _ = x.shape

    def body(dh_ref, dx1_ref, x_ref, w_ref, sc_ref, gx_ref, dsc_ref, dsh_ref, dw_ref):
        b, i = pl.program_id(0), pl.program_id(1)
        w = w_ref[...]
        dh = dh_ref[...]
        y, xh, rstd = _rms_fwd(x_ref[...], w)
        _acc_out(dsh_ref, i == 0, _colsum(dh))
        _acc_out(dsc_ref, i == 0, _colsum(dh * y))
        dx, dw_rows = _rms_bwd(dh * (1.0 + sc_ref[...]), xh, rstd, w)
        _acc_out(dw_ref, jnp.logical_and(b == 0, i == 0), _colsum(dw_rows))
        gx_ref[...] = dx1_ref[...] + dx

    row_shape = _sds((bsz, 1, D_MODEL), F32)
    return pl.pallas_call(
        body, name="norm1_bwd", grid=(bsz, seq // tm),
        in_specs=[_tok_spec(tm), _tok_spec(tm), _tok_spec(tm), _vec_spec(), _row_spec()],
        out_specs=[_tok_spec(tm), _row_spec(), _row_spec(), _vec_spec()],
        out_shape=[_sds(x.shape, F32), row_shape, row_shape, _sds((1, D_MODEL), F32)],
        compiler_params=_params("arbitrary", "arbitrary"),
    )(dh1, dx1, x, pre_w, sc1)


def _hgrn_bwd(dcat, proj, o_raw, states, lb_table, norm_w, comms=()):
    bsz, seq, _ = proj.shape
    nstep = seq // HG_TOK
    rec0 = ATT_WIDTH // LANE

    def body(tbl_ref, nw_ref, dr_ref, q_ref, f_ref, i_ref, g_ref, o_ref, st_ref,
             dq_ref, df_ref, di_ref, dg_ref, dlb_ref, dnw_ref, ds_scr):
        h, b, t = pl.program_id(0), pl.program_id(1), pl.program_id(2)

        @pl.when(t == 0)
        def _():
            ds_scr[...] = jnp.zeros_like(ds_scr)

        lower, upper = _block_masks()
        dlb_parts = []
        dnw_acc = jnp.zeros((1, LANE), F32)
        for hp in range(HG_HPS):
            ls = slice(hp * LANE, (hp + 1) * LANE)
            hq, v, hg = q_ref[:, ls], i_ref[:, ls], g_ref[:, ls]
            nw = nw_ref[...]
            c = _hgrn_common(tbl_ref[:, ls], f_ref[:, ls], hq)
            qd, kd, ku = c["qd"], c["kd"], c["ku"]
            y, on, rstd = _rms_fwd(o_ref[:, ls], nw)
            sg = _sigmoid(hg)
            dr = dr_ref[:, ls]
            dg_ref[:, ls] = (dr * y * (sg * (1.0 + hg * (1.0 - sg)))).astype(BF16)
            do, dnw_rows = _rms_bwd(dr * (hg * sg), on, rstd, nw)
            at = jnp.where(upper, _dot(kd, qd, NT_DIMS), 0.0)
            da = jnp.where(lower, _dot(do, v, NT_DIMS), 0.0)
            dat = jnp.where(upper, _dot(v, do, NT_DIMS), 0.0)
            dv = _dot(at, do)
            dqd = _dot(da, kd)
            dkd = _dot(dat, qd)
            do_c, qd_c, v_c, ku_c = [_chunk_slices(z.astype(BF16)) for z in (do, qd, v, ku)]
            outer = [_dot(do_c[j], qd_c[j], TN_DIMS) for j in range(HG_NCH)]
            ds = ds_scr[hp]
            ds_after = [None] * HG_NCH
            for j in reversed(range(HG_NCH)):
                ds_after[j] = ds
                ds = outer[j] + ds * c["e_last"][j]
            ds_scr[hp] = ds
            states = [st_ref[hp, j] for j in range(HG_NCH)]
            dv = dv + jnp.concatenate([_dot(ku_c[j], ds_after[j], NT_DIMS) for j in range(HG_NCH)], axis=0)
            dqd = dqd + jnp.concatenate([_dot(do_c[j], states[j]) for j in range(HG_NCH)], axis=0)
            dku = jnp.concatenate([_dot(v_c[j], ds_after[j]) for j in range(HG_NCH)], axis=0)
            dku_ku = dku * ku
            dbl = [_colsum(states[j] * ds_after[j]) * c["e_last"][j] + _colsum(dku_ku[j * HG_CHUNK:(j + 1) * HG_CHUNK])
                   for j in range(HG_NCH)]
            dk = dkd * c["e_nb"] + dku * c["e_rem"]
            db = dqd * qd - dkd * kd - dku_ku + jnp.where(_row_in_chunk() == HG_CHUNK - 1, _chunk_rows(dbl), 0.0)
            dfv = _chunk_cumsum(db, reverse=True) / c["f"] - dk
            sig, sq = c["sig"], c["sq"]
            df_ref[:, ls] = (dfv * (1.0 - c["lb"]) * sig * (1.0 - sig)).astype(BF16)
            dq_ref[:, ls] = (dqd * c["e_b"] * (sq * (1.0 + hq * (1.0 - sq)))).astype(BF16)
            di_ref[:, ls] = dv.astype(BF16)
            dlb_parts.append(_colsum(dfv * (1.0 - sig)))
            dnw_acc = dnw_acc + _colsum(dnw_rows)
        _acc_out(dlb_ref, jnp.logical_and(b == 0, t == 0), jnp.concatenate(dlb_parts, axis=1))
        _acc_out(dnw_ref, jnp.logical_and(h == 0, jnp.logical_and(b == 0, t == 0)), dnw_acc)

    rev = lambda t: nstep - 1 - t
    width = HG_HPS * LANE
    slab = lambda first: pl.BlockSpec((None, HG_TOK, width), lambda h, b, t: (b, rev(t), first // HG_HPS + h))
    head = pl.BlockSpec((None, HG_TOK, width), lambda h, b, t: (b, rev(t), h))
    grad_shape = _sds((bsz, seq, HG_WIDTH), BF16)
    return _call(
        body, (lb_table, norm_w, dcat, proj, proj, proj, proj, o_raw, states), name="hgrn_bwd",
        grid=(HG_HEADS // HG_HPS, bsz, nstep),
        in_specs=[pl.BlockSpec((2, width), lambda h, b, t: (0, h)), pl.BlockSpec((1, LANE), lambda h, b, t: (0, 0)),
                  slab(rec0), slab(HG_Q0), slab(HG_F0), slab(HG_I0), slab(HG_G0), head,
                  pl.BlockSpec((None, HG_HPS, HG_NCH, LANE, LANE), lambda h, b, t: (b, h, rev(t), 0, 0))],
        out_specs=[head, head, head, head, pl.BlockSpec((1, width), lambda h, b, t: (0, h)),
                   pl.BlockSpec((1, LANE), lambda h, b, t: (0, 0))],
        out_shape=[grad_shape, grad_shape, grad_shape, grad_shape, _sds((1, HG_WIDTH), F32), _sds((1, LANE), F32)],
        scratch_shapes=[pltpu.VMEM((HG_HPS, LANE, LANE), F32)],
        sem=("arbitrary", "arbitrary", "arbitrary"), comms=comms)


def _attn_bwd(dcat, raw, w_norm, qh, kh, vh, lse, sinks, tables, comms=()):
    bsz, nblk = qh.shape[0], qh.shape[1]
    seq = nblk * WINDOW
    nstep = nblk // ATT_BPS
    half = ROPE_DIM // 2

    def body(sink_ref, da_ref, raw_ref, w_ref, q_ref, kc_ref, kp_ref, vc_ref, vp_ref, l_ref, c_ref, u_ref, d_ref,
             o_ref, dw_ref, dsink_ref, carry_k, carry_v):
        b, i = pl.program_id(0), pl.program_id(1)
        first = jnp.logical_and(b == 0, i == 0)

        @pl.when(i == 0)
        def _():
            carry_k[...] = jnp.zeros_like(carry_k)
            carry_v[...] = jnp.zeros_like(carry_v)

        w = w_ref[...]
        _, on, rstd = _rms_fwd(raw_ref[...], w)
        do_step, dw_rows = _rms_bwd(da_ref[...], on, rstd, w)
        _acc_out(dw_ref, first, _colsum(dw_rows))
        lane8 = lax.broadcasted_iota(jnp.int32, (1, ATT_Q_HEADS), 1)
        dsink = jnp.zeros((1, ATT_Q_HEADS), F32)
        from_next_k, from_next_v = carry_k[...], carry_v[...]
        for blk in reversed(range(ATT_BPS)):
            tok = slice(blk * WINDOW, (blk + 1) * WINDOW)
            mask = _band_mask(True if blk else i < nstep - 1)
            raw_v, do_all = raw_ref[tok, :], do_step[tok]
            c, u, d = c_ref[tok, :], u_ref[tok, :], d_ref[tok, :]

            def unrope(g):
                return (g * c + pltpu.roll(g * u, LANE - half, 1) + pltpu.roll(g * d, half, 1)).astype(BF16)

            dq_parts, dk_own, dk_before, dv_own, dv_before = [], [], [], [], []
            for g in range(ATT_KV_HEADS):
                heads = [slice((g * ATT_GROUP + hh) * ATT_HEAD_DIM, (g * ATT_GROUP + hh + 1) * ATT_HEAD_DIM)
                         for hh in range(ATT_GROUP)]
                q = q_ref[blk, g]
                keys, vals = _band(kp_ref, kc_ref, g, blk), _band(vp_ref, vc_ref, g, blk)
                do_g = jnp.concatenate([do_all[:, hs] for hs in heads], axis=0)
                dsum = jnp.concatenate([jnp.sum(do_all[:, hs] * raw_v[:, hs], axis=-1, keepdims=True) for hs in heads], axis=0)
                lse_g = jnp.concatenate([l_ref[tok, g * ATT_GROUP + hh:g * ATT_GROUP + hh + 1] for hh in range(ATT_GROUP)], axis=0)
                p = jnp.where(mask, jnp.exp(_dot(q, keys, NT_DIMS) * ATT_SCALE - lse_g), 0.0)
                sink_part = jnp.exp(_sink_column(sink_ref, g) - lse_g) * dsum
                for hh in range(ATT_GROUP):
                    head_sum = jnp.sum(sink_part[hh * WINDOW:(hh + 1) * WINDOW], axis=0, keepdims=True)
                    dsink = dsink - jnp.where(lane8 == g * ATT_GROUP + hh, head_sum, 0.0)
                ds = p * (_dot(do_g, vals, NT_DIMS) - dsum) * ATT_SCALE
                dq_g = _dot(ds, keys)
                dq_parts += [dq_g[hh * WINDOW:(hh + 1) * WINDOW] for hh in range(ATT_GROUP)]
                dk_g = _dot(ds, q, TN_DIMS)
                dv_g = _dot(p, do_g, TN_DIMS)
                dk_before.append(dk_g[:WINDOW])
                dk_own.append(dk_g[WINDOW:])
                dv_before.append(dv_g[:WINDOW])
                dv_own.append(dv_g[WINDOW:])
            per_slab = LANE // ATT_HEAD_DIM
            for s in range(ATT_WIDTH // LANE):
                slab = jnp.concatenate(dq_parts[s * per_slab:(s + 1) * per_slab], axis=1)
                o_ref[tok, s * LANE:(s + 1) * LANE] = unrope(slab)
            o_ref[tok, ATT_WIDTH:ATT_WIDTH + LANE] = unrope(jnp.concatenate(dk_own, axis=1) + from_next_k)
            o_ref[tok, ATT_WIDTH + LANE:ATT_COLS] = (jnp.concatenate(dv_own, axis=1) + from_next_v).astype(BF16)
            from_next_k, from_next_v = jnp.concatenate(dk_before, axis=1), jnp.concatenate(dv_before, axis=1)
        carry_k[...] = from_next_k
        carry_v[...] = from_next_v
        _acc_out(dsink_ref, first, dsink)

    rows = ATT_BPS * WINDOW
    rev = lambda i: nstep - 1 - i
    cur = lambda width: pl.BlockSpec((None, rows, width), lambda b, i: (b, rev(i), 0))
    q_spec = pl.BlockSpec((None, ATT_BPS, ATT_KV_HEADS, GROUP_ROWS, ATT_HEAD_DIM), lambda b, i: (b, rev(i), 0, 0, 0))
    kv_cur = pl.BlockSpec((None, ATT_KV_HEADS, rows, ATT_HEAD_DIM), lambda b, i: (b, 0, rev(i), 0))
    kv_prev = pl.BlockSpec((None, ATT_KV_HEADS, WINDOW, ATT_HEAD_DIM), lambda b, i: (b, 0, jnp.maximum(ATT_BPS * rev(i) - 1, 0), 0))
    tab = pl.BlockSpec((rows, LANE), lambda b, i: (rev(i), 0))
    return _call(
        body, (sinks, dcat, raw, w_norm, qh, kh, kh, vh, vh, lse, *tables), name="attn_bwd", grid=(bsz, nstep),
        in_specs=[pl.BlockSpec(memory_space=pltpu.SMEM), cur(ATT_WIDTH), cur(ATT_WIDTH), _vec_spec(ATT_WIDTH), q_spec,
                  kv_cur, kv_prev, kv_cur, kv_prev, cur(ATT_Q_HEADS), tab, tab, tab],
        out_specs=[cur(ATT_COLS), _vec_spec(ATT_WIDTH), _vec_spec(ATT_Q_HEADS)],
        out_shape=[_sds((bsz, seq, ATT_COLS), BF16), _sds((1, ATT_WIDTH), F32), _sds((1, ATT_Q_HEADS), F32)],
        scratch_shapes=[pltpu.VMEM((WINDOW, LANE), F32), pltpu.VMEM((WINDOW, LANE), F32)],
        sem=("arbitrary", "arbitrary"), comms=comms)


def _other_chips(x, y):
    return [(1 - x, y), (x, 1 - y), (1 - x, 1 - y)]


def _sem_pair(n):
    return [pltpu.SemaphoreType.DMA((n,)), pltpu.SemaphoreType.DMA((n,))]


def _rows_of(ref, rows):
    return ref if rows is None else ref.at[pl.ds(rows[0], rows[1])]


def _plan_chip_gather(blocks, bufs, rows=None, forward_rows=None):
    n = len(blocks)

    def copies(ins, outs, sems):
        x, y, c = _mesh_pos()
        sends, lands = [], []
        for a in range(n):
            for j, chip in enumerate(_other_chips(x, y)):
                k = 3 * a + j
                sends.append(pltpu.make_async_remote_copy(
                    src_ref=_rows_of(ins[a], rows), dst_ref=_rows_of(outs[a].at[4 * x + 2 * y + c], rows), send_sem=sems[0].at[k],
                    recv_sem=sems[1].at[k], device_id=(*chip, c), device_id_type=MESH))
                slot = _rows_of(outs[a].at[4 * chip[0] + 2 * chip[1] + c], rows)
                lands.append(pltpu.make_async_remote_copy(
                    src_ref=slot, dst_ref=slot, send_sem=sems[0].at[k], recv_sem=sems[1].at[k],
                    device_id=(*chip, c), device_id_type=MESH))
                if forward_rows is not None:
                    k = 3 * (n + a) + j
                    mine = _rows_of(outs[a].at[4 * chip[0] + 2 * chip[1] + c], forward_rows)
                    sends.append(pltpu.make_async_remote_copy(
                        src_ref=mine, dst_ref=mine, send_sem=sems[0].at[k], recv_sem=sems[1].at[k],
                        device_id=(x, y, 1 - c), device_id_type=MESH))
                    theirs = _rows_of(outs[a].at[4 * chip[0] + 2 * chip[1] + 1 - c], forward_rows)
                    lands.append(pltpu.make_async_remote_copy(
                        src_ref=theirs, dst_ref=theirs, send_sem=sems[0].at[k], recv_sem=sems[1].at[k],
                        device_id=(x, y, 1 - c), device_id_type=MESH))
        return sends, lands

    def start(ins, outs, sems):
        for cp in copies(ins, outs, sems)[0]:
            cp.start()

    def finish(ins, outs, sems):
        sends, lands = copies(ins, outs, sems)
        for cp in lands:
            cp.wait_recv()
        for cp in sends:
            cp.wait_send()

    n_sems = 3 * n * (2 if forward_rows is not None else 1)
    return _Comm(list(blocks) + list(bufs), [_sds(b.shape, b.dtype) for b in bufs], _sem_pair(n_sems), start, finish,
                 aliases=[(n + a, a) for a in range(n)])


def _plan_pair_forward(bufs, rows=None):
    n = len(bufs)

    def copies(outs, sems):
        x, y, c = _mesh_pos()
        sends, lands = [], []
        for a in range(n):
            for j, chip in enumerate(_other_chips(x, y)):
                k = 3 * a + j
                slot = _rows_of(outs[a].at[4 * chip[0] + 2 * chip[1] + c], rows)
                sends.append(pltpu.make_async_remote_copy(
                    src_ref=slot, dst_ref=slot, send_sem=sems[0].at[k], recv_sem=sems[1].at[k],
                    device_id=(x, y, 1 - c), device_id_type=MESH))
                theirs = _rows_of(outs[a].at[4 * chip[0] + 2 * chip[1] + 1 - c], rows)
                lands.append(pltpu.make_async_remote_copy(
                    src_ref=theirs, dst_ref=theirs, send_sem=sems[0].at[k], recv_sem=sems[1].at[k],
                    device_id=(x, y, 1 - c), device_id_type=MESH))
        return sends, lands

    def start(ins, outs, sems):
        for cp in copies(outs, sems)[0]:
            cp.start()

    def finish(ins, outs, sems):
        sends, lands = copies(outs, sems)
        for cp in lands:
            cp.wait_recv()
        for cp in sends:
            cp.wait_send()

    return _Comm(list(bufs), [_sds(b.shape, b.dtype) for b in bufs], _sem_pair(3 * n), start, finish,
                 aliases=[(a, a) for a in range(n)])


def _plan_pair(arrays, other_half):
    n = len(arrays)
    per = N_CHIPS if other_half == "chip_major" else 1

    def copies(ins, outs, sems):
        x, y, c = _mesh_pos()
        out = []
        for a in range(n):
            for k in range(per):
                if other_half == "chip_major":
                    src, dst = ins[a].at[k, 1 - c], outs[a].at[k]
                else:
                    src, dst = (ins[a].at[1 - c] if other_half else ins[a]), outs[a]
                out.append(pltpu.make_async_remote_copy(
                    src_ref=src, dst_ref=dst, send_sem=sems[0].at[per * a + k], recv_sem=sems[1].at[per * a + k],
                    device_id=(x, y, 1 - c), device_id_type=MESH))
        return out

    def start(ins, outs, sems):
        for cp in copies(ins, outs, sems):
            cp.start()

    def finish(ins, outs, sems):
        for cp in copies(ins, outs, sems):
            cp.wait()

    if other_half == "chip_major":
        shapes = [_sds((a.shape[0],) + a.shape[2:], a.dtype) for a in arrays]
    else:
        shapes = [_sds(a.shape[1:] if other_half else a.shape, a.dtype) for a in arrays]
    return _Comm(list(arrays), shapes, _sem_pair(per * n), start, finish)


def _plan_chip_exchange(arrays):
    n = len(arrays)

    def copies(ins, outs, sems):
        x, y, c = _mesh_pos()
        sends, lands = [], []
        for a in range(n):
            for j, chip in enumerate(_other_chips(x, y)):
                k = 3 * a + j
                sends.append(pltpu.make_async_remote_copy(
                    src_ref=ins[a].at[2 * chip[0] + chip[1]], dst_ref=outs[a].at[2 * x + y], send_sem=sems[0].at[k],
                    recv_sem=sems[1].at[k], device_id=(*chip, c), device_id_type=MESH))
                slot = outs[a].at[2 * chip[0] + chip[1]]
                lands.append(pltpu.make_async_remote_copy(
                    src_ref=slot, dst_ref=slot, send_sem=sems[0].at[k], recv_sem=sems[1].at[k],
                    device_id=(*chip, c), device_id_type=MESH))
        return sends, lands

    def start(ins, outs, sems):
        for cp in copies(ins, outs, sems)[0]:
            cp.start()

    def finish(ins, outs, sems):
        sends, lands = copies(ins, outs, sems)
        for cp in lands:
            cp.wait_recv()
        for cp in sends:
            cp.wait_send()

    return _Comm(list(arrays), [_sds(a.shape, a.dtype) for a in arrays], _sem_pair(3 * n), start, finish)


def _comm_only(comms, name):
    return _call(lambda: None, (), name=name, grid=(), in_specs=[], out_specs=[], out_shape=[], sem=(), comms=comms)[1]


def _allgather8(arrays, name):
    return _comm_only([_plan_allgather8(arrays)], name)[0]


def _plan_allgather8(arrays):
    n = len(arrays)

    def parts(ins, outs, sems):
        send_sems, recv_sems, local_sems = sems
        x, y, c = _mesh_pos()
        me, sibling = (x, y, c), (x, y, 1 - c)
        chips = _other_chips(x, y)

        def copy(a, k, block, to, src=None):
            dst = outs[a].at[4 * block[0] + 2 * block[1] + block[2]]
            return pltpu.make_async_remote_copy(
                src_ref=dst if src is None else src, dst_ref=dst, send_sem=send_sems.at[7 * a + k],
                recv_sem=recv_sems.at[7 * a + k], device_id=to, device_id_type=MESH)

        mine = [pltpu.make_async_copy(ins[a], outs[a].at[4 * x + 2 * y + c], local_sems.at[a]) for a in range(n)]
        first = []
        for a in range(n):
            first.append(copy(a, 0, me, sibling, src=ins[a]))
            first += [copy(a, 1 + j, me, (*chip, c), src=ins[a]) for j, chip in enumerate(chips)]
        return copy, mine, first, me, sibling, chips, c

    def start(ins, outs, sems):
        _, mine, first, *_ = parts(ins, outs, sems)
        for cp in mine + first:
            cp.start()

    def finish(ins, outs, sems):
        copy, mine, first, me, sibling, chips, c = parts(ins, outs, sems)
        passed = []
        for j, chip in enumerate(chips):
            for a in range(n):
                copy(a, 1 + j, (*chip, c), me).wait_recv()
                fwd = copy(a, 4 + j, (*chip, c), sibling)
                fwd.start()
                passed.append(fwd)
        for a in range(n):
            copy(a, 0, sibling, me).wait_recv()
            for j, chip in enumerate(chips):
                copy(a, 4 + j, (*chip, 1 - c), me).wait_recv()
        for cp in first + passed:
            cp.wait_send()
        for cp in mine:
            cp.wait()

    sems = [pltpu.SemaphoreType.DMA((7 * n,)), pltpu.SemaphoreType.DMA((7 * n,)), pltpu.SemaphoreType.DMA((n,))]
    return _Comm(list(arrays), [_sds((N_DEV,) + a.shape, a.dtype) for a in arrays], sems, start, finish)


def _pair_sum(g, q, core, name, chip_major=False):
    rows, cols = g.shape[2:]
    tr = _row_tile(rows)

    def body(core_ref, g_ref, q_ref, o_ref):
        o_ref[...] = (g_ref[...] + q_ref[...]).astype(BF16)

    blk = pl.BlockSpec((None, tr, cols), lambda k, i, core_ref: (k, i, 0))
    if chip_major:
        own = pl.BlockSpec((None, None, tr, cols), lambda k, i, core_ref: (k, core_ref[0], i, 0))
    else:
        own = pl.BlockSpec((None, None, tr, cols), lambda k, i, core_ref: (core_ref[0], k, i, 0))
    return pl.pallas_call(
        body, name=name,
        grid_spec=pltpu.PrefetchScalarGridSpec(num_scalar_prefetch=1, grid=(N_CHIPS, rows // tr), in_specs=[own, blk], out_specs=blk),
        out_shape=_sds((N_CHIPS, rows, cols), BF16), compiler_params=_params("parallel", "parallel"),
    )(core, g, q)


def _sum_chips(own, landed, chip, name):
    _, rows, cols = own.shape
    tr = _row_tile(rows)

    def body(chip_ref, own_ref, a_ref, b_ref, c_ref, o_ref):
        acc = own_ref[...].astype(F32) + a_ref[...].astype(F32)
        o_ref[...] = (acc + b_ref[...].astype(F32)) + c_ref[...].astype(F32)

    blk = lambda flip: pl.BlockSpec((None, tr, cols), lambda i, chip_ref: (jnp.bitwise_xor(chip_ref[0], flip), i, 0))
    return pl.pallas_call(
        body, name=name,
        grid_spec=pltpu.PrefetchScalarGridSpec(num_scalar_prefetch=1, grid=(rows // tr,), in_specs=[blk(0), blk(1), blk(2), blk(3)],
                                               out_specs=pl.BlockSpec((tr, cols), lambda i, chip_ref: (i, 0))),
        out_shape=_sds((rows, cols), F32), compiler_params=_params("parallel"),
    )(chip, own, landed, landed, landed)


SUBLANES = 8


def _tile_rows(n_elems):
    return -(-n_elems // (SUBLANES * LANE)) * SUBLANES


SMALL_ITEMS = (("b_ada", N_MOD * D_MODEL), ("pre_w_mix", D_MODEL), ("post_w_mix", D_MODEL), ("pre_w_mlp", D_MODEL),
               ("post_w_mlp", D_MODEL), ("attn_out_w", ATT_WIDTH), ("hg_norm_w", HG_HEAD_DIM), ("attn_sinks", ATT_Q_HEADS),
               ("lb_0", HG_WIDTH), ("lb_1", HG_WIDTH))
SMALL_AT = {}
for _name, _size in SMALL_ITEMS:
    SMALL_AT[_name] = (sum(r for _, r in SMALL_AT.values()), _tile_rows(_size))
SMALL_ROWS = sum(r for _, r in SMALL_AT.values())
MOD_ROWS = SMALL_AT["b_ada"][1]
PLAIN_ROWS = SMALL_AT["lb_0"][0] - MOD_ROWS
LB_ROWS = SMALL_AT["lb_0"][1]


def _rows(a, nrows=None):
    flat = a.reshape(-1)
    nrows = _tile_rows(flat.shape[0]) if nrows is None else nrows
    return jnp.pad(flat, (0, nrows * LANE - flat.shape[0])).reshape(nrows, LANE)


def _pack_small(vals):
    vals = dict(vals, lb_0=vals["lb_table"][0], lb_1=vals["lb_table"][1])
    return jnp.concatenate([_rows(vals[name], SMALL_AT[name][1]) for name, _ in SMALL_ITEMS], axis=0)


def _unpack_small(p):
    def item(name, shape):
        first = SMALL_AT[name][0]
        size = shape[0] * shape[1]
        return p[first:first + SMALL_AT[name][1]].reshape(-1)[:size].reshape(shape)

    out = {name: item(name, (1, size)) for name, size in SMALL_ITEMS if not name.startswith("lb_")}
    out["lb_table"] = jnp.concatenate([item("lb_0", (1, HG_WIDTH)), item("lb_1", (1, HG_WIDTH))], axis=0)
    return out


def _pack_partials(dmod, plain, d_lb, loss_row):
    return jnp.concatenate([_rows(dmod, dmod.shape[0] * MOD_ROWS)] + [_rows(g) for g in plain] + [_rows(d_lb), _rows(loss_row)], axis=0)


def _small_update(packs, w, m, v, n_seq):
    mod_end = n_seq * MOD_ROWS
    lb_at = mod_end + PLAIN_ROWS
    t0, t1 = SMALL_AT["lb_0"][0], SMALL_AT["lb_1"][0]

    def body(p_ref, w_ref, m_ref, v_ref, g_ref, dl_ref, nm_ref, nv_ref, loss_ref):
        tot = p_ref[0]
        for d in range(1, N_DEV):
            tot = tot + p_ref[d]
        wv = w_ref[...]
        p1 = _sigmoid(wv[t1:t1 + LB_ROWS] - wv[t0:t0 + LB_ROWS])
        s = tot[lb_at:lb_at + LB_ROWS] * p1 * (1.0 - p1)
        g_bias = tot[0:MOD_ROWS]
        for q in range(1, n_seq):
            g_bias = g_bias + tot[q * MOD_ROWS:(q + 1) * MOD_ROWS]
        g = jnp.concatenate([g_bias, tot[mod_end:lb_at], -s, s], axis=0)
        g_ref[...] = g
        dl_ref[...], nm_ref[...], nv_ref[...] = _adamw_math(g, wv, m_ref[...], v_ref[...])
        loss_ref[...] = tot[lb_at + LB_ROWS:lb_at + LB_ROWS + SUBLANES]

    shp = _sds((SMALL_ROWS, LANE), F32)
    return pl.pallas_call(body, name="small_update", out_shape=[shp] * 4 + [_sds((SUBLANES, LANE), F32)],
                          compiler_params=_params())(packs, w, m, v)


def kernel(x, c, w_ada, b_ada, pre_w_mix, w_in, attn_sinks, attn_out_w, lb_table, hg_norm_w, w_out, post_w_mix, pre_w_mlp, w_up, w_down, post_w_mlp, loss_target, m_w_ada, m_b_ada, m_pre_w_mix, m_w_in, m_attn_sinks, m_attn_out_w, m_lb_table, m_hg_norm_w, m_w_out, m_post_w_mix, m_pre_w_mlp, m_w_up, m_w_down, m_post_w_mlp, v_w_ada, v_b_ada, v_pre_w_mix, v_w_in, v_attn_sinks, v_attn_out_w, v_lb_table, v_hg_norm_w, v_w_out, v_post_w_mix, v_pre_w_mlp, v_w_up, v_w_down, v_post_w_mlp):
    xi, yi, ci = _mesh_pos()
    chip = 2 * xi + yi
    dev = 2 * chip + ci
    bsz, seq, _ = x.shape
    ntok = bsz * seq
    ada_cols = w_ada.shape[2]
    core = jnp.reshape(ci, (1,)).astype(jnp.int32)
    chip_idx = jnp.reshape(chip, (1,)).astype(jnp.int32)
    flat = lambda a: a.reshape(ntok, a.shape[-1])
    unflat = lambda a: a.reshape(bsz, seq, a.shape[-1])
    tables = _rope_tables(seq)

    def row_half(w):
        rows = w.shape[1] // 2
        return lax.dynamic_slice_in_dim(w[0], ci * rows, rows, axis=0).astype(BF16)

    def gather_buffer(w):
        rows, cols = w.shape[1] // 2, w.shape[2]
        own = w[0].astype(BF16).reshape(2, rows, cols)
        return lax.dynamic_update_slice(lax.empty((N_DEV, rows, cols), BF16), own, (2 * chip, 0, 0))

    w_in_t, m_in_t, v_in_t = [jnp.transpose(a[0])[None] for a in (w_in, m_w_in, v_w_in)]
    c_g, in_g = _allgather8([c, row_half(w_in_t)], "gather_first")
    c_all = c_g.reshape(N_DEV * bsz, D_MODEL)
    w_in_full = in_g.reshape(IN_COLS, D_MODEL)

    b_cols = lax.dynamic_slice_in_dim(b_ada, chip * ada_cols, ada_cols, axis=1)
    mod_part = _ada_fwd(c_all, w_ada[0], b_cols)
    half_rows = mod_part.shape[0] // 2
    (mod_g,) = _allgather8([lax.dynamic_slice_in_dim(mod_part, ci * half_rows, half_rows, axis=0)], "gather_mod")
    mod_all = mod_g.reshape(N_CHIPS, 2, half_rows, ada_cols).transpose(1, 2, 0, 3).reshape(N_DEV * bsz, N_MOD * D_MODEL)
    mod = lax.dynamic_slice_in_dim(mod_all, dev * bsz, bsz, axis=0)
    sh1, sc1, g1, sh2, sc2, g2 = [mod[:, i * D_MODEL:(i + 1) * D_MODEL].reshape(bsz, 1, D_MODEL) for i in range(N_MOD)]

    up_rows = w_up.shape[1] // 4
    first_half, second_half = (0, up_rows), (up_rows, up_rows)
    (h1, proj, qh, kh, vh), ((out_g,), (up_g,)) = _in_proj_fused(
        x, pre_w_mix, sc1, sh1, w_in_full, tables,
        comms=[_plan_chip_gather([row_half(w_out)], [gather_buffer(w_out)]),
               _plan_chip_gather([row_half(w_up)], [gather_buffer(w_up)], rows=first_half)])
    (attn_raw, cat, lse), ((up_g,), (out_g,)) = _attn_fwd(
        qh, kh, vh, attn_sinks, attn_out_w,
        comms=[_plan_chip_gather([row_half(w_up)], [up_g], rows=second_half, forward_rows=first_half), _plan_pair_forward([out_g])])
    (o_raw, cat, states), ((down_g,), (up_g,)) = _hgrn_fwd(
        proj, lb_table, hg_norm_w, cat,
        comms=[_plan_chip_gather([row_half(w_down)], [gather_buffer(w_down)]), _plan_pair_forward([up_g], rows=second_half)])
    w_out_full = out_g.reshape(D_MODEL, D_MODEL)
    w_up4 = up_g.reshape(N_CHIPS, D_MODEL, D_MODEL)
    mix, x1, h2 = _out_proj_fused(cat, w_out_full, x, post_w_mix, g1, pre_w_mlp, sc2, sh2)
    big_tm = min(ntok, 2048)
    up_spec = pl.BlockSpec((None, D_MODEL, D_MODEL), lambda i, j: (j, 0, 0))
    r, ((down_g,),) = _mm(flat(h2), w_up4, name="up_proj", out_dtype=BF16, tm=big_tm, tn=D_MODEL, n_out=D_FF, b_spec=up_spec,
                          epi=lambda acc: jnp.maximum(acc, 0.0), comms=[_plan_pair_forward([down_g])])
    w_down_full = down_g.reshape(D_FF, D_MODEL)
    square = lambda t: t * t
    loss_row, dy, dd, dg2, d_post_mlp = _down_proj_fused(unflat(r), w_down_full, x1, post_w_mlp, g2, loss_target)

    dpre = _mm(flat(dd), w_down_full, name="down_bwd", out_dtype=BF16, trans_b=True, tm=big_tm, tn=D_MODEL, extra=(r,),
               epi=lambda acc, rt: acc * (2.0 * rt.astype(F32)))
    half_rows = D_MODEL // 2
    g_down = _mm_tn(r, flat(dd), name="down_wgrad", tk=half_rows, tn=D_MODEL, a_fn=square,
                    out_shape=_sds((2, N_CHIPS, half_rows, D_MODEL), F32),
                    out_spec=pl.BlockSpec((None, None, half_rows, D_MODEL), lambda i, j: (i % 2, i // 2, 0, 0)))
    (dx1, dmix, dsc2, dsh2, dg1, d_pre_mlp, d_post_mix), ((q_down,),) = _up_bwd_fused(
        unflat(dpre), w_up4, dy, x1, mix, pre_w_mlp, sc2, post_w_mix, g1, comms=[_plan_pair([g_down], True)])
    g_up = _mm_tn(flat(h2), dpre, name="up_wgrad", tk=D_MODEL, tn=half_rows,
                  out_shape=_sds((2, N_CHIPS, half_rows, D_MODEL), F32),
                  out_spec=pl.BlockSpec((2, None, half_rows, half_rows), lambda i, j: (0, j // 2, 0, j % 2)))
    s_down = _pair_sum(g_down, q_down, core, "pair_sum_down")

    dcat, ((q_up,),) = _mm(flat(dmix), w_out_full, name="out_bwd", out_dtype=F32, trans_b=True, comms=[_plan_pair([g_up], True)])
    dcat = unflat(dcat)
    s_up = _pair_sum(g_up, q_up, core, "pair_sum_up")
    out_rows = D_MODEL // N_CHIPS
    g_out = _mm_tn(flat(cat), flat(dmix), name="out_wgrad", tk=2 * out_rows, tn=half_rows,
                   out_shape=_sds((2, N_CHIPS, out_rows, half_rows), F32),
                   out_spec=pl.BlockSpec((None, 2, out_rows, half_rows), lambda i, j: (j, i, 0, 0)))
    (dhq, dhf, dhi, dhg, d_lb, d_hg_norm), ((x_down,), (q_out,)) = _hgrn_bwd(
        dcat, proj, o_raw, states, lb_table, hg_norm_w, comms=[_plan_chip_exchange([s_down]), _plan_pair([g_out], True)])
    half_down = _sum_chips(s_down, x_down, chip_idx, "sum_chips_down")
    s_out = _pair_sum(g_out, q_out, core, "pair_sum_out")
    (dproj_a, d_attn_out, d_sinks), ((their_down,), (x_up, x_out)) = _attn_bwd(
        dcat, attn_raw, attn_out_w, qh, kh, vh, lse, attn_sinks, tables,
        comms=[_plan_pair([half_down], False), _plan_chip_exchange([s_up, s_out])])
    half_up = _sum_chips(s_up, x_up, chip_idx, "sum_chips_up")
    half_out = _sum_chips(s_out, x_out, chip_idx, "sum_chips_out")
    dproj = flat(jnp.concatenate([dproj_a, dhq, dhf, dhi, dhg], axis=-1))
    in_rows = IN_COLS // N_CHIPS // 2
    g_in = _mm_tn(dproj, flat(h1), name="in_wgrad", tk=2 * LANE, tn=D_MODEL).reshape(N_CHIPS, 2, in_rows, D_MODEL)
    dh1, ((q_in,), (their_up, their_out)) = _mm(
        dproj, w_in_full, name="in_bwd", out_dtype=F32,
        comms=[_plan_pair([g_in], "chip_major"), _plan_pair([half_up, half_out], False)])
    s_in = _pair_sum(g_in, q_in, core, "pair_sum_in", chip_major=True)
    grad_x, dsc1, dsh1, d_pre_mix = _norm1_bwd(unflat(dh1), dx1, x, pre_w_mix, sc1)

    dmod = jnp.concatenate([dsh1, dsc1, dg1, dsh2, dsc2, dg2], axis=-1).reshape(bsz, N_MOD * D_MODEL)
    pack = _pack_partials(dmod, [d_pre_mix, d_post_mix, d_pre_mlp, d_post_mlp, d_attn_out, d_hg_norm, d_sinks], d_lb, loss_row)
    (packs,), (x_in,) = _comm_only([_plan_allgather8([pack]), _plan_chip_exchange([s_in])], "gather_small")
    half_in = _sum_chips(s_in, x_in, chip_idx, "sum_chips_in")
    ((their_in,),) = _comm_only([_plan_pair([half_in], False)], "pair_swap_in")
    w_small = dict(b_ada=b_ada, pre_w_mix=pre_w_mix, post_w_mix=post_w_mix, pre_w_mlp=pre_w_mlp, post_w_mlp=post_w_mlp,
                   attn_out_w=attn_out_w, hg_norm_w=hg_norm_w, attn_sinks=attn_sinks, lb_table=lb_table)
    m_small = dict(b_ada=m_b_ada, pre_w_mix=m_pre_w_mix, post_w_mix=m_post_w_mix, pre_w_mlp=m_pre_w_mlp, post_w_mlp=m_post_w_mlp,
                   attn_out_w=m_attn_out_w, hg_norm_w=m_hg_norm_w, attn_sinks=m_attn_sinks, lb_table=m_lb_table)
    v_small = dict(b_ada=v_b_ada, pre_w_mix=v_pre_w_mix, post_w_mix=v_post_w_mix, pre_w_mlp=v_pre_w_mlp, post_w_mlp=v_post_w_mlp,
                   attn_out_w=v_attn_out_w, hg_norm_w=v_hg_norm_w, attn_sinks=v_attn_sinks, lb_table=v_lb_table)
    *small_packed, loss_rows = _small_update(packs, _pack_small(w_small), _pack_small(m_small), _pack_small(v_small), bsz)
    small_out = [_unpack_small(p) for p in small_packed]
    loss = loss_rows[0, 0]

    dmod_all = packs[:, :bsz * MOD_ROWS, :].reshape(N_DEV * bsz, N_MOD * D_MODEL)
    dmod_cols = lax.dynamic_slice_in_dim(dmod_all, chip * ada_cols, ada_cols, axis=1)
    ada_out = _ada_bwd_adamw(c_all, dmod_cols, w_ada[0], m_w_ada[0], v_w_ada[0])

    big = dict(
        w_in=tuple(jnp.transpose(a) for a in _adamw_halves(half_in, their_in, core, w_in_t[0], m_in_t[0], v_in_t[0], axis=0,
                                                           name="adamw_in")),
        w_up=tuple(_adamw_halves(half_up, their_up, core, w_up[0], m_w_up[0], v_w_up[0], axis=0, name="adamw_up")),
        w_out=tuple(_adamw_halves(half_out, their_out, core, w_out[0], m_w_out[0], v_w_out[0], axis=1, name="adamw_out")),
        w_down=tuple(_adamw_halves(half_down, their_down, core, w_down[0], m_w_down[0], v_w_down[0], axis=0, name="adamw_down")),
        w_ada=tuple(ada_out),
    )
    order = ("w_ada", "b_ada", "pre_w_mix", "w_in", "attn_sinks", "attn_out_w", "lb_table", "hg_norm_w", "w_out", "post_w_mix",
             "pre_w_mlp", "w_up", "w_down", "post_w_mlp")
    outs = [loss, grad_x]
    for kind in range(4):
        for nm in order:
            outs.append(big[nm][kind][None] if nm in big else small_out[kind][nm])
    return tuple(outs)
```

```python
import jax
import jax.numpy as jnp
from jax import lax
from jax.experimental import pallas as pl
from jax.experimental.pallas import tpu as pltpu

F32 = jnp.float32
BF16 = jnp.bfloat16

D_MODEL = 1024
ATT_WIDTH = 512
ATT_HEAD_DIM = 64
ATT_Q_HEADS = 8
ATT_KV_HEADS = 2
ATT_GROUP = ATT_Q_HEADS // ATT_KV_HEADS
ATT_KV_COLS = ATT_KV_HEADS * ATT_HEAD_DIM
WINDOW = 128
ROPE_DIM = 16
ROPE_THETA = 500000.0
HG_WIDTH = 512
MIX_WIDTH = ATT_WIDTH + HG_WIDTH
HG_HEAD_DIM = 128
HG_HEADS = 4
HG_CHUNK = 32
IN_COLS = ATT_WIDTH + 2 * ATT_KV_COLS + 4 * HG_WIDTH
ATT_COLS = ATT_WIDTH + 2 * ATT_KV_COLS
D_FF = 4 * D_MODEL
N_MOD = 6
EPS = 1e-6
ATT_SCALE = ATT_HEAD_DIM ** -0.5

ADAM_LR = 0.001
ADAM_B1 = 0.9
ADAM_B2 = 0.999
ADAM_EPS = 1e-08
ADAM_WD = 0.01
ADAM_STEP = 10

N_CHIPS = 4
N_DEV = 8
LANE = 128
VMEM_LIMIT = 48 * 1024 * 1024
VMEM_LIMIT_BIG = 58 * 1024 * 1024
MESH = pl.DeviceIdType.MESH

NT_DIMS = (((1,), (1,)), ((), ()))
TN_DIMS = (((0,), (0,)), ((), ()))


def _sds(shape, dtype):
    return jax.ShapeDtypeStruct(tuple(shape), dtype)


def _params(*sem, vmem_limit=None):
    return pltpu.CompilerParams(dimension_semantics=sem, vmem_limit_bytes=VMEM_LIMIT if vmem_limit is None else vmem_limit)


def _sigmoid(x):
    return 1.0 / (1.0 + jnp.exp(-x))


def _dot(a, b, dims=None):
    a, b = a.astype(BF16), b.astype(BF16)
    if dims is None:
        return jnp.dot(a, b, preferred_element_type=F32)
    return lax.dot_general(a, b, dims, preferred_element_type=F32)


def _rms_fwd(x, w):
    rstd = lax.rsqrt(jnp.mean(x * x, axis=-1, keepdims=True) + EPS)
    xh = x * rstd
    return xh * w, xh, rstd


def _rms_bwd(dy, xh, rstd, w):
    dxh = dy * w
    dx = rstd * (dxh - xh * jnp.mean(dxh * xh, axis=-1, keepdims=True))
    return dx, dy * xh


def _colsum(x):
    return jnp.sum(x, axis=0, keepdims=True)


def _row_tile(rows, cap=256):
    return max(t for t in range(16, cap + 1, 16) if rows % t == 0)


HBM_SPEC = pl.BlockSpec(memory_space=pltpu.HBM)


def _mesh_pos():
    return lax.axis_index("x"), lax.axis_index("y"), lax.axis_index("c")


class _Comm:
    def __init__(self, ins, outs, sems, start, finish, aliases=()):
        self.ins, self.outs, self.sems = list(ins), list(outs), list(sems)
        self.start, self.finish, self.aliases = start, finish, tuple(aliases)


def _call(body, args, *, name, grid, in_specs, out_specs, out_shape, sem, scratch_shapes=(), comms=(), aliases=None,
          vmem_limit=None):
    scratch_shapes = list(scratch_shapes)
    if not comms:
        return pl.pallas_call(body, name=name, grid=grid, in_specs=in_specs, out_specs=out_specs, out_shape=out_shape,
                              input_output_aliases=dict(aliases or {}), scratch_shapes=scratch_shapes,
                              compiler_params=_params(*sem, vmem_limit=vmem_limit))(*args)
    single = not isinstance(out_shape, (list, tuple))
    out_specs_l = [out_specs] if single else list(out_specs)
    out_shape_l = [out_shape] if single else list(out_shape)
    n_in, n_out, n_scr = len(in_specs), len(out_shape_l), len(scratch_shapes)
    n_ci = [len(cm.ins) for cm in comms]
    n_co = [len(cm.outs) for cm in comms]
    n_cs = [len(cm.sems) for cm in comms]
    aliases = dict(aliases or {})
    for k, cm in enumerate(comms):
        for i, o in cm.aliases:
            aliases[n_in + sum(n_ci[:k]) + i] = n_out + sum(n_co[:k]) + o

    def fused(*refs):
        pos = [0]

        def take(n):
            part = refs[pos[0]:pos[0] + n]
            pos[0] += n
            return part

        ins = take(n_in)
        c_ins = [take(n) for n in n_ci]
        outs = take(n_out)
        c_outs = [take(n) for n in n_co]
        scr = take(n_scr)
        c_sems = [take(n) for n in n_cs]
        first, last = True, True
        for d, size in enumerate(grid):
            first = jnp.logical_and(first, pl.program_id(d) == 0)
            last = jnp.logical_and(last, pl.program_id(d) == size - 1)

        def run(which):
            for cm, ci, co, cs in zip(comms, c_ins, c_outs, c_sems):
                getattr(cm, which)(ci, co, cs)

        if grid:
            pl.when(first)(lambda: run("start"))
        else:
            run("start")
        body(*ins, *outs, *scr)
        if grid:
            pl.when(last)(lambda: run("finish"))
        else:
            run("finish")

    res = pl.pallas_call(
        fused, name=name, grid=grid, in_specs=list(in_specs) + [HBM_SPEC] * sum(n_ci),
        out_specs=out_specs_l + [HBM_SPEC] * sum(n_co), out_shape=out_shape_l + [s for cm in comms for s in cm.outs],
        input_output_aliases=aliases, scratch_shapes=scratch_shapes + [s for cm in comms for s in cm.sems],
        compiler_params=_params(*["arbitrary"] * len(grid), vmem_limit=vmem_limit),
    )(*args, *[a for cm in comms for a in cm.ins])
    main = res[:n_out]
    extra, at = [], n_out
    for n in n_co:
        extra.append(list(res[at:at + n]))
        at += n
    return (main[0] if single else list(main)), extra


def _mm(a, b, *, name, out_dtype, trans_b=False, tm=512, tn=None, extra=(), epi=None, b_spec=None, n_out=None, comms=()):
    m_total, k_total = a.shape
    if n_out is None:
        n_out = b.shape[0] if trans_b else b.shape[1]
    tn = n_out if tn is None else tn
    grid = (m_total // tm, n_out // tn)
    dims = NT_DIMS if trans_b else None

    def body(*refs):
        a_ref, b_ref = refs[0], refs[1]
        extra_refs = refs[2:2 + len(extra)]
        o_ref = refs[2 + len(extra)]
        acc = _dot(a_ref[...], b_ref[...], dims)
        if epi is not None:
            acc = epi(acc, *[r[...] for r in extra_refs])
        o_ref[...] = acc.astype(out_dtype)

    if b_spec is None:
        if trans_b:
            b_spec = pl.BlockSpec((tn, k_total), lambda i, j: (j, 0))
        else:
            b_spec = pl.BlockSpec((k_total, tn), lambda i, j: (0, j))
    in_specs = [pl.BlockSpec((tm, k_total), lambda i, j: (i, 0)), b_spec]
    in_specs += [pl.BlockSpec((tm, tn), lambda i, j: (i, j)) for _ in extra]
    return _call(
        body, (a, b, *extra), name=name, grid=grid, in_specs=in_specs,
        out_specs=pl.BlockSpec((tm, tn), lambda i, j: (i, j)),
        out_shape=_sds((m_total, n_out), out_dtype),
        sem=("parallel", "parallel"), comms=comms)


def _mm_tn(a, b, *, name, tk, tn, a_fn=None, out_shape=None, out_spec=None):
    m_total, k_total = a.shape
    n_total = b.shape[1]
    grid = (k_total // tk, n_total // tn)

    def body(a_ref, b_ref, o_ref):
        av = a_ref[...]
        part = _dot(av if a_fn is None else a_fn(av), b_ref[...], TN_DIMS)
        o_ref[...] = part.reshape(o_ref.shape)

    if out_shape is None:
        out_shape = _sds((k_total, n_total), F32)
        out_spec = pl.BlockSpec((tk, tn), lambda i, j: (i, j))
    return pl.pallas_call(
        body, name=name, grid=grid,
        in_specs=[pl.BlockSpec((m_total, tk), lambda i, j: (0, i)), pl.BlockSpec((m_total, tn), lambda i, j: (0, j))],
        out_specs=out_spec, out_shape=out_shape,
        compiler_params=_params("parallel", "parallel"),
    )(a, b)


def _ada_fwd(c_all, w_shard, b_shard):
    nb, ncol = c_all.shape[0], w_shard.shape[1]
    tn = 512

    def body(c_ref, w_ref, b_ref, o_ref):
        c = c_ref[...]
        o_ref[...] = _dot(c * _sigmoid(c), w_ref[...]) + b_ref[...]

    return pl.pallas_call(
        body, name="ada_fwd", grid=(ncol // tn,),
        in_specs=[pl.BlockSpec((nb, D_MODEL), lambda j: (0, 0)), pl.BlockSpec((D_MODEL, tn), lambda j: (0, j)),
                  pl.BlockSpec((1, tn), lambda j: (0, j))],
        out_specs=pl.BlockSpec((nb, tn), lambda j: (0, j)), out_shape=_sds((nb, ncol), F32),
        compiler_params=_params("parallel"),
    )(c_all, w_shard, b_shard)


def _adamw_math(g, w, m, v):
    m = ADAM_B1 * m + (1.0 - ADAM_B1) * g
    v = ADAM_B2 * v + (1.0 - ADAM_B2) * (g * g)
    m_hat = m / (1.0 - ADAM_B1 ** ADAM_STEP)
    v_hat = v / (1.0 - ADAM_B2 ** ADAM_STEP)
    delta = -ADAM_LR * (m_hat / (jnp.sqrt(v_hat) + ADAM_EPS) + ADAM_WD * w)
    return delta, m, v


def _ada_bwd_adamw(c_all, dmod_cols, w, m, v):
    nb, ncol = dmod_cols.shape
    tn = 256

    def body(c_ref, d_ref, w_ref, m_ref, v_ref, g_ref, dl_ref, nm_ref, nv_ref):
        c = c_ref[...]
        g = _dot(c * _sigmoid(c), d_ref[...], TN_DIMS)
        g_ref[...] = g
        dl_ref[...], nm_ref[...], nv_ref[...] = _adamw_math(g, w_ref[...], m_ref[...], v_ref[...])

    col = pl.BlockSpec((D_MODEL, tn), lambda j: (0, j))
    shp = _sds((D_MODEL, ncol), F32)
    return pl.pallas_call(
        body, name="ada_bwd_adamw", grid=(ncol // tn,),
        in_specs=[pl.BlockSpec((nb, D_MODEL), lambda j: (0, 0)), pl.BlockSpec((nb, tn), lambda j: (0, j)), col, col, col],
        out_specs=[col, col, col, col], out_shape=[shp, shp, shp, shp],
        compiler_params=_params("parallel"),
    )(c_all, dmod_cols, w, m, v)


def _adamw_halves(own, theirs, core, w, m, v, *, axis, name):
    r2, c2 = own.shape
    tr = _row_tile(r2)
    nt = r2 // tr

    def body(core_ref, own_ref, their_ref, w_ref, m_ref, v_ref, g_ref, dl_ref, nm_ref, nv_ref):
        g = jnp.where(pl.program_id(0) == core_ref[0], own_ref[...], their_ref[...])
        g_ref[...] = g
        dl_ref[...], nm_ref[...], nv_ref[...] = _adamw_math(g, w_ref[...], m_ref[...], v_ref[...])

    if axis == 0:
        full = pl.BlockSpec((tr, c2), lambda h, i, core_ref: (h * nt + i, 0))
    else:
        full = pl.BlockSpec((tr, c2), lambda h, i, core_ref: (i, h))
    half = pl.BlockSpec((tr, c2), lambda h, i, core_ref: (i, 0))
    shp = _sds(w.shape, F32)
    return pl.pallas_call(
        body, name=name,
        grid_spec=pltpu.PrefetchScalarGridSpec(num_scalar_prefetch=1, grid=(2, nt), in_specs=[half, half, full, full, full],
                                               out_specs=[full] * 4),
        out_shape=[shp] * 4, compiler_params=_params("parallel", "parallel"),
    )(core, own, theirs, w, m, v)


def _tok_spec(tm, width=D_MODEL):
    return pl.BlockSpec((None, tm, width), lambda b, i: (b, i, 0))


def _row_spec(width=D_MODEL):
    return pl.BlockSpec((None, 1, width), lambda b, i: (b, 0, 0))


def _vec_spec(width=D_MODEL):
    return pl.BlockSpec((1, width), lambda b, i: (0, 0))


class _RowsOf:
    def __init__(self, ref, first, count):
        self.ref, self.rows = ref, slice(first, first + count)

    def __getitem__(self, idx):
        return self.ref[self.rows, :]

    def __setitem__(self, idx, value):
        self.ref[self.rows, :] = value


def _mm_rows(a, b, *, name, tm, extra, extra_specs, out_specs, out_shape, epi, pro=None, trans_b=False, b_chunks=1, comms=(),
             parts=1, zero_per_seq=(), zero_once=(), vmem_limit=None):
    bsz, seq, k_total = a.shape
    kc = k_total // b_chunks
    dims = NT_DIMS if trans_b else None
    rows = tm // parts

    def body(*refs):
        a_ref, b_ref = refs[0], refs[1]
        ex, outs = refs[2:2 + len(extra)], refs[2 + len(extra):]
        if zero_per_seq:
            @pl.when(pl.program_id(1) == 0)
            def _():
                for k in zero_per_seq:
                    outs[k][...] = jnp.zeros_like(outs[k])
        if zero_once:
            @pl.when(jnp.logical_and(pl.program_id(0) == 0, pl.program_id(1) == 0))
            def _():
                for k in zero_once:
                    outs[k][...] = jnp.zeros_like(outs[k])

        def part_of(ref, p):
            tiled = len(ref.shape) == 2 and ref.shape[0] == tm
            return _RowsOf(ref, p * rows, rows) if tiled and parts > 1 else ref

        accs = []
        for p in range(parts):
            a_p, ex_p, outs_p = part_of(a_ref, p), [part_of(r, p) for r in ex], [part_of(r, p) for r in outs]
            if b_chunks == 1:
                accs.append(_dot(a_p[...] if pro is None else pro(a_p, ex_p, outs_p), b_ref[...], dims))
            else:
                acc = _dot(a_p[...][:, 0:kc], b_ref[0], NT_DIMS)
                for k in range(1, b_chunks):
                    acc = acc + _dot(a_p[...][:, k * kc:(k + 1) * kc], b_ref[k], NT_DIMS)
                accs.append(acc)
        for p in range(parts):
            epi(accs[p], [part_of(r, p) for r in ex], [part_of(r, p) for r in outs])

    b_spec = pl.BlockSpec(b.shape, lambda bb, i: (0,) * b.ndim)
    return _call(
        body, (a, b, *extra), name=name, grid=(bsz, seq // tm), in_specs=[_tok_spec(tm, k_total), b_spec, *extra_specs],
        out_specs=out_specs, out_shape=out_shape, sem=("arbitrary", "arbitrary"), comms=comms, vmem_limit=vmem_limit)


def _in_proj_fused(x, w, sc, sh, w_in_t, tables, comms=()):
    tm = 512
    bsz, seq, _ = x.shape
    half = ROPE_DIM // 2
    heads_per_slab = LANE // ATT_HEAD_DIM

    def pro(x_ref, ex, outs):
        y, _, _ = _rms_fwd(x_ref[...], ex[0][...])
        h = (y * (1.0 + ex[1][...]) + ex[2][...]).astype(BF16)
        outs[0][...] = h
        return h

    def epi(acc, ex, outs):
        c, u, d = ex[3][...], ex[4][...], ex[5][...]
        _, proj_ref, q_ref, k_ref, v_ref = outs
        proj_ref[...] = acc

        def rope(z):
            return (z * c + pltpu.roll(z, half, 1) * u + pltpu.roll(z, LANE - half, 1) * d).astype(BF16)

        for s in range(ATT_WIDTH // LANE):
            slab = rope(acc[:, s * LANE:(s + 1) * LANE])
            for part in range(heads_per_slab):
                g, hh = divmod(s * heads_per_slab + part, ATT_GROUP)
                piece = slab[:, part * ATT_HEAD_DIM:(part + 1) * ATT_HEAD_DIM]
                for blk in range(tm // WINDOW):
                    q_ref[blk, g, hh * WINDOW:(hh + 1) * WINDOW, :] = piece[blk * WINDOW:(blk + 1) * WINDOW]
        rk = rope(acc[:, ATT_WIDTH:ATT_WIDTH + LANE])
        vv = acc[:, ATT_WIDTH + LANE:ATT_COLS].astype(BF16)
        for g in range(ATT_KV_HEADS):
            k_ref[g] = rk[:, g * ATT_HEAD_DIM:(g + 1) * ATT_HEAD_DIM]
            v_ref[g] = vv[:, g * ATT_HEAD_DIM:(g + 1) * ATT_HEAD_DIM]

    cols = w_in_t.shape[0]
    tab = pl.BlockSpec((tm, LANE), lambda b, i: (i, 0))
    kv_spec = pl.BlockSpec((None, ATT_KV_HEADS, tm, ATT_HEAD_DIM), lambda b, i: (b, 0, i, 0))
    kv_shape = _sds((bsz, ATT_KV_HEADS, seq, ATT_HEAD_DIM), BF16)
    q_spec = pl.BlockSpec((None, tm // WINDOW, ATT_KV_HEADS, GROUP_ROWS, ATT_HEAD_DIM), lambda b, i: (b, i, 0, 0, 0))
    return _mm_rows(x, w_in_t, name="in_proj", tm=tm, extra=(w, sc, sh, *tables),
                    extra_specs=[_vec_spec(), _row_spec(), _row_spec(), tab, tab, tab],
                    out_specs=[_tok_spec(tm), _tok_spec(tm, cols), q_spec, kv_spec, kv_spec],
                    out_shape=[_sds(x.shape, BF16), _sds((bsz, seq, cols), F32),
                               _sds((bsz, seq // WINDOW, ATT_KV_HEADS, GROUP_ROWS, ATT_HEAD_DIM), BF16), kv_shape, kv_shape],
                    pro=pro, epi=epi, trans_b=True, comms=comms)


def _rope_tables(seq):
    half = ROPE_DIM // 2
    inv_freq = ROPE_THETA ** (-jnp.arange(0, ROPE_DIM, 2, dtype=F32) / ROPE_DIM)
    ang = jnp.arange(seq, dtype=F32)[:, None] * inv_freq[None, :]
    cos, sin = jnp.cos(ang), jnp.sin(ang)
    rest = ATT_HEAD_DIM - ROPE_DIM
    ones, zeros, zh = jnp.ones((seq, rest), F32), jnp.zeros((seq, rest), F32), jnp.zeros((seq, half), F32)
    reps = LANE // ATT_HEAD_DIM
    t_cos = jnp.tile(jnp.concatenate([cos, cos, ones], axis=1), (1, reps))
    t_up = jnp.tile(jnp.concatenate([zh, sin, zeros], axis=1), (1, reps))
    t_dn = jnp.tile(jnp.concatenate([-sin, zh, zeros], axis=1), (1, reps))
    return t_cos, t_up, t_dn


GROUP_ROWS = ATT_GROUP * WINDOW


ATT_BPS = 2


def _band_mask(has_prev):
    row = lax.broadcasted_iota(jnp.int32, (GROUP_ROWS, 2 * WINDOW), 0) % WINDOW
    col = lax.broadcasted_iota(jnp.int32, (GROUP_ROWS, 2 * WINDOW), 1)
    prev = jnp.logical_and(jnp.logical_and(col < WINDOW, col > row), has_prev)
    return jnp.logical_or(prev, jnp.logical_and(col >= WINDOW, col - WINDOW <= row))


def _sink_column(sink_ref, g):
    head = lax.broadcasted_iota(jnp.int32, (GROUP_ROWS, 1), 0) // WINDOW
    col = jnp.full((GROUP_ROWS, 1), sink_ref[0, g * ATT_GROUP], F32)
    for hh in range(1, ATT_GROUP):
        col = jnp.where(head == hh, sink_ref[0, g * ATT_GROUP + hh], col)
    return col


def _attn_specs():
    q_spec = pl.BlockSpec((None, ATT_BPS, ATT_KV_HEADS, GROUP_ROWS, ATT_HEAD_DIM), lambda b, i: (b, i, 0, 0, 0))
    kv_cur = pl.BlockSpec((None, ATT_KV_HEADS, ATT_BPS * WINDOW, ATT_HEAD_DIM), lambda b, i: (b, 0, i, 0))
    kv_prev = pl.BlockSpec((None, ATT_KV_HEADS, WINDOW, ATT_HEAD_DIM), lambda b, i: (b, 0, jnp.maximum(ATT_BPS * i - 1, 0), 0))
    return q_spec, kv_cur, kv_prev


def _band(prev_ref, cur_ref, g, blk):
    own = cur_ref[g, blk * WINDOW:(blk + 1) * WINDOW]
    before = prev_ref[g] if blk == 0 else cur_ref[g, (blk - 1) * WINDOW:blk * WINDOW]
    return jnp.concatenate([before, own], axis=0)


def _attn_fwd(qh, kh, vh, sinks, w_norm, comms=()):
    bsz, nblk = qh.shape[0], qh.shape[1]
    seq = nblk * WINDOW
    rows = ATT_BPS * WINDOW
    neg = float(jnp.finfo(jnp.float32).min)

    def body(sink_ref, q_ref, kc_ref, kp_ref, vc_ref, vp_ref, w_ref, raw_ref, an_ref, l_ref):
        for blk in range(ATT_BPS):
            mask = _band_mask(True if blk else pl.program_id(1) > 0)
            for g in range(ATT_KV_HEADS):
                keys, vals = _band(kp_ref, kc_ref, g, blk), _band(vp_ref, vc_ref, g, blk)
                sink = _sink_column(sink_ref, g)
                s = jnp.where(mask, _dot(q_ref[blk, g], keys, NT_DIMS) * ATT_SCALE, neg)
                m = jnp.maximum(jnp.max(s, axis=-1, keepdims=True), sink)
                p = jnp.where(mask, jnp.exp(s - m), 0.0)
                den = jnp.sum(p, axis=-1, keepdims=True) + jnp.exp(sink - m)
                o = _dot(p / den, vals)
                lse = m + jnp.log(den)
                tok = slice(blk * WINDOW, (blk + 1) * WINDOW)
                for hh in range(ATT_GROUP):
                    h = g * ATT_GROUP + hh
                    raw_ref[tok, h * ATT_HEAD_DIM:(h + 1) * ATT_HEAD_DIM] = o[hh * WINDOW:(hh + 1) * WINDOW]
                    l_ref[tok, h:h + 1] = lse[hh * WINDOW:(hh + 1) * WINDOW]
        y, _, _ = _rms_fwd(raw_ref[...], w_ref[...])
        an_ref[...] = y.astype(BF16)

    cur = lambda width: pl.BlockSpec((None, rows, width), lambda b, i: (b, i, 0))
    q_spec, kv_cur, kv_prev = _attn_specs()
    return _call(
        body, (sinks, qh, kh, kh, vh, vh, w_norm), name="attn_fwd", grid=(bsz, nblk // ATT_BPS),
        in_specs=[pl.BlockSpec(memory_space=pltpu.SMEM), q_spec, kv_cur, kv_prev, kv_cur, kv_prev, _vec_spec(ATT_WIDTH)],
        out_specs=[cur(ATT_WIDTH), cur(ATT_WIDTH), cur(ATT_Q_HEADS)],
        out_shape=[_sds((bsz, seq, ATT_WIDTH), F32), _sds((bsz, seq, MIX_WIDTH), BF16), _sds((bsz, seq, ATT_Q_HEADS), F32)],
        sem=("parallel", "parallel"), comms=comms)


HG_Q0 = ATT_COLS // LANE
HG_F0 = HG_Q0 + HG_HEADS
HG_I0 = HG_F0 + HG_HEADS
HG_G0 = HG_I0 + HG_HEADS
HG_TOK = 256
HG_NCH = HG_TOK // HG_CHUNK
HG_HPS = 2


def _block_masks():
    row = lax.broadcasted_iota(jnp.int32, (HG_TOK, HG_TOK), 0)
    col = lax.broadcasted_iota(jnp.int32, (HG_TOK, HG_TOK), 1)
    same = (row // HG_CHUNK) == (col // HG_CHUNK)
    return jnp.logical_and(same, col <= row), jnp.logical_and(same, col >= row)


def _row_in_chunk():
    return lax.broadcasted_iota(jnp.int32, (HG_TOK, LANE), 0) % HG_CHUNK


def _chunk_cumsum(x, reverse=False):
    ric = _row_in_chunk()
    shift = 1
    while shift < HG_CHUNK:
        if reverse:
            x = x + jnp.where(ric < HG_CHUNK - shift, pltpu.roll(x, HG_TOK - shift, 0), 0.0)
        else:
            x = x + jnp.where(ric >= shift, pltpu.roll(x, shift, 0), 0.0)
        shift *= 2
    return x


def _chunk_rows(rows):
    stacked = jnp.concatenate([r[None] for r in rows], axis=0)
    return jnp.broadcast_to(stacked, (HG_NCH, HG_CHUNK, LANE)).reshape(HG_TOK, LANE)


def _chunk_slices(x):
    return [x[j * HG_CHUNK:(j + 1) * HG_CHUNK] for j in range(HG_NCH)]


def _hgrn_common(tbl, hf, hq):
    lb = _sigmoid(tbl[1:2] - tbl[0:1])
    sig = _sigmoid(hf)
    f = lb + (1.0 - lb) * sig
    sq = _sigmoid(hq)
    q, k = hq * sq, 1.0 - f
    b = _chunk_cumsum(jnp.log(f))
    last = [b[(j + 1) * HG_CHUNK - 1:(j + 1) * HG_CHUNK] for j in range(HG_NCH)]
    bl = _chunk_rows(last)
    e_b, e_nb, e_rem = jnp.exp(b), jnp.exp(-b), jnp.exp(bl - b)
    e_last = [jnp.exp(r) for r in last]
    return dict(lb=lb, sig=sig, f=f, sq=sq, q=q, k=k, e_b=e_b, e_nb=e_nb, e_rem=e_rem, e_last=e_last,
                qd=q * e_b, kd=k * e_nb, ku=k * e_rem)


def _hgrn_fwd(proj, lb_table, norm_w, mix_in, comms=()):
    bsz, seq, _ = proj.shape
    nstep = seq // HG_TOK

    def body(tbl_ref, nw_ref, q_ref, f_ref, i_ref, g_ref, mix_ref, o_ref, rec_ref, st_ref, s_scr):
        @pl.when(pl.program_id(2) == 0)
        def _():
            s_scr[...] = jnp.zeros_like(s_scr)

        lower, _ = _block_masks()
        for hp in range(HG_HPS):
            ls = slice(hp * LANE, (hp + 1) * LANE)
            v, hg = i_ref[:, ls], g_ref[:, ls]
            t = _hgrn_common(tbl_ref[:, ls], f_ref[:, ls], q_ref[:, ls])
            a = jnp.where(lower, _dot(t["qd"], t["kd"], NT_DIMS), 0.0)
            o_intra = _dot(a, v)
            v_c, ku_c, qd_c = [_chunk_slices(z.astype(BF16)) for z in (v, t["ku"], t["qd"])]
            updates = [_dot(v_c[j], ku_c[j], TN_DIMS) for j in range(HG_NCH)]
            st = s_scr[hp]
            states = []
            for j in range(HG_NCH):
                states.append(st)
                st = st * t["e_last"][j] + updates[j]
            s_scr[hp] = st
            o = o_intra + jnp.concatenate([_dot(qd_c[j], states[j], NT_DIMS) for j in range(HG_NCH)], axis=0)
            for j in range(HG_NCH):
                st_ref[hp, j] = states[j]
            o_ref[:, ls] = o
            y, _, _ = _rms_fwd(o, nw_ref[...])
            rec_ref[:, ls] = (y * (hg * _sigmoid(hg))).astype(BF16)

    width = HG_HPS * LANE
    slab = lambda first: pl.BlockSpec((None, HG_TOK, width), lambda b, h, t: (b, t, first // HG_HPS + h))
    head_out = pl.BlockSpec((None, HG_TOK, width), lambda b, h, t: (b, t, h))
    mix_out = pl.BlockSpec((None, HG_TOK, width), lambda b, h, t: (b, t, ATT_WIDTH // width + h))
    return _call(
        body, (lb_table, norm_w, proj, proj, proj, proj, mix_in), name="hgrn_fwd", grid=(bsz, HG_HEADS // HG_HPS, nstep),
        in_specs=[pl.BlockSpec((2, width), lambda b, h, t: (0, h)), pl.BlockSpec((1, LANE), lambda b, h, t: (0, 0)),
                  slab(HG_Q0), slab(HG_F0), slab(HG_I0), slab(HG_G0), pl.BlockSpec(memory_space=pl.ANY)],
        out_specs=[head_out, mix_out,
                   pl.BlockSpec((None, HG_HPS, HG_NCH, LANE, LANE), lambda b, h, t: (b, h, t, 0, 0))],
        out_shape=[_sds((bsz, seq, HG_WIDTH), F32), _sds(mix_in.shape, BF16),
                   _sds((bsz, HG_HEADS, seq // HG_CHUNK, LANE, LANE), F32)],
        scratch_shapes=[pltpu.VMEM((HG_HPS, LANE, LANE), F32)],
        sem=("parallel", "parallel", "arbitrary"), comms=comms, aliases={6: 1})


def _out_proj_fused(cat, w_out, x, post_w, g1, pre_w, sc2, sh2):
    tm = 512

    def epi(mix, ex, outs):
        x_ref, pw_ref, g1_ref, w2_ref, sc_ref, sh_ref = ex
        outs[0][...] = mix
        n1, _, _ = _rms_fwd(mix, pw_ref[...])
        x1 = x_ref[...] + g1_ref[...] * n1
        outs[1][...] = x1
        y2, _, _ = _rms_fwd(x1, w2_ref[...])
        outs[2][...] = (y2 * (1.0 + sc_ref[...]) + sh_ref[...]).astype(BF16)

    return _mm_rows(cat, w_out, name="out_proj", tm=tm, extra=(x, post_w, g1, pre_w, sc2, sh2),
                    extra_specs=[_tok_spec(tm), _vec_spec(), _row_spec(), _vec_spec(), _row_spec(), _row_spec()],
                    out_specs=[_tok_spec(tm), _tok_spec(tm), _tok_spec(tm)],
                    out_shape=[_sds(x.shape, F32), _sds(x.shape, F32), _sds(x.shape, BF16)], epi=epi)


def _acc_out(ref, first, value):
    @pl.when(first)
    def _():
        ref[...] = value

    @pl.when(jnp.logical_not(first))
    def _():
        ref[...] += value


def _down_proj_fused(r, w_down, x1, post_w, g2, target):
    tm = 512
    bsz = x1.shape[0]

    def pro(r_ref, ex, outs):
        rv = r_ref[...]
        return rv * rv

    def epi(down, ex, outs):
        x1_ref, w_ref, g2_ref, t_ref = ex
        loss_ref, dy_ref, dd_ref, dg2_ref, dw_ref = outs
        w, g2v = w_ref[...], g2_ref[...]
        n2, dh, rstd = _rms_fwd(down, w)
        err = x1_ref[...] + g2v * n2 - t_ref[...]
        part = (0.5 / D_MODEL) * jnp.sum(jnp.sum(err * err, axis=-1, keepdims=True), axis=0, keepdims=True)
        loss_ref[...] += jnp.broadcast_to(part, (1, LANE))
        dy = err * (1.0 / D_MODEL)
        dy_ref[...] = dy
        dg2_ref[...] += _colsum(dy * n2)
        dd, dw_rows = _rms_bwd(dy * g2v, dh, rstd, w)
        dd_ref[...] = dd.astype(BF16)
        dw_ref[...] += _colsum(dw_rows)

    return _mm_rows(r, w_down, name="down_proj", tm=tm, extra=(x1, post_w, g2, target),
                    extra_specs=[_tok_spec(tm), _vec_spec(), _row_spec(), _tok_spec(tm)],
                    out_specs=[_vec_spec(LANE), _tok_spec(tm), _tok_spec(tm), _row_spec(), _vec_spec()],
                    out_shape=[_sds((1, LANE), F32), _sds(x1.shape, F32), _sds(x1.shape, BF16), _sds((bsz, 1, D_MODEL), F32),
                               _sds((1, D_MODEL), F32)], pro=pro, epi=epi, parts=2, zero_per_seq=(3,), zero_once=(0, 4),
                    vmem_limit=VMEM_LIMIT_BIG)


def _up_bwd_fused(dpre, w_up4, dy, x1, mix, pre_w, sc2, post_w, g1, comms=()):
    tm = 512
    bsz = x1.shape[0]

    def epi(dh2v, ex, outs):
        dy_ref, x1_ref, mix_ref, w2_ref, sc_ref, pw_ref, g1_ref = ex
        dx1_ref, dmix_ref, dsc_ref, dsh_ref, dg1_ref, dw2_ref, dpw_ref = outs
        w2, pw = w2_ref[...], pw_ref[...]
        y2, xh2, rstd2 = _rms_fwd(x1_ref[...], w2)
        dsh_ref[...] += _colsum(dh2v)
        dsc_ref[...] += _colsum(dh2v * y2)
        dx1n, dw_rows = _rms_bwd(dh2v * (1.0 + sc_ref[...]), xh2, rstd2, w2)
        dw2_ref[...] += _colsum(dw_rows)
        dx1 = dy_ref[...] + dx1n
        dx1_ref[...] = dx1
        n1, mh, rstd1 = _rms_fwd(mix_ref[...], pw)
        dg1_ref[...] += _colsum(dx1 * n1)
        dmix, dpw_rows = _rms_bwd(dx1 * g1_ref[...], mh, rstd1, pw)
        dmix_ref[...] = dmix.astype(BF16)
        dpw_ref[...] += _colsum(dpw_rows)

    row_shape = _sds((bsz, 1, D_MODEL), F32)
    vec_shape = _sds((1, D_MODEL), F32)
    return _mm_rows(dpre, w_up4, name="up_bwd", tm=tm, extra=(dy, x1, mix, pre_w, sc2, post_w, g1),
                    extra_specs=[_tok_spec(tm), _tok_spec(tm), _tok_spec(tm), _vec_spec(), _row_spec(), _vec_spec(), _row_spec()],
                    out_specs=[_tok_spec(tm), _tok_spec(tm), _row_spec(), _row_spec(), _row_spec(), _vec_spec(), _vec_spec()],
                    out_shape=[_sds(x1.shape, F32), _sds(x1.shape, BF16), row_shape, row_shape, row_shape, vec_shape, vec_shape],
                    epi=epi, b_chunks=w_up4.shape[0], comms=comms, parts=2, zero_per_seq=(2, 3, 4), zero_once=(5, 6),
                    vmem_limit=VMEM_LIMIT_BIG)


def _norm1_bwd(dh1, dx1, x, pre_w, sc1, tm=512, comms=()):
    bsz, seq, _ = x.shape

    def body(dh_ref, dx1_ref, x_ref, w_ref, sc_ref, gx_ref, dsc_ref, dsh_ref, dw_ref):
        b, i = pl.program_id(0), pl.program_id(1)
        w = w_ref[...]
        dh = dh_ref[...]
        y, xh, rstd = _rms_fwd(x_ref[...], w)
        _acc_out(dsh_ref, i == 0, _colsum(dh))
        _acc_out(dsc_ref, i == 0, _colsum(dh * y))
        dx, dw_rows = _rms_bwd(dh * (1.0 + sc_ref[...]), xh, rstd, w)
        _acc_out(dw_ref, jnp.logical_and(b == 0, i == 0), _colsum(dw_rows))
        gx_ref[...] = dx1_ref[...] + dx

    row_shape = _sds((bsz, 1, D_MODEL), F32)
    return _call(
        body, (dh1, dx1, x, pre_w, sc1), name="norm1_bwd", grid=(bsz, seq // tm),
        in_specs=[_tok_spec(tm), _tok_spec(tm), _tok_spec(tm), _vec_spec(), _row_spec()],
        out_specs=[_tok_spec(tm), _row_spec(), _row_spec(), _vec_spec()],
        out_shape=[_sds(x.shape, F32), row_shape, row_shape, _sds((1, D_MODEL), F32)],
        sem=("arbitrary", "arbitrary"), comms=comms)


def _hgrn_bwd(dcat, proj, o_raw, states, lb_table, norm_w, comms=()):
    bsz, seq, _ = proj.shape
    nstep = seq // HG_TOK
    rec0 = ATT_WIDTH // LANE

    def body(tbl_ref, nw_ref, dr_ref, q_ref, f_ref, i_ref, g_ref, o_ref, st_ref,
             dq_ref, df_ref, di_ref, dg_ref, dlb_ref, dnw_ref, ds_scr):
        h, b, t = pl.program_id(0), pl.program_id(1), pl.program_id(2)

        @pl.when(t == 0)
        def _():
            ds_scr[...] = jnp.zeros_like(ds_scr)

        lower, upper = _block_masks()
        dlb_parts = []
        dnw_acc = jnp.zeros((1, LANE), F32)
        for hp in range(HG_HPS):
            ls = slice(hp * LANE, (hp + 1) * LANE)
            hq, v, hg = q_ref[:, ls], i_ref[:, ls], g_ref[:, ls]
            nw = nw_ref[...]
            c = _hgrn_common(tbl_ref[:, ls], f_ref[:, ls], hq)
            qd, kd, ku = c["qd"], c["kd"], c["ku"]
            y, on, rstd = _rms_fwd(o_ref[:, ls], nw)
            sg = _sigmoid(hg)
            dr = dr_ref[:, ls]
            dg_ref[:, ls] = (dr * y * (sg * (1.0 + hg * (1.0 - sg)))).astype(BF16)
            do, dnw_rows = _rms_bwd(dr * (hg * sg), on, rstd, nw)
            at = jnp.where(upper, _dot(kd, qd, NT_DIMS), 0.0)
            da = jnp.where(lower, _dot(do, v, NT_DIMS), 0.0)
            dat = jnp.where(upper, _dot(v, do, NT_DIMS), 0.0)
            dv = _dot(at, do)
            dqd = _dot(da, kd)
            dkd = _dot(dat, qd)
            do_c, qd_c, v_c, ku_c = [_chunk_slices(z.astype(BF16)) for z in (do, qd, v, ku)]
            outer = [_dot(do_c[j], qd_c[j], TN_DIMS) for j in range(HG_NCH)]
            ds = ds_scr[hp]
            ds_after = [None] * HG_NCH
            for j in reversed(range(HG_NCH)):
                ds_after[j] = ds
                ds = outer[j] + ds * c["e_last"][j]
            ds_scr[hp] = ds
            states = [st_ref[hp, j] for j in range(HG_NCH)]
            dv = dv + jnp.concatenate([_dot(ku_c[j], ds_after[j], NT_DIMS) for j in range(HG_NCH)], axis=0)
            dqd = dqd + jnp.concatenate([_dot(do_c[j], states[j]) for j in range(HG_NCH)], axis=0)
            dku = jnp.concatenate([_dot(v_c[j], ds_after[j]) for j in range(HG_NCH)], axis=0)
            dku_ku = dku * ku
            dbl = [_colsum(states[j] * ds_after[j]) * c["e_last"][j] + _colsum(dku_ku[j * HG_CHUNK:(j + 1) * HG_CHUNK])
                   for j in range(HG_NCH)]
            dk = dkd * c["e_nb"] + dku * c["e_rem"]
            db = dqd * qd - dkd * kd - dku_ku + jnp.where(_row_in_chunk() == HG_CHUNK - 1, _chunk_rows(dbl), 0.0)
            dfv = _chunk_cumsum(db, reverse=True) / c["f"] - dk
            sig, sq = c["sig"], c["sq"]
            df_ref[:, ls] = (dfv * (1.0 - c["lb"]) * sig * (1.0 - sig)).astype(BF16)
            dq_ref[:, ls] = (dqd * c["e_b"] * (sq * (1.0 + hq * (1.0 - sq)))).astype(BF16)
            di_ref[:, ls] = dv.astype(BF16)
            dlb_parts.append(_colsum(dfv * (1.0 - sig)))
            dnw_acc = dnw_acc + _colsum(dnw_rows)
        _acc_out(dlb_ref, jnp.logical_and(b == 0, t == 0), jnp.concatenate(dlb_parts, axis=1))
        _acc_out(dnw_ref, jnp.logical_and(h == 0, jnp.logical_and(b == 0, t == 0)), dnw_acc)

    rev = lambda t: nstep - 1 - t
    width = HG_HPS * LANE
    slab = lambda first: pl.BlockSpec((None, HG_TOK, width), lambda h, b, t: (b, rev(t), first // HG_HPS + h))
    head = pl.BlockSpec((None, HG_TOK, width), lambda h, b, t: (b, rev(t), h))
    grad_shape = _sds((bsz, seq, HG_WIDTH), BF16)
    return _call(
        body, (lb_table, norm_w, dcat, proj, proj, proj, proj, o_raw, states), name="hgrn_bwd",
        grid=(HG_HEADS // HG_HPS, bsz, nstep),
        in_specs=[pl.BlockSpec((2, width), lambda h, b, t: (0, h)), pl.BlockSpec((1, LANE), lambda h, b, t: (0, 0)),
                  slab(rec0), slab(HG_Q0), slab(HG_F0), slab(HG_I0), slab(HG_G0), head,
                  pl.BlockSpec((None, HG_HPS, HG_NCH, LANE, LANE), lambda h, b, t: (b, h, rev(t), 0, 0))],
        out_specs=[head, head, head, head, pl.BlockSpec((1, width), lambda h, b, t: (0, h)),
                   pl.BlockSpec((1, LANE), lambda h, b, t: (0, 0))],
        out_shape=[grad_shape, grad_shape, grad_shape, grad_shape, _sds((1, HG_WIDTH), F32), _sds((1, LANE), F32)],
        scratch_shapes=[pltpu.VMEM((HG_HPS, LANE, LANE), F32)],
        sem=("arbitrary", "arbitrary", "arbitrary"), comms=comms)


def _attn_bwd(dcat, raw, w_norm, qh, kh, vh, lse, sinks, tables, comms=()):
    bsz, nblk = qh.shape[0], qh.shape[1]
    seq = nblk * WINDOW
    nstep = nblk // ATT_BPS
    half = ROPE_DIM // 2

    def body(sink_ref, da_ref, raw_ref, w_ref, q_ref, kc_ref, kp_ref, vc_ref, vp_ref, l_ref, c_ref, u_ref, d_ref,
             o_ref, dw_ref, dsink_ref, carry_k, carry_v):
        b, i = pl.program_id(0), pl.program_id(1)
        first = jnp.logical_and(b == 0, i == 0)

        @pl.when(i == 0)
        def _():
            carry_k[...] = jnp.zeros_like(carry_k)
            carry_v[...] = jnp.zeros_like(carry_v)

        w = w_ref[...]
        _, on, rstd = _rms_fwd(raw_ref[...], w)
        do_step, dw_rows = _rms_bwd(da_ref[...], on, rstd, w)
        _acc_out(dw_ref, first, _colsum(dw_rows))
        lane8 = lax.broadcasted_iota(jnp.int32, (1, ATT_Q_HEADS), 1)
        dsink = jnp.zeros((1, ATT_Q_HEADS), F32)
        from_next_k, from_next_v = carry_k[...], carry_v[...]
        for blk in reversed(range(ATT_BPS)):
            tok = slice(blk * WINDOW, (blk + 1) * WINDOW)
            mask = _band_mask(True if blk else i < nstep - 1)
            raw_v, do_all = raw_ref[tok, :], do_step[tok]
            c, u, d = c_ref[tok, :], u_ref[tok, :], d_ref[tok, :]

            def unrope(g):
                return (g * c + pltpu.roll(g * u, LANE - half, 1) + pltpu.roll(g * d, half, 1)).astype(BF16)

            dq_parts, dk_own, dk_before, dv_own, dv_before = [], [], [], [], []
            for g in range(ATT_KV_HEADS):
                heads = [slice((g * ATT_GROUP + hh) * ATT_HEAD_DIM, (g * ATT_GROUP + hh + 1) * ATT_HEAD_DIM)
                         for hh in range(ATT_GROUP)]
                q = q_ref[blk, g]
                keys, vals = _band(kp_ref, kc_ref, g, blk), _band(vp_ref, vc_ref, g, blk)
                do_g = jnp.concatenate([do_all[:, hs] for hs in heads], axis=0)
                dsum = jnp.concatenate([jnp.sum(do_all[:, hs] * raw_v[:, hs], axis=-1, keepdims=True) for hs in heads], axis=0)
                lse_g = jnp.concatenate([l_ref[tok, g * ATT_GROUP + hh:g * ATT_GROUP + hh + 1] for hh in range(ATT_GROUP)], axis=0)
                p = jnp.where(mask, jnp.exp(_dot(q, keys, NT_DIMS) * ATT_SCALE - lse_g), 0.0)
                sink_part = jnp.exp(_sink_column(sink_ref, g) - lse_g) * dsum
                for hh in range(ATT_GROUP):
                    head_sum = jnp.sum(sink_part[hh * WINDOW:(hh + 1) * WINDOW], axis=0, keepdims=True)
                    dsink = dsink - jnp.where(lane8 == g * ATT_GROUP + hh, head_sum, 0.0)
                ds = p * (_dot(do_g, vals, NT_DIMS) - dsum) * ATT_SCALE
                dq_g = _dot(ds, keys)
                dq_parts += [dq_g[hh * WINDOW:(hh + 1) * WINDOW] for hh in range(ATT_GROUP)]
                dk_g = _dot(ds, q, TN_DIMS)
                dv_g = _dot(p, do_g, TN_DIMS)
                dk_before.append(dk_g[:WINDOW])
                dk_own.append(dk_g[WINDOW:])
                dv_before.append(dv_g[:WINDOW])
                dv_own.append(dv_g[WINDOW:])
            per_slab = LANE // ATT_HEAD_DIM
            for s in range(ATT_WIDTH // LANE):
                slab = jnp.concatenate(dq_parts[s * per_slab:(s + 1) * per_slab], axis=1)
                o_ref[tok, s * LANE:(s + 1) * LANE] = unrope(slab)
            o_ref[tok, ATT_WIDTH:ATT_WIDTH + LANE] = unrope(jnp.concatenate(dk_own, axis=1) + from_next_k)
            o_ref[tok, ATT_WIDTH + LANE:ATT_COLS] = (jnp.concatenate(dv_own, axis=1) + from_next_v).astype(BF16)
            from_next_k, from_next_v = jnp.concatenate(dk_before, axis=1), jnp.concatenate(dv_before, axis=1)
        carry_k[...] = from_next_k
        carry_v[...] = from_next_v
        _acc_out(dsink_ref, first, dsink)

    rows = ATT_BPS * WINDOW
    rev = lambda i: nstep - 1 - i
    cur = lambda width: pl.BlockSpec((None, rows, width), lambda b, i: (b, rev(i), 0))
    q_spec = pl.BlockSpec((None, ATT_BPS, ATT_KV_HEADS, GROUP_ROWS, ATT_HEAD_DIM), lambda b, i: (b, rev(i), 0, 0, 0))
    kv_cur = pl.BlockSpec((None, ATT_KV_HEADS, rows, ATT_HEAD_DIM), lambda b, i: (b, 0, rev(i), 0))
    kv_prev = pl.BlockSpec((None, ATT_KV_HEADS, WINDOW, ATT_HEAD_DIM), lambda b, i: (b, 0, jnp.maximum(ATT_BPS * rev(i) - 1, 0), 0))
    tab = pl.BlockSpec((rows, LANE), lambda b, i: (rev(i), 0))
    return _call(
        body, (sinks, dcat, raw, w_norm, qh, kh, kh, vh, vh, lse, *tables), name="attn_bwd", grid=(bsz, nstep),
        in_specs=[pl.BlockSpec(memory_space=pltpu.SMEM), cur(ATT_WIDTH), cur(ATT_WIDTH), _vec_spec(ATT_WIDTH), q_spec,
                  kv_cur, kv_prev, kv_cur, kv_prev, cur(ATT_Q_HEADS), tab, tab, tab],
        out_specs=[cur(ATT_COLS), _vec_spec(ATT_WIDTH), _vec_spec(ATT_Q_HEADS)],
        out_shape=[_sds((bsz, seq, ATT_COLS), BF16), _sds((1, ATT_WIDTH), F32), _sds((1, ATT_Q_HEADS), F32)],
        scratch_shapes=[pltpu.VMEM((WINDOW, LANE), F32), pltpu.VMEM((WINDOW, LANE), F32)],
        sem=("arbitrary", "arbitrary"), comms=comms)


def _other_chips(x, y):
    return [(1 - x, y), (x, 1 - y), (1 - x, 1 - y)]


def _sem_pair(n):
    return [pltpu.SemaphoreType.DMA((n,)), pltpu.SemaphoreType.DMA((n,))]


def _rows_of(ref, rows):
    return ref if rows is None else ref.at[pl.ds(rows[0], rows[1])]


def _plan_chip_gather(blocks, bufs, rows=None, forward_rows=None):
    n = len(blocks)

    def copies(ins, outs, sems):
        x, y, c = _mesh_pos()
        sends, lands = [], []
        for a in range(n):
            for j, chip in enumerate(_other_chips(x, y)):
                k = 3 * a + j
                sends.append(pltpu.make_async_remote_copy(
                    src_ref=_rows_of(ins[a], rows), dst_ref=_rows_of(outs[a].at[4 * x + 2 * y + c], rows), send_sem=sems[0].at[k],
                    recv_sem=sems[1].at[k], device_id=(*chip, c), device_id_type=MESH))
                slot = _rows_of(outs[a].at[4 * chip[0] + 2 * chip[1] + c], rows)
                lands.append(pltpu.make_async_remote_copy(
                    src_ref=slot, dst_ref=slot, send_sem=sems[0].at[k], recv_sem=sems[1].at[k],
                    device_id=(*chip, c), device_id_type=MESH))
                if forward_rows is not None:
                    k = 3 * (n + a) + j
                    mine = _rows_of(outs[a].at[4 * chip[0] + 2 * chip[1] + c], forward_rows)
                    sends.append(pltpu.make_async_remote_copy(
                        src_ref=mine, dst_ref=mine, send_sem=sems[0].at[k], recv_sem=sems[1].at[k],
                        device_id=(x, y, 1 - c), device_id_type=MESH))
                    theirs = _rows_of(outs[a].at[4 * chip[0] + 2 * chip[1] + 1 - c], forward_rows)
                    lands.append(pltpu.make_async_remote_copy(
                        src_ref=theirs, dst_ref=theirs, send_sem=sems[0].at[k], recv_sem=sems[1].at[k],
                        device_id=(x, y, 1 - c), device_id_type=MESH))
        return sends, lands

    def start(ins, outs, sems):
        for cp in copies(ins, outs, sems)[0]:
            cp.start()

    def finish(ins, outs, sems):
        sends, lands = copies(ins, outs, sems)
        for cp in lands:
            cp.wait_recv()
        for cp in sends:
            cp.wait_send()

    n_sems = 3 * n * (2 if forward_rows is not None else 1)
    return _Comm(list(blocks) + list(bufs), [_sds(b.shape, b.dtype) for b in bufs], _sem_pair(n_sems), start, finish,
                 aliases=[(n + a, a) for a in range(n)])


def _plan_pair_forward(bufs, rows=None):
    n = len(bufs)

    def copies(outs, sems):
        x, y, c = _mesh_pos()
        sends, lands = [], []
        for a in range(n):
            for j, chip in enumerate(_other_chips(x, y)):
                k = 3 * a + j
                slot = _rows_of(outs[a].at[4 * chip[0] + 2 * chip[1] + c], rows)
                sends.append(pltpu.make_async_remote_copy(
                    src_ref=slot, dst_ref=slot, send_sem=sems[0].at[k], recv_sem=sems[1].at[k],
                    device_id=(x, y, 1 - c), device_id_type=MESH))
                theirs = _rows_of(outs[a].at[4 * chip[0] + 2 * chip[1] + 1 - c], rows)
                lands.append(pltpu.make_async_remote_copy(
                    src_ref=theirs, dst_ref=theirs, send_sem=sems[0].at[k], recv_sem=sems[1].at[k],
                    device_id=(x, y, 1 - c), device_id_type=MESH))
        return sends, lands

    def start(ins, outs, sems):
        for cp in copies(outs, sems)[0]:
            cp.start()

    def finish(ins, outs, sems):
        sends, lands = copies(outs, sems)
        for cp in lands:
            cp.wait_recv()
        for cp in sends:
            cp.wait_send()

    return _Comm(list(bufs), [_sds(b.shape, b.dtype) for b in bufs], _sem_pair(3 * n), start, finish,
                 aliases=[(a, a) for a in range(n)])


def _plan_pair(arrays, other_half):
    n = len(arrays)
    per = N_CHIPS if other_half == "chip_major" else 1

    def copies(ins, outs, sems):
        x, y, c = _mesh_pos()
        out = []
        for a in range(n):
            for k in range(per):
                if other_half == "chip_major":
                    src, dst = ins[a].at[k, 1 - c], outs[a].at[k]
                else:
                    src, dst = (ins[a].at[1 - c] if other_half else ins[a]), outs[a]
                out.append(pltpu.make_async_remote_copy(
                    src_ref=src, dst_ref=dst, send_sem=sems[0].at[per * a + k], recv_sem=sems[1].at[per * a + k],
                    device_id=(x, y, 1 - c), device_id_type=MESH))
        return out

    def start(ins, outs, sems):
        for cp in copies(ins, outs, sems):
            cp.start()

    def finish(ins, outs, sems):
        for cp in copies(ins, outs, sems):
            cp.wait()

    if other_half == "chip_major":
        shapes = [_sds((a.shape[0],) + a.shape[2:], a.dtype) for a in arrays]
    else:
        shapes = [_sds(a.shape[1:] if other_half else a.shape, a.dtype) for a in arrays]
    return _Comm(list(arrays), shapes, _sem_pair(per * n), start, finish)


def _plan_chip_exchange(arrays):
    n = len(arrays)

    def copies(ins, outs, sems):
        x, y, c = _mesh_pos()
        sends, lands = [], []
        for a in range(n):
            for j, chip in enumerate(_other_chips(x, y)):
                k = 3 * a + j
                sends.append(pltpu.make_async_remote_copy(
                    src_ref=ins[a].at[2 * chip[0] + chip[1]], dst_ref=outs[a].at[2 * x + y], send_sem=sems[0].at[k],
                    recv_sem=sems[1].at[k], device_id=(*chip, c), device_id_type=MESH))
                slot = outs[a].at[2 * chip[0] + chip[1]]
                lands.append(pltpu.make_async_remote_copy(
                    src_ref=slot, dst_ref=slot, send_sem=sems[0].at[k], recv_sem=sems[1].at[k],
                    device_id=(*chip, c), device_id_type=MESH))
        return sends, lands

    def start(ins, outs, sems):
        for cp in copies(ins, outs, sems)[0]:
            cp.start()

    def finish(ins, outs, sems):
        sends, lands = copies(ins, outs, sems)
        for cp in lands:
            cp.wait_recv()
        for cp in sends:
            cp.wait_send()

    return _Comm(list(arrays), [_sds(a.shape, a.dtype) for a in arrays], _sem_pair(3 * n), start, finish)


def _comm_only(comms, name):
    return _call(lambda: None, (), name=name, grid=(), in_specs=[], out_specs=[], out_shape=[], sem=(), comms=comms)[1]


def _allgather8(arrays, name):
    return _comm_only([_plan_allgather8(arrays)], name)[0]


def _plan_allgather8(arrays):
    n = len(arrays)

    def parts(ins, outs, sems):
        send_sems, recv_sems, local_sems = sems
        x, y, c = _mesh_pos()
        me, sibling = (x, y, c), (x, y, 1 - c)
        chips = _other_chips(x, y)

        def copy(a, k, block, to, src=None):
            dst = outs[a].at[4 * block[0] + 2 * block[1] + block[2]]
            return pltpu.make_async_remote_copy(
                src_ref=dst if src is None else src, dst_ref=dst, send_sem=send_sems.at[7 * a + k],
                recv_sem=recv_sems.at[7 * a + k], device_id=to, device_id_type=MESH)

        mine = [pltpu.make_async_copy(ins[a], outs[a].at[4 * x + 2 * y + c], local_sems.at[a]) for a in range(n)]
        first = []
        for a in range(n):
            first.append(copy(a, 0, me, sibling, src=ins[a]))
            first += [copy(a, 1 + j, me, (*chip, c), src=ins[a]) for j, chip in enumerate(chips)]
        return copy, mine, first, me, sibling, chips, c

    def start(ins, outs, sems):
        _, mine, first, *_ = parts(ins, outs, sems)
        for cp in mine + first:
            cp.start()

    def finish(ins, outs, sems):
        copy, mine, first, me, sibling, chips, c = parts(ins, outs, sems)
        passed = []
        for j, chip in enumerate(chips):
            for a in range(n):
                copy(a, 1 + j, (*chip, c), me).wait_recv()
                fwd = copy(a, 4 + j, (*chip, c), sibling)
                fwd.start()
                passed.append(fwd)
        for a in range(n):
            copy(a, 0, sibling, me).wait_recv()
            for j, chip in enumerate(chips):
                copy(a, 4 + j, (*chip, 1 - c), me).wait_recv()
        for cp in first + passed:
            cp.wait_send()
        for cp in mine:
            cp.wait()

    sems = [pltpu.SemaphoreType.DMA((7 * n,)), pltpu.SemaphoreType.DMA((7 * n,)), pltpu.SemaphoreType.DMA((n,))]
    return _Comm(list(arrays), [_sds((N_DEV,) + a.shape, a.dtype) for a in arrays], sems, start, finish)


def _pair_sum(g, q, core, name, chip_major=False):
    rows, cols = g.shape[2:]
    tr = _row_tile(rows)

    def body(core_ref, g_ref, q_ref, o_ref):
        o_ref[...] = (g_ref[...] + q_ref[...]).astype(BF16)

    blk = pl.BlockSpec((None, tr, cols), lambda k, i, core_ref: (k, i, 0))
    if chip_major:
        own = pl.BlockSpec((None, None, tr, cols), lambda k, i, core_ref: (k, core_ref[0], i, 0))
    else:
        own = pl.BlockSpec((None, None, tr, cols), lambda k, i, core_ref: (core_ref[0], k, i, 0))
    return pl.pallas_call(
        body, name=name,
        grid_spec=pltpu.PrefetchScalarGridSpec(num_scalar_prefetch=1, grid=(N_CHIPS, rows // tr), in_specs=[own, blk], out_specs=blk),
        out_shape=_sds((N_CHIPS, rows, cols), BF16), compiler_params=_params("parallel", "parallel"),
    )(core, g, q)


def _sum_chips(own, landed, chip, name):
    _, rows, cols = own.shape
    tr = _row_tile(rows)

    def body(chip_ref, own_ref, a_ref, b_ref, c_ref, o_ref):
        acc = own_ref[...].astype(F32) + a_ref[...].astype(F32)
        o_ref[...] = (acc + b_ref[...].astype(F32)) + c_ref[...].astype(F32)

    blk = lambda flip: pl.BlockSpec((None, tr, cols), lambda i, chip_ref: (jnp.bitwise_xor(chip_ref[0], flip), i, 0))
    return pl.pallas_call(
        body, name=name,
        grid_spec=pltpu.PrefetchScalarGridSpec(num_scalar_prefetch=1, grid=(rows // tr,), in_specs=[blk(0), blk(1), blk(2), blk(3)],
                                               out_specs=pl.BlockSpec((tr, cols), lambda i, chip_ref: (i, 0))),
        out_shape=_sds((rows, cols), F32), compiler_params=_params("parallel"),
    )(chip, own, landed, landed, landed)


SUBLANES = 8


def _tile_rows(n_elems):
    return -(-n_elems // (SUBLANES * LANE)) * SUBLANES


SMALL_ITEMS = (("b_ada", N_MOD * D_MODEL), ("pre_w_mix", D_MODEL), ("post_w_mix", D_MODEL), ("pre_w_mlp", D_MODEL),
               ("post_w_mlp", D_MODEL), ("attn_out_w", ATT_WIDTH), ("hg_norm_w", HG_HEAD_DIM), ("attn_sinks", ATT_Q_HEADS),
               ("lb_0", HG_WIDTH), ("lb_1", HG_WIDTH))
SMALL_AT = {}
for _name, _size in SMALL_ITEMS:
    SMALL_AT[_name] = (sum(r for _, r in SMALL_AT.values()), _tile_rows(_size))
SMALL_ROWS = sum(r for _, r in SMALL_AT.values())
MOD_ROWS = SMALL_AT["b_ada"][1]
PLAIN_ROWS = SMALL_AT["lb_0"][0] - MOD_ROWS
LB_ROWS = SMALL_AT["lb_0"][1]


def _rows(a, nrows=None):
    flat = a.reshape(-1)
    nrows = _tile_rows(flat.shape[0]) if nrows is None else nrows
    return jnp.pad(flat, (0, nrows * LANE - flat.shape[0])).reshape(nrows, LANE)


def _pack_small(vals):
    vals = dict(vals, lb_0=vals["lb_table"][0], lb_1=vals["lb_table"][1])
    return jnp.concatenate([_rows(vals[name], SMALL_AT[name][1]) for name, _ in SMALL_ITEMS], axis=0)


def _unpack_small(p):
    def item(name, shape):
        first = SMALL_AT[name][0]
        size = shape[0] * shape[1]
        return p[first:first + SMALL_AT[name][1]].reshape(-1)[:size].reshape(shape)

    out = {name: item(name, (1, size)) for name, size in SMALL_ITEMS if not name.startswith("lb_")}
    out["lb_table"] = jnp.concatenate([item("lb_0", (1, HG_WIDTH)), item("lb_1", (1, HG_WIDTH))], axis=0)
    return out


def _pack_partials(dmod, plain, d_lb, loss_row):
    return jnp.concatenate([_rows(dmod, dmod.shape[0] * MOD_ROWS)] + [_rows(g) for g in plain] + [_rows(d_lb), _rows(loss_row)], axis=0)


def _small_update(packs, w, m, v, n_seq):
    mod_end = n_seq * MOD_ROWS
    lb_at = mod_end + PLAIN_ROWS
    t0, t1 = SMALL_AT["lb_0"][0], SMALL_AT["lb_1"][0]

    def body(p_ref, w_ref, m_ref, v_ref, g_ref, dl_ref, nm_ref, nv_ref, loss_ref):
        tot = p_ref[0]
        for d in range(1, N_DEV):
            tot = tot + p_ref[d]
        wv = w_ref[...]
        p1 = _sigmoid(wv[t1:t1 + LB_ROWS] - wv[t0:t0 + LB_ROWS])
        s = tot[lb_at:lb_at + LB_ROWS] * p1 * (1.0 - p1)
        g_bias = tot[0:MOD_ROWS]
        for q in range(1, n_seq):
            g_bias = g_bias + tot[q * MOD_ROWS:(q + 1) * MOD_ROWS]
        g = jnp.concatenate([g_bias, tot[mod_end:lb_at], -s, s], axis=0)
        g_ref[...] = g
        dl_ref[...], nm_ref[...], nv_ref[...] = _adamw_math(g, wv, m_ref[...], v_ref[...])
        loss_ref[...] = tot[lb_at + LB_ROWS:lb_at + LB_ROWS + SUBLANES]

    shp = _sds((SMALL_ROWS, LANE), F32)
    return pl.pallas_call(body, name="small_update", out_shape=[shp] * 4 + [_sds((SUBLANES, LANE), F32)],
                          compiler_params=_params())(packs, w, m, v)


def kernel(x, c, w_ada, b_ada, pre_w_mix, w_in, attn_sinks, attn_out_w, lb_table, hg_norm_w, w_out, post_w_mix, pre_w_mlp, w_up, w_down, post_w_mlp, loss_target, m_w_ada, m_b_ada, m_pre_w_mix, m_w_in, m_attn_sinks, m_attn_out_w, m_lb_table, m_hg_norm_w, m_w_out, m_post_w_mix, m_pre_w_mlp, m_w_up, m_w_down, m_post_w_mlp, v_w_ada, v_b_ada, v_pre_w_mix, v_w_in, v_attn_sinks, v_attn_out_w, v_lb_table, v_hg_norm_w, v_w_out, v_post_w_mix, v_pre_w_mlp, v_w_up, v_w_down, v_post_w_mlp):
    xi, yi, ci = _mesh_pos()
    chip = 2 * xi + yi
    dev = 2 * chip + ci
    bsz, seq, _ = x.shape
    ntok = bsz * seq
    ada_cols = w_ada.shape[2]
    core = jnp.reshape(ci, (1,)).astype(jnp.int32)
    chip_idx = jnp.reshape(chip, (1,)).astype(jnp.int32)
    flat = lambda a: a.reshape(ntok, a.shape[-1])
    unflat = lambda a: a.reshape(bsz, seq, a.shape[-1])
    tables = _rope_tables(seq)

    def row_half(w):
        rows = w.shape[1] // 2
        return lax.dynamic_slice_in_dim(w[0], ci * rows, rows, axis=0).astype(BF16)

    def gather_buffer(w):
        rows, cols = w.shape[1] // 2, w.shape[2]
        own = w[0].astype(BF16).reshape(2, rows, cols)
        return lax.dynamic_update_slice(lax.empty((N_DEV, rows, cols), BF16), own, (2 * chip, 0, 0))

    w_in_t, m_in_t, v_in_t = [jnp.transpose(a[0])[None] for a in (w_in, m_w_in, v_w_in)]
    c_g, in_g = _allgather8([c, row_half(w_in_t)], "gather_first")
    c_all = c_g.reshape(N_DEV * bsz, D_MODEL)
    w_in_full = in_g.reshape(IN_COLS, D_MODEL)

    b_cols = lax.dynamic_slice_in_dim(b_ada, chip * ada_cols, ada_cols, axis=1)
    mod_part = _ada_fwd(c_all, w_ada[0], b_cols)
    half_rows = mod_part.shape[0] // 2
    (mod_g,) = _allgather8([lax.dynamic_slice_in_dim(mod_part, ci * half_rows, half_rows, axis=0)], "gather_mod")
    mod_all = mod_g.reshape(N_CHIPS, 2, half_rows, ada_cols).transpose(1, 2, 0, 3).reshape(N_DEV * bsz, N_MOD * D_MODEL)
    mod = lax.dynamic_slice_in_dim(mod_all, dev * bsz, bsz, axis=0)
    sh1, sc1, g1, sh2, sc2, g2 = [mod[:, i * D_MODEL:(i + 1) * D_MODEL].reshape(bsz, 1, D_MODEL) for i in range(N_MOD)]

    up_rows = w_up.shape[1] // 4
    first_half, second_half = (0, up_rows), (up_rows, up_rows)
    (h1, proj, qh, kh, vh), ((out_g,), (up_g,)) = _in_proj_fused(
        x, pre_w_mix, sc1, sh1, w_in_full, tables,
        comms=[_plan_chip_gather([row_half(w_out)], [gather_buffer(w_out)]),
               _plan_chip_gather([row_half(w_up)], [gather_buffer(w_up)], rows=first_half)])
    (attn_raw, cat, lse), ((up_g,), (out_g,)) = _attn_fwd(
        qh, kh, vh, attn_sinks, attn_out_w,
        comms=[_plan_chip_gather([row_half(w_up)], [up_g], rows=second_half, forward_rows=first_half), _plan_pair_forward([out_g])])
    (o_raw, cat, states), ((down_g,), (up_g,)) = _hgrn_fwd(
        proj, lb_table, hg_norm_w, cat,
        comms=[_plan_chip_gather([row_half(w_down)], [gather_buffer(w_down)]), _plan_pair_forward([up_g], rows=second_half)])
    w_out_full = out_g.reshape(D_MODEL, D_MODEL)
    w_up4 = up_g.reshape(N_CHIPS, D_MODEL, D_MODEL)
    mix, x1, h2 = _out_proj_fused(cat, w_out_full, x, post_w_mix, g1, pre_w_mlp, sc2, sh2)
    big_tm = min(ntok, 2048)
    up_spec = pl.BlockSpec((None, D_MODEL, D_MODEL), lambda i, j: (j, 0, 0))
    r, ((down_g,),) = _mm(flat(h2), w_up4, name="up_proj", out_dtype=BF16, tm=big_tm, tn=D_MODEL, n_out=D_FF, b_spec=up_spec,
                          epi=lambda acc: jnp.maximum(acc, 0.0), comms=[_plan_pair_forward([down_g])])
    w_down_full = down_g.reshape(D_FF, D_MODEL)
    square = lambda t: t * t
    loss_row, dy, dd, dg2, d_post_mlp = _down_proj_fused(unflat(r), w_down_full, x1, post_w_mlp, g2, loss_target)

    dpre = _mm(flat(dd), w_down_full, name="down_bwd", out_dtype=BF16, trans_b=True, tm=big_tm, tn=D_MODEL, extra=(r,),
               epi=lambda acc, rt: acc * (2.0 * rt.astype(F32)))
    half_rows = D_MODEL // 2
    g_down = _mm_tn(r, flat(dd), name="down_wgrad", tk=half_rows, tn=D_MODEL, a_fn=square,
                    out_shape=_sds((2, N_CHIPS, half_rows, D_MODEL), F32),
                    out_spec=pl.BlockSpec((None, None, half_rows, D_MODEL), lambda i, j: (i % 2, i // 2, 0, 0)))
    (dx1, dmix, dsc2, dsh2, dg1, d_pre_mlp, d_post_mix), ((q_down,),) = _up_bwd_fused(
        unflat(dpre), w_up4, dy, x1, mix, pre_w_mlp, sc2, post_w_mix, g1, comms=[_plan_pair([g_down], True)])
    g_up = _mm_tn(flat(h2), dpre, name="up_wgrad", tk=D_MODEL, tn=half_rows,
                  out_shape=_sds((2, N_CHIPS, half_rows, D_MODEL), F32),
                  out_spec=pl.BlockSpec((2, None, half_rows, half_rows), lambda i, j: (0, j // 2, 0, j % 2)))
    s_down = _pair_sum(g_down, q_down, core, "pair_sum_down")

    dcat, ((q_up,),) = _mm(flat(dmix), w_out_full, name="out_bwd", out_dtype=F32, trans_b=True, comms=[_plan_pair([g_up], True)])
    dcat = unflat(dcat)
    s_up = _pair_sum(g_up, q_up, core, "pair_sum_up")
    out_rows = D_MODEL // N_CHIPS
    g_out = _mm_tn(flat(cat), flat(dmix), name="out_wgrad", tk=2 * out_rows, tn=half_rows,
                   out_shape=_sds((2, N_CHIPS, out_rows, half_rows), F32),
                   out_spec=pl.BlockSpec((None, 2, out_rows, half_rows), lambda i, j: (j, i, 0, 0)))
    (dhq, dhf, dhi, dhg, d_lb, d_hg_norm), ((x_down,), (q_out,)) = _hgrn_bwd(
        dcat, proj, o_raw, states, lb_table, hg_norm_w, comms=[_plan_chip_exchange([s_down]), _plan_pair([g_out], True)])
    half_down = _sum_chips(s_down, x_down, chip_idx, "sum_chips_down")
    s_out = _pair_sum(g_out, q_out, core, "pair_sum_out")
    (dproj_a, d_attn_out, d_sinks), ((their_down,), (x_up, x_out)) = _attn_bwd(
        dcat, attn_raw, attn_out_w, qh, kh, vh, lse, attn_sinks, tables,
        comms=[_plan_pair([half_down], False), _plan_chip_exchange([s_up, s_out])])
    half_up = _sum_chips(s_up, x_up, chip_idx, "sum_chips_up")
    half_out = _sum_chips(s_out, x_out, chip_idx, "sum_chips_out")
    dproj = flat(jnp.concatenate([dproj_a, dhq, dhf, dhi, dhg], axis=-1))
    in_rows = IN_COLS // N_CHIPS // 2
    g_in = _mm_tn(dproj, flat(h1), name="in_wgrad", tk=2 * LANE, tn=D_MODEL).reshape(N_CHIPS, 2, in_rows, D_MODEL)
    dh1, ((q_in,), (their_up, their_out)) = _mm(
        dproj, w_in_full, name="in_bwd", out_dtype=F32,
        comms=[_plan_pair([g_in], "chip_major"), _plan_pair([half_up, half_out], False)])
    s_in = _pair_sum(g_in, q_in, core, "pair_sum_in", chip_major=True)
    (grad_x, dsc1, dsh1, d_pre_mix), ((x_in,),) = _norm1_bwd(unflat(dh1), dx1, x, pre_w_mix, sc1,
                                                             comms=[_plan_chip_exchange([s_in])])

    dmod = jnp.concatenate([dsh1, dsc1, dg1, dsh2, dsc2, dg2], axis=-1).reshape(bsz, N_MOD * D_MODEL)
    pack = _pack_partials(dmod, [d_pre_mix, d_post_mix, d_pre_mlp, d_post_mlp, d_attn_out, d_hg_norm, d_sinks], d_lb, loss_row)
    ((packs,),) = _comm_only([_plan_allgather8([pack])], "gather_small")
    half_in = _sum_chips(s_in, x_in, chip_idx, "sum_chips_in")
    ((their_in,),) = _comm_only([_plan_pair([half_in], False)], "pair_swap_in")
    w_small = dict(b_ada=b_ada, pre_w_mix=pre_w_mix, post_w_mix=post_w_mix, pre_w_mlp=pre_w_mlp, post_w_mlp=post_w_mlp,
                   attn_out_w=attn_out_w, hg_norm_w=hg_norm_w, attn_sinks=attn_sinks, lb_table=lb_table)
    m_small = dict(b_ada=m_b_ada, pre_w_mix=m_pre_w_mix, post_w_mix=m_post_w_mix, pre_w_mlp=m_pre_w_mlp, post_w_mlp=m_post_w_mlp,
                   attn_out_w=m_attn_out_w, hg_norm_w=m_hg_norm_w, attn_sinks=m_attn_sinks, lb_table=m_lb_table)
    v_small = dict(b_ada=v_b_ada, pre_w_mix=v_pre_w_mix, post_w_mix=v_post_w_mix, pre_w_mlp=v_pre_w_mlp, post_w_mlp=v_post_w_mlp,
                   attn_out_w=v_attn_out_w, hg_norm_w=v_hg_norm_w, attn_sinks=v_attn_sinks, lb_table=v_lb_table)
    *small_packed, loss_rows = _small_update(packs, _pack_small(w_small), _pack_small(m_small), _pack_small(v_small), bsz)
    small_out = [_unpack_small(p) for p in small_packed]
    loss = loss_rows[0, 0]

    dmod_all = packs[:, :bsz * MOD_ROWS, :].reshape(N_DEV * bsz, N_MOD * D_MODEL)
    dmod_cols = lax.dynamic_slice_in_dim(dmod_all, chip * ada_cols, ada_cols, axis=1)
    ada_out = _ada_bwd_adamw(c_all, dmod_cols, w_ada[0], m_w_ada[0], v_w_ada[0])

    big = dict(
        w_in=tuple(jnp.transpose(a) for a in _adamw_halves(half_in, their_in, core, w_in_t[0], m_in_t[0], v_in_t[0], axis=0,
                                                           name="adamw_in")),
        w_up=tuple(_adamw_halves(half_up, their_up, core, w_up[0], m_w_up[0], v_w_up[0], axis=0, name="adamw_up")),
        w_out=tuple(_adamw_halves(half_out, their_out, core, w_out[0], m_w_out[0], v_w_out[0], axis=1, name="adamw_out")),
        w_down=tuple(_adamw_halves(half_down, their_down, core, w_down[0], m_w_down[0], v_w_down[0], axis=0, name="adamw_down")),
        w_ada=tuple(ada_out),
    )
    order = ("w_ada", "b_ada", "pre_w_mix", "w_in", "attn_sinks", "attn_out_w", "lb_table", "hg_norm_w", "w_out", "post_w_mix",
             "pre_w_mlp", "w_up", "w_down", "post_w_mlp")
    outs = [loss, grad_x]
    for kind in range(4):
        for nm in order:
            outs.append(big[nm][kind][None] if nm in big else small_out[kind][nm])
    return tuple(outs)
```

```python
import jax
import jax.numpy as jnp
from jax import lax
from jax.experimental import pallas as pl
from jax.experimental.pallas import tpu as pltpu

F32 = jnp.float32
BF16 = jnp.bfloat16

D_MODEL = 1024
ATT_WIDTH = 512
ATT_HEAD_DIM = 64
ATT_Q_HEADS = 8
ATT_KV_HEADS = 2
ATT_GROUP = ATT_Q_HEADS // ATT_KV_HEADS
ATT_KV_COLS = ATT_KV_HEADS * ATT_HEAD_DIM
WINDOW = 128
ROPE_DIM = 16
ROPE_THETA = 500000.0
HG_WIDTH = 512
MIX_WIDTH = ATT_WIDTH + HG_WIDTH
HG_HEAD_DIM = 128
HG_HEADS = 4
HG_CHUNK = 32
IN_COLS = ATT_WIDTH + 2 * ATT_KV_COLS + 4 * HG_WIDTH
ATT_COLS = ATT_WIDTH + 2 * ATT_KV_COLS
D_FF = 4 * D_MODEL
N_MOD = 6
EPS = 1e-6
ATT_SCALE = ATT_HEAD_DIM ** -0.5

ADAM_LR = 0.001
ADAM_B1 = 0.9
ADAM_B2 = 0.999
ADAM_EPS = 1e-08
ADAM_WD = 0.01
ADAM_STEP = 10

N_CHIPS = 4
N_DEV = 8
LANE = 128
VMEM_LIMIT = 48 * 1024 * 1024
VMEM_LIMIT_BIG = 58 * 1024 * 1024
MESH = pl.DeviceIdType.MESH

NT_DIMS = (((1,), (1,)), ((), ()))
TN_DIMS = (((0,), (0,)), ((), ()))


def _sds(shape, dtype):
    return jax.ShapeDtypeStruct(tuple(shape), dtype)


def _params(*sem, vmem_limit=None):
    return pltpu.CompilerParams(dimension_semantics=sem, vmem_limit_bytes=VMEM_LIMIT if vmem_limit is None else vmem_limit)


def _sigmoid(x):
    return 1.0 / (1.0 + jnp.exp(-x))


def _dot(a, b, dims=None):
    a, b = a.astype(BF16), b.astype(BF16)
    if dims is None:
        return jnp.dot(a, b, preferred_element_type=F32)
    return lax.dot_general(a, b, dims, preferred_element_type=F32)


def _rms_fwd(x, w):
    rstd = lax.rsqrt(jnp.mean(x * x, axis=-1, keepdims=True) + EPS)
    xh = x * rstd
    return xh * w, xh, rstd


def _rms_bwd(dy, xh, rstd, w):
    dxh = dy * w
    dx = rstd * (dxh - xh * jnp.mean(dxh * xh, axis=-1, keepdims=True))
    return dx, dy * xh


def _colsum(x):
    return jnp.sum(x, axis=0, keepdims=True)


def _row_tile(rows, cap=256):
    return max(t for t in range(16, cap + 1, 16) if rows % t == 0)


HBM_SPEC = pl.BlockSpec(memory_space=pltpu.HBM)


def _mesh_pos():
    return lax.axis_index("x"), lax.axis_index("y"), lax.axis_index("c")


class _Comm:
    def __init__(self, ins, outs, sems, start, finish, aliases=()):
        self.ins, self.outs, self.sems = list(ins), list(outs), list(sems)
        self.start, self.finish, self.aliases = start, finish, tuple(aliases)


def _call(body, args, *, name, grid, in_specs, out_specs, out_shape, sem, scratch_shapes=(), comms=(), aliases=None,
          vmem_limit=None):
    scratch_shapes = list(scratch_shapes)
    if not comms:
        return pl.pallas_call(body, name=name, grid=grid, in_specs=in_specs, out_specs=out_specs, out_shape=out_shape,
                              input_output_aliases=dict(aliases or {}), scratch_shapes=scratch_shapes,
                              compiler_params=_params(*sem, vmem_limit=vmem_limit))(*args)
    single = not isinstance(out_shape, (list, tuple))
    out_specs_l = [out_specs] if single else list(out_specs)
    out_shape_l = [out_shape] if single else list(out_shape)
    n_in, n_out, n_scr = len(in_specs), len(out_shape_l), len(scratch_shapes)
    n_ci = [len(cm.ins) for cm in comms]
    n_co = [len(cm.outs) for cm in comms]
    n_cs = [len(cm.sems) for cm in comms]
    aliases = dict(aliases or {})
    for k, cm in enumerate(comms):
        for i, o in cm.aliases:
            aliases[n_in + sum(n_ci[:k]) + i] = n_out + sum(n_co[:k]) + o

    def fused(*refs):
        pos = [0]

        def take(n):
            part = refs[pos[0]:pos[0] + n]
            pos[0] += n
            return part

        ins = take(n_in)
        c_ins = [take(n) for n in n_ci]
        outs = take(n_out)
        c_outs = [take(n) for n in n_co]
        scr = take(n_scr)
        c_sems = [take(n) for n in n_cs]
        first, last = True, True
        for d, size in enumerate(grid):
            first = jnp.logical_and(first, pl.program_id(d) == 0)
            last = jnp.logical_and(last, pl.program_id(d) == size - 1)

        def run(which):
            for cm, ci, co, cs in zip(comms, c_ins, c_outs, c_sems):
                getattr(cm, which)(ci, co, cs)

        if grid:
            pl.when(first)(lambda: run("start"))
        else:
            run("start")
        body(*ins, *outs, *scr)
        if grid:
            pl.when(last)(lambda: run("finish"))
        else:
            run("finish")

    res = pl.pallas_call(
        fused, name=name, grid=grid, in_specs=list(in_specs) + [HBM_SPEC] * sum(n_ci),
        out_specs=out_specs_l + [HBM_SPEC] * sum(n_co), out_shape=out_shape_l + [s for cm in comms for s in cm.outs],
        input_output_aliases=aliases, scratch_shapes=scratch_shapes + [s for cm in comms for s in cm.sems],
        compiler_params=_params(*["arbitrary"] * len(grid), vmem_limit=vmem_limit),
    )(*args, *[a for cm in comms for a in cm.ins])
    main = res[:n_out]
    extra, at = [], n_out
    for n in n_co:
        extra.append(list(res[at:at + n]))
        at += n
    return (main[0] if single else list(main)), extra


def _mm(a, b, *, name, out_dtype, trans_b=False, tm=512, tn=None, extra=(), epi=None, b_spec=None, n_out=None, comms=()):
    m_total, k_total = a.shape
    if n_out is None:
        n_out = b.shape[0] if trans_b else b.shape[1]
    tn = n_out if tn is None else tn
    grid = (m_total // tm, n_out // tn)
    dims = NT_DIMS if trans_b else None

    def body(*refs):
        a_ref, b_ref = refs[0], refs[1]
        extra_refs = refs[2:2 + len(extra)]
        o_ref = refs[2 + len(extra)]
        acc = _dot(a_ref[...], b_ref[...], dims)
        if epi is not None:
            acc = epi(acc, *[r[...] for r in extra_refs])
        o_ref[...] = acc.astype(out_dtype)

    if b_spec is None:
        if trans_b:
            b_spec = pl.BlockSpec((tn, k_total), lambda i, j: (j, 0))
        else:
            b_spec = pl.BlockSpec((k_total, tn), lambda i, j: (0, j))
    in_specs = [pl.BlockSpec((tm, k_total), lambda i, j: (i, 0)), b_spec]
    in_specs += [pl.BlockSpec((tm, tn), lambda i, j: (i, j)) for _ in extra]
    return _call(
        body, (a, b, *extra), name=name, grid=grid, in_specs=in_specs,
        out_specs=pl.BlockSpec((tm, tn), lambda i, j: (i, j)),
        out_shape=_sds((m_total, n_out), out_dtype),
        sem=("parallel", "parallel"), comms=comms)


def _mm_tn(a, b, *, name, tk, tn, a_fn=None, out_shape=None, out_spec=None):
    m_total, k_total = a.shape
    n_total = b.shape[1]
    grid = (k_total // tk, n_total // tn)

    def body(a_ref, b_ref, o_ref):
        av = a_ref[...]
        part = _dot(av if a_fn is None else a_fn(av), b_ref[...], TN_DIMS)
        o_ref[...] = part.reshape(o_ref.shape)

    if out_shape is None:
        out_shape = _sds((k_total, n_total), F32)
        out_spec = pl.BlockSpec((tk, tn), lambda i, j: (i, j))
    return pl.pallas_call(
        body, name=name, grid=grid,
        in_specs=[pl.BlockSpec((m_total, tk), lambda i, j: (0, i)), pl.BlockSpec((m_total, tn), lambda i, j: (0, j))],
        out_specs=out_spec, out_shape=out_shape,
        compiler_params=_params("parallel", "parallel"),
    )(a, b)


def _ada_fwd(c_all, w_shard, b_shard):
    nb, ncol = c_all.shape[0], w_shard.shape[1]
    tn = 512

    def body(c_ref, w_ref, b_ref, o_ref):
        c = c_ref[...]
        o_ref[...] = _dot(c * _sigmoid(c), w_ref[...]) + b_ref[...]

    return pl.pallas_call(
        body, name="ada_fwd", grid=(ncol // tn,),
        in_specs=[pl.BlockSpec((nb, D_MODEL), lambda j: (0, 0)), pl.BlockSpec((D_MODEL, tn), lambda j: (0, j)),
                  pl.BlockSpec((1, tn), lambda j: (0, j))],
        out_specs=pl.BlockSpec((nb, tn), lambda j: (0, j)), out_shape=_sds((nb, ncol), F32),
        compiler_params=_params("parallel"),
    )(c_all, w_shard, b_shard)


def _adamw_math(g, w, m, v):
    m = ADAM_B1 * m + (1.0 - ADAM_B1) * g
    v = ADAM_B2 * v + (1.0 - ADAM_B2) * (g * g)
    m_hat = m / (1.0 - ADAM_B1 ** ADAM_STEP)
    v_hat = v / (1.0 - ADAM_B2 ** ADAM_STEP)
    delta = -ADAM_LR * (m_hat / (jnp.sqrt(v_hat) + ADAM_EPS) + ADAM_WD * w)
    return delta, m, v


def _ada_bwd_adamw(c_all, dmod_cols, w, m, v):
    nb, ncol = dmod_cols.shape
    tn = 256

    def body(c_ref, d_ref, w_ref, m_ref, v_ref, g_ref, dl_ref, nm_ref, nv_ref):
        c = c_ref[...]
        g = _dot(c * _sigmoid(c), d_ref[...], TN_DIMS)
        g_ref[...] = g
        dl_ref[...], nm_ref[...], nv_ref[...] = _adamw_math(g, w_ref[...], m_ref[...], v_ref[...])

    col = pl.BlockSpec((D_MODEL, tn), lambda j: (0, j))
    shp = _sds((D_MODEL, ncol), F32)
    return pl.pallas_call(
        body, name="ada_bwd_adamw", grid=(ncol // tn,),
        in_specs=[pl.BlockSpec((nb, D_MODEL), lambda j: (0, 0)), pl.BlockSpec((nb, tn), lambda j: (0, j)), col, col, col],
        out_specs=[col, col, col, col], out_shape=[shp, shp, shp, shp],
        compiler_params=_params("parallel"),
    )(c_all, dmod_cols, w, m, v)


def _adamw_halves(own, theirs, core, w, m, v, *, axis, name):
    r2, c2 = own.shape
    tr = _row_tile(r2)
    nt = r2 // tr

    def body(core_ref, own_ref, their_ref, w_ref, m_ref, v_ref, g_ref, dl_ref, nm_ref, nv_ref):
        g = jnp.where(pl.program_id(0) == core_ref[0], own_ref[...], their_ref[...])
        g_ref[...] = g
        dl_ref[...], nm_ref[...], nv_ref[...] = _adamw_math(g, w_ref[...], m_ref[...], v_ref[...])

    if axis == 0:
        full = pl.BlockSpec((tr, c2), lambda h, i, core_ref: (h * nt + i, 0))
    else:
        full = pl.BlockSpec((tr, c2), lambda h, i, core_ref: (i, h))
    half = pl.BlockSpec((tr, c2), lambda h, i, core_ref: (i, 0))
    shp = _sds(w.shape, F32)
    return pl.pallas_call(
        body, name=name,
        grid_spec=pltpu.PrefetchScalarGridSpec(num_scalar_prefetch=1, grid=(2, nt), in_specs=[half, half, full, full, full],
                                               out_specs=[full] * 4),
        out_shape=[shp] * 4, compiler_params=_params("parallel", "parallel"),
    )(core, own, theirs, w, m, v)


def _tok_spec(tm, width=D_MODEL):
    return pl.BlockSpec((None, tm, width), lambda b, i: (b, i, 0))


def _row_spec(width=D_MODEL):
    return pl.BlockSpec((None, 1, width), lambda b, i: (b, 0, 0))


def _vec_spec(width=D_MODEL):
    return pl.BlockSpec((1, width), lambda b, i: (0, 0))


class _RowsOf:
    def __init__(self, ref, first, count):
        self.ref, self.rows = ref, slice(first, first + count)

    def __getitem__(self, idx):
        return self.ref[self.rows, :]

    def __setitem__(self, idx, value):
        self.ref[self.rows, :] = value


def _mm_rows(a, b, *, name, tm, extra, extra_specs, out_specs, out_shape, epi, pro=None, trans_b=False, b_chunks=1, comms=(),
             parts=1, zero_per_seq=(), zero_once=(), vmem_limit=None):
    bsz, seq, k_total = a.shape
    kc = k_total // b_chunks
    dims = NT_DIMS if trans_b else None
    rows = tm // parts

    def body(*refs):
        a_ref, b_ref = refs[0], refs[1]
        ex, outs = refs[2:2 + len(extra)], refs[2 + len(extra):]
        if zero_per_seq:
            @pl.when(pl.program_id(1) == 0)
            def _():
                for k in zero_per_seq:
                    outs[k][...] = jnp.zeros_like(outs[k])
        if zero_once:
            @pl.when(jnp.logical_and(pl.program_id(0) == 0, pl.program_id(1) == 0))
            def _():
                for k in zero_once:
                    outs[k][...] = jnp.zeros_like(outs[k])

        def part_of(ref, p):
            tiled = len(ref.shape) == 2 and ref.shape[0] == tm
            return _RowsOf(ref, p * rows, rows) if tiled and parts > 1 else ref

        accs = []
        for p in range(parts):
            a_p, ex_p, outs_p = part_of(a_ref, p), [part_of(r, p) for r in ex], [part_of(r, p) for r in outs]
            if b_chunks == 1:
                accs.append(_dot(a_p[...] if pro is None else pro(a_p, ex_p, outs_p), b_ref[...], dims))
            else:
                acc = _dot(a_p[...][:, 0:kc], b_ref[0], NT_DIMS)
                for k in range(1, b_chunks):
                    acc = acc + _dot(a_p[...][:, k * kc:(k + 1) * kc], b_ref[k], NT_DIMS)
                accs.append(acc)
        for p in range(parts):
            epi(accs[p], [part_of(r, p) for r in ex], [part_of(r, p) for r in outs])

    b_spec = pl.BlockSpec(b.shape, lambda bb, i: (0,) * b.ndim)
    return _call(
        body, (a, b, *extra), name=name, grid=(bsz, seq // tm), in_specs=[_tok_spec(tm, k_total), b_spec, *extra_specs],
        out_specs=out_specs, out_shape=out_shape, sem=("arbitrary", "arbitrary"), comms=comms, vmem_limit=vmem_limit)


def _in_proj_fused(x, w, sc, sh, w_in_t, tables, comms=()):
    tm = 512
    bsz, seq, _ = x.shape
    half = ROPE_DIM // 2
    heads_per_slab = LANE // ATT_HEAD_DIM

    def pro(x_ref, ex, outs):
        y, _, _ = _rms_fwd(x_ref[...], ex[0][...])
        h = (y * (1.0 + ex[1][...]) + ex[2][...]).astype(BF16)
        outs[0][...] = h
        return h

    def epi(acc, ex, outs):
        c, u, d = ex[3][...], ex[4][...], ex[5][...]
        _, proj_ref, q_ref, k_ref, v_ref = outs
        proj_ref[...] = acc

        def rope(z):
            return (z * c + pltpu.roll(z, half, 1) * u + pltpu.roll(z, LANE - half, 1) * d).astype(BF16)

        for s in range(ATT_WIDTH // LANE):
            slab = rope(acc[:, s * LANE:(s + 1) * LANE])
            for part in range(heads_per_slab):
                g, hh = divmod(s * heads_per_slab + part, ATT_GROUP)
                piece = slab[:, part * ATT_HEAD_DIM:(part + 1) * ATT_HEAD_DIM]
                for blk in range(tm // WINDOW):
                    q_ref[blk, g, hh * WINDOW:(hh + 1) * WINDOW, :] = piece[blk * WINDOW:(blk + 1) * WINDOW]
        rk = rope(acc[:, ATT_WIDTH:ATT_WIDTH + LANE])
        vv = acc[:, ATT_WIDTH + LANE:ATT_COLS].astype(BF16)
        for g in range(ATT_KV_HEADS):
            k_ref[g] = rk[:, g * ATT_HEAD_DIM:(g + 1) * ATT_HEAD_DIM]
            v_ref[g] = vv[:, g * ATT_HEAD_DIM:(g + 1) * ATT_HEAD_DIM]

    cols = w_in_t.shape[0]
    tab = pl.BlockSpec((tm, LANE), lambda b, i: (i, 0))
    kv_spec = pl.BlockSpec((None, ATT_KV_HEADS, tm, ATT_HEAD_DIM), lambda b, i: (b, 0, i, 0))
    kv_shape = _sds((bsz, ATT_KV_HEADS, seq, ATT_HEAD_DIM), BF16)
    q_spec = pl.BlockSpec((None, tm // WINDOW, ATT_KV_HEADS, GROUP_ROWS, ATT_HEAD_DIM), lambda b, i: (b, i, 0, 0, 0))
    return _mm_rows(x, w_in_t, name="in_proj", tm=tm, extra=(w, sc, sh, *tables),
                    extra_specs=[_vec_spec(), _row_spec(), _row_spec(), tab, tab, tab],
                    out_specs=[_tok_spec(tm), _tok_spec(tm, cols), q_spec, kv_spec, kv_spec],
                    out_shape=[_sds(x.shape, BF16), _sds((bsz, seq, cols), F32),
                               _sds((bsz, seq // WINDOW, ATT_KV_HEADS, GROUP_ROWS, ATT_HEAD_DIM), BF16), kv_shape, kv_shape],
                    pro=pro, epi=epi, trans_b=True, comms=comms)


def _rope_tables(seq):
    half = ROPE_DIM // 2
    inv_freq = ROPE_THETA ** (-jnp.arange(0, ROPE_DIM, 2, dtype=F32) / ROPE_DIM)
    ang = jnp.arange(seq, dtype=F32)[:, None] * inv_freq[None, :]
    cos, sin = jnp.cos(ang), jnp.sin(ang)
    rest = ATT_HEAD_DIM - ROPE_DIM
    ones, zeros, zh = jnp.ones((seq, rest), F32), jnp.zeros((seq, rest), F32), jnp.zeros((seq, half), F32)
    reps = LANE // ATT_HEAD_DIM
    t_cos = jnp.tile(jnp.concatenate([cos, cos, ones], axis=1), (1, reps))
    t_up = jnp.tile(jnp.concatenate([zh, sin, zeros], axis=1), (1, reps))
    t_dn = jnp.tile(jnp.concatenate([-sin, zh, zeros], axis=1), (1, reps))
    return t_cos, t_up, t_dn


GROUP_ROWS = ATT_GROUP * WINDOW


ATT_BPS = 2


def _band_mask(has_prev):
    row = lax.broadcasted_iota(jnp.int32, (GROUP_ROWS, 2 * WINDOW), 0) % WINDOW
    col = lax.broadcasted_iota(jnp.int32, (GROUP_ROWS, 2 * WINDOW), 1)
    prev = jnp.logical_and(jnp.logical_and(col < WINDOW, col > row), has_prev)
    return jnp.logical_or(prev, jnp.logical_and(col >= WINDOW, col - WINDOW <= row))


def _sink_column(sink_ref, g):
    head = lax.broadcasted_iota(jnp.int32, (GROUP_ROWS, 1), 0) // WINDOW
    col = jnp.full((GROUP_ROWS, 1), sink_ref[0, g * ATT_GROUP], F32)
    for hh in range(1, ATT_GROUP):
        col = jnp.where(head == hh, sink_ref[0, g * ATT_GROUP + hh], col)
    return col


def _attn_specs():
    q_spec = pl.BlockSpec((None, ATT_BPS, ATT_KV_HEADS, GROUP_ROWS, ATT_HEAD_DIM), lambda b, i: (b, i, 0, 0, 0))
    kv_cur = pl.BlockSpec((None, ATT_KV_HEADS, ATT_BPS * WINDOW, ATT_HEAD_DIM), lambda b, i: (b, 0, i, 0))
    kv_prev = pl.BlockSpec((None, ATT_KV_HEADS, WINDOW, ATT_HEAD_DIM), lambda b, i: (b, 0, jnp.maximum(ATT_BPS * i - 1, 0), 0))
    return q_spec, kv_cur, kv_prev


def _band(prev_ref, cur_ref, g, blk):
    own = cur_ref[g, blk * WINDOW:(blk + 1) * WINDOW]
    before = prev_ref[g] if blk == 0 else cur_ref[g, (blk - 1) * WINDOW:blk * WINDOW]
    return jnp.concatenate([before, own], axis=0)


def _attn_fwd(qh, kh, vh, sinks, w_norm, comms=()):
    bsz, nblk = qh.shape[0], qh.shape[1]
    seq = nblk * WINDOW
    rows = ATT_BPS * WINDOW
    neg = float(jnp.finfo(jnp.float32).min)

    def body(sink_ref, q_ref, kc_ref, kp_ref, vc_ref, vp_ref, w_ref, raw_ref, an_ref, l_ref):
        for blk in range(ATT_BPS):
            mask = _band_mask(True if blk else pl.program_id(1) > 0)
            for g in range(ATT_KV_HEADS):
                keys, vals = _band(kp_ref, kc_ref, g, blk), _band(vp_ref, vc_ref, g, blk)
                sink = _sink_column(sink_ref, g)
                s = jnp.where(mask, _dot(q_ref[blk, g], keys, NT_DIMS) * ATT_SCALE, neg)
                m = jnp.maximum(jnp.max(s, axis=-1, keepdims=True), sink)
                p = jnp.where(mask, jnp.exp(s - m), 0.0)
                den = jnp.sum(p, axis=-1, keepdims=True) + jnp.exp(sink - m)
                o = _dot(p / den, vals)
                lse = m + jnp.log(den)
                tok = slice(blk * WINDOW, (blk + 1) * WINDOW)
                for hh in range(ATT_GROUP):
                    h = g * ATT_GROUP + hh
                    raw_ref[tok, h * ATT_HEAD_DIM:(h + 1) * ATT_HEAD_DIM] = o[hh * WINDOW:(hh + 1) * WINDOW]
                    l_ref[tok, h:h + 1] = lse[hh * WINDOW:(hh + 1) * WINDOW]
        y, _, _ = _rms_fwd(raw_ref[...], w_ref[...])
        an_ref[...] = y.astype(BF16)

    cur = lambda width: pl.BlockSpec((None, rows, width), lambda b, i: (b, i, 0))
    q_spec, kv_cur, kv_prev = _attn_specs()
    return _call(
        body, (sinks, qh, kh, kh, vh, vh, w_norm), name="attn_fwd", grid=(bsz, nblk // ATT_BPS),
        in_specs=[pl.BlockSpec(memory_space=pltpu.SMEM), q_spec, kv_cur, kv_prev, kv_cur, kv_prev, _vec_spec(ATT_WIDTH)],
        out_specs=[cur(ATT_WIDTH), cur(ATT_WIDTH), cur(ATT_Q_HEADS)],
        out_shape=[_sds((bsz, seq, ATT_WIDTH), F32), _sds((bsz, seq, MIX_WIDTH), BF16), _sds((bsz, seq, ATT_Q_HEADS), F32)],
        sem=("parallel", "parallel"), comms=comms)


HG_Q0 = ATT_COLS // LANE
HG_F0 = HG_Q0 + HG_HEADS
HG_I0 = HG_F0 + HG_HEADS
HG_G0 = HG_I0 + HG_HEADS
HG_TOK = 256
HG_NCH = HG_TOK // HG_CHUNK
HG_HPS = 2


def _block_masks():
    row = lax.broadcasted_iota(jnp.int32, (HG_TOK, HG_TOK), 0)
    col = lax.broadcasted_iota(jnp.int32, (HG_TOK, HG_TOK), 1)
    same = (row // HG_CHUNK) == (col // HG_CHUNK)
    return jnp.logical_and(same, col <= row), jnp.logical_and(same, col >= row)


def _row_in_chunk():
    return lax.broadcasted_iota(jnp.int32, (HG_TOK, LANE), 0) % HG_CHUNK


def _chunk_cumsum(x, reverse=False):
    ric = _row_in_chunk()
    shift = 1
    while shift < HG_CHUNK:
        if reverse:
            x = x + jnp.where(ric < HG_CHUNK - shift, pltpu.roll(x, HG_TOK - shift, 0), 0.0)
        else:
            x = x + jnp.where(ric >= shift, pltpu.roll(x, shift, 0), 0.0)
        shift *= 2
    return x


def _chunk_rows(rows):
    stacked = jnp.concatenate([r[None] for r in rows], axis=0)
    return jnp.broadcast_to(stacked, (HG_NCH, HG_CHUNK, LANE)).reshape(HG_TOK, LANE)


def _chunk_slices(x):
    return [x[j * HG_CHUNK:(j + 1) * HG_CHUNK] for j in range(HG_NCH)]


def _hgrn_common(tbl, hf, hq):
    lb = _sigmoid(tbl[1:2] - tbl[0:1])
    sig = _sigmoid(hf)
    f = lb + (1.0 - lb) * sig
    sq = _sigmoid(hq)
    q, k = hq * sq, 1.0 - f
    b = _chunk_cumsum(jnp.log(f))
    last = [b[(j + 1) * HG_CHUNK - 1:(j + 1) * HG_CHUNK] for j in range(HG_NCH)]
    bl = _chunk_rows(last)
    e_b, e_nb, e_rem = jnp.exp(b), jnp.exp(-b), jnp.exp(bl - b)
    e_last = [jnp.exp(r) for r in last]
    return dict(lb=lb, sig=sig, f=f, sq=sq, q=q, k=k, e_b=e_b, e_nb=e_nb, e_rem=e_rem, e_last=e_last,
                qd=q * e_b, kd=k * e_nb, ku=k * e_rem)


def _hgrn_fwd(proj, lb_table, norm_w, mix_in, comms=()):
    bsz, seq, _ = proj.shape
    nstep = seq // HG_TOK

    def body(tbl_ref, nw_ref, q_ref, f_ref, i_ref, g_ref, mix_ref, o_ref, rec_ref, st_ref, s_scr):
        @pl.when(pl.program_id(2) == 0)
        def _():
            s_scr[...] = jnp.zeros_like(s_scr)

        lower, _ = _block_masks()
        for hp in range(HG_HPS):
            ls = slice(hp * LANE, (hp + 1) * LANE)
            v, hg = i_ref[:, ls], g_ref[:, ls]
            t = _hgrn_common(tbl_ref[:, ls], f_ref[:, ls], q_ref[:, ls])
            a = jnp.where(lower, _dot(t["qd"], t["kd"], NT_DIMS), 0.0)
            o_intra = _dot(a, v)
            v_c, ku_c, qd_c = [_chunk_slices(z.astype(BF16)) for z in (v, t["ku"], t["qd"])]
            updates = [_dot(v_c[j], ku_c[j], TN_DIMS) for j in range(HG_NCH)]
            st = s_scr[hp]
            states = []
            for j in range(HG_NCH):
                states.append(st)
                st = st * t["e_last"][j] + updates[j]
            s_scr[hp] = st
            o = o_intra + jnp.concatenate([_dot(qd_c[j], states[j], NT_DIMS) for j in range(HG_NCH)], axis=0)
            for j in range(HG_NCH):
                st_ref[hp, j] = states[j]
            o_ref[:, ls] = o
            y, _, _ = _rms_fwd(o, nw_ref[...])
            rec_ref[:, ls] = (y * (hg * _sigmoid(hg))).astype(BF16)

    width = HG_HPS * LANE
    slab = lambda first: pl.BlockSpec((None, HG_TOK, width), lambda b, h, t: (b, t, first // HG_HPS + h))
    head_out = pl.BlockSpec((None, HG_TOK, width), lambda b, h, t: (b, t, h))
    mix_out = pl.BlockSpec((None, HG_TOK, width), lambda b, h, t: (b, t, ATT_WIDTH // width + h))
    return _call(
        body, (lb_table, norm_w, proj, proj, proj, proj, mix_in), name="hgrn_fwd", grid=(bsz, HG_HEADS // HG_HPS, nstep),
        in_specs=[pl.BlockSpec((2, width), lambda b, h, t: (0, h)), pl.BlockSpec((1, LANE), lambda b, h, t: (0, 0)),
                  slab(HG_Q0), slab(HG_F0), slab(HG_I0), slab(HG_G0), pl.BlockSpec(memory_space=pl.ANY)],
        out_specs=[head_out, mix_out,
                   pl.BlockSpec((None, HG_HPS, HG_NCH, LANE, LANE), lambda b, h, t: (b, h, t, 0, 0))],
        out_shape=[_sds((bsz, seq, HG_WIDTH), F32), _sds(mix_in.shape, BF16),
                   _sds((bsz, HG_HEADS, seq // HG_CHUNK, LANE, LANE), F32)],
        scratch_shapes=[pltpu.VMEM((HG_HPS, LANE, LANE), F32)],
        sem=("parallel", "parallel", "arbitrary"), comms=comms, aliases={6: 1})


def _out_proj_fused(cat, w_out, x, post_w, g1, pre_w, sc2, sh2):
    tm = 512

    def epi(mix, ex, outs):
        x_ref, pw_ref, g1_ref, w2_ref, sc_ref, sh_ref = ex
        outs[0][...] = mix
        n1, _, _ = _rms_fwd(mix, pw_ref[...])
        x1 = x_ref[...] + g1_ref[...] * n1
        outs[1][...] = x1
        y2, _, _ = _rms_fwd(x1, w2_ref[...])
        outs[2][...] = (y2 * (1.0 + sc_ref[...]) + sh_ref[...]).astype(BF16)

    return _mm_rows(cat, w_out, name="out_proj", tm=tm, extra=(x, post_w, g1, pre_w, sc2, sh2),
                    extra_specs=[_tok_spec(tm), _vec_spec(), _row_spec(), _vec_spec(), _row_spec(), _row_spec()],
                    out_specs=[_tok_spec(tm), _tok_spec(tm), _tok_spec(tm)],
                    out_shape=[_sds(x.shape, F32), _sds(x.shape, F32), _sds(x.shape, BF16)], epi=epi)


def _acc_out(ref, first, value):
    @pl.when(first)
    def _():
        ref[...] = value

    @pl.when(jnp.logical_not(first))
    def _():
        ref[...] += value


def _down_proj_fused(r, w_down, x1, post_w, g2, target):
    tm = 512
    bsz = x1.shape[0]

    def pro(r_ref, ex, outs):
        rv = r_ref[...]
        return rv * rv

    def epi(down, ex, outs):
        x1_ref, w_ref, g2_ref, t_ref = ex
        loss_ref, dy_ref, dd_ref, dg2_ref, dw_ref = outs
        w, g2v = w_ref[...], g2_ref[...]
        n2, dh, rstd = _rms_fwd(down, w)
        err = x1_ref[...] + g2v * n2 - t_ref[...]
        part = (0.5 / D_MODEL) * jnp.sum(jnp.sum(err * err, axis=-1, keepdims=True), axis=0, keepdims=True)
        loss_ref[...] += jnp.broadcast_to(part, (1, LANE))
        dy = err * (1.0 / D_MODEL)
        dy_ref[...] = dy
        dg2_ref[...] += _colsum(dy * n2)
        dd, dw_rows = _rms_bwd(dy * g2v, dh, rstd, w)
        dd_ref[...] = dd.astype(BF16)
        dw_ref[...] += _colsum(dw_rows)

    return _mm_rows(r, w_down, name="down_proj", tm=tm, extra=(x1, post_w, g2, target),
                    extra_specs=[_tok_spec(tm), _vec_spec(), _row_spec(), _tok_spec(tm)],
                    out_specs=[_vec_spec(LANE), _tok_spec(tm), _tok_spec(tm), _row_spec(), _vec_spec()],
                    out_shape=[_sds((1, LANE), F32), _sds(x1.shape, F32), _sds(x1.shape, BF16), _sds((bsz, 1, D_MODEL), F32),
                               _sds((1, D_MODEL), F32)], pro=pro, epi=epi, parts=2, zero_per_seq=(3,), zero_once=(0, 4),
                    vmem_limit=VMEM_LIMIT_BIG)


def _up_bwd_fused(dpre, w_up4, dy, x1, mix, pre_w, sc2, post_w, g1, comms=()):
    tm = 512
    bsz = x1.shape[0]

    def epi(dh2v, ex, outs):
        dy_ref, x1_ref, mix_ref, w2_ref, sc_ref, pw_ref, g1_ref = ex
        dx1_ref, dmix_ref, dsc_ref, dsh_ref, dg1_ref, dw2_ref, dpw_ref = outs
        w2, pw = w2_ref[...], pw_ref[...]
        y2, xh2, rstd2 = _rms_fwd(x1_ref[...], w2)
        dsh_ref[...] += _colsum(dh2v)
        dsc_ref[...] += _colsum(dh2v * y2)
        dx1n, dw_rows = _rms_bwd(dh2v * (1.0 + sc_ref[...]), xh2, rstd2, w2)
        dw2_ref[...] += _colsum(dw_rows)
        dx1 = dy_ref[...] + dx1n
        dx1_ref[...] = dx1
        n1, mh, rstd1 = _rms_fwd(mix_ref[...], pw)
        dg1_ref[...] += _colsum(dx1 * n1)
        dmix, dpw_rows = _rms_bwd(dx1 * g1_ref[...], mh, rstd1, pw)
        dmix_ref[...] = dmix.astype(BF16)
        dpw_ref[...] += _colsum(dpw_rows)

    row_shape = _sds((bsz, 1, D_MODEL), F32)
    vec_shape = _sds((1, D_MODEL), F32)
    return _mm_rows(dpre, w_up4, name="up_bwd", tm=tm, extra=(dy, x1, mix, pre_w, sc2, post_w, g1),
                    extra_specs=[_tok_spec(tm), _tok_spec(tm), _tok_spec(tm), _vec_spec(), _row_spec(), _vec_spec(), _row_spec()],
                    out_specs=[_tok_spec(tm), _tok_spec(tm), _row_spec(), _row_spec(), _row_spec(), _vec_spec(), _vec_spec()],
                    out_shape=[_sds(x1.shape, F32), _sds(x1.shape, BF16), row_shape, row_shape, row_shape, vec_shape, vec_shape],
                    epi=epi, b_chunks=w_up4.shape[0], comms=comms, parts=2, zero_per_seq=(2, 3, 4), zero_once=(5, 6),
                    vmem_limit=VMEM_LIMIT_BIG)


def _norm1_bwd(dh1, dx1, x, pre_w, sc1, tm=512, comms=()):
    bsz, seq, _ = x.shape

    def body(dh_ref, dx1_ref, x_ref, w_ref, sc_ref, gx_ref, dsc_ref, dsh_ref, dw_ref):
        b, i = pl.program_id(0), pl.program_id(1)
        w = w_ref[...]
        dh = dh_ref[...]
        y, xh, rstd = _rms_fwd(x_ref[...], w)
        _acc_out(dsh_ref, i == 0, _colsum(dh))
        _acc_out(dsc_ref, i == 0, _colsum(dh * y))
        dx, dw_rows = _rms_bwd(dh * (1.0 + sc_ref[...]), xh, rstd, w)
        _acc_out(dw_ref, jnp.logical_and(b == 0, i == 0), _colsum(dw_rows))
        gx_ref[...] = dx1_ref[...] + dx

    row_shape = _sds((bsz, 1, D_MODEL), F32)
    return _call(
        body, (dh1, dx1, x, pre_w, sc1), name="norm1_bwd", grid=(bsz, seq // tm),
        in_specs=[_tok_spec(tm), _tok_spec(tm), _tok_spec(tm), _vec_spec(), _row_spec()],
        out_specs=[_tok_spec(tm), _row_spec(), _row_spec(), _vec_spec()],
        out_shape=[_sds(x.shape, F32), row_shape, row_shape, _sds((1, D_MODEL), F32)],
        sem=("arbitrary", "arbitrary"), comms=comms)


def _hgrn_bwd(dcat, proj, o_raw, states, lb_table, norm_w, comms=()):
    bsz, seq, _ = proj.shape
    nstep = seq // HG_TOK
    rec0 = ATT_WIDTH // LANE

    def body(tbl_ref, nw_ref, dr_ref, q_ref, f_ref, i_ref, g_ref, o_ref, st_ref,
             dq_ref, df_ref, di_ref, dg_ref, dlb_ref, dnw_ref, ds_scr):
        h, b, t = pl.program_id(0), pl.program_id(1), pl.program_id(2)

        @pl.when(t == 0)
        def _():
            ds_scr[...] = jnp.zeros_like(ds_scr)

        lower, upper = _block_masks()
        dlb_parts = []
        dnw_acc = jnp.zeros((1, LANE), F32)
        for hp in range(HG_HPS):
            ls = slice(hp * LANE, (hp + 1) * LANE)
            hq, v, hg = q_ref[:, ls], i_ref[:, ls], g_ref[:, ls]
            nw = nw_ref[...]
            c = _hgrn_common(tbl_ref[:, ls], f_ref[:, ls], hq)
            qd, kd, ku = c["qd"], c["kd"], c["ku"]
            y, on, rstd = _rms_fwd(o_ref[:, ls], nw)
            sg = _sigmoid(hg)
            dr = dr_ref[:, ls]
            dg_ref[:, ls] = (dr * y * (sg * (1.0 + hg * (1.0 - sg)))).astype(BF16)
            do, dnw_rows = _rms_bwd(dr * (hg * sg), on, rstd, nw)
            at = jnp.where(upper, _dot(kd, qd, NT_DIMS), 0.0)
            da = jnp.where(lower, _dot(do, v, NT_DIMS), 0.0)
            dat = jnp.where(upper, _dot(v, do, NT_DIMS), 0.0)
            dv = _dot(at, do)
            dqd = _dot(da, kd)
            dkd = _dot(dat, qd)
            do_c, qd_c, v_c, ku_c = [_chunk_slices(z.astype(BF16)) for z in (do, qd, v, ku)]
            outer = [_dot(do_c[j], qd_c[j], TN_DIMS) for j in range(HG_NCH)]
            ds = ds_scr[hp]
            ds_after = [None] * HG_NCH
            for j in reversed(range(HG_NCH)):
                ds_after[j] = ds
                ds = outer[j] + ds * c["e_last"][j]
            ds_scr[hp] = ds
            states = [st_ref[hp, j] for j in range(HG_NCH)]
            dv = dv + jnp.concatenate([_dot(ku_c[j], ds_after[j], NT_DIMS) for j in range(HG_NCH)], axis=0)
            dqd = dqd + jnp.concatenate([_dot(do_c[j], states[j]) for j in range(HG_NCH)], axis=0)
            dku = jnp.concatenate([_dot(v_c[j], ds_after[j]) for j in range(HG_NCH)], axis=0)
            dku_ku = dku * ku
            dbl = [_colsum(states[j] * ds_after[j]) * c["e_last"][j] + _colsum(dku_ku[j * HG_CHUNK:(j + 1) * HG_CHUNK])
                   for j in range(HG_NCH)]
            dk = dkd * c["e_nb"] + dku * c["e_rem"]
            db = dqd * qd - dkd * kd - dku_ku + jnp.where(_row_in_chunk() == HG_CHUNK - 1, _chunk_rows(dbl), 0.0)
            dfv = _chunk_cumsum(db, reverse=True) / c["f"] - dk
            sig, sq = c["sig"], c["sq"]
            df_ref[:, ls] = (dfv * (1.0 - c["lb"]) * sig * (1.0 - sig)).astype(BF16)
            dq_ref[:, ls] = (dqd * c["e_b"] * (sq * (1.0 + hq * (1.0 - sq)))).astype(BF16)
            di_ref[:, ls] = dv.astype(BF16)
            dlb_parts.append(_colsum(dfv * (1.0 - sig)))
            dnw_acc = dnw_acc + _colsum(dnw_rows)
        _acc_out(dlb_ref, jnp.logical_and(b == 0, t == 0), jnp.concatenate(dlb_parts, axis=1))
        _acc_out(dnw_ref, jnp.logical_and(h == 0, jnp.logical_and(b == 0, t == 0)), dnw_acc)

    rev = lambda t: nstep - 1 - t
    width = HG_HPS * LANE
    slab = lambda first: pl.BlockSpec((None, HG_TOK, width), lambda h, b, t: (b, rev(t), first // HG_HPS + h))
    head = pl.BlockSpec((None, HG_TOK, width), lambda h, b, t: (b, rev(t), h))
    grad_shape = _sds((bsz, seq, HG_WIDTH), BF16)
    return _call(
        body, (lb_table, norm_w, dcat, proj, proj, proj, proj, o_raw, states), name="hgrn_bwd",
        grid=(HG_HEADS // HG_HPS, bsz, nstep),
        in_specs=[pl.BlockSpec((2, width), lambda h, b, t: (0, h)), pl.BlockSpec((1, LANE), lambda h, b, t: (0, 0)),
                  slab(rec0), slab(HG_Q0), slab(HG_F0), slab(HG_I0), slab(HG_G0), head,
                  pl.BlockSpec((None, HG_HPS, HG_NCH, LANE, LANE), lambda h, b, t: (b, h, rev(t), 0, 0))],
        out_specs=[head, head, head, head, pl.BlockSpec((1, width), lambda h, b, t: (0, h)),
                   pl.BlockSpec((1, LANE), lambda h, b, t: (0, 0))],
        out_shape=[grad_shape, grad_shape, grad_shape, grad_shape, _sds((1, HG_WIDTH), F32), _sds((1, LANE), F32)],
        scratch_shapes=[pltpu.VMEM((HG_HPS, LANE, LANE), F32)],
        sem=("arbitrary", "arbitrary", "arbitrary"), comms=comms)


def _attn_bwd(dcat, raw, w_norm, qh, kh, vh, lse, sinks, tables, comms=()):
    bsz, nblk = qh.shape[0], qh.shape[1]
    seq = nblk * WINDOW
    nstep = nblk // ATT_BPS
    half = ROPE_DIM // 2

    def body(sink_ref, da_ref, raw_ref, w_ref, q_ref, kc_ref, kp_ref, vc_ref, vp_ref, l_ref, c_ref, u_ref, d_ref,
             o_ref, dw_ref, dsink_ref, carry_k, carry_v):
        b, i = pl.program_id(0), pl.program_id(1)
        first = jnp.logical_and(b == 0, i == 0)

        @pl.when(i == 0)
        def _():
            carry_k[...] = jnp.zeros_like(carry_k)
            carry_v[...] = jnp.zeros_like(carry_v)

        w = w_ref[...]
        _, on, rstd = _rms_fwd(raw_ref[...], w)
        do_step, dw_rows = _rms_bwd(da_ref[...], on, rstd, w)
        _acc_out(dw_ref, first, _colsum(dw_rows))
        lane8 = lax.broadcasted_iota(jnp.int32, (1, ATT_Q_HEADS), 1)
        dsink = jnp.zeros((1, ATT_Q_HEADS), F32)
        from_next_k, from_next_v = carry_k[...], carry_v[...]
        for blk in reversed(range(ATT_BPS)):
            tok = slice(blk * WINDOW, (blk + 1) * WINDOW)
            mask = _band_mask(True if blk else i < nstep - 1)
            raw_v, do_all = raw_ref[tok, :], do_step[tok]
            c, u, d = c_ref[tok, :], u_ref[tok, :], d_ref[tok, :]

            def unrope(g):
                return (g * c + pltpu.roll(g * u, LANE - half, 1) + pltpu.roll(g * d, half, 1)).astype(BF16)

            dq_parts, dk_own, dk_before, dv_own, dv_before = [], [], [], [], []
            for g in range(ATT_KV_HEADS):
                heads = [slice((g * ATT_GROUP + hh) * ATT_HEAD_DIM, (g * ATT_GROUP + hh + 1) * ATT_HEAD_DIM)
                         for hh in range(ATT_GROUP)]
                q = q_ref[blk, g]
                keys, vals = _band(kp_ref, kc_ref, g, blk), _band(vp_ref, vc_ref, g, blk)
                do_g = jnp.concatenate([do_all[:, hs] for hs in heads], axis=0)
                dsum = jnp.concatenate([jnp.sum(do_all[:, hs] * raw_v[:, hs], axis=-1, keepdims=True) for hs in heads], axis=0)
                lse_g = jnp.concatenate([l_ref[tok, g * ATT_GROUP + hh:g * ATT_GROUP + hh + 1] for hh in range(ATT_GROUP)], axis=0)
                p = jnp.where(mask, jnp.exp(_dot(q, keys, NT_DIMS) * ATT_SCALE - lse_g), 0.0)
                sink_part = jnp.exp(_sink_column(sink_ref, g) - lse_g) * dsum
                for hh in range(ATT_GROUP):
                    head_sum = jnp.sum(sink_part[hh * WINDOW:(hh + 1) * WINDOW], axis=0, keepdims=True)
                    dsink = dsink - jnp.where(lane8 == g * ATT_GROUP + hh, head_sum, 0.0)
                ds = p * (_dot(do_g, vals, NT_DIMS) - dsum) * ATT_SCALE
                dq_g = _dot(ds, keys)
                dq_parts += [dq_g[hh * WINDOW:(hh + 1) * WINDOW] for hh in range(ATT_GROUP)]
                dk_g = _dot(ds, q, TN_DIMS)
                dv_g = _dot(p, do_g, TN_DIMS)
                dk_before.append(dk_g[:WINDOW])
                dk_own.append(dk_g[WINDOW:])
                dv_before.append(dv_g[:WINDOW])
                dv_own.append(dv_g[WINDOW:])
            per_slab = LANE // ATT_HEAD_DIM
            for s in range(ATT_WIDTH // LANE):
                slab = jnp.concatenate(dq_parts[s * per_slab:(s + 1) * per_slab], axis=1)
                o_ref[tok, s * LANE:(s + 1) * LANE] = unrope(slab)
            o_ref[tok, ATT_WIDTH:ATT_WIDTH + LANE] = unrope(jnp.concatenate(dk_own, axis=1) + from_next_k)
            o_ref[tok, ATT_WIDTH + LANE:ATT_COLS] = (jnp.concatenate(dv_own, axis=1) + from_next_v).astype(BF16)
            from_next_k, from_next_v = jnp.concatenate(dk_before, axis=1), jnp.concatenate(dv_before, axis=1)
        carry_k[...] = from_next_k
        carry_v[...] = from_next_v
        _acc_out(dsink_ref, first, dsink)

    rows = ATT_BPS * WINDOW
    rev = lambda i: nstep - 1 - i
    cur = lambda width: pl.BlockSpec((None, rows, width), lambda b, i: (b, rev(i), 0))
    q_spec = pl.BlockSpec((None, ATT_BPS, ATT_KV_HEADS, GROUP_ROWS, ATT_HEAD_DIM), lambda b, i: (b, rev(i), 0, 0, 0))
    kv_cur = pl.BlockSpec((None, ATT_KV_HEADS, rows, ATT_HEAD_DIM), lambda b, i: (b, 0, rev(i), 0))
    kv_prev = pl.BlockSpec((None, ATT_KV_HEADS, WINDOW, ATT_HEAD_DIM), lambda b, i: (b, 0, jnp.maximum(ATT_BPS * rev(i) - 1, 0), 0))
    tab = pl.BlockSpec((rows, LANE), lambda b, i: (rev(i), 0))
    return _call(
        body, (sinks, dcat, raw, w_norm, qh, kh, kh, vh, vh, lse, *tables), name="attn_bwd", grid=(bsz, nstep),
        in_specs=[pl.BlockSpec(memory_space=pltpu.SMEM), cur(ATT_WIDTH), cur(ATT_WIDTH), _vec_spec(ATT_WIDTH), q_spec,
                  kv_cur, kv_prev, kv_cur, kv_prev, cur(ATT_Q_HEADS), tab, tab, tab],
        out_specs=[cur(ATT_COLS), _vec_spec(ATT_WIDTH), _vec_spec(ATT_Q_HEADS)],
        out_shape=[_sds((bsz, seq, ATT_COLS), BF16), _sds((1, ATT_WIDTH), F32), _sds((1, ATT_Q_HEADS), F32)],
        scratch_shapes=[pltpu.VMEM((WINDOW, LANE), F32), pltpu.VMEM((WINDOW, LANE), F32)],
        sem=("arbitrary", "arbitrary"), comms=comms)


def _other_chips(x, y):
    return [(1 - x, y), (x, 1 - y), (1 - x, 1 - y)]


def _sem_pair(n):
    return [pltpu.SemaphoreType.DMA((n,)), pltpu.SemaphoreType.DMA((n,))]


def _rows_of(ref, rows):
    return ref if rows is None else ref.at[pl.ds(rows[0], rows[1])]


def _plan_chip_gather(blocks, bufs, rows=None, forward_rows=None):
    n = len(blocks)

    def copies(ins, outs, sems):
        x, y, c = _mesh_pos()
        sends, lands = [], []
        for a in range(n):
            for j, chip in enumerate(_other_chips(x, y)):
                k = 3 * a + j
                sends.append(pltpu.make_async_remote_copy(
                    src_ref=_rows_of(ins[a], rows), dst_ref=_rows_of(outs[a].at[4 * x + 2 * y + c], rows), send_sem=sems[0].at[k],
                    recv_sem=sems[1].at[k], device_id=(*chip, c), device_id_type=MESH))
                slot = _rows_of(outs[a].at[4 * chip[0] + 2 * chip[1] + c], rows)
                lands.append(pltpu.make_async_remote_copy(
                    src_ref=slot, dst_ref=slot, send_sem=sems[0].at[k], recv_sem=sems[1].at[k],
                    device_id=(*chip, c), device_id_type=MESH))
                if forward_rows is not None:
                    k = 3 * (n + a) + j
                    mine = _rows_of(outs[a].at[4 * chip[0] + 2 * chip[1] + c], forward_rows)
                    sends.append(pltpu.make_async_remote_copy(
                        src_ref=mine, dst_ref=mine, send_sem=sems[0].at[k], recv_sem=sems[1].at[k],
                        device_id=(x, y, 1 - c), device_id_type=MESH))
                    theirs = _rows_of(outs[a].at[4 * chip[0] + 2 * chip[1] + 1 - c], forward_rows)
                    lands.append(pltpu.make_async_remote_copy(
                        src_ref=theirs, dst_ref=theirs, send_sem=sems[0].at[k], recv_sem=sems[1].at[k],
                        device_id=(x, y, 1 - c), device_id_type=MESH))
        return sends, lands

    def start(ins, outs, sems):
        for cp in copies(ins, outs, sems)[0]:
            cp.start()

    def finish(ins, outs, sems):
        sends, lands = copies(ins, outs, sems)
        for cp in lands:
            cp.wait_recv()
        for cp in sends:
            cp.wait_send()

    n_sems = 3 * n * (2 if forward_rows is not None else 1)
    return _Comm(list(blocks) + list(bufs), [_sds(b.shape, b.dtype) for b in bufs], _sem_pair(n_sems), start, finish,
                 aliases=[(n + a, a) for a in range(n)])


def _plan_pair_forward(bufs, rows=None):
    n = len(bufs)

    def copies(outs, sems):
        x, y, c = _mesh_pos()
        sends, lands = [], []
        for a in range(n):
            for j, chip in enumerate(_other_chips(x, y)):
                k = 3 * a + j
                slot = _rows_of(outs[a].at[4 * chip[0] + 2 * chip[1] + c], rows)
                sends.append(pltpu.make_async_remote_copy(
                    src_ref=slot, dst_ref=slot, send_sem=sems[0].at[k], recv_sem=sems[1].at[k],
                    device_id=(x, y, 1 - c), device_id_type=MESH))
                theirs = _rows_of(outs[a].at[4 * chip[0] + 2 * chip[1] + 1 - c], rows)
                lands.append(pltpu.make_async_remote_copy(
                    src_ref=theirs, dst_ref=theirs, send_sem=sems[0].at[k], recv_sem=sems[1].at[k],
                    device_id=(x, y, 1 - c), device_id_type=MESH))
        return sends, lands

    def start(ins, outs, sems):
        for cp in copies(outs, sems)[0]:
            cp.start()

    def finish(ins, outs, sems):
        sends, lands = copies(outs, sems)
        for cp in lands:
            cp.wait_recv()
        for cp in sends:
            cp.wait_send()

    return _Comm(list(bufs), [_sds(b.shape, b.dtype) for b in bufs], _sem_pair(3 * n), start, finish,
                 aliases=[(a, a) for a in range(n)])


def _plan_pair(arrays, other_half):
    n = len(arrays)
    per = N_CHIPS if other_half == "chip_major" else 1

    def copies(ins, outs, sems):
        x, y, c = _mesh_pos()
        out = []
        for a in range(n):
            for k in range(per):
                if other_half == "chip_major":
                    src, dst = ins[a].at[k, 1 - c], outs[a].at[k]
                else:
                    src, dst = (ins[a].at[1 - c] if other_half else ins[a]), outs[a]
                out.append(pltpu.make_async_remote_copy(
                    src_ref=src, dst_ref=dst, send_sem=sems[0].at[per * a + k], recv_sem=sems[1].at[per * a + k],
                    device_id=(x, y, 1 - c), device_id_type=MESH))
        return out

    def start(ins, outs, sems):
        for cp in copies(ins, outs, sems):
            cp.start()

    def finish(ins, outs, sems):
        for cp in copies(ins, outs, sems):
            cp.wait()

    if other_half == "chip_major":
        shapes = [_sds((a.shape[0],) + a.shape[2:], a.dtype) for a in arrays]
    else:
        shapes = [_sds(a.shape[1:] if other_half else a.shape, a.dtype) for a in arrays]
    return _Comm(list(arrays), shapes, _sem_pair(per * n), start, finish)


def _plan_chip_exchange(arrays):
    n = len(arrays)

    def copies(ins, outs, sems):
        x, y, c = _mesh_pos()
        sends, lands = [], []
        for a in range(n):
            for j, chip in enumerate(_other_chips(x, y)):
                k = 3 * a + j
                sends.append(pltpu.make_async_remote_copy(
                    src_ref=ins[a].at[2 * chip[0] + chip[1]], dst_ref=outs[a].at[2 * x + y], send_sem=sems[0].at[k],
                    recv_sem=sems[1].at[k], device_id=(*chip, c), device_id_type=MESH))
                slot = outs[a].at[2 * chip[0] + chip[1]]
                lands.append(pltpu.make_async_remote_copy(
                    src_ref=slot, dst_ref=slot, send_sem=sems[0].at[k], recv_sem=sems[1].at[k],
                    device_id=(*chip, c), device_id_type=MESH))
        return sends, lands

    def start(ins, outs, sems):
        for cp in copies(ins, outs, sems)[0]:
            cp.start()

    def finish(ins, outs, sems):
        sends, lands = copies(ins, outs, sems)
        for cp in lands:
            cp.wait_recv()
        for cp in sends:
            cp.wait_send()

    return _Comm(list(arrays), [_sds(a.shape, a.dtype) for a in arrays], _sem_pair(3 * n), start, finish)


SEM_SPEC = pl.BlockSpec(memory_space=pltpu.SEMAPHORE)
N_OTHER = N_CHIPS - 1


def _exchange_copies(s_ref, land_ref, sems):
    x, y, c = _mesh_pos()
    return [pltpu.make_async_remote_copy(
        src_ref=s_ref.at[2 * chip[0] + chip[1]], dst_ref=land_ref.at[2 * x + y], send_sem=sems[j], recv_sem=sems[N_OTHER + j],
        device_id=(*chip, c), device_id_type=MESH) for j, chip in enumerate(_other_chips(x, y))]


def _exchange_start(s, name):
    def body(s_ref, land_ref, *outs):
        sems, token = outs[:2 * N_OTHER], outs[-1]
        for cp in _exchange_copies(s_ref, land_ref, sems):
            cp.start()
        token[...] = jnp.zeros_like(token)

    hbm = pltpu.HBM(s.shape, s.dtype)
    res = pl.pallas_call(
        body, name=name,
        out_shape=(pltpu.SemaphoreType.DMA(()),) * (2 * N_OTHER) + (hbm, hbm, _sds((SUBLANES, LANE), F32)),
        in_specs=(HBM_SPEC, HBM_SPEC),
        out_specs=(SEM_SPEC,) * (2 * N_OTHER) + (HBM_SPEC, HBM_SPEC, pl.BlockSpec(memory_space=pltpu.VMEM)),
        input_output_aliases={0: 2 * N_OTHER, 1: 2 * N_OTHER + 1},
        compiler_params=pltpu.CompilerParams(has_side_effects=pltpu.SideEffectType.DATAFLOW_SIDE_EFFECTING),
    )(pltpu.with_memory_space_constraint(s, pltpu.HBM), pltpu.with_memory_space_constraint(lax.empty(s.shape, s.dtype), pltpu.HBM))
    return res[:2 * N_OTHER], res[2 * N_OTHER], res[2 * N_OTHER + 1], res[-1]


def _exchange_wait(sems, s_thru, land_thru, afters, name):
    def body(s_ref, land_ref, *rest):
        for cp in _exchange_copies(s_ref, land_ref, rest[:2 * N_OTHER]):
            cp.wait_send()
            cp.wait_recv()

    hbm = pltpu.HBM(s_thru.shape, s_thru.dtype)
    return pl.pallas_call(
        body, name=name, out_shape=(hbm, hbm),
        in_specs=(HBM_SPEC, HBM_SPEC) + (SEM_SPEC,) * (2 * N_OTHER) + (pl.BlockSpec(memory_space=pl.ANY),) * len(afters),
        out_specs=(HBM_SPEC, HBM_SPEC), input_output_aliases={0: 0, 1: 1},
        compiler_params=pltpu.CompilerParams(has_side_effects=pltpu.SideEffectType.DATAFLOW_SIDE_EFFECTING),
    )(s_thru, land_thru, *sems, *afters)


def _comm_only(comms, name):
    return _call(lambda: None, (), name=name, grid=(), in_specs=[], out_specs=[], out_shape=[], sem=(), comms=comms)[1]


def _allgather8(arrays, name):
    return _comm_only([_plan_allgather8(arrays)], name)[0]


def _plan_allgather8(arrays):
    n = len(arrays)

    def parts(ins, outs, sems):
        send_sems, recv_sems, local_sems = sems
        x, y, c = _mesh_pos()
        me, sibling = (x, y, c), (x, y, 1 - c)
        chips = _other_chips(x, y)

        def copy(a, k, block, to, src=None):
            dst = outs[a].at[4 * block[0] + 2 * block[1] + block[2]]
            return pltpu.make_async_remote_copy(
                src_ref=dst if src is None else src, dst_ref=dst, send_sem=send_sems.at[7 * a + k],
                recv_sem=recv_sems.at[7 * a + k], device_id=to, device_id_type=MESH)

        mine = [pltpu.make_async_copy(ins[a], outs[a].at[4 * x + 2 * y + c], local_sems.at[a]) for a in range(n)]
        first = []
        for a in range(n):
            first.append(copy(a, 0, me, sibling, src=ins[a]))
            first += [copy(a, 1 + j, me, (*chip, c), src=ins[a]) for j, chip in enumerate(chips)]
        return copy, mine, first, me, sibling, chips, c

    def start(ins, outs, sems):
        _, mine, first, *_ = parts(ins, outs, sems)
        for cp in mine + first:
            cp.start()

    def finish(ins, outs, sems):
        copy, mine, first, me, sibling, chips, c = parts(ins, outs, sems)
        passed = []
        for j, chip in enumerate(chips):
            for a in range(n):
                copy(a, 1 + j, (*chip, c), me).wait_recv()
                fwd = copy(a, 4 + j, (*chip, c), sibling)
                fwd.start()
                passed.append(fwd)
        for a in range(n):
            copy(a, 0, sibling, me).wait_recv()
            for j, chip in enumerate(chips):
                copy(a, 4 + j, (*chip, 1 - c), me).wait_recv()
        for cp in first + passed:
            cp.wait_send()
        for cp in mine:
            cp.wait()

    sems = [pltpu.SemaphoreType.DMA((7 * n,)), pltpu.SemaphoreType.DMA((7 * n,)), pltpu.SemaphoreType.DMA((n,))]
    return _Comm(list(arrays), [_sds((N_DEV,) + a.shape, a.dtype) for a in arrays], sems, start, finish)


def _pair_sum(g, q, core, name, chip_major=False):
    rows, cols = g.shape[2:]
    tr = _row_tile(rows)

    def body(core_ref, g_ref, q_ref, o_ref):
        o_ref[...] = (g_ref[...] + q_ref[...]).astype(BF16)

    blk = pl.BlockSpec((None, tr, cols), lambda k, i, core_ref: (k, i, 0))
    if chip_major:
        own = pl.BlockSpec((None, None, tr, cols), lambda k, i, core_ref: (k, core_ref[0], i, 0))
    else:
        own = pl.BlockSpec((None, None, tr, cols), lambda k, i, core_ref: (core_ref[0], k, i, 0))
    return pl.pallas_call(
        body, name=name,
        grid_spec=pltpu.PrefetchScalarGridSpec(num_scalar_prefetch=1, grid=(N_CHIPS, rows // tr), in_specs=[own, blk], out_specs=blk),
        out_shape=_sds((N_CHIPS, rows, cols), BF16), compiler_params=_params("parallel", "parallel"),
    )(core, g, q)


def _sum_chips(own, landed, chip, name):
    _, rows, cols = own.shape
    tr = _row_tile(rows)

    def body(chip_ref, own_ref, a_ref, b_ref, c_ref, o_ref):
        acc = own_ref[...].astype(F32) + a_ref[...].astype(F32)
        o_ref[...] = (acc + b_ref[...].astype(F32)) + c_ref[...].astype(F32)

    blk = lambda flip: pl.BlockSpec((None, tr, cols), lambda i, chip_ref: (jnp.bitwise_xor(chip_ref[0], flip), i, 0))
    return pl.pallas_call(
        body, name=name,
        grid_spec=pltpu.PrefetchScalarGridSpec(num_scalar_prefetch=1, grid=(rows // tr,), in_specs=[blk(0), blk(1), blk(2), blk(3)],
                                               out_specs=pl.BlockSpec((tr, cols), lambda i, chip_ref: (i, 0))),
        out_shape=_sds((rows, cols), F32), compiler_params=_params("parallel"),
    )(chip, own, landed, landed, landed)


SUBLANES = 8


def _tile_rows(n_elems):
    return -(-n_elems // (SUBLANES * LANE)) * SUBLANES


SMALL_ITEMS = (("b_ada", N_MOD * D_MODEL), ("pre_w_mix", D_MODEL), ("post_w_mix", D_MODEL), ("pre_w_mlp", D_MODEL),
               ("post_w_mlp", D_MODEL), ("attn_out_w", ATT_WIDTH), ("hg_norm_w", HG_HEAD_DIM), ("attn_sinks", ATT_Q_HEADS),
               ("lb_0", HG_WIDTH), ("lb_1", HG_WIDTH))
SMALL_AT = {}
for _name, _size in SMALL_ITEMS:
    SMALL_AT[_name] = (sum(r for _, r in SMALL_AT.values()), _tile_rows(_size))
SMALL_ROWS = sum(r for _, r in SMALL_AT.values())
MOD_ROWS = SMALL_AT["b_ada"][1]
PLAIN_ROWS = SMALL_AT["lb_0"][0] - MOD_ROWS
LB_ROWS = SMALL_AT["lb_0"][1]


def _rows(a, nrows=None):
    flat = a.reshape(-1)
    nrows = _tile_rows(flat.shape[0]) if nrows is None else nrows
    return jnp.pad(flat, (0, nrows * LANE - flat.shape[0])).reshape(nrows, LANE)


def _pack_small(vals):
    vals = dict(vals, lb_0=vals["lb_table"][0], lb_1=vals["lb_table"][1])
    return jnp.concatenate([_rows(vals[name], SMALL_AT[name][1]) for name, _ in SMALL_ITEMS], axis=0)


def _unpack_small(p):
    def item(name, shape):
        first = SMALL_AT[name][0]
        size = shape[0] * shape[1]
        return p[first:first + SMALL_AT[name][1]].reshape(-1)[:size].reshape(shape)

    out = {name: item(name, (1, size)) for name, size in SMALL_ITEMS if not name.startswith("lb_")}
    out["lb_table"] = jnp.concatenate([item("lb_0", (1, HG_WIDTH)), item("lb_1", (1, HG_WIDTH))], axis=0)
    return out


def _pack_partials(dmod, plain, d_lb, loss_row):
    return jnp.concatenate([_rows(dmod, dmod.shape[0] * MOD_ROWS)] + [_rows(g) for g in plain] + [_rows(d_lb), _rows(loss_row)], axis=0)


def _small_update(packs, w, m, v, n_seq):
    mod_end = n_seq * MOD_ROWS
    lb_at = mod_end + PLAIN_ROWS
    t0, t1 = SMALL_AT["lb_0"][0], SMALL_AT["lb_1"][0]

    def body(p_ref, w_ref, m_ref, v_ref, g_ref, dl_ref, nm_ref, nv_ref, loss_ref):
        tot = p_ref[0]
        for d in range(1, N_DEV):
            tot = tot + p_ref[d]
        wv = w_ref[...]
        p1 = _sigmoid(wv[t1:t1 + LB_ROWS] - wv[t0:t0 + LB_ROWS])
        s = tot[lb_at:lb_at + LB_ROWS] * p1 * (1.0 - p1)
        g_bias = tot[0:MOD_ROWS]
        for q in range(1, n_seq):
            g_bias = g_bias + tot[q * MOD_ROWS:(q + 1) * MOD_ROWS]
        g = jnp.concatenate([g_bias, tot[mod_end:lb_at], -s, s], axis=0)
        g_ref[...] = g
        dl_ref[...], nm_ref[...], nv_ref[...] = _adamw_math(g, wv, m_ref[...], v_ref[...])
        loss_ref[...] = tot[lb_at + LB_ROWS:lb_at + LB_ROWS + SUBLANES]

    shp = _sds((SMALL_ROWS, LANE), F32)
    return pl.pallas_call(body, name="small_update", out_shape=[shp] * 4 + [_sds((SUBLANES, LANE), F32)],
                          compiler_params=_params())(packs, w, m, v)


def kernel(x, c, w_ada, b_ada, pre_w_mix, w_in, attn_sinks, attn_out_w, lb_table, hg_norm_w, w_out, post_w_mix, pre_w_mlp, w_up, w_down, post_w_mlp, loss_target, m_w_ada, m_b_ada, m_pre_w_mix, m_w_in, m_attn_sinks, m_attn_out_w, m_lb_table, m_hg_norm_w, m_w_out, m_post_w_mix, m_pre_w_mlp, m_w_up, m_w_down, m_post_w_mlp, v_w_ada, v_b_ada, v_pre_w_mix, v_w_in, v_attn_sinks, v_attn_out_w, v_lb_table, v_hg_norm_w, v_w_out, v_post_w_mix, v_pre_w_mlp, v_w_up, v_w_down, v_post_w_mlp):
    xi, yi, ci = _mesh_pos()
    chip = 2 * xi + yi
    dev = 2 * chip + ci
    bsz, seq, _ = x.shape
    ntok = bsz * seq
    ada_cols = w_ada.shape[2]
    core = jnp.reshape(ci, (1,)).astype(jnp.int32)
    chip_idx = jnp.reshape(chip, (1,)).astype(jnp.int32)
    flat = lambda a: a.reshape(ntok, a.shape[-1])
    unflat = lambda a: a.reshape(bsz, seq, a.shape[-1])
    tables = _rope_tables(seq)

    def row_half(w):
        rows = w.shape[1] // 2
        return lax.dynamic_slice_in_dim(w[0], ci * rows, rows, axis=0).astype(BF16)

    def gather_buffer(w):
        rows, cols = w.shape[1] // 2, w.shape[2]
        own = w[0].astype(BF16).reshape(2, rows, cols)
        return lax.dynamic_update_slice(lax.empty((N_DEV, rows, cols), BF16), own, (2 * chip, 0, 0))

    w_in_t, m_in_t, v_in_t = [jnp.transpose(a[0])[None] for a in (w_in, m_w_in, v_w_in)]
    c_g, in_g = _allgather8([c, row_half(w_in_t)], "gather_first")
    c_all = c_g.reshape(N_DEV * bsz, D_MODEL)
    w_in_full = in_g.reshape(IN_COLS, D_MODEL)

    b_cols = lax.dynamic_slice_in_dim(b_ada, chip * ada_cols, ada_cols, axis=1)
    mod_part = _ada_fwd(c_all, w_ada[0], b_cols)
    half_rows = mod_part.shape[0] // 2
    (mod_g,) = _allgather8([lax.dynamic_slice_in_dim(mod_part, ci * half_rows, half_rows, axis=0)], "gather_mod")
    mod_all = mod_g.reshape(N_CHIPS, 2, half_rows, ada_cols).transpose(1, 2, 0, 3).reshape(N_DEV * bsz, N_MOD * D_MODEL)
    mod = lax.dynamic_slice_in_dim(mod_all, dev * bsz, bsz, axis=0)
    sh1, sc1, g1, sh2, sc2, g2 = [mod[:, i * D_MODEL:(i + 1) * D_MODEL].reshape(bsz, 1, D_MODEL) for i in range(N_MOD)]

    up_rows = w_up.shape[1] // 4
    first_half, second_half = (0, up_rows), (up_rows, up_rows)
    (h1, proj, qh, kh, vh), ((out_g,), (up_g,)) = _in_proj_fused(
        x, pre_w_mix, sc1, sh1, w_in_full, tables,
        comms=[_plan_chip_gather([row_half(w_out)], [gather_buffer(w_out)]),
               _plan_chip_gather([row_half(w_up)], [gather_buffer(w_up)], rows=first_half)])
    (attn_raw, cat, lse), ((up_g,), (out_g,)) = _attn_fwd(
        qh, kh, vh, attn_sinks, attn_out_w,
        comms=[_plan_chip_gather([row_half(w_up)], [up_g], rows=second_half, forward_rows=first_half), _plan_pair_forward([out_g])])
    (o_raw, cat, states), ((down_g,), (up_g,)) = _hgrn_fwd(
        proj, lb_table, hg_norm_w, cat,
        comms=[_plan_chip_gather([row_half(w_down)], [gather_buffer(w_down)]), _plan_pair_forward([up_g], rows=second_half)])
    w_out_full = out_g.reshape(D_MODEL, D_MODEL)
    w_up4 = up_g.reshape(N_CHIPS, D_MODEL, D_MODEL)
    mix, x1, h2 = _out_proj_fused(cat, w_out_full, x, post_w_mix, g1, pre_w_mlp, sc2, sh2)
    big_tm = min(ntok, 2048)
    up_spec = pl.BlockSpec((None, D_MODEL, D_MODEL), lambda i, j: (j, 0, 0))
    r, ((down_g,),) = _mm(flat(h2), w_up4, name="up_proj", out_dtype=BF16, tm=big_tm, tn=D_MODEL, n_out=D_FF, b_spec=up_spec,
                          epi=lambda acc: jnp.maximum(acc, 0.0), comms=[_plan_pair_forward([down_g])])
    w_down_full = down_g.reshape(D_FF, D_MODEL)
    square = lambda t: t * t
    loss_row, dy, dd, dg2, d_post_mlp = _down_proj_fused(unflat(r), w_down_full, x1, post_w_mlp, g2, loss_target)

    dpre = _mm(flat(dd), w_down_full, name="down_bwd", out_dtype=BF16, trans_b=True, tm=big_tm, tn=D_MODEL, extra=(r,),
               epi=lambda acc, rt: acc * (2.0 * rt.astype(F32)))
    half_rows = D_MODEL // 2
    g_down = _mm_tn(r, flat(dd), name="down_wgrad", tk=half_rows, tn=D_MODEL, a_fn=square,
                    out_shape=_sds((2, N_CHIPS, half_rows, D_MODEL), F32),
                    out_spec=pl.BlockSpec((None, None, half_rows, D_MODEL), lambda i, j: (i % 2, i // 2, 0, 0)))
    (dx1, dmix, dsc2, dsh2, dg1, d_pre_mlp, d_post_mix), ((q_down,),) = _up_bwd_fused(
        unflat(dpre), w_up4, dy, x1, mix, pre_w_mlp, sc2, post_w_mix, g1, comms=[_plan_pair([g_down], True)])
    g_up = _mm_tn(flat(h2), dpre, name="up_wgrad", tk=D_MODEL, tn=half_rows,
                  out_shape=_sds((2, N_CHIPS, half_rows, D_MODEL), F32),
                  out_spec=pl.BlockSpec((2, None, half_rows, half_rows), lambda i, j: (0, j // 2, 0, j % 2)))
    s_down = _pair_sum(g_down, q_down, core, "pair_sum_down")

    dcat, ((q_up,),) = _mm(flat(dmix), w_out_full, name="out_bwd", out_dtype=F32, trans_b=True, comms=[_plan_pair([g_up], True)])
    dcat = unflat(dcat)
    s_up = _pair_sum(g_up, q_up, core, "pair_sum_up")
    out_rows = D_MODEL // N_CHIPS
    g_out = _mm_tn(flat(cat), flat(dmix), name="out_wgrad", tk=2 * out_rows, tn=half_rows,
                   out_shape=_sds((2, N_CHIPS, out_rows, half_rows), F32),
                   out_spec=pl.BlockSpec((None, 2, out_rows, half_rows), lambda i, j: (j, i, 0, 0)))
    (dhq, dhf, dhi, dhg, d_lb, d_hg_norm), ((x_down,), (q_out,)) = _hgrn_bwd(
        dcat, proj, o_raw, states, lb_table, hg_norm_w, comms=[_plan_chip_exchange([s_down]), _plan_pair([g_out], True)])
    half_down = _sum_chips(s_down, x_down, chip_idx, "sum_chips_down")
    s_out = _pair_sum(g_out, q_out, core, "pair_sum_out")
    (dproj_a, d_attn_out, d_sinks), ((their_down,), (x_up, x_out)) = _attn_bwd(
        dcat, attn_raw, attn_out_w, qh, kh, vh, lse, attn_sinks, tables,
        comms=[_plan_pair([half_down], False), _plan_chip_exchange([s_up, s_out])])
    half_up = _sum_chips(s_up, x_up, chip_idx, "sum_chips_up")
    half_out = _sum_chips(s_out, x_out, chip_idx, "sum_chips_out")
    dproj = flat(jnp.concatenate([dproj_a, dhq, dhf, dhi, dhg], axis=-1))
    in_rows = IN_COLS // N_CHIPS // 2
    g_in = _mm_tn(dproj, flat(h1), name="in_wgrad", tk=2 * LANE, tn=D_MODEL).reshape(N_CHIPS, 2, in_rows, D_MODEL)
    dh1, ((q_in,), (their_up, their_out)) = _mm(
        dproj, w_in_full, name="in_bwd", out_dtype=F32,
        comms=[_plan_pair([g_in], "chip_major"), _plan_pair([half_up, half_out], False)])
    s_in = _pair_sum(g_in, q_in, core, "pair_sum_in", chip_major=True)
    in_sems, s_in, in_landing, started = _exchange_start(s_in, "exchange_in_start")
    grad_x, dsc1, dsh1, d_pre_mix = _norm1_bwd(unflat(dh1), dx1, x, pre_w_mix + started[0:1, 0:1], sc1)

    dmod = jnp.concatenate([dsh1, dsc1, dg1, dsh2, dsc2, dg2], axis=-1).reshape(bsz, N_MOD * D_MODEL)
    pack = _pack_partials(dmod, [d_pre_mix, d_post_mix, d_pre_mlp, d_post_mlp, d_attn_out, d_hg_norm, d_sinks], d_lb, loss_row)
    ((packs,),) = _comm_only([_plan_allgather8([pack])], "gather_small")
    w_small = dict(b_ada=b_ada, pre_w_mix=pre_w_mix, post_w_mix=post_w_mix, pre_w_mlp=pre_w_mlp, post_w_mlp=post_w_mlp,
                   attn_out_w=attn_out_w, hg_norm_w=hg_norm_w, attn_sinks=attn_sinks, lb_table=lb_table)
    m_small = dict(b_ada=m_b_ada, pre_w_mix=m_pre_w_mix, post_w_mix=m_post_w_mix, pre_w_mlp=m_pre_w_mlp, post_w_mlp=m_post_w_mlp,
                   attn_out_w=m_attn_out_w, hg_norm_w=m_hg_norm_w, attn_sinks=m_attn_sinks, lb_table=m_lb_table)
    v_small = dict(b_ada=v_b_ada, pre_w_mix=v_pre_w_mix, post_w_mix=v_post_w_mix, pre_w_mlp=v_pre_w_mlp, post_w_mlp=v_post_w_mlp,
                   attn_out_w=v_attn_out_w, hg_norm_w=v_hg_norm_w, attn_sinks=v_attn_sinks, lb_table=v_lb_table)
    *small_packed, loss_rows = _small_update(packs, _pack_small(w_small), _pack_small(m_small), _pack_small(v_small), bsz)
    small_out = [_unpack_small(p) for p in small_packed]
    loss = loss_rows[0, 0]

    dmod_all = packs[:, :bsz * MOD_ROWS, :].reshape(N_DEV * bsz, N_MOD * D_MODEL)
    dmod_cols = lax.dynamic_slice_in_dim(dmod_all, chip * ada_cols, ada_cols, axis=1)
    ada_out = _ada_bwd_adamw(c_all, dmod_cols, w_ada[0], m_w_ada[0], v_w_ada[0])

    s_in, x_in = _exchange_wait(in_sems, s_in, in_landing, [grad_x, ada_out[0]], "exchange_in_wait")
    half_in = _sum_chips(s_in, x_in, chip_idx, "sum_chips_in")
    ((their_in,),) = _comm_only([_plan_pair([half_in], False)], "pair_swap_in")
    big = dict(
        w_in=tuple(jnp.transpose(a) for a in _adamw_halves(half_in, their_in, core, w_in_t[0], m_in_t[0], v_in_t[0], axis=0,
                                                           name="adamw_in")),
        w_up=tuple(_adamw_halves(half_up, their_up, core, w_up[0], m_w_up[0], v_w_up[0], axis=0, name="adamw_up")),
        w_out=tuple(_adamw_halves(half_out, their_out, core, w_out[0], m_w_out[0], v_w_out[0], axis=1, name="adamw_out")),
        w_down=tuple(_adamw_halves(half_down, their_down, core, w_down[0], m_w_down[0], v_w_down[0], axis=0, name="adamw_down")),
        w_ada=tuple(ada_out),
    )
    order = ("w_ada", "b_ada", "pre_w_mix", "w_in", "attn_sinks", "attn_out_w", "lb_table", "hg_norm_w", "w_out", "post_w_mix",
             "pre_w_mlp", "w_up", "w_down", "post_w_mlp")
    outs = [loss, grad_x]
    for kind in range(4):
        for nm in order:
            outs.append(big[nm][kind][None] if nm in big else small_out[kind][nm])
    return tuple(outs)
```

```python
import jax
import jax.numpy as jnp
from jax import lax
from jax.experimental import pallas as pl
from jax.experimental.pallas import tpu as pltpu

F32 = jnp.float32
BF16 = jnp.bfloat16

D_MODEL = 1024
ATT_WIDTH = 512
ATT_HEAD_DIM = 64
ATT_Q_HEADS = 8
ATT_KV_HEADS = 2
ATT_GROUP = ATT_Q_HEADS // ATT_KV_HEADS
ATT_KV_COLS = ATT_KV_HEADS * ATT_HEAD_DIM
WINDOW = 128
ROPE_DIM = 16
ROPE_THETA = 500000.0
HG_WIDTH = 512
MIX_WIDTH = ATT_WIDTH + HG_WIDTH
HG_HEAD_DIM = 128
HG_HEADS = 4
HG_CHUNK = 32
IN_COLS = ATT_WIDTH + 2 * ATT_KV_COLS + 4 * HG_WIDTH
ATT_COLS = ATT_WIDTH + 2 * ATT_KV_COLS
D_FF = 4 * D_MODEL
N_MOD = 6
EPS = 1e-6
ATT_SCALE = ATT_HEAD_DIM ** -0.5

ADAM_LR = 0.001
ADAM_B1 = 0.9
ADAM_B2 = 0.999
ADAM_EPS = 1e-08
ADAM_WD = 0.01
ADAM_STEP = 10

N_CHIPS = 4
N_DEV = 8
LANE = 128
VMEM_LIMIT = 48 * 1024 * 1024
VMEM_LIMIT_BIG = 58 * 1024 * 1024
MESH = pl.DeviceIdType.MESH

NT_DIMS = (((1,), (1,)), ((), ()))
TN_DIMS = (((0,), (0,)), ((), ()))


def _sds(shape, dtype):
    return jax.ShapeDtypeStruct(tuple(shape), dtype)


def _params(*sem, vmem_limit=None):
    return pltpu.CompilerParams(dimension_semantics=sem, vmem_limit_bytes=VMEM_LIMIT if vmem_limit is None else vmem_limit)


def _sigmoid(x):
    return 1.0 / (1.0 + jnp.exp(-x))


def _dot(a, b, dims=None):
    a, b = a.astype(BF16), b.astype(BF16)
    if dims is None:
        return jnp.dot(a, b, preferred_element_type=F32)
    return lax.dot_general(a, b, dims, preferred_element_type=F32)


def _rms_fwd(x, w):
    rstd = lax.rsqrt(jnp.mean(x * x, axis=-1, keepdims=True) + EPS)
    xh = x * rstd
    return xh * w, xh, rstd


def _rms_bwd(dy, xh, rstd, w):
    dxh = dy * w
    dx = rstd * (dxh - xh * jnp.mean(dxh * xh, axis=-1, keepdims=True))
    return dx, dy * xh


def _colsum(x):
    return jnp.sum(x, axis=0, keepdims=True)


def _row_tile(rows, cap=256):
    return max(t for t in range(16, cap + 1, 16) if rows % t == 0)


HBM_SPEC = pl.BlockSpec(memory_space=pltpu.HBM)


def _mesh_pos():
    return lax.axis_index("x"), lax.axis_index("y"), lax.axis_index("c")


class _Comm:
    def __init__(self, ins, outs, sems, start, finish, aliases=()):
        self.ins, self.outs, self.sems = list(ins), list(outs), list(sems)
        self.start, self.finish, self.aliases = start, finish, tuple(aliases)


def _call(body, args, *, name, grid, in_specs, out_specs, out_shape, sem, scratch_shapes=(), comms=(), aliases=None,
          vmem_limit=None):
    scratch_shapes = list(scratch_shapes)
    if not comms:
        return pl.pallas_call(body, name=name, grid=grid, in_specs=in_specs, out_specs=out_specs, out_shape=out_shape,
                              input_output_aliases=dict(aliases or {}), scratch_shapes=scratch_shapes,
                              compiler_params=_params(*sem, vmem_limit=vmem_limit))(*args)
    single = not isinstance(out_shape, (list, tuple))
    out_specs_l = [out_specs] if single else list(out_specs)
    out_shape_l = [out_shape] if single else list(out_shape)
    n_in, n_out, n_scr = len(in_specs), len(out_shape_l), len(scratch_shapes)
    n_ci = [len(cm.ins) for cm in comms]
    n_co = [len(cm.outs) for cm in comms]
    n_cs = [len(cm.sems) for cm in comms]
    aliases = dict(aliases or {})
    for k, cm in enumerate(comms):
        for i, o in cm.aliases:
            aliases[n_in + sum(n_ci[:k]) + i] = n_out + sum(n_co[:k]) + o

    def fused(*refs):
        pos = [0]

        def take(n):
            part = refs[pos[0]:pos[0] + n]
            pos[0] += n
            return part

        ins = take(n_in)
        c_ins = [take(n) for n in n_ci]
        outs = take(n_out)
        c_outs = [take(n) for n in n_co]
        scr = take(n_scr)
        c_sems = [take(n) for n in n_cs]
        first, last = True, True
        for d, size in enumerate(grid):
            first = jnp.logical_and(first, pl.program_id(d) == 0)
            last = jnp.logical_and(last, pl.program_id(d) == size - 1)

        def run(which):
            for cm, ci, co, cs in zip(comms, c_ins, c_outs, c_sems):
                getattr(cm, which)(ci, co, cs)

        if grid:
            pl.when(first)(lambda: run("start"))
        else:
            run("start")
        body(*ins, *outs, *scr)
        if grid:
            pl.when(last)(lambda: run("finish"))
        else:
            run("finish")

    res = pl.pallas_call(
        fused, name=name, grid=grid, in_specs=list(in_specs) + [HBM_SPEC] * sum(n_ci),
        out_specs=out_specs_l + [HBM_SPEC] * sum(n_co), out_shape=out_shape_l + [s for cm in comms for s in cm.outs],
        input_output_aliases=aliases, scratch_shapes=scratch_shapes + [s for cm in comms for s in cm.sems],
        compiler_params=_params(*["arbitrary"] * len(grid), vmem_limit=vmem_limit),
    )(*args, *[a for cm in comms for a in cm.ins])
    main = res[:n_out]
    extra, at = [], n_out
    for n in n_co:
        extra.append(list(res[at:at + n]))
        at += n
    return (main[0] if single else list(main)), extra


def _mm(a, b, *, name, out_dtype, trans_b=False, tm=512, tn=None, extra=(), epi=None, b_spec=None, n_out=None, comms=()):
    m_total, k_total = a.shape
    if n_out is None:
        n_out = b.shape[0] if trans_b else b.shape[1]
    tn = n_out if tn is None else tn
    grid = (m_total // tm, n_out // tn)
    dims = NT_DIMS if trans_b else None

    def body(*refs):
        a_ref, b_ref = refs[0], refs[1]
        extra_refs = refs[2:2 + len(extra)]
        o_ref = refs[2 + len(extra)]
        acc = _dot(a_ref[...], b_ref[...], dims)
        if epi is not None:
            acc = epi(acc, *[r[...] for r in extra_refs])
        o_ref[...] = acc.astype(out_dtype)

    if b_spec is None:
        if trans_b:
            b_spec = pl.BlockSpec((tn, k_total), lambda i, j: (j, 0))
        else:
            b_spec = pl.BlockSpec((k_total, tn), lambda i, j: (0, j))
    in_specs = [pl.BlockSpec((tm, k_total), lambda i, j: (i, 0)), b_spec]
    in_specs += [pl.BlockSpec((tm, tn), lambda i, j: (i, j)) for _ in extra]
    return _call(
        body, (a, b, *extra), name=name, grid=grid, in_specs=in_specs,
        out_specs=pl.BlockSpec((tm, tn), lambda i, j: (i, j)),
        out_shape=_sds((m_total, n_out), out_dtype),
        sem=("parallel", "parallel"), comms=comms)


def _mm_tn(a, b, *, name, tk, tn, a_fn=None, out_shape=None, out_spec=None):
    m_total, k_total = a.shape
    n_total = b.shape[1]
    grid = (k_total // tk, n_total // tn)

    def body(a_ref, b_ref, o_ref):
        av = a_ref[...]
        part = _dot(av if a_fn is None else a_fn(av), b_ref[...], TN_DIMS)
        o_ref[...] = part.reshape(o_ref.shape)

    if out_shape is None:
        out_shape = _sds((k_total, n_total), F32)
        out_spec = pl.BlockSpec((tk, tn), lambda i, j: (i, j))
    return pl.pallas_call(
        body, name=name, grid=grid,
        in_specs=[pl.BlockSpec((m_total, tk), lambda i, j: (0, i)), pl.BlockSpec((m_total, tn), lambda i, j: (0, j))],
        out_specs=out_spec, out_shape=out_shape,
        compiler_params=_params("parallel", "parallel"),
    )(a, b)


def _ada_fwd(c_all, w_shard, b_shard):
    nb, ncol = c_all.shape[0], w_shard.shape[1]
    tn = 512

    def body(c_ref, w_ref, b_ref, o_ref):
        c = c_ref[...]
        o_ref[...] = _dot(c * _sigmoid(c), w_ref[...]) + b_ref[...]

    return pl.pallas_call(
        body, name="ada_fwd", grid=(ncol // tn,),
        in_specs=[pl.BlockSpec((nb, D_MODEL), lambda j: (0, 0)), pl.BlockSpec((D_MODEL, tn), lambda j: (0, j)),
                  pl.BlockSpec((1, tn), lambda j: (0, j))],
        out_specs=pl.BlockSpec((nb, tn), lambda j: (0, j)), out_shape=_sds((nb, ncol), F32),
        compiler_params=_params("parallel"),
    )(c_all, w_shard, b_shard)


def _adamw_math(g, w, m, v):
    m = ADAM_B1 * m + (1.0 - ADAM_B1) * g
    v = ADAM_B2 * v + (1.0 - ADAM_B2) * (g * g)
    m_hat = m / (1.0 - ADAM_B1 ** ADAM_STEP)
    v_hat = v / (1.0 - ADAM_B2 ** ADAM_STEP)
    delta = -ADAM_LR * (m_hat / (jnp.sqrt(v_hat) + ADAM_EPS) + ADAM_WD * w)
    return delta, m, v


def _ada_bwd_adamw(c_all, dmod_cols, w, m, v):
    nb, ncol = dmod_cols.shape
    tn = 256

    def body(c_ref, d_ref, w_ref, m_ref, v_ref, g_ref, dl_ref, nm_ref, nv_ref):
        c = c_ref[...]
        g = _dot(c * _sigmoid(c), d_ref[...], TN_DIMS)
        g_ref[...] = g
        dl_ref[...], nm_ref[...], nv_ref[...] = _adamw_math(g, w_ref[...], m_ref[...], v_ref[...])

    col = pl.BlockSpec((D_MODEL, tn), lambda j: (0, j))
    shp = _sds((D_MODEL, ncol), F32)
    return pl.pallas_call(
        body, name="ada_bwd_adamw", grid=(ncol // tn,),
        in_specs=[pl.BlockSpec((nb, D_MODEL), lambda j: (0, 0)), pl.BlockSpec((nb, tn), lambda j: (0, j)), col, col, col],
        out_specs=[col, col, col, col], out_shape=[shp, shp, shp, shp],
        compiler_params=_params("parallel"),
    )(c_all, dmod_cols, w, m, v)


def _adamw_halves(own, theirs, core, w, m, v, *, axis, name):
    r2, c2 = own.shape
    tr = _row_tile(r2)
    nt = r2 // tr

    def body(core_ref, own_ref, their_ref, w_ref, m_ref, v_ref, g_ref, dl_ref, nm_ref, nv_ref):
        g = jnp.where(pl.program_id(0) == core_ref[0], own_ref[...], their_ref[...])
        g_ref[...] = g
        dl_ref[...], nm_ref[...], nv_ref[...] = _adamw_math(g, w_ref[...], m_ref[...], v_ref[...])

    if axis == 0:
        full = pl.BlockSpec((tr, c2), lambda h, i, core_ref: (h * nt + i, 0))
    else:
        full = pl.BlockSpec((tr, c2), lambda h, i, core_ref: (i, h))
    half = pl.BlockSpec((tr, c2), lambda h, i, core_ref: (i, 0))
    shp = _sds(w.shape, F32)
    return pl.pallas_call(
        body, name=name,
        grid_spec=pltpu.PrefetchScalarGridSpec(num_scalar_prefetch=1, grid=(2, nt), in_specs=[half, half, full, full, full],
                                               out_specs=[full] * 4),
        out_shape=[shp] * 4, compiler_params=_params("parallel", "parallel"),
    )(core, own, theirs, w, m, v)


def _tok_spec(tm, width=D_MODEL):
    return pl.BlockSpec((None, tm, width), lambda b, i: (b, i, 0))


def _row_spec(width=D_MODEL):
    return pl.BlockSpec((None, 1, width), lambda b, i: (b, 0, 0))


def _vec_spec(width=D_MODEL):
    return pl.BlockSpec((1, width), lambda b, i: (0, 0))


class _RowsOf:
    def __init__(self, ref, first, count):
        self.ref, self.rows = ref, slice(first, first + count)

    def __getitem__(self, idx):
        return self.ref[self.rows, :]

    def __setitem__(self, idx, value):
        self.ref[self.rows, :] = value


def _mm_rows(a, b, *, name, tm, extra, extra_specs, out_specs, out_shape, epi, pro=None, trans_b=False, b_chunks=1, comms=(),
             parts=1, zero_per_seq=(), zero_once=(), vmem_limit=None):
    bsz, seq, k_total = a.shape
    kc = k_total // b_chunks
    dims = NT_DIMS if trans_b else None
    rows = tm // parts

    def body(*refs):
        a_ref, b_ref = refs[0], refs[1]
        ex, outs = refs[2:2 + len(extra)], refs[2 + len(extra):]
        if zero_per_seq:
            @pl.when(pl.program_id(1) == 0)
            def _():
                for k in zero_per_seq:
                    outs[k][...] = jnp.zeros_like(outs[k])
        if zero_once:
            @pl.when(jnp.logical_and(pl.program_id(0) == 0, pl.program_id(1) == 0))
            def _():
                for k in zero_once:
                    outs[k][...] = jnp.zeros_like(outs[k])

        def part_of(ref, p):
            tiled = len(ref.shape) == 2 and ref.shape[0] == tm
            return _RowsOf(ref, p * rows, rows) if tiled and parts > 1 else ref

        accs = []
        for p in range(parts):
            a_p, ex_p, outs_p = part_of(a_ref, p), [part_of(r, p) for r in ex], [part_of(r, p) for r in outs]
            if b_chunks == 1:
                accs.append(_dot(a_p[...] if pro is None else pro(a_p, ex_p, outs_p), b_ref[...], dims))
            else:
                acc = _dot(a_p[...][:, 0:kc], b_ref[0], NT_DIMS)
                for k in range(1, b_chunks):
                    acc = acc + _dot(a_p[...][:, k * kc:(k + 1) * kc], b_ref[k], NT_DIMS)
                accs.append(acc)
        for p in range(parts):
            epi(accs[p], [part_of(r, p) for r in ex], [part_of(r, p) for r in outs])

    b_spec = pl.BlockSpec(b.shape, lambda bb, i: (0,) * b.ndim)
    return _call(
        body, (a, b, *extra), name=name, grid=(bsz, seq // tm), in_specs=[_tok_spec(tm, k_total), b_spec, *extra_specs],
        out_specs=out_specs, out_shape=out_shape, sem=("arbitrary", "arbitrary"), comms=comms, vmem_limit=vmem_limit)


def _in_proj_fused(x, w, sc, sh, w_in_t, tables, comms=()):
    tm = 512
    bsz, seq, _ = x.shape
    half = ROPE_DIM // 2
    heads_per_slab = LANE // ATT_HEAD_DIM

    def pro(x_ref, ex, outs):
        y, _, _ = _rms_fwd(x_ref[...], ex[0][...])
        h = (y * (1.0 + ex[1][...]) + ex[2][...]).astype(BF16)
        outs[0][...] = h
        return h

    def epi(acc, ex, outs):
        c, u, d = ex[3][...], ex[4][...], ex[5][...]
        _, proj_ref, q_ref, k_ref, v_ref = outs
        proj_ref[...] = acc

        def rope(z):
            return (z * c + pltpu.roll(z, half, 1) * u + pltpu.roll(z, LANE - half, 1) * d).astype(BF16)

        for s in range(ATT_WIDTH // LANE):
            slab = rope(acc[:, s * LANE:(s + 1) * LANE])
            for part in range(heads_per_slab):
                g, hh = divmod(s * heads_per_slab + part, ATT_GROUP)
                piece = slab[:, part * ATT_HEAD_DIM:(part + 1) * ATT_HEAD_DIM]
                for blk in range(tm // WINDOW):
                    q_ref[blk, g, hh * WINDOW:(hh + 1) * WINDOW, :] = piece[blk * WINDOW:(blk + 1) * WINDOW]
        rk = rope(acc[:, ATT_WIDTH:ATT_WIDTH + LANE])
        vv = acc[:, ATT_WIDTH + LANE:ATT_COLS].astype(BF16)
        for g in range(ATT_KV_HEADS):
            k_ref[g] = rk[:, g * ATT_HEAD_DIM:(g + 1) * ATT_HEAD_DIM]
            v_ref[g] = vv[:, g * ATT_HEAD_DIM:(g + 1) * ATT_HEAD_DIM]

    cols = w_in_t.shape[0]
    tab = pl.BlockSpec((tm, LANE), lambda b, i: (i, 0))
    kv_spec = pl.BlockSpec((None, ATT_KV_HEADS, tm, ATT_HEAD_DIM), lambda b, i: (b, 0, i, 0))
    kv_shape = _sds((bsz, ATT_KV_HEADS, seq, ATT_HEAD_DIM), BF16)
    q_spec = pl.BlockSpec((None, tm // WINDOW, ATT_KV_HEADS, GROUP_ROWS, ATT_HEAD_DIM), lambda b, i: (b, i, 0, 0, 0))
    return _mm_rows(x, w_in_t, name="in_proj", tm=tm, extra=(w, sc, sh, *tables),
                    extra_specs=[_vec_spec(), _row_spec(), _row_spec(), tab, tab, tab],
                    out_specs=[_tok_spec(tm), _tok_spec(tm, cols), q_spec, kv_spec, kv_spec],
                    out_shape=[_sds(x.shape, BF16), _sds((bsz, seq, cols), F32),
                               _sds((bsz, seq // WINDOW, ATT_KV_HEADS, GROUP_ROWS, ATT_HEAD_DIM), BF16), kv_shape, kv_shape],
                    pro=pro, epi=epi, trans_b=True, comms=comms)


def _rope_tables(seq):
    half = ROPE_DIM // 2
    inv_freq = ROPE_THETA ** (-jnp.arange(0, ROPE_DIM, 2, dtype=F32) / ROPE_DIM)
    ang = jnp.arange(seq, dtype=F32)[:, None] * inv_freq[None, :]
    cos, sin = jnp.cos(ang), jnp.sin(ang)
    rest = ATT_HEAD_DIM - ROPE_DIM
    ones, zeros, zh = jnp.ones((seq, rest), F32), jnp.zeros((seq, rest), F32), jnp.zeros((seq, half), F32)
    reps = LANE // ATT_HEAD_DIM
    t_cos = jnp.tile(jnp.concatenate([cos, cos, ones], axis=1), (1, reps))
    t_up = jnp.tile(jnp.concatenate([zh, sin, zeros], axis=1), (1, reps))
    t_dn = jnp.tile(jnp.concatenate([-sin, zh, zeros], axis=1), (1, reps))
    return t_cos, t_up, t_dn


GROUP_ROWS = ATT_GROUP * WINDOW


ATT_BPS = 2


def _band_mask(has_prev):
    row = lax.broadcasted_iota(jnp.int32, (GROUP_ROWS, 2 * WINDOW), 0) % WINDOW
    col = lax.broadcasted_iota(jnp.int32, (GROUP_ROWS, 2 * WINDOW), 1)
    prev = jnp.logical_and(jnp.logical_and(col < WINDOW, col > row), has_prev)
    return jnp.logical_or(prev, jnp.logical_and(col >= WINDOW, col - WINDOW <= row))


def _sink_column(sink_ref, g):
    head = lax.broadcasted_iota(jnp.int32, (GROUP_ROWS, 1), 0) // WINDOW
    col = jnp.full((GROUP_ROWS, 1), sink_ref[0, g * ATT_GROUP], F32)
    for hh in range(1, ATT_GROUP):
        col = jnp.where(head == hh, sink_ref[0, g * ATT_GROUP + hh], col)
    return col


def _attn_specs():
    q_spec = pl.BlockSpec((None, ATT_BPS, ATT_KV_HEADS, GROUP_ROWS, ATT_HEAD_DIM), lambda b, i: (b, i, 0, 0, 0))
    kv_cur = pl.BlockSpec((None, ATT_KV_HEADS, ATT_BPS * WINDOW, ATT_HEAD_DIM), lambda b, i: (b, 0, i, 0))
    kv_prev = pl.BlockSpec((None, ATT_KV_HEADS, WINDOW, ATT_HEAD_DIM), lambda b, i: (b, 0, jnp.maximum(ATT_BPS * i - 1, 0), 0))
    return q_spec, kv_cur, kv_prev


def _band(prev_ref, cur_ref, g, blk):
    own = cur_ref[g, blk * WINDOW:(blk + 1) * WINDOW]
    before = prev_ref[g] if blk == 0 else cur_ref[g, (blk - 1) * WINDOW:blk * WINDOW]
    return jnp.concatenate([before, own], axis=0)


def _attn_fwd(qh, kh, vh, sinks, w_norm, comms=()):
    bsz, nblk = qh.shape[0], qh.shape[1]
    seq = nblk * WINDOW
    rows = ATT_BPS * WINDOW
    neg = float(jnp.finfo(jnp.float32).min)

    def body(sink_ref, q_ref, kc_ref, kp_ref, vc_ref, vp_ref, w_ref, raw_ref, an_ref, l_ref):
        for blk in range(ATT_BPS):
            mask = _band_mask(True if blk else pl.program_id(1) > 0)
            for g in range(ATT_KV_HEADS):
                keys, vals = _band(kp_ref, kc_ref, g, blk), _band(vp_ref, vc_ref, g, blk)
                sink = _sink_column(sink_ref, g)
                s = jnp.where(mask, _dot(q_ref[blk, g], keys, NT_DIMS) * ATT_SCALE, neg)
                m = jnp.maximum(jnp.max(s, axis=-1, keepdims=True), sink)
                p = jnp.where(mask, jnp.exp(s - m), 0.0)
                den = jnp.sum(p, axis=-1, keepdims=True) + jnp.exp(sink - m)
                o = _dot(p / den, vals)
                lse = m + jnp.log(den)
                tok = slice(blk * WINDOW, (blk + 1) * WINDOW)
                for hh in range(ATT_GROUP):
                    h = g * ATT_GROUP + hh
                    raw_ref[tok, h * ATT_HEAD_DIM:(h + 1) * ATT_HEAD_DIM] = o[hh * WINDOW:(hh + 1) * WINDOW]
                    l_ref[tok, h:h + 1] = lse[hh * WINDOW:(hh + 1) * WINDOW]
        y, _, _ = _rms_fwd(raw_ref[...], w_ref[...])
        an_ref[...] = y.astype(BF16)

    cur = lambda width: pl.BlockSpec((None, rows, width), lambda b, i: (b, i, 0))
    q_spec, kv_cur, kv_prev = _attn_specs()
    return _call(
        body, (sinks, qh, kh, kh, vh, vh, w_norm), name="attn_fwd", grid=(bsz, nblk // ATT_BPS),
        in_specs=[pl.BlockSpec(memory_space=pltpu.SMEM), q_spec, kv_cur, kv_prev, kv_cur, kv_prev, _vec_spec(ATT_WIDTH)],
        out_specs=[cur(ATT_WIDTH), cur(ATT_WIDTH), cur(ATT_Q_HEADS)],
        out_shape=[_sds((bsz, seq, ATT_WIDTH), F32), _sds((bsz, seq, MIX_WIDTH), BF16), _sds((bsz, seq, ATT_Q_HEADS), F32)],
        sem=("parallel", "parallel"), comms=comms)


HG_Q0 = ATT_COLS // LANE
HG_F0 = HG_Q0 + HG_HEADS
HG_I0 = HG_F0 + HG_HEADS
HG_G0 = HG_I0 + HG_HEADS
HG_TOK = 256
HG_NCH = HG_TOK // HG_CHUNK
HG_HPS = 2


def _block_masks():
    row = lax.broadcasted_iota(jnp.int32, (HG_TOK, HG_TOK), 0)
    col = lax.broadcasted_iota(jnp.int32, (HG_TOK, HG_TOK), 1)
    same = (row // HG_CHUNK) == (col // HG_CHUNK)
    return jnp.logical_and(same, col <= row), jnp.logical_and(same, col >= row)


def _row_in_chunk():
    return lax.broadcasted_iota(jnp.int32, (HG_TOK, LANE), 0) % HG_CHUNK


def _chunk_cumsum(x, reverse=False):
    ric = _row_in_chunk()
    shift = 1
    while shift < HG_CHUNK:
        if reverse:
            x = x + jnp.where(ric < HG_CHUNK - shift, pltpu.roll(x, HG_TOK - shift, 0), 0.0)
        else:
            x = x + jnp.where(ric >= shift, pltpu.roll(x, shift, 0), 0.0)
        shift *= 2
    return x


def _chunk_rows(rows):
    stacked = jnp.concatenate([r[None] for r in rows], axis=0)
    return jnp.broadcast_to(stacked, (HG_NCH, HG_CHUNK, LANE)).reshape(HG_TOK, LANE)


def _chunk_slices(x):
    return [x[j * HG_CHUNK:(j + 1) * HG_CHUNK] for j in range(HG_NCH)]


def _hgrn_common(tbl, hf, hq):
    lb = _sigmoid(tbl[1:2] - tbl[0:1])
    sig = _sigmoid(hf)
    f = lb + (1.0 - lb) * sig
    sq = _sigmoid(hq)
    q, k = hq * sq, 1.0 - f
    b = _chunk_cumsum(jnp.log(f))
    last = [b[(j + 1) * HG_CHUNK - 1:(j + 1) * HG_CHUNK] for j in range(HG_NCH)]
    bl = _chunk_rows(last)
    e_b, e_nb, e_rem = jnp.exp(b), jnp.exp(-b), jnp.exp(bl - b)
    e_last = [jnp.exp(r) for r in last]
    return dict(lb=lb, sig=sig, f=f, sq=sq, q=q, k=k, e_b=e_b, e_nb=e_nb, e_rem=e_rem, e_last=e_last,
                qd=q * e_b, kd=k * e_nb, ku=k * e_rem)


def _hgrn_fwd(proj, lb_table, norm_w, mix_in, comms=()):
    bsz, seq, _ = proj.shape
    nstep = seq // HG_TOK

    def body(tbl_ref, nw_ref, q_ref, f_ref, i_ref, g_ref, mix_ref, o_ref, rec_ref, st_ref, s_scr):
        @pl.when(pl.program_id(2) == 0)
        def _():
            s_scr[...] = jnp.zeros_like(s_scr)

        lower, _ = _block_masks()
        for hp in range(HG_HPS):
            ls = slice(hp * LANE, (hp + 1) * LANE)
            v, hg = i_ref[:, ls], g_ref[:, ls]
            t = _hgrn_common(tbl_ref[:, ls], f_ref[:, ls], q_ref[:, ls])
            a = jnp.where(lower, _dot(t["qd"], t["kd"], NT_DIMS), 0.0)
            o_intra = _dot(a, v)
            v_c, ku_c, qd_c = [_chunk_slices(z.astype(BF16)) for z in (v, t["ku"], t["qd"])]
            updates = [_dot(v_c[j], ku_c[j], TN_DIMS) for j in range(HG_NCH)]
            st = s_scr[hp]
            states = []
            for j in range(HG_NCH):
                states.append(st)
                st = st * t["e_last"][j] + updates[j]
            s_scr[hp] = st
            o = o_intra + jnp.concatenate([_dot(qd_c[j], states[j], NT_DIMS) for j in range(HG_NCH)], axis=0)
            for j in range(HG_NCH):
                st_ref[hp, j] = states[j]
            o_ref[:, ls] = o
            y, _, _ = _rms_fwd(o, nw_ref[...])
            rec_ref[:, ls] = (y * (hg * _sigmoid(hg))).astype(BF16)

    width = HG_HPS * LANE
    slab = lambda first: pl.BlockSpec((None, HG_TOK, width), lambda b, h, t: (b, t, first // HG_HPS + h))
    head_out = pl.BlockSpec((None, HG_TOK, width), lambda b, h, t: (b, t, h))
    mix_out = pl.BlockSpec((None, HG_TOK, width), lambda b, h, t: (b, t, ATT_WIDTH // width + h))
    return _call(
        body, (lb_table, norm_w, proj, proj, proj, proj, mix_in), name="hgrn_fwd", grid=(bsz, HG_HEADS // HG_HPS, nstep),
        in_specs=[pl.BlockSpec((2, width), lambda b, h, t: (0, h)), pl.BlockSpec((1, LANE), lambda b, h, t: (0, 0)),
                  slab(HG_Q0), slab(HG_F0), slab(HG_I0), slab(HG_G0), pl.BlockSpec(memory_space=pl.ANY)],
        out_specs=[head_out, mix_out,
                   pl.BlockSpec((None, HG_HPS, HG_NCH, LANE, LANE), lambda b, h, t: (b, h, t, 0, 0))],
        out_shape=[_sds((bsz, seq, HG_WIDTH), F32), _sds(mix_in.shape, BF16),
                   _sds((bsz, HG_HEADS, seq // HG_CHUNK, LANE, LANE), F32)],
        scratch_shapes=[pltpu.VMEM((HG_HPS, LANE, LANE), F32)],
        sem=("parallel", "parallel", "arbitrary"), comms=comms, aliases={6: 1})


def _out_proj_fused(cat, w_out, x, post_w, g1, pre_w, sc2, sh2):
    tm = 512

    def epi(mix, ex, outs):
        x_ref, pw_ref, g1_ref, w2_ref, sc_ref, sh_ref = ex
        outs[0][...] = mix
        n1, _, _ = _rms_fwd(mix, pw_ref[...])
        x1 = x_ref[...] + g1_ref[...] * n1
        outs[1][...] = x1
        y2, _, _ = _rms_fwd(x1, w2_ref[...])
        outs[2][...] = (y2 * (1.0 + sc_ref[...]) + sh_ref[...]).astype(BF16)

    return _mm_rows(cat, w_out, name="out_proj", tm=tm, extra=(x, post_w, g1, pre_w, sc2, sh2),
                    extra_specs=[_tok_spec(tm), _vec_spec(), _row_spec(), _vec_spec(), _row_spec(), _row_spec()],
                    out_specs=[_tok_spec(tm), _tok_spec(tm), _tok_spec(tm)],
                    out_shape=[_sds(x.shape, F32), _sds(x.shape, F32), _sds(x.shape, BF16)], epi=epi)


def _acc_out(ref, first, value):
    @pl.when(first)
    def _():
        ref[...] = value

    @pl.when(jnp.logical_not(first))
    def _():
        ref[...] += value


def _down_proj_fused(r, w_down, x1, post_w, g2, target):
    tm = 512
    bsz = x1.shape[0]

    def pro(r_ref, ex, outs):
        rv = r_ref[...]
        return rv * rv

    def epi(down, ex, outs):
        x1_ref, w_ref, g2_ref, t_ref = ex
        loss_ref, dy_ref, dd_ref, dg2_ref, dw_ref = outs
        w, g2v = w_ref[...], g2_ref[...]
        n2, dh, rstd = _rms_fwd(down, w)
        err = x1_ref[...] + g2v * n2 - t_ref[...]
        part = (0.5 / D_MODEL) * jnp.sum(jnp.sum(err * err, axis=-1, keepdims=True), axis=0, keepdims=True)
        loss_ref[...] += jnp.broadcast_to(part, (1, LANE))
        dy = err * (1.0 / D_MODEL)
        dy_ref[...] = dy
        dg2_ref[...] += _colsum(dy * n2)
        dd, dw_rows = _rms_bwd(dy * g2v, dh, rstd, w)
        dd_ref[...] = dd.astype(BF16)
        dw_ref[...] += _colsum(dw_rows)

    return _mm_rows(r, w_down, name="down_proj", tm=tm, extra=(x1, post_w, g2, target),
                    extra_specs=[_tok_spec(tm), _vec_spec(), _row_spec(), _tok_spec(tm)],
                    out_specs=[_vec_spec(LANE), _tok_spec(tm), _tok_spec(tm), _row_spec(), _vec_spec()],
                    out_shape=[_sds((1, LANE), F32), _sds(x1.shape, F32), _sds(x1.shape, BF16), _sds((bsz, 1, D_MODEL), F32),
                               _sds((1, D_MODEL), F32)], pro=pro, epi=epi, parts=2, zero_per_seq=(3,), zero_once=(0, 4),
                    vmem_limit=VMEM_LIMIT_BIG)


def _up_bwd_fused(dpre, w_up4, dy, x1, mix, pre_w, sc2, post_w, g1, comms=()):
    tm = 512
    bsz = x1.shape[0]

    def epi(dh2v, ex, outs):
        dy_ref, x1_ref, mix_ref, w2_ref, sc_ref, pw_ref, g1_ref = ex
        dx1_ref, dmix_ref, dsc_ref, dsh_ref, dg1_ref, dw2_ref, dpw_ref = outs
        w2, pw = w2_ref[...], pw_ref[...]
        y2, xh2, rstd2 = _rms_fwd(x1_ref[...], w2)
        dsh_ref[...] += _colsum(dh2v)
        dsc_ref[...] += _colsum(dh2v * y2)
        dx1n, dw_rows = _rms_bwd(dh2v * (1.0 + sc_ref[...]), xh2, rstd2, w2)
        dw2_ref[...] += _colsum(dw_rows)
        dx1 = dy_ref[...] + dx1n
        dx1_ref[...] = dx1
        n1, mh, rstd1 = _rms_fwd(mix_ref[...], pw)
        dg1_ref[...] += _colsum(dx1 * n1)
        dmix, dpw_rows = _rms_bwd(dx1 * g1_ref[...], mh, rstd1, pw)
        dmix_ref[...] = dmix.astype(BF16)
        dpw_ref[...] += _colsum(dpw_rows)

    row_shape = _sds((bsz, 1, D_MODEL), F32)
    vec_shape = _sds((1, D_MODEL), F32)
    return _mm_rows(dpre, w_up4, name="up_bwd", tm=tm, extra=(dy, x1, mix, pre_w, sc2, post_w, g1),
                    extra_specs=[_tok_spec(tm), _tok_spec(tm), _tok_spec(tm), _vec_spec(), _row_spec(), _vec_spec(), _row_spec()],
                    out_specs=[_tok_spec(tm), _tok_spec(tm), _row_spec(), _row_spec(), _row_spec(), _vec_spec(), _vec_spec()],
                    out_shape=[_sds(x1.shape, F32), _sds(x1.shape, BF16), row_shape, row_shape, row_shape, vec_shape, vec_shape],
                    epi=epi, b_chunks=w_up4.shape[0], comms=comms, parts=2, zero_per_seq=(2, 3, 4), zero_once=(5, 6),
                    vmem_limit=VMEM_LIMIT_BIG)


def _norm1_bwd(dh1, dx1, x, pre_w, sc1, tm=512, comms=()):
    bsz, seq, _ = x.shape

    def body(dh_ref, dx1_ref, x_ref, w_ref, sc_ref, gx_ref, dsc_ref, dsh_ref, dw_ref):
        b, i = pl.program_id(0), pl.program_id(1)
        w = w_ref[...]
        dh = dh_ref[...]
        y, xh, rstd = _rms_fwd(x_ref[...], w)
        _acc_out(dsh_ref, i == 0, _colsum(dh))
        _acc_out(dsc_ref, i == 0, _colsum(dh * y))
        dx, dw_rows = _rms_bwd(dh * (1.0 + sc_ref[...]), xh, rstd, w)
        _acc_out(dw_ref, jnp.logical_and(b == 0, i == 0), _colsum(dw_rows))
        gx_ref[...] = dx1_ref[...] + dx

    row_shape = _sds((bsz, 1, D_MODEL), F32)
    return _call(
        body, (dh1, dx1, x, pre_w, sc1), name="norm1_bwd", grid=(bsz, seq // tm),
        in_specs=[_tok_spec(tm), _tok_spec(tm), _tok_spec(tm), _vec_spec(), _row_spec()],
        out_specs=[_tok_spec(tm), _row_spec(), _row_spec(), _vec_spec()],
        out_shape=[_sds(x.shape, F32), row_shape, row_shape, _sds((1, D_MODEL), F32)],
        sem=("arbitrary", "arbitrary"), comms=comms)


def _hgrn_bwd(dcat, proj, o_raw, states, lb_table, norm_w, comms=()):
    bsz, seq, _ = proj.shape
    nstep = seq // HG_TOK
    rec0 = ATT_WIDTH // LANE

    def body(tbl_ref, nw_ref, dr_ref, q_ref, f_ref, i_ref, g_ref, o_ref, st_ref,
             dq_ref, df_ref, di_ref, dg_ref, dlb_ref, dnw_ref, ds_scr):
        h, b, t = pl.program_id(0), pl.program_id(1), pl.program_id(2)

        @pl.when(t == 0)
        def _():
            ds_scr[...] = jnp.zeros_like(ds_scr)

        lower, upper = _block_masks()
        dlb_parts = []
        dnw_acc = jnp.zeros((1, LANE), F32)
        for hp in range(HG_HPS):
            ls = slice(hp * LANE, (hp + 1) * LANE)
            hq, v, hg = q_ref[:, ls], i_ref[:, ls], g_ref[:, ls]
            nw = nw_ref[...]
            c = _hgrn_common(tbl_ref[:, ls], f_ref[:, ls], hq)
            qd, kd, ku = c["qd"], c["kd"], c["ku"]
            y, on, rstd = _rms_fwd(o_ref[:, ls], nw)
            sg = _sigmoid(hg)
            dr = dr_ref[:, ls]
            dg_ref[:, ls] = (dr * y * (sg * (1.0 + hg * (1.0 - sg)))).astype(BF16)
            do, dnw_rows = _rms_bwd(dr * (hg * sg), on, rstd, nw)
            at = jnp.where(upper, _dot(kd, qd, NT_DIMS), 0.0)
            da = jnp.where(lower, _dot(do, v, NT_DIMS), 0.0)
            dat = jnp.where(upper, _dot(v, do, NT_DIMS), 0.0)
            dv = _dot(at, do)
            dqd = _dot(da, kd)
            dkd = _dot(dat, qd)
            do_c, qd_c, v_c, ku_c = [_chunk_slices(z.astype(BF16)) for z in (do, qd, v, ku)]
            outer = [_dot(do_c[j], qd_c[j], TN_DIMS) for j in range(HG_NCH)]
            ds = ds_scr[hp]
            ds_after = [None] * HG_NCH
            for j in reversed(range(HG_NCH)):
                ds_after[j] = ds
                ds = outer[j] + ds * c["e_last"][j]
            ds_scr[hp] = ds
            states = [st_ref[hp, j] for j in range(HG_NCH)]
            dv = dv + jnp.concatenate([_dot(ku_c[j], ds_after[j], NT_DIMS) for j in range(HG_NCH)], axis=0)
            dqd = dqd + jnp.concatenate([_dot(do_c[j], states[j]) for j in range(HG_NCH)], axis=0)
            dku = jnp.concatenate([_dot(v_c[j], ds_after[j]) for j in range(HG_NCH)], axis=0)
            dku_ku = dku * ku
            dbl = [_colsum(states[j] * ds_after[j]) * c["e_last"][j] + _colsum(dku_ku[j * HG_CHUNK:(j + 1) * HG_CHUNK])
                   for j in range(HG_NCH)]
            dk = dkd * c["e_nb"] + dku * c["e_rem"]
            db = dqd * qd - dkd * kd - dku_ku + jnp.where(_row_in_chunk() == HG_CHUNK - 1, _chunk_rows(dbl), 0.0)
            dfv = _chunk_cumsum(db, reverse=True) / c["f"] - dk
            sig, sq = c["sig"], c["sq"]
            df_ref[:, ls] = (dfv * (1.0 - c["lb"]) * sig * (1.0 - sig)).astype(BF16)
            dq_ref[:, ls] = (dqd * c["e_b"] * (sq * (1.0 + hq * (1.0 - sq)))).astype(BF16)
            di_ref[:, ls] = dv.astype(BF16)
            dlb_parts.append(_colsum(dfv * (1.0 - sig)))
            dnw_acc = dnw_acc + _colsum(dnw_rows)
        _acc_out(dlb_ref, jnp.logical_and(b == 0, t == 0), jnp.concatenate(dlb_parts, axis=1))
        _acc_out(dnw_ref, jnp.logical_and(h == 0, jnp.logical_and(b == 0, t == 0)), dnw_acc)

    rev = lambda t: nstep - 1 - t
    width = HG_HPS * LANE
    slab = lambda first: pl.BlockSpec((None, HG_TOK, width), lambda h, b, t: (b, rev(t), first // HG_HPS + h))
    head = pl.BlockSpec((None, HG_TOK, width), lambda h, b, t: (b, rev(t), h))
    grad_shape = _sds((bsz, seq, HG_WIDTH), BF16)
    return _call(
        body, (lb_table, norm_w, dcat, proj, proj, proj, proj, o_raw, states), name="hgrn_bwd",
        grid=(HG_HEADS // HG_HPS, bsz, nstep),
        in_specs=[pl.BlockSpec((2, width), lambda h, b, t: (0, h)), pl.BlockSpec((1, LANE), lambda h, b, t: (0, 0)),
                  slab(rec0), slab(HG_Q0), slab(HG_F0), slab(HG_I0), slab(HG_G0), head,
                  pl.BlockSpec((None, HG_HPS, HG_NCH, LANE, LANE), lambda h, b, t: (b, h, rev(t), 0, 0))],
        out_specs=[head, head, head, head, pl.BlockSpec((1, width), lambda h, b, t: (0, h)),
                   pl.BlockSpec((1, LANE), lambda h, b, t: (0, 0))],
        out_shape=[grad_shape, grad_shape, grad_shape, grad_shape, _sds((1, HG_WIDTH), F32), _sds((1, LANE), F32)],
        scratch_shapes=[pltpu.VMEM((HG_HPS, LANE, LANE), F32)],
        sem=("arbitrary", "arbitrary", "arbitrary"), comms=comms)


def _attn_bwd(dcat, raw, w_norm, qh, kh, vh, lse, sinks, tables, comms=()):
    bsz, nblk = qh.shape[0], qh.shape[1]
    seq = nblk * WINDOW
    nstep = nblk // ATT_BPS
    half = ROPE_DIM // 2

    def body(sink_ref, da_ref, raw_ref, w_ref, q_ref, kc_ref, kp_ref, vc_ref, vp_ref, l_ref, c_ref, u_ref, d_ref,
             o_ref, dw_ref, dsink_ref, carry_k, carry_v):
        b, i = pl.program_id(0), pl.program_id(1)
        first = jnp.logical_and(b == 0, i == 0)

        @pl.when(i == 0)
        def _():
            carry_k[...] = jnp.zeros_like(carry_k)
            carry_v[...] = jnp.zeros_like(carry_v)

        w = w_ref[...]
        _, on, rstd = _rms_fwd(raw_ref[...], w)
        do_step, dw_rows = _rms_bwd(da_ref[...], on, rstd, w)
        _acc_out(dw_ref, first, _colsum(dw_rows))
        lane8 = lax.broadcasted_iota(jnp.int32, (1, ATT_Q_HEADS), 1)
        dsink = jnp.zeros((1, ATT_Q_HEADS), F32)
        from_next_k, from_next_v = carry_k[...], carry_v[...]
        for blk in reversed(range(ATT_BPS)):
            tok = slice(blk * WINDOW, (blk + 1) * WINDOW)
            mask = _band_mask(True if blk else i < nstep - 1)
            raw_v, do_all = raw_ref[tok, :], do_step[tok]
            c, u, d = c_ref[tok, :], u_ref[tok, :], d_ref[tok, :]

            def unrope(g):
                return (g * c + pltpu.roll(g * u, LANE - half, 1) + pltpu.roll(g * d, half, 1)).astype(BF16)

            dq_parts, dk_own, dk_before, dv_own, dv_before = [], [], [], [], []
            for g in range(ATT_KV_HEADS):
                heads = [slice((g * ATT_GROUP + hh) * ATT_HEAD_DIM, (g * ATT_GROUP + hh + 1) * ATT_HEAD_DIM)
                         for hh in range(ATT_GROUP)]
                q = q_ref[blk, g]
                keys, vals = _band(kp_ref, kc_ref, g, blk), _band(vp_ref, vc_ref, g, blk)
                do_g = jnp.concatenate([do_all[:, hs] for hs in heads], axis=0)
                dsum = jnp.concatenate([jnp.sum(do_all[:, hs] * raw_v[:, hs], axis=-1, keepdims=True) for hs in heads], axis=0)
                lse_g = jnp.concatenate([l_ref[tok, g * ATT_GROUP + hh:g * ATT_GROUP + hh + 1] for hh in range(ATT_GROUP)], axis=0)
                p = jnp.where(mask, jnp.exp(_dot(q, keys, NT_DIMS) * ATT_SCALE - lse_g), 0.0)
                sink_part = jnp.exp(_sink_column(sink_ref, g) - lse_g) * dsum
                for hh in range(ATT_GROUP):
                    head_sum = jnp.sum(sink_part[hh * WINDOW:(hh + 1) * WINDOW], axis=0, keepdims=True)
                    dsink = dsink - jnp.where(lane8 == g * ATT_GROUP + hh, head_sum, 0.0)
                ds = p * (_dot(do_g, vals, NT_DIMS) - dsum) * ATT_SCALE
                dq_g = _dot(ds, keys)
                dq_parts += [dq_g[hh * WINDOW:(hh + 1) * WINDOW] for hh in range(ATT_GROUP)]
                dk_g = _dot(ds, q, TN_DIMS)
                dv_g = _dot(p, do_g, TN_DIMS)
                dk_before.append(dk_g[:WINDOW])
                dk_own.append(dk_g[WINDOW:])
                dv_before.append(dv_g[:WINDOW])
                dv_own.append(dv_g[WINDOW:])
            per_slab = LANE // ATT_HEAD_DIM
            for s in range(ATT_WIDTH // LANE):
                slab = jnp.concatenate(dq_parts[s * per_slab:(s + 1) * per_slab], axis=1)
                o_ref[tok, s * LANE:(s + 1) * LANE] = unrope(slab)
            o_ref[tok, ATT_WIDTH:ATT_WIDTH + LANE] = unrope(jnp.concatenate(dk_own, axis=1) + from_next_k)
            o_ref[tok, ATT_WIDTH + LANE:ATT_COLS] = (jnp.concatenate(dv_own, axis=1) + from_next_v).astype(BF16)
            from_next_k, from_next_v = jnp.concatenate(dk_before, axis=1), jnp.concatenate(dv_before, axis=1)
        carry_k[...] = from_next_k
        carry_v[...] = from_next_v
        _acc_out(dsink_ref, first, dsink)

    rows = ATT_BPS * WINDOW
    rev = lambda i: nstep - 1 - i
    cur = lambda width: pl.BlockSpec((None, rows, width), lambda b, i: (b, rev(i), 0))
    q_spec = pl.BlockSpec((None, ATT_BPS, ATT_KV_HEADS, GROUP_ROWS, ATT_HEAD_DIM), lambda b, i: (b, rev(i), 0, 0, 0))
    kv_cur = pl.BlockSpec((None, ATT_KV_HEADS, rows, ATT_HEAD_DIM), lambda b, i: (b, 0, rev(i), 0))
    kv_prev = pl.BlockSpec((None, ATT_KV_HEADS, WINDOW, ATT_HEAD_DIM), lambda b, i: (b, 0, jnp.maximum(ATT_BPS * rev(i) - 1, 0), 0))
    tab = pl.BlockSpec((rows, LANE), lambda b, i: (rev(i), 0))
    return _call(
        body, (sinks, dcat, raw, w_norm, qh, kh, kh, vh, vh, lse, *tables), name="attn_bwd", grid=(bsz, nstep),
        in_specs=[pl.BlockSpec(memory_space=pltpu.SMEM), cur(ATT_WIDTH), cur(ATT_WIDTH), _vec_spec(ATT_WIDTH), q_spec,
                  kv_cur, kv_prev, kv_cur, kv_prev, cur(ATT_Q_HEADS), tab, tab, tab],
        out_specs=[cur(ATT_COLS), _vec_spec(ATT_WIDTH), _vec_spec(ATT_Q_HEADS)],
        out_shape=[_sds((bsz, seq, ATT_COLS), BF16), _sds((1, ATT_WIDTH), F32), _sds((1, ATT_Q_HEADS), F32)],
        scratch_shapes=[pltpu.VMEM((WINDOW, LANE), F32), pltpu.VMEM((WINDOW, LANE), F32)],
        sem=("arbitrary", "arbitrary"), comms=comms)


def _other_chips(x, y):
    return [(1 - x, y), (x, 1 - y), (1 - x, 1 - y)]


def _sem_pair(n):
    return [pltpu.SemaphoreType.DMA((n,)), pltpu.SemaphoreType.DMA((n,))]


def _plan_pair_forward(bufs):
    n = len(bufs)

    def copies(outs, sems):
        x, y, c = _mesh_pos()
        sends, lands = [], []
        for a in range(n):
            for j, chip in enumerate(_other_chips(x, y)):
                k = 3 * a + j
                slot = outs[a].at[4 * chip[0] + 2 * chip[1] + c]
                sends.append(pltpu.make_async_remote_copy(
                    src_ref=slot, dst_ref=slot, send_sem=sems[0].at[k], recv_sem=sems[1].at[k],
                    device_id=(x, y, 1 - c), device_id_type=MESH))
                theirs = outs[a].at[4 * chip[0] + 2 * chip[1] + 1 - c]
                lands.append(pltpu.make_async_remote_copy(
                    src_ref=theirs, dst_ref=theirs, send_sem=sems[0].at[k], recv_sem=sems[1].at[k],
                    device_id=(x, y, 1 - c), device_id_type=MESH))
        return sends, lands

    def start(ins, outs, sems):
        for cp in copies(outs, sems)[0]:
            cp.start()

    def finish(ins, outs, sems):
        sends, lands = copies(outs, sems)
        for cp in lands:
            cp.wait_recv()
        for cp in sends:
            cp.wait_send()

    return _Comm(list(bufs), [_sds(b.shape, b.dtype) for b in bufs], _sem_pair(3 * n), start, finish,
                 aliases=[(a, a) for a in range(n)])


def _plan_pair(arrays, other_half):
    n = len(arrays)
    per = N_CHIPS if other_half == "chip_major" else 1

    def copies(ins, outs, sems):
        x, y, c = _mesh_pos()
        out = []
        for a in range(n):
            for k in range(per):
                if other_half == "chip_major":
                    src, dst = ins[a].at[k, 1 - c], outs[a].at[k]
                else:
                    src, dst = (ins[a].at[1 - c] if other_half else ins[a]), outs[a]
                out.append(pltpu.make_async_remote_copy(
                    src_ref=src, dst_ref=dst, send_sem=sems[0].at[per * a + k], recv_sem=sems[1].at[per * a + k],
                    device_id=(x, y, 1 - c), device_id_type=MESH))
        return out

    def start(ins, outs, sems):
        for cp in copies(ins, outs, sems):
            cp.start()

    def finish(ins, outs, sems):
        for cp in copies(ins, outs, sems):
            cp.wait()

    if other_half == "chip_major":
        shapes = [_sds((a.shape[0],) + a.shape[2:], a.dtype) for a in arrays]
    else:
        shapes = [_sds(a.shape[1:] if other_half else a.shape, a.dtype) for a in arrays]
    return _Comm(list(arrays), shapes, _sem_pair(per * n), start, finish)


def _plan_chip_exchange(arrays):
    n = len(arrays)

    def copies(ins, outs, sems):
        x, y, c = _mesh_pos()
        sends, lands = [], []
        for a in range(n):
            for j, chip in enumerate(_other_chips(x, y)):
                k = 3 * a + j
                sends.append(pltpu.make_async_remote_copy(
                    src_ref=ins[a].at[2 * chip[0] + chip[1]], dst_ref=outs[a].at[2 * x + y], send_sem=sems[0].at[k],
                    recv_sem=sems[1].at[k], device_id=(*chip, c), device_id_type=MESH))
                slot = outs[a].at[2 * chip[0] + chip[1]]
                lands.append(pltpu.make_async_remote_copy(
                    src_ref=slot, dst_ref=slot, send_sem=sems[0].at[k], recv_sem=sems[1].at[k],
                    device_id=(*chip, c), device_id_type=MESH))
        return sends, lands

    def start(ins, outs, sems):
        for cp in copies(ins, outs, sems)[0]:
            cp.start()

    def finish(ins, outs, sems):
        sends, lands = copies(ins, outs, sems)
        for cp in lands:
            cp.wait_recv()
        for cp in sends:
            cp.wait_send()

    return _Comm(list(arrays), [_sds(a.shape, a.dtype) for a in arrays], _sem_pair(3 * n), start, finish)


SEM_SPEC = pl.BlockSpec(memory_space=pltpu.SEMAPHORE)
N_OTHER = N_CHIPS - 1


def _exchange_copies(s_ref, land_ref, sems):
    x, y, c = _mesh_pos()
    return [pltpu.make_async_remote_copy(
        src_ref=s_ref.at[2 * chip[0] + chip[1]], dst_ref=land_ref.at[2 * x + y], send_sem=sems[j], recv_sem=sems[N_OTHER + j],
        device_id=(*chip, c), device_id_type=MESH) for j, chip in enumerate(_other_chips(x, y))]


def _exchange_start(s, name):
    def body(s_ref, land_ref, *outs):
        sems, token = outs[:2 * N_OTHER], outs[-1]
        for cp in _exchange_copies(s_ref, land_ref, sems):
            cp.start()
        token[...] = jnp.zeros_like(token)

    hbm = pltpu.HBM(s.shape, s.dtype)
    res = pl.pallas_call(
        body, name=name,
        out_shape=(pltpu.SemaphoreType.DMA(()),) * (2 * N_OTHER) + (hbm, hbm, _sds((SUBLANES, LANE), F32)),
        in_specs=(HBM_SPEC, HBM_SPEC),
        out_specs=(SEM_SPEC,) * (2 * N_OTHER) + (HBM_SPEC, HBM_SPEC, pl.BlockSpec(memory_space=pltpu.VMEM)),
        input_output_aliases={0: 2 * N_OTHER, 1: 2 * N_OTHER + 1},
        compiler_params=pltpu.CompilerParams(has_side_effects=pltpu.SideEffectType.DATAFLOW_SIDE_EFFECTING),
    )(pltpu.with_memory_space_constraint(s, pltpu.HBM), pltpu.with_memory_space_constraint(lax.empty(s.shape, s.dtype), pltpu.HBM))
    return res[:2 * N_OTHER], res[2 * N_OTHER], res[2 * N_OTHER + 1], res[-1]


def _exchange_wait(sems, s_thru, land_thru, afters, name):
    def body(s_ref, land_ref, *rest):
        for cp in _exchange_copies(s_ref, land_ref, rest[:2 * N_OTHER]):
            cp.wait_send()
            cp.wait_recv()

    hbm = pltpu.HBM(s_thru.shape, s_thru.dtype)
    return pl.pallas_call(
        body, name=name, out_shape=(hbm, hbm),
        in_specs=(HBM_SPEC, HBM_SPEC) + (SEM_SPEC,) * (2 * N_OTHER) + (pl.BlockSpec(memory_space=pl.ANY),) * len(afters),
        out_specs=(HBM_SPEC, HBM_SPEC), input_output_aliases={0: 0, 1: 1},
        compiler_params=pltpu.CompilerParams(has_side_effects=pltpu.SideEffectType.DATAFLOW_SIDE_EFFECTING),
    )(s_thru, land_thru, *sems, *afters)


def _gather_copies(block_ref, buf_ref, sems):
    x, y, c = _mesh_pos()
    return [pltpu.make_async_remote_copy(
        src_ref=block_ref, dst_ref=buf_ref.at[4 * x + 2 * y + c], send_sem=sems[j], recv_sem=sems[N_OTHER + j],
        device_id=(*chip, c), device_id_type=MESH) for j, chip in enumerate(_other_chips(x, y))]


def _gather_start(blocks, bufs, name):
    n = len(blocks)
    per = 2 * N_OTHER

    def body(*refs):
        ins, outs = refs[:2 * n], refs[2 * n:]
        for a in range(n):
            for cp in _gather_copies(ins[a], ins[n + a], outs[a * per:(a + 1) * per]):
                cp.start()
        outs[-1][...] = jnp.zeros_like(outs[-1])

    hbm = [pltpu.HBM(z.shape, z.dtype) for z in list(blocks) + list(bufs)]
    res = pl.pallas_call(
        body, name=name,
        out_shape=(pltpu.SemaphoreType.DMA(()),) * (n * per) + tuple(hbm) + (_sds((SUBLANES, LANE), F32),),
        in_specs=(HBM_SPEC,) * (2 * n),
        out_specs=(SEM_SPEC,) * (n * per) + (HBM_SPEC,) * (2 * n) + (pl.BlockSpec(memory_space=pltpu.VMEM),),
        input_output_aliases={k: n * per + k for k in range(2 * n)},
        compiler_params=pltpu.CompilerParams(has_side_effects=pltpu.SideEffectType.DATAFLOW_SIDE_EFFECTING),
    )(*[pltpu.with_memory_space_constraint(z, pltpu.HBM) for z in list(blocks) + list(bufs)])
    parts = [(res[a * per:(a + 1) * per], res[n * per + a], res[n * per + n + a]) for a in range(n)]
    return parts, res[-1]


def _gather_wait(part, afters, name):
    sems, block, buf = part

    def body(block_ref, buf_ref, *rest):
        for cp in _gather_copies(block_ref, buf_ref, rest[:2 * N_OTHER]):
            cp.wait_send()
            cp.wait_recv()

    return pl.pallas_call(
        body, name=name, out_shape=(pltpu.HBM(block.shape, block.dtype), pltpu.HBM(buf.shape, buf.dtype)),
        in_specs=(HBM_SPEC, HBM_SPEC) + (SEM_SPEC,) * (2 * N_OTHER) + (pl.BlockSpec(memory_space=pl.ANY),) * len(afters),
        out_specs=(HBM_SPEC, HBM_SPEC), input_output_aliases={0: 0, 1: 1},
        compiler_params=pltpu.CompilerParams(has_side_effects=pltpu.SideEffectType.DATAFLOW_SIDE_EFFECTING),
    )(block, buf, *sems, *afters)[1]


def _comm_only(comms, name):
    return _call(lambda: None, (), name=name, grid=(), in_specs=[], out_specs=[], out_shape=[], sem=(), comms=comms)[1]


def _allgather8(arrays, name):
    return _comm_only([_plan_allgather8(arrays)], name)[0]


def _plan_allgather8(arrays):
    n = len(arrays)

    def parts(ins, outs, sems):
        send_sems, recv_sems, local_sems = sems
        x, y, c = _mesh_pos()
        me, sibling = (x, y, c), (x, y, 1 - c)
        chips = _other_chips(x, y)

        def copy(a, k, block, to, src=None):
            dst = outs[a].at[4 * block[0] + 2 * block[1] + block[2]]
            return pltpu.make_async_remote_copy(
                src_ref=dst if src is None else src, dst_ref=dst, send_sem=send_sems.at[7 * a + k],
                recv_sem=recv_sems.at[7 * a + k], device_id=to, device_id_type=MESH)

        mine = [pltpu.make_async_copy(ins[a], outs[a].at[4 * x + 2 * y + c], local_sems.at[a]) for a in range(n)]
        first = []
        for a in range(n):
            first.append(copy(a, 0, me, sibling, src=ins[a]))
            first += [copy(a, 1 + j, me, (*chip, c), src=ins[a]) for j, chip in enumerate(chips)]
        return copy, mine, first, me, sibling, chips, c

    def start(ins, outs, sems):
        _, mine, first, *_ = parts(ins, outs, sems)
        for cp in mine + first:
            cp.start()

    def finish(ins, outs, sems):
        copy, mine, first, me, sibling, chips, c = parts(ins, outs, sems)
        passed = []
        for j, chip in enumerate(chips):
            for a in range(n):
                copy(a, 1 + j, (*chip, c), me).wait_recv()
                fwd = copy(a, 4 + j, (*chip, c), sibling)
                fwd.start()
                passed.append(fwd)
        for a in range(n):
            copy(a, 0, sibling, me).wait_recv()
            for j, chip in enumerate(chips):
                copy(a, 4 + j, (*chip, 1 - c), me).wait_recv()
        for cp in first + passed:
            cp.wait_send()
        for cp in mine:
            cp.wait()

    sems = [pltpu.SemaphoreType.DMA((7 * n,)), pltpu.SemaphoreType.DMA((7 * n,)), pltpu.SemaphoreType.DMA((n,))]
    return _Comm(list(arrays), [_sds((N_DEV,) + a.shape, a.dtype) for a in arrays], sems, start, finish)


def _pair_sum(g, q, core, name, chip_major=False):
    rows, cols = g.shape[2:]
    tr = _row_tile(rows)

    def body(core_ref, g_ref, q_ref, o_ref):
        o_ref[...] = (g_ref[...] + q_ref[...]).astype(BF16)

    blk = pl.BlockSpec((None, tr, cols), lambda k, i, core_ref: (k, i, 0))
    if chip_major:
        own = pl.BlockSpec((None, None, tr, cols), lambda k, i, core_ref: (k, core_ref[0], i, 0))
    else:
        own = pl.BlockSpec((None, None, tr, cols), lambda k, i, core_ref: (core_ref[0], k, i, 0))
    return pl.pallas_call(
        body, name=name,
        grid_spec=pltpu.PrefetchScalarGridSpec(num_scalar_prefetch=1, grid=(N_CHIPS, rows // tr), in_specs=[own, blk], out_specs=blk),
        out_shape=_sds((N_CHIPS, rows, cols), BF16), compiler_params=_params("parallel", "parallel"),
    )(core, g, q)


def _sum_chips(own, landed, chip, name):
    _, rows, cols = own.shape
    tr = _row_tile(rows)

    def body(chip_ref, own_ref, a_ref, b_ref, c_ref, o_ref):
        acc = own_ref[...].astype(F32) + a_ref[...].astype(F32)
        o_ref[...] = (acc + b_ref[...].astype(F32)) + c_ref[...].astype(F32)

    blk = lambda flip: pl.BlockSpec((None, tr, cols), lambda i, chip_ref: (jnp.bitwise_xor(chip_ref[0], flip), i, 0))
    return pl.pallas_call(
        body, name=name,
        grid_spec=pltpu.PrefetchScalarGridSpec(num_scalar_prefetch=1, grid=(rows // tr,), in_specs=[blk(0), blk(1), blk(2), blk(3)],
                                               out_specs=pl.BlockSpec((tr, cols), lambda i, chip_ref: (i, 0))),
        out_shape=_sds((rows, cols), F32), compiler_params=_params("parallel"),
    )(chip, own, landed, landed, landed)


SUBLANES = 8


def _tile_rows(n_elems):
    return -(-n_elems // (SUBLANES * LANE)) * SUBLANES


SMALL_ITEMS = (("b_ada", N_MOD * D_MODEL), ("pre_w_mix", D_MODEL), ("post_w_mix", D_MODEL), ("pre_w_mlp", D_MODEL),
               ("post_w_mlp", D_MODEL), ("attn_out_w", ATT_WIDTH), ("hg_norm_w", HG_HEAD_DIM), ("attn_sinks", ATT_Q_HEADS),
               ("lb_0", HG_WIDTH), ("lb_1", HG_WIDTH))
SMALL_AT = {}
for _name, _size in SMALL_ITEMS:
    SMALL_AT[_name] = (sum(r for _, r in SMALL_AT.values()), _tile_rows(_size))
SMALL_ROWS = sum(r for _, r in SMALL_AT.values())
MOD_ROWS = SMALL_AT["b_ada"][1]
PLAIN_ROWS = SMALL_AT["lb_0"][0] - MOD_ROWS
LB_ROWS = SMALL_AT["lb_0"][1]


def _rows(a, nrows=None):
    flat = a.reshape(-1)
    nrows = _tile_rows(flat.shape[0]) if nrows is None else nrows
    return jnp.pad(flat, (0, nrows * LANE - flat.shape[0])).reshape(nrows, LANE)


def _pack_small(vals):
    vals = dict(vals, lb_0=vals["lb_table"][0], lb_1=vals["lb_table"][1])
    return jnp.concatenate([_rows(vals[name], SMALL_AT[name][1]) for name, _ in SMALL_ITEMS], axis=0)


def _unpack_small(p):
    def item(name, shape):
        first = SMALL_AT[name][0]
        size = shape[0] * shape[1]
        return p[first:first + SMALL_AT[name][1]].reshape(-1)[:size].reshape(shape)

    out = {name: item(name, (1, size)) for name, size in SMALL_ITEMS if not name.startswith("lb_")}
    out["lb_table"] = jnp.concatenate([item("lb_0", (1, HG_WIDTH)), item("lb_1", (1, HG_WIDTH))], axis=0)
    return out


def _pack_partials(dmod, plain, d_lb, loss_row):
    return jnp.concatenate([_rows(dmod, dmod.shape[0] * MOD_ROWS)] + [_rows(g) for g in plain] + [_rows(d_lb), _rows(loss_row)], axis=0)


def _small_update(packs, w, m, v, n_seq):
    mod_end = n_seq * MOD_ROWS
    lb_at = mod_end + PLAIN_ROWS
    t0, t1 = SMALL_AT["lb_0"][0], SMALL_AT["lb_1"][0]

    def body(p_ref, w_ref, m_ref, v_ref, g_ref, dl_ref, nm_ref, nv_ref, loss_ref):
        tot = p_ref[0]
        for d in range(1, N_DEV):
            tot = tot + p_ref[d]
        wv = w_ref[...]
        p1 = _sigmoid(wv[t1:t1 + LB_ROWS] - wv[t0:t0 + LB_ROWS])
        s = tot[lb_at:lb_at + LB_ROWS] * p1 * (1.0 - p1)
        g_bias = tot[0:MOD_ROWS]
        for q in range(1, n_seq):
            g_bias = g_bias + tot[q * MOD_ROWS:(q + 1) * MOD_ROWS]
        g = jnp.concatenate([g_bias, tot[mod_end:lb_at], -s, s], axis=0)
        g_ref[...] = g
        dl_ref[...], nm_ref[...], nv_ref[...] = _adamw_math(g, wv, m_ref[...], v_ref[...])
        loss_ref[...] = tot[lb_at + LB_ROWS:lb_at + LB_ROWS + SUBLANES]

    shp = _sds((SMALL_ROWS, LANE), F32)
    return pl.pallas_call(body, name="small_update", out_shape=[shp] * 4 + [_sds((SUBLANES, LANE), F32)],
                          compiler_params=_params())(packs, w, m, v)


def kernel(x, c, w_ada, b_ada, pre_w_mix, w_in, attn_sinks, attn_out_w, lb_table, hg_norm_w, w_out, post_w_mix, pre_w_mlp, w_up, w_down, post_w_mlp, loss_target, m_w_ada, m_b_ada, m_pre_w_mix, m_w_in, m_attn_sinks, m_attn_out_w, m_lb_table, m_hg_norm_w, m_w_out, m_post_w_mix, m_pre_w_mlp, m_w_up, m_w_down, m_post_w_mlp, v_w_ada, v_b_ada, v_pre_w_mix, v_w_in, v_attn_sinks, v_attn_out_w, v_lb_table, v_hg_norm_w, v_w_out, v_post_w_mix, v_pre_w_mlp, v_w_up, v_w_down, v_post_w_mlp):
    xi, yi, ci = _mesh_pos()
    chip = 2 * xi + yi
    dev = 2 * chip + ci
    bsz, seq, _ = x.shape
    ntok = bsz * seq
    ada_cols = w_ada.shape[2]
    core = jnp.reshape(ci, (1,)).astype(jnp.int32)
    chip_idx = jnp.reshape(chip, (1,)).astype(jnp.int32)
    flat = lambda a: a.reshape(ntok, a.shape[-1])
    unflat = lambda a: a.reshape(bsz, seq, a.shape[-1])
    tables = _rope_tables(seq)

    def row_half(w):
        rows = w.shape[1] // 2
        return lax.dynamic_slice_in_dim(w[0], ci * rows, rows, axis=0).astype(BF16)

    def gather_buffer(w):
        rows, cols = w.shape[1] // 2, w.shape[2]
        own = w[0].astype(BF16).reshape(2, rows, cols)
        return lax.dynamic_update_slice(lax.empty((N_DEV, rows, cols), BF16), own, (2 * chip, 0, 0))

    w_in_t, m_in_t, v_in_t = [jnp.transpose(a[0])[None] for a in (w_in, m_w_in, v_w_in)]
    weights = (w_in_t, w_out, w_up, w_down)
    (in_part, out_part, up_part, down_part), started = _gather_start(
        [row_half(w) for w in weights], [gather_buffer(w) for w in weights], "gather_weights_start")
    (c_g,) = _allgather8([c + started[0:1, 0:1]], "gather_c")
    c_all = c_g.reshape(N_DEV * bsz, D_MODEL)

    b_cols = lax.dynamic_slice_in_dim(b_ada, chip * ada_cols, ada_cols, axis=1)
    mod_part = _ada_fwd(c_all, w_ada[0], b_cols)
    half_rows = mod_part.shape[0] // 2
    (mod_g,) = _allgather8([lax.dynamic_slice_in_dim(mod_part, ci * half_rows, half_rows, axis=0)], "gather_mod")
    mod_all = mod_g.reshape(N_CHIPS, 2, half_rows, ada_cols).transpose(1, 2, 0, 3).reshape(N_DEV * bsz, N_MOD * D_MODEL)
    mod = lax.dynamic_slice_in_dim(mod_all, dev * bsz, bsz, axis=0)
    sh1, sc1, g1, sh2, sc2, g2 = [mod[:, i * D_MODEL:(i + 1) * D_MODEL].reshape(bsz, 1, D_MODEL) for i in range(N_MOD)]

    ((in_g,),) = _comm_only([_plan_pair_forward([_gather_wait(in_part, [mod_g], "gather_in_wait")])], "forward_in")
    w_in_full = in_g.reshape(IN_COLS, D_MODEL)
    h1, proj, qh, kh, vh = _in_proj_fused(x, pre_w_mix, sc1, sh1, w_in_full, tables)
    out_g = _gather_wait(out_part, [proj], "gather_out_wait")
    (attn_raw, cat, lse), ((out_g,),) = _attn_fwd(qh, kh, vh, attn_sinks, attn_out_w, comms=[_plan_pair_forward([out_g])])
    up_g = _gather_wait(up_part, [attn_raw], "gather_up_wait")
    (o_raw, cat, states), ((up_g,),) = _hgrn_fwd(proj, lb_table, hg_norm_w, cat, comms=[_plan_pair_forward([up_g])])
    down_g = _gather_wait(down_part, [o_raw], "gather_down_wait")
    w_out_full = out_g.reshape(D_MODEL, D_MODEL)
    w_up4 = up_g.reshape(N_CHIPS, D_MODEL, D_MODEL)
    mix, x1, h2 = _out_proj_fused(cat, w_out_full, x, post_w_mix, g1, pre_w_mlp, sc2, sh2)
    big_tm = min(ntok, 2048)
    up_spec = pl.BlockSpec((None, D_MODEL, D_MODEL), lambda i, j: (j, 0, 0))
    r, ((down_g,),) = _mm(flat(h2), w_up4, name="up_proj", out_dtype=BF16, tm=big_tm, tn=D_MODEL, n_out=D_FF, b_spec=up_spec,
                          epi=lambda acc: jnp.maximum(acc, 0.0), comms=[_plan_pair_forward([down_g])])
    w_down_full = down_g.reshape(D_FF, D_MODEL)
    square = lambda t: t * t
    loss_row, dy, dd, dg2, d_post_mlp = _down_proj_fused(unflat(r), w_down_full, x1, post_w_mlp, g2, loss_target)

    dpre = _mm(flat(dd), w_down_full, name="down_bwd", out_dtype=BF16, trans_b=True, tm=big_tm, tn=D_MODEL, extra=(r,),
               epi=lambda acc, rt: acc * (2.0 * rt.astype(F32)))
    half_rows = D_MODEL // 2
    g_down = _mm_tn(r, flat(dd), name="down_wgrad", tk=half_rows, tn=D_MODEL, a_fn=square,
                    out_shape=_sds((2, N_CHIPS, half_rows, D_MODEL), F32),
                    out_spec=pl.BlockSpec((None, None, half_rows, D_MODEL), lambda i, j: (i % 2, i // 2, 0, 0)))
    (dx1, dmix, dsc2, dsh2, dg1, d_pre_mlp, d_post_mix), ((q_down,),) = _up_bwd_fused(
        unflat(dpre), w_up4, dy, x1, mix, pre_w_mlp, sc2, post_w_mix, g1, comms=[_plan_pair([g_down], True)])
    g_up = _mm_tn(flat(h2), dpre, name="up_wgrad", tk=D_MODEL, tn=half_rows,
                  out_shape=_sds((2, N_CHIPS, half_rows, D_MODEL), F32),
                  out_spec=pl.BlockSpec((2, None, half_rows, half_rows), lambda i, j: (0, j // 2, 0, j % 2)))
    s_down = _pair_sum(g_down, q_down, core, "pair_sum_down")

    dcat, ((q_up,),) = _mm(flat(dmix), w_out_full, name="out_bwd", out_dtype=F32, trans_b=True, comms=[_plan_pair([g_up], True)])
    dcat = unflat(dcat)
    s_up = _pair_sum(g_up, q_up, core, "pair_sum_up")
    out_rows = D_MODEL // N_CHIPS
    g_out = _mm_tn(flat(cat), flat(dmix), name="out_wgrad", tk=2 * out_rows, tn=half_rows,
                   out_shape=_sds((2, N_CHIPS, out_rows, half_rows), F32),
                   out_spec=pl.BlockSpec((None, 2, out_rows, half_rows), lambda i, j: (j, i, 0, 0)))
    (dhq, dhf, dhi, dhg, d_lb, d_hg_norm), ((x_down,), (q_out,)) = _hgrn_bwd(
        dcat, proj, o_raw, states, lb_table, hg_norm_w, comms=[_plan_chip_exchange([s_down]), _plan_pair([g_out], True)])
    half_down = _sum_chips(s_down, x_down, chip_idx, "sum_chips_down")
    s_out = _pair_sum(g_out, q_out, core, "pair_sum_out")
    (dproj_a, d_attn_out, d_sinks), ((their_down,), (x_up, x_out)) = _attn_bwd(
        dcat, attn_raw, attn_out_w, qh, kh, vh, lse, attn_sinks, tables,
        comms=[_plan_pair([half_down], False), _plan_chip_exchange([s_up, s_out])])
    half_up = _sum_chips(s_up, x_up, chip_idx, "sum_chips_up")
    half_out = _sum_chips(s_out, x_out, chip_idx, "sum_chips_out")
    dproj = flat(jnp.concatenate([dproj_a, dhq, dhf, dhi, dhg], axis=-1))
    in_rows = IN_COLS // N_CHIPS // 2
    g_in = _mm_tn(dproj, flat(h1), name="in_wgrad", tk=2 * LANE, tn=D_MODEL).reshape(N_CHIPS, 2, in_rows, D_MODEL)
    dh1, ((q_in,), (their_up, their_out)) = _mm(
        dproj, w_in_full, name="in_bwd", out_dtype=F32,
        comms=[_plan_pair([g_in], "chip_major"), _plan_pair([half_up, half_out], False)])
    s_in = _pair_sum(g_in, q_in, core, "pair_sum_in", chip_major=True)
    in_sems, s_in, in_landing, started = _exchange_start(s_in, "exchange_in_start")
    grad_x, dsc1, dsh1, d_pre_mix = _norm1_bwd(unflat(dh1), dx1, x, pre_w_mix + started[0:1, 0:1], sc1)

    dmod = jnp.concatenate([dsh1, dsc1, dg1, dsh2, dsc2, dg2], axis=-1).reshape(bsz, N_MOD * D_MODEL)
    pack = _pack_partials(dmod, [d_pre_mix, d_post_mix, d_pre_mlp, d_post_mlp, d_attn_out, d_hg_norm, d_sinks], d_lb, loss_row)
    ((packs,),) = _comm_only([_plan_allgather8([pack])], "gather_small")
    w_small = dict(b_ada=b_ada, pre_w_mix=pre_w_mix, post_w_mix=post_w_mix, pre_w_mlp=pre_w_mlp, post_w_mlp=post_w_mlp,
                   attn_out_w=attn_out_w, hg_norm_w=hg_norm_w, attn_sinks=attn_sinks, lb_table=lb_table)
    m_small = dict(b_ada=m_b_ada, pre_w_mix=m_pre_w_mix, post_w_mix=m_post_w_mix, pre_w_mlp=m_pre_w_mlp, post_w_mlp=m_post_w_mlp,
                   attn_out_w=m_attn_out_w, hg_norm_w=m_hg_norm_w, attn_sinks=m_attn_sinks, lb_table=m_lb_table)
    v_small = dict(b_ada=v_b_ada, pre_w_mix=v_pre_w_mix, post_w_mix=v_post_w_mix, pre_w_mlp=v_pre_w_mlp, post_w_mlp=v_post_w_mlp,
                   attn_out_w=v_attn_out_w, hg_norm_w=v_hg_norm_w, attn_sinks=v_attn_sinks, lb_table=v_lb_table)
    *small_packed, loss_rows = _small_update(packs, _pack_small(w_small), _pack_small(m_small), _pack_small(v_small), bsz)
    small_out = [_unpack_small(p) for p in small_packed]
    loss = loss_rows[0, 0]

    dmod_all = packs[:, :bsz * MOD_ROWS, :].reshape(N_DEV * bsz, N_MOD * D_MODEL)
    dmod_cols = lax.dynamic_slice_in_dim(dmod_all, chip * ada_cols, ada_cols, axis=1)
    ada_out = _ada_bwd_adamw(c_all, dmod_cols, w_ada[0], m_w_ada[0], v_w_ada[0])

    s_in, x_in = _exchange_wait(in_sems, s_in, in_landing, [grad_x, ada_out[0]], "exchange_in_wait")
    half_in = _sum_chips(s_in, x_in, chip_idx, "sum_chips_in")
    ((their_in,),) = _comm_only([_plan_pair([half_in], False)], "pair_swap_in")
    big = dict(
        w_in=tuple(jnp.transpose(a) for a in _adamw_halves(half_in, their_in, core, w_in_t[0], m_in_t[0], v_in_t[0], axis=0,
                                                           name="adamw_in")),
        w_up=tuple(_adamw_halves(half_up, their_up, core, w_up[0], m_w_up[0], v_w_up[0], axis=0, name="adamw_up")),
        w_out=tuple(_adamw_halves(half_out, their_out, core, w_out[0], m_w_out[0], v_w_out[0], axis=1, name="adamw_out")),
        w_down=tuple(_adamw_halves(half_down, their_down, core, w_down[0], m_w_down[0], v_w_down[0], axis=0, name="adamw_down")),
        w_ada=tuple(ada_out),
    )
    order = ("w_ada", "b_ada", "pre_w_mix", "w_in", "attn_sinks", "attn_out_w", "lb_table", "hg_norm_w", "w_out", "post_w_mix",
             "pre_w_mlp", "w_up", "w_down", "post_w_mlp")
    outs = [loss, grad_x]
    for kind in range(4):
        for nm in order:
            outs.append(big[nm][kind][None] if nm in big else small_out[kind][nm])
    return tuple(outs)
```

```python
import jax
import jax.numpy as jnp
from jax import lax
from jax.experimental import pallas as pl
from jax.experimental.pallas import tpu as pltpu

F32 = jnp.float32
BF16 = jnp.bfloat16

D_MODEL = 1024
ATT_WIDTH = 512
ATT_HEAD_DIM = 64
ATT_Q_HEADS = 8
ATT_KV_HEADS = 2
ATT_GROUP = ATT_Q_HEADS // ATT_KV_HEADS
ATT_KV_COLS = ATT_KV_HEADS * ATT_HEAD_DIM
WINDOW = 128
ROPE_DIM = 16
ROPE_THETA = 500000.0
HG_WIDTH = 512
MIX_WIDTH = ATT_WIDTH + HG_WIDTH
HG_HEAD_DIM = 128
HG_HEADS = 4
HG_CHUNK = 32
IN_COLS = ATT_WIDTH + 2 * ATT_KV_COLS + 4 * HG_WIDTH
ATT_COLS = ATT_WIDTH + 2 * ATT_KV_COLS
D_FF = 4 * D_MODEL
N_MOD = 6
EPS = 1e-6
ATT_SCALE = ATT_HEAD_DIM ** -0.5

ADAM_LR = 0.001
ADAM_B1 = 0.9
ADAM_B2 = 0.999
ADAM_EPS = 1e-08
ADAM_WD = 0.01
ADAM_STEP = 10

N_CHIPS = 4
N_DEV = 8
LANE = 128
VMEM_LIMIT = 48 * 1024 * 1024
VMEM_LIMIT_BIG = 58 * 1024 * 1024
MESH = pl.DeviceIdType.MESH

NT_DIMS = (((1,), (1,)), ((), ()))
TN_DIMS = (((0,), (0,)), ((), ()))


def _sds(shape, dtype):
    return jax.ShapeDtypeStruct(tuple(shape), dtype)


def _params(*sem, vmem_limit=None):
    return pltpu.CompilerParams(dimension_semantics=sem, vmem_limit_bytes=VMEM_LIMIT if vmem_limit is None else vmem_limit)


def _sigmoid(x):
    return 1.0 / (1.0 + jnp.exp(-x))


def _dot(a, b, dims=None):
    a, b = a.astype(BF16), b.astype(BF16)
    if dims is None:
        return jnp.dot(a, b, preferred_element_type=F32)
    return lax.dot_general(a, b, dims, preferred_element_type=F32)


def _rms_fwd(x, w):
    rstd = lax.rsqrt(jnp.mean(x * x, axis=-1, keepdims=True) + EPS)
    xh = x * rstd
    return xh * w, xh, rstd


def _rms_bwd(dy, xh, rstd, w):
    dxh = dy * w
    dx = rstd * (dxh - xh * jnp.mean(dxh * xh, axis=-1, keepdims=True))
    return dx, dy * xh


def _colsum(x):
    return jnp.sum(x, axis=0, keepdims=True)


def _row_tile(rows, cap=256):
    return max(t for t in range(16, cap + 1, 16) if rows % t == 0)


HBM_SPEC = pl.BlockSpec(memory_space=pltpu.HBM)


def _mesh_pos():
    return lax.axis_index("x"), lax.axis_index("y"), lax.axis_index("c")


class _Comm:
    def __init__(self, ins, outs, sems, start, finish, aliases=()):
        self.ins, self.outs, self.sems = list(ins), list(outs), list(sems)
        self.start, self.finish, self.aliases = start, finish, tuple(aliases)


def _call(body, args, *, name, grid, in_specs, out_specs, out_shape, sem, scratch_shapes=(), comms=(), aliases=None,
          vmem_limit=None):
    scratch_shapes = list(scratch_shapes)
    if not comms:
        return pl.pallas_call(body, name=name, grid=grid, in_specs=in_specs, out_specs=out_specs, out_shape=out_shape,
                              input_output_aliases=dict(aliases or {}), scratch_shapes=scratch_shapes,
                              compiler_params=_params(*sem, vmem_limit=vmem_limit))(*args)
    single = not isinstance(out_shape, (list, tuple))
    out_specs_l = [out_specs] if single else list(out_specs)
    out_shape_l = [out_shape] if single else list(out_shape)
    n_in, n_out, n_scr = len(in_specs), len(out_shape_l), len(scratch_shapes)
    n_ci = [len(cm.ins) for cm in comms]
    n_co = [len(cm.outs) for cm in comms]
    n_cs = [len(cm.sems) for cm in comms]
    aliases = dict(aliases or {})
    for k, cm in enumerate(comms):
        for i, o in cm.aliases:
            aliases[n_in + sum(n_ci[:k]) + i] = n_out + sum(n_co[:k]) + o

    def fused(*refs):
        pos = [0]

        def take(n):
            part = refs[pos[0]:pos[0] + n]
            pos[0] += n
            return part

        ins = take(n_in)
        c_ins = [take(n) for n in n_ci]
        outs = take(n_out)
        c_outs = [take(n) for n in n_co]
        scr = take(n_scr)
        c_sems = [take(n) for n in n_cs]
        first, last = True, True
        for d, size in enumerate(grid):
            first = jnp.logical_and(first, pl.program_id(d) == 0)
            last = jnp.logical_and(last, pl.program_id(d) == size - 1)

        def run(which):
            for cm, ci, co, cs in zip(comms, c_ins, c_outs, c_sems):
                getattr(cm, which)(ci, co, cs)

        if grid:
            pl.when(first)(lambda: run("start"))
        else:
            run("start")
        body(*ins, *outs, *scr)
        if grid:
            pl.when(last)(lambda: run("finish"))
        else:
            run("finish")

    res = pl.pallas_call(
        fused, name=name, grid=grid, in_specs=list(in_specs) + [HBM_SPEC] * sum(n_ci),
        out_specs=out_specs_l + [HBM_SPEC] * sum(n_co), out_shape=out_shape_l + [s for cm in comms for s in cm.outs],
        input_output_aliases=aliases, scratch_shapes=scratch_shapes + [s for cm in comms for s in cm.sems],
        compiler_params=_params(*["arbitrary"] * len(grid), vmem_limit=vmem_limit),
    )(*args, *[a for cm in comms for a in cm.ins])
    main = res[:n_out]
    extra, at = [], n_out
    for n in n_co:
        extra.append(list(res[at:at + n]))
        at += n
    return (main[0] if single else list(main)), extra


def _mm(a, b, *, name, out_dtype, trans_b=False, tm=512, tn=None, extra=(), epi=None, b_spec=None, n_out=None, comms=()):
    m_total, k_total = a.shape
    if n_out is None:
        n_out = b.shape[0] if trans_b else b.shape[1]
    tn = n_out if tn is None else tn
    grid = (m_total // tm, n_out // tn)
    dims = NT_DIMS if trans_b else None

    def body(*refs):
        a_ref, b_ref = refs[0], refs[1]
        extra_refs = refs[2:2 + len(extra)]
        o_ref = refs[2 + len(extra)]
        acc = _dot(a_ref[...], b_ref[...], dims)
        if epi is not None:
            acc = epi(acc, *[r[...] for r in extra_refs])
        o_ref[...] = acc.astype(out_dtype)

    if b_spec is None:
        if trans_b:
            b_spec = pl.BlockSpec((tn, k_total), lambda i, j: (j, 0))
        else:
            b_spec = pl.BlockSpec((k_total, tn), lambda i, j: (0, j))
    in_specs = [pl.BlockSpec((tm, k_total), lambda i, j: (i, 0)), b_spec]
    in_specs += [pl.BlockSpec((tm, tn), lambda i, j: (i, j)) for _ in extra]
    return _call(
        body, (a, b, *extra), name=name, grid=grid, in_specs=in_specs,
        out_specs=pl.BlockSpec((tm, tn), lambda i, j: (i, j)),
        out_shape=_sds((m_total, n_out), out_dtype),
        sem=("parallel", "parallel"), comms=comms)


def _mm_tn(a, b, *, name, tk, tn, a_fn=None, out_shape=None, out_spec=None):
    m_total, k_total = a.shape
    n_total = b.shape[1]
    grid = (k_total // tk, n_total // tn)

    def body(a_ref, b_ref, o_ref):
        av = a_ref[...]
        part = _dot(av if a_fn is None else a_fn(av), b_ref[...], TN_DIMS)
        o_ref[...] = part.reshape(o_ref.shape)

    if out_shape is None:
        out_shape = _sds((k_total, n_total), F32)
        out_spec = pl.BlockSpec((tk, tn), lambda i, j: (i, j))
    return pl.pallas_call(
        body, name=name, grid=grid,
        in_specs=[pl.BlockSpec((m_total, tk), lambda i, j: (0, i)), pl.BlockSpec((m_total, tn), lambda i, j: (0, j))],
        out_specs=out_spec, out_shape=out_shape,
        compiler_params=_params("parallel", "parallel"),
    )(a, b)


def _ada_fwd(c_all, w_shard, b_shard):
    nb, ncol = c_all.shape[0], w_shard.shape[1]
    tn = 512

    def body(c_ref, w_ref, b_ref, o_ref):
        c = c_ref[...]
        o_ref[...] = _dot(c * _sigmoid(c), w_ref[...]) + b_ref[...]

    return pl.pallas_call(
        body, name="ada_fwd", grid=(ncol // tn,),
        in_specs=[pl.BlockSpec((nb, D_MODEL), lambda j: (0, 0)), pl.BlockSpec((D_MODEL, tn), lambda j: (0, j)),
                  pl.BlockSpec((1, tn), lambda j: (0, j))],
        out_specs=pl.BlockSpec((nb, tn), lambda j: (0, j)), out_shape=_sds((nb, ncol), F32),
        compiler_params=_params("parallel"),
    )(c_all, w_shard, b_shard)


def _adamw_math(g, w, m, v):
    m = ADAM_B1 * m + (1.0 - ADAM_B1) * g
    v = ADAM_B2 * v + (1.0 - ADAM_B2) * (g * g)
    m_hat = m / (1.0 - ADAM_B1 ** ADAM_STEP)
    v_hat = v / (1.0 - ADAM_B2 ** ADAM_STEP)
    delta = -ADAM_LR * (m_hat / (jnp.sqrt(v_hat) + ADAM_EPS) + ADAM_WD * w)
    return delta, m, v


def _ada_bwd_adamw(c_all, dmod_cols, w, m, v):
    nb, ncol = dmod_cols.shape
    tn = 256

    def body(c_ref, d_ref, w_ref, m_ref, v_ref, g_ref, dl_ref, nm_ref, nv_ref):
        c = c_ref[...]
        g = _dot(c * _sigmoid(c), d_ref[...], TN_DIMS)
        g_ref[...] = g
        dl_ref[...], nm_ref[...], nv_ref[...] = _adamw_math(g, w_ref[...], m_ref[...], v_ref[...])

    col = pl.BlockSpec((D_MODEL, tn), lambda j: (0, j))
    shp = _sds((D_MODEL, ncol), F32)
    return pl.pallas_call(
        body, name="ada_bwd_adamw", grid=(ncol // tn,),
        in_specs=[pl.BlockSpec((nb, D_MODEL), lambda j: (0, 0)), pl.BlockSpec((nb, tn), lambda j: (0, j)), col, col, col],
        out_specs=[col, col, col, col], out_shape=[shp, shp, shp, shp],
        compiler_params=_params("parallel"),
    )(c_all, dmod_cols, w, m, v)


def _adamw_halves(own, theirs, core, w, m, v, *, axis, name):
    r2, c2 = own.shape
    tr = _row_tile(r2)
    nt = r2 // tr

    def body(core_ref, own_ref, their_ref, w_ref, m_ref, v_ref, g_ref, dl_ref, nm_ref, nv_ref):
        g = jnp.where(pl.program_id(0) == core_ref[0], own_ref[...], their_ref[...])
        g_ref[...] = g
        dl_ref[...], nm_ref[...], nv_ref[...] = _adamw_math(g, w_ref[...], m_ref[...], v_ref[...])

    if axis == 0:
        full = pl.BlockSpec((tr, c2), lambda h, i, core_ref: (h * nt + i, 0))
    else:
        full = pl.BlockSpec((tr, c2), lambda h, i, core_ref: (i, h))
    half = pl.BlockSpec((tr, c2), lambda h, i, core_ref: (i, 0))
    shp = _sds(w.shape, F32)
    return pl.pallas_call(
        body, name=name,
        grid_spec=pltpu.PrefetchScalarGridSpec(num_scalar_prefetch=1, grid=(2, nt), in_specs=[half, half, full, full, full],
                                               out_specs=[full] * 4),
        out_shape=[shp] * 4, compiler_params=_params("parallel", "parallel"),
    )(core, own, theirs, w, m, v)


def _tok_spec(tm, width=D_MODEL):
    return pl.BlockSpec((None, tm, width), lambda b, i: (b, i, 0))


def _row_spec(width=D_MODEL):
    return pl.BlockSpec((None, 1, width), lambda b, i: (b, 0, 0))


def _vec_spec(width=D_MODEL):
    return pl.BlockSpec((1, width), lambda b, i: (0, 0))


class _RowsOf:
    def __init__(self, ref, first, count):
        self.ref, self.rows = ref, slice(first, first + count)

    def __getitem__(self, idx):
        return self.ref[self.rows, :]

    def __setitem__(self, idx, value):
        self.ref[self.rows, :] = value


def _mm_rows(a, b, *, name, tm, extra, extra_specs, out_specs, out_shape, epi, pro=None, trans_b=False, b_chunks=1, comms=(),
             parts=1, zero_per_seq=(), zero_once=(), vmem_limit=None):
    bsz, seq, k_total = a.shape
    kc = k_total // b_chunks
    dims = NT_DIMS if trans_b else None
    rows = tm // parts

    def body(*refs):
        a_ref, b_ref = refs[0], refs[1]
        ex, outs = refs[2:2 + len(extra)], refs[2 + len(extra):]
        if zero_per_seq:
            @pl.when(pl.program_id(1) == 0)
            def _():
                for k in zero_per_seq:
                    outs[k][...] = jnp.zeros_like(outs[k])
        if zero_once:
            @pl.when(jnp.logical_and(pl.program_id(0) == 0, pl.program_id(1) == 0))
            def _():
                for k in zero_once:
                    outs[k][...] = jnp.zeros_like(outs[k])

        def part_of(ref, p):
            tiled = len(ref.shape) == 2 and ref.shape[0] == tm
            return _RowsOf(ref, p * rows, rows) if tiled and parts > 1 else ref

        accs = []
        for p in range(parts):
            a_p, ex_p, outs_p = part_of(a_ref, p), [part_of(r, p) for r in ex], [part_of(r, p) for r in outs]
            if b_chunks == 1:
                accs.append(_dot(a_p[...] if pro is None else pro(a_p, ex_p, outs_p), b_ref[...], dims))
            else:
                acc = _dot(a_p[...][:, 0:kc], b_ref[0], NT_DIMS)
                for k in range(1, b_chunks):
                    acc = acc + _dot(a_p[...][:, k * kc:(k + 1) * kc], b_ref[k], NT_DIMS)
                accs.append(acc)
        for p in range(parts):
            epi(accs[p], [part_of(r, p) for r in ex], [part_of(r, p) for r in outs])

    b_spec = pl.BlockSpec(b.shape, lambda bb, i: (0,) * b.ndim)
    return _call(
        body, (a, b, *extra), name=name, grid=(bsz, seq // tm), in_specs=[_tok_spec(tm, k_total), b_spec, *extra_specs],
        out_specs=out_specs, out_shape=out_shape, sem=("arbitrary", "arbitrary"), comms=comms, vmem_limit=vmem_limit)


def _in_proj_fused(x, w, sc, sh, w_in_t, tables, comms=()):
    tm = 512
    bsz, seq, _ = x.shape
    half = ROPE_DIM // 2
    heads_per_slab = LANE // ATT_HEAD_DIM

    def pro(x_ref, ex, outs):
        y, _, _ = _rms_fwd(x_ref[...], ex[0][...])
        h = (y * (1.0 + ex[1][...]) + ex[2][...]).astype(BF16)
        outs[0][...] = h
        return h

    def epi(acc, ex, outs):
        c, u, d = ex[3][...], ex[4][...], ex[5][...]
        _, proj_ref, q_ref, k_ref, v_ref = outs
        proj_ref[...] = acc

        def rope(z):
            return (z * c + pltpu.roll(z, half, 1) * u + pltpu.roll(z, LANE - half, 1) * d).astype(BF16)

        for s in range(ATT_WIDTH // LANE):
            slab = rope(acc[:, s * LANE:(s + 1) * LANE])
            for part in range(heads_per_slab):
                g, hh = divmod(s * heads_per_slab + part, ATT_GROUP)
                piece = slab[:, part * ATT_HEAD_DIM:(part + 1) * ATT_HEAD_DIM]
                for blk in range(tm // WINDOW):
                    q_ref[blk, g, hh * WINDOW:(hh + 1) * WINDOW, :] = piece[blk * WINDOW:(blk + 1) * WINDOW]
        rk = rope(acc[:, ATT_WIDTH:ATT_WIDTH + LANE])
        vv = acc[:, ATT_WIDTH + LANE:ATT_COLS].astype(BF16)
        for g in range(ATT_KV_HEADS):
            k_ref[g] = rk[:, g * ATT_HEAD_DIM:(g + 1) * ATT_HEAD_DIM]
            v_ref[g] = vv[:, g * ATT_HEAD_DIM:(g + 1) * ATT_HEAD_DIM]

    cols = w_in_t.shape[0]
    tab = pl.BlockSpec((tm, LANE), lambda b, i: (i, 0))
    kv_spec = pl.BlockSpec((None, ATT_KV_HEADS, tm, ATT_HEAD_DIM), lambda b, i: (b, 0, i, 0))
    kv_shape = _sds((bsz, ATT_KV_HEADS, seq, ATT_HEAD_DIM), BF16)
    q_spec = pl.BlockSpec((None, tm // WINDOW, ATT_KV_HEADS, GROUP_ROWS, ATT_HEAD_DIM), lambda b, i: (b, i, 0, 0, 0))
    return _mm_rows(x, w_in_t, name="in_proj", tm=tm, extra=(w, sc, sh, *tables),
                    extra_specs=[_vec_spec(), _row_spec(), _row_spec(), tab, tab, tab],
                    out_specs=[_tok_spec(tm), _tok_spec(tm, cols), q_spec, kv_spec, kv_spec],
                    out_shape=[_sds(x.shape, BF16), _sds((bsz, seq, cols), F32),
                               _sds((bsz, seq // WINDOW, ATT_KV_HEADS, GROUP_ROWS, ATT_HEAD_DIM), BF16), kv_shape, kv_shape],
                    pro=pro, epi=epi, trans_b=True, comms=comms)


def _rope_tables(seq):
    half = ROPE_DIM // 2
    inv_freq = ROPE_THETA ** (-jnp.arange(0, ROPE_DIM, 2, dtype=F32) / ROPE_DIM)
    ang = jnp.arange(seq, dtype=F32)[:, None] * inv_freq[None, :]
    cos, sin = jnp.cos(ang), jnp.sin(ang)
    rest = ATT_HEAD_DIM - ROPE_DIM
    ones, zeros, zh = jnp.ones((seq, rest), F32), jnp.zeros((seq, rest), F32), jnp.zeros((seq, half), F32)
    reps = LANE // ATT_HEAD_DIM
    t_cos = jnp.tile(jnp.concatenate([cos, cos, ones], axis=1), (1, reps))
    t_up = jnp.tile(jnp.concatenate([zh, sin, zeros], axis=1), (1, reps))
    t_dn = jnp.tile(jnp.concatenate([-sin, zh, zeros], axis=1), (1, reps))
    return t_cos, t_up, t_dn


GROUP_ROWS = ATT_GROUP * WINDOW


ATT_BPS = 2


def _band_mask(has_prev):
    row = lax.broadcasted_iota(jnp.int32, (GROUP_ROWS, 2 * WINDOW), 0) % WINDOW
    col = lax.broadcasted_iota(jnp.int32, (GROUP_ROWS, 2 * WINDOW), 1)
    prev = jnp.logical_and(jnp.logical_and(col < WINDOW, col > row), has_prev)
    return jnp.logical_or(prev, jnp.logical_and(col >= WINDOW, col - WINDOW <= row))


def _sink_column(sink_ref, g):
    head = lax.broadcasted_iota(jnp.int32, (GROUP_ROWS, 1), 0) // WINDOW
    col = jnp.full((GROUP_ROWS, 1), sink_ref[0, g * ATT_GROUP], F32)
    for hh in range(1, ATT_GROUP):
        col = jnp.where(head == hh, sink_ref[0, g * ATT_GROUP + hh], col)
    return col


def _attn_specs():
    q_spec = pl.BlockSpec((None, ATT_BPS, ATT_KV_HEADS, GROUP_ROWS, ATT_HEAD_DIM), lambda b, i: (b, i, 0, 0, 0))
    kv_cur = pl.BlockSpec((None, ATT_KV_HEADS, ATT_BPS * WINDOW, ATT_HEAD_DIM), lambda b, i: (b, 0, i, 0))
    kv_prev = pl.BlockSpec((None, ATT_KV_HEADS, WINDOW, ATT_HEAD_DIM), lambda b, i: (b, 0, jnp.maximum(ATT_BPS * i - 1, 0), 0))
    return q_spec, kv_cur, kv_prev


def _band(prev_ref, cur_ref, g, blk):
    own = cur_ref[g, blk * WINDOW:(blk + 1) * WINDOW]
    before = prev_ref[g] if blk == 0 else cur_ref[g, (blk - 1) * WINDOW:blk * WINDOW]
    return jnp.concatenate([before, own], axis=0)


def _attn_fwd(qh, kh, vh, sinks, w_norm, comms=()):
    bsz, nblk = qh.shape[0], qh.shape[1]
    seq = nblk * WINDOW
    rows = ATT_BPS * WINDOW
    neg = float(jnp.finfo(jnp.float32).min)

    def body(sink_ref, q_ref, kc_ref, kp_ref, vc_ref, vp_ref, w_ref, raw_ref, an_ref, l_ref):
        for blk in range(ATT_BPS):
            mask = _band_mask(True if blk else pl.program_id(1) > 0)
            for g in range(ATT_KV_HEADS):
                keys, vals = _band(kp_ref, kc_ref, g, blk), _band(vp_ref, vc_ref, g, blk)
                sink = _sink_column(sink_ref, g)
                s = jnp.where(mask, _dot(q_ref[blk, g], keys, NT_DIMS) * ATT_SCALE, neg)
                m = jnp.maximum(jnp.max(s, axis=-1, keepdims=True), sink)
                p = jnp.where(mask, jnp.exp(s - m), 0.0)
                den = jnp.sum(p, axis=-1, keepdims=True) + jnp.exp(sink - m)
                o = _dot(p / den, vals)
                lse = m + jnp.log(den)
                tok = slice(blk * WINDOW, (blk + 1) * WINDOW)
                for hh in range(ATT_GROUP):
                    h = g * ATT_GROUP + hh
                    raw_ref[tok, h * ATT_HEAD_DIM:(h + 1) * ATT_HEAD_DIM] = o[hh * WINDOW:(hh + 1) * WINDOW]
                    l_ref[tok, h:h + 1] = lse[hh * WINDOW:(hh + 1) * WINDOW]
        y, _, _ = _rms_fwd(raw_ref[...], w_ref[...])
        an_ref[...] = y.astype(BF16)

    cur = lambda width: pl.BlockSpec((None, rows, width), lambda b, i: (b, i, 0))
    q_spec, kv_cur, kv_prev = _attn_specs()
    return _call(
        body, (sinks, qh, kh, kh, vh, vh, w_norm), name="attn_fwd", grid=(bsz, nblk // ATT_BPS),
        in_specs=[pl.BlockSpec(memory_space=pltpu.SMEM), q_spec, kv_cur, kv_prev, kv_cur, kv_prev, _vec_spec(ATT_WIDTH)],
        out_specs=[cur(ATT_WIDTH), cur(ATT_WIDTH), cur(ATT_Q_HEADS)],
        out_shape=[_sds((bsz, seq, ATT_WIDTH), F32), _sds((bsz, seq, MIX_WIDTH), BF16), _sds((bsz, seq, ATT_Q_HEADS), F32)],
        sem=("parallel", "parallel"), comms=comms)


HG_Q0 = ATT_COLS // LANE
HG_F0 = HG_Q0 + HG_HEADS
HG_I0 = HG_F0 + HG_HEADS
HG_G0 = HG_I0 + HG_HEADS
HG_TOK = 256
HG_NCH = HG_TOK // HG_CHUNK
HG_HPS = 2


def _block_masks():
    row = lax.broadcasted_iota(jnp.int32, (HG_TOK, HG_TOK), 0)
    col = lax.broadcasted_iota(jnp.int32, (HG_TOK, HG_TOK), 1)
    same = (row // HG_CHUNK) == (col // HG_CHUNK)
    return jnp.logical_and(same, col <= row), jnp.logical_and(same, col >= row)


def _row_in_chunk():
    return lax.broadcasted_iota(jnp.int32, (HG_TOK, LANE), 0) % HG_CHUNK


def _chunk_cumsum(x, reverse=False):
    ric = _row_in_chunk()
    shift = 1
    while shift < HG_CHUNK:
        if reverse:
            x = x + jnp.where(ric < HG_CHUNK - shift, pltpu.roll(x, HG_TOK - shift, 0), 0.0)
        else:
            x = x + jnp.where(ric >= shift, pltpu.roll(x, shift, 0), 0.0)
        shift *= 2
    return x


def _chunk_rows(rows):
    stacked = jnp.concatenate([r[None] for r in rows], axis=0)
    return jnp.broadcast_to(stacked, (HG_NCH, HG_CHUNK, LANE)).reshape(HG_TOK, LANE)


def _chunk_slices(x):
    return [x[j * HG_CHUNK:(j + 1) * HG_CHUNK] for j in range(HG_NCH)]


def _hgrn_common(tbl, hf, hq):
    lb = _sigmoid(tbl[1:2] - tbl[0:1])
    sig = _sigmoid(hf)
    f = lb + (1.0 - lb) * sig
    sq = _sigmoid(hq)
    q, k = hq * sq, 1.0 - f
    b = _chunk_cumsum(jnp.log(f))
    last = [b[(j + 1) * HG_CHUNK - 1:(j + 1) * HG_CHUNK] for j in range(HG_NCH)]
    bl = _chunk_rows(last)
    e_b, e_nb, e_rem = jnp.exp(b), jnp.exp(-b), jnp.exp(bl - b)
    e_last = [jnp.exp(r) for r in last]
    return dict(lb=lb, sig=sig, f=f, sq=sq, q=q, k=k, e_b=e_b, e_nb=e_nb, e_rem=e_rem, e_last=e_last,
                qd=q * e_b, kd=k * e_nb, ku=k * e_rem)


def _hgrn_fwd(proj, lb_table, norm_w, mix_in, comms=()):
    bsz, seq, _ = proj.shape
    nstep = seq // HG_TOK

    def body(tbl_ref, nw_ref, q_ref, f_ref, i_ref, g_ref, mix_ref, o_ref, rec_ref, st_ref, s_scr):
        @pl.when(pl.program_id(2) == 0)
        def _():
            s_scr[...] = jnp.zeros_like(s_scr)

        lower, _ = _block_masks()
        for hp in range(HG_HPS):
            ls = slice(hp * LANE, (hp + 1) * LANE)
            v, hg = i_ref[:, ls], g_ref[:, ls]
            t = _hgrn_common(tbl_ref[:, ls], f_ref[:, ls], q_ref[:, ls])
            a = jnp.where(lower, _dot(t["qd"], t["kd"], NT_DIMS), 0.0)
            o_intra = _dot(a, v)
            v_c, ku_c, qd_c = [_chunk_slices(z.astype(BF16)) for z in (v, t["ku"], t["qd"])]
            updates = [_dot(v_c[j], ku_c[j], TN_DIMS) for j in range(HG_NCH)]
            st = s_scr[hp]
            states = []
            for j in range(HG_NCH):
                states.append(st)
                st = st * t["e_last"][j] + updates[j]
            s_scr[hp] = st
            o = o_intra + jnp.concatenate([_dot(qd_c[j], states[j], NT_DIMS) for j in range(HG_NCH)], axis=0)
            for j in range(HG_NCH):
                st_ref[hp, j] = states[j]
            o_ref[:, ls] = o
            y, _, _ = _rms_fwd(o, nw_ref[...])
            rec_ref[:, ls] = (y * (hg * _sigmoid(hg))).astype(BF16)

    width = HG_HPS * LANE
    slab = lambda first: pl.BlockSpec((None, HG_TOK, width), lambda b, h, t: (b, t, first // HG_HPS + h))
    head_out = pl.BlockSpec((None, HG_TOK, width), lambda b, h, t: (b, t, h))
    mix_out = pl.BlockSpec((None, HG_TOK, width), lambda b, h, t: (b, t, ATT_WIDTH // width + h))
    return _call(
        body, (lb_table, norm_w, proj, proj, proj, proj, mix_in), name="hgrn_fwd", grid=(bsz, HG_HEADS // HG_HPS, nstep),
        in_specs=[pl.BlockSpec((2, width), lambda b, h, t: (0, h)), pl.BlockSpec((1, LANE), lambda b, h, t: (0, 0)),
                  slab(HG_Q0), slab(HG_F0), slab(HG_I0), slab(HG_G0), pl.BlockSpec(memory_space=pl.ANY)],
        out_specs=[head_out, mix_out,
                   pl.BlockSpec((None, HG_HPS, HG_NCH, LANE, LANE), lambda b, h, t: (b, h, t, 0, 0))],
        out_shape=[_sds((bsz, seq, HG_WIDTH), F32), _sds(mix_in.shape, BF16),
                   _sds((bsz, HG_HEADS, seq // HG_CHUNK, LANE, LANE), F32)],
        scratch_shapes=[pltpu.VMEM((HG_HPS, LANE, LANE), F32)],
        sem=("parallel", "parallel", "arbitrary"), comms=comms, aliases={6: 1})


def _out_proj_fused(cat, w_out, x, post_w, g1, pre_w, sc2, sh2):
    tm = 512

    def epi(mix, ex, outs):
        x_ref, pw_ref, g1_ref, w2_ref, sc_ref, sh_ref = ex
        outs[0][...] = mix
        n1, _, _ = _rms_fwd(mix, pw_ref[...])
        x1 = x_ref[...] + g1_ref[...] * n1
        outs[1][...] = x1
        y2, _, _ = _rms_fwd(x1, w2_ref[...])
        outs[2][...] = (y2 * (1.0 + sc_ref[...]) + sh_ref[...]).astype(BF16)

    return _mm_rows(cat, w_out, name="out_proj", tm=tm, extra=(x, post_w, g1, pre_w, sc2, sh2),
                    extra_specs=[_tok_spec(tm), _vec_spec(), _row_spec(), _vec_spec(), _row_spec(), _row_spec()],
                    out_specs=[_tok_spec(tm), _tok_spec(tm), _tok_spec(tm)],
                    out_shape=[_sds(x.shape, F32), _sds(x.shape, F32), _sds(x.shape, BF16)], epi=epi)


def _acc_out(ref, first, value):
    @pl.when(first)
    def _():
        ref[...] = value

    @pl.when(jnp.logical_not(first))
    def _():
        ref[...] += value


def _down_proj_fused(r, w_down, x1, post_w, g2, target):
    tm = 512
    bsz = x1.shape[0]

    def pro(r_ref, ex, outs):
        rv = r_ref[...]
        return rv * rv

    def epi(down, ex, outs):
        x1_ref, w_ref, g2_ref, t_ref = ex
        loss_ref, dy_ref, dd_ref, dg2_ref, dw_ref = outs
        w, g2v = w_ref[...], g2_ref[...]
        n2, dh, rstd = _rms_fwd(down, w)
        err = x1_ref[...] + g2v * n2 - t_ref[...]
        part = (0.5 / D_MODEL) * jnp.sum(jnp.sum(err * err, axis=-1, keepdims=True), axis=0, keepdims=True)
        loss_ref[...] += jnp.broadcast_to(part, (1, LANE))
        dy = err * (1.0 / D_MODEL)
        dy_ref[...] = dy
        dg2_ref[...] += _colsum(dy * n2)
        dd, dw_rows = _rms_bwd(dy * g2v, dh, rstd, w)
        dd_ref[...] = dd.astype(BF16)
        dw_ref[...] += _colsum(dw_rows)

    return _mm_rows(r, w_down, name="down_proj", tm=tm, extra=(x1, post_w, g2, target),
                    extra_specs=[_tok_spec(tm), _vec_spec(), _row_spec(), _tok_spec(tm)],
                    out_specs=[_vec_spec(LANE), _tok_spec(tm), _tok_spec(tm), _row_spec(), _vec_spec()],
                    out_shape=[_sds((1, LANE), F32), _sds(x1.shape, F32), _sds(x1.shape, BF16), _sds((bsz, 1, D_MODEL), F32),
                               _sds((1, D_MODEL), F32)], pro=pro, epi=epi, parts=2, zero_per_seq=(3,), zero_once=(0, 4),
                    vmem_limit=VMEM_LIMIT_BIG)


def _up_bwd_fused(dpre, w_up4, dy, x1, mix, pre_w, sc2, post_w, g1, comms=()):
    tm = 512
    bsz = x1.shape[0]

    def epi(dh2v, ex, outs):
        dy_ref, x1_ref, mix_ref, w2_ref, sc_ref, pw_ref, g1_ref = ex
        dx1_ref, dmix_ref, dsc_ref, dsh_ref, dg1_ref, dw2_ref, dpw_ref = outs
        w2, pw = w2_ref[...], pw_ref[...]
        y2, xh2, rstd2 = _rms_fwd(x1_ref[...], w2)
        dsh_ref[...] += _colsum(dh2v)
        dsc_ref[...] += _colsum(dh2v * y2)
        dx1n, dw_rows = _rms_bwd(dh2v * (1.0 + sc_ref[...]), xh2, rstd2, w2)
        dw2_ref[...] += _colsum(dw_rows)
        dx1 = dy_ref[...] + dx1n
        dx1_ref[...] = dx1
        n1, mh, rstd1 = _rms_fwd(mix_ref[...], pw)
        dg1_ref[...] += _colsum(dx1 * n1)
        dmix, dpw_rows = _rms_bwd(dx1 * g1_ref[...], mh, rstd1, pw)
        dmix_ref[...] = dmix.astype(BF16)
        dpw_ref[...] += _colsum(dpw_rows)

    row_shape = _sds((bsz, 1, D_MODEL), F32)
    vec_shape = _sds((1, D_MODEL), F32)
    return _mm_rows(dpre, w_up4, name="up_bwd", tm=tm, extra=(dy, x1, mix, pre_w, sc2, post_w, g1),
                    extra_specs=[_tok_spec(tm), _tok_spec(tm), _tok_spec(tm), _vec_spec(), _row_spec(), _vec_spec(), _row_spec()],
                    out_specs=[_tok_spec(tm), _tok_spec(tm), _row_spec(), _row_spec(), _row_spec(), _vec_spec(), _vec_spec()],
                    out_shape=[_sds(x1.shape, F32), _sds(x1.shape, BF16), row_shape, row_shape, row_shape, vec_shape, vec_shape],
                    epi=epi, b_chunks=w_up4.shape[0], comms=comms, parts=2, zero_per_seq=(2, 3, 4), zero_once=(5, 6),
                    vmem_limit=VMEM_LIMIT_BIG)


def _norm1_bwd(dh1, dx1, x, pre_w, sc1, tm=512, comms=()):
    bsz, seq, _ = x.shape

    def body(dh_ref, dx1_ref, x_ref, w_ref, sc_ref, gx_ref, dsc_ref, dsh_ref, dw_ref):
        b, i = pl.program_id(0), pl.program_id(1)
        w = w_ref[...]
        dh = dh_ref[...]
        y, xh, rstd = _rms_fwd(x_ref[...], w)
        _acc_out(dsh_ref, i == 0, _colsum(dh))
        _acc_out(dsc_ref, i == 0, _colsum(dh * y))
        dx, dw_rows = _rms_bwd(dh * (1.0 + sc_ref[...]), xh, rstd, w)
        _acc_out(dw_ref, jnp.logical_and(b == 0, i == 0), _colsum(dw_rows))
        gx_ref[...] = dx1_ref[...] + dx

    row_shape = _sds((bsz, 1, D_MODEL), F32)
    return _call(
        body, (dh1, dx1, x, pre_w, sc1), name="norm1_bwd", grid=(bsz, seq // tm),
        in_specs=[_tok_spec(tm), _tok_spec(tm), _tok_spec(tm), _vec_spec(), _row_spec()],
        out_specs=[_tok_spec(tm), _row_spec(), _row_spec(), _vec_spec()],
        out_shape=[_sds(x.shape, F32), row_shape, row_shape, _sds((1, D_MODEL), F32)],
        sem=("arbitrary", "arbitrary"), comms=comms)


def _hgrn_bwd(dcat, proj, o_raw, states, lb_table, norm_w, comms=()):
    bsz, seq, _ = proj.shape
    nstep = seq // HG_TOK
    rec0 = ATT_WIDTH // LANE

    def body(tbl_ref, nw_ref, dr_ref, q_ref, f_ref, i_ref, g_ref, o_ref, st_ref,
             dq_ref, df_ref, di_ref, dg_ref, dlb_ref, dnw_ref, ds_scr):
        h, b, t = pl.program_id(0), pl.program_id(1), pl.program_id(2)

        @pl.when(t == 0)
        def _():
            ds_scr[...] = jnp.zeros_like(ds_scr)

        lower, upper = _block_masks()
        dlb_parts = []
        dnw_acc = jnp.zeros((1, LANE), F32)
        for hp in range(HG_HPS):
            ls = slice(hp * LANE, (hp + 1) * LANE)
            hq, v, hg = q_ref[:, ls], i_ref[:, ls], g_ref[:, ls]
            nw = nw_ref[...]
            c = _hgrn_common(tbl_ref[:, ls], f_ref[:, ls], hq)
            qd, kd, ku = c["qd"], c["kd"], c["ku"]
            y, on, rstd = _rms_fwd(o_ref[:, ls], nw)
            sg = _sigmoid(hg)
            dr = dr_ref[:, ls]
            dg_ref[:, ls] = (dr * y * (sg * (1.0 + hg * (1.0 - sg)))).astype(BF16)
            do, dnw_rows = _rms_bwd(dr * (hg * sg), on, rstd, nw)
            at = jnp.where(upper, _dot(kd, qd, NT_DIMS), 0.0)
            da = jnp.where(lower, _dot(do, v, NT_DIMS), 0.0)
            dat = jnp.where(upper, _dot(v, do, NT_DIMS), 0.0)
            dv = _dot(at, do)
            dqd = _dot(da, kd)
            dkd = _dot(dat, qd)
            do_c, qd_c, v_c, ku_c = [_chunk_slices(z.astype(BF16)) for z in (do, qd, v, ku)]
            outer = [_dot(do_c[j], qd_c[j], TN_DIMS) for j in range(HG_NCH)]
            ds = ds_scr[hp]
            ds_after = [None] * HG_NCH
            for j in reversed(range(HG_NCH)):
                ds_after[j] = ds
                ds = outer[j] + ds * c["e_last"][j]
            ds_scr[hp] = ds
            states = [st_ref[hp, j] for j in range(HG_NCH)]
            dv = dv + jnp.concatenate([_dot(ku_c[j], ds_after[j], NT_DIMS) for j in range(HG_NCH)], axis=0)
            dqd = dqd + jnp.concatenate([_dot(do_c[j], states[j]) for j in range(HG_NCH)], axis=0)
            dku = jnp.concatenate([_dot(v_c[j], ds_after[j]) for j in range(HG_NCH)], axis=0)
            dku_ku = dku * ku
            dbl = [_colsum(states[j] * ds_after[j]) * c["e_last"][j] + _colsum(dku_ku[j * HG_CHUNK:(j + 1) * HG_CHUNK])
                   for j in range(HG_NCH)]
            dk = dkd * c["e_nb"] + dku * c["e_rem"]
            db = dqd * qd - dkd * kd - dku_ku + jnp.where(_row_in_chunk() == HG_CHUNK - 1, _chunk_rows(dbl), 0.0)
            dfv = _chunk_cumsum(db, reverse=True) / c["f"] - dk
            sig, sq = c["sig"], c["sq"]
            df_ref[:, ls] = (dfv * (1.0 - c["lb"]) * sig * (1.0 - sig)).astype(BF16)
            dq_ref[:, ls] = (dqd * c["e_b"] * (sq * (1.0 + hq * (1.0 - sq)))).astype(BF16)
            di_ref[:, ls] = dv.astype(BF16)
            dlb_parts.append(_colsum(dfv * (1.0 - sig)))
            dnw_acc = dnw_acc + _colsum(dnw_rows)
        _acc_out(dlb_ref, jnp.logical_and(b == 0, t == 0), jnp.concatenate(dlb_parts, axis=1))
        _acc_out(dnw_ref, jnp.logical_and(h == 0, jnp.logical_and(b == 0, t == 0)), dnw_acc)

    rev = lambda t: nstep - 1 - t
    width = HG_HPS * LANE
    slab = lambda first: pl.BlockSpec((None, HG_TOK, width), lambda h, b, t: (b, rev(t), first // HG_HPS + h))
    head = pl.BlockSpec((None, HG_TOK, width), lambda h, b, t: (b, rev(t), h))
    grad_shape = _sds((bsz, seq, HG_WIDTH), BF16)
    return _call(
        body, (lb_table, norm_w, dcat, proj, proj, proj, proj, o_raw, states), name="hgrn_bwd",
        grid=(HG_HEADS // HG_HPS, bsz, nstep),
        in_specs=[pl.BlockSpec((2, width), lambda h, b, t: (0, h)), pl.BlockSpec((1, LANE), lambda h, b, t: (0, 0)),
                  slab(rec0), slab(HG_Q0), slab(HG_F0), slab(HG_I0), slab(HG_G0), head,
                  pl.BlockSpec((None, HG_HPS, HG_NCH, LANE, LANE), lambda h, b, t: (b, h, rev(t), 0, 0))],
        out_specs=[head, head, head, head, pl.BlockSpec((1, width), lambda h, b, t: (0, h)),
                   pl.BlockSpec((1, LANE), lambda h, b, t: (0, 0))],
        out_shape=[grad_shape, grad_shape, grad_shape, grad_shape, _sds((1, HG_WIDTH), F32), _sds((1, LANE), F32)],
        scratch_shapes=[pltpu.VMEM((HG_HPS, LANE, LANE), F32)],
        sem=("arbitrary", "arbitrary", "arbitrary"), comms=comms)


def _attn_bwd(dcat, raw, w_norm, qh, kh, vh, lse, sinks, tables, comms=()):
    bsz, nblk = qh.shape[0], qh.shape[1]
    seq = nblk * WINDOW
    nstep = nblk // ATT_BPS
    half = ROPE_DIM // 2

    def body(sink_ref, da_ref, raw_ref, w_ref, q_ref, kc_ref, kp_ref, vc_ref, vp_ref, l_ref, c_ref, u_ref, d_ref,
             o_ref, dw_ref, dsink_ref, carry_k, carry_v):
        b, i = pl.program_id(0), pl.program_id(1)
        first = jnp.logical_and(b == 0, i == 0)

        @pl.when(i == 0)
        def _():
            carry_k[...] = jnp.zeros_like(carry_k)
            carry_v[...] = jnp.zeros_like(carry_v)

        w = w_ref[...]
        _, on, rstd = _rms_fwd(raw_ref[...], w)
        do_step, dw_rows = _rms_bwd(da_ref[...], on, rstd, w)
        _acc_out(dw_ref, first, _colsum(dw_rows))
        lane8 = lax.broadcasted_iota(jnp.int32, (1, ATT_Q_HEADS), 1)
        dsink = jnp.zeros((1, ATT_Q_HEADS), F32)
        from_next_k, from_next_v = carry_k[...], carry_v[...]
        for blk in reversed(range(ATT_BPS)):
            tok = slice(blk * WINDOW, (blk + 1) * WINDOW)
            mask = _band_mask(True if blk else i < nstep - 1)
            raw_v, do_all = raw_ref[tok, :], do_step[tok]
            c, u, d = c_ref[tok, :], u_ref[tok, :], d_ref[tok, :]

            def unrope(g):
                return (g * c + pltpu.roll(g * u, LANE - half, 1) + pltpu.roll(g * d, half, 1)).astype(BF16)

            dq_parts, dk_own, dk_before, dv_own, dv_before = [], [], [], [], []
            for g in range(ATT_KV_HEADS):
                heads = [slice((g * ATT_GROUP + hh) * ATT_HEAD_DIM, (g * ATT_GROUP + hh + 1) * ATT_HEAD_DIM)
                         for hh in range(ATT_GROUP)]
                q = q_ref[blk, g]
                keys, vals = _band(kp_ref, kc_ref, g, blk), _band(vp_ref, vc_ref, g, blk)
                do_g = jnp.concatenate([do_all[:, hs] for hs in heads], axis=0)
                dsum = jnp.concatenate([jnp.sum(do_all[:, hs] * raw_v[:, hs], axis=-1, keepdims=True) for hs in heads], axis=0)
                lse_g = jnp.concatenate([l_ref[tok, g * ATT_GROUP + hh:g * ATT_GROUP + hh + 1] for hh in range(ATT_GROUP)], axis=0)
                p = jnp.where(mask, jnp.exp(_dot(q, keys, NT_DIMS) * ATT_SCALE - lse_g), 0.0)
                sink_part = jnp.exp(_sink_column(sink_ref, g) - lse_g) * dsum
                for hh in range(ATT_GROUP):
                    head_sum = jnp.sum(sink_part[hh * WINDOW:(hh + 1) * WINDOW], axis=0, keepdims=True)
                    dsink = dsink - jnp.where(lane8 == g * ATT_GROUP + hh, head_sum, 0.0)
                ds = p * (_dot(do_g, vals, NT_DIMS) - dsum) * ATT_SCALE
                dq_g = _dot(ds, keys)
                dq_parts += [dq_g[hh * WINDOW:(hh + 1) * WINDOW] for hh in range(ATT_GROUP)]
                dk_g = _dot(ds, q, TN_DIMS)
                dv_g = _dot(p, do_g, TN_DIMS)
                dk_before.append(dk_g[:WINDOW])
                dk_own.append(dk_g[WINDOW:])
                dv_before.append(dv_g[:WINDOW])
                dv_own.append(dv_g[WINDOW:])
            per_slab = LANE // ATT_HEAD_DIM
            for s in range(ATT_WIDTH // LANE):
                slab = jnp.concatenate(dq_parts[s * per_slab:(s + 1) * per_slab], axis=1)
                o_ref[tok, s * LANE:(s + 1) * LANE] = unrope(slab)
            o_ref[tok, ATT_WIDTH:ATT_WIDTH + LANE] = unrope(jnp.concatenate(dk_own, axis=1) + from_next_k)
            o_ref[tok, ATT_WIDTH + LANE:ATT_COLS] = (jnp.concatenate(dv_own, axis=1) + from_next_v).astype(BF16)
            from_next_k, from_next_v = jnp.concatenate(dk_before, axis=1), jnp.concatenate(dv_before, axis=1)
        carry_k[...] = from_next_k
        carry_v[...] = from_next_v
        _acc_out(dsink_ref, first, dsink)

    rows = ATT_BPS * WINDOW
    rev = lambda i: nstep - 1 - i
    cur = lambda width: pl.BlockSpec((None, rows, width), lambda b, i: (b, rev(i), 0))
    q_spec = pl.BlockSpec((None, ATT_BPS, ATT_KV_HEADS, GROUP_ROWS, ATT_HEAD_DIM), lambda b, i: (b, rev(i), 0, 0, 0))
    kv_cur = pl.BlockSpec((None, ATT_KV_HEADS, rows, ATT_HEAD_DIM), lambda b, i: (b, 0, rev(i), 0))
    kv_prev = pl.BlockSpec((None, ATT_KV_HEADS, WINDOW, ATT_HEAD_DIM), lambda b, i: (b, 0, jnp.maximum(ATT_BPS * rev(i) - 1, 0), 0))
    tab = pl.BlockSpec((rows, LANE), lambda b, i: (rev(i), 0))
    return _call(
        body, (sinks, dcat, raw, w_norm, qh, kh, kh, vh, vh, lse, *tables), name="attn_bwd", grid=(bsz, nstep),
        in_specs=[pl.BlockSpec(memory_space=pltpu.SMEM), cur(ATT_WIDTH), cur(ATT_WIDTH), _vec_spec(ATT_WIDTH), q_spec,
                  kv_cur, kv_prev, kv_cur, kv_prev, cur(ATT_Q_HEADS), tab, tab, tab],
        out_specs=[cur(ATT_COLS), _vec_spec(ATT_WIDTH), _vec_spec(ATT_Q_HEADS)],
        out_shape=[_sds((bsz, seq, ATT_COLS), BF16), _sds((1, ATT_WIDTH), F32), _sds((1, ATT_Q_HEADS), F32)],
        scratch_shapes=[pltpu.VMEM((WINDOW, LANE), F32), pltpu.VMEM((WINDOW, LANE), F32)],
        sem=("arbitrary", "arbitrary"), comms=comms)


def _other_chips(x, y):
    return [(1 - x, y), (x, 1 - y), (1 - x, 1 - y)]


def _sem_pair(n):
    return [pltpu.SemaphoreType.DMA((n,)), pltpu.SemaphoreType.DMA((n,))]


def _plan_pair_forward(bufs):
    n = len(bufs)

    def copies(outs, sems):
        x, y, c = _mesh_pos()
        sends, lands = [], []
        for a in range(n):
            for j, chip in enumerate(_other_chips(x, y)):
                k = 3 * a + j
                slot = outs[a].at[4 * chip[0] + 2 * chip[1] + c]
                sends.append(pltpu.make_async_remote_copy(
                    src_ref=slot, dst_ref=slot, send_sem=sems[0].at[k], recv_sem=sems[1].at[k],
                    device_id=(x, y, 1 - c), device_id_type=MESH))
                theirs = outs[a].at[4 * chip[0] + 2 * chip[1] + 1 - c]
                lands.append(pltpu.make_async_remote_copy(
                    src_ref=theirs, dst_ref=theirs, send_sem=sems[0].at[k], recv_sem=sems[1].at[k],
                    device_id=(x, y, 1 - c), device_id_type=MESH))
        return sends, lands

    def start(ins, outs, sems):
        for cp in copies(outs, sems)[0]:
            cp.start()

    def finish(ins, outs, sems):
        sends, lands = copies(outs, sems)
        for cp in lands:
            cp.wait_recv()
        for cp in sends:
            cp.wait_send()

    return _Comm(list(bufs), [_sds(b.shape, b.dtype) for b in bufs], _sem_pair(3 * n), start, finish,
                 aliases=[(a, a) for a in range(n)])


def _plan_pair(arrays, other_half):
    n = len(arrays)
    per = N_CHIPS if other_half == "chip_major" else 1

    def copies(ins, outs, sems):
        x, y, c = _mesh_pos()
        out = []
        for a in range(n):
            for k in range(per):
                if other_half == "chip_major":
                    src, dst = ins[a].at[k, 1 - c], outs[a].at[k]
                else:
                    src, dst = (ins[a].at[1 - c] if other_half else ins[a]), outs[a]
                out.append(pltpu.make_async_remote_copy(
                    src_ref=src, dst_ref=dst, send_sem=sems[0].at[per * a + k], recv_sem=sems[1].at[per * a + k],
                    device_id=(x, y, 1 - c), device_id_type=MESH))
        return out

    def start(ins, outs, sems):
        for cp in copies(ins, outs, sems):
            cp.start()

    def finish(ins, outs, sems):
        for cp in copies(ins, outs, sems):
            cp.wait()

    if other_half == "chip_major":
        shapes = [_sds((a.shape[0],) + a.shape[2:], a.dtype) for a in arrays]
    else:
        shapes = [_sds(a.shape[1:] if other_half else a.shape, a.dtype) for a in arrays]
    return _Comm(list(arrays), shapes, _sem_pair(per * n), start, finish)


def _plan_chip_exchange(arrays):
    n = len(arrays)

    def copies(ins, outs, sems):
        x, y, c = _mesh_pos()
        sends, lands = [], []
        for a in range(n):
            for j, chip in enumerate(_other_chips(x, y)):
                k = 3 * a + j
                sends.append(pltpu.make_async_remote_copy(
                    src_ref=ins[a].at[2 * chip[0] + chip[1]], dst_ref=outs[a].at[2 * x + y], send_sem=sems[0].at[k],
                    recv_sem=sems[1].at[k], device_id=(*chip, c), device_id_type=MESH))
                slot = outs[a].at[2 * chip[0] + chip[1]]
                lands.append(pltpu.make_async_remote_copy(
                    src_ref=slot, dst_ref=slot, send_sem=sems[0].at[k], recv_sem=sems[1].at[k],
                    device_id=(*chip, c), device_id_type=MESH))
        return sends, lands

    def start(ins, outs, sems):
        for cp in copies(ins, outs, sems)[0]:
            cp.start()

    def finish(ins, outs, sems):
        sends, lands = copies(ins, outs, sems)
        for cp in lands:
            cp.wait_recv()
        for cp in sends:
            cp.wait_send()

    return _Comm(list(arrays), [_sds(a.shape, a.dtype) for a in arrays], _sem_pair(3 * n), start, finish)


SEM_SPEC = pl.BlockSpec(memory_space=pltpu.SEMAPHORE)
N_OTHER = N_CHIPS - 1


def _exchange_copies(s_ref, land_ref, sems):
    x, y, c = _mesh_pos()
    return [pltpu.make_async_remote_copy(
        src_ref=s_ref.at[2 * chip[0] + chip[1]], dst_ref=land_ref.at[2 * x + y], send_sem=sems[j], recv_sem=sems[N_OTHER + j],
        device_id=(*chip, c), device_id_type=MESH) for j, chip in enumerate(_other_chips(x, y))]


def _exchange_start(s, name):
    def body(s_ref, land_ref, *outs):
        sems, token = outs[:2 * N_OTHER], outs[-1]
        for cp in _exchange_copies(s_ref, land_ref, sems):
            cp.start()
        token[...] = jnp.zeros_like(token)

    hbm = pltpu.HBM(s.shape, s.dtype)
    res = pl.pallas_call(
        body, name=name,
        out_shape=(pltpu.SemaphoreType.DMA(()),) * (2 * N_OTHER) + (hbm, hbm, _sds((SUBLANES, LANE), F32)),
        in_specs=(HBM_SPEC, HBM_SPEC),
        out_specs=(SEM_SPEC,) * (2 * N_OTHER) + (HBM_SPEC, HBM_SPEC, pl.BlockSpec(memory_space=pltpu.VMEM)),
        input_output_aliases={0: 2 * N_OTHER, 1: 2 * N_OTHER + 1},
        compiler_params=pltpu.CompilerParams(has_side_effects=pltpu.SideEffectType.DATAFLOW_SIDE_EFFECTING),
    )(pltpu.with_memory_space_constraint(s, pltpu.HBM), pltpu.with_memory_space_constraint(lax.empty(s.shape, s.dtype), pltpu.HBM))
    return res[:2 * N_OTHER], res[2 * N_OTHER], res[2 * N_OTHER + 1], res[-1]


def _exchange_wait(sems, s_thru, land_thru, afters, name):
    def body(s_ref, land_ref, *rest):
        for cp in _exchange_copies(s_ref, land_ref, rest[:2 * N_OTHER]):
            cp.wait_send()
            cp.wait_recv()

    hbm = pltpu.HBM(s_thru.shape, s_thru.dtype)
    return pl.pallas_call(
        body, name=name, out_shape=(hbm, hbm),
        in_specs=(HBM_SPEC, HBM_SPEC) + (SEM_SPEC,) * (2 * N_OTHER) + (pl.BlockSpec(memory_space=pl.ANY),) * len(afters),
        out_specs=(HBM_SPEC, HBM_SPEC), input_output_aliases={0: 0, 1: 1},
        compiler_params=pltpu.CompilerParams(has_side_effects=pltpu.SideEffectType.DATAFLOW_SIDE_EFFECTING),
    )(s_thru, land_thru, *sems, *afters)


def _gather_copies(block_ref, buf_ref, sems):
    x, y, c = _mesh_pos()
    return [pltpu.make_async_remote_copy(
        src_ref=block_ref, dst_ref=buf_ref.at[4 * x + 2 * y + c], send_sem=sems[j], recv_sem=sems[N_OTHER + j],
        device_id=(*chip, c), device_id_type=MESH) for j, chip in enumerate(_other_chips(x, y))]


def _gather_start(blocks, bufs, afters, name):
    n = len(blocks)
    per = 2 * N_OTHER

    def body(*refs):
        ins, outs = refs[:2 * n], refs[2 * n + len(afters):]
        for a in range(n):
            for cp in _gather_copies(ins[a], ins[n + a], outs[a * per:(a + 1) * per]):
                cp.start()
        outs[-1][...] = jnp.zeros_like(outs[-1])

    hbm = [pltpu.HBM(z.shape, z.dtype) for z in list(blocks) + list(bufs)]
    res = pl.pallas_call(
        body, name=name,
        out_shape=(pltpu.SemaphoreType.DMA(()),) * (n * per) + tuple(hbm) + (_sds((SUBLANES, LANE), F32),),
        in_specs=(HBM_SPEC,) * (2 * n) + (pl.BlockSpec(memory_space=pl.ANY),) * len(afters),
        out_specs=(SEM_SPEC,) * (n * per) + (HBM_SPEC,) * (2 * n) + (pl.BlockSpec(memory_space=pltpu.VMEM),),
        input_output_aliases={k: n * per + k for k in range(2 * n)},
        compiler_params=pltpu.CompilerParams(has_side_effects=pltpu.SideEffectType.DATAFLOW_SIDE_EFFECTING),
    )(*[pltpu.with_memory_space_constraint(z, pltpu.HBM) for z in list(blocks) + list(bufs)], *afters)
    parts = [(res[a * per:(a + 1) * per], res[n * per + a], res[n * per + n + a]) for a in range(n)]
    return parts, res[-1]


def _gather_wait(part, afters, name):
    sems, block, buf = part

    def body(block_ref, buf_ref, *rest):
        for cp in _gather_copies(block_ref, buf_ref, rest[:2 * N_OTHER]):
            cp.wait_send()
            cp.wait_recv()

    return pl.pallas_call(
        body, name=name, out_shape=(pltpu.HBM(block.shape, block.dtype), pltpu.HBM(buf.shape, buf.dtype)),
        in_specs=(HBM_SPEC, HBM_SPEC) + (SEM_SPEC,) * (2 * N_OTHER) + (pl.BlockSpec(memory_space=pl.ANY),) * len(afters),
        out_specs=(HBM_SPEC, HBM_SPEC), input_output_aliases={0: 0, 1: 1},
        compiler_params=pltpu.CompilerParams(has_side_effects=pltpu.SideEffectType.DATAFLOW_SIDE_EFFECTING),
    )(block, buf, *sems, *afters)[1]


def _comm_only(comms, name):
    return _call(lambda: None, (), name=name, grid=(), in_specs=[], out_specs=[], out_shape=[], sem=(), comms=comms)[1]


def _allgather8(arrays, name):
    return _comm_only([_plan_allgather8(arrays)], name)[0]


def _plan_allgather8(arrays):
    n = len(arrays)

    def parts(ins, outs, sems):
        send_sems, recv_sems, local_sems = sems
        x, y, c = _mesh_pos()
        me, sibling = (x, y, c), (x, y, 1 - c)
        chips = _other_chips(x, y)

        def copy(a, k, block, to, src=None):
            dst = outs[a].at[4 * block[0] + 2 * block[1] + block[2]]
            return pltpu.make_async_remote_copy(
                src_ref=dst if src is None else src, dst_ref=dst, send_sem=send_sems.at[7 * a + k],
                recv_sem=recv_sems.at[7 * a + k], device_id=to, device_id_type=MESH)

        mine = [pltpu.make_async_copy(ins[a], outs[a].at[4 * x + 2 * y + c], local_sems.at[a]) for a in range(n)]
        first = []
        for a in range(n):
            first.append(copy(a, 0, me, sibling, src=ins[a]))
            first += [copy(a, 1 + j, me, (*chip, c), src=ins[a]) for j, chip in enumerate(chips)]
        return copy, mine, first, me, sibling, chips, c

    def start(ins, outs, sems):
        _, mine, first, *_ = parts(ins, outs, sems)
        for cp in mine + first:
            cp.start()

    def finish(ins, outs, sems):
        copy, mine, first, me, sibling, chips, c = parts(ins, outs, sems)
        passed = []
        for j, chip in enumerate(chips):
            for a in range(n):
                copy(a, 1 + j, (*chip, c), me).wait_recv()
                fwd = copy(a, 4 + j, (*chip, c), sibling)
                fwd.start()
                passed.append(fwd)
        for a in range(n):
            copy(a, 0, sibling, me).wait_recv()
            for j, chip in enumerate(chips):
                copy(a, 4 + j, (*chip, 1 - c), me).wait_recv()
        for cp in first + passed:
            cp.wait_send()
        for cp in mine:
            cp.wait()

    sems = [pltpu.SemaphoreType.DMA((7 * n,)), pltpu.SemaphoreType.DMA((7 * n,)), pltpu.SemaphoreType.DMA((n,))]
    return _Comm(list(arrays), [_sds((N_DEV,) + a.shape, a.dtype) for a in arrays], sems, start, finish)


def _pair_sum(g, q, core, name, chip_major=False):
    rows, cols = g.shape[2:]
    tr = _row_tile(rows)

    def body(core_ref, g_ref, q_ref, o_ref):
        o_ref[...] = (g_ref[...] + q_ref[...]).astype(BF16)

    blk = pl.BlockSpec((None, tr, cols), lambda k, i, core_ref: (k, i, 0))
    if chip_major:
        own = pl.BlockSpec((None, None, tr, cols), lambda k, i, core_ref: (k, core_ref[0], i, 0))
    else:
        own = pl.BlockSpec((None, None, tr, cols), lambda k, i, core_ref: (core_ref[0], k, i, 0))
    return pl.pallas_call(
        body, name=name,
        grid_spec=pltpu.PrefetchScalarGridSpec(num_scalar_prefetch=1, grid=(N_CHIPS, rows // tr), in_specs=[own, blk], out_specs=blk),
        out_shape=_sds((N_CHIPS, rows, cols), BF16), compiler_params=_params("parallel", "parallel"),
    )(core, g, q)


def _sum_chips(own, landed, chip, name):
    _, rows, cols = own.shape
    tr = _row_tile(rows)

    def body(chip_ref, own_ref, a_ref, b_ref, c_ref, o_ref):
        acc = own_ref[...].astype(F32) + a_ref[...].astype(F32)
        o_ref[...] = (acc + b_ref[...].astype(F32)) + c_ref[...].astype(F32)

    blk = lambda flip: pl.BlockSpec((None, tr, cols), lambda i, chip_ref: (jnp.bitwise_xor(chip_ref[0], flip), i, 0))
    return pl.pallas_call(
        body, name=name,
        grid_spec=pltpu.PrefetchScalarGridSpec(num_scalar_prefetch=1, grid=(rows // tr,), in_specs=[blk(0), blk(1), blk(2), blk(3)],
                                               out_specs=pl.BlockSpec((tr, cols), lambda i, chip_ref: (i, 0))),
        out_shape=_sds((rows, cols), F32), compiler_params=_params("parallel"),
    )(chip, own, landed, landed, landed)


SUBLANES = 8


def _tile_rows(n_elems):
    return -(-n_elems // (SUBLANES * LANE)) * SUBLANES


SMALL_ITEMS = (("b_ada", N_MOD * D_MODEL), ("pre_w_mix", D_MODEL), ("post_w_mix", D_MODEL), ("pre_w_mlp", D_MODEL),
               ("post_w_mlp", D_MODEL), ("attn_out_w", ATT_WIDTH), ("hg_norm_w", HG_HEAD_DIM), ("attn_sinks", ATT_Q_HEADS),
               ("lb_0", HG_WIDTH), ("lb_1", HG_WIDTH))
SMALL_AT = {}
for _name, _size in SMALL_ITEMS:
    SMALL_AT[_name] = (sum(r for _, r in SMALL_AT.values()), _tile_rows(_size))
SMALL_ROWS = sum(r for _, r in SMALL_AT.values())
MOD_ROWS = SMALL_AT["b_ada"][1]
PLAIN_ROWS = SMALL_AT["lb_0"][0] - MOD_ROWS
LB_ROWS = SMALL_AT["lb_0"][1]


def _rows(a, nrows=None):
    flat = a.reshape(-1)
    nrows = _tile_rows(flat.shape[0]) if nrows is None else nrows
    return jnp.pad(flat, (0, nrows * LANE - flat.shape[0])).reshape(nrows, LANE)


def _pack_small(vals):
    vals = dict(vals, lb_0=vals["lb_table"][0], lb_1=vals["lb_table"][1])
    return jnp.concatenate([_rows(vals[name], SMALL_AT[name][1]) for name, _ in SMALL_ITEMS], axis=0)


def _unpack_small(p):
    def item(name, shape):
        first = SMALL_AT[name][0]
        size = shape[0] * shape[1]
        return p[first:first + SMALL_AT[name][1]].reshape(-1)[:size].reshape(shape)

    out = {name: item(name, (1, size)) for name, size in SMALL_ITEMS if not name.startswith("lb_")}
    out["lb_table"] = jnp.concatenate([item("lb_0", (1, HG_WIDTH)), item("lb_1", (1, HG_WIDTH))], axis=0)
    return out


def _pack_partials(dmod, plain, d_lb, loss_row):
    return jnp.concatenate([_rows(dmod, dmod.shape[0] * MOD_ROWS)] + [_rows(g) for g in plain] + [_rows(d_lb), _rows(loss_row)], axis=0)


def _small_update(packs, w, m, v, n_seq):
    mod_end = n_seq * MOD_ROWS
    lb_at = mod_end + PLAIN_ROWS
    t0, t1 = SMALL_AT["lb_0"][0], SMALL_AT["lb_1"][0]

    def body(p_ref, w_ref, m_ref, v_ref, g_ref, dl_ref, nm_ref, nv_ref, loss_ref):
        tot = p_ref[0]
        for d in range(1, N_DEV):
            tot = tot + p_ref[d]
        wv = w_ref[...]
        p1 = _sigmoid(wv[t1:t1 + LB_ROWS] - wv[t0:t0 + LB_ROWS])
        s = tot[lb_at:lb_at + LB_ROWS] * p1 * (1.0 - p1)
        g_bias = tot[0:MOD_ROWS]
        for q in range(1, n_seq):
            g_bias = g_bias + tot[q * MOD_ROWS:(q + 1) * MOD_ROWS]
        g = jnp.concatenate([g_bias, tot[mod_end:lb_at], -s, s], axis=0)
        g_ref[...] = g
        dl_ref[...], nm_ref[...], nv_ref[...] = _adamw_math(g, wv, m_ref[...], v_ref[...])
        loss_ref[...] = tot[lb_at + LB_ROWS:lb_at + LB_ROWS + SUBLANES]

    shp = _sds((SMALL_ROWS, LANE), F32)
    return pl.pallas_call(body, name="small_update", out_shape=[shp] * 4 + [_sds((SUBLANES, LANE), F32)],
                          compiler_params=_params())(packs, w, m, v)


def kernel(x, c, w_ada, b_ada, pre_w_mix, w_in, attn_sinks, attn_out_w, lb_table, hg_norm_w, w_out, post_w_mix, pre_w_mlp, w_up, w_down, post_w_mlp, loss_target, m_w_ada, m_b_ada, m_pre_w_mix, m_w_in, m_attn_sinks, m_attn_out_w, m_lb_table, m_hg_norm_w, m_w_out, m_post_w_mix, m_pre_w_mlp, m_w_up, m_w_down, m_post_w_mlp, v_w_ada, v_b_ada, v_pre_w_mix, v_w_in, v_attn_sinks, v_attn_out_w, v_lb_table, v_hg_norm_w, v_w_out, v_post_w_mix, v_pre_w_mlp, v_w_up, v_w_down, v_post_w_mlp):
    xi, yi, ci = _mesh_pos()
    chip = 2 * xi + yi
    dev = 2 * chip + ci
    bsz, seq, _ = x.shape
    ntok = bsz * seq
    ada_cols = w_ada.shape[2]
    core = jnp.reshape(ci, (1,)).astype(jnp.int32)
    chip_idx = jnp.reshape(chip, (1,)).astype(jnp.int32)
    flat = lambda a: a.reshape(ntok, a.shape[-1])
    unflat = lambda a: a.reshape(bsz, seq, a.shape[-1])
    tables = _rope_tables(seq)

    def row_half(w):
        rows = w.shape[1] // 2
        return lax.dynamic_slice_in_dim(w[0], ci * rows, rows, axis=0).astype(BF16)

    def gather_buffer(w):
        rows, cols = w.shape[1] // 2, w.shape[2]
        own = w[0].astype(BF16).reshape(2, rows, cols)
        return lax.dynamic_update_slice(lax.empty((N_DEV, rows, cols), BF16), own, (2 * chip, 0, 0))

    w_in_t, m_in_t, v_in_t = [jnp.transpose(a[0])[None] for a in (w_in, m_w_in, v_w_in)]
    (c_g,) = _allgather8([c], "gather_c")
    c_all = c_g.reshape(N_DEV * bsz, D_MODEL)

    b_cols = lax.dynamic_slice_in_dim(b_ada, chip * ada_cols, ada_cols, axis=1)
    mod_part = _ada_fwd(c_all, w_ada[0], b_cols)
    half_rows = mod_part.shape[0] // 2
    (mod_g,) = _allgather8([lax.dynamic_slice_in_dim(mod_part, ci * half_rows, half_rows, axis=0)], "gather_mod")
    mod_all = mod_g.reshape(N_CHIPS, 2, half_rows, ada_cols).transpose(1, 2, 0, 3).reshape(N_DEV * bsz, N_MOD * D_MODEL)
    mod = lax.dynamic_slice_in_dim(mod_all, dev * bsz, bsz, axis=0)
    sh1, sc1, g1, sh2, sc2, g2 = [mod[:, i * D_MODEL:(i + 1) * D_MODEL].reshape(bsz, 1, D_MODEL) for i in range(N_MOD)]

    weights = (w_in_t, w_out, w_up, w_down)
    (in_part, out_part, up_part, down_part), _ = _gather_start(
        [row_half(w) for w in weights], [gather_buffer(w) for w in weights], [mod_g], "gather_weights_start")

    ((in_g,),) = _comm_only([_plan_pair_forward([_gather_wait(in_part, [], "gather_in_wait")])], "forward_in")
    w_in_full = in_g.reshape(IN_COLS, D_MODEL)
    h1, proj, qh, kh, vh = _in_proj_fused(x, pre_w_mix, sc1, sh1, w_in_full, tables)
    out_g = _gather_wait(out_part, [proj], "gather_out_wait")
    (attn_raw, cat, lse), ((out_g,),) = _attn_fwd(qh, kh, vh, attn_sinks, attn_out_w, comms=[_plan_pair_forward([out_g])])
    up_g = _gather_wait(up_part, [attn_raw], "gather_up_wait")
    (o_raw, cat, states), ((up_g,),) = _hgrn_fwd(proj, lb_table, hg_norm_w, cat, comms=[_plan_pair_forward([up_g])])
    down_g = _gather_wait(down_part, [o_raw], "gather_down_wait")
    w_out_full = out_g.reshape(D_MODEL, D_MODEL)
    w_up4 = up_g.reshape(N_CHIPS, D_MODEL, D_MODEL)
    mix, x1, h2 = _out_proj_fused(cat, w_out_full, x, post_w_mix, g1, pre_w_mlp, sc2, sh2)
    big_tm = min(ntok, 2048)
    up_spec = pl.BlockSpec((None, D_MODEL, D_MODEL), lambda i, j: (j, 0, 0))
    r, ((down_g,),) = _mm(flat(h2), w_up4, name="up_proj", out_dtype=BF16, tm=big_tm, tn=D_MODEL, n_out=D_FF, b_spec=up_spec,
                          epi=lambda acc: jnp.maximum(acc, 0.0), comms=[_plan_pair_forward([down_g])])
    w_down_full = down_g.reshape(D_FF, D_MODEL)
    square = lambda t: t * t
    loss_row, dy, dd, dg2, d_post_mlp = _down_proj_fused(unflat(r), w_down_full, x1, post_w_mlp, g2, loss_target)

    dpre = _mm(flat(dd), w_down_full, name="down_bwd", out_dtype=BF16, trans_b=True, tm=big_tm, tn=D_MODEL, extra=(r,),
               epi=lambda acc, rt: acc * (2.0 * rt.astype(F32)))
    half_rows = D_MODEL // 2
    g_down = _mm_tn(r, flat(dd), name="down_wgrad", tk=half_rows, tn=D_MODEL, a_fn=square,
                    out_shape=_sds((2, N_CHIPS, half_rows, D_MODEL), F32),
                    out_spec=pl.BlockSpec((None, None, half_rows, D_MODEL), lambda i, j: (i % 2, i // 2, 0, 0)))
    (dx1, dmix, dsc2, dsh2, dg1, d_pre_mlp, d_post_mix), ((q_down,),) = _up_bwd_fused(
        unflat(dpre), w_up4, dy, x1, mix, pre_w_mlp, sc2, post_w_mix, g1, comms=[_plan_pair([g_down], True)])
    g_up = _mm_tn(flat(h2), dpre, name="up_wgrad", tk=D_MODEL, tn=half_rows,
                  out_shape=_sds((2, N_CHIPS, half_rows, D_MODEL), F32),
                  out_spec=pl.BlockSpec((2, None, half_rows, half_rows), lambda i, j: (0, j // 2, 0, j % 2)))
    s_down = _pair_sum(g_down, q_down, core, "pair_sum_down")

    dcat, ((q_up,),) = _mm(flat(dmix), w_out_full, name="out_bwd", out_dtype=F32, trans_b=True, comms=[_plan_pair([g_up], True)])
    dcat = unflat(dcat)
    s_up = _pair_sum(g_up, q_up, core, "pair_sum_up")
    out_rows = D_MODEL // N_CHIPS
    g_out = _mm_tn(flat(cat), flat(dmix), name="out_wgrad", tk=2 * out_rows, tn=half_rows,
                   out_shape=_sds((2, N_CHIPS, out_rows, half_rows), F32),
                   out_spec=pl.BlockSpec((None, 2, out_rows, half_rows), lambda i, j: (j, i, 0, 0)))
    (dhq, dhf, dhi, dhg, d_lb, d_hg_norm), ((x_down,), (q_out,)) = _hgrn_bwd(
        dcat, proj, o_raw, states, lb_table, hg_norm_w, comms=[_plan_chip_exchange([s_down]), _plan_pair([g_out], True)])
    half_down = _sum_chips(s_down, x_down, chip_idx, "sum_chips_down")
    s_out = _pair_sum(g_out, q_out, core, "pair_sum_out")
    (dproj_a, d_attn_out, d_sinks), ((their_down,), (x_up, x_out)) = _attn_bwd(
        dcat, attn_raw, attn_out_w, qh, kh, vh, lse, attn_sinks, tables,
        comms=[_plan_pair([half_down], False), _plan_chip_exchange([s_up, s_out])])
    half_up = _sum_chips(s_up, x_up, chip_idx, "sum_chips_up")
    half_out = _sum_chips(s_out, x_out, chip_idx, "sum_chips_out")
    dproj = flat(jnp.concatenate([dproj_a, dhq, dhf, dhi, dhg], axis=-1))
    in_rows = IN_COLS // N_CHIPS // 2
    g_in = _mm_tn(dproj, flat(h1), name="in_wgrad", tk=2 * LANE, tn=D_MODEL).reshape(N_CHIPS, 2, in_rows, D_MODEL)
    dh1, ((q_in,), (their_up, their_out)) = _mm(
        dproj, w_in_full, name="in_bwd", out_dtype=F32,
        comms=[_plan_pair([g_in], "chip_major"), _plan_pair([half_up, half_out], False)])
    s_in = _pair_sum(g_in, q_in, core, "pair_sum_in", chip_major=True)
    in_sems, s_in, in_landing, started = _exchange_start(s_in, "exchange_in_start")
    grad_x, dsc1, dsh1, d_pre_mix = _norm1_bwd(unflat(dh1), dx1, x, pre_w_mix + started[0:1, 0:1], sc1)

    dmod = jnp.concatenate([dsh1, dsc1, dg1, dsh2, dsc2, dg2], axis=-1).reshape(bsz, N_MOD * D_MODEL)
    pack = _pack_partials(dmod, [d_pre_mix, d_post_mix, d_pre_mlp, d_post_mlp, d_attn_out, d_hg_norm, d_sinks], d_lb, loss_row)
    ((packs,),) = _comm_only([_plan_allgather8([pack])], "gather_small")
    w_small = dict(b_ada=b_ada, pre_w_mix=pre_w_mix, post_w_mix=post_w_mix, pre_w_mlp=pre_w_mlp, post_w_mlp=post_w_mlp,
                   attn_out_w=attn_out_w, hg_norm_w=hg_norm_w, attn_sinks=attn_sinks, lb_table=lb_table)
    m_small = dict(b_ada=m_b_ada, pre_w_mix=m_pre_w_mix, post_w_mix=m_post_w_mix, pre_w_mlp=m_pre_w_mlp, post_w_mlp=m_post_w_mlp,
                   attn_out_w=m_attn_out_w, hg_norm_w=m_hg_norm_w, attn_sinks=m_attn_sinks, lb_table=m_lb_table)
    v_small = dict(b_ada=v_b_ada, pre_w_mix=v_pre_w_mix, post_w_mix=v_post_w_mix, pre_w_mlp=v_pre_w_mlp, post_w_mlp=v_post_w_mlp,
                   attn_out_w=v_attn_out_w, hg_norm_w=v_hg_norm_w, attn_sinks=v_attn_sinks, lb_table=v_lb_table)
    *small_packed, loss_rows = _small_update(packs, _pack_small(w_small), _pack_small(m_small), _pack_small(v_small), bsz)
    small_out = [_unpack_small(p) for p in small_packed]
    loss = loss_rows[0, 0]

    dmod_all = packs[:, :bsz * MOD_ROWS, :].reshape(N_DEV * bsz, N_MOD * D_MODEL)
    dmod_cols = lax.dynamic_slice_in_dim(dmod_all, chip * ada_cols, ada_cols, axis=1)
    ada_out = _ada_bwd_adamw(c_all, dmod_cols, w_ada[0], m_w_ada[0], v_w_ada[0])

    s_in, x_in = _exchange_wait(in_sems, s_in, in_landing, [grad_x, ada_out[0]], "exchange_in_wait")
    half_in = _sum_chips(s_in, x_in, chip_idx, "sum_chips_in")
    ((their_in,),) = _comm_only([_plan_pair([half_in], False)], "pair_swap_in")
    big = dict(
        w_in=tuple(jnp.transpose(a) for a in _adamw_halves(half_in, their_in, core, w_in_t[0], m_in_t[0], v_in_t[0], axis=0,
                                                           name="adamw_in")),
        w_up=tuple(_adamw_halves(half_up, their_up, core, w_up[0], m_w_up[0], v_w_up[0], axis=0, name="adamw_up")),
        w_out=tuple(_adamw_halves(half_out, their_out, core, w_out[0], m_w_out[0], v_w_out[0], axis=1, name="adamw_out")),
        w_down=tuple(_adamw_halves(half_down, their_down, core, w_down[0], m_w_down[0], v_w_down[0], axis=0, name="adamw_down")),
        w_ada=tuple(ada_out),
    )
    order = ("w_ada", "b_ada", "pre_w_mix", "w_in", "attn_sinks", "attn_out_w", "lb_table", "hg_norm_w", "w_out", "post_w_mix",
             "pre_w_mlp", "w_up", "w_down", "post_w_mlp")
    outs = [loss, grad_x]
    for kind in range(4):
        for nm in order:
            outs.append(big[nm][kind][None] if nm in big else small_out[kind][nm])
    return tuple(outs)
```

```python
import jax
import jax.numpy as jnp
from jax import lax
from jax.experimental import pallas as pl
from jax.experimental.pallas import tpu as pltpu

F32 = jnp.float32
BF16 = jnp.bfloat16

D_MODEL = 1024
ATT_WIDTH = 512
ATT_HEAD_DIM = 64
ATT_Q_HEADS = 8
ATT_KV_HEADS = 2
ATT_GROUP = ATT_Q_HEADS // ATT_KV_HEADS
ATT_KV_COLS = ATT_KV_HEADS * ATT_HEAD_DIM
WINDOW = 128
ROPE_DIM = 16
ROPE_THETA = 500000.0
HG_WIDTH = 512
MIX_WIDTH = ATT_WIDTH + HG_WIDTH
HG_HEAD_DIM = 128
HG_HEADS = 4
HG_CHUNK = 32
IN_COLS = ATT_WIDTH + 2 * ATT_KV_COLS + 4 * HG_WIDTH
ATT_COLS = ATT_WIDTH + 2 * ATT_KV_COLS
D_FF = 4 * D_MODEL
N_MOD = 6
EPS = 1e-6
ATT_SCALE = ATT_HEAD_DIM ** -0.5

ADAM_LR = 0.001
ADAM_B1 = 0.9
ADAM_B2 = 0.999
ADAM_EPS = 1e-08
ADAM_WD = 0.01
ADAM_STEP = 10

N_CHIPS = 4
N_DEV = 8
LANE = 128
VMEM_LIMIT = 48 * 1024 * 1024
VMEM_LIMIT_BIG = 58 * 1024 * 1024
MESH = pl.DeviceIdType.MESH

NT_DIMS = (((1,), (1,)), ((), ()))
TN_DIMS = (((0,), (0,)), ((), ()))


def _sds(shape, dtype):
    return jax.ShapeDtypeStruct(tuple(shape), dtype)


def _params(*sem, vmem_limit=None):
    return pltpu.CompilerParams(dimension_semantics=sem, vmem_limit_bytes=VMEM_LIMIT if vmem_limit is None else vmem_limit)


def _sigmoid(x):
    return 1.0 / (1.0 + jnp.exp(-x))


def _dot(a, b, dims=None):
    a, b = a.astype(BF16), b.astype(BF16)
    if dims is None:
        return jnp.dot(a, b, preferred_element_type=F32)
    return lax.dot_general(a, b, dims, preferred_element_type=F32)


def _rms_fwd(x, w):
    rstd = lax.rsqrt(jnp.mean(x * x, axis=-1, keepdims=True) + EPS)
    xh = x * rstd
    return xh * w, xh, rstd


def _rms_bwd(dy, xh, rstd, w):
    dxh = dy * w
    dx = rstd * (dxh - xh * jnp.mean(dxh * xh, axis=-1, keepdims=True))
    return dx, dy * xh


def _colsum(x):
    return jnp.sum(x, axis=0, keepdims=True)


def _row_tile(rows, cap=256):
    return max(t for t in range(16, cap + 1, 16) if rows % t == 0)


HBM_SPEC = pl.BlockSpec(memory_space=pltpu.HBM)


def _mesh_pos():
    return lax.axis_index("x"), lax.axis_index("y"), lax.axis_index("c")


class _Comm:
    def __init__(self, ins, outs, sems, start, finish, aliases=()):
        self.ins, self.outs, self.sems = list(ins), list(outs), list(sems)
        self.start, self.finish, self.aliases = start, finish, tuple(aliases)


def _call(body, args, *, name, grid, in_specs, out_specs, out_shape, sem, scratch_shapes=(), comms=(), aliases=None,
          vmem_limit=None):
    scratch_shapes = list(scratch_shapes)
    if not comms:
        return pl.pallas_call(body, name=name, grid=grid, in_specs=in_specs, out_specs=out_specs, out_shape=out_shape,
                              input_output_aliases=dict(aliases or {}), scratch_shapes=scratch_shapes,
                              compiler_params=_params(*sem, vmem_limit=vmem_limit))(*args)
    single = not isinstance(out_shape, (list, tuple))
    out_specs_l = [out_specs] if single else list(out_specs)
    out_shape_l = [out_shape] if single else list(out_shape)
    n_in, n_out, n_scr = len(in_specs), len(out_shape_l), len(scratch_shapes)
    n_ci = [len(cm.ins) for cm in comms]
    n_co = [len(cm.outs) for cm in comms]
    n_cs = [len(cm.sems) for cm in comms]
    aliases = dict(aliases or {})
    for k, cm in enumerate(comms):
        for i, o in cm.aliases:
            aliases[n_in + sum(n_ci[:k]) + i] = n_out + sum(n_co[:k]) + o

    def fused(*refs):
        pos = [0]

        def take(n):
            part = refs[pos[0]:pos[0] + n]
            pos[0] += n
            return part

        ins = take(n_in)
        c_ins = [take(n) for n in n_ci]
        outs = take(n_out)
        c_outs = [take(n) for n in n_co]
        scr = take(n_scr)
        c_sems = [take(n) for n in n_cs]
        first, last = True, True
        for d, size in enumerate(grid):
            first = jnp.logical_and(first, pl.program_id(d) == 0)
            last = jnp.logical_and(last, pl.program_id(d) == size - 1)

        def run(which):
            for cm, ci, co, cs in zip(comms, c_ins, c_outs, c_sems):
                getattr(cm, which)(ci, co, cs)

        if grid:
            pl.when(first)(lambda: run("start"))
        else:
            run("start")
        body(*ins, *outs, *scr)
        if grid:
            pl.when(last)(lambda: run("finish"))
        else:
            run("finish")

    res = pl.pallas_call(
        fused, name=name, grid=grid, in_specs=list(in_specs) + [HBM_SPEC] * sum(n_ci),
        out_specs=out_specs_l + [HBM_SPEC] * sum(n_co), out_shape=out_shape_l + [s for cm in comms for s in cm.outs],
        input_output_aliases=aliases, scratch_shapes=scratch_shapes + [s for cm in comms for s in cm.sems],
        compiler_params=_params(*["arbitrary"] * len(grid), vmem_limit=vmem_limit),
    )(*args, *[a for cm in comms for a in cm.ins])
    main = res[:n_out]
    extra, at = [], n_out
    for n in n_co:
        extra.append(list(res[at:at + n]))
        at += n
    return (main[0] if single else list(main)), extra


def _mm(a, b, *, name, out_dtype, trans_b=False, tm=512, tn=None, extra=(), epi=None, b_spec=None, n_out=None, comms=()):
    m_total, k_total = a.shape
    if n_out is None:
        n_out = b.shape[0] if trans_b else b.shape[1]
    tn = n_out if tn is None else tn
    grid = (m_total // tm, n_out // tn)
    dims = NT_DIMS if trans_b else None

    def body(*refs):
        a_ref, b_ref = refs[0], refs[1]
        extra_refs = refs[2:2 + len(extra)]
        o_ref = refs[2 + len(extra)]
        acc = _dot(a_ref[...], b_ref[...], dims)
        if epi is not None:
            acc = epi(acc, *[r[...] for r in extra_refs])
        o_ref[...] = acc.astype(out_dtype)

    if b_spec is None:
        if trans_b:
            b_spec = pl.BlockSpec((tn, k_total), lambda i, j: (j, 0))
        else:
            b_spec = pl.BlockSpec((k_total, tn), lambda i, j: (0, j))
    in_specs = [pl.BlockSpec((tm, k_total), lambda i, j: (i, 0)), b_spec]
    in_specs += [pl.BlockSpec((tm, tn), lambda i, j: (i, j)) for _ in extra]
    return _call(
        body, (a, b, *extra), name=name, grid=grid, in_specs=in_specs,
        out_specs=pl.BlockSpec((tm, tn), lambda i, j: (i, j)),
        out_shape=_sds((m_total, n_out), out_dtype),
        sem=("parallel", "parallel"), comms=comms)


def _mm_tn(a, b, *, name, tk, tn, a_fn=None, out_shape=None, out_spec=None):
    m_total, k_total = a.shape
    n_total = b.shape[1]
    grid = (k_total // tk, n_total // tn)

    def body(a_ref, b_ref, o_ref):
        av = a_ref[...]
        part = _dot(av if a_fn is None else a_fn(av), b_ref[...], TN_DIMS)
        o_ref[...] = part.reshape(o_ref.shape)

    if out_shape is None:
        out_shape = _sds((k_total, n_total), F32)
        out_spec = pl.BlockSpec((tk, tn), lambda i, j: (i, j))
    return pl.pallas_call(
        body, name=name, grid=grid,
        in_specs=[pl.BlockSpec((m_total, tk), lambda i, j: (0, i)), pl.BlockSpec((m_total, tn), lambda i, j: (0, j))],
        out_specs=out_spec, out_shape=out_shape,
        compiler_params=_params("parallel", "parallel"),
    )(a, b)


def _ada_fwd(c_all, w_shard, b_shard):
    nb, ncol = c_all.shape[0], w_shard.shape[1]
    tn = 512

    def body(c_ref, w_ref, b_ref, o_ref):
        c = c_ref[...]
        o_ref[...] = _dot(c * _sigmoid(c), w_ref[...]) + b_ref[...]

    return pl.pallas_call(
        body, name="ada_fwd", grid=(ncol // tn,),
        in_specs=[pl.BlockSpec((nb, D_MODEL), lambda j: (0, 0)), pl.BlockSpec((D_MODEL, tn), lambda j: (0, j)),
                  pl.BlockSpec((1, tn), lambda j: (0, j))],
        out_specs=pl.BlockSpec((nb, tn), lambda j: (0, j)), out_shape=_sds((nb, ncol), F32),
        compiler_params=_params("parallel"),
    )(c_all, w_shard, b_shard)


def _adamw_math(g, w, m, v):
    m = ADAM_B1 * m + (1.0 - ADAM_B1) * g
    v = ADAM_B2 * v + (1.0 - ADAM_B2) * (g * g)
    m_hat = m / (1.0 - ADAM_B1 ** ADAM_STEP)
    v_hat = v / (1.0 - ADAM_B2 ** ADAM_STEP)
    delta = -ADAM_LR * (m_hat / (jnp.sqrt(v_hat) + ADAM_EPS) + ADAM_WD * w)
    return delta, m, v


def _ada_bwd_adamw(c_all, dmod_cols, w, m, v):
    nb, ncol = dmod_cols.shape
    tn = 256

    def body(c_ref, d_ref, w_ref, m_ref, v_ref, g_ref, dl_ref, nm_ref, nv_ref):
        c = c_ref[...]
        g = _dot(c * _sigmoid(c), d_ref[...], TN_DIMS)
        g_ref[...] = g
        dl_ref[...], nm_ref[...], nv_ref[...] = _adamw_math(g, w_ref[...], m_ref[...], v_ref[...])

    col = pl.BlockSpec((D_MODEL, tn), lambda j: (0, j))
    shp = _sds((D_MODEL, ncol), F32)
    return pl.pallas_call(
        body, name="ada_bwd_adamw", grid=(ncol // tn,),
        in_specs=[pl.BlockSpec((nb, D_MODEL), lambda j: (0, 0)), pl.BlockSpec((nb, tn), lambda j: (0, j)), col, col, col],
        out_specs=[col, col, col, col], out_shape=[shp, shp, shp, shp],
        compiler_params=_params("parallel"),
    )(c_all, dmod_cols, w, m, v)


def _adamw_halves(own, theirs, core, w, m, v, *, axis, name):
    r2, c2 = own.shape
    tr = _row_tile(r2)
    nt = r2 // tr

    def body(core_ref, own_ref, their_ref, w_ref, m_ref, v_ref, g_ref, dl_ref, nm_ref, nv_ref):
        g = jnp.where(pl.program_id(0) == core_ref[0], own_ref[...], their_ref[...])
        g_ref[...] = g
        dl_ref[...], nm_ref[...], nv_ref[...] = _adamw_math(g, w_ref[...], m_ref[...], v_ref[...])

    if axis == 0:
        full = pl.BlockSpec((tr, c2), lambda h, i, core_ref: (h * nt + i, 0))
    else:
        full = pl.BlockSpec((tr, c2), lambda h, i, core_ref: (i, h))
    half = pl.BlockSpec((tr, c2), lambda h, i, core_ref: (i, 0))
    shp = _sds(w.shape, F32)
    return pl.pallas_call(
        body, name=name,
        grid_spec=pltpu.PrefetchScalarGridSpec(num_scalar_prefetch=1, grid=(2, nt), in_specs=[half, half, full, full, full],
                                               out_specs=[full] * 4),
        out_shape=[shp] * 4, compiler_params=_params("parallel", "parallel"),
    )(core, own, theirs, w, m, v)


def _tok_spec(tm, width=D_MODEL):
    return pl.BlockSpec((None, tm, width), lambda b, i: (b, i, 0))


def _row_spec(width=D_MODEL):
    return pl.BlockSpec((None, 1, width), lambda b, i: (b, 0, 0))


def _vec_spec(width=D_MODEL):
    return pl.BlockSpec((1, width), lambda b, i: (0, 0))


class _RowsOf:
    def __init__(self, ref, first, count):
        self.ref, self.rows = ref, slice(first, first + count)

    def __getitem__(self, idx):
        return self.ref[self.rows, :]

    def __setitem__(self, idx, value):
        self.ref[self.rows, :] = value


def _mm_rows(a, b, *, name, tm, extra, extra_specs, out_specs, out_shape, epi, pro=None, trans_b=False, b_chunks=1, comms=(),
             parts=1, zero_per_seq=(), zero_once=(), vmem_limit=None):
    bsz, seq, k_total = a.shape
    kc = k_total // b_chunks
    dims = NT_DIMS if trans_b else None
    rows = tm // parts

    def body(*refs):
        a_ref, b_ref = refs[0], refs[1]
        ex, outs = refs[2:2 + len(extra)], refs[2 + len(extra):]
        if zero_per_seq:
            @pl.when(pl.program_id(1) == 0)
            def _():
                for k in zero_per_seq:
                    outs[k][...] = jnp.zeros_like(outs[k])
        if zero_once:
            @pl.when(jnp.logical_and(pl.program_id(0) == 0, pl.program_id(1) == 0))
            def _():
                for k in zero_once:
                    outs[k][...] = jnp.zeros_like(outs[k])

        def part_of(ref, p):
            tiled = len(ref.shape) == 2 and ref.shape[0] == tm
            return _RowsOf(ref, p * rows, rows) if tiled and parts > 1 else ref

        accs = []
        for p in range(parts):
            a_p, ex_p, outs_p = part_of(a_ref, p), [part_of(r, p) for r in ex], [part_of(r, p) for r in outs]
            if b_chunks == 1:
                accs.append(_dot(a_p[...] if pro is None else pro(a_p, ex_p, outs_p), b_ref[...], dims))
            else:
                acc = _dot(a_p[...][:, 0:kc], b_ref[0], NT_DIMS)
                for k in range(1, b_chunks):
                    acc = acc + _dot(a_p[...][:, k * kc:(k + 1) * kc], b_ref[k], NT_DIMS)
                accs.append(acc)
        for p in range(parts):
            epi(accs[p], [part_of(r, p) for r in ex], [part_of(r, p) for r in outs])

    b_spec = pl.BlockSpec(b.shape, lambda bb, i: (0,) * b.ndim)
    return _call(
        body, (a, b, *extra), name=name, grid=(bsz, seq // tm), in_specs=[_tok_spec(tm, k_total), b_spec, *extra_specs],
        out_specs=out_specs, out_shape=out_shape, sem=("arbitrary", "arbitrary"), comms=comms, vmem_limit=vmem_limit)


def _in_proj_fused(x, w, sc, sh, w_in_t, tables, comms=()):
    tm = 512
    bsz, seq, _ = x.shape
    half = ROPE_DIM // 2
    heads_per_slab = LANE // ATT_HEAD_DIM

    def pro(x_ref, ex, outs):
        y, _, _ = _rms_fwd(x_ref[...], ex[0][...])
        h = (y * (1.0 + ex[1][...]) + ex[2][...]).astype(BF16)
        outs[0][...] = h
        return h

    def epi(acc, ex, outs):
        c, u, d = ex[3][...], ex[4][...], ex[5][...]
        _, proj_ref, q_ref, k_ref, v_ref = outs
        proj_ref[...] = acc

        def rope(z):
            return (z * c + pltpu.roll(z, half, 1) * u + pltpu.roll(z, LANE - half, 1) * d).astype(BF16)

        for s in range(ATT_WIDTH // LANE):
            slab = rope(acc[:, s * LANE:(s + 1) * LANE])
            for part in range(heads_per_slab):
                g, hh = divmod(s * heads_per_slab + part, ATT_GROUP)
                piece = slab[:, part * ATT_HEAD_DIM:(part + 1) * ATT_HEAD_DIM]
                for blk in range(tm // WINDOW):
                    q_ref[blk, g, hh * WINDOW:(hh + 1) * WINDOW, :] = piece[blk * WINDOW:(blk + 1) * WINDOW]
        rk = rope(acc[:, ATT_WIDTH:ATT_WIDTH + LANE])
        vv = acc[:, ATT_WIDTH + LANE:ATT_COLS].astype(BF16)
        for g in range(ATT_KV_HEADS):
            k_ref[g] = rk[:, g * ATT_HEAD_DIM:(g + 1) * ATT_HEAD_DIM]
            v_ref[g] = vv[:, g * ATT_HEAD_DIM:(g + 1) * ATT_HEAD_DIM]

    cols = w_in_t.shape[0]
    tab = pl.BlockSpec((tm, LANE), lambda b, i: (i, 0))
    kv_spec = pl.BlockSpec((None, ATT_KV_HEADS, tm, ATT_HEAD_DIM), lambda b, i: (b, 0, i, 0))
    kv_shape = _sds((bsz, ATT_KV_HEADS, seq, ATT_HEAD_DIM), BF16)
    q_spec = pl.BlockSpec((None, tm // WINDOW, ATT_KV_HEADS, GROUP_ROWS, ATT_HEAD_DIM), lambda b, i: (b, i, 0, 0, 0))
    return _mm_rows(x, w_in_t, name="in_proj", tm=tm, extra=(w, sc, sh, *tables),
                    extra_specs=[_vec_spec(), _row_spec(), _row_spec(), tab, tab, tab],
                    out_specs=[_tok_spec(tm), _tok_spec(tm, cols), q_spec, kv_spec, kv_spec],
                    out_shape=[_sds(x.shape, BF16), _sds((bsz, seq, cols), F32),
                               _sds((bsz, seq // WINDOW, ATT_KV_HEADS, GROUP_ROWS, ATT_HEAD_DIM), BF16), kv_shape, kv_shape],
                    pro=pro, epi=epi, trans_b=True, comms=comms)


def _rope_tables(seq):
    half = ROPE_DIM // 2
    inv_freq = ROPE_THETA ** (-jnp.arange(0, ROPE_DIM, 2, dtype=F32) / ROPE_DIM)
    ang = jnp.arange(seq, dtype=F32)[:, None] * inv_freq[None, :]
    cos, sin = jnp.cos(ang), jnp.sin(ang)
    rest = ATT_HEAD_DIM - ROPE_DIM
    ones, zeros, zh = jnp.ones((seq, rest), F32), jnp.zeros((seq, rest), F32), jnp.zeros((seq, half), F32)
    reps = LANE // ATT_HEAD_DIM
    t_cos = jnp.tile(jnp.concatenate([cos, cos, ones], axis=1), (1, reps))
    t_up = jnp.tile(jnp.concatenate([zh, sin, zeros], axis=1), (1, reps))
    t_dn = jnp.tile(jnp.concatenate([-sin, zh, zeros], axis=1), (1, reps))
    return t_cos, t_up, t_dn


GROUP_ROWS = ATT_GROUP * WINDOW


ATT_BPS = 2


def _band_mask(has_prev):
    row = lax.broadcasted_iota(jnp.int32, (GROUP_ROWS, 2 * WINDOW), 0) % WINDOW
    col = lax.broadcasted_iota(jnp.int32, (GROUP_ROWS, 2 * WINDOW), 1)
    prev = jnp.logical_and(jnp.logical_and(col < WINDOW, col > row), has_prev)
    return jnp.logical_or(prev, jnp.logical_and(col >= WINDOW, col - WINDOW <= row))


def _sink_column(sink_ref, g):
    head = lax.broadcasted_iota(jnp.int32, (GROUP_ROWS, 1), 0) // WINDOW
    col = jnp.full((GROUP_ROWS, 1), sink_ref[0, g * ATT_GROUP], F32)
    for hh in range(1, ATT_GROUP):
        col = jnp.where(head == hh, sink_ref[0, g * ATT_GROUP + hh], col)
    return col


def _attn_specs():
    q_spec = pl.BlockSpec((None, ATT_BPS, ATT_KV_HEADS, GROUP_ROWS, ATT_HEAD_DIM), lambda b, i: (b, i, 0, 0, 0))
    kv_cur = pl.BlockSpec((None, ATT_KV_HEADS, ATT_BPS * WINDOW, ATT_HEAD_DIM), lambda b, i: (b, 0, i, 0))
    kv_prev = pl.BlockSpec((None, ATT_KV_HEADS, WINDOW, ATT_HEAD_DIM), lambda b, i: (b, 0, jnp.maximum(ATT_BPS * i - 1, 0), 0))
    return q_spec, kv_cur, kv_prev


def _band(prev_ref, cur_ref, g, blk):
    own = cur_ref[g, blk * WINDOW:(blk + 1) * WINDOW]
    before = prev_ref[g] if blk == 0 else cur_ref[g, (blk - 1) * WINDOW:blk * WINDOW]
    return jnp.concatenate([before, own], axis=0)


def _attn_fwd(qh, kh, vh, sinks, w_norm, comms=()):
    bsz, nblk = qh.shape[0], qh.shape[1]
    seq = nblk * WINDOW
    rows = ATT_BPS * WINDOW
    neg = float(jnp.finfo(jnp.float32).min)

    def body(sink_ref, q_ref, kc_ref, kp_ref, vc_ref, vp_ref, w_ref, raw_ref, an_ref, l_ref):
        for blk in range(ATT_BPS):
            mask = _band_mask(True if blk else pl.program_id(1) > 0)
            for g in range(ATT_KV_HEADS):
                keys, vals = _band(kp_ref, kc_ref, g, blk), _band(vp_ref, vc_ref, g, blk)
                sink = _sink_column(sink_ref, g)
                s = jnp.where(mask, _dot(q_ref[blk, g], keys, NT_DIMS) * ATT_SCALE, neg)
                m = jnp.maximum(jnp.max(s, axis=-1, keepdims=True), sink)
                p = jnp.where(mask, jnp.exp(s - m), 0.0)
                den = jnp.sum(p, axis=-1, keepdims=True) + jnp.exp(sink - m)
                o = _dot(p / den, vals)
                lse = m + jnp.log(den)
                tok = slice(blk * WINDOW, (blk + 1) * WINDOW)
                for hh in range(ATT_GROUP):
                    h = g * ATT_GROUP + hh
                    raw_ref[tok, h * ATT_HEAD_DIM:(h + 1) * ATT_HEAD_DIM] = o[hh * WINDOW:(hh + 1) * WINDOW]
                    l_ref[tok, h:h + 1] = lse[hh * WINDOW:(hh + 1) * WINDOW]
        y, _, _ = _rms_fwd(raw_ref[...], w_ref[...])
        an_ref[...] = y.astype(BF16)

    cur = lambda width: pl.BlockSpec((None, rows, width), lambda b, i: (b, i, 0))
    q_spec, kv_cur, kv_prev = _attn_specs()
    return _call(
        body, (sinks, qh, kh, kh, vh, vh, w_norm), name="attn_fwd", grid=(bsz, nblk // ATT_BPS),
        in_specs=[pl.BlockSpec(memory_space=pltpu.SMEM), q_spec, kv_cur, kv_prev, kv_cur, kv_prev, _vec_spec(ATT_WIDTH)],
        out_specs=[cur(ATT_WIDTH), cur(ATT_WIDTH), cur(ATT_Q_HEADS)],
        out_shape=[_sds((bsz, seq, ATT_WIDTH), F32), _sds((bsz, seq, MIX_WIDTH), BF16), _sds((bsz, seq, ATT_Q_HEADS), F32)],
        sem=("parallel", "parallel"), comms=comms)


HG_Q0 = ATT_COLS // LANE
HG_F0 = HG_Q0 + HG_HEADS
HG_I0 = HG_F0 + HG_HEADS
HG_G0 = HG_I0 + HG_HEADS
HG_TOK = 256
HG_NCH = HG_TOK // HG_CHUNK
HG_HPS = 2


def _block_masks():
    row = lax.broadcasted_iota(jnp.int32, (HG_TOK, HG_TOK), 0)
    col = lax.broadcasted_iota(jnp.int32, (HG_TOK, HG_TOK), 1)
    same = (row // HG_CHUNK) == (col // HG_CHUNK)
    return jnp.logical_and(same, col <= row), jnp.logical_and(same, col >= row)


def _row_in_chunk():
    return lax.broadcasted_iota(jnp.int32, (HG_TOK, LANE), 0) % HG_CHUNK


def _chunk_cumsum(x, reverse=False):
    ric = _row_in_chunk()
    shift = 1
    while shift < HG_CHUNK:
        if reverse:
            x = x + jnp.where(ric < HG_CHUNK - shift, pltpu.roll(x, HG_TOK - shift, 0), 0.0)
        else:
            x = x + jnp.where(ric >= shift, pltpu.roll(x, shift, 0), 0.0)
        shift *= 2
    return x


def _chunk_rows(rows):
    stacked = jnp.concatenate([r[None] for r in rows], axis=0)
    return jnp.broadcast_to(stacked, (HG_NCH, HG_CHUNK, LANE)).reshape(HG_TOK, LANE)


def _chunk_slices(x):
    return [x[j * HG_CHUNK:(j + 1) * HG_CHUNK] for j in range(HG_NCH)]


def _hgrn_common(tbl, hf, hq):
    lb = _sigmoid(tbl[1:2] - tbl[0:1])
    sig = _sigmoid(hf)
    f = lb + (1.0 - lb) * sig
    sq = _sigmoid(hq)
    q, k = hq * sq, 1.0 - f
    b = _chunk_cumsum(jnp.log(f))
    last = [b[(j + 1) * HG_CHUNK - 1:(j + 1) * HG_CHUNK] for j in range(HG_NCH)]
    bl = _chunk_rows(last)
    e_b, e_nb, e_rem = jnp.exp(b), jnp.exp(-b), jnp.exp(bl - b)
    e_last = [jnp.exp(r) for r in last]
    return dict(lb=lb, sig=sig, f=f, sq=sq, q=q, k=k, e_b=e_b, e_nb=e_nb, e_rem=e_rem, e_last=e_last,
                qd=q * e_b, kd=k * e_nb, ku=k * e_rem)


def _hgrn_fwd(proj, lb_table, norm_w, mix_in, comms=()):
    bsz, seq, _ = proj.shape
    nstep = seq // HG_TOK

    def body(tbl_ref, nw_ref, q_ref, f_ref, i_ref, g_ref, mix_ref, o_ref, rec_ref, st_ref, s_scr):
        @pl.when(pl.program_id(2) == 0)
        def _():
            s_scr[...] = jnp.zeros_like(s_scr)

        lower, _ = _block_masks()
        for hp in range(HG_HPS):
            ls = slice(hp * LANE, (hp + 1) * LANE)
            v, hg = i_ref[:, ls], g_ref[:, ls]
            t = _hgrn_common(tbl_ref[:, ls], f_ref[:, ls], q_ref[:, ls])
            a = jnp.where(lower, _dot(t["qd"], t["kd"], NT_DIMS), 0.0)
            o_intra = _dot(a, v)
            v_c, ku_c, qd_c = [_chunk_slices(z.astype(BF16)) for z in (v, t["ku"], t["qd"])]
            updates = [_dot(v_c[j], ku_c[j], TN_DIMS) for j in range(HG_NCH)]
            st = s_scr[hp]
            states = []
            for j in range(HG_NCH):
                states.append(st)
                st = st * t["e_last"][j] + updates[j]
            s_scr[hp] = st
            o = o_intra + jnp.concatenate([_dot(qd_c[j], states[j], NT_DIMS) for j in range(HG_NCH)], axis=0)
            for j in range(HG_NCH):
                st_ref[hp, j] = states[j]
            o_ref[:, ls] = o
            y, _, _ = _rms_fwd(o, nw_ref[...])
            rec_ref[:, ls] = (y * (hg * _sigmoid(hg))).astype(BF16)

    width = HG_HPS * LANE
    slab = lambda first: pl.BlockSpec((None, HG_TOK, width), lambda b, h, t: (b, t, first // HG_HPS + h))
    head_out = pl.BlockSpec((None, HG_TOK, width), lambda b, h, t: (b, t, h))
    mix_out = pl.BlockSpec((None, HG_TOK, width), lambda b, h, t: (b, t, ATT_WIDTH // width + h))
    return _call(
        body, (lb_table, norm_w, proj, proj, proj, proj, mix_in), name="hgrn_fwd", grid=(bsz, HG_HEADS // HG_HPS, nstep),
        in_specs=[pl.BlockSpec((2, width), lambda b, h, t: (0, h)), pl.BlockSpec((1, LANE), lambda b, h, t: (0, 0)),
                  slab(HG_Q0), slab(HG_F0), slab(HG_I0), slab(HG_G0), pl.BlockSpec(memory_space=pl.ANY)],
        out_specs=[head_out, mix_out,
                   pl.BlockSpec((None, HG_HPS, HG_NCH, LANE, LANE), lambda b, h, t: (b, h, t, 0, 0))],
        out_shape=[_sds((bsz, seq, HG_WIDTH), F32), _sds(mix_in.shape, BF16),
                   _sds((bsz, HG_HEADS, seq // HG_CHUNK, LANE, LANE), F32)],
        scratch_shapes=[pltpu.VMEM((HG_HPS, LANE, LANE), F32)],
        sem=("parallel", "parallel", "arbitrary"), comms=comms, aliases={6: 1})


def _out_proj_fused(cat, w_out, x, post_w, g1, pre_w, sc2, sh2):
    tm = 512

    def epi(mix, ex, outs):
        x_ref, pw_ref, g1_ref, w2_ref, sc_ref, sh_ref = ex
        outs[0][...] = mix
        n1, _, _ = _rms_fwd(mix, pw_ref[...])
        x1 = x_ref[...] + g1_ref[...] * n1
        outs[1][...] = x1
        y2, _, _ = _rms_fwd(x1, w2_ref[...])
        outs[2][...] = (y2 * (1.0 + sc_ref[...]) + sh_ref[...]).astype(BF16)

    return _mm_rows(cat, w_out, name="out_proj", tm=tm, extra=(x, post_w, g1, pre_w, sc2, sh2),
                    extra_specs=[_tok_spec(tm), _vec_spec(), _row_spec(), _vec_spec(), _row_spec(), _row_spec()],
                    out_specs=[_tok_spec(tm), _tok_spec(tm), _tok_spec(tm)],
                    out_shape=[_sds(x.shape, F32), _sds(x.shape, F32), _sds(x.shape, BF16)], epi=epi)


def _acc_out(ref, first, value):
    @pl.when(first)
    def _():
        ref[...] = value

    @pl.when(jnp.logical_not(first))
    def _():
        ref[...] += value


def _down_proj_fused(r, w_down, x1, post_w, g2, target):
    tm = 512
    bsz = x1.shape[0]

    def pro(r_ref, ex, outs):
        rv = r_ref[...]
        return rv * rv

    def epi(down, ex, outs):
        x1_ref, w_ref, g2_ref, t_ref = ex
        loss_ref, dy_ref, dd_ref, dg2_ref, dw_ref = outs
        w, g2v = w_ref[...], g2_ref[...]
        n2, dh, rstd = _rms_fwd(down, w)
        err = x1_ref[...] + g2v * n2 - t_ref[...]
        part = (0.5 / D_MODEL) * jnp.sum(jnp.sum(err * err, axis=-1, keepdims=True), axis=0, keepdims=True)
        loss_ref[...] += jnp.broadcast_to(part, (1, LANE))
        dy = err * (1.0 / D_MODEL)
        dy_ref[...] = dy
        dg2_ref[...] += _colsum(dy * n2)
        dd, dw_rows = _rms_bwd(dy * g2v, dh, rstd, w)
        dd_ref[...] = dd.astype(BF16)
        dw_ref[...] += _colsum(dw_rows)

    return _mm_rows(r, w_down, name="down_proj", tm=tm, extra=(x1, post_w, g2, target),
                    extra_specs=[_tok_spec(tm), _vec_spec(), _row_spec(), _tok_spec(tm)],
                    out_specs=[_vec_spec(LANE), _tok_spec(tm), _tok_spec(tm), _row_spec(), _vec_spec()],
                    out_shape=[_sds((1, LANE), F32), _sds(x1.shape, F32), _sds(x1.shape, BF16), _sds((bsz, 1, D_MODEL), F32),
                               _sds((1, D_MODEL), F32)], pro=pro, epi=epi, parts=2, zero_per_seq=(3,), zero_once=(0, 4),
                    vmem_limit=VMEM_LIMIT_BIG)


def _up_bwd_fused(dpre, w_up4, dy, x1, mix, pre_w, sc2, post_w, g1, comms=()):
    tm = 512
    bsz = x1.shape[0]

    def epi(dh2v, ex, outs):
        dy_ref, x1_ref, mix_ref, w2_ref, sc_ref, pw_ref, g1_ref = ex
        dx1_ref, dmix_ref, dsc_ref, dsh_ref, dg1_ref, dw2_ref, dpw_ref = outs
        w2, pw = w2_ref[...], pw_ref[...]
        y2, xh2, rstd2 = _rms_fwd(x1_ref[...], w2)
        dsh_ref[...] += _colsum(dh2v)
        dsc_ref[...] += _colsum(dh2v * y2)
        dx1n, dw_rows = _rms_bwd(dh2v * (1.0 + sc_ref[...]), xh2, rstd2, w2)
        dw2_ref[...] += _colsum(dw_rows)
        dx1 = dy_ref[...] + dx1n
        dx1_ref[...] = dx1
        n1, mh, rstd1 = _rms_fwd(mix_ref[...], pw)
        dg1_ref[...] += _colsum(dx1 * n1)
        dmix, dpw_rows = _rms_bwd(dx1 * g1_ref[...], mh, rstd1, pw)
        dmix_ref[...] = dmix.astype(BF16)
        dpw_ref[...] += _colsum(dpw_rows)

    row_shape = _sds((bsz, 1, D_MODEL), F32)
    vec_shape = _sds((1, D_MODEL), F32)
    return _mm_rows(dpre, w_up4, name="up_bwd", tm=tm, extra=(dy, x1, mix, pre_w, sc2, post_w, g1),
                    extra_specs=[_tok_spec(tm), _tok_spec(tm), _tok_spec(tm), _vec_spec(), _row_spec(), _vec_spec(), _row_spec()],
                    out_specs=[_tok_spec(tm), _tok_spec(tm), _row_spec(), _row_spec(), _row_spec(), _vec_spec(), _vec_spec()],
                    out_shape=[_sds(x1.shape, F32), _sds(x1.shape, BF16), row_shape, row_shape, row_shape, vec_shape, vec_shape],
                    epi=epi, b_chunks=w_up4.shape[0], comms=comms, parts=2, zero_per_seq=(2, 3, 4), zero_once=(5, 6),
                    vmem_limit=VMEM_LIMIT_BIG)


def _norm1_bwd(dh1, dx1, x, pre_w, sc1, tm=512, comms=()):
    bsz, seq, _ = x.shape

    def body(dh_ref, dx1_ref, x_ref, w_ref, sc_ref, gx_ref, dsc_ref, dsh_ref, dw_ref):
        b, i = pl.program_id(0), pl.program_id(1)
        w = w_ref[...]
        dh = dh_ref[...]
        y, xh, rstd = _rms_fwd(x_ref[...], w)
        _acc_out(dsh_ref, i == 0, _colsum(dh))
        _acc_out(dsc_ref, i == 0, _colsum(dh * y))
        dx, dw_rows = _rms_bwd(dh * (1.0 + sc_ref[...]), xh, rstd, w)
        _acc_out(dw_ref, jnp.logical_and(b == 0, i == 0), _colsum(dw_rows))
        gx_ref[...] = dx1_ref[...] + dx

    row_shape = _sds((bsz, 1, D_MODEL), F32)
    return _call(
        body, (dh1, dx1, x, pre_w, sc1), name="norm1_bwd", grid=(bsz, seq // tm),
        in_specs=[_tok_spec(tm), _tok_spec(tm), _tok_spec(tm), _vec_spec(), _row_spec()],
        out_specs=[_tok_spec(tm), _row_spec(), _row_spec(), _vec_spec()],
        out_shape=[_sds(x.shape, F32), row_shape, row_shape, _sds((1, D_MODEL), F32)],
        sem=("arbitrary", "arbitrary"), comms=comms)


def _hgrn_bwd(dcat, proj, o_raw, states, lb_table, norm_w, comms=()):
    bsz, seq, _ = proj.shape
    nstep = seq // HG_TOK
    rec0 = ATT_WIDTH // LANE

    def body(tbl_ref, nw_ref, dr_ref, q_ref, f_ref, i_ref, g_ref, o_ref, st_ref,
             dq_ref, df_ref, di_ref, dg_ref, dlb_ref, dnw_ref, ds_scr):
        h, b, t = pl.program_id(0), pl.program_id(1), pl.program_id(2)

        @pl.when(t == 0)
        def _():
            ds_scr[...] = jnp.zeros_like(ds_scr)

        lower, upper = _block_masks()
        dlb_parts = []
        dnw_acc = jnp.zeros((1, LANE), F32)
        for hp in range(HG_HPS):
            ls = slice(hp * LANE, (hp + 1) * LANE)
            hq, v, hg = q_ref[:, ls], i_ref[:, ls], g_ref[:, ls]
            nw = nw_ref[...]
            c = _hgrn_common(tbl_ref[:, ls], f_ref[:, ls], hq)
            qd, kd, ku = c["qd"], c["kd"], c["ku"]
            y, on, rstd = _rms_fwd(o_ref[:, ls], nw)
            sg = _sigmoid(hg)
            dr = dr_ref[:, ls]
            dg_ref[:, ls] = (dr * y * (sg * (1.0 + hg * (1.0 - sg)))).astype(BF16)
            do, dnw_rows = _rms_bwd(dr * (hg * sg), on, rstd, nw)
            at = jnp.where(upper, _dot(kd, qd, NT_DIMS), 0.0)
            da = jnp.where(lower, _dot(do, v, NT_DIMS), 0.0)
            dat = jnp.where(upper, _dot(v, do, NT_DIMS), 0.0)
            dv = _dot(at, do)
            dqd = _dot(da, kd)
            dkd = _dot(dat, qd)
            do_c, qd_c, v_c, ku_c = [_chunk_slices(z.astype(BF16)) for z in (do, qd, v, ku)]
            outer = [_dot(do_c[j], qd_c[j], TN_DIMS) for j in range(HG_NCH)]
            ds = ds_scr[hp]
            ds_after = [None] * HG_NCH
            for j in reversed(range(HG_NCH)):
                ds_after[j] = ds
                ds = outer[j] + ds * c["e_last"][j]
            ds_scr[hp] = ds
            states = [st_ref[hp, j] for j in range(HG_NCH)]
            dv = dv + jnp.concatenate([_dot(ku_c[j], ds_after[j], NT_DIMS) for j in range(HG_NCH)], axis=0)
            dqd = dqd + jnp.concatenate([_dot(do_c[j], states[j]) for j in range(HG_NCH)], axis=0)
            dku = jnp.concatenate([_dot(v_c[j], ds_after[j]) for j in range(HG_NCH)], axis=0)
            dku_ku = dku * ku
            dbl = [_colsum(states[j] * ds_after[j]) * c["e_last"][j] + _colsum(dku_ku[j * HG_CHUNK:(j + 1) * HG_CHUNK])
                   for j in range(HG_NCH)]
            dk = dkd * c["e_nb"] + dku * c["e_rem"]
            db = dqd * qd - dkd * kd - dku_ku + jnp.where(_row_in_chunk() == HG_CHUNK - 1, _chunk_rows(dbl), 0.0)
            dfv = _chunk_cumsum(db, reverse=True) / c["f"] - dk
            sig, sq = c["sig"], c["sq"]
            df_ref[:, ls] = (dfv * (1.0 - c["lb"]) * sig * (1.0 - sig)).astype(BF16)
            dq_ref[:, ls] = (dqd * c["e_b"] * (sq * (1.0 + hq * (1.0 - sq)))).astype(BF16)
            di_ref[:, ls] = dv.astype(BF16)
            dlb_parts.append(_colsum(dfv * (1.0 - sig)))
            dnw_acc = dnw_acc + _colsum(dnw_rows)
        _acc_out(dlb_ref, jnp.logical_and(b == 0, t == 0), jnp.concatenate(dlb_parts, axis=1))
        _acc_out(dnw_ref, jnp.logical_and(h == 0, jnp.logical_and(b == 0, t == 0)), dnw_acc)

    rev = lambda t: nstep - 1 - t
    width = HG_HPS * LANE
    slab = lambda first: pl.BlockSpec((None, HG_TOK, width), lambda h, b, t: (b, rev(t), first // HG_HPS + h))
    head = pl.BlockSpec((None, HG_TOK, width), lambda h, b, t: (b, rev(t), h))
    grad_shape = _sds((bsz, seq, HG_WIDTH), BF16)
    return _call(
        body, (lb_table, norm_w, dcat, proj, proj, proj, proj, o_raw, states), name="hgrn_bwd",
        grid=(HG_HEADS // HG_HPS, bsz, nstep),
        in_specs=[pl.BlockSpec((2, width), lambda h, b, t: (0, h)), pl.BlockSpec((1, LANE), lambda h, b, t: (0, 0)),
                  slab(rec0), slab(HG_Q0), slab(HG_F0), slab(HG_I0), slab(HG_G0), head,
                  pl.BlockSpec((None, HG_HPS, HG_NCH, LANE, LANE), lambda h, b, t: (b, h, rev(t), 0, 0))],
        out_specs=[head, head, head, head, pl.BlockSpec((1, width), lambda h, b, t: (0, h)),
                   pl.BlockSpec((1, LANE), lambda h, b, t: (0, 0))],
        out_shape=[grad_shape, grad_shape, grad_shape, grad_shape, _sds((1, HG_WIDTH), F32), _sds((1, LANE), F32)],
        scratch_shapes=[pltpu.VMEM((HG_HPS, LANE, LANE), F32)],
        sem=("arbitrary", "arbitrary", "arbitrary"), comms=comms)


def _attn_bwd(dcat, raw, w_norm, qh, kh, vh, lse, sinks, tables, comms=()):
    bsz, nblk = qh.shape[0], qh.shape[1]
    seq = nblk * WINDOW
    nstep = nblk // ATT_BPS
    half = ROPE_DIM // 2

    def body(sink_ref, da_ref, raw_ref, w_ref, q_ref, kc_ref, kp_ref, vc_ref, vp_ref, l_ref, c_ref, u_ref, d_ref,
             o_ref, dw_ref, dsink_ref, carry_k, carry_v):
        b, i = pl.program_id(0), pl.program_id(1)
        first = jnp.logical_and(b == 0, i == 0)

        @pl.when(i == 0)
        def _():
            carry_k[...] = jnp.zeros_like(carry_k)
            carry_v[...] = jnp.zeros_like(carry_v)

        w = w_ref[...]
        _, on, rstd = _rms_fwd(raw_ref[...], w)
        do_step, dw_rows = _rms_bwd(da_ref[...], on, rstd, w)
        _acc_out(dw_ref, first, _colsum(dw_rows))
        lane8 = lax.broadcasted_iota(jnp.int32, (1, ATT_Q_HEADS), 1)
        dsink = jnp.zeros((1, ATT_Q_HEADS), F32)
        from_next_k, from_next_v = carry_k[...], carry_v[...]
        for blk in reversed(range(ATT_BPS)):
            tok = slice(blk * WINDOW, (blk + 1) * WINDOW)
            mask = _band_mask(True if blk else i < nstep - 1)
            raw_v, do_all = raw_ref[tok, :], do_step[tok]
            c, u, d = c_ref[tok, :], u_ref[tok, :], d_ref[tok, :]

            def unrope(g):
                return (g * c + pltpu.roll(g * u, LANE - half, 1) + pltpu.roll(g * d, half, 1)).astype(BF16)

            dq_parts, dk_own, dk_before, dv_own, dv_before = [], [], [], [], []
            for g in range(ATT_KV_HEADS):
                heads = [slice((g * ATT_GROUP + hh) * ATT_HEAD_DIM, (g * ATT_GROUP + hh + 1) * ATT_HEAD_DIM)
                         for hh in range(ATT_GROUP)]
                q = q_ref[blk, g]
                keys, vals = _band(kp_ref, kc_ref, g, blk), _band(vp_ref, vc_ref, g, blk)
                do_g = jnp.concatenate([do_all[:, hs] for hs in heads], axis=0)
                dsum = jnp.concatenate([jnp.sum(do_all[:, hs] * raw_v[:, hs], axis=-1, keepdims=True) for hs in heads], axis=0)
                lse_g = jnp.concatenate([l_ref[tok, g * ATT_GROUP + hh:g * ATT_GROUP + hh + 1] for hh in range(ATT_GROUP)], axis=0)
                p = jnp.where(mask, jnp.exp(_dot(q, keys, NT_DIMS) * ATT_SCALE - lse_g), 0.0)
                sink_part = jnp.exp(_sink_column(sink_ref, g) - lse_g) * dsum
                for hh in range(ATT_GROUP):
                    head_sum = jnp.sum(sink_part[hh * WINDOW:(hh + 1) * WINDOW], axis=0, keepdims=True)
                    dsink = dsink - jnp.where(lane8 == g * ATT_GROUP + hh, head_sum, 0.0)
                ds = p * (_dot(do_g, vals, NT_DIMS) - dsum) * ATT_SCALE
                dq_g = _dot(ds, keys)
                dq_parts += [dq_g[hh * WINDOW:(hh + 1) * WINDOW] for hh in range(ATT_GROUP)]
                dk_g = _dot(ds, q, TN_DIMS)
                dv_g = _dot(p, do_g, TN_DIMS)
                dk_before.append(dk_g[:WINDOW])
                dk_own.append(dk_g[WINDOW:])
                dv_before.append(dv_g[:WINDOW])
                dv_own.append(dv_g[WINDOW:])
            per_slab = LANE // ATT_HEAD_DIM
            for s in range(ATT_WIDTH // LANE):
                slab = jnp.concatenate(dq_parts[s * per_slab:(s + 1) * per_slab], axis=1)
                o_ref[tok, s * LANE:(s + 1) * LANE] = unrope(slab)
            o_ref[tok, ATT_WIDTH:ATT_WIDTH + LANE] = unrope(jnp.concatenate(dk_own, axis=1) + from_next_k)
            o_ref[tok, ATT_WIDTH + LANE:ATT_COLS] = (jnp.concatenate(dv_own, axis=1) + from_next_v).astype(BF16)
            from_next_k, from_next_v = jnp.concatenate(dk_before, axis=1), jnp.concatenate(dv_before, axis=1)
        carry_k[...] = from_next_k
        carry_v[...] = from_next_v
        _acc_out(dsink_ref, first, dsink)

    rows = ATT_BPS * WINDOW
    rev = lambda i: nstep - 1 - i
    cur = lambda width: pl.BlockSpec((None, rows, width), lambda b, i: (b, rev(i), 0))
    q_spec = pl.BlockSpec((None, ATT_BPS, ATT_KV_HEADS, GROUP_ROWS, ATT_HEAD_DIM), lambda b, i: (b, rev(i), 0, 0, 0))
    kv_cur = pl.BlockSpec((None, ATT_KV_HEADS, rows, ATT_HEAD_DIM), lambda b, i: (b, 0, rev(i), 0))
    kv_prev = pl.BlockSpec((None, ATT_KV_HEADS, WINDOW, ATT_HEAD_DIM), lambda b, i: (b, 0, jnp.maximum(ATT_BPS * rev(i) - 1, 0), 0))
    tab = pl.BlockSpec((rows, LANE), lambda b, i: (rev(i), 0))
    return _call(
        body, (sinks, dcat, raw, w_norm, qh, kh, kh, vh, vh, lse, *tables), name="attn_bwd", grid=(bsz, nstep),
        in_specs=[pl.BlockSpec(memory_space=pltpu.SMEM), cur(ATT_WIDTH), cur(ATT_WIDTH), _vec_spec(ATT_WIDTH), q_spec,
                  kv_cur, kv_prev, kv_cur, kv_prev, cur(ATT_Q_HEADS), tab, tab, tab],
        out_specs=[cur(ATT_COLS), _vec_spec(ATT_WIDTH), _vec_spec(ATT_Q_HEADS)],
        out_shape=[_sds((bsz, seq, ATT_COLS), BF16), _sds((1, ATT_WIDTH), F32), _sds((1, ATT_Q_HEADS), F32)],
        scratch_shapes=[pltpu.VMEM((WINDOW, LANE), F32), pltpu.VMEM((WINDOW, LANE), F32)],
        sem=("arbitrary", "arbitrary"), comms=comms)


def _other_chips(x, y):
    return [(1 - x, y), (x, 1 - y), (1 - x, 1 - y)]


def _sem_pair(n):
    return [pltpu.SemaphoreType.DMA((n,)), pltpu.SemaphoreType.DMA((n,))]


def _plan_pair_forward(bufs):
    n = len(bufs)

    def copies(outs, sems):
        x, y, c = _mesh_pos()
        sends, lands = [], []
        for a in range(n):
            for j, chip in enumerate(_other_chips(x, y)):
                k = 3 * a + j
                slot = outs[a].at[4 * chip[0] + 2 * chip[1] + c]
                sends.append(pltpu.make_async_remote_copy(
                    src_ref=slot, dst_ref=slot, send_sem=sems[0].at[k], recv_sem=sems[1].at[k],
                    device_id=(x, y, 1 - c), device_id_type=MESH))
                theirs = outs[a].at[4 * chip[0] + 2 * chip[1] + 1 - c]
                lands.append(pltpu.make_async_remote_copy(
                    src_ref=theirs, dst_ref=theirs, send_sem=sems[0].at[k], recv_sem=sems[1].at[k],
                    device_id=(x, y, 1 - c), device_id_type=MESH))
        return sends, lands

    def start(ins, outs, sems):
        for cp in copies(outs, sems)[0]:
            cp.start()

    def finish(ins, outs, sems):
        sends, lands = copies(outs, sems)
        for cp in lands:
            cp.wait_recv()
        for cp in sends:
            cp.wait_send()

    return _Comm(list(bufs), [_sds(b.shape, b.dtype) for b in bufs], _sem_pair(3 * n), start, finish,
                 aliases=[(a, a) for a in range(n)])


def _plan_pair(arrays, other_half):
    n = len(arrays)
    per = N_CHIPS if other_half == "chip_major" else 1

    def copies(ins, outs, sems):
        x, y, c = _mesh_pos()
        out = []
        for a in range(n):
            for k in range(per):
                if other_half == "chip_major":
                    src, dst = ins[a].at[k, 1 - c], outs[a].at[k]
                else:
                    src, dst = (ins[a].at[1 - c] if other_half else ins[a]), outs[a]
                out.append(pltpu.make_async_remote_copy(
                    src_ref=src, dst_ref=dst, send_sem=sems[0].at[per * a + k], recv_sem=sems[1].at[per * a + k],
                    device_id=(x, y, 1 - c), device_id_type=MESH))
        return out

    def start(ins, outs, sems):
        for cp in copies(ins, outs, sems):
            cp.start()

    def finish(ins, outs, sems):
        for cp in copies(ins, outs, sems):
            cp.wait()

    if other_half == "chip_major":
        shapes = [_sds((a.shape[0],) + a.shape[2:], a.dtype) for a in arrays]
    else:
        shapes = [_sds(a.shape[1:] if other_half else a.shape, a.dtype) for a in arrays]
    return _Comm(list(arrays), shapes, _sem_pair(per * n), start, finish)


def _plan_chip_exchange(arrays):
    n = len(arrays)

    def copies(ins, outs, sems):
        x, y, c = _mesh_pos()
        sends, lands = [], []
        for a in range(n):
            for j, chip in enumerate(_other_chips(x, y)):
                k = 3 * a + j
                sends.append(pltpu.make_async_remote_copy(
                    src_ref=ins[a].at[2 * chip[0] + chip[1]], dst_ref=outs[a].at[2 * x + y], send_sem=sems[0].at[k],
                    recv_sem=sems[1].at[k], device_id=(*chip, c), device_id_type=MESH))
                slot = outs[a].at[2 * chip[0] + chip[1]]
                lands.append(pltpu.make_async_remote_copy(
                    src_ref=slot, dst_ref=slot, send_sem=sems[0].at[k], recv_sem=sems[1].at[k],
                    device_id=(*chip, c), device_id_type=MESH))
        return sends, lands

    def start(ins, outs, sems):
        for cp in copies(ins, outs, sems)[0]:
            cp.start()

    def finish(ins, outs, sems):
        sends, lands = copies(ins, outs, sems)
        for cp in lands:
            cp.wait_recv()
        for cp in sends:
            cp.wait_send()

    return _Comm(list(arrays), [_sds(a.shape, a.dtype) for a in arrays], _sem_pair(3 * n), start, finish)


SEM_SPEC = pl.BlockSpec(memory_space=pltpu.SEMAPHORE)
N_OTHER = N_CHIPS - 1


def _exchange_copies(s_ref, land_ref, sems):
    x, y, c = _mesh_pos()
    return [pltpu.make_async_remote_copy(
        src_ref=s_ref.at[2 * chip[0] + chip[1]], dst_ref=land_ref.at[2 * x + y], send_sem=sems[j], recv_sem=sems[N_OTHER + j],
        device_id=(*chip, c), device_id_type=MESH) for j, chip in enumerate(_other_chips(x, y))]


def _exchange_start(s, name):
    def body(s_ref, land_ref, *outs):
        sems, token = outs[:2 * N_OTHER], outs[-1]
        for cp in _exchange_copies(s_ref, land_ref, sems):
            cp.start()
        token[...] = jnp.zeros_like(token)

    hbm = pltpu.HBM(s.shape, s.dtype)
    res = pl.pallas_call(
        body, name=name,
        out_shape=(pltpu.SemaphoreType.DMA(()),) * (2 * N_OTHER) + (hbm, hbm, _sds((SUBLANES, LANE), F32)),
        in_specs=(HBM_SPEC, HBM_SPEC),
        out_specs=(SEM_SPEC,) * (2 * N_OTHER) + (HBM_SPEC, HBM_SPEC, pl.BlockSpec(memory_space=pltpu.VMEM)),
        input_output_aliases={0: 2 * N_OTHER, 1: 2 * N_OTHER + 1},
        compiler_params=pltpu.CompilerParams(has_side_effects=pltpu.SideEffectType.DATAFLOW_SIDE_EFFECTING),
    )(pltpu.with_memory_space_constraint(s, pltpu.HBM), pltpu.with_memory_space_constraint(lax.empty(s.shape, s.dtype), pltpu.HBM))
    return res[:2 * N_OTHER], res[2 * N_OTHER], res[2 * N_OTHER + 1], res[-1]


def _exchange_wait(sems, s_thru, land_thru, afters, name):
    def body(s_ref, land_ref, *rest):
        for cp in _exchange_copies(s_ref, land_ref, rest[:2 * N_OTHER]):
            cp.wait_send()
            cp.wait_recv()

    hbm = pltpu.HBM(s_thru.shape, s_thru.dtype)
    return pl.pallas_call(
        body, name=name, out_shape=(hbm, hbm),
        in_specs=(HBM_SPEC, HBM_SPEC) + (SEM_SPEC,) * (2 * N_OTHER) + (pl.BlockSpec(memory_space=pl.ANY),) * len(afters),
        out_specs=(HBM_SPEC, HBM_SPEC), input_output_aliases={0: 0, 1: 1},
        compiler_params=pltpu.CompilerParams(has_side_effects=pltpu.SideEffectType.DATAFLOW_SIDE_EFFECTING),
    )(s_thru, land_thru, *sems, *afters)


def _gather_copies(block_ref, buf_ref, sems):
    x, y, c = _mesh_pos()
    return [pltpu.make_async_remote_copy(
        src_ref=block_ref, dst_ref=buf_ref.at[4 * x + 2 * y + c], send_sem=sems[j], recv_sem=sems[N_OTHER + j],
        device_id=(*chip, c), device_id_type=MESH) for j, chip in enumerate(_other_chips(x, y))]


def _gather_start(blocks, bufs, afters, name):
    n = len(blocks)
    per = 2 * N_OTHER

    def body(*refs):
        ins, outs = refs[:2 * n], refs[2 * n + len(afters):]
        for a in range(n):
            for cp in _gather_copies(ins[a], ins[n + a], outs[a * per:(a + 1) * per]):
                cp.start()
        outs[-1][...] = jnp.zeros_like(outs[-1])

    hbm = [pltpu.HBM(z.shape, z.dtype) for z in list(blocks) + list(bufs)]
    res = pl.pallas_call(
        body, name=name,
        out_shape=(pltpu.SemaphoreType.DMA(()),) * (n * per) + tuple(hbm) + (_sds((SUBLANES, LANE), F32),),
        in_specs=(HBM_SPEC,) * (2 * n) + (pl.BlockSpec(memory_space=pl.ANY),) * len(afters),
        out_specs=(SEM_SPEC,) * (n * per) + (HBM_SPEC,) * (2 * n) + (pl.BlockSpec(memory_space=pltpu.VMEM),),
        input_output_aliases={k: n * per + k for k in range(2 * n)},
        compiler_params=pltpu.CompilerParams(has_side_effects=pltpu.SideEffectType.DATAFLOW_SIDE_EFFECTING),
    )(*[pltpu.with_memory_space_constraint(z, pltpu.HBM) for z in list(blocks) + list(bufs)], *afters)
    parts = [(res[a * per:(a + 1) * per], res[n * per + a], res[n * per + n + a]) for a in range(n)]
    return parts, res[-1]


def _gather_wait(part, afters, name):
    sems, block, buf = part

    def body(block_ref, buf_ref, *rest):
        for cp in _gather_copies(block_ref, buf_ref, rest[:2 * N_OTHER]):
            cp.wait_send()
            cp.wait_recv()

    return pl.pallas_call(
        body, name=name, out_shape=(pltpu.HBM(block.shape, block.dtype), pltpu.HBM(buf.shape, buf.dtype)),
        in_specs=(HBM_SPEC, HBM_SPEC) + (SEM_SPEC,) * (2 * N_OTHER) + (pl.BlockSpec(memory_space=pl.ANY),) * len(afters),
        out_specs=(HBM_SPEC, HBM_SPEC), input_output_aliases={0: 0, 1: 1},
        compiler_params=pltpu.CompilerParams(has_side_effects=pltpu.SideEffectType.DATAFLOW_SIDE_EFFECTING),
    )(block, buf, *sems, *afters)[1]


def _comm_only(comms, name):
    return _call(lambda: None, (), name=name, grid=(), in_specs=[], out_specs=[], out_shape=[], sem=(), comms=comms)[1]


def _allgather8(arrays, name):
    return _comm_only([_plan_allgather8(arrays)], name)[0]


def _plan_allgather8(arrays):
    n = len(arrays)

    def parts(ins, outs, sems):
        send_sems, recv_sems, local_sems = sems
        x, y, c = _mesh_pos()
        me, sibling = (x, y, c), (x, y, 1 - c)
        chips = _other_chips(x, y)

        def copy(a, k, block, to, src=None):
            dst = outs[a].at[4 * block[0] + 2 * block[1] + block[2]]
            return pltpu.make_async_remote_copy(
                src_ref=dst if src is None else src, dst_ref=dst, send_sem=send_sems.at[7 * a + k],
                recv_sem=recv_sems.at[7 * a + k], device_id=to, device_id_type=MESH)

        mine = [pltpu.make_async_copy(ins[a], outs[a].at[4 * x + 2 * y + c], local_sems.at[a]) for a in range(n)]
        first = []
        for a in range(n):
            first.append(copy(a, 0, me, sibling, src=ins[a]))
            first += [copy(a, 1 + j, me, (*chip, c), src=ins[a]) for j, chip in enumerate(chips)]
        return copy, mine, first, me, sibling, chips, c

    def start(ins, outs, sems):
        _, mine, first, *_ = parts(ins, outs, sems)
        for cp in mine + first:
            cp.start()

    def finish(ins, outs, sems):
        copy, mine, first, me, sibling, chips, c = parts(ins, outs, sems)
        passed = []
        for j, chip in enumerate(chips):
            for a in range(n):
                copy(a, 1 + j, (*chip, c), me).wait_recv()
                fwd = copy(a, 4 + j, (*chip, c), sibling)
                fwd.start()
                passed.append(fwd)
        for a in range(n):
            copy(a, 0, sibling, me).wait_recv()
            for j, chip in enumerate(chips):
                copy(a, 4 + j, (*chip, 1 - c), me).wait_recv()
        for cp in first + passed:
            cp.wait_send()
        for cp in mine:
            cp.wait()

    sems = [pltpu.SemaphoreType.DMA((7 * n,)), pltpu.SemaphoreType.DMA((7 * n,)), pltpu.SemaphoreType.DMA((n,))]
    return _Comm(list(arrays), [_sds((N_DEV,) + a.shape, a.dtype) for a in arrays], sems, start, finish)


def _pair_sum(g, q, core, name, chip_major=False):
    rows, cols = g.shape[2:]
    tr = _row_tile(rows)

    def body(core_ref, g_ref, q_ref, o_ref):
        o_ref[...] = (g_ref[...] + q_ref[...]).astype(BF16)

    blk = pl.BlockSpec((None, tr, cols), lambda k, i, core_ref: (k, i, 0))
    if chip_major:
        own = pl.BlockSpec((None, None, tr, cols), lambda k, i, core_ref: (k, core_ref[0], i, 0))
    else:
        own = pl.BlockSpec((None, None, tr, cols), lambda k, i, core_ref: (core_ref[0], k, i, 0))
    return pl.pallas_call(
        body, name=name,
        grid_spec=pltpu.PrefetchScalarGridSpec(num_scalar_prefetch=1, grid=(N_CHIPS, rows // tr), in_specs=[own, blk], out_specs=blk),
        out_shape=_sds((N_CHIPS, rows, cols), BF16), compiler_params=_params("parallel", "parallel"),
    )(core, g, q)


def _sum_chips(own, landed, chip, name):
    _, rows, cols = own.shape
    tr = _row_tile(rows)

    def body(chip_ref, own_ref, a_ref, b_ref, c_ref, o_ref):
        acc = own_ref[...].astype(F32) + a_ref[...].astype(F32)
        o_ref[...] = (acc + b_ref[...].astype(F32)) + c_ref[...].astype(F32)

    blk = lambda flip: pl.BlockSpec((None, tr, cols), lambda i, chip_ref: (jnp.bitwise_xor(chip_ref[0], flip), i, 0))
    return pl.pallas_call(
        body, name=name,
        grid_spec=pltpu.PrefetchScalarGridSpec(num_scalar_prefetch=1, grid=(rows // tr,), in_specs=[blk(0), blk(1), blk(2), blk(3)],
                                               out_specs=pl.BlockSpec((tr, cols), lambda i, chip_ref: (i, 0))),
        out_shape=_sds((rows, cols), F32), compiler_params=_params("parallel"),
    )(chip, own, landed, landed, landed)


SUBLANES = 8


def _tile_rows(n_elems):
    return -(-n_elems // (SUBLANES * LANE)) * SUBLANES


SMALL_ITEMS = (("b_ada", N_MOD * D_MODEL), ("pre_w_mix", D_MODEL), ("post_w_mix", D_MODEL), ("pre_w_mlp", D_MODEL),
               ("post_w_mlp", D_MODEL), ("attn_out_w", ATT_WIDTH), ("hg_norm_w", HG_HEAD_DIM), ("attn_sinks", ATT_Q_HEADS),
               ("lb_0", HG_WIDTH), ("lb_1", HG_WIDTH))
SMALL_AT = {}
for _name, _size in SMALL_ITEMS:
    SMALL_AT[_name] = (sum(r for _, r in SMALL_AT.values()), _tile_rows(_size))
SMALL_ROWS = sum(r for _, r in SMALL_AT.values())
MOD_ROWS = SMALL_AT["b_ada"][1]
PLAIN_ROWS = SMALL_AT["lb_0"][0] - MOD_ROWS
LB_ROWS = SMALL_AT["lb_0"][1]


def _rows(a, nrows=None):
    flat = a.reshape(-1)
    nrows = _tile_rows(flat.shape[0]) if nrows is None else nrows
    return jnp.pad(flat, (0, nrows * LANE - flat.shape[0])).reshape(nrows, LANE)


def _pack_small(vals):
    vals = dict(vals, lb_0=vals["lb_table"][0], lb_1=vals["lb_table"][1])
    return jnp.concatenate([_rows(vals[name], SMALL_AT[name][1]) for name, _ in SMALL_ITEMS], axis=0)


def _unpack_small(p):
    def item(name, shape):
        first = SMALL_AT[name][0]
        size = shape[0] * shape[1]
        return p[first:first + SMALL_AT[name][1]].reshape(-1)[:size].reshape(shape)

    out = {name: item(name, (1, size)) for name, size in SMALL_ITEMS if not name.startswith("lb_")}
    out["lb_table"] = jnp.concatenate([item("lb_0", (1, HG_WIDTH)), item("lb_1", (1, HG_WIDTH))], axis=0)
    return out


def _pack_partials(dmod, plain, d_lb, loss_row):
    return jnp.concatenate([_rows(dmod, dmod.shape[0] * MOD_ROWS)] + [_rows(g) for g in plain] + [_rows(d_lb), _rows(loss_row)], axis=0)


def _small_update(packs, w, m, v, n_seq):
    mod_end = n_seq * MOD_ROWS
    lb_at = mod_end + PLAIN_ROWS
    t0, t1 = SMALL_AT["lb_0"][0], SMALL_AT["lb_1"][0]

    def body(p_ref, w_ref, m_ref, v_ref, g_ref, dl_ref, nm_ref, nv_ref, loss_ref):
        tot = p_ref[0]
        for d in range(1, N_DEV):
            tot = tot + p_ref[d]
        wv = w_ref[...]
        p1 = _sigmoid(wv[t1:t1 + LB_ROWS] - wv[t0:t0 + LB_ROWS])
        s = tot[lb_at:lb_at + LB_ROWS] * p1 * (1.0 - p1)
        g_bias = tot[0:MOD_ROWS]
        for q in range(1, n_seq):
            g_bias = g_bias + tot[q * MOD_ROWS:(q + 1) * MOD_ROWS]
        g = jnp.concatenate([g_bias, tot[mod_end:lb_at], -s, s], axis=0)
        g_ref[...] = g
        dl_ref[...], nm_ref[...], nv_ref[...] = _adamw_math(g, wv, m_ref[...], v_ref[...])
        loss_ref[...] = tot[lb_at + LB_ROWS:lb_at + LB_ROWS + SUBLANES]

    shp = _sds((SMALL_ROWS, LANE), F32)
    return pl.pallas_call(body, name="small_update", out_shape=[shp] * 4 + [_sds((SUBLANES, LANE), F32)],
                          compiler_params=_params())(packs, w, m, v)


def kernel(x, c, w_ada, b_ada, pre_w_mix, w_in, attn_sinks, attn_out_w, lb_table, hg_norm_w, w_out, post_w_mix, pre_w_mlp, w_up, w_down, post_w_mlp, loss_target, m_w_ada, m_b_ada, m_pre_w_mix, m_w_in, m_attn_sinks, m_attn_out_w, m_lb_table, m_hg_norm_w, m_w_out, m_post_w_mix, m_pre_w_mlp, m_w_up, m_w_down, m_post_w_mlp, v_w_ada, v_b_ada, v_pre_w_mix, v_w_in, v_attn_sinks, v_attn_out_w, v_lb_table, v_hg_norm_w, v_w_out, v_post_w_mix, v_pre_w_mlp, v_w_up, v_w_down, v_post_w_mlp):
    xi, yi, ci = _mesh_pos()
    chip = 2 * xi + yi
    dev = 2 * chip + ci
    bsz, seq, _ = x.shape
    ntok = bsz * seq
    ada_cols = w_ada.shape[2]
    core = jnp.reshape(ci, (1,)).astype(jnp.int32)
    chip_idx = jnp.reshape(chip, (1,)).astype(jnp.int32)
    flat = lambda a: a.reshape(ntok, a.shape[-1])
    unflat = lambda a: a.reshape(bsz, seq, a.shape[-1])
    tables = _rope_tables(seq)

    def row_half(w):
        rows = w.shape[1] // 2
        return lax.dynamic_slice_in_dim(w[0], ci * rows, rows, axis=0).astype(BF16)

    def gather_buffer(w):
        rows, cols = w.shape[1] // 2, w.shape[2]
        own = w[0].astype(BF16).reshape(2, rows, cols)
        return lax.dynamic_update_slice(lax.empty((N_DEV, rows, cols), BF16), own, (2 * chip, 0, 0))

    w_in_t, m_in_t, v_in_t = [jnp.transpose(a[0])[None] for a in (w_in, m_w_in, v_w_in)]
    c_g, in_g = _allgather8([c, row_half(w_in_t)], "gather_first")
    c_all = c_g.reshape(N_DEV * bsz, D_MODEL)
    w_in_full = in_g.reshape(IN_COLS, D_MODEL)

    b_cols = lax.dynamic_slice_in_dim(b_ada, chip * ada_cols, ada_cols, axis=1)
    mod_part = _ada_fwd(c_all, w_ada[0], b_cols)
    half_rows = mod_part.shape[0] // 2
    (mod_g,) = _allgather8([lax.dynamic_slice_in_dim(mod_part, ci * half_rows, half_rows, axis=0)], "gather_mod")
    mod_all = mod_g.reshape(N_CHIPS, 2, half_rows, ada_cols).transpose(1, 2, 0, 3).reshape(N_DEV * bsz, N_MOD * D_MODEL)
    mod = lax.dynamic_slice_in_dim(mod_all, dev * bsz, bsz, axis=0)
    sh1, sc1, g1, sh2, sc2, g2 = [mod[:, i * D_MODEL:(i + 1) * D_MODEL].reshape(bsz, 1, D_MODEL) for i in range(N_MOD)]

    weights = (w_out, w_up, w_down)
    (out_part, up_part, down_part), _ = _gather_start(
        [row_half(w) for w in weights], [gather_buffer(w) for w in weights], [mod_g], "gather_weights_start")

    h1, proj, qh, kh, vh = _in_proj_fused(x, pre_w_mix, sc1, sh1, w_in_full, tables)
    out_g = _gather_wait(out_part, [proj], "gather_out_wait")
    (attn_raw, cat, lse), ((out_g,),) = _attn_fwd(qh, kh, vh, attn_sinks, attn_out_w, comms=[_plan_pair_forward([out_g])])
    up_g = _gather_wait(up_part, [attn_raw], "gather_up_wait")
    (o_raw, cat, states), ((up_g,),) = _hgrn_fwd(proj, lb_table, hg_norm_w, cat, comms=[_plan_pair_forward([up_g])])
    down_g = _gather_wait(down_part, [o_raw], "gather_down_wait")
    w_out_full = out_g.reshape(D_MODEL, D_MODEL)
    w_up4 = up_g.reshape(N_CHIPS, D_MODEL, D_MODEL)
    mix, x1, h2 = _out_proj_fused(cat, w_out_full, x, post_w_mix, g1, pre_w_mlp, sc2, sh2)
    big_tm = min(ntok, 2048)
    up_spec = pl.BlockSpec((None, D_MODEL, D_MODEL), lambda i, j: (j, 0, 0))
    r, ((down_g,),) = _mm(flat(h2), w_up4, name="up_proj", out_dtype=BF16, tm=big_tm, tn=D_MODEL, n_out=D_FF, b_spec=up_spec,
                          epi=lambda acc: jnp.maximum(acc, 0.0), comms=[_plan_pair_forward([down_g])])
    w_down_full = down_g.reshape(D_FF, D_MODEL)
    square = lambda t: t * t
    loss_row, dy, dd, dg2, d_post_mlp = _down_proj_fused(unflat(r), w_down_full, x1, post_w_mlp, g2, loss_target)

    dpre = _mm(flat(dd), w_down_full, name="down_bwd", out_dtype=BF16, trans_b=True, tm=big_tm, tn=D_MODEL, extra=(r,),
               epi=lambda acc, rt: acc * (2.0 * rt.astype(F32)))
    half_rows = D_MODEL // 2
    g_down = _mm_tn(r, flat(dd), name="down_wgrad", tk=half_rows, tn=D_MODEL, a_fn=square,
                    out_shape=_sds((2, N_CHIPS, half_rows, D_MODEL), F32),
                    out_spec=pl.BlockSpec((None, None, half_rows, D_MODEL), lambda i, j: (i % 2, i // 2, 0, 0)))
    (dx1, dmix, dsc2, dsh2, dg1, d_pre_mlp, d_post_mix), ((q_down,),) = _up_bwd_fused(
        unflat(dpre), w_up4, dy, x1, mix, pre_w_mlp, sc2, post_w_mix, g1, comms=[_plan_pair([g_down], True)])
    g_up = _mm_tn(flat(h2), dpre, name="up_wgrad", tk=D_MODEL, tn=half_rows,
                  out_shape=_sds((2, N_CHIPS, half_rows, D_MODEL), F32),
                  out_spec=pl.BlockSpec((2, None, half_rows, half_rows), lambda i, j: (0, j // 2, 0, j % 2)))
    s_down = _pair_sum(g_down, q_down, core, "pair_sum_down")

    dcat, ((q_up,),) = _mm(flat(dmix), w_out_full, name="out_bwd", out_dtype=F32, trans_b=True, comms=[_plan_pair([g_up], True)])
    dcat = unflat(dcat)
    s_up = _pair_sum(g_up, q_up, core, "pair_sum_up")
    out_rows = D_MODEL // N_CHIPS
    g_out = _mm_tn(flat(cat), flat(dmix), name="out_wgrad", tk=2 * out_rows, tn=half_rows,
                   out_shape=_sds((2, N_CHIPS, out_rows, half_rows), F32),
                   out_spec=pl.BlockSpec((None, 2, out_rows, half_rows), lambda i, j: (j, i, 0, 0)))
    (dhq, dhf, dhi, dhg, d_lb, d_hg_norm), ((x_down,), (q_out,)) = _hgrn_bwd(
        dcat, proj, o_raw, states, lb_table, hg_norm_w, comms=[_plan_chip_exchange([s_down]), _plan_pair([g_out], True)])
    half_down = _sum_chips(s_down, x_down, chip_idx, "sum_chips_down")
    s_out = _pair_sum(g_out, q_out, core, "pair_sum_out")
    (dproj_a, d_attn_out, d_sinks), ((their_down,), (x_up, x_out)) = _attn_bwd(
        dcat, attn_raw, attn_out_w, qh, kh, vh, lse, attn_sinks, tables,
        comms=[_plan_pair([half_down], False), _plan_chip_exchange([s_up, s_out])])
    half_up = _sum_chips(s_up, x_up, chip_idx, "sum_chips_up")
    half_out = _sum_chips(s_out, x_out, chip_idx, "sum_chips_out")
    dproj = flat(jnp.concatenate([dproj_a, dhq, dhf, dhi, dhg], axis=-1))
    in_rows = IN_COLS // N_CHIPS // 2
    g_in = _mm_tn(dproj, flat(h1), name="in_wgrad", tk=2 * LANE, tn=D_MODEL).reshape(N_CHIPS, 2, in_rows, D_MODEL)
    dh1, ((q_in,), (their_up, their_out)) = _mm(
        dproj, w_in_full, name="in_bwd", out_dtype=F32,
        comms=[_plan_pair([g_in], "chip_major"), _plan_pair([half_up, half_out], False)])
    s_in = _pair_sum(g_in, q_in, core, "pair_sum_in", chip_major=True)
    in_sems, s_in, in_landing, started = _exchange_start(s_in, "exchange_in_start")
    grad_x, dsc1, dsh1, d_pre_mix = _norm1_bwd(unflat(dh1), dx1, x, pre_w_mix + started[0:1, 0:1], sc1)

    dmod = jnp.concatenate([dsh1, dsc1, dg1, dsh2, dsc2, dg2], axis=-1).reshape(bsz, N_MOD * D_MODEL)
    pack = _pack_partials(dmod, [d_pre_mix, d_post_mix, d_pre_mlp, d_post_mlp, d_attn_out, d_hg_norm, d_sinks], d_lb, loss_row)
    ((packs,),) = _comm_only([_plan_allgather8([pack])], "gather_small")
    w_small = dict(b_ada=b_ada, pre_w_mix=pre_w_mix, post_w_mix=post_w_mix, pre_w_mlp=pre_w_mlp, post_w_mlp=post_w_mlp,
                   attn_out_w=attn_out_w, hg_norm_w=hg_norm_w, attn_sinks=attn_sinks, lb_table=lb_table)
    m_small = dict(b_ada=m_b_ada, pre_w_mix=m_pre_w_mix, post_w_mix=m_post_w_mix, pre_w_mlp=m_pre_w_mlp, post_w_mlp=m_post_w_mlp,
                   attn_out_w=m_attn_out_w, hg_norm_w=m_hg_norm_w, attn_sinks=m_attn_sinks, lb_table=m_lb_table)
    v_small = dict(b_ada=v_b_ada, pre_w_mix=v_pre_w_mix, post_w_mix=v_post_w_mix, pre_w_mlp=v_pre_w_mlp, post_w_mlp=v_post_w_mlp,
                   attn_out_w=v_attn_out_w, hg_norm_w=v_hg_norm_w, attn_sinks=v_attn_sinks, lb_table=v_lb_table)
    *small_packed, loss_rows = _small_update(packs, _pack_small(w_small), _pack_small(m_small), _pack_small(v_small), bsz)
    small_out = [_unpack_small(p) for p in small_packed]
    loss = loss_rows[0, 0]

    dmod_all = packs[:, :bsz * MOD_ROWS, :].reshape(N_DEV * bsz, N_MOD * D_MODEL)
    dmod_cols = lax.dynamic_slice_in_dim(dmod_all, chip * ada_cols, ada_cols, axis=1)
    ada_out = _ada_bwd_adamw(c_all, dmod_cols, w_ada[0], m_w_ada[0], v_w_ada[0])

    s_in, x_in = _exchange_wait(in_sems, s_in, in_landing, [grad_x, ada_out[0]], "exchange_in_wait")
    half_in = _sum_chips(s_in, x_in, chip_idx, "sum_chips_in")
    ((their_in,),) = _comm_only([_plan_pair([half_in], False)], "pair_swap_in")
    big = dict(
        w_in=tuple(jnp.transpose(a) for a in _adamw_halves(half_in, their_in, core, w_in_t[0], m_in_t[0], v_in_t[0], axis=0,
                                                           name="adamw_in")),
        w_up=tuple(_adamw_halves(half_up, their_up, core, w_up[0], m_w_up[0], v_w_up[0], axis=0, name="adamw_up")),
        w_out=tuple(_adamw_halves(half_out, their_out, core, w_out[0], m_w_out[0], v_w_out[0], axis=1, name="adamw_out")),
        w_down=tuple(_adamw_halves(half_down, their_down, core, w_down[0], m_w_down[0], v_w_down[0], axis=0, name="adamw_down")),
        w_ada=tuple(ada_out),
    )
    order = ("w_ada", "b_ada", "pre_w_mix", "w_in", "attn_sinks", "attn_out_w", "lb_table", "hg_norm_w", "w_out", "post_w_mix",
             "pre_w_mlp", "w_up", "w_down", "post_w_mlp")
    outs = [loss, grad_x]
    for kind in range(4):
        for nm in order:
            outs.append(big[nm][kind][None] if nm in big else small_out[kind][nm])
    return tuple(outs)
```

```python
import jax
import jax.numpy as jnp
from jax import lax
from jax.experimental import pallas as pl
from jax.experimental.pallas import tpu as pltpu

F32 = jnp.float32
BF16 = jnp.bfloat16

D_MODEL = 1024
ATT_WIDTH = 512
ATT_HEAD_DIM = 64
ATT_Q_HEADS = 8
ATT_KV_HEADS = 2
ATT_GROUP = ATT_Q_HEADS // ATT_KV_HEADS
ATT_KV_COLS = ATT_KV_HEADS * ATT_HEAD_DIM
WINDOW = 128
ROPE_DIM = 16
ROPE_THETA = 500000.0
HG_WIDTH = 512
MIX_WIDTH = ATT_WIDTH + HG_WIDTH
HG_HEAD_DIM = 128
HG_HEADS = 4
HG_CHUNK = 32
IN_COLS = ATT_WIDTH + 2 * ATT_KV_COLS + 4 * HG_WIDTH
ATT_COLS = ATT_WIDTH + 2 * ATT_KV_COLS
D_FF = 4 * D_MODEL
N_MOD = 6
EPS = 1e-6
ATT_SCALE = ATT_HEAD_DIM ** -0.5

ADAM_LR = 0.001
ADAM_B1 = 0.9
ADAM_B2 = 0.999
ADAM_EPS = 1e-08
ADAM_WD = 0.01
ADAM_STEP = 10

N_CHIPS = 4
N_DEV = 8
LANE = 128
VMEM_LIMIT = 48 * 1024 * 1024
VMEM_LIMIT_BIG = 58 * 1024 * 1024
MESH = pl.DeviceIdType.MESH

NT_DIMS = (((1,), (1,)), ((), ()))
TN_DIMS = (((0,), (0,)), ((), ()))


def _sds(shape, dtype):
    return jax.ShapeDtypeStruct(tuple(shape), dtype)


def _params(*sem, vmem_limit=None):
    return pltpu.CompilerParams(dimension_semantics=sem, vmem_limit_bytes=VMEM_LIMIT if vmem_limit is None else vmem_limit)


def _sigmoid(x):
    return 1.0 / (1.0 + jnp.exp(-x))


def _dot(a, b, dims=None):
    a, b = a.astype(BF16), b.astype(BF16)
    if dims is None:
        return jnp.dot(a, b, preferred_element_type=F32)
    return lax.dot_general(a, b, dims, preferred_element_type=F32)


def _rms_fwd(x, w):
    rstd = lax.rsqrt(jnp.mean(x * x, axis=-1, keepdims=True) + EPS)
    xh = x * rstd
    return xh * w, xh, rstd


def _rms_bwd(dy, xh, rstd, w):
    dxh = dy * w
    dx = rstd * (dxh - xh * jnp.mean(dxh * xh, axis=-1, keepdims=True))
    return dx, dy * xh


def _colsum(x):
    return jnp.sum(x, axis=0, keepdims=True)


def _row_tile(rows, cap=256):
    return max(t for t in range(16, cap + 1, 16) if rows % t == 0)


HBM_SPEC = pl.BlockSpec(memory_space=pltpu.HBM)


def _mesh_pos():
    return lax.axis_index("x"), lax.axis_index("y"), lax.axis_index("c")


class _Comm:
    def __init__(self, ins, outs, sems, start, finish, aliases=()):
        self.ins, self.outs, self.sems = list(ins), list(outs), list(sems)
        self.start, self.finish, self.aliases = start, finish, tuple(aliases)


def _call(body, args, *, name, grid, in_specs, out_specs, out_shape, sem, scratch_shapes=(), comms=(), aliases=None,
          vmem_limit=None):
    scratch_shapes = list(scratch_shapes)
    if not comms:
        return pl.pallas_call(body, name=name, grid=grid, in_specs=in_specs, out_specs=out_specs, out_shape=out_shape,
                              input_output_aliases=dict(aliases or {}), scratch_shapes=scratch_shapes,
                              compiler_params=_params(*sem, vmem_limit=vmem_limit))(*args)
    single = not isinstance(out_shape, (list, tuple))
    out_specs_l = [out_specs] if single else list(out_specs)
    out_shape_l = [out_shape] if single else list(out_shape)
    n_in, n_out, n_scr = len(in_specs), len(out_shape_l), len(scratch_shapes)
    n_ci = [len(cm.ins) for cm in comms]
    n_co = [len(cm.outs) for cm in comms]
    n_cs = [len(cm.sems) for cm in comms]
    aliases = dict(aliases or {})
    for k, cm in enumerate(comms):
        for i, o in cm.aliases:
            aliases[n_in + sum(n_ci[:k]) + i] = n_out + sum(n_co[:k]) + o

    def fused(*refs):
        pos = [0]

        def take(n):
            part = refs[pos[0]:pos[0] + n]
            pos[0] += n
            return part

        ins = take(n_in)
        c_ins = [take(n) for n in n_ci]
        outs = take(n_out)
        c_outs = [take(n) for n in n_co]
        scr = take(n_scr)
        c_sems = [take(n) for n in n_cs]
        first, last = True, True
        for d, size in enumerate(grid):
            first = jnp.logical_and(first, pl.program_id(d) == 0)
            last = jnp.logical_and(last, pl.program_id(d) == size - 1)

        def run(which):
            for cm, ci, co, cs in zip(comms, c_ins, c_outs, c_sems):
                getattr(cm, which)(ci, co, cs)

        if grid:
            pl.when(first)(lambda: run("start"))
        else:
            run("start")
        body(*ins, *outs, *scr)
        if grid:
            pl.when(last)(lambda: run("finish"))
        else:
            run("finish")

    res = pl.pallas_call(
        fused, name=name, grid=grid, in_specs=list(in_specs) + [HBM_SPEC] * sum(n_ci),
        out_specs=out_specs_l + [HBM_SPEC] * sum(n_co), out_shape=out_shape_l + [s for cm in comms for s in cm.outs],
        input_output_aliases=aliases, scratch_shapes=scratch_shapes + [s for cm in comms for s in cm.sems],
        compiler_params=_params(*["arbitrary"] * len(grid), vmem_limit=vmem_limit),
    )(*args, *[a for cm in comms for a in cm.ins])
    main = res[:n_out]
    extra, at = [], n_out
    for n in n_co:
        extra.append(list(res[at:at + n]))
        at += n
    return (main[0] if single else list(main)), extra


def _mm(a, b, *, name, out_dtype, trans_b=False, tm=512, tn=None, extra=(), epi=None, b_spec=None, n_out=None, comms=()):
    m_total, k_total = a.shape
    if n_out is None:
        n_out = b.shape[0] if trans_b else b.shape[1]
    tn = n_out if tn is None else tn
    grid = (m_total // tm, n_out // tn)
    dims = NT_DIMS if trans_b else None

    def body(*refs):
        a_ref, b_ref = refs[0], refs[1]
        extra_refs = refs[2:2 + len(extra)]
        o_ref = refs[2 + len(extra)]
        acc = _dot(a_ref[...], b_ref[...], dims)
        if epi is not None:
            acc = epi(acc, *[r[...] for r in extra_refs])
        o_ref[...] = acc.astype(out_dtype)

    if b_spec is None:
        if trans_b:
            b_spec = pl.BlockSpec((tn, k_total), lambda i, j: (j, 0))
        else:
            b_spec = pl.BlockSpec((k_total, tn), lambda i, j: (0, j))
    in_specs = [pl.BlockSpec((tm, k_total), lambda i, j: (i, 0)), b_spec]
    in_specs += [pl.BlockSpec((tm, tn), lambda i, j: (i, j)) for _ in extra]
    return _call(
        body, (a, b, *extra), name=name, grid=grid, in_specs=in_specs,
        out_specs=pl.BlockSpec((tm, tn), lambda i, j: (i, j)),
        out_shape=_sds((m_total, n_out), out_dtype),
        sem=("parallel", "parallel"), comms=comms)


def _mm_tn(a, b, *, name, tk, tn, a_fn=None, out_shape=None, out_spec=None):
    m_total, k_total = a.shape
    n_total = b.shape[1]
    grid = (k_total // tk, n_total // tn)

    def body(a_ref, b_ref, o_ref):
        av = a_ref[...]
        part = _dot(av if a_fn is None else a_fn(av), b_ref[...], TN_DIMS)
        o_ref[...] = part.reshape(o_ref.shape)

    if out_shape is None:
        out_shape = _sds((k_total, n_total), F32)
        out_spec = pl.BlockSpec((tk, tn), lambda i, j: (i, j))
    return pl.pallas_call(
        body, name=name, grid=grid,
        in_specs=[pl.BlockSpec((m_total, tk), lambda i, j: (0, i)), pl.BlockSpec((m_total, tn), lambda i, j: (0, j))],
        out_specs=out_spec, out_shape=out_shape,
        compiler_params=_params("parallel", "parallel"),
    )(a, b)


def _ada_fwd(c_all, w_shard, b_shard):
    nb, ncol = c_all.shape[0], w_shard.shape[1]
    tn = 512

    def body(c_ref, w_ref, b_ref, o_ref):
        c = c_ref[...]
        o_ref[...] = _dot(c * _sigmoid(c), w_ref[...]) + b_ref[...]

    return pl.pallas_call(
        body, name="ada_fwd", grid=(ncol // tn,),
        in_specs=[pl.BlockSpec((nb, D_MODEL), lambda j: (0, 0)), pl.BlockSpec((D_MODEL, tn), lambda j: (0, j)),
                  pl.BlockSpec((1, tn), lambda j: (0, j))],
        out_specs=pl.BlockSpec((nb, tn), lambda j: (0, j)), out_shape=_sds((nb, ncol), F32),
        compiler_params=_params("parallel"),
    )(c_all, w_shard, b_shard)


def _adamw_math(g, w, m, v):
    m = ADAM_B1 * m + (1.0 - ADAM_B1) * g
    v = ADAM_B2 * v + (1.0 - ADAM_B2) * (g * g)
    m_hat = m / (1.0 - ADAM_B1 ** ADAM_STEP)
    v_hat = v / (1.0 - ADAM_B2 ** ADAM_STEP)
    delta = -ADAM_LR * (m_hat / (jnp.sqrt(v_hat) + ADAM_EPS) + ADAM_WD * w)
    return delta, m, v


def _ada_bwd_adamw(c_all, dmod_cols, w, m, v):
    nb, ncol = dmod_cols.shape
    tn = 256

    def body(c_ref, d_ref, w_ref, m_ref, v_ref, g_ref, dl_ref, nm_ref, nv_ref):
        c = c_ref[...]
        g = _dot(c * _sigmoid(c), d_ref[...], TN_DIMS)
        g_ref[...] = g
        dl_ref[...], nm_ref[...], nv_ref[...] = _adamw_math(g, w_ref[...], m_ref[...], v_ref[...])

    col = pl.BlockSpec((D_MODEL, tn), lambda j: (0, j))
    shp = _sds((D_MODEL, ncol), F32)
    return pl.pallas_call(
        body, name="ada_bwd_adamw", grid=(ncol // tn,),
        in_specs=[pl.BlockSpec((nb, D_MODEL), lambda j: (0, 0)), pl.BlockSpec((nb, tn), lambda j: (0, j)), col, col, col],
        out_specs=[col, col, col, col], out_shape=[shp, shp, shp, shp],
        compiler_params=_params("parallel"),
    )(c_all, dmod_cols, w, m, v)


def _adamw_halves(own, theirs, core, w, m, v, *, axis, name):
    r2, c2 = own.shape
    tr = _row_tile(r2)
    nt = r2 // tr

    def body(core_ref, own_ref, their_ref, w_ref, m_ref, v_ref, g_ref, dl_ref, nm_ref, nv_ref):
        g = jnp.where(pl.program_id(0) == core_ref[0], own_ref[...], their_ref[...])
        g_ref[...] = g
        dl_ref[...], nm_ref[...], nv_ref[...] = _adamw_math(g, w_ref[...], m_ref[...], v_ref[...])

    if axis == 0:
        full = pl.BlockSpec((tr, c2), lambda h, i, core_ref: (h * nt + i, 0))
    else:
        full = pl.BlockSpec((tr, c2), lambda h, i, core_ref: (i, h))
    half = pl.BlockSpec((tr, c2), lambda h, i, core_ref: (i, 0))
    shp = _sds(w.shape, F32)
    return pl.pallas_call(
        body, name=name,
        grid_spec=pltpu.PrefetchScalarGridSpec(num_scalar_prefetch=1, grid=(2, nt), in_specs=[half, half, full, full, full],
                                               out_specs=[full] * 4),
        out_shape=[shp] * 4, compiler_params=_params("parallel", "parallel"),
    )(core, own, theirs, w, m, v)


def _tok_spec(tm, width=D_MODEL):
    return pl.BlockSpec((None, tm, width), lambda b, i: (b, i, 0))


def _row_spec(width=D_MODEL):
    return pl.BlockSpec((None, 1, width), lambda b, i: (b, 0, 0))


def _vec_spec(width=D_MODEL):
    return pl.BlockSpec((1, width), lambda b, i: (0, 0))


class _RowsOf:
    def __init__(self, ref, first, count):
        self.ref, self.rows = ref, slice(first, first + count)

    def __getitem__(self, idx):
        return self.ref[self.rows, :]

    def __setitem__(self, idx, value):
        self.ref[self.rows, :] = value


def _mm_rows(a, b, *, name, tm, extra, extra_specs, out_specs, out_shape, epi, pro=None, trans_b=False, b_chunks=1, comms=(),
             parts=1, zero_per_seq=(), zero_once=(), vmem_limit=None):
    bsz, seq, k_total = a.shape
    kc = k_total // b_chunks
    dims = NT_DIMS if trans_b else None
    rows = tm // parts

    def body(*refs):
        a_ref, b_ref = refs[0], refs[1]
        ex, outs = refs[2:2 + len(extra)], refs[2 + len(extra):]
        if zero_per_seq:
            @pl.when(pl.program_id(1) == 0)
            def _():
                for k in zero_per_seq:
                    outs[k][...] = jnp.zeros_like(outs[k])
        if zero_once:
            @pl.when(jnp.logical_and(pl.program_id(0) == 0, pl.program_id(1) == 0))
            def _():
                for k in zero_once:
                    outs[k][...] = jnp.zeros_like(outs[k])

        def part_of(ref, p):
            tiled = len(ref.shape) == 2 and ref.shape[0] == tm
            return _RowsOf(ref, p * rows, rows) if tiled and parts > 1 else ref

        accs = []
        for p in range(parts):
            a_p, ex_p, outs_p = part_of(a_ref, p), [part_of(r, p) for r in ex], [part_of(r, p) for r in outs]
            if b_chunks == 1:
                accs.append(_dot(a_p[...] if pro is None else pro(a_p, ex_p, outs_p), b_ref[...], dims))
            else:
                acc = _dot(a_p[...][:, 0:kc], b_ref[0], NT_DIMS)
                for k in range(1, b_chunks):
                    acc = acc + _dot(a_p[...][:, k * kc:(k + 1) * kc], b_ref[k], NT_DIMS)
                accs.append(acc)
        for p in range(parts):
            epi(accs[p], [part_of(r, p) for r in ex], [part_of(r, p) for r in outs])

    b_spec = pl.BlockSpec(b.shape, lambda bb, i: (0,) * b.ndim)
    return _call(
        body, (a, b, *extra), name=name, grid=(bsz, seq // tm), in_specs=[_tok_spec(tm, k_total), b_spec, *extra_specs],
        out_specs=out_specs, out_shape=out_shape, sem=("arbitrary", "arbitrary"), comms=comms, vmem_limit=vmem_limit)


def _in_proj_fused(x, w, sc, sh, w_in_t, tables, comms=()):
    tm = 512
    bsz, seq, _ = x.shape
    half = ROPE_DIM // 2
    heads_per_slab = LANE // ATT_HEAD_DIM

    def pro(x_ref, ex, outs):
        y, _, _ = _rms_fwd(x_ref[...], ex[0][...])
        h = (y * (1.0 + ex[1][...]) + ex[2][...]).astype(BF16)
        outs[0][...] = h
        return h

    def epi(acc, ex, outs):
        c, u, d = ex[3][...], ex[4][...], ex[5][...]
        _, proj_ref, q_ref, k_ref, v_ref = outs
        proj_ref[...] = acc

        def rope(z):
            return (z * c + pltpu.roll(z, half, 1) * u + pltpu.roll(z, LANE - half, 1) * d).astype(BF16)

        for s in range(ATT_WIDTH // LANE):
            slab = rope(acc[:, s * LANE:(s + 1) * LANE])
            for part in range(heads_per_slab):
                g, hh = divmod(s * heads_per_slab + part, ATT_GROUP)
                piece = slab[:, part * ATT_HEAD_DIM:(part + 1) * ATT_HEAD_DIM]
                for blk in range(tm // WINDOW):
                    q_ref[blk, g, hh * WINDOW:(hh + 1) * WINDOW, :] = piece[blk * WINDOW:(blk + 1) * WINDOW]
        rk = rope(acc[:, ATT_WIDTH:ATT_WIDTH + LANE])
        vv = acc[:, ATT_WIDTH + LANE:ATT_COLS].astype(BF16)
        for g in range(ATT_KV_HEADS):
            k_ref[g] = rk[:, g * ATT_HEAD_DIM:(g + 1) * ATT_HEAD_DIM]
            v_ref[g] = vv[:, g * ATT_HEAD_DIM:(g + 1) * ATT_HEAD_DIM]

    cols = w_in_t.shape[0]
    tab = pl.BlockSpec((tm, LANE), lambda b, i: (i, 0))
    kv_spec = pl.BlockSpec((None, ATT_KV_HEADS, tm, ATT_HEAD_DIM), lambda b, i: (b, 0, i, 0))
    kv_shape = _sds((bsz, ATT_KV_HEADS, seq, ATT_HEAD_DIM), BF16)
    q_spec = pl.BlockSpec((None, tm // WINDOW, ATT_KV_HEADS, GROUP_ROWS, ATT_HEAD_DIM), lambda b, i: (b, i, 0, 0, 0))
    return _mm_rows(x, w_in_t, name="in_proj", tm=tm, extra=(w, sc, sh, *tables),
                    extra_specs=[_vec_spec(), _row_spec(), _row_spec(), tab, tab, tab],
                    out_specs=[_tok_spec(tm), _tok_spec(tm, cols), q_spec, kv_spec, kv_spec],
                    out_shape=[_sds(x.shape, BF16), _sds((bsz, seq, cols), F32),
                               _sds((bsz, seq // WINDOW, ATT_KV_HEADS, GROUP_ROWS, ATT_HEAD_DIM), BF16), kv_shape, kv_shape],
                    pro=pro, epi=epi, trans_b=True, comms=comms)


def _rope_tables(seq):
    half = ROPE_DIM // 2
    inv_freq = ROPE_THETA ** (-jnp.arange(0, ROPE_DIM, 2, dtype=F32) / ROPE_DIM)
    ang = jnp.arange(seq, dtype=F32)[:, None] * inv_freq[None, :]
    cos, sin = jnp.cos(ang), jnp.sin(ang)
    rest = ATT_HEAD_DIM - ROPE_DIM
    ones, zeros, zh = jnp.ones((seq, rest), F32), jnp.zeros((seq, rest), F32), jnp.zeros((seq, half), F32)
    reps = LANE // ATT_HEAD_DIM
    t_cos = jnp.tile(jnp.concatenate([cos, cos, ones], axis=1), (1, reps))
    t_up = jnp.tile(jnp.concatenate([zh, sin, zeros], axis=1), (1, reps))
    t_dn = jnp.tile(jnp.concatenate([-sin, zh, zeros], axis=1), (1, reps))
    return t_cos, t_up, t_dn


GROUP_ROWS = ATT_GROUP * WINDOW


ATT_BPS = 2


def _band_mask(has_prev):
    row = lax.broadcasted_iota(jnp.int32, (GROUP_ROWS, 2 * WINDOW), 0) % WINDOW
    col = lax.broadcasted_iota(jnp.int32, (GROUP_ROWS, 2 * WINDOW), 1)
    prev = jnp.logical_and(jnp.logical_and(col < WINDOW, col > row), has_prev)
    return jnp.logical_or(prev, jnp.logical_and(col >= WINDOW, col - WINDOW <= row))


def _sink_column(sink_ref, g):
    head = lax.broadcasted_iota(jnp.int32, (GROUP_ROWS, 1), 0) // WINDOW
    col = jnp.full((GROUP_ROWS, 1), sink_ref[0, g * ATT_GROUP], F32)
    for hh in range(1, ATT_GROUP):
        col = jnp.where(head == hh, sink_ref[0, g * ATT_GROUP + hh], col)
    return col


def _attn_specs():
    q_spec = pl.BlockSpec((None, ATT_BPS, ATT_KV_HEADS, GROUP_ROWS, ATT_HEAD_DIM), lambda b, i: (b, i, 0, 0, 0))
    kv_cur = pl.BlockSpec((None, ATT_KV_HEADS, ATT_BPS * WINDOW, ATT_HEAD_DIM), lambda b, i: (b, 0, i, 0))
    kv_prev = pl.BlockSpec((None, ATT_KV_HEADS, WINDOW, ATT_HEAD_DIM), lambda b, i: (b, 0, jnp.maximum(ATT_BPS * i - 1, 0), 0))
    return q_spec, kv_cur, kv_prev


def _band(prev_ref, cur_ref, g, blk):
    own = cur_ref[g, blk * WINDOW:(blk + 1) * WINDOW]
    before = prev_ref[g] if blk == 0 else cur_ref[g, (blk - 1) * WINDOW:blk * WINDOW]
    return jnp.concatenate([before, own], axis=0)


def _attn_fwd(qh, kh, vh, sinks, w_norm, comms=()):
    bsz, nblk = qh.shape[0], qh.shape[1]
    seq = nblk * WINDOW
    rows = ATT_BPS * WINDOW
    neg = float(jnp.finfo(jnp.float32).min)

    def body(sink_ref, q_ref, kc_ref, kp_ref, vc_ref, vp_ref, w_ref, raw_ref, an_ref, l_ref):
        for blk in range(ATT_BPS):
            mask = _band_mask(True if blk else pl.program_id(1) > 0)
            for g in range(ATT_KV_HEADS):
                keys, vals = _band(kp_ref, kc_ref, g, blk), _band(vp_ref, vc_ref, g, blk)
                sink = _sink_column(sink_ref, g)
                s = jnp.where(mask, _dot(q_ref[blk, g], keys, NT_DIMS) * ATT_SCALE, neg)
                m = jnp.maximum(jnp.max(s, axis=-1, keepdims=True), sink)
                p = jnp.where(mask, jnp.exp(s - m), 0.0)
                den = jnp.sum(p, axis=-1, keepdims=True) + jnp.exp(sink - m)
                o = _dot(p / den, vals)
                lse = m + jnp.log(den)
                tok = slice(blk * WINDOW, (blk + 1) * WINDOW)
                for hh in range(ATT_GROUP):
                    h = g * ATT_GROUP + hh
                    raw_ref[tok, h * ATT_HEAD_DIM:(h + 1) * ATT_HEAD_DIM] = o[hh * WINDOW:(hh + 1) * WINDOW]
                    l_ref[tok, h:h + 1] = lse[hh * WINDOW:(hh + 1) * WINDOW]
        y, _, _ = _rms_fwd(raw_ref[...], w_ref[...])
        an_ref[...] = y.astype(BF16)

    cur = lambda width: pl.BlockSpec((None, rows, width), lambda b, i: (b, i, 0))
    q_spec, kv_cur, kv_prev = _attn_specs()
    return _call(
        body, (sinks, qh, kh, kh, vh, vh, w_norm), name="attn_fwd", grid=(bsz, nblk // ATT_BPS),
        in_specs=[pl.BlockSpec(memory_space=pltpu.SMEM), q_spec, kv_cur, kv_prev, kv_cur, kv_prev, _vec_spec(ATT_WIDTH)],
        out_specs=[cur(ATT_WIDTH), cur(ATT_WIDTH), cur(ATT_Q_HEADS)],
        out_shape=[_sds((bsz, seq, ATT_WIDTH), F32), _sds((bsz, seq, MIX_WIDTH), BF16), _sds((bsz, seq, ATT_Q_HEADS), F32)],
        sem=("parallel", "parallel"), comms=comms)


HG_Q0 = ATT_COLS // LANE
HG_F0 = HG_Q0 + HG_HEADS
HG_I0 = HG_F0 + HG_HEADS
HG_G0 = HG_I0 + HG_HEADS
HG_TOK = 256
HG_NCH = HG_TOK // HG_CHUNK
HG_HPS = 2


def _block_masks():
    row = lax.broadcasted_iota(jnp.int32, (HG_TOK, HG_TOK), 0)
    col = lax.broadcasted_iota(jnp.int32, (HG_TOK, HG_TOK), 1)
    same = (row // HG_CHUNK) == (col // HG_CHUNK)
    return jnp.logical_and(same, col <= row), jnp.logical_and(same, col >= row)


def _row_in_chunk():
    return lax.broadcasted_iota(jnp.int32, (HG_TOK, LANE), 0) % HG_CHUNK


def _chunk_cumsum(x, reverse=False):
    ric = _row_in_chunk()
    shift = 1
    while shift < HG_CHUNK:
        if reverse:
            x = x + jnp.where(ric < HG_CHUNK - shift, pltpu.roll(x, HG_TOK - shift, 0), 0.0)
        else:
            x = x + jnp.where(ric >= shift, pltpu.roll(x, shift, 0), 0.0)
        shift *= 2
    return x


def _chunk_rows(rows):
    stacked = jnp.concatenate([r[None] for r in rows], axis=0)
    return jnp.broadcast_to(stacked, (HG_NCH, HG_CHUNK, LANE)).reshape(HG_TOK, LANE)


def _chunk_slices(x):
    return [x[j * HG_CHUNK:(j + 1) * HG_CHUNK] for j in range(HG_NCH)]


def _hgrn_common(tbl, hf, hq):
    lb = _sigmoid(tbl[1:2] - tbl[0:1])
    sig = _sigmoid(hf)
    f = lb + (1.0 - lb) * sig
    sq = _sigmoid(hq)
    q, k = hq * sq, 1.0 - f
    b = _chunk_cumsum(jnp.log(f))
    last = [b[(j + 1) * HG_CHUNK - 1:(j + 1) * HG_CHUNK] for j in range(HG_NCH)]
    bl = _chunk_rows(last)
    e_b, e_nb, e_rem = jnp.exp(b), jnp.exp(-b), jnp.exp(bl - b)
    e_last = [jnp.exp(r) for r in last]
    return dict(lb=lb, sig=sig, f=f, sq=sq, q=q, k=k, e_b=e_b, e_nb=e_nb, e_rem=e_rem, e_last=e_last,
                qd=q * e_b, kd=k * e_nb, ku=k * e_rem)


def _hgrn_fwd(proj, lb_table, norm_w, mix_in, comms=()):
    bsz, seq, _ = proj.shape
    nstep = seq // HG_TOK

    def body(tbl_ref, nw_ref, q_ref, f_ref, i_ref, g_ref, mix_ref, o_ref, rec_ref, st_ref, s_scr):
        @pl.when(pl.program_id(2) == 0)
        def _():
            s_scr[...] = jnp.zeros_like(s_scr)

        lower, _ = _block_masks()
        for hp in range(HG_HPS):
            ls = slice(hp * LANE, (hp + 1) * LANE)
            v, hg = i_ref[:, ls], g_ref[:, ls]
            t = _hgrn_common(tbl_ref[:, ls], f_ref[:, ls], q_ref[:, ls])
            a = jnp.where(lower, _dot(t["qd"], t["kd"], NT_DIMS), 0.0)
            o_intra = _dot(a, v)
            v_c, ku_c, qd_c = [_chunk_slices(z.astype(BF16)) for z in (v, t["ku"], t["qd"])]
            updates = [_dot(v_c[j], ku_c[j], TN_DIMS) for j in range(HG_NCH)]
            st = s_scr[hp]
            states = []
            for j in range(HG_NCH):
                states.append(st)
                st = st * t["e_last"][j] + updates[j]
            s_scr[hp] = st
            o = o_intra + jnp.concatenate([_dot(qd_c[j], states[j], NT_DIMS) for j in range(HG_NCH)], axis=0)
            for j in range(HG_NCH):
                st_ref[hp, j] = states[j]
            o_ref[:, ls] = o
            y, _, _ = _rms_fwd(o, nw_ref[...])
            rec_ref[:, ls] = (y * (hg * _sigmoid(hg))).astype(BF16)

    width = HG_HPS * LANE
    slab = lambda first: pl.BlockSpec((None, HG_TOK, width), lambda b, h, t: (b, t, first // HG_HPS + h))
    head_out = pl.BlockSpec((None, HG_TOK, width), lambda b, h, t: (b, t, h))
    mix_out = pl.BlockSpec((None, HG_TOK, width), lambda b, h, t: (b, t, ATT_WIDTH // width + h))
    return _call(
        body, (lb_table, norm_w, proj, proj, proj, proj, mix_in), name="hgrn_fwd", grid=(bsz, HG_HEADS // HG_HPS, nstep),
        in_specs=[pl.BlockSpec((2, width), lambda b, h, t: (0, h)), pl.BlockSpec((1, LANE), lambda b, h, t: (0, 0)),
                  slab(HG_Q0), slab(HG_F0), slab(HG_I0), slab(HG_G0), pl.BlockSpec(memory_space=pl.ANY)],
        out_specs=[head_out, mix_out,
                   pl.BlockSpec((None, HG_HPS, HG_NCH, LANE, LANE), lambda b, h, t: (b, h, t, 0, 0))],
        out_shape=[_sds((bsz, seq, HG_WIDTH), F32), _sds(mix_in.shape, BF16),
                   _sds((bsz, HG_HEADS, seq // HG_CHUNK, LANE, LANE), F32)],
        scratch_shapes=[pltpu.VMEM((HG_HPS, LANE, LANE), F32)],
        sem=("parallel", "parallel", "arbitrary"), comms=comms, aliases={6: 1})


def _out_proj_fused(cat, w_out, x, post_w, g1, pre_w, sc2, sh2):
    tm = 512

    def epi(mix, ex, outs):
        x_ref, pw_ref, g1_ref, w2_ref, sc_ref, sh_ref = ex
        outs[0][...] = mix
        n1, _, _ = _rms_fwd(mix, pw_ref[...])
        x1 = x_ref[...] + g1_ref[...] * n1
        outs[1][...] = x1
        y2, _, _ = _rms_fwd(x1, w2_ref[...])
        outs[2][...] = (y2 * (1.0 + sc_ref[...]) + sh_ref[...]).astype(BF16)

    return _mm_rows(cat, w_out, name="out_proj", tm=tm, extra=(x, post_w, g1, pre_w, sc2, sh2),
                    extra_specs=[_tok_spec(tm), _vec_spec(), _row_spec(), _vec_spec(), _row_spec(), _row_spec()],
                    out_specs=[_tok_spec(tm), _tok_spec(tm), _tok_spec(tm)],
                    out_shape=[_sds(x.shape, F32), _sds(x.shape, F32), _sds(x.shape, BF16)], epi=epi)


def _acc_out(ref, first, value):
    @pl.when(first)
    def _():
        ref[...] = value

    @pl.when(jnp.logical_not(first))
    def _():
        ref[...] += value


def _down_proj_fused(r, w_down, x1, post_w, g2, target):
    tm = 512
    bsz = x1.shape[0]

    def pro(r_ref, ex, outs):
        rv = r_ref[...]
        return rv * rv

    def epi(down, ex, outs):
        x1_ref, w_ref, g2_ref, t_ref = ex
        loss_ref, dy_ref, dd_ref, dg2_ref, dw_ref = outs
        w, g2v = w_ref[...], g2_ref[...]
        n2, dh, rstd = _rms_fwd(down, w)
        err = x1_ref[...] + g2v * n2 - t_ref[...]
        part = (0.5 / D_MODEL) * jnp.sum(jnp.sum(err * err, axis=-1, keepdims=True), axis=0, keepdims=True)
        loss_ref[...] += jnp.broadcast_to(part, (1, LANE))
        dy = err * (1.0 / D_MODEL)
        dy_ref[...] = dy
        dg2_ref[...] += _colsum(dy * n2)
        dd, dw_rows = _rms_bwd(dy * g2v, dh, rstd, w)
        dd_ref[...] = dd.astype(BF16)
        dw_ref[...] += _colsum(dw_rows)

    return _mm_rows(r, w_down, name="down_proj", tm=tm, extra=(x1, post_w, g2, target),
                    extra_specs=[_tok_spec(tm), _vec_spec(), _row_spec(), _tok_spec(tm)],
                    out_specs=[_vec_spec(LANE), _tok_spec(tm), _tok_spec(tm), _row_spec(), _vec_spec()],
                    out_shape=[_sds((1, LANE), F32), _sds(x1.shape, F32), _sds(x1.shape, BF16), _sds((bsz, 1, D_MODEL), F32),
                               _sds((1, D_MODEL), F32)], pro=pro, epi=epi, parts=2, zero_per_seq=(3,), zero_once=(0, 4),
                    vmem_limit=VMEM_LIMIT_BIG)


def _up_bwd_fused(dpre, w_up4, dy, x1, mix, pre_w, sc2, post_w, g1, comms=()):
    tm = 512
    bsz = x1.shape[0]

    def epi(dh2v, ex, outs):
        dy_ref, x1_ref, mix_ref, w2_ref, sc_ref, pw_ref, g1_ref = ex
        dx1_ref, dmix_ref, dsc_ref, dsh_ref, dg1_ref, dw2_ref, dpw_ref = outs
        w2, pw = w2_ref[...], pw_ref[...]
        y2, xh2, rstd2 = _rms_fwd(x1_ref[...], w2)
        dsh_ref[...] += _colsum(dh2v)
        dsc_ref[...] += _colsum(dh2v * y2)
        dx1n, dw_rows = _rms_bwd(dh2v * (1.0 + sc_ref[...]), xh2, rstd2, w2)
        dw2_ref[...] += _colsum(dw_rows)
        dx1 = dy_ref[...] + dx1n
        dx1_ref[...] = dx1
        n1, mh, rstd1 = _rms_fwd(mix_ref[...], pw)
        dg1_ref[...] += _colsum(dx1 * n1)
        dmix, dpw_rows = _rms_bwd(dx1 * g1_ref[...], mh, rstd1, pw)
        dmix_ref[...] = dmix.astype(BF16)
        dpw_ref[...] += _colsum(dpw_rows)

    row_shape = _sds((bsz, 1, D_MODEL), F32)
    vec_shape = _sds((1, D_MODEL), F32)
    return _mm_rows(dpre, w_up4, name="up_bwd", tm=tm, extra=(dy, x1, mix, pre_w, sc2, post_w, g1),
                    extra_specs=[_tok_spec(tm), _tok_spec(tm), _tok_spec(tm), _vec_spec(), _row_spec(), _vec_spec(), _row_spec()],
                    out_specs=[_tok_spec(tm), _tok_spec(tm), _row_spec(), _row_spec(), _row_spec(), _vec_spec(), _vec_spec()],
                    out_shape=[_sds(x1.shape, F32), _sds(x1.shape, BF16), row_shape, row_shape, row_shape, vec_shape, vec_shape],
                    epi=epi, b_chunks=w_up4.shape[0], comms=comms, parts=2, zero_per_seq=(2, 3, 4), zero_once=(5, 6),
                    vmem_limit=VMEM_LIMIT_BIG)


def _norm1_bwd(dh1, dx1, x, pre_w, sc1, tm=512, comms=()):
    bsz, seq, _ = x.shape

    def body(dh_ref, dx1_ref, x_ref, w_ref, sc_ref, gx_ref, dsc_ref, dsh_ref, dw_ref):
        b, i = pl.program_id(0), pl.program_id(1)
        w = w_ref[...]
        dh = dh_ref[...]
        y, xh, rstd = _rms_fwd(x_ref[...], w)
        _acc_out(dsh_ref, i == 0, _colsum(dh))
        _acc_out(dsc_ref, i == 0, _colsum(dh * y))
        dx, dw_rows = _rms_bwd(dh * (1.0 + sc_ref[...]), xh, rstd, w)
        _acc_out(dw_ref, jnp.logical_and(b == 0, i == 0), _colsum(dw_rows))
        gx_ref[...] = dx1_ref[...] + dx

    row_shape = _sds((bsz, 1, D_MODEL), F32)
    return _call(
        body, (dh1, dx1, x, pre_w, sc1), name="norm1_bwd", grid=(bsz, seq // tm),
        in_specs=[_tok_spec(tm), _tok_spec(tm), _tok_spec(tm), _vec_spec(), _row_spec()],
        out_specs=[_tok_spec(tm), _row_spec(), _row_spec(), _vec_spec()],
        out_shape=[_sds(x.shape, F32), row_shape, row_shape, _sds((1, D_MODEL), F32)],
        sem=("arbitrary", "arbitrary"), comms=comms)


def _hgrn_bwd(dcat, proj, o_raw, states, lb_table, norm_w, comms=()):
    bsz, seq, _ = proj.shape
    nstep = seq // HG_TOK
    rec0 = ATT_WIDTH // LANE

    def body(tbl_ref, nw_ref, dr_ref, q_ref, f_ref, i_ref, g_ref, o_ref, st_ref,
             dq_ref, df_ref, di_ref, dg_ref, dlb_ref, dnw_ref, ds_scr):
        h, b, t = pl.program_id(0), pl.program_id(1), pl.program_id(2)

        @pl.when(t == 0)
        def _():
            ds_scr[...] = jnp.zeros_like(ds_scr)

        lower, upper = _block_masks()
        dlb_parts = []
        dnw_acc = jnp.zeros((1, LANE), F32)
        for hp in range(HG_HPS):
            ls = slice(hp * LANE, (hp + 1) * LANE)
            hq, v, hg = q_ref[:, ls], i_ref[:, ls], g_ref[:, ls]
            nw = nw_ref[...]
            c = _hgrn_common(tbl_ref[:, ls], f_ref[:, ls], hq)
            qd, kd, ku = c["qd"], c["kd"], c["ku"]
            y, on, rstd = _rms_fwd(o_ref[:, ls], nw)
            sg = _sigmoid(hg)
            dr = dr_ref[:, ls]
            dg_ref[:, ls] = (dr * y * (sg * (1.0 + hg * (1.0 - sg)))).astype(BF16)
            do, dnw_rows = _rms_bwd(dr * (hg * sg), on, rstd, nw)
            at = jnp.where(upper, _dot(kd, qd, NT_DIMS), 0.0)
            da = jnp.where(lower, _dot(do, v, NT_DIMS), 0.0)
            dat = jnp.where(upper, _dot(v, do, NT_DIMS), 0.0)
            dv = _dot(at, do)
            dqd = _dot(da, kd)
            dkd = _dot(dat, qd)
            do_c, qd_c, v_c, ku_c = [_chunk_slices(z.astype(BF16)) for z in (do, qd, v, ku)]
            outer = [_dot(do_c[j], qd_c[j], TN_DIMS) for j in range(HG_NCH)]
            ds = ds_scr[hp]
            ds_after = [None] * HG_NCH
            for j in reversed(range(HG_NCH)):
                ds_after[j] = ds
                ds = outer[j] + ds * c["e_last"][j]
            ds_scr[hp] = ds
            states = [st_ref[hp, j] for j in range(HG_NCH)]
            dv = dv + jnp.concatenate([_dot(ku_c[j], ds_after[j], NT_DIMS) for j in range(HG_NCH)], axis=0)
            dqd = dqd + jnp.concatenate([_dot(do_c[j], states[j]) for j in range(HG_NCH)], axis=0)
            dku = jnp.concatenate([_dot(v_c[j], ds_after[j]) for j in range(HG_NCH)], axis=0)
            dku_ku = dku * ku
            dbl = [_colsum(states[j] * ds_after[j]) * c["e_last"][j] + _colsum(dku_ku[j * HG_CHUNK:(j + 1) * HG_CHUNK])
                   for j in range(HG_NCH)]
            dk = dkd * c["e_nb"] + dku * c["e_rem"]
            db = dqd * qd - dkd * kd - dku_ku + jnp.where(_row_in_chunk() == HG_CHUNK - 1, _chunk_rows(dbl), 0.0)
            dfv = _chunk_cumsum(db, reverse=True) / c["f"] - dk
            sig, sq = c["sig"], c["sq"]
            df_ref[:, ls] = (dfv * (1.0 - c["lb"]) * sig * (1.0 - sig)).astype(BF16)
            dq_ref[:, ls] = (dqd * c["e_b"] * (sq * (1.0 + hq * (1.0 - sq)))).astype(BF16)
            di_ref[:, ls] = dv.astype(BF16)
            dlb_parts.append(_colsum(dfv * (1.0 - sig)))
            dnw_acc = dnw_acc + _colsum(dnw_rows)
        _acc_out(dlb_ref, jnp.logical_and(b == 0, t == 0), jnp.concatenate(dlb_parts, axis=1))
        _acc_out(dnw_ref, jnp.logical_and(h == 0, jnp.logical_and(b == 0, t == 0)), dnw_acc)

    rev = lambda t: nstep - 1 - t
    width = HG_HPS * LANE
    slab = lambda first: pl.BlockSpec((None, HG_TOK, width), lambda h, b, t: (b, rev(t), first // HG_HPS + h))
    head = pl.BlockSpec((None, HG_TOK, width), lambda h, b, t: (b, rev(t), h))
    grad_shape = _sds((bsz, seq, HG_WIDTH), BF16)
    return _call(
        body, (lb_table, norm_w, dcat, proj, proj, proj, proj, o_raw, states), name="hgrn_bwd",
        grid=(HG_HEADS // HG_HPS, bsz, nstep),
        in_specs=[pl.BlockSpec((2, width), lambda h, b, t: (0, h)), pl.BlockSpec((1, LANE), lambda h, b, t: (0, 0)),
                  slab(rec0), slab(HG_Q0), slab(HG_F0), slab(HG_I0), slab(HG_G0), head,
                  pl.BlockSpec((None, HG_HPS, HG_NCH, LANE, LANE), lambda h, b, t: (b, h, rev(t), 0, 0))],
        out_specs=[head, head, head, head, pl.BlockSpec((1, width), lambda h, b, t: (0, h)),
                   pl.BlockSpec((1, LANE), lambda h, b, t: (0, 0))],
        out_shape=[grad_shape, grad_shape, grad_shape, grad_shape, _sds((1, HG_WIDTH), F32), _sds((1, LANE), F32)],
        scratch_shapes=[pltpu.VMEM((HG_HPS, LANE, LANE), F32)],
        sem=("arbitrary", "arbitrary", "arbitrary"), comms=comms)


def _attn_bwd(dcat, raw, w_norm, qh, kh, vh, lse, sinks, tables, comms=()):
    bsz, nblk = qh.shape[0], qh.shape[1]
    seq = nblk * WINDOW
    nstep = nblk // ATT_BPS
    half = ROPE_DIM // 2

    def body(sink_ref, da_ref, raw_ref, w_ref, q_ref, kc_ref, kp_ref, vc_ref, vp_ref, l_ref, c_ref, u_ref, d_ref,
             o_ref, dw_ref, dsink_ref, carry_k, carry_v):
        b, i = pl.program_id(0), pl.program_id(1)
        first = jnp.logical_and(b == 0, i == 0)

        @pl.when(i == 0)
        def _():
            carry_k[...] = jnp.zeros_like(carry_k)
            carry_v[...] = jnp.zeros_like(carry_v)

        w = w_ref[...]
        _, on, rstd = _rms_fwd(raw_ref[...], w)
        do_step, dw_rows = _rms_bwd(da_ref[...], on, rstd, w)
        _acc_out(dw_ref, first, _colsum(dw_rows))
        lane8 = lax.broadcasted_iota(jnp.int32, (1, ATT_Q_HEADS), 1)
        dsink = jnp.zeros((1, ATT_Q_HEADS), F32)
        from_next_k, from_next_v = carry_k[...], carry_v[...]
        for blk in reversed(range(ATT_BPS)):
            tok = slice(blk * WINDOW, (blk + 1) * WINDOW)
            mask = _band_mask(True if blk else i < nstep - 1)
            raw_v, do_all = raw_ref[tok, :], do_step[tok]
            c, u, d = c_ref[tok, :], u_ref[tok, :], d_ref[tok, :]

            def unrope(g):
                return (g * c + pltpu.roll(g * u, LANE - half, 1) + pltpu.roll(g * d, half, 1)).astype(BF16)

            dq_parts, dk_own, dk_before, dv_own, dv_before = [], [], [], [], []
            for g in range(ATT_KV_HEADS):
                heads = [slice((g * ATT_GROUP + hh) * ATT_HEAD_DIM, (g * ATT_GROUP + hh + 1) * ATT_HEAD_DIM)
                         for hh in range(ATT_GROUP)]
                q = q_ref[blk, g]
                keys, vals = _band(kp_ref, kc_ref, g, blk), _band(vp_ref, vc_ref, g, blk)
                do_g = jnp.concatenate([do_all[:, hs] for hs in heads], axis=0)
                dsum = jnp.concatenate([jnp.sum(do_all[:, hs] * raw_v[:, hs], axis=-1, keepdims=True) for hs in heads], axis=0)
                lse_g = jnp.concatenate([l_ref[tok, g * ATT_GROUP + hh:g * ATT_GROUP + hh + 1] for hh in range(ATT_GROUP)], axis=0)
                p = jnp.where(mask, jnp.exp(_dot(q, keys, NT_DIMS) * ATT_SCALE - lse_g), 0.0)
                sink_part = jnp.exp(_sink_column(sink_ref, g) - lse_g) * dsum
                for hh in range(ATT_GROUP):
                    head_sum = jnp.sum(sink_part[hh * WINDOW:(hh + 1) * WINDOW], axis=0, keepdims=True)
                    dsink = dsink - jnp.where(lane8 == g * ATT_GROUP + hh, head_sum, 0.0)
                ds = p * (_dot(do_g, vals, NT_DIMS) - dsum) * ATT_SCALE
                dq_g = _dot(ds, keys)
                dq_parts += [dq_g[hh * WINDOW:(hh + 1) * WINDOW] for hh in range(ATT_GROUP)]
                dk_g = _dot(ds, q, TN_DIMS)
                dv_g = _dot(p, do_g, TN_DIMS)
                dk_before.append(dk_g[:WINDOW])
                dk_own.append(dk_g[WINDOW:])
                dv_before.append(dv_g[:WINDOW])
                dv_own.append(dv_g[WINDOW:])
            per_slab = LANE // ATT_HEAD_DIM
            for s in range(ATT_WIDTH // LANE):
                slab = jnp.concatenate(dq_parts[s * per_slab:(s + 1) * per_slab], axis=1)
                o_ref[tok, s * LANE:(s + 1) * LANE] = unrope(slab)
            o_ref[tok, ATT_WIDTH:ATT_WIDTH + LANE] = unrope(jnp.concatenate(dk_own, axis=1) + from_next_k)
            o_ref[tok, ATT_WIDTH + LANE:ATT_COLS] = (jnp.concatenate(dv_own, axis=1) + from_next_v).astype(BF16)
            from_next_k, from_next_v = jnp.concatenate(dk_before, axis=1), jnp.concatenate(dv_before, axis=1)
        carry_k[...] = from_next_k
        carry_v[...] = from_next_v
        _acc_out(dsink_ref, first, dsink)

    rows = ATT_BPS * WINDOW
    rev = lambda i: nstep - 1 - i
    cur = lambda width: pl.BlockSpec((None, rows, width), lambda b, i: (b, rev(i), 0))
    q_spec = pl.BlockSpec((None, ATT_BPS, ATT_KV_HEADS, GROUP_ROWS, ATT_HEAD_DIM), lambda b, i: (b, rev(i), 0, 0, 0))
    kv_cur = pl.BlockSpec((None, ATT_KV_HEADS, rows, ATT_HEAD_DIM), lambda b, i: (b, 0, rev(i), 0))
    kv_prev = pl.BlockSpec((None, ATT_KV_HEADS, WINDOW, ATT_HEAD_DIM), lambda b, i: (b, 0, jnp.maximum(ATT_BPS * rev(i) - 1, 0), 0))
    tab = pl.BlockSpec((rows, LANE), lambda b, i: (rev(i), 0))
    return _call(
        body, (sinks, dcat, raw, w_norm, qh, kh, kh, vh, vh, lse, *tables), name="attn_bwd", grid=(bsz, nstep),
        in_specs=[pl.BlockSpec(memory_space=pltpu.SMEM), cur(ATT_WIDTH), cur(ATT_WIDTH), _vec_spec(ATT_WIDTH), q_spec,
                  kv_cur, kv_prev, kv_cur, kv_prev, cur(ATT_Q_HEADS), tab, tab, tab],
        out_specs=[cur(ATT_COLS), _vec_spec(ATT_WIDTH), _vec_spec(ATT_Q_HEADS)],
        out_shape=[_sds((bsz, seq, ATT_COLS), BF16), _sds((1, ATT_WIDTH), F32), _sds((1, ATT_Q_HEADS), F32)],
        scratch_shapes=[pltpu.VMEM((WINDOW, LANE), F32), pltpu.VMEM((WINDOW, LANE), F32)],
        sem=("arbitrary", "arbitrary"), comms=comms)


def _other_chips(x, y):
    return [(1 - x, y), (x, 1 - y), (1 - x, 1 - y)]


def _sem_pair(n):
    return [pltpu.SemaphoreType.DMA((n,)), pltpu.SemaphoreType.DMA((n,))]


def _plan_pair_forward(bufs):
    n = len(bufs)

    def copies(outs, sems):
        x, y, c = _mesh_pos()
        sends, lands = [], []
        for a in range(n):
            for j, chip in enumerate(_other_chips(x, y)):
                k = 3 * a + j
                slot = outs[a].at[4 * chip[0] + 2 * chip[1] + c]
                sends.append(pltpu.make_async_remote_copy(
                    src_ref=slot, dst_ref=slot, send_sem=sems[0].at[k], recv_sem=sems[1].at[k],
                    device_id=(x, y, 1 - c), device_id_type=MESH))
                theirs = outs[a].at[4 * chip[0] + 2 * chip[1] + 1 - c]
                lands.append(pltpu.make_async_remote_copy(
                    src_ref=theirs, dst_ref=theirs, send_sem=sems[0].at[k], recv_sem=sems[1].at[k],
                    device_id=(x, y, 1 - c), device_id_type=MESH))
        return sends, lands

    def start(ins, outs, sems):
        for cp in copies(outs, sems)[0]:
            cp.start()

    def finish(ins, outs, sems):
        sends, lands = copies(outs, sems)
        for cp in lands:
            cp.wait_recv()
        for cp in sends:
            cp.wait_send()

    return _Comm(list(bufs), [_sds(b.shape, b.dtype) for b in bufs], _sem_pair(3 * n), start, finish,
                 aliases=[(a, a) for a in range(n)])


def _plan_pair(arrays, other_half):
    n = len(arrays)
    per = N_CHIPS if other_half == "chip_major" else 1

    def copies(ins, outs, sems):
        x, y, c = _mesh_pos()
        out = []
        for a in range(n):
            for k in range(per):
                if other_half == "chip_major":
                    src, dst = ins[a].at[k, 1 - c], outs[a].at[k]
                else:
                    src, dst = (ins[a].at[1 - c] if other_half else ins[a]), outs[a]
                out.append(pltpu.make_async_remote_copy(
                    src_ref=src, dst_ref=dst, send_sem=sems[0].at[per * a + k], recv_sem=sems[1].at[per * a + k],
                    device_id=(x, y, 1 - c), device_id_type=MESH))
        return out

    def start(ins, outs, sems):
        for cp in copies(ins, outs, sems):
            cp.start()

    def finish(ins, outs, sems):
        for cp in copies(ins, outs, sems):
            cp.wait()

    if other_half == "chip_major":
        shapes = [_sds((a.shape[0],) + a.shape[2:], a.dtype) for a in arrays]
    else:
        shapes = [_sds(a.shape[1:] if other_half else a.shape, a.dtype) for a in arrays]
    return _Comm(list(arrays), shapes, _sem_pair(per * n), start, finish)


def _plan_chip_exchange(arrays):
    n = len(arrays)

    def copies(ins, outs, sems):
        x, y, c = _mesh_pos()
        sends, lands = [], []
        for a in range(n):
            for j, chip in enumerate(_other_chips(x, y)):
                k = 3 * a + j
                sends.append(pltpu.make_async_remote_copy(
                    src_ref=ins[a].at[2 * chip[0] + chip[1]], dst_ref=outs[a].at[2 * x + y], send_sem=sems[0].at[k],
                    recv_sem=sems[1].at[k], device_id=(*chip, c), device_id_type=MESH))
                slot = outs[a].at[2 * chip[0] + chip[1]]
                lands.append(pltpu.make_async_remote_copy(
                    src_ref=slot, dst_ref=slot, send_sem=sems[0].at[k], recv_sem=sems[1].at[k],
                    device_id=(*chip, c), device_id_type=MESH))
        return sends, lands

    def start(ins, outs, sems):
        for cp in copies(ins, outs, sems)[0]:
            cp.start()

    def finish(ins, outs, sems):
        sends, lands = copies(ins, outs, sems)
        for cp in lands:
            cp.wait_recv()
        for cp in sends:
            cp.wait_send()

    return _Comm(list(arrays), [_sds(a.shape, a.dtype) for a in arrays], _sem_pair(3 * n), start, finish)


SEM_SPEC = pl.BlockSpec(memory_space=pltpu.SEMAPHORE)
N_OTHER = N_CHIPS - 1


def _exchange_copies(s_ref, land_ref, sems):
    x, y, c = _mesh_pos()
    return [pltpu.make_async_remote_copy(
        src_ref=s_ref.at[2 * chip[0] + chip[1]], dst_ref=land_ref.at[2 * x + y], send_sem=sems[j], recv_sem=sems[N_OTHER + j],
        device_id=(*chip, c), device_id_type=MESH) for j, chip in enumerate(_other_chips(x, y))]


def _exchange_start(s, name):
    def body(s_ref, land_ref, *outs):
        sems, token = outs[:2 * N_OTHER], outs[-1]
        for cp in _exchange_copies(s_ref, land_ref, sems):
            cp.start()
        token[...] = jnp.zeros_like(token)

    hbm = pltpu.HBM(s.shape, s.dtype)
    res = pl.pallas_call(
        body, name=name,
        out_shape=(pltpu.SemaphoreType.DMA(()),) * (2 * N_OTHER) + (hbm, hbm, _sds((SUBLANES, LANE), F32)),
        in_specs=(HBM_SPEC, HBM_SPEC),
        out_specs=(SEM_SPEC,) * (2 * N_OTHER) + (HBM_SPEC, HBM_SPEC, pl.BlockSpec(memory_space=pltpu.VMEM)),
        input_output_aliases={0: 2 * N_OTHER, 1: 2 * N_OTHER + 1},
        compiler_params=pltpu.CompilerParams(has_side_effects=pltpu.SideEffectType.DATAFLOW_SIDE_EFFECTING),
    )(pltpu.with_memory_space_constraint(s, pltpu.HBM), pltpu.with_memory_space_constraint(lax.empty(s.shape, s.dtype), pltpu.HBM))
    return res[:2 * N_OTHER], res[2 * N_OTHER], res[2 * N_OTHER + 1], res[-1]


def _exchange_wait(sems, s_thru, land_thru, afters, name):
    def body(s_ref, land_ref, *rest):
        for cp in _exchange_copies(s_ref, land_ref, rest[:2 * N_OTHER]):
            cp.wait_send()
            cp.wait_recv()

    hbm = pltpu.HBM(s_thru.shape, s_thru.dtype)
    return pl.pallas_call(
        body, name=name, out_shape=(hbm, hbm),
        in_specs=(HBM_SPEC, HBM_SPEC) + (SEM_SPEC,) * (2 * N_OTHER) + (pl.BlockSpec(memory_space=pl.ANY),) * len(afters),
        out_specs=(HBM_SPEC, HBM_SPEC), input_output_aliases={0: 0, 1: 1},
        compiler_params=pltpu.CompilerParams(has_side_effects=pltpu.SideEffectType.DATAFLOW_SIDE_EFFECTING),
    )(s_thru, land_thru, *sems, *afters)


def _gather_copies(block_ref, buf_ref, sems):
    x, y, c = _mesh_pos()
    return [pltpu.make_async_remote_copy(
        src_ref=block_ref, dst_ref=buf_ref.at[4 * x + 2 * y + c], send_sem=sems[j], recv_sem=sems[N_OTHER + j],
        device_id=(*chip, c), device_id_type=MESH) for j, chip in enumerate(_other_chips(x, y))]


def _gather_start(blocks, bufs, afters, name):
    n = len(blocks)
    per = 2 * N_OTHER

    def body(*refs):
        ins, outs = refs[:2 * n], refs[2 * n + len(afters):]
        for a in range(n):
            for cp in _gather_copies(ins[a], ins[n + a], outs[a * per:(a + 1) * per]):
                cp.start()
        outs[-1][...] = jnp.zeros_like(outs[-1])

    hbm = [pltpu.HBM(z.shape, z.dtype) for z in list(blocks) + list(bufs)]
    res = pl.pallas_call(
        body, name=name,
        out_shape=(pltpu.SemaphoreType.DMA(()),) * (n * per) + tuple(hbm) + (_sds((SUBLANES, LANE), F32),),
        in_specs=(HBM_SPEC,) * (2 * n) + (pl.BlockSpec(memory_space=pl.ANY),) * len(afters),
        out_specs=(SEM_SPEC,) * (n * per) + (HBM_SPEC,) * (2 * n) + (pl.BlockSpec(memory_space=pltpu.VMEM),),
        input_output_aliases={k: n * per + k for k in range(2 * n)},
        compiler_params=pltpu.CompilerParams(has_side_effects=pltpu.SideEffectType.DATAFLOW_SIDE_EFFECTING),
    )(*[pltpu.with_memory_space_constraint(z, pltpu.HBM) for z in list(blocks) + list(bufs)], *afters)
    parts = [(res[a * per:(a + 1) * per], res[n * per + a], res[n * per + n + a]) for a in range(n)]
    return parts, res[-1]


def _gather_wait(part, afters, name):
    sems, block, buf = part

    def body(block_ref, buf_ref, *rest):
        for cp in _gather_copies(block_ref, buf_ref, rest[:2 * N_OTHER]):
            cp.wait_send()
            cp.wait_recv()

    return pl.pallas_call(
        body, name=name, out_shape=(pltpu.HBM(block.shape, block.dtype), pltpu.HBM(buf.shape, buf.dtype)),
        in_specs=(HBM_SPEC, HBM_SPEC) + (SEM_SPEC,) * (2 * N_OTHER) + (pl.BlockSpec(memory_space=pl.ANY),) * len(afters),
        out_specs=(HBM_SPEC, HBM_SPEC), input_output_aliases={0: 0, 1: 1},
        compiler_params=pltpu.CompilerParams(has_side_effects=pltpu.SideEffectType.DATAFLOW_SIDE_EFFECTING),
    )(block, buf, *sems, *afters)[1]


def _comm_only(comms, name):
    return _call(lambda: None, (), name=name, grid=(), in_specs=[], out_specs=[], out_shape=[], sem=(), comms=comms)[1]


def _allgather8(arrays, name):
    return _comm_only([_plan_allgather8(arrays)], name)[0]


def _plan_allgather8(arrays):
    n = len(arrays)

    def parts(ins, outs, sems):
        send_sems, recv_sems, local_sems = sems
        x, y, c = _mesh_pos()
        me, sibling = (x, y, c), (x, y, 1 - c)
        chips = _other_chips(x, y)

        def copy(a, k, block, to, src=None):
            dst = outs[a].at[4 * block[0] + 2 * block[1] + block[2]]
            return pltpu.make_async_remote_copy(
                src_ref=dst if src is None else src, dst_ref=dst, send_sem=send_sems.at[7 * a + k],
                recv_sem=recv_sems.at[7 * a + k], device_id=to, device_id_type=MESH)

        mine = [pltpu.make_async_copy(ins[a], outs[a].at[4 * x + 2 * y + c], local_sems.at[a]) for a in range(n)]
        first = []
        for a in range(n):
            first.append(copy(a, 0, me, sibling, src=ins[a]))
            first += [copy(a, 1 + j, me, (*chip, c), src=ins[a]) for j, chip in enumerate(chips)]
        return copy, mine, first, me, sibling, chips, c

    def start(ins, outs, sems):
        _, mine, first, *_ = parts(ins, outs, sems)
        for cp in mine + first:
            cp.start()

    def finish(ins, outs, sems):
        copy, mine, first, me, sibling, chips, c = parts(ins, outs, sems)
        passed = []
        for j, chip in enumerate(chips):
            for a in range(n):
                copy(a, 1 + j, (*chip, c), me).wait_recv()
                fwd = copy(a, 4 + j, (*chip, c), sibling)
                fwd.start()
                passed.append(fwd)
        for a in range(n):
            copy(a, 0, sibling, me).wait_recv()
            for j, chip in enumerate(chips):
                copy(a, 4 + j, (*chip, 1 - c), me).wait_recv()
        for cp in first + passed:
            cp.wait_send()
        for cp in mine:
            cp.wait()

    sems = [pltpu.SemaphoreType.DMA((7 * n,)), pltpu.SemaphoreType.DMA((7 * n,)), pltpu.SemaphoreType.DMA((n,))]
    return _Comm(list(arrays), [_sds((N_DEV,) + a.shape, a.dtype) for a in arrays], sems, start, finish)


def _pair_sum(g, q, core, name, chip_major=False):
    rows, cols = g.shape[2:]
    tr = _row_tile(rows)

    def body(core_ref, g_ref, q_ref, o_ref):
        o_ref[...] = (g_ref[...] + q_ref[...]).astype(BF16)

    blk = pl.BlockSpec((None, tr, cols), lambda k, i, core_ref: (k, i, 0))
    if chip_major:
        own = pl.BlockSpec((None, None, tr, cols), lambda k, i, core_ref: (k, core_ref[0], i, 0))
    else:
        own = pl.BlockSpec((None, None, tr, cols), lambda k, i, core_ref: (core_ref[0], k, i, 0))
    return pl.pallas_call(
        body, name=name,
        grid_spec=pltpu.PrefetchScalarGridSpec(num_scalar_prefetch=1, grid=(N_CHIPS, rows // tr), in_specs=[own, blk], out_specs=blk),
        out_shape=_sds((N_CHIPS, rows, cols), BF16), compiler_params=_params("parallel", "parallel"),
    )(core, g, q)


def _sum_chips(own, landed, chip, name):
    _, rows, cols = own.shape
    tr = _row_tile(rows)

    def body(chip_ref, own_ref, a_ref, b_ref, c_ref, o_ref):
        acc = own_ref[...].astype(F32) + a_ref[...].astype(F32)
        o_ref[...] = (acc + b_ref[...].astype(F32)) + c_ref[...].astype(F32)

    blk = lambda flip: pl.BlockSpec((None, tr, cols), lambda i, chip_ref: (jnp.bitwise_xor(chip_ref[0], flip), i, 0))
    return pl.pallas_call(
        body, name=name,
        grid_spec=pltpu.PrefetchScalarGridSpec(num_scalar_prefetch=1, grid=(rows // tr,), in_specs=[blk(0), blk(1), blk(2), blk(3)],
                                               out_specs=pl.BlockSpec((tr, cols), lambda i, chip_ref: (i, 0))),
        out_shape=_sds((rows, cols), F32), compiler_params=_params("parallel"),
    )(chip, own, landed, landed, landed)


SUBLANES = 8


def _tile_rows(n_elems):
    return -(-n_elems // (SUBLANES * LANE)) * SUBLANES


SMALL_ITEMS = (("b_ada", N_MOD * D_MODEL), ("pre_w_mix", D_MODEL), ("post_w_mix", D_MODEL), ("pre_w_mlp", D_MODEL),
               ("post_w_mlp", D_MODEL), ("attn_out_w", ATT_WIDTH), ("hg_norm_w", HG_HEAD_DIM), ("attn_sinks", ATT_Q_HEADS),
               ("lb_0", HG_WIDTH), ("lb_1", HG_WIDTH))
SMALL_AT = {}
for _name, _size in SMALL_ITEMS:
    SMALL_AT[_name] = (sum(r for _, r in SMALL_AT.values()), _tile_rows(_size))
SMALL_ROWS = sum(r for _, r in SMALL_AT.values())
MOD_ROWS = SMALL_AT["b_ada"][1]
PLAIN_ROWS = SMALL_AT["lb_0"][0] - MOD_ROWS
LB_ROWS = SMALL_AT["lb_0"][1]


def _rows(a, nrows=None):
    flat = a.reshape(-1)
    nrows = _tile_rows(flat.shape[0]) if nrows is None else nrows
    return jnp.pad(flat, (0, nrows * LANE - flat.shape[0])).reshape(nrows, LANE)


def _pack_small(vals):
    vals = dict(vals, lb_0=vals["lb_table"][0], lb_1=vals["lb_table"][1])
    return jnp.concatenate([_rows(vals[name], SMALL_AT[name][1]) for name, _ in SMALL_ITEMS], axis=0)


def _unpack_small(p):
    def item(name, shape):
        first = SMALL_AT[name][0]
        size = shape[0] * shape[1]
        return p[first:first + SMALL_AT[name][1]].reshape(-1)[:size].reshape(shape)

    out = {name: item(name, (1, size)) for name, size in SMALL_ITEMS if not name.startswith("lb_")}
    out["lb_table"] = jnp.concatenate([item("lb_0", (1, HG_WIDTH)), item("lb_1", (1, HG_WIDTH))], axis=0)
    return out


def _pack_partials(dmod, plain, d_lb, loss_row):
    return jnp.concatenate([_rows(dmod, dmod.shape[0] * MOD_ROWS)] + [_rows(g) for g in plain] + [_rows(d_lb), _rows(loss_row)], axis=0)


def _small_update(packs, w, m, v, n_seq):
    mod_end = n_seq * MOD_ROWS
    lb_at = mod_end + PLAIN_ROWS
    t0, t1 = SMALL_AT["lb_0"][0], SMALL_AT["lb_1"][0]

    def body(p_ref, w_ref, m_ref, v_ref, g_ref, dl_ref, nm_ref, nv_ref, loss_ref):
        tot = p_ref[0]
        for d in range(1, N_DEV):
            tot = tot + p_ref[d]
        wv = w_ref[...]
        p1 = _sigmoid(wv[t1:t1 + LB_ROWS] - wv[t0:t0 + LB_ROWS])
        s = tot[lb_at:lb_at + LB_ROWS] * p1 * (1.0 - p1)
        g_bias = tot[0:MOD_ROWS]
        for q in range(1, n_seq):
            g_bias = g_bias + tot[q * MOD_ROWS:(q + 1) * MOD_ROWS]
        g = jnp.concatenate([g_bias, tot[mod_end:lb_at], -s, s], axis=0)
        g_ref[...] = g
        dl_ref[...], nm_ref[...], nv_ref[...] = _adamw_math(g, wv, m_ref[...], v_ref[...])
        loss_ref[...] = tot[lb_at + LB_ROWS:lb_at + LB_ROWS + SUBLANES]

    shp = _sds((SMALL_ROWS, LANE), F32)
    return pl.pallas_call(body, name="small_update", out_shape=[shp] * 4 + [_sds((SUBLANES, LANE), F32)],
                          compiler_params=_params())(packs, w, m, v)


def kernel(x, c, w_ada, b_ada, pre_w_mix, w_in, attn_sinks, attn_out_w, lb_table, hg_norm_w, w_out, post_w_mix, pre_w_mlp, w_up, w_down, post_w_mlp, loss_target, m_w_ada, m_b_ada, m_pre_w_mix, m_w_in, m_attn_sinks, m_attn_out_w, m_lb_table, m_hg_norm_w, m_w_out, m_post_w_mix, m_pre_w_mlp, m_w_up, m_w_down, m_post_w_mlp, v_w_ada, v_b_ada, v_pre_w_mix, v_w_in, v_attn_sinks, v_attn_out_w, v_lb_table, v_hg_norm_w, v_w_out, v_post_w_mix, v_pre_w_mlp, v_w_up, v_w_down, v_post_w_mlp):
    xi, yi, ci = _mesh_pos()
    chip = 2 * xi + yi
    dev = 2 * chip + ci
    bsz, seq, _ = x.shape
    ntok = bsz * seq
    ada_cols = w_ada.shape[2]
    core = jnp.reshape(ci, (1,)).astype(jnp.int32)
    chip_idx = jnp.reshape(chip, (1,)).astype(jnp.int32)
    flat = lambda a: a.reshape(ntok, a.shape[-1])
    unflat = lambda a: a.reshape(bsz, seq, a.shape[-1])
    tables = _rope_tables(seq)

    def row_half(w):
        rows = w.shape[1] // 2
        return lax.dynamic_slice_in_dim(w[0], ci * rows, rows, axis=0).astype(BF16)

    def gather_buffer(w):
        rows, cols = w.shape[1] // 2, w.shape[2]
        own = w[0].astype(BF16).reshape(2, rows, cols)
        return lax.dynamic_update_slice(lax.empty((N_DEV, rows, cols), BF16), own, (2 * chip, 0, 0))

    w_in_t, m_in_t, v_in_t = [jnp.transpose(a[0])[None] for a in (w_in, m_w_in, v_w_in)]
    c_g, in_g = _allgather8([c, row_half(w_in_t)], "gather_first")
    c_all = c_g.reshape(N_DEV * bsz, D_MODEL)
    w_in_full = in_g.reshape(IN_COLS, D_MODEL)

    b_cols = lax.dynamic_slice_in_dim(b_ada, chip * ada_cols, ada_cols, axis=1)
    mod_part = _ada_fwd(c_all, w_ada[0], b_cols)
    half_rows = mod_part.shape[0] // 2
    (mod_g,) = _allgather8([lax.dynamic_slice_in_dim(mod_part, ci * half_rows, half_rows, axis=0)], "gather_mod")
    mod_all = mod_g.reshape(N_CHIPS, 2, half_rows, ada_cols).transpose(1, 2, 0, 3).reshape(N_DEV * bsz, N_MOD * D_MODEL)
    mod = lax.dynamic_slice_in_dim(mod_all, dev * bsz, bsz, axis=0)
    sh1, sc1, g1, sh2, sc2, g2 = [mod[:, i * D_MODEL:(i + 1) * D_MODEL].reshape(bsz, 1, D_MODEL) for i in range(N_MOD)]

    weights = (w_out, w_up, w_down)
    (out_part, up_part, down_part), started = _gather_start(
        [row_half(w) for w in weights], [gather_buffer(w) for w in weights], [mod_g], "gather_weights_start")

    h1, proj, qh, kh, vh = _in_proj_fused(x, pre_w_mix, sc1 + started[0:1, 0:1], sh1, w_in_full, tables)
    out_g = _gather_wait(out_part, [proj], "gather_out_wait")
    (attn_raw, cat, lse), ((out_g,),) = _attn_fwd(qh, kh, vh, attn_sinks, attn_out_w, comms=[_plan_pair_forward([out_g])])
    up_g = _gather_wait(up_part, [attn_raw], "gather_up_wait")
    (o_raw, cat, states), ((up_g,),) = _hgrn_fwd(proj, lb_table, hg_norm_w, cat, comms=[_plan_pair_forward([up_g])])
    down_g = _gather_wait(down_part, [o_raw], "gather_down_wait")
    w_out_full = out_g.reshape(D_MODEL, D_MODEL)
    w_up4 = up_g.reshape(N_CHIPS, D_MODEL, D_MODEL)
    mix, x1, h2 = _out_proj_fused(cat, w_out_full, x, post_w_mix, g1, pre_w_mlp, sc2, sh2)
    big_tm = min(ntok, 2048)
    up_spec = pl.BlockSpec((None, D_MODEL, D_MODEL), lambda i, j: (j, 0, 0))
    r, ((down_g,),) = _mm(flat(h2), w_up4, name="up_proj", out_dtype=BF16, tm=big_tm, tn=D_MODEL, n_out=D_FF, b_spec=up_spec,
                          epi=lambda acc: jnp.maximum(acc, 0.0), comms=[_plan_pair_forward([down_g])])
    w_down_full = down_g.reshape(D_FF, D_MODEL)
    square = lambda t: t * t
    loss_row, dy, dd, dg2, d_post_mlp = _down_proj_fused(unflat(r), w_down_full, x1, post_w_mlp, g2, loss_target)

    dpre = _mm(flat(dd), w_down_full, name="down_bwd", out_dtype=BF16, trans_b=True, tm=big_tm, tn=D_MODEL, extra=(r,),
               epi=lambda acc, rt: acc * (2.0 * rt.astype(F32)))
    half_rows = D_MODEL // 2
    g_down = _mm_tn(r, flat(dd), name="down_wgrad", tk=half_rows, tn=D_MODEL, a_fn=square,
                    out_shape=_sds((2, N_CHIPS, half_rows, D_MODEL), F32),
                    out_spec=pl.BlockSpec((None, None, half_rows, D_MODEL), lambda i, j: (i % 2, i // 2, 0, 0)))
    (dx1, dmix, dsc2, dsh2, dg1, d_pre_mlp, d_post_mix), ((q_down,),) = _up_bwd_fused(
        unflat(dpre), w_up4, dy, x1, mix, pre_w_mlp, sc2, post_w_mix, g1, comms=[_plan_pair([g_down], True)])
    g_up = _mm_tn(flat(h2), dpre, name="up_wgrad", tk=D_MODEL, tn=half_rows,
                  out_shape=_sds((2, N_CHIPS, half_rows, D_MODEL), F32),
                  out_spec=pl.BlockSpec((2, None, half_rows, half_rows), lambda i, j: (0, j // 2, 0, j % 2)))
    s_down = _pair_sum(g_down, q_down, core, "pair_sum_down")

    dcat, ((q_up,),) = _mm(flat(dmix), w_out_full, name="out_bwd", out_dtype=F32, trans_b=True, comms=[_plan_pair([g_up], True)])
    dcat = unflat(dcat)
    s_up = _pair_sum(g_up, q_up, core, "pair_sum_up")
    out_rows = D_MODEL // N_CHIPS
    g_out = _mm_tn(flat(cat), flat(dmix), name="out_wgrad", tk=2 * out_rows, tn=half_rows,
                   out_shape=_sds((2, N_CHIPS, out_rows, half_rows), F32),
                   out_spec=pl.BlockSpec((None, 2, out_rows, half_rows), lambda i, j: (j, i, 0, 0)))
    (dhq, dhf, dhi, dhg, d_lb, d_hg_norm), ((x_down,), (q_out,)) = _hgrn_bwd(
        dcat, proj, o_raw, states, lb_table, hg_norm_w, comms=[_plan_chip_exchange([s_down]), _plan_pair([g_out], True)])
    half_down = _sum_chips(s_down, x_down, chip_idx, "sum_chips_down")
    s_out = _pair_sum(g_out, q_out, core, "pair_sum_out")
    (dproj_a, d_attn_out, d_sinks), ((their_down,), (x_up, x_out)) = _attn_bwd(
        dcat, attn_raw, attn_out_w, qh, kh, vh, lse, attn_sinks, tables,
        comms=[_plan_pair([half_down], False), _plan_chip_exchange([s_up, s_out])])
    half_up = _sum_chips(s_up, x_up, chip_idx, "sum_chips_up")
    half_out = _sum_chips(s_out, x_out, chip_idx, "sum_chips_out")
    dproj = flat(jnp.concatenate([dproj_a, dhq, dhf, dhi, dhg], axis=-1))
    in_rows = IN_COLS // N_CHIPS // 2
    g_in = _mm_tn(dproj, flat(h1), name="in_wgrad", tk=2 * LANE, tn=D_MODEL).reshape(N_CHIPS, 2, in_rows, D_MODEL)
    dh1, ((q_in,), (their_up, their_out)) = _mm(
        dproj, w_in_full, name="in_bwd", out_dtype=F32,
        comms=[_plan_pair([g_in], "chip_major"), _plan_pair([half_up, half_out], False)])
    s_in = _pair_sum(g_in, q_in, core, "pair_sum_in", chip_major=True)
    in_sems, s_in, in_landing, started = _exchange_start(s_in, "exchange_in_start")
    grad_x, dsc1, dsh1, d_pre_mix = _norm1_bwd(unflat(dh1), dx1, x, pre_w_mix + started[0:1, 0:1], sc1)

    dmod = jnp.concatenate([dsh1, dsc1, dg1, dsh2, dsc2, dg2], axis=-1).reshape(bsz, N_MOD * D_MODEL)
    pack = _pack_partials(dmod, [d_pre_mix, d_post_mix, d_pre_mlp, d_post_mlp, d_attn_out, d_hg_norm, d_sinks], d_lb, loss_row)
    ((packs,),) = _comm_only([_plan_allgather8([pack])], "gather_small")
    w_small = dict(b_ada=b_ada, pre_w_mix=pre_w_mix, post_w_mix=post_w_mix, pre_w_mlp=pre_w_mlp, post_w_mlp=post_w_mlp,
                   attn_out_w=attn_out_w, hg_norm_w=hg_norm_w, attn_sinks=attn_sinks, lb_table=lb_table)
    m_small = dict(b_ada=m_b_ada, pre_w_mix=m_pre_w_mix, post_w_mix=m_post_w_mix, pre_w_mlp=m_pre_w_mlp, post_w_mlp=m_post_w_mlp,
                   attn_out_w=m_attn_out_w, hg_norm_w=m_hg_norm_w, attn_sinks=m_attn_sinks, lb_table=m_lb_table)
    v_small = dict(b_ada=v_b_ada, pre_w_mix=v_pre_w_mix, post_w_mix=v_post_w_mix, pre_w_mlp=v_pre_w_mlp, post_w_mlp=v_post_w_mlp,
                   attn_out_w=v_attn_out_w, hg_norm_w=v_hg_norm_w, attn_sinks=v_attn_sinks, lb_table=v_lb_table)
    *small_packed, loss_rows = _small_update(packs, _pack_small(w_small), _pack_small(m_small), _pack_small(v_small), bsz)
    small_out = [_unpack_small(p) for p in small_packed]
    loss = loss_rows[0, 0]

    dmod_all = packs[:, :bsz * MOD_ROWS, :].reshape(N_DEV * bsz, N_MOD * D_MODEL)
    dmod_cols = lax.dynamic_slice_in_dim(dmod_all, chip * ada_cols, ada_cols, axis=1)
    ada_out = _ada_bwd_adamw(c_all, dmod_cols, w_ada[0], m_w_ada[0], v_w_ada[0])

    s_in, x_in = _exchange_wait(in_sems, s_in, in_landing, [grad_x, ada_out[0]], "exchange_in_wait")
    half_in = _sum_chips(s_in, x_in, chip_idx, "sum_chips_in")
    ((their_in,),) = _comm_only([_plan_pair([half_in], False)], "pair_swap_in")
    big = dict(
        w_in=tuple(jnp.transpose(a) for a in _adamw_halves(half_in, their_in, core, w_in_t[0], m_in_t[0], v_in_t[0], axis=0,
                                                           name="adamw_in")),
        w_up=tuple(_adamw_halves(half_up, their_up, core, w_up[0], m_w_up[0], v_w_up[0], axis=0, name="adamw_up")),
        w_out=tuple(_adamw_halves(half_out, their_out, core, w_out[0], m_w_out[0], v_w_out[0], axis=1, name="adamw_out")),
        w_down=tuple(_adamw_halves(half_down, their_down, core, w_down[0], m_w_down[0], v_w_down[0], axis=0, name="adamw_down")),
        w_ada=tuple(ada_out),
    )
    order = ("w_ada", "b_ada", "pre_w_mix", "w_in", "attn_sinks", "attn_out_w", "lb_table", "hg_norm_w", "w_out", "post_w_mix",
             "pre_w_mlp", "w_up", "w_down", "post_w_mlp")
    outs = [loss, grad_x]
    for kind in range(4):
        for nm in order:
            outs.append(big[nm][kind][None] if nm in big else small_out[kind][nm])
    return tuple(outs)
```

```python
import jax
import jax.numpy as jnp
from jax import lax
from jax.experimental import pallas as pl
from jax.experimental.pallas import tpu as pltpu

F32 = jnp.float32
BF16 = jnp.bfloat16

D_MODEL = 1024
ATT_WIDTH = 512
ATT_HEAD_DIM = 64
ATT_Q_HEADS = 8
ATT_KV_HEADS = 2
ATT_GROUP = ATT_Q_HEADS // ATT_KV_HEADS
ATT_KV_COLS = ATT_KV_HEADS * ATT_HEAD_DIM
WINDOW = 128
ROPE_DIM = 16
ROPE_THETA = 500000.0
HG_WIDTH = 512
MIX_WIDTH = ATT_WIDTH + HG_WIDTH
HG_HEAD_DIM = 128
HG_HEADS = 4
HG_CHUNK = 32
IN_COLS = ATT_WIDTH + 2 * ATT_KV_COLS + 4 * HG_WIDTH
ATT_COLS = ATT_WIDTH + 2 * ATT_KV_COLS
D_FF = 4 * D_MODEL
N_MOD = 6
EPS = 1e-6
ATT_SCALE = ATT_HEAD_DIM ** -0.5

ADAM_LR = 0.001
ADAM_B1 = 0.9
ADAM_B2 = 0.999
ADAM_EPS = 1e-08
ADAM_WD = 0.01
ADAM_STEP = 10

N_CHIPS = 4
N_DEV = 8
LANE = 128
VMEM_LIMIT = 48 * 1024 * 1024
VMEM_LIMIT_BIG = 58 * 1024 * 1024
MESH = pl.DeviceIdType.MESH

NT_DIMS = (((1,), (1,)), ((), ()))
TN_DIMS = (((0,), (0,)), ((), ()))


def _sds(shape, dtype):
    return jax.ShapeDtypeStruct(tuple(shape), dtype)


def _params(*sem, vmem_limit=None):
    return pltpu.CompilerParams(dimension_semantics=sem, vmem_limit_bytes=VMEM_LIMIT if vmem_limit is None else vmem_limit)


def _sigmoid(x):
    return 1.0 / (1.0 + jnp.exp(-x))


def _dot(a, b, dims=None):
    a, b = a.astype(BF16), b.astype(BF16)
    if dims is None:
        return jnp.dot(a, b, preferred_element_type=F32)
    return lax.dot_general(a, b, dims, preferred_element_type=F32)


def _rms_fwd(x, w):
    rstd = lax.rsqrt(jnp.mean(x * x, axis=-1, keepdims=True) + EPS)
    xh = x * rstd
    return xh * w, xh, rstd


def _rms_bwd(dy, xh, rstd, w):
    dxh = dy * w
    dx = rstd * (dxh - xh * jnp.mean(dxh * xh, axis=-1, keepdims=True))
    return dx, dy * xh


def _colsum(x):
    return jnp.sum(x, axis=0, keepdims=True)


def _row_tile(rows, cap=256):
    return max(t for t in range(16, cap + 1, 16) if rows % t == 0)


HBM_SPEC = pl.BlockSpec(memory_space=pltpu.HBM)


def _mesh_pos():
    return lax.axis_index("x"), lax.axis_index("y"), lax.axis_index("c")


class _Comm:
    def __init__(self, ins, outs, sems, start, finish, aliases=()):
        self.ins, self.outs, self.sems = list(ins), list(outs), list(sems)
        self.start, self.finish, self.aliases = start, finish, tuple(aliases)


def _call(body, args, *, name, grid, in_specs, out_specs, out_shape, sem, scratch_shapes=(), comms=(), aliases=None,
          vmem_limit=None):
    scratch_shapes = list(scratch_shapes)
    if not comms:
        return pl.pallas_call(body, name=name, grid=grid, in_specs=in_specs, out_specs=out_specs, out_shape=out_shape,
                              input_output_aliases=dict(aliases or {}), scratch_shapes=scratch_shapes,
                              compiler_params=_params(*sem, vmem_limit=vmem_limit))(*args)
    single = not isinstance(out_shape, (list, tuple))
    out_specs_l = [out_specs] if single else list(out_specs)
    out_shape_l = [out_shape] if single else list(out_shape)
    n_in, n_out, n_scr = len(in_specs), len(out_shape_l), len(scratch_shapes)
    n_ci = [len(cm.ins) for cm in comms]
    n_co = [len(cm.outs) for cm in comms]
    n_cs = [len(cm.sems) for cm in comms]
    aliases = dict(aliases or {})
    for k, cm in enumerate(comms):
        for i, o in cm.aliases:
            aliases[n_in + sum(n_ci[:k]) + i] = n_out + sum(n_co[:k]) + o

    def fused(*refs):
        pos = [0]

        def take(n):
            part = refs[pos[0]:pos[0] + n]
            pos[0] += n
            return part

        ins = take(n_in)
        c_ins = [take(n) for n in n_ci]
        outs = take(n_out)
        c_outs = [take(n) for n in n_co]
        scr = take(n_scr)
        c_sems = [take(n) for n in n_cs]
        first, last = True, True
        for d, size in enumerate(grid):
            first = jnp.logical_and(first, pl.program_id(d) == 0)
            last = jnp.logical_and(last, pl.program_id(d) == size - 1)

        def run(which):
            for cm, ci, co, cs in zip(comms, c_ins, c_outs, c_sems):
                getattr(cm, which)(ci, co, cs)

        if grid:
            pl.when(first)(lambda: run("start"))
        else:
            run("start")
        body(*ins, *outs, *scr)
        if grid:
            pl.when(last)(lambda: run("finish"))
        else:
            run("finish")

    res = pl.pallas_call(
        fused, name=name, grid=grid, in_specs=list(in_specs) + [HBM_SPEC] * sum(n_ci),
        out_specs=out_specs_l + [HBM_SPEC] * sum(n_co), out_shape=out_shape_l + [s for cm in comms for s in cm.outs],
        input_output_aliases=aliases, scratch_shapes=scratch_shapes + [s for cm in comms for s in cm.sems],
        compiler_params=_params(*["arbitrary"] * len(grid), vmem_limit=vmem_limit),
    )(*args, *[a for cm in comms for a in cm.ins])
    main = res[:n_out]
    extra, at = [], n_out
    for n in n_co:
        extra.append(list(res[at:at + n]))
        at += n
    return (main[0] if single else list(main)), extra


def _mm(a, b, *, name, out_dtype, trans_b=False, tm=512, tn=None, extra=(), epi=None, b_spec=None, n_out=None, comms=()):
    m_total, k_total = a.shape
    if n_out is None:
        n_out = b.shape[0] if trans_b else b.shape[1]
    tn = n_out if tn is None else tn
    grid = (m_total // tm, n_out // tn)
    dims = NT_DIMS if trans_b else None

    def body(*refs):
        a_ref, b_ref = refs[0], refs[1]
        extra_refs = refs[2:2 + len(extra)]
        o_ref = refs[2 + len(extra)]
        acc = _dot(a_ref[...], b_ref[...], dims)
        if epi is not None:
            acc = epi(acc, *[r[...] for r in extra_refs])
        o_ref[...] = acc.astype(out_dtype)

    if b_spec is None:
        if trans_b:
            b_spec = pl.BlockSpec((tn, k_total), lambda i, j: (j, 0))
        else:
            b_spec = pl.BlockSpec((k_total, tn), lambda i, j: (0, j))
    in_specs = [pl.BlockSpec((tm, k_total), lambda i, j: (i, 0)), b_spec]
    in_specs += [pl.BlockSpec((tm, tn), lambda i, j: (i, j)) for _ in extra]
    return _call(
        body, (a, b, *extra), name=name, grid=grid, in_specs=in_specs,
        out_specs=pl.BlockSpec((tm, tn), lambda i, j: (i, j)),
        out_shape=_sds((m_total, n_out), out_dtype),
        sem=("parallel", "parallel"), comms=comms)


def _mm_tn(a, b, *, name, tk, tn, a_fn=None, out_shape=None, out_spec=None):
    m_total, k_total = a.shape
    n_total = b.shape[1]
    grid = (k_total // tk, n_total // tn)

    def body(a_ref, b_ref, o_ref):
        av = a_ref[...]
        part = _dot(av if a_fn is None else a_fn(av), b_ref[...], TN_DIMS)
        o_ref[...] = part.reshape(o_ref.shape)

    if out_shape is None:
        out_shape = _sds((k_total, n_total), F32)
        out_spec = pl.BlockSpec((tk, tn), lambda i, j: (i, j))
    return pl.pallas_call(
        body, name=name, grid=grid,
        in_specs=[pl.BlockSpec((m_total, tk), lambda i, j: (0, i)), pl.BlockSpec((m_total, tn), lambda i, j: (0, j))],
        out_specs=out_spec, out_shape=out_shape,
        compiler_params=_params("parallel", "parallel"),
    )(a, b)


def _ada_fwd(c_all, w_shard, b_shard):
    nb, ncol = c_all.shape[0], w_shard.shape[1]
    tn = 512

    def body(c_ref, w_ref, b_ref, o_ref):
        c = c_ref[...]
        o_ref[...] = _dot(c * _sigmoid(c), w_ref[...]) + b_ref[...]

    return pl.pallas_call(
        body, name="ada_fwd", grid=(ncol // tn,),
        in_specs=[pl.BlockSpec((nb, D_MODEL), lambda j: (0, 0)), pl.BlockSpec((D_MODEL, tn), lambda j: (0, j)),
                  pl.BlockSpec((1, tn), lambda j: (0, j))],
        out_specs=pl.BlockSpec((nb, tn), lambda j: (0, j)), out_shape=_sds((nb, ncol), F32),
        compiler_params=_params("parallel"),
    )(c_all, w_shard, b_shard)


def _adamw_math(g, w, m, v):
    m = ADAM_B1 * m + (1.0 - ADAM_B1) * g
    v = ADAM_B2 * v + (1.0 - ADAM_B2) * (g * g)
    m_hat = m / (1.0 - ADAM_B1 ** ADAM_STEP)
    v_hat = v / (1.0 - ADAM_B2 ** ADAM_STEP)
    delta = -ADAM_LR * (m_hat / (jnp.sqrt(v_hat) + ADAM_EPS) + ADAM_WD * w)
    return delta, m, v


def _ada_bwd_adamw(c_all, dmod_cols, w, m, v):
    nb, ncol = dmod_cols.shape
    tn = 256

    def body(c_ref, d_ref, w_ref, m_ref, v_ref, g_ref, dl_ref, nm_ref, nv_ref):
        c = c_ref[...]
        g = _dot(c * _sigmoid(c), d_ref[...], TN_DIMS)
        g_ref[...] = g
        dl_ref[...], nm_ref[...], nv_ref[...] = _adamw_math(g, w_ref[...], m_ref[...], v_ref[...])

    col = pl.BlockSpec((D_MODEL, tn), lambda j: (0, j))
    shp = _sds((D_MODEL, ncol), F32)
    return pl.pallas_call(
        body, name="ada_bwd_adamw", grid=(ncol // tn,),
        in_specs=[pl.BlockSpec((nb, D_MODEL), lambda j: (0, 0)), pl.BlockSpec((nb, tn), lambda j: (0, j)), col, col, col],
        out_specs=[col, col, col, col], out_shape=[shp, shp, shp, shp],
        compiler_params=_params("parallel"),
    )(c_all, dmod_cols, w, m, v)


def _adamw_halves(own, theirs, core, w, m, v, *, axis, name):
    r2, c2 = own.shape
    tr = _row_tile(r2)
    nt = r2 // tr

    def body(core_ref, own_ref, their_ref, w_ref, m_ref, v_ref, g_ref, dl_ref, nm_ref, nv_ref):
        g = jnp.where(pl.program_id(0) == core_ref[0], own_ref[...], their_ref[...])
        g_ref[...] = g
        dl_ref[...], nm_ref[...], nv_ref[...] = _adamw_math(g, w_ref[...], m_ref[...], v_ref[...])

    if axis == 0:
        full = pl.BlockSpec((tr, c2), lambda h, i, core_ref: (h * nt + i, 0))
    else:
        full = pl.BlockSpec((tr, c2), lambda h, i, core_ref: (i, h))
    half = pl.BlockSpec((tr, c2), lambda h, i, core_ref: (i, 0))
    shp = _sds(w.shape, F32)
    return pl.pallas_call(
        body, name=name,
        grid_spec=pltpu.PrefetchScalarGridSpec(num_scalar_prefetch=1, grid=(2, nt), in_specs=[half, half, full, full, full],
                                               out_specs=[full] * 4),
        out_shape=[shp] * 4, compiler_params=_params("parallel", "parallel"),
    )(core, own, theirs, w, m, v)


def _tok_spec(tm, width=D_MODEL):
    return pl.BlockSpec((None, tm, width), lambda b, i: (b, i, 0))


def _row_spec(width=D_MODEL):
    return pl.BlockSpec((None, 1, width), lambda b, i: (b, 0, 0))


def _vec_spec(width=D_MODEL):
    return pl.BlockSpec((1, width), lambda b, i: (0, 0))


class _RowsOf:
    def __init__(self, ref, first, count):
        self.ref, self.rows = ref, slice(first, first + count)

    def __getitem__(self, idx):
        return self.ref[self.rows, :]

    def __setitem__(self, idx, value):
        self.ref[self.rows, :] = value


def _mm_rows(a, b, *, name, tm, extra, extra_specs, out_specs, out_shape, epi, pro=None, trans_b=False, b_chunks=1, comms=(),
             parts=1, zero_per_seq=(), zero_once=(), vmem_limit=None):
    bsz, seq, k_total = a.shape
    kc = k_total // b_chunks
    dims = NT_DIMS if trans_b else None
    rows = tm // parts

    def body(*refs):
        a_ref, b_ref = refs[0], refs[1]
        ex, outs = refs[2:2 + len(extra)], refs[2 + len(extra):]
        if zero_per_seq:
            @pl.when(pl.program_id(1) == 0)
            def _():
                for k in zero_per_seq:
                    outs[k][...] = jnp.zeros_like(outs[k])
        if zero_once:
            @pl.when(jnp.logical_and(pl.program_id(0) == 0, pl.program_id(1) == 0))
            def _():
                for k in zero_once:
                    outs[k][...] = jnp.zeros_like(outs[k])

        def part_of(ref, p):
            tiled = len(ref.shape) == 2 and ref.shape[0] == tm
            return _RowsOf(ref, p * rows, rows) if tiled and parts > 1 else ref

        accs = []
        for p in range(parts):
            a_p, ex_p, outs_p = part_of(a_ref, p), [part_of(r, p) for r in ex], [part_of(r, p) for r in outs]
            if b_chunks == 1:
                accs.append(_dot(a_p[...] if pro is None else pro(a_p, ex_p, outs_p), b_ref[...], dims))
            else:
                acc = _dot(a_p[...][:, 0:kc], b_ref[0], NT_DIMS)
                for k in range(1, b_chunks):
                    acc = acc + _dot(a_p[...][:, k * kc:(k + 1) * kc], b_ref[k], NT_DIMS)
                accs.append(acc)
        for p in range(parts):
            epi(accs[p], [part_of(r, p) for r in ex], [part_of(r, p) for r in outs])

    b_spec = pl.BlockSpec(b.shape, lambda bb, i: (0,) * b.ndim)
    return _call(
        body, (a, b, *extra), name=name, grid=(bsz, seq // tm), in_specs=[_tok_spec(tm, k_total), b_spec, *extra_specs],
        out_specs=out_specs, out_shape=out_shape, sem=("arbitrary", "arbitrary"), comms=comms, vmem_limit=vmem_limit)


def _in_proj_fused(x, w, sc, sh, w_in_t, tables, comms=()):
    tm = 512
    bsz, seq, _ = x.shape
    half = ROPE_DIM // 2
    heads_per_slab = LANE // ATT_HEAD_DIM

    def pro(x_ref, ex, outs):
        y, _, _ = _rms_fwd(x_ref[...], ex[0][...])
        h = (y * (1.0 + ex[1][...]) + ex[2][...]).astype(BF16)
        outs[0][...] = h
        return h

    def epi(acc, ex, outs):
        c, u, d = ex[3][...], ex[4][...], ex[5][...]
        _, proj_ref, q_ref, k_ref, v_ref = outs
        proj_ref[...] = acc

        def rope(z):
            return (z * c + pltpu.roll(z, half, 1) * u + pltpu.roll(z, LANE - half, 1) * d).astype(BF16)

        for s in range(ATT_WIDTH // LANE):
            slab = rope(acc[:, s * LANE:(s + 1) * LANE])
            for part in range(heads_per_slab):
                g, hh = divmod(s * heads_per_slab + part, ATT_GROUP)
                piece = slab[:, part * ATT_HEAD_DIM:(part + 1) * ATT_HEAD_DIM]
                for blk in range(tm // WINDOW):
                    q_ref[blk, g, hh * WINDOW:(hh + 1) * WINDOW, :] = piece[blk * WINDOW:(blk + 1) * WINDOW]
        rk = rope(acc[:, ATT_WIDTH:ATT_WIDTH + LANE])
        vv = acc[:, ATT_WIDTH + LANE:ATT_COLS].astype(BF16)
        for g in range(ATT_KV_HEADS):
            k_ref[g] = rk[:, g * ATT_HEAD_DIM:(g + 1) * ATT_HEAD_DIM]
            v_ref[g] = vv[:, g * ATT_HEAD_DIM:(g + 1) * ATT_HEAD_DIM]

    cols = w_in_t.shape[0]
    tab = pl.BlockSpec((tm, LANE), lambda b, i: (i, 0))
    kv_spec = pl.BlockSpec((None, ATT_KV_HEADS, tm, ATT_HEAD_DIM), lambda b, i: (b, 0, i, 0))
    kv_shape = _sds((bsz, ATT_KV_HEADS, seq, ATT_HEAD_DIM), BF16)
    q_spec = pl.BlockSpec((None, tm // WINDOW, ATT_KV_HEADS, GROUP_ROWS, ATT_HEAD_DIM), lambda b, i: (b, i, 0, 0, 0))
    return _mm_rows(x, w_in_t, name="in_proj", tm=tm, extra=(w, sc, sh, *tables),
                    extra_specs=[_vec_spec(), _row_spec(), _row_spec(), tab, tab, tab],
                    out_specs=[_tok_spec(tm), _tok_spec(tm, cols), q_spec, kv_spec, kv_spec],
                    out_shape=[_sds(x.shape, BF16), _sds((bsz, seq, cols), F32),
                               _sds((bsz, seq // WINDOW, ATT_KV_HEADS, GROUP_ROWS, ATT_HEAD_DIM), BF16), kv_shape, kv_shape],
                    pro=pro, epi=epi, trans_b=True, comms=comms)


def _rope_tables(seq):
    half = ROPE_DIM // 2
    inv_freq = ROPE_THETA ** (-jnp.arange(0, ROPE_DIM, 2, dtype=F32) / ROPE_DIM)
    ang = jnp.arange(seq, dtype=F32)[:, None] * inv_freq[None, :]
    cos, sin = jnp.cos(ang), jnp.sin(ang)
    rest = ATT_HEAD_DIM - ROPE_DIM
    ones, zeros, zh = jnp.ones((seq, rest), F32), jnp.zeros((seq, rest), F32), jnp.zeros((seq, half), F32)
    reps = LANE // ATT_HEAD_DIM
    t_cos = jnp.tile(jnp.concatenate([cos, cos, ones], axis=1), (1, reps))
    t_up = jnp.tile(jnp.concatenate([zh, sin, zeros], axis=1), (1, reps))
    t_dn = jnp.tile(jnp.concatenate([-sin, zh, zeros], axis=1), (1, reps))
    return t_cos, t_up, t_dn


GROUP_ROWS = ATT_GROUP * WINDOW


ATT_BPS = 2


MASKED = -1e30


def _band_biases():
    row = jnp.arange(GROUP_ROWS)[:, None] % WINDOW
    col = jnp.arange(2 * WINDOW)[None, :]
    own = jnp.logical_and(col >= WINDOW, col - WINDOW <= row)
    before = jnp.logical_and(col < WINDOW, col > row)
    return (jnp.where(jnp.logical_or(own, before), 0.0, MASKED).astype(F32), jnp.where(own, 0.0, MASKED).astype(F32))


def _band_bias(full_ref, first_ref, has_prev):
    return full_ref[...] if has_prev is True else jnp.where(has_prev, full_ref[...], first_ref[...])


def _sink_column(sink_ref, g):
    head = lax.broadcasted_iota(jnp.int32, (GROUP_ROWS, 1), 0) // WINDOW
    col = jnp.full((GROUP_ROWS, 1), sink_ref[0, g * ATT_GROUP], F32)
    for hh in range(1, ATT_GROUP):
        col = jnp.where(head == hh, sink_ref[0, g * ATT_GROUP + hh], col)
    return col


def _bias_spec():
    return pl.BlockSpec((GROUP_ROWS, 2 * WINDOW), lambda b, i: (0, 0))


def _attn_specs():
    q_spec = pl.BlockSpec((None, ATT_BPS, ATT_KV_HEADS, GROUP_ROWS, ATT_HEAD_DIM), lambda b, i: (b, i, 0, 0, 0))
    kv_cur = pl.BlockSpec((None, ATT_KV_HEADS, ATT_BPS * WINDOW, ATT_HEAD_DIM), lambda b, i: (b, 0, i, 0))
    kv_prev = pl.BlockSpec((None, ATT_KV_HEADS, WINDOW, ATT_HEAD_DIM), lambda b, i: (b, 0, jnp.maximum(ATT_BPS * i - 1, 0), 0))
    return q_spec, kv_cur, kv_prev


def _band(prev_ref, cur_ref, g, blk):
    own = cur_ref[g, blk * WINDOW:(blk + 1) * WINDOW]
    before = prev_ref[g] if blk == 0 else cur_ref[g, (blk - 1) * WINDOW:blk * WINDOW]
    return jnp.concatenate([before, own], axis=0)


def _attn_fwd(qh, kh, vh, sinks, w_norm, biases, comms=()):
    bsz, nblk = qh.shape[0], qh.shape[1]
    seq = nblk * WINDOW
    rows = ATT_BPS * WINDOW

    def body(sink_ref, q_ref, kc_ref, kp_ref, vc_ref, vp_ref, w_ref, full_ref, first_ref, raw_ref, an_ref, l_ref):
        for blk in range(ATT_BPS):
            bias = _band_bias(full_ref, first_ref, True if blk else pl.program_id(1) > 0)
            for g in range(ATT_KV_HEADS):
                keys, vals = _band(kp_ref, kc_ref, g, blk), _band(vp_ref, vc_ref, g, blk)
                sink = _sink_column(sink_ref, g)
                s = _dot(q_ref[blk, g], keys, NT_DIMS) * ATT_SCALE + bias
                m = jnp.maximum(jnp.max(s, axis=-1, keepdims=True), sink)
                p = jnp.exp(s - m)
                den = jnp.sum(p, axis=-1, keepdims=True) + jnp.exp(sink - m)
                o = _dot(p / den, vals)
                lse = m + jnp.log(den)
                tok = slice(blk * WINDOW, (blk + 1) * WINDOW)
                for hh in range(ATT_GROUP):
                    h = g * ATT_GROUP + hh
                    raw_ref[tok, h * ATT_HEAD_DIM:(h + 1) * ATT_HEAD_DIM] = o[hh * WINDOW:(hh + 1) * WINDOW]
                    l_ref[tok, h:h + 1] = lse[hh * WINDOW:(hh + 1) * WINDOW]
        y, _, _ = _rms_fwd(raw_ref[...], w_ref[...])
        an_ref[...] = y.astype(BF16)

    cur = lambda width: pl.BlockSpec((None, rows, width), lambda b, i: (b, i, 0))
    q_spec, kv_cur, kv_prev = _attn_specs()
    return _call(
        body, (sinks, qh, kh, kh, vh, vh, w_norm, *biases), name="attn_fwd", grid=(bsz, nblk // ATT_BPS),
        in_specs=[pl.BlockSpec(memory_space=pltpu.SMEM), q_spec, kv_cur, kv_prev, kv_cur, kv_prev, _vec_spec(ATT_WIDTH),
                  _bias_spec(), _bias_spec()],
        out_specs=[cur(ATT_WIDTH), cur(ATT_WIDTH), cur(ATT_Q_HEADS)],
        out_shape=[_sds((bsz, seq, ATT_WIDTH), F32), _sds((bsz, seq, MIX_WIDTH), BF16), _sds((bsz, seq, ATT_Q_HEADS), F32)],
        sem=("parallel", "parallel"), comms=comms)


HG_Q0 = ATT_COLS // LANE
HG_F0 = HG_Q0 + HG_HEADS
HG_I0 = HG_F0 + HG_HEADS
HG_G0 = HG_I0 + HG_HEADS
HG_TOK = 256
HG_NCH = HG_TOK // HG_CHUNK
HG_HPS = 2


def _block_masks():
    row = jnp.arange(HG_TOK)[:, None]
    col = jnp.arange(HG_TOK)[None, :]
    same = (row // HG_CHUNK) == (col // HG_CHUNK)
    return jnp.logical_and(same, col <= row).astype(F32), jnp.logical_and(same, col >= row).astype(F32)


def _row_in_chunk():
    return lax.broadcasted_iota(jnp.int32, (HG_TOK, LANE), 0) % HG_CHUNK


def _chunk_cumsum(x, reverse=False):
    ric = _row_in_chunk()
    shift = 1
    while shift < HG_CHUNK:
        if reverse:
            x = x + jnp.where(ric < HG_CHUNK - shift, pltpu.roll(x, HG_TOK - shift, 0), 0.0)
        else:
            x = x + jnp.where(ric >= shift, pltpu.roll(x, shift, 0), 0.0)
        shift *= 2
    return x


def _chunk_rows(rows):
    stacked = jnp.concatenate([r[None] for r in rows], axis=0)
    return jnp.broadcast_to(stacked, (HG_NCH, HG_CHUNK, LANE)).reshape(HG_TOK, LANE)


def _chunk_slices(x):
    return [x[j * HG_CHUNK:(j + 1) * HG_CHUNK] for j in range(HG_NCH)]


def _hgrn_common(tbl, hf, hq):
    lb = _sigmoid(tbl[1:2] - tbl[0:1])
    sig = _sigmoid(hf)
    f = lb + (1.0 - lb) * sig
    sq = _sigmoid(hq)
    q, k = hq * sq, 1.0 - f
    b = _chunk_cumsum(jnp.log(f))
    last = [b[(j + 1) * HG_CHUNK - 1:(j + 1) * HG_CHUNK] for j in range(HG_NCH)]
    bl = _chunk_rows(last)
    e_b, e_nb, e_rem = jnp.exp(b), jnp.exp(-b), jnp.exp(bl - b)
    e_last = [jnp.exp(r) for r in last]
    return dict(lb=lb, sig=sig, f=f, sq=sq, q=q, k=k, e_b=e_b, e_nb=e_nb, e_rem=e_rem, e_last=e_last,
                qd=q * e_b, kd=k * e_nb, ku=k * e_rem)


def _hgrn_fwd(proj, lb_table, norm_w, mix_in, masks, comms=()):
    bsz, seq, _ = proj.shape
    nstep = seq // HG_TOK

    def body(tbl_ref, nw_ref, q_ref, f_ref, i_ref, g_ref, mix_ref, lower_ref, o_ref, rec_ref, st_ref, s_scr):
        @pl.when(pl.program_id(2) == 0)
        def _():
            s_scr[...] = jnp.zeros_like(s_scr)

        lower = lower_ref[...]
        for hp in range(HG_HPS):
            ls = slice(hp * LANE, (hp + 1) * LANE)
            v, hg = i_ref[:, ls], g_ref[:, ls]
            t = _hgrn_common(tbl_ref[:, ls], f_ref[:, ls], q_ref[:, ls])
            a = _dot(t["qd"], t["kd"], NT_DIMS) * lower
            o_intra = _dot(a, v)
            v_c, ku_c, qd_c = [_chunk_slices(z.astype(BF16)) for z in (v, t["ku"], t["qd"])]
            updates = [_dot(v_c[j], ku_c[j], TN_DIMS) for j in range(HG_NCH)]
            st = s_scr[hp]
            states = []
            for j in range(HG_NCH):
                states.append(st)
                st = st * t["e_last"][j] + updates[j]
            s_scr[hp] = st
            o = o_intra + jnp.concatenate([_dot(qd_c[j], states[j], NT_DIMS) for j in range(HG_NCH)], axis=0)
            for j in range(HG_NCH):
                st_ref[hp, j] = states[j]
            o_ref[:, ls] = o
            y, _, _ = _rms_fwd(o, nw_ref[...])
            rec_ref[:, ls] = (y * (hg * _sigmoid(hg))).astype(BF16)

    width = HG_HPS * LANE
    slab = lambda first: pl.BlockSpec((None, HG_TOK, width), lambda b, h, t: (b, t, first // HG_HPS + h))
    head_out = pl.BlockSpec((None, HG_TOK, width), lambda b, h, t: (b, t, h))
    mix_out = pl.BlockSpec((None, HG_TOK, width), lambda b, h, t: (b, t, ATT_WIDTH // width + h))
    return _call(
        body, (lb_table, norm_w, proj, proj, proj, proj, mix_in, masks[0]), name="hgrn_fwd", grid=(bsz, HG_HEADS // HG_HPS, nstep),
        in_specs=[pl.BlockSpec((2, width), lambda b, h, t: (0, h)), pl.BlockSpec((1, LANE), lambda b, h, t: (0, 0)),
                  slab(HG_Q0), slab(HG_F0), slab(HG_I0), slab(HG_G0), pl.BlockSpec(memory_space=pl.ANY),
                  pl.BlockSpec((HG_TOK, HG_TOK), lambda b, h, t: (0, 0))],
        out_specs=[head_out, mix_out,
                   pl.BlockSpec((None, HG_HPS, HG_NCH, LANE, LANE), lambda b, h, t: (b, h, t, 0, 0))],
        out_shape=[_sds((bsz, seq, HG_WIDTH), F32), _sds(mix_in.shape, BF16),
                   _sds((bsz, HG_HEADS, seq // HG_CHUNK, LANE, LANE), F32)],
        scratch_shapes=[pltpu.VMEM((HG_HPS, LANE, LANE), F32)],
        sem=("parallel", "parallel", "arbitrary"), comms=comms, aliases={6: 1})


def _out_proj_fused(cat, w_out, x, post_w, g1, pre_w, sc2, sh2):
    tm = 512

    def epi(mix, ex, outs):
        x_ref, pw_ref, g1_ref, w2_ref, sc_ref, sh_ref = ex
        outs[0][...] = mix
        n1, _, _ = _rms_fwd(mix, pw_ref[...])
        x1 = x_ref[...] + g1_ref[...] * n1
        outs[1][...] = x1
        y2, _, _ = _rms_fwd(x1, w2_ref[...])
        outs[2][...] = (y2 * (1.0 + sc_ref[...]) + sh_ref[...]).astype(BF16)

    return _mm_rows(cat, w_out, name="out_proj", tm=tm, extra=(x, post_w, g1, pre_w, sc2, sh2),
                    extra_specs=[_tok_spec(tm), _vec_spec(), _row_spec(), _vec_spec(), _row_spec(), _row_spec()],
                    out_specs=[_tok_spec(tm), _tok_spec(tm), _tok_spec(tm)],
                    out_shape=[_sds(x.shape, F32), _sds(x.shape, F32), _sds(x.shape, BF16)], epi=epi)


def _acc_out(ref, first, value):
    @pl.when(first)
    def _():
        ref[...] = value

    @pl.when(jnp.logical_not(first))
    def _():
        ref[...] += value


def _down_proj_fused(r, w_down, x1, post_w, g2, target):
    tm = 512
    bsz = x1.shape[0]

    def pro(r_ref, ex, outs):
        rv = r_ref[...]
        return rv * rv

    def epi(down, ex, outs):
        x1_ref, w_ref, g2_ref, t_ref = ex
        loss_ref, dy_ref, dd_ref, dg2_ref, dw_ref = outs
        w, g2v = w_ref[...], g2_ref[...]
        n2, dh, rstd = _rms_fwd(down, w)
        err = x1_ref[...] + g2v * n2 - t_ref[...]
        part = (0.5 / D_MODEL) * jnp.sum(jnp.sum(err * err, axis=-1, keepdims=True), axis=0, keepdims=True)
        loss_ref[...] += jnp.broadcast_to(part, (1, LANE))
        dy = err * (1.0 / D_MODEL)
        dy_ref[...] = dy
        dg2_ref[...] += _colsum(dy * n2)
        dd, dw_rows = _rms_bwd(dy * g2v, dh, rstd, w)
        dd_ref[...] = dd.astype(BF16)
        dw_ref[...] += _colsum(dw_rows)

    return _mm_rows(r, w_down, name="down_proj", tm=tm, extra=(x1, post_w, g2, target),
                    extra_specs=[_tok_spec(tm), _vec_spec(), _row_spec(), _tok_spec(tm)],
                    out_specs=[_vec_spec(LANE), _tok_spec(tm), _tok_spec(tm), _row_spec(), _vec_spec()],
                    out_shape=[_sds((1, LANE), F32), _sds(x1.shape, F32), _sds(x1.shape, BF16), _sds((bsz, 1, D_MODEL), F32),
                               _sds((1, D_MODEL), F32)], pro=pro, epi=epi, parts=2, zero_per_seq=(3,), zero_once=(0, 4),
                    vmem_limit=VMEM_LIMIT_BIG)


def _up_bwd_fused(dpre, w_up4, dy, x1, mix, pre_w, sc2, post_w, g1, comms=()):
    tm = 512
    bsz = x1.shape[0]

    def epi(dh2v, ex, outs):
        dy_ref, x1_ref, mix_ref, w2_ref, sc_ref, pw_ref, g1_ref = ex
        dx1_ref, dmix_ref, dsc_ref, dsh_ref, dg1_ref, dw2_ref, dpw_ref = outs
        w2, pw = w2_ref[...], pw_ref[...]
        y2, xh2, rstd2 = _rms_fwd(x1_ref[...], w2)
        dsh_ref[...] += _colsum(dh2v)
        dsc_ref[...] += _colsum(dh2v * y2)
        dx1n, dw_rows = _rms_bwd(dh2v * (1.0 + sc_ref[...]), xh2, rstd2, w2)
        dw2_ref[...] += _colsum(dw_rows)
        dx1 = dy_ref[...] + dx1n
        dx1_ref[...] = dx1
        n1, mh, rstd1 = _rms_fwd(mix_ref[...], pw)
        dg1_ref[...] += _colsum(dx1 * n1)
        dmix, dpw_rows = _rms_bwd(dx1 * g1_ref[...], mh, rstd1, pw)
        dmix_ref[...] = dmix.astype(BF16)
        dpw_ref[...] += _colsum(dpw_rows)

    row_shape = _sds((bsz, 1, D_MODEL), F32)
    vec_shape = _sds((1, D_MODEL), F32)
    return _mm_rows(dpre, w_up4, name="up_bwd", tm=tm, extra=(dy, x1, mix, pre_w, sc2, post_w, g1),
                    extra_specs=[_tok_spec(tm), _tok_spec(tm), _tok_spec(tm), _vec_spec(), _row_spec(), _vec_spec(), _row_spec()],
                    out_specs=[_tok_spec(tm), _tok_spec(tm), _row_spec(), _row_spec(), _row_spec(), _vec_spec(), _vec_spec()],
                    out_shape=[_sds(x1.shape, F32), _sds(x1.shape, BF16), row_shape, row_shape, row_shape, vec_shape, vec_shape],
                    epi=epi, b_chunks=w_up4.shape[0], comms=comms, parts=2, zero_per_seq=(2, 3, 4), zero_once=(5, 6),
                    vmem_limit=VMEM_LIMIT_BIG)


def _norm1_bwd(dh1, dx1, x, pre_w, sc1, tm=512, comms=()):
    bsz, seq, _ = x.shape

    def body(dh_ref, dx1_ref, x_ref, w_ref, sc_ref, gx_ref, dsc_ref, dsh_ref, dw_ref):
        b, i = pl.program_id(0), pl.program_id(1)
        w = w_ref[...]
        dh = dh_ref[...]
        y, xh, rstd = _rms_fwd(x_ref[...], w)
        _acc_out(dsh_ref, i == 0, _colsum(dh))
        _acc_out(dsc_ref, i == 0, _colsum(dh * y))
        dx, dw_rows = _rms_bwd(dh * (1.0 + sc_ref[...]), xh, rstd, w)
        _acc_out(dw_ref, jnp.logical_and(b == 0, i == 0), _colsum(dw_rows))
        gx_ref[...] = dx1_ref[...] + dx

    row_shape = _sds((bsz, 1, D_MODEL), F32)
    return _call(
        body, (dh1, dx1, x, pre_w, sc1), name="norm1_bwd", grid=(bsz, seq // tm),
        in_specs=[_tok_spec(tm), _tok_spec(tm), _tok_spec(tm), _vec_spec(), _row_spec()],
        out_specs=[_tok_spec(tm), _row_spec(), _row_spec(), _vec_spec()],
        out_shape=[_sds(x.shape, F32), row_shape, row_shape, _sds((1, D_MODEL), F32)],
        sem=("arbitrary", "arbitrary"), comms=comms)


def _hgrn_bwd(dcat, proj, o_raw, states, lb_table, norm_w, masks, comms=()):
    bsz, seq, _ = proj.shape
    nstep = seq // HG_TOK
    rec0 = ATT_WIDTH // LANE

    def body(tbl_ref, nw_ref, dr_ref, q_ref, f_ref, i_ref, g_ref, o_ref, st_ref, lower_ref, upper_ref,
             dq_ref, df_ref, di_ref, dg_ref, dlb_ref, dnw_ref, ds_scr):
        h, b, t = pl.program_id(0), pl.program_id(1), pl.program_id(2)

        @pl.when(t == 0)
        def _():
            ds_scr[...] = jnp.zeros_like(ds_scr)

        lower, upper = lower_ref[...], upper_ref[...]
        dlb_parts = []
        dnw_acc = jnp.zeros((1, LANE), F32)
        for hp in range(HG_HPS):
            ls = slice(hp * LANE, (hp + 1) * LANE)
            hq, v, hg = q_ref[:, ls], i_ref[:, ls], g_ref[:, ls]
            nw = nw_ref[...]
            c = _hgrn_common(tbl_ref[:, ls], f_ref[:, ls], hq)
            qd, kd, ku = c["qd"], c["kd"], c["ku"]
            y, on, rstd = _rms_fwd(o_ref[:, ls], nw)
            sg = _sigmoid(hg)
            dr = dr_ref[:, ls]
            dg_ref[:, ls] = (dr * y * (sg * (1.0 + hg * (1.0 - sg)))).astype(BF16)
            do, dnw_rows = _rms_bwd(dr * (hg * sg), on, rstd, nw)
            at = _dot(kd, qd, NT_DIMS) * upper
            da = _dot(do, v, NT_DIMS) * lower
            dat = _dot(v, do, NT_DIMS) * upper
            dv = _dot(at, do)
            dqd = _dot(da, kd)
            dkd = _dot(dat, qd)
            do_c, qd_c, v_c, ku_c = [_chunk_slices(z.astype(BF16)) for z in (do, qd, v, ku)]
            outer = [_dot(do_c[j], qd_c[j], TN_DIMS) for j in range(HG_NCH)]
            ds = ds_scr[hp]
            ds_after = [None] * HG_NCH
            for j in reversed(range(HG_NCH)):
                ds_after[j] = ds
                ds = outer[j] + ds * c["e_last"][j]
            ds_scr[hp] = ds
            states = [st_ref[hp, j] for j in range(HG_NCH)]
            dv = dv + jnp.concatenate([_dot(ku_c[j], ds_after[j], NT_DIMS) for j in range(HG_NCH)], axis=0)
            dqd = dqd + jnp.concatenate([_dot(do_c[j], states[j]) for j in range(HG_NCH)], axis=0)
            dku = jnp.concatenate([_dot(v_c[j], ds_after[j]) for j in range(HG_NCH)], axis=0)
            dku_ku = dku * ku
            dbl = [_colsum(states[j] * ds_after[j]) * c["e_last"][j] + _colsum(dku_ku[j * HG_CHUNK:(j + 1) * HG_CHUNK])
                   for j in range(HG_NCH)]
            dk = dkd * c["e_nb"] + dku * c["e_rem"]
            db = dqd * qd - dkd * kd - dku_ku + jnp.where(_row_in_chunk() == HG_CHUNK - 1, _chunk_rows(dbl), 0.0)
            dfv = _chunk_cumsum(db, reverse=True) / c["f"] - dk
            sig, sq = c["sig"], c["sq"]
            df_ref[:, ls] = (dfv * (1.0 - c["lb"]) * sig * (1.0 - sig)).astype(BF16)
            dq_ref[:, ls] = (dqd * c["e_b"] * (sq * (1.0 + hq * (1.0 - sq)))).astype(BF16)
            di_ref[:, ls] = dv.astype(BF16)
            dlb_parts.append(_colsum(dfv * (1.0 - sig)))
            dnw_acc = dnw_acc + _colsum(dnw_rows)
        _acc_out(dlb_ref, jnp.logical_and(b == 0, t == 0), jnp.concatenate(dlb_parts, axis=1))
        _acc_out(dnw_ref, jnp.logical_and(h == 0, jnp.logical_and(b == 0, t == 0)), dnw_acc)

    rev = lambda t: nstep - 1 - t
    width = HG_HPS * LANE
    slab = lambda first: pl.BlockSpec((None, HG_TOK, width), lambda h, b, t: (b, rev(t), first // HG_HPS + h))
    head = pl.BlockSpec((None, HG_TOK, width), lambda h, b, t: (b, rev(t), h))
    grad_shape = _sds((bsz, seq, HG_WIDTH), BF16)
    return _call(
        body, (lb_table, norm_w, dcat, proj, proj, proj, proj, o_raw, states, *masks), name="hgrn_bwd",
        grid=(HG_HEADS // HG_HPS, bsz, nstep),
        in_specs=[pl.BlockSpec((2, width), lambda h, b, t: (0, h)), pl.BlockSpec((1, LANE), lambda h, b, t: (0, 0)),
                  slab(rec0), slab(HG_Q0), slab(HG_F0), slab(HG_I0), slab(HG_G0), head,
                  pl.BlockSpec((None, HG_HPS, HG_NCH, LANE, LANE), lambda h, b, t: (b, h, rev(t), 0, 0)),
                  pl.BlockSpec((HG_TOK, HG_TOK), lambda h, b, t: (0, 0)), pl.BlockSpec((HG_TOK, HG_TOK), lambda h, b, t: (0, 0))],
        out_specs=[head, head, head, head, pl.BlockSpec((1, width), lambda h, b, t: (0, h)),
                   pl.BlockSpec((1, LANE), lambda h, b, t: (0, 0))],
        out_shape=[grad_shape, grad_shape, grad_shape, grad_shape, _sds((1, HG_WIDTH), F32), _sds((1, LANE), F32)],
        scratch_shapes=[pltpu.VMEM((HG_HPS, LANE, LANE), F32)],
        sem=("arbitrary", "arbitrary", "arbitrary"), comms=comms)


def _attn_bwd(dcat, raw, w_norm, qh, kh, vh, lse, sinks, tables, biases, comms=()):
    bsz, nblk = qh.shape[0], qh.shape[1]
    seq = nblk * WINDOW
    nstep = nblk // ATT_BPS
    half = ROPE_DIM // 2

    def body(sink_ref, da_ref, raw_ref, w_ref, q_ref, kc_ref, kp_ref, vc_ref, vp_ref, l_ref, c_ref, u_ref, d_ref,
             full_ref, first_ref, o_ref, dw_ref, dsink_ref, carry_k, carry_v):
        b, i = pl.program_id(0), pl.program_id(1)
        first = jnp.logical_and(b == 0, i == 0)

        @pl.when(i == 0)
        def _():
            carry_k[...] = jnp.zeros_like(carry_k)
            carry_v[...] = jnp.zeros_like(carry_v)

        w = w_ref[...]
        _, on, rstd = _rms_fwd(raw_ref[...], w)
        do_step, dw_rows = _rms_bwd(da_ref[...], on, rstd, w)
        _acc_out(dw_ref, first, _colsum(dw_rows))
        lane8 = lax.broadcasted_iota(jnp.int32, (1, ATT_Q_HEADS), 1)
        dsink = jnp.zeros((1, ATT_Q_HEADS), F32)
        from_next_k, from_next_v = carry_k[...], carry_v[...]
        for blk in reversed(range(ATT_BPS)):
            tok = slice(blk * WINDOW, (blk + 1) * WINDOW)
            bias = _band_bias(full_ref, first_ref, True if blk else i < nstep - 1)
            raw_v, do_all = raw_ref[tok, :], do_step[tok]
            c, u, d = c_ref[tok, :], u_ref[tok, :], d_ref[tok, :]

            def unrope(g):
                return (g * c + pltpu.roll(g * u, LANE - half, 1) + pltpu.roll(g * d, half, 1)).astype(BF16)

            dq_parts, dk_own, dk_before, dv_own, dv_before = [], [], [], [], []
            for g in range(ATT_KV_HEADS):
                heads = [slice((g * ATT_GROUP + hh) * ATT_HEAD_DIM, (g * ATT_GROUP + hh + 1) * ATT_HEAD_DIM)
                         for hh in range(ATT_GROUP)]
                q = q_ref[blk, g]
                keys, vals = _band(kp_ref, kc_ref, g, blk), _band(vp_ref, vc_ref, g, blk)
                do_g = jnp.concatenate([do_all[:, hs] for hs in heads], axis=0)
                dsum = jnp.concatenate([jnp.sum(do_all[:, hs] * raw_v[:, hs], axis=-1, keepdims=True) for hs in heads], axis=0)
                lse_g = jnp.concatenate([l_ref[tok, g * ATT_GROUP + hh:g * ATT_GROUP + hh + 1] for hh in range(ATT_GROUP)], axis=0)
                p = jnp.exp(_dot(q, keys, NT_DIMS) * ATT_SCALE + bias - lse_g)
                sink_part = jnp.exp(_sink_column(sink_ref, g) - lse_g) * dsum
                for hh in range(ATT_GROUP):
                    head_sum = jnp.sum(sink_part[hh * WINDOW:(hh + 1) * WINDOW], axis=0, keepdims=True)
                    dsink = dsink - jnp.where(lane8 == g * ATT_GROUP + hh, head_sum, 0.0)
                ds = p * (_dot(do_g, vals, NT_DIMS) - dsum) * ATT_SCALE
                dq_g = _dot(ds, keys)
                dq_parts += [dq_g[hh * WINDOW:(hh + 1) * WINDOW] for hh in range(ATT_GROUP)]
                dk_g = _dot(ds, q, TN_DIMS)
                dv_g = _dot(p, do_g, TN_DIMS)
                dk_before.append(dk_g[:WINDOW])
                dk_own.append(dk_g[WINDOW:])
                dv_before.append(dv_g[:WINDOW])
                dv_own.append(dv_g[WINDOW:])
            per_slab = LANE // ATT_HEAD_DIM
            for s in range(ATT_WIDTH // LANE):
                slab = jnp.concatenate(dq_parts[s * per_slab:(s + 1) * per_slab], axis=1)
                o_ref[tok, s * LANE:(s + 1) * LANE] = unrope(slab)
            o_ref[tok, ATT_WIDTH:ATT_WIDTH + LANE] = unrope(jnp.concatenate(dk_own, axis=1) + from_next_k)
            o_ref[tok, ATT_WIDTH + LANE:ATT_COLS] = (jnp.concatenate(dv_own, axis=1) + from_next_v).astype(BF16)
            from_next_k, from_next_v = jnp.concatenate(dk_before, axis=1), jnp.concatenate(dv_before, axis=1)
        carry_k[...] = from_next_k
        carry_v[...] = from_next_v
        _acc_out(dsink_ref, first, dsink)

    rows = ATT_BPS * WINDOW
    rev = lambda i: nstep - 1 - i
    cur = lambda width: pl.BlockSpec((None, rows, width), lambda b, i: (b, rev(i), 0))
    q_spec = pl.BlockSpec((None, ATT_BPS, ATT_KV_HEADS, GROUP_ROWS, ATT_HEAD_DIM), lambda b, i: (b, rev(i), 0, 0, 0))
    kv_cur = pl.BlockSpec((None, ATT_KV_HEADS, rows, ATT_HEAD_DIM), lambda b, i: (b, 0, rev(i), 0))
    kv_prev = pl.BlockSpec((None, ATT_KV_HEADS, WINDOW, ATT_HEAD_DIM), lambda b, i: (b, 0, jnp.maximum(ATT_BPS * rev(i) - 1, 0), 0))
    tab = pl.BlockSpec((rows, LANE), lambda b, i: (rev(i), 0))
    return _call(
        body, (sinks, dcat, raw, w_norm, qh, kh, kh, vh, vh, lse, *tables, *biases), name="attn_bwd", grid=(bsz, nstep),
        in_specs=[pl.BlockSpec(memory_space=pltpu.SMEM), cur(ATT_WIDTH), cur(ATT_WIDTH), _vec_spec(ATT_WIDTH), q_spec,
                  kv_cur, kv_prev, kv_cur, kv_prev, cur(ATT_Q_HEADS), tab, tab, tab, _bias_spec(), _bias_spec()],
        out_specs=[cur(ATT_COLS), _vec_spec(ATT_WIDTH), _vec_spec(ATT_Q_HEADS)],
        out_shape=[_sds((bsz, seq, ATT_COLS), BF16), _sds((1, ATT_WIDTH), F32), _sds((1, ATT_Q_HEADS), F32)],
        scratch_shapes=[pltpu.VMEM((WINDOW, LANE), F32), pltpu.VMEM((WINDOW, LANE), F32)],
        sem=("arbitrary", "arbitrary"), comms=comms)


def _other_chips(x, y):
    return [(1 - x, y), (x, 1 - y), (1 - x, 1 - y)]


def _sem_pair(n):
    return [pltpu.SemaphoreType.DMA((n,)), pltpu.SemaphoreType.DMA((n,))]


def _plan_pair_forward(bufs):
    n = len(bufs)

    def copies(outs, sems):
        x, y, c = _mesh_pos()
        sends, lands = [], []
        for a in range(n):
            for j, chip in enumerate(_other_chips(x, y)):
                k = 3 * a + j
                slot = outs[a].at[4 * chip[0] + 2 * chip[1] + c]
                sends.append(pltpu.make_async_remote_copy(
                    src_ref=slot, dst_ref=slot, send_sem=sems[0].at[k], recv_sem=sems[1].at[k],
                    device_id=(x, y, 1 - c), device_id_type=MESH))
                theirs = outs[a].at[4 * chip[0] + 2 * chip[1] + 1 - c]
                lands.append(pltpu.make_async_remote_copy(
                    src_ref=theirs, dst_ref=theirs, send_sem=sems[0].at[k], recv_sem=sems[1].at[k],
                    device_id=(x, y, 1 - c), device_id_type=MESH))
        return sends, lands

    def start(ins, outs, sems):
        for cp in copies(outs, sems)[0]:
            cp.start()

    def finish(ins, outs, sems):
        sends, lands = copies(outs, sems)
        for cp in lands:
            cp.wait_recv()
        for cp in sends:
            cp.wait_send()

    return _Comm(list(bufs), [_sds(b.shape, b.dtype) for b in bufs], _sem_pair(3 * n), start, finish,
                 aliases=[(a, a) for a in range(n)])


def _plan_pair(arrays, other_half):
    n = len(arrays)
    per = N_CHIPS if other_half == "chip_major" else 1

    def copies(ins, outs, sems):
        x, y, c = _mesh_pos()
        out = []
        for a in range(n):
            for k in range(per):
                if other_half == "chip_major":
                    src, dst = ins[a].at[k, 1 - c], outs[a].at[k]
                else:
                    src, dst = (ins[a].at[1 - c] if other_half else ins[a]), outs[a]
                out.append(pltpu.make_async_remote_copy(
                    src_ref=src, dst_ref=dst, send_sem=sems[0].at[per * a + k], recv_sem=sems[1].at[per * a + k],
                    device_id=(x, y, 1 - c), device_id_type=MESH))
        return out

    def start(ins, outs, sems):
        for cp in copies(ins, outs, sems):
            cp.start()

    def finish(ins, outs, sems):
        for cp in copies(ins, outs, sems):
            cp.wait()

    if other_half == "chip_major":
        shapes = [_sds((a.shape[0],) + a.shape[2:], a.dtype) for a in arrays]
    else:
        shapes = [_sds(a.shape[1:] if other_half else a.shape, a.dtype) for a in arrays]
    return _Comm(list(arrays), shapes, _sem_pair(per * n), start, finish)


def _plan_chip_exchange(arrays):
    n = len(arrays)

    def copies(ins, outs, sems):
        x, y, c = _mesh_pos()
        sends, lands = [], []
        for a in range(n):
            for j, chip in enumerate(_other_chips(x, y)):
                k = 3 * a + j
                sends.append(pltpu.make_async_remote_copy(
                    src_ref=ins[a].at[2 * chip[0] + chip[1]], dst_ref=outs[a].at[2 * x + y], send_sem=sems[0].at[k],
                    recv_sem=sems[1].at[k], device_id=(*chip, c), device_id_type=MESH))
                slot = outs[a].at[2 * chip[0] + chip[1]]
                lands.append(pltpu.make_async_remote_copy(
                    src_ref=slot, dst_ref=slot, send_sem=sems[0].at[k], recv_sem=sems[1].at[k],
                    device_id=(*chip, c), device_id_type=MESH))
        return sends, lands

    def start(ins, outs, sems):
        for cp in copies(ins, outs, sems)[0]:
            cp.start()

    def finish(ins, outs, sems):
        sends, lands = copies(ins, outs, sems)
        for cp in lands:
            cp.wait_recv()
        for cp in sends:
            cp.wait_send()

    return _Comm(list(arrays), [_sds(a.shape, a.dtype) for a in arrays], _sem_pair(3 * n), start, finish)


SEM_SPEC = pl.BlockSpec(memory_space=pltpu.SEMAPHORE)
N_OTHER = N_CHIPS - 1


def _exchange_copies(s_ref, land_ref, sems):
    x, y, c = _mesh_pos()
    return [pltpu.make_async_remote_copy(
        src_ref=s_ref.at[2 * chip[0] + chip[1]], dst_ref=land_ref.at[2 * x + y], send_sem=sems[j], recv_sem=sems[N_OTHER + j],
        device_id=(*chip, c), device_id_type=MESH) for j, chip in enumerate(_other_chips(x, y))]


def _exchange_start(s, name):
    def body(s_ref, land_ref, *outs):
        sems, token = outs[:2 * N_OTHER], outs[-1]
        for cp in _exchange_copies(s_ref, land_ref, sems):
            cp.start()
        token[...] = jnp.zeros_like(token)

    hbm = pltpu.HBM(s.shape, s.dtype)
    res = pl.pallas_call(
        body, name=name,
        out_shape=(pltpu.SemaphoreType.DMA(()),) * (2 * N_OTHER) + (hbm, hbm, _sds((SUBLANES, LANE), F32)),
        in_specs=(HBM_SPEC, HBM_SPEC),
        out_specs=(SEM_SPEC,) * (2 * N_OTHER) + (HBM_SPEC, HBM_SPEC, pl.BlockSpec(memory_space=pltpu.VMEM)),
        input_output_aliases={0: 2 * N_OTHER, 1: 2 * N_OTHER + 1},
        compiler_params=pltpu.CompilerParams(has_side_effects=pltpu.SideEffectType.DATAFLOW_SIDE_EFFECTING),
    )(pltpu.with_memory_space_constraint(s, pltpu.HBM), pltpu.with_memory_space_constraint(lax.empty(s.shape, s.dtype), pltpu.HBM))
    return res[:2 * N_OTHER], res[2 * N_OTHER], res[2 * N_OTHER + 1], res[-1]


def _exchange_wait(sems, s_thru, land_thru, afters, name):
    def body(s_ref, land_ref, *rest):
        for cp in _exchange_copies(s_ref, land_ref, rest[:2 * N_OTHER]):
            cp.wait_send()
            cp.wait_recv()

    hbm = pltpu.HBM(s_thru.shape, s_thru.dtype)
    return pl.pallas_call(
        body, name=name, out_shape=(hbm, hbm),
        in_specs=(HBM_SPEC, HBM_SPEC) + (SEM_SPEC,) * (2 * N_OTHER) + (pl.BlockSpec(memory_space=pl.ANY),) * len(afters),
        out_specs=(HBM_SPEC, HBM_SPEC), input_output_aliases={0: 0, 1: 1},
        compiler_params=pltpu.CompilerParams(has_side_effects=pltpu.SideEffectType.DATAFLOW_SIDE_EFFECTING),
    )(s_thru, land_thru, *sems, *afters)


def _gather_copies(block_ref, buf_ref, sems):
    x, y, c = _mesh_pos()
    return [pltpu.make_async_remote_copy(
        src_ref=block_ref, dst_ref=buf_ref.at[4 * x + 2 * y + c], send_sem=sems[j], recv_sem=sems[N_OTHER + j],
        device_id=(*chip, c), device_id_type=MESH) for j, chip in enumerate(_other_chips(x, y))]


def _gather_start(blocks, bufs, afters, name):
    n = len(blocks)
    per = 2 * N_OTHER

    def body(*refs):
        ins, outs = refs[:2 * n], refs[2 * n + len(afters):]
        for a in range(n):
            for cp in _gather_copies(ins[a], ins[n + a], outs[a * per:(a + 1) * per]):
                cp.start()
        outs[-1][...] = jnp.zeros_like(outs[-1])

    hbm = [pltpu.HBM(z.shape, z.dtype) for z in list(blocks) + list(bufs)]
    res = pl.pallas_call(
        body, name=name,
        out_shape=(pltpu.SemaphoreType.DMA(()),) * (n * per) + tuple(hbm) + (_sds((SUBLANES, LANE), F32),),
        in_specs=(HBM_SPEC,) * (2 * n) + (pl.BlockSpec(memory_space=pl.ANY),) * len(afters),
        out_specs=(SEM_SPEC,) * (n * per) + (HBM_SPEC,) * (2 * n) + (pl.BlockSpec(memory_space=pltpu.VMEM),),
        input_output_aliases={k: n * per + k for k in range(2 * n)},
        compiler_params=pltpu.CompilerParams(has_side_effects=pltpu.SideEffectType.DATAFLOW_SIDE_EFFECTING),
    )(*[pltpu.with_memory_space_constraint(z, pltpu.HBM) for z in list(blocks) + list(bufs)], *afters)
    parts = [(res[a * per:(a + 1) * per], res[n * per + a], res[n * per + n + a]) for a in range(n)]
    return parts, res[-1]


def _gather_wait(part, afters, name):
    sems, block, buf = part

    def body(block_ref, buf_ref, *rest):
        for cp in _gather_copies(block_ref, buf_ref, rest[:2 * N_OTHER]):
            cp.wait_send()
            cp.wait_recv()

    return pl.pallas_call(
        body, name=name, out_shape=(pltpu.HBM(block.shape, block.dtype), pltpu.HBM(buf.shape, buf.dtype)),
        in_specs=(HBM_SPEC, HBM_SPEC) + (SEM_SPEC,) * (2 * N_OTHER) + (pl.BlockSpec(memory_space=pl.ANY),) * len(afters),
        out_specs=(HBM_SPEC, HBM_SPEC), input_output_aliases={0: 0, 1: 1},
        compiler_params=pltpu.CompilerParams(has_side_effects=pltpu.SideEffectType.DATAFLOW_SIDE_EFFECTING),
    )(block, buf, *sems, *afters)[1]


def _comm_only(comms, name):
    return _call(lambda: None, (), name=name, grid=(), in_specs=[], out_specs=[], out_shape=[], sem=(), comms=comms)[1]


def _allgather8(arrays, name):
    return _comm_only([_plan_allgather8(arrays)], name)[0]


def _plan_allgather8(arrays):
    n = len(arrays)

    def parts(ins, outs, sems):
        send_sems, recv_sems, local_sems = sems
        x, y, c = _mesh_pos()
        me, sibling = (x, y, c), (x, y, 1 - c)
        chips = _other_chips(x, y)

        def copy(a, k, block, to, src=None):
            dst = outs[a].at[4 * block[0] + 2 * block[1] + block[2]]
            return pltpu.make_async_remote_copy(
                src_ref=dst if src is None else src, dst_ref=dst, send_sem=send_sems.at[7 * a + k],
                recv_sem=recv_sems.at[7 * a + k], device_id=to, device_id_type=MESH)

        mine = [pltpu.make_async_copy(ins[a], outs[a].at[4 * x + 2 * y + c], local_sems.at[a]) for a in range(n)]
        first = []
        for a in range(n):
            first.append(copy(a, 0, me, sibling, src=ins[a]))
            first += [copy(a, 1 + j, me, (*chip, c), src=ins[a]) for j, chip in enumerate(chips)]
        return copy, mine, first, me, sibling, chips, c

    def start(ins, outs, sems):
        _, mine, first, *_ = parts(ins, outs, sems)
        for cp in mine + first:
            cp.start()

    def finish(ins, outs, sems):
        copy, mine, first, me, sibling, chips, c = parts(ins, outs, sems)
        passed = []
        for j, chip in enumerate(chips):
            for a in range(n):
                copy(a, 1 + j, (*chip, c), me).wait_recv()
                fwd = copy(a, 4 + j, (*chip, c), sibling)
                fwd.start()
                passed.append(fwd)
        for a in range(n):
            copy(a, 0, sibling, me).wait_recv()
            for j, chip in enumerate(chips):
                copy(a, 4 + j, (*chip, 1 - c), me).wait_recv()
        for cp in first + passed:
            cp.wait_send()
        for cp in mine:
            cp.wait()

    sems = [pltpu.SemaphoreType.DMA((7 * n,)), pltpu.SemaphoreType.DMA((7 * n,)), pltpu.SemaphoreType.DMA((n,))]
    return _Comm(list(arrays), [_sds((N_DEV,) + a.shape, a.dtype) for a in arrays], sems, start, finish)


def _pair_sum(g, q, core, name, chip_major=False):
    rows, cols = g.shape[2:]
    tr = _row_tile(rows)

    def body(core_ref, g_ref, q_ref, o_ref):
        o_ref[...] = (g_ref[...] + q_ref[...]).astype(BF16)

    blk = pl.BlockSpec((None, tr, cols), lambda k, i, core_ref: (k, i, 0))
    if chip_major:
        own = pl.BlockSpec((None, None, tr, cols), lambda k, i, core_ref: (k, core_ref[0], i, 0))
    else:
        own = pl.BlockSpec((None, None, tr, cols), lambda k, i, core_ref: (core_ref[0], k, i, 0))
    return pl.pallas_call(
        body, name=name,
        grid_spec=pltpu.PrefetchScalarGridSpec(num_scalar_prefetch=1, grid=(N_CHIPS, rows // tr), in_specs=[own, blk], out_specs=blk),
        out_shape=_sds((N_CHIPS, rows, cols), BF16), compiler_params=_params("parallel", "parallel"),
    )(core, g, q)


def _sum_chips(own, landed, chip, name):
    _, rows, cols = own.shape
    tr = _row_tile(rows)

    def body(chip_ref, own_ref, a_ref, b_ref, c_ref, o_ref):
        acc = own_ref[...].astype(F32) + a_ref[...].astype(F32)
        o_ref[...] = (acc + b_ref[...].astype(F32)) + c_ref[...].astype(F32)

    blk = lambda flip: pl.BlockSpec((None, tr, cols), lambda i, chip_ref: (jnp.bitwise_xor(chip_ref[0], flip), i, 0))
    return pl.pallas_call(
        body, name=name,
        grid_spec=pltpu.PrefetchScalarGridSpec(num_scalar_prefetch=1, grid=(rows // tr,), in_specs=[blk(0), blk(1), blk(2), blk(3)],
                                               out_specs=pl.BlockSpec((tr, cols), lambda i, chip_ref: (i, 0))),
        out_shape=_sds((rows, cols), F32), compiler_params=_params("parallel"),
    )(chip, own, landed, landed, landed)


SUBLANES = 8


def _tile_rows(n_elems):
    return -(-n_elems // (SUBLANES * LANE)) * SUBLANES


SMALL_ITEMS = (("b_ada", N_MOD * D_MODEL), ("pre_w_mix", D_MODEL), ("post_w_mix", D_MODEL), ("pre_w_mlp", D_MODEL),
               ("post_w_mlp", D_MODEL), ("attn_out_w", ATT_WIDTH), ("hg_norm_w", HG_HEAD_DIM), ("attn_sinks", ATT_Q_HEADS),
               ("lb_0", HG_WIDTH), ("lb_1", HG_WIDTH))
SMALL_AT = {}
for _name, _size in SMALL_ITEMS:
    SMALL_AT[_name] = (sum(r for _, r in SMALL_AT.values()), _tile_rows(_size))
SMALL_ROWS = sum(r for _, r in SMALL_AT.values())
MOD_ROWS = SMALL_AT["b_ada"][1]
PLAIN_ROWS = SMALL_AT["lb_0"][0] - MOD_ROWS
LB_ROWS = SMALL_AT["lb_0"][1]


def _rows(a, nrows=None):
    flat = a.reshape(-1)
    nrows = _tile_rows(flat.shape[0]) if nrows is None else nrows
    return jnp.pad(flat, (0, nrows * LANE - flat.shape[0])).reshape(nrows, LANE)


def _pack_small(vals):
    vals = dict(vals, lb_0=vals["lb_table"][0], lb_1=vals["lb_table"][1])
    return jnp.concatenate([_rows(vals[name], SMALL_AT[name][1]) for name, _ in SMALL_ITEMS], axis=0)


def _unpack_small(p):
    def item(name, shape):
        first = SMALL_AT[name][0]
        size = shape[0] * shape[1]
        return p[first:first + SMALL_AT[name][1]].reshape(-1)[:size].reshape(shape)

    out = {name: item(name, (1, size)) for name, size in SMALL_ITEMS if not name.startswith("lb_")}
    out["lb_table"] = jnp.concatenate([item("lb_0", (1, HG_WIDTH)), item("lb_1", (1, HG_WIDTH))], axis=0)
    return out


def _pack_partials(dmod, plain, d_lb, loss_row):
    return jnp.concatenate([_rows(dmod, dmod.shape[0] * MOD_ROWS)] + [_rows(g) for g in plain] + [_rows(d_lb), _rows(loss_row)], axis=0)


def _small_update(packs, w, m, v, n_seq):
    mod_end = n_seq * MOD_ROWS
    lb_at = mod_end + PLAIN_ROWS
    t0, t1 = SMALL_AT["lb_0"][0], SMALL_AT["lb_1"][0]

    def body(p_ref, w_ref, m_ref, v_ref, g_ref, dl_ref, nm_ref, nv_ref, loss_ref):
        tot = p_ref[0]
        for d in range(1, N_DEV):
            tot = tot + p_ref[d]
        wv = w_ref[...]
        p1 = _sigmoid(wv[t1:t1 + LB_ROWS] - wv[t0:t0 + LB_ROWS])
        s = tot[lb_at:lb_at + LB_ROWS] * p1 * (1.0 - p1)
        g_bias = tot[0:MOD_ROWS]
        for q in range(1, n_seq):
            g_bias = g_bias + tot[q * MOD_ROWS:(q + 1) * MOD_ROWS]
        g = jnp.concatenate([g_bias, tot[mod_end:lb_at], -s, s], axis=0)
        g_ref[...] = g
        dl_ref[...], nm_ref[...], nv_ref[...] = _adamw_math(g, wv, m_ref[...], v_ref[...])
        loss_ref[...] = tot[lb_at + LB_ROWS:lb_at + LB_ROWS + SUBLANES]

    shp = _sds((SMALL_ROWS, LANE), F32)
    return pl.pallas_call(body, name="small_update", out_shape=[shp] * 4 + [_sds((SUBLANES, LANE), F32)],
                          compiler_params=_params())(packs, w, m, v)


def kernel(x, c, w_ada, b_ada, pre_w_mix, w_in, attn_sinks, attn_out_w, lb_table, hg_norm_w, w_out, post_w_mix, pre_w_mlp, w_up, w_down, post_w_mlp, loss_target, m_w_ada, m_b_ada, m_pre_w_mix, m_w_in, m_attn_sinks, m_attn_out_w, m_lb_table, m_hg_norm_w, m_w_out, m_post_w_mix, m_pre_w_mlp, m_w_up, m_w_down, m_post_w_mlp, v_w_ada, v_b_ada, v_pre_w_mix, v_w_in, v_attn_sinks, v_attn_out_w, v_lb_table, v_hg_norm_w, v_w_out, v_post_w_mix, v_pre_w_mlp, v_w_up, v_w_down, v_post_w_mlp):
    xi, yi, ci = _mesh_pos()
    chip = 2 * xi + yi
    dev = 2 * chip + ci
    bsz, seq, _ = x.shape
    ntok = bsz * seq
    ada_cols = w_ada.shape[2]
    core = jnp.reshape(ci, (1,)).astype(jnp.int32)
    chip_idx = jnp.reshape(chip, (1,)).astype(jnp.int32)
    flat = lambda a: a.reshape(ntok, a.shape[-1])
    unflat = lambda a: a.reshape(bsz, seq, a.shape[-1])
    tables = _rope_tables(seq)
    biases, chunk_masks = _band_biases(), _block_masks()

    def row_half(w):
        rows = w.shape[1] // 2
        return lax.dynamic_slice_in_dim(w[0], ci * rows, rows, axis=0).astype(BF16)

    def gather_buffer(w):
        rows, cols = w.shape[1] // 2, w.shape[2]
        own = w[0].astype(BF16).reshape(2, rows, cols)
        return lax.dynamic_update_slice(lax.empty((N_DEV, rows, cols), BF16), own, (2 * chip, 0, 0))

    w_in_t, m_in_t, v_in_t = [jnp.transpose(a[0])[None] for a in (w_in, m_w_in, v_w_in)]
    c_g, in_g = _allgather8([c, row_half(w_in_t)], "gather_first")
    c_all = c_g.reshape(N_DEV * bsz, D_MODEL)
    w_in_full = in_g.reshape(IN_COLS, D_MODEL)

    b_cols = lax.dynamic_slice_in_dim(b_ada, chip * ada_cols, ada_cols, axis=1)
    mod_part = _ada_fwd(c_all, w_ada[0], b_cols)
    half_rows = mod_part.shape[0] // 2
    (mod_g,) = _allgather8([lax.dynamic_slice_in_dim(mod_part, ci * half_rows, half_rows, axis=0)], "gather_mod")
    mod_all = mod_g.reshape(N_CHIPS, 2, half_rows, ada_cols).transpose(1, 2, 0, 3).reshape(N_DEV * bsz, N_MOD * D_MODEL)
    mod = lax.dynamic_slice_in_dim(mod_all, dev * bsz, bsz, axis=0)
    sh1, sc1, g1, sh2, sc2, g2 = [mod[:, i * D_MODEL:(i + 1) * D_MODEL].reshape(bsz, 1, D_MODEL) for i in range(N_MOD)]

    weights = (w_out, w_up, w_down)
    (out_part, up_part, down_part), started = _gather_start(
        [row_half(w) for w in weights], [gather_buffer(w) for w in weights], [mod_g], "gather_weights_start")

    h1, proj, qh, kh, vh = _in_proj_fused(x, pre_w_mix, sc1 + started[0:1, 0:1], sh1, w_in_full, tables)
    out_g = _gather_wait(out_part, [proj], "gather_out_wait")
    (attn_raw, cat, lse), ((out_g,),) = _attn_fwd(qh, kh, vh, attn_sinks, attn_out_w, biases, comms=[_plan_pair_forward([out_g])])
    up_g = _gather_wait(up_part, [attn_raw], "gather_up_wait")
    (o_raw, cat, states), ((up_g,),) = _hgrn_fwd(proj, lb_table, hg_norm_w, cat, chunk_masks, comms=[_plan_pair_forward([up_g])])
    down_g = _gather_wait(down_part, [o_raw], "gather_down_wait")
    w_out_full = out_g.reshape(D_MODEL, D_MODEL)
    w_up4 = up_g.reshape(N_CHIPS, D_MODEL, D_MODEL)
    mix, x1, h2 = _out_proj_fused(cat, w_out_full, x, post_w_mix, g1, pre_w_mlp, sc2, sh2)
    big_tm = min(ntok, 2048)
    up_spec = pl.BlockSpec((None, D_MODEL, D_MODEL), lambda i, j: (j, 0, 0))
    r, ((down_g,),) = _mm(flat(h2), w_up4, name="up_proj", out_dtype=BF16, tm=big_tm, tn=D_MODEL, n_out=D_FF, b_spec=up_spec,
                          epi=lambda acc: jnp.maximum(acc, 0.0), comms=[_plan_pair_forward([down_g])])
    w_down_full = down_g.reshape(D_FF, D_MODEL)
    square = lambda t: t * t
    loss_row, dy, dd, dg2, d_post_mlp = _down_proj_fused(unflat(r), w_down_full, x1, post_w_mlp, g2, loss_target)

    dpre = _mm(flat(dd), w_down_full, name="down_bwd", out_dtype=BF16, trans_b=True, tm=big_tm, tn=D_MODEL, extra=(r,),
               epi=lambda acc, rt: acc * (2.0 * rt.astype(F32)))
    half_rows = D_MODEL // 2
    g_down = _mm_tn(r, flat(dd), name="down_wgrad", tk=half_rows, tn=D_MODEL, a_fn=square,
                    out_shape=_sds((2, N_CHIPS, half_rows, D_MODEL), F32),
                    out_spec=pl.BlockSpec((None, None, half_rows, D_MODEL), lambda i, j: (i % 2, i // 2, 0, 0)))
    (dx1, dmix, dsc2, dsh2, dg1, d_pre_mlp, d_post_mix), ((q_down,),) = _up_bwd_fused(
        unflat(dpre), w_up4, dy, x1, mix, pre_w_mlp, sc2, post_w_mix, g1, comms=[_plan_pair([g_down], True)])
    g_up = _mm_tn(flat(h2), dpre, name="up_wgrad", tk=D_MODEL, tn=half_rows,
                  out_shape=_sds((2, N_CHIPS, half_rows, D_MODEL), F32),
                  out_spec=pl.BlockSpec((2, None, half_rows, half_rows), lambda i, j: (0, j // 2, 0, j % 2)))
    s_down = _pair_sum(g_down, q_down, core, "pair_sum_down")

    dcat, ((q_up,),) = _mm(flat(dmix), w_out_full, name="out_bwd", out_dtype=F32, trans_b=True, comms=[_plan_pair([g_up], True)])
    dcat = unflat(dcat)
    s_up = _pair_sum(g_up, q_up, core, "pair_sum_up")
    out_rows = D_MODEL // N_CHIPS
    g_out = _mm_tn(flat(cat), flat(dmix), name="out_wgrad", tk=2 * out_rows, tn=half_rows,
                   out_shape=_sds((2, N_CHIPS, out_rows, half_rows), F32),
                   out_spec=pl.BlockSpec((None, 2, out_rows, half_rows), lambda i, j: (j, i, 0, 0)))
    (dhq, dhf, dhi, dhg, d_lb, d_hg_norm), ((x_down,), (q_out,)) = _hgrn_bwd(
        dcat, proj, o_raw, states, lb_table, hg_norm_w, chunk_masks, comms=[_plan_chip_exchange([s_down]), _plan_pair([g_out], True)])
    half_down = _sum_chips(s_down, x_down, chip_idx, "sum_chips_down")
    s_out = _pair_sum(g_out, q_out, core, "pair_sum_out")
    (dproj_a, d_attn_out, d_sinks), ((their_down,), (x_up, x_out)) = _attn_bwd(
        dcat, attn_raw, attn_out_w, qh, kh, vh, lse, attn_sinks, tables, biases,
        comms=[_plan_pair([half_down], False), _plan_chip_exchange([s_up, s_out])])
    half_up = _sum_chips(s_up, x_up, chip_idx, "sum_chips_up")
    half_out = _sum_chips(s_out, x_out, chip_idx, "sum_chips_out")
    dproj = flat(jnp.concatenate([dproj_a, dhq, dhf, dhi, dhg], axis=-1))
    in_rows = IN_COLS // N_CHIPS // 2
    g_in = _mm_tn(dproj, flat(h1), name="in_wgrad", tk=2 * LANE, tn=D_MODEL).reshape(N_CHIPS, 2, in_rows, D_MODEL)
    dh1, ((q_in,), (their_up, their_out)) = _mm(
        dproj, w_in_full, name="in_bwd", out_dtype=F32,
        comms=[_plan_pair([g_in], "chip_major"), _plan_pair([half_up, half_out], False)])
    s_in = _pair_sum(g_in, q_in, core, "pair_sum_in", chip_major=True)
    in_sems, s_in, in_landing, started = _exchange_start(s_in, "exchange_in_start")
    grad_x, dsc1, dsh1, d_pre_mix = _norm1_bwd(unflat(dh1), dx1, x, pre_w_mix + started[0:1, 0:1], sc1)

    dmod = jnp.concatenate([dsh1, dsc1, dg1, dsh2, dsc2, dg2], axis=-1).reshape(bsz, N_MOD * D_MODEL)
    pack = _pack_partials(dmod, [d_pre_mix, d_post_mix, d_pre_mlp, d_post_mlp, d_attn_out, d_hg_norm, d_sinks], d_lb, loss_row)
    ((packs,),) = _comm_only([_plan_allgather8([pack])], "gather_small")
    w_small = dict(b_ada=b_ada, pre_w_mix=pre_w_mix, post_w_mix=post_w_mix, pre_w_mlp=pre_w_mlp, post_w_mlp=post_w_mlp,
                   attn_out_w=attn_out_w, hg_norm_w=hg_norm_w, attn_sinks=attn_sinks, lb_table=lb_table)
    m_small = dict(b_ada=m_b_ada, pre_w_mix=m_pre_w_mix, post_w_mix=m_post_w_mix, pre_w_mlp=m_pre_w_mlp, post_w_mlp=m_post_w_mlp,
                   attn_out_w=m_attn_out_w, hg_norm_w=m_hg_norm_w, attn_sinks=m_attn_sinks, lb_table=m_lb_table)
    v_small = dict(b_ada=v_b_ada, pre_w_mix=v_pre_w_mix, post_w_mix=v_post_w_mix, pre_w_mlp=v_pre_w_mlp, post_w_mlp=v_post_w_mlp,
                   attn_out_w=v_attn_out_w, hg_norm_w=v_hg_norm_w, attn_sinks=v_attn_sinks, lb_table=v_lb_table)
    *small_packed, loss_rows = _small_update(packs, _pack_small(w_small), _pack_small(m_small), _pack_small(v_small), bsz)
    small_out = [_unpack_small(p) for p in small_packed]
    loss = loss_rows[0, 0]

    dmod_all = packs[:, :bsz * MOD_ROWS, :].reshape(N_DEV * bsz, N_MOD * D_MODEL)
    dmod_cols = lax.dynamic_slice_in_dim(dmod_all, chip * ada_cols, ada_cols, axis=1)
    ada_out = _ada_bwd_adamw(c_all, dmod_cols, w_ada[0], m_w_ada[0], v_w_ada[0])

    s_in, x_in = _exchange_wait(in_sems, s_in, in_landing, [grad_x, ada_out[0]], "exchange_in_wait")
    half_in = _sum_chips(s_in, x_in, chip_idx, "sum_chips_in")
    ((their_in,),) = _comm_only([_plan_pair([half_in], False)], "pair_swap_in")
    big = dict(
        w_in=tuple(jnp.transpose(a) for a in _adamw_halves(half_in, their_in, core, w_in_t[0], m_in_t[0], v_in_t[0], axis=0,
                                                           name="adamw_in")),
        w_up=tuple(_adamw_halves(half_up, their_up, core, w_up[0], m_w_up[0], v_w_up[0], axis=0, name="adamw_up")),
        w_out=tuple(_adamw_halves(half_out, their_out, core, w_out[0], m_w_out[0], v_w_out[0], axis=1, name="adamw_out")),
        w_down=tuple(_adamw_halves(half_down, their_down, core, w_down[0], m_w_down[0], v_w_down[0], axis=0, name="adamw_down")),
        w_ada=tuple(ada_out),
    )
    order = ("w_ada", "b_ada", "pre_w_mix", "w_in", "attn_sinks", "attn_out_w", "lb_table", "hg_norm_w", "w_out", "post_w_mix",
             "pre_w_mlp", "w_up", "w_down", "post_w_mlp")
    outs = [loss, grad_x]
    for kind in range(4):
        for nm in order:
            outs.append(big[nm][kind][None] if nm in big else small_out[kind][nm])
    return tuple(outs)
```

```python
import jax
import jax.numpy as jnp
from jax import lax
from jax.experimental import pallas as pl
from jax.experimental.pallas import tpu as pltpu

F32 = jnp.float32
BF16 = jnp.bfloat16

D_MODEL = 1024
ATT_WIDTH = 512
ATT_HEAD_DIM = 64
ATT_Q_HEADS = 8
ATT_KV_HEADS = 2
ATT_GROUP = ATT_Q_HEADS // ATT_KV_HEADS
ATT_KV_COLS = ATT_KV_HEADS * ATT_HEAD_DIM
WINDOW = 128
ROPE_DIM = 16
ROPE_THETA = 500000.0
HG_WIDTH = 512
MIX_WIDTH = ATT_WIDTH + HG_WIDTH
HG_HEAD_DIM = 128
HG_HEADS = 4
HG_CHUNK = 32
IN_COLS = ATT_WIDTH + 2 * ATT_KV_COLS + 4 * HG_WIDTH
ATT_COLS = ATT_WIDTH + 2 * ATT_KV_COLS
D_FF = 4 * D_MODEL
N_MOD = 6
EPS = 1e-6
ATT_SCALE = ATT_HEAD_DIM ** -0.5

ADAM_LR = 0.001
ADAM_B1 = 0.9
ADAM_B2 = 0.999
ADAM_EPS = 1e-08
ADAM_WD = 0.01
ADAM_STEP = 10

N_CHIPS = 4
N_DEV = 8
LANE = 128
VMEM_LIMIT = 48 * 1024 * 1024
VMEM_LIMIT_BIG = 58 * 1024 * 1024
MESH = pl.DeviceIdType.MESH

NT_DIMS = (((1,), (1,)), ((), ()))
TN_DIMS = (((0,), (0,)), ((), ()))


def _sds(shape, dtype):
    return jax.ShapeDtypeStruct(tuple(shape), dtype)


def _params(*sem, vmem_limit=None):
    return pltpu.CompilerParams(dimension_semantics=sem, vmem_limit_bytes=VMEM_LIMIT if vmem_limit is None else vmem_limit)


def _sigmoid(x):
    return 1.0 / (1.0 + jnp.exp(-x))


def _dot(a, b, dims=None):
    a, b = a.astype(BF16), b.astype(BF16)
    if dims is None:
        return jnp.dot(a, b, preferred_element_type=F32)
    return lax.dot_general(a, b, dims, preferred_element_type=F32)


def _rms_fwd(x, w):
    rstd = lax.rsqrt(jnp.mean(x * x, axis=-1, keepdims=True) + EPS)
    xh = x * rstd
    return xh * w, xh, rstd


def _rms_bwd(dy, xh, rstd, w):
    dxh = dy * w
    dx = rstd * (dxh - xh * jnp.mean(dxh * xh, axis=-1, keepdims=True))
    return dx, dy * xh


def _colsum(x):
    return jnp.sum(x, axis=0, keepdims=True)


def _row_tile(rows, cap=256):
    return max(t for t in range(16, cap + 1, 16) if rows % t == 0)


HBM_SPEC = pl.BlockSpec(memory_space=pltpu.HBM)


def _mesh_pos():
    return lax.axis_index("x"), lax.axis_index("y"), lax.axis_index("c")


class _Comm:
    def __init__(self, ins, outs, sems, start, finish, aliases=()):
        self.ins, self.outs, self.sems = list(ins), list(outs), list(sems)
        self.start, self.finish, self.aliases = start, finish, tuple(aliases)


def _call(body, args, *, name, grid, in_specs, out_specs, out_shape, sem, scratch_shapes=(), comms=(), aliases=None,
          vmem_limit=None):
    scratch_shapes = list(scratch_shapes)
    if not comms:
        return pl.pallas_call(body, name=name, grid=grid, in_specs=in_specs, out_specs=out_specs, out_shape=out_shape,
                              input_output_aliases=dict(aliases or {}), scratch_shapes=scratch_shapes,
                              compiler_params=_params(*sem, vmem_limit=vmem_limit))(*args)
    single = not isinstance(out_shape, (list, tuple))
    out_specs_l = [out_specs] if single else list(out_specs)
    out_shape_l = [out_shape] if single else list(out_shape)
    n_in, n_out, n_scr = len(in_specs), len(out_shape_l), len(scratch_shapes)
    n_ci = [len(cm.ins) for cm in comms]
    n_co = [len(cm.outs) for cm in comms]
    n_cs = [len(cm.sems) for cm in comms]
    aliases = dict(aliases or {})
    for k, cm in enumerate(comms):
        for i, o in cm.aliases:
            aliases[n_in + sum(n_ci[:k]) + i] = n_out + sum(n_co[:k]) + o

    def fused(*refs):
        pos = [0]

        def take(n):
            part = refs[pos[0]:pos[0] + n]
            pos[0] += n
            return part

        ins = take(n_in)
        c_ins = [take(n) for n in n_ci]
        outs = take(n_out)
        c_outs = [take(n) for n in n_co]
        scr = take(n_scr)
        c_sems = [take(n) for n in n_cs]
        first, last = True, True
        for d, size in enumerate(grid):
            first = jnp.logical_and(first, pl.program_id(d) == 0)
            last = jnp.logical_and(last, pl.program_id(d) == size - 1)

        def run(which):
            for cm, ci, co, cs in zip(comms, c_ins, c_outs, c_sems):
                getattr(cm, which)(ci, co, cs)

        if grid:
            pl.when(first)(lambda: run("start"))
        else:
            run("start")
        body(*ins, *outs, *scr)
        if grid:
            pl.when(last)(lambda: run("finish"))
        else:
            run("finish")

    res = pl.pallas_call(
        fused, name=name, grid=grid, in_specs=list(in_specs) + [HBM_SPEC] * sum(n_ci),
        out_specs=out_specs_l + [HBM_SPEC] * sum(n_co), out_shape=out_shape_l + [s for cm in comms for s in cm.outs],
        input_output_aliases=aliases, scratch_shapes=scratch_shapes + [s for cm in comms for s in cm.sems],
        compiler_params=_params(*["arbitrary"] * len(grid), vmem_limit=vmem_limit),
    )(*args, *[a for cm in comms for a in cm.ins])
    main = res[:n_out]
    extra, at = [], n_out
    for n in n_co:
        extra.append(list(res[at:at + n]))
        at += n
    return (main[0] if single else list(main)), extra


def _mm(a, b, *, name, out_dtype, trans_b=False, tm=512, tn=None, extra=(), epi=None, b_spec=None, n_out=None, comms=()):
    m_total, k_total = a.shape
    if n_out is None:
        n_out = b.shape[0] if trans_b else b.shape[1]
    tn = n_out if tn is None else tn
    grid = (m_total // tm, n_out // tn)
    dims = NT_DIMS if trans_b else None

    def body(*refs):
        a_ref, b_ref = refs[0], refs[1]
        extra_refs = refs[2:2 + len(extra)]
        o_ref = refs[2 + len(extra)]
        acc = _dot(a_ref[...], b_ref[...], dims)
        if epi is not None:
            acc = epi(acc, *[r[...] for r in extra_refs])
        o_ref[...] = acc.astype(out_dtype)

    if b_spec is None:
        if trans_b:
            b_spec = pl.BlockSpec((tn, k_total), lambda i, j: (j, 0))
        else:
            b_spec = pl.BlockSpec((k_total, tn), lambda i, j: (0, j))
    in_specs = [pl.BlockSpec((tm, k_total), lambda i, j: (i, 0)), b_spec]
    in_specs += [pl.BlockSpec((tm, tn), lambda i, j: (i, j)) for _ in extra]
    return _call(
        body, (a, b, *extra), name=name, grid=grid, in_specs=in_specs,
        out_specs=pl.BlockSpec((tm, tn), lambda i, j: (i, j)),
        out_shape=_sds((m_total, n_out), out_dtype),
        sem=("parallel", "parallel"), comms=comms)


def _mm_tn(a, b, *, name, tk, tn, a_fn=None, out_shape=None, out_spec=None):
    m_total, k_total = a.shape
    n_total = b.shape[1]
    grid = (k_total // tk, n_total // tn)

    def body(a_ref, b_ref, o_ref):
        av = a_ref[...]
        part = _dot(av if a_fn is None else a_fn(av), b_ref[...], TN_DIMS)
        o_ref[...] = part.reshape(o_ref.shape)

    if out_shape is None:
        out_shape = _sds((k_total, n_total), F32)
        out_spec = pl.BlockSpec((tk, tn), lambda i, j: (i, j))
    return pl.pallas_call(
        body, name=name, grid=grid,
        in_specs=[pl.BlockSpec((m_total, tk), lambda i, j: (0, i)), pl.BlockSpec((m_total, tn), lambda i, j: (0, j))],
        out_specs=out_spec, out_shape=out_shape,
        compiler_params=_params("parallel", "parallel"),
    )(a, b)


def _ada_fwd(c_all, w_shard, b_shard):
    nb, ncol = c_all.shape[0], w_shard.shape[1]
    tn = 512

    def body(c_ref, w_ref, b_ref, o_ref):
        c = c_ref[...]
        o_ref[...] = _dot(c * _sigmoid(c), w_ref[...]) + b_ref[...]

    return pl.pallas_call(
        body, name="ada_fwd", grid=(ncol // tn,),
        in_specs=[pl.BlockSpec((nb, D_MODEL), lambda j: (0, 0)), pl.BlockSpec((D_MODEL, tn), lambda j: (0, j)),
                  pl.BlockSpec((1, tn), lambda j: (0, j))],
        out_specs=pl.BlockSpec((nb, tn), lambda j: (0, j)), out_shape=_sds((nb, ncol), F32),
        compiler_params=_params("parallel"),
    )(c_all, w_shard, b_shard)


def _adamw_math(g, w, m, v):
    m = ADAM_B1 * m + (1.0 - ADAM_B1) * g
    v = ADAM_B2 * v + (1.0 - ADAM_B2) * (g * g)
    m_hat = m / (1.0 - ADAM_B1 ** ADAM_STEP)
    v_hat = v / (1.0 - ADAM_B2 ** ADAM_STEP)
    delta = -ADAM_LR * (m_hat / (jnp.sqrt(v_hat) + ADAM_EPS) + ADAM_WD * w)
    return delta, m, v


def _ada_bwd_adamw(c_all, dmod_cols, w, m, v):
    nb, ncol = dmod_cols.shape
    tn = 256

    def body(c_ref, d_ref, w_ref, m_ref, v_ref, g_ref, dl_ref, nm_ref, nv_ref):
        c = c_ref[...]
        g = _dot(c * _sigmoid(c), d_ref[...], TN_DIMS)
        g_ref[...] = g
        dl_ref[...], nm_ref[...], nv_ref[...] = _adamw_math(g, w_ref[...], m_ref[...], v_ref[...])

    col = pl.BlockSpec((D_MODEL, tn), lambda j: (0, j))
    shp = _sds((D_MODEL, ncol), F32)
    return pl.pallas_call(
        body, name="ada_bwd_adamw", grid=(ncol // tn,),
        in_specs=[pl.BlockSpec((nb, D_MODEL), lambda j: (0, 0)), pl.BlockSpec((nb, tn), lambda j: (0, j)), col, col, col],
        out_specs=[col, col, col, col], out_shape=[shp, shp, shp, shp],
        compiler_params=_params("parallel"),
    )(c_all, dmod_cols, w, m, v)


def _adamw_halves(own, theirs, core, w, m, v, *, axis, name):
    r2, c2 = own.shape
    tr = _row_tile(r2)
    nt = r2 // tr

    def body(core_ref, own_ref, their_ref, w_ref, m_ref, v_ref, g_ref, dl_ref, nm_ref, nv_ref):
        g = jnp.where(pl.program_id(0) == core_ref[0], own_ref[...], their_ref[...])
        g_ref[...] = g
        dl_ref[...], nm_ref[...], nv_ref[...] = _adamw_math(g, w_ref[...], m_ref[...], v_ref[...])

    if axis == 0:
        full = pl.BlockSpec((tr, c2), lambda h, i, core_ref: (h * nt + i, 0))
    else:
        full = pl.BlockSpec((tr, c2), lambda h, i, core_ref: (i, h))
    half = pl.BlockSpec((tr, c2), lambda h, i, core_ref: (i, 0))
    shp = _sds(w.shape, F32)
    return pl.pallas_call(
        body, name=name,
        grid_spec=pltpu.PrefetchScalarGridSpec(num_scalar_prefetch=1, grid=(2, nt), in_specs=[half, half, full, full, full],
                                               out_specs=[full] * 4),
        out_shape=[shp] * 4, compiler_params=_params("parallel", "parallel"),
    )(core, own, theirs, w, m, v)


def _tok_spec(tm, width=D_MODEL):
    return pl.BlockSpec((None, tm, width), lambda b, i: (b, i, 0))


def _row_spec(width=D_MODEL):
    return pl.BlockSpec((None, 1, width), lambda b, i: (b, 0, 0))


def _vec_spec(width=D_MODEL):
    return pl.BlockSpec((1, width), lambda b, i: (0, 0))


class _RowsOf:
    def __init__(self, ref, first, count):
        self.ref, self.rows = ref, slice(first, first + count)

    def __getitem__(self, idx):
        return self.ref[self.rows, :]

    def __setitem__(self, idx, value):
        self.ref[self.rows, :] = value


def _mm_rows(a, b, *, name, tm, extra, extra_specs, out_specs, out_shape, epi, pro=None, trans_b=False, b_chunks=1, comms=(),
             parts=1, zero_per_seq=(), zero_once=(), vmem_limit=None):
    bsz, seq, k_total = a.shape
    kc = k_total // b_chunks
    dims = NT_DIMS if trans_b else None
    rows = tm // parts

    def body(*refs):
        a_ref, b_ref = refs[0], refs[1]
        ex, outs = refs[2:2 + len(extra)], refs[2 + len(extra):]
        if zero_per_seq:
            @pl.when(pl.program_id(1) == 0)
            def _():
                for k in zero_per_seq:
                    outs[k][...] = jnp.zeros_like(outs[k])
        if zero_once:
            @pl.when(jnp.logical_and(pl.program_id(0) == 0, pl.program_id(1) == 0))
            def _():
                for k in zero_once:
                    outs[k][...] = jnp.zeros_like(outs[k])

        def part_of(ref, p):
            tiled = len(ref.shape) == 2 and ref.shape[0] == tm
            return _RowsOf(ref, p * rows, rows) if tiled and parts > 1 else ref

        accs = []
        for p in range(parts):
            a_p, ex_p, outs_p = part_of(a_ref, p), [part_of(r, p) for r in ex], [part_of(r, p) for r in outs]
            if b_chunks == 1:
                accs.append(_dot(a_p[...] if pro is None else pro(a_p, ex_p, outs_p), b_ref[...], dims))
            else:
                acc = _dot(a_p[...][:, 0:kc], b_ref[0], NT_DIMS)
                for k in range(1, b_chunks):
                    acc = acc + _dot(a_p[...][:, k * kc:(k + 1) * kc], b_ref[k], NT_DIMS)
                accs.append(acc)
        for p in range(parts):
            epi(accs[p], [part_of(r, p) for r in ex], [part_of(r, p) for r in outs])

    b_spec = pl.BlockSpec(b.shape, lambda bb, i: (0,) * b.ndim)
    return _call(
        body, (a, b, *extra), name=name, grid=(bsz, seq // tm), in_specs=[_tok_spec(tm, k_total), b_spec, *extra_specs],
        out_specs=out_specs, out_shape=out_shape, sem=("arbitrary", "arbitrary"), comms=comms, vmem_limit=vmem_limit)


def _in_proj_fused(x, w, sc, sh, w_in_t, tables, comms=()):
    tm = 512
    bsz, seq, _ = x.shape
    half = ROPE_DIM // 2
    heads_per_slab = LANE // ATT_HEAD_DIM

    def pro(x_ref, ex, outs):
        y, _, _ = _rms_fwd(x_ref[...], ex[0][...])
        h = (y * (1.0 + ex[1][...]) + ex[2][...]).astype(BF16)
        outs[0][...] = h
        return h

    def epi(acc, ex, outs):
        c, u, d = ex[3][...], ex[4][...], ex[5][...]
        _, proj_ref, q_ref, k_ref, v_ref = outs
        proj_ref[...] = acc

        def rope(z):
            return (z * c + pltpu.roll(z, half, 1) * u + pltpu.roll(z, LANE - half, 1) * d).astype(BF16)

        for s in range(ATT_WIDTH // LANE):
            slab = rope(acc[:, s * LANE:(s + 1) * LANE])
            for part in range(heads_per_slab):
                g, hh = divmod(s * heads_per_slab + part, ATT_GROUP)
                piece = slab[:, part * ATT_HEAD_DIM:(part + 1) * ATT_HEAD_DIM]
                for blk in range(tm // WINDOW):
                    q_ref[blk, g, hh * WINDOW:(hh + 1) * WINDOW, :] = piece[blk * WINDOW:(blk + 1) * WINDOW]
        rk = rope(acc[:, ATT_WIDTH:ATT_WIDTH + LANE])
        vv = acc[:, ATT_WIDTH + LANE:ATT_COLS].astype(BF16)
        for g in range(ATT_KV_HEADS):
            k_ref[g] = rk[:, g * ATT_HEAD_DIM:(g + 1) * ATT_HEAD_DIM]
            v_ref[g] = vv[:, g * ATT_HEAD_DIM:(g + 1) * ATT_HEAD_DIM]

    cols = w_in_t.shape[0]
    tab = pl.BlockSpec((tm, LANE), lambda b, i: (i, 0))
    kv_spec = pl.BlockSpec((None, ATT_KV_HEADS, tm, ATT_HEAD_DIM), lambda b, i: (b, 0, i, 0))
    kv_shape = _sds((bsz, ATT_KV_HEADS, seq, ATT_HEAD_DIM), BF16)
    q_spec = pl.BlockSpec((None, tm // WINDOW, ATT_KV_HEADS, GROUP_ROWS, ATT_HEAD_DIM), lambda b, i: (b, i, 0, 0, 0))
    return _mm_rows(x, w_in_t, name="in_proj", tm=tm, extra=(w, sc, sh, *tables),
                    extra_specs=[_vec_spec(), _row_spec(), _row_spec(), tab, tab, tab],
                    out_specs=[_tok_spec(tm), _tok_spec(tm, cols), q_spec, kv_spec, kv_spec],
                    out_shape=[_sds(x.shape, BF16), _sds((bsz, seq, cols), F32),
                               _sds((bsz, seq // WINDOW, ATT_KV_HEADS, GROUP_ROWS, ATT_HEAD_DIM), BF16), kv_shape, kv_shape],
                    pro=pro, epi=epi, trans_b=True, comms=comms)


def _rope_tables(seq):
    half = ROPE_DIM // 2
    inv_freq = ROPE_THETA ** (-jnp.arange(0, ROPE_DIM, 2, dtype=F32) / ROPE_DIM)
    ang = jnp.arange(seq, dtype=F32)[:, None] * inv_freq[None, :]
    cos, sin = jnp.cos(ang), jnp.sin(ang)
    rest = ATT_HEAD_DIM - ROPE_DIM
    ones, zeros, zh = jnp.ones((seq, rest), F32), jnp.zeros((seq, rest), F32), jnp.zeros((seq, half), F32)
    reps = LANE // ATT_HEAD_DIM
    t_cos = jnp.tile(jnp.concatenate([cos, cos, ones], axis=1), (1, reps))
    t_up = jnp.tile(jnp.concatenate([zh, sin, zeros], axis=1), (1, reps))
    t_dn = jnp.tile(jnp.concatenate([-sin, zh, zeros], axis=1), (1, reps))
    return t_cos, t_up, t_dn


GROUP_ROWS = ATT_GROUP * WINDOW


ATT_BPS = 2


MASKED = -1e30


def _band_biases():
    row = jnp.arange(GROUP_ROWS)[:, None] % WINDOW
    col = jnp.arange(2 * WINDOW)[None, :]
    own = jnp.logical_and(col >= WINDOW, col - WINDOW <= row)
    before = jnp.logical_and(col < WINDOW, col > row)
    return (jnp.where(jnp.logical_or(own, before), 0.0, MASKED).astype(F32), jnp.where(own, 0.0, MASKED).astype(F32))


def _band_bias(full_ref, first_ref, has_prev):
    return full_ref[...] if has_prev is True else jnp.where(has_prev, full_ref[...], first_ref[...])


def _sink_column(sink_ref, g):
    head = lax.broadcasted_iota(jnp.int32, (GROUP_ROWS, 1), 0) // WINDOW
    col = jnp.full((GROUP_ROWS, 1), sink_ref[0, g * ATT_GROUP], F32)
    for hh in range(1, ATT_GROUP):
        col = jnp.where(head == hh, sink_ref[0, g * ATT_GROUP + hh], col)
    return col


def _bias_spec():
    return pl.BlockSpec((GROUP_ROWS, 2 * WINDOW), lambda b, i: (0, 0))


def _attn_specs():
    q_spec = pl.BlockSpec((None, ATT_BPS, ATT_KV_HEADS, GROUP_ROWS, ATT_HEAD_DIM), lambda b, i: (b, i, 0, 0, 0))
    kv_cur = pl.BlockSpec((None, ATT_KV_HEADS, ATT_BPS * WINDOW, ATT_HEAD_DIM), lambda b, i: (b, 0, i, 0))
    kv_prev = pl.BlockSpec((None, ATT_KV_HEADS, WINDOW, ATT_HEAD_DIM), lambda b, i: (b, 0, jnp.maximum(ATT_BPS * i - 1, 0), 0))
    return q_spec, kv_cur, kv_prev


def _band(prev_ref, cur_ref, g, blk):
    own = cur_ref[g, blk * WINDOW:(blk + 1) * WINDOW]
    before = prev_ref[g] if blk == 0 else cur_ref[g, (blk - 1) * WINDOW:blk * WINDOW]
    return jnp.concatenate([before, own], axis=0)


def _attn_fwd(qh, kh, vh, sinks, w_norm, biases, comms=()):
    bsz, nblk = qh.shape[0], qh.shape[1]
    seq = nblk * WINDOW
    rows = ATT_BPS * WINDOW

    def body(sink_ref, q_ref, kc_ref, kp_ref, vc_ref, vp_ref, w_ref, full_ref, first_ref, raw_ref, an_ref, l_ref):
        for blk in range(ATT_BPS):
            bias = _band_bias(full_ref, first_ref, True if blk else pl.program_id(1) > 0)
            for g in range(ATT_KV_HEADS):
                keys, vals = _band(kp_ref, kc_ref, g, blk), _band(vp_ref, vc_ref, g, blk)
                sink = _sink_column(sink_ref, g)
                s = _dot(q_ref[blk, g], keys, NT_DIMS) * ATT_SCALE + bias
                m = jnp.maximum(jnp.max(s, axis=-1, keepdims=True), sink)
                p = jnp.exp(s - m)
                den = jnp.sum(p, axis=-1, keepdims=True) + jnp.exp(sink - m)
                o = _dot(p / den, vals)
                lse = m + jnp.log(den)
                tok = slice(blk * WINDOW, (blk + 1) * WINDOW)
                for hh in range(ATT_GROUP):
                    h = g * ATT_GROUP + hh
                    raw_ref[tok, h * ATT_HEAD_DIM:(h + 1) * ATT_HEAD_DIM] = o[hh * WINDOW:(hh + 1) * WINDOW]
                    l_ref[tok, h:h + 1] = lse[hh * WINDOW:(hh + 1) * WINDOW]
        y, _, _ = _rms_fwd(raw_ref[...], w_ref[...])
        an_ref[...] = y.astype(BF16)

    cur = lambda width: pl.BlockSpec((None, rows, width), lambda b, i: (b, i, 0))
    q_spec, kv_cur, kv_prev = _attn_specs()
    return _call(
        body, (sinks, qh, kh, kh, vh, vh, w_norm, *biases), name="attn_fwd", grid=(bsz, nblk // ATT_BPS),
        in_specs=[pl.BlockSpec(memory_space=pltpu.SMEM), q_spec, kv_cur, kv_prev, kv_cur, kv_prev, _vec_spec(ATT_WIDTH),
                  _bias_spec(), _bias_spec()],
        out_specs=[cur(ATT_WIDTH), cur(ATT_WIDTH), cur(ATT_Q_HEADS)],
        out_shape=[_sds((bsz, seq, ATT_WIDTH), F32), _sds((bsz, seq, MIX_WIDTH), BF16), _sds((bsz, seq, ATT_Q_HEADS), F32)],
        sem=("parallel", "parallel"), comms=comms)


HG_Q0 = ATT_COLS // LANE
HG_F0 = HG_Q0 + HG_HEADS
HG_I0 = HG_F0 + HG_HEADS
HG_G0 = HG_I0 + HG_HEADS
HG_TOK = 256
HG_NCH = HG_TOK // HG_CHUNK
HG_HPS = 2


def _block_masks():
    row = jnp.arange(HG_TOK)[:, None]
    col = jnp.arange(HG_TOK)[None, :]
    same = (row // HG_CHUNK) == (col // HG_CHUNK)
    return jnp.logical_and(same, col <= row).astype(F32), jnp.logical_and(same, col >= row).astype(F32)


def _row_in_chunk():
    return lax.broadcasted_iota(jnp.int32, (HG_TOK, LANE), 0) % HG_CHUNK


def _chunk_cumsum(x, reverse=False):
    ric = _row_in_chunk()
    shift = 1
    while shift < HG_CHUNK:
        if reverse:
            x = x + jnp.where(ric < HG_CHUNK - shift, pltpu.roll(x, HG_TOK - shift, 0), 0.0)
        else:
            x = x + jnp.where(ric >= shift, pltpu.roll(x, shift, 0), 0.0)
        shift *= 2
    return x


def _chunk_rows(rows):
    stacked = jnp.concatenate([r[None] for r in rows], axis=0)
    return jnp.broadcast_to(stacked, (HG_NCH, HG_CHUNK, LANE)).reshape(HG_TOK, LANE)


def _chunk_slices(x):
    return [x[j * HG_CHUNK:(j + 1) * HG_CHUNK] for j in range(HG_NCH)]


def _hgrn_common(tbl, hf, hq):
    lb = _sigmoid(tbl[1:2] - tbl[0:1])
    sig = _sigmoid(hf)
    f = lb + (1.0 - lb) * sig
    sq = _sigmoid(hq)
    q, k = hq * sq, 1.0 - f
    b = _chunk_cumsum(jnp.log(f))
    last = [b[(j + 1) * HG_CHUNK - 1:(j + 1) * HG_CHUNK] for j in range(HG_NCH)]
    bl = _chunk_rows(last)
    e_b, e_nb, e_rem = jnp.exp(b), jnp.exp(-b), jnp.exp(bl - b)
    e_last = [jnp.exp(r) for r in last]
    return dict(lb=lb, sig=sig, f=f, sq=sq, q=q, k=k, e_b=e_b, e_nb=e_nb, e_rem=e_rem, e_last=e_last,
                qd=q * e_b, kd=k * e_nb, ku=k * e_rem)


def _hgrn_fwd(proj, lb_table, norm_w, mix_in, masks, comms=()):
    bsz, seq, _ = proj.shape
    nstep = seq // HG_TOK

    def body(tbl_ref, nw_ref, q_ref, f_ref, i_ref, g_ref, mix_ref, lower_ref, o_ref, rec_ref, st_ref, s_scr):
        @pl.when(pl.program_id(2) == 0)
        def _():
            s_scr[...] = jnp.zeros_like(s_scr)

        lower = lower_ref[...]
        for hp in range(HG_HPS):
            ls = slice(hp * LANE, (hp + 1) * LANE)
            v, hg = i_ref[:, ls], g_ref[:, ls]
            t = _hgrn_common(tbl_ref[:, ls], f_ref[:, ls], q_ref[:, ls])
            a = _dot(t["qd"], t["kd"], NT_DIMS) * lower
            o_intra = _dot(a, v)
            v_c, ku_c, qd_c = [_chunk_slices(z.astype(BF16)) for z in (v, t["ku"], t["qd"])]
            updates = [_dot(v_c[j], ku_c[j], TN_DIMS) for j in range(HG_NCH)]
            st = s_scr[hp]
            states = []
            for j in range(HG_NCH):
                states.append(st)
                st = st * t["e_last"][j] + updates[j]
            s_scr[hp] = st
            o = o_intra + jnp.concatenate([_dot(qd_c[j], states[j], NT_DIMS) for j in range(HG_NCH)], axis=0)
            st_ref[hp, 0] = states[0]
            o_ref[:, ls] = o
            y, _, _ = _rms_fwd(o, nw_ref[...])
            rec_ref[:, ls] = (y * (hg * _sigmoid(hg))).astype(BF16)

    width = HG_HPS * LANE
    slab = lambda first: pl.BlockSpec((None, HG_TOK, width), lambda b, h, t: (b, t, first // HG_HPS + h))
    head_out = pl.BlockSpec((None, HG_TOK, width), lambda b, h, t: (b, t, h))
    mix_out = pl.BlockSpec((None, HG_TOK, width), lambda b, h, t: (b, t, ATT_WIDTH // width + h))
    return _call(
        body, (lb_table, norm_w, proj, proj, proj, proj, mix_in, masks[0]), name="hgrn_fwd", grid=(bsz, HG_HEADS // HG_HPS, nstep),
        in_specs=[pl.BlockSpec((2, width), lambda b, h, t: (0, h)), pl.BlockSpec((1, LANE), lambda b, h, t: (0, 0)),
                  slab(HG_Q0), slab(HG_F0), slab(HG_I0), slab(HG_G0), pl.BlockSpec(memory_space=pl.ANY),
                  pl.BlockSpec((HG_TOK, HG_TOK), lambda b, h, t: (0, 0))],
        out_specs=[head_out, mix_out,
                   pl.BlockSpec((None, HG_HPS, 1, LANE, LANE), lambda b, h, t: (b, h, t, 0, 0))],
        out_shape=[_sds((bsz, seq, HG_WIDTH), F32), _sds(mix_in.shape, BF16),
                   _sds((bsz, HG_HEADS, nstep, LANE, LANE), F32)],
        scratch_shapes=[pltpu.VMEM((HG_HPS, LANE, LANE), F32)],
        sem=("parallel", "parallel", "arbitrary"), comms=comms, aliases={6: 1})


def _out_proj_fused(cat, w_out, x, post_w, g1, pre_w, sc2, sh2):
    tm = 512

    def epi(mix, ex, outs):
        x_ref, pw_ref, g1_ref, w2_ref, sc_ref, sh_ref = ex
        outs[0][...] = mix
        n1, _, _ = _rms_fwd(mix, pw_ref[...])
        x1 = x_ref[...] + g1_ref[...] * n1
        outs[1][...] = x1
        y2, _, _ = _rms_fwd(x1, w2_ref[...])
        outs[2][...] = (y2 * (1.0 + sc_ref[...]) + sh_ref[...]).astype(BF16)

    return _mm_rows(cat, w_out, name="out_proj", tm=tm, extra=(x, post_w, g1, pre_w, sc2, sh2),
                    extra_specs=[_tok_spec(tm), _vec_spec(), _row_spec(), _vec_spec(), _row_spec(), _row_spec()],
                    out_specs=[_tok_spec(tm), _tok_spec(tm), _tok_spec(tm)],
                    out_shape=[_sds(x.shape, F32), _sds(x.shape, F32), _sds(x.shape, BF16)], epi=epi)


def _acc_out(ref, first, value):
    @pl.when(first)
    def _():
        ref[...] = value

    @pl.when(jnp.logical_not(first))
    def _():
        ref[...] += value


def _down_proj_fused(r, w_down, x1, post_w, g2, target):
    tm = 512
    bsz = x1.shape[0]

    def pro(r_ref, ex, outs):
        rv = r_ref[...]
        return rv * rv

    def epi(down, ex, outs):
        x1_ref, w_ref, g2_ref, t_ref = ex
        loss_ref, dy_ref, dd_ref, dg2_ref, dw_ref = outs
        w, g2v = w_ref[...], g2_ref[...]
        n2, dh, rstd = _rms_fwd(down, w)
        err = x1_ref[...] + g2v * n2 - t_ref[...]
        part = (0.5 / D_MODEL) * jnp.sum(jnp.sum(err * err, axis=-1, keepdims=True), axis=0, keepdims=True)
        loss_ref[...] += jnp.broadcast_to(part, (1, LANE))
        dy = err * (1.0 / D_MODEL)
        dy_ref[...] = dy
        dg2_ref[...] += _colsum(dy * n2)
        dd, dw_rows = _rms_bwd(dy * g2v, dh, rstd, w)
        dd_ref[...] = dd.astype(BF16)
        dw_ref[...] += _colsum(dw_rows)

    return _mm_rows(r, w_down, name="down_proj", tm=tm, extra=(x1, post_w, g2, target),
                    extra_specs=[_tok_spec(tm), _vec_spec(), _row_spec(), _tok_spec(tm)],
                    out_specs=[_vec_spec(LANE), _tok_spec(tm), _tok_spec(tm), _row_spec(), _vec_spec()],
                    out_shape=[_sds((1, LANE), F32), _sds(x1.shape, F32), _sds(x1.shape, BF16), _sds((bsz, 1, D_MODEL), F32),
                               _sds((1, D_MODEL), F32)], pro=pro, epi=epi, parts=2, zero_per_seq=(3,), zero_once=(0, 4),
                    vmem_limit=VMEM_LIMIT_BIG)


def _up_bwd_fused(dpre, w_up4, dy, x1, mix, pre_w, sc2, post_w, g1, comms=()):
    tm = 512
    bsz = x1.shape[0]

    def epi(dh2v, ex, outs):
        dy_ref, x1_ref, mix_ref, w2_ref, sc_ref, pw_ref, g1_ref = ex
        dx1_ref, dmix_ref, dsc_ref, dsh_ref, dg1_ref, dw2_ref, dpw_ref = outs
        w2, pw = w2_ref[...], pw_ref[...]
        y2, xh2, rstd2 = _rms_fwd(x1_ref[...], w2)
        dsh_ref[...] += _colsum(dh2v)
        dsc_ref[...] += _colsum(dh2v * y2)
        dx1n, dw_rows = _rms_bwd(dh2v * (1.0 + sc_ref[...]), xh2, rstd2, w2)
        dw2_ref[...] += _colsum(dw_rows)
        dx1 = dy_ref[...] + dx1n
        dx1_ref[...] = dx1
        n1, mh, rstd1 = _rms_fwd(mix_ref[...], pw)
        dg1_ref[...] += _colsum(dx1 * n1)
        dmix, dpw_rows = _rms_bwd(dx1 * g1_ref[...], mh, rstd1, pw)
        dmix_ref[...] = dmix.astype(BF16)
        dpw_ref[...] += _colsum(dpw_rows)

    row_shape = _sds((bsz, 1, D_MODEL), F32)
    vec_shape = _sds((1, D_MODEL), F32)
    return _mm_rows(dpre, w_up4, name="up_bwd", tm=tm, extra=(dy, x1, mix, pre_w, sc2, post_w, g1),
                    extra_specs=[_tok_spec(tm), _tok_spec(tm), _tok_spec(tm), _vec_spec(), _row_spec(), _vec_spec(), _row_spec()],
                    out_specs=[_tok_spec(tm), _tok_spec(tm), _row_spec(), _row_spec(), _row_spec(), _vec_spec(), _vec_spec()],
                    out_shape=[_sds(x1.shape, F32), _sds(x1.shape, BF16), row_shape, row_shape, row_shape, vec_shape, vec_shape],
                    epi=epi, b_chunks=w_up4.shape[0], comms=comms, parts=2, zero_per_seq=(2, 3, 4), zero_once=(5, 6),
                    vmem_limit=VMEM_LIMIT_BIG)


def _norm1_bwd(dh1, dx1, x, pre_w, sc1, tm=512, comms=()):
    bsz, seq, _ = x.shape

    def body(dh_ref, dx1_ref, x_ref, w_ref, sc_ref, gx_ref, dsc_ref, dsh_ref, dw_ref):
        b, i = pl.program_id(0), pl.program_id(1)
        w = w_ref[...]
        dh = dh_ref[...]
        y, xh, rstd = _rms_fwd(x_ref[...], w)
        _acc_out(dsh_ref, i == 0, _colsum(dh))
        _acc_out(dsc_ref, i == 0, _colsum(dh * y))
        dx, dw_rows = _rms_bwd(dh * (1.0 + sc_ref[...]), xh, rstd, w)
        _acc_out(dw_ref, jnp.logical_and(b == 0, i == 0), _colsum(dw_rows))
        gx_ref[...] = dx1_ref[...] + dx

    row_shape = _sds((bsz, 1, D_MODEL), F32)
    return _call(
        body, (dh1, dx1, x, pre_w, sc1), name="norm1_bwd", grid=(bsz, seq // tm),
        in_specs=[_tok_spec(tm), _tok_spec(tm), _tok_spec(tm), _vec_spec(), _row_spec()],
        out_specs=[_tok_spec(tm), _row_spec(), _row_spec(), _vec_spec()],
        out_shape=[_sds(x.shape, F32), row_shape, row_shape, _sds((1, D_MODEL), F32)],
        sem=("arbitrary", "arbitrary"), comms=comms)


def _hgrn_bwd(dcat, proj, o_raw, states, lb_table, norm_w, masks, comms=()):
    bsz, seq, _ = proj.shape
    nstep = seq // HG_TOK
    rec0 = ATT_WIDTH // LANE

    def body(tbl_ref, nw_ref, dr_ref, q_ref, f_ref, i_ref, g_ref, o_ref, st_ref, lower_ref, upper_ref,
             dq_ref, df_ref, di_ref, dg_ref, dlb_ref, dnw_ref, ds_scr):
        h, b, t = pl.program_id(0), pl.program_id(1), pl.program_id(2)

        @pl.when(t == 0)
        def _():
            ds_scr[...] = jnp.zeros_like(ds_scr)

        lower, upper = lower_ref[...], upper_ref[...]
        dlb_parts = []
        dnw_acc = jnp.zeros((1, LANE), F32)
        for hp in range(HG_HPS):
            ls = slice(hp * LANE, (hp + 1) * LANE)
            hq, v, hg = q_ref[:, ls], i_ref[:, ls], g_ref[:, ls]
            nw = nw_ref[...]
            c = _hgrn_common(tbl_ref[:, ls], f_ref[:, ls], hq)
            qd, kd, ku = c["qd"], c["kd"], c["ku"]
            y, on, rstd = _rms_fwd(o_ref[:, ls], nw)
            sg = _sigmoid(hg)
            dr = dr_ref[:, ls]
            dg_ref[:, ls] = (dr * y * (sg * (1.0 + hg * (1.0 - sg)))).astype(BF16)
            do, dnw_rows = _rms_bwd(dr * (hg * sg), on, rstd, nw)
            at = _dot(kd, qd, NT_DIMS) * upper
            da = _dot(do, v, NT_DIMS) * lower
            dat = _dot(v, do, NT_DIMS) * upper
            dv = _dot(at, do)
            dqd = _dot(da, kd)
            dkd = _dot(dat, qd)
            do_c, qd_c, v_c, ku_c = [_chunk_slices(z.astype(BF16)) for z in (do, qd, v, ku)]
            outer = [_dot(do_c[j], qd_c[j], TN_DIMS) for j in range(HG_NCH)]
            ds = ds_scr[hp]
            ds_after = [None] * HG_NCH
            for j in reversed(range(HG_NCH)):
                ds_after[j] = ds
                ds = outer[j] + ds * c["e_last"][j]
            ds_scr[hp] = ds
            updates = [_dot(v_c[j], ku_c[j], TN_DIMS) for j in range(HG_NCH)]
            states = [st_ref[hp, 0]]
            for j in range(HG_NCH - 1):
                states.append(states[j] * c["e_last"][j] + updates[j])
            dv = dv + jnp.concatenate([_dot(ku_c[j], ds_after[j], NT_DIMS) for j in range(HG_NCH)], axis=0)
            dqd = dqd + jnp.concatenate([_dot(do_c[j], states[j]) for j in range(HG_NCH)], axis=0)
            dku = jnp.concatenate([_dot(v_c[j], ds_after[j]) for j in range(HG_NCH)], axis=0)
            dku_ku = dku * ku
            dbl = [_colsum(states[j] * ds_after[j]) * c["e_last"][j] + _colsum(dku_ku[j * HG_CHUNK:(j + 1) * HG_CHUNK])
                   for j in range(HG_NCH)]
            dk = dkd * c["e_nb"] + dku * c["e_rem"]
            db = dqd * qd - dkd * kd - dku_ku + jnp.where(_row_in_chunk() == HG_CHUNK - 1, _chunk_rows(dbl), 0.0)
            dfv = _chunk_cumsum(db, reverse=True) / c["f"] - dk
            sig, sq = c["sig"], c["sq"]
            df_ref[:, ls] = (dfv * (1.0 - c["lb"]) * sig * (1.0 - sig)).astype(BF16)
            dq_ref[:, ls] = (dqd * c["e_b"] * (sq * (1.0 + hq * (1.0 - sq)))).astype(BF16)
            di_ref[:, ls] = dv.astype(BF16)
            dlb_parts.append(_colsum(dfv * (1.0 - sig)))
            dnw_acc = dnw_acc + _colsum(dnw_rows)
        _acc_out(dlb_ref, jnp.logical_and(b == 0, t == 0), jnp.concatenate(dlb_parts, axis=1))
        _acc_out(dnw_ref, jnp.logical_and(h == 0, jnp.logical_and(b == 0, t == 0)), dnw_acc)

    rev = lambda t: nstep - 1 - t
    width = HG_HPS * LANE
    slab = lambda first: pl.BlockSpec((None, HG_TOK, width), lambda h, b, t: (b, rev(t), first // HG_HPS + h))
    head = pl.BlockSpec((None, HG_TOK, width), lambda h, b, t: (b, rev(t), h))
    grad_shape = _sds((bsz, seq, HG_WIDTH), BF16)
    return _call(
        body, (lb_table, norm_w, dcat, proj, proj, proj, proj, o_raw, states, *masks), name="hgrn_bwd",
        grid=(HG_HEADS // HG_HPS, bsz, nstep),
        in_specs=[pl.BlockSpec((2, width), lambda h, b, t: (0, h)), pl.BlockSpec((1, LANE), lambda h, b, t: (0, 0)),
                  slab(rec0), slab(HG_Q0), slab(HG_F0), slab(HG_I0), slab(HG_G0), head,
                  pl.BlockSpec((None, HG_HPS, 1, LANE, LANE), lambda h, b, t: (b, h, rev(t), 0, 0)),
                  pl.BlockSpec((HG_TOK, HG_TOK), lambda h, b, t: (0, 0)), pl.BlockSpec((HG_TOK, HG_TOK), lambda h, b, t: (0, 0))],
        out_specs=[head, head, head, head, pl.BlockSpec((1, width), lambda h, b, t: (0, h)),
                   pl.BlockSpec((1, LANE), lambda h, b, t: (0, 0))],
        out_shape=[grad_shape, grad_shape, grad_shape, grad_shape, _sds((1, HG_WIDTH), F32), _sds((1, LANE), F32)],
        scratch_shapes=[pltpu.VMEM((HG_HPS, LANE, LANE), F32)],
        sem=("arbitrary", "arbitrary", "arbitrary"), comms=comms)


def _attn_bwd(dcat, raw, w_norm, qh, kh, vh, lse, sinks, tables, biases, comms=()):
    bsz, nblk = qh.shape[0], qh.shape[1]
    seq = nblk * WINDOW
    nstep = nblk // ATT_BPS
    half = ROPE_DIM // 2

    def body(sink_ref, da_ref, raw_ref, w_ref, q_ref, kc_ref, kp_ref, vc_ref, vp_ref, l_ref, c_ref, u_ref, d_ref,
             full_ref, first_ref, o_ref, dw_ref, dsink_ref, carry_k, carry_v):
        b, i = pl.program_id(0), pl.program_id(1)
        first = jnp.logical_and(b == 0, i == 0)

        @pl.when(i == 0)
        def _():
            carry_k[...] = jnp.zeros_like(carry_k)
            carry_v[...] = jnp.zeros_like(carry_v)

        w = w_ref[...]
        _, on, rstd = _rms_fwd(raw_ref[...], w)
        do_step, dw_rows = _rms_bwd(da_ref[...], on, rstd, w)
        _acc_out(dw_ref, first, _colsum(dw_rows))
        lane8 = lax.broadcasted_iota(jnp.int32, (1, ATT_Q_HEADS), 1)
        dsink = jnp.zeros((1, ATT_Q_HEADS), F32)
        from_next_k, from_next_v = carry_k[...], carry_v[...]
        for blk in reversed(range(ATT_BPS)):
            tok = slice(blk * WINDOW, (blk + 1) * WINDOW)
            bias = _band_bias(full_ref, first_ref, True if blk else i < nstep - 1)
            raw_v, do_all = raw_ref[tok, :], do_step[tok]
            c, u, d = c_ref[tok, :], u_ref[tok, :], d_ref[tok, :]

            def unrope(g):
                return (g * c + pltpu.roll(g * u, LANE - half, 1) + pltpu.roll(g * d, half, 1)).astype(BF16)

            dq_parts, dk_own, dk_before, dv_own, dv_before = [], [], [], [], []
            for g in range(ATT_KV_HEADS):
                heads = [slice((g * ATT_GROUP + hh) * ATT_HEAD_DIM, (g * ATT_GROUP + hh + 1) * ATT_HEAD_DIM)
                         for hh in range(ATT_GROUP)]
                q = q_ref[blk, g]
                keys, vals = _band(kp_ref, kc_ref, g, blk), _band(vp_ref, vc_ref, g, blk)
                do_g = jnp.concatenate([do_all[:, hs] for hs in heads], axis=0)
                dsum = jnp.concatenate([jnp.sum(do_all[:, hs] * raw_v[:, hs], axis=-1, keepdims=True) for hs in heads], axis=0)
                lse_g = jnp.concatenate([l_ref[tok, g * ATT_GROUP + hh:g * ATT_GROUP + hh + 1] for hh in range(ATT_GROUP)], axis=0)
                p = jnp.exp(_dot(q, keys, NT_DIMS) * ATT_SCALE + bias - lse_g)
                sink_part = jnp.exp(_sink_column(sink_ref, g) - lse_g) * dsum
                for hh in range(ATT_GROUP):
                    head_sum = jnp.sum(sink_part[hh * WINDOW:(hh + 1) * WINDOW], axis=0, keepdims=True)
                    dsink = dsink - jnp.where(lane8 == g * ATT_GROUP + hh, head_sum, 0.0)
                ds = p * (_dot(do_g, vals, NT_DIMS) - dsum) * ATT_SCALE
                dq_g = _dot(ds, keys)
                dq_parts += [dq_g[hh * WINDOW:(hh + 1) * WINDOW] for hh in range(ATT_GROUP)]
                dk_g = _dot(ds, q, TN_DIMS)
                dv_g = _dot(p, do_g, TN_DIMS)
                dk_before.append(dk_g[:WINDOW])
                dk_own.append(dk_g[WINDOW:])
                dv_before.append(dv_g[:WINDOW])
                dv_own.append(dv_g[WINDOW:])
            per_slab = LANE // ATT_HEAD_DIM
            for s in range(ATT_WIDTH // LANE):
                slab = jnp.concatenate(dq_parts[s * per_slab:(s + 1) * per_slab], axis=1)
                o_ref[tok, s * LANE:(s + 1) * LANE] = unrope(slab)
            o_ref[tok, ATT_WIDTH:ATT_WIDTH + LANE] = unrope(jnp.concatenate(dk_own, axis=1) + from_next_k)
            o_ref[tok, ATT_WIDTH + LANE:ATT_COLS] = (jnp.concatenate(dv_own, axis=1) + from_next_v).astype(BF16)
            from_next_k, from_next_v = jnp.concatenate(dk_before, axis=1), jnp.concatenate(dv_before, axis=1)
        carry_k[...] = from_next_k
        carry_v[...] = from_next_v
        _acc_out(dsink_ref, first, dsink)

    rows = ATT_BPS * WINDOW
    rev = lambda i: nstep - 1 - i
    cur = lambda width: pl.BlockSpec((None, rows, width), lambda b, i: (b, rev(i), 0))
    q_spec = pl.BlockSpec((None, ATT_BPS, ATT_KV_HEADS, GROUP_ROWS, ATT_HEAD_DIM), lambda b, i: (b, rev(i), 0, 0, 0))
    kv_cur = pl.BlockSpec((None, ATT_KV_HEADS, rows, ATT_HEAD_DIM), lambda b, i: (b, 0, rev(i), 0))
    kv_prev = pl.BlockSpec((None, ATT_KV_HEADS, WINDOW, ATT_HEAD_DIM), lambda b, i: (b, 0, jnp.maximum(ATT_BPS * rev(i) - 1, 0), 0))
    tab = pl.BlockSpec((rows, LANE), lambda b, i: (rev(i), 0))
    return _call(
        body, (sinks, dcat, raw, w_norm, qh, kh, kh, vh, vh, lse, *tables, *biases), name="attn_bwd", grid=(bsz, nstep),
        in_specs=[pl.BlockSpec(memory_space=pltpu.SMEM), cur(ATT_WIDTH), cur(ATT_WIDTH), _vec_spec(ATT_WIDTH), q_spec,
                  kv_cur, kv_prev, kv_cur, kv_prev, cur(ATT_Q_HEADS), tab, tab, tab, _bias_spec(), _bias_spec()],
        out_specs=[cur(ATT_COLS), _vec_spec(ATT_WIDTH), _vec_spec(ATT_Q_HEADS)],
        out_shape=[_sds((bsz, seq, ATT_COLS), BF16), _sds((1, ATT_WIDTH), F32), _sds((1, ATT_Q_HEADS), F32)],
        scratch_shapes=[pltpu.VMEM((WINDOW, LANE), F32), pltpu.VMEM((WINDOW, LANE), F32)],
        sem=("arbitrary", "arbitrary"), comms=comms)


def _other_chips(x, y):
    return [(1 - x, y), (x, 1 - y), (1 - x, 1 - y)]


def _sem_pair(n):
    return [pltpu.SemaphoreType.DMA((n,)), pltpu.SemaphoreType.DMA((n,))]


def _plan_pair_forward(bufs):
    n = len(bufs)

    def copies(outs, sems):
        x, y, c = _mesh_pos()
        sends, lands = [], []
        for a in range(n):
            for j, chip in enumerate(_other_chips(x, y)):
                k = 3 * a + j
                slot = outs[a].at[4 * chip[0] + 2 * chip[1] + c]
                sends.append(pltpu.make_async_remote_copy(
                    src_ref=slot, dst_ref=slot, send_sem=sems[0].at[k], recv_sem=sems[1].at[k],
                    device_id=(x, y, 1 - c), device_id_type=MESH))
                theirs = outs[a].at[4 * chip[0] + 2 * chip[1] + 1 - c]
                lands.append(pltpu.make_async_remote_copy(
                    src_ref=theirs, dst_ref=theirs, send_sem=sems[0].at[k], recv_sem=sems[1].at[k],
                    device_id=(x, y, 1 - c), device_id_type=MESH))
        return sends, lands

    def start(ins, outs, sems):
        for cp in copies(outs, sems)[0]:
            cp.start()

    def finish(ins, outs, sems):
        sends, lands = copies(outs, sems)
        for cp in lands:
            cp.wait_recv()
        for cp in sends:
            cp.wait_send()

    return _Comm(list(bufs), [_sds(b.shape, b.dtype) for b in bufs], _sem_pair(3 * n), start, finish,
                 aliases=[(a, a) for a in range(n)])


def _plan_pair(arrays, other_half):
    n = len(arrays)
    per = N_CHIPS if other_half == "chip_major" else 1

    def copies(ins, outs, sems):
        x, y, c = _mesh_pos()
        out = []
        for a in range(n):
            for k in range(per):
                if other_half == "chip_major":
                    src, dst = ins[a].at[k, 1 - c], outs[a].at[k]
                else:
                    src, dst = (ins[a].at[1 - c] if other_half else ins[a]), outs[a]
                out.append(pltpu.make_async_remote_copy(
                    src_ref=src, dst_ref=dst, send_sem=sems[0].at[per * a + k], recv_sem=sems[1].at[per * a + k],
                    device_id=(x, y, 1 - c), device_id_type=MESH))
        return out

    def start(ins, outs, sems):
        for cp in copies(ins, outs, sems):
            cp.start()

    def finish(ins, outs, sems):
        for cp in copies(ins, outs, sems):
            cp.wait()

    if other_half == "chip_major":
        shapes = [_sds((a.shape[0],) + a.shape[2:], a.dtype) for a in arrays]
    else:
        shapes = [_sds(a.shape[1:] if other_half else a.shape, a.dtype) for a in arrays]
    return _Comm(list(arrays), shapes, _sem_pair(per * n), start, finish)


def _plan_chip_exchange(arrays):
    n = len(arrays)

    def copies(ins, outs, sems):
        x, y, c = _mesh_pos()
        sends, lands = [], []
        for a in range(n):
            for j, chip in enumerate(_other_chips(x, y)):
                k = 3 * a + j
                sends.append(pltpu.make_async_remote_copy(
                    src_ref=ins[a].at[2 * chip[0] + chip[1]], dst_ref=outs[a].at[2 * x + y], send_sem=sems[0].at[k],
                    recv_sem=sems[1].at[k], device_id=(*chip, c), device_id_type=MESH))
                slot = outs[a].at[2 * chip[0] + chip[1]]
                lands.append(pltpu.make_async_remote_copy(
                    src_ref=slot, dst_ref=slot, send_sem=sems[0].at[k], recv_sem=sems[1].at[k],
                    device_id=(*chip, c), device_id_type=MESH))
        return sends, lands

    def start(ins, outs, sems):
        for cp in copies(ins, outs, sems)[0]:
            cp.start()

    def finish(ins, outs, sems):
        sends, lands = copies(ins, outs, sems)
        for cp in lands:
            cp.wait_recv()
        for cp in sends:
            cp.wait_send()

    return _Comm(list(arrays), [_sds(a.shape, a.dtype) for a in arrays], _sem_pair(3 * n), start, finish)


SEM_SPEC = pl.BlockSpec(memory_space=pltpu.SEMAPHORE)
N_OTHER = N_CHIPS - 1


def _exchange_copies(s_ref, land_ref, sems):
    x, y, c = _mesh_pos()
    return [pltpu.make_async_remote_copy(
        src_ref=s_ref.at[2 * chip[0] + chip[1]], dst_ref=land_ref.at[2 * x + y], send_sem=sems[j], recv_sem=sems[N_OTHER + j],
        device_id=(*chip, c), device_id_type=MESH) for j, chip in enumerate(_other_chips(x, y))]


def _exchange_start(s, name):
    def body(s_ref, land_ref, *outs):
        sems, token = outs[:2 * N_OTHER], outs[-1]
        for cp in _exchange_copies(s_ref, land_ref, sems):
            cp.start()
        token[...] = jnp.zeros_like(token)

    hbm = pltpu.HBM(s.shape, s.dtype)
    res = pl.pallas_call(
        body, name=name,
        out_shape=(pltpu.SemaphoreType.DMA(()),) * (2 * N_OTHER) + (hbm, hbm, _sds((SUBLANES, LANE), F32)),
        in_specs=(HBM_SPEC, HBM_SPEC),
        out_specs=(SEM_SPEC,) * (2 * N_OTHER) + (HBM_SPEC, HBM_SPEC, pl.BlockSpec(memory_space=pltpu.VMEM)),
        input_output_aliases={0: 2 * N_OTHER, 1: 2 * N_OTHER + 1},
        compiler_params=pltpu.CompilerParams(has_side_effects=pltpu.SideEffectType.DATAFLOW_SIDE_EFFECTING),
    )(pltpu.with_memory_space_constraint(s, pltpu.HBM), pltpu.with_memory_space_constraint(lax.empty(s.shape, s.dtype), pltpu.HBM))
    return res[:2 * N_OTHER], res[2 * N_OTHER], res[2 * N_OTHER + 1], res[-1]


def _exchange_wait(sems, s_thru, land_thru, afters, name):
    def body(s_ref, land_ref, *rest):
        for cp in _exchange_copies(s_ref, land_ref, rest[:2 * N_OTHER]):
            cp.wait_send()
            cp.wait_recv()

    hbm = pltpu.HBM(s_thru.shape, s_thru.dtype)
    return pl.pallas_call(
        body, name=name, out_shape=(hbm, hbm),
        in_specs=(HBM_SPEC, HBM_SPEC) + (SEM_SPEC,) * (2 * N_OTHER) + (pl.BlockSpec(memory_space=pl.ANY),) * len(afters),
        out_specs=(HBM_SPEC, HBM_SPEC), input_output_aliases={0: 0, 1: 1},
        compiler_params=pltpu.CompilerParams(has_side_effects=pltpu.SideEffectType.DATAFLOW_SIDE_EFFECTING),
    )(s_thru, land_thru, *sems, *afters)


def _gather_copies(block_ref, buf_ref, sems):
    x, y, c = _mesh_pos()
    return [pltpu.make_async_remote_copy(
        src_ref=block_ref, dst_ref=buf_ref.at[4 * x + 2 * y + c], send_sem=sems[j], recv_sem=sems[N_OTHER + j],
        device_id=(*chip, c), device_id_type=MESH) for j, chip in enumerate(_other_chips(x, y))]


def _gather_start(blocks, bufs, afters, name):
    n = len(blocks)
    per = 2 * N_OTHER

    def body(*refs):
        ins, outs = refs[:2 * n], refs[2 * n + len(afters):]
        for a in range(n):
            for cp in _gather_copies(ins[a], ins[n + a], outs[a * per:(a + 1) * per]):
                cp.start()
        outs[-1][...] = jnp.zeros_like(outs[-1])

    hbm = [pltpu.HBM(z.shape, z.dtype) for z in list(blocks) + list(bufs)]
    res = pl.pallas_call(
        body, name=name,
        out_shape=(pltpu.SemaphoreType.DMA(()),) * (n * per) + tuple(hbm) + (_sds((SUBLANES, LANE), F32),),
        in_specs=(HBM_SPEC,) * (2 * n) + (pl.BlockSpec(memory_space=pl.ANY),) * len(afters),
        out_specs=(SEM_SPEC,) * (n * per) + (HBM_SPEC,) * (2 * n) + (pl.BlockSpec(memory_space=pltpu.VMEM),),
        input_output_aliases={k: n * per + k for k in range(2 * n)},
        compiler_params=pltpu.CompilerParams(has_side_effects=pltpu.SideEffectType.DATAFLOW_SIDE_EFFECTING),
    )(*[pltpu.with_memory_space_constraint(z, pltpu.HBM) for z in list(blocks) + list(bufs)], *afters)
    parts = [(res[a * per:(a + 1) * per], res[n * per + a], res[n * per + n + a]) for a in range(n)]
    return parts, res[-1]


def _gather_wait(part, afters, name):
    sems, block, buf = part

    def body(block_ref, buf_ref, *rest):
        for cp in _gather_copies(block_ref, buf_ref, rest[:2 * N_OTHER]):
            cp.wait_send()
            cp.wait_recv()

    return pl.pallas_call(
        body, name=name, out_shape=(pltpu.HBM(block.shape, block.dtype), pltpu.HBM(buf.shape, buf.dtype)),
        in_specs=(HBM_SPEC, HBM_SPEC) + (SEM_SPEC,) * (2 * N_OTHER) + (pl.BlockSpec(memory_space=pl.ANY),) * len(afters),
        out_specs=(HBM_SPEC, HBM_SPEC), input_output_aliases={0: 0, 1: 1},
        compiler_params=pltpu.CompilerParams(has_side_effects=pltpu.SideEffectType.DATAFLOW_SIDE_EFFECTING),
    )(block, buf, *sems, *afters)[1]


def _comm_only(comms, name):
    return _call(lambda: None, (), name=name, grid=(), in_specs=[], out_specs=[], out_shape=[], sem=(), comms=comms)[1]


def _allgather8(arrays, name):
    return _comm_only([_plan_allgather8(arrays)], name)[0]


def _plan_allgather8(arrays):
    n = len(arrays)

    def parts(ins, outs, sems):
        send_sems, recv_sems, local_sems = sems
        x, y, c = _mesh_pos()
        me, sibling = (x, y, c), (x, y, 1 - c)
        chips = _other_chips(x, y)

        def copy(a, k, block, to, src=None):
            dst = outs[a].at[4 * block[0] + 2 * block[1] + block[2]]
            return pltpu.make_async_remote_copy(
                src_ref=dst if src is None else src, dst_ref=dst, send_sem=send_sems.at[7 * a + k],
                recv_sem=recv_sems.at[7 * a + k], device_id=to, device_id_type=MESH)

        mine = [pltpu.make_async_copy(ins[a], outs[a].at[4 * x + 2 * y + c], local_sems.at[a]) for a in range(n)]
        first = []
        for a in range(n):
            first.append(copy(a, 0, me, sibling, src=ins[a]))
            first += [copy(a, 1 + j, me, (*chip, c), src=ins[a]) for j, chip in enumerate(chips)]
        return copy, mine, first, me, sibling, chips, c

    def start(ins, outs, sems):
        _, mine, first, *_ = parts(ins, outs, sems)
        for cp in mine + first:
            cp.start()

    def finish(ins, outs, sems):
        copy, mine, first, me, sibling, chips, c = parts(ins, outs, sems)
        passed = []
        for j, chip in enumerate(chips):
            for a in range(n):
                copy(a, 1 + j, (*chip, c), me).wait_recv()
                fwd = copy(a, 4 + j, (*chip, c), sibling)
                fwd.start()
                passed.append(fwd)
        for a in range(n):
            copy(a, 0, sibling, me).wait_recv()
            for j, chip in enumerate(chips):
                copy(a, 4 + j, (*chip, 1 - c), me).wait_recv()
        for cp in first + passed:
            cp.wait_send()
        for cp in mine:
            cp.wait()

    sems = [pltpu.SemaphoreType.DMA((7 * n,)), pltpu.SemaphoreType.DMA((7 * n,)), pltpu.SemaphoreType.DMA((n,))]
    return _Comm(list(arrays), [_sds((N_DEV,) + a.shape, a.dtype) for a in arrays], sems, start, finish)


def _pair_sum(g, q, core, name, chip_major=False):
    rows, cols = g.shape[2:]
    tr = _row_tile(rows)

    def body(core_ref, g_ref, q_ref, o_ref):
        o_ref[...] = (g_ref[...] + q_ref[...]).astype(BF16)

    blk = pl.BlockSpec((None, tr, cols), lambda k, i, core_ref: (k, i, 0))
    if chip_major:
        own = pl.BlockSpec((None, None, tr, cols), lambda k, i, core_ref: (k, core_ref[0], i, 0))
    else:
        own = pl.BlockSpec((None, None, tr, cols), lambda k, i, core_ref: (core_ref[0], k, i, 0))
    return pl.pallas_call(
        body, name=name,
        grid_spec=pltpu.PrefetchScalarGridSpec(num_scalar_prefetch=1, grid=(N_CHIPS, rows // tr), in_specs=[own, blk], out_specs=blk),
        out_shape=_sds((N_CHIPS, rows, cols), BF16), compiler_params=_params("parallel", "parallel"),
    )(core, g, q)


def _sum_chips(own, landed, chip, name):
    _, rows, cols = own.shape
    tr = _row_tile(rows)

    def body(chip_ref, own_ref, a_ref, b_ref, c_ref, o_ref):
        acc = own_ref[...].astype(F32) + a_ref[...].astype(F32)
        o_ref[...] = (acc + b_ref[...].astype(F32)) + c_ref[...].astype(F32)

    blk = lambda flip: pl.BlockSpec((None, tr, cols), lambda i, chip_ref: (jnp.bitwise_xor(chip_ref[0], flip), i, 0))
    return pl.pallas_call(
        body, name=name,
        grid_spec=pltpu.PrefetchScalarGridSpec(num_scalar_prefetch=1, grid=(rows // tr,), in_specs=[blk(0), blk(1), blk(2), blk(3)],
                                               out_specs=pl.BlockSpec((tr, cols), lambda i, chip_ref: (i, 0))),
        out_shape=_sds((rows, cols), F32), compiler_params=_params("parallel"),
    )(chip, own, landed, landed, landed)


SUBLANES = 8


def _tile_rows(n_elems):
    return -(-n_elems // (SUBLANES * LANE)) * SUBLANES


SMALL_ITEMS = (("b_ada", N_MOD * D_MODEL), ("pre_w_mix", D_MODEL), ("post_w_mix", D_MODEL), ("pre_w_mlp", D_MODEL),
               ("post_w_mlp", D_MODEL), ("attn_out_w", ATT_WIDTH), ("hg_norm_w", HG_HEAD_DIM), ("attn_sinks", ATT_Q_HEADS),
               ("lb_0", HG_WIDTH), ("lb_1", HG_WIDTH))
SMALL_AT = {}
for _name, _size in SMALL_ITEMS:
    SMALL_AT[_name] = (sum(r for _, r in SMALL_AT.values()), _tile_rows(_size))
SMALL_ROWS = sum(r for _, r in SMALL_AT.values())
MOD_ROWS = SMALL_AT["b_ada"][1]
PLAIN_ROWS = SMALL_AT["lb_0"][0] - MOD_ROWS
LB_ROWS = SMALL_AT["lb_0"][1]


def _rows(a, nrows=None):
    flat = a.reshape(-1)
    nrows = _tile_rows(flat.shape[0]) if nrows is None else nrows
    return jnp.pad(flat, (0, nrows * LANE - flat.shape[0])).reshape(nrows, LANE)


def _pack_small(vals):
    vals = dict(vals, lb_0=vals["lb_table"][0], lb_1=vals["lb_table"][1])
    return jnp.concatenate([_rows(vals[name], SMALL_AT[name][1]) for name, _ in SMALL_ITEMS], axis=0)


def _unpack_small(p):
    def item(name, shape):
        first = SMALL_AT[name][0]
        size = shape[0] * shape[1]
        return p[first:first + SMALL_AT[name][1]].reshape(-1)[:size].reshape(shape)

    out = {name: item(name, (1, size)) for name, size in SMALL_ITEMS if not name.startswith("lb_")}
    out["lb_table"] = jnp.concatenate([item("lb_0", (1, HG_WIDTH)), item("lb_1", (1, HG_WIDTH))], axis=0)
    return out


def _pack_partials(dmod, plain, d_lb, loss_row):
    return jnp.concatenate([_rows(dmod, dmod.shape[0] * MOD_ROWS)] + [_rows(g) for g in plain] + [_rows(d_lb), _rows(loss_row)], axis=0)


def _small_update(packs, w, m, v, n_seq):
    mod_end = n_seq * MOD_ROWS
    lb_at = mod_end + PLAIN_ROWS
    t0, t1 = SMALL_AT["lb_0"][0], SMALL_AT["lb_1"][0]

    def body(p_ref, w_ref, m_ref, v_ref, g_ref, dl_ref, nm_ref, nv_ref, loss_ref):
        tot = p_ref[0]
        for d in range(1, N_DEV):
            tot = tot + p_ref[d]
        wv = w_ref[...]
        p1 = _sigmoid(wv[t1:t1 + LB_ROWS] - wv[t0:t0 + LB_ROWS])
        s = tot[lb_at:lb_at + LB_ROWS] * p1 * (1.0 - p1)
        g_bias = tot[0:MOD_ROWS]
        for q in range(1, n_seq):
            g_bias = g_bias + tot[q * MOD_ROWS:(q + 1) * MOD_ROWS]
        g = jnp.concatenate([g_bias, tot[mod_end:lb_at], -s, s], axis=0)
        g_ref[...] = g
        dl_ref[...], nm_ref[...], nv_ref[...] = _adamw_math(g, wv, m_ref[...], v_ref[...])
        loss_ref[...] = tot[lb_at + LB_ROWS:lb_at + LB_ROWS + SUBLANES]

    shp = _sds((SMALL_ROWS, LANE), F32)
    return pl.pallas_call(body, name="small_update", out_shape=[shp] * 4 + [_sds((SUBLANES, LANE), F32)],
                          compiler_params=_params())(packs, w, m, v)


def kernel(x, c, w_ada, b_ada, pre_w_mix, w_in, attn_sinks, attn_out_w, lb_table, hg_norm_w, w_out, post_w_mix, pre_w_mlp, w_up, w_down, post_w_mlp, loss_target, m_w_ada, m_b_ada, m_pre_w_mix, m_w_in, m_attn_sinks, m_attn_out_w, m_lb_table, m_hg_norm_w, m_w_out, m_post_w_mix, m_pre_w_mlp, m_w_up, m_w_down, m_post_w_mlp, v_w_ada, v_b_ada, v_pre_w_mix, v_w_in, v_attn_sinks, v_attn_out_w, v_lb_table, v_hg_norm_w, v_w_out, v_post_w_mix, v_pre_w_mlp, v_w_up, v_w_down, v_post_w_mlp):
    xi, yi, ci = _mesh_pos()
    chip = 2 * xi + yi
    dev = 2 * chip + ci
    bsz, seq, _ = x.shape
    ntok = bsz * seq
    ada_cols = w_ada.shape[2]
    core = jnp.reshape(ci, (1,)).astype(jnp.int32)
    chip_idx = jnp.reshape(chip, (1,)).astype(jnp.int32)
    flat = lambda a: a.reshape(ntok, a.shape[-1])
    unflat = lambda a: a.reshape(bsz, seq, a.shape[-1])
    tables = _rope_tables(seq)
    biases, chunk_masks = _band_biases(), _block_masks()

    def row_half(w):
        rows = w.shape[1] // 2
        return lax.dynamic_slice_in_dim(w[0], ci * rows, rows, axis=0).astype(BF16)

    def gather_buffer(w):
        rows, cols = w.shape[1] // 2, w.shape[2]
        own = w[0].astype(BF16).reshape(2, rows, cols)
        return lax.dynamic_update_slice(lax.empty((N_DEV, rows, cols), BF16), own, (2 * chip, 0, 0))

    w_in_t, m_in_t, v_in_t = [jnp.transpose(a[0])[None] for a in (w_in, m_w_in, v_w_in)]
    c_g, in_g = _allgather8([c, row_half(w_in_t)], "gather_first")
    c_all = c_g.reshape(N_DEV * bsz, D_MODEL)
    w_in_full = in_g.reshape(IN_COLS, D_MODEL)

    b_cols = lax.dynamic_slice_in_dim(b_ada, chip * ada_cols, ada_cols, axis=1)
    mod_part = _ada_fwd(c_all, w_ada[0], b_cols)
    half_rows = mod_part.shape[0] // 2
    (mod_g,) = _allgather8([lax.dynamic_slice_in_dim(mod_part, ci * half_rows, half_rows, axis=0)], "gather_mod")
    mod_all = mod_g.reshape(N_CHIPS, 2, half_rows, ada_cols).transpose(1, 2, 0, 3).reshape(N_DEV * bsz, N_MOD * D_MODEL)
    mod = lax.dynamic_slice_in_dim(mod_all, dev * bsz, bsz, axis=0)
    sh1, sc1, g1, sh2, sc2, g2 = [mod[:, i * D_MODEL:(i + 1) * D_MODEL].reshape(bsz, 1, D_MODEL) for i in range(N_MOD)]

    weights = (w_out, w_up, w_down)
    (out_part, up_part, down_part), started = _gather_start(
        [row_half(w) for w in weights], [gather_buffer(w) for w in weights], [mod_g], "gather_weights_start")

    h1, proj, qh, kh, vh = _in_proj_fused(x, pre_w_mix, sc1 + started[0:1, 0:1], sh1, w_in_full, tables)
    out_g = _gather_wait(out_part, [proj], "gather_out_wait")
    (attn_raw, cat, lse), ((out_g,),) = _attn_fwd(qh, kh, vh, attn_sinks, attn_out_w, biases, comms=[_plan_pair_forward([out_g])])
    up_g = _gather_wait(up_part, [attn_raw], "gather_up_wait")
    (o_raw, cat, states), ((up_g,),) = _hgrn_fwd(proj, lb_table, hg_norm_w, cat, chunk_masks, comms=[_plan_pair_forward([up_g])])
    down_g = _gather_wait(down_part, [o_raw], "gather_down_wait")
    w_out_full = out_g.reshape(D_MODEL, D_MODEL)
    w_up4 = up_g.reshape(N_CHIPS, D_MODEL, D_MODEL)
    mix, x1, h2 = _out_proj_fused(cat, w_out_full, x, post_w_mix, g1, pre_w_mlp, sc2, sh2)
    big_tm = min(ntok, 2048)
    up_spec = pl.BlockSpec((None, D_MODEL, D_MODEL), lambda i, j: (j, 0, 0))
    r, ((down_g,),) = _mm(flat(h2), w_up4, name="up_proj", out_dtype=BF16, tm=big_tm, tn=D_MODEL, n_out=D_FF, b_spec=up_spec,
                          epi=lambda acc: jnp.maximum(acc, 0.0), comms=[_plan_pair_forward([down_g])])
    w_down_full = down_g.reshape(D_FF, D_MODEL)
    square = lambda t: t * t
    loss_row, dy, dd, dg2, d_post_mlp = _down_proj_fused(unflat(r), w_down_full, x1, post_w_mlp, g2, loss_target)

    dpre = _mm(flat(dd), w_down_full, name="down_bwd", out_dtype=BF16, trans_b=True, tm=big_tm, tn=D_MODEL, extra=(r,),
               epi=lambda acc, rt: acc * (2.0 * rt.astype(F32)))
    half_rows = D_MODEL // 2
    g_down = _mm_tn(r, flat(dd), name="down_wgrad", tk=half_rows, tn=D_MODEL, a_fn=square,
                    out_shape=_sds((2, N_CHIPS, half_rows, D_MODEL), F32),
                    out_spec=pl.BlockSpec((None, None, half_rows, D_MODEL), lambda i, j: (i % 2, i // 2, 0, 0)))
    (dx1, dmix, dsc2, dsh2, dg1, d_pre_mlp, d_post_mix), ((q_down,),) = _up_bwd_fused(
        unflat(dpre), w_up4, dy, x1, mix, pre_w_mlp, sc2, post_w_mix, g1, comms=[_plan_pair([g_down], True)])
    g_up = _mm_tn(flat(h2), dpre, name="up_wgrad", tk=D_MODEL, tn=half_rows,
                  out_shape=_sds((2, N_CHIPS, half_rows, D_MODEL), F32),
                  out_spec=pl.BlockSpec((2, None, half_rows, half_rows), lambda i, j: (0, j // 2, 0, j % 2)))
    s_down = _pair_sum(g_down, q_down, core, "pair_sum_down")

    dcat, ((q_up,),) = _mm(flat(dmix), w_out_full, name="out_bwd", out_dtype=F32, trans_b=True, comms=[_plan_pair([g_up], True)])
    dcat = unflat(dcat)
    s_up = _pair_sum(g_up, q_up, core, "pair_sum_up")
    out_rows = D_MODEL // N_CHIPS
    g_out = _mm_tn(flat(cat), flat(dmix), name="out_wgrad", tk=2 * out_rows, tn=half_rows,
                   out_shape=_sds((2, N_CHIPS, out_rows, half_rows), F32),
                   out_spec=pl.BlockSpec((None, 2, out_rows, half_rows), lambda i, j: (j, i, 0, 0)))
    (dhq, dhf, dhi, dhg, d_lb, d_hg_norm), ((x_down,), (q_out,)) = _hgrn_bwd(
        dcat, proj, o_raw, states, lb_table, hg_norm_w, chunk_masks, comms=[_plan_chip_exchange([s_down]), _plan_pair([g_out], True)])
    half_down = _sum_chips(s_down, x_down, chip_idx, "sum_chips_down")
    s_out = _pair_sum(g_out, q_out, core, "pair_sum_out")
    (dproj_a, d_attn_out, d_sinks), ((their_down,), (x_up, x_out)) = _attn_bwd(
        dcat, attn_raw, attn_out_w, qh, kh, vh, lse, attn_sinks, tables, biases,
        comms=[_plan_pair([half_down], False), _plan_chip_exchange([s_up, s_out])])
    half_up = _sum_chips(s_up, x_up, chip_idx, "sum_chips_up")
    half_out = _sum_chips(s_out, x_out, chip_idx, "sum_chips_out")
    dproj = flat(jnp.concatenate([dproj_a, dhq, dhf, dhi, dhg], axis=-1))
    in_rows = IN_COLS // N_CHIPS // 2
    g_in = _mm_tn(dproj, flat(h1), name="in_wgrad", tk=2 * LANE, tn=D_MODEL).reshape(N_CHIPS, 2, in_rows, D_MODEL)
    dh1, ((q_in,), (their_up, their_out)) = _mm(
        dproj, w_in_full, name="in_bwd", out_dtype=F32,
        comms=[_plan_pair([g_in], "chip_major"), _plan_pair([half_up, half_out], False)])
    s_in = _pair_sum(g_in, q_in, core, "pair_sum_in", chip_major=True)
    in_sems, s_in, in_landing, started = _exchange_start(s_in, "exchange_in_start")
    grad_x, dsc1, dsh1, d_pre_mix = _norm1_bwd(unflat(dh1), dx1, x, pre_w_mix + started[0:1, 0:1], sc1)

    dmod = jnp.concatenate([dsh1, dsc1, dg1, dsh2, dsc2, dg2], axis=-1).reshape(bsz, N_MOD * D_MODEL)
    pack = _pack_partials(dmod, [d_pre_mix, d_post_mix, d_pre_mlp, d_post_mlp, d_attn_out, d_hg_norm, d_sinks], d_lb, loss_row)
    ((packs,),) = _comm_only([_plan_allgather8([pack])], "gather_small")
    w_small = dict(b_ada=b_ada, pre_w_mix=pre_w_mix, post_w_mix=post_w_mix, pre_w_mlp=pre_w_mlp, post_w_mlp=post_w_mlp,
                   attn_out_w=attn_out_w, hg_norm_w=hg_norm_w, attn_sinks=attn_sinks, lb_table=lb_table)
    m_small = dict(b_ada=m_b_ada, pre_w_mix=m_pre_w_mix, post_w_mix=m_post_w_mix, pre_w_mlp=m_pre_w_mlp, post_w_mlp=m_post_w_mlp,
                   attn_out_w=m_attn_out_w, hg_norm_w=m_hg_norm_w, attn_sinks=m_attn_sinks, lb_table=m_lb_table)
    v_small = dict(b_ada=v_b_ada, pre_w_mix=v_pre_w_mix, post_w_mix=v_post_w_mix, pre_w_mlp=v_pre_w_mlp, post_w_mlp=v_post_w_mlp,
                   attn_out_w=v_attn_out_w, hg_norm_w=v_hg_norm_w, attn_sinks=v_attn_sinks, lb_table=v_lb_table)
    *small_packed, loss_rows = _small_update(packs, _pack_small(w_small), _pack_small(m_small), _pack_small(v_small), bsz)
    small_out = [_unpack_small(p) for p in small_packed]
    loss = loss_rows[0, 0]

    dmod_all = packs[:, :bsz * MOD_ROWS, :].reshape(N_DEV * bsz, N_MOD * D_MODEL)
    dmod_cols = lax.dynamic_slice_in_dim(dmod_all, chip * ada_cols, ada_cols, axis=1)
    ada_out = _ada_bwd_adamw(c_all, dmod_cols, w_ada[0], m_w_ada[0], v_w_ada[0])

    s_in, x_in = _exchange_wait(in_sems, s_in, in_landing, [grad_x, ada_out[0]], "exchange_in_wait")
    half_in = _sum_chips(s_in, x_in, chip_idx, "sum_chips_in")
    ((their_in,),) = _comm_only([_plan_pair([half_in], False)], "pair_swap_in")
    big = dict(
        w_in=tuple(jnp.transpose(a) for a in _adamw_halves(half_in, their_in, core, w_in_t[0], m_in_t[0], v_in_t[0], axis=0,
                                                           name="adamw_in")),
        w_up=tuple(_adamw_halves(half_up, their_up, core, w_up[0], m_w_up[0], v_w_up[0], axis=0, name="adamw_up")),
        w_out=tuple(_adamw_halves(half_out, their_out, core, w_out[0], m_w_out[0], v_w_out[0], axis=1, name="adamw_out")),
        w_down=tuple(_adamw_halves(half_down, their_down, core, w_down[0], m_w_down[0], v_w_down[0], axis=0, name="adamw_down")),
        w_ada=tuple(ada_out),
    )
    order = ("w_ada", "b_ada", "pre_w_mix", "w_in", "attn_sinks", "attn_out_w", "lb_table", "hg_norm_w", "w_out", "post_w_mix",
             "pre_w_mlp", "w_up", "w_down", "post_w_mlp")
    outs = [loss, grad_x]
    for kind in range(4):
        for nm in order:
            outs.append(big[nm][kind][None] if nm in big else small_out[kind][nm])
    return tuple(outs)
```

```python
import jax
import jax.numpy as jnp
from jax import lax
from jax.experimental import pallas as pl
from jax.experimental.pallas import tpu as pltpu

F32 = jnp.float32
BF16 = jnp.bfloat16

D_MODEL = 1024
ATT_WIDTH = 512
ATT_HEAD_DIM = 64
ATT_Q_HEADS = 8
ATT_KV_HEADS = 2
ATT_GROUP = ATT_Q_HEADS // ATT_KV_HEADS
ATT_KV_COLS = ATT_KV_HEADS * ATT_HEAD_DIM
WINDOW = 128
ROPE_DIM = 16
ROPE_THETA = 500000.0
HG_WIDTH = 512
MIX_WIDTH = ATT_WIDTH + HG_WIDTH
HG_HEAD_DIM = 128
HG_HEADS = 4
HG_CHUNK = 32
IN_COLS = ATT_WIDTH + 2 * ATT_KV_COLS + 4 * HG_WIDTH
ATT_COLS = ATT_WIDTH + 2 * ATT_KV_COLS
D_FF = 4 * D_MODEL
N_MOD = 6
EPS = 1e-6
ATT_SCALE = ATT_HEAD_DIM ** -0.5

ADAM_LR = 0.001
ADAM_B1 = 0.9
ADAM_B2 = 0.999
ADAM_EPS = 1e-08
ADAM_WD = 0.01
ADAM_STEP = 10

N_CHIPS = 4
N_DEV = 8
LANE = 128
VMEM_LIMIT = 48 * 1024 * 1024
VMEM_LIMIT_BIG = 58 * 1024 * 1024
MESH = pl.DeviceIdType.MESH

NT_DIMS = (((1,), (1,)), ((), ()))
TN_DIMS = (((0,), (0,)), ((), ()))


def _sds(shape, dtype):
    return jax.ShapeDtypeStruct(tuple(shape), dtype)


def _params(*sem, vmem_limit=None):
    return pltpu.CompilerParams(dimension_semantics=sem, vmem_limit_bytes=VMEM_LIMIT if vmem_limit is None else vmem_limit)


def _sigmoid(x):
    return 1.0 / (1.0 + jnp.exp(-x))


def _dot(a, b, dims=None):
    a, b = a.astype(BF16), b.astype(BF16)
    if dims is None:
        return jnp.dot(a, b, preferred_element_type=F32)
    return lax.dot_general(a, b, dims, preferred_element_type=F32)


def _rms_fwd(x, w):
    rstd = lax.rsqrt(jnp.mean(x * x, axis=-1, keepdims=True) + EPS)
    xh = x * rstd
    return xh * w, xh, rstd


def _rms_bwd(dy, xh, rstd, w):
    dxh = dy * w
    dx = rstd * (dxh - xh * jnp.mean(dxh * xh, axis=-1, keepdims=True))
    return dx, dy * xh


def _colsum(x):
    return jnp.sum(x, axis=0, keepdims=True)


def _row_tile(rows, cap=256):
    return max(t for t in range(16, cap + 1, 16) if rows % t == 0)


HBM_SPEC = pl.BlockSpec(memory_space=pltpu.HBM)


def _mesh_pos():
    return lax.axis_index("x"), lax.axis_index("y"), lax.axis_index("c")


class _Comm:
    def __init__(self, ins, outs, sems, start, finish, aliases=()):
        self.ins, self.outs, self.sems = list(ins), list(outs), list(sems)
        self.start, self.finish, self.aliases = start, finish, tuple(aliases)


def _call(body, args, *, name, grid, in_specs, out_specs, out_shape, sem, scratch_shapes=(), comms=(), aliases=None,
          vmem_limit=None):
    scratch_shapes = list(scratch_shapes)
    if not comms:
        return pl.pallas_call(body, name=name, grid=grid, in_specs=in_specs, out_specs=out_specs, out_shape=out_shape,
                              input_output_aliases=dict(aliases or {}), scratch_shapes=scratch_shapes,
                              compiler_params=_params(*sem, vmem_limit=vmem_limit))(*args)
    single = not isinstance(out_shape, (list, tuple))
    out_specs_l = [out_specs] if single else list(out_specs)
    out_shape_l = [out_shape] if single else list(out_shape)
    n_in, n_out, n_scr = len(in_specs), len(out_shape_l), len(scratch_shapes)
    n_ci = [len(cm.ins) for cm in comms]
    n_co = [len(cm.outs) for cm in comms]
    n_cs = [len(cm.sems) for cm in comms]
    aliases = dict(aliases or {})
    for k, cm in enumerate(comms):
        for i, o in cm.aliases:
            aliases[n_in + sum(n_ci[:k]) + i] = n_out + sum(n_co[:k]) + o

    def fused(*refs):
        pos = [0]

        def take(n):
            part = refs[pos[0]:pos[0] + n]
            pos[0] += n
            return part

        ins = take(n_in)
        c_ins = [take(n) for n in n_ci]
        outs = take(n_out)
        c_outs = [take(n) for n in n_co]
        scr = take(n_scr)
        c_sems = [take(n) for n in n_cs]
        first, last = True, True
        for d, size in enumerate(grid):
            first = jnp.logical_and(first, pl.program_id(d) == 0)
            last = jnp.logical_and(last, pl.program_id(d) == size - 1)

        def run(which):
            for cm, ci, co, cs in zip(comms, c_ins, c_outs, c_sems):
                getattr(cm, which)(ci, co, cs)

        if grid:
            pl.when(first)(lambda: run("start"))
        else:
            run("start")
        body(*ins, *outs, *scr)
        if grid:
            pl.when(last)(lambda: run("finish"))
        else:
            run("finish")

    res = pl.pallas_call(
        fused, name=name, grid=grid, in_specs=list(in_specs) + [HBM_SPEC] * sum(n_ci),
        out_specs=out_specs_l + [HBM_SPEC] * sum(n_co), out_shape=out_shape_l + [s for cm in comms for s in cm.outs],
        input_output_aliases=aliases, scratch_shapes=scratch_shapes + [s for cm in comms for s in cm.sems],
        compiler_params=_params(*["arbitrary"] * len(grid), vmem_limit=vmem_limit),
    )(*args, *[a for cm in comms for a in cm.ins])
    main = res[:n_out]
    extra, at = [], n_out
    for n in n_co:
        extra.append(list(res[at:at + n]))
        at += n
    return (main[0] if single else list(main)), extra


def _mm(a, b, *, name, out_dtype, trans_b=False, tm=512, tn=None, extra=(), epi=None, b_spec=None, n_out=None, comms=()):
    m_total, k_total = a.shape
    if n_out is None:
        n_out = b.shape[0] if trans_b else b.shape[1]
    tn = n_out if tn is None else tn
    grid = (m_total // tm, n_out // tn)
    dims = NT_DIMS if trans_b else None

    def body(*refs):
        a_ref, b_ref = refs[0], refs[1]
        extra_refs = refs[2:2 + len(extra)]
        o_ref = refs[2 + len(extra)]
        acc = _dot(a_ref[...], b_ref[...], dims)
        if epi is not None:
            acc = epi(acc, *[r[...] for r in extra_refs])
        o_ref[...] = acc.astype(out_dtype)

    if b_spec is None:
        if trans_b:
            b_spec = pl.BlockSpec((tn, k_total), lambda i, j: (j, 0))
        else:
            b_spec = pl.BlockSpec((k_total, tn), lambda i, j: (0, j))
    in_specs = [pl.BlockSpec((tm, k_total), lambda i, j: (i, 0)), b_spec]
    in_specs += [pl.BlockSpec((tm, tn), lambda i, j: (i, j)) for _ in extra]
    return _call(
        body, (a, b, *extra), name=name, grid=grid, in_specs=in_specs,
        out_specs=pl.BlockSpec((tm, tn), lambda i, j: (i, j)),
        out_shape=_sds((m_total, n_out), out_dtype),
        sem=("parallel", "parallel"), comms=comms)


def _mm_tn(a, b, *, name, tk, tn, a_fn=None, out_shape=None, out_spec=None):
    m_total, k_total = a.shape
    n_total = b.shape[1]
    grid = (k_total // tk, n_total // tn)

    def body(a_ref, b_ref, o_ref):
        av = a_ref[...]
        part = _dot(av if a_fn is None else a_fn(av), b_ref[...], TN_DIMS)
        o_ref[...] = part.reshape(o_ref.shape)

    if out_shape is None:
        out_shape = _sds((k_total, n_total), F32)
        out_spec = pl.BlockSpec((tk, tn), lambda i, j: (i, j))
    return pl.pallas_call(
        body, name=name, grid=grid,
        in_specs=[pl.BlockSpec((m_total, tk), lambda i, j: (0, i)), pl.BlockSpec((m_total, tn), lambda i, j: (0, j))],
        out_specs=out_spec, out_shape=out_shape,
        compiler_params=_params("parallel", "parallel"),
    )(a, b)


def _ada_fwd(c_all, w_shard, b_shard):
    nb, ncol = c_all.shape[0], w_shard.shape[1]
    tn = 512

    def body(c_ref, w_ref, b_ref, o_ref):
        c = c_ref[...]
        o_ref[...] = _dot(c * _sigmoid(c), w_ref[...]) + b_ref[...]

    return pl.pallas_call(
        body, name="ada_fwd", grid=(ncol // tn,),
        in_specs=[pl.BlockSpec((nb, D_MODEL), lambda j: (0, 0)), pl.BlockSpec((D_MODEL, tn), lambda j: (0, j)),
                  pl.BlockSpec((1, tn), lambda j: (0, j))],
        out_specs=pl.BlockSpec((nb, tn), lambda j: (0, j)), out_shape=_sds((nb, ncol), F32),
        compiler_params=_params("parallel"),
    )(c_all, w_shard, b_shard)


def _adamw_math(g, w, m, v):
    m = ADAM_B1 * m + (1.0 - ADAM_B1) * g
    v = ADAM_B2 * v + (1.0 - ADAM_B2) * (g * g)
    m_hat = m / (1.0 - ADAM_B1 ** ADAM_STEP)
    v_hat = v / (1.0 - ADAM_B2 ** ADAM_STEP)
    delta = -ADAM_LR * (m_hat / (jnp.sqrt(v_hat) + ADAM_EPS) + ADAM_WD * w)
    return delta, m, v


def _ada_bwd_adamw(c_all, dmod_cols, w, m, v):
    nb, ncol = dmod_cols.shape
    tn = 256

    def body(c_ref, d_ref, w_ref, m_ref, v_ref, g_ref, dl_ref, nm_ref, nv_ref):
        c = c_ref[...]
        g = _dot(c * _sigmoid(c), d_ref[...], TN_DIMS)
        g_ref[...] = g
        dl_ref[...], nm_ref[...], nv_ref[...] = _adamw_math(g, w_ref[...], m_ref[...], v_ref[...])

    col = pl.BlockSpec((D_MODEL, tn), lambda j: (0, j))
    shp = _sds((D_MODEL, ncol), F32)
    return pl.pallas_call(
        body, name="ada_bwd_adamw", grid=(ncol // tn,),
        in_specs=[pl.BlockSpec((nb, D_MODEL), lambda j: (0, 0)), pl.BlockSpec((nb, tn), lambda j: (0, j)), col, col, col],
        out_specs=[col, col, col, col], out_shape=[shp, shp, shp, shp],
        compiler_params=_params("parallel"),
    )(c_all, dmod_cols, w, m, v)


def _adamw_halves(own, theirs, core, w, m, v, *, axis, name):
    r2, c2 = own.shape
    tr = _row_tile(r2)
    nt = r2 // tr

    def body(core_ref, own_ref, their_ref, w_ref, m_ref, v_ref, g_ref, dl_ref, nm_ref, nv_ref):
        g = jnp.where(pl.program_id(0) == core_ref[0], own_ref[...], their_ref[...])
        g_ref[...] = g
        dl_ref[...], nm_ref[...], nv_ref[...] = _adamw_math(g, w_ref[...], m_ref[...], v_ref[...])

    if axis == 0:
        full = pl.BlockSpec((tr, c2), lambda h, i, core_ref: (h * nt + i, 0))
    else:
        full = pl.BlockSpec((tr, c2), lambda h, i, core_ref: (i, h))
    half = pl.BlockSpec((tr, c2), lambda h, i, core_ref: (i, 0))
    shp = _sds(w.shape, F32)
    return pl.pallas_call(
        body, name=name,
        grid_spec=pltpu.PrefetchScalarGridSpec(num_scalar_prefetch=1, grid=(2, nt), in_specs=[half, half, full, full, full],
                                               out_specs=[full] * 4),
        out_shape=[shp] * 4, compiler_params=_params("parallel", "parallel"),
    )(core, own, theirs, w, m, v)


def _tok_spec(tm, width=D_MODEL):
    return pl.BlockSpec((None, tm, width), lambda b, i: (b, i, 0))


def _row_spec(width=D_MODEL):
    return pl.BlockSpec((None, 1, width), lambda b, i: (b, 0, 0))


def _vec_spec(width=D_MODEL):
    return pl.BlockSpec((1, width), lambda b, i: (0, 0))


class _RowsOf:
    def __init__(self, ref, first, count):
        self.ref, self.rows = ref, slice(first, first + count)

    def __getitem__(self, idx):
        return self.ref[self.rows, :]

    def __setitem__(self, idx, value):
        self.ref[self.rows, :] = value


def _mm_rows(a, b, *, name, tm, extra, extra_specs, out_specs, out_shape, epi, pro=None, trans_b=False, b_chunks=1, comms=(),
             parts=1, zero_per_seq=(), zero_once=(), vmem_limit=None):
    bsz, seq, k_total = a.shape
    kc = k_total // b_chunks
    dims = NT_DIMS if trans_b else None
    rows = tm // parts

    def body(*refs):
        a_ref, b_ref = refs[0], refs[1]
        ex, outs = refs[2:2 + len(extra)], refs[2 + len(extra):]
        if zero_per_seq:
            @pl.when(pl.program_id(1) == 0)
            def _():
                for k in zero_per_seq:
                    outs[k][...] = jnp.zeros_like(outs[k])
        if zero_once:
            @pl.when(jnp.logical_and(pl.program_id(0) == 0, pl.program_id(1) == 0))
            def _():
                for k in zero_once:
                    outs[k][...] = jnp.zeros_like(outs[k])

        def part_of(ref, p):
            tiled = len(ref.shape) == 2 and ref.shape[0] == tm
            return _RowsOf(ref, p * rows, rows) if tiled and parts > 1 else ref

        accs = []
        for p in range(parts):
            a_p, ex_p, outs_p = part_of(a_ref, p), [part_of(r, p) for r in ex], [part_of(r, p) for r in outs]
            if b_chunks == 1:
                accs.append(_dot(a_p[...] if pro is None else pro(a_p, ex_p, outs_p), b_ref[...], dims))
            else:
                acc = _dot(a_p[...][:, 0:kc], b_ref[0], NT_DIMS)
                for k in range(1, b_chunks):
                    acc = acc + _dot(a_p[...][:, k * kc:(k + 1) * kc], b_ref[k], NT_DIMS)
                accs.append(acc)
        for p in range(parts):
            epi(accs[p], [part_of(r, p) for r in ex], [part_of(r, p) for r in outs])

    b_spec = pl.BlockSpec(b.shape, lambda bb, i: (0,) * b.ndim)
    return _call(
        body, (a, b, *extra), name=name, grid=(bsz, seq // tm), in_specs=[_tok_spec(tm, k_total), b_spec, *extra_specs],
        out_specs=out_specs, out_shape=out_shape, sem=("arbitrary", "arbitrary"), comms=comms, vmem_limit=vmem_limit)


def _in_proj_fused(x, w, sc, sh, w_in_t, tables, comms=()):
    tm = 512
    bsz, seq, _ = x.shape
    half = ROPE_DIM // 2
    heads_per_slab = LANE // ATT_HEAD_DIM

    def pro(x_ref, ex, outs):
        y, _, _ = _rms_fwd(x_ref[...], ex[0][...])
        h = (y * (1.0 + ex[1][...]) + ex[2][...]).astype(BF16)
        outs[0][...] = h
        return h

    def epi(acc, ex, outs):
        c, u, d = ex[3][...], ex[4][...], ex[5][...]
        _, proj_ref, q_ref, k_ref, v_ref = outs
        proj_ref[...] = acc

        def rope(z):
            return (z * c + pltpu.roll(z, half, 1) * u + pltpu.roll(z, LANE - half, 1) * d).astype(BF16)

        for s in range(ATT_WIDTH // LANE):
            slab = rope(acc[:, s * LANE:(s + 1) * LANE])
            for part in range(heads_per_slab):
                g, hh = divmod(s * heads_per_slab + part, ATT_GROUP)
                piece = slab[:, part * ATT_HEAD_DIM:(part + 1) * ATT_HEAD_DIM]
                for blk in range(tm // WINDOW):
                    q_ref[blk, g, hh * WINDOW:(hh + 1) * WINDOW, :] = piece[blk * WINDOW:(blk + 1) * WINDOW]
        rk = rope(acc[:, ATT_WIDTH:ATT_WIDTH + LANE])
        vv = acc[:, ATT_WIDTH + LANE:ATT_COLS].astype(BF16)
        for g in range(ATT_KV_HEADS):
            k_ref[g] = rk[:, g * ATT_HEAD_DIM:(g + 1) * ATT_HEAD_DIM]
            v_ref[g] = vv[:, g * ATT_HEAD_DIM:(g + 1) * ATT_HEAD_DIM]

    cols = w_in_t.shape[0]
    tab = pl.BlockSpec((tm, LANE), lambda b, i: (i, 0))
    kv_spec = pl.BlockSpec((None, ATT_KV_HEADS, tm, ATT_HEAD_DIM), lambda b, i: (b, 0, i, 0))
    kv_shape = _sds((bsz, ATT_KV_HEADS, seq, ATT_HEAD_DIM), BF16)
    q_spec = pl.BlockSpec((None, tm // WINDOW, ATT_KV_HEADS, GROUP_ROWS, ATT_HEAD_DIM), lambda b, i: (b, i, 0, 0, 0))
    return _mm_rows(x, w_in_t, name="in_proj", tm=tm, extra=(w, sc, sh, *tables),
                    extra_specs=[_vec_spec(), _row_spec(), _row_spec(), tab, tab, tab],
                    out_specs=[_tok_spec(tm), _tok_spec(tm, cols), q_spec, kv_spec, kv_spec],
                    out_shape=[_sds(x.shape, BF16), _sds((bsz, seq, cols), F32),
                               _sds((bsz, seq // WINDOW, ATT_KV_HEADS, GROUP_ROWS, ATT_HEAD_DIM), BF16), kv_shape, kv_shape],
                    pro=pro, epi=epi, trans_b=True, comms=comms)


def _rope_tables(seq):
    half = ROPE_DIM // 2
    inv_freq = ROPE_THETA ** (-jnp.arange(0, ROPE_DIM, 2, dtype=F32) / ROPE_DIM)
    ang = jnp.arange(seq, dtype=F32)[:, None] * inv_freq[None, :]
    cos, sin = jnp.cos(ang), jnp.sin(ang)
    rest = ATT_HEAD_DIM - ROPE_DIM
    ones, zeros, zh = jnp.ones((seq, rest), F32), jnp.zeros((seq, rest), F32), jnp.zeros((seq, half), F32)
    reps = LANE // ATT_HEAD_DIM
    t_cos = jnp.tile(jnp.concatenate([cos, cos, ones], axis=1), (1, reps))
    t_up = jnp.tile(jnp.concatenate([zh, sin, zeros], axis=1), (1, reps))
    t_dn = jnp.tile(jnp.concatenate([-sin, zh, zeros], axis=1), (1, reps))
    return t_cos, t_up, t_dn


GROUP_ROWS = ATT_GROUP * WINDOW


ATT_BPS = 2


MASKED = -1e30


def _band_biases():
    row = jnp.arange(GROUP_ROWS)[:, None] % WINDOW
    col = jnp.arange(2 * WINDOW)[None, :]
    own = jnp.logical_and(col >= WINDOW, col - WINDOW <= row)
    before = jnp.logical_and(col < WINDOW, col > row)
    return (jnp.where(jnp.logical_or(own, before), 0.0, MASKED).astype(F32), jnp.where(own, 0.0, MASKED).astype(F32))


def _band_bias(full_ref, first_ref, has_prev):
    return full_ref[...] if has_prev is True else jnp.where(has_prev, full_ref[...], first_ref[...])


def _sink_column(sink_ref, g):
    head = lax.broadcasted_iota(jnp.int32, (GROUP_ROWS, 1), 0) // WINDOW
    col = jnp.full((GROUP_ROWS, 1), sink_ref[0, g * ATT_GROUP], F32)
    for hh in range(1, ATT_GROUP):
        col = jnp.where(head == hh, sink_ref[0, g * ATT_GROUP + hh], col)
    return col


def _bias_spec():
    return pl.BlockSpec((GROUP_ROWS, 2 * WINDOW), lambda b, i: (0, 0))


def _attn_specs():
    q_spec = pl.BlockSpec((None, ATT_BPS, ATT_KV_HEADS, GROUP_ROWS, ATT_HEAD_DIM), lambda b, i: (b, i, 0, 0, 0))
    kv_cur = pl.BlockSpec((None, ATT_KV_HEADS, ATT_BPS * WINDOW, ATT_HEAD_DIM), lambda b, i: (b, 0, i, 0))
    kv_prev = pl.BlockSpec((None, ATT_KV_HEADS, WINDOW, ATT_HEAD_DIM), lambda b, i: (b, 0, jnp.maximum(ATT_BPS * i - 1, 0), 0))
    return q_spec, kv_cur, kv_prev


def _band(prev_ref, cur_ref, g, blk):
    own = cur_ref[g, blk * WINDOW:(blk + 1) * WINDOW]
    before = prev_ref[g] if blk == 0 else cur_ref[g, (blk - 1) * WINDOW:blk * WINDOW]
    return jnp.concatenate([before, own], axis=0)


def _attn_fwd(qh, kh, vh, sinks, w_norm, biases, comms=()):
    bsz, nblk = qh.shape[0], qh.shape[1]
    seq = nblk * WINDOW
    rows = ATT_BPS * WINDOW

    def body(sink_ref, q_ref, kc_ref, kp_ref, vc_ref, vp_ref, w_ref, full_ref, first_ref, raw_ref, an_ref, l_ref):
        for blk in range(ATT_BPS):
            bias = _band_bias(full_ref, first_ref, True if blk else pl.program_id(1) > 0)
            for g in range(ATT_KV_HEADS):
                keys, vals = _band(kp_ref, kc_ref, g, blk), _band(vp_ref, vc_ref, g, blk)
                sink = _sink_column(sink_ref, g)
                s = _dot(q_ref[blk, g], keys, NT_DIMS) * ATT_SCALE + bias
                m = jnp.maximum(jnp.max(s, axis=-1, keepdims=True), sink)
                p = jnp.exp(s - m)
                den = jnp.sum(p, axis=-1, keepdims=True) + jnp.exp(sink - m)
                o = _dot(p / den, vals)
                lse = m + jnp.log(den)
                tok = slice(blk * WINDOW, (blk + 1) * WINDOW)
                for hh in range(ATT_GROUP):
                    h = g * ATT_GROUP + hh
                    raw_ref[tok, h * ATT_HEAD_DIM:(h + 1) * ATT_HEAD_DIM] = o[hh * WINDOW:(hh + 1) * WINDOW]
                    l_ref[tok, h:h + 1] = lse[hh * WINDOW:(hh + 1) * WINDOW]
        y, _, _ = _rms_fwd(raw_ref[...], w_ref[...])
        an_ref[...] = y.astype(BF16)

    cur = lambda width: pl.BlockSpec((None, rows, width), lambda b, i: (b, i, 0))
    q_spec, kv_cur, kv_prev = _attn_specs()
    return _call(
        body, (sinks, qh, kh, kh, vh, vh, w_norm, *biases), name="attn_fwd", grid=(bsz, nblk // ATT_BPS),
        in_specs=[pl.BlockSpec(memory_space=pltpu.SMEM), q_spec, kv_cur, kv_prev, kv_cur, kv_prev, _vec_spec(ATT_WIDTH),
                  _bias_spec(), _bias_spec()],
        out_specs=[cur(ATT_WIDTH), cur(ATT_WIDTH), cur(ATT_Q_HEADS)],
        out_shape=[_sds((bsz, seq, ATT_WIDTH), F32), _sds((bsz, seq, MIX_WIDTH), BF16), _sds((bsz, seq, ATT_Q_HEADS), F32)],
        sem=("parallel", "parallel"), comms=comms)


HG_Q0 = ATT_COLS // LANE
HG_F0 = HG_Q0 + HG_HEADS
HG_I0 = HG_F0 + HG_HEADS
HG_G0 = HG_I0 + HG_HEADS
HG_TOK = 256
HG_NCH = HG_TOK // HG_CHUNK
HG_HPS = 2


def _block_masks():
    row = jnp.arange(HG_TOK)[:, None]
    col = jnp.arange(HG_TOK)[None, :]
    same = (row // HG_CHUNK) == (col // HG_CHUNK)
    return jnp.logical_and(same, col <= row).astype(F32), jnp.logical_and(same, col >= row).astype(F32)


def _row_in_chunk():
    return lax.broadcasted_iota(jnp.int32, (HG_TOK, LANE), 0) % HG_CHUNK


def _chunk_cumsum(x, reverse=False):
    ric = _row_in_chunk()
    shift = 1
    while shift < HG_CHUNK:
        if reverse:
            x = x + jnp.where(ric < HG_CHUNK - shift, pltpu.roll(x, HG_TOK - shift, 0), 0.0)
        else:
            x = x + jnp.where(ric >= shift, pltpu.roll(x, shift, 0), 0.0)
        shift *= 2
    return x


def _chunk_rows(rows):
    stacked = jnp.concatenate([r[None] for r in rows], axis=0)
    return jnp.broadcast_to(stacked, (HG_NCH, HG_CHUNK, LANE)).reshape(HG_TOK, LANE)


def _chunk_slices(x):
    return [x[j * HG_CHUNK:(j + 1) * HG_CHUNK] for j in range(HG_NCH)]


def _hgrn_common(tbl, hf, hq):
    lb = _sigmoid(tbl[1:2] - tbl[0:1])
    sig = _sigmoid(hf)
    f = lb + (1.0 - lb) * sig
    sq = _sigmoid(hq)
    q, k = hq * sq, 1.0 - f
    b = _chunk_cumsum(jnp.log(f))
    last = [b[(j + 1) * HG_CHUNK - 1:(j + 1) * HG_CHUNK] for j in range(HG_NCH)]
    bl = _chunk_rows(last)
    e_b, e_nb, e_rem = jnp.exp(b), jnp.exp(-b), jnp.exp(bl - b)
    e_last = [jnp.exp(r) for r in last]
    return dict(lb=lb, sig=sig, f=f, sq=sq, q=q, k=k, e_b=e_b, e_nb=e_nb, e_rem=e_rem, e_last=e_last,
                qd=q * e_b, kd=k * e_nb, ku=k * e_rem)


def _hgrn_fwd(proj, lb_table, norm_w, mix_in, masks, comms=()):
    bsz, seq, _ = proj.shape
    nstep = seq // HG_TOK

    def body(tbl_ref, nw_ref, q_ref, f_ref, i_ref, g_ref, mix_ref, lower_ref, o_ref, rec_ref, st_ref, s_scr):
        @pl.when(pl.program_id(2) == 0)
        def _():
            s_scr[...] = jnp.zeros_like(s_scr)

        lower = lower_ref[...]
        for hp in range(HG_HPS):
            ls = slice(hp * LANE, (hp + 1) * LANE)
            v, hg = i_ref[:, ls], g_ref[:, ls]
            t = _hgrn_common(tbl_ref[:, ls], f_ref[:, ls], q_ref[:, ls])
            a = _dot(t["qd"], t["kd"], NT_DIMS) * lower
            o_intra = _dot(a, v)
            v_c, ku_c, qd_c = [_chunk_slices(z.astype(BF16)) for z in (v, t["ku"], t["qd"])]
            updates = [_dot(v_c[j], ku_c[j], TN_DIMS) for j in range(HG_NCH)]
            st = s_scr[hp]
            states = []
            for j in range(HG_NCH):
                states.append(st)
                st = st * t["e_last"][j] + updates[j]
            s_scr[hp] = st
            o = o_intra + jnp.concatenate([_dot(qd_c[j], states[j], NT_DIMS) for j in range(HG_NCH)], axis=0)
            st_ref[hp, 0] = states[0]
            o_ref[:, ls] = o
            y, _, _ = _rms_fwd(o, nw_ref[...])
            rec_ref[:, ls] = (y * (hg * _sigmoid(hg))).astype(BF16)

    width = HG_HPS * LANE
    slab = lambda first: pl.BlockSpec((None, HG_TOK, width), lambda b, h, t: (b, t, first // HG_HPS + h))
    head_out = pl.BlockSpec((None, HG_TOK, width), lambda b, h, t: (b, t, h))
    mix_out = pl.BlockSpec((None, HG_TOK, width), lambda b, h, t: (b, t, ATT_WIDTH // width + h))
    return _call(
        body, (lb_table, norm_w, proj, proj, proj, proj, mix_in, masks[0]), name="hgrn_fwd", grid=(bsz, HG_HEADS // HG_HPS, nstep),
        in_specs=[pl.BlockSpec((2, width), lambda b, h, t: (0, h)), pl.BlockSpec((1, LANE), lambda b, h, t: (0, 0)),
                  slab(HG_Q0), slab(HG_F0), slab(HG_I0), slab(HG_G0), pl.BlockSpec(memory_space=pl.ANY),
                  pl.BlockSpec((HG_TOK, HG_TOK), lambda b, h, t: (0, 0))],
        out_specs=[head_out, mix_out,
                   pl.BlockSpec((None, HG_HPS, 1, LANE, LANE), lambda b, h, t: (b, h, t, 0, 0))],
        out_shape=[_sds((bsz, seq, HG_WIDTH), F32), _sds(mix_in.shape, BF16),
                   _sds((bsz, HG_HEADS, nstep, LANE, LANE), F32)],
        scratch_shapes=[pltpu.VMEM((HG_HPS, LANE, LANE), F32)],
        sem=("parallel", "parallel", "arbitrary"), comms=comms, aliases={6: 1})


def _out_proj_fused(cat, w_out, x, post_w, g1, pre_w, sc2, sh2):
    tm = 512

    def epi(mix, ex, outs):
        x_ref, pw_ref, g1_ref, w2_ref, sc_ref, sh_ref = ex
        outs[0][...] = mix
        n1, _, _ = _rms_fwd(mix, pw_ref[...])
        x1 = x_ref[...] + g1_ref[...] * n1
        outs[1][...] = x1
        y2, _, _ = _rms_fwd(x1, w2_ref[...])
        outs[2][...] = (y2 * (1.0 + sc_ref[...]) + sh_ref[...]).astype(BF16)

    return _mm_rows(cat, w_out, name="out_proj", tm=tm, extra=(x, post_w, g1, pre_w, sc2, sh2),
                    extra_specs=[_tok_spec(tm), _vec_spec(), _row_spec(), _vec_spec(), _row_spec(), _row_spec()],
                    out_specs=[_tok_spec(tm), _tok_spec(tm), _tok_spec(tm)],
                    out_shape=[_sds(x.shape, F32), _sds(x.shape, F32), _sds(x.shape, BF16)], epi=epi)


def _acc_out(ref, first, value):
    @pl.when(first)
    def _():
        ref[...] = value

    @pl.when(jnp.logical_not(first))
    def _():
        ref[...] += value


def _down_proj_fused(r, w_down, x1, post_w, g2, target):
    tm = 512
    bsz = x1.shape[0]

    def pro(r_ref, ex, outs):
        rv = r_ref[...]
        return rv * rv

    def epi(down, ex, outs):
        x1_ref, w_ref, g2_ref, t_ref = ex
        loss_ref, dy_ref, dd_ref, dg2_ref, dw_ref = outs
        w, g2v = w_ref[...], g2_ref[...]
        n2, dh, rstd = _rms_fwd(down, w)
        err = x1_ref[...] + g2v * n2 - t_ref[...]
        part = (0.5 / D_MODEL) * jnp.sum(jnp.sum(err * err, axis=-1, keepdims=True), axis=0, keepdims=True)
        loss_ref[...] += jnp.broadcast_to(part, (1, LANE))
        dy = err * (1.0 / D_MODEL)
        dy_ref[...] = dy
        dg2_ref[...] += _colsum(dy * n2)
        dd, dw_rows = _rms_bwd(dy * g2v, dh, rstd, w)
        dd_ref[...] = dd.astype(BF16)
        dw_ref[...] += _colsum(dw_rows)

    return _mm_rows(r, w_down, name="down_proj", tm=tm, extra=(x1, post_w, g2, target),
                    extra_specs=[_tok_spec(tm), _vec_spec(), _row_spec(), _tok_spec(tm)],
                    out_specs=[_vec_spec(LANE), _tok_spec(tm), _tok_spec(tm), _row_spec(), _vec_spec()],
                    out_shape=[_sds((1, LANE), F32), _sds(x1.shape, F32), _sds(x1.shape, BF16), _sds((bsz, 1, D_MODEL), F32),
                               _sds((1, D_MODEL), F32)], pro=pro, epi=epi, parts=2, zero_per_seq=(3,), zero_once=(0, 4),
                    vmem_limit=VMEM_LIMIT_BIG)


def _up_bwd_fused(dpre, w_up4, dy, x1, mix, pre_w, sc2, post_w, g1, comms=()):
    tm = 512
    bsz = x1.shape[0]

    def epi(dh2v, ex, outs):
        dy_ref, x1_ref, mix_ref, w2_ref, sc_ref, pw_ref, g1_ref = ex
        dx1_ref, dmix_ref, dsc_ref, dsh_ref, dg1_ref, dw2_ref, dpw_ref = outs
        w2, pw = w2_ref[...], pw_ref[...]
        y2, xh2, rstd2 = _rms_fwd(x1_ref[...], w2)
        dsh_ref[...] += _colsum(dh2v)
        dsc_ref[...] += _colsum(dh2v * y2)
        dx1n, dw_rows = _rms_bwd(dh2v * (1.0 + sc_ref[...]), xh2, rstd2, w2)
        dw2_ref[...] += _colsum(dw_rows)
        dx1 = dy_ref[...] + dx1n
        dx1_ref[...] = dx1
        n1, mh, rstd1 = _rms_fwd(mix_ref[...], pw)
        dg1_ref[...] += _colsum(dx1 * n1)
        dmix, dpw_rows = _rms_bwd(dx1 * g1_ref[...], mh, rstd1, pw)
        dmix_ref[...] = dmix.astype(BF16)
        dpw_ref[...] += _colsum(dpw_rows)

    row_shape = _sds((bsz, 1, D_MODEL), F32)
    vec_shape = _sds((1, D_MODEL), F32)
    return _mm_rows(dpre, w_up4, name="up_bwd", tm=tm, extra=(dy, x1, mix, pre_w, sc2, post_w, g1),
                    extra_specs=[_tok_spec(tm), _tok_spec(tm), _tok_spec(tm), _vec_spec(), _row_spec(), _vec_spec(), _row_spec()],
                    out_specs=[_tok_spec(tm), _tok_spec(tm), _row_spec(), _row_spec(), _row_spec(), _vec_spec(), _vec_spec()],
                    out_shape=[_sds(x1.shape, F32), _sds(x1.shape, BF16), row_shape, row_shape, row_shape, vec_shape, vec_shape],
                    epi=epi, b_chunks=w_up4.shape[0], comms=comms, parts=2, zero_per_seq=(2, 3, 4), zero_once=(5, 6),
                    vmem_limit=VMEM_LIMIT_BIG)


def _norm1_bwd(dh1, dx1, x, pre_w, sc1, tm=512, comms=()):
    bsz, seq, _ = x.shape

    def body(dh_ref, dx1_ref, x_ref, w_ref, sc_ref, gx_ref, dsc_ref, dsh_ref, dw_ref):
        b, i = pl.program_id(0), pl.program_id(1)
        w = w_ref[...]
        dh = dh_ref[...]
        y, xh, rstd = _rms_fwd(x_ref[...], w)
        _acc_out(dsh_ref, i == 0, _colsum(dh))
        _acc_out(dsc_ref, i == 0, _colsum(dh * y))
        dx, dw_rows = _rms_bwd(dh * (1.0 + sc_ref[...]), xh, rstd, w)
        _acc_out(dw_ref, jnp.logical_and(b == 0, i == 0), _colsum(dw_rows))
        gx_ref[...] = dx1_ref[...] + dx

    row_shape = _sds((bsz, 1, D_MODEL), F32)
    return _call(
        body, (dh1, dx1, x, pre_w, sc1), name="norm1_bwd", grid=(bsz, seq // tm),
        in_specs=[_tok_spec(tm), _tok_spec(tm), _tok_spec(tm), _vec_spec(), _row_spec()],
        out_specs=[_tok_spec(tm), _row_spec(), _row_spec(), _vec_spec()],
        out_shape=[_sds(x.shape, F32), row_shape, row_shape, _sds((1, D_MODEL), F32)],
        sem=("arbitrary", "arbitrary"), comms=comms)


def _hgrn_bwd(dcat, proj, o_raw, states, lb_table, norm_w, masks, comms=()):
    bsz, seq, _ = proj.shape
    nstep = seq // HG_TOK
    rec0 = ATT_WIDTH // LANE
    width = HG_HPS * LANE
    slabs = (HG_Q0, HG_F0, HG_I0, HG_G0)
    n_steps = (HG_HEADS // HG_HPS) * bsz * nstep
    assert n_steps >= 2

    def body(tbl_ref, nw_ref, dr_ref, q_ref, f_ref, i_ref, g_ref, o_ref, st_ref, lower_ref, upper_ref,
             dproj_ref, dlb_ref, dnw_ref, ds_scr, grad_buf, grad_sem):
        h, b, t = pl.program_id(0), pl.program_id(1), pl.program_id(2)
        step = (h * bsz + b) * nstep + t
        slot = step % 2
        dq_k, df_k, di_k, dg_k = range(4)

        def grad_copies(of_step):
            hh, bb, tt = of_step // (bsz * nstep), (of_step // nstep) % bsz, of_step % nstep
            rows = pl.ds(pl.multiple_of((nstep - 1 - tt) * HG_TOK, HG_TOK), HG_TOK)
            return [pltpu.make_async_copy(
                grad_buf.at[of_step % 2, k],
                dproj_ref.at[bb, rows, pl.ds(pl.multiple_of(slabs[k] * LANE + hh * width, width), width)],
                grad_sem.at[of_step % 2, k]) for k in range(4)]

        @pl.when(step >= 2)
        def _():
            for cp in grad_copies(step - 2):
                cp.wait()

        @pl.when(t == 0)
        def _():
            ds_scr[...] = jnp.zeros_like(ds_scr)

        lower, upper = lower_ref[...], upper_ref[...]
        dlb_parts = []
        dnw_acc = jnp.zeros((1, LANE), F32)
        for hp in range(HG_HPS):
            ls = slice(hp * LANE, (hp + 1) * LANE)
            hq, v, hg = q_ref[:, ls], i_ref[:, ls], g_ref[:, ls]
            nw = nw_ref[...]
            c = _hgrn_common(tbl_ref[:, ls], f_ref[:, ls], hq)
            qd, kd, ku = c["qd"], c["kd"], c["ku"]
            y, on, rstd = _rms_fwd(o_ref[:, ls], nw)
            sg = _sigmoid(hg)
            dr = dr_ref[:, ls]
            grad_buf[slot, dg_k, :, ls] = (dr * y * (sg * (1.0 + hg * (1.0 - sg)))).astype(BF16)
            do, dnw_rows = _rms_bwd(dr * (hg * sg), on, rstd, nw)
            at = _dot(kd, qd, NT_DIMS) * upper
            da = _dot(do, v, NT_DIMS) * lower
            dat = _dot(v, do, NT_DIMS) * upper
            dv = _dot(at, do)
            dqd = _dot(da, kd)
            dkd = _dot(dat, qd)
            do_c, qd_c, v_c, ku_c = [_chunk_slices(z.astype(BF16)) for z in (do, qd, v, ku)]
            outer = [_dot(do_c[j], qd_c[j], TN_DIMS) for j in range(HG_NCH)]
            ds = ds_scr[hp]
            ds_after = [None] * HG_NCH
            for j in reversed(range(HG_NCH)):
                ds_after[j] = ds
                ds = outer[j] + ds * c["e_last"][j]
            ds_scr[hp] = ds
            updates = [_dot(v_c[j], ku_c[j], TN_DIMS) for j in range(HG_NCH)]
            states = [st_ref[hp, 0]]
            for j in range(HG_NCH - 1):
                states.append(states[j] * c["e_last"][j] + updates[j])
            dv = dv + jnp.concatenate([_dot(ku_c[j], ds_after[j], NT_DIMS) for j in range(HG_NCH)], axis=0)
            dqd = dqd + jnp.concatenate([_dot(do_c[j], states[j]) for j in range(HG_NCH)], axis=0)
            dku = jnp.concatenate([_dot(v_c[j], ds_after[j]) for j in range(HG_NCH)], axis=0)
            dku_ku = dku * ku
            dbl = [_colsum(states[j] * ds_after[j]) * c["e_last"][j] + _colsum(dku_ku[j * HG_CHUNK:(j + 1) * HG_CHUNK])
                   for j in range(HG_NCH)]
            dk = dkd * c["e_nb"] + dku * c["e_rem"]
            db = dqd * qd - dkd * kd - dku_ku + jnp.where(_row_in_chunk() == HG_CHUNK - 1, _chunk_rows(dbl), 0.0)
            dfv = _chunk_cumsum(db, reverse=True) / c["f"] - dk
            sig, sq = c["sig"], c["sq"]
            grad_buf[slot, df_k, :, ls] = (dfv * (1.0 - c["lb"]) * sig * (1.0 - sig)).astype(BF16)
            grad_buf[slot, dq_k, :, ls] = (dqd * c["e_b"] * (sq * (1.0 + hq * (1.0 - sq)))).astype(BF16)
            grad_buf[slot, di_k, :, ls] = dv.astype(BF16)
            dlb_parts.append(_colsum(dfv * (1.0 - sig)))
            dnw_acc = dnw_acc + _colsum(dnw_rows)
        _acc_out(dlb_ref, jnp.logical_and(b == 0, t == 0), jnp.concatenate(dlb_parts, axis=1))
        _acc_out(dnw_ref, jnp.logical_and(h == 0, jnp.logical_and(b == 0, t == 0)), dnw_acc)
        for cp in grad_copies(step):
            cp.start()

        @pl.when(step == n_steps - 1)
        def _():
            for cp in grad_copies(step - 1) + grad_copies(step):
                cp.wait()

    rev = lambda t: nstep - 1 - t
    slab = lambda first: pl.BlockSpec((None, HG_TOK, width), lambda h, b, t: (b, rev(t), first // HG_HPS + h))
    head = pl.BlockSpec((None, HG_TOK, width), lambda h, b, t: (b, rev(t), h))
    return _call(
        body, (lb_table, norm_w, dcat, proj, proj, proj, proj, o_raw, states, *masks), name="hgrn_bwd",
        grid=(HG_HEADS // HG_HPS, bsz, nstep),
        in_specs=[pl.BlockSpec((2, width), lambda h, b, t: (0, h)), pl.BlockSpec((1, LANE), lambda h, b, t: (0, 0)),
                  slab(rec0), slab(HG_Q0), slab(HG_F0), slab(HG_I0), slab(HG_G0), head,
                  pl.BlockSpec((None, HG_HPS, 1, LANE, LANE), lambda h, b, t: (b, h, rev(t), 0, 0)),
                  pl.BlockSpec((HG_TOK, HG_TOK), lambda h, b, t: (0, 0)), pl.BlockSpec((HG_TOK, HG_TOK), lambda h, b, t: (0, 0))],
        out_specs=[pl.BlockSpec(memory_space=pl.ANY), pl.BlockSpec((1, width), lambda h, b, t: (0, h)),
                   pl.BlockSpec((1, LANE), lambda h, b, t: (0, 0))],
        out_shape=[_sds((bsz, seq, IN_COLS), BF16), _sds((1, HG_WIDTH), F32), _sds((1, LANE), F32)],
        scratch_shapes=[pltpu.VMEM((HG_HPS, LANE, LANE), F32), pltpu.VMEM((2, 4, HG_TOK, width), BF16),
                        pltpu.SemaphoreType.DMA((2, 4))],
        sem=("arbitrary", "arbitrary", "arbitrary"), comms=comms)


def _attn_bwd(dcat, raw, w_norm, qh, kh, vh, lse, sinks, tables, biases, dproj, comms=()):
    bsz, nblk = qh.shape[0], qh.shape[1]
    seq = nblk * WINDOW
    nstep = nblk // ATT_BPS
    half = ROPE_DIM // 2

    def body(sink_ref, da_ref, raw_ref, w_ref, q_ref, kc_ref, kp_ref, vc_ref, vp_ref, l_ref, c_ref, u_ref, d_ref,
             full_ref, first_ref, dproj_ref, o_ref, dw_ref, dsink_ref, carry_k, carry_v):
        b, i = pl.program_id(0), pl.program_id(1)
        first = jnp.logical_and(b == 0, i == 0)

        @pl.when(i == 0)
        def _():
            carry_k[...] = jnp.zeros_like(carry_k)
            carry_v[...] = jnp.zeros_like(carry_v)

        w = w_ref[...]
        _, on, rstd = _rms_fwd(raw_ref[...], w)
        do_step, dw_rows = _rms_bwd(da_ref[...], on, rstd, w)
        _acc_out(dw_ref, first, _colsum(dw_rows))
        lane8 = lax.broadcasted_iota(jnp.int32, (1, ATT_Q_HEADS), 1)
        dsink = jnp.zeros((1, ATT_Q_HEADS), F32)
        from_next_k, from_next_v = carry_k[...], carry_v[...]
        for blk in reversed(range(ATT_BPS)):
            tok = slice(blk * WINDOW, (blk + 1) * WINDOW)
            bias = _band_bias(full_ref, first_ref, True if blk else i < nstep - 1)
            raw_v, do_all = raw_ref[tok, :], do_step[tok]
            c, u, d = c_ref[tok, :], u_ref[tok, :], d_ref[tok, :]

            def unrope(g):
                return (g * c + pltpu.roll(g * u, LANE - half, 1) + pltpu.roll(g * d, half, 1)).astype(BF16)

            dq_parts, dk_own, dk_before, dv_own, dv_before = [], [], [], [], []
            for g in range(ATT_KV_HEADS):
                heads = [slice((g * ATT_GROUP + hh) * ATT_HEAD_DIM, (g * ATT_GROUP + hh + 1) * ATT_HEAD_DIM)
                         for hh in range(ATT_GROUP)]
                q = q_ref[blk, g]
                keys, vals = _band(kp_ref, kc_ref, g, blk), _band(vp_ref, vc_ref, g, blk)
                do_g = jnp.concatenate([do_all[:, hs] for hs in heads], axis=0)
                dsum = jnp.concatenate([jnp.sum(do_all[:, hs] * raw_v[:, hs], axis=-1, keepdims=True) for hs in heads], axis=0)
                lse_g = jnp.concatenate([l_ref[tok, g * ATT_GROUP + hh:g * ATT_GROUP + hh + 1] for hh in range(ATT_GROUP)], axis=0)
                p = jnp.exp(_dot(q, keys, NT_DIMS) * ATT_SCALE + bias - lse_g)
                sink_part = jnp.exp(_sink_column(sink_ref, g) - lse_g) * dsum
                for hh in range(ATT_GROUP):
                    head_sum = jnp.sum(sink_part[hh * WINDOW:(hh + 1) * WINDOW], axis=0, keepdims=True)
                    dsink = dsink - jnp.where(lane8 == g * ATT_GROUP + hh, head_sum, 0.0)
                ds = p * (_dot(do_g, vals, NT_DIMS) - dsum) * ATT_SCALE
                dq_g = _dot(ds, keys)
                dq_parts += [dq_g[hh * WINDOW:(hh + 1) * WINDOW] for hh in range(ATT_GROUP)]
                dk_g = _dot(ds, q, TN_DIMS)
                dv_g = _dot(p, do_g, TN_DIMS)
                dk_before.append(dk_g[:WINDOW])
                dk_own.append(dk_g[WINDOW:])
                dv_before.append(dv_g[:WINDOW])
                dv_own.append(dv_g[WINDOW:])
            per_slab = LANE // ATT_HEAD_DIM
            for s in range(ATT_WIDTH // LANE):
                slab = jnp.concatenate(dq_parts[s * per_slab:(s + 1) * per_slab], axis=1)
                o_ref[tok, s * LANE:(s + 1) * LANE] = unrope(slab)
            o_ref[tok, ATT_WIDTH:ATT_WIDTH + LANE] = unrope(jnp.concatenate(dk_own, axis=1) + from_next_k)
            o_ref[tok, ATT_WIDTH + LANE:ATT_COLS] = (jnp.concatenate(dv_own, axis=1) + from_next_v).astype(BF16)
            from_next_k, from_next_v = jnp.concatenate(dk_before, axis=1), jnp.concatenate(dv_before, axis=1)
        carry_k[...] = from_next_k
        carry_v[...] = from_next_v
        _acc_out(dsink_ref, first, dsink)

    rows = ATT_BPS * WINDOW
    rev = lambda i: nstep - 1 - i
    cur = lambda width: pl.BlockSpec((None, rows, width), lambda b, i: (b, rev(i), 0))
    q_spec = pl.BlockSpec((None, ATT_BPS, ATT_KV_HEADS, GROUP_ROWS, ATT_HEAD_DIM), lambda b, i: (b, rev(i), 0, 0, 0))
    kv_cur = pl.BlockSpec((None, ATT_KV_HEADS, rows, ATT_HEAD_DIM), lambda b, i: (b, 0, rev(i), 0))
    kv_prev = pl.BlockSpec((None, ATT_KV_HEADS, WINDOW, ATT_HEAD_DIM), lambda b, i: (b, 0, jnp.maximum(ATT_BPS * rev(i) - 1, 0), 0))
    tab = pl.BlockSpec((rows, LANE), lambda b, i: (rev(i), 0))
    return _call(
        body, (sinks, dcat, raw, w_norm, qh, kh, kh, vh, vh, lse, *tables, *biases, dproj), name="attn_bwd", grid=(bsz, nstep),
        in_specs=[pl.BlockSpec(memory_space=pltpu.SMEM), cur(ATT_WIDTH), cur(ATT_WIDTH), _vec_spec(ATT_WIDTH), q_spec,
                  kv_cur, kv_prev, kv_cur, kv_prev, cur(ATT_Q_HEADS), tab, tab, tab, _bias_spec(), _bias_spec(),
                  pl.BlockSpec(memory_space=pl.ANY)],
        out_specs=[cur(ATT_COLS), _vec_spec(ATT_WIDTH), _vec_spec(ATT_Q_HEADS)],
        out_shape=[_sds(dproj.shape, BF16), _sds((1, ATT_WIDTH), F32), _sds((1, ATT_Q_HEADS), F32)],
        scratch_shapes=[pltpu.VMEM((WINDOW, LANE), F32), pltpu.VMEM((WINDOW, LANE), F32)],
        sem=("arbitrary", "arbitrary"), comms=comms, aliases={15: 0})


def _other_chips(x, y):
    return [(1 - x, y), (x, 1 - y), (1 - x, 1 - y)]


def _sem_pair(n):
    return [pltpu.SemaphoreType.DMA((n,)), pltpu.SemaphoreType.DMA((n,))]


def _plan_pair_forward(bufs):
    n = len(bufs)

    def copies(outs, sems):
        x, y, c = _mesh_pos()
        sends, lands = [], []
        for a in range(n):
            for j, chip in enumerate(_other_chips(x, y)):
                k = 3 * a + j
                slot = outs[a].at[4 * chip[0] + 2 * chip[1] + c]
                sends.append(pltpu.make_async_remote_copy(
                    src_ref=slot, dst_ref=slot, send_sem=sems[0].at[k], recv_sem=sems[1].at[k],
                    device_id=(x, y, 1 - c), device_id_type=MESH))
                theirs = outs[a].at[4 * chip[0] + 2 * chip[1] + 1 - c]
                lands.append(pltpu.make_async_remote_copy(
                    src_ref=theirs, dst_ref=theirs, send_sem=sems[0].at[k], recv_sem=sems[1].at[k],
                    device_id=(x, y, 1 - c), device_id_type=MESH))
        return sends, lands

    def start(ins, outs, sems):
        for cp in copies(outs, sems)[0]:
            cp.start()

    def finish(ins, outs, sems):
        sends, lands = copies(outs, sems)
        for cp in lands:
            cp.wait_recv()
        for cp in sends:
            cp.wait_send()

    return _Comm(list(bufs), [_sds(b.shape, b.dtype) for b in bufs], _sem_pair(3 * n), start, finish,
                 aliases=[(a, a) for a in range(n)])


def _plan_pair(arrays, other_half):
    n = len(arrays)
    per = N_CHIPS if other_half == "chip_major" else 1

    def copies(ins, outs, sems):
        x, y, c = _mesh_pos()
        out = []
        for a in range(n):
            for k in range(per):
                if other_half == "chip_major":
                    src, dst = ins[a].at[k, 1 - c], outs[a].at[k]
                else:
                    src, dst = (ins[a].at[1 - c] if other_half else ins[a]), outs[a]
                out.append(pltpu.make_async_remote_copy(
                    src_ref=src, dst_ref=dst, send_sem=sems[0].at[per * a + k], recv_sem=sems[1].at[per * a + k],
                    device_id=(x, y, 1 - c), device_id_type=MESH))
        return out

    def start(ins, outs, sems):
        for cp in copies(ins, outs, sems):
            cp.start()

    def finish(ins, outs, sems):
        for cp in copies(ins, outs, sems):
            cp.wait()

    if other_half == "chip_major":
        shapes = [_sds((a.shape[0],) + a.shape[2:], a.dtype) for a in arrays]
    else:
        shapes = [_sds(a.shape[1:] if other_half else a.shape, a.dtype) for a in arrays]
    return _Comm(list(arrays), shapes, _sem_pair(per * n), start, finish)


def _plan_chip_exchange(arrays):
    n = len(arrays)

    def copies(ins, outs, sems):
        x, y, c = _mesh_pos()
        sends, lands = [], []
        for a in range(n):
            for j, chip in enumerate(_other_chips(x, y)):
                k = 3 * a + j
                sends.append(pltpu.make_async_remote_copy(
                    src_ref=ins[a].at[2 * chip[0] + chip[1]], dst_ref=outs[a].at[2 * x + y], send_sem=sems[0].at[k],
                    recv_sem=sems[1].at[k], device_id=(*chip, c), device_id_type=MESH))
                slot = outs[a].at[2 * chip[0] + chip[1]]
                lands.append(pltpu.make_async_remote_copy(
                    src_ref=slot, dst_ref=slot, send_sem=sems[0].at[k], recv_sem=sems[1].at[k],
                    device_id=(*chip, c), device_id_type=MESH))
        return sends, lands

    def start(ins, outs, sems):
        for cp in copies(ins, outs, sems)[0]:
            cp.start()

    def finish(ins, outs, sems):
        sends, lands = copies(ins, outs, sems)
        for cp in lands:
            cp.wait_recv()
        for cp in sends:
            cp.wait_send()

    return _Comm(list(arrays), [_sds(a.shape, a.dtype) for a in arrays], _sem_pair(3 * n), start, finish)


SEM_SPEC = pl.BlockSpec(memory_space=pltpu.SEMAPHORE)
N_OTHER = N_CHIPS - 1


def _exchange_copies(s_ref, land_ref, sems):
    x, y, c = _mesh_pos()
    return [pltpu.make_async_remote_copy(
        src_ref=s_ref.at[2 * chip[0] + chip[1]], dst_ref=land_ref.at[2 * x + y], send_sem=sems[j], recv_sem=sems[N_OTHER + j],
        device_id=(*chip, c), device_id_type=MESH) for j, chip in enumerate(_other_chips(x, y))]


def _exchange_start(s, name):
    def body(s_ref, land_ref, *outs):
        sems, token = outs[:2 * N_OTHER], outs[-1]
        for cp in _exchange_copies(s_ref, land_ref, sems):
            cp.start()
        token[...] = jnp.zeros_like(token)

    hbm = pltpu.HBM(s.shape, s.dtype)
    res = pl.pallas_call(
        body, name=name,
        out_shape=(pltpu.SemaphoreType.DMA(()),) * (2 * N_OTHER) + (hbm, hbm, _sds((SUBLANES, LANE), F32)),
        in_specs=(HBM_SPEC, HBM_SPEC),
        out_specs=(SEM_SPEC,) * (2 * N_OTHER) + (HBM_SPEC, HBM_SPEC, pl.BlockSpec(memory_space=pltpu.VMEM)),
        input_output_aliases={0: 2 * N_OTHER, 1: 2 * N_OTHER + 1},
        compiler_params=pltpu.CompilerParams(has_side_effects=pltpu.SideEffectType.DATAFLOW_SIDE_EFFECTING),
    )(pltpu.with_memory_space_constraint(s, pltpu.HBM), pltpu.with_memory_space_constraint(lax.empty(s.shape, s.dtype), pltpu.HBM))
    return res[:2 * N_OTHER], res[2 * N_OTHER], res[2 * N_OTHER + 1], res[-1]


def _exchange_wait(sems, s_thru, land_thru, afters, name):
    def body(s_ref, land_ref, *rest):
        for cp in _exchange_copies(s_ref, land_ref, rest[:2 * N_OTHER]):
            cp.wait_send()
            cp.wait_recv()

    hbm = pltpu.HBM(s_thru.shape, s_thru.dtype)
    return pl.pallas_call(
        body, name=name, out_shape=(hbm, hbm),
        in_specs=(HBM_SPEC, HBM_SPEC) + (SEM_SPEC,) * (2 * N_OTHER) + (pl.BlockSpec(memory_space=pl.ANY),) * len(afters),
        out_specs=(HBM_SPEC, HBM_SPEC), input_output_aliases={0: 0, 1: 1},
        compiler_params=pltpu.CompilerParams(has_side_effects=pltpu.SideEffectType.DATAFLOW_SIDE_EFFECTING),
    )(s_thru, land_thru, *sems, *afters)


def _gather_copies(block_ref, buf_ref, sems):
    x, y, c = _mesh_pos()
    return [pltpu.make_async_remote_copy(
        src_ref=block_ref, dst_ref=buf_ref.at[4 * x + 2 * y + c], send_sem=sems[j], recv_sem=sems[N_OTHER + j],
        device_id=(*chip, c), device_id_type=MESH) for j, chip in enumerate(_other_chips(x, y))]


def _gather_start(blocks, bufs, afters, name):
    n = len(blocks)
    per = 2 * N_OTHER

    def body(*refs):
        ins, outs = refs[:2 * n], refs[2 * n + len(afters):]
        for a in range(n):
            for cp in _gather_copies(ins[a], ins[n + a], outs[a * per:(a + 1) * per]):
                cp.start()
        outs[-1][...] = jnp.zeros_like(outs[-1])

    hbm = [pltpu.HBM(z.shape, z.dtype) for z in list(blocks) + list(bufs)]
    res = pl.pallas_call(
        body, name=name,
        out_shape=(pltpu.SemaphoreType.DMA(()),) * (n * per) + tuple(hbm) + (_sds((SUBLANES, LANE), F32),),
        in_specs=(HBM_SPEC,) * (2 * n) + (pl.BlockSpec(memory_space=pl.ANY),) * len(afters),
        out_specs=(SEM_SPEC,) * (n * per) + (HBM_SPEC,) * (2 * n) + (pl.BlockSpec(memory_space=pltpu.VMEM),),
        input_output_aliases={k: n * per + k for k in range(2 * n)},
        compiler_params=pltpu.CompilerParams(has_side_effects=pltpu.SideEffectType.DATAFLOW_SIDE_EFFECTING),
    )(*[pltpu.with_memory_space_constraint(z, pltpu.HBM) for z in list(blocks) + list(bufs)], *afters)
    parts = [(res[a * per:(a + 1) * per], res[n * per + a], res[n * per + n + a]) for a in range(n)]
    return parts, res[-1]


def _gather_wait(part, afters, name):
    sems, block, buf = part

    def body(block_ref, buf_ref, *rest):
        for cp in _gather_copies(block_ref, buf_ref, rest[:2 * N_OTHER]):
            cp.wait_send()
            cp.wait_recv()

    return pl.pallas_call(
        body, name=name, out_shape=(pltpu.HBM(block.shape, block.dtype), pltpu.HBM(buf.shape, buf.dtype)),
        in_specs=(HBM_SPEC, HBM_SPEC) + (SEM_SPEC,) * (2 * N_OTHER) + (pl.BlockSpec(memory_space=pl.ANY),) * len(afters),
        out_specs=(HBM_SPEC, HBM_SPEC), input_output_aliases={0: 0, 1: 1},
        compiler_params=pltpu.CompilerParams(has_side_effects=pltpu.SideEffectType.DATAFLOW_SIDE_EFFECTING),
    )(block, buf, *sems, *afters)[1]


def _comm_only(comms, name):
    return _call(lambda: None, (), name=name, grid=(), in_specs=[], out_specs=[], out_shape=[], sem=(), comms=comms)[1]


def _allgather8(arrays, name):
    return _comm_only([_plan_allgather8(arrays)], name)[0]


def _plan_allgather8(arrays):
    n = len(arrays)

    def parts(ins, outs, sems):
        send_sems, recv_sems, local_sems = sems
        x, y, c = _mesh_pos()
        me, sibling = (x, y, c), (x, y, 1 - c)
        chips = _other_chips(x, y)

        def copy(a, k, block, to, src=None):
            dst = outs[a].at[4 * block[0] + 2 * block[1] + block[2]]
            return pltpu.make_async_remote_copy(
                src_ref=dst if src is None else src, dst_ref=dst, send_sem=send_sems.at[7 * a + k],
                recv_sem=recv_sems.at[7 * a + k], device_id=to, device_id_type=MESH)

        mine = [pltpu.make_async_copy(ins[a], outs[a].at[4 * x + 2 * y + c], local_sems.at[a]) for a in range(n)]
        first = []
        for a in range(n):
            first.append(copy(a, 0, me, sibling, src=ins[a]))
            first += [copy(a, 1 + j, me, (*chip, c), src=ins[a]) for j, chip in enumerate(chips)]
        return copy, mine, first, me, sibling, chips, c

    def start(ins, outs, sems):
        _, mine, first, *_ = parts(ins, outs, sems)
        for cp in mine + first:
            cp.start()

    def finish(ins, outs, sems):
        copy, mine, first, me, sibling, chips, c = parts(ins, outs, sems)
        passed = []
        for j, chip in enumerate(chips):
            for a in range(n):
                copy(a, 1 + j, (*chip, c), me).wait_recv()
                fwd = copy(a, 4 + j, (*chip, c), sibling)
                fwd.start()
                passed.append(fwd)
        for a in range(n):
            copy(a, 0, sibling, me).wait_recv()
            for j, chip in enumerate(chips):
                copy(a, 4 + j, (*chip, 1 - c), me).wait_recv()
        for cp in first + passed:
            cp.wait_send()
        for cp in mine:
            cp.wait()

    sems = [pltpu.SemaphoreType.DMA((7 * n,)), pltpu.SemaphoreType.DMA((7 * n,)), pltpu.SemaphoreType.DMA((n,))]
    return _Comm(list(arrays), [_sds((N_DEV,) + a.shape, a.dtype) for a in arrays], sems, start, finish)


def _pair_sum(g, q, core, name, chip_major=False):
    rows, cols = g.shape[2:]
    tr = _row_tile(rows)

    def body(core_ref, g_ref, q_ref, o_ref):
        o_ref[...] = (g_ref[...] + q_ref[...]).astype(BF16)

    blk = pl.BlockSpec((None, tr, cols), lambda k, i, core_ref: (k, i, 0))
    if chip_major:
        own = pl.BlockSpec((None, None, tr, cols), lambda k, i, core_ref: (k, core_ref[0], i, 0))
    else:
        own = pl.BlockSpec((None, None, tr, cols), lambda k, i, core_ref: (core_ref[0], k, i, 0))
    return pl.pallas_call(
        body, name=name,
        grid_spec=pltpu.PrefetchScalarGridSpec(num_scalar_prefetch=1, grid=(N_CHIPS, rows // tr), in_specs=[own, blk], out_specs=blk),
        out_shape=_sds((N_CHIPS, rows, cols), BF16), compiler_params=_params("parallel", "parallel"),
    )(core, g, q)


def _sum_chips(own, landed, chip, name):
    _, rows, cols = own.shape
    tr = _row_tile(rows)

    def body(chip_ref, own_ref, a_ref, b_ref, c_ref, o_ref):
        acc = own_ref[...].astype(F32) + a_ref[...].astype(F32)
        o_ref[...] = (acc + b_ref[...].astype(F32)) + c_ref[...].astype(F32)

    blk = lambda flip: pl.BlockSpec((None, tr, cols), lambda i, chip_ref: (jnp.bitwise_xor(chip_ref[0], flip), i, 0))
    return pl.pallas_call(
        body, name=name,
        grid_spec=pltpu.PrefetchScalarGridSpec(num_scalar_prefetch=1, grid=(rows // tr,), in_specs=[blk(0), blk(1), blk(2), blk(3)],
                                               out_specs=pl.BlockSpec((tr, cols), lambda i, chip_ref: (i, 0))),
        out_shape=_sds((rows, cols), F32), compiler_params=_params("parallel"),
    )(chip, own, landed, landed, landed)


SUBLANES = 8


def _tile_rows(n_elems):
    return -(-n_elems // (SUBLANES * LANE)) * SUBLANES


SMALL_ITEMS = (("b_ada", N_MOD * D_MODEL), ("pre_w_mix", D_MODEL), ("post_w_mix", D_MODEL), ("pre_w_mlp", D_MODEL),
               ("post_w_mlp", D_MODEL), ("attn_out_w", ATT_WIDTH), ("hg_norm_w", HG_HEAD_DIM), ("attn_sinks", ATT_Q_HEADS),
               ("lb_0", HG_WIDTH), ("lb_1", HG_WIDTH))
SMALL_AT = {}
for _name, _size in SMALL_ITEMS:
    SMALL_AT[_name] = (sum(r for _, r in SMALL_AT.values()), _tile_rows(_size))
SMALL_ROWS = sum(r for _, r in SMALL_AT.values())
MOD_ROWS = SMALL_AT["b_ada"][1]
PLAIN_ROWS = SMALL_AT["lb_0"][0] - MOD_ROWS
LB_ROWS = SMALL_AT["lb_0"][1]


def _rows(a, nrows=None):
    flat = a.reshape(-1)
    nrows = _tile_rows(flat.shape[0]) if nrows is None else nrows
    return jnp.pad(flat, (0, nrows * LANE - flat.shape[0])).reshape(nrows, LANE)


def _pack_small(vals):
    vals = dict(vals, lb_0=vals["lb_table"][0], lb_1=vals["lb_table"][1])
    return jnp.concatenate([_rows(vals[name], SMALL_AT[name][1]) for name, _ in SMALL_ITEMS], axis=0)


def _unpack_small(p):
    def item(name, shape):
        first = SMALL_AT[name][0]
        size = shape[0] * shape[1]
        return p[first:first + SMALL_AT[name][1]].reshape(-1)[:size].reshape(shape)

    out = {name: item(name, (1, size)) for name, size in SMALL_ITEMS if not name.startswith("lb_")}
    out["lb_table"] = jnp.concatenate([item("lb_0", (1, HG_WIDTH)), item("lb_1", (1, HG_WIDTH))], axis=0)
    return out


def _pack_partials(dmod, plain, d_lb, loss_row):
    return jnp.concatenate([_rows(dmod, dmod.shape[0] * MOD_ROWS)] + [_rows(g) for g in plain] + [_rows(d_lb), _rows(loss_row)], axis=0)


def _small_update(packs, w, m, v, n_seq):
    mod_end = n_seq * MOD_ROWS
    lb_at = mod_end + PLAIN_ROWS
    t0, t1 = SMALL_AT["lb_0"][0], SMALL_AT["lb_1"][0]

    def body(p_ref, w_ref, m_ref, v_ref, g_ref, dl_ref, nm_ref, nv_ref, loss_ref):
        tot = p_ref[0]
        for d in range(1, N_DEV):
            tot = tot + p_ref[d]
        wv = w_ref[...]
        p1 = _sigmoid(wv[t1:t1 + LB_ROWS] - wv[t0:t0 + LB_ROWS])
        s = tot[lb_at:lb_at + LB_ROWS] * p1 * (1.0 - p1)
        g_bias = tot[0:MOD_ROWS]
        for q in range(1, n_seq):
            g_bias = g_bias + tot[q * MOD_ROWS:(q + 1) * MOD_ROWS]
        g = jnp.concatenate([g_bias, tot[mod_end:lb_at], -s, s], axis=0)
        g_ref[...] = g
        dl_ref[...], nm_ref[...], nv_ref[...] = _adamw_math(g, wv, m_ref[...], v_ref[...])
        loss_ref[...] = tot[lb_at + LB_ROWS:lb_at + LB_ROWS + SUBLANES]

    shp = _sds((SMALL_ROWS, LANE), F32)
    return pl.pallas_call(body, name="small_update", out_shape=[shp] * 4 + [_sds((SUBLANES, LANE), F32)],
                          compiler_params=_params())(packs, w, m, v)


def kernel(x, c, w_ada, b_ada, pre_w_mix, w_in, attn_sinks, attn_out_w, lb_table, hg_norm_w, w_out, post_w_mix, pre_w_mlp, w_up, w_down, post_w_mlp, loss_target, m_w_ada, m_b_ada, m_pre_w_mix, m_w_in, m_attn_sinks, m_attn_out_w, m_lb_table, m_hg_norm_w, m_w_out, m_post_w_mix, m_pre_w_mlp, m_w_up, m_w_down, m_post_w_mlp, v_w_ada, v_b_ada, v_pre_w_mix, v_w_in, v_attn_sinks, v_attn_out_w, v_lb_table, v_hg_norm_w, v_w_out, v_post_w_mix, v_pre_w_mlp, v_w_up, v_w_down, v_post_w_mlp):
    xi, yi, ci = _mesh_pos()
    chip = 2 * xi + yi
    dev = 2 * chip + ci
    bsz, seq, _ = x.shape
    ntok = bsz * seq
    ada_cols = w_ada.shape[2]
    core = jnp.reshape(ci, (1,)).astype(jnp.int32)
    chip_idx = jnp.reshape(chip, (1,)).astype(jnp.int32)
    flat = lambda a: a.reshape(ntok, a.shape[-1])
    unflat = lambda a: a.reshape(bsz, seq, a.shape[-1])
    tables = _rope_tables(seq)
    biases, chunk_masks = _band_biases(), _block_masks()

    def row_half(w):
        rows = w.shape[1] // 2
        return lax.dynamic_slice_in_dim(w[0], ci * rows, rows, axis=0).astype(BF16)

    def gather_buffer(w):
        rows, cols = w.shape[1] // 2, w.shape[2]
        own = w[0].astype(BF16).reshape(2, rows, cols)
        return lax.dynamic_update_slice(lax.empty((N_DEV, rows, cols), BF16), own, (2 * chip, 0, 0))

    w_in_t, m_in_t, v_in_t = [jnp.transpose(a[0])[None] for a in (w_in, m_w_in, v_w_in)]
    c_g, in_g = _allgather8([c, row_half(w_in_t)], "gather_first")
    c_all = c_g.reshape(N_DEV * bsz, D_MODEL)
    w_in_full = in_g.reshape(IN_COLS, D_MODEL)

    b_cols = lax.dynamic_slice_in_dim(b_ada, chip * ada_cols, ada_cols, axis=1)
    mod_part = _ada_fwd(c_all, w_ada[0], b_cols)
    half_rows = mod_part.shape[0] // 2
    (mod_g,) = _allgather8([lax.dynamic_slice_in_dim(mod_part, ci * half_rows, half_rows, axis=0)], "gather_mod")
    mod_all = mod_g.reshape(N_CHIPS, 2, half_rows, ada_cols).transpose(1, 2, 0, 3).reshape(N_DEV * bsz, N_MOD * D_MODEL)
    mod = lax.dynamic_slice_in_dim(mod_all, dev * bsz, bsz, axis=0)
    sh1, sc1, g1, sh2, sc2, g2 = [mod[:, i * D_MODEL:(i + 1) * D_MODEL].reshape(bsz, 1, D_MODEL) for i in range(N_MOD)]

    weights = (w_out, w_up, w_down)
    (out_part, up_part, down_part), started = _gather_start(
        [row_half(w) for w in weights], [gather_buffer(w) for w in weights], [mod_g], "gather_weights_start")

    h1, proj, qh, kh, vh = _in_proj_fused(x, pre_w_mix, sc1 + started[0:1, 0:1], sh1, w_in_full, tables)
    out_g = _gather_wait(out_part, [proj], "gather_out_wait")
    (attn_raw, cat, lse), ((out_g,),) = _attn_fwd(qh, kh, vh, attn_sinks, attn_out_w, biases, comms=[_plan_pair_forward([out_g])])
    up_g = _gather_wait(up_part, [attn_raw], "gather_up_wait")
    (o_raw, cat, states), ((up_g,),) = _hgrn_fwd(proj, lb_table, hg_norm_w, cat, chunk_masks, comms=[_plan_pair_forward([up_g])])
    down_g = _gather_wait(down_part, [o_raw], "gather_down_wait")
    w_out_full = out_g.reshape(D_MODEL, D_MODEL)
    w_up4 = up_g.reshape(N_CHIPS, D_MODEL, D_MODEL)
    mix, x1, h2 = _out_proj_fused(cat, w_out_full, x, post_w_mix, g1, pre_w_mlp, sc2, sh2)
    big_tm = min(ntok, 2048)
    up_spec = pl.BlockSpec((None, D_MODEL, D_MODEL), lambda i, j: (j, 0, 0))
    r, ((down_g,),) = _mm(flat(h2), w_up4, name="up_proj", out_dtype=BF16, tm=big_tm, tn=D_MODEL, n_out=D_FF, b_spec=up_spec,
                          epi=lambda acc: jnp.maximum(acc, 0.0), comms=[_plan_pair_forward([down_g])])
    w_down_full = down_g.reshape(D_FF, D_MODEL)
    square = lambda t: t * t
    loss_row, dy, dd, dg2, d_post_mlp = _down_proj_fused(unflat(r), w_down_full, x1, post_w_mlp, g2, loss_target)

    dpre = _mm(flat(dd), w_down_full, name="down_bwd", out_dtype=BF16, trans_b=True, tm=big_tm, tn=D_MODEL, extra=(r,),
               epi=lambda acc, rt: acc * (2.0 * rt.astype(F32)))
    half_rows = D_MODEL // 2
    g_down = _mm_tn(r, flat(dd), name="down_wgrad", tk=half_rows, tn=D_MODEL, a_fn=square,
                    out_shape=_sds((2, N_CHIPS, half_rows, D_MODEL), F32),
                    out_spec=pl.BlockSpec((None, None, half_rows, D_MODEL), lambda i, j: (i % 2, i // 2, 0, 0)))
    (dx1, dmix, dsc2, dsh2, dg1, d_pre_mlp, d_post_mix), ((q_down,),) = _up_bwd_fused(
        unflat(dpre), w_up4, dy, x1, mix, pre_w_mlp, sc2, post_w_mix, g1, comms=[_plan_pair([g_down], True)])
    g_up = _mm_tn(flat(h2), dpre, name="up_wgrad", tk=D_MODEL, tn=half_rows,
                  out_shape=_sds((2, N_CHIPS, half_rows, D_MODEL), F32),
                  out_spec=pl.BlockSpec((2, None, half_rows, half_rows), lambda i, j: (0, j // 2, 0, j % 2)))
    s_down = _pair_sum(g_down, q_down, core, "pair_sum_down")

    dcat, ((q_up,),) = _mm(flat(dmix), w_out_full, name="out_bwd", out_dtype=F32, trans_b=True, comms=[_plan_pair([g_up], True)])
    dcat = unflat(dcat)
    s_up = _pair_sum(g_up, q_up, core, "pair_sum_up")
    out_rows = D_MODEL // N_CHIPS
    g_out = _mm_tn(flat(cat), flat(dmix), name="out_wgrad", tk=2 * out_rows, tn=half_rows,
                   out_shape=_sds((2, N_CHIPS, out_rows, half_rows), F32),
                   out_spec=pl.BlockSpec((None, 2, out_rows, half_rows), lambda i, j: (j, i, 0, 0)))
    (dproj_rec, d_lb, d_hg_norm), ((x_down,), (q_out,)) = _hgrn_bwd(
        dcat, proj, o_raw, states, lb_table, hg_norm_w, chunk_masks, comms=[_plan_chip_exchange([s_down]), _plan_pair([g_out], True)])
    half_down = _sum_chips(s_down, x_down, chip_idx, "sum_chips_down")
    s_out = _pair_sum(g_out, q_out, core, "pair_sum_out")
    (dproj, d_attn_out, d_sinks), ((their_down,), (x_up, x_out)) = _attn_bwd(
        dcat, attn_raw, attn_out_w, qh, kh, vh, lse, attn_sinks, tables, biases, dproj_rec,
        comms=[_plan_pair([half_down], False), _plan_chip_exchange([s_up, s_out])])
    half_up = _sum_chips(s_up, x_up, chip_idx, "sum_chips_up")
    half_out = _sum_chips(s_out, x_out, chip_idx, "sum_chips_out")
    dproj = flat(dproj)
    in_rows = IN_COLS // N_CHIPS // 2
    g_in = _mm_tn(dproj, flat(h1), name="in_wgrad", tk=2 * LANE, tn=D_MODEL).reshape(N_CHIPS, 2, in_rows, D_MODEL)
    dh1, ((q_in,), (their_up, their_out)) = _mm(
        dproj, w_in_full, name="in_bwd", out_dtype=F32,
        comms=[_plan_pair([g_in], "chip_major"), _plan_pair([half_up, half_out], False)])
    s_in = _pair_sum(g_in, q_in, core, "pair_sum_in", chip_major=True)
    in_sems, s_in, in_landing, started = _exchange_start(s_in, "exchange_in_start")
    grad_x, dsc1, dsh1, d_pre_mix = _norm1_bwd(unflat(dh1), dx1, x, pre_w_mix + started[0:1, 0:1], sc1)

    dmod = jnp.concatenate([dsh1, dsc1, dg1, dsh2, dsc2, dg2], axis=-1).reshape(bsz, N_MOD * D_MODEL)
    pack = _pack_partials(dmod, [d_pre_mix, d_post_mix, d_pre_mlp, d_post_mlp, d_attn_out, d_hg_norm, d_sinks], d_lb, loss_row)
    ((packs,),) = _comm_only([_plan_allgather8([pack])], "gather_small")
    w_small = dict(b_ada=b_ada, pre_w_mix=pre_w_mix, post_w_mix=post_w_mix, pre_w_mlp=pre_w_mlp, post_w_mlp=post_w_mlp,
                   attn_out_w=attn_out_w, hg_norm_w=hg_norm_w, attn_sinks=attn_sinks, lb_table=lb_table)
    m_small = dict(b_ada=m_b_ada, pre_w_mix=m_pre_w_mix, post_w_mix=m_post_w_mix, pre_w_mlp=m_pre_w_mlp, post_w_mlp=m_post_w_mlp,
                   attn_out_w=m_attn_out_w, hg_norm_w=m_hg_norm_w, attn_sinks=m_attn_sinks, lb_table=m_lb_table)
    v_small = dict(b_ada=v_b_ada, pre_w_mix=v_pre_w_mix, post_w_mix=v_post_w_mix, pre_w_mlp=v_pre_w_mlp, post_w_mlp=v_post_w_mlp,
                   attn_out_w=v_attn_out_w, hg_norm_w=v_hg_norm_w, attn_sinks=v_attn_sinks, lb_table=v_lb_table)
    *small_packed, loss_rows = _small_update(packs, _pack_small(w_small), _pack_small(m_small), _pack_small(v_small), bsz)
    small_out = [_unpack_small(p) for p in small_packed]
    loss = loss_rows[0, 0]

    dmod_all = packs[:, :bsz * MOD_ROWS, :].reshape(N_DEV * bsz, N_MOD * D_MODEL)
    dmod_cols = lax.dynamic_slice_in_dim(dmod_all, chip * ada_cols, ada_cols, axis=1)
    ada_out = _ada_bwd_adamw(c_all, dmod_cols, w_ada[0], m_w_ada[0], v_w_ada[0])

    s_in, x_in = _exchange_wait(in_sems, s_in, in_landing, [grad_x, ada_out[0]], "exchange_in_wait")
    half_in = _sum_chips(s_in, x_in, chip_idx, "sum_chips_in")
    ((their_in,),) = _comm_only([_plan_pair([half_in], False)], "pair_swap_in")
    big = dict(
        w_in=tuple(jnp.transpose(a) for a in _adamw_halves(half_in, their_in, core, w_in_t[0], m_in_t[0], v_in_t[0], axis=0,
                                                           name="adamw_in")),
        w_up=tuple(_adamw_halves(half_up, their_up, core, w_up[0], m_w_up[0], v_w_up[0], axis=0, name="adamw_up")),
        w_out=tuple(_adamw_halves(half_out, their_out, core, w_out[0], m_w_out[0], v_w_out[0], axis=1, name="adamw_out")),
        w_down=tuple(_adamw_halves(half_down, their_down, core, w_down[0], m_w_down[0], v_w_down[0], axis=0, name="adamw_down")),
        w_ada=tuple(ada_out),
    )
    order = ("w_ada", "b_ada", "pre_w_mix", "w_in", "attn_sinks", "attn_out_w", "lb_table", "hg_norm_w", "w_out", "post_w_mix",
             "pre_w_mlp", "w_up", "w_down", "post_w_mlp")
    outs = [loss, grad_x]
    for kind in range(4):
        for nm in order:
            outs.append(big[nm][kind][None] if nm in big else small_out[kind][nm])
    return tuple(outs)
```

```python
import jax
import jax.numpy as jnp
from jax import lax
from jax.experimental import pallas as pl
from jax.experimental.pallas import tpu as pltpu

F32 = jnp.float32
BF16 = jnp.bfloat16

D_MODEL = 1024
ATT_WIDTH = 512
ATT_HEAD_DIM = 64
ATT_Q_HEADS = 8
ATT_KV_HEADS = 2
ATT_GROUP = ATT_Q_HEADS // ATT_KV_HEADS
ATT_KV_COLS = ATT_KV_HEADS * ATT_HEAD_DIM
WINDOW = 128
ROPE_DIM = 16
ROPE_THETA = 500000.0
HG_WIDTH = 512
MIX_WIDTH = ATT_WIDTH + HG_WIDTH
HG_HEAD_DIM = 128
HG_HEADS = 4
HG_CHUNK = 32
IN_COLS = ATT_WIDTH + 2 * ATT_KV_COLS + 4 * HG_WIDTH
ATT_COLS = ATT_WIDTH + 2 * ATT_KV_COLS
D_FF = 4 * D_MODEL
N_MOD = 6
EPS = 1e-6
ATT_SCALE = ATT_HEAD_DIM ** -0.5

ADAM_LR = 0.001
ADAM_B1 = 0.9
ADAM_B2 = 0.999
ADAM_EPS = 1e-08
ADAM_WD = 0.01
ADAM_STEP = 10

N_CHIPS = 4
N_DEV = 8
LANE = 128
VMEM_LIMIT = 48 * 1024 * 1024
VMEM_LIMIT_BIG = 58 * 1024 * 1024
MESH = pl.DeviceIdType.MESH

NT_DIMS = (((1,), (1,)), ((), ()))
TN_DIMS = (((0,), (0,)), ((), ()))


def _sds(shape, dtype):
    return jax.ShapeDtypeStruct(tuple(shape), dtype)


def _params(*sem, vmem_limit=None):
    return pltpu.CompilerParams(dimension_semantics=sem, vmem_limit_bytes=VMEM_LIMIT if vmem_limit is None else vmem_limit)


def _sigmoid(x):
    return 1.0 / (1.0 + jnp.exp(-x))


def _dot(a, b, dims=None):
    a, b = a.astype(BF16), b.astype(BF16)
    if dims is None:
        return jnp.dot(a, b, preferred_element_type=F32)
    return lax.dot_general(a, b, dims, preferred_element_type=F32)


def _rms_fwd(x, w):
    rstd = lax.rsqrt(jnp.mean(x * x, axis=-1, keepdims=True) + EPS)
    xh = x * rstd
    return xh * w, xh, rstd


def _rms_bwd(dy, xh, rstd, w):
    dxh = dy * w
    dx = rstd * (dxh - xh * jnp.mean(dxh * xh, axis=-1, keepdims=True))
    return dx, dy * xh


def _colsum(x):
    return jnp.sum(x, axis=0, keepdims=True)


def _row_tile(rows, cap=256):
    return max(t for t in range(16, cap + 1, 16) if rows % t == 0)


HBM_SPEC = pl.BlockSpec(memory_space=pltpu.HBM)


def _mesh_pos():
    return lax.axis_index("x"), lax.axis_index("y"), lax.axis_index("c")


class _Comm:
    def __init__(self, ins, outs, sems, start, finish, aliases=()):
        self.ins, self.outs, self.sems = list(ins), list(outs), list(sems)
        self.start, self.finish, self.aliases = start, finish, tuple(aliases)


def _call(body, args, *, name, grid, in_specs, out_specs, out_shape, sem, scratch_shapes=(), comms=(), aliases=None,
          vmem_limit=None):
    scratch_shapes = list(scratch_shapes)
    if not comms:
        return pl.pallas_call(body, name=name, grid=grid, in_specs=in_specs, out_specs=out_specs, out_shape=out_shape,
                              input_output_aliases=dict(aliases or {}), scratch_shapes=scratch_shapes,
                              compiler_params=_params(*sem, vmem_limit=vmem_limit))(*args)
    single = not isinstance(out_shape, (list, tuple))
    out_specs_l = [out_specs] if single else list(out_specs)
    out_shape_l = [out_shape] if single else list(out_shape)
    n_in, n_out, n_scr = len(in_specs), len(out_shape_l), len(scratch_shapes)
    n_ci = [len(cm.ins) for cm in comms]
    n_co = [len(cm.outs) for cm in comms]
    n_cs = [len(cm.sems) for cm in comms]
    aliases = dict(aliases or {})
    for k, cm in enumerate(comms):
        for i, o in cm.aliases:
            aliases[n_in + sum(n_ci[:k]) + i] = n_out + sum(n_co[:k]) + o

    def fused(*refs):
        pos = [0]

        def take(n):
            part = refs[pos[0]:pos[0] + n]
            pos[0] += n
            return part

        ins = take(n_in)
        c_ins = [take(n) for n in n_ci]
        outs = take(n_out)
        c_outs = [take(n) for n in n_co]
        scr = take(n_scr)
        c_sems = [take(n) for n in n_cs]
        first, last = True, True
        for d, size in enumerate(grid):
            first = jnp.logical_and(first, pl.program_id(d) == 0)
            last = jnp.logical_and(last, pl.program_id(d) == size - 1)

        def run(which):
            for cm, ci, co, cs in zip(comms, c_ins, c_outs, c_sems):
                getattr(cm, which)(ci, co, cs)

        if grid:
            pl.when(first)(lambda: run("start"))
        else:
            run("start")
        body(*ins, *outs, *scr)
        if grid:
            pl.when(last)(lambda: run("finish"))
        else:
            run("finish")

    res = pl.pallas_call(
        fused, name=name, grid=grid, in_specs=list(in_specs) + [HBM_SPEC] * sum(n_ci),
        out_specs=out_specs_l + [HBM_SPEC] * sum(n_co), out_shape=out_shape_l + [s for cm in comms for s in cm.outs],
        input_output_aliases=aliases, scratch_shapes=scratch_shapes + [s for cm in comms for s in cm.sems],
        compiler_params=_params(*["arbitrary"] * len(grid), vmem_limit=vmem_limit),
    )(*args, *[a for cm in comms for a in cm.ins])
    main = res[:n_out]
    extra, at = [], n_out
    for n in n_co:
        extra.append(list(res[at:at + n]))
        at += n
    return (main[0] if single else list(main)), extra


def _mm(a, b, *, name, out_dtype, trans_b=False, tm=512, tn=None, extra=(), epi=None, b_spec=None, n_out=None, comms=()):
    m_total, k_total = a.shape
    if n_out is None:
        n_out = b.shape[0] if trans_b else b.shape[1]
    tn = n_out if tn is None else tn
    grid = (m_total // tm, n_out // tn)
    dims = NT_DIMS if trans_b else None

    def body(*refs):
        a_ref, b_ref = refs[0], refs[1]
        extra_refs = refs[2:2 + len(extra)]
        o_ref = refs[2 + len(extra)]
        acc = _dot(a_ref[...], b_ref[...], dims)
        if epi is not None:
            acc = epi(acc, *[r[...] for r in extra_refs])
        o_ref[...] = acc.astype(out_dtype)

    if b_spec is None:
        if trans_b:
            b_spec = pl.BlockSpec((tn, k_total), lambda i, j: (j, 0))
        else:
            b_spec = pl.BlockSpec((k_total, tn), lambda i, j: (0, j))
    in_specs = [pl.BlockSpec((tm, k_total), lambda i, j: (i, 0)), b_spec]
    in_specs += [pl.BlockSpec((tm, tn), lambda i, j: (i, j)) for _ in extra]
    return _call(
        body, (a, b, *extra), name=name, grid=grid, in_specs=in_specs,
        out_specs=pl.BlockSpec((tm, tn), lambda i, j: (i, j)),
        out_shape=_sds((m_total, n_out), out_dtype),
        sem=("parallel", "parallel"), comms=comms)


def _mm_tn(a, b, *, name, tk, tn, a_fn=None, out_shape=None, out_spec=None):
    m_total, k_total = a.shape
    n_total = b.shape[1]
    grid = (k_total // tk, n_total // tn)

    def body(a_ref, b_ref, o_ref):
        av = a_ref[...]
        part = _dot(av if a_fn is None else a_fn(av), b_ref[...], TN_DIMS)
        o_ref[...] = part.reshape(o_ref.shape)

    if out_shape is None:
        out_shape = _sds((k_total, n_total), F32)
        out_spec = pl.BlockSpec((tk, tn), lambda i, j: (i, j))
    return pl.pallas_call(
        body, name=name, grid=grid,
        in_specs=[pl.BlockSpec((m_total, tk), lambda i, j: (0, i)), pl.BlockSpec((m_total, tn), lambda i, j: (0, j))],
        out_specs=out_spec, out_shape=out_shape,
        compiler_params=_params("parallel", "parallel"),
    )(a, b)


def _ada_fwd(c_all, w_shard, b_shard):
    nb, ncol = c_all.shape[0], w_shard.shape[1]
    tn = 512

    def body(c_ref, w_ref, b_ref, o_ref):
        c = c_ref[...]
        o_ref[...] = _dot(c * _sigmoid(c), w_ref[...]) + b_ref[...]

    return pl.pallas_call(
        body, name="ada_fwd", grid=(ncol // tn,),
        in_specs=[pl.BlockSpec((nb, D_MODEL), lambda j: (0, 0)), pl.BlockSpec((D_MODEL, tn), lambda j: (0, j)),
                  pl.BlockSpec((1, tn), lambda j: (0, j))],
        out_specs=pl.BlockSpec((nb, tn), lambda j: (0, j)), out_shape=_sds((nb, ncol), F32),
        compiler_params=_params("parallel"),
    )(c_all, w_shard, b_shard)


def _adamw_math(g, w, m, v):
    m = ADAM_B1 * m + (1.0 - ADAM_B1) * g
    v = ADAM_B2 * v + (1.0 - ADAM_B2) * (g * g)
    m_hat = m / (1.0 - ADAM_B1 ** ADAM_STEP)
    v_hat = v / (1.0 - ADAM_B2 ** ADAM_STEP)
    delta = -ADAM_LR * (m_hat / (jnp.sqrt(v_hat) + ADAM_EPS) + ADAM_WD * w)
    return delta, m, v


def _ada_bwd_adamw(c_all, dmod_cols, w, m, v):
    nb, ncol = dmod_cols.shape
    tn = 256

    def body(c_ref, d_ref, w_ref, m_ref, v_ref, g_ref, dl_ref, nm_ref, nv_ref):
        c = c_ref[...]
        g = _dot(c * _sigmoid(c), d_ref[...], TN_DIMS)
        g_ref[...] = g
        dl_ref[...], nm_ref[...], nv_ref[...] = _adamw_math(g, w_ref[...], m_ref[...], v_ref[...])

    col = pl.BlockSpec((D_MODEL, tn), lambda j: (0, j))
    shp = _sds((D_MODEL, ncol), F32)
    return pl.pallas_call(
        body, name="ada_bwd_adamw", grid=(ncol // tn,),
        in_specs=[pl.BlockSpec((nb, D_MODEL), lambda j: (0, 0)), pl.BlockSpec((nb, tn), lambda j: (0, j)), col, col, col],
        out_specs=[col, col, col, col], out_shape=[shp, shp, shp, shp],
        compiler_params=_params("parallel"),
    )(c_all, dmod_cols, w, m, v)


def _adamw_halves(own, theirs, core, w, m, v, *, axis, name):
    r2, c2 = own.shape
    tr = _row_tile(r2)
    nt = r2 // tr

    def body(core_ref, own_ref, their_ref, w_ref, m_ref, v_ref, g_ref, dl_ref, nm_ref, nv_ref):
        g = jnp.where(pl.program_id(0) == core_ref[0], own_ref[...], their_ref[...])
        g_ref[...] = g
        dl_ref[...], nm_ref[...], nv_ref[...] = _adamw_math(g, w_ref[...], m_ref[...], v_ref[...])

    if axis == 0:
        full = pl.BlockSpec((tr, c2), lambda h, i, core_ref: (h * nt + i, 0))
    else:
        full = pl.BlockSpec((tr, c2), lambda h, i, core_ref: (i, h))
    half = pl.BlockSpec((tr, c2), lambda h, i, core_ref: (i, 0))
    shp = _sds(w.shape, F32)
    return pl.pallas_call(
        body, name=name,
        grid_spec=pltpu.PrefetchScalarGridSpec(num_scalar_prefetch=1, grid=(2, nt), in_specs=[half, half, full, full, full],
                                               out_specs=[full] * 4),
        out_shape=[shp] * 4, compiler_params=_params("parallel", "parallel"),
    )(core, own, theirs, w, m, v)


def _tok_spec(tm, width=D_MODEL):
    return pl.BlockSpec((None, tm, width), lambda b, i: (b, i, 0))


def _row_spec(width=D_MODEL):
    return pl.BlockSpec((None, 1, width), lambda b, i: (b, 0, 0))


def _vec_spec(width=D_MODEL):
    return pl.BlockSpec((1, width), lambda b, i: (0, 0))


class _RowsOf:
    def __init__(self, ref, first, count):
        self.ref, self.rows = ref, slice(first, first + count)

    def __getitem__(self, idx):
        return self.ref[self.rows, :]

    def __setitem__(self, idx, value):
        self.ref[self.rows, :] = value


def _mm_rows(a, b, *, name, tm, extra, extra_specs, out_specs, out_shape, epi, pro=None, trans_b=False, b_chunks=1, comms=(),
             parts=1, zero_per_seq=(), zero_once=(), vmem_limit=None):
    bsz, seq, k_total = a.shape
    kc = k_total // b_chunks
    dims = NT_DIMS if trans_b else None
    rows = tm // parts

    def body(*refs):
        a_ref, b_ref = refs[0], refs[1]
        ex, outs = refs[2:2 + len(extra)], refs[2 + len(extra):]
        if zero_per_seq:
            @pl.when(pl.program_id(1) == 0)
            def _():
                for k in zero_per_seq:
                    outs[k][...] = jnp.zeros_like(outs[k])
        if zero_once:
            @pl.when(jnp.logical_and(pl.program_id(0) == 0, pl.program_id(1) == 0))
            def _():
                for k in zero_once:
                    outs[k][...] = jnp.zeros_like(outs[k])

        def part_of(ref, p):
            tiled = len(ref.shape) == 2 and ref.shape[0] == tm
            return _RowsOf(ref, p * rows, rows) if tiled and parts > 1 else ref

        accs = []
        for p in range(parts):
            a_p, ex_p, outs_p = part_of(a_ref, p), [part_of(r, p) for r in ex], [part_of(r, p) for r in outs]
            if b_chunks == 1:
                accs.append(_dot(a_p[...] if pro is None else pro(a_p, ex_p, outs_p), b_ref[...], dims))
            else:
                acc = _dot(a_p[...][:, 0:kc], b_ref[0], NT_DIMS)
                for k in range(1, b_chunks):
                    acc = acc + _dot(a_p[...][:, k * kc:(k + 1) * kc], b_ref[k], NT_DIMS)
                accs.append(acc)
        for p in range(parts):
            epi(accs[p], [part_of(r, p) for r in ex], [part_of(r, p) for r in outs])

    b_spec = pl.BlockSpec(b.shape, lambda bb, i: (0,) * b.ndim)
    return _call(
        body, (a, b, *extra), name=name, grid=(bsz, seq // tm), in_specs=[_tok_spec(tm, k_total), b_spec, *extra_specs],
        out_specs=out_specs, out_shape=out_shape, sem=("arbitrary", "arbitrary"), comms=comms, vmem_limit=vmem_limit)


def _in_proj_fused(x, w, sc, sh, w_in_t, tables, comms=()):
    tm = 512
    bsz, seq, _ = x.shape
    half = ROPE_DIM // 2
    heads_per_slab = LANE // ATT_HEAD_DIM

    def pro(x_ref, ex, outs):
        y, _, _ = _rms_fwd(x_ref[...], ex[0][...])
        h = (y * (1.0 + ex[1][...]) + ex[2][...]).astype(BF16)
        outs[0][...] = h
        return h

    def epi(acc, ex, outs):
        c, u, d = ex[3][...], ex[4][...], ex[5][...]
        _, proj_ref, q_ref, k_ref, v_ref = outs
        proj_ref[...] = acc

        def rope(z):
            return (z * c + pltpu.roll(z, half, 1) * u + pltpu.roll(z, LANE - half, 1) * d).astype(BF16)

        for s in range(ATT_WIDTH // LANE):
            slab = rope(acc[:, s * LANE:(s + 1) * LANE])
            for part in range(heads_per_slab):
                g, hh = divmod(s * heads_per_slab + part, ATT_GROUP)
                piece = slab[:, part * ATT_HEAD_DIM:(part + 1) * ATT_HEAD_DIM]
                for blk in range(tm // WINDOW):
                    q_ref[blk, g, hh * WINDOW:(hh + 1) * WINDOW, :] = piece[blk * WINDOW:(blk + 1) * WINDOW]
        rk = rope(acc[:, ATT_WIDTH:ATT_WIDTH + LANE])
        vv = acc[:, ATT_WIDTH + LANE:ATT_COLS].astype(BF16)
        for g in range(ATT_KV_HEADS):
            k_ref[g] = rk[:, g * ATT_HEAD_DIM:(g + 1) * ATT_HEAD_DIM]
            v_ref[g] = vv[:, g * ATT_HEAD_DIM:(g + 1) * ATT_HEAD_DIM]

    cols = w_in_t.shape[0]
    tab = pl.BlockSpec((tm, LANE), lambda b, i: (i, 0))
    kv_spec = pl.BlockSpec((None, ATT_KV_HEADS, tm, ATT_HEAD_DIM), lambda b, i: (b, 0, i, 0))
    kv_shape = _sds((bsz, ATT_KV_HEADS, seq, ATT_HEAD_DIM), BF16)
    q_spec = pl.BlockSpec((None, tm // WINDOW, ATT_KV_HEADS, GROUP_ROWS, ATT_HEAD_DIM), lambda b, i: (b, i, 0, 0, 0))
    return _mm_rows(x, w_in_t, name="in_proj", tm=tm, extra=(w, sc, sh, *tables),
                    extra_specs=[_vec_spec(), _row_spec(), _row_spec(), tab, tab, tab],
                    out_specs=[_tok_spec(tm), _tok_spec(tm, cols), q_spec, kv_spec, kv_spec],
                    out_shape=[_sds(x.shape, BF16), _sds((bsz, seq, cols), F32),
                               _sds((bsz, seq // WINDOW, ATT_KV_HEADS, GROUP_ROWS, ATT_HEAD_DIM), BF16), kv_shape, kv_shape],
                    pro=pro, epi=epi, trans_b=True, comms=comms)


def _rope_tables(seq):
    half = ROPE_DIM // 2
    inv_freq = ROPE_THETA ** (-jnp.arange(0, ROPE_DIM, 2, dtype=F32) / ROPE_DIM)
    ang = jnp.arange(seq, dtype=F32)[:, None] * inv_freq[None, :]
    cos, sin = jnp.cos(ang), jnp.sin(ang)
    rest = ATT_HEAD_DIM - ROPE_DIM
    ones, zeros, zh = jnp.ones((seq, rest), F32), jnp.zeros((seq, rest), F32), jnp.zeros((seq, half), F32)
    reps = LANE // ATT_HEAD_DIM
    t_cos = jnp.tile(jnp.concatenate([cos, cos, ones], axis=1), (1, reps))
    t_up = jnp.tile(jnp.concatenate([zh, sin, zeros], axis=1), (1, reps))
    t_dn = jnp.tile(jnp.concatenate([-sin, zh, zeros], axis=1), (1, reps))
    return t_cos, t_up, t_dn


GROUP_ROWS = ATT_GROUP * WINDOW


ATT_BPS = 2


MASKED = -1e30


def _band_biases():
    row = jnp.arange(GROUP_ROWS)[:, None] % WINDOW
    col = jnp.arange(2 * WINDOW)[None, :]
    own = jnp.logical_and(col >= WINDOW, col - WINDOW <= row)
    before = jnp.logical_and(col < WINDOW, col > row)
    return (jnp.where(jnp.logical_or(own, before), 0.0, MASKED).astype(F32), jnp.where(own, 0.0, MASKED).astype(F32))


def _band_bias(full_ref, first_ref, has_prev):
    return full_ref[...] if has_prev is True else jnp.where(has_prev, full_ref[...], first_ref[...])


def _sink_column(sink_ref, g):
    head = lax.broadcasted_iota(jnp.int32, (GROUP_ROWS, 1), 0) // WINDOW
    col = jnp.full((GROUP_ROWS, 1), sink_ref[0, g * ATT_GROUP], F32)
    for hh in range(1, ATT_GROUP):
        col = jnp.where(head == hh, sink_ref[0, g * ATT_GROUP + hh], col)
    return col


def _sink_row(sink_ref, g):
    return jnp.concatenate([jnp.full((1, WINDOW), sink_ref[0, g * ATT_GROUP + hh], F32) for hh in range(ATT_GROUP)], axis=1)


def _bias_spec(transposed=False):
    shape = (2 * WINDOW, GROUP_ROWS) if transposed else (GROUP_ROWS, 2 * WINDOW)
    return pl.BlockSpec(shape, lambda b, i: (0, 0))


def _attn_specs():
    q_spec = pl.BlockSpec((None, ATT_BPS, ATT_KV_HEADS, GROUP_ROWS, ATT_HEAD_DIM), lambda b, i: (b, i, 0, 0, 0))
    kv_cur = pl.BlockSpec((None, ATT_KV_HEADS, ATT_BPS * WINDOW, ATT_HEAD_DIM), lambda b, i: (b, 0, i, 0))
    kv_prev = pl.BlockSpec((None, ATT_KV_HEADS, WINDOW, ATT_HEAD_DIM), lambda b, i: (b, 0, jnp.maximum(ATT_BPS * i - 1, 0), 0))
    return q_spec, kv_cur, kv_prev


def _band(prev_ref, cur_ref, g, blk):
    own = cur_ref[g, blk * WINDOW:(blk + 1) * WINDOW]
    before = prev_ref[g] if blk == 0 else cur_ref[g, (blk - 1) * WINDOW:blk * WINDOW]
    return jnp.concatenate([before, own], axis=0)


def _attn_fwd(qh, kh, vh, sinks, w_norm, biases, comms=()):
    bsz, nblk = qh.shape[0], qh.shape[1]
    seq = nblk * WINDOW
    rows = ATT_BPS * WINDOW

    def body(sink_ref, q_ref, kc_ref, kp_ref, vc_ref, vp_ref, w_ref, full_ref, first_ref, raw_ref, an_ref, l_ref):
        l_ref[...] = jnp.zeros_like(l_ref)
        for blk in range(ATT_BPS):
            bias = _band_bias(full_ref, first_ref, True if blk else pl.program_id(1) > 0)
            for g in range(ATT_KV_HEADS):
                keys, vals = _band(kp_ref, kc_ref, g, blk), _band(vp_ref, vc_ref, g, blk)
                sink = _sink_column(sink_ref, g)
                s = _dot(q_ref[blk, g], keys, NT_DIMS) * ATT_SCALE + bias
                m = jnp.maximum(jnp.max(s, axis=-1, keepdims=True), sink)
                p = jnp.exp(s - m)
                den = jnp.sum(p, axis=-1, keepdims=True) + jnp.exp(sink - m)
                o = _dot(p / den, vals)
                lse = m + jnp.log(den)
                tok = slice(blk * WINDOW, (blk + 1) * WINDOW)
                for hh in range(ATT_GROUP):
                    h = g * ATT_GROUP + hh
                    raw_ref[tok, h * ATT_HEAD_DIM:(h + 1) * ATT_HEAD_DIM] = o[hh * WINDOW:(hh + 1) * WINDOW]
                    l_ref[tok, h:h + 1] = lse[hh * WINDOW:(hh + 1) * WINDOW]
        y, _, _ = _rms_fwd(raw_ref[...], w_ref[...])
        an_ref[...] = y.astype(BF16)

    cur = lambda width: pl.BlockSpec((None, rows, width), lambda b, i: (b, i, 0))
    q_spec, kv_cur, kv_prev = _attn_specs()
    return _call(
        body, (sinks, qh, kh, kh, vh, vh, w_norm, *biases), name="attn_fwd", grid=(bsz, nblk // ATT_BPS),
        in_specs=[pl.BlockSpec(memory_space=pltpu.SMEM), q_spec, kv_cur, kv_prev, kv_cur, kv_prev, _vec_spec(ATT_WIDTH),
                  _bias_spec(), _bias_spec()],
        out_specs=[cur(ATT_WIDTH), cur(ATT_WIDTH), cur(LANE)],
        out_shape=[_sds((bsz, seq, ATT_WIDTH), F32), _sds((bsz, seq, MIX_WIDTH), BF16), _sds((bsz, seq, LANE), F32)],
        sem=("parallel", "parallel"), comms=comms)


HG_Q0 = ATT_COLS // LANE
HG_F0 = HG_Q0 + HG_HEADS
HG_I0 = HG_F0 + HG_HEADS
HG_G0 = HG_I0 + HG_HEADS
HG_TOK = 256
HG_NCH = HG_TOK // HG_CHUNK
HG_HPS = 2


def _block_masks():
    row = jnp.arange(HG_TOK)[:, None]
    col = jnp.arange(HG_TOK)[None, :]
    same = (row // HG_CHUNK) == (col // HG_CHUNK)
    return jnp.logical_and(same, col <= row).astype(F32), jnp.logical_and(same, col >= row).astype(F32)


def _row_in_chunk():
    return lax.broadcasted_iota(jnp.int32, (HG_TOK, LANE), 0) % HG_CHUNK


def _chunk_cumsum(x, reverse=False):
    ric = _row_in_chunk()
    shift = 1
    while shift < HG_CHUNK:
        if reverse:
            x = x + jnp.where(ric < HG_CHUNK - shift, pltpu.roll(x, HG_TOK - shift, 0), 0.0)
        else:
            x = x + jnp.where(ric >= shift, pltpu.roll(x, shift, 0), 0.0)
        shift *= 2
    return x


def _chunk_rows(rows):
    stacked = jnp.concatenate([r[None] for r in rows], axis=0)
    return jnp.broadcast_to(stacked, (HG_NCH, HG_CHUNK, LANE)).reshape(HG_TOK, LANE)


def _chunk_slices(x):
    return [x[j * HG_CHUNK:(j + 1) * HG_CHUNK] for j in range(HG_NCH)]


def _hgrn_common(tbl, hf, hq):
    lb = _sigmoid(tbl[1:2] - tbl[0:1])
    sig = _sigmoid(hf)
    f = lb + (1.0 - lb) * sig
    sq = _sigmoid(hq)
    q, k = hq * sq, 1.0 - f
    b = _chunk_cumsum(jnp.log(f))
    last = [b[(j + 1) * HG_CHUNK - 1:(j + 1) * HG_CHUNK] for j in range(HG_NCH)]
    bl = _chunk_rows(last)
    e_b, e_nb, e_rem = jnp.exp(b), jnp.exp(-b), jnp.exp(bl - b)
    e_last = [jnp.exp(r) for r in last]
    return dict(lb=lb, sig=sig, f=f, sq=sq, q=q, k=k, e_b=e_b, e_nb=e_nb, e_rem=e_rem, e_last=e_last,
                qd=q * e_b, kd=k * e_nb, ku=k * e_rem)


def _hgrn_fwd(proj, lb_table, norm_w, mix_in, masks, comms=()):
    bsz, seq, _ = proj.shape
    nstep = seq // HG_TOK

    def body(tbl_ref, nw_ref, q_ref, f_ref, i_ref, g_ref, mix_ref, lower_ref, o_ref, rec_ref, st_ref, s_scr):
        @pl.when(pl.program_id(2) == 0)
        def _():
            s_scr[...] = jnp.zeros_like(s_scr)

        lower = lower_ref[...]
        for hp in range(HG_HPS):
            ls = slice(hp * LANE, (hp + 1) * LANE)
            v, hg = i_ref[:, ls], g_ref[:, ls]
            t = _hgrn_common(tbl_ref[:, ls], f_ref[:, ls], q_ref[:, ls])
            a = _dot(t["qd"], t["kd"], NT_DIMS) * lower
            o_intra = _dot(a, v)
            v_c, ku_c, qd_c = [_chunk_slices(z.astype(BF16)) for z in (v, t["ku"], t["qd"])]
            updates = [_dot(v_c[j], ku_c[j], TN_DIMS) for j in range(HG_NCH)]
            st = s_scr[hp]
            states = []
            for j in range(HG_NCH):
                states.append(st)
                st = st * t["e_last"][j] + updates[j]
            s_scr[hp] = st
            o = o_intra + jnp.concatenate([_dot(qd_c[j], states[j], NT_DIMS) for j in range(HG_NCH)], axis=0)
            st_ref[hp, 0] = states[0]
            o_ref[:, ls] = o
            y, _, _ = _rms_fwd(o, nw_ref[...])
            rec_ref[:, ls] = (y * (hg * _sigmoid(hg))).astype(BF16)

    width = HG_HPS * LANE
    slab = lambda first: pl.BlockSpec((None, HG_TOK, width), lambda b, h, t: (b, t, first // HG_HPS + h))
    head_out = pl.BlockSpec((None, HG_TOK, width), lambda b, h, t: (b, t, h))
    mix_out = pl.BlockSpec((None, HG_TOK, width), lambda b, h, t: (b, t, ATT_WIDTH // width + h))
    return _call(
        body, (lb_table, norm_w, proj, proj, proj, proj, mix_in, masks[0]), name="hgrn_fwd", grid=(bsz, HG_HEADS // HG_HPS, nstep),
        in_specs=[pl.BlockSpec((2, width), lambda b, h, t: (0, h)), pl.BlockSpec((1, LANE), lambda b, h, t: (0, 0)),
                  slab(HG_Q0), slab(HG_F0), slab(HG_I0), slab(HG_G0), pl.BlockSpec(memory_space=pl.ANY),
                  pl.BlockSpec((HG_TOK, HG_TOK), lambda b, h, t: (0, 0))],
        out_specs=[head_out, mix_out,
                   pl.BlockSpec((None, HG_HPS, 1, LANE, LANE), lambda b, h, t: (b, h, t, 0, 0))],
        out_shape=[_sds((bsz, seq, HG_WIDTH), F32), _sds(mix_in.shape, BF16),
                   _sds((bsz, HG_HEADS, nstep, LANE, LANE), F32)],
        scratch_shapes=[pltpu.VMEM((HG_HPS, LANE, LANE), F32)],
        sem=("parallel", "parallel", "arbitrary"), comms=comms, aliases={6: 1})


def _out_proj_fused(cat, w_out, x, post_w, g1, pre_w, sc2, sh2):
    tm = 512

    def epi(mix, ex, outs):
        x_ref, pw_ref, g1_ref, w2_ref, sc_ref, sh_ref = ex
        outs[0][...] = mix
        n1, _, _ = _rms_fwd(mix, pw_ref[...])
        x1 = x_ref[...] + g1_ref[...] * n1
        outs[1][...] = x1
        y2, _, _ = _rms_fwd(x1, w2_ref[...])
        outs[2][...] = (y2 * (1.0 + sc_ref[...]) + sh_ref[...]).astype(BF16)

    return _mm_rows(cat, w_out, name="out_proj", tm=tm, extra=(x, post_w, g1, pre_w, sc2, sh2),
                    extra_specs=[_tok_spec(tm), _vec_spec(), _row_spec(), _vec_spec(), _row_spec(), _row_spec()],
                    out_specs=[_tok_spec(tm), _tok_spec(tm), _tok_spec(tm)],
                    out_shape=[_sds(x.shape, F32), _sds(x.shape, F32), _sds(x.shape, BF16)], epi=epi)


def _acc_out(ref, first, value):
    @pl.when(first)
    def _():
        ref[...] = value

    @pl.when(jnp.logical_not(first))
    def _():
        ref[...] += value


def _down_proj_fused(r, w_down, x1, post_w, g2, target):
    tm = 512
    bsz = x1.shape[0]

    def pro(r_ref, ex, outs):
        rv = r_ref[...]
        return rv * rv

    def epi(down, ex, outs):
        x1_ref, w_ref, g2_ref, t_ref = ex
        loss_ref, dy_ref, dd_ref, dg2_ref, dw_ref = outs
        w, g2v = w_ref[...], g2_ref[...]
        n2, dh, rstd = _rms_fwd(down, w)
        err = x1_ref[...] + g2v * n2 - t_ref[...]
        part = (0.5 / D_MODEL) * jnp.sum(jnp.sum(err * err, axis=-1, keepdims=True), axis=0, keepdims=True)
        loss_ref[...] += jnp.broadcast_to(part, (1, LANE))
        dy = err * (1.0 / D_MODEL)
        dy_ref[...] = dy
        dg2_ref[...] += _colsum(dy * n2)
        dd, dw_rows = _rms_bwd(dy * g2v, dh, rstd, w)
        dd_ref[...] = dd.astype(BF16)
        dw_ref[...] += _colsum(dw_rows)

    return _mm_rows(r, w_down, name="down_proj", tm=tm, extra=(x1, post_w, g2, target),
                    extra_specs=[_tok_spec(tm), _vec_spec(), _row_spec(), _tok_spec(tm)],
                    out_specs=[_vec_spec(LANE), _tok_spec(tm), _tok_spec(tm), _row_spec(), _vec_spec()],
                    out_shape=[_sds((1, LANE), F32), _sds(x1.shape, F32), _sds(x1.shape, BF16), _sds((bsz, 1, D_MODEL), F32),
                               _sds((1, D_MODEL), F32)], pro=pro, epi=epi, parts=2, zero_per_seq=(3,), zero_once=(0, 4),
                    vmem_limit=VMEM_LIMIT_BIG)


def _up_bwd_fused(dpre, w_up4, dy, x1, mix, pre_w, sc2, post_w, g1, comms=()):
    tm = 512
    bsz = x1.shape[0]

    def epi(dh2v, ex, outs):
        dy_ref, x1_ref, mix_ref, w2_ref, sc_ref, pw_ref, g1_ref = ex
        dx1_ref, dmix_ref, dsc_ref, dsh_ref, dg1_ref, dw2_ref, dpw_ref = outs
        w2, pw = w2_ref[...], pw_ref[...]
        y2, xh2, rstd2 = _rms_fwd(x1_ref[...], w2)
        dsh_ref[...] += _colsum(dh2v)
        dsc_ref[...] += _colsum(dh2v * y2)
        dx1n, dw_rows = _rms_bwd(dh2v * (1.0 + sc_ref[...]), xh2, rstd2, w2)
        dw2_ref[...] += _colsum(dw_rows)
        dx1 = dy_ref[...] + dx1n
        dx1_ref[...] = dx1
        n1, mh, rstd1 = _rms_fwd(mix_ref[...], pw)
        dg1_ref[...] += _colsum(dx1 * n1)
        dmix, dpw_rows = _rms_bwd(dx1 * g1_ref[...], mh, rstd1, pw)
        dmix_ref[...] = dmix.astype(BF16)
        dpw_ref[...] += _colsum(dpw_rows)

    row_shape = _sds((bsz, 1, D_MODEL), F32)
    vec_shape = _sds((1, D_MODEL), F32)
    return _mm_rows(dpre, w_up4, name="up_bwd", tm=tm, extra=(dy, x1, mix, pre_w, sc2, post_w, g1),
                    extra_specs=[_tok_spec(tm), _tok_spec(tm), _tok_spec(tm), _vec_spec(), _row_spec(), _vec_spec(), _row_spec()],
                    out_specs=[_tok_spec(tm), _tok_spec(tm), _row_spec(), _row_spec(), _row_spec(), _vec_spec(), _vec_spec()],
                    out_shape=[_sds(x1.shape, F32), _sds(x1.shape, BF16), row_shape, row_shape, row_shape, vec_shape, vec_shape],
                    epi=epi, b_chunks=w_up4.shape[0], comms=comms, parts=2, zero_per_seq=(2, 3, 4), zero_once=(5, 6),
                    vmem_limit=VMEM_LIMIT_BIG)


def _norm1_bwd(dh1, dx1, x, pre_w, sc1, tm=512, comms=()):
    bsz, seq, _ = x.shape

    def body(dh_ref, dx1_ref, x_ref, w_ref, sc_ref, gx_ref, dsc_ref, dsh_ref, dw_ref):
        b, i = pl.program_id(0), pl.program_id(1)
        w = w_ref[...]
        dh = dh_ref[...]
        y, xh, rstd = _rms_fwd(x_ref[...], w)
        _acc_out(dsh_ref, i == 0, _colsum(dh))
        _acc_out(dsc_ref, i == 0, _colsum(dh * y))
        dx, dw_rows = _rms_bwd(dh * (1.0 + sc_ref[...]), xh, rstd, w)
        _acc_out(dw_ref, jnp.logical_and(b == 0, i == 0), _colsum(dw_rows))
        gx_ref[...] = dx1_ref[...] + dx

    row_shape = _sds((bsz, 1, D_MODEL), F32)
    return _call(
        body, (dh1, dx1, x, pre_w, sc1), name="norm1_bwd", grid=(bsz, seq // tm),
        in_specs=[_tok_spec(tm), _tok_spec(tm), _tok_spec(tm), _vec_spec(), _row_spec()],
        out_specs=[_tok_spec(tm), _row_spec(), _row_spec(), _vec_spec()],
        out_shape=[_sds(x.shape, F32), row_shape, row_shape, _sds((1, D_MODEL), F32)],
        sem=("arbitrary", "arbitrary"), comms=comms)


def _hgrn_bwd(dcat, proj, o_raw, states, lb_table, norm_w, masks, comms=()):
    bsz, seq, _ = proj.shape
    nstep = seq // HG_TOK
    rec0 = ATT_WIDTH // LANE
    width = HG_HPS * LANE
    slabs = (HG_Q0, HG_F0, HG_I0, HG_G0)
    n_steps = (HG_HEADS // HG_HPS) * bsz * nstep
    assert n_steps >= 2

    def body(tbl_ref, nw_ref, dr_ref, q_ref, f_ref, i_ref, g_ref, o_ref, st_ref, lower_ref, upper_ref,
             dproj_ref, dlb_ref, dnw_ref, ds_scr, grad_buf, grad_sem):
        h, b, t = pl.program_id(0), pl.program_id(1), pl.program_id(2)
        step = (h * bsz + b) * nstep + t
        slot = step % 2
        dq_k, df_k, di_k, dg_k = range(4)

        def grad_copies(of_step):
            hh, bb, tt = of_step // (bsz * nstep), (of_step // nstep) % bsz, of_step % nstep
            rows = pl.ds(pl.multiple_of((nstep - 1 - tt) * HG_TOK, HG_TOK), HG_TOK)
            return [pltpu.make_async_copy(
                grad_buf.at[of_step % 2, k],
                dproj_ref.at[bb, rows, pl.ds(pl.multiple_of(slabs[k] * LANE + hh * width, width), width)],
                grad_sem.at[of_step % 2, k]) for k in range(4)]

        @pl.when(step >= 2)
        def _():
            for cp in grad_copies(step - 2):
                cp.wait()

        @pl.when(t == 0)
        def _():
            ds_scr[...] = jnp.zeros_like(ds_scr)

        lower, upper = lower_ref[...], upper_ref[...]
        dlb_parts = []
        dnw_acc = jnp.zeros((1, LANE), F32)
        for hp in range(HG_HPS):
            ls = slice(hp * LANE, (hp + 1) * LANE)
            hq, v, hg = q_ref[:, ls], i_ref[:, ls], g_ref[:, ls]
            nw = nw_ref[...]
            c = _hgrn_common(tbl_ref[:, ls], f_ref[:, ls], hq)
            qd, kd, ku = c["qd"], c["kd"], c["ku"]
            y, on, rstd = _rms_fwd(o_ref[:, ls], nw)
            sg = _sigmoid(hg)
            dr = dr_ref[:, ls]
            grad_buf[slot, dg_k, :, ls] = (dr * y * (sg * (1.0 + hg * (1.0 - sg)))).astype(BF16)
            do, dnw_rows = _rms_bwd(dr * (hg * sg), on, rstd, nw)
            at = _dot(kd, qd, NT_DIMS) * upper
            da = _dot(do, v, NT_DIMS) * lower
            dat = _dot(v, do, NT_DIMS) * upper
            dv = _dot(at, do)
            dqd = _dot(da, kd)
            dkd = _dot(dat, qd)
            do_c, qd_c, v_c, ku_c = [_chunk_slices(z.astype(BF16)) for z in (do, qd, v, ku)]
            outer = [_dot(do_c[j], qd_c[j], TN_DIMS) for j in range(HG_NCH)]
            ds = ds_scr[hp]
            ds_after = [None] * HG_NCH
            for j in reversed(range(HG_NCH)):
                ds_after[j] = ds
                ds = outer[j] + ds * c["e_last"][j]
            ds_scr[hp] = ds
            updates = [_dot(v_c[j], ku_c[j], TN_DIMS) for j in range(HG_NCH)]
            states = [st_ref[hp, 0]]
            for j in range(HG_NCH - 1):
                states.append(states[j] * c["e_last"][j] + updates[j])
            dv = dv + jnp.concatenate([_dot(ku_c[j], ds_after[j], NT_DIMS) for j in range(HG_NCH)], axis=0)
            dqd = dqd + jnp.concatenate([_dot(do_c[j], states[j]) for j in range(HG_NCH)], axis=0)
            dku = jnp.concatenate([_dot(v_c[j], ds_after[j]) for j in range(HG_NCH)], axis=0)
            dku_ku = dku * ku
            dbl = [_colsum(states[j] * ds_after[j]) * c["e_last"][j] + _colsum(dku_ku[j * HG_CHUNK:(j + 1) * HG_CHUNK])
                   for j in range(HG_NCH)]
            dk = dkd * c["e_nb"] + dku * c["e_rem"]
            db = dqd * qd - dkd * kd - dku_ku + jnp.where(_row_in_chunk() == HG_CHUNK - 1, _chunk_rows(dbl), 0.0)
            dfv = _chunk_cumsum(db, reverse=True) / c["f"] - dk
            sig, sq = c["sig"], c["sq"]
            grad_buf[slot, df_k, :, ls] = (dfv * (1.0 - c["lb"]) * sig * (1.0 - sig)).astype(BF16)
            grad_buf[slot, dq_k, :, ls] = (dqd * c["e_b"] * (sq * (1.0 + hq * (1.0 - sq)))).astype(BF16)
            grad_buf[slot, di_k, :, ls] = dv.astype(BF16)
            dlb_parts.append(_colsum(dfv * (1.0 - sig)))
            dnw_acc = dnw_acc + _colsum(dnw_rows)
        _acc_out(dlb_ref, jnp.logical_and(b == 0, t == 0), jnp.concatenate(dlb_parts, axis=1))
        _acc_out(dnw_ref, jnp.logical_and(h == 0, jnp.logical_and(b == 0, t == 0)), dnw_acc)
        for cp in grad_copies(step):
            cp.start()

        @pl.when(step == n_steps - 1)
        def _():
            for cp in grad_copies(step - 1) + grad_copies(step):
                cp.wait()

    rev = lambda t: nstep - 1 - t
    slab = lambda first: pl.BlockSpec((None, HG_TOK, width), lambda h, b, t: (b, rev(t), first // HG_HPS + h))
    head = pl.BlockSpec((None, HG_TOK, width), lambda h, b, t: (b, rev(t), h))
    return _call(
        body, (lb_table, norm_w, dcat, proj, proj, proj, proj, o_raw, states, *masks), name="hgrn_bwd",
        grid=(HG_HEADS // HG_HPS, bsz, nstep),
        in_specs=[pl.BlockSpec((2, width), lambda h, b, t: (0, h)), pl.BlockSpec((1, LANE), lambda h, b, t: (0, 0)),
                  slab(rec0), slab(HG_Q0), slab(HG_F0), slab(HG_I0), slab(HG_G0), head,
                  pl.BlockSpec((None, HG_HPS, 1, LANE, LANE), lambda h, b, t: (b, h, rev(t), 0, 0)),
                  pl.BlockSpec((HG_TOK, HG_TOK), lambda h, b, t: (0, 0)), pl.BlockSpec((HG_TOK, HG_TOK), lambda h, b, t: (0, 0))],
        out_specs=[pl.BlockSpec(memory_space=pl.ANY), pl.BlockSpec((1, width), lambda h, b, t: (0, h)),
                   pl.BlockSpec((1, LANE), lambda h, b, t: (0, 0))],
        out_shape=[_sds((bsz, seq, IN_COLS), BF16), _sds((1, HG_WIDTH), F32), _sds((1, LANE), F32)],
        scratch_shapes=[pltpu.VMEM((HG_HPS, LANE, LANE), F32), pltpu.VMEM((2, 4, HG_TOK, width), BF16),
                        pltpu.SemaphoreType.DMA((2, 4))],
        sem=("arbitrary", "arbitrary", "arbitrary"), comms=comms)


def _attn_bwd(dcat, raw, w_norm, qh, kh, vh, lse, sinks, tables, biases, dproj, comms=()):
    bsz, nblk = qh.shape[0], qh.shape[1]
    seq = nblk * WINDOW
    nstep = nblk // ATT_BPS
    half = ROPE_DIM // 2

    def body(sink_ref, da_ref, raw_ref, w_ref, q_ref, kc_ref, kp_ref, vc_ref, vp_ref, l_ref, c_ref, u_ref, d_ref,
             full_ref, first_ref, dproj_ref, o_ref, dw_ref, dsink_ref, carry_k, carry_v):
        b, i = pl.program_id(0), pl.program_id(1)
        first = jnp.logical_and(b == 0, i == 0)

        @pl.when(i == 0)
        def _():
            carry_k[...] = jnp.zeros_like(carry_k)
            carry_v[...] = jnp.zeros_like(carry_v)

        w = w_ref[...]
        _, on, rstd = _rms_fwd(raw_ref[...], w)
        do_step, dw_rows = _rms_bwd(da_ref[...], on, rstd, w)
        _acc_out(dw_ref, first, _colsum(dw_rows))
        lane8 = lax.broadcasted_iota(jnp.int32, (1, ATT_Q_HEADS), 1)
        dsink = jnp.zeros((1, ATT_Q_HEADS), F32)
        head_cols = jnp.where(lax.broadcasted_iota(jnp.int32, (2 * ATT_Q_HEADS, ATT_WIDTH), 1) // ATT_HEAD_DIM
                              == lax.broadcasted_iota(jnp.int32, (2 * ATT_Q_HEADS, ATT_WIDTH), 0), 1.0, 0.0)
        from_next_k, from_next_v = carry_k[...], carry_v[...]
        for blk in reversed(range(ATT_BPS)):
            tok = slice(blk * WINDOW, (blk + 1) * WINDOW)
            bias = _band_bias(full_ref, first_ref, True if blk else i < nstep - 1)
            do_all = do_step[tok]
            c, u, d = c_ref[tok, :], u_ref[tok, :], d_ref[tok, :]
            lse_t = l_ref[tok, :].T
            prod = do_all * raw_ref[tok, :]
            prod_hi = prod.astype(BF16)
            prod_lo = prod - prod_hi.astype(F32)
            dsum_t = _dot(head_cols, prod_hi, NT_DIMS) + _dot(head_cols, prod_lo, NT_DIMS)

            def unrope(g):
                return (g * c + pltpu.roll(g * u, LANE - half, 1) + pltpu.roll(g * d, half, 1)).astype(BF16)

            dq_parts, dk_own, dk_before, dv_own, dv_before = [], [], [], [], []
            for g in range(ATT_KV_HEADS):
                heads = [slice((g * ATT_GROUP + hh) * ATT_HEAD_DIM, (g * ATT_GROUP + hh + 1) * ATT_HEAD_DIM)
                         for hh in range(ATT_GROUP)]
                q = q_ref[blk, g]
                keys, vals = _band(kp_ref, kc_ref, g, blk), _band(vp_ref, vc_ref, g, blk)
                do_g = jnp.concatenate([do_all[:, hs] for hs in heads], axis=0)
                group_row = lambda z: jnp.concatenate(
                    [z[g * ATT_GROUP + hh:g * ATT_GROUP + hh + 1, :] for hh in range(ATT_GROUP)], axis=1)
                dsum, lse_g = group_row(dsum_t), group_row(lse_t)
                p_t = jnp.exp(_dot(keys, q, NT_DIMS) * ATT_SCALE + bias - lse_g)
                sink_part = jnp.exp(_sink_row(sink_ref, g) - lse_g) * dsum
                for hh in range(ATT_GROUP):
                    head_sum = jnp.sum(sink_part[:, hh * WINDOW:(hh + 1) * WINDOW], axis=1, keepdims=True)
                    dsink = dsink - jnp.where(lane8 == g * ATT_GROUP + hh, head_sum, 0.0)
                ds_t = p_t * (_dot(vals, do_g, NT_DIMS) - dsum) * ATT_SCALE
                dq_g = _dot(ds_t, keys, TN_DIMS)
                dq_parts += [dq_g[hh * WINDOW:(hh + 1) * WINDOW] for hh in range(ATT_GROUP)]
                dk_g = _dot(ds_t, q)
                dv_g = _dot(p_t, do_g)
                dk_before.append(dk_g[:WINDOW])
                dk_own.append(dk_g[WINDOW:])
                dv_before.append(dv_g[:WINDOW])
                dv_own.append(dv_g[WINDOW:])
            per_slab = LANE // ATT_HEAD_DIM
            for s in range(ATT_WIDTH // LANE):
                slab = jnp.concatenate(dq_parts[s * per_slab:(s + 1) * per_slab], axis=1)
                o_ref[tok, s * LANE:(s + 1) * LANE] = unrope(slab)
            o_ref[tok, ATT_WIDTH:ATT_WIDTH + LANE] = unrope(jnp.concatenate(dk_own, axis=1) + from_next_k)
            o_ref[tok, ATT_WIDTH + LANE:ATT_COLS] = (jnp.concatenate(dv_own, axis=1) + from_next_v).astype(BF16)
            from_next_k, from_next_v = jnp.concatenate(dk_before, axis=1), jnp.concatenate(dv_before, axis=1)
        carry_k[...] = from_next_k
        carry_v[...] = from_next_v
        _acc_out(dsink_ref, first, dsink)

    rows = ATT_BPS * WINDOW
    rev = lambda i: nstep - 1 - i
    cur = lambda width: pl.BlockSpec((None, rows, width), lambda b, i: (b, rev(i), 0))
    q_spec = pl.BlockSpec((None, ATT_BPS, ATT_KV_HEADS, GROUP_ROWS, ATT_HEAD_DIM), lambda b, i: (b, rev(i), 0, 0, 0))
    kv_cur = pl.BlockSpec((None, ATT_KV_HEADS, rows, ATT_HEAD_DIM), lambda b, i: (b, 0, rev(i), 0))
    kv_prev = pl.BlockSpec((None, ATT_KV_HEADS, WINDOW, ATT_HEAD_DIM), lambda b, i: (b, 0, jnp.maximum(ATT_BPS * rev(i) - 1, 0), 0))
    tab = pl.BlockSpec((rows, LANE), lambda b, i: (rev(i), 0))
    return _call(
        body, (sinks, dcat, raw, w_norm, qh, kh, kh, vh, vh, lse, *tables, *biases, dproj), name="attn_bwd", grid=(bsz, nstep),
        in_specs=[pl.BlockSpec(memory_space=pltpu.SMEM), cur(ATT_WIDTH), cur(ATT_WIDTH), _vec_spec(ATT_WIDTH), q_spec,
                  kv_cur, kv_prev, kv_cur, kv_prev, cur(LANE), tab, tab, tab, _bias_spec(True), _bias_spec(True),
                  pl.BlockSpec(memory_space=pl.ANY)],
        out_specs=[cur(ATT_COLS), _vec_spec(ATT_WIDTH), _vec_spec(ATT_Q_HEADS)],
        out_shape=[_sds(dproj.shape, BF16), _sds((1, ATT_WIDTH), F32), _sds((1, ATT_Q_HEADS), F32)],
        scratch_shapes=[pltpu.VMEM((WINDOW, LANE), F32), pltpu.VMEM((WINDOW, LANE), F32)],
        sem=("arbitrary", "arbitrary"), comms=comms, aliases={15: 0})


def _other_chips(x, y):
    return [(1 - x, y), (x, 1 - y), (1 - x, 1 - y)]


def _sem_pair(n):
    return [pltpu.SemaphoreType.DMA((n,)), pltpu.SemaphoreType.DMA((n,))]


def _plan_pair_forward(bufs):
    n = len(bufs)

    def copies(outs, sems):
        x, y, c = _mesh_pos()
        sends, lands = [], []
        for a in range(n):
            for j, chip in enumerate(_other_chips(x, y)):
                k = 3 * a + j
                slot = outs[a].at[4 * chip[0] + 2 * chip[1] + c]
                sends.append(pltpu.make_async_remote_copy(
                    src_ref=slot, dst_ref=slot, send_sem=sems[0].at[k], recv_sem=sems[1].at[k],
                    device_id=(x, y, 1 - c), device_id_type=MESH))
                theirs = outs[a].at[4 * chip[0] + 2 * chip[1] + 1 - c]
                lands.append(pltpu.make_async_remote_copy(
                    src_ref=theirs, dst_ref=theirs, send_sem=sems[0].at[k], recv_sem=sems[1].at[k],
                    device_id=(x, y, 1 - c), device_id_type=MESH))
        return sends, lands

    def start(ins, outs, sems):
        for cp in copies(outs, sems)[0]:
            cp.start()

    def finish(ins, outs, sems):
        sends, lands = copies(outs, sems)
        for cp in lands:
            cp.wait_recv()
        for cp in sends:
            cp.wait_send()

    return _Comm(list(bufs), [_sds(b.shape, b.dtype) for b in bufs], _sem_pair(3 * n), start, finish,
                 aliases=[(a, a) for a in range(n)])


def _plan_pair(arrays, other_half):
    n = len(arrays)
    per = N_CHIPS if other_half == "chip_major" else 1

    def copies(ins, outs, sems):
        x, y, c = _mesh_pos()
        out = []
        for a in range(n):
            for k in range(per):
                if other_half == "chip_major":
                    src, dst = ins[a].at[k, 1 - c], outs[a].at[k]
                else:
                    src, dst = (ins[a].at[1 - c] if other_half else ins[a]), outs[a]
                out.append(pltpu.make_async_remote_copy(
                    src_ref=src, dst_ref=dst, send_sem=sems[0].at[per * a + k], recv_sem=sems[1].at[per * a + k],
                    device_id=(x, y, 1 - c), device_id_type=MESH))
        return out

    def start(ins, outs, sems):
        for cp in copies(ins, outs, sems):
            cp.start()

    def finish(ins, outs, sems):
        for cp in copies(ins, outs, sems):
            cp.wait()

    if other_half == "chip_major":
        shapes = [_sds((a.shape[0],) + a.shape[2:], a.dtype) for a in arrays]
    else:
        shapes = [_sds(a.shape[1:] if other_half else a.shape, a.dtype) for a in arrays]
    return _Comm(list(arrays), shapes, _sem_pair(per * n), start, finish)


def _plan_chip_exchange(arrays):
    n = len(arrays)

    def copies(ins, outs, sems):
        x, y, c = _mesh_pos()
        sends, lands = [], []
        for a in range(n):
            for j, chip in enumerate(_other_chips(x, y)):
                k = 3 * a + j
                sends.append(pltpu.make_async_remote_copy(
                    src_ref=ins[a].at[2 * chip[0] + chip[1]], dst_ref=outs[a].at[2 * x + y], send_sem=sems[0].at[k],
                    recv_sem=sems[1].at[k], device_id=(*chip, c), device_id_type=MESH))
                slot = outs[a].at[2 * chip[0] + chip[1]]
                lands.append(pltpu.make_async_remote_copy(
                    src_ref=slot, dst_ref=slot, send_sem=sems[0].at[k], recv_sem=sems[1].at[k],
                    device_id=(*chip, c), device_id_type=MESH))
        return sends, lands

    def start(ins, outs, sems):
        for cp in copies(ins, outs, sems)[0]:
            cp.start()

    def finish(ins, outs, sems):
        sends, lands = copies(ins, outs, sems)
        for cp in lands:
            cp.wait_recv()
        for cp in sends:
            cp.wait_send()

    return _Comm(list(arrays), [_sds(a.shape, a.dtype) for a in arrays], _sem_pair(3 * n), start, finish)


SEM_SPEC = pl.BlockSpec(memory_space=pltpu.SEMAPHORE)
N_OTHER = N_CHIPS - 1


def _exchange_copies(s_ref, land_ref, sems):
    x, y, c = _mesh_pos()
    return [pltpu.make_async_remote_copy(
        src_ref=s_ref.at[2 * chip[0] + chip[1]], dst_ref=land_ref.at[2 * x + y], send_sem=sems[j], recv_sem=sems[N_OTHER + j],
        device_id=(*chip, c), device_id_type=MESH) for j, chip in enumerate(_other_chips(x, y))]


def _exchange_start(s, name):
    def body(s_ref, land_ref, *outs):
        sems, token = outs[:2 * N_OTHER], outs[-1]
        for cp in _exchange_copies(s_ref, land_ref, sems):
            cp.start()
        token[...] = jnp.zeros_like(token)

    hbm = pltpu.HBM(s.shape, s.dtype)
    res = pl.pallas_call(
        body, name=name,
        out_shape=(pltpu.SemaphoreType.DMA(()),) * (2 * N_OTHER) + (hbm, hbm, _sds((SUBLANES, LANE), F32)),
        in_specs=(HBM_SPEC, HBM_SPEC),
        out_specs=(SEM_SPEC,) * (2 * N_OTHER) + (HBM_SPEC, HBM_SPEC, pl.BlockSpec(memory_space=pltpu.VMEM)),
        input_output_aliases={0: 2 * N_OTHER, 1: 2 * N_OTHER + 1},
        compiler_params=pltpu.CompilerParams(has_side_effects=pltpu.SideEffectType.DATAFLOW_SIDE_EFFECTING),
    )(pltpu.with_memory_space_constraint(s, pltpu.HBM), pltpu.with_memory_space_constraint(lax.empty(s.shape, s.dtype), pltpu.HBM))
    return res[:2 * N_OTHER], res[2 * N_OTHER], res[2 * N_OTHER + 1], res[-1]


def _exchange_wait(sems, s_thru, land_thru, afters, name):
    def body(s_ref, land_ref, *rest):
        for cp in _exchange_copies(s_ref, land_ref, rest[:2 * N_OTHER]):
            cp.wait_send()
            cp.wait_recv()

    hbm = pltpu.HBM(s_thru.shape, s_thru.dtype)
    return pl.pallas_call(
        body, name=name, out_shape=(hbm, hbm),
        in_specs=(HBM_SPEC, HBM_SPEC) + (SEM_SPEC,) * (2 * N_OTHER) + (pl.BlockSpec(memory_space=pl.ANY),) * len(afters),
        out_specs=(HBM_SPEC, HBM_SPEC), input_output_aliases={0: 0, 1: 1},
        compiler_params=pltpu.CompilerParams(has_side_effects=pltpu.SideEffectType.DATAFLOW_SIDE_EFFECTING),
    )(s_thru, land_thru, *sems, *afters)


def _gather_copies(block_ref, buf_ref, sems):
    x, y, c = _mesh_pos()
    return [pltpu.make_async_remote_copy(
        src_ref=block_ref, dst_ref=buf_ref.at[4 * x + 2 * y + c], send_sem=sems[j], recv_sem=sems[N_OTHER + j],
        device_id=(*chip, c), device_id_type=MESH) for j, chip in enumerate(_other_chips(x, y))]


def _gather_start(blocks, bufs, afters, name):
    n = len(blocks)
    per = 2 * N_OTHER

    def body(*refs):
        ins, outs = refs[:2 * n], refs[2 * n + len(afters):]
        for a in range(n):
            for cp in _gather_copies(ins[a], ins[n + a], outs[a * per:(a + 1) * per]):
                cp.start()
        outs[-1][...] = jnp.zeros_like(outs[-1])

    hbm = [pltpu.HBM(z.shape, z.dtype) for z in list(blocks) + list(bufs)]
    res = pl.pallas_call(
        body, name=name,
        out_shape=(pltpu.SemaphoreType.DMA(()),) * (n * per) + tuple(hbm) + (_sds((SUBLANES, LANE), F32),),
        in_specs=(HBM_SPEC,) * (2 * n) + (pl.BlockSpec(memory_space=pl.ANY),) * len(afters),
        out_specs=(SEM_SPEC,) * (n * per) + (HBM_SPEC,) * (2 * n) + (pl.BlockSpec(memory_space=pltpu.VMEM),),
        input_output_aliases={k: n * per + k for k in range(2 * n)},
        compiler_params=pltpu.CompilerParams(has_side_effects=pltpu.SideEffectType.DATAFLOW_SIDE_EFFECTING),
    )(*[pltpu.with_memory_space_constraint(z, pltpu.HBM) for z in list(blocks) + list(bufs)], *afters)
    parts = [(res[a * per:(a + 1) * per], res[n * per + a], res[n * per + n + a]) for a in range(n)]
    return parts, res[-1]


def _gather_wait(part, afters, name):
    sems, block, buf = part

    def body(block_ref, buf_ref, *rest):
        for cp in _gather_copies(block_ref, buf_ref, rest[:2 * N_OTHER]):
            cp.wait_send()
            cp.wait_recv()

    return pl.pallas_call(
        body, name=name, out_shape=(pltpu.HBM(block.shape, block.dtype), pltpu.HBM(buf.shape, buf.dtype)),
        in_specs=(HBM_SPEC, HBM_SPEC) + (SEM_SPEC,) * (2 * N_OTHER) + (pl.BlockSpec(memory_space=pl.ANY),) * len(afters),
        out_specs=(HBM_SPEC, HBM_SPEC), input_output_aliases={0: 0, 1: 1},
        compiler_params=pltpu.CompilerParams(has_side_effects=pltpu.SideEffectType.DATAFLOW_SIDE_EFFECTING),
    )(block, buf, *sems, *afters)[1]


def _comm_only(comms, name):
    return _call(lambda: None, (), name=name, grid=(), in_specs=[], out_specs=[], out_shape=[], sem=(), comms=comms)[1]


def _allgather8(arrays, name):
    return _comm_only([_plan_allgather8(arrays)], name)[0]


def _plan_allgather8(arrays):
    n = len(arrays)

    def parts(ins, outs, sems):
        send_sems, recv_sems, local_sems = sems
        x, y, c = _mesh_pos()
        me, sibling = (x, y, c), (x, y, 1 - c)
        chips = _other_chips(x, y)

        def copy(a, k, block, to, src=None):
            dst = outs[a].at[4 * block[0] + 2 * block[1] + block[2]]
            return pltpu.make_async_remote_copy(
                src_ref=dst if src is None else src, dst_ref=dst, send_sem=send_sems.at[7 * a + k],
                recv_sem=recv_sems.at[7 * a + k], device_id=to, device_id_type=MESH)

        mine = [pltpu.make_async_copy(ins[a], outs[a].at[4 * x + 2 * y + c], local_sems.at[a]) for a in range(n)]
        first = []
        for a in range(n):
            first.append(copy(a, 0, me, sibling, src=ins[a]))
            first += [copy(a, 1 + j, me, (*chip, c), src=ins[a]) for j, chip in enumerate(chips)]
        return copy, mine, first, me, sibling, chips, c

    def start(ins, outs, sems):
        _, mine, first, *_ = parts(ins, outs, sems)
        for cp in mine + first:
            cp.start()

    def finish(ins, outs, sems):
        copy, mine, first, me, sibling, chips, c = parts(ins, outs, sems)
        passed = []
        for j, chip in enumerate(chips):
            for a in range(n):
                copy(a, 1 + j, (*chip, c), me).wait_recv()
                fwd = copy(a, 4 + j, (*chip, c), sibling)
                fwd.start()
                passed.append(fwd)
        for a in range(n):
            copy(a, 0, sibling, me).wait_recv()
            for j, chip in enumerate(chips):
                copy(a, 4 + j, (*chip, 1 - c), me).wait_recv()
        for cp in first + passed:
            cp.wait_send()
        for cp in mine:
            cp.wait()

    sems = [pltpu.SemaphoreType.DMA((7 * n,)), pltpu.SemaphoreType.DMA((7 * n,)), pltpu.SemaphoreType.DMA((n,))]
    return _Comm(list(arrays), [_sds((N_DEV,) + a.shape, a.dtype) for a in arrays], sems, start, finish)


def _pair_sum(g, q, core, name, chip_major=False):
    rows, cols = g.shape[2:]
    tr = _row_tile(rows)

    def body(core_ref, g_ref, q_ref, o_ref):
        o_ref[...] = (g_ref[...] + q_ref[...]).astype(BF16)

    blk = pl.BlockSpec((None, tr, cols), lambda k, i, core_ref: (k, i, 0))
    if chip_major:
        own = pl.BlockSpec((None, None, tr, cols), lambda k, i, core_ref: (k, core_ref[0], i, 0))
    else:
        own = pl.BlockSpec((None, None, tr, cols), lambda k, i, core_ref: (core_ref[0], k, i, 0))
    return pl.pallas_call(
        body, name=name,
        grid_spec=pltpu.PrefetchScalarGridSpec(num_scalar_prefetch=1, grid=(N_CHIPS, rows // tr), in_specs=[own, blk], out_specs=blk),
        out_shape=_sds((N_CHIPS, rows, cols), BF16), compiler_params=_params("parallel", "parallel"),
    )(core, g, q)


def _sum_chips(own, landed, chip, name):
    _, rows, cols = own.shape
    tr = _row_tile(rows)

    def body(chip_ref, own_ref, a_ref, b_ref, c_ref, o_ref):
        acc = own_ref[...].astype(F32) + a_ref[...].astype(F32)
        o_ref[...] = (acc + b_ref[...].astype(F32)) + c_ref[...].astype(F32)

    blk = lambda flip: pl.BlockSpec((None, tr, cols), lambda i, chip_ref: (jnp.bitwise_xor(chip_ref[0], flip), i, 0))
    return pl.pallas_call(
        body, name=name,
        grid_spec=pltpu.PrefetchScalarGridSpec(num_scalar_prefetch=1, grid=(rows // tr,), in_specs=[blk(0), blk(1), blk(2), blk(3)],
                                               out_specs=pl.BlockSpec((tr, cols), lambda i, chip_ref: (i, 0))),
        out_shape=_sds((rows, cols), F32), compiler_params=_params("parallel"),
    )(chip, own, landed, landed, landed)


SUBLANES = 8


def _tile_rows(n_elems):
    return -(-n_elems // (SUBLANES * LANE)) * SUBLANES


SMALL_ITEMS = (("b_ada", N_MOD * D_MODEL), ("pre_w_mix", D_MODEL), ("post_w_mix", D_MODEL), ("pre_w_mlp", D_MODEL),
               ("post_w_mlp", D_MODEL), ("attn_out_w", ATT_WIDTH), ("hg_norm_w", HG_HEAD_DIM), ("attn_sinks", ATT_Q_HEADS),
               ("lb_0", HG_WIDTH), ("lb_1", HG_WIDTH))
SMALL_AT = {}
for _name, _size in SMALL_ITEMS:
    SMALL_AT[_name] = (sum(r for _, r in SMALL_AT.values()), _tile_rows(_size))
SMALL_ROWS = sum(r for _, r in SMALL_AT.values())
MOD_ROWS = SMALL_AT["b_ada"][1]
PLAIN_ROWS = SMALL_AT["lb_0"][0] - MOD_ROWS
LB_ROWS = SMALL_AT["lb_0"][1]


def _rows(a, nrows=None):
    flat = a.reshape(-1)
    nrows = _tile_rows(flat.shape[0]) if nrows is None else nrows
    return jnp.pad(flat, (0, nrows * LANE - flat.shape[0])).reshape(nrows, LANE)


def _pack_small(vals):
    vals = dict(vals, lb_0=vals["lb_table"][0], lb_1=vals["lb_table"][1])
    return jnp.concatenate([_rows(vals[name], SMALL_AT[name][1]) for name, _ in SMALL_ITEMS], axis=0)


def _unpack_small(p):
    def item(name, shape):
        first = SMALL_AT[name][0]
        size = shape[0] * shape[1]
        return p[first:first + SMALL_AT[name][1]].reshape(-1)[:size].reshape(shape)

    out = {name: item(name, (1, size)) for name, size in SMALL_ITEMS if not name.startswith("lb_")}
    out["lb_table"] = jnp.concatenate([item("lb_0", (1, HG_WIDTH)), item("lb_1", (1, HG_WIDTH))], axis=0)
    return out


def _pack_partials(dmod, plain, d_lb, loss_row):
    return jnp.concatenate([_rows(dmod, dmod.shape[0] * MOD_ROWS)] + [_rows(g) for g in plain] + [_rows(d_lb), _rows(loss_row)], axis=0)


def _small_update(packs, w, m, v, n_seq):
    mod_end = n_seq * MOD_ROWS
    lb_at = mod_end + PLAIN_ROWS
    t0, t1 = SMALL_AT["lb_0"][0], SMALL_AT["lb_1"][0]

    def body(p_ref, w_ref, m_ref, v_ref, g_ref, dl_ref, nm_ref, nv_ref, loss_ref):
        tot = p_ref[0]
        for d in range(1, N_DEV):
            tot = tot + p_ref[d]
        wv = w_ref[...]
        p1 = _sigmoid(wv[t1:t1 + LB_ROWS] - wv[t0:t0 + LB_ROWS])
        s = tot[lb_at:lb_at + LB_ROWS] * p1 * (1.0 - p1)
        g_bias = tot[0:MOD_ROWS]
        for q in range(1, n_seq):
            g_bias = g_bias + tot[q * MOD_ROWS:(q + 1) * MOD_ROWS]
        g = jnp.concatenate([g_bias, tot[mod_end:lb_at], -s, s], axis=0)
        g_ref[...] = g
        dl_ref[...], nm_ref[...], nv_ref[...] = _adamw_math(g, wv, m_ref[...], v_ref[...])
        loss_ref[...] = tot[lb_at + LB_ROWS:lb_at + LB_ROWS + SUBLANES]

    shp = _sds((SMALL_ROWS, LANE), F32)
    return pl.pallas_call(body, name="small_update", out_shape=[shp] * 4 + [_sds((SUBLANES, LANE), F32)],
                          compiler_params=_params())(packs, w, m, v)


def kernel(x, c, w_ada, b_ada, pre_w_mix, w_in, attn_sinks, attn_out_w, lb_table, hg_norm_w, w_out, post_w_mix, pre_w_mlp, w_up, w_down, post_w_mlp, loss_target, m_w_ada, m_b_ada, m_pre_w_mix, m_w_in, m_attn_sinks, m_attn_out_w, m_lb_table, m_hg_norm_w, m_w_out, m_post_w_mix, m_pre_w_mlp, m_w_up, m_w_down, m_post_w_mlp, v_w_ada, v_b_ada, v_pre_w_mix, v_w_in, v_attn_sinks, v_attn_out_w, v_lb_table, v_hg_norm_w, v_w_out, v_post_w_mix, v_pre_w_mlp, v_w_up, v_w_down, v_post_w_mlp):
    xi, yi, ci = _mesh_pos()
    chip = 2 * xi + yi
    dev = 2 * chip + ci
    bsz, seq, _ = x.shape
    ntok = bsz * seq
    ada_cols = w_ada.shape[2]
    core = jnp.reshape(ci, (1,)).astype(jnp.int32)
    chip_idx = jnp.reshape(chip, (1,)).astype(jnp.int32)
    flat = lambda a: a.reshape(ntok, a.shape[-1])
    unflat = lambda a: a.reshape(bsz, seq, a.shape[-1])
    tables = _rope_tables(seq)
    biases, chunk_masks = _band_biases(), _block_masks()

    def row_half(w):
        rows = w.shape[1] // 2
        return lax.dynamic_slice_in_dim(w[0], ci * rows, rows, axis=0).astype(BF16)

    def gather_buffer(w):
        rows, cols = w.shape[1] // 2, w.shape[2]
        own = w[0].astype(BF16).reshape(2, rows, cols)
        return lax.dynamic_update_slice(lax.empty((N_DEV, rows, cols), BF16), own, (2 * chip, 0, 0))

    w_in_t, m_in_t, v_in_t = [jnp.transpose(a[0])[None] for a in (w_in, m_w_in, v_w_in)]
    c_g, in_g = _allgather8([c, row_half(w_in_t)], "gather_first")
    c_all = c_g.reshape(N_DEV * bsz, D_MODEL)
    w_in_full = in_g.reshape(IN_COLS, D_MODEL)

    b_cols = lax.dynamic_slice_in_dim(b_ada, chip * ada_cols, ada_cols, axis=1)
    mod_part = _ada_fwd(c_all, w_ada[0], b_cols)
    half_rows = mod_part.shape[0] // 2
    (mod_g,) = _allgather8([lax.dynamic_slice_in_dim(mod_part, ci * half_rows, half_rows, axis=0)], "gather_mod")
    mod_all = mod_g.reshape(N_CHIPS, 2, half_rows, ada_cols).transpose(1, 2, 0, 3).reshape(N_DEV * bsz, N_MOD * D_MODEL)
    mod = lax.dynamic_slice_in_dim(mod_all, dev * bsz, bsz, axis=0)
    sh1, sc1, g1, sh2, sc2, g2 = [mod[:, i * D_MODEL:(i + 1) * D_MODEL].reshape(bsz, 1, D_MODEL) for i in range(N_MOD)]

    weights = (w_out, w_up, w_down)
    (out_part, up_part, down_part), started = _gather_start(
        [row_half(w) for w in weights], [gather_buffer(w) for w in weights], [mod_g], "gather_weights_start")

    h1, proj, qh, kh, vh = _in_proj_fused(x, pre_w_mix, sc1 + started[0:1, 0:1], sh1, w_in_full, tables)
    out_g = _gather_wait(out_part, [proj], "gather_out_wait")
    (attn_raw, cat, lse), ((out_g,),) = _attn_fwd(qh, kh, vh, attn_sinks, attn_out_w, biases, comms=[_plan_pair_forward([out_g])])
    up_g = _gather_wait(up_part, [attn_raw], "gather_up_wait")
    (o_raw, cat, states), ((up_g,),) = _hgrn_fwd(proj, lb_table, hg_norm_w, cat, chunk_masks, comms=[_plan_pair_forward([up_g])])
    down_g = _gather_wait(down_part, [o_raw], "gather_down_wait")
    w_out_full = out_g.reshape(D_MODEL, D_MODEL)
    w_up4 = up_g.reshape(N_CHIPS, D_MODEL, D_MODEL)
    mix, x1, h2 = _out_proj_fused(cat, w_out_full, x, post_w_mix, g1, pre_w_mlp, sc2, sh2)
    big_tm = min(ntok, 2048)
    up_spec = pl.BlockSpec((None, D_MODEL, D_MODEL), lambda i, j: (j, 0, 0))
    r, ((down_g,),) = _mm(flat(h2), w_up4, name="up_proj", out_dtype=BF16, tm=big_tm, tn=D_MODEL, n_out=D_FF, b_spec=up_spec,
                          epi=lambda acc: jnp.maximum(acc, 0.0), comms=[_plan_pair_forward([down_g])])
    w_down_full = down_g.reshape(D_FF, D_MODEL)
    square = lambda t: t * t
    loss_row, dy, dd, dg2, d_post_mlp = _down_proj_fused(unflat(r), w_down_full, x1, post_w_mlp, g2, loss_target)

    dpre = _mm(flat(dd), w_down_full, name="down_bwd", out_dtype=BF16, trans_b=True, tm=big_tm, tn=D_MODEL, extra=(r,),
               epi=lambda acc, rt: acc * (2.0 * rt.astype(F32)))
    half_rows = D_MODEL // 2
    g_down = _mm_tn(r, flat(dd), name="down_wgrad", tk=half_rows, tn=D_MODEL, a_fn=square,
                    out_shape=_sds((2, N_CHIPS, half_rows, D_MODEL), F32),
                    out_spec=pl.BlockSpec((None, None, half_rows, D_MODEL), lambda i, j: (i % 2, i // 2, 0, 0)))
    (dx1, dmix, dsc2, dsh2, dg1, d_pre_mlp, d_post_mix), ((q_down,),) = _up_bwd_fused(
        unflat(dpre), w_up4, dy, x1, mix, pre_w_mlp, sc2, post_w_mix, g1, comms=[_plan_pair([g_down], True)])
    g_up = _mm_tn(flat(h2), dpre, name="up_wgrad", tk=D_MODEL, tn=half_rows,
                  out_shape=_sds((2, N_CHIPS, half_rows, D_MODEL), F32),
                  out_spec=pl.BlockSpec((2, None, half_rows, half_rows), lambda i, j: (0, j // 2, 0, j % 2)))
    s_down = _pair_sum(g_down, q_down, core, "pair_sum_down")

    dcat, ((q_up,),) = _mm(flat(dmix), w_out_full, name="out_bwd", out_dtype=F32, trans_b=True, comms=[_plan_pair([g_up], True)])
    dcat = unflat(dcat)
    s_up = _pair_sum(g_up, q_up, core, "pair_sum_up")
    out_rows = D_MODEL // N_CHIPS
    g_out = _mm_tn(flat(cat), flat(dmix), name="out_wgrad", tk=2 * out_rows, tn=half_rows,
                   out_shape=_sds((2, N_CHIPS, out_rows, half_rows), F32),
                   out_spec=pl.BlockSpec((None, 2, out_rows, half_rows), lambda i, j: (j, i, 0, 0)))
    (dproj_rec, d_lb, d_hg_norm), ((x_down,), (q_out,)) = _hgrn_bwd(
        dcat, proj, o_raw, states, lb_table, hg_norm_w, chunk_masks, comms=[_plan_chip_exchange([s_down]), _plan_pair([g_out], True)])
    half_down = _sum_chips(s_down, x_down, chip_idx, "sum_chips_down")
    s_out = _pair_sum(g_out, q_out, core, "pair_sum_out")
    (dproj, d_attn_out, d_sinks), ((their_down,), (x_up, x_out)) = _attn_bwd(
        dcat, attn_raw, attn_out_w, qh, kh, vh, lse, attn_sinks, tables, [bias.T for bias in biases], dproj_rec,
        comms=[_plan_pair([half_down], False), _plan_chip_exchange([s_up, s_out])])
    half_up = _sum_chips(s_up, x_up, chip_idx, "sum_chips_up")
    half_out = _sum_chips(s_out, x_out, chip_idx, "sum_chips_out")
    dproj = flat(dproj)
    in_rows = IN_COLS // N_CHIPS // 2
    g_in = _mm_tn(dproj, flat(h1), name="in_wgrad", tk=2 * LANE, tn=D_MODEL).reshape(N_CHIPS, 2, in_rows, D_MODEL)
    dh1, ((q_in,), (their_up, their_out)) = _mm(
        dproj, w_in_full, name="in_bwd", out_dtype=F32,
        comms=[_plan_pair([g_in], "chip_major"), _plan_pair([half_up, half_out], False)])
    s_in = _pair_sum(g_in, q_in, core, "pair_sum_in", chip_major=True)
    in_sems, s_in, in_landing, started = _exchange_start(s_in, "exchange_in_start")
    grad_x, dsc1, dsh1, d_pre_mix = _norm1_bwd(unflat(dh1), dx1, x, pre_w_mix + started[0:1, 0:1], sc1)

    dmod = jnp.concatenate([dsh1, dsc1, dg1, dsh2, dsc2, dg2], axis=-1).reshape(bsz, N_MOD * D_MODEL)
    pack = _pack_partials(dmod, [d_pre_mix, d_post_mix, d_pre_mlp, d_post_mlp, d_attn_out, d_hg_norm, d_sinks], d_lb, loss_row)
    ((packs,),) = _comm_only([_plan_allgather8([pack])], "gather_small")
    w_small = dict(b_ada=b_ada, pre_w_mix=pre_w_mix, post_w_mix=post_w_mix, pre_w_mlp=pre_w_mlp, post_w_mlp=post_w_mlp,
                   attn_out_w=attn_out_w, hg_norm_w=hg_norm_w, attn_sinks=attn_sinks, lb_table=lb_table)
    m_small = dict(b_ada=m_b_ada, pre_w_mix=m_pre_w_mix, post_w_mix=m_post_w_mix, pre_w_mlp=m_pre_w_mlp, post_w_mlp=m_post_w_mlp,
                   attn_out_w=m_attn_out_w, hg_norm_w=m_hg_norm_w, attn_sinks=m_attn_sinks, lb_table=m_lb_table)
    v_small = dict(b_ada=v_b_ada, pre_w_mix=v_pre_w_mix, post_w_mix=v_post_w_mix, pre_w_mlp=v_pre_w_mlp, post_w_mlp=v_post_w_mlp,
                   attn_out_w=v_attn_out_w, hg_norm_w=v_hg_norm_w, attn_sinks=v_attn_sinks, lb_table=v_lb_table)
    *small_packed, loss_rows = _small_update(packs, _pack_small(w_small), _pack_small(m_small), _pack_small(v_small), bsz)
    small_out = [_unpack_small(p) for p in small_packed]
    loss = loss_rows[0, 0]

    dmod_all = packs[:, :bsz * MOD_ROWS, :].reshape(N_DEV * bsz, N_MOD * D_MODEL)
    dmod_cols = lax.dynamic_slice_in_dim(dmod_all, chip * ada_cols, ada_cols, axis=1)
    ada_out = _ada_bwd_adamw(c_all, dmod_cols, w_ada[0], m_w_ada[0], v_w_ada[0])

    s_in, x_in = _exchange_wait(in_sems, s_in, in_landing, [grad_x, ada_out[0]], "exchange_in_wait")
    half_in = _sum_chips(s_in, x_in, chip_idx, "sum_chips_in")
    ((their_in,),) = _comm_only([_plan_pair([half_in], False)], "pair_swap_in")
    big = dict(
        w_in=tuple(jnp.transpose(a) for a in _adamw_halves(half_in, their_in, core, w_in_t[0], m_in_t[0], v_in_t[0], axis=0,
                                                           name="adamw_in")),
        w_up=tuple(_adamw_halves(half_up, their_up, core, w_up[0], m_w_up[0], v_w_up[0], axis=0, name="adamw_up")),
        w_out=tuple(_adamw_halves(half_out, their_out, core, w_out[0], m_w_out[0], v_w_out[0], axis=1, name="adamw_out")),
        w_down=tuple(_adamw_halves(half_down, their_down, core, w_down[0], m_w_down[0], v_w_down[0], axis=0, name="adamw_down")),
        w_ada=tuple(ada_out),
    )
    order = ("w_ada", "b_ada", "pre_w_mix", "w_in", "attn_sinks", "attn_out_w", "lb_table", "hg_norm_w", "w_out", "post_w_mix",
             "pre_w_mlp", "w_up", "w_down", "post_w_mlp")
    outs = [loss, grad_x]
    for kind in range(4):
        for nm in order:
            outs.append(big[nm][kind][None] if nm in big else small_out[kind][nm])
    return tuple(outs)
```

```python
import jax
import jax.numpy as jnp
from jax import lax
from jax.experimental import pallas as pl
from jax.experimental.pallas import tpu as pltpu

F32 = jnp.float32
BF16 = jnp.bfloat16

D_MODEL = 1024
ATT_WIDTH = 512
ATT_HEAD_DIM = 64
ATT_Q_HEADS = 8
ATT_KV_HEADS = 2
ATT_GROUP = ATT_Q_HEADS // ATT_KV_HEADS
ATT_KV_COLS = ATT_KV_HEADS * ATT_HEAD_DIM
WINDOW = 128
ROPE_DIM = 16
ROPE_THETA = 500000.0
HG_WIDTH = 512
MIX_WIDTH = ATT_WIDTH + HG_WIDTH
HG_HEAD_DIM = 128
HG_HEADS = 4
HG_CHUNK = 32
IN_COLS = ATT_WIDTH + 2 * ATT_KV_COLS + 4 * HG_WIDTH
ATT_COLS = ATT_WIDTH + 2 * ATT_KV_COLS
D_FF = 4 * D_MODEL
N_MOD = 6
EPS = 1e-6
ATT_SCALE = ATT_HEAD_DIM ** -0.5

ADAM_LR = 0.001
ADAM_B1 = 0.9
ADAM_B2 = 0.999
ADAM_EPS = 1e-08
ADAM_WD = 0.01
ADAM_STEP = 10

N_CHIPS = 4
N_DEV = 8
LANE = 128
VMEM_LIMIT = 48 * 1024 * 1024
VMEM_LIMIT_BIG = 58 * 1024 * 1024
MESH = pl.DeviceIdType.MESH

NT_DIMS = (((1,), (1,)), ((), ()))
TN_DIMS = (((0,), (0,)), ((), ()))


def _sds(shape, dtype):
    return jax.ShapeDtypeStruct(tuple(shape), dtype)


def _params(*sem, vmem_limit=None):
    return pltpu.CompilerParams(dimension_semantics=sem, vmem_limit_bytes=VMEM_LIMIT if vmem_limit is None else vmem_limit)


def _sigmoid(x):
    return 1.0 / (1.0 + jnp.exp(-x))


def _dot(a, b, dims=None):
    a, b = a.astype(BF16), b.astype(BF16)
    if dims is None:
        return jnp.dot(a, b, preferred_element_type=F32)
    return lax.dot_general(a, b, dims, preferred_element_type=F32)


def _rms_fwd(x, w):
    rstd = lax.rsqrt(jnp.mean(x * x, axis=-1, keepdims=True) + EPS)
    xh = x * rstd
    return xh * w, xh, rstd


def _rms_bwd(dy, xh, rstd, w):
    dxh = dy * w
    dx = rstd * (dxh - xh * jnp.mean(dxh * xh, axis=-1, keepdims=True))
    return dx, dy * xh


def _colsum(x):
    return jnp.sum(x, axis=0, keepdims=True)


def _rms_hat(x):
    rstd = lax.rsqrt(jnp.mean(x * x, axis=-1, keepdims=True) + EPS)
    return x * rstd, rstd


def _rms_bwd_gain(dy, gain, xh, rstd):
    dxh = dy * gain
    dx = rstd * (dxh - xh * jnp.mean(dxh * xh, axis=-1, keepdims=True))
    return dx, _colsum(dy * xh)


def _row_tile(rows, cap=256):
    return max(t for t in range(16, cap + 1, 16) if rows % t == 0)


HBM_SPEC = pl.BlockSpec(memory_space=pltpu.HBM)


def _mesh_pos():
    return lax.axis_index("x"), lax.axis_index("y"), lax.axis_index("c")


class _Comm:
    def __init__(self, ins, outs, sems, start, finish, aliases=()):
        self.ins, self.outs, self.sems = list(ins), list(outs), list(sems)
        self.start, self.finish, self.aliases = start, finish, tuple(aliases)


def _call(body, args, *, name, grid, in_specs, out_specs, out_shape, sem, scratch_shapes=(), comms=(), aliases=None,
          vmem_limit=None):
    scratch_shapes = list(scratch_shapes)
    if not comms:
        return pl.pallas_call(body, name=name, grid=grid, in_specs=in_specs, out_specs=out_specs, out_shape=out_shape,
                              input_output_aliases=dict(aliases or {}), scratch_shapes=scratch_shapes,
                              compiler_params=_params(*sem, vmem_limit=vmem_limit))(*args)
    single = not isinstance(out_shape, (list, tuple))
    out_specs_l = [out_specs] if single else list(out_specs)
    out_shape_l = [out_shape] if single else list(out_shape)
    n_in, n_out, n_scr = len(in_specs), len(out_shape_l), len(scratch_shapes)
    n_ci = [len(cm.ins) for cm in comms]
    n_co = [len(cm.outs) for cm in comms]
    n_cs = [len(cm.sems) for cm in comms]
    aliases = dict(aliases or {})
    for k, cm in enumerate(comms):
        for i, o in cm.aliases:
            aliases[n_in + sum(n_ci[:k]) + i] = n_out + sum(n_co[:k]) + o

    def fused(*refs):
        pos = [0]

        def take(n):
            part = refs[pos[0]:pos[0] + n]
            pos[0] += n
            return part

        ins = take(n_in)
        c_ins = [take(n) for n in n_ci]
        outs = take(n_out)
        c_outs = [take(n) for n in n_co]
        scr = take(n_scr)
        c_sems = [take(n) for n in n_cs]
        first, last = True, True
        for d, size in enumerate(grid):
            first = jnp.logical_and(first, pl.program_id(d) == 0)
            last = jnp.logical_and(last, pl.program_id(d) == size - 1)

        def run(which):
            for cm, ci, co, cs in zip(comms, c_ins, c_outs, c_sems):
                getattr(cm, which)(ci, co, cs)

        if grid:
            pl.when(first)(lambda: run("start"))
        else:
            run("start")
        body(*ins, *outs, *scr)
        if grid:
            pl.when(last)(lambda: run("finish"))
        else:
            run("finish")

    res = pl.pallas_call(
        fused, name=name, grid=grid, in_specs=list(in_specs) + [HBM_SPEC] * sum(n_ci),
        out_specs=out_specs_l + [HBM_SPEC] * sum(n_co), out_shape=out_shape_l + [s for cm in comms for s in cm.outs],
        input_output_aliases=aliases, scratch_shapes=scratch_shapes + [s for cm in comms for s in cm.sems],
        compiler_params=_params(*["arbitrary"] * len(grid), vmem_limit=vmem_limit),
    )(*args, *[a for cm in comms for a in cm.ins])
    main = res[:n_out]
    extra, at = [], n_out
    for n in n_co:
        extra.append(list(res[at:at + n]))
        at += n
    return (main[0] if single else list(main)), extra


def _mm(a, b, *, name, out_dtype, trans_b=False, tm=512, tn=None, extra=(), epi=None, b_spec=None, n_out=None, comms=()):
    m_total, k_total = a.shape
    if n_out is None:
        n_out = b.shape[0] if trans_b else b.shape[1]
    tn = n_out if tn is None else tn
    grid = (m_total // tm, n_out // tn)
    dims = NT_DIMS if trans_b else None

    def body(*refs):
        a_ref, b_ref = refs[0], refs[1]
        extra_refs = refs[2:2 + len(extra)]
        o_ref = refs[2 + len(extra)]
        acc = _dot(a_ref[...], b_ref[...], dims)
        if epi is not None:
            acc = epi(acc, *[r[...] for r in extra_refs])
        o_ref[...] = acc.astype(out_dtype)

    if b_spec is None:
        if trans_b:
            b_spec = pl.BlockSpec((tn, k_total), lambda i, j: (j, 0))
        else:
            b_spec = pl.BlockSpec((k_total, tn), lambda i, j: (0, j))
    in_specs = [pl.BlockSpec((tm, k_total), lambda i, j: (i, 0)), b_spec]
    in_specs += [pl.BlockSpec((tm, tn), lambda i, j: (i, j)) for _ in extra]
    return _call(
        body, (a, b, *extra), name=name, grid=grid, in_specs=in_specs,
        out_specs=pl.BlockSpec((tm, tn), lambda i, j: (i, j)),
        out_shape=_sds((m_total, n_out), out_dtype),
        sem=("parallel", "parallel"), comms=comms)


def _mm_tn(a, b, *, name, tk, tn, a_fn=None, out_shape=None, out_spec=None):
    m_total, k_total = a.shape
    n_total = b.shape[1]
    grid = (k_total // tk, n_total // tn)

    def body(a_ref, b_ref, o_ref):
        av = a_ref[...]
        part = _dot(av if a_fn is None else a_fn(av), b_ref[...], TN_DIMS)
        o_ref[...] = part.reshape(o_ref.shape)

    if out_shape is None:
        out_shape = _sds((k_total, n_total), F32)
        out_spec = pl.BlockSpec((tk, tn), lambda i, j: (i, j))
    return pl.pallas_call(
        body, name=name, grid=grid,
        in_specs=[pl.BlockSpec((m_total, tk), lambda i, j: (0, i)), pl.BlockSpec((m_total, tn), lambda i, j: (0, j))],
        out_specs=out_spec, out_shape=out_shape,
        compiler_params=_params("parallel", "parallel"),
    )(a, b)


def _ada_fwd(c_all, w_shard, b_shard):
    nb, ncol = c_all.shape[0], w_shard.shape[1]
    tn = 512

    def body(c_ref, w_ref, b_ref, o_ref):
        c = c_ref[...]
        o_ref[...] = _dot(c * _sigmoid(c), w_ref[...]) + b_ref[...]

    return pl.pallas_call(
        body, name="ada_fwd", grid=(ncol // tn,),
        in_specs=[pl.BlockSpec((nb, D_MODEL), lambda j: (0, 0)), pl.BlockSpec((D_MODEL, tn), lambda j: (0, j)),
                  pl.BlockSpec((1, tn), lambda j: (0, j))],
        out_specs=pl.BlockSpec((nb, tn), lambda j: (0, j)), out_shape=_sds((nb, ncol), F32),
        compiler_params=_params("parallel"),
    )(c_all, w_shard, b_shard)


def _adamw_math(g, w, m, v):
    m = ADAM_B1 * m + (1.0 - ADAM_B1) * g
    v = ADAM_B2 * v + (1.0 - ADAM_B2) * (g * g)
    m_hat = m / (1.0 - ADAM_B1 ** ADAM_STEP)
    v_hat = v / (1.0 - ADAM_B2 ** ADAM_STEP)
    delta = -ADAM_LR * (m_hat / (jnp.sqrt(v_hat) + ADAM_EPS) + ADAM_WD * w)
    return delta, m, v


def _ada_bwd_adamw(c_all, dmod_cols, w, m, v):
    nb, ncol = dmod_cols.shape
    tn = 256

    def body(c_ref, d_ref, w_ref, m_ref, v_ref, g_ref, dl_ref, nm_ref, nv_ref):
        c = c_ref[...]
        g = _dot(c * _sigmoid(c), d_ref[...], TN_DIMS)
        g_ref[...] = g
        dl_ref[...], nm_ref[...], nv_ref[...] = _adamw_math(g, w_ref[...], m_ref[...], v_ref[...])

    col = pl.BlockSpec((D_MODEL, tn), lambda j: (0, j))
    shp = _sds((D_MODEL, ncol), F32)
    return pl.pallas_call(
        body, name="ada_bwd_adamw", grid=(ncol // tn,),
        in_specs=[pl.BlockSpec((nb, D_MODEL), lambda j: (0, 0)), pl.BlockSpec((nb, tn), lambda j: (0, j)), col, col, col],
        out_specs=[col, col, col, col], out_shape=[shp, shp, shp, shp],
        compiler_params=_params("parallel"),
    )(c_all, dmod_cols, w, m, v)


def _adamw_halves(own, theirs, core, w, m, v, *, axis, name):
    r2, c2 = own.shape
    tr = _row_tile(r2)
    nt = r2 // tr

    def body(core_ref, own_ref, their_ref, w_ref, m_ref, v_ref, g_ref, dl_ref, nm_ref, nv_ref):
        g = jnp.where(pl.program_id(0) == core_ref[0], own_ref[...], their_ref[...])
        g_ref[...] = g
        dl_ref[...], nm_ref[...], nv_ref[...] = _adamw_math(g, w_ref[...], m_ref[...], v_ref[...])

    if axis == 0:
        full = pl.BlockSpec((tr, c2), lambda h, i, core_ref: (h * nt + i, 0))
    else:
        full = pl.BlockSpec((tr, c2), lambda h, i, core_ref: (i, h))
    half = pl.BlockSpec((tr, c2), lambda h, i, core_ref: (i, 0))
    shp = _sds(w.shape, F32)
    return pl.pallas_call(
        body, name=name,
        grid_spec=pltpu.PrefetchScalarGridSpec(num_scalar_prefetch=1, grid=(2, nt), in_specs=[half, half, full, full, full],
                                               out_specs=[full] * 4),
        out_shape=[shp] * 4, compiler_params=_params("parallel", "parallel"),
    )(core, own, theirs, w, m, v)


def _tok_spec(tm, width=D_MODEL):
    return pl.BlockSpec((None, tm, width), lambda b, i: (b, i, 0))


def _row_spec(width=D_MODEL):
    return pl.BlockSpec((None, 1, width), lambda b, i: (b, 0, 0))


def _vec_spec(width=D_MODEL):
    return pl.BlockSpec((1, width), lambda b, i: (0, 0))


class _RowsOf:
    def __init__(self, ref, first, count):
        self.ref, self.rows = ref, slice(first, first + count)

    def __getitem__(self, idx):
        return self.ref[self.rows, :]

    def __setitem__(self, idx, value):
        self.ref[self.rows, :] = value


def _mm_rows(a, b, *, name, tm, extra, extra_specs, out_specs, out_shape, epi, pro=None, trans_b=False, b_chunks=1, comms=(),
             parts=1, zero_per_seq=(), zero_once=(), vmem_limit=None):
    bsz, seq, k_total = a.shape
    kc = k_total // b_chunks
    dims = NT_DIMS if trans_b else None
    rows = tm // parts

    def body(*refs):
        a_ref, b_ref = refs[0], refs[1]
        ex, outs = refs[2:2 + len(extra)], refs[2 + len(extra):]
        if zero_per_seq:
            @pl.when(pl.program_id(1) == 0)
            def _():
                for k in zero_per_seq:
                    outs[k][...] = jnp.zeros_like(outs[k])
        if zero_once:
            @pl.when(jnp.logical_and(pl.program_id(0) == 0, pl.program_id(1) == 0))
            def _():
                for k in zero_once:
                    outs[k][...] = jnp.zeros_like(outs[k])

        def part_of(ref, p):
            tiled = len(ref.shape) == 2 and ref.shape[0] == tm
            return _RowsOf(ref, p * rows, rows) if tiled and parts > 1 else ref

        accs = []
        for p in range(parts):
            a_p, ex_p, outs_p = part_of(a_ref, p), [part_of(r, p) for r in ex], [part_of(r, p) for r in outs]
            if b_chunks == 1:
                accs.append(_dot(a_p[...] if pro is None else pro(a_p, ex_p, outs_p), b_ref[...], dims))
            else:
                acc = _dot(a_p[...][:, 0:kc], b_ref[0], NT_DIMS)
                for k in range(1, b_chunks):
                    acc = acc + _dot(a_p[...][:, k * kc:(k + 1) * kc], b_ref[k], NT_DIMS)
                accs.append(acc)
        for p in range(parts):
            epi(accs[p], [part_of(r, p) for r in ex], [part_of(r, p) for r in outs])

    b_spec = pl.BlockSpec(b.shape, lambda bb, i: (0,) * b.ndim)
    return _call(
        body, (a, b, *extra), name=name, grid=(bsz, seq // tm), in_specs=[_tok_spec(tm, k_total), b_spec, *extra_specs],
        out_specs=out_specs, out_shape=out_shape, sem=("arbitrary", "arbitrary"), comms=comms, vmem_limit=vmem_limit)


def _in_proj_fused(x, w, sc, sh, w_in_t, tables, comms=()):
    tm = 512
    bsz, seq, _ = x.shape
    half = ROPE_DIM // 2
    heads_per_slab = LANE // ATT_HEAD_DIM

    def pro(x_ref, ex, outs):
        y, _, _ = _rms_fwd(x_ref[...], ex[0][...])
        h = (y * (1.0 + ex[1][...]) + ex[2][...]).astype(BF16)
        outs[0][...] = h
        return h

    def epi(acc, ex, outs):
        c, u, d = ex[3][...], ex[4][...], ex[5][...]
        _, proj_ref, q_ref, k_ref, v_ref = outs
        proj_ref[...] = acc

        def rope(z):
            return (z * c + pltpu.roll(z, half, 1) * u + pltpu.roll(z, LANE - half, 1) * d).astype(BF16)

        for s in range(ATT_WIDTH // LANE):
            slab = rope(acc[:, s * LANE:(s + 1) * LANE])
            for part in range(heads_per_slab):
                g, hh = divmod(s * heads_per_slab + part, ATT_GROUP)
                piece = slab[:, part * ATT_HEAD_DIM:(part + 1) * ATT_HEAD_DIM]
                for blk in range(tm // WINDOW):
                    q_ref[blk, g, hh * WINDOW:(hh + 1) * WINDOW, :] = piece[blk * WINDOW:(blk + 1) * WINDOW]
        rk = rope(acc[:, ATT_WIDTH:ATT_WIDTH + LANE])
        vv = acc[:, ATT_WIDTH + LANE:ATT_COLS].astype(BF16)
        for g in range(ATT_KV_HEADS):
            k_ref[g] = rk[:, g * ATT_HEAD_DIM:(g + 1) * ATT_HEAD_DIM]
            v_ref[g] = vv[:, g * ATT_HEAD_DIM:(g + 1) * ATT_HEAD_DIM]

    cols = w_in_t.shape[0]
    tab = pl.BlockSpec((tm, LANE), lambda b, i: (i, 0))
    kv_spec = pl.BlockSpec((None, ATT_KV_HEADS, tm, ATT_HEAD_DIM), lambda b, i: (b, 0, i, 0))
    kv_shape = _sds((bsz, ATT_KV_HEADS, seq, ATT_HEAD_DIM), BF16)
    q_spec = pl.BlockSpec((None, tm // WINDOW, ATT_KV_HEADS, GROUP_ROWS, ATT_HEAD_DIM), lambda b, i: (b, i, 0, 0, 0))
    return _mm_rows(x, w_in_t, name="in_proj", tm=tm, extra=(w, sc, sh, *tables),
                    extra_specs=[_vec_spec(), _row_spec(), _row_spec(), tab, tab, tab],
                    out_specs=[_tok_spec(tm), _tok_spec(tm, cols), q_spec, kv_spec, kv_spec],
                    out_shape=[_sds(x.shape, BF16), _sds((bsz, seq, cols), F32),
                               _sds((bsz, seq // WINDOW, ATT_KV_HEADS, GROUP_ROWS, ATT_HEAD_DIM), BF16), kv_shape, kv_shape],
                    pro=pro, epi=epi, trans_b=True, comms=comms)


def _rope_tables(seq):
    half = ROPE_DIM // 2
    inv_freq = ROPE_THETA ** (-jnp.arange(0, ROPE_DIM, 2, dtype=F32) / ROPE_DIM)
    ang = jnp.arange(seq, dtype=F32)[:, None] * inv_freq[None, :]
    cos, sin = jnp.cos(ang), jnp.sin(ang)
    rest = ATT_HEAD_DIM - ROPE_DIM
    ones, zeros, zh = jnp.ones((seq, rest), F32), jnp.zeros((seq, rest), F32), jnp.zeros((seq, half), F32)
    reps = LANE // ATT_HEAD_DIM
    t_cos = jnp.tile(jnp.concatenate([cos, cos, ones], axis=1), (1, reps))
    t_up = jnp.tile(jnp.concatenate([zh, sin, zeros], axis=1), (1, reps))
    t_dn = jnp.tile(jnp.concatenate([-sin, zh, zeros], axis=1), (1, reps))
    return t_cos, t_up, t_dn


GROUP_ROWS = ATT_GROUP * WINDOW


ATT_BPS = 2


MASKED = -1e30


def _band_biases():
    row = jnp.arange(GROUP_ROWS)[:, None] % WINDOW
    col = jnp.arange(2 * WINDOW)[None, :]
    own = jnp.logical_and(col >= WINDOW, col - WINDOW <= row)
    before = jnp.logical_and(col < WINDOW, col > row)
    return (jnp.where(jnp.logical_or(own, before), 0.0, MASKED).astype(F32), jnp.where(own, 0.0, MASKED).astype(F32))


def _band_bias(full_ref, first_ref, has_prev):
    return full_ref[...] if has_prev is True else jnp.where(has_prev, full_ref[...], first_ref[...])


def _sink_column(sink_ref, g):
    head = lax.broadcasted_iota(jnp.int32, (GROUP_ROWS, 1), 0) // WINDOW
    col = jnp.full((GROUP_ROWS, 1), sink_ref[0, g * ATT_GROUP], F32)
    for hh in range(1, ATT_GROUP):
        col = jnp.where(head == hh, sink_ref[0, g * ATT_GROUP + hh], col)
    return col


def _sink_row(sink_ref, g):
    return jnp.concatenate([jnp.full((1, WINDOW), sink_ref[0, g * ATT_GROUP + hh], F32) for hh in range(ATT_GROUP)], axis=1)


def _bias_spec(transposed=False):
    shape = (2 * WINDOW, GROUP_ROWS) if transposed else (GROUP_ROWS, 2 * WINDOW)
    return pl.BlockSpec(shape, lambda b, i: (0, 0))


def _attn_specs():
    q_spec = pl.BlockSpec((None, ATT_BPS, ATT_KV_HEADS, GROUP_ROWS, ATT_HEAD_DIM), lambda b, i: (b, i, 0, 0, 0))
    kv_cur = pl.BlockSpec((None, ATT_KV_HEADS, ATT_BPS * WINDOW, ATT_HEAD_DIM), lambda b, i: (b, 0, i, 0))
    kv_prev = pl.BlockSpec((None, ATT_KV_HEADS, WINDOW, ATT_HEAD_DIM), lambda b, i: (b, 0, jnp.maximum(ATT_BPS * i - 1, 0), 0))
    return q_spec, kv_cur, kv_prev


def _band(prev_ref, cur_ref, g, blk):
    own = cur_ref[g, blk * WINDOW:(blk + 1) * WINDOW]
    before = prev_ref[g] if blk == 0 else cur_ref[g, (blk - 1) * WINDOW:blk * WINDOW]
    return jnp.concatenate([before, own], axis=0)


def _attn_fwd(qh, kh, vh, sinks, w_norm, biases, comms=()):
    bsz, nblk = qh.shape[0], qh.shape[1]
    seq = nblk * WINDOW
    rows = ATT_BPS * WINDOW

    def body(sink_ref, q_ref, kc_ref, kp_ref, vc_ref, vp_ref, w_ref, full_ref, first_ref, raw_ref, an_ref, l_ref):
        l_ref[...] = jnp.zeros_like(l_ref)
        for blk in range(ATT_BPS):
            bias = _band_bias(full_ref, first_ref, True if blk else pl.program_id(1) > 0)
            for g in range(ATT_KV_HEADS):
                keys, vals = _band(kp_ref, kc_ref, g, blk), _band(vp_ref, vc_ref, g, blk)
                sink = _sink_column(sink_ref, g)
                s = _dot(q_ref[blk, g], keys, NT_DIMS) * ATT_SCALE + bias
                m = jnp.maximum(jnp.max(s, axis=-1, keepdims=True), sink)
                p = jnp.exp(s - m)
                den = jnp.sum(p, axis=-1, keepdims=True) + jnp.exp(sink - m)
                o = _dot(p / den, vals)
                lse = m + jnp.log(den)
                tok = slice(blk * WINDOW, (blk + 1) * WINDOW)
                for hh in range(ATT_GROUP):
                    h = g * ATT_GROUP + hh
                    raw_ref[tok, h * ATT_HEAD_DIM:(h + 1) * ATT_HEAD_DIM] = o[hh * WINDOW:(hh + 1) * WINDOW]
                    l_ref[tok, h:h + 1] = lse[hh * WINDOW:(hh + 1) * WINDOW]
        y, _, _ = _rms_fwd(raw_ref[...], w_ref[...])
        an_ref[...] = y.astype(BF16)

    cur = lambda width: pl.BlockSpec((None, rows, width), lambda b, i: (b, i, 0))
    q_spec, kv_cur, kv_prev = _attn_specs()
    return _call(
        body, (sinks, qh, kh, kh, vh, vh, w_norm, *biases), name="attn_fwd", grid=(bsz, nblk // ATT_BPS),
        in_specs=[pl.BlockSpec(memory_space=pltpu.SMEM), q_spec, kv_cur, kv_prev, kv_cur, kv_prev, _vec_spec(ATT_WIDTH),
                  _bias_spec(), _bias_spec()],
        out_specs=[cur(ATT_WIDTH), cur(ATT_WIDTH), cur(LANE)],
        out_shape=[_sds((bsz, seq, ATT_WIDTH), F32), _sds((bsz, seq, MIX_WIDTH), BF16), _sds((bsz, seq, LANE), F32)],
        sem=("parallel", "parallel"), comms=comms)


HG_Q0 = ATT_COLS // LANE
HG_F0 = HG_Q0 + HG_HEADS
HG_I0 = HG_F0 + HG_HEADS
HG_G0 = HG_I0 + HG_HEADS
HG_TOK = 256
HG_NCH = HG_TOK // HG_CHUNK
HG_HPS = 2


def _block_masks():
    row = jnp.arange(HG_TOK)[:, None]
    col = jnp.arange(HG_TOK)[None, :]
    same = (row // HG_CHUNK) == (col // HG_CHUNK)
    return jnp.logical_and(same, col <= row).astype(F32), jnp.logical_and(same, col >= row).astype(F32)


def _row_in_chunk():
    return lax.broadcasted_iota(jnp.int32, (HG_TOK, LANE), 0) % HG_CHUNK


def _chunk_cumsum(x, reverse=False):
    ric = _row_in_chunk()
    shift = 1
    while shift < HG_CHUNK:
        if reverse:
            x = x + jnp.where(ric < HG_CHUNK - shift, pltpu.roll(x, HG_TOK - shift, 0), 0.0)
        else:
            x = x + jnp.where(ric >= shift, pltpu.roll(x, shift, 0), 0.0)
        shift *= 2
    return x


def _chunk_rows(rows):
    stacked = jnp.concatenate([r[None] for r in rows], axis=0)
    return jnp.broadcast_to(stacked, (HG_NCH, HG_CHUNK, LANE)).reshape(HG_TOK, LANE)


def _chunk_slices(x):
    return [x[j * HG_CHUNK:(j + 1) * HG_CHUNK] for j in range(HG_NCH)]


def _hgrn_common(tbl, hf, hq):
    lb = _sigmoid(tbl[1:2] - tbl[0:1])
    sig = _sigmoid(hf)
    f = lb + (1.0 - lb) * sig
    sq = _sigmoid(hq)
    q, k = hq * sq, 1.0 - f
    b = _chunk_cumsum(jnp.log(f))
    last = [b[(j + 1) * HG_CHUNK - 1:(j + 1) * HG_CHUNK] for j in range(HG_NCH)]
    bl = _chunk_rows(last)
    e_b, e_nb, e_rem = jnp.exp(b), jnp.exp(-b), jnp.exp(bl - b)
    e_last = [jnp.exp(r) for r in last]
    return dict(lb=lb, sig=sig, f=f, sq=sq, q=q, k=k, e_b=e_b, e_nb=e_nb, e_rem=e_rem, e_last=e_last,
                qd=q * e_b, kd=k * e_nb, ku=k * e_rem)


def _hgrn_fwd(proj, lb_table, norm_w, mix_in, masks, comms=()):
    bsz, seq, _ = proj.shape
    nstep = seq // HG_TOK

    def body(tbl_ref, nw_ref, q_ref, f_ref, i_ref, g_ref, mix_ref, lower_ref, o_ref, rec_ref, st_ref, s_scr):
        @pl.when(pl.program_id(2) == 0)
        def _():
            s_scr[...] = jnp.zeros_like(s_scr)

        lower = lower_ref[...]
        for hp in range(HG_HPS):
            ls = slice(hp * LANE, (hp + 1) * LANE)
            v, hg = i_ref[:, ls], g_ref[:, ls]
            t = _hgrn_common(tbl_ref[:, ls], f_ref[:, ls], q_ref[:, ls])
            a = _dot(t["qd"], t["kd"], NT_DIMS) * lower
            o_intra = _dot(a, v)
            v_c, ku_c, qd_c = [_chunk_slices(z.astype(BF16)) for z in (v, t["ku"], t["qd"])]
            updates = [_dot(v_c[j], ku_c[j], TN_DIMS) for j in range(HG_NCH)]
            st = s_scr[hp]
            states = []
            for j in range(HG_NCH):
                states.append(st)
                st = st * t["e_last"][j] + updates[j]
            s_scr[hp] = st
            o = o_intra + jnp.concatenate([_dot(qd_c[j], states[j], NT_DIMS) for j in range(HG_NCH)], axis=0)
            st_ref[hp, 0] = states[0]
            o_ref[:, ls] = o
            y, _, _ = _rms_fwd(o, nw_ref[...])
            rec_ref[:, ls] = (y * (hg * _sigmoid(hg))).astype(BF16)

    width = HG_HPS * LANE
    slab = lambda first: pl.BlockSpec((None, HG_TOK, width), lambda b, h, t: (b, t, first // HG_HPS + h))
    head_out = pl.BlockSpec((None, HG_TOK, width), lambda b, h, t: (b, t, h))
    mix_out = pl.BlockSpec((None, HG_TOK, width), lambda b, h, t: (b, t, ATT_WIDTH // width + h))
    return _call(
        body, (lb_table, norm_w, proj, proj, proj, proj, mix_in, masks[0]), name="hgrn_fwd", grid=(bsz, HG_HEADS // HG_HPS, nstep),
        in_specs=[pl.BlockSpec((2, width), lambda b, h, t: (0, h)), pl.BlockSpec((1, LANE), lambda b, h, t: (0, 0)),
                  slab(HG_Q0), slab(HG_F0), slab(HG_I0), slab(HG_G0), pl.BlockSpec(memory_space=pl.ANY),
                  pl.BlockSpec((HG_TOK, HG_TOK), lambda b, h, t: (0, 0))],
        out_specs=[head_out, mix_out,
                   pl.BlockSpec((None, HG_HPS, 1, LANE, LANE), lambda b, h, t: (b, h, t, 0, 0))],
        out_shape=[_sds((bsz, seq, HG_WIDTH), F32), _sds(mix_in.shape, BF16),
                   _sds((bsz, HG_HEADS, nstep, LANE, LANE), F32)],
        scratch_shapes=[pltpu.VMEM((HG_HPS, LANE, LANE), F32)],
        sem=("parallel", "parallel", "arbitrary"), comms=comms, aliases={6: 1})


def _out_proj_fused(cat, w_out, x, post_w, g1, pre_w, sc2, sh2):
    tm = 512

    def epi(mix, ex, outs):
        x_ref, pw_ref, g1_ref, w2_ref, sc_ref, sh_ref = ex
        outs[0][...] = mix
        n1, _, _ = _rms_fwd(mix, pw_ref[...])
        x1 = x_ref[...] + g1_ref[...] * n1
        outs[1][...] = x1
        y2, _, _ = _rms_fwd(x1, w2_ref[...])
        outs[2][...] = (y2 * (1.0 + sc_ref[...]) + sh_ref[...]).astype(BF16)

    return _mm_rows(cat, w_out, name="out_proj", tm=tm, extra=(x, post_w, g1, pre_w, sc2, sh2),
                    extra_specs=[_tok_spec(tm), _vec_spec(), _row_spec(), _vec_spec(), _row_spec(), _row_spec()],
                    out_specs=[_tok_spec(tm), _tok_spec(tm), _tok_spec(tm)],
                    out_shape=[_sds(x.shape, F32), _sds(x.shape, F32), _sds(x.shape, BF16)], epi=epi)


def _acc_out(ref, first, value):
    @pl.when(first)
    def _():
        ref[...] = value

    @pl.when(jnp.logical_not(first))
    def _():
        ref[...] += value


def _down_proj_fused(r, w_down, x1, post_w, g2, target):
    tm = 512
    bsz = x1.shape[0]

    def pro(r_ref, ex, outs):
        rv = r_ref[...]
        return rv * rv

    def epi(down, ex, outs):
        x1_ref, w_ref, g2_ref, t_ref = ex
        loss_ref, dy_ref, dd_ref, dg2_ref, dw_ref = outs
        w, g2v = w_ref[...], g2_ref[...]
        gain = g2v * w
        dh, rstd = _rms_hat(down)
        err = x1_ref[...] + dh * gain - t_ref[...]
        part = (0.5 / D_MODEL) * jnp.sum(jnp.sum(err * err, axis=-1, keepdims=True), axis=0, keepdims=True)
        loss_ref[...] += jnp.broadcast_to(part, (1, LANE))
        dy = err * (1.0 / D_MODEL)
        dy_ref[...] = dy
        dd, per_col = _rms_bwd_gain(dy, gain, dh, rstd)
        dd_ref[...] = dd.astype(BF16)
        dg2_ref[...] += per_col * w
        dw_ref[...] += per_col * g2v

    return _mm_rows(r, w_down, name="down_proj", tm=tm, extra=(x1, post_w, g2, target),
                    extra_specs=[_tok_spec(tm), _vec_spec(), _row_spec(), _tok_spec(tm)],
                    out_specs=[_vec_spec(LANE), _tok_spec(tm), _tok_spec(tm), _row_spec(), _vec_spec()],
                    out_shape=[_sds((1, LANE), F32), _sds(x1.shape, F32), _sds(x1.shape, BF16), _sds((bsz, 1, D_MODEL), F32),
                               _sds((1, D_MODEL), F32)], pro=pro, epi=epi, parts=2, zero_per_seq=(3,), zero_once=(0, 4),
                    vmem_limit=VMEM_LIMIT_BIG)


def _up_bwd_fused(dpre, w_up4, dy, x1, mix, pre_w, sc2, post_w, g1, comms=()):
    tm = 512
    bsz = x1.shape[0]

    def epi(dh2v, ex, outs):
        dy_ref, x1_ref, mix_ref, w2_ref, sc_ref, pw_ref, g1_ref = ex
        dx1_ref, dmix_ref, dsc_ref, dsh_ref, dg1_ref, dw2_ref, dpw_ref = outs
        w2, pw, g1v = w2_ref[...], pw_ref[...], g1_ref[...]
        mod2 = 1.0 + sc_ref[...]
        xh2, rstd2 = _rms_hat(x1_ref[...])
        dsh_ref[...] += _colsum(dh2v)
        dx1n, per_col2 = _rms_bwd_gain(dh2v, mod2 * w2, xh2, rstd2)
        dsc_ref[...] += per_col2 * w2
        dw2_ref[...] += per_col2 * mod2
        dx1 = dy_ref[...] + dx1n
        dx1_ref[...] = dx1
        mh, rstd1 = _rms_hat(mix_ref[...])
        dmix, per_col1 = _rms_bwd_gain(dx1, g1v * pw, mh, rstd1)
        dmix_ref[...] = dmix.astype(BF16)
        dg1_ref[...] += per_col1 * pw
        dpw_ref[...] += per_col1 * g1v

    row_shape = _sds((bsz, 1, D_MODEL), F32)
    vec_shape = _sds((1, D_MODEL), F32)
    return _mm_rows(dpre, w_up4, name="up_bwd", tm=tm, extra=(dy, x1, mix, pre_w, sc2, post_w, g1),
                    extra_specs=[_tok_spec(tm), _tok_spec(tm), _tok_spec(tm), _vec_spec(), _row_spec(), _vec_spec(), _row_spec()],
                    out_specs=[_tok_spec(tm), _tok_spec(tm), _row_spec(), _row_spec(), _row_spec(), _vec_spec(), _vec_spec()],
                    out_shape=[_sds(x1.shape, F32), _sds(x1.shape, BF16), row_shape, row_shape, row_shape, vec_shape, vec_shape],
                    epi=epi, b_chunks=w_up4.shape[0], comms=comms, parts=2, zero_per_seq=(2, 3, 4), zero_once=(5, 6),
                    vmem_limit=VMEM_LIMIT_BIG)


def _norm1_bwd(dh1, dx1, x, pre_w, sc1, tm=512, comms=()):
    bsz, seq, _ = x.shape

    def body(dh_ref, dx1_ref, x_ref, w_ref, sc_ref, gx_ref, dsc_ref, dsh_ref, dw_ref):
        b, i = pl.program_id(0), pl.program_id(1)
        w = w_ref[...]
        dh = dh_ref[...]
        mod = 1.0 + sc_ref[...]
        xh, rstd = _rms_hat(x_ref[...])
        dx, per_col = _rms_bwd_gain(dh, mod * w, xh, rstd)
        _acc_out(dsh_ref, i == 0, _colsum(dh))
        _acc_out(dsc_ref, i == 0, per_col * w)
        _acc_out(dw_ref, jnp.logical_and(b == 0, i == 0), per_col * mod)
        gx_ref[...] = dx1_ref[...] + dx

    row_shape = _sds((bsz, 1, D_MODEL), F32)
    return _call(
        body, (dh1, dx1, x, pre_w, sc1), name="norm1_bwd", grid=(bsz, seq // tm),
        in_specs=[_tok_spec(tm), _tok_spec(tm), _tok_spec(tm), _vec_spec(), _row_spec()],
        out_specs=[_tok_spec(tm), _row_spec(), _row_spec(), _vec_spec()],
        out_shape=[_sds(x.shape, F32), row_shape, row_shape, _sds((1, D_MODEL), F32)],
        sem=("arbitrary", "arbitrary"), comms=comms)


def _hgrn_bwd(dcat, proj, o_raw, states, lb_table, norm_w, masks, comms=()):
    bsz, seq, _ = proj.shape
    nstep = seq // HG_TOK
    rec0 = ATT_WIDTH // LANE
    width = HG_HPS * LANE
    slabs = (HG_Q0, HG_F0, HG_I0, HG_G0)
    n_steps = (HG_HEADS // HG_HPS) * bsz * nstep
    assert n_steps >= 2

    def body(tbl_ref, nw_ref, dr_ref, q_ref, f_ref, i_ref, g_ref, o_ref, st_ref, lower_ref, upper_ref,
             dproj_ref, dlb_ref, dnw_ref, ds_scr, grad_buf, grad_sem):
        h, b, t = pl.program_id(0), pl.program_id(1), pl.program_id(2)
        step = (h * bsz + b) * nstep + t
        slot = step % 2
        dq_k, df_k, di_k, dg_k = range(4)

        def grad_copies(of_step):
            hh, bb, tt = of_step // (bsz * nstep), (of_step // nstep) % bsz, of_step % nstep
            rows = pl.ds(pl.multiple_of((nstep - 1 - tt) * HG_TOK, HG_TOK), HG_TOK)
            return [pltpu.make_async_copy(
                grad_buf.at[of_step % 2, k],
                dproj_ref.at[bb, rows, pl.ds(pl.multiple_of(slabs[k] * LANE + hh * width, width), width)],
                grad_sem.at[of_step % 2, k]) for k in range(4)]

        @pl.when(step >= 2)
        def _():
            for cp in grad_copies(step - 2):
                cp.wait()

        @pl.when(t == 0)
        def _():
            ds_scr[...] = jnp.zeros_like(ds_scr)

        lower, upper = lower_ref[...], upper_ref[...]
        dlb_parts = []
        dnw_acc = jnp.zeros((1, LANE), F32)
        for hp in range(HG_HPS):
            ls = slice(hp * LANE, (hp + 1) * LANE)
            hq, v, hg = q_ref[:, ls], i_ref[:, ls], g_ref[:, ls]
            nw = nw_ref[...]
            c = _hgrn_common(tbl_ref[:, ls], f_ref[:, ls], hq)
            qd, kd, ku = c["qd"], c["kd"], c["ku"]
            y, on, rstd = _rms_fwd(o_ref[:, ls], nw)
            sg = _sigmoid(hg)
            dr = dr_ref[:, ls]
            grad_buf[slot, dg_k, :, ls] = (dr * y * (sg * (1.0 + hg * (1.0 - sg)))).astype(BF16)
            do, dnw_rows = _rms_bwd(dr * (hg * sg), on, rstd, nw)
            at = _dot(kd, qd, NT_DIMS) * upper
            da = _dot(do, v, NT_DIMS) * lower
            dat = _dot(v, do, NT_DIMS) * upper
            dv = _dot(at, do)
            dqd = _dot(da, kd)
            dkd = _dot(dat, qd)
            do_c, qd_c, v_c, ku_c = [_chunk_slices(z.astype(BF16)) for z in (do, qd, v, ku)]
            outer = [_dot(do_c[j], qd_c[j], TN_DIMS) for j in range(HG_NCH)]
            ds = ds_scr[hp]
            ds_after = [None] * HG_NCH
            for j in reversed(range(HG_NCH)):
                ds_after[j] = ds
                ds = outer[j] + ds * c["e_last"][j]
            ds_scr[hp] = ds
            updates = [_dot(v_c[j], ku_c[j], TN_DIMS) for j in range(HG_NCH)]
            states = [st_ref[hp, 0]]
            for j in range(HG_NCH - 1):
                states.append(states[j] * c["e_last"][j] + updates[j])
            dv = dv + jnp.concatenate([_dot(ku_c[j], ds_after[j], NT_DIMS) for j in range(HG_NCH)], axis=0)
            dqd = dqd + jnp.concatenate([_dot(do_c[j], states[j]) for j in range(HG_NCH)], axis=0)
            dku = jnp.concatenate([_dot(v_c[j], ds_after[j]) for j in range(HG_NCH)], axis=0)
            dku_ku = dku * ku
            dbl = [_colsum(states[j] * ds_after[j]) * c["e_last"][j] + _colsum(dku_ku[j * HG_CHUNK:(j + 1) * HG_CHUNK])
                   for j in range(HG_NCH)]
            dk = dkd * c["e_nb"] + dku * c["e_rem"]
            db = dqd * qd - dkd * kd - dku_ku + jnp.where(_row_in_chunk() == HG_CHUNK - 1, _chunk_rows(dbl), 0.0)
            dfv = _chunk_cumsum(db, reverse=True) / c["f"] - dk
            sig, sq = c["sig"], c["sq"]
            grad_buf[slot, df_k, :, ls] = (dfv * (1.0 - c["lb"]) * sig * (1.0 - sig)).astype(BF16)
            grad_buf[slot, dq_k, :, ls] = (dqd * c["e_b"] * (sq * (1.0 + hq * (1.0 - sq)))).astype(BF16)
            grad_buf[slot, di_k, :, ls] = dv.astype(BF16)
            dlb_parts.append(_colsum(dfv * (1.0 - sig)))
            dnw_acc = dnw_acc + _colsum(dnw_rows)
        _acc_out(dlb_ref, jnp.logical_and(b == 0, t == 0), jnp.concatenate(dlb_parts, axis=1))
        _acc_out(dnw_ref, jnp.logical_and(h == 0, jnp.logical_and(b == 0, t == 0)), dnw_acc)
        for cp in grad_copies(step):
            cp.start()

        @pl.when(step == n_steps - 1)
        def _():
            for cp in grad_copies(step - 1) + grad_copies(step):
                cp.wait()

    rev = lambda t: nstep - 1 - t
    slab = lambda first: pl.BlockSpec((None, HG_TOK, width), lambda h, b, t: (b, rev(t), first // HG_HPS + h))
    head = pl.BlockSpec((None, HG_TOK, width), lambda h, b, t: (b, rev(t), h))
    return _call(
        body, (lb_table, norm_w, dcat, proj, proj, proj, proj, o_raw, states, *masks), name="hgrn_bwd",
        grid=(HG_HEADS // HG_HPS, bsz, nstep),
        in_specs=[pl.BlockSpec((2, width), lambda h, b, t: (0, h)), pl.BlockSpec((1, LANE), lambda h, b, t: (0, 0)),
                  slab(rec0), slab(HG_Q0), slab(HG_F0), slab(HG_I0), slab(HG_G0), head,
                  pl.BlockSpec((None, HG_HPS, 1, LANE, LANE), lambda h, b, t: (b, h, rev(t), 0, 0)),
                  pl.BlockSpec((HG_TOK, HG_TOK), lambda h, b, t: (0, 0)), pl.BlockSpec((HG_TOK, HG_TOK), lambda h, b, t: (0, 0))],
        out_specs=[pl.BlockSpec(memory_space=pl.ANY), pl.BlockSpec((1, width), lambda h, b, t: (0, h)),
                   pl.BlockSpec((1, LANE), lambda h, b, t: (0, 0))],
        out_shape=[_sds((bsz, seq, IN_COLS), BF16), _sds((1, HG_WIDTH), F32), _sds((1, LANE), F32)],
        scratch_shapes=[pltpu.VMEM((HG_HPS, LANE, LANE), F32), pltpu.VMEM((2, 4, HG_TOK, width), BF16),
                        pltpu.SemaphoreType.DMA((2, 4))],
        sem=("arbitrary", "arbitrary", "arbitrary"), comms=comms)


def _attn_bwd(dcat, raw, w_norm, qh, kh, vh, lse, sinks, tables, biases, dproj, comms=()):
    bsz, nblk = qh.shape[0], qh.shape[1]
    seq = nblk * WINDOW
    nstep = nblk // ATT_BPS
    half = ROPE_DIM // 2

    def body(sink_ref, da_ref, raw_ref, w_ref, q_ref, kc_ref, kp_ref, vc_ref, vp_ref, l_ref, c_ref, u_ref, d_ref,
             full_ref, first_ref, dproj_ref, o_ref, dw_ref, dsink_ref, carry_k, carry_v):
        b, i = pl.program_id(0), pl.program_id(1)
        first = jnp.logical_and(b == 0, i == 0)

        @pl.when(i == 0)
        def _():
            carry_k[...] = jnp.zeros_like(carry_k)
            carry_v[...] = jnp.zeros_like(carry_v)

        w = w_ref[...]
        _, on, rstd = _rms_fwd(raw_ref[...], w)
        do_step, dw_rows = _rms_bwd(da_ref[...], on, rstd, w)
        _acc_out(dw_ref, first, _colsum(dw_rows))
        lane8 = lax.broadcasted_iota(jnp.int32, (1, ATT_Q_HEADS), 1)
        dsink = jnp.zeros((1, ATT_Q_HEADS), F32)
        head_cols = jnp.where(lax.broadcasted_iota(jnp.int32, (2 * ATT_Q_HEADS, ATT_WIDTH), 1) // ATT_HEAD_DIM
                              == lax.broadcasted_iota(jnp.int32, (2 * ATT_Q_HEADS, ATT_WIDTH), 0), 1.0, 0.0)
        from_next_k, from_next_v = carry_k[...], carry_v[...]
        for blk in reversed(range(ATT_BPS)):
            tok = slice(blk * WINDOW, (blk + 1) * WINDOW)
            bias = _band_bias(full_ref, first_ref, True if blk else i < nstep - 1)
            do_all = do_step[tok]
            c, u, d = c_ref[tok, :], u_ref[tok, :], d_ref[tok, :]
            lse_t = l_ref[tok, :].T
            prod = do_all * raw_ref[tok, :]
            prod_hi = prod.astype(BF16)
            prod_lo = prod - prod_hi.astype(F32)
            dsum_t = _dot(head_cols, prod_hi, NT_DIMS) + _dot(head_cols, prod_lo, NT_DIMS)

            def unrope(g):
                return (g * c + pltpu.roll(g * u, LANE - half, 1) + pltpu.roll(g * d, half, 1)).astype(BF16)

            dq_parts, dk_own, dk_before, dv_own, dv_before = [], [], [], [], []
            for g in range(ATT_KV_HEADS):
                heads = [slice((g * ATT_GROUP + hh) * ATT_HEAD_DIM, (g * ATT_GROUP + hh + 1) * ATT_HEAD_DIM)
                         for hh in range(ATT_GROUP)]
                q = q_ref[blk, g]
                keys, vals = _band(kp_ref, kc_ref, g, blk), _band(vp_ref, vc_ref, g, blk)
                do_g = jnp.concatenate([do_all[:, hs] for hs in heads], axis=0)
                group_row = lambda z: jnp.concatenate(
                    [z[g * ATT_GROUP + hh:g * ATT_GROUP + hh + 1, :] for hh in range(ATT_GROUP)], axis=1)
                dsum, lse_g = group_row(dsum_t), group_row(lse_t)
                p_t = jnp.exp(_dot(keys, q, NT_DIMS) * ATT_SCALE + bias - lse_g)
                sink_part = jnp.exp(_sink_row(sink_ref, g) - lse_g) * dsum
                for hh in range(ATT_GROUP):
                    head_sum = jnp.sum(sink_part[:, hh * WINDOW:(hh + 1) * WINDOW], axis=1, keepdims=True)
                    dsink = dsink - jnp.where(lane8 == g * ATT_GROUP + hh, head_sum, 0.0)
                ds_t = p_t * (_dot(vals, do_g, NT_DIMS) - dsum) * ATT_SCALE
                dq_g = _dot(ds_t, keys, TN_DIMS)
                dq_parts += [dq_g[hh * WINDOW:(hh + 1) * WINDOW] for hh in range(ATT_GROUP)]
                dk_g = _dot(ds_t, q)
                dv_g = _dot(p_t, do_g)
                dk_before.append(dk_g[:WINDOW])
                dk_own.append(dk_g[WINDOW:])
                dv_before.append(dv_g[:WINDOW])
                dv_own.append(dv_g[WINDOW:])
            per_slab = LANE // ATT_HEAD_DIM
            for s in range(ATT_WIDTH // LANE):
                slab = jnp.concatenate(dq_parts[s * per_slab:(s + 1) * per_slab], axis=1)
                o_ref[tok, s * LANE:(s + 1) * LANE] = unrope(slab)
            o_ref[tok, ATT_WIDTH:ATT_WIDTH + LANE] = unrope(jnp.concatenate(dk_own, axis=1) + from_next_k)
            o_ref[tok, ATT_WIDTH + LANE:ATT_COLS] = (jnp.concatenate(dv_own, axis=1) + from_next_v).astype(BF16)
            from_next_k, from_next_v = jnp.concatenate(dk_before, axis=1), jnp.concatenate(dv_before, axis=1)
        carry_k[...] = from_next_k
        carry_v[...] = from_next_v
        _acc_out(dsink_ref, first, dsink)

    rows = ATT_BPS * WINDOW
    rev = lambda i: nstep - 1 - i
    cur = lambda width: pl.BlockSpec((None, rows, width), lambda b, i: (b, rev(i), 0))
    q_spec = pl.BlockSpec((None, ATT_BPS, ATT_KV_HEADS, GROUP_ROWS, ATT_HEAD_DIM), lambda b, i: (b, rev(i), 0, 0, 0))
    kv_cur = pl.BlockSpec((None, ATT_KV_HEADS, rows, ATT_HEAD_DIM), lambda b, i: (b, 0, rev(i), 0))
    kv_prev = pl.BlockSpec((None, ATT_KV_HEADS, WINDOW, ATT_HEAD_DIM), lambda b, i: (b, 0, jnp.maximum(ATT_BPS * rev(i) - 1, 0), 0))
    tab = pl.BlockSpec((rows, LANE), lambda b, i: (rev(i), 0))
    return _call(
        body, (sinks, dcat, raw, w_norm, qh, kh, kh, vh, vh, lse, *tables, *biases, dproj), name="attn_bwd", grid=(bsz, nstep),
        in_specs=[pl.BlockSpec(memory_space=pltpu.SMEM), cur(ATT_WIDTH), cur(ATT_WIDTH), _vec_spec(ATT_WIDTH), q_spec,
                  kv_cur, kv_prev, kv_cur, kv_prev, cur(LANE), tab, tab, tab, _bias_spec(True), _bias_spec(True),
                  pl.BlockSpec(memory_space=pl.ANY)],
        out_specs=[cur(ATT_COLS), _vec_spec(ATT_WIDTH), _vec_spec(ATT_Q_HEADS)],
        out_shape=[_sds(dproj.shape, BF16), _sds((1, ATT_WIDTH), F32), _sds((1, ATT_Q_HEADS), F32)],
        scratch_shapes=[pltpu.VMEM((WINDOW, LANE), F32), pltpu.VMEM((WINDOW, LANE), F32)],
        sem=("arbitrary", "arbitrary"), comms=comms, aliases={15: 0})


def _other_chips(x, y):
    return [(1 - x, y), (x, 1 - y), (1 - x, 1 - y)]


def _sem_pair(n):
    return [pltpu.SemaphoreType.DMA((n,)), pltpu.SemaphoreType.DMA((n,))]


def _plan_pair_forward(bufs):
    n = len(bufs)

    def copies(outs, sems):
        x, y, c = _mesh_pos()
        sends, lands = [], []
        for a in range(n):
            for j, chip in enumerate(_other_chips(x, y)):
                k = 3 * a + j
                slot = outs[a].at[4 * chip[0] + 2 * chip[1] + c]
                sends.append(pltpu.make_async_remote_copy(
                    src_ref=slot, dst_ref=slot, send_sem=sems[0].at[k], recv_sem=sems[1].at[k],
                    device_id=(x, y, 1 - c), device_id_type=MESH))
                theirs = outs[a].at[4 * chip[0] + 2 * chip[1] + 1 - c]
                lands.append(pltpu.make_async_remote_copy(
                    src_ref=theirs, dst_ref=theirs, send_sem=sems[0].at[k], recv_sem=sems[1].at[k],
                    device_id=(x, y, 1 - c), device_id_type=MESH))
        return sends, lands

    def start(ins, outs, sems):
        for cp in copies(outs, sems)[0]:
            cp.start()

    def finish(ins, outs, sems):
        sends, lands = copies(outs, sems)
        for cp in lands:
            cp.wait_recv()
        for cp in sends:
            cp.wait_send()

    return _Comm(list(bufs), [_sds(b.shape, b.dtype) for b in bufs], _sem_pair(3 * n), start, finish,
                 aliases=[(a, a) for a in range(n)])


def _plan_pair(arrays, other_half):
    n = len(arrays)
    per = N_CHIPS if other_half == "chip_major" else 1

    def copies(ins, outs, sems):
        x, y, c = _mesh_pos()
        out = []
        for a in range(n):
            for k in range(per):
                if other_half == "chip_major":
                    src, dst = ins[a].at[k, 1 - c], outs[a].at[k]
                else:
                    src, dst = (ins[a].at[1 - c] if other_half else ins[a]), outs[a]
                out.append(pltpu.make_async_remote_copy(
                    src_ref=src, dst_ref=dst, send_sem=sems[0].at[per * a + k], recv_sem=sems[1].at[per * a + k],
                    device_id=(x, y, 1 - c), device_id_type=MESH))
        return out

    def start(ins, outs, sems):
        for cp in copies(ins, outs, sems):
            cp.start()

    def finish(ins, outs, sems):
        for cp in copies(ins, outs, sems):
            cp.wait()

    if other_half == "chip_major":
        shapes = [_sds((a.shape[0],) + a.shape[2:], a.dtype) for a in arrays]
    else:
        shapes = [_sds(a.shape[1:] if other_half else a.shape, a.dtype) for a in arrays]
    return _Comm(list(arrays), shapes, _sem_pair(per * n), start, finish)


def _plan_chip_exchange(arrays):
    n = len(arrays)

    def copies(ins, outs, sems):
        x, y, c = _mesh_pos()
        sends, lands = [], []
        for a in range(n):
            for j, chip in enumerate(_other_chips(x, y)):
                k = 3 * a + j
                sends.append(pltpu.make_async_remote_copy(
                    src_ref=ins[a].at[2 * chip[0] + chip[1]], dst_ref=outs[a].at[2 * x + y], send_sem=sems[0].at[k],
                    recv_sem=sems[1].at[k], device_id=(*chip, c), device_id_type=MESH))
                slot = outs[a].at[2 * chip[0] + chip[1]]
                lands.append(pltpu.make_async_remote_copy(
                    src_ref=slot, dst_ref=slot, send_sem=sems[0].at[k], recv_sem=sems[1].at[k],
                    device_id=(*chip, c), device_id_type=MESH))
        return sends, lands

    def start(ins, outs, sems):
        for cp in copies(ins, outs, sems)[0]:
            cp.start()

    def finish(ins, outs, sems):
        sends, lands = copies(ins, outs, sems)
        for cp in lands:
            cp.wait_recv()
        for cp in sends:
            cp.wait_send()

    return _Comm(list(arrays), [_sds(a.shape, a.dtype) for a in arrays], _sem_pair(3 * n), start, finish)


SEM_SPEC = pl.BlockSpec(memory_space=pltpu.SEMAPHORE)
N_OTHER = N_CHIPS - 1


def _exchange_copies(s_ref, land_ref, sems):
    x, y, c = _mesh_pos()
    return [pltpu.make_async_remote_copy(
        src_ref=s_ref.at[2 * chip[0] + chip[1]], dst_ref=land_ref.at[2 * x + y], send_sem=sems[j], recv_sem=sems[N_OTHER + j],
        device_id=(*chip, c), device_id_type=MESH) for j, chip in enumerate(_other_chips(x, y))]


def _exchange_start(s, name):
    def body(s_ref, land_ref, *outs):
        sems, token = outs[:2 * N_OTHER], outs[-1]
        for cp in _exchange_copies(s_ref, land_ref, sems):
            cp.start()
        token[...] = jnp.zeros_like(token)

    hbm = pltpu.HBM(s.shape, s.dtype)
    res = pl.pallas_call(
        body, name=name,
        out_shape=(pltpu.SemaphoreType.DMA(()),) * (2 * N_OTHER) + (hbm, hbm, _sds((SUBLANES, LANE), F32)),
        in_specs=(HBM_SPEC, HBM_SPEC),
        out_specs=(SEM_SPEC,) * (2 * N_OTHER) + (HBM_SPEC, HBM_SPEC, pl.BlockSpec(memory_space=pltpu.VMEM)),
        input_output_aliases={0: 2 * N_OTHER, 1: 2 * N_OTHER + 1},
        compiler_params=pltpu.CompilerParams(has_side_effects=pltpu.SideEffectType.DATAFLOW_SIDE_EFFECTING),
    )(pltpu.with_memory_space_constraint(s, pltpu.HBM), pltpu.with_memory_space_constraint(lax.empty(s.shape, s.dtype), pltpu.HBM))
    return res[:2 * N_OTHER], res[2 * N_OTHER], res[2 * N_OTHER + 1], res[-1]


def _exchange_wait(sems, s_thru, land_thru, afters, name):
    def body(s_ref, land_ref, *rest):
        for cp in _exchange_copies(s_ref, land_ref, rest[:2 * N_OTHER]):
            cp.wait_send()
            cp.wait_recv()

    hbm = pltpu.HBM(s_thru.shape, s_thru.dtype)
    return pl.pallas_call(
        body, name=name, out_shape=(hbm, hbm),
        in_specs=(HBM_SPEC, HBM_SPEC) + (SEM_SPEC,) * (2 * N_OTHER) + (pl.BlockSpec(memory_space=pl.ANY),) * len(afters),
        out_specs=(HBM_SPEC, HBM_SPEC), input_output_aliases={0: 0, 1: 1},
        compiler_params=pltpu.CompilerParams(has_side_effects=pltpu.SideEffectType.DATAFLOW_SIDE_EFFECTING),
    )(s_thru, land_thru, *sems, *afters)


def _gather_copies(block_ref, buf_ref, sems):
    x, y, c = _mesh_pos()
    return [pltpu.make_async_remote_copy(
        src_ref=block_ref, dst_ref=buf_ref.at[4 * x + 2 * y + c], send_sem=sems[j], recv_sem=sems[N_OTHER + j],
        device_id=(*chip, c), device_id_type=MESH) for j, chip in enumerate(_other_chips(x, y))]


def _gather_start(blocks, bufs, afters, name):
    n = len(blocks)
    per = 2 * N_OTHER

    def body(*refs):
        ins, outs = refs[:2 * n], refs[2 * n + len(afters):]
        for a in range(n):
            for cp in _gather_copies(ins[a], ins[n + a], outs[a * per:(a + 1) * per]):
                cp.start()
        outs[-1][...] = jnp.zeros_like(outs[-1])

    hbm = [pltpu.HBM(z.shape, z.dtype) for z in list(blocks) + list(bufs)]
    res = pl.pallas_call(
        body, name=name,
        out_shape=(pltpu.SemaphoreType.DMA(()),) * (n * per) + tuple(hbm) + (_sds((SUBLANES, LANE), F32),),
        in_specs=(HBM_SPEC,) * (2 * n) + (pl.BlockSpec(memory_space=pl.ANY),) * len(afters),
        out_specs=(SEM_SPEC,) * (n * per) + (HBM_SPEC,) * (2 * n) + (pl.BlockSpec(memory_space=pltpu.VMEM),),
        input_output_aliases={k: n * per + k for k in range(2 * n)},
        compiler_params=pltpu.CompilerParams(has_side_effects=pltpu.SideEffectType.DATAFLOW_SIDE_EFFECTING),
    )(*[pltpu.with_memory_space_constraint(z, pltpu.HBM) for z in list(blocks) + list(bufs)], *afters)
    parts = [(res[a * per:(a + 1) * per], res[n * per + a], res[n * per + n + a]) for a in range(n)]
    return parts, res[-1]


def _gather_wait(part, afters, name):
    sems, block, buf = part

    def body(block_ref, buf_ref, *rest):
        for cp in _gather_copies(block_ref, buf_ref, rest[:2 * N_OTHER]):
            cp.wait_send()
            cp.wait_recv()

    return pl.pallas_call(
        body, name=name, out_shape=(pltpu.HBM(block.shape, block.dtype), pltpu.HBM(buf.shape, buf.dtype)),
        in_specs=(HBM_SPEC, HBM_SPEC) + (SEM_SPEC,) * (2 * N_OTHER) + (pl.BlockSpec(memory_space=pl.ANY),) * len(afters),
        out_specs=(HBM_SPEC, HBM_SPEC), input_output_aliases={0: 0, 1: 1},
        compiler_params=pltpu.CompilerParams(has_side_effects=pltpu.SideEffectType.DATAFLOW_SIDE_EFFECTING),
    )(block, buf, *sems, *afters)[1]


def _comm_only(comms, name):
    return _call(lambda: None, (), name=name, grid=(), in_specs=[], out_specs=[], out_shape=[], sem=(), comms=comms)[1]


def _allgather8(arrays, name):
    return _comm_only([_plan_allgather8(arrays)], name)[0]


def _plan_allgather8(arrays):
    n = len(arrays)

    def parts(ins, outs, sems):
        send_sems, recv_sems, local_sems = sems
        x, y, c = _mesh_pos()
        me, sibling = (x, y, c), (x, y, 1 - c)
        chips = _other_chips(x, y)

        def copy(a, k, block, to, src=None):
            dst = outs[a].at[4 * block[0] + 2 * block[1] + block[2]]
            return pltpu.make_async_remote_copy(
                src_ref=dst if src is None else src, dst_ref=dst, send_sem=send_sems.at[7 * a + k],
                recv_sem=recv_sems.at[7 * a + k], device_id=to, device_id_type=MESH)

        mine = [pltpu.make_async_copy(ins[a], outs[a].at[4 * x + 2 * y + c], local_sems.at[a]) for a in range(n)]
        first = []
        for a in range(n):
            first.append(copy(a, 0, me, sibling, src=ins[a]))
            first += [copy(a, 1 + j, me, (*chip, c), src=ins[a]) for j, chip in enumerate(chips)]
        return copy, mine, first, me, sibling, chips, c

    def start(ins, outs, sems):
        _, mine, first, *_ = parts(ins, outs, sems)
        for cp in mine + first:
            cp.start()

    def finish(ins, outs, sems):
        copy, mine, first, me, sibling, chips, c = parts(ins, outs, sems)
        passed = []
        for j, chip in enumerate(chips):
            for a in range(n):
                copy(a, 1 + j, (*chip, c), me).wait_recv()
                fwd = copy(a, 4 + j, (*chip, c), sibling)
                fwd.start()
                passed.append(fwd)
        for a in range(n):
            copy(a, 0, sibling, me).wait_recv()
            for j, chip in enumerate(chips):
                copy(a, 4 + j, (*chip, 1 - c), me).wait_recv()
        for cp in first + passed:
            cp.wait_send()
        for cp in mine:
            cp.wait()

    sems = [pltpu.SemaphoreType.DMA((7 * n,)), pltpu.SemaphoreType.DMA((7 * n,)), pltpu.SemaphoreType.DMA((n,))]
    return _Comm(list(arrays), [_sds((N_DEV,) + a.shape, a.dtype) for a in arrays], sems, start, finish)


def _pair_sum(g, q, core, name, chip_major=False):
    rows, cols = g.shape[2:]
    tr = _row_tile(rows)

    def body(core_ref, g_ref, q_ref, o_ref):
        o_ref[...] = (g_ref[...] + q_ref[...]).astype(BF16)

    blk = pl.BlockSpec((None, tr, cols), lambda k, i, core_ref: (k, i, 0))
    if chip_major:
        own = pl.BlockSpec((None, None, tr, cols), lambda k, i, core_ref: (k, core_ref[0], i, 0))
    else:
        own = pl.BlockSpec((None, None, tr, cols), lambda k, i, core_ref: (core_ref[0], k, i, 0))
    return pl.pallas_call(
        body, name=name,
        grid_spec=pltpu.PrefetchScalarGridSpec(num_scalar_prefetch=1, grid=(N_CHIPS, rows // tr), in_specs=[own, blk], out_specs=blk),
        out_shape=_sds((N_CHIPS, rows, cols), BF16), compiler_params=_params("parallel", "parallel"),
    )(core, g, q)


def _sum_chips(own, landed, chip, name):
    _, rows, cols = own.shape
    tr = _row_tile(rows)

    def body(chip_ref, own_ref, a_ref, b_ref, c_ref, o_ref):
        acc = own_ref[...].astype(F32) + a_ref[...].astype(F32)
        o_ref[...] = (acc + b_ref[...].astype(F32)) + c_ref[...].astype(F32)

    blk = lambda flip: pl.BlockSpec((None, tr, cols), lambda i, chip_ref: (jnp.bitwise_xor(chip_ref[0], flip), i, 0))
    return pl.pallas_call(
        body, name=name,
        grid_spec=pltpu.PrefetchScalarGridSpec(num_scalar_prefetch=1, grid=(rows // tr,), in_specs=[blk(0), blk(1), blk(2), blk(3)],
                                               out_specs=pl.BlockSpec((tr, cols), lambda i, chip_ref: (i, 0))),
        out_shape=_sds((rows, cols), F32), compiler_params=_params("parallel"),
    )(chip, own, landed, landed, landed)


SUBLANES = 8


def _tile_rows(n_elems):
    return -(-n_elems // (SUBLANES * LANE)) * SUBLANES


SMALL_ITEMS = (("b_ada", N_MOD * D_MODEL), ("pre_w_mix", D_MODEL), ("post_w_mix", D_MODEL), ("pre_w_mlp", D_MODEL),
               ("post_w_mlp", D_MODEL), ("attn_out_w", ATT_WIDTH), ("hg_norm_w", HG_HEAD_DIM), ("attn_sinks", ATT_Q_HEADS),
               ("lb_0", HG_WIDTH), ("lb_1", HG_WIDTH))
SMALL_AT = {}
for _name, _size in SMALL_ITEMS:
    SMALL_AT[_name] = (sum(r for _, r in SMALL_AT.values()), _tile_rows(_size))
SMALL_ROWS = sum(r for _, r in SMALL_AT.values())
MOD_ROWS = SMALL_AT["b_ada"][1]
PLAIN_ROWS = SMALL_AT["lb_0"][0] - MOD_ROWS
LB_ROWS = SMALL_AT["lb_0"][1]


def _rows(a, nrows=None):
    flat = a.reshape(-1)
    nrows = _tile_rows(flat.shape[0]) if nrows is None else nrows
    return jnp.pad(flat, (0, nrows * LANE - flat.shape[0])).reshape(nrows, LANE)


def _pack_small(vals):
    vals = dict(vals, lb_0=vals["lb_table"][0], lb_1=vals["lb_table"][1])
    return jnp.concatenate([_rows(vals[name], SMALL_AT[name][1]) for name, _ in SMALL_ITEMS], axis=0)


def _unpack_small(p):
    def item(name, shape):
        first = SMALL_AT[name][0]
        size = shape[0] * shape[1]
        return p[first:first + SMALL_AT[name][1]].reshape(-1)[:size].reshape(shape)

    out = {name: item(name, (1, size)) for name, size in SMALL_ITEMS if not name.startswith("lb_")}
    out["lb_table"] = jnp.concatenate([item("lb_0", (1, HG_WIDTH)), item("lb_1", (1, HG_WIDTH))], axis=0)
    return out


def _pack_partials(dmod, plain, d_lb, loss_row):
    return jnp.concatenate([_rows(dmod, dmod.shape[0] * MOD_ROWS)] + [_rows(g) for g in plain] + [_rows(d_lb), _rows(loss_row)], axis=0)


def _small_update(packs, w, m, v, n_seq):
    mod_end = n_seq * MOD_ROWS
    lb_at = mod_end + PLAIN_ROWS
    t0, t1 = SMALL_AT["lb_0"][0], SMALL_AT["lb_1"][0]

    def body(p_ref, w_ref, m_ref, v_ref, g_ref, dl_ref, nm_ref, nv_ref, loss_ref):
        tot = p_ref[0]
        for d in range(1, N_DEV):
            tot = tot + p_ref[d]
        wv = w_ref[...]
        p1 = _sigmoid(wv[t1:t1 + LB_ROWS] - wv[t0:t0 + LB_ROWS])
        s = tot[lb_at:lb_at + LB_ROWS] * p1 * (1.0 - p1)
        g_bias = tot[0:MOD_ROWS]
        for q in range(1, n_seq):
            g_bias = g_bias + tot[q * MOD_ROWS:(q + 1) * MOD_ROWS]
        g = jnp.concatenate([g_bias, tot[mod_end:lb_at], -s, s], axis=0)
        g_ref[...] = g
        dl_ref[...], nm_ref[...], nv_ref[...] = _adamw_math(g, wv, m_ref[...], v_ref[...])
        loss_ref[...] = tot[lb_at + LB_ROWS:lb_at + LB_ROWS + SUBLANES]

    shp = _sds((SMALL_ROWS, LANE), F32)
    return pl.pallas_call(body, name="small_update", out_shape=[shp] * 4 + [_sds((SUBLANES, LANE), F32)],
                          compiler_params=_params())(packs, w, m, v)


def kernel(x, c, w_ada, b_ada, pre_w_mix, w_in, attn_sinks, attn_out_w, lb_table, hg_norm_w, w_out, post_w_mix, pre_w_mlp, w_up, w_down, post_w_mlp, loss_target, m_w_ada, m_b_ada, m_pre_w_mix, m_w_in, m_attn_sinks, m_attn_out_w, m_lb_table, m_hg_norm_w, m_w_out, m_post_w_mix, m_pre_w_mlp, m_w_up, m_w_down, m_post_w_mlp, v_w_ada, v_b_ada, v_pre_w_mix, v_w_in, v_attn_sinks, v_attn_out_w, v_lb_table, v_hg_norm_w, v_w_out, v_post_w_mix, v_pre_w_mlp, v_w_up, v_w_down, v_post_w_mlp):
    xi, yi, ci = _mesh_pos()
    chip = 2 * xi + yi
    dev = 2 * chip + ci
    bsz, seq, _ = x.shape
    ntok = bsz * seq
    ada_cols = w_ada.shape[2]
    core = jnp.reshape(ci, (1,)).astype(jnp.int32)
    chip_idx = jnp.reshape(chip, (1,)).astype(jnp.int32)
    flat = lambda a: a.reshape(ntok, a.shape[-1])
    unflat = lambda a: a.reshape(bsz, seq, a.shape[-1])
    tables = _rope_tables(seq)
    biases, chunk_masks = _band_biases(), _block_masks()

    def row_half(w):
        rows = w.shape[1] // 2
        return lax.dynamic_slice_in_dim(w[0], ci * rows, rows, axis=0).astype(BF16)

    def gather_buffer(w):
        rows, cols = w.shape[1] // 2, w.shape[2]
        own = w[0].astype(BF16).reshape(2, rows, cols)
        return lax.dynamic_update_slice(lax.empty((N_DEV, rows, cols), BF16), own, (2 * chip, 0, 0))

    w_in_t, m_in_t, v_in_t = [jnp.transpose(a[0])[None] for a in (w_in, m_w_in, v_w_in)]
    c_g, in_g = _allgather8([c, row_half(w_in_t)], "gather_first")
    c_all = c_g.reshape(N_DEV * bsz, D_MODEL)
    w_in_full = in_g.reshape(IN_COLS, D_MODEL)

    b_cols = lax.dynamic_slice_in_dim(b_ada, chip * ada_cols, ada_cols, axis=1)
    mod_part = _ada_fwd(c_all, w_ada[0], b_cols)
    half_rows = mod_part.shape[0] // 2
    (mod_g,) = _allgather8([lax.dynamic_slice_in_dim(mod_part, ci * half_rows, half_rows, axis=0)], "gather_mod")
    mod_all = mod_g.reshape(N_CHIPS, 2, half_rows, ada_cols).transpose(1, 2, 0, 3).reshape(N_DEV * bsz, N_MOD * D_MODEL)
    mod = lax.dynamic_slice_in_dim(mod_all, dev * bsz, bsz, axis=0)
    sh1, sc1, g1, sh2, sc2, g2 = [mod[:, i * D_MODEL:(i + 1) * D_MODEL].reshape(bsz, 1, D_MODEL) for i in range(N_MOD)]

    weights = (w_out, w_up, w_down)
    (out_part, up_part, down_part), started = _gather_start(
        [row_half(w) for w in weights], [gather_buffer(w) for w in weights], [mod_g], "gather_weights_start")

    h1, proj, qh, kh, vh = _in_proj_fused(x, pre_w_mix, sc1 + started[0:1, 0:1], sh1, w_in_full, tables)
    out_g = _gather_wait(out_part, [proj], "gather_out_wait")
    (attn_raw, cat, lse), ((out_g,),) = _attn_fwd(qh, kh, vh, attn_sinks, attn_out_w, biases, comms=[_plan_pair_forward([out_g])])
    up_g = _gather_wait(up_part, [attn_raw], "gather_up_wait")
    (o_raw, cat, states), ((up_g,),) = _hgrn_fwd(proj, lb_table, hg_norm_w, cat, chunk_masks, comms=[_plan_pair_forward([up_g])])
    down_g = _gather_wait(down_part, [o_raw], "gather_down_wait")
    w_out_full = out_g.reshape(D_MODEL, D_MODEL)
    w_up4 = up_g.reshape(N_CHIPS, D_MODEL, D_MODEL)
    mix, x1, h2 = _out_proj_fused(cat, w_out_full, x, post_w_mix, g1, pre_w_mlp, sc2, sh2)
    big_tm = min(ntok, 2048)
    up_spec = pl.BlockSpec((None, D_MODEL, D_MODEL), lambda i, j: (j, 0, 0))
    r, ((down_g,),) = _mm(flat(h2), w_up4, name="up_proj", out_dtype=BF16, tm=big_tm, tn=D_MODEL, n_out=D_FF, b_spec=up_spec,
                          epi=lambda acc: jnp.maximum(acc, 0.0), comms=[_plan_pair_forward([down_g])])
    w_down_full = down_g.reshape(D_FF, D_MODEL)
    square = lambda t: t * t
    loss_row, dy, dd, dg2, d_post_mlp = _down_proj_fused(unflat(r), w_down_full, x1, post_w_mlp, g2, loss_target)

    dpre = _mm(flat(dd), w_down_full, name="down_bwd", out_dtype=BF16, trans_b=True, tm=big_tm, tn=D_MODEL, extra=(r,),
               epi=lambda acc, rt: acc * (2.0 * rt.astype(F32)))
    half_rows = D_MODEL // 2
    g_down = _mm_tn(r, flat(dd), name="down_wgrad", tk=half_rows, tn=D_MODEL, a_fn=square,
                    out_shape=_sds((2, N_CHIPS, half_rows, D_MODEL), F32),
                    out_spec=pl.BlockSpec((None, None, half_rows, D_MODEL), lambda i, j: (i % 2, i // 2, 0, 0)))
    (dx1, dmix, dsc2, dsh2, dg1, d_pre_mlp, d_post_mix), ((q_down,),) = _up_bwd_fused(
        unflat(dpre), w_up4, dy, x1, mix, pre_w_mlp, sc2, post_w_mix, g1, comms=[_plan_pair([g_down], True)])
    g_up = _mm_tn(flat(h2), dpre, name="up_wgrad", tk=D_MODEL, tn=half_rows,
                  out_shape=_sds((2, N_CHIPS, half_rows, D_MODEL), F32),
                  out_spec=pl.BlockSpec((2, None, half_rows, half_rows), lambda i, j: (0, j // 2, 0, j % 2)))
    s_down = _pair_sum(g_down, q_down, core, "pair_sum_down")

    dcat, ((q_up,),) = _mm(flat(dmix), w_out_full, name="out_bwd", out_dtype=F32, trans_b=True, comms=[_plan_pair([g_up], True)])
    dcat = unflat(dcat)
    s_up = _pair_sum(g_up, q_up, core, "pair_sum_up")
    out_rows = D_MODEL // N_CHIPS
    g_out = _mm_tn(flat(cat), flat(dmix), name="out_wgrad", tk=2 * out_rows, tn=half_rows,
                   out_shape=_sds((2, N_CHIPS, out_rows, half_rows), F32),
                   out_spec=pl.BlockSpec((None, 2, out_rows, half_rows), lambda i, j: (j, i, 0, 0)))
    (dproj_rec, d_lb, d_hg_norm), ((x_down,), (q_out,)) = _hgrn_bwd(
        dcat, proj, o_raw, states, lb_table, hg_norm_w, chunk_masks, comms=[_plan_chip_exchange([s_down]), _plan_pair([g_out], True)])
    half_down = _sum_chips(s_down, x_down, chip_idx, "sum_chips_down")
    s_out = _pair_sum(g_out, q_out, core, "pair_sum_out")
    (dproj, d_attn_out, d_sinks), ((their_down,), (x_up, x_out)) = _attn_bwd(
        dcat, attn_raw, attn_out_w, qh, kh, vh, lse, attn_sinks, tables, [bias.T for bias in biases], dproj_rec,
        comms=[_plan_pair([half_down], False), _plan_chip_exchange([s_up, s_out])])
    half_up = _sum_chips(s_up, x_up, chip_idx, "sum_chips_up")
    half_out = _sum_chips(s_out, x_out, chip_idx, "sum_chips_out")
    dproj = flat(dproj)
    in_rows = IN_COLS // N_CHIPS // 2
    g_in = _mm_tn(dproj, flat(h1), name="in_wgrad", tk=2 * LANE, tn=D_MODEL).reshape(N_CHIPS, 2, in_rows, D_MODEL)
    dh1, ((q_in,), (their_up, their_out)) = _mm(
        dproj, w_in_full, name="in_bwd", out_dtype=F32,
        comms=[_plan_pair([g_in], "chip_major"), _plan_pair([half_up, half_out], False)])
    s_in = _pair_sum(g_in, q_in, core, "pair_sum_in", chip_major=True)
    in_sems, s_in, in_landing, started = _exchange_start(s_in, "exchange_in_start")
    grad_x, dsc1, dsh1, d_pre_mix = _norm1_bwd(unflat(dh1), dx1, x, pre_w_mix + started[0:1, 0:1], sc1)

    dmod = jnp.concatenate([dsh1, dsc1, dg1, dsh2, dsc2, dg2], axis=-1).reshape(bsz, N_MOD * D_MODEL)
    pack = _pack_partials(dmod, [d_pre_mix, d_post_mix, d_pre_mlp, d_post_mlp, d_attn_out, d_hg_norm, d_sinks], d_lb, loss_row)
    ((packs,),) = _comm_only([_plan_allgather8([pack])], "gather_small")
    w_small = dict(b_ada=b_ada, pre_w_mix=pre_w_mix, post_w_mix=post_w_mix, pre_w_mlp=pre_w_mlp, post_w_mlp=post_w_mlp,
                   attn_out_w=attn_out_w, hg_norm_w=hg_norm_w, attn_sinks=attn_sinks, lb_table=lb_table)
    m_small = dict(b_ada=m_b_ada, pre_w_mix=m_pre_w_mix, post_w_mix=m_post_w_mix, pre_w_mlp=m_pre_w_mlp, post_w_mlp=m_post_w_mlp,
                   attn_out_w=m_attn_out_w, hg_norm_w=m_hg_norm_w, attn_sinks=m_attn_sinks, lb_table=m_lb_table)
    v_small = dict(b_ada=v_b_ada, pre_w_mix=v_pre_w_mix, post_w_mix=v_post_w_mix, pre_w_mlp=v_pre_w_mlp, post_w_mlp=v_post_w_mlp,
                   attn_out_w=v_attn_out_w, hg_norm_w=v_hg_norm_w, attn_sinks=v_attn_sinks, lb_table=v_lb_table)
    *small_packed, loss_rows = _small_update(packs, _pack_small(w_small), _pack_small(m_small), _pack_small(v_small), bsz)
    small_out = [_unpack_small(p) for p in small_packed]
    loss = loss_rows[0, 0]

    dmod_all = packs[:, :bsz * MOD_ROWS, :].reshape(N_DEV * bsz, N_MOD * D_MODEL)
    dmod_cols = lax.dynamic_slice_in_dim(dmod_all, chip * ada_cols, ada_cols, axis=1)
    ada_out = _ada_bwd_adamw(c_all, dmod_cols, w_ada[0], m_w_ada[0], v_w_ada[0])

    s_in, x_in = _exchange_wait(in_sems, s_in, in_landing, [grad_x, ada_out[0]], "exchange_in_wait")
    half_in = _sum_chips(s_in, x_in, chip_idx, "sum_chips_in")
    ((their_in,),) = _comm_only([_plan_pair([half_in], False)], "pair_swap_in")
    big = dict(
        w_in=tuple(jnp.transpose(a) for a in _adamw_halves(half_in, their_in, core, w_in_t[0], m_in_t[0], v_in_t[0], axis=0,
                                                           name="adamw_in")),
        w_up=tuple(_adamw_halves(half_up, their_up, core, w_up[0], m_w_up[0], v_w_up[0], axis=0, name="adamw_up")),
        w_out=tuple(_adamw_halves(half_out, their_out, core, w_out[0], m_w_out[0], v_w_out[0], axis=1, name="adamw_out")),
        w_down=tuple(_adamw_halves(half_down, their_down, core, w_down[0], m_w_down[0], v_w_down[0], axis=0, name="adamw_down")),
        w_ada=tuple(ada_out),
    )
    order = ("w_ada", "b_ada", "pre_w_mix", "w_in", "attn_sinks", "attn_out_w", "lb_table", "hg_norm_w", "w_out", "post_w_mix",
             "pre_w_mlp", "w_up", "w_down", "post_w_mlp")
    outs = [loss, grad_x]
    for kind in range(4):
        for nm in order:
            outs.append(big[nm][kind][None] if nm in big else small_out[kind][nm])
    return tuple(outs)
```

```python
import jax
import jax.numpy as jnp
from jax import lax
from jax.experimental import pallas as pl
from jax.experimental.pallas import tpu as pltpu

F32 = jnp.float32
BF16 = jnp.bfloat16

D_MODEL = 1024
ATT_WIDTH = 512
ATT_HEAD_DIM = 64
ATT_Q_HEADS = 8
ATT_KV_HEADS = 2
ATT_GROUP = ATT_Q_HEADS // ATT_KV_HEADS
ATT_KV_COLS = ATT_KV_HEADS * ATT_HEAD_DIM
WINDOW = 128
ROPE_DIM = 16
ROPE_THETA = 500000.0
HG_WIDTH = 512
MIX_WIDTH = ATT_WIDTH + HG_WIDTH
HG_HEAD_DIM = 128
HG_HEADS = 4
HG_CHUNK = 32
IN_COLS = ATT_WIDTH + 2 * ATT_KV_COLS + 4 * HG_WIDTH
ATT_COLS = ATT_WIDTH + 2 * ATT_KV_COLS
D_FF = 4 * D_MODEL
N_MOD = 6
EPS = 1e-6
ATT_SCALE = ATT_HEAD_DIM ** -0.5

ADAM_LR = 0.001
ADAM_B1 = 0.9
ADAM_B2 = 0.999
ADAM_EPS = 1e-08
ADAM_WD = 0.01
ADAM_STEP = 10

N_CHIPS = 4
N_DEV = 8
LANE = 128
VMEM_LIMIT = 48 * 1024 * 1024
VMEM_LIMIT_BIG = 58 * 1024 * 1024
MESH = pl.DeviceIdType.MESH

NT_DIMS = (((1,), (1,)), ((), ()))
TN_DIMS = (((0,), (0,)), ((), ()))


def _sds(shape, dtype):
    return jax.ShapeDtypeStruct(tuple(shape), dtype)


def _params(*sem, vmem_limit=None):
    return pltpu.CompilerParams(dimension_semantics=sem, vmem_limit_bytes=VMEM_LIMIT if vmem_limit is None else vmem_limit)


def _sigmoid(x):
    return 1.0 / (1.0 + jnp.exp(-x))


def _dot(a, b, dims=None):
    a, b = a.astype(BF16), b.astype(BF16)
    if dims is None:
        return jnp.dot(a, b, preferred_element_type=F32)
    return lax.dot_general(a, b, dims, preferred_element_type=F32)


def _rms_fwd(x, w):
    rstd = lax.rsqrt(jnp.mean(x * x, axis=-1, keepdims=True) + EPS)
    xh = x * rstd
    return xh * w, xh, rstd


def _rms_bwd(dy, xh, rstd, w):
    dxh = dy * w
    dx = rstd * (dxh - xh * jnp.mean(dxh * xh, axis=-1, keepdims=True))
    return dx, dy * xh


def _colsum(x):
    return jnp.sum(x, axis=0, keepdims=True)


def _rms_hat(x):
    rstd = lax.rsqrt(jnp.mean(x * x, axis=-1, keepdims=True) + EPS)
    return x * rstd, rstd


def _rms_bwd_gain(dy, gain, xh, rstd):
    dxh = dy * gain
    dx = rstd * (dxh - xh * jnp.mean(dxh * xh, axis=-1, keepdims=True))
    return dx, _colsum(dy * xh)


def _row_tile(rows, cap=256):
    return max(t for t in range(16, cap + 1, 16) if rows % t == 0)


HBM_SPEC = pl.BlockSpec(memory_space=pltpu.HBM)


def _mesh_pos():
    return lax.axis_index("x"), lax.axis_index("y"), lax.axis_index("c")


class _Comm:
    def __init__(self, ins, outs, sems, start, finish, aliases=()):
        self.ins, self.outs, self.sems = list(ins), list(outs), list(sems)
        self.start, self.finish, self.aliases = start, finish, tuple(aliases)


def _call(body, args, *, name, grid, in_specs, out_specs, out_shape, sem, scratch_shapes=(), comms=(), aliases=None,
          vmem_limit=None):
    scratch_shapes = list(scratch_shapes)
    if not comms:
        return pl.pallas_call(body, name=name, grid=grid, in_specs=in_specs, out_specs=out_specs, out_shape=out_shape,
                              input_output_aliases=dict(aliases or {}), scratch_shapes=scratch_shapes,
                              compiler_params=_params(*sem, vmem_limit=vmem_limit))(*args)
    single = not isinstance(out_shape, (list, tuple))
    out_specs_l = [out_specs] if single else list(out_specs)
    out_shape_l = [out_shape] if single else list(out_shape)
    n_in, n_out, n_scr = len(in_specs), len(out_shape_l), len(scratch_shapes)
    n_ci = [len(cm.ins) for cm in comms]
    n_co = [len(cm.outs) for cm in comms]
    n_cs = [len(cm.sems) for cm in comms]
    aliases = dict(aliases or {})
    for k, cm in enumerate(comms):
        for i, o in cm.aliases:
            aliases[n_in + sum(n_ci[:k]) + i] = n_out + sum(n_co[:k]) + o

    def fused(*refs):
        pos = [0]

        def take(n):
            part = refs[pos[0]:pos[0] + n]
            pos[0] += n
            return part

        ins = take(n_in)
        c_ins = [take(n) for n in n_ci]
        outs = take(n_out)
        c_outs = [take(n) for n in n_co]
        scr = take(n_scr)
        c_sems = [take(n) for n in n_cs]
        first, last = True, True
        for d, size in enumerate(grid):
            first = jnp.logical_and(first, pl.program_id(d) == 0)
            last = jnp.logical_and(last, pl.program_id(d) == size - 1)

        def run(which):
            for cm, ci, co, cs in zip(comms, c_ins, c_outs, c_sems):
                getattr(cm, which)(ci, co, cs)

        if grid:
            pl.when(first)(lambda: run("start"))
        else:
            run("start")
        body(*ins, *outs, *scr)
        if grid:
            pl.when(last)(lambda: run("finish"))
        else:
            run("finish")

    res = pl.pallas_call(
        fused, name=name, grid=grid, in_specs=list(in_specs) + [HBM_SPEC] * sum(n_ci),
        out_specs=out_specs_l + [HBM_SPEC] * sum(n_co), out_shape=out_shape_l + [s for cm in comms for s in cm.outs],
        input_output_aliases=aliases, scratch_shapes=scratch_shapes + [s for cm in comms for s in cm.sems],
        compiler_params=_params(*["arbitrary"] * len(grid), vmem_limit=vmem_limit),
    )(*args, *[a for cm in comms for a in cm.ins])
    main = res[:n_out]
    extra, at = [], n_out
    for n in n_co:
        extra.append(list(res[at:at + n]))
        at += n
    return (main[0] if single else list(main)), extra


def _mm(a, b, *, name, out_dtype, trans_b=False, tm=512, tn=None, extra=(), epi=None, b_spec=None, n_out=None, comms=()):
    m_total, k_total = a.shape
    if n_out is None:
        n_out = b.shape[0] if trans_b else b.shape[1]
    tn = n_out if tn is None else tn
    grid = (m_total // tm, n_out // tn)
    dims = NT_DIMS if trans_b else None

    def body(*refs):
        a_ref, b_ref = refs[0], refs[1]
        extra_refs = refs[2:2 + len(extra)]
        o_ref = refs[2 + len(extra)]
        acc = _dot(a_ref[...], b_ref[...], dims)
        if epi is not None:
            acc = epi(acc, *[r[...] for r in extra_refs])
        o_ref[...] = acc.astype(out_dtype)

    if b_spec is None:
        if trans_b:
            b_spec = pl.BlockSpec((tn, k_total), lambda i, j: (j, 0))
        else:
            b_spec = pl.BlockSpec((k_total, tn), lambda i, j: (0, j))
    in_specs = [pl.BlockSpec((tm, k_total), lambda i, j: (i, 0)), b_spec]
    in_specs += [pl.BlockSpec((tm, tn), lambda i, j: (i, j)) for _ in extra]
    return _call(
        body, (a, b, *extra), name=name, grid=grid, in_specs=in_specs,
        out_specs=pl.BlockSpec((tm, tn), lambda i, j: (i, j)),
        out_shape=_sds((m_total, n_out), out_dtype),
        sem=("parallel", "parallel"), comms=comms)


def _mm_tn(a, b, *, name, tk, tn, a_fn=None, out_shape=None, out_spec=None):
    m_total, k_total = a.shape
    n_total = b.shape[1]
    grid = (k_total // tk, n_total // tn)

    def body(a_ref, b_ref, o_ref):
        av = a_ref[...]
        part = _dot(av if a_fn is None else a_fn(av), b_ref[...], TN_DIMS)
        o_ref[...] = part.reshape(o_ref.shape)

    if out_shape is None:
        out_shape = _sds((k_total, n_total), F32)
        out_spec = pl.BlockSpec((tk, tn), lambda i, j: (i, j))
    return pl.pallas_call(
        body, name=name, grid=grid,
        in_specs=[pl.BlockSpec((m_total, tk), lambda i, j: (0, i)), pl.BlockSpec((m_total, tn), lambda i, j: (0, j))],
        out_specs=out_spec, out_shape=out_shape,
        compiler_params=_params("parallel", "parallel"),
    )(a, b)


def _ada_fwd(c_all, w_shard, b_shard):
    nb, ncol = c_all.shape[0], w_shard.shape[1]
    tn = 512

    def body(c_ref, w_ref, b_ref, o_ref):
        c = c_ref[...]
        o_ref[...] = _dot(c * _sigmoid(c), w_ref[...]) + b_ref[...]

    return pl.pallas_call(
        body, name="ada_fwd", grid=(ncol // tn,),
        in_specs=[pl.BlockSpec((nb, D_MODEL), lambda j: (0, 0)), pl.BlockSpec((D_MODEL, tn), lambda j: (0, j)),
                  pl.BlockSpec((1, tn), lambda j: (0, j))],
        out_specs=pl.BlockSpec((nb, tn), lambda j: (0, j)), out_shape=_sds((nb, ncol), F32),
        compiler_params=_params("parallel"),
    )(c_all, w_shard, b_shard)


def _adamw_math(g, w, m, v):
    m = ADAM_B1 * m + (1.0 - ADAM_B1) * g
    v = ADAM_B2 * v + (1.0 - ADAM_B2) * (g * g)
    m_hat = m / (1.0 - ADAM_B1 ** ADAM_STEP)
    v_hat = v / (1.0 - ADAM_B2 ** ADAM_STEP)
    delta = -ADAM_LR * (m_hat / (jnp.sqrt(v_hat) + ADAM_EPS) + ADAM_WD * w)
    return delta, m, v


def _ada_bwd_adamw(c_all, dmod_cols, w, m, v):
    nb, ncol = dmod_cols.shape
    tn = 256

    def body(c_ref, d_ref, w_ref, m_ref, v_ref, g_ref, dl_ref, nm_ref, nv_ref):
        c = c_ref[...]
        g = _dot(c * _sigmoid(c), d_ref[...], TN_DIMS)
        g_ref[...] = g
        dl_ref[...], nm_ref[...], nv_ref[...] = _adamw_math(g, w_ref[...], m_ref[...], v_ref[...])

    col = pl.BlockSpec((D_MODEL, tn), lambda j: (0, j))
    shp = _sds((D_MODEL, ncol), F32)
    return pl.pallas_call(
        body, name="ada_bwd_adamw", grid=(ncol // tn,),
        in_specs=[pl.BlockSpec((nb, D_MODEL), lambda j: (0, 0)), pl.BlockSpec((nb, tn), lambda j: (0, j)), col, col, col],
        out_specs=[col, col, col, col], out_shape=[shp, shp, shp, shp],
        compiler_params=_params("parallel"),
    )(c_all, dmod_cols, w, m, v)


def _adamw_halves(own, theirs, core, w, m, v, *, axis, name):
    r2, c2 = own.shape
    tr = _row_tile(r2)
    nt = r2 // tr

    def body(core_ref, own_ref, their_ref, w_ref, m_ref, v_ref, g_ref, dl_ref, nm_ref, nv_ref):
        g = jnp.where(pl.program_id(0) == core_ref[0], own_ref[...], their_ref[...])
        g_ref[...] = g
        dl_ref[...], nm_ref[...], nv_ref[...] = _adamw_math(g, w_ref[...], m_ref[...], v_ref[...])

    if axis == 0:
        full = pl.BlockSpec((tr, c2), lambda h, i, core_ref: (h * nt + i, 0))
    else:
        full = pl.BlockSpec((tr, c2), lambda h, i, core_ref: (i, h))
    half = pl.BlockSpec((tr, c2), lambda h, i, core_ref: (i, 0))
    shp = _sds(w.shape, F32)
    return pl.pallas_call(
        body, name=name,
        grid_spec=pltpu.PrefetchScalarGridSpec(num_scalar_prefetch=1, grid=(2, nt), in_specs=[half, half, full, full, full],
                                               out_specs=[full] * 4),
        out_shape=[shp] * 4, compiler_params=_params("parallel", "parallel"),
    )(core, own, theirs, w, m, v)


def _tok_spec(tm, width=D_MODEL):
    return pl.BlockSpec((None, tm, width), lambda b, i: (b, i, 0))


def _row_spec(width=D_MODEL):
    return pl.BlockSpec((None, 1, width), lambda b, i: (b, 0, 0))


def _vec_spec(width=D_MODEL):
    return pl.BlockSpec((1, width), lambda b, i: (0, 0))


class _RowsOf:
    def __init__(self, ref, first, count):
        self.ref, self.rows = ref, slice(first, first + count)

    def __getitem__(self, idx):
        return self.ref[self.rows, :]

    def __setitem__(self, idx, value):
        self.ref[self.rows, :] = value


def _mm_rows(a, b, *, name, tm, extra, extra_specs, out_specs, out_shape, epi, pro=None, trans_b=False, b_chunks=1, comms=(),
             parts=1, zero_per_seq=(), zero_once=(), vmem_limit=None):
    bsz, seq, k_total = a.shape
    kc = k_total // b_chunks
    dims = NT_DIMS if trans_b else None
    rows = tm // parts

    def body(*refs):
        a_ref, b_ref = refs[0], refs[1]
        ex, outs = refs[2:2 + len(extra)], refs[2 + len(extra):]
        if zero_per_seq:
            @pl.when(pl.program_id(1) == 0)
            def _():
                for k in zero_per_seq:
                    outs[k][...] = jnp.zeros_like(outs[k])
        if zero_once:
            @pl.when(jnp.logical_and(pl.program_id(0) == 0, pl.program_id(1) == 0))
            def _():
                for k in zero_once:
                    outs[k][...] = jnp.zeros_like(outs[k])

        def part_of(ref, p):
            tiled = len(ref.shape) == 2 and ref.shape[0] == tm
            return _RowsOf(ref, p * rows, rows) if tiled and parts > 1 else ref

        accs = []
        for p in range(parts):
            a_p, ex_p, outs_p = part_of(a_ref, p), [part_of(r, p) for r in ex], [part_of(r, p) for r in outs]
            if b_chunks == 1:
                accs.append(_dot(a_p[...] if pro is None else pro(a_p, ex_p, outs_p), b_ref[...], dims))
            else:
                acc = _dot(a_p[...][:, 0:kc], b_ref[0], NT_DIMS)
                for k in range(1, b_chunks):
                    acc = acc + _dot(a_p[...][:, k * kc:(k + 1) * kc], b_ref[k], NT_DIMS)
                accs.append(acc)
        for p in range(parts):
            epi(accs[p], [part_of(r, p) for r in ex], [part_of(r, p) for r in outs])

    b_spec = pl.BlockSpec(b.shape, lambda bb, i: (0,) * b.ndim)
    return _call(
        body, (a, b, *extra), name=name, grid=(bsz, seq // tm), in_specs=[_tok_spec(tm, k_total), b_spec, *extra_specs],
        out_specs=out_specs, out_shape=out_shape, sem=("arbitrary", "arbitrary"), comms=comms, vmem_limit=vmem_limit)


def _in_proj_fused(x, w, sc, sh, w_in_t, tables, comms=()):
    tm = 512
    bsz, seq, _ = x.shape
    half = ROPE_DIM // 2
    heads_per_slab = LANE // ATT_HEAD_DIM

    def pro(x_ref, ex, outs):
        y, _, _ = _rms_fwd(x_ref[...], ex[0][...])
        h = (y * (1.0 + ex[1][...]) + ex[2][...]).astype(BF16)
        outs[0][...] = h
        return h

    def epi(acc, ex, outs):
        c, u, d = ex[3][...], ex[4][...], ex[5][...]
        _, rec_ref, q_ref, k_ref, v_ref = outs
        for k in range(HG_SLABS):
            rec_ref[k] = acc[:, ATT_COLS + k * HG_WIDTH:ATT_COLS + (k + 1) * HG_WIDTH]

        def rope(z):
            return (z * c + pltpu.roll(z, half, 1) * u + pltpu.roll(z, LANE - half, 1) * d).astype(BF16)

        for s in range(ATT_WIDTH // LANE):
            slab = rope(acc[:, s * LANE:(s + 1) * LANE])
            for part in range(heads_per_slab):
                g, hh = divmod(s * heads_per_slab + part, ATT_GROUP)
                piece = slab[:, part * ATT_HEAD_DIM:(part + 1) * ATT_HEAD_DIM]
                for blk in range(tm // WINDOW):
                    q_ref[blk, g, hh * WINDOW:(hh + 1) * WINDOW, :] = piece[blk * WINDOW:(blk + 1) * WINDOW]
        rk = rope(acc[:, ATT_WIDTH:ATT_WIDTH + LANE])
        vv = acc[:, ATT_WIDTH + LANE:ATT_COLS].astype(BF16)
        for g in range(ATT_KV_HEADS):
            k_ref[g] = rk[:, g * ATT_HEAD_DIM:(g + 1) * ATT_HEAD_DIM]
            v_ref[g] = vv[:, g * ATT_HEAD_DIM:(g + 1) * ATT_HEAD_DIM]

    tab = pl.BlockSpec((tm, LANE), lambda b, i: (i, 0))
    kv_spec = pl.BlockSpec((None, ATT_KV_HEADS, tm, ATT_HEAD_DIM), lambda b, i: (b, 0, i, 0))
    kv_shape = _sds((bsz, ATT_KV_HEADS, seq, ATT_HEAD_DIM), BF16)
    q_spec = pl.BlockSpec((None, tm // WINDOW, ATT_KV_HEADS, GROUP_ROWS, ATT_HEAD_DIM), lambda b, i: (b, i, 0, 0, 0))
    return _mm_rows(x, w_in_t, name="in_proj", tm=tm, extra=(w, sc, sh, *tables),
                    extra_specs=[_vec_spec(), _row_spec(), _row_spec(), tab, tab, tab],
                    out_specs=[_tok_spec(tm), pl.BlockSpec((None, HG_SLABS, tm, HG_WIDTH), lambda b, i: (b, 0, i, 0)), q_spec,
                               kv_spec, kv_spec],
                    out_shape=[_sds(x.shape, BF16), _sds((bsz, HG_SLABS, seq, HG_WIDTH), F32),
                               _sds((bsz, seq // WINDOW, ATT_KV_HEADS, GROUP_ROWS, ATT_HEAD_DIM), BF16), kv_shape, kv_shape],
                    pro=pro, epi=epi, trans_b=True, comms=comms)


def _rope_tables(seq):
    half = ROPE_DIM // 2
    inv_freq = ROPE_THETA ** (-jnp.arange(0, ROPE_DIM, 2, dtype=F32) / ROPE_DIM)
    ang = jnp.arange(seq, dtype=F32)[:, None] * inv_freq[None, :]
    cos, sin = jnp.cos(ang), jnp.sin(ang)
    rest = ATT_HEAD_DIM - ROPE_DIM
    ones, zeros, zh = jnp.ones((seq, rest), F32), jnp.zeros((seq, rest), F32), jnp.zeros((seq, half), F32)
    reps = LANE // ATT_HEAD_DIM
    t_cos = jnp.tile(jnp.concatenate([cos, cos, ones], axis=1), (1, reps))
    t_up = jnp.tile(jnp.concatenate([zh, sin, zeros], axis=1), (1, reps))
    t_dn = jnp.tile(jnp.concatenate([-sin, zh, zeros], axis=1), (1, reps))
    return t_cos, t_up, t_dn


GROUP_ROWS = ATT_GROUP * WINDOW


ATT_BPS = 2


MASKED = -1e30


def _band_biases():
    row = jnp.arange(GROUP_ROWS)[:, None] % WINDOW
    col = jnp.arange(2 * WINDOW)[None, :]
    own = jnp.logical_and(col >= WINDOW, col - WINDOW <= row)
    before = jnp.logical_and(col < WINDOW, col > row)
    return (jnp.where(jnp.logical_or(own, before), 0.0, MASKED).astype(F32), jnp.where(own, 0.0, MASKED).astype(F32))


def _band_bias(full_ref, first_ref, has_prev):
    return full_ref[...] if has_prev is True else jnp.where(has_prev, full_ref[...], first_ref[...])


def _sink_column(sink_ref, g):
    head = lax.broadcasted_iota(jnp.int32, (GROUP_ROWS, 1), 0) // WINDOW
    col = jnp.full((GROUP_ROWS, 1), sink_ref[0, g * ATT_GROUP], F32)
    for hh in range(1, ATT_GROUP):
        col = jnp.where(head == hh, sink_ref[0, g * ATT_GROUP + hh], col)
    return col


def _sink_row(sink_ref, g):
    return jnp.concatenate([jnp.full((1, WINDOW), sink_ref[0, g * ATT_GROUP + hh], F32) for hh in range(ATT_GROUP)], axis=1)


def _bias_spec(transposed=False):
    shape = (2 * WINDOW, GROUP_ROWS) if transposed else (GROUP_ROWS, 2 * WINDOW)
    return pl.BlockSpec(shape, lambda b, i: (0, 0))


def _attn_specs():
    q_spec = pl.BlockSpec((None, ATT_BPS, ATT_KV_HEADS, GROUP_ROWS, ATT_HEAD_DIM), lambda b, i: (b, i, 0, 0, 0))
    kv_cur = pl.BlockSpec((None, ATT_KV_HEADS, ATT_BPS * WINDOW, ATT_HEAD_DIM), lambda b, i: (b, 0, i, 0))
    kv_prev = pl.BlockSpec((None, ATT_KV_HEADS, WINDOW, ATT_HEAD_DIM), lambda b, i: (b, 0, jnp.maximum(ATT_BPS * i - 1, 0), 0))
    return q_spec, kv_cur, kv_prev


def _band(prev_ref, cur_ref, g, blk):
    own = cur_ref[g, blk * WINDOW:(blk + 1) * WINDOW]
    before = prev_ref[g] if blk == 0 else cur_ref[g, (blk - 1) * WINDOW:blk * WINDOW]
    return jnp.concatenate([before, own], axis=0)


def _attn_fwd(qh, kh, vh, sinks, w_norm, biases, comms=()):
    bsz, nblk = qh.shape[0], qh.shape[1]
    seq = nblk * WINDOW
    rows = ATT_BPS * WINDOW

    def body(sink_ref, q_ref, kc_ref, kp_ref, vc_ref, vp_ref, w_ref, full_ref, first_ref, raw_ref, an_ref, l_ref):
        l_ref[...] = jnp.zeros_like(l_ref)
        for blk in range(ATT_BPS):
            bias = _band_bias(full_ref, first_ref, True if blk else pl.program_id(1) > 0)
            for g in range(ATT_KV_HEADS):
                keys, vals = _band(kp_ref, kc_ref, g, blk), _band(vp_ref, vc_ref, g, blk)
                sink = _sink_column(sink_ref, g)
                s = _dot(q_ref[blk, g], keys, NT_DIMS) * ATT_SCALE + bias
                m = jnp.maximum(jnp.max(s, axis=-1, keepdims=True), sink)
                p = jnp.exp(s - m)
                den = jnp.sum(p, axis=-1, keepdims=True) + jnp.exp(sink - m)
                o = _dot(p / den, vals)
                lse = m + jnp.log(den)
                tok = slice(blk * WINDOW, (blk + 1) * WINDOW)
                for hh in range(ATT_GROUP):
                    h = g * ATT_GROUP + hh
                    raw_ref[tok, h * ATT_HEAD_DIM:(h + 1) * ATT_HEAD_DIM] = o[hh * WINDOW:(hh + 1) * WINDOW]
                    l_ref[tok, h:h + 1] = lse[hh * WINDOW:(hh + 1) * WINDOW]
        y, _, _ = _rms_fwd(raw_ref[...], w_ref[...])
        an_ref[...] = y.astype(BF16)

    cur = lambda width: pl.BlockSpec((None, rows, width), lambda b, i: (b, i, 0))
    q_spec, kv_cur, kv_prev = _attn_specs()
    return _call(
        body, (sinks, qh, kh, kh, vh, vh, w_norm, *biases), name="attn_fwd", grid=(bsz, nblk // ATT_BPS),
        in_specs=[pl.BlockSpec(memory_space=pltpu.SMEM), q_spec, kv_cur, kv_prev, kv_cur, kv_prev, _vec_spec(ATT_WIDTH),
                  _bias_spec(), _bias_spec()],
        out_specs=[cur(ATT_WIDTH), cur(ATT_WIDTH), cur(LANE)],
        out_shape=[_sds((bsz, seq, ATT_WIDTH), F32), _sds((bsz, seq, MIX_WIDTH), BF16), _sds((bsz, seq, LANE), F32)],
        sem=("parallel", "parallel"), comms=comms)


HG_Q0 = ATT_COLS // LANE
HG_F0 = HG_Q0 + HG_HEADS
HG_I0 = HG_F0 + HG_HEADS
HG_G0 = HG_I0 + HG_HEADS
HG_SLABS = 4
HG_Q, HG_F, HG_I, HG_G = range(HG_SLABS)
HG_TOK = 256
HG_NCH = HG_TOK // HG_CHUNK
HG_HPS = 2


def _block_masks():
    row = jnp.arange(HG_TOK)[:, None]
    col = jnp.arange(HG_TOK)[None, :]
    same = (row // HG_CHUNK) == (col // HG_CHUNK)
    return jnp.logical_and(same, col <= row).astype(F32), jnp.logical_and(same, col >= row).astype(F32)


def _row_in_chunk():
    return lax.broadcasted_iota(jnp.int32, (HG_TOK, LANE), 0) % HG_CHUNK


def _chunk_cumsum(x, reverse=False):
    ric = _row_in_chunk()
    shift = 1
    while shift < HG_CHUNK:
        if reverse:
            x = x + jnp.where(ric < HG_CHUNK - shift, pltpu.roll(x, HG_TOK - shift, 0), 0.0)
        else:
            x = x + jnp.where(ric >= shift, pltpu.roll(x, shift, 0), 0.0)
        shift *= 2
    return x


def _chunk_rows(rows):
    stacked = jnp.concatenate([r[None] for r in rows], axis=0)
    return jnp.broadcast_to(stacked, (HG_NCH, HG_CHUNK, LANE)).reshape(HG_TOK, LANE)


def _chunk_slices(x):
    return [x[j * HG_CHUNK:(j + 1) * HG_CHUNK] for j in range(HG_NCH)]


def _hgrn_common(tbl, hf, hq):
    lb = _sigmoid(tbl[1:2] - tbl[0:1])
    sig = _sigmoid(hf)
    f = lb + (1.0 - lb) * sig
    sq = _sigmoid(hq)
    q, k = hq * sq, 1.0 - f
    b = _chunk_cumsum(jnp.log(f))
    last = [b[(j + 1) * HG_CHUNK - 1:(j + 1) * HG_CHUNK] for j in range(HG_NCH)]
    bl = _chunk_rows(last)
    e_b, e_nb, e_rem = jnp.exp(b), jnp.exp(-b), jnp.exp(bl - b)
    e_last = [jnp.exp(r) for r in last]
    return dict(lb=lb, sig=sig, f=f, sq=sq, q=q, k=k, e_b=e_b, e_nb=e_nb, e_rem=e_rem, e_last=e_last,
                qd=q * e_b, kd=k * e_nb, ku=k * e_rem)


def _hgrn_fwd(proj, lb_table, norm_w, mix_in, masks, comms=()):
    bsz, _, seq, _ = proj.shape
    nstep = seq // HG_TOK

    def body(tbl_ref, nw_ref, p_ref, mix_ref, lower_ref, o_ref, rec_ref, st_ref, s_scr):
        @pl.when(pl.program_id(2) == 0)
        def _():
            s_scr[...] = jnp.zeros_like(s_scr)

        lower = lower_ref[...]
        for hp in range(HG_HPS):
            ls = slice(hp * LANE, (hp + 1) * LANE)
            v, hg = p_ref[HG_I, :, ls], p_ref[HG_G, :, ls]
            t = _hgrn_common(tbl_ref[:, ls], p_ref[HG_F, :, ls], p_ref[HG_Q, :, ls])
            a = _dot(t["qd"], t["kd"], NT_DIMS) * lower
            o_intra = _dot(a, v)
            v_c, ku_c, qd_c = [_chunk_slices(z.astype(BF16)) for z in (v, t["ku"], t["qd"])]
            updates = [_dot(v_c[j], ku_c[j], TN_DIMS) for j in range(HG_NCH)]
            st = s_scr[hp]
            states = []
            for j in range(HG_NCH):
                states.append(st)
                st = st * t["e_last"][j] + updates[j]
            s_scr[hp] = st
            o = o_intra + jnp.concatenate([_dot(qd_c[j], states[j], NT_DIMS) for j in range(HG_NCH)], axis=0)
            st_ref[hp, 0] = states[0]
            o_ref[:, ls] = o
            y, _, _ = _rms_fwd(o, nw_ref[...])
            rec_ref[:, ls] = (y * (hg * _sigmoid(hg))).astype(BF16)

    width = HG_HPS * LANE
    head_out = pl.BlockSpec((None, HG_TOK, width), lambda b, h, t: (b, t, h))
    mix_out = pl.BlockSpec((None, HG_TOK, width), lambda b, h, t: (b, t, ATT_WIDTH // width + h))
    return _call(
        body, (lb_table, norm_w, proj, mix_in, masks[0]), name="hgrn_fwd", grid=(bsz, HG_HEADS // HG_HPS, nstep),
        in_specs=[pl.BlockSpec((2, width), lambda b, h, t: (0, h)), pl.BlockSpec((1, LANE), lambda b, h, t: (0, 0)),
                  pl.BlockSpec((None, HG_SLABS, HG_TOK, width), lambda b, h, t: (b, 0, t, h)), pl.BlockSpec(memory_space=pl.ANY),
                  pl.BlockSpec((HG_TOK, HG_TOK), lambda b, h, t: (0, 0))],
        out_specs=[head_out, mix_out,
                   pl.BlockSpec((None, HG_HPS, 1, LANE, LANE), lambda b, h, t: (b, h, t, 0, 0))],
        out_shape=[_sds((bsz, seq, HG_WIDTH), F32), _sds(mix_in.shape, BF16),
                   _sds((bsz, HG_HEADS, nstep, LANE, LANE), F32)],
        scratch_shapes=[pltpu.VMEM((HG_HPS, LANE, LANE), F32)],
        sem=("parallel", "parallel", "arbitrary"), comms=comms, aliases={3: 1})


def _out_proj_fused(cat, w_out, x, post_w, g1, pre_w, sc2, sh2):
    tm = 512

    def epi(mix, ex, outs):
        x_ref, pw_ref, g1_ref, w2_ref, sc_ref, sh_ref = ex
        outs[0][...] = mix
        n1, _, _ = _rms_fwd(mix, pw_ref[...])
        x1 = x_ref[...] + g1_ref[...] * n1
        outs[1][...] = x1
        y2, _, _ = _rms_fwd(x1, w2_ref[...])
        outs[2][...] = (y2 * (1.0 + sc_ref[...]) + sh_ref[...]).astype(BF16)

    return _mm_rows(cat, w_out, name="out_proj", tm=tm, extra=(x, post_w, g1, pre_w, sc2, sh2),
                    extra_specs=[_tok_spec(tm), _vec_spec(), _row_spec(), _vec_spec(), _row_spec(), _row_spec()],
                    out_specs=[_tok_spec(tm), _tok_spec(tm), _tok_spec(tm)],
                    out_shape=[_sds(x.shape, F32), _sds(x.shape, F32), _sds(x.shape, BF16)], epi=epi)


def _acc_out(ref, first, value):
    @pl.when(first)
    def _():
        ref[...] = value

    @pl.when(jnp.logical_not(first))
    def _():
        ref[...] += value


def _down_proj_fused(r, w_down, x1, post_w, g2, target):
    tm = 512
    bsz = x1.shape[0]

    def pro(r_ref, ex, outs):
        rv = r_ref[...]
        return rv * rv

    def epi(down, ex, outs):
        x1_ref, w_ref, g2_ref, t_ref = ex
        loss_ref, dy_ref, dd_ref, dg2_ref, dw_ref = outs
        w, g2v = w_ref[...], g2_ref[...]
        gain = g2v * w
        dh, rstd = _rms_hat(down)
        err = x1_ref[...] + dh * gain - t_ref[...]
        part = (0.5 / D_MODEL) * jnp.sum(jnp.sum(err * err, axis=-1, keepdims=True), axis=0, keepdims=True)
        loss_ref[...] += jnp.broadcast_to(part, (1, LANE))
        dy = err * (1.0 / D_MODEL)
        dy_ref[...] = dy
        dd, per_col = _rms_bwd_gain(dy, gain, dh, rstd)
        dd_ref[...] = dd.astype(BF16)
        dg2_ref[...] += per_col * w
        dw_ref[...] += per_col * g2v

    return _mm_rows(r, w_down, name="down_proj", tm=tm, extra=(x1, post_w, g2, target),
                    extra_specs=[_tok_spec(tm), _vec_spec(), _row_spec(), _tok_spec(tm)],
                    out_specs=[_vec_spec(LANE), _tok_spec(tm), _tok_spec(tm), _row_spec(), _vec_spec()],
                    out_shape=[_sds((1, LANE), F32), _sds(x1.shape, F32), _sds(x1.shape, BF16), _sds((bsz, 1, D_MODEL), F32),
                               _sds((1, D_MODEL), F32)], pro=pro, epi=epi, parts=2, zero_per_seq=(3,), zero_once=(0, 4),
                    vmem_limit=VMEM_LIMIT_BIG)


def _up_bwd_fused(dpre, w_up4, dy, x1, mix, pre_w, sc2, post_w, g1, comms=()):
    tm = 512
    bsz = x1.shape[0]

    def epi(dh2v, ex, outs):
        dy_ref, x1_ref, mix_ref, w2_ref, sc_ref, pw_ref, g1_ref = ex
        dx1_ref, dmix_ref, dsc_ref, dsh_ref, dg1_ref, dw2_ref, dpw_ref = outs
        w2, pw, g1v = w2_ref[...], pw_ref[...], g1_ref[...]
        mod2 = 1.0 + sc_ref[...]
        xh2, rstd2 = _rms_hat(x1_ref[...])
        dsh_ref[...] += _colsum(dh2v)
        dx1n, per_col2 = _rms_bwd_gain(dh2v, mod2 * w2, xh2, rstd2)
        dsc_ref[...] += per_col2 * w2
        dw2_ref[...] += per_col2 * mod2
        dx1 = dy_ref[...] + dx1n
        dx1_ref[...] = dx1
        mh, rstd1 = _rms_hat(mix_ref[...])
        dmix, per_col1 = _rms_bwd_gain(dx1, g1v * pw, mh, rstd1)
        dmix_ref[...] = dmix.astype(BF16)
        dg1_ref[...] += per_col1 * pw
        dpw_ref[...] += per_col1 * g1v

    row_shape = _sds((bsz, 1, D_MODEL), F32)
    vec_shape = _sds((1, D_MODEL), F32)
    return _mm_rows(dpre, w_up4, name="up_bwd", tm=tm, extra=(dy, x1, mix, pre_w, sc2, post_w, g1),
                    extra_specs=[_tok_spec(tm), _tok_spec(tm), _tok_spec(tm), _vec_spec(), _row_spec(), _vec_spec(), _row_spec()],
                    out_specs=[_tok_spec(tm), _tok_spec(tm), _row_spec(), _row_spec(), _row_spec(), _vec_spec(), _vec_spec()],
                    out_shape=[_sds(x1.shape, F32), _sds(x1.shape, BF16), row_shape, row_shape, row_shape, vec_shape, vec_shape],
                    epi=epi, b_chunks=w_up4.shape[0], comms=comms, parts=2, zero_per_seq=(2, 3, 4), zero_once=(5, 6),
                    vmem_limit=VMEM_LIMIT_BIG)


def _norm1_bwd(dh1, dx1, x, pre_w, sc1, tm=512, comms=()):
    bsz, seq, _ = x.shape

    def body(dh_ref, dx1_ref, x_ref, w_ref, sc_ref, gx_ref, dsc_ref, dsh_ref, dw_ref):
        b, i = pl.program_id(0), pl.program_id(1)
        w = w_ref[...]
        dh = dh_ref[...]
        mod = 1.0 + sc_ref[...]
        xh, rstd = _rms_hat(x_ref[...])
        dx, per_col = _rms_bwd_gain(dh, mod * w, xh, rstd)
        _acc_out(dsh_ref, i == 0, _colsum(dh))
        _acc_out(dsc_ref, i == 0, per_col * w)
        _acc_out(dw_ref, jnp.logical_and(b == 0, i == 0), per_col * mod)
        gx_ref[...] = dx1_ref[...] + dx

    row_shape = _sds((bsz, 1, D_MODEL), F32)
    return _call(
        body, (dh1, dx1, x, pre_w, sc1), name="norm1_bwd", grid=(bsz, seq // tm),
        in_specs=[_tok_spec(tm), _tok_spec(tm), _tok_spec(tm), _vec_spec(), _row_spec()],
        out_specs=[_tok_spec(tm), _row_spec(), _row_spec(), _vec_spec()],
        out_shape=[_sds(x.shape, F32), row_shape, row_shape, _sds((1, D_MODEL), F32)],
        sem=("arbitrary", "arbitrary"), comms=comms)


def _hgrn_bwd(dcat, proj, o_raw, states, lb_table, norm_w, masks, comms=()):
    bsz, _, seq, _ = proj.shape
    nstep = seq // HG_TOK
    rec0 = ATT_WIDTH // LANE
    width = HG_HPS * LANE
    slabs = (HG_Q0, HG_F0, HG_I0, HG_G0)
    n_steps = (HG_HEADS // HG_HPS) * bsz * nstep
    assert n_steps >= 2

    def body(tbl_ref, nw_ref, dr_ref, p_ref, o_ref, st_ref, lower_ref, upper_ref,
             dproj_ref, dlb_ref, dnw_ref, ds_scr, grad_buf, grad_sem):
        h, b, t = pl.program_id(0), pl.program_id(1), pl.program_id(2)
        step = (h * bsz + b) * nstep + t
        slot = step % 2
        dq_k, df_k, di_k, dg_k = range(4)

        def grad_copies(of_step):
            hh, bb, tt = of_step // (bsz * nstep), (of_step // nstep) % bsz, of_step % nstep
            rows = pl.ds(pl.multiple_of((nstep - 1 - tt) * HG_TOK, HG_TOK), HG_TOK)
            return [pltpu.make_async_copy(
                grad_buf.at[of_step % 2, k],
                dproj_ref.at[bb, rows, pl.ds(pl.multiple_of(slabs[k] * LANE + hh * width, width), width)],
                grad_sem.at[of_step % 2, k]) for k in range(4)]

        @pl.when(step >= 2)
        def _():
            for cp in grad_copies(step - 2):
                cp.wait()

        @pl.when(t == 0)
        def _():
            ds_scr[...] = jnp.zeros_like(ds_scr)

        lower, upper = lower_ref[...], upper_ref[...]
        dlb_parts = []
        dnw_acc = jnp.zeros((1, LANE), F32)
        for hp in range(HG_HPS):
            ls = slice(hp * LANE, (hp + 1) * LANE)
            hq, v, hg = p_ref[HG_Q, :, ls], p_ref[HG_I, :, ls], p_ref[HG_G, :, ls]
            nw = nw_ref[...]
            c = _hgrn_common(tbl_ref[:, ls], p_ref[HG_F, :, ls], hq)
            qd, kd, ku = c["qd"], c["kd"], c["ku"]
            y, on, rstd = _rms_fwd(o_ref[:, ls], nw)
            sg = _sigmoid(hg)
            dr = dr_ref[:, ls]
            grad_buf[slot, dg_k, :, ls] = (dr * y * (sg * (1.0 + hg * (1.0 - sg)))).astype(BF16)
            do, dnw_rows = _rms_bwd(dr * (hg * sg), on, rstd, nw)
            at = _dot(kd, qd, NT_DIMS) * upper
            da = _dot(do, v, NT_DIMS) * lower
            dat = _dot(v, do, NT_DIMS) * upper
            dv = _dot(at, do)
            dqd = _dot(da, kd)
            dkd = _dot(dat, qd)
            do_c, qd_c, v_c, ku_c = [_chunk_slices(z.astype(BF16)) for z in (do, qd, v, ku)]
            outer = [_dot(do_c[j], qd_c[j], TN_DIMS) for j in range(HG_NCH)]
            ds = ds_scr[hp]
            ds_after = [None] * HG_NCH
            for j in reversed(range(HG_NCH)):
                ds_after[j] = ds
                ds = outer[j] + ds * c["e_last"][j]
            ds_scr[hp] = ds
            updates = [_dot(v_c[j], ku_c[j], TN_DIMS) for j in range(HG_NCH)]
            states = [st_ref[hp, 0]]
            for j in range(HG_NCH - 1):
                states.append(states[j] * c["e_last"][j] + updates[j])
            dv = dv + jnp.concatenate([_dot(ku_c[j], ds_after[j], NT_DIMS) for j in range(HG_NCH)], axis=0)
            dqd = dqd + jnp.concatenate([_dot(do_c[j], states[j]) for j in range(HG_NCH)], axis=0)
            dku = jnp.concatenate([_dot(v_c[j], ds_after[j]) for j in range(HG_NCH)], axis=0)
            dku_ku = dku * ku
            dbl = [_colsum(states[j] * ds_after[j]) * c["e_last"][j] + _colsum(dku_ku[j * HG_CHUNK:(j + 1) * HG_CHUNK])
                   for j in range(HG_NCH)]
            dk = dkd * c["e_nb"] + dku * c["e_rem"]
            db = dqd * qd - dkd * kd - dku_ku + jnp.where(_row_in_chunk() == HG_CHUNK - 1, _chunk_rows(dbl), 0.0)
            dfv = _chunk_cumsum(db, reverse=True) / c["f"] - dk
            sig, sq = c["sig"], c["sq"]
            grad_buf[slot, df_k, :, ls] = (dfv * (1.0 - c["lb"]) * sig * (1.0 - sig)).astype(BF16)
            grad_buf[slot, dq_k, :, ls] = (dqd * c["e_b"] * (sq * (1.0 + hq * (1.0 - sq)))).astype(BF16)
            grad_buf[slot, di_k, :, ls] = dv.astype(BF16)
            dlb_parts.append(_colsum(dfv * (1.0 - sig)))
            dnw_acc = dnw_acc + _colsum(dnw_rows)
        _acc_out(dlb_ref, jnp.logical_and(b == 0, t == 0), jnp.concatenate(dlb_parts, axis=1))
        _acc_out(dnw_ref, jnp.logical_and(h == 0, jnp.logical_and(b == 0, t == 0)), dnw_acc)
        for cp in grad_copies(step):
            cp.start()

        @pl.when(step == n_steps - 1)
        def _():
            for cp in grad_copies(step - 1) + grad_copies(step):
                cp.wait()

    rev = lambda t: nstep - 1 - t
    slab = lambda first: pl.BlockSpec((None, HG_TOK, width), lambda h, b, t: (b, rev(t), first // HG_HPS + h))
    head = pl.BlockSpec((None, HG_TOK, width), lambda h, b, t: (b, rev(t), h))
    return _call(
        body, (lb_table, norm_w, dcat, proj, o_raw, states, *masks), name="hgrn_bwd",
        grid=(HG_HEADS // HG_HPS, bsz, nstep),
        in_specs=[pl.BlockSpec((2, width), lambda h, b, t: (0, h)), pl.BlockSpec((1, LANE), lambda h, b, t: (0, 0)),
                  slab(rec0), pl.BlockSpec((None, HG_SLABS, HG_TOK, width), lambda h, b, t: (b, 0, rev(t), h)), head,
                  pl.BlockSpec((None, HG_HPS, 1, LANE, LANE), lambda h, b, t: (b, h, rev(t), 0, 0)),
                  pl.BlockSpec((HG_TOK, HG_TOK), lambda h, b, t: (0, 0)), pl.BlockSpec((HG_TOK, HG_TOK), lambda h, b, t: (0, 0))],
        out_specs=[pl.BlockSpec(memory_space=pl.ANY), pl.BlockSpec((1, width), lambda h, b, t: (0, h)),
                   pl.BlockSpec((1, LANE), lambda h, b, t: (0, 0))],
        out_shape=[_sds((bsz, seq, IN_COLS), BF16), _sds((1, HG_WIDTH), F32), _sds((1, LANE), F32)],
        scratch_shapes=[pltpu.VMEM((HG_HPS, LANE, LANE), F32), pltpu.VMEM((2, 4, HG_TOK, width), BF16),
                        pltpu.SemaphoreType.DMA((2, 4))],
        sem=("arbitrary", "arbitrary", "arbitrary"), comms=comms)


def _attn_bwd(dcat, raw, w_norm, qh, kh, vh, lse, sinks, tables, biases, dproj, comms=()):
    bsz, nblk = qh.shape[0], qh.shape[1]
    seq = nblk * WINDOW
    nstep = nblk // ATT_BPS
    half = ROPE_DIM // 2

    def body(sink_ref, da_ref, raw_ref, w_ref, q_ref, kc_ref, kp_ref, vc_ref, vp_ref, l_ref, c_ref, u_ref, d_ref,
             full_ref, first_ref, dproj_ref, o_ref, dw_ref, dsink_ref, carry_k, carry_v):
        b, i = pl.program_id(0), pl.program_id(1)
        first = jnp.logical_and(b == 0, i == 0)

        @pl.when(i == 0)
        def _():
            carry_k[...] = jnp.zeros_like(carry_k)
            carry_v[...] = jnp.zeros_like(carry_v)

        w = w_ref[...]
        _, on, rstd = _rms_fwd(raw_ref[...], w)
        do_step, dw_rows = _rms_bwd(da_ref[...], on, rstd, w)
        _acc_out(dw_ref, first, _colsum(dw_rows))
        lane8 = lax.broadcasted_iota(jnp.int32, (1, ATT_Q_HEADS), 1)
        dsink = jnp.zeros((1, ATT_Q_HEADS), F32)
        head_cols = jnp.where(lax.broadcasted_iota(jnp.int32, (2 * ATT_Q_HEADS, ATT_WIDTH), 1) // ATT_HEAD_DIM
                              == lax.broadcasted_iota(jnp.int32, (2 * ATT_Q_HEADS, ATT_WIDTH), 0), 1.0, 0.0)
        from_next_k, from_next_v = carry_k[...], carry_v[...]
        for blk in reversed(range(ATT_BPS)):
            tok = slice(blk * WINDOW, (blk + 1) * WINDOW)
            bias = _band_bias(full_ref, first_ref, True if blk else i < nstep - 1)
            do_all = do_step[tok]
            c, u, d = c_ref[tok, :], u_ref[tok, :], d_ref[tok, :]
            lse_t = l_ref[tok, :].T
            prod = do_all * raw_ref[tok, :]
            prod_hi = prod.astype(BF16)
            prod_lo = prod - prod_hi.astype(F32)
            dsum_t = _dot(head_cols, prod_hi, NT_DIMS) + _dot(head_cols, prod_lo, NT_DIMS)

            def unrope(g):
                return (g * c + pltpu.roll(g * u, LANE - half, 1) + pltpu.roll(g * d, half, 1)).astype(BF16)

            dq_parts, dk_own, dk_before, dv_own, dv_before = [], [], [], [], []
            for g in range(ATT_KV_HEADS):
                heads = [slice((g * ATT_GROUP + hh) * ATT_HEAD_DIM, (g * ATT_GROUP + hh + 1) * ATT_HEAD_DIM)
                         for hh in range(ATT_GROUP)]
                q = q_ref[blk, g]
                keys, vals = _band(kp_ref, kc_ref, g, blk), _band(vp_ref, vc_ref, g, blk)
                do_g = jnp.concatenate([do_all[:, hs] for hs in heads], axis=0)
                group_row = lambda z: jnp.concatenate(
                    [z[g * ATT_GROUP + hh:g * ATT_GROUP + hh + 1, :] for hh in range(ATT_GROUP)], axis=1)
                dsum, lse_g = group_row(dsum_t), group_row(lse_t)
                p_t = jnp.exp(_dot(keys, q, NT_DIMS) * ATT_SCALE + bias - lse_g)
                sink_part = jnp.exp(_sink_row(sink_ref, g) - lse_g) * dsum
                for hh in range(ATT_GROUP):
                    head_sum = jnp.sum(sink_part[:, hh * WINDOW:(hh + 1) * WINDOW], axis=1, keepdims=True)
                    dsink = dsink - jnp.where(lane8 == g * ATT_GROUP + hh, head_sum, 0.0)
                ds_t = p_t * (_dot(vals, do_g, NT_DIMS) - dsum) * ATT_SCALE
                dq_g = _dot(ds_t, keys, TN_DIMS)
                dq_parts += [dq_g[hh * WINDOW:(hh + 1) * WINDOW] for hh in range(ATT_GROUP)]
                dk_g = _dot(ds_t, q)
                dv_g = _dot(p_t, do_g)
                dk_before.append(dk_g[:WINDOW])
                dk_own.append(dk_g[WINDOW:])
                dv_before.append(dv_g[:WINDOW])
                dv_own.append(dv_g[WINDOW:])
            per_slab = LANE // ATT_HEAD_DIM
            for s in range(ATT_WIDTH // LANE):
                slab = jnp.concatenate(dq_parts[s * per_slab:(s + 1) * per_slab], axis=1)
                o_ref[tok, s * LANE:(s + 1) * LANE] = unrope(slab)
            o_ref[tok, ATT_WIDTH:ATT_WIDTH + LANE] = unrope(jnp.concatenate(dk_own, axis=1) + from_next_k)
            o_ref[tok, ATT_WIDTH + LANE:ATT_COLS] = (jnp.concatenate(dv_own, axis=1) + from_next_v).astype(BF16)
            from_next_k, from_next_v = jnp.concatenate(dk_before, axis=1), jnp.concatenate(dv_before, axis=1)
        carry_k[...] = from_next_k
        carry_v[...] = from_next_v
        _acc_out(dsink_ref, first, dsink)

    rows = ATT_BPS * WINDOW
    rev = lambda i: nstep - 1 - i
    cur = lambda width: pl.BlockSpec((None, rows, width), lambda b, i: (b, rev(i), 0))
    q_spec = pl.BlockSpec((None, ATT_BPS, ATT_KV_HEADS, GROUP_ROWS, ATT_HEAD_DIM), lambda b, i: (b, rev(i), 0, 0, 0))
    kv_cur = pl.BlockSpec((None, ATT_KV_HEADS, rows, ATT_HEAD_DIM), lambda b, i: (b, 0, rev(i), 0))
    kv_prev = pl.BlockSpec((None, ATT_KV_HEADS, WINDOW, ATT_HEAD_DIM), lambda b, i: (b, 0, jnp.maximum(ATT_BPS * rev(i) - 1, 0), 0))
    tab = pl.BlockSpec((rows, LANE), lambda b, i: (rev(i), 0))
    return _call(
        body, (sinks, dcat, raw, w_norm, qh, kh, kh, vh, vh, lse, *tables, *biases, dproj), name="attn_bwd", grid=(bsz, nstep),
        in_specs=[pl.BlockSpec(memory_space=pltpu.SMEM), cur(ATT_WIDTH), cur(ATT_WIDTH), _vec_spec(ATT_WIDTH), q_spec,
                  kv_cur, kv_prev, kv_cur, kv_prev, cur(LANE), tab, tab, tab, _bias_spec(True), _bias_spec(True),
                  pl.BlockSpec(memory_space=pl.ANY)],
        out_specs=[cur(ATT_COLS), _vec_spec(ATT_WIDTH), _vec_spec(ATT_Q_HEADS)],
        out_shape=[_sds(dproj.shape, BF16), _sds((1, ATT_WIDTH), F32), _sds((1, ATT_Q_HEADS), F32)],
        scratch_shapes=[pltpu.VMEM((WINDOW, LANE), F32), pltpu.VMEM((WINDOW, LANE), F32)],
        sem=("arbitrary", "arbitrary"), comms=comms, aliases={15: 0})


def _other_chips(x, y):
    return [(1 - x, y), (x, 1 - y), (1 - x, 1 - y)]


def _sem_pair(n):
    return [pltpu.SemaphoreType.DMA((n,)), pltpu.SemaphoreType.DMA((n,))]


def _plan_pair_forward(bufs):
    n = len(bufs)

    def copies(outs, sems):
        x, y, c = _mesh_pos()
        sends, lands = [], []
        for a in range(n):
            for j, chip in enumerate(_other_chips(x, y)):
                k = 3 * a + j
                slot = outs[a].at[4 * chip[0] + 2 * chip[1] + c]
                sends.append(pltpu.make_async_remote_copy(
                    src_ref=slot, dst_ref=slot, send_sem=sems[0].at[k], recv_sem=sems[1].at[k],
                    device_id=(x, y, 1 - c), device_id_type=MESH))
                theirs = outs[a].at[4 * chip[0] + 2 * chip[1] + 1 - c]
                lands.append(pltpu.make_async_remote_copy(
                    src_ref=theirs, dst_ref=theirs, send_sem=sems[0].at[k], recv_sem=sems[1].at[k],
                    device_id=(x, y, 1 - c), device_id_type=MESH))
        return sends, lands

    def start(ins, outs, sems):
        for cp in copies(outs, sems)[0]:
            cp.start()

    def finish(ins, outs, sems):
        sends, lands = copies(outs, sems)
        for cp in lands:
            cp.wait_recv()
        for cp in sends:
            cp.wait_send()

    return _Comm(list(bufs), [_sds(b.shape, b.dtype) for b in bufs], _sem_pair(3 * n), start, finish,
                 aliases=[(a, a) for a in range(n)])


def _plan_pair(arrays, other_half):
    n = len(arrays)
    per = N_CHIPS if other_half == "chip_major" else 1

    def copies(ins, outs, sems):
        x, y, c = _mesh_pos()
        out = []
        for a in range(n):
            for k in range(per):
                if other_half == "chip_major":
                    src, dst = ins[a].at[k, 1 - c], outs[a].at[k]
                else:
                    src, dst = (ins[a].at[1 - c] if other_half else ins[a]), outs[a]
                out.append(pltpu.make_async_remote_copy(
                    src_ref=src, dst_ref=dst, send_sem=sems[0].at[per * a + k], recv_sem=sems[1].at[per * a + k],
                    device_id=(x, y, 1 - c), device_id_type=MESH))
        return out

    def start(ins, outs, sems):
        for cp in copies(ins, outs, sems):
            cp.start()

    def finish(ins, outs, sems):
        for cp in copies(ins, outs, sems):
            cp.wait()

    if other_half == "chip_major":
        shapes = [_sds((a.shape[0],) + a.shape[2:], a.dtype) for a in arrays]
    else:
        shapes = [_sds(a.shape[1:] if other_half else a.shape, a.dtype) for a in arrays]
    return _Comm(list(arrays), shapes, _sem_pair(per * n), start, finish)


def _plan_chip_exchange(arrays):
    n = len(arrays)

    def copies(ins, outs, sems):
        x, y, c = _mesh_pos()
        sends, lands = [], []
        for a in range(n):
            for j, chip in enumerate(_other_chips(x, y)):
                k = 3 * a + j
                sends.append(pltpu.make_async_remote_copy(
                    src_ref=ins[a].at[2 * chip[0] + chip[1]], dst_ref=outs[a].at[2 * x + y], send_sem=sems[0].at[k],
                    recv_sem=sems[1].at[k], device_id=(*chip, c), device_id_type=MESH))
                slot = outs[a].at[2 * chip[0] + chip[1]]
                lands.append(pltpu.make_async_remote_copy(
                    src_ref=slot, dst_ref=slot, send_sem=sems[0].at[k], recv_sem=sems[1].at[k],
                    device_id=(*chip, c), device_id_type=MESH))
        return sends, lands

    def start(ins, outs, sems):
        for cp in copies(ins, outs, sems)[0]:
            cp.start()

    def finish(ins, outs, sems):
        sends, lands = copies(ins, outs, sems)
        for cp in lands:
            cp.wait_recv()
        for cp in sends:
            cp.wait_send()

    return _Comm(list(arrays), [_sds(a.shape, a.dtype) for a in arrays], _sem_pair(3 * n), start, finish)


SEM_SPEC = pl.BlockSpec(memory_space=pltpu.SEMAPHORE)
N_OTHER = N_CHIPS - 1


def _exchange_copies(s_ref, land_ref, sems):
    x, y, c = _mesh_pos()
    return [pltpu.make_async_remote_copy(
        src_ref=s_ref.at[2 * chip[0] + chip[1]], dst_ref=land_ref.at[2 * x + y], send_sem=sems[j], recv_sem=sems[N_OTHER + j],
        device_id=(*chip, c), device_id_type=MESH) for j, chip in enumerate(_other_chips(x, y))]


def _exchange_start(s, name):
    def body(s_ref, land_ref, *outs):
        sems, token = outs[:2 * N_OTHER], outs[-1]
        for cp in _exchange_copies(s_ref, land_ref, sems):
            cp.start()
        token[...] = jnp.zeros_like(token)

    hbm = pltpu.HBM(s.shape, s.dtype)
    res = pl.pallas_call(
        body, name=name,
        out_shape=(pltpu.SemaphoreType.DMA(()),) * (2 * N_OTHER) + (hbm, hbm, _sds((SUBLANES, LANE), F32)),
        in_specs=(HBM_SPEC, HBM_SPEC),
        out_specs=(SEM_SPEC,) * (2 * N_OTHER) + (HBM_SPEC, HBM_SPEC, pl.BlockSpec(memory_space=pltpu.VMEM)),
        input_output_aliases={0: 2 * N_OTHER, 1: 2 * N_OTHER + 1},
        compiler_params=pltpu.CompilerParams(has_side_effects=pltpu.SideEffectType.DATAFLOW_SIDE_EFFECTING),
    )(pltpu.with_memory_space_constraint(s, pltpu.HBM), pltpu.with_memory_space_constraint(lax.empty(s.shape, s.dtype), pltpu.HBM))
    return res[:2 * N_OTHER], res[2 * N_OTHER], res[2 * N_OTHER + 1], res[-1]


def _exchange_wait(sems, s_thru, land_thru, afters, name):
    def body(s_ref, land_ref, *rest):
        for cp in _exchange_copies(s_ref, land_ref, rest[:2 * N_OTHER]):
            cp.wait_send()
            cp.wait_recv()

    hbm = pltpu.HBM(s_thru.shape, s_thru.dtype)
    return pl.pallas_call(
        body, name=name, out_shape=(hbm, hbm),
        in_specs=(HBM_SPEC, HBM_SPEC) + (SEM_SPEC,) * (2 * N_OTHER) + (pl.BlockSpec(memory_space=pl.ANY),) * len(afters),
        out_specs=(HBM_SPEC, HBM_SPEC), input_output_aliases={0: 0, 1: 1},
        compiler_params=pltpu.CompilerParams(has_side_effects=pltpu.SideEffectType.DATAFLOW_SIDE_EFFECTING),
    )(s_thru, land_thru, *sems, *afters)


def _gather_copies(block_ref, buf_ref, sems):
    x, y, c = _mesh_pos()
    return [pltpu.make_async_remote_copy(
        src_ref=block_ref, dst_ref=buf_ref.at[4 * x + 2 * y + c], send_sem=sems[j], recv_sem=sems[N_OTHER + j],
        device_id=(*chip, c), device_id_type=MESH) for j, chip in enumerate(_other_chips(x, y))]


def _gather_start(blocks, bufs, afters, name):
    n = len(blocks)
    per = 2 * N_OTHER

    def body(*refs):
        ins, outs = refs[:2 * n], refs[2 * n + len(afters):]
        for a in range(n):
            for cp in _gather_copies(ins[a], ins[n + a], outs[a * per:(a + 1) * per]):
                cp.start()
        outs[-1][...] = jnp.zeros_like(outs[-1])

    hbm = [pltpu.HBM(z.shape, z.dtype) for z in list(blocks) + list(bufs)]
    res = pl.pallas_call(
        body, name=name,
        out_shape=(pltpu.SemaphoreType.DMA(()),) * (n * per) + tuple(hbm) + (_sds((SUBLANES, LANE), F32),),
        in_specs=(HBM_SPEC,) * (2 * n) + (pl.BlockSpec(memory_space=pl.ANY),) * len(afters),
        out_specs=(SEM_SPEC,) * (n * per) + (HBM_SPEC,) * (2 * n) + (pl.BlockSpec(memory_space=pltpu.VMEM),),
        input_output_aliases={k: n * per + k for k in range(2 * n)},
        compiler_params=pltpu.CompilerParams(has_side_effects=pltpu.SideEffectType.DATAFLOW_SIDE_EFFECTING),
    )(*[pltpu.with_memory_space_constraint(z, pltpu.HBM) for z in list(blocks) + list(bufs)], *afters)
    parts = [(res[a * per:(a + 1) * per], res[n * per + a], res[n * per + n + a]) for a in range(n)]
    return parts, res[-1]


def _gather_wait(part, afters, name):
    sems, block, buf = part

    def body(block_ref, buf_ref, *rest):
        for cp in _gather_copies(block_ref, buf_ref, rest[:2 * N_OTHER]):
            cp.wait_send()
            cp.wait_recv()

    return pl.pallas_call(
        body, name=name, out_shape=(pltpu.HBM(block.shape, block.dtype), pltpu.HBM(buf.shape, buf.dtype)),
        in_specs=(HBM_SPEC, HBM_SPEC) + (SEM_SPEC,) * (2 * N_OTHER) + (pl.BlockSpec(memory_space=pl.ANY),) * len(afters),
        out_specs=(HBM_SPEC, HBM_SPEC), input_output_aliases={0: 0, 1: 1},
        compiler_params=pltpu.CompilerParams(has_side_effects=pltpu.SideEffectType.DATAFLOW_SIDE_EFFECTING),
    )(block, buf, *sems, *afters)[1]


def _comm_only(comms, name):
    return _call(lambda: None, (), name=name, grid=(), in_specs=[], out_specs=[], out_shape=[], sem=(), comms=comms)[1]


def _allgather8(arrays, name):
    return _comm_only([_plan_allgather8(arrays)], name)[0]


def _plan_allgather8(arrays):
    n = len(arrays)

    def parts(ins, outs, sems):
        send_sems, recv_sems, local_sems = sems
        x, y, c = _mesh_pos()
        me, sibling = (x, y, c), (x, y, 1 - c)
        chips = _other_chips(x, y)

        def copy(a, k, block, to, src=None):
            dst = outs[a].at[4 * block[0] + 2 * block[1] + block[2]]
            return pltpu.make_async_remote_copy(
                src_ref=dst if src is None else src, dst_ref=dst, send_sem=send_sems.at[7 * a + k],
                recv_sem=recv_sems.at[7 * a + k], device_id=to, device_id_type=MESH)

        mine = [pltpu.make_async_copy(ins[a], outs[a].at[4 * x + 2 * y + c], local_sems.at[a]) for a in range(n)]
        first = []
        for a in range(n):
            first.append(copy(a, 0, me, sibling, src=ins[a]))
            first += [copy(a, 1 + j, me, (*chip, c), src=ins[a]) for j, chip in enumerate(chips)]
        return copy, mine, first, me, sibling, chips, c

    def start(ins, outs, sems):
        _, mine, first, *_ = parts(ins, outs, sems)
        for cp in mine + first:
            cp.start()

    def finish(ins, outs, sems):
        copy, mine, first, me, sibling, chips, c = parts(ins, outs, sems)
        passed = []
        for j, chip in enumerate(chips):
            for a in range(n):
                copy(a, 1 + j, (*chip, c), me).wait_recv()
                fwd = copy(a, 4 + j, (*chip, c), sibling)
                fwd.start()
                passed.append(fwd)
        for a in range(n):
            copy(a, 0, sibling, me).wait_recv()
            for j, chip in enumerate(chips):
                copy(a, 4 + j, (*chip, 1 - c), me).wait_recv()
        for cp in first + passed:
            cp.wait_send()
        for cp in mine:
            cp.wait()

    sems = [pltpu.SemaphoreType.DMA((7 * n,)), pltpu.SemaphoreType.DMA((7 * n,)), pltpu.SemaphoreType.DMA((n,))]
    return _Comm(list(arrays), [_sds((N_DEV,) + a.shape, a.dtype) for a in arrays], sems, start, finish)


def _pair_sum(g, q, core, name, chip_major=False):
    rows, cols = g.shape[2:]
    tr = _row_tile(rows)

    def body(core_ref, g_ref, q_ref, o_ref):
        o_ref[...] = (g_ref[...] + q_ref[...]).astype(BF16)

    blk = pl.BlockSpec((None, tr, cols), lambda k, i, core_ref: (k, i, 0))
    if chip_major:
        own = pl.BlockSpec((None, None, tr, cols), lambda k, i, core_ref: (k, core_ref[0], i, 0))
    else:
        own = pl.BlockSpec((None, None, tr, cols), lambda k, i, core_ref: (core_ref[0], k, i, 0))
    return pl.pallas_call(
        body, name=name,
        grid_spec=pltpu.PrefetchScalarGridSpec(num_scalar_prefetch=1, grid=(N_CHIPS, rows // tr), in_specs=[own, blk], out_specs=blk),
        out_shape=_sds((N_CHIPS, rows, cols), BF16), compiler_params=_params("parallel", "parallel"),
    )(core, g, q)


def _sum_chips(own, landed, chip, name):
    _, rows, cols = own.shape
    tr = _row_tile(rows)

    def body(chip_ref, own_ref, a_ref, b_ref, c_ref, o_ref):
        acc = own_ref[...].astype(F32) + a_ref[...].astype(F32)
        o_ref[...] = (acc + b_ref[...].astype(F32)) + c_ref[...].astype(F32)

    blk = lambda flip: pl.BlockSpec((None, tr, cols), lambda i, chip_ref: (jnp.bitwise_xor(chip_ref[0], flip), i, 0))
    return pl.pallas_call(
        body, name=name,
        grid_spec=pltpu.PrefetchScalarGridSpec(num_scalar_prefetch=1, grid=(rows // tr,), in_specs=[blk(0), blk(1), blk(2), blk(3)],
                                               out_specs=pl.BlockSpec((tr, cols), lambda i, chip_ref: (i, 0))),
        out_shape=_sds((rows, cols), F32), compiler_params=_params("parallel"),
    )(chip, own, landed, landed, landed)


SUBLANES = 8


def _tile_rows(n_elems):
    return -(-n_elems // (SUBLANES * LANE)) * SUBLANES


SMALL_ITEMS = (("b_ada", N_MOD * D_MODEL), ("pre_w_mix", D_MODEL), ("post_w_mix", D_MODEL), ("pre_w_mlp", D_MODEL),
               ("post_w_mlp", D_MODEL), ("attn_out_w", ATT_WIDTH), ("hg_norm_w", HG_HEAD_DIM), ("attn_sinks", ATT_Q_HEADS),
               ("lb_0", HG_WIDTH), ("lb_1", HG_WIDTH))
SMALL_AT = {}
for _name, _size in SMALL_ITEMS:
    SMALL_AT[_name] = (sum(r for _, r in SMALL_AT.values()), _tile_rows(_size))
SMALL_ROWS = sum(r for _, r in SMALL_AT.values())
MOD_ROWS = SMALL_AT["b_ada"][1]
PLAIN_ROWS = SMALL_AT["lb_0"][0] - MOD_ROWS
LB_ROWS = SMALL_AT["lb_0"][1]


def _rows(a, nrows=None):
    flat = a.reshape(-1)
    nrows = _tile_rows(flat.shape[0]) if nrows is None else nrows
    return jnp.pad(flat, (0, nrows * LANE - flat.shape[0])).reshape(nrows, LANE)


def _pack_small(vals):
    vals = dict(vals, lb_0=vals["lb_table"][0], lb_1=vals["lb_table"][1])
    return jnp.concatenate([_rows(vals[name], SMALL_AT[name][1]) for name, _ in SMALL_ITEMS], axis=0)


def _unpack_small(p):
    def item(name, shape):
        first = SMALL_AT[name][0]
        size = shape[0] * shape[1]
        return p[first:first + SMALL_AT[name][1]].reshape(-1)[:size].reshape(shape)

    out = {name: item(name, (1, size)) for name, size in SMALL_ITEMS if not name.startswith("lb_")}
    out["lb_table"] = jnp.concatenate([item("lb_0", (1, HG_WIDTH)), item("lb_1", (1, HG_WIDTH))], axis=0)
    return out


def _pack_partials(dmod, plain, d_lb, loss_row):
    return jnp.concatenate([_rows(dmod, dmod.shape[0] * MOD_ROWS)] + [_rows(g) for g in plain] + [_rows(d_lb), _rows(loss_row)], axis=0)


def _small_update(packs, w, m, v, n_seq):
    mod_end = n_seq * MOD_ROWS
    lb_at = mod_end + PLAIN_ROWS
    t0, t1 = SMALL_AT["lb_0"][0], SMALL_AT["lb_1"][0]

    def body(p_ref, w_ref, m_ref, v_ref, g_ref, dl_ref, nm_ref, nv_ref, loss_ref):
        tot = p_ref[0]
        for d in range(1, N_DEV):
            tot = tot + p_ref[d]
        wv = w_ref[...]
        p1 = _sigmoid(wv[t1:t1 + LB_ROWS] - wv[t0:t0 + LB_ROWS])
        s = tot[lb_at:lb_at + LB_ROWS] * p1 * (1.0 - p1)
        g_bias = tot[0:MOD_ROWS]
        for q in range(1, n_seq):
            g_bias = g_bias + tot[q * MOD_ROWS:(q + 1) * MOD_ROWS]
        g = jnp.concatenate([g_bias, tot[mod_end:lb_at], -s, s], axis=0)
        g_ref[...] = g
        dl_ref[...], nm_ref[...], nv_ref[...] = _adamw_math(g, wv, m_ref[...], v_ref[...])
        loss_ref[...] = tot[lb_at + LB_ROWS:lb_at + LB_ROWS + SUBLANES]

    shp = _sds((SMALL_ROWS, LANE), F32)
    return pl.pallas_call(body, name="small_update", out_shape=[shp] * 4 + [_sds((SUBLANES, LANE), F32)],
                          compiler_params=_params())(packs, w, m, v)


def kernel(x, c, w_ada, b_ada, pre_w_mix, w_in, attn_sinks, attn_out_w, lb_table, hg_norm_w, w_out, post_w_mix, pre_w_mlp, w_up, w_down, post_w_mlp, loss_target, m_w_ada, m_b_ada, m_pre_w_mix, m_w_in, m_attn_sinks, m_attn_out_w, m_lb_table, m_hg_norm_w, m_w_out, m_post_w_mix, m_pre_w_mlp, m_w_up, m_w_down, m_post_w_mlp, v_w_ada, v_b_ada, v_pre_w_mix, v_w_in, v_attn_sinks, v_attn_out_w, v_lb_table, v_hg_norm_w, v_w_out, v_post_w_mix, v_pre_w_mlp, v_w_up, v_w_down, v_post_w_mlp):
    xi, yi, ci = _mesh_pos()
    chip = 2 * xi + yi
    dev = 2 * chip + ci
    bsz, seq, _ = x.shape
    ntok = bsz * seq
    ada_cols = w_ada.shape[2]
    core = jnp.reshape(ci, (1,)).astype(jnp.int32)
    chip_idx = jnp.reshape(chip, (1,)).astype(jnp.int32)
    flat = lambda a: a.reshape(ntok, a.shape[-1])
    unflat = lambda a: a.reshape(bsz, seq, a.shape[-1])
    tables = _rope_tables(seq)
    biases, chunk_masks = _band_biases(), _block_masks()

    def row_half(w):
        rows = w.shape[1] // 2
        return lax.dynamic_slice_in_dim(w[0], ci * rows, rows, axis=0).astype(BF16)

    def gather_buffer(w):
        rows, cols = w.shape[1] // 2, w.shape[2]
        own = w[0].astype(BF16).reshape(2, rows, cols)
        return lax.dynamic_update_slice(lax.empty((N_DEV, rows, cols), BF16), own, (2 * chip, 0, 0))

    w_in_t, m_in_t, v_in_t = [jnp.transpose(a[0])[None] for a in (w_in, m_w_in, v_w_in)]
    c_g, in_g = _allgather8([c, row_half(w_in_t)], "gather_first")
    c_all = c_g.reshape(N_DEV * bsz, D_MODEL)
    w_in_full = in_g.reshape(IN_COLS, D_MODEL)

    b_cols = lax.dynamic_slice_in_dim(b_ada, chip * ada_cols, ada_cols, axis=1)
    mod_part = _ada_fwd(c_all, w_ada[0], b_cols)
    half_rows = mod_part.shape[0] // 2
    (mod_g,) = _allgather8([lax.dynamic_slice_in_dim(mod_part, ci * half_rows, half_rows, axis=0)], "gather_mod")
    mod_all = mod_g.reshape(N_CHIPS, 2, half_rows, ada_cols).transpose(1, 2, 0, 3).reshape(N_DEV * bsz, N_MOD * D_MODEL)
    mod = lax.dynamic_slice_in_dim(mod_all, dev * bsz, bsz, axis=0)
    sh1, sc1, g1, sh2, sc2, g2 = [mod[:, i * D_MODEL:(i + 1) * D_MODEL].reshape(bsz, 1, D_MODEL) for i in range(N_MOD)]

    weights = (w_out, w_up, w_down)
    (out_part, up_part, down_part), started = _gather_start(
        [row_half(w) for w in weights], [gather_buffer(w) for w in weights], [mod_g], "gather_weights_start")

    h1, proj, qh, kh, vh = _in_proj_fused(x, pre_w_mix, sc1 + started[0:1, 0:1], sh1, w_in_full, tables)
    out_g = _gather_wait(out_part, [proj], "gather_out_wait")
    (attn_raw, cat, lse), ((out_g,),) = _attn_fwd(qh, kh, vh, attn_sinks, attn_out_w, biases, comms=[_plan_pair_forward([out_g])])
    up_g = _gather_wait(up_part, [attn_raw], "gather_up_wait")
    (o_raw, cat, states), ((up_g,),) = _hgrn_fwd(proj, lb_table, hg_norm_w, cat, chunk_masks, comms=[_plan_pair_forward([up_g])])
    down_g = _gather_wait(down_part, [o_raw], "gather_down_wait")
    w_out_full = out_g.reshape(D_MODEL, D_MODEL)
    w_up4 = up_g.reshape(N_CHIPS, D_MODEL, D_MODEL)
    mix, x1, h2 = _out_proj_fused(cat, w_out_full, x, post_w_mix, g1, pre_w_mlp, sc2, sh2)
    big_tm = min(ntok, 2048)
    up_spec = pl.BlockSpec((None, D_MODEL, D_MODEL), lambda i, j: (j, 0, 0))
    r, ((down_g,),) = _mm(flat(h2), w_up4, name="up_proj", out_dtype=BF16, tm=big_tm, tn=D_MODEL, n_out=D_FF, b_spec=up_spec,
                          epi=lambda acc: jnp.maximum(acc, 0.0), comms=[_plan_pair_forward([down_g])])
    w_down_full = down_g.reshape(D_FF, D_MODEL)
    square = lambda t: t * t
    loss_row, dy, dd, dg2, d_post_mlp = _down_proj_fused(unflat(r), w_down_full, x1, post_w_mlp, g2, loss_target)

    dpre = _mm(flat(dd), w_down_full, name="down_bwd", out_dtype=BF16, trans_b=True, tm=big_tm, tn=D_MODEL, extra=(r,),
               epi=lambda acc, rt: acc * (2.0 * rt.astype(F32)))
    half_rows = D_MODEL // 2
    g_down = _mm_tn(r, flat(dd), name="down_wgrad", tk=half_rows, tn=D_MODEL, a_fn=square,
                    out_shape=_sds((2, N_CHIPS, half_rows, D_MODEL), F32),
                    out_spec=pl.BlockSpec((None, None, half_rows, D_MODEL), lambda i, j: (i % 2, i // 2, 0, 0)))
    (dx1, dmix, dsc2, dsh2, dg1, d_pre_mlp, d_post_mix), ((q_down,),) = _up_bwd_fused(
        unflat(dpre), w_up4, dy, x1, mix, pre_w_mlp, sc2, post_w_mix, g1, comms=[_plan_pair([g_down], True)])
    g_up = _mm_tn(flat(h2), dpre, name="up_wgrad", tk=D_MODEL, tn=half_rows,
                  out_shape=_sds((2, N_CHIPS, half_rows, D_MODEL), F32),
                  out_spec=pl.BlockSpec((2, None, half_rows, half_rows), lambda i, j: (0, j // 2, 0, j % 2)))
    s_down = _pair_sum(g_down, q_down, core, "pair_sum_down")

    dcat, ((q_up,),) = _mm(flat(dmix), w_out_full, name="out_bwd", out_dtype=F32, trans_b=True, comms=[_plan_pair([g_up], True)])
    dcat = unflat(dcat)
    s_up = _pair_sum(g_up, q_up, core, "pair_sum_up")
    out_rows = D_MODEL // N_CHIPS
    g_out = _mm_tn(flat(cat), flat(dmix), name="out_wgrad", tk=2 * out_rows, tn=half_rows,
                   out_shape=_sds((2, N_CHIPS, out_rows, half_rows), F32),
                   out_spec=pl.BlockSpec((None, 2, out_rows, half_rows), lambda i, j: (j, i, 0, 0)))
    (dproj_rec, d_lb, d_hg_norm), ((x_down,), (q_out,)) = _hgrn_bwd(
        dcat, proj, o_raw, states, lb_table, hg_norm_w, chunk_masks, comms=[_plan_chip_exchange([s_down]), _plan_pair([g_out], True)])
    half_down = _sum_chips(s_down, x_down, chip_idx, "sum_chips_down")
    s_out = _pair_sum(g_out, q_out, core, "pair_sum_out")
    (dproj, d_attn_out, d_sinks), ((their_down,), (x_up, x_out)) = _attn_bwd(
        dcat, attn_raw, attn_out_w, qh, kh, vh, lse, attn_sinks, tables, [bias.T for bias in biases], dproj_rec,
        comms=[_plan_pair([half_down], False), _plan_chip_exchange([s_up, s_out])])
    half_up = _sum_chips(s_up, x_up, chip_idx, "sum_chips_up")
    half_out = _sum_chips(s_out, x_out, chip_idx, "sum_chips_out")
    dproj = flat(dproj)
    in_rows = IN_COLS // N_CHIPS // 2
    g_in = _mm_tn(dproj, flat(h1), name="in_wgrad", tk=2 * LANE, tn=D_MODEL).reshape(N_CHIPS, 2, in_rows, D_MODEL)
    dh1, ((q_in,), (their_up, their_out)) = _mm(
        dproj, w_in_full, name="in_bwd", out_dtype=F32,
        comms=[_plan_pair([g_in], "chip_major"), _plan_pair([half_up, half_out], False)])
    s_in = _pair_sum(g_in, q_in, core, "pair_sum_in", chip_major=True)
    in_sems, s_in, in_landing, started = _exchange_start(s_in, "exchange_in_start")
    grad_x, dsc1, dsh1, d_pre_mix = _norm1_bwd(unflat(dh1), dx1, x, pre_w_mix + started[0:1, 0:1], sc1)

    dmod = jnp.concatenate([dsh1, dsc1, dg1, dsh2, dsc2, dg2], axis=-1).reshape(bsz, N_MOD * D_MODEL)
    pack = _pack_partials(dmod, [d_pre_mix, d_post_mix, d_pre_mlp, d_post_mlp, d_attn_out, d_hg_norm, d_sinks], d_lb, loss_row)
    ((packs,),) = _comm_only([_plan_allgather8([pack])], "gather_small")
    w_small = dict(b_ada=b_ada, pre_w_mix=pre_w_mix, post_w_mix=post_w_mix, pre_w_mlp=pre_w_mlp, post_w_mlp=post_w_mlp,
                   attn_out_w=attn_out_w, hg_norm_w=hg_norm_w, attn_sinks=attn_sinks, lb_table=lb_table)
    m_small = dict(b_ada=m_b_ada, pre_w_mix=m_pre_w_mix, post_w_mix=m_post_w_mix, pre_w_mlp=m_pre_w_mlp, post_w_mlp=m_post_w_mlp,
                   attn_out_w=m_attn_out_w, hg_norm_w=m_hg_norm_w, attn_sinks=m_attn_sinks, lb_table=m_lb_table)
    v_small = dict(b_ada=v_b_ada, pre_w_mix=v_pre_w_mix, post_w_mix=v_post_w_mix, pre_w_mlp=v_pre_w_mlp, post_w_mlp=v_post_w_mlp,
                   attn_out_w=v_attn_out_w, hg_norm_w=v_hg_norm_w, attn_sinks=v_attn_sinks, lb_table=v_lb_table)
    *small_packed, loss_rows = _small_update(packs, _pack_small(w_small), _pack_small(m_small), _pack_small(v_small), bsz)
    small_out = [_unpack_small(p) for p in small_packed]
    loss = loss_rows[0, 0]

    dmod_all = packs[:, :bsz * MOD_ROWS, :].reshape(N_DEV * bsz, N_MOD * D_MODEL)
    dmod_cols = lax.dynamic_slice_in_dim(dmod_all, chip * ada_cols, ada_cols, axis=1)
    ada_out = _ada_bwd_adamw(c_all, dmod_cols, w_ada[0], m_w_ada[0], v_w_ada[0])

    s_in, x_in = _exchange_wait(in_sems, s_in, in_landing, [grad_x, ada_out[0]], "exchange_in_wait")
    half_in = _sum_chips(s_in, x_in, chip_idx, "sum_chips_in")
    ((their_in,),) = _comm_only([_plan_pair([half_in], False)], "pair_swap_in")
    big = dict(
        w_in=tuple(jnp.transpose(a) for a in _adamw_halves(half_in, their_in, core, w_in_t[0], m_in_t[0], v_in_t[0], axis=0,
                                                           name="adamw_in")),
        w_up=tuple(_adamw_halves(half_up, their_up, core, w_up[0], m_w_up[0], v_w_up[0], axis=0, name="adamw_up")),
        w_out=tuple(_adamw_halves(half_out, their_out, core, w_out[0], m_w_out[0], v_w_out[0], axis=1, name="adamw_out")),
        w_down=tuple(_adamw_halves(half_down, their_down, core, w_down[0], m_w_down[0], v_w_down[0], axis=0, name="adamw_down")),
        w_ada=tuple(ada_out),
    )
    order = ("w_ada", "b_ada", "pre_w_mix", "w_in", "attn_sinks", "attn_out_w", "lb_table", "hg_norm_w", "w_out", "post_w_mix",
             "pre_w_mlp", "w_up", "w_down", "post_w_mlp")
    outs = [loss, grad_x]
    for kind in range(4):
        for nm in order:
            outs.append(big[nm][kind][None] if nm in big else small_out[kind][nm])
    return tuple(outs)
```

```python
import jax
import jax.numpy as jnp
from jax import lax
from jax.experimental import pallas as pl
from jax.experimental.pallas import tpu as pltpu

F32 = jnp.float32
BF16 = jnp.bfloat16

D_MODEL = 1024
ATT_WIDTH = 512
ATT_HEAD_DIM = 64
ATT_Q_HEADS = 8
ATT_KV_HEADS = 2
ATT_GROUP = ATT_Q_HEADS // ATT_KV_HEADS
ATT_KV_COLS = ATT_KV_HEADS * ATT_HEAD_DIM
WINDOW = 128
ROPE_DIM = 16
ROPE_THETA = 500000.0
HG_WIDTH = 512
MIX_WIDTH = ATT_WIDTH + HG_WIDTH
HG_HEAD_DIM = 128
HG_HEADS = 4
HG_CHUNK = 32
IN_COLS = ATT_WIDTH + 2 * ATT_KV_COLS + 4 * HG_WIDTH
ATT_COLS = ATT_WIDTH + 2 * ATT_KV_COLS
D_FF = 4 * D_MODEL
N_MOD = 6
EPS = 1e-6
ATT_SCALE = ATT_HEAD_DIM ** -0.5

ADAM_LR = 0.001
ADAM_B1 = 0.9
ADAM_B2 = 0.999
ADAM_EPS = 1e-08
ADAM_WD = 0.01
ADAM_STEP = 10

N_CHIPS = 4
N_DEV = 8
LANE = 128
VMEM_LIMIT = 48 * 1024 * 1024
VMEM_LIMIT_BIG = 58 * 1024 * 1024
MESH = pl.DeviceIdType.MESH

NT_DIMS = (((1,), (1,)), ((), ()))
TN_DIMS = (((0,), (0,)), ((), ()))


def _sds(shape, dtype):
    return jax.ShapeDtypeStruct(tuple(shape), dtype)


def _params(*sem, vmem_limit=None):
    return pltpu.CompilerParams(dimension_semantics=sem, vmem_limit_bytes=VMEM_LIMIT if vmem_limit is None else vmem_limit)


def _sigmoid(x):
    return 1.0 / (1.0 + jnp.exp(-x))


def _dot(a, b, dims=None):
    a, b = a.astype(BF16), b.astype(BF16)
    if dims is None:
        return jnp.dot(a, b, preferred_element_type=F32)
    return lax.dot_general(a, b, dims, preferred_element_type=F32)


def _rms_fwd(x, w):
    rstd = lax.rsqrt(jnp.mean(x * x, axis=-1, keepdims=True) + EPS)
    xh = x * rstd
    return xh * w, xh, rstd


def _rms_bwd(dy, xh, rstd, w):
    dxh = dy * w
    dx = rstd * (dxh - xh * jnp.mean(dxh * xh, axis=-1, keepdims=True))
    return dx, dy * xh


def _colsum(x):
    return jnp.sum(x, axis=0, keepdims=True)


def _rms_hat(x):
    rstd = lax.rsqrt(jnp.mean(x * x, axis=-1, keepdims=True) + EPS)
    return x * rstd, rstd


def _rms_bwd_gain(dy, gain, xh, rstd):
    dxh = dy * gain
    dx = rstd * (dxh - xh * jnp.mean(dxh * xh, axis=-1, keepdims=True))
    return dx, _colsum(dy * xh)


def _row_tile(rows, cap=256):
    return max(t for t in range(16, cap + 1, 16) if rows % t == 0)


HBM_SPEC = pl.BlockSpec(memory_space=pltpu.HBM)


def _mesh_pos():
    return lax.axis_index("x"), lax.axis_index("y"), lax.axis_index("c")


class _Comm:
    def __init__(self, ins, outs, sems, start, finish, aliases=()):
        self.ins, self.outs, self.sems = list(ins), list(outs), list(sems)
        self.start, self.finish, self.aliases = start, finish, tuple(aliases)


def _call(body, args, *, name, grid, in_specs, out_specs, out_shape, sem, scratch_shapes=(), comms=(), aliases=None,
          vmem_limit=None):
    scratch_shapes = list(scratch_shapes)
    if not comms:
        return pl.pallas_call(body, name=name, grid=grid, in_specs=in_specs, out_specs=out_specs, out_shape=out_shape,
                              input_output_aliases=dict(aliases or {}), scratch_shapes=scratch_shapes,
                              compiler_params=_params(*sem, vmem_limit=vmem_limit))(*args)
    single = not isinstance(out_shape, (list, tuple))
    out_specs_l = [out_specs] if single else list(out_specs)
    out_shape_l = [out_shape] if single else list(out_shape)
    n_in, n_out, n_scr = len(in_specs), len(out_shape_l), len(scratch_shapes)
    n_ci = [len(cm.ins) for cm in comms]
    n_co = [len(cm.outs) for cm in comms]
    n_cs = [len(cm.sems) for cm in comms]
    aliases = dict(aliases or {})
    for k, cm in enumerate(comms):
        for i, o in cm.aliases:
            aliases[n_in + sum(n_ci[:k]) + i] = n_out + sum(n_co[:k]) + o

    def fused(*refs):
        pos = [0]

        def take(n):
            part = refs[pos[0]:pos[0] + n]
            pos[0] += n
            return part

        ins = take(n_in)
        c_ins = [take(n) for n in n_ci]
        outs = take(n_out)
        c_outs = [take(n) for n in n_co]
        scr = take(n_scr)
        c_sems = [take(n) for n in n_cs]
        first, last = True, True
        for d, size in enumerate(grid):
            first = jnp.logical_and(first, pl.program_id(d) == 0)
            last = jnp.logical_and(last, pl.program_id(d) == size - 1)

        def run(which):
            for cm, ci, co, cs in zip(comms, c_ins, c_outs, c_sems):
                getattr(cm, which)(ci, co, cs)

        if grid:
            pl.when(first)(lambda: run("start"))
        else:
            run("start")
        body(*ins, *outs, *scr)
        if grid:
            pl.when(last)(lambda: run("finish"))
        else:
            run("finish")

    res = pl.pallas_call(
        fused, name=name, grid=grid, in_specs=list(in_specs) + [HBM_SPEC] * sum(n_ci),
        out_specs=out_specs_l + [HBM_SPEC] * sum(n_co), out_shape=out_shape_l + [s for cm in comms for s in cm.outs],
        input_output_aliases=aliases, scratch_shapes=scratch_shapes + [s for cm in comms for s in cm.sems],
        compiler_params=_params(*["arbitrary"] * len(grid), vmem_limit=vmem_limit),
    )(*args, *[a for cm in comms for a in cm.ins])
    main = res[:n_out]
    extra, at = [], n_out
    for n in n_co:
        extra.append(list(res[at:at + n]))
        at += n
    return (main[0] if single else list(main)), extra


def _mm(a, b, *, name, out_dtype, trans_b=False, tm=512, tn=None, extra=(), epi=None, b_spec=None, n_out=None, comms=()):
    m_total, k_total = a.shape
    if n_out is None:
        n_out = b.shape[0] if trans_b else b.shape[1]
    tn = n_out if tn is None else tn
    grid = (m_total // tm, n_out // tn)
    dims = NT_DIMS if trans_b else None

    def body(*refs):
        a_ref, b_ref = refs[0], refs[1]
        extra_refs = refs[2:2 + len(extra)]
        o_ref = refs[2 + len(extra)]
        acc = _dot(a_ref[...], b_ref[...], dims)
        if epi is not None:
            acc = epi(acc, *[r[...] for r in extra_refs])
        o_ref[...] = acc.astype(out_dtype)

    if b_spec is None:
        if trans_b:
            b_spec = pl.BlockSpec((tn, k_total), lambda i, j: (j, 0))
        else:
            b_spec = pl.BlockSpec((k_total, tn), lambda i, j: (0, j))
    in_specs = [pl.BlockSpec((tm, k_total), lambda i, j: (i, 0)), b_spec]
    in_specs += [pl.BlockSpec((tm, tn), lambda i, j: (i, j)) for _ in extra]
    return _call(
        body, (a, b, *extra), name=name, grid=grid, in_specs=in_specs,
        out_specs=pl.BlockSpec((tm, tn), lambda i, j: (i, j)),
        out_shape=_sds((m_total, n_out), out_dtype),
        sem=("parallel", "parallel"), comms=comms)


def _mm_tn(a, b, *, name, tk, tn, a_fn=None, out_shape=None, out_spec=None):
    m_total, k_total = a.shape
    n_total = b.shape[1]
    grid = (k_total // tk, n_total // tn)

    def body(a_ref, b_ref, o_ref):
        av = a_ref[...]
        part = _dot(av if a_fn is None else a_fn(av), b_ref[...], TN_DIMS)
        o_ref[...] = part.reshape(o_ref.shape)

    if out_shape is None:
        out_shape = _sds((k_total, n_total), F32)
        out_spec = pl.BlockSpec((tk, tn), lambda i, j: (i, j))
    return pl.pallas_call(
        body, name=name, grid=grid,
        in_specs=[pl.BlockSpec((m_total, tk), lambda i, j: (0, i)), pl.BlockSpec((m_total, tn), lambda i, j: (0, j))],
        out_specs=out_spec, out_shape=out_shape,
        compiler_params=_params("parallel", "parallel"),
    )(a, b)


def _ada_fwd(c_all, w_shard, b_shard):
    nb, ncol = c_all.shape[0], w_shard.shape[1]
    tn = 512

    def body(c_ref, w_ref, b_ref, o_ref):
        c = c_ref[...]
        o_ref[...] = _dot(c * _sigmoid(c), w_ref[...]) + b_ref[...]

    return pl.pallas_call(
        body, name="ada_fwd", grid=(ncol // tn,),
        in_specs=[pl.BlockSpec((nb, D_MODEL), lambda j: (0, 0)), pl.BlockSpec((D_MODEL, tn), lambda j: (0, j)),
                  pl.BlockSpec((1, tn), lambda j: (0, j))],
        out_specs=pl.BlockSpec((nb, tn), lambda j: (0, j)), out_shape=_sds((nb, ncol), F32),
        compiler_params=_params("parallel"),
    )(c_all, w_shard, b_shard)


def _adamw_math(g, w, m, v):
    m = ADAM_B1 * m + (1.0 - ADAM_B1) * g
    v = ADAM_B2 * v + (1.0 - ADAM_B2) * (g * g)
    m_hat = m / (1.0 - ADAM_B1 ** ADAM_STEP)
    v_hat = v / (1.0 - ADAM_B2 ** ADAM_STEP)
    delta = -ADAM_LR * (m_hat / (jnp.sqrt(v_hat) + ADAM_EPS) + ADAM_WD * w)
    return delta, m, v


def _ada_bwd_adamw(c_all, dmod_cols, w, m, v):
    nb, ncol = dmod_cols.shape
    tn = 256

    def body(c_ref, d_ref, w_ref, m_ref, v_ref, g_ref, dl_ref, nm_ref, nv_ref):
        c = c_ref[...]
        g = _dot(c * _sigmoid(c), d_ref[...], TN_DIMS)
        g_ref[...] = g
        dl_ref[...], nm_ref[...], nv_ref[...] = _adamw_math(g, w_ref[...], m_ref[...], v_ref[...])

    col = pl.BlockSpec((D_MODEL, tn), lambda j: (0, j))
    shp = _sds((D_MODEL, ncol), F32)
    return pl.pallas_call(
        body, name="ada_bwd_adamw", grid=(ncol // tn,),
        in_specs=[pl.BlockSpec((nb, D_MODEL), lambda j: (0, 0)), pl.BlockSpec((nb, tn), lambda j: (0, j)), col, col, col],
        out_specs=[col, col, col, col], out_shape=[shp, shp, shp, shp],
        compiler_params=_params("parallel"),
    )(c_all, dmod_cols, w, m, v)


def _adamw_halves(own, theirs, core, w, m, v, *, axis, name):
    r2, c2 = own.shape
    tr = _row_tile(r2)
    nt = r2 // tr

    def body(core_ref, own_ref, their_ref, w_ref, m_ref, v_ref, g_ref, dl_ref, nm_ref, nv_ref):
        g = jnp.where(pl.program_id(0) == core_ref[0], own_ref[...], their_ref[...])
        g_ref[...] = g
        dl_ref[...], nm_ref[...], nv_ref[...] = _adamw_math(g, w_ref[...], m_ref[...], v_ref[...])

    if axis == 0:
        full = pl.BlockSpec((tr, c2), lambda h, i, core_ref: (h * nt + i, 0))
    else:
        full = pl.BlockSpec((tr, c2), lambda h, i, core_ref: (i, h))
    half = pl.BlockSpec((tr, c2), lambda h, i, core_ref: (i, 0))
    shp = _sds(w.shape, F32)
    return pl.pallas_call(
        body, name=name,
        grid_spec=pltpu.PrefetchScalarGridSpec(num_scalar_prefetch=1, grid=(2, nt), in_specs=[half, half, full, full, full],
                                               out_specs=[full] * 4),
        out_shape=[shp] * 4, compiler_params=_params("parallel", "parallel"),
    )(core, own, theirs, w, m, v)


def _tok_spec(tm, width=D_MODEL):
    return pl.BlockSpec((None, tm, width), lambda b, i: (b, i, 0))


def _row_spec(width=D_MODEL):
    return pl.BlockSpec((None, 1, width), lambda b, i: (b, 0, 0))


def _vec_spec(width=D_MODEL):
    return pl.BlockSpec((1, width), lambda b, i: (0, 0))


class _RowsOf:
    def __init__(self, ref, first, count):
        self.ref, self.rows = ref, slice(first, first + count)

    def __getitem__(self, idx):
        return self.ref[self.rows, :]

    def __setitem__(self, idx, value):
        self.ref[self.rows, :] = value


def _mm_rows(a, b, *, name, tm, extra, extra_specs, out_specs, out_shape, epi, pro=None, trans_b=False, b_chunks=1, comms=(),
             parts=1, zero_per_seq=(), zero_once=(), vmem_limit=None):
    bsz, seq, k_total = a.shape
    kc = k_total // b_chunks
    dims = NT_DIMS if trans_b else None
    rows = tm // parts

    def body(*refs):
        a_ref, b_ref = refs[0], refs[1]
        ex, outs = refs[2:2 + len(extra)], refs[2 + len(extra):]
        if zero_per_seq:
            @pl.when(pl.program_id(1) == 0)
            def _():
                for k in zero_per_seq:
                    outs[k][...] = jnp.zeros_like(outs[k])
        if zero_once:
            @pl.when(jnp.logical_and(pl.program_id(0) == 0, pl.program_id(1) == 0))
            def _():
                for k in zero_once:
                    outs[k][...] = jnp.zeros_like(outs[k])

        def part_of(ref, p):
            tiled = len(ref.shape) == 2 and ref.shape[0] == tm
            return _RowsOf(ref, p * rows, rows) if tiled and parts > 1 else ref

        accs = []
        for p in range(parts):
            a_p, ex_p, outs_p = part_of(a_ref, p), [part_of(r, p) for r in ex], [part_of(r, p) for r in outs]
            if b_chunks == 1:
                accs.append(_dot(a_p[...] if pro is None else pro(a_p, ex_p, outs_p), b_ref[...], dims))
            else:
                acc = _dot(a_p[...][:, 0:kc], b_ref[0], NT_DIMS)
                for k in range(1, b_chunks):
                    acc = acc + _dot(a_p[...][:, k * kc:(k + 1) * kc], b_ref[k], NT_DIMS)
                accs.append(acc)
        for p in range(parts):
            epi(accs[p], [part_of(r, p) for r in ex], [part_of(r, p) for r in outs])

    b_spec = pl.BlockSpec(b.shape, lambda bb, i: (0,) * b.ndim)
    return _call(
        body, (a, b, *extra), name=name, grid=(bsz, seq // tm), in_specs=[_tok_spec(tm, k_total), b_spec, *extra_specs],
        out_specs=out_specs, out_shape=out_shape, sem=("arbitrary", "arbitrary"), comms=comms, vmem_limit=vmem_limit)


def _in_proj_fused(x, w, sc, sh, w_in_t, tables, comms=()):
    tm = 512
    bsz, seq, _ = x.shape
    half = ROPE_DIM // 2
    heads_per_slab = LANE // ATT_HEAD_DIM

    def pro(x_ref, ex, outs):
        y, _, _ = _rms_fwd(x_ref[...], ex[0][...])
        h = (y * (1.0 + ex[1][...]) + ex[2][...]).astype(BF16)
        outs[0][...] = h
        return h

    def epi(acc, ex, outs):
        c, u, d = ex[3][...], ex[4][...], ex[5][...]
        _, rec_ref, q_ref, k_ref, v_ref = outs
        for k in range(HG_SLABS):
            rec_ref[k] = acc[:, ATT_COLS + k * HG_WIDTH:ATT_COLS + (k + 1) * HG_WIDTH]

        def rope(z):
            return (z * c + pltpu.roll(z, half, 1) * u + pltpu.roll(z, LANE - half, 1) * d).astype(BF16)

        for s in range(ATT_WIDTH // LANE):
            slab = rope(acc[:, s * LANE:(s + 1) * LANE])
            for part in range(heads_per_slab):
                g, hh = divmod(s * heads_per_slab + part, ATT_GROUP)
                piece = slab[:, part * ATT_HEAD_DIM:(part + 1) * ATT_HEAD_DIM]
                for blk in range(tm // WINDOW):
                    q_ref[blk, g, hh * WINDOW:(hh + 1) * WINDOW, :] = piece[blk * WINDOW:(blk + 1) * WINDOW]
        rk = rope(acc[:, ATT_WIDTH:ATT_WIDTH + LANE])
        vv = acc[:, ATT_WIDTH + LANE:ATT_COLS].astype(BF16)
        for g in range(ATT_KV_HEADS):
            k_ref[g] = rk[:, g * ATT_HEAD_DIM:(g + 1) * ATT_HEAD_DIM]
            v_ref[g] = vv[:, g * ATT_HEAD_DIM:(g + 1) * ATT_HEAD_DIM]

    tab = pl.BlockSpec((tm, LANE), lambda b, i: (i, 0))
    kv_spec = pl.BlockSpec((None, ATT_KV_HEADS, tm, ATT_HEAD_DIM), lambda b, i: (b, 0, i, 0))
    kv_shape = _sds((bsz, ATT_KV_HEADS, seq, ATT_HEAD_DIM), BF16)
    q_spec = pl.BlockSpec((None, tm // WINDOW, ATT_KV_HEADS, GROUP_ROWS, ATT_HEAD_DIM), lambda b, i: (b, i, 0, 0, 0))
    return _mm_rows(x, w_in_t, name="in_proj", tm=tm, extra=(w, sc, sh, *tables),
                    extra_specs=[_vec_spec(), _row_spec(), _row_spec(), tab, tab, tab],
                    out_specs=[_tok_spec(tm), pl.BlockSpec((None, HG_SLABS, tm, HG_WIDTH), lambda b, i: (b, 0, i, 0)), q_spec,
                               kv_spec, kv_spec],
                    out_shape=[_sds(x.shape, BF16), _sds((bsz, HG_SLABS, seq, HG_WIDTH), F32),
                               _sds((bsz, seq // WINDOW, ATT_KV_HEADS, GROUP_ROWS, ATT_HEAD_DIM), BF16), kv_shape, kv_shape],
                    pro=pro, epi=epi, trans_b=True, comms=comms)


def _rope_tables(seq):
    half = ROPE_DIM // 2
    inv_freq = ROPE_THETA ** (-jnp.arange(0, ROPE_DIM, 2, dtype=F32) / ROPE_DIM)
    ang = jnp.arange(seq, dtype=F32)[:, None] * inv_freq[None, :]
    cos, sin = jnp.cos(ang), jnp.sin(ang)
    rest = ATT_HEAD_DIM - ROPE_DIM
    ones, zeros, zh = jnp.ones((seq, rest), F32), jnp.zeros((seq, rest), F32), jnp.zeros((seq, half), F32)
    reps = LANE // ATT_HEAD_DIM
    t_cos = jnp.tile(jnp.concatenate([cos, cos, ones], axis=1), (1, reps))
    t_up = jnp.tile(jnp.concatenate([zh, sin, zeros], axis=1), (1, reps))
    t_dn = jnp.tile(jnp.concatenate([-sin, zh, zeros], axis=1), (1, reps))
    return t_cos, t_up, t_dn


GROUP_ROWS = ATT_GROUP * WINDOW


ATT_BPS = 2


MASKED = -1e30


def _band_biases():
    row = jnp.arange(GROUP_ROWS)[:, None] % WINDOW
    col = jnp.arange(2 * WINDOW)[None, :]
    own = jnp.logical_and(col >= WINDOW, col - WINDOW <= row)
    before = jnp.logical_and(col < WINDOW, col > row)
    return (jnp.where(jnp.logical_or(own, before), 0.0, MASKED).astype(F32), jnp.where(own, 0.0, MASKED).astype(F32))


def _band_bias(full_ref, first_ref, has_prev):
    return full_ref[...] if has_prev is True else jnp.where(has_prev, full_ref[...], first_ref[...])


def _sink_column(sink_ref, g):
    head = lax.broadcasted_iota(jnp.int32, (GROUP_ROWS, 1), 0) // WINDOW
    col = jnp.full((GROUP_ROWS, 1), sink_ref[0, g * ATT_GROUP], F32)
    for hh in range(1, ATT_GROUP):
        col = jnp.where(head == hh, sink_ref[0, g * ATT_GROUP + hh], col)
    return col


def _sink_row(sink_ref, g):
    return jnp.concatenate([jnp.full((1, WINDOW), sink_ref[0, g * ATT_GROUP + hh], F32) for hh in range(ATT_GROUP)], axis=1)


def _bias_spec(transposed=False):
    shape = (2 * WINDOW, GROUP_ROWS) if transposed else (GROUP_ROWS, 2 * WINDOW)
    return pl.BlockSpec(shape, lambda b, i: (0, 0))


def _attn_specs():
    q_spec = pl.BlockSpec((None, ATT_BPS, ATT_KV_HEADS, GROUP_ROWS, ATT_HEAD_DIM), lambda b, i: (b, i, 0, 0, 0))
    kv_cur = pl.BlockSpec((None, ATT_KV_HEADS, ATT_BPS * WINDOW, ATT_HEAD_DIM), lambda b, i: (b, 0, i, 0))
    kv_prev = pl.BlockSpec((None, ATT_KV_HEADS, WINDOW, ATT_HEAD_DIM), lambda b, i: (b, 0, jnp.maximum(ATT_BPS * i - 1, 0), 0))
    return q_spec, kv_cur, kv_prev


def _band(prev_ref, cur_ref, g, blk):
    own = cur_ref[g, blk * WINDOW:(blk + 1) * WINDOW]
    before = prev_ref[g] if blk == 0 else cur_ref[g, (blk - 1) * WINDOW:blk * WINDOW]
    return jnp.concatenate([before, own], axis=0)


def _attn_fwd(qh, kh, vh, sinks, w_norm, biases, comms=()):
    bsz, nblk = qh.shape[0], qh.shape[1]
    seq = nblk * WINDOW
    rows = ATT_BPS * WINDOW

    def body(sink_ref, q_ref, kc_ref, kp_ref, vc_ref, vp_ref, w_ref, full_ref, first_ref, raw_ref, an_ref, l_ref):
        l_ref[...] = jnp.zeros_like(l_ref)
        for blk in range(ATT_BPS):
            bias = _band_bias(full_ref, first_ref, True if blk else pl.program_id(1) > 0)
            groups = range(ATT_KV_HEADS)
            keys, vals = [_band(kp_ref, kc_ref, g, blk) for g in groups], [_band(vp_ref, vc_ref, g, blk) for g in groups]
            sink = [_sink_column(sink_ref, g) for g in groups]
            s = [_dot(q_ref[blk, g], keys[g], NT_DIMS) * ATT_SCALE + bias for g in groups]
            m = [jnp.maximum(jnp.max(s[g], axis=-1, keepdims=True), sink[g]) for g in groups]
            p = [jnp.exp(s[g] - m[g]) for g in groups]
            den = [jnp.sum(p[g], axis=-1, keepdims=True) + jnp.exp(sink[g] - m[g]) for g in groups]
            o = [_dot(p[g] / den[g], vals[g]) for g in groups]
            lse = [m[g] + jnp.log(den[g]) for g in groups]
            tok = slice(blk * WINDOW, (blk + 1) * WINDOW)
            for g in groups:
                for hh in range(ATT_GROUP):
                    h = g * ATT_GROUP + hh
                    raw_ref[tok, h * ATT_HEAD_DIM:(h + 1) * ATT_HEAD_DIM] = o[g][hh * WINDOW:(hh + 1) * WINDOW]
                    l_ref[tok, h:h + 1] = lse[g][hh * WINDOW:(hh + 1) * WINDOW]
        y, _, _ = _rms_fwd(raw_ref[...], w_ref[...])
        an_ref[...] = y.astype(BF16)

    cur = lambda width: pl.BlockSpec((None, rows, width), lambda b, i: (b, i, 0))
    q_spec, kv_cur, kv_prev = _attn_specs()
    return _call(
        body, (sinks, qh, kh, kh, vh, vh, w_norm, *biases), name="attn_fwd", grid=(bsz, nblk // ATT_BPS),
        in_specs=[pl.BlockSpec(memory_space=pltpu.SMEM), q_spec, kv_cur, kv_prev, kv_cur, kv_prev, _vec_spec(ATT_WIDTH),
                  _bias_spec(), _bias_spec()],
        out_specs=[cur(ATT_WIDTH), cur(ATT_WIDTH), cur(LANE)],
        out_shape=[_sds((bsz, seq, ATT_WIDTH), F32), _sds((bsz, seq, MIX_WIDTH), BF16), _sds((bsz, seq, LANE), F32)],
        sem=("parallel", "parallel"), comms=comms)


HG_Q0 = ATT_COLS // LANE
HG_F0 = HG_Q0 + HG_HEADS
HG_I0 = HG_F0 + HG_HEADS
HG_G0 = HG_I0 + HG_HEADS
HG_SLABS = 4
HG_Q, HG_F, HG_I, HG_G = range(HG_SLABS)
HG_TOK = 256
HG_NCH = HG_TOK // HG_CHUNK
HG_HPS = 2


def _block_masks():
    row = jnp.arange(HG_TOK)[:, None]
    col = jnp.arange(HG_TOK)[None, :]
    same = (row // HG_CHUNK) == (col // HG_CHUNK)
    return jnp.logical_and(same, col <= row).astype(F32), jnp.logical_and(same, col >= row).astype(F32)


def _row_in_chunk():
    return lax.broadcasted_iota(jnp.int32, (HG_TOK, LANE), 0) % HG_CHUNK


def _chunk_cumsum(x, reverse=False):
    ric = _row_in_chunk()
    shift = 1
    while shift < HG_CHUNK:
        if reverse:
            x = x + jnp.where(ric < HG_CHUNK - shift, pltpu.roll(x, HG_TOK - shift, 0), 0.0)
        else:
            x = x + jnp.where(ric >= shift, pltpu.roll(x, shift, 0), 0.0)
        shift *= 2
    return x


def _chunk_rows(rows):
    stacked = jnp.concatenate([r[None] for r in rows], axis=0)
    return jnp.broadcast_to(stacked, (HG_NCH, HG_CHUNK, LANE)).reshape(HG_TOK, LANE)


def _chunk_slices(x):
    return [x[j * HG_CHUNK:(j + 1) * HG_CHUNK] for j in range(HG_NCH)]


def _hgrn_common(tbl, hf, hq):
    lb = _sigmoid(tbl[1:2] - tbl[0:1])
    sig = _sigmoid(hf)
    f = lb + (1.0 - lb) * sig
    sq = _sigmoid(hq)
    q, k = hq * sq, 1.0 - f
    b = _chunk_cumsum(jnp.log(f))
    last = [b[(j + 1) * HG_CHUNK - 1:(j + 1) * HG_CHUNK] for j in range(HG_NCH)]
    bl = _chunk_rows(last)
    e_b, e_nb, e_rem = jnp.exp(b), jnp.exp(-b), jnp.exp(bl - b)
    e_last = [jnp.exp(r) for r in last]
    return dict(lb=lb, sig=sig, f=f, sq=sq, q=q, k=k, e_b=e_b, e_nb=e_nb, e_rem=e_rem, e_last=e_last,
                qd=q * e_b, kd=k * e_nb, ku=k * e_rem)


def _hgrn_fwd(proj, lb_table, norm_w, mix_in, masks, comms=()):
    bsz, _, seq, _ = proj.shape
    nstep = seq // HG_TOK

    def body(tbl_ref, nw_ref, p_ref, mix_ref, lower_ref, o_ref, rec_ref, st_ref, s_scr):
        @pl.when(pl.program_id(2) == 0)
        def _():
            s_scr[...] = jnp.zeros_like(s_scr)

        lower = lower_ref[...]
        for hp in range(HG_HPS):
            ls = slice(hp * LANE, (hp + 1) * LANE)
            v, hg = p_ref[HG_I, :, ls], p_ref[HG_G, :, ls]
            t = _hgrn_common(tbl_ref[:, ls], p_ref[HG_F, :, ls], p_ref[HG_Q, :, ls])
            a = _dot(t["qd"], t["kd"], NT_DIMS) * lower
            o_intra = _dot(a, v)
            v_c, ku_c, qd_c = [_chunk_slices(z.astype(BF16)) for z in (v, t["ku"], t["qd"])]
            updates = [_dot(v_c[j], ku_c[j], TN_DIMS) for j in range(HG_NCH)]
            st = s_scr[hp]
            states = []
            for j in range(HG_NCH):
                states.append(st)
                st = st * t["e_last"][j] + updates[j]
            s_scr[hp] = st
            o = o_intra + jnp.concatenate([_dot(qd_c[j], states[j], NT_DIMS) for j in range(HG_NCH)], axis=0)
            st_ref[hp, 0] = states[0]
            o_ref[:, ls] = o
            y, _, _ = _rms_fwd(o, nw_ref[...])
            rec_ref[:, ls] = (y * (hg * _sigmoid(hg))).astype(BF16)

    width = HG_HPS * LANE
    head_out = pl.BlockSpec((None, HG_TOK, width), lambda b, h, t: (b, t, h))
    mix_out = pl.BlockSpec((None, HG_TOK, width), lambda b, h, t: (b, t, ATT_WIDTH // width + h))
    return _call(
        body, (lb_table, norm_w, proj, mix_in, masks[0]), name="hgrn_fwd", grid=(bsz, HG_HEADS // HG_HPS, nstep),
        in_specs=[pl.BlockSpec((2, width), lambda b, h, t: (0, h)), pl.BlockSpec((1, LANE), lambda b, h, t: (0, 0)),
                  pl.BlockSpec((None, HG_SLABS, HG_TOK, width), lambda b, h, t: (b, 0, t, h)), pl.BlockSpec(memory_space=pl.ANY),
                  pl.BlockSpec((HG_TOK, HG_TOK), lambda b, h, t: (0, 0))],
        out_specs=[head_out, mix_out,
                   pl.BlockSpec((None, HG_HPS, 1, LANE, LANE), lambda b, h, t: (b, h, t, 0, 0))],
        out_shape=[_sds((bsz, seq, HG_WIDTH), F32), _sds(mix_in.shape, BF16),
                   _sds((bsz, HG_HEADS, nstep, LANE, LANE), F32)],
        scratch_shapes=[pltpu.VMEM((HG_HPS, LANE, LANE), F32)],
        sem=("parallel", "parallel", "arbitrary"), comms=comms, aliases={3: 1})


def _out_proj_fused(cat, w_out, x, post_w, g1, pre_w, sc2, sh2):
    tm = 512

    def epi(mix, ex, outs):
        x_ref, pw_ref, g1_ref, w2_ref, sc_ref, sh_ref = ex
        outs[0][...] = mix
        n1, _, _ = _rms_fwd(mix, pw_ref[...])
        x1 = x_ref[...] + g1_ref[...] * n1
        outs[1][...] = x1
        y2, _, _ = _rms_fwd(x1, w2_ref[...])
        outs[2][...] = (y2 * (1.0 + sc_ref[...]) + sh_ref[...]).astype(BF16)

    return _mm_rows(cat, w_out, name="out_proj", tm=tm, extra=(x, post_w, g1, pre_w, sc2, sh2),
                    extra_specs=[_tok_spec(tm), _vec_spec(), _row_spec(), _vec_spec(), _row_spec(), _row_spec()],
                    out_specs=[_tok_spec(tm), _tok_spec(tm), _tok_spec(tm)],
                    out_shape=[_sds(x.shape, F32), _sds(x.shape, F32), _sds(x.shape, BF16)], epi=epi)


def _acc_out(ref, first, value):
    @pl.when(first)
    def _():
        ref[...] = value

    @pl.when(jnp.logical_not(first))
    def _():
        ref[...] += value


def _down_proj_fused(r, w_down, x1, post_w, g2, target):
    tm = 512
    bsz = x1.shape[0]

    def pro(r_ref, ex, outs):
        rv = r_ref[...]
        return rv * rv

    def epi(down, ex, outs):
        x1_ref, w_ref, g2_ref, t_ref = ex
        loss_ref, dy_ref, dd_ref, dg2_ref, dw_ref = outs
        w, g2v = w_ref[...], g2_ref[...]
        gain = g2v * w
        dh, rstd = _rms_hat(down)
        err = x1_ref[...] + dh * gain - t_ref[...]
        part = (0.5 / D_MODEL) * jnp.sum(jnp.sum(err * err, axis=-1, keepdims=True), axis=0, keepdims=True)
        loss_ref[...] += jnp.broadcast_to(part, (1, LANE))
        dy = err * (1.0 / D_MODEL)
        dy_ref[...] = dy
        dd, per_col = _rms_bwd_gain(dy, gain, dh, rstd)
        dd_ref[...] = dd.astype(BF16)
        dg2_ref[...] += per_col * w
        dw_ref[...] += per_col * g2v

    return _mm_rows(r, w_down, name="down_proj", tm=tm, extra=(x1, post_w, g2, target),
                    extra_specs=[_tok_spec(tm), _vec_spec(), _row_spec(), _tok_spec(tm)],
                    out_specs=[_vec_spec(LANE), _tok_spec(tm), _tok_spec(tm), _row_spec(), _vec_spec()],
                    out_shape=[_sds((1, LANE), F32), _sds(x1.shape, F32), _sds(x1.shape, BF16), _sds((bsz, 1, D_MODEL), F32),
                               _sds((1, D_MODEL), F32)], pro=pro, epi=epi, parts=2, zero_per_seq=(3,), zero_once=(0, 4),
                    vmem_limit=VMEM_LIMIT_BIG)


def _up_bwd_fused(dpre, w_up4, dy, x1, mix, pre_w, sc2, post_w, g1, comms=()):
    tm = 512
    bsz = x1.shape[0]

    def epi(dh2v, ex, outs):
        dy_ref, x1_ref, mix_ref, w2_ref, sc_ref, pw_ref, g1_ref = ex
        dx1_ref, dmix_ref, dsc_ref, dsh_ref, dg1_ref, dw2_ref, dpw_ref = outs
        w2, pw, g1v = w2_ref[...], pw_ref[...], g1_ref[...]
        mod2 = 1.0 + sc_ref[...]
        xh2, rstd2 = _rms_hat(x1_ref[...])
        dsh_ref[...] += _colsum(dh2v)
        dx1n, per_col2 = _rms_bwd_gain(dh2v, mod2 * w2, xh2, rstd2)
        dsc_ref[...] += per_col2 * w2
        dw2_ref[...] += per_col2 * mod2
        dx1 = dy_ref[...] + dx1n
        dx1_ref[...] = dx1
        mh, rstd1 = _rms_hat(mix_ref[...])
        dmix, per_col1 = _rms_bwd_gain(dx1, g1v * pw, mh, rstd1)
        dmix_ref[...] = dmix.astype(BF16)
        dg1_ref[...] += per_col1 * pw
        dpw_ref[...] += per_col1 * g1v

    row_shape = _sds((bsz, 1, D_MODEL), F32)
    vec_shape = _sds((1, D_MODEL), F32)
    return _mm_rows(dpre, w_up4, name="up_bwd", tm=tm, extra=(dy, x1, mix, pre_w, sc2, post_w, g1),
                    extra_specs=[_tok_spec(tm), _tok_spec(tm), _tok_spec(tm), _vec_spec(), _row_spec(), _vec_spec(), _row_spec()],
                    out_specs=[_tok_spec(tm), _tok_spec(tm), _row_spec(), _row_spec(), _row_spec(), _vec_spec(), _vec_spec()],
                    out_shape=[_sds(x1.shape, F32), _sds(x1.shape, BF16), row_shape, row_shape, row_shape, vec_shape, vec_shape],
                    epi=epi, b_chunks=w_up4.shape[0], comms=comms, parts=2, zero_per_seq=(2, 3, 4), zero_once=(5, 6),
                    vmem_limit=VMEM_LIMIT_BIG)


def _norm1_bwd(dh1, dx1, x, pre_w, sc1, tm=512, comms=()):
    bsz, seq, _ = x.shape

    def body(dh_ref, dx1_ref, x_ref, w_ref, sc_ref, gx_ref, dsc_ref, dsh_ref, dw_ref):
        b, i = pl.program_id(0), pl.program_id(1)
        w = w_ref[...]
        dh = dh_ref[...]
        mod = 1.0 + sc_ref[...]
        xh, rstd = _rms_hat(x_ref[...])
        dx, per_col = _rms_bwd_gain(dh, mod * w, xh, rstd)
        _acc_out(dsh_ref, i == 0, _colsum(dh))
        _acc_out(dsc_ref, i == 0, per_col * w)
        _acc_out(dw_ref, jnp.logical_and(b == 0, i == 0), per_col * mod)
        gx_ref[...] = dx1_ref[...] + dx

    row_shape = _sds((bsz, 1, D_MODEL), F32)
    return _call(
        body, (dh1, dx1, x, pre_w, sc1), name="norm1_bwd", grid=(bsz, seq // tm),
        in_specs=[_tok_spec(tm), _tok_spec(tm), _tok_spec(tm), _vec_spec(), _row_spec()],
        out_specs=[_tok_spec(tm), _row_spec(), _row_spec(), _vec_spec()],
        out_shape=[_sds(x.shape, F32), row_shape, row_shape, _sds((1, D_MODEL), F32)],
        sem=("arbitrary", "arbitrary"), comms=comms)


def _hgrn_bwd(dcat, proj, o_raw, states, lb_table, norm_w, masks, comms=()):
    bsz, _, seq, _ = proj.shape
    nstep = seq // HG_TOK
    rec0 = ATT_WIDTH // LANE
    width = HG_HPS * LANE
    slabs = (HG_Q0, HG_F0, HG_I0, HG_G0)
    n_steps = (HG_HEADS // HG_HPS) * bsz * nstep
    assert n_steps >= 2

    def body(tbl_ref, nw_ref, dr_ref, p_ref, o_ref, st_ref, lower_ref, upper_ref,
             dproj_ref, dlb_ref, dnw_ref, ds_scr, grad_buf, grad_sem):
        h, b, t = pl.program_id(0), pl.program_id(1), pl.program_id(2)
        step = (h * bsz + b) * nstep + t
        slot = step % 2
        dq_k, df_k, di_k, dg_k = range(4)

        def grad_copies(of_step):
            hh, bb, tt = of_step // (bsz * nstep), (of_step // nstep) % bsz, of_step % nstep
            rows = pl.ds(pl.multiple_of((nstep - 1 - tt) * HG_TOK, HG_TOK), HG_TOK)
            return [pltpu.make_async_copy(
                grad_buf.at[of_step % 2, k],
                dproj_ref.at[bb, rows, pl.ds(pl.multiple_of(slabs[k] * LANE + hh * width, width), width)],
                grad_sem.at[of_step % 2, k]) for k in range(4)]

        @pl.when(step >= 2)
        def _():
            for cp in grad_copies(step - 2):
                cp.wait()

        @pl.when(t == 0)
        def _():
            ds_scr[...] = jnp.zeros_like(ds_scr)

        lower, upper = lower_ref[...], upper_ref[...]
        dlb_parts = []
        dnw_acc = jnp.zeros((1, LANE), F32)
        for hp in range(HG_HPS):
            ls = slice(hp * LANE, (hp + 1) * LANE)
            hq, v, hg = p_ref[HG_Q, :, ls], p_ref[HG_I, :, ls], p_ref[HG_G, :, ls]
            nw = nw_ref[...]
            c = _hgrn_common(tbl_ref[:, ls], p_ref[HG_F, :, ls], hq)
            qd, kd, ku = c["qd"], c["kd"], c["ku"]
            y, on, rstd = _rms_fwd(o_ref[:, ls], nw)
            sg = _sigmoid(hg)
            dr = dr_ref[:, ls]
            grad_buf[slot, dg_k, :, ls] = (dr * y * (sg * (1.0 + hg * (1.0 - sg)))).astype(BF16)
            do, dnw_rows = _rms_bwd(dr * (hg * sg), on, rstd, nw)
            at = _dot(kd, qd, NT_DIMS) * upper
            da = _dot(do, v, NT_DIMS) * lower
            dat = _dot(v, do, NT_DIMS) * upper
            dv = _dot(at, do)
            dqd = _dot(da, kd)
            dkd = _dot(dat, qd)
            do_c, qd_c, v_c, ku_c = [_chunk_slices(z.astype(BF16)) for z in (do, qd, v, ku)]
            outer = [_dot(do_c[j], qd_c[j], TN_DIMS) for j in range(HG_NCH)]
            ds = ds_scr[hp]
            ds_after = [None] * HG_NCH
            for j in reversed(range(HG_NCH)):
                ds_after[j] = ds
                ds = outer[j] + ds * c["e_last"][j]
            ds_scr[hp] = ds
            updates = [_dot(v_c[j], ku_c[j], TN_DIMS) for j in range(HG_NCH)]
            states = [st_ref[hp, 0]]
            for j in range(HG_NCH - 1):
                states.append(states[j] * c["e_last"][j] + updates[j])
            dv = dv + jnp.concatenate([_dot(ku_c[j], ds_after[j], NT_DIMS) for j in range(HG_NCH)], axis=0)
            dqd = dqd + jnp.concatenate([_dot(do_c[j], states[j]) for j in range(HG_NCH)], axis=0)
            dku = jnp.concatenate([_dot(v_c[j], ds_after[j]) for j in range(HG_NCH)], axis=0)
            dku_ku = dku * ku
            dbl = [_colsum(states[j] * ds_after[j]) * c["e_last"][j] + _colsum(dku_ku[j * HG_CHUNK:(j + 1) * HG_CHUNK])
                   for j in range(HG_NCH)]
            dk = dkd * c["e_nb"] + dku * c["e_rem"]
            db = dqd * qd - dkd * kd - dku_ku + jnp.where(_row_in_chunk() == HG_CHUNK - 1, _chunk_rows(dbl), 0.0)
            dfv = _chunk_cumsum(db, reverse=True) / c["f"] - dk
            sig, sq = c["sig"], c["sq"]
            grad_buf[slot, df_k, :, ls] = (dfv * (1.0 - c["lb"]) * sig * (1.0 - sig)).astype(BF16)
            grad_buf[slot, dq_k, :, ls] = (dqd * c["e_b"] * (sq * (1.0 + hq * (1.0 - sq)))).astype(BF16)
            grad_buf[slot, di_k, :, ls] = dv.astype(BF16)
            dlb_parts.append(_colsum(dfv * (1.0 - sig)))
            dnw_acc = dnw_acc + _colsum(dnw_rows)
        _acc_out(dlb_ref, jnp.logical_and(b == 0, t == 0), jnp.concatenate(dlb_parts, axis=1))
        _acc_out(dnw_ref, jnp.logical_and(h == 0, jnp.logical_and(b == 0, t == 0)), dnw_acc)
        for cp in grad_copies(step):
            cp.start()

        @pl.when(step == n_steps - 1)
        def _():
            for cp in grad_copies(step - 1) + grad_copies(step):
                cp.wait()

    rev = lambda t: nstep - 1 - t
    slab = lambda first: pl.BlockSpec((None, HG_TOK, width), lambda h, b, t: (b, rev(t), first // HG_HPS + h))
    head = pl.BlockSpec((None, HG_TOK, width), lambda h, b, t: (b, rev(t), h))
    return _call(
        body, (lb_table, norm_w, dcat, proj, o_raw, states, *masks), name="hgrn_bwd",
        grid=(HG_HEADS // HG_HPS, bsz, nstep),
        in_specs=[pl.BlockSpec((2, width), lambda h, b, t: (0, h)), pl.BlockSpec((1, LANE), lambda h, b, t: (0, 0)),
                  slab(rec0), pl.BlockSpec((None, HG_SLABS, HG_TOK, width), lambda h, b, t: (b, 0, rev(t), h)), head,
                  pl.BlockSpec((None, HG_HPS, 1, LANE, LANE), lambda h, b, t: (b, h, rev(t), 0, 0)),
                  pl.BlockSpec((HG_TOK, HG_TOK), lambda h, b, t: (0, 0)), pl.BlockSpec((HG_TOK, HG_TOK), lambda h, b, t: (0, 0))],
        out_specs=[pl.BlockSpec(memory_space=pl.ANY), pl.BlockSpec((1, width), lambda h, b, t: (0, h)),
                   pl.BlockSpec((1, LANE), lambda h, b, t: (0, 0))],
        out_shape=[_sds((bsz, seq, IN_COLS), BF16), _sds((1, HG_WIDTH), F32), _sds((1, LANE), F32)],
        scratch_shapes=[pltpu.VMEM((HG_HPS, LANE, LANE), F32), pltpu.VMEM((2, 4, HG_TOK, width), BF16),
                        pltpu.SemaphoreType.DMA((2, 4))],
        sem=("arbitrary", "arbitrary", "arbitrary"), comms=comms)


def _attn_bwd(dcat, raw, w_norm, qh, kh, vh, lse, sinks, tables, biases, dproj, comms=()):
    bsz, nblk = qh.shape[0], qh.shape[1]
    seq = nblk * WINDOW
    nstep = nblk // ATT_BPS
    half = ROPE_DIM // 2

    def body(sink_ref, da_ref, raw_ref, w_ref, q_ref, kc_ref, kp_ref, vc_ref, vp_ref, l_ref, c_ref, u_ref, d_ref,
             full_ref, first_ref, dproj_ref, o_ref, dw_ref, dsink_ref, carry_k, carry_v):
        b, i = pl.program_id(0), pl.program_id(1)
        first = jnp.logical_and(b == 0, i == 0)

        @pl.when(i == 0)
        def _():
            carry_k[...] = jnp.zeros_like(carry_k)
            carry_v[...] = jnp.zeros_like(carry_v)

        w = w_ref[...]
        _, on, rstd = _rms_fwd(raw_ref[...], w)
        do_step, dw_rows = _rms_bwd(da_ref[...], on, rstd, w)
        _acc_out(dw_ref, first, _colsum(dw_rows))
        lane8 = lax.broadcasted_iota(jnp.int32, (1, ATT_Q_HEADS), 1)
        dsink = jnp.zeros((1, ATT_Q_HEADS), F32)
        head_cols = jnp.where(lax.broadcasted_iota(jnp.int32, (2 * ATT_Q_HEADS, ATT_WIDTH), 1) // ATT_HEAD_DIM
                              == lax.broadcasted_iota(jnp.int32, (2 * ATT_Q_HEADS, ATT_WIDTH), 0), 1.0, 0.0)
        from_next_k, from_next_v = carry_k[...], carry_v[...]
        for blk in reversed(range(ATT_BPS)):
            tok = slice(blk * WINDOW, (blk + 1) * WINDOW)
            bias = _band_bias(full_ref, first_ref, True if blk else i < nstep - 1)
            do_all = do_step[tok]
            c, u, d = c_ref[tok, :], u_ref[tok, :], d_ref[tok, :]
            lse_t = l_ref[tok, :].T
            prod = do_all * raw_ref[tok, :]
            prod_hi = prod.astype(BF16)
            prod_lo = prod - prod_hi.astype(F32)
            dsum_t = _dot(head_cols, prod_hi, NT_DIMS) + _dot(head_cols, prod_lo, NT_DIMS)

            def unrope(g):
                return (g * c + pltpu.roll(g * u, LANE - half, 1) + pltpu.roll(g * d, half, 1)).astype(BF16)

            groups = range(ATT_KV_HEADS)
            group_row = lambda z, g: jnp.concatenate(
                [z[g * ATT_GROUP + hh:g * ATT_GROUP + hh + 1, :] for hh in range(ATT_GROUP)], axis=1)
            q = [q_ref[blk, g] for g in groups]
            keys, vals = [_band(kp_ref, kc_ref, g, blk) for g in groups], [_band(vp_ref, vc_ref, g, blk) for g in groups]
            do_g = [jnp.concatenate([do_all[:, (g * ATT_GROUP + hh) * ATT_HEAD_DIM:(g * ATT_GROUP + hh + 1) * ATT_HEAD_DIM]
                                     for hh in range(ATT_GROUP)], axis=0) for g in groups]
            dsum, lse_g = [group_row(dsum_t, g) for g in groups], [group_row(lse_t, g) for g in groups]
            s_t = [_dot(keys[g], q[g], NT_DIMS) for g in groups]
            dp_t = [_dot(vals[g], do_g[g], NT_DIMS) for g in groups]
            p_t = [jnp.exp(s_t[g] * ATT_SCALE + bias - lse_g[g]) for g in groups]
            ds_t = [p_t[g] * (dp_t[g] - dsum[g]) * ATT_SCALE for g in groups]
            dq_g = [_dot(ds_t[g], keys[g], TN_DIMS) for g in groups]
            dk_g = [_dot(ds_t[g], q[g]) for g in groups]
            dv_g = [_dot(p_t[g], do_g[g]) for g in groups]
            for g in groups:
                sink_part = jnp.exp(_sink_row(sink_ref, g) - lse_g[g]) * dsum[g]
                for hh in range(ATT_GROUP):
                    head_sum = jnp.sum(sink_part[:, hh * WINDOW:(hh + 1) * WINDOW], axis=1, keepdims=True)
                    dsink = dsink - jnp.where(lane8 == g * ATT_GROUP + hh, head_sum, 0.0)
            dq_parts = [dq_g[g][hh * WINDOW:(hh + 1) * WINDOW] for g in groups for hh in range(ATT_GROUP)]
            dk_before, dk_own = [z[:WINDOW] for z in dk_g], [z[WINDOW:] for z in dk_g]
            dv_before, dv_own = [z[:WINDOW] for z in dv_g], [z[WINDOW:] for z in dv_g]
            per_slab = LANE // ATT_HEAD_DIM
            for s in range(ATT_WIDTH // LANE):
                slab = jnp.concatenate(dq_parts[s * per_slab:(s + 1) * per_slab], axis=1)
                o_ref[tok, s * LANE:(s + 1) * LANE] = unrope(slab)
            o_ref[tok, ATT_WIDTH:ATT_WIDTH + LANE] = unrope(jnp.concatenate(dk_own, axis=1) + from_next_k)
            o_ref[tok, ATT_WIDTH + LANE:ATT_COLS] = (jnp.concatenate(dv_own, axis=1) + from_next_v).astype(BF16)
            from_next_k, from_next_v = jnp.concatenate(dk_before, axis=1), jnp.concatenate(dv_before, axis=1)
        carry_k[...] = from_next_k
        carry_v[...] = from_next_v
        _acc_out(dsink_ref, first, dsink)

    rows = ATT_BPS * WINDOW
    rev = lambda i: nstep - 1 - i
    cur = lambda width: pl.BlockSpec((None, rows, width), lambda b, i: (b, rev(i), 0))
    q_spec = pl.BlockSpec((None, ATT_BPS, ATT_KV_HEADS, GROUP_ROWS, ATT_HEAD_DIM), lambda b, i: (b, rev(i), 0, 0, 0))
    kv_cur = pl.BlockSpec((None, ATT_KV_HEADS, rows, ATT_HEAD_DIM), lambda b, i: (b, 0, rev(i), 0))
    kv_prev = pl.BlockSpec((None, ATT_KV_HEADS, WINDOW, ATT_HEAD_DIM), lambda b, i: (b, 0, jnp.maximum(ATT_BPS * rev(i) - 1, 0), 0))
    tab = pl.BlockSpec((rows, LANE), lambda b, i: (rev(i), 0))
    return _call(
        body, (sinks, dcat, raw, w_norm, qh, kh, kh, vh, vh, lse, *tables, *biases, dproj), name="attn_bwd", grid=(bsz, nstep),
        in_specs=[pl.BlockSpec(memory_space=pltpu.SMEM), cur(ATT_WIDTH), cur(ATT_WIDTH), _vec_spec(ATT_WIDTH), q_spec,
                  kv_cur, kv_prev, kv_cur, kv_prev, cur(LANE), tab, tab, tab, _bias_spec(True), _bias_spec(True),
                  pl.BlockSpec(memory_space=pl.ANY)],
        out_specs=[cur(ATT_COLS), _vec_spec(ATT_WIDTH), _vec_spec(ATT_Q_HEADS)],
        out_shape=[_sds(dproj.shape, BF16), _sds((1, ATT_WIDTH), F32), _sds((1, ATT_Q_HEADS), F32)],
        scratch_shapes=[pltpu.VMEM((WINDOW, LANE), F32), pltpu.VMEM((WINDOW, LANE), F32)],
        sem=("arbitrary", "arbitrary"), comms=comms, aliases={15: 0})


def _other_chips(x, y):
    return [(1 - x, y), (x, 1 - y), (1 - x, 1 - y)]


def _sem_pair(n):
    return [pltpu.SemaphoreType.DMA((n,)), pltpu.SemaphoreType.DMA((n,))]


def _plan_pair_forward(bufs):
    n = len(bufs)

    def copies(outs, sems):
        x, y, c = _mesh_pos()
        sends, lands = [], []
        for a in range(n):
            for j, chip in enumerate(_other_chips(x, y)):
                k = 3 * a + j
                slot = outs[a].at[4 * chip[0] + 2 * chip[1] + c]
                sends.append(pltpu.make_async_remote_copy(
                    src_ref=slot, dst_ref=slot, send_sem=sems[0].at[k], recv_sem=sems[1].at[k],
                    device_id=(x, y, 1 - c), device_id_type=MESH))
                theirs = outs[a].at[4 * chip[0] + 2 * chip[1] + 1 - c]
                lands.append(pltpu.make_async_remote_copy(
                    src_ref=theirs, dst_ref=theirs, send_sem=sems[0].at[k], recv_sem=sems[1].at[k],
                    device_id=(x, y, 1 - c), device_id_type=MESH))
        return sends, lands

    def start(ins, outs, sems):
        for cp in copies(outs, sems)[0]:
            cp.start()

    def finish(ins, outs, sems):
        sends, lands = copies(outs, sems)
        for cp in lands:
            cp.wait_recv()
        for cp in sends:
            cp.wait_send()

    return _Comm(list(bufs), [_sds(b.shape, b.dtype) for b in bufs], _sem_pair(3 * n), start, finish,
                 aliases=[(a, a) for a in range(n)])


def _plan_pair(arrays, other_half):
    n = len(arrays)
    per = N_CHIPS if other_half == "chip_major" else 1

    def copies(ins, outs, sems):
        x, y, c = _mesh_pos()
        out = []
        for a in range(n):
            for k in range(per):
                if other_half == "chip_major":
                    src, dst = ins[a].at[k, 1 - c], outs[a].at[k]
                else:
                    src, dst = (ins[a].at[1 - c] if other_half else ins[a]), outs[a]
                out.append(pltpu.make_async_remote_copy(
                    src_ref=src, dst_ref=dst, send_sem=sems[0].at[per * a + k], recv_sem=sems[1].at[per * a + k],
                    device_id=(x, y, 1 - c), device_id_type=MESH))
        return out

    def start(ins, outs, sems):
        for cp in copies(ins, outs, sems):
            cp.start()

    def finish(ins, outs, sems):
        for cp in copies(ins, outs, sems):
            cp.wait()

    if other_half == "chip_major":
        shapes = [_sds((a.shape[0],) + a.shape[2:], a.dtype) for a in arrays]
    else:
        shapes = [_sds(a.shape[1:] if other_half else a.shape, a.dtype) for a in arrays]
    return _Comm(list(arrays), shapes, _sem_pair(per * n), start, finish)


def _plan_chip_exchange(arrays):
    n = len(arrays)

    def copies(ins, outs, sems):
        x, y, c = _mesh_pos()
        sends, lands = [], []
        for a in range(n):
            for j, chip in enumerate(_other_chips(x, y)):
                k = 3 * a + j
                sends.append(pltpu.make_async_remote_copy(
                    src_ref=ins[a].at[2 * chip[0] + chip[1]], dst_ref=outs[a].at[2 * x + y], send_sem=sems[0].at[k],
                    recv_sem=sems[1].at[k], device_id=(*chip, c), device_id_type=MESH))
                slot = outs[a].at[2 * chip[0] + chip[1]]
                lands.append(pltpu.make_async_remote_copy(
                    src_ref=slot, dst_ref=slot, send_sem=sems[0].at[k], recv_sem=sems[1].at[k],
                    device_id=(*chip, c), device_id_type=MESH))
        return sends, lands

    def start(ins, outs, sems):
        for cp in copies(ins, outs, sems)[0]:
            cp.start()

    def finish(ins, outs, sems):
        sends, lands = copies(ins, outs, sems)
        for cp in lands:
            cp.wait_recv()
        for cp in sends:
            cp.wait_send()

    return _Comm(list(arrays), [_sds(a.shape, a.dtype) for a in arrays], _sem_pair(3 * n), start, finish)


SEM_SPEC = pl.BlockSpec(memory_space=pltpu.SEMAPHORE)
N_OTHER = N_CHIPS - 1


def _exchange_copies(s_ref, land_ref, sems):
    x, y, c = _mesh_pos()
    return [pltpu.make_async_remote_copy(
        src_ref=s_ref.at[2 * chip[0] + chip[1]], dst_ref=land_ref.at[2 * x + y], send_sem=sems[j], recv_sem=sems[N_OTHER + j],
        device_id=(*chip, c), device_id_type=MESH) for j, chip in enumerate(_other_chips(x, y))]


def _exchange_start(s, name):
    def body(s_ref, land_ref, *outs):
        sems, token = outs[:2 * N_OTHER], outs[-1]
        for cp in _exchange_copies(s_ref, land_ref, sems):
            cp.start()
        token[...] = jnp.zeros_like(token)

    hbm = pltpu.HBM(s.shape, s.dtype)
    res = pl.pallas_call(
        body, name=name,
        out_shape=(pltpu.SemaphoreType.DMA(()),) * (2 * N_OTHER) + (hbm, hbm, _sds((SUBLANES, LANE), F32)),
        in_specs=(HBM_SPEC, HBM_SPEC),
        out_specs=(SEM_SPEC,) * (2 * N_OTHER) + (HBM_SPEC, HBM_SPEC, pl.BlockSpec(memory_space=pltpu.VMEM)),
        input_output_aliases={0: 2 * N_OTHER, 1: 2 * N_OTHER + 1},
        compiler_params=pltpu.CompilerParams(has_side_effects=pltpu.SideEffectType.DATAFLOW_SIDE_EFFECTING),
    )(pltpu.with_memory_space_constraint(s, pltpu.HBM), pltpu.with_memory_space_constraint(lax.empty(s.shape, s.dtype), pltpu.HBM))
    return res[:2 * N_OTHER], res[2 * N_OTHER], res[2 * N_OTHER + 1], res[-1]


def _exchange_wait(sems, s_thru, land_thru, afters, name):
    def body(s_ref, land_ref, *rest):
        for cp in _exchange_copies(s_ref, land_ref, rest[:2 * N_OTHER]):
            cp.wait_send()
            cp.wait_recv()

    hbm = pltpu.HBM(s_thru.shape, s_thru.dtype)
    return pl.pallas_call(
        body, name=name, out_shape=(hbm, hbm),
        in_specs=(HBM_SPEC, HBM_SPEC) + (SEM_SPEC,) * (2 * N_OTHER) + (pl.BlockSpec(memory_space=pl.ANY),) * len(afters),
        out_specs=(HBM_SPEC, HBM_SPEC), input_output_aliases={0: 0, 1: 1},
        compiler_params=pltpu.CompilerParams(has_side_effects=pltpu.SideEffectType.DATAFLOW_SIDE_EFFECTING),
    )(s_thru, land_thru, *sems, *afters)


def _gather_copies(block_ref, buf_ref, sems):
    x, y, c = _mesh_pos()
    return [pltpu.make_async_remote_copy(
        src_ref=block_ref, dst_ref=buf_ref.at[4 * x + 2 * y + c], send_sem=sems[j], recv_sem=sems[N_OTHER + j],
        device_id=(*chip, c), device_id_type=MESH) for j, chip in enumerate(_other_chips(x, y))]


def _gather_start(blocks, bufs, afters, name):
    n = len(blocks)
    per = 2 * N_OTHER

    def body(*refs):
        ins, outs = refs[:2 * n], refs[2 * n + len(afters):]
        for a in range(n):
            for cp in _gather_copies(ins[a], ins[n + a], outs[a * per:(a + 1) * per]):
                cp.start()
        outs[-1][...] = jnp.zeros_like(outs[-1])

    hbm = [pltpu.HBM(z.shape, z.dtype) for z in list(blocks) + list(bufs)]
    res = pl.pallas_call(
        body, name=name,
        out_shape=(pltpu.SemaphoreType.DMA(()),) * (n * per) + tuple(hbm) + (_sds((SUBLANES, LANE), F32),),
        in_specs=(HBM_SPEC,) * (2 * n) + (pl.BlockSpec(memory_space=pl.ANY),) * len(afters),
        out_specs=(SEM_SPEC,) * (n * per) + (HBM_SPEC,) * (2 * n) + (pl.BlockSpec(memory_space=pltpu.VMEM),),
        input_output_aliases={k: n * per + k for k in range(2 * n)},
        compiler_params=pltpu.CompilerParams(has_side_effects=pltpu.SideEffectType.DATAFLOW_SIDE_EFFECTING),
    )(*[pltpu.with_memory_space_constraint(z, pltpu.HBM) for z in list(blocks) + list(bufs)], *afters)
    parts = [(res[a * per:(a + 1) * per], res[n * per + a], res[n * per + n + a]) for a in range(n)]
    return parts, res[-1]


def _gather_wait(part, afters, name):
    sems, block, buf = part

    def body(block_ref, buf_ref, *rest):
        for cp in _gather_copies(block_ref, buf_ref, rest[:2 * N_OTHER]):
            cp.wait_send()
            cp.wait_recv()

    return pl.pallas_call(
        body, name=name, out_shape=(pltpu.HBM(block.shape, block.dtype), pltpu.HBM(buf.shape, buf.dtype)),
        in_specs=(HBM_SPEC, HBM_SPEC) + (SEM_SPEC,) * (2 * N_OTHER) + (pl.BlockSpec(memory_space=pl.ANY),) * len(afters),
        out_specs=(HBM_SPEC, HBM_SPEC), input_output_aliases={0: 0, 1: 1},
        compiler_params=pltpu.CompilerParams(has_side_effects=pltpu.SideEffectType.DATAFLOW_SIDE_EFFECTING),
    )(block, buf, *sems, *afters)[1]


def _comm_only(comms, name):
    return _call(lambda: None, (), name=name, grid=(), in_specs=[], out_specs=[], out_shape=[], sem=(), comms=comms)[1]


def _allgather8(arrays, name):
    return _comm_only([_plan_allgather8(arrays)], name)[0]


def _plan_allgather8(arrays):
    n = len(arrays)

    def parts(ins, outs, sems):
        send_sems, recv_sems, local_sems = sems
        x, y, c = _mesh_pos()
        me, sibling = (x, y, c), (x, y, 1 - c)
        chips = _other_chips(x, y)

        def copy(a, k, block, to, src=None):
            dst = outs[a].at[4 * block[0] + 2 * block[1] + block[2]]
            return pltpu.make_async_remote_copy(
                src_ref=dst if src is None else src, dst_ref=dst, send_sem=send_sems.at[7 * a + k],
                recv_sem=recv_sems.at[7 * a + k], device_id=to, device_id_type=MESH)

        mine = [pltpu.make_async_copy(ins[a], outs[a].at[4 * x + 2 * y + c], local_sems.at[a]) for a in range(n)]
        first = []
        for a in range(n):
            first.append(copy(a, 0, me, sibling, src=ins[a]))
            first += [copy(a, 1 + j, me, (*chip, c), src=ins[a]) for j, chip in enumerate(chips)]
        return copy, mine, first, me, sibling, chips, c

    def start(ins, outs, sems):
        _, mine, first, *_ = parts(ins, outs, sems)
        for cp in mine + first:
            cp.start()

    def finish(ins, outs, sems):
        copy, mine, first, me, sibling, chips, c = parts(ins, outs, sems)
        passed = []
        for j, chip in enumerate(chips):
            for a in range(n):
                copy(a, 1 + j, (*chip, c), me).wait_recv()
                fwd = copy(a, 4 + j, (*chip, c), sibling)
                fwd.start()
                passed.append(fwd)
        for a in range(n):
            copy(a, 0, sibling, me).wait_recv()
            for j, chip in enumerate(chips):
                copy(a, 4 + j, (*chip, 1 - c), me).wait_recv()
        for cp in first + passed:
            cp.wait_send()
        for cp in mine:
            cp.wait()

    sems = [pltpu.SemaphoreType.DMA((7 * n,)), pltpu.SemaphoreType.DMA((7 * n,)), pltpu.SemaphoreType.DMA((n,))]
    return _Comm(list(arrays), [_sds((N_DEV,) + a.shape, a.dtype) for a in arrays], sems, start, finish)


def _pair_sum(g, q, core, name, chip_major=False):
    rows, cols = g.shape[2:]
    tr = _row_tile(rows)

    def body(core_ref, g_ref, q_ref, o_ref):
        o_ref[...] = (g_ref[...] + q_ref[...]).astype(BF16)

    blk = pl.BlockSpec((None, tr, cols), lambda k, i, core_ref: (k, i, 0))
    if chip_major:
        own = pl.BlockSpec((None, None, tr, cols), lambda k, i, core_ref: (k, core_ref[0], i, 0))
    else:
        own = pl.BlockSpec((None, None, tr, cols), lambda k, i, core_ref: (core_ref[0], k, i, 0))
    return pl.pallas_call(
        body, name=name,
        grid_spec=pltpu.PrefetchScalarGridSpec(num_scalar_prefetch=1, grid=(N_CHIPS, rows // tr), in_specs=[own, blk], out_specs=blk),
        out_shape=_sds((N_CHIPS, rows, cols), BF16), compiler_params=_params("parallel", "parallel"),
    )(core, g, q)


def _sum_chips(own, landed, chip, name):
    _, rows, cols = own.shape
    tr = _row_tile(rows)

    def body(chip_ref, own_ref, a_ref, b_ref, c_ref, o_ref):
        acc = own_ref[...].astype(F32) + a_ref[...].astype(F32)
        o_ref[...] = (acc + b_ref[...].astype(F32)) + c_ref[...].astype(F32)

    blk = lambda flip: pl.BlockSpec((None, tr, cols), lambda i, chip_ref: (jnp.bitwise_xor(chip_ref[0], flip), i, 0))
    return pl.pallas_call(
        body, name=name,
        grid_spec=pltpu.PrefetchScalarGridSpec(num_scalar_prefetch=1, grid=(rows // tr,), in_specs=[blk(0), blk(1), blk(2), blk(3)],
                                               out_specs=pl.BlockSpec((tr, cols), lambda i, chip_ref: (i, 0))),
        out_shape=_sds((rows, cols), F32), compiler_params=_params("parallel"),
    )(chip, own, landed, landed, landed)


SUBLANES = 8


def _tile_rows(n_elems):
    return -(-n_elems // (SUBLANES * LANE)) * SUBLANES


SMALL_ITEMS = (("b_ada", N_MOD * D_MODEL), ("pre_w_mix", D_MODEL), ("post_w_mix", D_MODEL), ("pre_w_mlp", D_MODEL),
               ("post_w_mlp", D_MODEL), ("attn_out_w", ATT_WIDTH), ("hg_norm_w", HG_HEAD_DIM), ("attn_sinks", ATT_Q_HEADS),
               ("lb_0", HG_WIDTH), ("lb_1", HG_WIDTH))
SMALL_AT = {}
for _name, _size in SMALL_ITEMS:
    SMALL_AT[_name] = (sum(r for _, r in SMALL_AT.values()), _tile_rows(_size))
SMALL_ROWS = sum(r for _, r in SMALL_AT.values())
MOD_ROWS = SMALL_AT["b_ada"][1]
PLAIN_ROWS = SMALL_AT["lb_0"][0] - MOD_ROWS
LB_ROWS = SMALL_AT["lb_0"][1]


def _rows(a, nrows=None):
    flat = a.reshape(-1)
    nrows = _tile_rows(flat.shape[0]) if nrows is None else nrows
    return jnp.pad(flat, (0, nrows * LANE - flat.shape[0])).reshape(nrows, LANE)


def _pack_small(vals):
    vals = dict(vals, lb_0=vals["lb_table"][0], lb_1=vals["lb_table"][1])
    return jnp.concatenate([_rows(vals[name], SMALL_AT[name][1]) for name, _ in SMALL_ITEMS], axis=0)


def _unpack_small(p):
    def item(name, shape):
        first = SMALL_AT[name][0]
        size = shape[0] * shape[1]
        return p[first:first + SMALL_AT[name][1]].reshape(-1)[:size].reshape(shape)

    out = {name: item(name, (1, size)) for name, size in SMALL_ITEMS if not name.startswith("lb_")}
    out["lb_table"] = jnp.concatenate([item("lb_0", (1, HG_WIDTH)), item("lb_1", (1, HG_WIDTH))], axis=0)
    return out


def _pack_partials(dmod, plain, d_lb, loss_row):
    return jnp.concatenate([_rows(dmod, dmod.shape[0] * MOD_ROWS)] + [_rows(g) for g in plain] + [_rows(d_lb), _rows(loss_row)], axis=0)


def _small_update(packs, w, m, v, n_seq):
    mod_end = n_seq * MOD_ROWS
    lb_at = mod_end + PLAIN_ROWS
    t0, t1 = SMALL_AT["lb_0"][0], SMALL_AT["lb_1"][0]

    def body(p_ref, w_ref, m_ref, v_ref, g_ref, dl_ref, nm_ref, nv_ref, loss_ref):
        tot = p_ref[0]
        for d in range(1, N_DEV):
            tot = tot + p_ref[d]
        wv = w_ref[...]
        p1 = _sigmoid(wv[t1:t1 + LB_ROWS] - wv[t0:t0 + LB_ROWS])
        s = tot[lb_at:lb_at + LB_ROWS] * p1 * (1.0 - p1)
        g_bias = tot[0:MOD_ROWS]
        for q in range(1, n_seq):
            g_bias = g_bias + tot[q * MOD_ROWS:(q + 1) * MOD_ROWS]
        g = jnp.concatenate([g_bias, tot[mod_end:lb_at], -s, s], axis=0)
        g_ref[...] = g
        dl_ref[...], nm_ref[...], nv_ref[...] = _adamw_math(g, wv, m_ref[...], v_ref[...])
        loss_ref[...] = tot[lb_at + LB_ROWS:lb_at + LB_ROWS + SUBLANES]

    shp = _sds((SMALL_ROWS, LANE), F32)
    return pl.pallas_call(body, name="small_update", out_shape=[shp] * 4 + [_sds((SUBLANES, LANE), F32)],
                          compiler_params=_params())(packs, w, m, v)


def kernel(x, c, w_ada, b_ada, pre_w_mix, w_in, attn_sinks, attn_out_w, lb_table, hg_norm_w, w_out, post_w_mix, pre_w_mlp, w_up, w_down, post_w_mlp, loss_target, m_w_ada, m_b_ada, m_pre_w_mix, m_w_in, m_attn_sinks, m_attn_out_w, m_lb_table, m_hg_norm_w, m_w_out, m_post_w_mix, m_pre_w_mlp, m_w_up, m_w_down, m_post_w_mlp, v_w_ada, v_b_ada, v_pre_w_mix, v_w_in, v_attn_sinks, v_attn_out_w, v_lb_table, v_hg_norm_w, v_w_out, v_post_w_mix, v_pre_w_mlp, v_w_up, v_w_down, v_post_w_mlp):
    xi, yi, ci = _mesh_pos()
    chip = 2 * xi + yi
    dev = 2 * chip + ci
    bsz, seq, _ = x.shape
    ntok = bsz * seq
    ada_cols = w_ada.shape[2]
    core = jnp.reshape(ci, (1,)).astype(jnp.int32)
    chip_idx = jnp.reshape(chip, (1,)).astype(jnp.int32)
    flat = lambda a: a.reshape(ntok, a.shape[-1])
    unflat = lambda a: a.reshape(bsz, seq, a.shape[-1])
    tables = _rope_tables(seq)
    biases, chunk_masks = _band_biases(), _block_masks()

    def row_half(w):
        rows = w.shape[1] // 2
        return lax.dynamic_slice_in_dim(w[0], ci * rows, rows, axis=0).astype(BF16)

    def gather_buffer(w):
        rows, cols = w.shape[1] // 2, w.shape[2]
        own = w[0].astype(BF16).reshape(2, rows, cols)
        return lax.dynamic_update_slice(lax.empty((N_DEV, rows, cols), BF16), own, (2 * chip, 0, 0))

    w_in_t, m_in_t, v_in_t = [jnp.transpose(a[0])[None] for a in (w_in, m_w_in, v_w_in)]
    c_g, in_g = _allgather8([c, row_half(w_in_t)], "gather_first")
    c_all = c_g.reshape(N_DEV * bsz, D_MODEL)
    w_in_full = in_g.reshape(IN_COLS, D_MODEL)

    b_cols = lax.dynamic_slice_in_dim(b_ada, chip * ada_cols, ada_cols, axis=1)
    mod_part = _ada_fwd(c_all, w_ada[0], b_cols)
    half_rows = mod_part.shape[0] // 2
    (mod_g,) = _allgather8([lax.dynamic_slice_in_dim(mod_part, ci * half_rows, half_rows, axis=0)], "gather_mod")
    mod_all = mod_g.reshape(N_CHIPS, 2, half_rows, ada_cols).transpose(1, 2, 0, 3).reshape(N_DEV * bsz, N_MOD * D_MODEL)
    mod = lax.dynamic_slice_in_dim(mod_all, dev * bsz, bsz, axis=0)
    sh1, sc1, g1, sh2, sc2, g2 = [mod[:, i * D_MODEL:(i + 1) * D_MODEL].reshape(bsz, 1, D_MODEL) for i in range(N_MOD)]

    weights = (w_out, w_up, w_down)
    (out_part, up_part, down_part), started = _gather_start(
        [row_half(w) for w in weights], [gather_buffer(w) for w in weights], [mod_g], "gather_weights_start")

    h1, proj, qh, kh, vh = _in_proj_fused(x, pre_w_mix, sc1 + started[0:1, 0:1], sh1, w_in_full, tables)
    out_g = _gather_wait(out_part, [proj], "gather_out_wait")
    (attn_raw, cat, lse), ((out_g,),) = _attn_fwd(qh, kh, vh, attn_sinks, attn_out_w, biases, comms=[_plan_pair_forward([out_g])])
    up_g = _gather_wait(up_part, [attn_raw], "gather_up_wait")
    (o_raw, cat, states), ((up_g,),) = _hgrn_fwd(proj, lb_table, hg_norm_w, cat, chunk_masks, comms=[_plan_pair_forward([up_g])])
    down_g = _gather_wait(down_part, [o_raw], "gather_down_wait")
    w_out_full = out_g.reshape(D_MODEL, D_MODEL)
    w_up4 = up_g.reshape(N_CHIPS, D_MODEL, D_MODEL)
    mix, x1, h2 = _out_proj_fused(cat, w_out_full, x, post_w_mix, g1, pre_w_mlp, sc2, sh2)
    big_tm = min(ntok, 2048)
    up_spec = pl.BlockSpec((None, D_MODEL, D_MODEL), lambda i, j: (j, 0, 0))
    r, ((down_g,),) = _mm(flat(h2), w_up4, name="up_proj", out_dtype=BF16, tm=big_tm, tn=D_MODEL, n_out=D_FF, b_spec=up_spec,
                          epi=lambda acc: jnp.maximum(acc, 0.0), comms=[_plan_pair_forward([down_g])])
    w_down_full = down_g.reshape(D_FF, D_MODEL)
    square = lambda t: t * t
    loss_row, dy, dd, dg2, d_post_mlp = _down_proj_fused(unflat(r), w_down_full, x1, post_w_mlp, g2, loss_target)

    dpre = _mm(flat(dd), w_down_full, name="down_bwd", out_dtype=BF16, trans_b=True, tm=big_tm, tn=D_MODEL, extra=(r,),
               epi=lambda acc, rt: acc * (2.0 * rt.astype(F32)))
    half_rows = D_MODEL // 2
    g_down = _mm_tn(r, flat(dd), name="down_wgrad", tk=half_rows, tn=D_MODEL, a_fn=square,
                    out_shape=_sds((2, N_CHIPS, half_rows, D_MODEL), F32),
                    out_spec=pl.BlockSpec((None, None, half_rows, D_MODEL), lambda i, j: (i % 2, i // 2, 0, 0)))
    (dx1, dmix, dsc2, dsh2, dg1, d_pre_mlp, d_post_mix), ((q_down,),) = _up_bwd_fused(
        unflat(dpre), w_up4, dy, x1, mix, pre_w_mlp, sc2, post_w_mix, g1, comms=[_plan_pair([g_down], True)])
    g_up = _mm_tn(flat(h2), dpre, name="up_wgrad", tk=D_MODEL, tn=half_rows,
                  out_shape=_sds((2, N_CHIPS, half_rows, D_MODEL), F32),
                  out_spec=pl.BlockSpec((2, None, half_rows, half_rows), lambda i, j: (0, j // 2, 0, j % 2)))
    s_down = _pair_sum(g_down, q_down, core, "pair_sum_down")

    dcat, ((q_up,),) = _mm(flat(dmix), w_out_full, name="out_bwd", out_dtype=F32, trans_b=True, comms=[_plan_pair([g_up], True)])
    dcat = unflat(dcat)
    s_up = _pair_sum(g_up, q_up, core, "pair_sum_up")
    out_rows = D_MODEL // N_CHIPS
    g_out = _mm_tn(flat(cat), flat(dmix), name="out_wgrad", tk=2 * out_rows, tn=half_rows,
                   out_shape=_sds((2, N_CHIPS, out_rows, half_rows), F32),
                   out_spec=pl.BlockSpec((None, 2, out_rows, half_rows), lambda i, j: (j, i, 0, 0)))
    (dproj_rec, d_lb, d_hg_norm), ((x_down,), (q_out,)) = _hgrn_bwd(
        dcat, proj, o_raw, states, lb_table, hg_norm_w, chunk_masks, comms=[_plan_chip_exchange([s_down]), _plan_pair([g_out], True)])
    half_down = _sum_chips(s_down, x_down, chip_idx, "sum_chips_down")
    s_out = _pair_sum(g_out, q_out, core, "pair_sum_out")
    (dproj, d_attn_out, d_sinks), ((their_down,), (x_up, x_out)) = _attn_bwd(
        dcat, attn_raw, attn_out_w, qh, kh, vh, lse, attn_sinks, tables, [bias.T for bias in biases], dproj_rec,
        comms=[_plan_pair([half_down], False), _plan_chip_exchange([s_up, s_out])])
    half_up = _sum_chips(s_up, x_up, chip_idx, "sum_chips_up")
    half_out = _sum_chips(s_out, x_out, chip_idx, "sum_chips_out")
    dproj = flat(dproj)
    in_rows = IN_COLS // N_CHIPS // 2
    g_in = _mm_tn(dproj, flat(h1), name="in_wgrad", tk=2 * LANE, tn=D_MODEL).reshape(N_CHIPS, 2, in_rows, D_MODEL)
    dh1, ((q_in,), (their_up, their_out)) = _mm(
        dproj, w_in_full, name="in_bwd", out_dtype=F32,
        comms=[_plan_pair([g_in], "chip_major"), _plan_pair([half_up, half_out], False)])
    s_in = _pair_sum(g_in, q_in, core, "pair_sum_in", chip_major=True)
    in_sems, s_in, in_landing, started = _exchange_start(s_in, "exchange_in_start")
    grad_x, dsc1, dsh1, d_pre_mix = _norm1_bwd(unflat(dh1), dx1, x, pre_w_mix + started[0:1, 0:1], sc1)

    dmod = jnp.concatenate([dsh1, dsc1, dg1, dsh2, dsc2, dg2], axis=-1).reshape(bsz, N_MOD * D_MODEL)
    pack = _pack_partials(dmod, [d_pre_mix, d_post_mix, d_pre_mlp, d_post_mlp, d_attn_out, d_hg_norm, d_sinks], d_lb, loss_row)
    ((packs,),) = _comm_only([_plan_allgather8([pack])], "gather_small")
    w_small = dict(b_ada=b_ada, pre_w_mix=pre_w_mix, post_w_mix=post_w_mix, pre_w_mlp=pre_w_mlp, post_w_mlp=post_w_mlp,
                   attn_out_w=attn_out_w, hg_norm_w=hg_norm_w, attn_sinks=attn_sinks, lb_table=lb_table)
    m_small = dict(b_ada=m_b_ada, pre_w_mix=m_pre_w_mix, post_w_mix=m_post_w_mix, pre_w_mlp=m_pre_w_mlp, post_w_mlp=m_post_w_mlp,
                   attn_out_w=m_attn_out_w, hg_norm_w=m_hg_norm_w, attn_sinks=m_attn_sinks, lb_table=m_lb_table)
    v_small = dict(b_ada=v_b_ada, pre_w_mix=v_pre_w_mix, post_w_mix=v_post_w_mix, pre_w_mlp=v_pre_w_mlp, post_w_mlp=v_post_w_mlp,
                   attn_out_w=v_attn_out_w, hg_norm_w=v_hg_norm_w, attn_sinks=v_attn_sinks, lb_table=v_lb_table)
    *small_packed, loss_rows = _small_update(packs, _pack_small(w_small), _pack_small(m_small), _pack_small(v_small), bsz)
    small_out = [_unpack_small(p) for p in small_packed]
    loss = loss_rows[0, 0]

    dmod_all = packs[:, :bsz * MOD_ROWS, :].reshape(N_DEV * bsz, N_MOD * D_MODEL)
    dmod_cols = lax.dynamic_slice_in_dim(dmod_all, chip * ada_cols, ada_cols, axis=1)
    ada_out = _ada_bwd_adamw(c_all, dmod_cols, w_ada[0], m_w_ada[0], v_w_ada[0])

    s_in, x_in = _exchange_wait(in_sems, s_in, in_landing, [grad_x, ada_out[0]], "exchange_in_wait")
    half_in = _sum_chips(s_in, x_in, chip_idx, "sum_chips_in")
    ((their_in,),) = _comm_only([_plan_pair([half_in], False)], "pair_swap_in")
    big = dict(
        w_in=tuple(jnp.transpose(a) for a in _adamw_halves(half_in, their_in, core, w_in_t[0], m_in_t[0], v_in_t[0], axis=0,
                                                           name="adamw_in")),
        w_up=tuple(_adamw_halves(half_up, their_up, core, w_up[0], m_w_up[0], v_w_up[0], axis=0, name="adamw_up")),
        w_out=tuple(_adamw_halves(half_out, their_out, core, w_out[0], m_w_out[0], v_w_out[0], axis=1, name="adamw_out")),
        w_down=tuple(_adamw_halves(half_down, their_down, core, w_down[0], m_w_down[0], v_w_down[0], axis=0, name="adamw_down")),
        w_ada=tuple(ada_out),
    )
    order = ("w_ada", "b_ada", "pre_w_mix", "w_in", "attn_sinks", "attn_out_w", "lb_table", "hg_norm_w", "w_out", "post_w_mix",
             "pre_w_mlp", "w_up", "w_down", "post_w_mlp")
    outs = [loss, grad_x]
    for kind in range(4):
        for nm in order:
            outs.append(big[nm][kind][None] if nm in big else small_out[kind][nm])
    return tuple(outs)
```

```python
import jax
import jax.numpy as jnp
from jax import lax
from jax.experimental import pallas as pl
from jax.experimental.pallas import tpu as pltpu

F32 = jnp.float32
BF16 = jnp.bfloat16

D_MODEL = 1024
ATT_WIDTH = 512
ATT_HEAD_DIM = 64
ATT_Q_HEADS = 8
ATT_KV_HEADS = 2
ATT_GROUP = ATT_Q_HEADS // ATT_KV_HEADS
ATT_KV_COLS = ATT_KV_HEADS * ATT_HEAD_DIM
WINDOW = 128
ROPE_DIM = 16
ROPE_THETA = 500000.0
HG_WIDTH = 512
MIX_WIDTH = ATT_WIDTH + HG_WIDTH
HG_HEAD_DIM = 128
HG_HEADS = 4
HG_CHUNK = 32
IN_COLS = ATT_WIDTH + 2 * ATT_KV_COLS + 4 * HG_WIDTH
ATT_COLS = ATT_WIDTH + 2 * ATT_KV_COLS
D_FF = 4 * D_MODEL
N_MOD = 6
EPS = 1e-6
ATT_SCALE = ATT_HEAD_DIM ** -0.5

ADAM_LR = 0.001
ADAM_B1 = 0.9
ADAM_B2 = 0.999
ADAM_EPS = 1e-08
ADAM_WD = 0.01
ADAM_STEP = 10

N_CHIPS = 4
N_DEV = 8
LANE = 128
VMEM_LIMIT = 48 * 1024 * 1024
VMEM_LIMIT_BIG = 58 * 1024 * 1024
MESH = pl.DeviceIdType.MESH

NT_DIMS = (((1,), (1,)), ((), ()))
TN_DIMS = (((0,), (0,)), ((), ()))


def _sds(shape, dtype):
    return jax.ShapeDtypeStruct(tuple(shape), dtype)


def _params(*sem, vmem_limit=None):
    return pltpu.CompilerParams(dimension_semantics=sem, vmem_limit_bytes=VMEM_LIMIT if vmem_limit is None else vmem_limit)


def _sigmoid(x):
    return 1.0 / (1.0 + jnp.exp(-x))


def _dot(a, b, dims=None):
    a, b = a.astype(BF16), b.astype(BF16)
    if dims is None:
        return jnp.dot(a, b, preferred_element_type=F32)
    return lax.dot_general(a, b, dims, preferred_element_type=F32)


def _rms_fwd(x, w):
    rstd = lax.rsqrt(jnp.mean(x * x, axis=-1, keepdims=True) + EPS)
    xh = x * rstd
    return xh * w, xh, rstd


def _rms_bwd(dy, xh, rstd, w):
    dxh = dy * w
    dx = rstd * (dxh - xh * jnp.mean(dxh * xh, axis=-1, keepdims=True))
    return dx, dy * xh


def _colsum(x):
    return jnp.sum(x, axis=0, keepdims=True)


def _rms_hat(x):
    rstd = lax.rsqrt(jnp.mean(x * x, axis=-1, keepdims=True) + EPS)
    return x * rstd, rstd


def _rms_bwd_gain(dy, gain, xh, rstd):
    dxh = dy * gain
    dx = rstd * (dxh - xh * jnp.mean(dxh * xh, axis=-1, keepdims=True))
    return dx, _colsum(dy * xh)


def _row_tile(rows, cap=256):
    return max(t for t in range(16, cap + 1, 16) if rows % t == 0)


HBM_SPEC = pl.BlockSpec(memory_space=pltpu.HBM)


def _mesh_pos():
    return lax.axis_index("x"), lax.axis_index("y"), lax.axis_index("c")


class _Comm:
    def __init__(self, ins, outs, sems, start, finish, aliases=()):
        self.ins, self.outs, self.sems = list(ins), list(outs), list(sems)
        self.start, self.finish, self.aliases = start, finish, tuple(aliases)


def _call(body, args, *, name, grid, in_specs, out_specs, out_shape, sem, scratch_shapes=(), comms=(), aliases=None,
          vmem_limit=None):
    scratch_shapes = list(scratch_shapes)
    if not comms:
        return pl.pallas_call(body, name=name, grid=grid, in_specs=in_specs, out_specs=out_specs, out_shape=out_shape,
                              input_output_aliases=dict(aliases or {}), scratch_shapes=scratch_shapes,
                              compiler_params=_params(*sem, vmem_limit=vmem_limit))(*args)
    single = not isinstance(out_shape, (list, tuple))
    out_specs_l = [out_specs] if single else list(out_specs)
    out_shape_l = [out_shape] if single else list(out_shape)
    n_in, n_out, n_scr = len(in_specs), len(out_shape_l), len(scratch_shapes)
    n_ci = [len(cm.ins) for cm in comms]
    n_co = [len(cm.outs) for cm in comms]
    n_cs = [len(cm.sems) for cm in comms]
    aliases = dict(aliases or {})
    for k, cm in enumerate(comms):
        for i, o in cm.aliases:
            aliases[n_in + sum(n_ci[:k]) + i] = n_out + sum(n_co[:k]) + o

    def fused(*refs):
        pos = [0]

        def take(n):
            part = refs[pos[0]:pos[0] + n]
            pos[0] += n
            return part

        ins = take(n_in)
        c_ins = [take(n) for n in n_ci]
        outs = take(n_out)
        c_outs = [take(n) for n in n_co]
        scr = take(n_scr)
        c_sems = [take(n) for n in n_cs]
        first, last = True, True
        for d, size in enumerate(grid):
            first = jnp.logical_and(first, pl.program_id(d) == 0)
            last = jnp.logical_and(last, pl.program_id(d) == size - 1)

        def run(which):
            for cm, ci, co, cs in zip(comms, c_ins, c_outs, c_sems):
                getattr(cm, which)(ci, co, cs)

        if grid:
            pl.when(first)(lambda: run("start"))
        else:
            run("start")
        body(*ins, *outs, *scr)
        if grid:
            pl.when(last)(lambda: run("finish"))
        else:
            run("finish")

    res = pl.pallas_call(
        fused, name=name, grid=grid, in_specs=list(in_specs) + [HBM_SPEC] * sum(n_ci),
        out_specs=out_specs_l + [HBM_SPEC] * sum(n_co), out_shape=out_shape_l + [s for cm in comms for s in cm.outs],
        input_output_aliases=aliases, scratch_shapes=scratch_shapes + [s for cm in comms for s in cm.sems],
        compiler_params=_params(*["arbitrary"] * len(grid), vmem_limit=vmem_limit),
    )(*args, *[a for cm in comms for a in cm.ins])
    main = res[:n_out]
    extra, at = [], n_out
    for n in n_co:
        extra.append(list(res[at:at + n]))
        at += n
    return (main[0] if single else list(main)), extra


def _mm(a, b, *, name, out_dtype, trans_b=False, tm=512, tn=None, extra=(), epi=None, b_spec=None, n_out=None, comms=()):
    m_total, k_total = a.shape
    if n_out is None:
        n_out = b.shape[0] if trans_b else b.shape[1]
    tn = n_out if tn is None else tn
    grid = (m_total // tm, n_out // tn)
    dims = NT_DIMS if trans_b else None

    def body(*refs):
        a_ref, b_ref = refs[0], refs[1]
        extra_refs = refs[2:2 + len(extra)]
        o_ref = refs[2 + len(extra)]
        acc = _dot(a_ref[...], b_ref[...], dims)
        if epi is not None:
            acc = epi(acc, *[r[...] for r in extra_refs])
        o_ref[...] = acc.astype(out_dtype)

    if b_spec is None:
        if trans_b:
            b_spec = pl.BlockSpec((tn, k_total), lambda i, j: (j, 0))
        else:
            b_spec = pl.BlockSpec((k_total, tn), lambda i, j: (0, j))
    in_specs = [pl.BlockSpec((tm, k_total), lambda i, j: (i, 0)), b_spec]
    in_specs += [pl.BlockSpec((tm, tn), lambda i, j: (i, j)) for _ in extra]
    return _call(
        body, (a, b, *extra), name=name, grid=grid, in_specs=in_specs,
        out_specs=pl.BlockSpec((tm, tn), lambda i, j: (i, j)),
        out_shape=_sds((m_total, n_out), out_dtype),
        sem=("parallel", "parallel"), comms=comms)


def _mm_tn(a, b, *, name, tk, tn, a_fn=None, out_shape=None, out_spec=None, comms=()):
    m_total, k_total = a.shape
    n_total = b.shape[1]
    grid = (k_total // tk, n_total // tn)

    def body(a_ref, b_ref, o_ref):
        av = a_ref[...]
        part = _dot(av if a_fn is None else a_fn(av), b_ref[...], TN_DIMS)
        o_ref[...] = part.reshape(o_ref.shape)

    if out_shape is None:
        out_shape = _sds((k_total, n_total), F32)
        out_spec = pl.BlockSpec((tk, tn), lambda i, j: (i, j))
    return _call(
        body, (a, b), name=name, grid=grid,
        in_specs=[pl.BlockSpec((m_total, tk), lambda i, j: (0, i)), pl.BlockSpec((m_total, tn), lambda i, j: (0, j))],
        out_specs=out_spec, out_shape=out_shape, sem=("parallel", "parallel"), comms=comms)


def _ada_fwd(c_all, w_shard, b_shard):
    nb, ncol = c_all.shape[0], w_shard.shape[1]
    tn = 512

    def body(c_ref, w_ref, b_ref, o_ref):
        c = c_ref[...]
        o_ref[...] = _dot(c * _sigmoid(c), w_ref[...]) + b_ref[...]

    return pl.pallas_call(
        body, name="ada_fwd", grid=(ncol // tn,),
        in_specs=[pl.BlockSpec((nb, D_MODEL), lambda j: (0, 0)), pl.BlockSpec((D_MODEL, tn), lambda j: (0, j)),
                  pl.BlockSpec((1, tn), lambda j: (0, j))],
        out_specs=pl.BlockSpec((nb, tn), lambda j: (0, j)), out_shape=_sds((nb, ncol), F32),
        compiler_params=_params("parallel"),
    )(c_all, w_shard, b_shard)


def _adamw_math(g, w, m, v):
    m = ADAM_B1 * m + (1.0 - ADAM_B1) * g
    v = ADAM_B2 * v + (1.0 - ADAM_B2) * (g * g)
    m_hat = m / (1.0 - ADAM_B1 ** ADAM_STEP)
    v_hat = v / (1.0 - ADAM_B2 ** ADAM_STEP)
    delta = -ADAM_LR * (m_hat / (jnp.sqrt(v_hat) + ADAM_EPS) + ADAM_WD * w)
    return delta, m, v


def _ada_bwd_adamw(c_all, dmod_cols, w, m, v):
    nb, ncol = dmod_cols.shape
    tn = 256

    def body(c_ref, d_ref, w_ref, m_ref, v_ref, g_ref, dl_ref, nm_ref, nv_ref):
        c = c_ref[...]
        g = _dot(c * _sigmoid(c), d_ref[...], TN_DIMS)
        g_ref[...] = g
        dl_ref[...], nm_ref[...], nv_ref[...] = _adamw_math(g, w_ref[...], m_ref[...], v_ref[...])

    col = pl.BlockSpec((D_MODEL, tn), lambda j: (0, j))
    shp = _sds((D_MODEL, ncol), F32)
    return pl.pallas_call(
        body, name="ada_bwd_adamw", grid=(ncol // tn,),
        in_specs=[pl.BlockSpec((nb, D_MODEL), lambda j: (0, 0)), pl.BlockSpec((nb, tn), lambda j: (0, j)), col, col, col],
        out_specs=[col, col, col, col], out_shape=[shp, shp, shp, shp],
        compiler_params=_params("parallel"),
    )(c_all, dmod_cols, w, m, v)


def _adamw_halves(own, theirs, core, w, m, v, *, axis, name):
    r2, c2 = own.shape
    tr = _row_tile(r2)
    nt = r2 // tr

    def body(core_ref, own_ref, their_ref, w_ref, m_ref, v_ref, g_ref, dl_ref, nm_ref, nv_ref):
        g = jnp.where(pl.program_id(0) == core_ref[0], own_ref[...], their_ref[...])
        g_ref[...] = g
        dl_ref[...], nm_ref[...], nv_ref[...] = _adamw_math(g, w_ref[...], m_ref[...], v_ref[...])

    if axis == 0:
        full = pl.BlockSpec((tr, c2), lambda h, i, core_ref: (h * nt + i, 0))
    else:
        full = pl.BlockSpec((tr, c2), lambda h, i, core_ref: (i, h))
    half = pl.BlockSpec((tr, c2), lambda h, i, core_ref: (i, 0))
    shp = _sds(w.shape, F32)
    return pl.pallas_call(
        body, name=name,
        grid_spec=pltpu.PrefetchScalarGridSpec(num_scalar_prefetch=1, grid=(2, nt), in_specs=[half, half, full, full, full],
                                               out_specs=[full] * 4),
        out_shape=[shp] * 4, compiler_params=_params("parallel", "parallel"),
    )(core, own, theirs, w, m, v)


def _tok_spec(tm, width=D_MODEL):
    return pl.BlockSpec((None, tm, width), lambda b, i: (b, i, 0))


def _row_spec(width=D_MODEL):
    return pl.BlockSpec((None, 1, width), lambda b, i: (b, 0, 0))


def _vec_spec(width=D_MODEL):
    return pl.BlockSpec((1, width), lambda b, i: (0, 0))


class _RowsOf:
    def __init__(self, ref, first, count):
        self.ref, self.rows = ref, slice(first, first + count)

    def __getitem__(self, idx):
        return self.ref[self.rows, :]

    def __setitem__(self, idx, value):
        self.ref[self.rows, :] = value


def _mm_rows(a, b, *, name, tm, extra, extra_specs, out_specs, out_shape, epi, pro=None, trans_b=False, b_chunks=1, comms=(),
             parts=1, zero_per_seq=(), zero_once=(), vmem_limit=None):
    bsz, seq, k_total = a.shape
    kc = k_total // b_chunks
    dims = NT_DIMS if trans_b else None
    rows = tm // parts

    def body(*refs):
        a_ref, b_ref = refs[0], refs[1]
        ex, outs = refs[2:2 + len(extra)], refs[2 + len(extra):]
        if zero_per_seq:
            @pl.when(pl.program_id(1) == 0)
            def _():
                for k in zero_per_seq:
                    outs[k][...] = jnp.zeros_like(outs[k])
        if zero_once:
            @pl.when(jnp.logical_and(pl.program_id(0) == 0, pl.program_id(1) == 0))
            def _():
                for k in zero_once:
                    outs[k][...] = jnp.zeros_like(outs[k])

        def part_of(ref, p):
            tiled = len(ref.shape) == 2 and ref.shape[0] == tm
            return _RowsOf(ref, p * rows, rows) if tiled and parts > 1 else ref

        accs = []
        for p in range(parts):
            a_p, ex_p, outs_p = part_of(a_ref, p), [part_of(r, p) for r in ex], [part_of(r, p) for r in outs]
            if b_chunks == 1:
                accs.append(_dot(a_p[...] if pro is None else pro(a_p, ex_p, outs_p), b_ref[...], dims))
            else:
                acc = _dot(a_p[...][:, 0:kc], b_ref[0], NT_DIMS)
                for k in range(1, b_chunks):
                    acc = acc + _dot(a_p[...][:, k * kc:(k + 1) * kc], b_ref[k], NT_DIMS)
                accs.append(acc)
        for p in range(parts):
            epi(accs[p], [part_of(r, p) for r in ex], [part_of(r, p) for r in outs])

    b_spec = pl.BlockSpec(b.shape, lambda bb, i: (0,) * b.ndim)
    return _call(
        body, (a, b, *extra), name=name, grid=(bsz, seq // tm), in_specs=[_tok_spec(tm, k_total), b_spec, *extra_specs],
        out_specs=out_specs, out_shape=out_shape, sem=("arbitrary", "arbitrary"), comms=comms, vmem_limit=vmem_limit)


def _in_proj_fused(x, w, sc, sh, w_in_t, tables, comms=()):
    tm = 512
    bsz, seq, _ = x.shape
    half = ROPE_DIM // 2
    heads_per_slab = LANE // ATT_HEAD_DIM

    def pro(x_ref, ex, outs):
        y, _, _ = _rms_fwd(x_ref[...], ex[0][...])
        h = (y * (1.0 + ex[1][...]) + ex[2][...]).astype(BF16)
        outs[0][...] = h
        return h

    def epi(acc, ex, outs):
        c, u, d = ex[3][...], ex[4][...], ex[5][...]
        _, rec_ref, q_ref, k_ref, v_ref = outs
        for k in range(HG_SLABS):
            rec_ref[k] = acc[:, ATT_COLS + k * HG_WIDTH:ATT_COLS + (k + 1) * HG_WIDTH]

        def rope(z):
            return (z * c + pltpu.roll(z, half, 1) * u + pltpu.roll(z, LANE - half, 1) * d).astype(BF16)

        for s in range(ATT_WIDTH // LANE):
            slab = rope(acc[:, s * LANE:(s + 1) * LANE])
            for part in range(heads_per_slab):
                g, hh = divmod(s * heads_per_slab + part, ATT_GROUP)
                piece = slab[:, part * ATT_HEAD_DIM:(part + 1) * ATT_HEAD_DIM]
                for blk in range(tm // WINDOW):
                    q_ref[blk, g, hh * WINDOW:(hh + 1) * WINDOW, :] = piece[blk * WINDOW:(blk + 1) * WINDOW]
        rk = rope(acc[:, ATT_WIDTH:ATT_WIDTH + LANE])
        vv = acc[:, ATT_WIDTH + LANE:ATT_COLS].astype(BF16)
        for g in range(ATT_KV_HEADS):
            k_ref[g] = rk[:, g * ATT_HEAD_DIM:(g + 1) * ATT_HEAD_DIM]
            v_ref[g] = vv[:, g * ATT_HEAD_DIM:(g + 1) * ATT_HEAD_DIM]

    tab = pl.BlockSpec((tm, LANE), lambda b, i: (i, 0))
    kv_spec = pl.BlockSpec((None, ATT_KV_HEADS, tm, ATT_HEAD_DIM), lambda b, i: (b, 0, i, 0))
    kv_shape = _sds((bsz, ATT_KV_HEADS, seq, ATT_HEAD_DIM), BF16)
    q_spec = pl.BlockSpec((None, tm // WINDOW, ATT_KV_HEADS, GROUP_ROWS, ATT_HEAD_DIM), lambda b, i: (b, i, 0, 0, 0))
    return _mm_rows(x, w_in_t, name="in_proj", tm=tm, extra=(w, sc, sh, *tables),
                    extra_specs=[_vec_spec(), _row_spec(), _row_spec(), tab, tab, tab],
                    out_specs=[_tok_spec(tm), pl.BlockSpec((None, HG_SLABS, tm, HG_WIDTH), lambda b, i: (b, 0, i, 0)), q_spec,
                               kv_spec, kv_spec],
                    out_shape=[_sds(x.shape, BF16), _sds((bsz, HG_SLABS, seq, HG_WIDTH), F32),
                               _sds((bsz, seq // WINDOW, ATT_KV_HEADS, GROUP_ROWS, ATT_HEAD_DIM), BF16), kv_shape, kv_shape],
                    pro=pro, epi=epi, trans_b=True, comms=comms)


def _rope_tables(seq):
    half = ROPE_DIM // 2
    inv_freq = ROPE_THETA ** (-jnp.arange(0, ROPE_DIM, 2, dtype=F32) / ROPE_DIM)
    ang = jnp.arange(seq, dtype=F32)[:, None] * inv_freq[None, :]
    cos, sin = jnp.cos(ang), jnp.sin(ang)
    rest = ATT_HEAD_DIM - ROPE_DIM
    ones, zeros, zh = jnp.ones((seq, rest), F32), jnp.zeros((seq, rest), F32), jnp.zeros((seq, half), F32)
    reps = LANE // ATT_HEAD_DIM
    t_cos = jnp.tile(jnp.concatenate([cos, cos, ones], axis=1), (1, reps))
    t_up = jnp.tile(jnp.concatenate([zh, sin, zeros], axis=1), (1, reps))
    t_dn = jnp.tile(jnp.concatenate([-sin, zh, zeros], axis=1), (1, reps))
    return t_cos, t_up, t_dn


GROUP_ROWS = ATT_GROUP * WINDOW


ATT_BPS = 2


MASKED = -1e30


def _band_biases():
    row = jnp.arange(GROUP_ROWS)[:, None] % WINDOW
    col = jnp.arange(2 * WINDOW)[None, :]
    own = jnp.logical_and(col >= WINDOW, col - WINDOW <= row)
    before = jnp.logical_and(col < WINDOW, col > row)
    return (jnp.where(jnp.logical_or(own, before), 0.0, MASKED).astype(F32), jnp.where(own, 0.0, MASKED).astype(F32))


def _band_bias(full_ref, first_ref, has_prev):
    return full_ref[...] if has_prev is True else jnp.where(has_prev, full_ref[...], first_ref[...])


def _sink_column(sink_ref, g):
    head = lax.broadcasted_iota(jnp.int32, (GROUP_ROWS, 1), 0) // WINDOW
    col = jnp.full((GROUP_ROWS, 1), sink_ref[0, g * ATT_GROUP], F32)
    for hh in range(1, ATT_GROUP):
        col = jnp.where(head == hh, sink_ref[0, g * ATT_GROUP + hh], col)
    return col


def _sink_row(sink_ref, g):
    return jnp.concatenate([jnp.full((1, WINDOW), sink_ref[0, g * ATT_GROUP + hh], F32) for hh in range(ATT_GROUP)], axis=1)


def _bias_spec(transposed=False):
    shape = (2 * WINDOW, GROUP_ROWS) if transposed else (GROUP_ROWS, 2 * WINDOW)
    return pl.BlockSpec(shape, lambda b, i: (0, 0))


def _attn_specs():
    q_spec = pl.BlockSpec((None, ATT_BPS, ATT_KV_HEADS, GROUP_ROWS, ATT_HEAD_DIM), lambda b, i: (b, i, 0, 0, 0))
    kv_cur = pl.BlockSpec((None, ATT_KV_HEADS, ATT_BPS * WINDOW, ATT_HEAD_DIM), lambda b, i: (b, 0, i, 0))
    kv_prev = pl.BlockSpec((None, ATT_KV_HEADS, WINDOW, ATT_HEAD_DIM), lambda b, i: (b, 0, jnp.maximum(ATT_BPS * i - 1, 0), 0))
    return q_spec, kv_cur, kv_prev


def _band(prev_ref, cur_ref, g, blk):
    own = cur_ref[g, blk * WINDOW:(blk + 1) * WINDOW]
    before = prev_ref[g] if blk == 0 else cur_ref[g, (blk - 1) * WINDOW:blk * WINDOW]
    return jnp.concatenate([before, own], axis=0)


def _attn_fwd(qh, kh, vh, sinks, w_norm, biases, comms=()):
    bsz, nblk = qh.shape[0], qh.shape[1]
    seq = nblk * WINDOW
    rows = ATT_BPS * WINDOW

    def body(sink_ref, q_ref, kc_ref, kp_ref, vc_ref, vp_ref, w_ref, full_ref, first_ref, raw_ref, an_ref, l_ref):
        l_ref[...] = jnp.zeros_like(l_ref)
        for blk in range(ATT_BPS):
            bias = _band_bias(full_ref, first_ref, True if blk else pl.program_id(1) > 0)
            groups = range(ATT_KV_HEADS)
            keys, vals = [_band(kp_ref, kc_ref, g, blk) for g in groups], [_band(vp_ref, vc_ref, g, blk) for g in groups]
            sink = [_sink_column(sink_ref, g) for g in groups]
            s = [_dot(q_ref[blk, g], keys[g], NT_DIMS) * ATT_SCALE + bias for g in groups]
            m = [jnp.maximum(jnp.max(s[g], axis=-1, keepdims=True), sink[g]) for g in groups]
            p = [jnp.exp(s[g] - m[g]) for g in groups]
            den = [jnp.sum(p[g], axis=-1, keepdims=True) + jnp.exp(sink[g] - m[g]) for g in groups]
            o = [_dot(p[g] / den[g], vals[g]) for g in groups]
            lse = [m[g] + jnp.log(den[g]) for g in groups]
            tok = slice(blk * WINDOW, (blk + 1) * WINDOW)
            for g in groups:
                for hh in range(ATT_GROUP):
                    h = g * ATT_GROUP + hh
                    raw_ref[tok, h * ATT_HEAD_DIM:(h + 1) * ATT_HEAD_DIM] = o[g][hh * WINDOW:(hh + 1) * WINDOW]
                    l_ref[tok, h:h + 1] = lse[g][hh * WINDOW:(hh + 1) * WINDOW]
        y, _, _ = _rms_fwd(raw_ref[...], w_ref[...])
        an_ref[...] = y.astype(BF16)

    cur = lambda width: pl.BlockSpec((None, rows, width), lambda b, i: (b, i, 0))
    q_spec, kv_cur, kv_prev = _attn_specs()
    return _call(
        body, (sinks, qh, kh, kh, vh, vh, w_norm, *biases), name="attn_fwd", grid=(bsz, nblk // ATT_BPS),
        in_specs=[pl.BlockSpec(memory_space=pltpu.SMEM), q_spec, kv_cur, kv_prev, kv_cur, kv_prev, _vec_spec(ATT_WIDTH),
                  _bias_spec(), _bias_spec()],
        out_specs=[cur(ATT_WIDTH), cur(ATT_WIDTH), cur(LANE)],
        out_shape=[_sds((bsz, seq, ATT_WIDTH), F32), _sds((bsz, seq, MIX_WIDTH), BF16), _sds((bsz, seq, LANE), F32)],
        sem=("parallel", "parallel"), comms=comms)


HG_Q0 = ATT_COLS // LANE
HG_F0 = HG_Q0 + HG_HEADS
HG_I0 = HG_F0 + HG_HEADS
HG_G0 = HG_I0 + HG_HEADS
HG_SLABS = 4
HG_Q, HG_F, HG_I, HG_G = range(HG_SLABS)
HG_TOK = 256
HG_NCH = HG_TOK // HG_CHUNK
HG_HPS = 2


def _block_masks():
    row = jnp.arange(HG_TOK)[:, None]
    col = jnp.arange(HG_TOK)[None, :]
    same = (row // HG_CHUNK) == (col // HG_CHUNK)
    return jnp.logical_and(same, col <= row).astype(F32), jnp.logical_and(same, col >= row).astype(F32)


def _row_in_chunk():
    return lax.broadcasted_iota(jnp.int32, (HG_TOK, LANE), 0) % HG_CHUNK


def _chunk_cumsum(x, reverse=False):
    ric = _row_in_chunk()
    shift = 1
    while shift < HG_CHUNK:
        if reverse:
            x = x + jnp.where(ric < HG_CHUNK - shift, pltpu.roll(x, HG_TOK - shift, 0), 0.0)
        else:
            x = x + jnp.where(ric >= shift, pltpu.roll(x, shift, 0), 0.0)
        shift *= 2
    return x


def _chunk_rows(rows):
    stacked = jnp.concatenate([r[None] for r in rows], axis=0)
    return jnp.broadcast_to(stacked, (HG_NCH, HG_CHUNK, LANE)).reshape(HG_TOK, LANE)


def _chunk_slices(x):
    return [x[j * HG_CHUNK:(j + 1) * HG_CHUNK] for j in range(HG_NCH)]


def _hgrn_common(tbl, hf, hq):
    lb = _sigmoid(tbl[1:2] - tbl[0:1])
    sig = _sigmoid(hf)
    f = lb + (1.0 - lb) * sig
    sq = _sigmoid(hq)
    q, k = hq * sq, 1.0 - f
    b = _chunk_cumsum(jnp.log(f))
    last = [b[(j + 1) * HG_CHUNK - 1:(j + 1) * HG_CHUNK] for j in range(HG_NCH)]
    bl = _chunk_rows(last)
    e_b, e_nb, e_rem = jnp.exp(b), jnp.exp(-b), jnp.exp(bl - b)
    e_last = [jnp.exp(r) for r in last]
    return dict(lb=lb, sig=sig, f=f, sq=sq, q=q, k=k, e_b=e_b, e_nb=e_nb, e_rem=e_rem, e_last=e_last,
                qd=q * e_b, kd=k * e_nb, ku=k * e_rem)


def _hgrn_fwd(proj, lb_table, norm_w, mix_in, masks, comms=()):
    bsz, _, seq, _ = proj.shape
    nstep = seq // HG_TOK

    def body(tbl_ref, nw_ref, p_ref, mix_ref, lower_ref, o_ref, rec_ref, st_ref, s_scr):
        @pl.when(pl.program_id(2) == 0)
        def _():
            s_scr[...] = jnp.zeros_like(s_scr)

        lower = lower_ref[...]
        for hp in range(HG_HPS):
            ls = slice(hp * LANE, (hp + 1) * LANE)
            v, hg = p_ref[HG_I, :, ls], p_ref[HG_G, :, ls]
            t = _hgrn_common(tbl_ref[:, ls], p_ref[HG_F, :, ls], p_ref[HG_Q, :, ls])
            a = _dot(t["qd"], t["kd"], NT_DIMS) * lower
            o_intra = _dot(a, v)
            v_c, ku_c, qd_c = [_chunk_slices(z.astype(BF16)) for z in (v, t["ku"], t["qd"])]
            updates = [_dot(v_c[j], ku_c[j], TN_DIMS) for j in range(HG_NCH)]
            st = s_scr[hp]
            states = []
            for j in range(HG_NCH):
                states.append(st)
                st = st * t["e_last"][j] + updates[j]
            s_scr[hp] = st
            o = o_intra + jnp.concatenate([_dot(qd_c[j], states[j], NT_DIMS) for j in range(HG_NCH)], axis=0)
            st_ref[hp, 0] = states[0]
            o_ref[:, ls] = o
            y, _, _ = _rms_fwd(o, nw_ref[...])
            rec_ref[:, ls] = (y * (hg * _sigmoid(hg))).astype(BF16)

    width = HG_HPS * LANE
    head_out = pl.BlockSpec((None, HG_TOK, width), lambda b, h, t: (b, t, h))
    mix_out = pl.BlockSpec((None, HG_TOK, width), lambda b, h, t: (b, t, ATT_WIDTH // width + h))
    return _call(
        body, (lb_table, norm_w, proj, mix_in, masks[0]), name="hgrn_fwd", grid=(bsz, HG_HEADS // HG_HPS, nstep),
        in_specs=[pl.BlockSpec((2, width), lambda b, h, t: (0, h)), pl.BlockSpec((1, LANE), lambda b, h, t: (0, 0)),
                  pl.BlockSpec((None, HG_SLABS, HG_TOK, width), lambda b, h, t: (b, 0, t, h)), pl.BlockSpec(memory_space=pl.ANY),
                  pl.BlockSpec((HG_TOK, HG_TOK), lambda b, h, t: (0, 0))],
        out_specs=[head_out, mix_out,
                   pl.BlockSpec((None, HG_HPS, 1, LANE, LANE), lambda b, h, t: (b, h, t, 0, 0))],
        out_shape=[_sds((bsz, seq, HG_WIDTH), F32), _sds(mix_in.shape, BF16),
                   _sds((bsz, HG_HEADS, nstep, LANE, LANE), F32)],
        scratch_shapes=[pltpu.VMEM((HG_HPS, LANE, LANE), F32)],
        sem=("parallel", "parallel", "arbitrary"), comms=comms, aliases={3: 1})


def _out_proj_fused(cat, w_out, x, post_w, g1, pre_w, sc2, sh2):
    tm = 512

    def epi(mix, ex, outs):
        x_ref, pw_ref, g1_ref, w2_ref, sc_ref, sh_ref = ex
        outs[0][...] = mix
        n1, _, _ = _rms_fwd(mix, pw_ref[...])
        x1 = x_ref[...] + g1_ref[...] * n1
        outs[1][...] = x1
        y2, _, _ = _rms_fwd(x1, w2_ref[...])
        outs[2][...] = (y2 * (1.0 + sc_ref[...]) + sh_ref[...]).astype(BF16)

    return _mm_rows(cat, w_out, name="out_proj", tm=tm, extra=(x, post_w, g1, pre_w, sc2, sh2),
                    extra_specs=[_tok_spec(tm), _vec_spec(), _row_spec(), _vec_spec(), _row_spec(), _row_spec()],
                    out_specs=[_tok_spec(tm), _tok_spec(tm), _tok_spec(tm)],
                    out_shape=[_sds(x.shape, F32), _sds(x.shape, F32), _sds(x.shape, BF16)], epi=epi)


def _acc_out(ref, first, value):
    @pl.when(first)
    def _():
        ref[...] = value

    @pl.when(jnp.logical_not(first))
    def _():
        ref[...] += value


def _down_proj_fused(r, w_down, x1, post_w, g2, target):
    tm = 512
    bsz = x1.shape[0]

    def pro(r_ref, ex, outs):
        rv = r_ref[...]
        return rv * rv

    def epi(down, ex, outs):
        x1_ref, w_ref, g2_ref, t_ref = ex
        loss_ref, dy_ref, dd_ref, dg2_ref, dw_ref = outs
        w, g2v = w_ref[...], g2_ref[...]
        gain = g2v * w
        dh, rstd = _rms_hat(down)
        err = x1_ref[...] + dh * gain - t_ref[...]
        part = (0.5 / D_MODEL) * jnp.sum(jnp.sum(err * err, axis=-1, keepdims=True), axis=0, keepdims=True)
        loss_ref[...] += jnp.broadcast_to(part, (1, LANE))
        dy = err * (1.0 / D_MODEL)
        dy_ref[...] = dy
        dd, per_col = _rms_bwd_gain(dy, gain, dh, rstd)
        dd_ref[...] = dd.astype(BF16)
        dg2_ref[...] += per_col * w
        dw_ref[...] += per_col * g2v

    return _mm_rows(r, w_down, name="down_proj", tm=tm, extra=(x1, post_w, g2, target),
                    extra_specs=[_tok_spec(tm), _vec_spec(), _row_spec(), _tok_spec(tm)],
                    out_specs=[_vec_spec(LANE), _tok_spec(tm), _tok_spec(tm), _row_spec(), _vec_spec()],
                    out_shape=[_sds((1, LANE), F32), _sds(x1.shape, F32), _sds(x1.shape, BF16), _sds((bsz, 1, D_MODEL), F32),
                               _sds((1, D_MODEL), F32)], pro=pro, epi=epi, parts=2, zero_per_seq=(3,), zero_once=(0, 4),
                    vmem_limit=VMEM_LIMIT_BIG)


def _up_bwd_fused(dpre, w_up4, dy, x1, mix, pre_w, sc2, post_w, g1, comms=()):
    tm = 512
    bsz = x1.shape[0]

    def epi(dh2v, ex, outs):
        dy_ref, x1_ref, mix_ref, w2_ref, sc_ref, pw_ref, g1_ref = ex
        dx1_ref, dmix_ref, dsc_ref, dsh_ref, dg1_ref, dw2_ref, dpw_ref = outs
        w2, pw, g1v = w2_ref[...], pw_ref[...], g1_ref[...]
        mod2 = 1.0 + sc_ref[...]
        xh2, rstd2 = _rms_hat(x1_ref[...])
        dsh_ref[...] += _colsum(dh2v)
        dx1n, per_col2 = _rms_bwd_gain(dh2v, mod2 * w2, xh2, rstd2)
        dsc_ref[...] += per_col2 * w2
        dw2_ref[...] += per_col2 * mod2
        dx1 = dy_ref[...] + dx1n
        dx1_ref[...] = dx1
        mh, rstd1 = _rms_hat(mix_ref[...])
        dmix, per_col1 = _rms_bwd_gain(dx1, g1v * pw, mh, rstd1)
        dmix_ref[...] = dmix.astype(BF16)
        dg1_ref[...] += per_col1 * pw
        dpw_ref[...] += per_col1 * g1v

    row_shape = _sds((bsz, 1, D_MODEL), F32)
    vec_shape = _sds((1, D_MODEL), F32)
    return _mm_rows(dpre, w_up4, name="up_bwd", tm=tm, extra=(dy, x1, mix, pre_w, sc2, post_w, g1),
                    extra_specs=[_tok_spec(tm), _tok_spec(tm), _tok_spec(tm), _vec_spec(), _row_spec(), _vec_spec(), _row_spec()],
                    out_specs=[_tok_spec(tm), _tok_spec(tm), _row_spec(), _row_spec(), _row_spec(), _vec_spec(), _vec_spec()],
                    out_shape=[_sds(x1.shape, F32), _sds(x1.shape, BF16), row_shape, row_shape, row_shape, vec_shape, vec_shape],
                    epi=epi, b_chunks=w_up4.shape[0], comms=comms, parts=2, zero_per_seq=(2, 3, 4), zero_once=(5, 6),
                    vmem_limit=VMEM_LIMIT_BIG)


def _norm1_bwd(dh1, dx1, x, pre_w, sc1, tm=512, comms=()):
    bsz, seq, _ = x.shape

    def body(dh_ref, dx1_ref, x_ref, w_ref, sc_ref, gx_ref, dsc_ref, dsh_ref, dw_ref):
        b, i = pl.program_id(0), pl.program_id(1)
        w = w_ref[...]
        dh = dh_ref[...]
        mod = 1.0 + sc_ref[...]
        xh, rstd = _rms_hat(x_ref[...])
        dx, per_col = _rms_bwd_gain(dh, mod * w, xh, rstd)
        _acc_out(dsh_ref, i == 0, _colsum(dh))
        _acc_out(dsc_ref, i == 0, per_col * w)
        _acc_out(dw_ref, jnp.logical_and(b == 0, i == 0), per_col * mod)
        gx_ref[...] = dx1_ref[...] + dx

    row_shape = _sds((bsz, 1, D_MODEL), F32)
    return _call(
        body, (dh1, dx1, x, pre_w, sc1), name="norm1_bwd", grid=(bsz, seq // tm),
        in_specs=[_tok_spec(tm), _tok_spec(tm), _tok_spec(tm), _vec_spec(), _row_spec()],
        out_specs=[_tok_spec(tm), _row_spec(), _row_spec(), _vec_spec()],
        out_shape=[_sds(x.shape, F32), row_shape, row_shape, _sds((1, D_MODEL), F32)],
        sem=("arbitrary", "arbitrary"), comms=comms)


def _hgrn_bwd(dcat, proj, o_raw, states, lb_table, norm_w, masks, comms=()):
    bsz, _, seq, _ = proj.shape
    nstep = seq // HG_TOK
    rec0 = ATT_WIDTH // LANE
    width = HG_HPS * LANE
    slabs = (HG_Q0, HG_F0, HG_I0, HG_G0)
    n_steps = (HG_HEADS // HG_HPS) * bsz * nstep
    assert n_steps >= 2

    def body(tbl_ref, nw_ref, dr_ref, p_ref, o_ref, st_ref, lower_ref, upper_ref,
             dproj_ref, dlb_ref, dnw_ref, ds_scr, grad_buf, grad_sem):
        h, b, t = pl.program_id(0), pl.program_id(1), pl.program_id(2)
        step = (h * bsz + b) * nstep + t
        slot = step % 2
        dq_k, df_k, di_k, dg_k = range(4)

        def grad_copies(of_step):
            hh, bb, tt = of_step // (bsz * nstep), (of_step // nstep) % bsz, of_step % nstep
            rows = pl.ds(pl.multiple_of((nstep - 1 - tt) * HG_TOK, HG_TOK), HG_TOK)
            return [pltpu.make_async_copy(
                grad_buf.at[of_step % 2, k],
                dproj_ref.at[bb, rows, pl.ds(pl.multiple_of(slabs[k] * LANE + hh * width, width), width)],
                grad_sem.at[of_step % 2, k]) for k in range(4)]

        @pl.when(step >= 2)
        def _():
            for cp in grad_copies(step - 2):
                cp.wait()

        @pl.when(t == 0)
        def _():
            ds_scr[...] = jnp.zeros_like(ds_scr)

        lower, upper = lower_ref[...], upper_ref[...]
        dlb_parts = []
        dnw_acc = jnp.zeros((1, LANE), F32)
        for hp in range(HG_HPS):
            ls = slice(hp * LANE, (hp + 1) * LANE)
            hq, v, hg = p_ref[HG_Q, :, ls], p_ref[HG_I, :, ls], p_ref[HG_G, :, ls]
            nw = nw_ref[...]
            c = _hgrn_common(tbl_ref[:, ls], p_ref[HG_F, :, ls], hq)
            qd, kd, ku = c["qd"], c["kd"], c["ku"]
            y, on, rstd = _rms_fwd(o_ref[:, ls], nw)
            sg = _sigmoid(hg)
            dr = dr_ref[:, ls]
            grad_buf[slot, dg_k, :, ls] = (dr * y * (sg * (1.0 + hg * (1.0 - sg)))).astype(BF16)
            do, dnw_rows = _rms_bwd(dr * (hg * sg), on, rstd, nw)
            at = _dot(kd, qd, NT_DIMS) * upper
            da = _dot(do, v, NT_DIMS) * lower
            dat = _dot(v, do, NT_DIMS) * upper
            dv = _dot(at, do)
            dqd = _dot(da, kd)
            dkd = _dot(dat, qd)
            do_c, qd_c, v_c, ku_c = [_chunk_slices(z.astype(BF16)) for z in (do, qd, v, ku)]
            outer = [_dot(do_c[j], qd_c[j], TN_DIMS) for j in range(HG_NCH)]
            ds = ds_scr[hp]
            ds_after = [None] * HG_NCH
            for j in reversed(range(HG_NCH)):
                ds_after[j] = ds
                ds = outer[j] + ds * c["e_last"][j]
            ds_scr[hp] = ds
            updates = [_dot(v_c[j], ku_c[j], TN_DIMS) for j in range(HG_NCH)]
            states = [st_ref[hp, 0]]
            for j in range(HG_NCH - 1):
                states.append(states[j] * c["e_last"][j] + updates[j])
            dv = dv + jnp.concatenate([_dot(ku_c[j], ds_after[j], NT_DIMS) for j in range(HG_NCH)], axis=0)
            dqd = dqd + jnp.concatenate([_dot(do_c[j], states[j]) for j in range(HG_NCH)], axis=0)
            dku = jnp.concatenate([_dot(v_c[j], ds_after[j]) for j in range(HG_NCH)], axis=0)
            dku_ku = dku * ku
            dbl = [_colsum(states[j] * ds_after[j]) * c["e_last"][j] + _colsum(dku_ku[j * HG_CHUNK:(j + 1) * HG_CHUNK])
                   for j in range(HG_NCH)]
            dk = dkd * c["e_nb"] + dku * c["e_rem"]
            db = dqd * qd - dkd * kd - dku_ku + jnp.where(_row_in_chunk() == HG_CHUNK - 1, _chunk_rows(dbl), 0.0)
            dfv = _chunk_cumsum(db, reverse=True) / c["f"] - dk
            sig, sq = c["sig"], c["sq"]
            grad_buf[slot, df_k, :, ls] = (dfv * (1.0 - c["lb"]) * sig * (1.0 - sig)).astype(BF16)
            grad_buf[slot, dq_k, :, ls] = (dqd * c["e_b"] * (sq * (1.0 + hq * (1.0 - sq)))).astype(BF16)
            grad_buf[slot, di_k, :, ls] = dv.astype(BF16)
            dlb_parts.append(_colsum(dfv * (1.0 - sig)))
            dnw_acc = dnw_acc + _colsum(dnw_rows)
        _acc_out(dlb_ref, jnp.logical_and(b == 0, t == 0), jnp.concatenate(dlb_parts, axis=1))
        _acc_out(dnw_ref, jnp.logical_and(h == 0, jnp.logical_and(b == 0, t == 0)), dnw_acc)
        for cp in grad_copies(step):
            cp.start()

        @pl.when(step == n_steps - 1)
        def _():
            for cp in grad_copies(step - 1) + grad_copies(step):
                cp.wait()

    rev = lambda t: nstep - 1 - t
    slab = lambda first: pl.BlockSpec((None, HG_TOK, width), lambda h, b, t: (b, rev(t), first // HG_HPS + h))
    head = pl.BlockSpec((None, HG_TOK, width), lambda h, b, t: (b, rev(t), h))
    return _call(
        body, (lb_table, norm_w, dcat, proj, o_raw, states, *masks), name="hgrn_bwd",
        grid=(HG_HEADS // HG_HPS, bsz, nstep),
        in_specs=[pl.BlockSpec((2, width), lambda h, b, t: (0, h)), pl.BlockSpec((1, LANE), lambda h, b, t: (0, 0)),
                  slab(rec0), pl.BlockSpec((None, HG_SLABS, HG_TOK, width), lambda h, b, t: (b, 0, rev(t), h)), head,
                  pl.BlockSpec((None, HG_HPS, 1, LANE, LANE), lambda h, b, t: (b, h, rev(t), 0, 0)),
                  pl.BlockSpec((HG_TOK, HG_TOK), lambda h, b, t: (0, 0)), pl.BlockSpec((HG_TOK, HG_TOK), lambda h, b, t: (0, 0))],
        out_specs=[pl.BlockSpec(memory_space=pl.ANY), pl.BlockSpec((1, width), lambda h, b, t: (0, h)),
                   pl.BlockSpec((1, LANE), lambda h, b, t: (0, 0))],
        out_shape=[_sds((bsz, seq, IN_COLS), BF16), _sds((1, HG_WIDTH), F32), _sds((1, LANE), F32)],
        scratch_shapes=[pltpu.VMEM((HG_HPS, LANE, LANE), F32), pltpu.VMEM((2, 4, HG_TOK, width), BF16),
                        pltpu.SemaphoreType.DMA((2, 4))],
        sem=("arbitrary", "arbitrary", "arbitrary"), comms=comms)


def _attn_bwd(dcat, raw, w_norm, qh, kh, vh, lse, sinks, tables, biases, dproj, comms=()):
    bsz, nblk = qh.shape[0], qh.shape[1]
    seq = nblk * WINDOW
    nstep = nblk // ATT_BPS
    half = ROPE_DIM // 2

    def body(sink_ref, da_ref, raw_ref, w_ref, q_ref, kc_ref, kp_ref, vc_ref, vp_ref, l_ref, c_ref, u_ref, d_ref,
             full_ref, first_ref, dproj_ref, o_ref, dw_ref, dsink_ref, carry_k, carry_v):
        b, i = pl.program_id(0), pl.program_id(1)
        first = jnp.logical_and(b == 0, i == 0)

        @pl.when(i == 0)
        def _():
            carry_k[...] = jnp.zeros_like(carry_k)
            carry_v[...] = jnp.zeros_like(carry_v)

        w = w_ref[...]
        _, on, rstd = _rms_fwd(raw_ref[...], w)
        do_step, dw_rows = _rms_bwd(da_ref[...], on, rstd, w)
        _acc_out(dw_ref, first, _colsum(dw_rows))
        lane8 = lax.broadcasted_iota(jnp.int32, (1, ATT_Q_HEADS), 1)
        dsink = jnp.zeros((1, ATT_Q_HEADS), F32)
        head_cols = jnp.where(lax.broadcasted_iota(jnp.int32, (2 * ATT_Q_HEADS, ATT_WIDTH), 1) // ATT_HEAD_DIM
                              == lax.broadcasted_iota(jnp.int32, (2 * ATT_Q_HEADS, ATT_WIDTH), 0), 1.0, 0.0)
        from_next_k, from_next_v = carry_k[...], carry_v[...]
        for blk in reversed(range(ATT_BPS)):
            tok = slice(blk * WINDOW, (blk + 1) * WINDOW)
            bias = _band_bias(full_ref, first_ref, True if blk else i < nstep - 1)
            do_all = do_step[tok]
            c, u, d = c_ref[tok, :], u_ref[tok, :], d_ref[tok, :]
            lse_t = l_ref[tok, :].T
            prod = do_all * raw_ref[tok, :]
            prod_hi = prod.astype(BF16)
            prod_lo = prod - prod_hi.astype(F32)
            dsum_t = _dot(head_cols, prod_hi, NT_DIMS) + _dot(head_cols, prod_lo, NT_DIMS)

            def unrope(g):
                return (g * c + pltpu.roll(g * u, LANE - half, 1) + pltpu.roll(g * d, half, 1)).astype(BF16)

            groups = range(ATT_KV_HEADS)
            group_row = lambda z, g: jnp.concatenate(
                [z[g * ATT_GROUP + hh:g * ATT_GROUP + hh + 1, :] for hh in range(ATT_GROUP)], axis=1)
            q = [q_ref[blk, g] for g in groups]
            keys, vals = [_band(kp_ref, kc_ref, g, blk) for g in groups], [_band(vp_ref, vc_ref, g, blk) for g in groups]
            do_g = [jnp.concatenate([do_all[:, (g * ATT_GROUP + hh) * ATT_HEAD_DIM:(g * ATT_GROUP + hh + 1) * ATT_HEAD_DIM]
                                     for hh in range(ATT_GROUP)], axis=0) for g in groups]
            dsum, lse_g = [group_row(dsum_t, g) for g in groups], [group_row(lse_t, g) for g in groups]
            s_t = [_dot(keys[g], q[g], NT_DIMS) for g in groups]
            dp_t = [_dot(vals[g], do_g[g], NT_DIMS) for g in groups]
            p_t = [jnp.exp(s_t[g] * ATT_SCALE + bias - lse_g[g]) for g in groups]
            ds_t = [p_t[g] * (dp_t[g] - dsum[g]) * ATT_SCALE for g in groups]
            dq_g = [_dot(ds_t[g], keys[g], TN_DIMS) for g in groups]
            dk_g = [_dot(ds_t[g], q[g]) for g in groups]
            dv_g = [_dot(p_t[g], do_g[g]) for g in groups]
            for g in groups:
                sink_part = jnp.exp(_sink_row(sink_ref, g) - lse_g[g]) * dsum[g]
                for hh in range(ATT_GROUP):
                    head_sum = jnp.sum(sink_part[:, hh * WINDOW:(hh + 1) * WINDOW], axis=1, keepdims=True)
                    dsink = dsink - jnp.where(lane8 == g * ATT_GROUP + hh, head_sum, 0.0)
            dq_parts = [dq_g[g][hh * WINDOW:(hh + 1) * WINDOW] for g in groups for hh in range(ATT_GROUP)]
            dk_before, dk_own = [z[:WINDOW] for z in dk_g], [z[WINDOW:] for z in dk_g]
            dv_before, dv_own = [z[:WINDOW] for z in dv_g], [z[WINDOW:] for z in dv_g]
            per_slab = LANE // ATT_HEAD_DIM
            for s in range(ATT_WIDTH // LANE):
                slab = jnp.concatenate(dq_parts[s * per_slab:(s + 1) * per_slab], axis=1)
                o_ref[tok, s * LANE:(s + 1) * LANE] = unrope(slab)
            o_ref[tok, ATT_WIDTH:ATT_WIDTH + LANE] = unrope(jnp.concatenate(dk_own, axis=1) + from_next_k)
            o_ref[tok, ATT_WIDTH + LANE:ATT_COLS] = (jnp.concatenate(dv_own, axis=1) + from_next_v).astype(BF16)
            from_next_k, from_next_v = jnp.concatenate(dk_before, axis=1), jnp.concatenate(dv_before, axis=1)
        carry_k[...] = from_next_k
        carry_v[...] = from_next_v
        _acc_out(dsink_ref, first, dsink)

    rows = ATT_BPS * WINDOW
    rev = lambda i: nstep - 1 - i
    cur = lambda width: pl.BlockSpec((None, rows, width), lambda b, i: (b, rev(i), 0))
    q_spec = pl.BlockSpec((None, ATT_BPS, ATT_KV_HEADS, GROUP_ROWS, ATT_HEAD_DIM), lambda b, i: (b, rev(i), 0, 0, 0))
    kv_cur = pl.BlockSpec((None, ATT_KV_HEADS, rows, ATT_HEAD_DIM), lambda b, i: (b, 0, rev(i), 0))
    kv_prev = pl.BlockSpec((None, ATT_KV_HEADS, WINDOW, ATT_HEAD_DIM), lambda b, i: (b, 0, jnp.maximum(ATT_BPS * rev(i) - 1, 0), 0))
    tab = pl.BlockSpec((rows, LANE), lambda b, i: (rev(i), 0))
    return _call(
        body, (sinks, dcat, raw, w_norm, qh, kh, kh, vh, vh, lse, *tables, *biases, dproj), name="attn_bwd", grid=(bsz, nstep),
        in_specs=[pl.BlockSpec(memory_space=pltpu.SMEM), cur(ATT_WIDTH), cur(ATT_WIDTH), _vec_spec(ATT_WIDTH), q_spec,
                  kv_cur, kv_prev, kv_cur, kv_prev, cur(LANE), tab, tab, tab, _bias_spec(True), _bias_spec(True),
                  pl.BlockSpec(memory_space=pl.ANY)],
        out_specs=[cur(ATT_COLS), _vec_spec(ATT_WIDTH), _vec_spec(ATT_Q_HEADS)],
        out_shape=[_sds(dproj.shape, BF16), _sds((1, ATT_WIDTH), F32), _sds((1, ATT_Q_HEADS), F32)],
        scratch_shapes=[pltpu.VMEM((WINDOW, LANE), F32), pltpu.VMEM((WINDOW, LANE), F32)],
        sem=("arbitrary", "arbitrary"), comms=comms, aliases={15: 0})


def _other_chips(x, y):
    return [(1 - x, y), (x, 1 - y), (1 - x, 1 - y)]


def _sem_pair(n):
    return [pltpu.SemaphoreType.DMA((n,)), pltpu.SemaphoreType.DMA((n,))]


def _plan_pair_forward(bufs):
    n = len(bufs)

    def copies(outs, sems):
        x, y, c = _mesh_pos()
        sends, lands = [], []
        for a in range(n):
            for j, chip in enumerate(_other_chips(x, y)):
                k = 3 * a + j
                slot = outs[a].at[4 * chip[0] + 2 * chip[1] + c]
                sends.append(pltpu.make_async_remote_copy(
                    src_ref=slot, dst_ref=slot, send_sem=sems[0].at[k], recv_sem=sems[1].at[k],
                    device_id=(x, y, 1 - c), device_id_type=MESH))
                theirs = outs[a].at[4 * chip[0] + 2 * chip[1] + 1 - c]
                lands.append(pltpu.make_async_remote_copy(
                    src_ref=theirs, dst_ref=theirs, send_sem=sems[0].at[k], recv_sem=sems[1].at[k],
                    device_id=(x, y, 1 - c), device_id_type=MESH))
        return sends, lands

    def start(ins, outs, sems):
        for cp in copies(outs, sems)[0]:
            cp.start()

    def finish(ins, outs, sems):
        sends, lands = copies(outs, sems)
        for cp in lands:
            cp.wait_recv()
        for cp in sends:
            cp.wait_send()

    return _Comm(list(bufs), [_sds(b.shape, b.dtype) for b in bufs], _sem_pair(3 * n), start, finish,
                 aliases=[(a, a) for a in range(n)])


def _plan_pair(arrays, other_half):
    n = len(arrays)
    per = N_CHIPS if other_half == "chip_major" else 1

    def copies(ins, outs, sems):
        x, y, c = _mesh_pos()
        out = []
        for a in range(n):
            for k in range(per):
                if other_half == "chip_major":
                    src, dst = ins[a].at[k, 1 - c], outs[a].at[k]
                else:
                    src, dst = (ins[a].at[1 - c] if other_half else ins[a]), outs[a]
                out.append(pltpu.make_async_remote_copy(
                    src_ref=src, dst_ref=dst, send_sem=sems[0].at[per * a + k], recv_sem=sems[1].at[per * a + k],
                    device_id=(x, y, 1 - c), device_id_type=MESH))
        return out

    def start(ins, outs, sems):
        for cp in copies(ins, outs, sems):
            cp.start()

    def finish(ins, outs, sems):
        for cp in copies(ins, outs, sems):
            cp.wait()

    if other_half == "chip_major":
        shapes = [_sds((a.shape[0],) + a.shape[2:], a.dtype) for a in arrays]
    else:
        shapes = [_sds(a.shape[1:] if other_half else a.shape, a.dtype) for a in arrays]
    return _Comm(list(arrays), shapes, _sem_pair(per * n), start, finish)


def _plan_chip_exchange(arrays):
    n = len(arrays)

    def copies(ins, outs, sems):
        x, y, c = _mesh_pos()
        sends, lands = [], []
        for a in range(n):
            for j, chip in enumerate(_other_chips(x, y)):
                k = 3 * a + j
                sends.append(pltpu.make_async_remote_copy(
                    src_ref=ins[a].at[2 * chip[0] + chip[1]], dst_ref=outs[a].at[2 * x + y], send_sem=sems[0].at[k],
                    recv_sem=sems[1].at[k], device_id=(*chip, c), device_id_type=MESH))
                slot = outs[a].at[2 * chip[0] + chip[1]]
                lands.append(pltpu.make_async_remote_copy(
                    src_ref=slot, dst_ref=slot, send_sem=sems[0].at[k], recv_sem=sems[1].at[k],
                    device_id=(*chip, c), device_id_type=MESH))
        return sends, lands

    def start(ins, outs, sems):
        for cp in copies(ins, outs, sems)[0]:
            cp.start()

    def finish(ins, outs, sems):
        sends, lands = copies(ins, outs, sems)
        for cp in lands:
            cp.wait_recv()
        for cp in sends:
            cp.wait_send()

    return _Comm(list(arrays), [_sds(a.shape, a.dtype) for a in arrays], _sem_pair(3 * n), start, finish)


SEM_SPEC = pl.BlockSpec(memory_space=pltpu.SEMAPHORE)
N_OTHER = N_CHIPS - 1


def _exchange_copies(s_ref, land_ref, sems):
    x, y, c = _mesh_pos()
    return [pltpu.make_async_remote_copy(
        src_ref=s_ref.at[2 * chip[0] + chip[1]], dst_ref=land_ref.at[2 * x + y], send_sem=sems[j], recv_sem=sems[N_OTHER + j],
        device_id=(*chip, c), device_id_type=MESH) for j, chip in enumerate(_other_chips(x, y))]


def _exchange_start(s, name):
    def body(s_ref, land_ref, *outs):
        sems, token = outs[:2 * N_OTHER], outs[-1]
        for cp in _exchange_copies(s_ref, land_ref, sems):
            cp.start()
        token[...] = jnp.zeros_like(token)

    hbm = pltpu.HBM(s.shape, s.dtype)
    res = pl.pallas_call(
        body, name=name,
        out_shape=(pltpu.SemaphoreType.DMA(()),) * (2 * N_OTHER) + (hbm, hbm, _sds((SUBLANES, LANE), F32)),
        in_specs=(HBM_SPEC, HBM_SPEC),
        out_specs=(SEM_SPEC,) * (2 * N_OTHER) + (HBM_SPEC, HBM_SPEC, pl.BlockSpec(memory_space=pltpu.VMEM)),
        input_output_aliases={0: 2 * N_OTHER, 1: 2 * N_OTHER + 1},
        compiler_params=pltpu.CompilerParams(has_side_effects=pltpu.SideEffectType.DATAFLOW_SIDE_EFFECTING),
    )(pltpu.with_memory_space_constraint(s, pltpu.HBM), pltpu.with_memory_space_constraint(lax.empty(s.shape, s.dtype), pltpu.HBM))
    return res[:2 * N_OTHER], res[2 * N_OTHER], res[2 * N_OTHER + 1], res[-1]


def _exchange_wait(sems, s_thru, land_thru, afters, name):
    def body(s_ref, land_ref, *rest):
        for cp in _exchange_copies(s_ref, land_ref, rest[:2 * N_OTHER]):
            cp.wait_send()
            cp.wait_recv()

    hbm = pltpu.HBM(s_thru.shape, s_thru.dtype)
    return pl.pallas_call(
        body, name=name, out_shape=(hbm, hbm),
        in_specs=(HBM_SPEC, HBM_SPEC) + (SEM_SPEC,) * (2 * N_OTHER) + (pl.BlockSpec(memory_space=pl.ANY),) * len(afters),
        out_specs=(HBM_SPEC, HBM_SPEC), input_output_aliases={0: 0, 1: 1},
        compiler_params=pltpu.CompilerParams(has_side_effects=pltpu.SideEffectType.DATAFLOW_SIDE_EFFECTING),
    )(s_thru, land_thru, *sems, *afters)


def _gather_copies(block_ref, buf_ref, sems):
    x, y, c = _mesh_pos()
    return [pltpu.make_async_remote_copy(
        src_ref=block_ref, dst_ref=buf_ref.at[4 * x + 2 * y + c], send_sem=sems[j], recv_sem=sems[N_OTHER + j],
        device_id=(*chip, c), device_id_type=MESH) for j, chip in enumerate(_other_chips(x, y))]


def _gather_start(blocks, bufs, afters, name):
    n = len(blocks)
    per = 2 * N_OTHER

    def body(*refs):
        ins, outs = refs[:2 * n], refs[2 * n + len(afters):]
        for a in range(n):
            for cp in _gather_copies(ins[a], ins[n + a], outs[a * per:(a + 1) * per]):
                cp.start()
        outs[-1][...] = jnp.zeros_like(outs[-1])

    hbm = [pltpu.HBM(z.shape, z.dtype) for z in list(blocks) + list(bufs)]
    res = pl.pallas_call(
        body, name=name,
        out_shape=(pltpu.SemaphoreType.DMA(()),) * (n * per) + tuple(hbm) + (_sds((SUBLANES, LANE), F32),),
        in_specs=(HBM_SPEC,) * (2 * n) + (pl.BlockSpec(memory_space=pl.ANY),) * len(afters),
        out_specs=(SEM_SPEC,) * (n * per) + (HBM_SPEC,) * (2 * n) + (pl.BlockSpec(memory_space=pltpu.VMEM),),
        input_output_aliases={k: n * per + k for k in range(2 * n)},
        compiler_params=pltpu.CompilerParams(has_side_effects=pltpu.SideEffectType.DATAFLOW_SIDE_EFFECTING),
    )(*[pltpu.with_memory_space_constraint(z, pltpu.HBM) for z in list(blocks) + list(bufs)], *afters)
    parts = [(res[a * per:(a + 1) * per], res[n * per + a], res[n * per + n + a]) for a in range(n)]
    return parts, res[-1]


def _gather_wait(part, afters, name):
    sems, block, buf = part

    def body(block_ref, buf_ref, *rest):
        for cp in _gather_copies(block_ref, buf_ref, rest[:2 * N_OTHER]):
            cp.wait_send()
            cp.wait_recv()

    return pl.pallas_call(
        body, name=name, out_shape=(pltpu.HBM(block.shape, block.dtype), pltpu.HBM(buf.shape, buf.dtype)),
        in_specs=(HBM_SPEC, HBM_SPEC) + (SEM_SPEC,) * (2 * N_OTHER) + (pl.BlockSpec(memory_space=pl.ANY),) * len(afters),
        out_specs=(HBM_SPEC, HBM_SPEC), input_output_aliases={0: 0, 1: 1},
        compiler_params=pltpu.CompilerParams(has_side_effects=pltpu.SideEffectType.DATAFLOW_SIDE_EFFECTING),
    )(block, buf, *sems, *afters)[1]


def _comm_only(comms, name):
    return _call(lambda: None, (), name=name, grid=(), in_specs=[], out_specs=[], out_shape=[], sem=(), comms=comms)[1]


def _allgather8(arrays, name):
    return _comm_only([_plan_allgather8(arrays)], name)[0]


def _plan_allgather8(arrays):
    n = len(arrays)

    def parts(ins, outs, sems):
        send_sems, recv_sems, local_sems = sems
        x, y, c = _mesh_pos()
        me, sibling = (x, y, c), (x, y, 1 - c)
        chips = _other_chips(x, y)

        def copy(a, k, block, to, src=None):
            dst = outs[a].at[4 * block[0] + 2 * block[1] + block[2]]
            return pltpu.make_async_remote_copy(
                src_ref=dst if src is None else src, dst_ref=dst, send_sem=send_sems.at[7 * a + k],
                recv_sem=recv_sems.at[7 * a + k], device_id=to, device_id_type=MESH)

        mine = [pltpu.make_async_copy(ins[a], outs[a].at[4 * x + 2 * y + c], local_sems.at[a]) for a in range(n)]
        first = []
        for a in range(n):
            first.append(copy(a, 0, me, sibling, src=ins[a]))
            first += [copy(a, 1 + j, me, (*chip, c), src=ins[a]) for j, chip in enumerate(chips)]
        return copy, mine, first, me, sibling, chips, c

    def start(ins, outs, sems):
        _, mine, first, *_ = parts(ins, outs, sems)
        for cp in mine + first:
            cp.start()

    def finish(ins, outs, sems):
        copy, mine, first, me, sibling, chips, c = parts(ins, outs, sems)
        passed = []
        for j, chip in enumerate(chips):
            for a in range(n):
                copy(a, 1 + j, (*chip, c), me).wait_recv()
                fwd = copy(a, 4 + j, (*chip, c), sibling)
                fwd.start()
                passed.append(fwd)
        for a in range(n):
            copy(a, 0, sibling, me).wait_recv()
            for j, chip in enumerate(chips):
                copy(a, 4 + j, (*chip, 1 - c), me).wait_recv()
        for cp in first + passed:
            cp.wait_send()
        for cp in mine:
            cp.wait()

    sems = [pltpu.SemaphoreType.DMA((7 * n,)), pltpu.SemaphoreType.DMA((7 * n,)), pltpu.SemaphoreType.DMA((n,))]
    return _Comm(list(arrays), [_sds((N_DEV,) + a.shape, a.dtype) for a in arrays], sems, start, finish)


def _pair_sum(g, q, core, name, chip_major=False):
    rows, cols = g.shape[2:]
    tr = _row_tile(rows)

    def body(core_ref, g_ref, q_ref, o_ref):
        o_ref[...] = (g_ref[...] + q_ref[...]).astype(BF16)

    blk = pl.BlockSpec((None, tr, cols), lambda k, i, core_ref: (k, i, 0))
    if chip_major:
        own = pl.BlockSpec((None, None, tr, cols), lambda k, i, core_ref: (k, core_ref[0], i, 0))
    else:
        own = pl.BlockSpec((None, None, tr, cols), lambda k, i, core_ref: (core_ref[0], k, i, 0))
    return pl.pallas_call(
        body, name=name,
        grid_spec=pltpu.PrefetchScalarGridSpec(num_scalar_prefetch=1, grid=(N_CHIPS, rows // tr), in_specs=[own, blk], out_specs=blk),
        out_shape=_sds((N_CHIPS, rows, cols), BF16), compiler_params=_params("parallel", "parallel"),
    )(core, g, q)


def _sum_chips(own, landed, chip, name):
    _, rows, cols = own.shape
    tr = _row_tile(rows)

    def body(chip_ref, own_ref, a_ref, b_ref, c_ref, o_ref):
        acc = own_ref[...].astype(F32) + a_ref[...].astype(F32)
        o_ref[...] = (acc + b_ref[...].astype(F32)) + c_ref[...].astype(F32)

    blk = lambda flip: pl.BlockSpec((None, tr, cols), lambda i, chip_ref: (jnp.bitwise_xor(chip_ref[0], flip), i, 0))
    return pl.pallas_call(
        body, name=name,
        grid_spec=pltpu.PrefetchScalarGridSpec(num_scalar_prefetch=1, grid=(rows // tr,), in_specs=[blk(0), blk(1), blk(2), blk(3)],
                                               out_specs=pl.BlockSpec((tr, cols), lambda i, chip_ref: (i, 0))),
        out_shape=_sds((rows, cols), F32), compiler_params=_params("parallel"),
    )(chip, own, landed, landed, landed)


SUBLANES = 8


def _tile_rows(n_elems):
    return -(-n_elems // (SUBLANES * LANE)) * SUBLANES


SMALL_ITEMS = (("b_ada", N_MOD * D_MODEL), ("pre_w_mix", D_MODEL), ("post_w_mix", D_MODEL), ("pre_w_mlp", D_MODEL),
               ("post_w_mlp", D_MODEL), ("attn_out_w", ATT_WIDTH), ("hg_norm_w", HG_HEAD_DIM), ("attn_sinks", ATT_Q_HEADS),
               ("lb_0", HG_WIDTH), ("lb_1", HG_WIDTH))
SMALL_AT = {}
for _name, _size in SMALL_ITEMS:
    SMALL_AT[_name] = (sum(r for _, r in SMALL_AT.values()), _tile_rows(_size))
SMALL_ROWS = sum(r for _, r in SMALL_AT.values())
MOD_ROWS = SMALL_AT["b_ada"][1]
PLAIN_ROWS = SMALL_AT["lb_0"][0] - MOD_ROWS
LB_ROWS = SMALL_AT["lb_0"][1]


def _rows(a, nrows=None):
    flat = a.reshape(-1)
    nrows = _tile_rows(flat.shape[0]) if nrows is None else nrows
    return jnp.pad(flat, (0, nrows * LANE - flat.shape[0])).reshape(nrows, LANE)


def _pack_small(vals):
    vals = dict(vals, lb_0=vals["lb_table"][0], lb_1=vals["lb_table"][1])
    return jnp.concatenate([_rows(vals[name], SMALL_AT[name][1]) for name, _ in SMALL_ITEMS], axis=0)


def _unpack_small(p):
    def item(name, shape):
        first = SMALL_AT[name][0]
        size = shape[0] * shape[1]
        return p[first:first + SMALL_AT[name][1]].reshape(-1)[:size].reshape(shape)

    out = {name: item(name, (1, size)) for name, size in SMALL_ITEMS if not name.startswith("lb_")}
    out["lb_table"] = jnp.concatenate([item("lb_0", (1, HG_WIDTH)), item("lb_1", (1, HG_WIDTH))], axis=0)
    return out


def _pack_partials(dmod, plain, d_lb, loss_row):
    return jnp.concatenate([_rows(dmod, dmod.shape[0] * MOD_ROWS)] + [_rows(g) for g in plain] + [_rows(d_lb), _rows(loss_row)], axis=0)


def _small_update(packs, w, m, v, n_seq):
    mod_end = n_seq * MOD_ROWS
    lb_at = mod_end + PLAIN_ROWS
    t0, t1 = SMALL_AT["lb_0"][0], SMALL_AT["lb_1"][0]

    def body(p_ref, w_ref, m_ref, v_ref, g_ref, dl_ref, nm_ref, nv_ref, loss_ref):
        tot = p_ref[0]
        for d in range(1, N_DEV):
            tot = tot + p_ref[d]
        wv = w_ref[...]
        p1 = _sigmoid(wv[t1:t1 + LB_ROWS] - wv[t0:t0 + LB_ROWS])
        s = tot[lb_at:lb_at + LB_ROWS] * p1 * (1.0 - p1)
        g_bias = tot[0:MOD_ROWS]
        for q in range(1, n_seq):
            g_bias = g_bias + tot[q * MOD_ROWS:(q + 1) * MOD_ROWS]
        g = jnp.concatenate([g_bias, tot[mod_end:lb_at], -s, s], axis=0)
        g_ref[...] = g
        dl_ref[...], nm_ref[...], nv_ref[...] = _adamw_math(g, wv, m_ref[...], v_ref[...])
        loss_ref[...] = tot[lb_at + LB_ROWS:lb_at + LB_ROWS + SUBLANES]

    shp = _sds((SMALL_ROWS, LANE), F32)
    return pl.pallas_call(body, name="small_update", out_shape=[shp] * 4 + [_sds((SUBLANES, LANE), F32)],
                          compiler_params=_params())(packs, w, m, v)


def kernel(x, c, w_ada, b_ada, pre_w_mix, w_in, attn_sinks, attn_out_w, lb_table, hg_norm_w, w_out, post_w_mix, pre_w_mlp, w_up, w_down, post_w_mlp, loss_target, m_w_ada, m_b_ada, m_pre_w_mix, m_w_in, m_attn_sinks, m_attn_out_w, m_lb_table, m_hg_norm_w, m_w_out, m_post_w_mix, m_pre_w_mlp, m_w_up, m_w_down, m_post_w_mlp, v_w_ada, v_b_ada, v_pre_w_mix, v_w_in, v_attn_sinks, v_attn_out_w, v_lb_table, v_hg_norm_w, v_w_out, v_post_w_mix, v_pre_w_mlp, v_w_up, v_w_down, v_post_w_mlp):
    xi, yi, ci = _mesh_pos()
    chip = 2 * xi + yi
    dev = 2 * chip + ci
    bsz, seq, _ = x.shape
    ntok = bsz * seq
    ada_cols = w_ada.shape[2]
    core = jnp.reshape(ci, (1,)).astype(jnp.int32)
    chip_idx = jnp.reshape(chip, (1,)).astype(jnp.int32)
    flat = lambda a: a.reshape(ntok, a.shape[-1])
    unflat = lambda a: a.reshape(bsz, seq, a.shape[-1])
    tables = _rope_tables(seq)
    biases, chunk_masks = _band_biases(), _block_masks()

    def row_half(w):
        rows = w.shape[1] // 2
        return lax.dynamic_slice_in_dim(w[0], ci * rows, rows, axis=0).astype(BF16)

    def gather_buffer(w):
        rows, cols = w.shape[1] // 2, w.shape[2]
        own = w[0].astype(BF16).reshape(2, rows, cols)
        return lax.dynamic_update_slice(lax.empty((N_DEV, rows, cols), BF16), own, (2 * chip, 0, 0))

    w_in_t, m_in_t, v_in_t = [jnp.transpose(a[0])[None] for a in (w_in, m_w_in, v_w_in)]
    c_g, in_g = _allgather8([c, row_half(w_in_t)], "gather_first")
    c_all = c_g.reshape(N_DEV * bsz, D_MODEL)
    w_in_full = in_g.reshape(IN_COLS, D_MODEL)

    b_cols = lax.dynamic_slice_in_dim(b_ada, chip * ada_cols, ada_cols, axis=1)
    mod_part = _ada_fwd(c_all, w_ada[0], b_cols)
    half_rows = mod_part.shape[0] // 2
    (mod_g,) = _allgather8([lax.dynamic_slice_in_dim(mod_part, ci * half_rows, half_rows, axis=0)], "gather_mod")
    mod_all = mod_g.reshape(N_CHIPS, 2, half_rows, ada_cols).transpose(1, 2, 0, 3).reshape(N_DEV * bsz, N_MOD * D_MODEL)
    mod = lax.dynamic_slice_in_dim(mod_all, dev * bsz, bsz, axis=0)
    sh1, sc1, g1, sh2, sc2, g2 = [mod[:, i * D_MODEL:(i + 1) * D_MODEL].reshape(bsz, 1, D_MODEL) for i in range(N_MOD)]

    weights = (w_out, w_up, w_down)
    (out_part, up_part, down_part), started = _gather_start(
        [row_half(w) for w in weights], [gather_buffer(w) for w in weights], [mod_g], "gather_weights_start")

    h1, proj, qh, kh, vh = _in_proj_fused(x, pre_w_mix, sc1 + started[0:1, 0:1], sh1, w_in_full, tables)
    out_g = _gather_wait(out_part, [proj], "gather_out_wait")
    (attn_raw, cat, lse), ((out_g,),) = _attn_fwd(qh, kh, vh, attn_sinks, attn_out_w, biases, comms=[_plan_pair_forward([out_g])])
    up_g = _gather_wait(up_part, [attn_raw], "gather_up_wait")
    (o_raw, cat, states), ((up_g,),) = _hgrn_fwd(proj, lb_table, hg_norm_w, cat, chunk_masks, comms=[_plan_pair_forward([up_g])])
    down_g = _gather_wait(down_part, [o_raw], "gather_down_wait")
    w_out_full = out_g.reshape(D_MODEL, D_MODEL)
    w_up4 = up_g.reshape(N_CHIPS, D_MODEL, D_MODEL)
    mix, x1, h2 = _out_proj_fused(cat, w_out_full, x, post_w_mix, g1, pre_w_mlp, sc2, sh2)
    big_tm = min(ntok, 2048)
    up_spec = pl.BlockSpec((None, D_MODEL, D_MODEL), lambda i, j: (j, 0, 0))
    r, ((down_g,),) = _mm(flat(h2), w_up4, name="up_proj", out_dtype=BF16, tm=big_tm, tn=D_MODEL, n_out=D_FF, b_spec=up_spec,
                          epi=lambda acc: jnp.maximum(acc, 0.0), comms=[_plan_pair_forward([down_g])])
    w_down_full = down_g.reshape(D_FF, D_MODEL)
    square = lambda t: t * t
    loss_row, dy, dd, dg2, d_post_mlp = _down_proj_fused(unflat(r), w_down_full, x1, post_w_mlp, g2, loss_target)

    dpre = _mm(flat(dd), w_down_full, name="down_bwd", out_dtype=BF16, trans_b=True, tm=big_tm, tn=D_MODEL, extra=(r,),
               epi=lambda acc, rt: acc * (2.0 * rt.astype(F32)))
    half_rows = D_MODEL // 2
    g_down = _mm_tn(r, flat(dd), name="down_wgrad", tk=half_rows, tn=D_MODEL, a_fn=square,
                    out_shape=_sds((2, N_CHIPS, half_rows, D_MODEL), F32),
                    out_spec=pl.BlockSpec((None, None, half_rows, D_MODEL), lambda i, j: (i % 2, i // 2, 0, 0)))
    (dx1, dmix, dsc2, dsh2, dg1, d_pre_mlp, d_post_mix), ((q_down,),) = _up_bwd_fused(
        unflat(dpre), w_up4, dy, x1, mix, pre_w_mlp, sc2, post_w_mix, g1, comms=[_plan_pair([g_down], True)])
    g_up = _mm_tn(flat(h2), dpre, name="up_wgrad", tk=D_MODEL, tn=half_rows,
                  out_shape=_sds((2, N_CHIPS, half_rows, D_MODEL), F32),
                  out_spec=pl.BlockSpec((2, None, half_rows, half_rows), lambda i, j: (0, j // 2, 0, j % 2)))
    s_down = _pair_sum(g_down, q_down, core, "pair_sum_down")

    dcat, ((q_up,),) = _mm(flat(dmix), w_out_full, name="out_bwd", out_dtype=F32, trans_b=True, comms=[_plan_pair([g_up], True)])
    dcat = unflat(dcat)
    s_up = _pair_sum(g_up, q_up, core, "pair_sum_up")
    out_rows = D_MODEL // N_CHIPS
    g_out = _mm_tn(flat(cat), flat(dmix), name="out_wgrad", tk=2 * out_rows, tn=half_rows,
                   out_shape=_sds((2, N_CHIPS, out_rows, half_rows), F32),
                   out_spec=pl.BlockSpec((None, 2, out_rows, half_rows), lambda i, j: (j, i, 0, 0)))
    (dproj_rec, d_lb, d_hg_norm), ((x_down,), (q_out,)) = _hgrn_bwd(
        dcat, proj, o_raw, states, lb_table, hg_norm_w, chunk_masks, comms=[_plan_chip_exchange([s_down]), _plan_pair([g_out], True)])
    half_down = _sum_chips(s_down, x_down, chip_idx, "sum_chips_down")
    s_out = _pair_sum(g_out, q_out, core, "pair_sum_out")
    (dproj, d_attn_out, d_sinks), ((their_down,), (x_up,)) = _attn_bwd(
        dcat, attn_raw, attn_out_w, qh, kh, vh, lse, attn_sinks, tables, [bias.T for bias in biases], dproj_rec,
        comms=[_plan_pair([half_down], False), _plan_chip_exchange([s_up])])
    half_up = _sum_chips(s_up, x_up, chip_idx, "sum_chips_up")
    dproj = flat(dproj)
    in_rows = IN_COLS // N_CHIPS // 2
    g_in, ((x_out,),) = _mm_tn(dproj, flat(h1), name="in_wgrad", tk=2 * LANE, tn=D_MODEL, comms=[_plan_chip_exchange([s_out])])
    g_in = g_in.reshape(N_CHIPS, 2, in_rows, D_MODEL)
    half_out = _sum_chips(s_out, x_out, chip_idx, "sum_chips_out")
    dh1, ((q_in,), (their_up, their_out)) = _mm(
        dproj, w_in_full, name="in_bwd", out_dtype=F32,
        comms=[_plan_pair([g_in], "chip_major"), _plan_pair([half_up, half_out], False)])
    s_in = _pair_sum(g_in, q_in, core, "pair_sum_in", chip_major=True)
    in_sems, s_in, in_landing, started = _exchange_start(s_in, "exchange_in_start")
    grad_x, dsc1, dsh1, d_pre_mix = _norm1_bwd(unflat(dh1), dx1, x, pre_w_mix + started[0:1, 0:1], sc1)

    dmod = jnp.concatenate([dsh1, dsc1, dg1, dsh2, dsc2, dg2], axis=-1).reshape(bsz, N_MOD * D_MODEL)
    pack = _pack_partials(dmod, [d_pre_mix, d_post_mix, d_pre_mlp, d_post_mlp, d_attn_out, d_hg_norm, d_sinks], d_lb, loss_row)
    ((packs,),) = _comm_only([_plan_allgather8([pack])], "gather_small")
    w_small = dict(b_ada=b_ada, pre_w_mix=pre_w_mix, post_w_mix=post_w_mix, pre_w_mlp=pre_w_mlp, post_w_mlp=post_w_mlp,
                   attn_out_w=attn_out_w, hg_norm_w=hg_norm_w, attn_sinks=attn_sinks, lb_table=lb_table)
    m_small = dict(b_ada=m_b_ada, pre_w_mix=m_pre_w_mix, post_w_mix=m_post_w_mix, pre_w_mlp=m_pre_w_mlp, post_w_mlp=m_post_w_mlp,
                   attn_out_w=m_attn_out_w, hg_norm_w=m_hg_norm_w, attn_sinks=m_attn_sinks, lb_table=m_lb_table)
    v_small = dict(b_ada=v_b_ada, pre_w_mix=v_pre_w_mix, post_w_mix=v_post_w_mix, pre_w_mlp=v_pre_w_mlp, post_w_mlp=v_post_w_mlp,
                   attn_out_w=v_attn_out_w, hg_norm_w=v_hg_norm_w, attn_sinks=v_attn_sinks, lb_table=v_lb_table)
    *small_packed, loss_rows = _small_update(packs, _pack_small(w_small), _pack_small(m_small), _pack_small(v_small), bsz)
    small_out = [_unpack_small(p) for p in small_packed]
    loss = loss_rows[0, 0]

    dmod_all = packs[:, :bsz * MOD_ROWS, :].reshape(N_DEV * bsz, N_MOD * D_MODEL)
    dmod_cols = lax.dynamic_slice_in_dim(dmod_all, chip * ada_cols, ada_cols, axis=1)
    ada_out = _ada_bwd_adamw(c_all, dmod_cols, w_ada[0], m_w_ada[0], v_w_ada[0])

    s_in, x_in = _exchange_wait(in_sems, s_in, in_landing, [grad_x, ada_out[0]], "exchange_in_wait")
    half_in = _sum_chips(s_in, x_in, chip_idx, "sum_chips_in")
    ((their_in,),) = _comm_only([_plan_pair([half_in], False)], "pair_swap_in")
    big = dict(
        w_in=tuple(jnp.transpose(a) for a in _adamw_halves(half_in, their_in, core, w_in_t[0], m_in_t[0], v_in_t[0], axis=0,
                                                           name="adamw_in")),
        w_up=tuple(_adamw_halves(half_up, their_up, core, w_up[0], m_w_up[0], v_w_up[0], axis=0, name="adamw_up")),
        w_out=tuple(_adamw_halves(half_out, their_out, core, w_out[0], m_w_out[0], v_w_out[0], axis=1, name="adamw_out")),
        w_down=tuple(_adamw_halves(half_down, their_down, core, w_down[0], m_w_down[0], v_w_down[0], axis=0, name="adamw_down")),
        w_ada=tuple(ada_out),
    )
    order = ("w_ada", "b_ada", "pre_w_mix", "w_in", "attn_sinks", "attn_out_w", "lb_table", "hg_norm_w", "w_out", "post_w_mix",
             "pre_w_mlp", "w_up", "w_down", "post_w_mlp")
    outs = [loss, grad_x]
    for kind in range(4):
        for nm in order:
            outs.append(big[nm][kind][None] if nm in big else small_out[kind][nm])
    return tuple(outs)
```

```python
import jax
import jax.numpy as jnp
from jax import lax
from jax.experimental import pallas as pl
from jax.experimental.pallas import tpu as pltpu

F32 = jnp.float32
BF16 = jnp.bfloat16

D_MODEL = 1024
ATT_WIDTH = 512
ATT_HEAD_DIM = 64
ATT_Q_HEADS = 8
ATT_KV_HEADS = 2
ATT_GROUP = ATT_Q_HEADS // ATT_KV_HEADS
ATT_KV_COLS = ATT_KV_HEADS * ATT_HEAD_DIM
WINDOW = 128
ROPE_DIM = 16
ROPE_THETA = 500000.0
HG_WIDTH = 512
MIX_WIDTH = ATT_WIDTH + HG_WIDTH
HG_HEAD_DIM = 128
HG_HEADS = 4
HG_CHUNK = 32
IN_COLS = ATT_WIDTH + 2 * ATT_KV_COLS + 4 * HG_WIDTH
ATT_COLS = ATT_WIDTH + 2 * ATT_KV_COLS
D_FF = 4 * D_MODEL
N_MOD = 6
EPS = 1e-6
ATT_SCALE = ATT_HEAD_DIM ** -0.5

ADAM_LR = 0.001
ADAM_B1 = 0.9
ADAM_B2 = 0.999
ADAM_EPS = 1e-08
ADAM_WD = 0.01
ADAM_STEP = 10

N_CHIPS = 4
N_DEV = 8
LANE = 128
VMEM_LIMIT = 48 * 1024 * 1024
VMEM_LIMIT_BIG = 58 * 1024 * 1024
MESH = pl.DeviceIdType.MESH

NT_DIMS = (((1,), (1,)), ((), ()))
TN_DIMS = (((0,), (0,)), ((), ()))


def _sds(shape, dtype):
    return jax.ShapeDtypeStruct(tuple(shape), dtype)


def _params(*sem, vmem_limit=None):
    return pltpu.CompilerParams(dimension_semantics=sem, vmem_limit_bytes=VMEM_LIMIT if vmem_limit is None else vmem_limit)


def _sigmoid(x):
    return 1.0 / (1.0 + jnp.exp(-x))


def _dot(a, b, dims=None):
    a, b = a.astype(BF16), b.astype(BF16)
    if dims is None:
        return jnp.dot(a, b, preferred_element_type=F32)
    return lax.dot_general(a, b, dims, preferred_element_type=F32)


def _rms_fwd(x, w):
    rstd = lax.rsqrt(jnp.mean(x * x, axis=-1, keepdims=True) + EPS)
    xh = x * rstd
    return xh * w, xh, rstd


def _rms_bwd(dy, xh, rstd, w):
    dxh = dy * w
    dx = rstd * (dxh - xh * jnp.mean(dxh * xh, axis=-1, keepdims=True))
    return dx, dy * xh


def _colsum(x):
    return jnp.sum(x, axis=0, keepdims=True)


def _rms_hat(x):
    rstd = lax.rsqrt(jnp.mean(x * x, axis=-1, keepdims=True) + EPS)
    return x * rstd, rstd


def _rms_bwd_gain(dy, gain, xh, rstd):
    dxh = dy * gain
    dx = rstd * (dxh - xh * jnp.mean(dxh * xh, axis=-1, keepdims=True))
    return dx, _colsum(dy * xh)


def _row_tile(rows, cap=256):
    return max(t for t in range(16, cap + 1, 16) if rows % t == 0)


HBM_SPEC = pl.BlockSpec(memory_space=pltpu.HBM)


def _mesh_pos():
    return lax.axis_index("x"), lax.axis_index("y"), lax.axis_index("c")


class _Comm:
    def __init__(self, ins, outs, sems, start, finish, aliases=()):
        self.ins, self.outs, self.sems = list(ins), list(outs), list(sems)
        self.start, self.finish, self.aliases = start, finish, tuple(aliases)


def _call(body, args, *, name, grid, in_specs, out_specs, out_shape, sem, scratch_shapes=(), comms=(), aliases=None,
          vmem_limit=None):
    scratch_shapes = list(scratch_shapes)
    if not comms:
        return pl.pallas_call(body, name=name, grid=grid, in_specs=in_specs, out_specs=out_specs, out_shape=out_shape,
                              input_output_aliases=dict(aliases or {}), scratch_shapes=scratch_shapes,
                              compiler_params=_params(*sem, vmem_limit=vmem_limit))(*args)
    single = not isinstance(out_shape, (list, tuple))
    out_specs_l = [out_specs] if single else list(out_specs)
    out_shape_l = [out_shape] if single else list(out_shape)
    n_in, n_out, n_scr = len(in_specs), len(out_shape_l), len(scratch_shapes)
    n_ci = [len(cm.ins) for cm in comms]
    n_co = [len(cm.outs) for cm in comms]
    n_cs = [len(cm.sems) for cm in comms]
    aliases = dict(aliases or {})
    for k, cm in enumerate(comms):
        for i, o in cm.aliases:
            aliases[n_in + sum(n_ci[:k]) + i] = n_out + sum(n_co[:k]) + o

    def fused(*refs):
        pos = [0]

        def take(n):
            part = refs[pos[0]:pos[0] + n]
            pos[0] += n
            return part

        ins = take(n_in)
        c_ins = [take(n) for n in n_ci]
        outs = take(n_out)
        c_outs = [take(n) for n in n_co]
        scr = take(n_scr)
        c_sems = [take(n) for n in n_cs]
        first, last = True, True
        for d, size in enumerate(grid):
            first = jnp.logical_and(first, pl.program_id(d) == 0)
            last = jnp.logical_and(last, pl.program_id(d) == size - 1)

        def run(which):
            for cm, ci, co, cs in zip(comms, c_ins, c_outs, c_sems):
                getattr(cm, which)(ci, co, cs)

        if grid:
            pl.when(first)(lambda: run("start"))
        else:
            run("start")
        body(*ins, *outs, *scr)
        if grid:
            pl.when(last)(lambda: run("finish"))
        else:
            run("finish")

    res = pl.pallas_call(
        fused, name=name, grid=grid, in_specs=list(in_specs) + [HBM_SPEC] * sum(n_ci),
        out_specs=out_specs_l + [HBM_SPEC] * sum(n_co), out_shape=out_shape_l + [s for cm in comms for s in cm.outs],
        input_output_aliases=aliases, scratch_shapes=scratch_shapes + [s for cm in comms for s in cm.sems],
        compiler_params=_params(*["arbitrary"] * len(grid), vmem_limit=vmem_limit),
    )(*args, *[a for cm in comms for a in cm.ins])
    main = res[:n_out]
    extra, at = [], n_out
    for n in n_co:
        extra.append(list(res[at:at + n]))
        at += n
    return (main[0] if single else list(main)), extra


def _mm(a, b, *, name, out_dtype, trans_b=False, tm=512, tn=None, extra=(), epi=None, b_spec=None, n_out=None, comms=()):
    m_total, k_total = a.shape
    if n_out is None:
        n_out = b.shape[0] if trans_b else b.shape[1]
    tn = n_out if tn is None else tn
    grid = (m_total // tm, n_out // tn)
    dims = NT_DIMS if trans_b else None

    def body(*refs):
        a_ref, b_ref = refs[0], refs[1]
        extra_refs = refs[2:2 + len(extra)]
        o_ref = refs[2 + len(extra)]
        acc = _dot(a_ref[...], b_ref[...], dims)
        if epi is not None:
            acc = epi(acc, *[r[...] for r in extra_refs])
        o_ref[...] = acc.astype(out_dtype)

    if b_spec is None:
        if trans_b:
            b_spec = pl.BlockSpec((tn, k_total), lambda i, j: (j, 0))
        else:
            b_spec = pl.BlockSpec((k_total, tn), lambda i, j: (0, j))
    in_specs = [pl.BlockSpec((tm, k_total), lambda i, j: (i, 0)), b_spec]
    in_specs += [pl.BlockSpec((tm, tn), lambda i, j: (i, j)) for _ in extra]
    return _call(
        body, (a, b, *extra), name=name, grid=grid, in_specs=in_specs,
        out_specs=pl.BlockSpec((tm, tn), lambda i, j: (i, j)),
        out_shape=_sds((m_total, n_out), out_dtype),
        sem=("parallel", "parallel"), comms=comms)


def _mm_tn(a, b, *, name, tk, tn, a_fn=None, out_shape=None, out_spec=None, comms=()):
    m_total, k_total = a.shape
    n_total = b.shape[1]
    grid = (k_total // tk, n_total // tn)

    def body(a_ref, b_ref, o_ref):
        av = a_ref[...]
        part = _dot(av if a_fn is None else a_fn(av), b_ref[...], TN_DIMS)
        o_ref[...] = part.reshape(o_ref.shape)

    if out_shape is None:
        out_shape = _sds((k_total, n_total), F32)
        out_spec = pl.BlockSpec((tk, tn), lambda i, j: (i, j))
    return _call(
        body, (a, b), name=name, grid=grid,
        in_specs=[pl.BlockSpec((m_total, tk), lambda i, j: (0, i)), pl.BlockSpec((m_total, tn), lambda i, j: (0, j))],
        out_specs=out_spec, out_shape=out_shape, sem=("parallel", "parallel"), comms=comms)


def _ada_fwd(c_all, w_shard, b_shard):
    nb, ncol = c_all.shape[0], w_shard.shape[1]
    tn = 512

    def body(c_ref, w_ref, b_ref, o_ref):
        c = c_ref[...]
        o_ref[...] = _dot(c * _sigmoid(c), w_ref[...]) + b_ref[...]

    return pl.pallas_call(
        body, name="ada_fwd", grid=(ncol // tn,),
        in_specs=[pl.BlockSpec((nb, D_MODEL), lambda j: (0, 0)), pl.BlockSpec((D_MODEL, tn), lambda j: (0, j)),
                  pl.BlockSpec((1, tn), lambda j: (0, j))],
        out_specs=pl.BlockSpec((nb, tn), lambda j: (0, j)), out_shape=_sds((nb, ncol), F32),
        compiler_params=_params("parallel"),
    )(c_all, w_shard, b_shard)


def _adamw_math(g, w, m, v):
    m = ADAM_B1 * m + (1.0 - ADAM_B1) * g
    v = ADAM_B2 * v + (1.0 - ADAM_B2) * (g * g)
    m_hat = m / (1.0 - ADAM_B1 ** ADAM_STEP)
    v_hat = v / (1.0 - ADAM_B2 ** ADAM_STEP)
    delta = -ADAM_LR * (m_hat / (jnp.sqrt(v_hat) + ADAM_EPS) + ADAM_WD * w)
    return delta, m, v


def _ada_bwd_adamw(c_all, dmod_cols, w, m, v):
    nb, ncol = dmod_cols.shape
    tn = 256

    def body(c_ref, d_ref, w_ref, m_ref, v_ref, g_ref, dl_ref, nm_ref, nv_ref):
        c = c_ref[...]
        g = _dot(c * _sigmoid(c), d_ref[...], TN_DIMS)
        g_ref[...] = g
        dl_ref[...], nm_ref[...], nv_ref[...] = _adamw_math(g, w_ref[...], m_ref[...], v_ref[...])

    col = pl.BlockSpec((D_MODEL, tn), lambda j: (0, j))
    shp = _sds((D_MODEL, ncol), F32)
    return pl.pallas_call(
        body, name="ada_bwd_adamw", grid=(ncol // tn,),
        in_specs=[pl.BlockSpec((nb, D_MODEL), lambda j: (0, 0)), pl.BlockSpec((nb, tn), lambda j: (0, j)), col, col, col],
        out_specs=[col, col, col, col], out_shape=[shp, shp, shp, shp],
        compiler_params=_params("parallel"),
    )(c_all, dmod_cols, w, m, v)


def _adamw_halves(own, theirs, core, w, m, v, *, axis, name):
    r2, c2 = own.shape
    tr = _row_tile(r2)
    nt = r2 // tr

    def body(core_ref, own_ref, their_ref, w_ref, m_ref, v_ref, g_ref, dl_ref, nm_ref, nv_ref):
        g = jnp.where(pl.program_id(0) == core_ref[0], own_ref[...], their_ref[...])
        g_ref[...] = g
        dl_ref[...], nm_ref[...], nv_ref[...] = _adamw_math(g, w_ref[...], m_ref[...], v_ref[...])

    if axis == 0:
        full = pl.BlockSpec((tr, c2), lambda h, i, core_ref: (h * nt + i, 0))
    else:
        full = pl.BlockSpec((tr, c2), lambda h, i, core_ref: (i, h))
    half = pl.BlockSpec((tr, c2), lambda h, i, core_ref: (i, 0))
    shp = _sds(w.shape, F32)
    return pl.pallas_call(
        body, name=name,
        grid_spec=pltpu.PrefetchScalarGridSpec(num_scalar_prefetch=1, grid=(2, nt), in_specs=[half, half, full, full, full],
                                               out_specs=[full] * 4),
        out_shape=[shp] * 4, compiler_params=_params("parallel", "parallel"),
    )(core, own, theirs, w, m, v)


def _tok_spec(tm, width=D_MODEL):
    return pl.BlockSpec((None, tm, width), lambda b, i: (b, i, 0))


def _row_spec(width=D_MODEL):
    return pl.BlockSpec((None, 1, width), lambda b, i: (b, 0, 0))


def _vec_spec(width=D_MODEL):
    return pl.BlockSpec((1, width), lambda b, i: (0, 0))


class _RowsOf:
    def __init__(self, ref, first, count):
        self.ref, self.rows = ref, slice(first, first + count)

    def __getitem__(self, idx):
        return self.ref[self.rows, :]

    def __setitem__(self, idx, value):
        self.ref[self.rows, :] = value


def _mm_rows(a, b, *, name, tm, extra, extra_specs, out_specs, out_shape, epi, pro=None, trans_b=False, b_chunks=1, comms=(),
             parts=1, zero_per_seq=(), zero_once=(), vmem_limit=None):
    bsz, seq, k_total = a.shape
    kc = k_total // b_chunks
    dims = NT_DIMS if trans_b else None
    rows = tm // parts

    def body(*refs):
        a_ref, b_ref = refs[0], refs[1]
        ex, outs = refs[2:2 + len(extra)], refs[2 + len(extra):]
        if zero_per_seq:
            @pl.when(pl.program_id(1) == 0)
            def _():
                for k in zero_per_seq:
                    outs[k][...] = jnp.zeros_like(outs[k])
        if zero_once:
            @pl.when(jnp.logical_and(pl.program_id(0) == 0, pl.program_id(1) == 0))
            def _():
                for k in zero_once:
                    outs[k][...] = jnp.zeros_like(outs[k])

        def part_of(ref, p):
            tiled = len(ref.shape) == 2 and ref.shape[0] == tm
            return _RowsOf(ref, p * rows, rows) if tiled and parts > 1 else ref

        accs = []
        for p in range(parts):
            a_p, ex_p, outs_p = part_of(a_ref, p), [part_of(r, p) for r in ex], [part_of(r, p) for r in outs]
            if b_chunks == 1:
                accs.append(_dot(a_p[...] if pro is None else pro(a_p, ex_p, outs_p), b_ref[...], dims))
            else:
                acc = _dot(a_p[...][:, 0:kc], b_ref[0], NT_DIMS)
                for k in range(1, b_chunks):
                    acc = acc + _dot(a_p[...][:, k * kc:(k + 1) * kc], b_ref[k], NT_DIMS)
                accs.append(acc)
        for p in range(parts):
            epi(accs[p], [part_of(r, p) for r in ex], [part_of(r, p) for r in outs])

    b_spec = pl.BlockSpec(b.shape, lambda bb, i: (0,) * b.ndim)
    return _call(
        body, (a, b, *extra), name=name, grid=(bsz, seq // tm), in_specs=[_tok_spec(tm, k_total), b_spec, *extra_specs],
        out_specs=out_specs, out_shape=out_shape, sem=("arbitrary", "arbitrary"), comms=comms, vmem_limit=vmem_limit)


def _in_proj_fused(x, w, sc, sh, w_in_t, tables, comms=()):
    tm = 512
    bsz, seq, _ = x.shape
    half = ROPE_DIM // 2
    heads_per_slab = LANE // ATT_HEAD_DIM

    def pro(x_ref, ex, outs):
        y, _, _ = _rms_fwd(x_ref[...], ex[0][...])
        h = (y * (1.0 + ex[1][...]) + ex[2][...]).astype(BF16)
        outs[0][...] = h
        return h

    def epi(acc, ex, outs):
        c, u, d = ex[3][...], ex[4][...], ex[5][...]
        _, rec_ref, q_ref, k_ref, v_ref = outs
        for k in range(HG_SLABS):
            rec_ref[k] = acc[:, ATT_COLS + k * HG_WIDTH:ATT_COLS + (k + 1) * HG_WIDTH]

        def rope(z):
            return (z * c + pltpu.roll(z, half, 1) * u + pltpu.roll(z, LANE - half, 1) * d).astype(BF16)

        for s in range(ATT_WIDTH // LANE):
            slab = rope(acc[:, s * LANE:(s + 1) * LANE])
            for part in range(heads_per_slab):
                g, hh = divmod(s * heads_per_slab + part, ATT_GROUP)
                piece = slab[:, part * ATT_HEAD_DIM:(part + 1) * ATT_HEAD_DIM]
                for blk in range(tm // WINDOW):
                    q_ref[blk, g, hh * WINDOW:(hh + 1) * WINDOW, :] = piece[blk * WINDOW:(blk + 1) * WINDOW]
        rk = rope(acc[:, ATT_WIDTH:ATT_WIDTH + LANE])
        vv = acc[:, ATT_WIDTH + LANE:ATT_COLS].astype(BF16)
        for g in range(ATT_KV_HEADS):
            k_ref[g] = rk[:, g * ATT_HEAD_DIM:(g + 1) * ATT_HEAD_DIM]
            v_ref[g] = vv[:, g * ATT_HEAD_DIM:(g + 1) * ATT_HEAD_DIM]

    tab = pl.BlockSpec((tm, LANE), lambda b, i: (i, 0))
    kv_spec = pl.BlockSpec((None, ATT_KV_HEADS, tm, ATT_HEAD_DIM), lambda b, i: (b, 0, i, 0))
    kv_shape = _sds((bsz, ATT_KV_HEADS, seq, ATT_HEAD_DIM), BF16)
    q_spec = pl.BlockSpec((None, tm // WINDOW, ATT_KV_HEADS, GROUP_ROWS, ATT_HEAD_DIM), lambda b, i: (b, i, 0, 0, 0))
    return _mm_rows(x, w_in_t, name="in_proj", tm=tm, extra=(w, sc, sh, *tables),
                    extra_specs=[_vec_spec(), _row_spec(), _row_spec(), tab, tab, tab],
                    out_specs=[_tok_spec(tm), pl.BlockSpec((None, HG_SLABS, tm, HG_WIDTH), lambda b, i: (b, 0, i, 0)), q_spec,
                               kv_spec, kv_spec],
                    out_shape=[_sds(x.shape, BF16), _sds((bsz, HG_SLABS, seq, HG_WIDTH), F32),
                               _sds((bsz, seq // WINDOW, ATT_KV_HEADS, GROUP_ROWS, ATT_HEAD_DIM), BF16), kv_shape, kv_shape],
                    pro=pro, epi=epi, trans_b=True, comms=comms)


def _rope_tables(seq):
    half = ROPE_DIM // 2
    inv_freq = ROPE_THETA ** (-jnp.arange(0, ROPE_DIM, 2, dtype=F32) / ROPE_DIM)
    ang = jnp.arange(seq, dtype=F32)[:, None] * inv_freq[None, :]
    cos, sin = jnp.cos(ang), jnp.sin(ang)
    rest = ATT_HEAD_DIM - ROPE_DIM
    ones, zeros, zh = jnp.ones((seq, rest), F32), jnp.zeros((seq, rest), F32), jnp.zeros((seq, half), F32)
    reps = LANE // ATT_HEAD_DIM
    t_cos = jnp.tile(jnp.concatenate([cos, cos, ones], axis=1), (1, reps))
    t_up = jnp.tile(jnp.concatenate([zh, sin, zeros], axis=1), (1, reps))
    t_dn = jnp.tile(jnp.concatenate([-sin, zh, zeros], axis=1), (1, reps))
    return t_cos, t_up, t_dn


GROUP_ROWS = ATT_GROUP * WINDOW


ATT_BPS = 2


MASKED = -1e30


def _band_biases():
    row = jnp.arange(GROUP_ROWS)[:, None] % WINDOW
    col = jnp.arange(2 * WINDOW)[None, :]
    own = jnp.logical_and(col >= WINDOW, col - WINDOW <= row)
    before = jnp.logical_and(col < WINDOW, col > row)
    return (jnp.where(jnp.logical_or(own, before), 0.0, MASKED).astype(F32), jnp.where(own, 0.0, MASKED).astype(F32))


def _band_bias(full_ref, first_ref, has_prev):
    return full_ref[...] if has_prev is True else jnp.where(has_prev, full_ref[...], first_ref[...])


def _sink_column(sink_ref, g):
    head = lax.broadcasted_iota(jnp.int32, (GROUP_ROWS, 1), 0) // WINDOW
    col = jnp.full((GROUP_ROWS, 1), sink_ref[0, g * ATT_GROUP], F32)
    for hh in range(1, ATT_GROUP):
        col = jnp.where(head == hh, sink_ref[0, g * ATT_GROUP + hh], col)
    return col


def _sink_row(sink_ref, g):
    return jnp.concatenate([jnp.full((1, WINDOW), sink_ref[0, g * ATT_GROUP + hh], F32) for hh in range(ATT_GROUP)], axis=1)


def _bias_spec(transposed=False):
    shape = (2 * WINDOW, GROUP_ROWS) if transposed else (GROUP_ROWS, 2 * WINDOW)
    return pl.BlockSpec(shape, lambda b, i: (0, 0))


def _attn_specs():
    q_spec = pl.BlockSpec((None, ATT_BPS, ATT_KV_HEADS, GROUP_ROWS, ATT_HEAD_DIM), lambda b, i: (b, i, 0, 0, 0))
    kv_cur = pl.BlockSpec((None, ATT_KV_HEADS, ATT_BPS * WINDOW, ATT_HEAD_DIM), lambda b, i: (b, 0, i, 0))
    kv_prev = pl.BlockSpec((None, ATT_KV_HEADS, WINDOW, ATT_HEAD_DIM), lambda b, i: (b, 0, jnp.maximum(ATT_BPS * i - 1, 0), 0))
    return q_spec, kv_cur, kv_prev


def _band(prev_ref, cur_ref, g, blk):
    own = cur_ref[g, blk * WINDOW:(blk + 1) * WINDOW]
    before = prev_ref[g] if blk == 0 else cur_ref[g, (blk - 1) * WINDOW:blk * WINDOW]
    return jnp.concatenate([before, own], axis=0)


def _attn_fwd(qh, kh, vh, sinks, w_norm, biases, comms=()):
    bsz, nblk = qh.shape[0], qh.shape[1]
    seq = nblk * WINDOW
    rows = ATT_BPS * WINDOW

    def body(sink_ref, q_ref, kc_ref, kp_ref, vc_ref, vp_ref, w_ref, full_ref, first_ref, raw_ref, an_ref, l_ref):
        l_ref[...] = jnp.zeros_like(l_ref)
        for blk in range(ATT_BPS):
            bias = _band_bias(full_ref, first_ref, True if blk else pl.program_id(1) > 0)
            groups = range(ATT_KV_HEADS)
            keys, vals = [_band(kp_ref, kc_ref, g, blk) for g in groups], [_band(vp_ref, vc_ref, g, blk) for g in groups]
            sink = [_sink_column(sink_ref, g) for g in groups]
            s = [_dot(q_ref[blk, g], keys[g], NT_DIMS) * ATT_SCALE + bias for g in groups]
            m = [jnp.maximum(jnp.max(s[g], axis=-1, keepdims=True), sink[g]) for g in groups]
            p = [jnp.exp(s[g] - m[g]) for g in groups]
            den = [jnp.sum(p[g], axis=-1, keepdims=True) + jnp.exp(sink[g] - m[g]) for g in groups]
            o = [_dot(p[g] / den[g], vals[g]) for g in groups]
            lse = [m[g] + jnp.log(den[g]) for g in groups]
            tok = slice(blk * WINDOW, (blk + 1) * WINDOW)
            for g in groups:
                for hh in range(ATT_GROUP):
                    h = g * ATT_GROUP + hh
                    raw_ref[tok, h * ATT_HEAD_DIM:(h + 1) * ATT_HEAD_DIM] = o[g][hh * WINDOW:(hh + 1) * WINDOW]
                    l_ref[tok, h:h + 1] = lse[g][hh * WINDOW:(hh + 1) * WINDOW]
        y, _, _ = _rms_fwd(raw_ref[...], w_ref[...])
        an_ref[...] = y.astype(BF16)

    cur = lambda width: pl.BlockSpec((None, rows, width), lambda b, i: (b, i, 0))
    q_spec, kv_cur, kv_prev = _attn_specs()
    return _call(
        body, (sinks, qh, kh, kh, vh, vh, w_norm, *biases), name="attn_fwd", grid=(bsz, nblk // ATT_BPS),
        in_specs=[pl.BlockSpec(memory_space=pltpu.SMEM), q_spec, kv_cur, kv_prev, kv_cur, kv_prev, _vec_spec(ATT_WIDTH),
                  _bias_spec(), _bias_spec()],
        out_specs=[cur(ATT_WIDTH), cur(ATT_WIDTH), cur(LANE)],
        out_shape=[_sds((bsz, seq, ATT_WIDTH), F32), _sds((bsz, seq, MIX_WIDTH), BF16), _sds((bsz, seq, LANE), F32)],
        sem=("parallel", "parallel"), comms=comms)


HG_Q0 = ATT_COLS // LANE
HG_F0 = HG_Q0 + HG_HEADS
HG_I0 = HG_F0 + HG_HEADS
HG_G0 = HG_I0 + HG_HEADS
HG_SLABS = 4
HG_Q, HG_F, HG_I, HG_G = range(HG_SLABS)
HG_TOK = 256
HG_NCH = HG_TOK // HG_CHUNK
HG_HPS = 2


def _block_masks():
    row = jnp.arange(HG_TOK)[:, None]
    col = jnp.arange(HG_TOK)[None, :]
    same = (row // HG_CHUNK) == (col // HG_CHUNK)
    return jnp.logical_and(same, col <= row).astype(F32), jnp.logical_and(same, col >= row).astype(F32)


def _row_in_chunk():
    return lax.broadcasted_iota(jnp.int32, (HG_TOK, LANE), 0) % HG_CHUNK


def _chunk_cumsum(x, reverse=False):
    ric = _row_in_chunk()
    shift = 1
    while shift < HG_CHUNK:
        if reverse:
            x = x + jnp.where(ric < HG_CHUNK - shift, pltpu.roll(x, HG_TOK - shift, 0), 0.0)
        else:
            x = x + jnp.where(ric >= shift, pltpu.roll(x, shift, 0), 0.0)
        shift *= 2
    return x


def _chunk_rows(rows):
    stacked = jnp.concatenate([r[None] for r in rows], axis=0)
    return jnp.broadcast_to(stacked, (HG_NCH, HG_CHUNK, LANE)).reshape(HG_TOK, LANE)


def _chunk_slices(x):
    return [x[j * HG_CHUNK:(j + 1) * HG_CHUNK] for j in range(HG_NCH)]


def _in_step(stages):
    stages = list(stages)
    while stages:
        stages = [g for g in stages if next(g, stages) is not stages]


def _hgrn_common(tbl, hf, hq):
    lb = _sigmoid(tbl[1:2] - tbl[0:1])
    sig = _sigmoid(hf)
    f = lb + (1.0 - lb) * sig
    sq = _sigmoid(hq)
    q, k = hq * sq, 1.0 - f
    b = _chunk_cumsum(jnp.log(f))
    last = [b[(j + 1) * HG_CHUNK - 1:(j + 1) * HG_CHUNK] for j in range(HG_NCH)]
    bl = _chunk_rows(last)
    e_b, e_nb, e_rem = jnp.exp(b), jnp.exp(-b), jnp.exp(bl - b)
    e_last = [jnp.exp(r) for r in last]
    return dict(lb=lb, sig=sig, f=f, sq=sq, q=q, k=k, e_b=e_b, e_nb=e_nb, e_rem=e_rem, e_last=e_last,
                qd=q * e_b, kd=k * e_nb, ku=k * e_rem)


def _hgrn_fwd(proj, lb_table, norm_w, mix_in, masks, comms=()):
    bsz, _, seq, _ = proj.shape
    nstep = seq // HG_TOK

    def body(tbl_ref, nw_ref, p_ref, mix_ref, lower_ref, o_ref, rec_ref, st_ref, s_scr):
        @pl.when(pl.program_id(2) == 0)
        def _():
            s_scr[...] = jnp.zeros_like(s_scr)

        lower = lower_ref[...]

        def head(hp):
            ls = slice(hp * LANE, (hp + 1) * LANE)
            v, hg = p_ref[HG_I, :, ls], p_ref[HG_G, :, ls]
            t = _hgrn_common(tbl_ref[:, ls], p_ref[HG_F, :, ls], p_ref[HG_Q, :, ls])
            yield
            a = _dot(t["qd"], t["kd"], NT_DIMS) * lower
            o_intra = _dot(a, v)
            yield
            v_c, ku_c, qd_c = [_chunk_slices(z.astype(BF16)) for z in (v, t["ku"], t["qd"])]
            updates = [_dot(v_c[j], ku_c[j], TN_DIMS) for j in range(HG_NCH)]
            yield
            st = s_scr[hp]
            states = []
            for j in range(HG_NCH):
                states.append(st)
                st = st * t["e_last"][j] + updates[j]
            s_scr[hp] = st
            yield
            o = o_intra + jnp.concatenate([_dot(qd_c[j], states[j], NT_DIMS) for j in range(HG_NCH)], axis=0)
            yield
            st_ref[hp, 0] = states[0]
            o_ref[:, ls] = o
            y, _, _ = _rms_fwd(o, nw_ref[...])
            rec_ref[:, ls] = (y * (hg * _sigmoid(hg))).astype(BF16)

        _in_step(head(hp) for hp in range(HG_HPS))

    width = HG_HPS * LANE
    head_out = pl.BlockSpec((None, HG_TOK, width), lambda b, h, t: (b, t, h))
    mix_out = pl.BlockSpec((None, HG_TOK, width), lambda b, h, t: (b, t, ATT_WIDTH // width + h))
    return _call(
        body, (lb_table, norm_w, proj, mix_in, masks[0]), name="hgrn_fwd", grid=(bsz, HG_HEADS // HG_HPS, nstep),
        in_specs=[pl.BlockSpec((2, width), lambda b, h, t: (0, h)), pl.BlockSpec((1, LANE), lambda b, h, t: (0, 0)),
                  pl.BlockSpec((None, HG_SLABS, HG_TOK, width), lambda b, h, t: (b, 0, t, h)), pl.BlockSpec(memory_space=pl.ANY),
                  pl.BlockSpec((HG_TOK, HG_TOK), lambda b, h, t: (0, 0))],
        out_specs=[head_out, mix_out,
                   pl.BlockSpec((None, HG_HPS, 1, LANE, LANE), lambda b, h, t: (b, h, t, 0, 0))],
        out_shape=[_sds((bsz, seq, HG_WIDTH), F32), _sds(mix_in.shape, BF16),
                   _sds((bsz, HG_HEADS, nstep, LANE, LANE), F32)],
        scratch_shapes=[pltpu.VMEM((HG_HPS, LANE, LANE), F32)],
        sem=("parallel", "parallel", "arbitrary"), comms=comms, aliases={3: 1})


def _out_proj_fused(cat, w_out, x, post_w, g1, pre_w, sc2, sh2):
    tm = 512

    def epi(mix, ex, outs):
        x_ref, pw_ref, g1_ref, w2_ref, sc_ref, sh_ref = ex
        outs[0][...] = mix
        n1, _, _ = _rms_fwd(mix, pw_ref[...])
        x1 = x_ref[...] + g1_ref[...] * n1
        outs[1][...] = x1
        y2, _, _ = _rms_fwd(x1, w2_ref[...])
        outs[2][...] = (y2 * (1.0 + sc_ref[...]) + sh_ref[...]).astype(BF16)

    return _mm_rows(cat, w_out, name="out_proj", tm=tm, extra=(x, post_w, g1, pre_w, sc2, sh2),
                    extra_specs=[_tok_spec(tm), _vec_spec(), _row_spec(), _vec_spec(), _row_spec(), _row_spec()],
                    out_specs=[_tok_spec(tm), _tok_spec(tm), _tok_spec(tm)],
                    out_shape=[_sds(x.shape, F32), _sds(x.shape, F32), _sds(x.shape, BF16)], epi=epi)


def _acc_out(ref, first, value):
    @pl.when(first)
    def _():
        ref[...] = value

    @pl.when(jnp.logical_not(first))
    def _():
        ref[...] += value


def _down_proj_fused(r, w_down, x1, post_w, g2, target):
    tm = 512
    bsz = x1.shape[0]

    def pro(r_ref, ex, outs):
        rv = r_ref[...]
        return rv * rv

    def epi(down, ex, outs):
        x1_ref, w_ref, g2_ref, t_ref = ex
        loss_ref, dy_ref, dd_ref, dg2_ref, dw_ref = outs
        w, g2v = w_ref[...], g2_ref[...]
        gain = g2v * w
        dh, rstd = _rms_hat(down)
        err = x1_ref[...] + dh * gain - t_ref[...]
        part = (0.5 / D_MODEL) * jnp.sum(jnp.sum(err * err, axis=-1, keepdims=True), axis=0, keepdims=True)
        loss_ref[...] += jnp.broadcast_to(part, (1, LANE))
        dy = err * (1.0 / D_MODEL)
        dy_ref[...] = dy
        dd, per_col = _rms_bwd_gain(dy, gain, dh, rstd)
        dd_ref[...] = dd.astype(BF16)
        dg2_ref[...] += per_col * w
        dw_ref[...] += per_col * g2v

    return _mm_rows(r, w_down, name="down_proj", tm=tm, extra=(x1, post_w, g2, target),
                    extra_specs=[_tok_spec(tm), _vec_spec(), _row_spec(), _tok_spec(tm)],
                    out_specs=[_vec_spec(LANE), _tok_spec(tm), _tok_spec(tm), _row_spec(), _vec_spec()],
                    out_shape=[_sds((1, LANE), F32), _sds(x1.shape, F32), _sds(x1.shape, BF16), _sds((bsz, 1, D_MODEL), F32),
                               _sds((1, D_MODEL), F32)], pro=pro, epi=epi, parts=2, zero_per_seq=(3,), zero_once=(0, 4),
                    vmem_limit=VMEM_LIMIT_BIG)


def _up_bwd_fused(dpre, w_up4, dy, x1, mix, pre_w, sc2, post_w, g1, comms=()):
    tm = 512
    bsz = x1.shape[0]

    def epi(dh2v, ex, outs):
        dy_ref, x1_ref, mix_ref, w2_ref, sc_ref, pw_ref, g1_ref = ex
        dx1_ref, dmix_ref, dsc_ref, dsh_ref, dg1_ref, dw2_ref, dpw_ref = outs
        w2, pw, g1v = w2_ref[...], pw_ref[...], g1_ref[...]
        mod2 = 1.0 + sc_ref[...]
        xh2, rstd2 = _rms_hat(x1_ref[...])
        dsh_ref[...] += _colsum(dh2v)
        dx1n, per_col2 = _rms_bwd_gain(dh2v, mod2 * w2, xh2, rstd2)
        dsc_ref[...] += per_col2 * w2
        dw2_ref[...] += per_col2 * mod2
        dx1 = dy_ref[...] + dx1n
        dx1_ref[...] = dx1
        mh, rstd1 = _rms_hat(mix_ref[...])
        dmix, per_col1 = _rms_bwd_gain(dx1, g1v * pw, mh, rstd1)
        dmix_ref[...] = dmix.astype(BF16)
        dg1_ref[...] += per_col1 * pw
        dpw_ref[...] += per_col1 * g1v

    row_shape = _sds((bsz, 1, D_MODEL), F32)
    vec_shape = _sds((1, D_MODEL), F32)
    return _mm_rows(dpre, w_up4, name="up_bwd", tm=tm, extra=(dy, x1, mix, pre_w, sc2, post_w, g1),
                    extra_specs=[_tok_spec(tm), _tok_spec(tm), _tok_spec(tm), _vec_spec(), _row_spec(), _vec_spec(), _row_spec()],
                    out_specs=[_tok_spec(tm), _tok_spec(tm), _row_spec(), _row_spec(), _row_spec(), _vec_spec(), _vec_spec()],
                    out_shape=[_sds(x1.shape, F32), _sds(x1.shape, BF16), row_shape, row_shape, row_shape, vec_shape, vec_shape],
                    epi=epi, b_chunks=w_up4.shape[0], comms=comms, parts=2, zero_per_seq=(2, 3, 4), zero_once=(5, 6),
                    vmem_limit=VMEM_LIMIT_BIG)


def _norm1_bwd(dh1, dx1, x, pre_w, sc1, tm=512, comms=()):
    bsz, seq, _ = x.shape

    def body(dh_ref, dx1_ref, x_ref, w_ref, sc_ref, gx_ref, dsc_ref, dsh_ref, dw_ref):
        b, i = pl.program_id(0), pl.program_id(1)
        w = w_ref[...]
        dh = dh_ref[...]
        mod = 1.0 + sc_ref[...]
        xh, rstd = _rms_hat(x_ref[...])
        dx, per_col = _rms_bwd_gain(dh, mod * w, xh, rstd)
        _acc_out(dsh_ref, i == 0, _colsum(dh))
        _acc_out(dsc_ref, i == 0, per_col * w)
        _acc_out(dw_ref, jnp.logical_and(b == 0, i == 0), per_col * mod)
        gx_ref[...] = dx1_ref[...] + dx

    row_shape = _sds((bsz, 1, D_MODEL), F32)
    return _call(
        body, (dh1, dx1, x, pre_w, sc1), name="norm1_bwd", grid=(bsz, seq // tm),
        in_specs=[_tok_spec(tm), _tok_spec(tm), _tok_spec(tm), _vec_spec(), _row_spec()],
        out_specs=[_tok_spec(tm), _row_spec(), _row_spec(), _vec_spec()],
        out_shape=[_sds(x.shape, F32), row_shape, row_shape, _sds((1, D_MODEL), F32)],
        sem=("arbitrary", "arbitrary"), comms=comms)


def _hgrn_bwd(dcat, proj, o_raw, states, lb_table, norm_w, masks, comms=()):
    bsz, _, seq, _ = proj.shape
    nstep = seq // HG_TOK
    rec0 = ATT_WIDTH // LANE
    width = HG_HPS * LANE
    slabs = (HG_Q0, HG_F0, HG_I0, HG_G0)
    n_steps = (HG_HEADS // HG_HPS) * bsz * nstep
    assert n_steps >= 2

    def body(tbl_ref, nw_ref, dr_ref, p_ref, o_ref, st_ref, lower_ref, upper_ref,
             dproj_ref, dlb_ref, dnw_ref, ds_scr, grad_buf, grad_sem):
        h, b, t = pl.program_id(0), pl.program_id(1), pl.program_id(2)
        step = (h * bsz + b) * nstep + t
        slot = step % 2
        dq_k, df_k, di_k, dg_k = range(4)

        def grad_copies(of_step):
            hh, bb, tt = of_step // (bsz * nstep), (of_step // nstep) % bsz, of_step % nstep
            rows = pl.ds(pl.multiple_of((nstep - 1 - tt) * HG_TOK, HG_TOK), HG_TOK)
            return [pltpu.make_async_copy(
                grad_buf.at[of_step % 2, k],
                dproj_ref.at[bb, rows, pl.ds(pl.multiple_of(slabs[k] * LANE + hh * width, width), width)],
                grad_sem.at[of_step % 2, k]) for k in range(4)]

        @pl.when(step >= 2)
        def _():
            for cp in grad_copies(step - 2):
                cp.wait()

        @pl.when(t == 0)
        def _():
            ds_scr[...] = jnp.zeros_like(ds_scr)

        lower, upper = lower_ref[...], upper_ref[...]
        dlb_parts, dnw_parts = [None] * HG_HPS, [None] * HG_HPS

        def head(hp):
            ls = slice(hp * LANE, (hp + 1) * LANE)
            hq, v, hg = p_ref[HG_Q, :, ls], p_ref[HG_I, :, ls], p_ref[HG_G, :, ls]
            nw = nw_ref[...]
            c = _hgrn_common(tbl_ref[:, ls], p_ref[HG_F, :, ls], hq)
            qd, kd, ku = c["qd"], c["kd"], c["ku"]
            yield
            y, on, rstd = _rms_fwd(o_ref[:, ls], nw)
            sg = _sigmoid(hg)
            dr = dr_ref[:, ls]
            grad_buf[slot, dg_k, :, ls] = (dr * y * (sg * (1.0 + hg * (1.0 - sg)))).astype(BF16)
            do, dnw_rows = _rms_bwd(dr * (hg * sg), on, rstd, nw)
            yield
            at = _dot(kd, qd, NT_DIMS) * upper
            da = _dot(do, v, NT_DIMS) * lower
            dat = _dot(v, do, NT_DIMS) * upper
            yield
            dv = _dot(at, do)
            dqd = _dot(da, kd)
            dkd = _dot(dat, qd)
            yield
            do_c, qd_c, v_c, ku_c = [_chunk_slices(z.astype(BF16)) for z in (do, qd, v, ku)]
            outer = [_dot(do_c[j], qd_c[j], TN_DIMS) for j in range(HG_NCH)]
            yield
            ds = ds_scr[hp]
            ds_after = [None] * HG_NCH
            for j in reversed(range(HG_NCH)):
                ds_after[j] = ds
                ds = outer[j] + ds * c["e_last"][j]
            ds_scr[hp] = ds
            yield
            updates = [_dot(v_c[j], ku_c[j], TN_DIMS) for j in range(HG_NCH)]
            yield
            states = [st_ref[hp, 0]]
            for j in range(HG_NCH - 1):
                states.append(states[j] * c["e_last"][j] + updates[j])
            dv = dv + jnp.concatenate([_dot(ku_c[j], ds_after[j], NT_DIMS) for j in range(HG_NCH)], axis=0)
            dqd = dqd + jnp.concatenate([_dot(do_c[j], states[j]) for j in range(HG_NCH)], axis=0)
            dku = jnp.concatenate([_dot(v_c[j], ds_after[j]) for j in range(HG_NCH)], axis=0)
            yield
            dku_ku = dku * ku
            dbl = [_colsum(states[j] * ds_after[j]) * c["e_last"][j] + _colsum(dku_ku[j * HG_CHUNK:(j + 1) * HG_CHUNK])
                   for j in range(HG_NCH)]
            dk = dkd * c["e_nb"] + dku * c["e_rem"]
            db = dqd * qd - dkd * kd - dku_ku + jnp.where(_row_in_chunk() == HG_CHUNK - 1, _chunk_rows(dbl), 0.0)
            dfv = _chunk_cumsum(db, reverse=True) / c["f"] - dk
            sig, sq = c["sig"], c["sq"]
            grad_buf[slot, df_k, :, ls] = (dfv * (1.0 - c["lb"]) * sig * (1.0 - sig)).astype(BF16)
            grad_buf[slot, dq_k, :, ls] = (dqd * c["e_b"] * (sq * (1.0 + hq * (1.0 - sq)))).astype(BF16)
            grad_buf[slot, di_k, :, ls] = dv.astype(BF16)
            dlb_parts[hp] = _colsum(dfv * (1.0 - sig))
            dnw_parts[hp] = _colsum(dnw_rows)

        _in_step(head(hp) for hp in range(HG_HPS))
        _acc_out(dlb_ref, jnp.logical_and(b == 0, t == 0), jnp.concatenate(dlb_parts, axis=1))
        _acc_out(dnw_ref, jnp.logical_and(h == 0, jnp.logical_and(b == 0, t == 0)), sum(dnw_parts[1:], dnw_parts[0]))
        for cp in grad_copies(step):
            cp.start()

        @pl.when(step == n_steps - 1)
        def _():
            for cp in grad_copies(step - 1) + grad_copies(step):
                cp.wait()

    rev = lambda t: nstep - 1 - t
    slab = lambda first: pl.BlockSpec((None, HG_TOK, width), lambda h, b, t: (b, rev(t), first // HG_HPS + h))
    head = pl.BlockSpec((None, HG_TOK, width), lambda h, b, t: (b, rev(t), h))
    return _call(
        body, (lb_table, norm_w, dcat, proj, o_raw, states, *masks), name="hgrn_bwd",
        grid=(HG_HEADS // HG_HPS, bsz, nstep),
        in_specs=[pl.BlockSpec((2, width), lambda h, b, t: (0, h)), pl.BlockSpec((1, LANE), lambda h, b, t: (0, 0)),
                  slab(rec0), pl.BlockSpec((None, HG_SLABS, HG_TOK, width), lambda h, b, t: (b, 0, rev(t), h)), head,
                  pl.BlockSpec((None, HG_HPS, 1, LANE, LANE), lambda h, b, t: (b, h, rev(t), 0, 0)),
                  pl.BlockSpec((HG_TOK, HG_TOK), lambda h, b, t: (0, 0)), pl.BlockSpec((HG_TOK, HG_TOK), lambda h, b, t: (0, 0))],
        out_specs=[pl.BlockSpec(memory_space=pl.ANY), pl.BlockSpec((1, width), lambda h, b, t: (0, h)),
                   pl.BlockSpec((1, LANE), lambda h, b, t: (0, 0))],
        out_shape=[_sds((bsz, seq, IN_COLS), BF16), _sds((1, HG_WIDTH), F32), _sds((1, LANE), F32)],
        scratch_shapes=[pltpu.VMEM((HG_HPS, LANE, LANE), F32), pltpu.VMEM((2, 4, HG_TOK, width), BF16),
                        pltpu.SemaphoreType.DMA((2, 4))],
        sem=("arbitrary", "arbitrary", "arbitrary"), comms=comms)


def _attn_bwd(dcat, raw, w_norm, qh, kh, vh, lse, sinks, tables, biases, dproj, comms=()):
    bsz, nblk = qh.shape[0], qh.shape[1]
    seq = nblk * WINDOW
    nstep = nblk // ATT_BPS
    half = ROPE_DIM // 2

    def body(sink_ref, da_ref, raw_ref, w_ref, q_ref, kc_ref, kp_ref, vc_ref, vp_ref, l_ref, c_ref, u_ref, d_ref,
             full_ref, first_ref, dproj_ref, o_ref, dw_ref, dsink_ref, carry_k, carry_v):
        b, i = pl.program_id(0), pl.program_id(1)
        first = jnp.logical_and(b == 0, i == 0)

        @pl.when(i == 0)
        def _():
            carry_k[...] = jnp.zeros_like(carry_k)
            carry_v[...] = jnp.zeros_like(carry_v)

        w = w_ref[...]
        _, on, rstd = _rms_fwd(raw_ref[...], w)
        do_step, dw_rows = _rms_bwd(da_ref[...], on, rstd, w)
        _acc_out(dw_ref, first, _colsum(dw_rows))
        lane8 = lax.broadcasted_iota(jnp.int32, (1, ATT_Q_HEADS), 1)
        dsink = jnp.zeros((1, ATT_Q_HEADS), F32)
        head_cols = jnp.where(lax.broadcasted_iota(jnp.int32, (2 * ATT_Q_HEADS, ATT_WIDTH), 1) // ATT_HEAD_DIM
                              == lax.broadcasted_iota(jnp.int32, (2 * ATT_Q_HEADS, ATT_WIDTH), 0), 1.0, 0.0)
        from_next_k, from_next_v = carry_k[...], carry_v[...]
        for blk in reversed(range(ATT_BPS)):
            tok = slice(blk * WINDOW, (blk + 1) * WINDOW)
            bias = _band_bias(full_ref, first_ref, True if blk else i < nstep - 1)
            do_all = do_step[tok]
            c, u, d = c_ref[tok, :], u_ref[tok, :], d_ref[tok, :]
            lse_t = l_ref[tok, :].T
            prod = do_all * raw_ref[tok, :]
            prod_hi = prod.astype(BF16)
            prod_lo = prod - prod_hi.astype(F32)
            dsum_t = _dot(head_cols, prod_hi, NT_DIMS) + _dot(head_cols, prod_lo, NT_DIMS)

            def unrope(g):
                return (g * c + pltpu.roll(g * u, LANE - half, 1) + pltpu.roll(g * d, half, 1)).astype(BF16)

            groups = range(ATT_KV_HEADS)
            group_row = lambda z, g: jnp.concatenate(
                [z[g * ATT_GROUP + hh:g * ATT_GROUP + hh + 1, :] for hh in range(ATT_GROUP)], axis=1)
            q = [q_ref[blk, g] for g in groups]
            keys, vals = [_band(kp_ref, kc_ref, g, blk) for g in groups], [_band(vp_ref, vc_ref, g, blk) for g in groups]
            do_g = [jnp.concatenate([do_all[:, (g * ATT_GROUP + hh) * ATT_HEAD_DIM:(g * ATT_GROUP + hh + 1) * ATT_HEAD_DIM]
                                     for hh in range(ATT_GROUP)], axis=0) for g in groups]
            dsum, lse_g = [group_row(dsum_t, g) for g in groups], [group_row(lse_t, g) for g in groups]
            s_t = [_dot(keys[g], q[g], NT_DIMS) for g in groups]
            dp_t = [_dot(vals[g], do_g[g], NT_DIMS) for g in groups]
            p_t = [jnp.exp(s_t[g] * ATT_SCALE + bias - lse_g[g]) for g in groups]
            ds_t = [p_t[g] * (dp_t[g] - dsum[g]) * ATT_SCALE for g in groups]
            dq_g = [_dot(ds_t[g], keys[g], TN_DIMS) for g in groups]
            dk_g = [_dot(ds_t[g], q[g]) for g in groups]
            dv_g = [_dot(p_t[g], do_g[g]) for g in groups]
            for g in groups:
                sink_part = jnp.exp(_sink_row(sink_ref, g) - lse_g[g]) * dsum[g]
                for hh in range(ATT_GROUP):
                    head_sum = jnp.sum(sink_part[:, hh * WINDOW:(hh + 1) * WINDOW], axis=1, keepdims=True)
                    dsink = dsink - jnp.where(lane8 == g * ATT_GROUP + hh, head_sum, 0.0)
            dq_parts = [dq_g[g][hh * WINDOW:(hh + 1) * WINDOW] for g in groups for hh in range(ATT_GROUP)]
            dk_before, dk_own = [z[:WINDOW] for z in dk_g], [z[WINDOW:] for z in dk_g]
            dv_before, dv_own = [z[:WINDOW] for z in dv_g], [z[WINDOW:] for z in dv_g]
            per_slab = LANE // ATT_HEAD_DIM
            for s in range(ATT_WIDTH // LANE):
                slab = jnp.concatenate(dq_parts[s * per_slab:(s + 1) * per_slab], axis=1)
                o_ref[tok, s * LANE:(s + 1) * LANE] = unrope(slab)
            o_ref[tok, ATT_WIDTH:ATT_WIDTH + LANE] = unrope(jnp.concatenate(dk_own, axis=1) + from_next_k)
            o_ref[tok, ATT_WIDTH + LANE:ATT_COLS] = (jnp.concatenate(dv_own, axis=1) + from_next_v).astype(BF16)
            from_next_k, from_next_v = jnp.concatenate(dk_before, axis=1), jnp.concatenate(dv_before, axis=1)
        carry_k[...] = from_next_k
        carry_v[...] = from_next_v
        _acc_out(dsink_ref, first, dsink)

    rows = ATT_BPS * WINDOW
    rev = lambda i: nstep - 1 - i
    cur = lambda width: pl.BlockSpec((None, rows, width), lambda b, i: (b, rev(i), 0))
    q_spec = pl.BlockSpec((None, ATT_BPS, ATT_KV_HEADS, GROUP_ROWS, ATT_HEAD_DIM), lambda b, i: (b, rev(i), 0, 0, 0))
    kv_cur = pl.BlockSpec((None, ATT_KV_HEADS, rows, ATT_HEAD_DIM), lambda b, i: (b, 0, rev(i), 0))
    kv_prev = pl.BlockSpec((None, ATT_KV_HEADS, WINDOW, ATT_HEAD_DIM), lambda b, i: (b, 0, jnp.maximum(ATT_BPS * rev(i) - 1, 0), 0))
    tab = pl.BlockSpec((rows, LANE), lambda b, i: (rev(i), 0))
    return _call(
        body, (sinks, dcat, raw, w_norm, qh, kh, kh, vh, vh, lse, *tables, *biases, dproj), name="attn_bwd", grid=(bsz, nstep),
        in_specs=[pl.BlockSpec(memory_space=pltpu.SMEM), cur(ATT_WIDTH), cur(ATT_WIDTH), _vec_spec(ATT_WIDTH), q_spec,
                  kv_cur, kv_prev, kv_cur, kv_prev, cur(LANE), tab, tab, tab, _bias_spec(True), _bias_spec(True),
                  pl.BlockSpec(memory_space=pl.ANY)],
        out_specs=[cur(ATT_COLS), _vec_spec(ATT_WIDTH), _vec_spec(ATT_Q_HEADS)],
        out_shape=[_sds(dproj.shape, BF16), _sds((1, ATT_WIDTH), F32), _sds((1, ATT_Q_HEADS), F32)],
        scratch_shapes=[pltpu.VMEM((WINDOW, LANE), F32), pltpu.VMEM((WINDOW, LANE), F32)],
        sem=("arbitrary", "arbitrary"), comms=comms, aliases={15: 0})


def _other_chips(x, y):
    return [(1 - x, y), (x, 1 - y), (1 - x, 1 - y)]


def _sem_pair(n):
    return [pltpu.SemaphoreType.DMA((n,)), pltpu.SemaphoreType.DMA((n,))]


def _plan_pair_forward(bufs):
    n = len(bufs)

    def copies(outs, sems):
        x, y, c = _mesh_pos()
        sends, lands = [], []
        for a in range(n):
            for j, chip in enumerate(_other_chips(x, y)):
                k = 3 * a + j
                slot = outs[a].at[4 * chip[0] + 2 * chip[1] + c]
                sends.append(pltpu.make_async_remote_copy(
                    src_ref=slot, dst_ref=slot, send_sem=sems[0].at[k], recv_sem=sems[1].at[k],
                    device_id=(x, y, 1 - c), device_id_type=MESH))
                theirs = outs[a].at[4 * chip[0] + 2 * chip[1] + 1 - c]
                lands.append(pltpu.make_async_remote_copy(
                    src_ref=theirs, dst_ref=theirs, send_sem=sems[0].at[k], recv_sem=sems[1].at[k],
                    device_id=(x, y, 1 - c), device_id_type=MESH))
        return sends, lands

    def start(ins, outs, sems):
        for cp in copies(outs, sems)[0]:
            cp.start()

    def finish(ins, outs, sems):
        sends, lands = copies(outs, sems)
        for cp in lands:
            cp.wait_recv()
        for cp in sends:
            cp.wait_send()

    return _Comm(list(bufs), [_sds(b.shape, b.dtype) for b in bufs], _sem_pair(3 * n), start, finish,
                 aliases=[(a, a) for a in range(n)])


def _plan_pair(arrays, other_half):
    n = len(arrays)
    per = N_CHIPS if other_half == "chip_major" else 1

    def copies(ins, outs, sems):
        x, y, c = _mesh_pos()
        out = []
        for a in range(n):
            for k in range(per):
                if other_half == "chip_major":
                    src, dst = ins[a].at[k, 1 - c], outs[a].at[k]
                else:
                    src, dst = (ins[a].at[1 - c] if other_half else ins[a]), outs[a]
                out.append(pltpu.make_async_remote_copy(
                    src_ref=src, dst_ref=dst, send_sem=sems[0].at[per * a + k], recv_sem=sems[1].at[per * a + k],
                    device_id=(x, y, 1 - c), device_id_type=MESH))
        return out

    def start(ins, outs, sems):
        for cp in copies(ins, outs, sems):
            cp.start()

    def finish(ins, outs, sems):
        for cp in copies(ins, outs, sems):
            cp.wait()

    if other_half == "chip_major":
        shapes = [_sds((a.shape[0],) + a.shape[2:], a.dtype) for a in arrays]
    else:
        shapes = [_sds(a.shape[1:] if other_half else a.shape, a.dtype) for a in arrays]
    return _Comm(list(arrays), shapes, _sem_pair(per * n), start, finish)


def _plan_chip_exchange(arrays):
    n = len(arrays)

    def copies(ins, outs, sems):
        x, y, c = _mesh_pos()
        sends, lands = [], []
        for a in range(n):
            for j, chip in enumerate(_other_chips(x, y)):
                k = 3 * a + j
                sends.append(pltpu.make_async_remote_copy(
                    src_ref=ins[a].at[2 * chip[0] + chip[1]], dst_ref=outs[a].at[2 * x + y], send_sem=sems[0].at[k],
                    recv_sem=sems[1].at[k], device_id=(*chip, c), device_id_type=MESH))
                slot = outs[a].at[2 * chip[0] + chip[1]]
                lands.append(pltpu.make_async_remote_copy(
                    src_ref=slot, dst_ref=slot, send_sem=sems[0].at[k], recv_sem=sems[1].at[k],
                    device_id=(*chip, c), device_id_type=MESH))
        return sends, lands

    def start(ins, outs, sems):
        for cp in copies(ins, outs, sems)[0]:
            cp.start()

    def finish(ins, outs, sems):
        sends, lands = copies(ins, outs, sems)
        for cp in lands:
            cp.wait_recv()
        for cp in sends:
            cp.wait_send()

    return _Comm(list(arrays), [_sds(a.shape, a.dtype) for a in arrays], _sem_pair(3 * n), start, finish)


SEM_SPEC = pl.BlockSpec(memory_space=pltpu.SEMAPHORE)
N_OTHER = N_CHIPS - 1


def _exchange_copies(s_ref, land_ref, sems):
    x, y, c = _mesh_pos()
    return [pltpu.make_async_remote_copy(
        src_ref=s_ref.at[2 * chip[0] + chip[1]], dst_ref=land_ref.at[2 * x + y], send_sem=sems[j], recv_sem=sems[N_OTHER + j],
        device_id=(*chip, c), device_id_type=MESH) for j, chip in enumerate(_other_chips(x, y))]


def _exchange_start(s, name):
    def body(s_ref, land_ref, *outs):
        sems, token = outs[:2 * N_OTHER], outs[-1]
        for cp in _exchange_copies(s_ref, land_ref, sems):
            cp.start()
        token[...] = jnp.zeros_like(token)

    hbm = pltpu.HBM(s.shape, s.dtype)
    res = pl.pallas_call(
        body, name=name,
        out_shape=(pltpu.SemaphoreType.DMA(()),) * (2 * N_OTHER) + (hbm, hbm, _sds((SUBLANES, LANE), F32)),
        in_specs=(HBM_SPEC, HBM_SPEC),
        out_specs=(SEM_SPEC,) * (2 * N_OTHER) + (HBM_SPEC, HBM_SPEC, pl.BlockSpec(memory_space=pltpu.VMEM)),
        input_output_aliases={0: 2 * N_OTHER, 1: 2 * N_OTHER + 1},
        compiler_params=pltpu.CompilerParams(has_side_effects=pltpu.SideEffectType.DATAFLOW_SIDE_EFFECTING),
    )(pltpu.with_memory_space_constraint(s, pltpu.HBM), pltpu.with_memory_space_constraint(lax.empty(s.shape, s.dtype), pltpu.HBM))
    return res[:2 * N_OTHER], res[2 * N_OTHER], res[2 * N_OTHER + 1], res[-1]


def _exchange_wait(sems, s_thru, land_thru, afters, name):
    def body(s_ref, land_ref, *rest):
        for cp in _exchange_copies(s_ref, land_ref, rest[:2 * N_OTHER]):
            cp.wait_send()
            cp.wait_recv()

    hbm = pltpu.HBM(s_thru.shape, s_thru.dtype)
    return pl.pallas_call(
        body, name=name, out_shape=(hbm, hbm),
        in_specs=(HBM_SPEC, HBM_SPEC) + (SEM_SPEC,) * (2 * N_OTHER) + (pl.BlockSpec(memory_space=pl.ANY),) * len(afters),
        out_specs=(HBM_SPEC, HBM_SPEC), input_output_aliases={0: 0, 1: 1},
        compiler_params=pltpu.CompilerParams(has_side_effects=pltpu.SideEffectType.DATAFLOW_SIDE_EFFECTING),
    )(s_thru, land_thru, *sems, *afters)


def _gather_copies(block_ref, buf_ref, sems):
    x, y, c = _mesh_pos()
    return [pltpu.make_async_remote_copy(
        src_ref=block_ref, dst_ref=buf_ref.at[4 * x + 2 * y + c], send_sem=sems[j], recv_sem=sems[N_OTHER + j],
        device_id=(*chip, c), device_id_type=MESH) for j, chip in enumerate(_other_chips(x, y))]


def _gather_start(blocks, bufs, afters, name):
    n = len(blocks)
    per = 2 * N_OTHER

    def body(*refs):
        ins, outs = refs[:2 * n], refs[2 * n + len(afters):]
        for a in range(n):
            for cp in _gather_copies(ins[a], ins[n + a], outs[a * per:(a + 1) * per]):
                cp.start()
        outs[-1][...] = jnp.zeros_like(outs[-1])

    hbm = [pltpu.HBM(z.shape, z.dtype) for z in list(blocks) + list(bufs)]
    res = pl.pallas_call(
        body, name=name,
        out_shape=(pltpu.SemaphoreType.DMA(()),) * (n * per) + tuple(hbm) + (_sds((SUBLANES, LANE), F32),),
        in_specs=(HBM_SPEC,) * (2 * n) + (pl.BlockSpec(memory_space=pl.ANY),) * len(afters),
        out_specs=(SEM_SPEC,) * (n * per) + (HBM_SPEC,) * (2 * n) + (pl.BlockSpec(memory_space=pltpu.VMEM),),
        input_output_aliases={k: n * per + k for k in range(2 * n)},
        compiler_params=pltpu.CompilerParams(has_side_effects=pltpu.SideEffectType.DATAFLOW_SIDE_EFFECTING),
    )(*[pltpu.with_memory_space_constraint(z, pltpu.HBM) for z in list(blocks) + list(bufs)], *afters)
    parts = [(res[a * per:(a + 1) * per], res[n * per + a], res[n * per + n + a]) for a in range(n)]
    return parts, res[-1]


def _gather_wait(part, afters, name):
    sems, block, buf = part

    def body(block_ref, buf_ref, *rest):
        for cp in _gather_copies(block_ref, buf_ref, rest[:2 * N_OTHER]):
            cp.wait_send()
            cp.wait_recv()

    return pl.pallas_call(
        body, name=name, out_shape=(pltpu.HBM(block.shape, block.dtype), pltpu.HBM(buf.shape, buf.dtype)),
        in_specs=(HBM_SPEC, HBM_SPEC) + (SEM_SPEC,) * (2 * N_OTHER) + (pl.BlockSpec(memory_space=pl.ANY),) * len(afters),
        out_specs=(HBM_SPEC, HBM_SPEC), input_output_aliases={0: 0, 1: 1},
        compiler_params=pltpu.CompilerParams(has_side_effects=pltpu.SideEffectType.DATAFLOW_SIDE_EFFECTING),
    )(block, buf, *sems, *afters)[1]


def _comm_only(comms, name):
    return _call(lambda: None, (), name=name, grid=(), in_specs=[], out_specs=[], out_shape=[], sem=(), comms=comms)[1]


def _allgather8(arrays, name):
    return _comm_only([_plan_allgather8(arrays)], name)[0]


def _plan_allgather8(arrays):
    n = len(arrays)

    def parts(ins, outs, sems):
        send_sems, recv_sems, local_sems = sems
        x, y, c = _mesh_pos()
        me, sibling = (x, y, c), (x, y, 1 - c)
        chips = _other_chips(x, y)

        def copy(a, k, block, to, src=None):
            dst = outs[a].at[4 * block[0] + 2 * block[1] + block[2]]
            return pltpu.make_async_remote_copy(
                src_ref=dst if src is None else src, dst_ref=dst, send_sem=send_sems.at[7 * a + k],
                recv_sem=recv_sems.at[7 * a + k], device_id=to, device_id_type=MESH)

        mine = [pltpu.make_async_copy(ins[a], outs[a].at[4 * x + 2 * y + c], local_sems.at[a]) for a in range(n)]
        first = []
        for a in range(n):
            first.append(copy(a, 0, me, sibling, src=ins[a]))
            first += [copy(a, 1 + j, me, (*chip, c), src=ins[a]) for j, chip in enumerate(chips)]
        return copy, mine, first, me, sibling, chips, c

    def start(ins, outs, sems):
        _, mine, first, *_ = parts(ins, outs, sems)
        for cp in mine + first:
            cp.start()

    def finish(ins, outs, sems):
        copy, mine, first, me, sibling, chips, c = parts(ins, outs, sems)
        passed = []
        for j, chip in enumerate(chips):
            for a in range(n):
                copy(a, 1 + j, (*chip, c), me).wait_recv()
                fwd = copy(a, 4 + j, (*chip, c), sibling)
                fwd.start()
                passed.append(fwd)
        for a in range(n):
            copy(a, 0, sibling, me).wait_recv()
            for j, chip in enumerate(chips):
                copy(a, 4 + j, (*chip, 1 - c), me).wait_recv()
        for cp in first + passed:
            cp.wait_send()
        for cp in mine:
            cp.wait()

    sems = [pltpu.SemaphoreType.DMA((7 * n,)), pltpu.SemaphoreType.DMA((7 * n,)), pltpu.SemaphoreType.DMA((n,))]
    return _Comm(list(arrays), [_sds((N_DEV,) + a.shape, a.dtype) for a in arrays], sems, start, finish)


def _pair_sum(g, q, core, name, chip_major=False):
    rows, cols = g.shape[2:]
    tr = _row_tile(rows)

    def body(core_ref, g_ref, q_ref, o_ref):
        o_ref[...] = (g_ref[...] + q_ref[...]).astype(BF16)

    blk = pl.BlockSpec((None, tr, cols), lambda k, i, core_ref: (k, i, 0))
    if chip_major:
        own = pl.BlockSpec((None, None, tr, cols), lambda k, i, core_ref: (k, core_ref[0], i, 0))
    else:
        own = pl.BlockSpec((None, None, tr, cols), lambda k, i, core_ref: (core_ref[0], k, i, 0))
    return pl.pallas_call(
        body, name=name,
        grid_spec=pltpu.PrefetchScalarGridSpec(num_scalar_prefetch=1, grid=(N_CHIPS, rows // tr), in_specs=[own, blk], out_specs=blk),
        out_shape=_sds((N_CHIPS, rows, cols), BF16), compiler_params=_params("parallel", "parallel"),
    )(core, g, q)


def _sum_chips(own, landed, chip, name):
    _, rows, cols = own.shape
    tr = _row_tile(rows)

    def body(chip_ref, own_ref, a_ref, b_ref, c_ref, o_ref):
        acc = own_ref[...].astype(F32) + a_ref[...].astype(F32)
        o_ref[...] = (acc + b_ref[...].astype(F32)) + c_ref[...].astype(F32)

    blk = lambda flip: pl.BlockSpec((None, tr, cols), lambda i, chip_ref: (jnp.bitwise_xor(chip_ref[0], flip), i, 0))
    return pl.pallas_call(
        body, name=name,
        grid_spec=pltpu.PrefetchScalarGridSpec(num_scalar_prefetch=1, grid=(rows // tr,), in_specs=[blk(0), blk(1), blk(2), blk(3)],
                                               out_specs=pl.BlockSpec((tr, cols), lambda i, chip_ref: (i, 0))),
        out_shape=_sds((rows, cols), F32), compiler_params=_params("parallel"),
    )(chip, own, landed, landed, landed)


SUBLANES = 8


def _tile_rows(n_elems):
    return -(-n_elems // (SUBLANES * LANE)) * SUBLANES


SMALL_ITEMS = (("b_ada", N_MOD * D_MODEL), ("pre_w_mix", D_MODEL), ("post_w_mix", D_MODEL), ("pre_w_mlp", D_MODEL),
               ("post_w_mlp", D_MODEL), ("attn_out_w", ATT_WIDTH), ("hg_norm_w", HG_HEAD_DIM), ("attn_sinks", ATT_Q_HEADS),
               ("lb_0", HG_WIDTH), ("lb_1", HG_WIDTH))
SMALL_AT = {}
for _name, _size in SMALL_ITEMS:
    SMALL_AT[_name] = (sum(r for _, r in SMALL_AT.values()), _tile_rows(_size))
SMALL_ROWS = sum(r for _, r in SMALL_AT.values())
MOD_ROWS = SMALL_AT["b_ada"][1]
PLAIN_ROWS = SMALL_AT["lb_0"][0] - MOD_ROWS
LB_ROWS = SMALL_AT["lb_0"][1]


def _rows(a, nrows=None):
    flat = a.reshape(-1)
    nrows = _tile_rows(flat.shape[0]) if nrows is None else nrows
    return jnp.pad(flat, (0, nrows * LANE - flat.shape[0])).reshape(nrows, LANE)


def _pack_small(vals):
    vals = dict(vals, lb_0=vals["lb_table"][0], lb_1=vals["lb_table"][1])
    return jnp.concatenate([_rows(vals[name], SMALL_AT[name][1]) for name, _ in SMALL_ITEMS], axis=0)


def _unpack_small(p):
    def item(name, shape):
        first = SMALL_AT[name][0]
        size = shape[0] * shape[1]
        return p[first:first + SMALL_AT[name][1]].reshape(-1)[:size].reshape(shape)

    out = {name: item(name, (1, size)) for name, size in SMALL_ITEMS if not name.startswith("lb_")}
    out["lb_table"] = jnp.concatenate([item("lb_0", (1, HG_WIDTH)), item("lb_1", (1, HG_WIDTH))], axis=0)
    return out


def _pack_partials(dmod, plain, d_lb, loss_row):
    return jnp.concatenate([_rows(dmod, dmod.shape[0] * MOD_ROWS)] + [_rows(g) for g in plain] + [_rows(d_lb), _rows(loss_row)], axis=0)


def _small_update(packs, w, m, v, n_seq):
    mod_end = n_seq * MOD_ROWS
    lb_at = mod_end + PLAIN_ROWS
    t0, t1 = SMALL_AT["lb_0"][0], SMALL_AT["lb_1"][0]

    def body(p_ref, w_ref, m_ref, v_ref, g_ref, dl_ref, nm_ref, nv_ref, loss_ref):
        tot = p_ref[0]
        for d in range(1, N_DEV):
            tot = tot + p_ref[d]
        wv = w_ref[...]
        p1 = _sigmoid(wv[t1:t1 + LB_ROWS] - wv[t0:t0 + LB_ROWS])
        s = tot[lb_at:lb_at + LB_ROWS] * p1 * (1.0 - p1)
        g_bias = tot[0:MOD_ROWS]
        for q in range(1, n_seq):
            g_bias = g_bias + tot[q * MOD_ROWS:(q + 1) * MOD_ROWS]
        g = jnp.concatenate([g_bias, tot[mod_end:lb_at], -s, s], axis=0)
        g_ref[...] = g
        dl_ref[...], nm_ref[...], nv_ref[...] = _adamw_math(g, wv, m_ref[...], v_ref[...])
        loss_ref[...] = tot[lb_at + LB_ROWS:lb_at + LB_ROWS + SUBLANES]

    shp = _sds((SMALL_ROWS, LANE), F32)
    return pl.pallas_call(body, name="small_update", out_shape=[shp] * 4 + [_sds((SUBLANES, LANE), F32)],
                          compiler_params=_params())(packs, w, m, v)


def kernel(x, c, w_ada, b_ada, pre_w_mix, w_in, attn_sinks, attn_out_w, lb_table, hg_norm_w, w_out, post_w_mix, pre_w_mlp, w_up, w_down, post_w_mlp, loss_target, m_w_ada, m_b_ada, m_pre_w_mix, m_w_in, m_attn_sinks, m_attn_out_w, m_lb_table, m_hg_norm_w, m_w_out, m_post_w_mix, m_pre_w_mlp, m_w_up, m_w_down, m_post_w_mlp, v_w_ada, v_b_ada, v_pre_w_mix, v_w_in, v_attn_sinks, v_attn_out_w, v_lb_table, v_hg_norm_w, v_w_out, v_post_w_mix, v_pre_w_mlp, v_w_up, v_w_down, v_post_w_mlp):
    xi, yi, ci = _mesh_pos()
    chip = 2 * xi + yi
    dev = 2 * chip + ci
    bsz, seq, _ = x.shape
    ntok = bsz * seq
    ada_cols = w_ada.shape[2]
    core = jnp.reshape(ci, (1,)).astype(jnp.int32)
    chip_idx = jnp.reshape(chip, (1,)).astype(jnp.int32)
    flat = lambda a: a.reshape(ntok, a.shape[-1])
    unflat = lambda a: a.reshape(bsz, seq, a.shape[-1])
    tables = _rope_tables(seq)
    biases, chunk_masks = _band_biases(), _block_masks()

    def row_half(w):
        rows = w.shape[1] // 2
        return lax.dynamic_slice_in_dim(w[0], ci * rows, rows, axis=0).astype(BF16)

    def gather_buffer(w):
        rows, cols = w.shape[1] // 2, w.shape[2]
        own = w[0].astype(BF16).reshape(2, rows, cols)
        return lax.dynamic_update_slice(lax.empty((N_DEV, rows, cols), BF16), own, (2 * chip, 0, 0))

    w_in_t, m_in_t, v_in_t = [jnp.transpose(a[0])[None] for a in (w_in, m_w_in, v_w_in)]
    c_g, in_g = _allgather8([c, row_half(w_in_t)], "gather_first")
    c_all = c_g.reshape(N_DEV * bsz, D_MODEL)
    w_in_full = in_g.reshape(IN_COLS, D_MODEL)

    b_cols = lax.dynamic_slice_in_dim(b_ada, chip * ada_cols, ada_cols, axis=1)
    mod_part = _ada_fwd(c_all, w_ada[0], b_cols)
    half_rows = mod_part.shape[0] // 2
    (mod_g,) = _allgather8([lax.dynamic_slice_in_dim(mod_part, ci * half_rows, half_rows, axis=0)], "gather_mod")
    mod_all = mod_g.reshape(N_CHIPS, 2, half_rows, ada_cols).transpose(1, 2, 0, 3).reshape(N_DEV * bsz, N_MOD * D_MODEL)
    mod = lax.dynamic_slice_in_dim(mod_all, dev * bsz, bsz, axis=0)
    sh1, sc1, g1, sh2, sc2, g2 = [mod[:, i * D_MODEL:(i + 1) * D_MODEL].reshape(bsz, 1, D_MODEL) for i in range(N_MOD)]

    weights = (w_out, w_up, w_down)
    (out_part, up_part, down_part), started = _gather_start(
        [row_half(w) for w in weights], [gather_buffer(w) for w in weights], [mod_g], "gather_weights_start")

    h1, proj, qh, kh, vh = _in_proj_fused(x, pre_w_mix, sc1 + started[0:1, 0:1], sh1, w_in_full, tables)
    out_g = _gather_wait(out_part, [proj], "gather_out_wait")
    (attn_raw, cat, lse), ((out_g,),) = _attn_fwd(qh, kh, vh, attn_sinks, attn_out_w, biases, comms=[_plan_pair_forward([out_g])])
    up_g = _gather_wait(up_part, [attn_raw], "gather_up_wait")
    (o_raw, cat, states), ((up_g,),) = _hgrn_fwd(proj, lb_table, hg_norm_w, cat, chunk_masks, comms=[_plan_pair_forward([up_g])])
    down_g = _gather_wait(down_part, [o_raw], "gather_down_wait")
    w_out_full = out_g.reshape(D_MODEL, D_MODEL)
    w_up4 = up_g.reshape(N_CHIPS, D_MODEL, D_MODEL)
    mix, x1, h2 = _out_proj_fused(cat, w_out_full, x, post_w_mix, g1, pre_w_mlp, sc2, sh2)
    big_tm = min(ntok, 2048)
    up_spec = pl.BlockSpec((None, D_MODEL, D_MODEL), lambda i, j: (j, 0, 0))
    r, ((down_g,),) = _mm(flat(h2), w_up4, name="up_proj", out_dtype=BF16, tm=big_tm, tn=D_MODEL, n_out=D_FF, b_spec=up_spec,
                          epi=lambda acc: jnp.maximum(acc, 0.0), comms=[_plan_pair_forward([down_g])])
    w_down_full = down_g.reshape(D_FF, D_MODEL)
    square = lambda t: t * t
    loss_row, dy, dd, dg2, d_post_mlp = _down_proj_fused(unflat(r), w_down_full, x1, post_w_mlp, g2, loss_target)

    dpre = _mm(flat(dd), w_down_full, name="down_bwd", out_dtype=BF16, trans_b=True, tm=big_tm, tn=D_MODEL, extra=(r,),
               epi=lambda acc, rt: acc * (2.0 * rt.astype(F32)))
    half_rows = D_MODEL // 2
    g_down = _mm_tn(r, flat(dd), name="down_wgrad", tk=half_rows, tn=D_MODEL, a_fn=square,
                    out_shape=_sds((2, N_CHIPS, half_rows, D_MODEL), F32),
                    out_spec=pl.BlockSpec((None, None, half_rows, D_MODEL), lambda i, j: (i % 2, i // 2, 0, 0)))
    (dx1, dmix, dsc2, dsh2, dg1, d_pre_mlp, d_post_mix), ((q_down,),) = _up_bwd_fused(
        unflat(dpre), w_up4, dy, x1, mix, pre_w_mlp, sc2, post_w_mix, g1, comms=[_plan_pair([g_down], True)])
    g_up = _mm_tn(flat(h2), dpre, name="up_wgrad", tk=D_MODEL, tn=half_rows,
                  out_shape=_sds((2, N_CHIPS, half_rows, D_MODEL), F32),
                  out_spec=pl.BlockSpec((2, None, half_rows, half_rows), lambda i, j: (0, j // 2, 0, j % 2)))
    s_down = _pair_sum(g_down, q_down, core, "pair_sum_down")

    dcat, ((q_up,),) = _mm(flat(dmix), w_out_full, name="out_bwd", out_dtype=F32, trans_b=True, comms=[_plan_pair([g_up], True)])
    dcat = unflat(dcat)
    s_up = _pair_sum(g_up, q_up, core, "pair_sum_up")
    out_rows = D_MODEL // N_CHIPS
    g_out = _mm_tn(flat(cat), flat(dmix), name="out_wgrad", tk=2 * out_rows, tn=half_rows,
                   out_shape=_sds((2, N_CHIPS, out_rows, half_rows), F32),
                   out_spec=pl.BlockSpec((None, 2, out_rows, half_rows), lambda i, j: (j, i, 0, 0)))
    (dproj_rec, d_lb, d_hg_norm), ((x_down,), (q_out,)) = _hgrn_bwd(
        dcat, proj, o_raw, states, lb_table, hg_norm_w, chunk_masks, comms=[_plan_chip_exchange([s_down]), _plan_pair([g_out], True)])
    half_down = _sum_chips(s_down, x_down, chip_idx, "sum_chips_down")
    s_out = _pair_sum(g_out, q_out, core, "pair_sum_out")
    (dproj, d_attn_out, d_sinks), ((their_down,), (x_up,)) = _attn_bwd(
        dcat, attn_raw, attn_out_w, qh, kh, vh, lse, attn_sinks, tables, [bias.T for bias in biases], dproj_rec,
        comms=[_plan_pair([half_down], False), _plan_chip_exchange([s_up])])
    half_up = _sum_chips(s_up, x_up, chip_idx, "sum_chips_up")
    dproj = flat(dproj)
    in_rows = IN_COLS // N_CHIPS // 2
    g_in, ((x_out,),) = _mm_tn(dproj, flat(h1), name="in_wgrad", tk=2 * LANE, tn=D_MODEL, comms=[_plan_chip_exchange([s_out])])
    g_in = g_in.reshape(N_CHIPS, 2, in_rows, D_MODEL)
    half_out = _sum_chips(s_out, x_out, chip_idx, "sum_chips_out")
    dh1, ((q_in,), (their_up, their_out)) = _mm(
        dproj, w_in_full, name="in_bwd", out_dtype=F32,
        comms=[_plan_pair([g_in], "chip_major"), _plan_pair([half_up, half_out], False)])
    s_in = _pair_sum(g_in, q_in, core, "pair_sum_in", chip_major=True)
    in_sems, s_in, in_landing, started = _exchange_start(s_in, "exchange_in_start")
    grad_x, dsc1, dsh1, d_pre_mix = _norm1_bwd(unflat(dh1), dx1, x, pre_w_mix + started[0:1, 0:1], sc1)

    dmod = jnp.concatenate([dsh1, dsc1, dg1, dsh2, dsc2, dg2], axis=-1).reshape(bsz, N_MOD * D_MODEL)
    pack = _pack_partials(dmod, [d_pre_mix, d_post_mix, d_pre_mlp, d_post_mlp, d_attn_out, d_hg_norm, d_sinks], d_lb, loss_row)
    ((packs,),) = _comm_only([_plan_allgather8([pack])], "gather_small")
    w_small = dict(b_ada=b_ada, pre_w_mix=pre_w_mix, post_w_mix=post_w_mix, pre_w_mlp=pre_w_mlp, post_w_mlp=post_w_mlp,
                   attn_out_w=attn_out_w, hg_norm_w=hg_norm_w, attn_sinks=attn_sinks, lb_table=lb_table)
    m_small = dict(b_ada=m_b_ada, pre_w_mix=m_pre_w_mix, post_w_mix=m_post_w_mix, pre_w_mlp=m_pre_w_mlp, post_w_mlp=m_post_w_mlp,
                   attn_out_w=m_attn_out_w, hg_norm_w=m_hg_norm_w, attn_sinks=m_attn_sinks, lb_table=m_lb_table)
    v_small = dict(b_ada=v_b_ada, pre_w_mix=v_pre_w_mix, post_w_mix=v_post_w_mix, pre_w_mlp=v_pre_w_mlp, post_w_mlp=v_post_w_mlp,
                   attn_out_w=v_attn_out_w, hg_norm_w=v_hg_norm_w, attn_sinks=v_attn_sinks, lb_table=v_lb_table)
    *small_packed, loss_rows = _small_update(packs, _pack_small(w_small), _pack_small(m_small), _pack_small(v_small), bsz)
    small_out = [_unpack_small(p) for p in small_packed]
    loss = loss_rows[0, 0]

    dmod_all = packs[:, :bsz * MOD_ROWS, :].reshape(N_DEV * bsz, N_MOD * D_MODEL)
    dmod_cols = lax.dynamic_slice_in_dim(dmod_all, chip * ada_cols, ada_cols, axis=1)
    ada_out = _ada_bwd_adamw(c_all, dmod_cols, w_ada[0], m_w_ada[0], v_w_ada[0])

    s_in, x_in = _exchange_wait(in_sems, s_in, in_landing, [grad_x, ada_out[0]], "exchange_in_wait")
    half_in = _sum_chips(s_in, x_in, chip_idx, "sum_chips_in")
    ((their_in,),) = _comm_only([_plan_pair([half_in], False)], "pair_swap_in")
    big = dict(
        w_in=tuple(jnp.transpose(a) for a in _adamw_halves(half_in, their_in, core, w_in_t[0], m_in_t[0], v_in_t[0], axis=0,
                                                           name="adamw_in")),
        w_up=tuple(_adamw_halves(half_up, their_up, core, w_up[0], m_w_up[0], v_w_up[0], axis=0, name="adamw_up")),
        w_out=tuple(_adamw_halves(half_out, their_out, core, w_out[0], m_w_out[0], v_w_out[0], axis=1, name="adamw_out")),
        w_down=tuple(_adamw_halves(half_down, their_down, core, w_down[0], m_w_down[0], v_w_down[0], axis=0, name="adamw_down")),
        w_ada=tuple(ada_out),
    )
    order = ("w_ada", "b_ada", "pre_w_mix", "w_in", "attn_sinks", "attn_out_w", "lb_table", "hg_norm_w", "w_out", "post_w_mix",
             "pre_w_mlp", "w_up", "w_down", "post_w_mlp")
    outs = [loss, grad_x]
    for kind in range(4):
        for nm in order:
            outs.append(big[nm][kind][None] if nm in big else small_out[kind][nm])
    return tuple(outs)
```

```python
import jax
import jax.numpy as jnp
from jax import lax
from jax.experimental import pallas as pl
from jax.experimental.pallas import tpu as pltpu

F32 = jnp.float32
BF16 = jnp.bfloat16

D_MODEL = 1024
ATT_WIDTH = 512
ATT_HEAD_DIM = 64
ATT_Q_HEADS = 8
ATT_KV_HEADS = 2
ATT_GROUP = ATT_Q_HEADS // ATT_KV_HEADS
ATT_KV_COLS = ATT_KV_HEADS * ATT_HEAD_DIM
WINDOW = 128
ROPE_DIM = 16
ROPE_THETA = 500000.0
HG_WIDTH = 512
MIX_WIDTH = ATT_WIDTH + HG_WIDTH
HG_HEAD_DIM = 128
HG_HEADS = 4
HG_CHUNK = 32
IN_COLS = ATT_WIDTH + 2 * ATT_KV_COLS + 4 * HG_WIDTH
ATT_COLS = ATT_WIDTH + 2 * ATT_KV_COLS
D_FF = 4 * D_MODEL
N_MOD = 6
EPS = 1e-6
ATT_SCALE = ATT_HEAD_DIM ** -0.5

ADAM_LR = 0.001
ADAM_B1 = 0.9
ADAM_B2 = 0.999
ADAM_EPS = 1e-08
ADAM_WD = 0.01
ADAM_STEP = 10

N_CHIPS = 4
N_DEV = 8
LANE = 128
VMEM_LIMIT = 48 * 1024 * 1024
VMEM_LIMIT_BIG = 58 * 1024 * 1024
MESH = pl.DeviceIdType.MESH

NT_DIMS = (((1,), (1,)), ((), ()))
TN_DIMS = (((0,), (0,)), ((), ()))


def _sds(shape, dtype):
    return jax.ShapeDtypeStruct(tuple(shape), dtype)


def _params(*sem, vmem_limit=None):
    return pltpu.CompilerParams(dimension_semantics=sem, vmem_limit_bytes=VMEM_LIMIT if vmem_limit is None else vmem_limit)


def _sigmoid(x):
    return 1.0 / (1.0 + jnp.exp(-x))


def _dot(a, b, dims=None):
    a, b = a.astype(BF16), b.astype(BF16)
    if dims is None:
        return jnp.dot(a, b, preferred_element_type=F32)
    return lax.dot_general(a, b, dims, preferred_element_type=F32)


def _rms_fwd(x, w):
    rstd = lax.rsqrt(jnp.mean(x * x, axis=-1, keepdims=True) + EPS)
    xh = x * rstd
    return xh * w, xh, rstd


def _rms_bwd(dy, xh, rstd, w):
    dxh = dy * w
    dx = rstd * (dxh - xh * jnp.mean(dxh * xh, axis=-1, keepdims=True))
    return dx, dy * xh


def _colsum(x):
    return jnp.sum(x, axis=0, keepdims=True)


def _rms_hat(x):
    rstd = lax.rsqrt(jnp.mean(x * x, axis=-1, keepdims=True) + EPS)
    return x * rstd, rstd


def _rms_bwd_gain(dy, gain, xh, rstd):
    dxh = dy * gain
    dx = rstd * (dxh - xh * jnp.mean(dxh * xh, axis=-1, keepdims=True))
    return dx, _colsum(dy * xh)


def _row_tile(rows, cap=256):
    return max(t for t in range(16, cap + 1, 16) if rows % t == 0)


HBM_SPEC = pl.BlockSpec(memory_space=pltpu.HBM)


def _mesh_pos():
    return lax.axis_index("x"), lax.axis_index("y"), lax.axis_index("c")


class _Comm:
    def __init__(self, ins, outs, sems, start, finish, aliases=()):
        self.ins, self.outs, self.sems = list(ins), list(outs), list(sems)
        self.start, self.finish, self.aliases = start, finish, tuple(aliases)


def _call(body, args, *, name, grid, in_specs, out_specs, out_shape, sem, scratch_shapes=(), comms=(), aliases=None,
          vmem_limit=None):
    scratch_shapes = list(scratch_shapes)
    if not comms:
        return pl.pallas_call(body, name=name, grid=grid, in_specs=in_specs, out_specs=out_specs, out_shape=out_shape,
                              input_output_aliases=dict(aliases or {}), scratch_shapes=scratch_shapes,
                              compiler_params=_params(*sem, vmem_limit=vmem_limit))(*args)
    single = not isinstance(out_shape, (list, tuple))
    out_specs_l = [out_specs] if single else list(out_specs)
    out_shape_l = [out_shape] if single else list(out_shape)
    n_in, n_out, n_scr = len(in_specs), len(out_shape_l), len(scratch_shapes)
    n_ci = [len(cm.ins) for cm in comms]
    n_co = [len(cm.outs) for cm in comms]
    n_cs = [len(cm.sems) for cm in comms]
    aliases = dict(aliases or {})
    for k, cm in enumerate(comms):
        for i, o in cm.aliases:
            aliases[n_in + sum(n_ci[:k]) + i] = n_out + sum(n_co[:k]) + o

    def fused(*refs):
        pos = [0]

        def take(n):
            part = refs[pos[0]:pos[0] + n]
            pos[0] += n
            return part

        ins = take(n_in)
        c_ins = [take(n) for n in n_ci]
        outs = take(n_out)
        c_outs = [take(n) for n in n_co]
        scr = take(n_scr)
        c_sems = [take(n) for n in n_cs]
        first, last = True, True
        for d, size in enumerate(grid):
            first = jnp.logical_and(first, pl.program_id(d) == 0)
            last = jnp.logical_and(last, pl.program_id(d) == size - 1)

        def run(which):
            for cm, ci, co, cs in zip(comms, c_ins, c_outs, c_sems):
                getattr(cm, which)(ci, co, cs)

        if grid:
            pl.when(first)(lambda: run("start"))
        else:
            run("start")
        body(*ins, *outs, *scr)
        if grid:
            pl.when(last)(lambda: run("finish"))
        else:
            run("finish")

    res = pl.pallas_call(
        fused, name=name, grid=grid, in_specs=list(in_specs) + [HBM_SPEC] * sum(n_ci),
        out_specs=out_specs_l + [HBM_SPEC] * sum(n_co), out_shape=out_shape_l + [s for cm in comms for s in cm.outs],
        input_output_aliases=aliases, scratch_shapes=scratch_shapes + [s for cm in comms for s in cm.sems],
        compiler_params=_params(*["arbitrary"] * len(grid), vmem_limit=vmem_limit),
    )(*args, *[a for cm in comms for a in cm.ins])
    main = res[:n_out]
    extra, at = [], n_out
    for n in n_co:
        extra.append(list(res[at:at + n]))
        at += n
    return (main[0] if single else list(main)), extra


def _mm(a, b, *, name, out_dtype, trans_b=False, tm=512, tn=None, extra=(), epi=None, b_spec=None, n_out=None, comms=()):
    m_total, k_total = a.shape
    if n_out is None:
        n_out = b.shape[0] if trans_b else b.shape[1]
    tn = n_out if tn is None else tn
    grid = (m_total // tm, n_out // tn)
    dims = NT_DIMS if trans_b else None

    def body(*refs):
        a_ref, b_ref = refs[0], refs[1]
        extra_refs = refs[2:2 + len(extra)]
        o_ref = refs[2 + len(extra)]
        acc = _dot(a_ref[...], b_ref[...], dims)
        if epi is not None:
            acc = epi(acc, *[r[...] for r in extra_refs])
        o_ref[...] = acc.astype(out_dtype)

    if b_spec is None:
        if trans_b:
            b_spec = pl.BlockSpec((tn, k_total), lambda i, j: (j, 0))
        else:
            b_spec = pl.BlockSpec((k_total, tn), lambda i, j: (0, j))
    in_specs = [pl.BlockSpec((tm, k_total), lambda i, j: (i, 0)), b_spec]
    in_specs += [pl.BlockSpec((tm, tn), lambda i, j: (i, j)) for _ in extra]
    return _call(
        body, (a, b, *extra), name=name, grid=grid, in_specs=in_specs,
        out_specs=pl.BlockSpec((tm, tn), lambda i, j: (i, j)),
        out_shape=_sds((m_total, n_out), out_dtype),
        sem=("parallel", "parallel"), comms=comms)


def _mm_tn(a, b, *, name, tk, tn, a_fn=None, out_shape=None, out_spec=None, comms=()):
    m_total, k_total = a.shape
    n_total = b.shape[1]
    grid = (k_total // tk, n_total // tn)

    def body(a_ref, b_ref, o_ref):
        av = a_ref[...]
        part = _dot(av if a_fn is None else a_fn(av), b_ref[...], TN_DIMS)
        o_ref[...] = part.reshape(o_ref.shape)

    if out_shape is None:
        out_shape = _sds((k_total, n_total), F32)
        out_spec = pl.BlockSpec((tk, tn), lambda i, j: (i, j))
    return _call(
        body, (a, b), name=name, grid=grid,
        in_specs=[pl.BlockSpec((m_total, tk), lambda i, j: (0, i)), pl.BlockSpec((m_total, tn), lambda i, j: (0, j))],
        out_specs=out_spec, out_shape=out_shape, sem=("parallel", "parallel"), comms=comms)


def _ada_fwd(c_all, w_shard, b_shard):
    nb, ncol = c_all.shape[0], w_shard.shape[1]
    tn = 512

    def body(c_ref, w_ref, b_ref, o_ref):
        c = c_ref[...]
        o_ref[...] = _dot(c * _sigmoid(c), w_ref[...]) + b_ref[...]

    return pl.pallas_call(
        body, name="ada_fwd", grid=(ncol // tn,),
        in_specs=[pl.BlockSpec((nb, D_MODEL), lambda j: (0, 0)), pl.BlockSpec((D_MODEL, tn), lambda j: (0, j)),
                  pl.BlockSpec((1, tn), lambda j: (0, j))],
        out_specs=pl.BlockSpec((nb, tn), lambda j: (0, j)), out_shape=_sds((nb, ncol), F32),
        compiler_params=_params("parallel"),
    )(c_all, w_shard, b_shard)


def _adamw_math(g, w, m, v):
    m = ADAM_B1 * m + (1.0 - ADAM_B1) * g
    v = ADAM_B2 * v + (1.0 - ADAM_B2) * (g * g)
    m_hat = m / (1.0 - ADAM_B1 ** ADAM_STEP)
    v_hat = v / (1.0 - ADAM_B2 ** ADAM_STEP)
    delta = -ADAM_LR * (m_hat / (jnp.sqrt(v_hat) + ADAM_EPS) + ADAM_WD * w)
    return delta, m, v


def _ada_bwd_adamw(c_all, dmod_cols, w, m, v):
    nb, ncol = dmod_cols.shape
    tn = 256

    def body(c_ref, d_ref, w_ref, m_ref, v_ref, g_ref, dl_ref, nm_ref, nv_ref):
        c = c_ref[...]
        g = _dot(c * _sigmoid(c), d_ref[...], TN_DIMS)
        g_ref[...] = g
        dl_ref[...], nm_ref[...], nv_ref[...] = _adamw_math(g, w_ref[...], m_ref[...], v_ref[...])

    col = pl.BlockSpec((D_MODEL, tn), lambda j: (0, j))
    shp = _sds((D_MODEL, ncol), F32)
    return pl.pallas_call(
        body, name="ada_bwd_adamw", grid=(ncol // tn,),
        in_specs=[pl.BlockSpec((nb, D_MODEL), lambda j: (0, 0)), pl.BlockSpec((nb, tn), lambda j: (0, j)), col, col, col],
        out_specs=[col, col, col, col], out_shape=[shp, shp, shp, shp],
        compiler_params=_params("parallel"),
    )(c_all, dmod_cols, w, m, v)


def _adamw_halves(own, theirs, core, w, m, v, *, axis, name):
    r2, c2 = own.shape
    tr = _row_tile(r2)
    nt = r2 // tr

    def body(core_ref, own_ref, their_ref, w_ref, m_ref, v_ref, g_ref, dl_ref, nm_ref, nv_ref):
        g = jnp.where(pl.program_id(0) == core_ref[0], own_ref[...], their_ref[...])
        g_ref[...] = g
        dl_ref[...], nm_ref[...], nv_ref[...] = _adamw_math(g, w_ref[...], m_ref[...], v_ref[...])

    if axis == 0:
        full = pl.BlockSpec((tr, c2), lambda h, i, core_ref: (h * nt + i, 0))
    else:
        full = pl.BlockSpec((tr, c2), lambda h, i, core_ref: (i, h))
    half = pl.BlockSpec((tr, c2), lambda h, i, core_ref: (i, 0))
    shp = _sds(w.shape, F32)
    return pl.pallas_call(
        body, name=name,
        grid_spec=pltpu.PrefetchScalarGridSpec(num_scalar_prefetch=1, grid=(2, nt), in_specs=[half, half, full, full, full],
                                               out_specs=[full] * 4),
        out_shape=[shp] * 4, compiler_params=_params("parallel", "parallel"),
    )(core, own, theirs, w, m, v)


def _tok_spec(tm, width=D_MODEL):
    return pl.BlockSpec((None, tm, width), lambda b, i: (b, i, 0))


def _row_spec(width=D_MODEL):
    return pl.BlockSpec((None, 1, width), lambda b, i: (b, 0, 0))


def _vec_spec(width=D_MODEL):
    return pl.BlockSpec((1, width), lambda b, i: (0, 0))


class _RowsOf:
    def __init__(self, ref, first, count):
        self.ref, self.rows = ref, slice(first, first + count)

    def __getitem__(self, idx):
        return self.ref[self.rows, :]

    def __setitem__(self, idx, value):
        self.ref[self.rows, :] = value


def _mm_rows(a, b, *, name, tm, extra, extra_specs, out_specs, out_shape, epi, pro=None, trans_b=False, b_chunks=1, comms=(),
             parts=1, zero_per_seq=(), zero_once=(), vmem_limit=None):
    bsz, seq, k_total = a.shape
    kc = k_total // b_chunks
    dims = NT_DIMS if trans_b else None
    rows = tm // parts

    def body(*refs):
        a_ref, b_ref = refs[0], refs[1]
        ex, outs = refs[2:2 + len(extra)], refs[2 + len(extra):]
        if zero_per_seq:
            @pl.when(pl.program_id(1) == 0)
            def _():
                for k in zero_per_seq:
                    outs[k][...] = jnp.zeros_like(outs[k])
        if zero_once:
            @pl.when(jnp.logical_and(pl.program_id(0) == 0, pl.program_id(1) == 0))
            def _():
                for k in zero_once:
                    outs[k][...] = jnp.zeros_like(outs[k])

        def part_of(ref, p):
            tiled = len(ref.shape) == 2 and ref.shape[0] == tm
            return _RowsOf(ref, p * rows, rows) if tiled and parts > 1 else ref

        accs = []
        for p in range(parts):
            a_p, ex_p, outs_p = part_of(a_ref, p), [part_of(r, p) for r in ex], [part_of(r, p) for r in outs]
            if b_chunks == 1:
                accs.append(_dot(a_p[...] if pro is None else pro(a_p, ex_p, outs_p), b_ref[...], dims))
            else:
                acc = _dot(a_p[...][:, 0:kc], b_ref[0], NT_DIMS)
                for k in range(1, b_chunks):
                    acc = acc + _dot(a_p[...][:, k * kc:(k + 1) * kc], b_ref[k], NT_DIMS)
                accs.append(acc)
        for p in range(parts):
            epi(accs[p], [part_of(r, p) for r in ex], [part_of(r, p) for r in outs])

    b_spec = pl.BlockSpec(b.shape, lambda bb, i: (0,) * b.ndim)
    return _call(
        body, (a, b, *extra), name=name, grid=(bsz, seq // tm), in_specs=[_tok_spec(tm, k_total), b_spec, *extra_specs],
        out_specs=out_specs, out_shape=out_shape, sem=("arbitrary", "arbitrary"), comms=comms, vmem_limit=vmem_limit)


def _in_proj_fused(x, w, sc, sh, w_in_t, tables, comms=()):
    tm = 512
    bsz, seq, _ = x.shape
    half = ROPE_DIM // 2
    heads_per_slab = LANE // ATT_HEAD_DIM

    def pro(x_ref, ex, outs):
        y, _, _ = _rms_fwd(x_ref[...], ex[0][...])
        h = (y * (1.0 + ex[1][...]) + ex[2][...]).astype(BF16)
        outs[0][...] = h
        return h

    def epi(acc, ex, outs):
        c, u, d = ex[3][...], ex[4][...], ex[5][...]
        _, rec_ref, q_ref, k_ref, v_ref = outs
        for k in range(HG_SLABS):
            rec_ref[k] = acc[:, ATT_COLS + k * HG_WIDTH:ATT_COLS + (k + 1) * HG_WIDTH]

        def rope(z):
            return (z * c + pltpu.roll(z, half, 1) * u + pltpu.roll(z, LANE - half, 1) * d).astype(BF16)

        for s in range(ATT_WIDTH // LANE):
            slab = rope(acc[:, s * LANE:(s + 1) * LANE])
            for part in range(heads_per_slab):
                g, hh = divmod(s * heads_per_slab + part, ATT_GROUP)
                piece = slab[:, part * ATT_HEAD_DIM:(part + 1) * ATT_HEAD_DIM]
                for blk in range(tm // WINDOW):
                    q_ref[blk, g, hh * WINDOW:(hh + 1) * WINDOW, :] = piece[blk * WINDOW:(blk + 1) * WINDOW]
        rk = rope(acc[:, ATT_WIDTH:ATT_WIDTH + LANE])
        vv = acc[:, ATT_WIDTH + LANE:ATT_COLS].astype(BF16)
        for g in range(ATT_KV_HEADS):
            k_ref[g] = rk[:, g * ATT_HEAD_DIM:(g + 1) * ATT_HEAD_DIM]
            v_ref[g] = vv[:, g * ATT_HEAD_DIM:(g + 1) * ATT_HEAD_DIM]

    tab = pl.BlockSpec((tm, LANE), lambda b, i: (i, 0))
    kv_spec = pl.BlockSpec((None, ATT_KV_HEADS, tm, ATT_HEAD_DIM), lambda b, i: (b, 0, i, 0))
    kv_shape = _sds((bsz, ATT_KV_HEADS, seq, ATT_HEAD_DIM), BF16)
    q_spec = pl.BlockSpec((None, tm // WINDOW, ATT_KV_HEADS, GROUP_ROWS, ATT_HEAD_DIM), lambda b, i: (b, i, 0, 0, 0))
    return _mm_rows(x, w_in_t, name="in_proj", tm=tm, extra=(w, sc, sh, *tables),
                    extra_specs=[_vec_spec(), _row_spec(), _row_spec(), tab, tab, tab],
                    out_specs=[_tok_spec(tm), pl.BlockSpec((None, HG_SLABS, tm, HG_WIDTH), lambda b, i: (b, 0, i, 0)), q_spec,
                               kv_spec, kv_spec],
                    out_shape=[_sds(x.shape, BF16), _sds((bsz, HG_SLABS, seq, HG_WIDTH), F32),
                               _sds((bsz, seq // WINDOW, ATT_KV_HEADS, GROUP_ROWS, ATT_HEAD_DIM), BF16), kv_shape, kv_shape],
                    pro=pro, epi=epi, trans_b=True, comms=comms)


def _rope_tables(seq):
    half = ROPE_DIM // 2
    inv_freq = ROPE_THETA ** (-jnp.arange(0, ROPE_DIM, 2, dtype=F32) / ROPE_DIM)
    ang = jnp.arange(seq, dtype=F32)[:, None] * inv_freq[None, :]
    cos, sin = jnp.cos(ang), jnp.sin(ang)
    rest = ATT_HEAD_DIM - ROPE_DIM
    ones, zeros, zh = jnp.ones((seq, rest), F32), jnp.zeros((seq, rest), F32), jnp.zeros((seq, half), F32)
    reps = LANE // ATT_HEAD_DIM
    t_cos = jnp.tile(jnp.concatenate([cos, cos, ones], axis=1), (1, reps))
    t_up = jnp.tile(jnp.concatenate([zh, sin, zeros], axis=1), (1, reps))
    t_dn = jnp.tile(jnp.concatenate([-sin, zh, zeros], axis=1), (1, reps))
    return t_cos, t_up, t_dn


GROUP_ROWS = ATT_GROUP * WINDOW


ATT_BPS = 2


MASKED = -1e30


def _band_biases():
    row = jnp.arange(GROUP_ROWS)[:, None] % WINDOW
    col = jnp.arange(2 * WINDOW)[None, :]
    own = jnp.logical_and(col >= WINDOW, col - WINDOW <= row)
    before = jnp.logical_and(col < WINDOW, col > row)
    return (jnp.where(jnp.logical_or(own, before), 0.0, MASKED).astype(F32), jnp.where(own, 0.0, MASKED).astype(F32))


def _band_bias(full_ref, first_ref, has_prev):
    return full_ref[...] if has_prev is True else jnp.where(has_prev, full_ref[...], first_ref[...])


def _sink_column(sink_ref, g):
    head = lax.broadcasted_iota(jnp.int32, (GROUP_ROWS, 1), 0) // WINDOW
    col = jnp.full((GROUP_ROWS, 1), sink_ref[0, g * ATT_GROUP], F32)
    for hh in range(1, ATT_GROUP):
        col = jnp.where(head == hh, sink_ref[0, g * ATT_GROUP + hh], col)
    return col


def _sink_row(sink_ref, g):
    return jnp.concatenate([jnp.full((1, WINDOW), sink_ref[0, g * ATT_GROUP + hh], F32) for hh in range(ATT_GROUP)], axis=1)


def _bias_spec(transposed=False):
    shape = (2 * WINDOW, GROUP_ROWS) if transposed else (GROUP_ROWS, 2 * WINDOW)
    return pl.BlockSpec(shape, lambda b, i: (0, 0))


def _attn_specs():
    q_spec = pl.BlockSpec((None, ATT_BPS, ATT_KV_HEADS, GROUP_ROWS, ATT_HEAD_DIM), lambda b, i: (b, i, 0, 0, 0))
    kv_cur = pl.BlockSpec((None, ATT_KV_HEADS, ATT_BPS * WINDOW, ATT_HEAD_DIM), lambda b, i: (b, 0, i, 0))
    kv_prev = pl.BlockSpec((None, ATT_KV_HEADS, WINDOW, ATT_HEAD_DIM), lambda b, i: (b, 0, jnp.maximum(ATT_BPS * i - 1, 0), 0))
    return q_spec, kv_cur, kv_prev


def _band(prev_ref, cur_ref, g, blk):
    own = cur_ref[g, blk * WINDOW:(blk + 1) * WINDOW]
    before = prev_ref[g] if blk == 0 else cur_ref[g, (blk - 1) * WINDOW:blk * WINDOW]
    return jnp.concatenate([before, own], axis=0)


def _attn_fwd(qh, kh, vh, sinks, w_norm, biases, comms=()):
    bsz, nblk = qh.shape[0], qh.shape[1]
    seq = nblk * WINDOW
    rows = ATT_BPS * WINDOW

    def body(sink_ref, q_ref, kc_ref, kp_ref, vc_ref, vp_ref, w_ref, full_ref, first_ref, raw_ref, an_ref, l_ref):
        l_ref[...] = jnp.zeros_like(l_ref)
        def block(blk):
            bias = _band_bias(full_ref, first_ref, True if blk else pl.program_id(1) > 0)
            groups = range(ATT_KV_HEADS)
            keys, vals = [_band(kp_ref, kc_ref, g, blk) for g in groups], [_band(vp_ref, vc_ref, g, blk) for g in groups]
            sink = [_sink_column(sink_ref, g) for g in groups]
            s = [_dot(q_ref[blk, g], keys[g], NT_DIMS) * ATT_SCALE + bias for g in groups]
            yield
            m = [jnp.maximum(jnp.max(s[g], axis=-1, keepdims=True), sink[g]) for g in groups]
            p = [jnp.exp(s[g] - m[g]) for g in groups]
            den = [jnp.sum(p[g], axis=-1, keepdims=True) + jnp.exp(sink[g] - m[g]) for g in groups]
            yield
            o = [_dot(p[g] / den[g], vals[g]) for g in groups]
            yield
            lse = [m[g] + jnp.log(den[g]) for g in groups]
            tok = slice(blk * WINDOW, (blk + 1) * WINDOW)
            for g in groups:
                for hh in range(ATT_GROUP):
                    h = g * ATT_GROUP + hh
                    raw_ref[tok, h * ATT_HEAD_DIM:(h + 1) * ATT_HEAD_DIM] = o[g][hh * WINDOW:(hh + 1) * WINDOW]
                    l_ref[tok, h:h + 1] = lse[g][hh * WINDOW:(hh + 1) * WINDOW]

        _in_step(block(blk) for blk in range(ATT_BPS))
        y, _, _ = _rms_fwd(raw_ref[...], w_ref[...])
        an_ref[...] = y.astype(BF16)

    cur = lambda width: pl.BlockSpec((None, rows, width), lambda b, i: (b, i, 0))
    q_spec, kv_cur, kv_prev = _attn_specs()
    return _call(
        body, (sinks, qh, kh, kh, vh, vh, w_norm, *biases), name="attn_fwd", grid=(bsz, nblk // ATT_BPS),
        in_specs=[pl.BlockSpec(memory_space=pltpu.SMEM), q_spec, kv_cur, kv_prev, kv_cur, kv_prev, _vec_spec(ATT_WIDTH),
                  _bias_spec(), _bias_spec()],
        out_specs=[cur(ATT_WIDTH), cur(ATT_WIDTH), cur(LANE)],
        out_shape=[_sds((bsz, seq, ATT_WIDTH), F32), _sds((bsz, seq, MIX_WIDTH), BF16), _sds((bsz, seq, LANE), F32)],
        sem=("parallel", "parallel"), comms=comms)


HG_Q0 = ATT_COLS // LANE
HG_F0 = HG_Q0 + HG_HEADS
HG_I0 = HG_F0 + HG_HEADS
HG_G0 = HG_I0 + HG_HEADS
HG_SLABS = 4
HG_Q, HG_F, HG_I, HG_G = range(HG_SLABS)
HG_TOK = 256
HG_NCH = HG_TOK // HG_CHUNK
HG_HPS = 2


def _block_masks():
    row = jnp.arange(HG_TOK)[:, None]
    col = jnp.arange(HG_TOK)[None, :]
    same = (row // HG_CHUNK) == (col // HG_CHUNK)
    return jnp.logical_and(same, col <= row).astype(F32), jnp.logical_and(same, col >= row).astype(F32)


def _row_in_chunk():
    return lax.broadcasted_iota(jnp.int32, (HG_TOK, LANE), 0) % HG_CHUNK


def _chunk_cumsum(x, reverse=False):
    ric = _row_in_chunk()
    shift = 1
    while shift < HG_CHUNK:
        if reverse:
            x = x + jnp.where(ric < HG_CHUNK - shift, pltpu.roll(x, HG_TOK - shift, 0), 0.0)
        else:
            x = x + jnp.where(ric >= shift, pltpu.roll(x, shift, 0), 0.0)
        shift *= 2
    return x


def _chunk_rows(rows):
    stacked = jnp.concatenate([r[None] for r in rows], axis=0)
    return jnp.broadcast_to(stacked, (HG_NCH, HG_CHUNK, LANE)).reshape(HG_TOK, LANE)


def _chunk_slices(x):
    return [x[j * HG_CHUNK:(j + 1) * HG_CHUNK] for j in range(HG_NCH)]


def _in_step(stages):
    stages = list(stages)
    while stages:
        stages = [g for g in stages if next(g, stages) is not stages]


def _hgrn_common(tbl, hf, hq):
    lb = _sigmoid(tbl[1:2] - tbl[0:1])
    sig = _sigmoid(hf)
    f = lb + (1.0 - lb) * sig
    sq = _sigmoid(hq)
    q, k = hq * sq, 1.0 - f
    b = _chunk_cumsum(jnp.log(f))
    last = [b[(j + 1) * HG_CHUNK - 1:(j + 1) * HG_CHUNK] for j in range(HG_NCH)]
    bl = _chunk_rows(last)
    e_b, e_nb, e_rem = jnp.exp(b), jnp.exp(-b), jnp.exp(bl - b)
    e_last = [jnp.exp(r) for r in last]
    return dict(lb=lb, sig=sig, f=f, sq=sq, q=q, k=k, e_b=e_b, e_nb=e_nb, e_rem=e_rem, e_last=e_last,
                qd=q * e_b, kd=k * e_nb, ku=k * e_rem)


def _hgrn_fwd(proj, lb_table, norm_w, mix_in, masks, comms=()):
    bsz, _, seq, _ = proj.shape
    nstep = seq // HG_TOK

    def body(tbl_ref, nw_ref, p_ref, mix_ref, lower_ref, o_ref, rec_ref, st_ref, s_scr):
        @pl.when(pl.program_id(2) == 0)
        def _():
            s_scr[...] = jnp.zeros_like(s_scr)

        lower = lower_ref[...]

        def head(hp):
            ls = slice(hp * LANE, (hp + 1) * LANE)
            v, hg = p_ref[HG_I, :, ls], p_ref[HG_G, :, ls]
            t = _hgrn_common(tbl_ref[:, ls], p_ref[HG_F, :, ls], p_ref[HG_Q, :, ls])
            yield
            a = _dot(t["qd"], t["kd"], NT_DIMS) * lower
            o_intra = _dot(a, v)
            yield
            v_c, ku_c, qd_c = [_chunk_slices(z.astype(BF16)) for z in (v, t["ku"], t["qd"])]
            updates = [_dot(v_c[j], ku_c[j], TN_DIMS) for j in range(HG_NCH)]
            yield
            st = s_scr[hp]
            states = []
            for j in range(HG_NCH):
                states.append(st)
                st = st * t["e_last"][j] + updates[j]
            s_scr[hp] = st
            yield
            o = o_intra + jnp.concatenate([_dot(qd_c[j], states[j], NT_DIMS) for j in range(HG_NCH)], axis=0)
            yield
            st_ref[hp, 0] = states[0]
            o_ref[:, ls] = o
            y, _, _ = _rms_fwd(o, nw_ref[...])
            rec_ref[:, ls] = (y * (hg * _sigmoid(hg))).astype(BF16)

        _in_step(head(hp) for hp in range(HG_HPS))

    width = HG_HPS * LANE
    head_out = pl.BlockSpec((None, HG_TOK, width), lambda b, h, t: (b, t, h))
    mix_out = pl.BlockSpec((None, HG_TOK, width), lambda b, h, t: (b, t, ATT_WIDTH // width + h))
    return _call(
        body, (lb_table, norm_w, proj, mix_in, masks[0]), name="hgrn_fwd", grid=(bsz, HG_HEADS // HG_HPS, nstep),
        in_specs=[pl.BlockSpec((2, width), lambda b, h, t: (0, h)), pl.BlockSpec((1, LANE), lambda b, h, t: (0, 0)),
                  pl.BlockSpec((None, HG_SLABS, HG_TOK, width), lambda b, h, t: (b, 0, t, h)), pl.BlockSpec(memory_space=pl.ANY),
                  pl.BlockSpec((HG_TOK, HG_TOK), lambda b, h, t: (0, 0))],
        out_specs=[head_out, mix_out,
                   pl.BlockSpec((None, HG_HPS, 1, LANE, LANE), lambda b, h, t: (b, h, t, 0, 0))],
        out_shape=[_sds((bsz, seq, HG_WIDTH), F32), _sds(mix_in.shape, BF16),
                   _sds((bsz, HG_HEADS, nstep, LANE, LANE), F32)],
        scratch_shapes=[pltpu.VMEM((HG_HPS, LANE, LANE), F32)],
        sem=("parallel", "parallel", "arbitrary"), comms=comms, aliases={3: 1})


def _out_proj_fused(cat, w_out, x, post_w, g1, pre_w, sc2, sh2):
    tm = 512

    def epi(mix, ex, outs):
        x_ref, pw_ref, g1_ref, w2_ref, sc_ref, sh_ref = ex
        outs[0][...] = mix
        n1, _, _ = _rms_fwd(mix, pw_ref[...])
        x1 = x_ref[...] + g1_ref[...] * n1
        outs[1][...] = x1
        y2, _, _ = _rms_fwd(x1, w2_ref[...])
        outs[2][...] = (y2 * (1.0 + sc_ref[...]) + sh_ref[...]).astype(BF16)

    return _mm_rows(cat, w_out, name="out_proj", tm=tm, extra=(x, post_w, g1, pre_w, sc2, sh2),
                    extra_specs=[_tok_spec(tm), _vec_spec(), _row_spec(), _vec_spec(), _row_spec(), _row_spec()],
                    out_specs=[_tok_spec(tm), _tok_spec(tm), _tok_spec(tm)],
                    out_shape=[_sds(x.shape, F32), _sds(x.shape, F32), _sds(x.shape, BF16)], epi=epi)


def _acc_out(ref, first, value):
    @pl.when(first)
    def _():
        ref[...] = value

    @pl.when(jnp.logical_not(first))
    def _():
        ref[...] += value


def _down_proj_fused(r, w_down, x1, post_w, g2, target):
    tm = 512
    bsz = x1.shape[0]

    def pro(r_ref, ex, outs):
        rv = r_ref[...]
        return rv * rv

    def epi(down, ex, outs):
        x1_ref, w_ref, g2_ref, t_ref = ex
        loss_ref, dy_ref, dd_ref, dg2_ref, dw_ref = outs
        w, g2v = w_ref[...], g2_ref[...]
        gain = g2v * w
        dh, rstd = _rms_hat(down)
        err = x1_ref[...] + dh * gain - t_ref[...]
        part = (0.5 / D_MODEL) * jnp.sum(jnp.sum(err * err, axis=-1, keepdims=True), axis=0, keepdims=True)
        loss_ref[...] += jnp.broadcast_to(part, (1, LANE))
        dy = err * (1.0 / D_MODEL)
        dy_ref[...] = dy
        dd, per_col = _rms_bwd_gain(dy, gain, dh, rstd)
        dd_ref[...] = dd.astype(BF16)
        dg2_ref[...] += per_col * w
        dw_ref[...] += per_col * g2v

    return _mm_rows(r, w_down, name="down_proj", tm=tm, extra=(x1, post_w, g2, target),
                    extra_specs=[_tok_spec(tm), _vec_spec(), _row_spec(), _tok_spec(tm)],
                    out_specs=[_vec_spec(LANE), _tok_spec(tm), _tok_spec(tm), _row_spec(), _vec_spec()],
                    out_shape=[_sds((1, LANE), F32), _sds(x1.shape, F32), _sds(x1.shape, BF16), _sds((bsz, 1, D_MODEL), F32),
                               _sds((1, D_MODEL), F32)], pro=pro, epi=epi, parts=2, zero_per_seq=(3,), zero_once=(0, 4),
                    vmem_limit=VMEM_LIMIT_BIG)


def _up_bwd_fused(dpre, w_up4, dy, x1, mix, pre_w, sc2, post_w, g1, comms=()):
    tm = 512
    bsz = x1.shape[0]

    def epi(dh2v, ex, outs):
        dy_ref, x1_ref, mix_ref, w2_ref, sc_ref, pw_ref, g1_ref = ex
        dx1_ref, dmix_ref, dsc_ref, dsh_ref, dg1_ref, dw2_ref, dpw_ref = outs
        w2, pw, g1v = w2_ref[...], pw_ref[...], g1_ref[...]
        mod2 = 1.0 + sc_ref[...]
        xh2, rstd2 = _rms_hat(x1_ref[...])
        dsh_ref[...] += _colsum(dh2v)
        dx1n, per_col2 = _rms_bwd_gain(dh2v, mod2 * w2, xh2, rstd2)
        dsc_ref[...] += per_col2 * w2
        dw2_ref[...] += per_col2 * mod2
        dx1 = dy_ref[...] + dx1n
        dx1_ref[...] = dx1
        mh, rstd1 = _rms_hat(mix_ref[...])
        dmix, per_col1 = _rms_bwd_gain(dx1, g1v * pw, mh, rstd1)
        dmix_ref[...] = dmix.astype(BF16)
        dg1_ref[...] += per_col1 * pw
        dpw_ref[...] += per_col1 * g1v

    row_shape = _sds((bsz, 1, D_MODEL), F32)
    vec_shape = _sds((1, D_MODEL), F32)
    return _mm_rows(dpre, w_up4, name="up_bwd", tm=tm, extra=(dy, x1, mix, pre_w, sc2, post_w, g1),
                    extra_specs=[_tok_spec(tm), _tok_spec(tm), _tok_spec(tm), _vec_spec(), _row_spec(), _vec_spec(), _row_spec()],
                    out_specs=[_tok_spec(tm), _tok_spec(tm), _row_spec(), _row_spec(), _row_spec(), _vec_spec(), _vec_spec()],
                    out_shape=[_sds(x1.shape, F32), _sds(x1.shape, BF16), row_shape, row_shape, row_shape, vec_shape, vec_shape],
                    epi=epi, b_chunks=w_up4.shape[0], comms=comms, parts=2, zero_per_seq=(2, 3, 4), zero_once=(5, 6),
                    vmem_limit=VMEM_LIMIT_BIG)


def _norm1_bwd(dh1, dx1, x, pre_w, sc1, tm=512, comms=()):
    bsz, seq, _ = x.shape

    def body(dh_ref, dx1_ref, x_ref, w_ref, sc_ref, gx_ref, dsc_ref, dsh_ref, dw_ref):
        b, i = pl.program_id(0), pl.program_id(1)
        w = w_ref[...]
        dh = dh_ref[...]
        mod = 1.0 + sc_ref[...]
        xh, rstd = _rms_hat(x_ref[...])
        dx, per_col = _rms_bwd_gain(dh, mod * w, xh, rstd)
        _acc_out(dsh_ref, i == 0, _colsum(dh))
        _acc_out(dsc_ref, i == 0, per_col * w)
        _acc_out(dw_ref, jnp.logical_and(b == 0, i == 0), per_col * mod)
        gx_ref[...] = dx1_ref[...] + dx

    row_shape = _sds((bsz, 1, D_MODEL), F32)
    return _call(
        body, (dh1, dx1, x, pre_w, sc1), name="norm1_bwd", grid=(bsz, seq // tm),
        in_specs=[_tok_spec(tm), _tok_spec(tm), _tok_spec(tm), _vec_spec(), _row_spec()],
        out_specs=[_tok_spec(tm), _row_spec(), _row_spec(), _vec_spec()],
        out_shape=[_sds(x.shape, F32), row_shape, row_shape, _sds((1, D_MODEL), F32)],
        sem=("arbitrary", "arbitrary"), comms=comms)


def _hgrn_bwd(dcat, proj, o_raw, states, lb_table, norm_w, masks, comms=()):
    bsz, _, seq, _ = proj.shape
    nstep = seq // HG_TOK
    rec0 = ATT_WIDTH // LANE
    width = HG_HPS * LANE
    slabs = (HG_Q0, HG_F0, HG_I0, HG_G0)
    n_steps = (HG_HEADS // HG_HPS) * bsz * nstep
    assert n_steps >= 2

    def body(tbl_ref, nw_ref, dr_ref, p_ref, o_ref, st_ref, lower_ref, upper_ref,
             dproj_ref, dlb_ref, dnw_ref, ds_scr, grad_buf, grad_sem):
        h, b, t = pl.program_id(0), pl.program_id(1), pl.program_id(2)
        step = (h * bsz + b) * nstep + t
        slot = step % 2
        dq_k, df_k, di_k, dg_k = range(4)

        def grad_copies(of_step):
            hh, bb, tt = of_step // (bsz * nstep), (of_step // nstep) % bsz, of_step % nstep
            rows = pl.ds(pl.multiple_of((nstep - 1 - tt) * HG_TOK, HG_TOK), HG_TOK)
            return [pltpu.make_async_copy(
                grad_buf.at[of_step % 2, k],
                dproj_ref.at[bb, rows, pl.ds(pl.multiple_of(slabs[k] * LANE + hh * width, width), width)],
                grad_sem.at[of_step % 2, k]) for k in range(4)]

        @pl.when(step >= 2)
        def _():
            for cp in grad_copies(step - 2):
                cp.wait()

        @pl.when(t == 0)
        def _():
            ds_scr[...] = jnp.zeros_like(ds_scr)

        lower, upper = lower_ref[...], upper_ref[...]
        dlb_parts, dnw_parts = [None] * HG_HPS, [None] * HG_HPS

        def head(hp):
            ls = slice(hp * LANE, (hp + 1) * LANE)
            hq, v, hg = p_ref[HG_Q, :, ls], p_ref[HG_I, :, ls], p_ref[HG_G, :, ls]
            nw = nw_ref[...]
            c = _hgrn_common(tbl_ref[:, ls], p_ref[HG_F, :, ls], hq)
            qd, kd, ku = c["qd"], c["kd"], c["ku"]
            yield
            y, on, rstd = _rms_fwd(o_ref[:, ls], nw)
            sg = _sigmoid(hg)
            dr = dr_ref[:, ls]
            grad_buf[slot, dg_k, :, ls] = (dr * y * (sg * (1.0 + hg * (1.0 - sg)))).astype(BF16)
            do, dnw_rows = _rms_bwd(dr * (hg * sg), on, rstd, nw)
            yield
            at = _dot(kd, qd, NT_DIMS) * upper
            da = _dot(do, v, NT_DIMS) * lower
            dat = _dot(v, do, NT_DIMS) * upper
            yield
            dv = _dot(at, do)
            dqd = _dot(da, kd)
            dkd = _dot(dat, qd)
            yield
            do_c, qd_c, v_c, ku_c = [_chunk_slices(z.astype(BF16)) for z in (do, qd, v, ku)]
            outer = [_dot(do_c[j], qd_c[j], TN_DIMS) for j in range(HG_NCH)]
            yield
            ds = ds_scr[hp]
            ds_after = [None] * HG_NCH
            for j in reversed(range(HG_NCH)):
                ds_after[j] = ds
                ds = outer[j] + ds * c["e_last"][j]
            ds_scr[hp] = ds
            yield
            updates = [_dot(v_c[j], ku_c[j], TN_DIMS) for j in range(HG_NCH)]
            yield
            states = [st_ref[hp, 0]]
            for j in range(HG_NCH - 1):
                states.append(states[j] * c["e_last"][j] + updates[j])
            dv = dv + jnp.concatenate([_dot(ku_c[j], ds_after[j], NT_DIMS) for j in range(HG_NCH)], axis=0)
            dqd = dqd + jnp.concatenate([_dot(do_c[j], states[j]) for j in range(HG_NCH)], axis=0)
            dku = jnp.concatenate([_dot(v_c[j], ds_after[j]) for j in range(HG_NCH)], axis=0)
            yield
            dku_ku = dku * ku
            dbl = [_colsum(states[j] * ds_after[j]) * c["e_last"][j] + _colsum(dku_ku[j * HG_CHUNK:(j + 1) * HG_CHUNK])
                   for j in range(HG_NCH)]
            dk = dkd * c["e_nb"] + dku * c["e_rem"]
            db = dqd * qd - dkd * kd - dku_ku + jnp.where(_row_in_chunk() == HG_CHUNK - 1, _chunk_rows(dbl), 0.0)
            dfv = _chunk_cumsum(db, reverse=True) / c["f"] - dk
            sig, sq = c["sig"], c["sq"]
            grad_buf[slot, df_k, :, ls] = (dfv * (1.0 - c["lb"]) * sig * (1.0 - sig)).astype(BF16)
            grad_buf[slot, dq_k, :, ls] = (dqd * c["e_b"] * (sq * (1.0 + hq * (1.0 - sq)))).astype(BF16)
            grad_buf[slot, di_k, :, ls] = dv.astype(BF16)
            dlb_parts[hp] = _colsum(dfv * (1.0 - sig))
            dnw_parts[hp] = _colsum(dnw_rows)

        _in_step(head(hp) for hp in range(HG_HPS))
        _acc_out(dlb_ref, jnp.logical_and(b == 0, t == 0), jnp.concatenate(dlb_parts, axis=1))
        _acc_out(dnw_ref, jnp.logical_and(h == 0, jnp.logical_and(b == 0, t == 0)), sum(dnw_parts[1:], dnw_parts[0]))
        for cp in grad_copies(step):
            cp.start()

        @pl.when(step == n_steps - 1)
        def _():
            for cp in grad_copies(step - 1) + grad_copies(step):
                cp.wait()

    rev = lambda t: nstep - 1 - t
    slab = lambda first: pl.BlockSpec((None, HG_TOK, width), lambda h, b, t: (b, rev(t), first // HG_HPS + h))
    head = pl.BlockSpec((None, HG_TOK, width), lambda h, b, t: (b, rev(t), h))
    return _call(
        body, (lb_table, norm_w, dcat, proj, o_raw, states, *masks), name="hgrn_bwd",
        grid=(HG_HEADS // HG_HPS, bsz, nstep),
        in_specs=[pl.BlockSpec((2, width), lambda h, b, t: (0, h)), pl.BlockSpec((1, LANE), lambda h, b, t: (0, 0)),
                  slab(rec0), pl.BlockSpec((None, HG_SLABS, HG_TOK, width), lambda h, b, t: (b, 0, rev(t), h)), head,
                  pl.BlockSpec((None, HG_HPS, 1, LANE, LANE), lambda h, b, t: (b, h, rev(t), 0, 0)),
                  pl.BlockSpec((HG_TOK, HG_TOK), lambda h, b, t: (0, 0)), pl.BlockSpec((HG_TOK, HG_TOK), lambda h, b, t: (0, 0))],
        out_specs=[pl.BlockSpec(memory_space=pl.ANY), pl.BlockSpec((1, width), lambda h, b, t: (0, h)),
                   pl.BlockSpec((1, LANE), lambda h, b, t: (0, 0))],
        out_shape=[_sds((bsz, seq, IN_COLS), BF16), _sds((1, HG_WIDTH), F32), _sds((1, LANE), F32)],
        scratch_shapes=[pltpu.VMEM((HG_HPS, LANE, LANE), F32), pltpu.VMEM((2, 4, HG_TOK, width), BF16),
                        pltpu.SemaphoreType.DMA((2, 4))],
        sem=("arbitrary", "arbitrary", "arbitrary"), comms=comms)


def _attn_bwd(dcat, raw, w_norm, qh, kh, vh, lse, sinks, tables, biases, dproj, comms=()):
    bsz, nblk = qh.shape[0], qh.shape[1]
    seq = nblk * WINDOW
    nstep = nblk // ATT_BPS
    half = ROPE_DIM // 2

    def body(sink_ref, da_ref, raw_ref, w_ref, q_ref, kc_ref, kp_ref, vc_ref, vp_ref, l_ref, c_ref, u_ref, d_ref,
             full_ref, first_ref, dproj_ref, o_ref, dw_ref, dsink_ref, carry_k, carry_v):
        b, i = pl.program_id(0), pl.program_id(1)
        first = jnp.logical_and(b == 0, i == 0)

        @pl.when(i == 0)
        def _():
            carry_k[...] = jnp.zeros_like(carry_k)
            carry_v[...] = jnp.zeros_like(carry_v)

        w = w_ref[...]
        _, on, rstd = _rms_fwd(raw_ref[...], w)
        do_step, dw_rows = _rms_bwd(da_ref[...], on, rstd, w)
        _acc_out(dw_ref, first, _colsum(dw_rows))
        lane8 = lax.broadcasted_iota(jnp.int32, (1, ATT_Q_HEADS), 1)
        head_cols = jnp.where(lax.broadcasted_iota(jnp.int32, (2 * ATT_Q_HEADS, ATT_WIDTH), 1) // ATT_HEAD_DIM
                              == lax.broadcasted_iota(jnp.int32, (2 * ATT_Q_HEADS, ATT_WIDTH), 0), 1.0, 0.0)
        from_next = dict(k=carry_k[...], v=carry_v[...], dsink=jnp.zeros((1, ATT_Q_HEADS), F32))

        def block(blk):
            tok = slice(blk * WINDOW, (blk + 1) * WINDOW)
            bias = _band_bias(full_ref, first_ref, True if blk else i < nstep - 1)
            do_all = do_step[tok]
            c, u, d = c_ref[tok, :], u_ref[tok, :], d_ref[tok, :]
            lse_t = l_ref[tok, :].T
            prod = do_all * raw_ref[tok, :]
            prod_hi = prod.astype(BF16)
            prod_lo = prod - prod_hi.astype(F32)
            dsum_t = _dot(head_cols, prod_hi, NT_DIMS) + _dot(head_cols, prod_lo, NT_DIMS)

            def unrope(g):
                return (g * c + pltpu.roll(g * u, LANE - half, 1) + pltpu.roll(g * d, half, 1)).astype(BF16)

            groups = range(ATT_KV_HEADS)
            group_row = lambda z, g: jnp.concatenate(
                [z[g * ATT_GROUP + hh:g * ATT_GROUP + hh + 1, :] for hh in range(ATT_GROUP)], axis=1)
            q = [q_ref[blk, g] for g in groups]
            keys, vals = [_band(kp_ref, kc_ref, g, blk) for g in groups], [_band(vp_ref, vc_ref, g, blk) for g in groups]
            do_g = [jnp.concatenate([do_all[:, (g * ATT_GROUP + hh) * ATT_HEAD_DIM:(g * ATT_GROUP + hh + 1) * ATT_HEAD_DIM]
                                     for hh in range(ATT_GROUP)], axis=0) for g in groups]
            dsum, lse_g = [group_row(dsum_t, g) for g in groups], [group_row(lse_t, g) for g in groups]
            s_t = [_dot(keys[g], q[g], NT_DIMS) for g in groups]
            dp_t = [_dot(vals[g], do_g[g], NT_DIMS) for g in groups]
            yield
            p_t = [jnp.exp(s_t[g] * ATT_SCALE + bias - lse_g[g]) for g in groups]
            ds_t = [p_t[g] * (dp_t[g] - dsum[g]) * ATT_SCALE for g in groups]
            yield
            dq_g = [_dot(ds_t[g], keys[g], TN_DIMS) for g in groups]
            dk_g = [_dot(ds_t[g], q[g]) for g in groups]
            dv_g = [_dot(p_t[g], do_g[g]) for g in groups]
            yield
            for g in groups:
                sink_part = jnp.exp(_sink_row(sink_ref, g) - lse_g[g]) * dsum[g]
                for hh in range(ATT_GROUP):
                    head_sum = jnp.sum(sink_part[:, hh * WINDOW:(hh + 1) * WINDOW], axis=1, keepdims=True)
                    from_next["dsink"] = from_next["dsink"] - jnp.where(lane8 == g * ATT_GROUP + hh, head_sum, 0.0)
            dq_parts = [dq_g[g][hh * WINDOW:(hh + 1) * WINDOW] for g in groups for hh in range(ATT_GROUP)]
            dk_before, dk_own = [z[:WINDOW] for z in dk_g], [z[WINDOW:] for z in dk_g]
            dv_before, dv_own = [z[:WINDOW] for z in dv_g], [z[WINDOW:] for z in dv_g]
            per_slab = LANE // ATT_HEAD_DIM
            for s in range(ATT_WIDTH // LANE):
                slab = jnp.concatenate(dq_parts[s * per_slab:(s + 1) * per_slab], axis=1)
                o_ref[tok, s * LANE:(s + 1) * LANE] = unrope(slab)
            o_ref[tok, ATT_WIDTH:ATT_WIDTH + LANE] = unrope(jnp.concatenate(dk_own, axis=1) + from_next["k"])
            o_ref[tok, ATT_WIDTH + LANE:ATT_COLS] = (jnp.concatenate(dv_own, axis=1) + from_next["v"]).astype(BF16)
            from_next.update(k=jnp.concatenate(dk_before, axis=1), v=jnp.concatenate(dv_before, axis=1))

        _in_step(block(blk) for blk in reversed(range(ATT_BPS)))
        carry_k[...] = from_next["k"]
        carry_v[...] = from_next["v"]
        _acc_out(dsink_ref, first, from_next["dsink"])

    rows = ATT_BPS * WINDOW
    rev = lambda i: nstep - 1 - i
    cur = lambda width: pl.BlockSpec((None, rows, width), lambda b, i: (b, rev(i), 0))
    q_spec = pl.BlockSpec((None, ATT_BPS, ATT_KV_HEADS, GROUP_ROWS, ATT_HEAD_DIM), lambda b, i: (b, rev(i), 0, 0, 0))
    kv_cur = pl.BlockSpec((None, ATT_KV_HEADS, rows, ATT_HEAD_DIM), lambda b, i: (b, 0, rev(i), 0))
    kv_prev = pl.BlockSpec((None, ATT_KV_HEADS, WINDOW, ATT_HEAD_DIM), lambda b, i: (b, 0, jnp.maximum(ATT_BPS * rev(i) - 1, 0), 0))
    tab = pl.BlockSpec((rows, LANE), lambda b, i: (rev(i), 0))
    return _call(
        body, (sinks, dcat, raw, w_norm, qh, kh, kh, vh, vh, lse, *tables, *biases, dproj), name="attn_bwd", grid=(bsz, nstep),
        in_specs=[pl.BlockSpec(memory_space=pltpu.SMEM), cur(ATT_WIDTH), cur(ATT_WIDTH), _vec_spec(ATT_WIDTH), q_spec,
                  kv_cur, kv_prev, kv_cur, kv_prev, cur(LANE), tab, tab, tab, _bias_spec(True), _bias_spec(True),
                  pl.BlockSpec(memory_space=pl.ANY)],
        out_specs=[cur(ATT_COLS), _vec_spec(ATT_WIDTH), _vec_spec(ATT_Q_HEADS)],
        out_shape=[_sds(dproj.shape, BF16), _sds((1, ATT_WIDTH), F32), _sds((1, ATT_Q_HEADS), F32)],
        scratch_shapes=[pltpu.VMEM((WINDOW, LANE), F32), pltpu.VMEM((WINDOW, LANE), F32)],
        sem=("arbitrary", "arbitrary"), comms=comms, aliases={15: 0})


def _other_chips(x, y):
    return [(1 - x, y), (x, 1 - y), (1 - x, 1 - y)]


def _sem_pair(n):
    return [pltpu.SemaphoreType.DMA((n,)), pltpu.SemaphoreType.DMA((n,))]


def _plan_pair_forward(bufs):
    n = len(bufs)

    def copies(outs, sems):
        x, y, c = _mesh_pos()
        sends, lands = [], []
        for a in range(n):
            for j, chip in enumerate(_other_chips(x, y)):
                k = 3 * a + j
                slot = outs[a].at[4 * chip[0] + 2 * chip[1] + c]
                sends.append(pltpu.make_async_remote_copy(
                    src_ref=slot, dst_ref=slot, send_sem=sems[0].at[k], recv_sem=sems[1].at[k],
                    device_id=(x, y, 1 - c), device_id_type=MESH))
                theirs = outs[a].at[4 * chip[0] + 2 * chip[1] + 1 - c]
                lands.append(pltpu.make_async_remote_copy(
                    src_ref=theirs, dst_ref=theirs, send_sem=sems[0].at[k], recv_sem=sems[1].at[k],
                    device_id=(x, y, 1 - c), device_id_type=MESH))
        return sends, lands

    def start(ins, outs, sems):
        for cp in copies(outs, sems)[0]:
            cp.start()

    def finish(ins, outs, sems):
        sends, lands = copies(outs, sems)
        for cp in lands:
            cp.wait_recv()
        for cp in sends:
            cp.wait_send()

    return _Comm(list(bufs), [_sds(b.shape, b.dtype) for b in bufs], _sem_pair(3 * n), start, finish,
                 aliases=[(a, a) for a in range(n)])


def _plan_pair(arrays, other_half):
    n = len(arrays)
    per = N_CHIPS if other_half == "chip_major" else 1

    def copies(ins, outs, sems):
        x, y, c = _mesh_pos()
        out = []
        for a in range(n):
            for k in range(per):
                if other_half == "chip_major":
                    src, dst = ins[a].at[k, 1 - c], outs[a].at[k]
                else:
                    src, dst = (ins[a].at[1 - c] if other_half else ins[a]), outs[a]
                out.append(pltpu.make_async_remote_copy(
                    src_ref=src, dst_ref=dst, send_sem=sems[0].at[per * a + k], recv_sem=sems[1].at[per * a + k],
                    device_id=(x, y, 1 - c), device_id_type=MESH))
        return out

    def start(ins, outs, sems):
        for cp in copies(ins, outs, sems):
            cp.start()

    def finish(ins, outs, sems):
        for cp in copies(ins, outs, sems):
            cp.wait()

    if other_half == "chip_major":
        shapes = [_sds((a.shape[0],) + a.shape[2:], a.dtype) for a in arrays]
    else:
        shapes = [_sds(a.shape[1:] if other_half else a.shape, a.dtype) for a in arrays]
    return _Comm(list(arrays), shapes, _sem_pair(per * n), start, finish)


def _plan_chip_exchange(arrays):
    n = len(arrays)

    def copies(ins, outs, sems):
        x, y, c = _mesh_pos()
        sends, lands = [], []
        for a in range(n):
            for j, chip in enumerate(_other_chips(x, y)):
                k = 3 * a + j
                sends.append(pltpu.make_async_remote_copy(
                    src_ref=ins[a].at[2 * chip[0] + chip[1]], dst_ref=outs[a].at[2 * x + y], send_sem=sems[0].at[k],
                    recv_sem=sems[1].at[k], device_id=(*chip, c), device_id_type=MESH))
                slot = outs[a].at[2 * chip[0] + chip[1]]
                lands.append(pltpu.make_async_remote_copy(
                    src_ref=slot, dst_ref=slot, send_sem=sems[0].at[k], recv_sem=sems[1].at[k],
                    device_id=(*chip, c), device_id_type=MESH))
        return sends, lands

    def start(ins, outs, sems):
        for cp in copies(ins, outs, sems)[0]:
            cp.start()

    def finish(ins, outs, sems):
        sends, lands = copies(ins, outs, sems)
        for cp in lands:
            cp.wait_recv()
        for cp in sends:
            cp.wait_send()

    return _Comm(list(arrays), [_sds(a.shape, a.dtype) for a in arrays], _sem_pair(3 * n), start, finish)


SEM_SPEC = pl.BlockSpec(memory_space=pltpu.SEMAPHORE)
N_OTHER = N_CHIPS - 1


def _exchange_copies(s_ref, land_ref, sems):
    x, y, c = _mesh_pos()
    return [pltpu.make_async_remote_copy(
        src_ref=s_ref.at[2 * chip[0] + chip[1]], dst_ref=land_ref.at[2 * x + y], send_sem=sems[j], recv_sem=sems[N_OTHER + j],
        device_id=(*chip, c), device_id_type=MESH) for j, chip in enumerate(_other_chips(x, y))]


def _exchange_start(s, name):
    def body(s_ref, land_ref, *outs):
        sems, token = outs[:2 * N_OTHER], outs[-1]
        for cp in _exchange_copies(s_ref, land_ref, sems):
            cp.start()
        token[...] = jnp.zeros_like(token)

    hbm = pltpu.HBM(s.shape, s.dtype)
    res = pl.pallas_call(
        body, name=name,
        out_shape=(pltpu.SemaphoreType.DMA(()),) * (2 * N_OTHER) + (hbm, hbm, _sds((SUBLANES, LANE), F32)),
        in_specs=(HBM_SPEC, HBM_SPEC),
        out_specs=(SEM_SPEC,) * (2 * N_OTHER) + (HBM_SPEC, HBM_SPEC, pl.BlockSpec(memory_space=pltpu.VMEM)),
        input_output_aliases={0: 2 * N_OTHER, 1: 2 * N_OTHER + 1},
        compiler_params=pltpu.CompilerParams(has_side_effects=pltpu.SideEffectType.DATAFLOW_SIDE_EFFECTING),
    )(pltpu.with_memory_space_constraint(s, pltpu.HBM), pltpu.with_memory_space_constraint(lax.empty(s.shape, s.dtype), pltpu.HBM))
    return res[:2 * N_OTHER], res[2 * N_OTHER], res[2 * N_OTHER + 1], res[-1]


def _exchange_wait(sems, s_thru, land_thru, afters, name):
    def body(s_ref, land_ref, *rest):
        for cp in _exchange_copies(s_ref, land_ref, rest[:2 * N_OTHER]):
            cp.wait_send()
            cp.wait_recv()

    hbm = pltpu.HBM(s_thru.shape, s_thru.dtype)
    return pl.pallas_call(
        body, name=name, out_shape=(hbm, hbm),
        in_specs=(HBM_SPEC, HBM_SPEC) + (SEM_SPEC,) * (2 * N_OTHER) + (pl.BlockSpec(memory_space=pl.ANY),) * len(afters),
        out_specs=(HBM_SPEC, HBM_SPEC), input_output_aliases={0: 0, 1: 1},
        compiler_params=pltpu.CompilerParams(has_side_effects=pltpu.SideEffectType.DATAFLOW_SIDE_EFFECTING),
    )(s_thru, land_thru, *sems, *afters)


def _gather_copies(block_ref, buf_ref, sems):
    x, y, c = _mesh_pos()
    return [pltpu.make_async_remote_copy(
        src_ref=block_ref, dst_ref=buf_ref.at[4 * x + 2 * y + c], send_sem=sems[j], recv_sem=sems[N_OTHER + j],
        device_id=(*chip, c), device_id_type=MESH) for j, chip in enumerate(_other_chips(x, y))]


def _gather_start(blocks, bufs, afters, name):
    n = len(blocks)
    per = 2 * N_OTHER

    def body(*refs):
        ins, outs = refs[:2 * n], refs[2 * n + len(afters):]
        for a in range(n):
            for cp in _gather_copies(ins[a], ins[n + a], outs[a * per:(a + 1) * per]):
                cp.start()
        outs[-1][...] = jnp.zeros_like(outs[-1])

    hbm = [pltpu.HBM(z.shape, z.dtype) for z in list(blocks) + list(bufs)]
    res = pl.pallas_call(
        body, name=name,
        out_shape=(pltpu.SemaphoreType.DMA(()),) * (n * per) + tuple(hbm) + (_sds((SUBLANES, LANE), F32),),
        in_specs=(HBM_SPEC,) * (2 * n) + (pl.BlockSpec(memory_space=pl.ANY),) * len(afters),
        out_specs=(SEM_SPEC,) * (n * per) + (HBM_SPEC,) * (2 * n) + (pl.BlockSpec(memory_space=pltpu.VMEM),),
        input_output_aliases={k: n * per + k for k in range(2 * n)},
        compiler_params=pltpu.CompilerParams(has_side_effects=pltpu.SideEffectType.DATAFLOW_SIDE_EFFECTING),
    )(*[pltpu.with_memory_space_constraint(z, pltpu.HBM) for z in list(blocks) + list(bufs)], *afters)
    parts = [(res[a * per:(a + 1) * per], res[n * per + a], res[n * per + n + a]) for a in range(n)]
    return parts, res[-1]


def _gather_wait(part, afters, name):
    sems, block, buf = part

    def body(block_ref, buf_ref, *rest):
        for cp in _gather_copies(block_ref, buf_ref, rest[:2 * N_OTHER]):
            cp.wait_send()
            cp.wait_recv()

    return pl.pallas_call(
        body, name=name, out_shape=(pltpu.HBM(block.shape, block.dtype), pltpu.HBM(buf.shape, buf.dtype)),
        in_specs=(HBM_SPEC, HBM_SPEC) + (SEM_SPEC,) * (2 * N_OTHER) + (pl.BlockSpec(memory_space=pl.ANY),) * len(afters),
        out_specs=(HBM_SPEC, HBM_SPEC), input_output_aliases={0: 0, 1: 1},
        compiler_params=pltpu.CompilerParams(has_side_effects=pltpu.SideEffectType.DATAFLOW_SIDE_EFFECTING),
    )(block, buf, *sems, *afters)[1]


def _comm_only(comms, name):
    return _call(lambda: None, (), name=name, grid=(), in_specs=[], out_specs=[], out_shape=[], sem=(), comms=comms)[1]


def _allgather8(arrays, name):
    return _comm_only([_plan_allgather8(arrays)], name)[0]


def _plan_allgather8(arrays):
    n = len(arrays)

    def parts(ins, outs, sems):
        send_sems, recv_sems, local_sems = sems
        x, y, c = _mesh_pos()
        me, sibling = (x, y, c), (x, y, 1 - c)
        chips = _other_chips(x, y)

        def copy(a, k, block, to, src=None):
            dst = outs[a].at[4 * block[0] + 2 * block[1] + block[2]]
            return pltpu.make_async_remote_copy(
                src_ref=dst if src is None else src, dst_ref=dst, send_sem=send_sems.at[7 * a + k],
                recv_sem=recv_sems.at[7 * a + k], device_id=to, device_id_type=MESH)

        mine = [pltpu.make_async_copy(ins[a], outs[a].at[4 * x + 2 * y + c], local_sems.at[a]) for a in range(n)]
        first = []
        for a in range(n):
            first.append(copy(a, 0, me, sibling, src=ins[a]))
            first += [copy(a, 1 + j, me, (*chip, c), src=ins[a]) for j, chip in enumerate(chips)]
        return copy, mine, first, me, sibling, chips, c

    def start(ins, outs, sems):
        _, mine, first, *_ = parts(ins, outs, sems)
        for cp in mine + first:
            cp.start()

    def finish(ins, outs, sems):
        copy, mine, first, me, sibling, chips, c = parts(ins, outs, sems)
        passed = []
        for j, chip in enumerate(chips):
            for a in range(n):
                copy(a, 1 + j, (*chip, c), me).wait_recv()
                fwd = copy(a, 4 + j, (*chip, c), sibling)
                fwd.start()
                passed.append(fwd)
        for a in range(n):
            copy(a, 0, sibling, me).wait_recv()
            for j, chip in enumerate(chips):
                copy(a, 4 + j, (*chip, 1 - c), me).wait_recv()
        for cp in first + passed:
            cp.wait_send()
        for cp in mine:
            cp.wait()

    sems = [pltpu.SemaphoreType.DMA((7 * n,)), pltpu.SemaphoreType.DMA((7 * n,)), pltpu.SemaphoreType.DMA((n,))]
    return _Comm(list(arrays), [_sds((N_DEV,) + a.shape, a.dtype) for a in arrays], sems, start, finish)


def _pair_sum(g, q, core, name, chip_major=False):
    rows, cols = g.shape[2:]
    tr = _row_tile(rows)

    def body(core_ref, g_ref, q_ref, o_ref):
        o_ref[...] = (g_ref[...] + q_ref[...]).astype(BF16)

    blk = pl.BlockSpec((None, tr, cols), lambda k, i, core_ref: (k, i, 0))
    if chip_major:
        own = pl.BlockSpec((None, None, tr, cols), lambda k, i, core_ref: (k, core_ref[0], i, 0))
    else:
        own = pl.BlockSpec((None, None, tr, cols), lambda k, i, core_ref: (core_ref[0], k, i, 0))
    return pl.pallas_call(
        body, name=name,
        grid_spec=pltpu.PrefetchScalarGridSpec(num_scalar_prefetch=1, grid=(N_CHIPS, rows // tr), in_specs=[own, blk], out_specs=blk),
        out_shape=_sds((N_CHIPS, rows, cols), BF16), compiler_params=_params("parallel", "parallel"),
    )(core, g, q)


def _sum_chips(own, landed, chip, name):
    _, rows, cols = own.shape
    tr = _row_tile(rows)

    def body(chip_ref, own_ref, a_ref, b_ref, c_ref, o_ref):
        acc = own_ref[...].astype(F32) + a_ref[...].astype(F32)
        o_ref[...] = (acc + b_ref[...].astype(F32)) + c_ref[...].astype(F32)

    blk = lambda flip: pl.BlockSpec((None, tr, cols), lambda i, chip_ref: (jnp.bitwise_xor(chip_ref[0], flip), i, 0))
    return pl.pallas_call(
        body, name=name,
        grid_spec=pltpu.PrefetchScalarGridSpec(num_scalar_prefetch=1, grid=(rows // tr,), in_specs=[blk(0), blk(1), blk(2), blk(3)],
                                               out_specs=pl.BlockSpec((tr, cols), lambda i, chip_ref: (i, 0))),
        out_shape=_sds((rows, cols), F32), compiler_params=_params("parallel"),
    )(chip, own, landed, landed, landed)


SUBLANES = 8


def _tile_rows(n_elems):
    return -(-n_elems // (SUBLANES * LANE)) * SUBLANES


SMALL_ITEMS = (("b_ada", N_MOD * D_MODEL), ("pre_w_mix", D_MODEL), ("post_w_mix", D_MODEL), ("pre_w_mlp", D_MODEL),
               ("post_w_mlp", D_MODEL), ("attn_out_w", ATT_WIDTH), ("hg_norm_w", HG_HEAD_DIM), ("attn_sinks", ATT_Q_HEADS),
               ("lb_0", HG_WIDTH), ("lb_1", HG_WIDTH))
SMALL_AT = {}
for _name, _size in SMALL_ITEMS:
    SMALL_AT[_name] = (sum(r for _, r in SMALL_AT.values()), _tile_rows(_size))
SMALL_ROWS = sum(r for _, r in SMALL_AT.values())
MOD_ROWS = SMALL_AT["b_ada"][1]
PLAIN_ROWS = SMALL_AT["lb_0"][0] - MOD_ROWS
LB_ROWS = SMALL_AT["lb_0"][1]


def _rows(a, nrows=None):
    flat = a.reshape(-1)
    nrows = _tile_rows(flat.shape[0]) if nrows is None else nrows
    return jnp.pad(flat, (0, nrows * LANE - flat.shape[0])).reshape(nrows, LANE)


def _pack_small(vals):
    vals = dict(vals, lb_0=vals["lb_table"][0], lb_1=vals["lb_table"][1])
    return jnp.concatenate([_rows(vals[name], SMALL_AT[name][1]) for name, _ in SMALL_ITEMS], axis=0)


def _unpack_small(p):
    def item(name, shape):
        first = SMALL_AT[name][0]
        size = shape[0] * shape[1]
        return p[first:first + SMALL_AT[name][1]].reshape(-1)[:size].reshape(shape)

    out = {name: item(name, (1, size)) for name, size in SMALL_ITEMS if not name.startswith("lb_")}
    out["lb_table"] = jnp.concatenate([item("lb_0", (1, HG_WIDTH)), item("lb_1", (1, HG_WIDTH))], axis=0)
    return out


def _pack_partials(dmod, plain, d_lb, loss_row):
    return jnp.concatenate([_rows(dmod, dmod.shape[0] * MOD_ROWS)] + [_rows(g) for g in plain] + [_rows(d_lb), _rows(loss_row)], axis=0)


def _small_update(packs, w, m, v, n_seq):
    mod_end = n_seq * MOD_ROWS
    lb_at = mod_end + PLAIN_ROWS
    t0, t1 = SMALL_AT["lb_0"][0], SMALL_AT["lb_1"][0]

    def body(p_ref, w_ref, m_ref, v_ref, g_ref, dl_ref, nm_ref, nv_ref, loss_ref):
        tot = p_ref[0]
        for d in range(1, N_DEV):
            tot = tot + p_ref[d]
        wv = w_ref[...]
        p1 = _sigmoid(wv[t1:t1 + LB_ROWS] - wv[t0:t0 + LB_ROWS])
        s = tot[lb_at:lb_at + LB_ROWS] * p1 * (1.0 - p1)
        g_bias = tot[0:MOD_ROWS]
        for q in range(1, n_seq):
            g_bias = g_bias + tot[q * MOD_ROWS:(q + 1) * MOD_ROWS]
        g = jnp.concatenate([g_bias, tot[mod_end:lb_at], -s, s], axis=0)
        g_ref[...] = g
        dl_ref[...], nm_ref[...], nv_ref[...] = _adamw_math(g, wv, m_ref[...], v_ref[...])
        loss_ref[...] = tot[lb_at + LB_ROWS:lb_at + LB_ROWS + SUBLANES]

    shp = _sds((SMALL_ROWS, LANE), F32)
    return pl.pallas_call(body, name="small_update", out_shape=[shp] * 4 + [_sds((SUBLANES, LANE), F32)],
                          compiler_params=_params())(packs, w, m, v)


def kernel(x, c, w_ada, b_ada, pre_w_mix, w_in, attn_sinks, attn_out_w, lb_table, hg_norm_w, w_out, post_w_mix, pre_w_mlp, w_up, w_down, post_w_mlp, loss_target, m_w_ada, m_b_ada, m_pre_w_mix, m_w_in, m_attn_sinks, m_attn_out_w, m_lb_table, m_hg_norm_w, m_w_out, m_post_w_mix, m_pre_w_mlp, m_w_up, m_w_down, m_post_w_mlp, v_w_ada, v_b_ada, v_pre_w_mix, v_w_in, v_attn_sinks, v_attn_out_w, v_lb_table, v_hg_norm_w, v_w_out, v_post_w_mix, v_pre_w_mlp, v_w_up, v_w_down, v_post_w_mlp):
    xi, yi, ci = _mesh_pos()
    chip = 2 * xi + yi
    dev = 2 * chip + ci
    bsz, seq, _ = x.shape
    ntok = bsz * seq
    ada_cols = w_ada.shape[2]
    core = jnp.reshape(ci, (1,)).astype(jnp.int32)
    chip_idx = jnp.reshape(chip, (1,)).astype(jnp.int32)
    flat = lambda a: a.reshape(ntok, a.shape[-1])
    unflat = lambda a: a.reshape(bsz, seq, a.shape[-1])
    tables = _rope_tables(seq)
    biases, chunk_masks = _band_biases(), _block_masks()

    def row_half(w):
        rows = w.shape[1] // 2
        return lax.dynamic_slice_in_dim(w[0], ci * rows, rows, axis=0).astype(BF16)

    def gather_buffer(w):
        rows, cols = w.shape[1] // 2, w.shape[2]
        own = w[0].astype(BF16).reshape(2, rows, cols)
        return lax.dynamic_update_slice(lax.empty((N_DEV, rows, cols), BF16), own, (2 * chip, 0, 0))

    w_in_t, m_in_t, v_in_t = [jnp.transpose(a[0])[None] for a in (w_in, m_w_in, v_w_in)]
    c_g, in_g = _allgather8([c, row_half(w_in_t)], "gather_first")
    c_all = c_g.reshape(N_DEV * bsz, D_MODEL)
    w_in_full = in_g.reshape(IN_COLS, D_MODEL)

    b_cols = lax.dynamic_slice_in_dim(b_ada, chip * ada_cols, ada_cols, axis=1)
    mod_part = _ada_fwd(c_all, w_ada[0], b_cols)
    half_rows = mod_part.shape[0] // 2
    (mod_g,) = _allgather8([lax.dynamic_slice_in_dim(mod_part, ci * half_rows, half_rows, axis=0)], "gather_mod")
    mod_all = mod_g.reshape(N_CHIPS, 2, half_rows, ada_cols).transpose(1, 2, 0, 3).reshape(N_DEV * bsz, N_MOD * D_MODEL)
    mod = lax.dynamic_slice_in_dim(mod_all, dev * bsz, bsz, axis=0)
    sh1, sc1, g1, sh2, sc2, g2 = [mod[:, i * D_MODEL:(i + 1) * D_MODEL].reshape(bsz, 1, D_MODEL) for i in range(N_MOD)]

    weights = (w_out, w_up, w_down)
    (out_part, up_part, down_part), started = _gather_start(
        [row_half(w) for w in weights], [gather_buffer(w) for w in weights], [mod_g], "gather_weights_start")

    h1, proj, qh, kh, vh = _in_proj_fused(x, pre_w_mix, sc1 + started[0:1, 0:1], sh1, w_in_full, tables)
    out_g = _gather_wait(out_part, [proj], "gather_out_wait")
    (attn_raw, cat, lse), ((out_g,),) = _attn_fwd(qh, kh, vh, attn_sinks, attn_out_w, biases, comms=[_plan_pair_forward([out_g])])
    up_g = _gather_wait(up_part, [attn_raw], "gather_up_wait")
    (o_raw, cat, states), ((up_g,),) = _hgrn_fwd(proj, lb_table, hg_norm_w, cat, chunk_masks, comms=[_plan_pair_forward([up_g])])
    down_g = _gather_wait(down_part, [o_raw], "gather_down_wait")
    w_out_full = out_g.reshape(D_MODEL, D_MODEL)
    w_up4 = up_g.reshape(N_CHIPS, D_MODEL, D_MODEL)
    mix, x1, h2 = _out_proj_fused(cat, w_out_full, x, post_w_mix, g1, pre_w_mlp, sc2, sh2)
    big_tm = min(ntok, 2048)
    up_spec = pl.BlockSpec((None, D_MODEL, D_MODEL), lambda i, j: (j, 0, 0))
    r, ((down_g,),) = _mm(flat(h2), w_up4, name="up_proj", out_dtype=BF16, tm=big_tm, tn=D_MODEL, n_out=D_FF, b_spec=up_spec,
                          epi=lambda acc: jnp.maximum(acc, 0.0), comms=[_plan_pair_forward([down_g])])
    w_down_full = down_g.reshape(D_FF, D_MODEL)
    square = lambda t: t * t
    loss_row, dy, dd, dg2, d_post_mlp = _down_proj_fused(unflat(r), w_down_full, x1, post_w_mlp, g2, loss_target)

    dpre = _mm(flat(dd), w_down_full, name="down_bwd", out_dtype=BF16, trans_b=True, tm=big_tm, tn=D_MODEL, extra=(r,),
               epi=lambda acc, rt: acc * (2.0 * rt.astype(F32)))
    half_rows = D_MODEL // 2
    g_down = _mm_tn(r, flat(dd), name="down_wgrad", tk=half_rows, tn=D_MODEL, a_fn=square,
                    out_shape=_sds((2, N_CHIPS, half_rows, D_MODEL), F32),
                    out_spec=pl.BlockSpec((None, None, half_rows, D_MODEL), lambda i, j: (i % 2, i // 2, 0, 0)))
    (dx1, dmix, dsc2, dsh2, dg1, d_pre_mlp, d_post_mix), ((q_down,),) = _up_bwd_fused(
        unflat(dpre), w_up4, dy, x1, mix, pre_w_mlp, sc2, post_w_mix, g1, comms=[_plan_pair([g_down], True)])
    g_up = _mm_tn(flat(h2), dpre, name="up_wgrad", tk=D_MODEL, tn=half_rows,
                  out_shape=_sds((2, N_CHIPS, half_rows, D_MODEL), F32),
                  out_spec=pl.BlockSpec((2, None, half_rows, half_rows), lambda i, j: (0, j // 2, 0, j % 2)))
    s_down = _pair_sum(g_down, q_down, core, "pair_sum_down")

    dcat, ((q_up,),) = _mm(flat(dmix), w_out_full, name="out_bwd", out_dtype=F32, trans_b=True, comms=[_plan_pair([g_up], True)])
    dcat = unflat(dcat)
    s_up = _pair_sum(g_up, q_up, core, "pair_sum_up")
    out_rows = D_MODEL // N_CHIPS
    g_out = _mm_tn(flat(cat), flat(dmix), name="out_wgrad", tk=2 * out_rows, tn=half_rows,
                   out_shape=_sds((2, N_CHIPS, out_rows, half_rows), F32),
                   out_spec=pl.BlockSpec((None, 2, out_rows, half_rows), lambda i, j: (j, i, 0, 0)))
    (dproj_rec, d_lb, d_hg_norm), ((x_down,), (q_out,)) = _hgrn_bwd(
        dcat, proj, o_raw, states, lb_table, hg_norm_w, chunk_masks, comms=[_plan_chip_exchange([s_down]), _plan_pair([g_out], True)])
    half_down = _sum_chips(s_down, x_down, chip_idx, "sum_chips_down")
    s_out = _pair_sum(g_out, q_out, core, "pair_sum_out")
    (dproj, d_attn_out, d_sinks), ((their_down,), (x_up,)) = _attn_bwd(
        dcat, attn_raw, attn_out_w, qh, kh, vh, lse, attn_sinks, tables, [bias.T for bias in biases], dproj_rec,
        comms=[_plan_pair([half_down], False), _plan_chip_exchange([s_up])])
    half_up = _sum_chips(s_up, x_up, chip_idx, "sum_chips_up")
    dproj = flat(dproj)
    in_rows = IN_COLS // N_CHIPS // 2
    g_in, ((x_out,),) = _mm_tn(dproj, flat(h1), name="in_wgrad", tk=2 * LANE, tn=D_MODEL, comms=[_plan_chip_exchange([s_out])])
    g_in = g_in.reshape(N_CHIPS, 2, in_rows, D_MODEL)
    half_out = _sum_chips(s_out, x_out, chip_idx, "sum_chips_out")
    dh1, ((q_in,), (their_up, their_out)) = _mm(
        dproj, w_in_full, name="in_bwd", out_dtype=F32,
        comms=[_plan_pair([g_in], "chip_major"), _plan_pair([half_up, half_out], False)])
    s_in = _pair_sum(g_in, q_in, core, "pair_sum_in", chip_major=True)
    in_sems, s_in, in_landing, started = _exchange_start(s_in, "exchange_in_start")
    grad_x, dsc1, dsh1, d_pre_mix = _norm1_bwd(unflat(dh1), dx1, x, pre_w_mix + started[0:1, 0:1], sc1)

    dmod = jnp.concatenate([dsh1, dsc1, dg1, dsh2, dsc2, dg2], axis=-1).reshape(bsz, N_MOD * D_MODEL)
    pack = _pack_partials(dmod, [d_pre_mix, d_post_mix, d_pre_mlp, d_post_mlp, d_attn_out, d_hg_norm, d_sinks], d_lb, loss_row)
    ((packs,),) = _comm_only([_plan_allgather8([pack])], "gather_small")
    w_small = dict(b_ada=b_ada, pre_w_mix=pre_w_mix, post_w_mix=post_w_mix, pre_w_mlp=pre_w_mlp, post_w_mlp=post_w_mlp,
                   attn_out_w=attn_out_w, hg_norm_w=hg_norm_w, attn_sinks=attn_sinks, lb_table=lb_table)
    m_small = dict(b_ada=m_b_ada, pre_w_mix=m_pre_w_mix, post_w_mix=m_post_w_mix, pre_w_mlp=m_pre_w_mlp, post_w_mlp=m_post_w_mlp,
                   attn_out_w=m_attn_out_w, hg_norm_w=m_hg_norm_w, attn_sinks=m_attn_sinks, lb_table=m_lb_table)
    v_small = dict(b_ada=v_b_ada, pre_w_mix=v_pre_w_mix, post_w_mix=v_post_w_mix, pre_w_mlp=v_pre_w_mlp, post_w_mlp=v_post_w_mlp,
                   attn_out_w=v_attn_out_w, hg_norm_w=v_hg_norm_w, attn_sinks=v_attn_sinks, lb_table=v_lb_table)
    *small_packed, loss_rows = _small_update(packs, _pack_small(w_small), _pack_small(m_small), _pack_small(v_small), bsz)
    small_out = [_unpack_small(p) for p in small_packed]
    loss = loss_rows[0, 0]

    dmod_all = packs[:, :bsz * MOD_ROWS, :].reshape(N_DEV * bsz, N_MOD * D_MODEL)
    dmod_cols = lax.dynamic_slice_in_dim(dmod_all, chip * ada_cols, ada_cols, axis=1)
    ada_out = _ada_bwd_adamw(c_all, dmod_cols, w_ada[0], m_w_ada[0], v_w_ada[0])

    s_in, x_in = _exchange_wait(in_sems, s_in, in_landing, [grad_x, ada_out[0]], "exchange_in_wait")
    half_in = _sum_chips(s_in, x_in, chip_idx, "sum_chips_in")
    ((their_in,),) = _comm_only([_plan_pair([half_in], False)], "pair_swap_in")
    big = dict(
        w_in=tuple(jnp.transpose(a) for a in _adamw_halves(half_in, their_in, core, w_in_t[0], m_in_t[0], v_in_t[0], axis=0,
                                                           name="adamw_in")),
        w_up=tuple(_adamw_halves(half_up, their_up, core, w_up[0], m_w_up[0], v_w_up[0], axis=0, name="adamw_up")),
        w_out=tuple(_adamw_halves(half_out, their_out, core, w_out[0], m_w_out[0], v_w_out[0], axis=1, name="adamw_out")),
        w_down=tuple(_adamw_halves(half_down, their_down, core, w_down[0], m_w_down[0], v_w_down[0], axis=0, name="adamw_down")),
        w_ada=tuple(ada_out),
    )
    order = ("w_ada", "b_ada", "pre_w_mix", "w_in", "attn_sinks", "attn_out_w", "lb_table", "hg_norm_w", "w_out", "post_w_mix",
             "pre_w_mlp", "w_up", "w_down", "post_w_mlp")
    outs = [loss, grad_x]
    for kind in range(4):
        for nm in order:
            outs.append(big[nm][kind][None] if nm in big else small_out[kind][nm])
    return tuple(outs)
```

```python
import jax
import jax.numpy as jnp
from jax import lax
from jax.experimental import pallas as pl
from jax.experimental.pallas import tpu as pltpu

F32 = jnp.float32
BF16 = jnp.bfloat16

D_MODEL = 1024
ATT_WIDTH = 512
ATT_HEAD_DIM = 64
ATT_Q_HEADS = 8
ATT_KV_HEADS = 2
ATT_GROUP = ATT_Q_HEADS // ATT_KV_HEADS
ATT_KV_COLS = ATT_KV_HEADS * ATT_HEAD_DIM
WINDOW = 128
ROPE_DIM = 16
ROPE_THETA = 500000.0
HG_WIDTH = 512
MIX_WIDTH = ATT_WIDTH + HG_WIDTH
HG_HEAD_DIM = 128
HG_HEADS = 4
HG_CHUNK = 32
IN_COLS = ATT_WIDTH + 2 * ATT_KV_COLS + 4 * HG_WIDTH
ATT_COLS = ATT_WIDTH + 2 * ATT_KV_COLS
D_FF = 4 * D_MODEL
N_MOD = 6
EPS = 1e-6
ATT_SCALE = ATT_HEAD_DIM ** -0.5

ADAM_LR = 0.001
ADAM_B1 = 0.9
ADAM_B2 = 0.999
ADAM_EPS = 1e-08
ADAM_WD = 0.01
ADAM_STEP = 10

N_CHIPS = 4
N_DEV = 8
LANE = 128
VMEM_LIMIT = 48 * 1024 * 1024
VMEM_LIMIT_BIG = 58 * 1024 * 1024
MESH = pl.DeviceIdType.MESH

NT_DIMS = (((1,), (1,)), ((), ()))
TN_DIMS = (((0,), (0,)), ((), ()))


def _sds(shape, dtype):
    return jax.ShapeDtypeStruct(tuple(shape), dtype)


def _params(*sem, vmem_limit=None):
    return pltpu.CompilerParams(dimension_semantics=sem, vmem_limit_bytes=VMEM_LIMIT if vmem_limit is None else vmem_limit)


def _sigmoid(x):
    return 1.0 / (1.0 + jnp.exp(-x))


def _dot(a, b, dims=None):
    a, b = a.astype(BF16), b.astype(BF16)
    if dims is None:
        return jnp.dot(a, b, preferred_element_type=F32)
    return lax.dot_general(a, b, dims, preferred_element_type=F32)


def _rms_fwd(x, w):
    rstd = lax.rsqrt(jnp.mean(x * x, axis=-1, keepdims=True) + EPS)
    xh = x * rstd
    return xh * w, xh, rstd


def _rms_bwd(dy, xh, rstd, w):
    dxh = dy * w
    dx = rstd * (dxh - xh * jnp.mean(dxh * xh, axis=-1, keepdims=True))
    return dx, dy * xh


def _colsum(x):
    return jnp.sum(x, axis=0, keepdims=True)


def _rms_hat(x):
    rstd = lax.rsqrt(jnp.mean(x * x, axis=-1, keepdims=True) + EPS)
    return x * rstd, rstd


def _rms_bwd_gain(dy, gain, xh, rstd):
    dxh = dy * gain
    dx = rstd * (dxh - xh * jnp.mean(dxh * xh, axis=-1, keepdims=True))
    return dx, _colsum(dy * xh)


def _row_tile(rows, cap=256):
    return max(t for t in range(16, cap + 1, 16) if rows % t == 0)


HBM_SPEC = pl.BlockSpec(memory_space=pltpu.HBM)


def _mesh_pos():
    return lax.axis_index("x"), lax.axis_index("y"), lax.axis_index("c")


class _Comm:
    def __init__(self, ins, outs, sems, start, finish, aliases=()):
        self.ins, self.outs, self.sems = list(ins), list(outs), list(sems)
        self.start, self.finish, self.aliases = start, finish, tuple(aliases)


def _call(body, args, *, name, grid, in_specs, out_specs, out_shape, sem, scratch_shapes=(), comms=(), aliases=None,
          vmem_limit=None):
    scratch_shapes = list(scratch_shapes)
    if not comms:
        return pl.pallas_call(body, name=name, grid=grid, in_specs=in_specs, out_specs=out_specs, out_shape=out_shape,
                              input_output_aliases=dict(aliases or {}), scratch_shapes=scratch_shapes,
                              compiler_params=_params(*sem, vmem_limit=vmem_limit))(*args)
    single = not isinstance(out_shape, (list, tuple))
    out_specs_l = [out_specs] if single else list(out_specs)
    out_shape_l = [out_shape] if single else list(out_shape)
    n_in, n_out, n_scr = len(in_specs), len(out_shape_l), len(scratch_shapes)
    n_ci = [len(cm.ins) for cm in comms]
    n_co = [len(cm.outs) for cm in comms]
    n_cs = [len(cm.sems) for cm in comms]
    aliases = dict(aliases or {})
    for k, cm in enumerate(comms):
        for i, o in cm.aliases:
            aliases[n_in + sum(n_ci[:k]) + i] = n_out + sum(n_co[:k]) + o

    def fused(*refs):
        pos = [0]

        def take(n):
            part = refs[pos[0]:pos[0] + n]
            pos[0] += n
            return part

        ins = take(n_in)
        c_ins = [take(n) for n in n_ci]
        outs = take(n_out)
        c_outs = [take(n) for n in n_co]
        scr = take(n_scr)
        c_sems = [take(n) for n in n_cs]
        first, last = True, True
        for d, size in enumerate(grid):
            first = jnp.logical_and(first, pl.program_id(d) == 0)
            last = jnp.logical_and(last, pl.program_id(d) == size - 1)

        def run(which):
            for cm, ci, co, cs in zip(comms, c_ins, c_outs, c_sems):
                getattr(cm, which)(ci, co, cs)

        if grid:
            pl.when(first)(lambda: run("start"))
        else:
            run("start")
        body(*ins, *outs, *scr)
        if grid:
            pl.when(last)(lambda: run("finish"))
        else:
            run("finish")

    res = pl.pallas_call(
        fused, name=name, grid=grid, in_specs=list(in_specs) + [HBM_SPEC] * sum(n_ci),
        out_specs=out_specs_l + [HBM_SPEC] * sum(n_co), out_shape=out_shape_l + [s for cm in comms for s in cm.outs],
        input_output_aliases=aliases, scratch_shapes=scratch_shapes + [s for cm in comms for s in cm.sems],
        compiler_params=_params(*["arbitrary"] * len(grid), vmem_limit=vmem_limit),
    )(*args, *[a for cm in comms for a in cm.ins])
    main = res[:n_out]
    extra, at = [], n_out
    for n in n_co:
        extra.append(list(res[at:at + n]))
        at += n
    return (main[0] if single else list(main)), extra


def _mm(a, b, *, name, out_dtype, trans_b=False, tm=512, tn=None, extra=(), epi=None, b_spec=None, n_out=None, comms=()):
    m_total, k_total = a.shape
    if n_out is None:
        n_out = b.shape[0] if trans_b else b.shape[1]
    tn = n_out if tn is None else tn
    grid = (m_total // tm, n_out // tn)
    dims = NT_DIMS if trans_b else None

    def body(*refs):
        a_ref, b_ref = refs[0], refs[1]
        extra_refs = refs[2:2 + len(extra)]
        o_ref = refs[2 + len(extra)]
        acc = _dot(a_ref[...], b_ref[...], dims)
        if epi is not None:
            acc = epi(acc, *[r[...] for r in extra_refs])
        o_ref[...] = acc.astype(out_dtype)

    if b_spec is None:
        if trans_b:
            b_spec = pl.BlockSpec((tn, k_total), lambda i, j: (j, 0))
        else:
            b_spec = pl.BlockSpec((k_total, tn), lambda i, j: (0, j))
    in_specs = [pl.BlockSpec((tm, k_total), lambda i, j: (i, 0)), b_spec]
    in_specs += [pl.BlockSpec((tm, tn), lambda i, j: (i, j)) for _ in extra]
    return _call(
        body, (a, b, *extra), name=name, grid=grid, in_specs=in_specs,
        out_specs=pl.BlockSpec((tm, tn), lambda i, j: (i, j)),
        out_shape=_sds((m_total, n_out), out_dtype),
        sem=("parallel", "parallel"), comms=comms)


def _mm_tn(a, b, *, name, tk, tn, a_fn=None, out_shape=None, out_spec=None, comms=()):
    m_total, k_total = a.shape
    n_total = b.shape[1]
    grid = (k_total // tk, n_total // tn)

    def body(a_ref, b_ref, o_ref):
        av = a_ref[...]
        part = _dot(av if a_fn is None else a_fn(av), b_ref[...], TN_DIMS)
        o_ref[...] = part.reshape(o_ref.shape)

    if out_shape is None:
        out_shape = _sds((k_total, n_total), F32)
        out_spec = pl.BlockSpec((tk, tn), lambda i, j: (i, j))
    return _call(
        body, (a, b), name=name, grid=grid,
        in_specs=[pl.BlockSpec((m_total, tk), lambda i, j: (0, i)), pl.BlockSpec((m_total, tn), lambda i, j: (0, j))],
        out_specs=out_spec, out_shape=out_shape, sem=("parallel", "parallel"), comms=comms)


def _ada_fwd(c_all, w_shard, b_shard):
    nb, ncol = c_all.shape[0], w_shard.shape[1]
    tn = 512

    def body(c_ref, w_ref, b_ref, o_ref):
        c = c_ref[...]
        o_ref[...] = _dot(c * _sigmoid(c), w_ref[...]) + b_ref[...]

    return pl.pallas_call(
        body, name="ada_fwd", grid=(ncol // tn,),
        in_specs=[pl.BlockSpec((nb, D_MODEL), lambda j: (0, 0)), pl.BlockSpec((D_MODEL, tn), lambda j: (0, j)),
                  pl.BlockSpec((1, tn), lambda j: (0, j))],
        out_specs=pl.BlockSpec((nb, tn), lambda j: (0, j)), out_shape=_sds((nb, ncol), F32),
        compiler_params=_params("parallel"),
    )(c_all, w_shard, b_shard)


def _adamw_math(g, w, m, v):
    m = ADAM_B1 * m + (1.0 - ADAM_B1) * g
    v = ADAM_B2 * v + (1.0 - ADAM_B2) * (g * g)
    m_hat = m / (1.0 - ADAM_B1 ** ADAM_STEP)
    v_hat = v / (1.0 - ADAM_B2 ** ADAM_STEP)
    delta = -ADAM_LR * (m_hat / (jnp.sqrt(v_hat) + ADAM_EPS) + ADAM_WD * w)
    return delta, m, v


def _ada_bwd_adamw(c_all, dmod_cols, w, m, v):
    nb, ncol = dmod_cols.shape
    tn = 256

    def body(c_ref, d_ref, w_ref, m_ref, v_ref, g_ref, dl_ref, nm_ref, nv_ref):
        c = c_ref[...]
        g = _dot(c * _sigmoid(c), d_ref[...], TN_DIMS)
        g_ref[...] = g
        dl_ref[...], nm_ref[...], nv_ref[...] = _adamw_math(g, w_ref[...], m_ref[...], v_ref[...])

    col = pl.BlockSpec((D_MODEL, tn), lambda j: (0, j))
    shp = _sds((D_MODEL, ncol), F32)
    return pl.pallas_call(
        body, name="ada_bwd_adamw", grid=(ncol // tn,),
        in_specs=[pl.BlockSpec((nb, D_MODEL), lambda j: (0, 0)), pl.BlockSpec((nb, tn), lambda j: (0, j)), col, col, col],
        out_specs=[col, col, col, col], out_shape=[shp, shp, shp, shp],
        compiler_params=_params("parallel"),
    )(c_all, dmod_cols, w, m, v)


def _adamw_halves(own, theirs, core, w, m, v, *, axis, name):
    r2, c2 = own.shape
    tr = _row_tile(r2)
    nt = r2 // tr

    def body(core_ref, own_ref, their_ref, w_ref, m_ref, v_ref, g_ref, dl_ref, nm_ref, nv_ref):
        g = jnp.where(pl.program_id(0) == core_ref[0], own_ref[...], their_ref[...])
        g_ref[...] = g
        dl_ref[...], nm_ref[...], nv_ref[...] = _adamw_math(g, w_ref[...], m_ref[...], v_ref[...])

    if axis == 0:
        full = pl.BlockSpec((tr, c2), lambda h, i, core_ref: (h * nt + i, 0))
    else:
        full = pl.BlockSpec((tr, c2), lambda h, i, core_ref: (i, h))
    half = pl.BlockSpec((tr, c2), lambda h, i, core_ref: (i, 0))
    shp = _sds(w.shape, F32)
    return pl.pallas_call(
        body, name=name,
        grid_spec=pltpu.PrefetchScalarGridSpec(num_scalar_prefetch=1, grid=(2, nt), in_specs=[half, half, full, full, full],
                                               out_specs=[full] * 4),
        out_shape=[shp] * 4, compiler_params=_params("parallel", "parallel"),
    )(core, own, theirs, w, m, v)


def _tok_spec(tm, width=D_MODEL):
    return pl.BlockSpec((None, tm, width), lambda b, i: (b, i, 0))


def _row_spec(width=D_MODEL):
    return pl.BlockSpec((None, 1, width), lambda b, i: (b, 0, 0))


def _vec_spec(width=D_MODEL):
    return pl.BlockSpec((1, width), lambda b, i: (0, 0))


class _RowsOf:
    def __init__(self, ref, first, count):
        self.ref, self.rows = ref, slice(first, first + count)

    def __getitem__(self, idx):
        return self.ref[self.rows, :]

    def __setitem__(self, idx, value):
        self.ref[self.rows, :] = value


def _mm_rows(a, b, *, name, tm, extra, extra_specs, out_specs, out_shape, epi, pro=None, trans_b=False, b_chunks=1, comms=(),
             parts=1, zero_per_seq=(), zero_once=(), vmem_limit=None):
    bsz, seq, k_total = a.shape
    kc = k_total // b_chunks
    dims = NT_DIMS if trans_b else None
    rows = tm // parts

    def body(*refs):
        a_ref, b_ref = refs[0], refs[1]
        ex, outs = refs[2:2 + len(extra)], refs[2 + len(extra):]
        if zero_per_seq:
            @pl.when(pl.program_id(1) == 0)
            def _():
                for k in zero_per_seq:
                    outs[k][...] = jnp.zeros_like(outs[k])
        if zero_once:
            @pl.when(jnp.logical_and(pl.program_id(0) == 0, pl.program_id(1) == 0))
            def _():
                for k in zero_once:
                    outs[k][...] = jnp.zeros_like(outs[k])

        def part_of(ref, p):
            tiled = len(ref.shape) == 2 and ref.shape[0] == tm
            return _RowsOf(ref, p * rows, rows) if tiled and parts > 1 else ref

        accs = []
        for p in range(parts):
            a_p, ex_p, outs_p = part_of(a_ref, p), [part_of(r, p) for r in ex], [part_of(r, p) for r in outs]
            if b_chunks == 1:
                accs.append(_dot(a_p[...] if pro is None else pro(a_p, ex_p, outs_p), b_ref[...], dims))
            else:
                acc = _dot(a_p[...][:, 0:kc], b_ref[0], NT_DIMS)
                for k in range(1, b_chunks):
                    acc = acc + _dot(a_p[...][:, k * kc:(k + 1) * kc], b_ref[k], NT_DIMS)
                accs.append(acc)
        for p in range(parts):
            epi(accs[p], [part_of(r, p) for r in ex], [part_of(r, p) for r in outs])

    b_spec = pl.BlockSpec(b.shape, lambda bb, i: (0,) * b.ndim)
    return _call(
        body, (a, b, *extra), name=name, grid=(bsz, seq // tm), in_specs=[_tok_spec(tm, k_total), b_spec, *extra_specs],
        out_specs=out_specs, out_shape=out_shape, sem=("arbitrary", "arbitrary"), comms=comms, vmem_limit=vmem_limit)


def _in_proj_fused(x, w, sc, sh, w_in_t, tables, comms=()):
    tm = 512
    bsz, seq, _ = x.shape
    half = ROPE_DIM // 2
    heads_per_slab = LANE // ATT_HEAD_DIM

    def pro(x_ref, ex, outs):
        y, _, _ = _rms_fwd(x_ref[...], ex[0][...])
        h = (y * (1.0 + ex[1][...]) + ex[2][...]).astype(BF16)
        outs[0][...] = h
        return h

    def epi(acc, ex, outs):
        c, u, d = ex[3][...], ex[4][...], ex[5][...]
        _, rec_ref, q_ref, k_ref, v_ref = outs
        for k in range(HG_SLABS):
            rec_ref[k] = acc[:, ATT_COLS + k * HG_WIDTH:ATT_COLS + (k + 1) * HG_WIDTH]

        def rope(z):
            return (z * c + pltpu.roll(z, half, 1) * u + pltpu.roll(z, LANE - half, 1) * d).astype(BF16)

        for s in range(ATT_WIDTH // LANE):
            slab = rope(acc[:, s * LANE:(s + 1) * LANE])
            for part in range(heads_per_slab):
                g, hh = divmod(s * heads_per_slab + part, ATT_GROUP)
                piece = slab[:, part * ATT_HEAD_DIM:(part + 1) * ATT_HEAD_DIM]
                for blk in range(tm // WINDOW):
                    q_ref[blk, g, hh * WINDOW:(hh + 1) * WINDOW, :] = piece[blk * WINDOW:(blk + 1) * WINDOW]
        rk = rope(acc[:, ATT_WIDTH:ATT_WIDTH + LANE])
        vv = acc[:, ATT_WIDTH + LANE:ATT_COLS].astype(BF16)
        for g in range(ATT_KV_HEADS):
            k_ref[g] = rk[:, g * ATT_HEAD_DIM:(g + 1) * ATT_HEAD_DIM]
            v_ref[g] = vv[:, g * ATT_HEAD_DIM:(g + 1) * ATT_HEAD_DIM]

    tab = pl.BlockSpec((tm, LANE), lambda b, i: (i, 0))
    kv_spec = pl.BlockSpec((None, ATT_KV_HEADS, tm, ATT_HEAD_DIM), lambda b, i: (b, 0, i, 0))
    kv_shape = _sds((bsz, ATT_KV_HEADS, seq, ATT_HEAD_DIM), BF16)
    q_spec = pl.BlockSpec((None, tm // WINDOW, ATT_KV_HEADS, GROUP_ROWS, ATT_HEAD_DIM), lambda b, i: (b, i, 0, 0, 0))
    return _mm_rows(x, w_in_t, name="in_proj", tm=tm, extra=(w, sc, sh, *tables),
                    extra_specs=[_vec_spec(), _row_spec(), _row_spec(), tab, tab, tab],
                    out_specs=[_tok_spec(tm), pl.BlockSpec((None, HG_SLABS, tm, HG_WIDTH), lambda b, i: (b, 0, i, 0)), q_spec,
                               kv_spec, kv_spec],
                    out_shape=[_sds(x.shape, BF16), _sds((bsz, HG_SLABS, seq, HG_WIDTH), F32),
                               _sds((bsz, seq // WINDOW, ATT_KV_HEADS, GROUP_ROWS, ATT_HEAD_DIM), BF16), kv_shape, kv_shape],
                    pro=pro, epi=epi, trans_b=True, comms=comms)


def _rope_tables(seq):
    half = ROPE_DIM // 2
    inv_freq = ROPE_THETA ** (-jnp.arange(0, ROPE_DIM, 2, dtype=F32) / ROPE_DIM)
    ang = jnp.arange(seq, dtype=F32)[:, None] * inv_freq[None, :]
    cos, sin = jnp.cos(ang), jnp.sin(ang)
    rest = ATT_HEAD_DIM - ROPE_DIM
    ones, zeros, zh = jnp.ones((seq, rest), F32), jnp.zeros((seq, rest), F32), jnp.zeros((seq, half), F32)
    reps = LANE // ATT_HEAD_DIM
    t_cos = jnp.tile(jnp.concatenate([cos, cos, ones], axis=1), (1, reps))
    t_up = jnp.tile(jnp.concatenate([zh, sin, zeros], axis=1), (1, reps))
    t_dn = jnp.tile(jnp.concatenate([-sin, zh, zeros], axis=1), (1, reps))
    return t_cos, t_up, t_dn


GROUP_ROWS = ATT_GROUP * WINDOW


ATT_BPS = 2


MASKED = -1e30


def _band_biases():
    row = jnp.arange(GROUP_ROWS)[:, None] % WINDOW
    col = jnp.arange(2 * WINDOW)[None, :]
    own = jnp.logical_and(col >= WINDOW, col - WINDOW <= row)
    before = jnp.logical_and(col < WINDOW, col > row)
    return (jnp.where(jnp.logical_or(own, before), 0.0, MASKED).astype(F32), jnp.where(own, 0.0, MASKED).astype(F32))


def _band_bias(full_ref, first_ref, has_prev):
    return full_ref[...] if has_prev is True else jnp.where(has_prev, full_ref[...], first_ref[...])


def _sink_column(sink_ref, g):
    head = lax.broadcasted_iota(jnp.int32, (GROUP_ROWS, 1), 0) // WINDOW
    col = jnp.full((GROUP_ROWS, 1), sink_ref[0, g * ATT_GROUP], F32)
    for hh in range(1, ATT_GROUP):
        col = jnp.where(head == hh, sink_ref[0, g * ATT_GROUP + hh], col)
    return col


def _sink_row(sink_ref, g):
    return jnp.concatenate([jnp.full((1, WINDOW), sink_ref[0, g * ATT_GROUP + hh], F32) for hh in range(ATT_GROUP)], axis=1)


def _bias_spec(transposed=False):
    shape = (2 * WINDOW, GROUP_ROWS) if transposed else (GROUP_ROWS, 2 * WINDOW)
    return pl.BlockSpec(shape, lambda b, i: (0, 0))


def _attn_specs():
    q_spec = pl.BlockSpec((None, ATT_BPS, ATT_KV_HEADS, GROUP_ROWS, ATT_HEAD_DIM), lambda b, i: (b, i, 0, 0, 0))
    kv_cur = pl.BlockSpec((None, ATT_KV_HEADS, ATT_BPS * WINDOW, ATT_HEAD_DIM), lambda b, i: (b, 0, i, 0))
    kv_prev = pl.BlockSpec((None, ATT_KV_HEADS, WINDOW, ATT_HEAD_DIM), lambda b, i: (b, 0, jnp.maximum(ATT_BPS * i - 1, 0), 0))
    return q_spec, kv_cur, kv_prev


def _band(prev_ref, cur_ref, g, blk):
    own = cur_ref[g, blk * WINDOW:(blk + 1) * WINDOW]
    before = prev_ref[g] if blk == 0 else cur_ref[g, (blk - 1) * WINDOW:blk * WINDOW]
    return jnp.concatenate([before, own], axis=0)


def _attn_fwd(qh, kh, vh, sinks, w_norm, biases, comms=()):
    bsz, nblk = qh.shape[0], qh.shape[1]
    seq = nblk * WINDOW
    rows = ATT_BPS * WINDOW

    def body(sink_ref, q_ref, kc_ref, kp_ref, vc_ref, vp_ref, w_ref, full_ref, first_ref, raw_ref, an_ref, l_ref):
        l_ref[...] = jnp.zeros_like(l_ref)
        def block(blk):
            bias = _band_bias(full_ref, first_ref, True if blk else pl.program_id(1) > 0)
            groups = range(ATT_KV_HEADS)
            keys, vals = [_band(kp_ref, kc_ref, g, blk) for g in groups], [_band(vp_ref, vc_ref, g, blk) for g in groups]
            sink = [_sink_column(sink_ref, g) for g in groups]
            s = [_dot(q_ref[blk, g], keys[g], NT_DIMS) * ATT_SCALE + bias for g in groups]
            yield
            m = [jnp.maximum(jnp.max(s[g], axis=-1, keepdims=True), sink[g]) for g in groups]
            p = [jnp.exp(s[g] - m[g]) for g in groups]
            den = [jnp.sum(p[g], axis=-1, keepdims=True) + jnp.exp(sink[g] - m[g]) for g in groups]
            yield
            o = [_dot(p[g] / den[g], vals[g]) for g in groups]
            yield
            lse = [m[g] + jnp.log(den[g]) for g in groups]
            tok = slice(blk * WINDOW, (blk + 1) * WINDOW)
            for g in groups:
                for hh in range(ATT_GROUP):
                    h = g * ATT_GROUP + hh
                    raw_ref[tok, h * ATT_HEAD_DIM:(h + 1) * ATT_HEAD_DIM] = o[g][hh * WINDOW:(hh + 1) * WINDOW]
                    l_ref[tok, h:h + 1] = lse[g][hh * WINDOW:(hh + 1) * WINDOW]

        _in_step(block(blk) for blk in range(ATT_BPS))
        y, _, _ = _rms_fwd(raw_ref[...], w_ref[...])
        an_ref[...] = y.astype(BF16)

    cur = lambda width: pl.BlockSpec((None, rows, width), lambda b, i: (b, i, 0))
    q_spec, kv_cur, kv_prev = _attn_specs()
    return _call(
        body, (sinks, qh, kh, kh, vh, vh, w_norm, *biases), name="attn_fwd", grid=(bsz, nblk // ATT_BPS),
        in_specs=[pl.BlockSpec(memory_space=pltpu.SMEM), q_spec, kv_cur, kv_prev, kv_cur, kv_prev, _vec_spec(ATT_WIDTH),
                  _bias_spec(), _bias_spec()],
        out_specs=[cur(ATT_WIDTH), cur(ATT_WIDTH), cur(LANE)],
        out_shape=[_sds((bsz, seq, ATT_WIDTH), F32), _sds((bsz, seq, MIX_WIDTH), BF16), _sds((bsz, seq, LANE), F32)],
        sem=("parallel", "parallel"), comms=comms)


HG_Q0 = ATT_COLS // LANE
HG_F0 = HG_Q0 + HG_HEADS
HG_I0 = HG_F0 + HG_HEADS
HG_G0 = HG_I0 + HG_HEADS
HG_SLABS = 4
HG_Q, HG_F, HG_I, HG_G = range(HG_SLABS)
HG_TOK = 256
HG_NCH = HG_TOK // HG_CHUNK
HG_HPS = 2


def _block_masks():
    row = jnp.arange(HG_TOK)[:, None]
    col = jnp.arange(HG_TOK)[None, :]
    same = (row // HG_CHUNK) == (col // HG_CHUNK)
    return jnp.logical_and(same, col <= row).astype(F32), jnp.logical_and(same, col >= row).astype(F32)


def _row_in_chunk():
    return lax.broadcasted_iota(jnp.int32, (HG_TOK, LANE), 0) % HG_CHUNK


def _chunk_cumsum(x, reverse=False):
    ric = _row_in_chunk()
    shift = 1
    while shift < HG_CHUNK:
        if reverse:
            x = x + jnp.where(ric < HG_CHUNK - shift, pltpu.roll(x, HG_TOK - shift, 0), 0.0)
        else:
            x = x + jnp.where(ric >= shift, pltpu.roll(x, shift, 0), 0.0)
        shift *= 2
    return x


def _chunk_rows(rows):
    stacked = jnp.concatenate([r[None] for r in rows], axis=0)
    return jnp.broadcast_to(stacked, (HG_NCH, HG_CHUNK, LANE)).reshape(HG_TOK, LANE)


def _chunk_slices(x):
    return [x[j * HG_CHUNK:(j + 1) * HG_CHUNK] for j in range(HG_NCH)]


def _in_step(stages):
    stages = list(stages)
    while stages:
        stages = [g for g in stages if next(g, stages) is not stages]


def _hgrn_common(tbl, hf, hq):
    lb = _sigmoid(tbl[1:2] - tbl[0:1])
    sig = _sigmoid(hf)
    f = lb + (1.0 - lb) * sig
    sq = _sigmoid(hq)
    q, k = hq * sq, 1.0 - f
    b = _chunk_cumsum(jnp.log(f))
    last = [b[(j + 1) * HG_CHUNK - 1:(j + 1) * HG_CHUNK] for j in range(HG_NCH)]
    bl = _chunk_rows(last)
    e_b, e_nb, e_rem = jnp.exp(b), jnp.exp(-b), jnp.exp(bl - b)
    e_last = [jnp.exp(r) for r in last]
    return dict(lb=lb, sig=sig, f=f, sq=sq, q=q, k=k, e_b=e_b, e_nb=e_nb, e_rem=e_rem, e_last=e_last,
                qd=q * e_b, kd=k * e_nb, ku=k * e_rem)


def _hgrn_fwd(proj, lb_table, norm_w, mix_in, masks, comms=()):
    bsz, _, seq, _ = proj.shape
    nstep = seq // HG_TOK

    def body(tbl_ref, nw_ref, p_ref, mix_ref, lower_ref, o_ref, rec_ref, st_ref, s_scr):
        @pl.when(pl.program_id(2) == 0)
        def _():
            s_scr[...] = jnp.zeros_like(s_scr)

        lower = lower_ref[...]

        def head(hp):
            ls = slice(hp * LANE, (hp + 1) * LANE)
            v, hg = p_ref[HG_I, :, ls], p_ref[HG_G, :, ls]
            t = _hgrn_common(tbl_ref[:, ls], p_ref[HG_F, :, ls], p_ref[HG_Q, :, ls])
            yield
            a = _dot(t["qd"], t["kd"], NT_DIMS) * lower
            o_intra = _dot(a, v)
            yield
            v_c, ku_c, qd_c = [_chunk_slices(z.astype(BF16)) for z in (v, t["ku"], t["qd"])]
            updates = [_dot(v_c[j], ku_c[j], TN_DIMS) for j in range(HG_NCH)]
            yield
            st = s_scr[hp]
            states = []
            for j in range(HG_NCH):
                states.append(st)
                st = st * t["e_last"][j] + updates[j]
            s_scr[hp] = st
            yield
            o = o_intra + jnp.concatenate([_dot(qd_c[j], states[j], NT_DIMS) for j in range(HG_NCH)], axis=0)
            yield
            st_ref[hp, 0] = states[0]
            o_ref[:, ls] = o
            y, _, _ = _rms_fwd(o, nw_ref[...])
            rec_ref[:, ls] = (y * (hg * _sigmoid(hg))).astype(BF16)

        _in_step(head(hp) for hp in range(HG_HPS))

    width = HG_HPS * LANE
    head_out = pl.BlockSpec((None, HG_TOK, width), lambda b, h, t: (b, t, h))
    mix_out = pl.BlockSpec((None, HG_TOK, width), lambda b, h, t: (b, t, ATT_WIDTH // width + h))
    return _call(
        body, (lb_table, norm_w, proj, mix_in, masks[0]), name="hgrn_fwd", grid=(bsz, HG_HEADS // HG_HPS, nstep),
        in_specs=[pl.BlockSpec((2, width), lambda b, h, t: (0, h)), pl.BlockSpec((1, LANE), lambda b, h, t: (0, 0)),
                  pl.BlockSpec((None, HG_SLABS, HG_TOK, width), lambda b, h, t: (b, 0, t, h)), pl.BlockSpec(memory_space=pl.ANY),
                  pl.BlockSpec((HG_TOK, HG_TOK), lambda b, h, t: (0, 0))],
        out_specs=[head_out, mix_out,
                   pl.BlockSpec((None, HG_HPS, 1, LANE, LANE), lambda b, h, t: (b, h, t, 0, 0))],
        out_shape=[_sds((bsz, seq, HG_WIDTH), F32), _sds(mix_in.shape, BF16),
                   _sds((bsz, HG_HEADS, nstep, LANE, LANE), F32)],
        scratch_shapes=[pltpu.VMEM((HG_HPS, LANE, LANE), F32)],
        sem=("parallel", "parallel", "arbitrary"), comms=comms, aliases={3: 1})


def _out_proj_fused(cat, w_out, x, post_w, g1, pre_w, sc2, sh2):
    tm = 512

    def epi(mix, ex, outs):
        x_ref, pw_ref, g1_ref, w2_ref, sc_ref, sh_ref = ex
        outs[0][...] = mix
        n1, _, _ = _rms_fwd(mix, pw_ref[...])
        x1 = x_ref[...] + g1_ref[...] * n1
        outs[1][...] = x1
        y2, _, _ = _rms_fwd(x1, w2_ref[...])
        outs[2][...] = (y2 * (1.0 + sc_ref[...]) + sh_ref[...]).astype(BF16)

    return _mm_rows(cat, w_out, name="out_proj", tm=tm, extra=(x, post_w, g1, pre_w, sc2, sh2),
                    extra_specs=[_tok_spec(tm), _vec_spec(), _row_spec(), _vec_spec(), _row_spec(), _row_spec()],
                    out_specs=[_tok_spec(tm), _tok_spec(tm), _tok_spec(tm)],
                    out_shape=[_sds(x.shape, F32), _sds(x.shape, F32), _sds(x.shape, BF16)], epi=epi)


def _acc_out(ref, first, value):
    @pl.when(first)
    def _():
        ref[...] = value

    @pl.when(jnp.logical_not(first))
    def _():
        ref[...] += value


def _down_proj_fused(r, w_down, x1, post_w, g2, target):
    tm = 512
    bsz = x1.shape[0]

    def pro(r_ref, ex, outs):
        rv = r_ref[...]
        return rv * rv

    def epi(down, ex, outs):
        x1_ref, w_ref, g2_ref, t_ref = ex
        loss_ref, dy_ref, dd_ref, dg2_ref, dw_ref = outs
        w, g2v = w_ref[...], g2_ref[...]
        gain = g2v * w
        dh, rstd = _rms_hat(down)
        err = x1_ref[...] + dh * gain - t_ref[...]
        part = (0.5 / D_MODEL) * jnp.sum(jnp.sum(err * err, axis=-1, keepdims=True), axis=0, keepdims=True)
        loss_ref[...] += jnp.broadcast_to(part, (1, LANE))
        dy = err * (1.0 / D_MODEL)
        dy_ref[...] = dy
        dd, per_col = _rms_bwd_gain(dy, gain, dh, rstd)
        dd_ref[...] = dd.astype(BF16)
        dg2_ref[...] += per_col * w
        dw_ref[...] += per_col * g2v

    return _mm_rows(r, w_down, name="down_proj", tm=tm, extra=(x1, post_w, g2, target),
                    extra_specs=[_tok_spec(tm), _vec_spec(), _row_spec(), _tok_spec(tm)],
                    out_specs=[_vec_spec(LANE), _tok_spec(tm), _tok_spec(tm), _row_spec(), _vec_spec()],
                    out_shape=[_sds((1, LANE), F32), _sds(x1.shape, F32), _sds(x1.shape, BF16), _sds((bsz, 1, D_MODEL), F32),
                               _sds((1, D_MODEL), F32)], pro=pro, epi=epi, parts=2, zero_per_seq=(3,), zero_once=(0, 4),
                    vmem_limit=VMEM_LIMIT_BIG)


def _up_bwd_fused(dpre, w_up4, dy, x1, mix, pre_w, sc2, post_w, g1, comms=()):
    tm = 512
    bsz = x1.shape[0]

    def epi(dh2v, ex, outs):
        dy_ref, x1_ref, mix_ref, w2_ref, sc_ref, pw_ref, g1_ref = ex
        dx1_ref, dmix_ref, dsc_ref, dsh_ref, dg1_ref, dw2_ref, dpw_ref = outs
        w2, pw, g1v = w2_ref[...], pw_ref[...], g1_ref[...]
        mod2 = 1.0 + sc_ref[...]
        xh2, rstd2 = _rms_hat(x1_ref[...])
        dsh_ref[...] += _colsum(dh2v)
        dx1n, per_col2 = _rms_bwd_gain(dh2v, mod2 * w2, xh2, rstd2)
        dsc_ref[...] += per_col2 * w2
        dw2_ref[...] += per_col2 * mod2
        dx1 = dy_ref[...] + dx1n
        dx1_ref[...] = dx1
        mh, rstd1 = _rms_hat(mix_ref[...])
        dmix, per_col1 = _rms_bwd_gain(dx1, g1v * pw, mh, rstd1)
        dmix_ref[...] = dmix.astype(BF16)
        dg1_ref[...] += per_col1 * pw
        dpw_ref[...] += per_col1 * g1v

    row_shape = _sds((bsz, 1, D_MODEL), F32)
    vec_shape = _sds((1, D_MODEL), F32)
    return _mm_rows(dpre, w_up4, name="up_bwd", tm=tm, extra=(dy, x1, mix, pre_w, sc2, post_w, g1),
                    extra_specs=[_tok_spec(tm), _tok_spec(tm), _tok_spec(tm), _vec_spec(), _row_spec(), _vec_spec(), _row_spec()],
                    out_specs=[_tok_spec(tm), _tok_spec(tm), _row_spec(), _row_spec(), _row_spec(), _vec_spec(), _vec_spec()],
                    out_shape=[_sds(x1.shape, F32), _sds(x1.shape, BF16), row_shape, row_shape, row_shape, vec_shape, vec_shape],
                    epi=epi, b_chunks=w_up4.shape[0], comms=comms, parts=2, zero_per_seq=(2, 3, 4), zero_once=(5, 6),
                    vmem_limit=VMEM_LIMIT_BIG)


def _norm1_bwd(dh1, dx1, x, pre_w, sc1, tm=512, comms=()):
    bsz, seq, _ = x.shape

    def body(dh_ref, dx1_ref, x_ref, w_ref, sc_ref, gx_ref, dsc_ref, dsh_ref, dw_ref):
        b, i = pl.program_id(0), pl.program_id(1)
        w = w_ref[...]
        dh = dh_ref[...]
        mod = 1.0 + sc_ref[...]
        xh, rstd = _rms_hat(x_ref[...])
        dx, per_col = _rms_bwd_gain(dh, mod * w, xh, rstd)
        _acc_out(dsh_ref, i == 0, _colsum(dh))
        _acc_out(dsc_ref, i == 0, per_col * w)
        _acc_out(dw_ref, jnp.logical_and(b == 0, i == 0), per_col * mod)
        gx_ref[...] = dx1_ref[...] + dx

    row_shape = _sds((bsz, 1, D_MODEL), F32)
    return _call(
        body, (dh1, dx1, x, pre_w, sc1), name="norm1_bwd", grid=(bsz, seq // tm),
        in_specs=[_tok_spec(tm), _tok_spec(tm), _tok_spec(tm), _vec_spec(), _row_spec()],
        out_specs=[_tok_spec(tm), _row_spec(), _row_spec(), _vec_spec()],
        out_shape=[_sds(x.shape, F32), row_shape, row_shape, _sds((1, D_MODEL), F32)],
        sem=("arbitrary", "arbitrary"), comms=comms)


def _hgrn_bwd(dcat, proj, o_raw, states, lb_table, norm_w, masks, comms=()):
    bsz, _, seq, _ = proj.shape
    nstep = seq // HG_TOK
    rec0 = ATT_WIDTH // LANE
    width = HG_HPS * LANE
    slabs = (HG_Q0, HG_F0, HG_I0, HG_G0)
    n_steps = (HG_HEADS // HG_HPS) * bsz * nstep
    assert n_steps >= 2

    def body(tbl_ref, nw_ref, dr_ref, p_ref, o_ref, st_ref, lower_ref, upper_ref,
             dproj_ref, dlb_ref, dnw_ref, ds_scr, grad_buf, grad_sem):
        h, b, t = pl.program_id(0), pl.program_id(1), pl.program_id(2)
        step = (h * bsz + b) * nstep + t
        slot = step % 2
        dq_k, df_k, di_k, dg_k = range(4)

        def grad_copies(of_step):
            hh, bb, tt = of_step // (bsz * nstep), (of_step // nstep) % bsz, of_step % nstep
            rows = pl.ds(pl.multiple_of((nstep - 1 - tt) * HG_TOK, HG_TOK), HG_TOK)
            return [pltpu.make_async_copy(
                grad_buf.at[of_step % 2, k],
                dproj_ref.at[bb, rows, pl.ds(pl.multiple_of(slabs[k] * LANE + hh * width, width), width)],
                grad_sem.at[of_step % 2, k]) for k in range(4)]

        @pl.when(step >= 2)
        def _():
            for cp in grad_copies(step - 2):
                cp.wait()

        @pl.when(t == 0)
        def _():
            ds_scr[...] = jnp.zeros_like(ds_scr)

        lower, upper = lower_ref[...], upper_ref[...]
        dlb_parts, dnw_parts = [None] * HG_HPS, [None] * HG_HPS

        def head(hp):
            ls = slice(hp * LANE, (hp + 1) * LANE)
            hq, v, hg = p_ref[HG_Q, :, ls], p_ref[HG_I, :, ls], p_ref[HG_G, :, ls]
            nw = nw_ref[...]
            c = _hgrn_common(tbl_ref[:, ls], p_ref[HG_F, :, ls], hq)
            qd, kd, ku = c["qd"], c["kd"], c["ku"]
            yield
            y, on, rstd = _rms_fwd(o_ref[:, ls], nw)
            sg = _sigmoid(hg)
            dr = dr_ref[:, ls]
            grad_buf[slot, dg_k, :, ls] = (dr * y * (sg * (1.0 + hg * (1.0 - sg)))).astype(BF16)
            do, dnw_rows = _rms_bwd(dr * (hg * sg), on, rstd, nw)
            yield
            at = _dot(kd, qd, NT_DIMS) * upper
            da = _dot(do, v, NT_DIMS) * lower
            dat = _dot(v, do, NT_DIMS) * upper
            yield
            dv = _dot(at, do)
            dqd = _dot(da, kd)
            dkd = _dot(dat, qd)
            yield
            do_c, qd_c, v_c, ku_c = [_chunk_slices(z.astype(BF16)) for z in (do, qd, v, ku)]
            outer = [_dot(do_c[j], qd_c[j], TN_DIMS) for j in range(HG_NCH)]
            yield
            ds = ds_scr[hp]
            ds_after = [None] * HG_NCH
            for j in reversed(range(HG_NCH)):
                ds_after[j] = ds
                ds = outer[j] + ds * c["e_last"][j]
            ds_scr[hp] = ds
            yield
            updates = [_dot(v_c[j], ku_c[j], TN_DIMS) for j in range(HG_NCH)]
            yield
            states = [st_ref[hp, 0]]
            for j in range(HG_NCH - 1):
                states.append(states[j] * c["e_last"][j] + updates[j])
            dv = dv + jnp.concatenate([_dot(ku_c[j], ds_after[j], NT_DIMS) for j in range(HG_NCH)], axis=0)
            dqd = dqd + jnp.concatenate([_dot(do_c[j], states[j]) for j in range(HG_NCH)], axis=0)
            dku = jnp.concatenate([_dot(v_c[j], ds_after[j]) for j in range(HG_NCH)], axis=0)
            yield
            dku_ku = dku * ku
            dbl = [_colsum(states[j] * ds_after[j]) * c["e_last"][j] + _colsum(dku_ku[j * HG_CHUNK:(j + 1) * HG_CHUNK])
                   for j in range(HG_NCH)]
            dk = dkd * c["e_nb"] + dku * c["e_rem"]
            db = dqd * qd - dkd * kd - dku_ku + jnp.where(_row_in_chunk() == HG_CHUNK - 1, _chunk_rows(dbl), 0.0)
            dfv = _chunk_cumsum(db, reverse=True) / c["f"] - dk
            sig, sq = c["sig"], c["sq"]
            grad_buf[slot, df_k, :, ls] = (dfv * (1.0 - c["lb"]) * sig * (1.0 - sig)).astype(BF16)
            grad_buf[slot, dq_k, :, ls] = (dqd * c["e_b"] * (sq * (1.0 + hq * (1.0 - sq)))).astype(BF16)
            grad_buf[slot, di_k, :, ls] = dv.astype(BF16)
            dlb_parts[hp] = _colsum(dfv * (1.0 - sig))
            dnw_parts[hp] = _colsum(dnw_rows)

        _in_step(head(hp) for hp in range(HG_HPS))
        _acc_out(dlb_ref, jnp.logical_and(b == 0, t == 0), jnp.concatenate(dlb_parts, axis=1))
        _acc_out(dnw_ref, jnp.logical_and(h == 0, jnp.logical_and(b == 0, t == 0)), sum(dnw_parts[1:], dnw_parts[0]))
        for cp in grad_copies(step):
            cp.start()

        @pl.when(step == n_steps - 1)
        def _():
            for cp in grad_copies(step - 1) + grad_copies(step):
                cp.wait()

    rev = lambda t: nstep - 1 - t
    slab = lambda first: pl.BlockSpec((None, HG_TOK, width), lambda h, b, t: (b, rev(t), first // HG_HPS + h))
    head = pl.BlockSpec((None, HG_TOK, width), lambda h, b, t: (b, rev(t), h))
    return _call(
        body, (lb_table, norm_w, dcat, proj, o_raw, states, *masks), name="hgrn_bwd",
        grid=(HG_HEADS // HG_HPS, bsz, nstep),
        in_specs=[pl.BlockSpec((2, width), lambda h, b, t: (0, h)), pl.BlockSpec((1, LANE), lambda h, b, t: (0, 0)),
                  slab(rec0), pl.BlockSpec((None, HG_SLABS, HG_TOK, width), lambda h, b, t: (b, 0, rev(t), h)), head,
                  pl.BlockSpec((None, HG_HPS, 1, LANE, LANE), lambda h, b, t: (b, h, rev(t), 0, 0)),
                  pl.BlockSpec((HG_TOK, HG_TOK), lambda h, b, t: (0, 0)), pl.BlockSpec((HG_TOK, HG_TOK), lambda h, b, t: (0, 0))],
        out_specs=[pl.BlockSpec(memory_space=pl.ANY), pl.BlockSpec((1, width), lambda h, b, t: (0, h)),
                   pl.BlockSpec((1, LANE), lambda h, b, t: (0, 0))],
        out_shape=[_sds((bsz, seq, IN_COLS), BF16), _sds((1, HG_WIDTH), F32), _sds((1, LANE), F32)],
        scratch_shapes=[pltpu.VMEM((HG_HPS, LANE, LANE), F32), pltpu.VMEM((2, 4, HG_TOK, width), BF16),
                        pltpu.SemaphoreType.DMA((2, 4))],
        sem=("arbitrary", "arbitrary", "arbitrary"), comms=comms)


def _attn_bwd(dcat, raw, w_norm, qh, kh, vh, lse, sinks, tables, biases, dproj, comms=()):
    bsz, nblk = qh.shape[0], qh.shape[1]
    seq = nblk * WINDOW
    nstep = nblk // ATT_BPS
    half = ROPE_DIM // 2

    def body(sink_ref, da_ref, raw_ref, w_ref, q_ref, kc_ref, kp_ref, vc_ref, vp_ref, l_ref, c_ref, u_ref, d_ref,
             full_ref, first_ref, dproj_ref, o_ref, dw_ref, dsink_ref, carry_k, carry_v):
        b, i = pl.program_id(0), pl.program_id(1)
        first = jnp.logical_and(b == 0, i == 0)

        @pl.when(i == 0)
        def _():
            carry_k[...] = jnp.zeros_like(carry_k)
            carry_v[...] = jnp.zeros_like(carry_v)

        w = w_ref[...]
        _, on, rstd = _rms_fwd(raw_ref[...], w)
        do_step, dw_rows = _rms_bwd(da_ref[...], on, rstd, w)
        _acc_out(dw_ref, first, _colsum(dw_rows))
        lane8 = lax.broadcasted_iota(jnp.int32, (1, ATT_Q_HEADS), 1)
        dsink = jnp.zeros((1, ATT_Q_HEADS), F32)
        head_cols = jnp.where(lax.broadcasted_iota(jnp.int32, (2 * ATT_Q_HEADS, ATT_WIDTH), 1) // ATT_HEAD_DIM
                              == lax.broadcasted_iota(jnp.int32, (2 * ATT_Q_HEADS, ATT_WIDTH), 0), 1.0, 0.0)
        from_next_k, from_next_v = carry_k[...], carry_v[...]
        for blk in reversed(range(ATT_BPS)):
            tok = slice(blk * WINDOW, (blk + 1) * WINDOW)
            bias = _band_bias(full_ref, first_ref, True if blk else i < nstep - 1)
            do_all = do_step[tok]
            c, u, d = c_ref[tok, :], u_ref[tok, :], d_ref[tok, :]
            lse_t = l_ref[tok, :].T
            prod = do_all * raw_ref[tok, :]
            prod_hi = prod.astype(BF16)
            prod_lo = prod - prod_hi.astype(F32)
            dsum_t = _dot(head_cols, prod_hi, NT_DIMS) + _dot(head_cols, prod_lo, NT_DIMS)

            def unrope(g):
                return (g * c + pltpu.roll(g * u, LANE - half, 1) + pltpu.roll(g * d, half, 1)).astype(BF16)

            groups = range(ATT_KV_HEADS)
            group_row = lambda z, g: jnp.concatenate(
                [z[g * ATT_GROUP + hh:g * ATT_GROUP + hh + 1, :] for hh in range(ATT_GROUP)], axis=1)
            q = [q_ref[blk, g] for g in groups]
            keys, vals = [_band(kp_ref, kc_ref, g, blk) for g in groups], [_band(vp_ref, vc_ref, g, blk) for g in groups]
            do_g = [jnp.concatenate([do_all[:, (g * ATT_GROUP + hh) * ATT_HEAD_DIM:(g * ATT_GROUP + hh + 1) * ATT_HEAD_DIM]
                                     for hh in range(ATT_GROUP)], axis=0) for g in groups]
            dsum, lse_g = [group_row(dsum_t, g) for g in groups], [group_row(lse_t, g) for g in groups]
            s_t = [_dot(keys[g], q[g], NT_DIMS) for g in groups]
            dp_t = [_dot(vals[g], do_g[g], NT_DIMS) for g in groups]
            p_t = [jnp.exp(s_t[g] * ATT_SCALE + bias - lse_g[g]) for g in groups]
            ds_t = [p_t[g] * (dp_t[g] - dsum[g]) * ATT_SCALE for g in groups]
            dq_g = [_dot(ds_t[g], keys[g], TN_DIMS) for g in groups]
            dk_g = [_dot(ds_t[g], q[g]) for g in groups]
            dv_g = [_dot(p_t[g], do_g[g]) for g in groups]
            for g in groups:
                sink_part = jnp.exp(_sink_row(sink_ref, g) - lse_g[g]) * dsum[g]
                for hh in range(ATT_GROUP):
                    head_sum = jnp.sum(sink_part[:, hh * WINDOW:(hh + 1) * WINDOW], axis=1, keepdims=True)
                    dsink = dsink - jnp.where(lane8 == g * ATT_GROUP + hh, head_sum, 0.0)
            dq_parts = [dq_g[g][hh * WINDOW:(hh + 1) * WINDOW] for g in groups for hh in range(ATT_GROUP)]
            dk_before, dk_own = [z[:WINDOW] for z in dk_g], [z[WINDOW:] for z in dk_g]
            dv_before, dv_own = [z[:WINDOW] for z in dv_g], [z[WINDOW:] for z in dv_g]
            per_slab = LANE // ATT_HEAD_DIM
            for s in range(ATT_WIDTH // LANE):
                slab = jnp.concatenate(dq_parts[s * per_slab:(s + 1) * per_slab], axis=1)
                o_ref[tok, s * LANE:(s + 1) * LANE] = unrope(slab)
            o_ref[tok, ATT_WIDTH:ATT_WIDTH + LANE] = unrope(jnp.concatenate(dk_own, axis=1) + from_next_k)
            o_ref[tok, ATT_WIDTH + LANE:ATT_COLS] = (jnp.concatenate(dv_own, axis=1) + from_next_v).astype(BF16)
            from_next_k, from_next_v = jnp.concatenate(dk_before, axis=1), jnp.concatenate(dv_before, axis=1)
        carry_k[...] = from_next_k
        carry_v[...] = from_next_v
        _acc_out(dsink_ref, first, dsink)

    rows = ATT_BPS * WINDOW
    rev = lambda i: nstep - 1 - i
    cur = lambda width: pl.BlockSpec((None, rows, width), lambda b, i: (b, rev(i), 0))
    q_spec = pl.BlockSpec((None, ATT_BPS, ATT_KV_HEADS, GROUP_ROWS, ATT_HEAD_DIM), lambda b, i: (b, rev(i), 0, 0, 0))
    kv_cur = pl.BlockSpec((None, ATT_KV_HEADS, rows, ATT_HEAD_DIM), lambda b, i: (b, 0, rev(i), 0))
    kv_prev = pl.BlockSpec((None, ATT_KV_HEADS, WINDOW, ATT_HEAD_DIM), lambda b, i: (b, 0, jnp.maximum(ATT_BPS * rev(i) - 1, 0), 0))
    tab = pl.BlockSpec((rows, LANE), lambda b, i: (rev(i), 0))
    return _call(
        body, (sinks, dcat, raw, w_norm, qh, kh, kh, vh, vh, lse, *tables, *biases, dproj), name="attn_bwd", grid=(bsz, nstep),
        in_specs=[pl.BlockSpec(memory_space=pltpu.SMEM), cur(ATT_WIDTH), cur(ATT_WIDTH), _vec_spec(ATT_WIDTH), q_spec,
                  kv_cur, kv_prev, kv_cur, kv_prev, cur(LANE), tab, tab, tab, _bias_spec(True), _bias_spec(True),
                  pl.BlockSpec(memory_space=pl.ANY)],
        out_specs=[cur(ATT_COLS), _vec_spec(ATT_WIDTH), _vec_spec(ATT_Q_HEADS)],
        out_shape=[_sds(dproj.shape, BF16), _sds((1, ATT_WIDTH), F32), _sds((1, ATT_Q_HEADS), F32)],
        scratch_shapes=[pltpu.VMEM((WINDOW, LANE), F32), pltpu.VMEM((WINDOW, LANE), F32)],
        sem=("arbitrary", "arbitrary"), comms=comms, aliases={15: 0})


def _other_chips(x, y):
    return [(1 - x, y), (x, 1 - y), (1 - x, 1 - y)]


def _sem_pair(n):
    return [pltpu.SemaphoreType.DMA((n,)), pltpu.SemaphoreType.DMA((n,))]


def _plan_pair_forward(bufs):
    n = len(bufs)

    def copies(outs, sems):
        x, y, c = _mesh_pos()
        sends, lands = [], []
        for a in range(n):
            for j, chip in enumerate(_other_chips(x, y)):
                k = 3 * a + j
                slot = outs[a].at[4 * chip[0] + 2 * chip[1] + c]
                sends.append(pltpu.make_async_remote_copy(
                    src_ref=slot, dst_ref=slot, send_sem=sems[0].at[k], recv_sem=sems[1].at[k],
                    device_id=(x, y, 1 - c), device_id_type=MESH))
                theirs = outs[a].at[4 * chip[0] + 2 * chip[1] + 1 - c]
                lands.append(pltpu.make_async_remote_copy(
                    src_ref=theirs, dst_ref=theirs, send_sem=sems[0].at[k], recv_sem=sems[1].at[k],
                    device_id=(x, y, 1 - c), device_id_type=MESH))
        return sends, lands

    def start(ins, outs, sems):
        for cp in copies(outs, sems)[0]:
            cp.start()

    def finish(ins, outs, sems):
        sends, lands = copies(outs, sems)
        for cp in lands:
            cp.wait_recv()
        for cp in sends:
            cp.wait_send()

    return _Comm(list(bufs), [_sds(b.shape, b.dtype) for b in bufs], _sem_pair(3 * n), start, finish,
                 aliases=[(a, a) for a in range(n)])


def _plan_pair(arrays, other_half):
    n = len(arrays)
    per = N_CHIPS if other_half == "chip_major" else 1

    def copies(ins, outs, sems):
        x, y, c = _mesh_pos()
        out = []
        for a in range(n):
            for k in range(per):
                if other_half == "chip_major":
                    src, dst = ins[a].at[k, 1 - c], outs[a].at[k]
                else:
                    src, dst = (ins[a].at[1 - c] if other_half else ins[a]), outs[a]
                out.append(pltpu.make_async_remote_copy(
                    src_ref=src, dst_ref=dst, send_sem=sems[0].at[per * a + k], recv_sem=sems[1].at[per * a + k],
                    device_id=(x, y, 1 - c), device_id_type=MESH))
        return out

    def start(ins, outs, sems):
        for cp in copies(ins, outs, sems):
            cp.start()

    def finish(ins, outs, sems):
        for cp in copies(ins, outs, sems):
            cp.wait()

    if other_half == "chip_major":
        shapes = [_sds((a.shape[0],) + a.shape[2:], a.dtype) for a in arrays]
    else:
        shapes = [_sds(a.shape[1:] if other_half else a.shape, a.dtype) for a in arrays]
    return _Comm(list(arrays), shapes, _sem_pair(per * n), start, finish)


def _plan_chip_exchange(arrays):
    n = len(arrays)

    def copies(ins, outs, sems):
        x, y, c = _mesh_pos()
        sends, lands = [], []
        for a in range(n):
            for j, chip in enumerate(_other_chips(x, y)):
                k = 3 * a + j
                sends.append(pltpu.make_async_remote_copy(
                    src_ref=ins[a].at[2 * chip[0] + chip[1]], dst_ref=outs[a].at[2 * x + y], send_sem=sems[0].at[k],
                    recv_sem=sems[1].at[k], device_id=(*chip, c), device_id_type=MESH))
                slot = outs[a].at[2 * chip[0] + chip[1]]
                lands.append(pltpu.make_async_remote_copy(
                    src_ref=slot, dst_ref=slot, send_sem=sems[0].at[k], recv_sem=sems[1].at[k],
                    device_id=(*chip, c), device_id_type=MESH))
        return sends, lands

    def start(ins, outs, sems):
        for cp in copies(ins, outs, sems)[0]:
            cp.start()

    def finish(ins, outs, sems):
        sends, lands = copies(ins, outs, sems)
        for cp in lands:
            cp.wait_recv()
        for cp in sends:
            cp.wait_send()

    return _Comm(list(arrays), [_sds(a.shape, a.dtype) for a in arrays], _sem_pair(3 * n), start, finish)


SEM_SPEC = pl.BlockSpec(memory_space=pltpu.SEMAPHORE)
N_OTHER = N_CHIPS - 1


def _exchange_copies(s_ref, land_ref, sems):
    x, y, c = _mesh_pos()
    return [pltpu.make_async_remote_copy(
        src_ref=s_ref.at[2 * chip[0] + chip[1]], dst_ref=land_ref.at[2 * x + y], send_sem=sems[j], recv_sem=sems[N_OTHER + j],
        device_id=(*chip, c), device_id_type=MESH) for j, chip in enumerate(_other_chips(x, y))]


def _exchange_start(s, name):
    def body(s_ref, land_ref, *outs):
        sems, token = outs[:2 * N_OTHER], outs[-1]
        for cp in _exchange_copies(s_ref, land_ref, sems):
            cp.start()
        token[...] = jnp.zeros_like(token)

    hbm = pltpu.HBM(s.shape, s.dtype)
    res = pl.pallas_call(
        body, name=name,
        out_shape=(pltpu.SemaphoreType.DMA(()),) * (2 * N_OTHER) + (hbm, hbm, _sds((SUBLANES, LANE), F32)),
        in_specs=(HBM_SPEC, HBM_SPEC),
        out_specs=(SEM_SPEC,) * (2 * N_OTHER) + (HBM_SPEC, HBM_SPEC, pl.BlockSpec(memory_space=pltpu.VMEM)),
        input_output_aliases={0: 2 * N_OTHER, 1: 2 * N_OTHER + 1},
        compiler_params=pltpu.CompilerParams(has_side_effects=pltpu.SideEffectType.DATAFLOW_SIDE_EFFECTING),
    )(pltpu.with_memory_space_constraint(s, pltpu.HBM), pltpu.with_memory_space_constraint(lax.empty(s.shape, s.dtype), pltpu.HBM))
    return res[:2 * N_OTHER], res[2 * N_OTHER], res[2 * N_OTHER + 1], res[-1]


def _exchange_wait(sems, s_thru, land_thru, afters, name):
    def body(s_ref, land_ref, *rest):
        for cp in _exchange_copies(s_ref, land_ref, rest[:2 * N_OTHER]):
            cp.wait_send()
            cp.wait_recv()

    hbm = pltpu.HBM(s_thru.shape, s_thru.dtype)
    return pl.pallas_call(
        body, name=name, out_shape=(hbm, hbm),
        in_specs=(HBM_SPEC, HBM_SPEC) + (SEM_SPEC,) * (2 * N_OTHER) + (pl.BlockSpec(memory_space=pl.ANY),) * len(afters),
        out_specs=(HBM_SPEC, HBM_SPEC), input_output_aliases={0: 0, 1: 1},
        compiler_params=pltpu.CompilerParams(has_side_effects=pltpu.SideEffectType.DATAFLOW_SIDE_EFFECTING),
    )(s_thru, land_thru, *sems, *afters)


def _gather_copies(block_ref, buf_ref, sems):
    x, y, c = _mesh_pos()
    return [pltpu.make_async_remote_copy(
        src_ref=block_ref, dst_ref=buf_ref.at[4 * x + 2 * y + c], send_sem=sems[j], recv_sem=sems[N_OTHER + j],
        device_id=(*chip, c), device_id_type=MESH) for j, chip in enumerate(_other_chips(x, y))]


def _gather_start(blocks, bufs, afters, name):
    n = len(blocks)
    per = 2 * N_OTHER

    def body(*refs):
        ins, outs = refs[:2 * n], refs[2 * n + len(afters):]
        for a in range(n):
            for cp in _gather_copies(ins[a], ins[n + a], outs[a * per:(a + 1) * per]):
                cp.start()
        outs[-1][...] = jnp.zeros_like(outs[-1])

    hbm = [pltpu.HBM(z.shape, z.dtype) for z in list(blocks) + list(bufs)]
    res = pl.pallas_call(
        body, name=name,
        out_shape=(pltpu.SemaphoreType.DMA(()),) * (n * per) + tuple(hbm) + (_sds((SUBLANES, LANE), F32),),
        in_specs=(HBM_SPEC,) * (2 * n) + (pl.BlockSpec(memory_space=pl.ANY),) * len(afters),
        out_specs=(SEM_SPEC,) * (n * per) + (HBM_SPEC,) * (2 * n) + (pl.BlockSpec(memory_space=pltpu.VMEM),),
        input_output_aliases={k: n * per + k for k in range(2 * n)},
        compiler_params=pltpu.CompilerParams(has_side_effects=pltpu.SideEffectType.DATAFLOW_SIDE_EFFECTING),
    )(*[pltpu.with_memory_space_constraint(z, pltpu.HBM) for z in list(blocks) + list(bufs)], *afters)
    parts = [(res[a * per:(a + 1) * per], res[n * per + a], res[n * per + n + a]) for a in range(n)]
    return parts, res[-1]


def _gather_wait(part, afters, name):
    sems, block, buf = part

    def body(block_ref, buf_ref, *rest):
        for cp in _gather_copies(block_ref, buf_ref, rest[:2 * N_OTHER]):
            cp.wait_send()
            cp.wait_recv()

    return pl.pallas_call(
        body, name=name, out_shape=(pltpu.HBM(block.shape, block.dtype), pltpu.HBM(buf.shape, buf.dtype)),
        in_specs=(HBM_SPEC, HBM_SPEC) + (SEM_SPEC,) * (2 * N_OTHER) + (pl.BlockSpec(memory_space=pl.ANY),) * len(afters),
        out_specs=(HBM_SPEC, HBM_SPEC), input_output_aliases={0: 0, 1: 1},
        compiler_params=pltpu.CompilerParams(has_side_effects=pltpu.SideEffectType.DATAFLOW_SIDE_EFFECTING),
    )(block, buf, *sems, *afters)[1]


def _comm_only(comms, name):
    return _call(lambda: None, (), name=name, grid=(), in_specs=[], out_specs=[], out_shape=[], sem=(), comms=comms)[1]


def _allgather8(arrays, name):
    return _comm_only([_plan_allgather8(arrays)], name)[0]


def _plan_allgather8(arrays):
    n = len(arrays)

    def parts(ins, outs, sems):
        send_sems, recv_sems, local_sems = sems
        x, y, c = _mesh_pos()
        me, sibling = (x, y, c), (x, y, 1 - c)
        chips = _other_chips(x, y)

        def copy(a, k, block, to, src=None):
            dst = outs[a].at[4 * block[0] + 2 * block[1] + block[2]]
            return pltpu.make_async_remote_copy(
                src_ref=dst if src is None else src, dst_ref=dst, send_sem=send_sems.at[7 * a + k],
                recv_sem=recv_sems.at[7 * a + k], device_id=to, device_id_type=MESH)

        mine = [pltpu.make_async_copy(ins[a], outs[a].at[4 * x + 2 * y + c], local_sems.at[a]) for a in range(n)]
        first = []
        for a in range(n):
            first.append(copy(a, 0, me, sibling, src=ins[a]))
            first += [copy(a, 1 + j, me, (*chip, c), src=ins[a]) for j, chip in enumerate(chips)]
        return copy, mine, first, me, sibling, chips, c

    def start(ins, outs, sems):
        _, mine, first, *_ = parts(ins, outs, sems)
        for cp in mine + first:
            cp.start()

    def finish(ins, outs, sems):
        copy, mine, first, me, sibling, chips, c = parts(ins, outs, sems)
        passed = []
        for j, chip in enumerate(chips):
            for a in range(n):
                copy(a, 1 + j, (*chip, c), me).wait_recv()
                fwd = copy(a, 4 + j, (*chip, c), sibling)
                fwd.start()
                passed.append(fwd)
        for a in range(n):
            copy(a, 0, sibling, me).wait_recv()
            for j, chip in enumerate(chips):
                copy(a, 4 + j, (*chip, 1 - c), me).wait_recv()
        for cp in first + passed:
            cp.wait_send()
        for cp in mine:
            cp.wait()

    sems = [pltpu.SemaphoreType.DMA((7 * n,)), pltpu.SemaphoreType.DMA((7 * n,)), pltpu.SemaphoreType.DMA((n,))]
    return _Comm(list(arrays), [_sds((N_DEV,) + a.shape, a.dtype) for a in arrays], sems, start, finish)


def _pair_sum(g, q, core, name, chip_major=False):
    rows, cols = g.shape[2:]
    tr = _row_tile(rows)

    def body(core_ref, g_ref, q_ref, o_ref):
        o_ref[...] = (g_ref[...] + q_ref[...]).astype(BF16)

    blk = pl.BlockSpec((None, tr, cols), lambda k, i, core_ref: (k, i, 0))
    if chip_major:
        own = pl.BlockSpec((None, None, tr, cols), lambda k, i, core_ref: (k, core_ref[0], i, 0))
    else:
        own = pl.BlockSpec((None, None, tr, cols), lambda k, i, core_ref: (core_ref[0], k, i, 0))
    return pl.pallas_call(
        body, name=name,
        grid_spec=pltpu.PrefetchScalarGridSpec(num_scalar_prefetch=1, grid=(N_CHIPS, rows // tr), in_specs=[own, blk], out_specs=blk),
        out_shape=_sds((N_CHIPS, rows, cols), BF16), compiler_params=_params("parallel", "parallel"),
    )(core, g, q)


def _sum_chips(own, landed, chip, name):
    _, rows, cols = own.shape
    tr = _row_tile(rows)

    def body(chip_ref, own_ref, a_ref, b_ref, c_ref, o_ref):
        acc = own_ref[...].astype(F32) + a_ref[...].astype(F32)
        o_ref[...] = (acc + b_ref[...].astype(F32)) + c_ref[...].astype(F32)

    blk = lambda flip: pl.BlockSpec((None, tr, cols), lambda i, chip_ref: (jnp.bitwise_xor(chip_ref[0], flip), i, 0))
    return pl.pallas_call(
        body, name=name,
        grid_spec=pltpu.PrefetchScalarGridSpec(num_scalar_prefetch=1, grid=(rows // tr,), in_specs=[blk(0), blk(1), blk(2), blk(3)],
                                               out_specs=pl.BlockSpec((tr, cols), lambda i, chip_ref: (i, 0))),
        out_shape=_sds((rows, cols), F32), compiler_params=_params("parallel"),
    )(chip, own, landed, landed, landed)


SUBLANES = 8


def _tile_rows(n_elems):
    return -(-n_elems // (SUBLANES * LANE)) * SUBLANES


SMALL_ITEMS = (("b_ada", N_MOD * D_MODEL), ("pre_w_mix", D_MODEL), ("post_w_mix", D_MODEL), ("pre_w_mlp", D_MODEL),
               ("post_w_mlp", D_MODEL), ("attn_out_w", ATT_WIDTH), ("hg_norm_w", HG_HEAD_DIM), ("attn_sinks", ATT_Q_HEADS),
               ("lb_0", HG_WIDTH), ("lb_1", HG_WIDTH))
SMALL_AT = {}
for _name, _size in SMALL_ITEMS:
    SMALL_AT[_name] = (sum(r for _, r in SMALL_AT.values()), _tile_rows(_size))
SMALL_ROWS = sum(r for _, r in SMALL_AT.values())
MOD_ROWS = SMALL_AT["b_ada"][1]
PLAIN_ROWS = SMALL_AT["lb_0"][0] - MOD_ROWS
LB_ROWS = SMALL_AT["lb_0"][1]


def _rows(a, nrows=None):
    flat = a.reshape(-1)
    nrows = _tile_rows(flat.shape[0]) if nrows is None else nrows
    return jnp.pad(flat, (0, nrows * LANE - flat.shape[0])).reshape(nrows, LANE)


def _pack_small(vals):
    vals = dict(vals, lb_0=vals["lb_table"][0], lb_1=vals["lb_table"][1])
    return jnp.concatenate([_rows(vals[name], SMALL_AT[name][1]) for name, _ in SMALL_ITEMS], axis=0)


def _unpack_small(p):
    def item(name, shape):
        first = SMALL_AT[name][0]
        size = shape[0] * shape[1]
        return p[first:first + SMALL_AT[name][1]].reshape(-1)[:size].reshape(shape)

    out = {name: item(name, (1, size)) for name, size in SMALL_ITEMS if not name.startswith("lb_")}
    out["lb_table"] = jnp.concatenate([item("lb_0", (1, HG_WIDTH)), item("lb_1", (1, HG_WIDTH))], axis=0)
    return out


def _pack_partials(dmod, plain, d_lb, loss_row):
    return jnp.concatenate([_rows(dmod, dmod.shape[0] * MOD_ROWS)] + [_rows(g) for g in plain] + [_rows(d_lb), _rows(loss_row)], axis=0)


def _small_update(packs, w, m, v, n_seq):
    mod_end = n_seq * MOD_ROWS
    lb_at = mod_end + PLAIN_ROWS
    t0, t1 = SMALL_AT["lb_0"][0], SMALL_AT["lb_1"][0]

    def body(p_ref, w_ref, m_ref, v_ref, g_ref, dl_ref, nm_ref, nv_ref, loss_ref):
        tot = p_ref[0]
        for d in range(1, N_DEV):
            tot = tot + p_ref[d]
        wv = w_ref[...]
        p1 = _sigmoid(wv[t1:t1 + LB_ROWS] - wv[t0:t0 + LB_ROWS])
        s = tot[lb_at:lb_at + LB_ROWS] * p1 * (1.0 - p1)
        g_bias = tot[0:MOD_ROWS]
        for q in range(1, n_seq):
            g_bias = g_bias + tot[q * MOD_ROWS:(q + 1) * MOD_ROWS]
        g = jnp.concatenate([g_bias, tot[mod_end:lb_at], -s, s], axis=0)
        g_ref[...] = g
        dl_ref[...], nm_ref[...], nv_ref[...] = _adamw_math(g, wv, m_ref[...], v_ref[...])
        loss_ref[...] = tot[lb_at + LB_ROWS:lb_at + LB_ROWS + SUBLANES]

    shp = _sds((SMALL_ROWS, LANE), F32)
    return pl.pallas_call(body, name="small_update", out_shape=[shp] * 4 + [_sds((SUBLANES, LANE), F32)],
                          compiler_params=_params())(packs, w, m, v)


def kernel(x, c, w_ada, b_ada, pre_w_mix, w_in, attn_sinks, attn_out_w, lb_table, hg_norm_w, w_out, post_w_mix, pre_w_mlp, w_up, w_down, post_w_mlp, loss_target, m_w_ada, m_b_ada, m_pre_w_mix, m_w_in, m_attn_sinks, m_attn_out_w, m_lb_table, m_hg_norm_w, m_w_out, m_post_w_mix, m_pre_w_mlp, m_w_up, m_w_down, m_post_w_mlp, v_w_ada, v_b_ada, v_pre_w_mix, v_w_in, v_attn_sinks, v_attn_out_w, v_lb_table, v_hg_norm_w, v_w_out, v_post_w_mix, v_pre_w_mlp, v_w_up, v_w_down, v_post_w_mlp):
    xi, yi, ci = _mesh_pos()
    chip = 2 * xi + yi
    dev = 2 * chip + ci
    bsz, seq, _ = x.shape
    ntok = bsz * seq
    ada_cols = w_ada.shape[2]
    core = jnp.reshape(ci, (1,)).astype(jnp.int32)
    chip_idx = jnp.reshape(chip, (1,)).astype(jnp.int32)
    flat = lambda a: a.reshape(ntok, a.shape[-1])
    unflat = lambda a: a.reshape(bsz, seq, a.shape[-1])
    tables = _rope_tables(seq)
    biases, chunk_masks = _band_biases(), _block_masks()

    def row_half(w):
        rows = w.shape[1] // 2
        return lax.dynamic_slice_in_dim(w[0], ci * rows, rows, axis=0).astype(BF16)

    def gather_buffer(w):
        rows, cols = w.shape[1] // 2, w.shape[2]
        own = w[0].astype(BF16).reshape(2, rows, cols)
        return lax.dynamic_update_slice(lax.empty((N_DEV, rows, cols), BF16), own, (2 * chip, 0, 0))

    w_in_t, m_in_t, v_in_t = [jnp.transpose(a[0])[None] for a in (w_in, m_w_in, v_w_in)]
    c_g, in_g = _allgather8([c, row_half(w_in_t)], "gather_first")
    c_all = c_g.reshape(N_DEV * bsz, D_MODEL)
    w_in_full = in_g.reshape(IN_COLS, D_MODEL)

    b_cols = lax.dynamic_slice_in_dim(b_ada, chip * ada_cols, ada_cols, axis=1)
    mod_part = _ada_fwd(c_all, w_ada[0], b_cols)
    half_rows = mod_part.shape[0] // 2
    (mod_g,) = _allgather8([lax.dynamic_slice_in_dim(mod_part, ci * half_rows, half_rows, axis=0)], "gather_mod")
    mod_all = mod_g.reshape(N_CHIPS, 2, half_rows, ada_cols).transpose(1, 2, 0, 3).reshape(N_DEV * bsz, N_MOD * D_MODEL)
    mod = lax.dynamic_slice_in_dim(mod_all, dev * bsz, bsz, axis=0)
    sh1, sc1, g1, sh2, sc2, g2 = [mod[:, i * D_MODEL:(i + 1) * D_MODEL].reshape(bsz, 1, D_MODEL) for i in range(N_MOD)]

    weights = (w_out, w_up, w_down)
    (out_part, up_part, down_part), started = _gather_start(
        [row_half(w) for w in weights], [gather_buffer(w) for w in weights], [mod_g], "gather_weights_start")

    h1, proj, qh, kh, vh = _in_proj_fused(x, pre_w_mix, sc1 + started[0:1, 0:1], sh1, w_in_full, tables)
    out_g = _gather_wait(out_part, [proj], "gather_out_wait")
    (attn_raw, cat, lse), ((out_g,),) = _attn_fwd(qh, kh, vh, attn_sinks, attn_out_w, biases, comms=[_plan_pair_forward([out_g])])
    up_g = _gather_wait(up_part, [attn_raw], "gather_up_wait")
    (o_raw, cat, states), ((up_g,),) = _hgrn_fwd(proj, lb_table, hg_norm_w, cat, chunk_masks, comms=[_plan_pair_forward([up_g])])
    down_g = _gather_wait(down_part, [o_raw], "gather_down_wait")
    w_out_full = out_g.reshape(D_MODEL, D_MODEL)
    w_up4 = up_g.reshape(N_CHIPS, D_MODEL, D_MODEL)
    mix, x1, h2 = _out_proj_fused(cat, w_out_full, x, post_w_mix, g1, pre_w_mlp, sc2, sh2)
    big_tm = min(ntok, 2048)
    up_spec = pl.BlockSpec((None, D_MODEL, D_MODEL), lambda i, j: (j, 0, 0))
    r, ((down_g,),) = _mm(flat(h2), w_up4, name="up_proj", out_dtype=BF16, tm=big_tm, tn=D_MODEL, n_out=D_FF, b_spec=up_spec,
                          epi=lambda acc: jnp.maximum(acc, 0.0), comms=[_plan_pair_forward([down_g])])
    w_down_full = down_g.reshape(D_FF, D_MODEL)
    square = lambda t: t * t
    loss_row, dy, dd, dg2, d_post_mlp = _down_proj_fused(unflat(r), w_down_full, x1, post_w_mlp, g2, loss_target)

    dpre = _mm(flat(dd), w_down_full, name="down_bwd", out_dtype=BF16, trans_b=True, tm=big_tm, tn=D_MODEL, extra=(r,),
               epi=lambda acc, rt: acc * (2.0 * rt.astype(F32)))
    half_rows = D_MODEL // 2
    g_down = _mm_tn(r, flat(dd), name="down_wgrad", tk=half_rows, tn=D_MODEL, a_fn=square,
                    out_shape=_sds((2, N_CHIPS, half_rows, D_MODEL), F32),
                    out_spec=pl.BlockSpec((None, None, half_rows, D_MODEL), lambda i, j: (i % 2, i // 2, 0, 0)))
    (dx1, dmix, dsc2, dsh2, dg1, d_pre_mlp, d_post_mix), ((q_down,),) = _up_bwd_fused(
        unflat(dpre), w_up4, dy, x1, mix, pre_w_mlp, sc2, post_w_mix, g1, comms=[_plan_pair([g_down], True)])
    g_up = _mm_tn(flat(h2), dpre, name="up_wgrad", tk=D_MODEL, tn=half_rows,
                  out_shape=_sds((2, N_CHIPS, half_rows, D_MODEL), F32),
                  out_spec=pl.BlockSpec((2, None, half_rows, half_rows), lambda i, j: (0, j // 2, 0, j % 2)))
    s_down = _pair_sum(g_down, q_down, core, "pair_sum_down")

    dcat, ((q_up,),) = _mm(flat(dmix), w_out_full, name="out_bwd", out_dtype=F32, trans_b=True, comms=[_plan_pair([g_up], True)])
    dcat = unflat(dcat)
    s_up = _pair_sum(g_up, q_up, core, "pair_sum_up")
    out_rows = D_MODEL // N_CHIPS
    g_out = _mm_tn(flat(cat), flat(dmix), name="out_wgrad", tk=2 * out_rows, tn=half_rows,
                   out_shape=_sds((2, N_CHIPS, out_rows, half_rows), F32),
                   out_spec=pl.BlockSpec((None, 2, out_rows, half_rows), lambda i, j: (j, i, 0, 0)))
    (dproj_rec, d_lb, d_hg_norm), ((x_down,), (q_out,)) = _hgrn_bwd(
        dcat, proj, o_raw, states, lb_table, hg_norm_w, chunk_masks, comms=[_plan_chip_exchange([s_down]), _plan_pair([g_out], True)])
    half_down = _sum_chips(s_down, x_down, chip_idx, "sum_chips_down")
    s_out = _pair_sum(g_out, q_out, core, "pair_sum_out")
    (dproj, d_attn_out, d_sinks), ((their_down,), (x_up,)) = _attn_bwd(
        dcat, attn_raw, attn_out_w, qh, kh, vh, lse, attn_sinks, tables, [bias.T for bias in biases], dproj_rec,
        comms=[_plan_pair([half_down], False), _plan_chip_exchange([s_up])])
    half_up = _sum_chips(s_up, x_up, chip_idx, "sum_chips_up")
    dproj = flat(dproj)
    in_rows = IN_COLS // N_CHIPS // 2
    g_in, ((x_out,),) = _mm_tn(dproj, flat(h1), name="in_wgrad", tk=2 * LANE, tn=D_MODEL, comms=[_plan_chip_exchange([s_out])])
    g_in = g_in.reshape(N_CHIPS, 2, in_rows, D_MODEL)
    half_out = _sum_chips(s_out, x_out, chip_idx, "sum_chips_out")
    dh1, ((q_in,), (their_up, their_out)) = _mm(
        dproj, w_in_full, name="in_bwd", out_dtype=F32,
        comms=[_plan_pair([g_in], "chip_major"), _plan_pair([half_up, half_out], False)])
    s_in = _pair_sum(g_in, q_in, core, "pair_sum_in", chip_major=True)
    in_sems, s_in, in_landing, started = _exchange_start(s_in, "exchange_in_start")
    grad_x, dsc1, dsh1, d_pre_mix = _norm1_bwd(unflat(dh1), dx1, x, pre_w_mix + started[0:1, 0:1], sc1)

    dmod = jnp.concatenate([dsh1, dsc1, dg1, dsh2, dsc2, dg2], axis=-1).reshape(bsz, N_MOD * D_MODEL)
    pack = _pack_partials(dmod, [d_pre_mix, d_post_mix, d_pre_mlp, d_post_mlp, d_attn_out, d_hg_norm, d_sinks], d_lb, loss_row)
    ((packs,),) = _comm_only([_plan_allgather8([pack])], "gather_small")
    w_small = dict(b_ada=b_ada, pre_w_mix=pre_w_mix, post_w_mix=post_w_mix, pre_w_mlp=pre_w_mlp, post_w_mlp=post_w_mlp,
                   attn_out_w=attn_out_w, hg_norm_w=hg_norm_w, attn_sinks=attn_sinks, lb_table=lb_table)
    m_small = dict(b_ada=m_b_ada, pre_w_mix=m_pre_w_mix, post_w_mix=m_post_w_mix, pre_w_mlp=m_pre_w_mlp, post_w_mlp=m_post_w_mlp,
                   attn_out_w=m_attn_out_w, hg_norm_w=m_hg_norm_w, attn_sinks=m_attn_sinks, lb_table=m_lb_table)
    v_small = dict(b_ada=v_b_ada, pre_w_mix=v_pre_w_mix, post_w_mix=v_post_w_mix, pre_w_mlp=v_pre_w_mlp, post_w_mlp=v_post_w_mlp,
                   attn_out_w=v_attn_out_w, hg_norm_w=v_hg_norm_w, attn_sinks=v_attn_sinks, lb_table=v_lb_table)
    *small_packed, loss_rows = _small_update(packs, _pack_small(w_small), _pack_small(m_small), _pack_small(v_small), bsz)
    small_out = [_unpack_small(p) for p in small_packed]
    loss = loss_rows[0, 0]

    dmod_all = packs[:, :bsz * MOD_ROWS, :].reshape(N_DEV * bsz, N_MOD * D_MODEL)
    dmod_cols = lax.dynamic_slice_in_dim(dmod_all, chip * ada_cols, ada_cols, axis=1)
    ada_out = _ada_bwd_adamw(c_all, dmod_cols, w_ada[0], m_w_ada[0], v_w_ada[0])

    s_in, x_in = _exchange_wait(in_sems, s_in, in_landing, [grad_x, ada_out[0]], "exchange_in_wait")
    half_in = _sum_chips(s_in, x_in, chip_idx, "sum_chips_in")
    ((their_in,),) = _comm_only([_plan_pair([half_in], False)], "pair_swap_in")
    big = dict(
        w_in=tuple(jnp.transpose(a) for a in _adamw_halves(half_in, their_in, core, w_in_t[0], m_in_t[0], v_in_t[0], axis=0,
                                                           name="adamw_in")),
        w_up=tuple(_adamw_halves(half_up, their_up, core, w_up[0], m_w_up[0], v_w_up[0], axis=0, name="adamw_up")),
        w_out=tuple(_adamw_halves(half_out, their_out, core, w_out[0], m_w_out[0], v_w_out[0], axis=1, name="adamw_out")),
        w_down=tuple(_adamw_halves(half_down, their_down, core, w_down[0], m_w_down[0], v_w_down[0], axis=0, name="adamw_down")),
        w_ada=tuple(ada_out),
    )
    order = ("w_ada", "b_ada", "pre_w_mix", "w_in", "attn_sinks", "attn_out_w", "lb_table", "hg_norm_w", "w_out", "post_w_mix",
             "pre_w_mlp", "w_up", "w_down", "post_w_mlp")
    outs = [loss, grad_x]
    for kind in range(4):
        for nm in order:
            outs.append(big[nm][kind][None] if nm in big else small_out[kind][nm])
    return tuple(outs)
```

```python
import jax
import jax.numpy as jnp
from jax import lax
from jax.experimental import pallas as pl
from jax.experimental.pallas import tpu as pltpu

F32 = jnp.float32
BF16 = jnp.bfloat16

D_MODEL = 1024
ATT_WIDTH = 512
ATT_HEAD_DIM = 64
ATT_Q_HEADS = 8
ATT_KV_HEADS = 2
ATT_GROUP = ATT_Q_HEADS // ATT_KV_HEADS
ATT_KV_COLS = ATT_KV_HEADS * ATT_HEAD_DIM
WINDOW = 128
ROPE_DIM = 16
ROPE_THETA = 500000.0
HG_WIDTH = 512
MIX_WIDTH = ATT_WIDTH + HG_WIDTH
HG_HEAD_DIM = 128
HG_HEADS = 4
HG_CHUNK = 32
IN_COLS = ATT_WIDTH + 2 * ATT_KV_COLS + 4 * HG_WIDTH
ATT_COLS = ATT_WIDTH + 2 * ATT_KV_COLS
D_FF = 4 * D_MODEL
N_MOD = 6
EPS = 1e-6
ATT_SCALE = ATT_HEAD_DIM ** -0.5

ADAM_LR = 0.001
ADAM_B1 = 0.9
ADAM_B2 = 0.999
ADAM_EPS = 1e-08
ADAM_WD = 0.01
ADAM_STEP = 10

N_CHIPS = 4
N_DEV = 8
LANE = 128
VMEM_LIMIT = 48 * 1024 * 1024
VMEM_LIMIT_BIG = 58 * 1024 * 1024
MESH = pl.DeviceIdType.MESH

NT_DIMS = (((1,), (1,)), ((), ()))
TN_DIMS = (((0,), (0,)), ((), ()))


def _sds(shape, dtype):
    return jax.ShapeDtypeStruct(tuple(shape), dtype)


def _params(*sem, vmem_limit=None):
    return pltpu.CompilerParams(dimension_semantics=sem, vmem_limit_bytes=VMEM_LIMIT if vmem_limit is None else vmem_limit)


def _sigmoid(x):
    return 1.0 / (1.0 + jnp.exp(-x))


def _dot(a, b, dims=None):
    a, b = a.astype(BF16), b.astype(BF16)
    if dims is None:
        return jnp.dot(a, b, preferred_element_type=F32)
    return lax.dot_general(a, b, dims, preferred_element_type=F32)


def _rms_fwd(x, w):
    rstd = lax.rsqrt(jnp.mean(x * x, axis=-1, keepdims=True) + EPS)
    xh = x * rstd
    return xh * w, xh, rstd


def _rms_bwd(dy, xh, rstd, w):
    dxh = dy * w
    dx = rstd * (dxh - xh * jnp.mean(dxh * xh, axis=-1, keepdims=True))
    return dx, dy * xh


def _colsum(x):
    return jnp.sum(x, axis=0, keepdims=True)


def _rms_hat(x):
    rstd = lax.rsqrt(jnp.mean(x * x, axis=-1, keepdims=True) + EPS)
    return x * rstd, rstd


def _rms_bwd_gain(dy, gain, xh, rstd):
    dxh = dy * gain
    dx = rstd * (dxh - xh * jnp.mean(dxh * xh, axis=-1, keepdims=True))
    return dx, _colsum(dy * xh)


def _row_tile(rows, cap=256):
    return max(t for t in range(16, cap + 1, 16) if rows % t == 0)


HBM_SPEC = pl.BlockSpec(memory_space=pltpu.HBM)


def _mesh_pos():
    return lax.axis_index("x"), lax.axis_index("y"), lax.axis_index("c")


class _Comm:
    def __init__(self, ins, outs, sems, start, finish, aliases=()):
        self.ins, self.outs, self.sems = list(ins), list(outs), list(sems)
        self.start, self.finish, self.aliases = start, finish, tuple(aliases)


def _call(body, args, *, name, grid, in_specs, out_specs, out_shape, sem, scratch_shapes=(), comms=(), aliases=None,
          vmem_limit=None):
    scratch_shapes = list(scratch_shapes)
    if not comms:
        return pl.pallas_call(body, name=name, grid=grid, in_specs=in_specs, out_specs=out_specs, out_shape=out_shape,
                              input_output_aliases=dict(aliases or {}), scratch_shapes=scratch_shapes,
                              compiler_params=_params(*sem, vmem_limit=vmem_limit))(*args)
    single = not isinstance(out_shape, (list, tuple))
    out_specs_l = [out_specs] if single else list(out_specs)
    out_shape_l = [out_shape] if single else list(out_shape)
    n_in, n_out, n_scr = len(in_specs), len(out_shape_l), len(scratch_shapes)
    n_ci = [len(cm.ins) for cm in comms]
    n_co = [len(cm.outs) for cm in comms]
    n_cs = [len(cm.sems) for cm in comms]
    aliases = dict(aliases or {})
    for k, cm in enumerate(comms):
        for i, o in cm.aliases:
            aliases[n_in + sum(n_ci[:k]) + i] = n_out + sum(n_co[:k]) + o

    def fused(*refs):
        pos = [0]

        def take(n):
            part = refs[pos[0]:pos[0] + n]
            pos[0] += n
            return part

        ins = take(n_in)
        c_ins = [take(n) for n in n_ci]
        outs = take(n_out)
        c_outs = [take(n) for n in n_co]
        scr = take(n_scr)
        c_sems = [take(n) for n in n_cs]
        first, last = True, True
        for d, size in enumerate(grid):
            first = jnp.logical_and(first, pl.program_id(d) == 0)
            last = jnp.logical_and(last, pl.program_id(d) == size - 1)

        def run(which):
            for cm, ci, co, cs in zip(comms, c_ins, c_outs, c_sems):
                getattr(cm, which)(ci, co, cs)

        if grid:
            pl.when(first)(lambda: run("start"))
        else:
            run("start")
        body(*ins, *outs, *scr)
        if grid:
            pl.when(last)(lambda: run("finish"))
        else:
            run("finish")

    res = pl.pallas_call(
        fused, name=name, grid=grid, in_specs=list(in_specs) + [HBM_SPEC] * sum(n_ci),
        out_specs=out_specs_l + [HBM_SPEC] * sum(n_co), out_shape=out_shape_l + [s for cm in comms for s in cm.outs],
        input_output_aliases=aliases, scratch_shapes=scratch_shapes + [s for cm in comms for s in cm.sems],
        compiler_params=_params(*["arbitrary"] * len(grid), vmem_limit=vmem_limit),
    )(*args, *[a for cm in comms for a in cm.ins])
    main = res[:n_out]
    extra, at = [], n_out
    for n in n_co:
        extra.append(list(res[at:at + n]))
        at += n
    return (main[0] if single else list(main)), extra


def _mm(a, b, *, name, out_dtype, trans_b=False, tm=512, tn=None, extra=(), epi=None, b_spec=None, n_out=None, comms=()):
    m_total, k_total = a.shape
    if n_out is None:
        n_out = b.shape[0] if trans_b else b.shape[1]
    tn = n_out if tn is None else tn
    grid = (m_total // tm, n_out // tn)
    dims = NT_DIMS if trans_b else None

    def body(*refs):
        a_ref, b_ref = refs[0], refs[1]
        extra_refs = refs[2:2 + len(extra)]
        o_ref = refs[2 + len(extra)]
        acc = _dot(a_ref[...], b_ref[...], dims)
        if epi is not None:
            acc = epi(acc, *[r[...] for r in extra_refs])
        o_ref[...] = acc.astype(out_dtype)

    if b_spec is None:
        if trans_b:
            b_spec = pl.BlockSpec((tn, k_total), lambda i, j: (j, 0))
        else:
            b_spec = pl.BlockSpec((k_total, tn), lambda i, j: (0, j))
    in_specs = [pl.BlockSpec((tm, k_total), lambda i, j: (i, 0)), b_spec]
    in_specs += [pl.BlockSpec((tm, tn), lambda i, j: (i, j)) for _ in extra]
    return _call(
        body, (a, b, *extra), name=name, grid=grid, in_specs=in_specs,
        out_specs=pl.BlockSpec((tm, tn), lambda i, j: (i, j)),
        out_shape=_sds((m_total, n_out), out_dtype),
        sem=("parallel", "parallel"), comms=comms)


def _mm_tn(a, b, *, name, tk, tn, a_fn=None, out_shape=None, out_spec=None, comms=()):
    m_total, k_total = a.shape
    n_total = b.shape[1]
    grid = (k_total // tk, n_total // tn)

    def body(a_ref, b_ref, o_ref):
        av = a_ref[...]
        part = _dot(av if a_fn is None else a_fn(av), b_ref[...], TN_DIMS)
        o_ref[...] = part.reshape(o_ref.shape)

    if out_shape is None:
        out_shape = _sds((k_total, n_total), F32)
        out_spec = pl.BlockSpec((tk, tn), lambda i, j: (i, j))
    return _call(
        body, (a, b), name=name, grid=grid,
        in_specs=[pl.BlockSpec((m_total, tk), lambda i, j: (0, i)), pl.BlockSpec((m_total, tn), lambda i, j: (0, j))],
        out_specs=out_spec, out_shape=out_shape, sem=("parallel", "parallel"), comms=comms)


def _ada_fwd(c_all, w_shard, b_shard):
    nb, ncol = c_all.shape[0], w_shard.shape[1]
    tn = 512

    def body(c_ref, w_ref, b_ref, o_ref):
        c = c_ref[...]
        o_ref[...] = _dot(c * _sigmoid(c), w_ref[...]) + b_ref[...]

    return pl.pallas_call(
        body, name="ada_fwd", grid=(ncol // tn,),
        in_specs=[pl.BlockSpec((nb, D_MODEL), lambda j: (0, 0)), pl.BlockSpec((D_MODEL, tn), lambda j: (0, j)),
                  pl.BlockSpec((1, tn), lambda j: (0, j))],
        out_specs=pl.BlockSpec((nb, tn), lambda j: (0, j)), out_shape=_sds((nb, ncol), F32),
        compiler_params=_params("parallel"),
    )(c_all, w_shard, b_shard)


def _adamw_math(g, w, m, v):
    m = ADAM_B1 * m + (1.0 - ADAM_B1) * g
    v = ADAM_B2 * v + (1.0 - ADAM_B2) * (g * g)
    m_hat = m / (1.0 - ADAM_B1 ** ADAM_STEP)
    v_hat = v / (1.0 - ADAM_B2 ** ADAM_STEP)
    delta = -ADAM_LR * (m_hat / (jnp.sqrt(v_hat) + ADAM_EPS) + ADAM_WD * w)
    return delta, m, v


def _ada_bwd_adamw(c_all, dmod_cols, w, m, v):
    nb, ncol = dmod_cols.shape
    tn = 256

    def body(c_ref, d_ref, w_ref, m_ref, v_ref, g_ref, dl_ref, nm_ref, nv_ref):
        c = c_ref[...]
        g = _dot(c * _sigmoid(c), d_ref[...], TN_DIMS)
        g_ref[...] = g
        dl_ref[...], nm_ref[...], nv_ref[...] = _adamw_math(g, w_ref[...], m_ref[...], v_ref[...])

    col = pl.BlockSpec((D_MODEL, tn), lambda j: (0, j))
    shp = _sds((D_MODEL, ncol), F32)
    return pl.pallas_call(
        body, name="ada_bwd_adamw", grid=(ncol // tn,),
        in_specs=[pl.BlockSpec((nb, D_MODEL), lambda j: (0, 0)), pl.BlockSpec((nb, tn), lambda j: (0, j)), col, col, col],
        out_specs=[col, col, col, col], out_shape=[shp, shp, shp, shp],
        compiler_params=_params("parallel"),
    )(c_all, dmod_cols, w, m, v)


def _adamw_halves(own, theirs, core, w, m, v, *, axis, name):
    r2, c2 = own.shape
    tr = _row_tile(r2)
    nt = r2 // tr

    def body(core_ref, own_ref, their_ref, w_ref, m_ref, v_ref, g_ref, dl_ref, nm_ref, nv_ref):
        g = jnp.where(pl.program_id(0) == core_ref[0], own_ref[...], their_ref[...])
        g_ref[...] = g
        dl_ref[...], nm_ref[...], nv_ref[...] = _adamw_math(g, w_ref[...], m_ref[...], v_ref[...])

    if axis == 0:
        full = pl.BlockSpec((tr, c2), lambda h, i, core_ref: (h * nt + i, 0))
    else:
        full = pl.BlockSpec((tr, c2), lambda h, i, core_ref: (i, h))
    half = pl.BlockSpec((tr, c2), lambda h, i, core_ref: (i, 0))
    shp = _sds(w.shape, F32)
    return pl.pallas_call(
        body, name=name,
        grid_spec=pltpu.PrefetchScalarGridSpec(num_scalar_prefetch=1, grid=(2, nt), in_specs=[half, half, full, full, full],
                                               out_specs=[full] * 4),
        out_shape=[shp] * 4, compiler_params=_params("parallel", "parallel"),
    )(core, own, theirs, w, m, v)


def _tok_spec(tm, width=D_MODEL):
    return pl.BlockSpec((None, tm, width), lambda b, i: (b, i, 0))


def _row_spec(width=D_MODEL):
    return pl.BlockSpec((None, 1, width), lambda b, i: (b, 0, 0))


def _vec_spec(width=D_MODEL):
    return pl.BlockSpec((1, width), lambda b, i: (0, 0))


class _RowsOf:
    def __init__(self, ref, first, count):
        self.ref, self.rows = ref, slice(first, first + count)

    def __getitem__(self, idx):
        return self.ref[self.rows, :]

    def __setitem__(self, idx, value):
        self.ref[self.rows, :] = value


def _mm_rows(a, b, *, name, tm, extra, extra_specs, out_specs, out_shape, epi, pro=None, trans_b=False, b_chunks=1, comms=(),
             parts=1, zero_per_seq=(), zero_once=(), vmem_limit=None):
    bsz, seq, k_total = a.shape
    kc = k_total // b_chunks
    dims = NT_DIMS if trans_b else None
    rows = tm // parts

    def body(*refs):
        a_ref, b_ref = refs[0], refs[1]
        ex, outs = refs[2:2 + len(extra)], refs[2 + len(extra):]
        if zero_per_seq:
            @pl.when(pl.program_id(1) == 0)
            def _():
                for k in zero_per_seq:
                    outs[k][...] = jnp.zeros_like(outs[k])
        if zero_once:
            @pl.when(jnp.logical_and(pl.program_id(0) == 0, pl.program_id(1) == 0))
            def _():
                for k in zero_once:
                    outs[k][...] = jnp.zeros_like(outs[k])

        def part_of(ref, p):
            tiled = len(ref.shape) == 2 and ref.shape[0] == tm
            return _RowsOf(ref, p * rows, rows) if tiled and parts > 1 else ref

        accs = []
        for p in range(parts):
            a_p, ex_p, outs_p = part_of(a_ref, p), [part_of(r, p) for r in ex], [part_of(r, p) for r in outs]
            if b_chunks == 1:
                accs.append(_dot(a_p[...] if pro is None else pro(a_p, ex_p, outs_p), b_ref[...], dims))
            else:
                acc = _dot(a_p[...][:, 0:kc], b_ref[0], NT_DIMS)
                for k in range(1, b_chunks):
                    acc = acc + _dot(a_p[...][:, k * kc:(k + 1) * kc], b_ref[k], NT_DIMS)
                accs.append(acc)
        for p in range(parts):
            epi(accs[p], [part_of(r, p) for r in ex], [part_of(r, p) for r in outs])

    b_spec = pl.BlockSpec(b.shape, lambda bb, i: (0,) * b.ndim)
    return _call(
        body, (a, b, *extra), name=name, grid=(bsz, seq // tm), in_specs=[_tok_spec(tm, k_total), b_spec, *extra_specs],
        out_specs=out_specs, out_shape=out_shape, sem=("arbitrary", "arbitrary"), comms=comms, vmem_limit=vmem_limit)


def _in_proj_fused(x, w, sc, sh, w_in_t, tables, comms=()):
    tm = 512
    bsz, seq, _ = x.shape
    half = ROPE_DIM // 2
    heads_per_slab = LANE // ATT_HEAD_DIM

    def pro(x_ref, ex, outs):
        y, _, _ = _rms_fwd(x_ref[...], ex[0][...])
        h = (y * (1.0 + ex[1][...]) + ex[2][...]).astype(BF16)
        outs[0][...] = h
        return h

    def epi(acc, ex, outs):
        c, u, d = ex[3][...], ex[4][...], ex[5][...]
        _, rec_ref, q_ref, k_ref, v_ref = outs
        for k in range(HG_SLABS):
            rec_ref[k] = acc[:, ATT_COLS + k * HG_WIDTH:ATT_COLS + (k + 1) * HG_WIDTH]

        def rope(z):
            return (z * c + pltpu.roll(z, half, 1) * u + pltpu.roll(z, LANE - half, 1) * d).astype(BF16)

        for s in range(ATT_WIDTH // LANE):
            slab = rope(acc[:, s * LANE:(s + 1) * LANE])
            for part in range(heads_per_slab):
                g, hh = divmod(s * heads_per_slab + part, ATT_GROUP)
                piece = slab[:, part * ATT_HEAD_DIM:(part + 1) * ATT_HEAD_DIM]
                for blk in range(tm // WINDOW):
                    q_ref[blk, g, hh * WINDOW:(hh + 1) * WINDOW, :] = piece[blk * WINDOW:(blk + 1) * WINDOW]
        rk = rope(acc[:, ATT_WIDTH:ATT_WIDTH + LANE])
        vv = acc[:, ATT_WIDTH + LANE:ATT_COLS].astype(BF16)
        for g in range(ATT_KV_HEADS):
            k_ref[g] = rk[:, g * ATT_HEAD_DIM:(g + 1) * ATT_HEAD_DIM]
            v_ref[g] = vv[:, g * ATT_HEAD_DIM:(g + 1) * ATT_HEAD_DIM]

    tab = pl.BlockSpec((tm, LANE), lambda b, i: (i, 0))
    kv_spec = pl.BlockSpec((None, ATT_KV_HEADS, tm, ATT_HEAD_DIM), lambda b, i: (b, 0, i, 0))
    kv_shape = _sds((bsz, ATT_KV_HEADS, seq, ATT_HEAD_DIM), BF16)
    q_spec = pl.BlockSpec((None, tm // WINDOW, ATT_KV_HEADS, GROUP_ROWS, ATT_HEAD_DIM), lambda b, i: (b, i, 0, 0, 0))
    return _mm_rows(x, w_in_t, name="in_proj", tm=tm, extra=(w, sc, sh, *tables),
                    extra_specs=[_vec_spec(), _row_spec(), _row_spec(), tab, tab, tab],
                    out_specs=[_tok_spec(tm), pl.BlockSpec((None, HG_SLABS, tm, HG_WIDTH), lambda b, i: (b, 0, i, 0)), q_spec,
                               kv_spec, kv_spec],
                    out_shape=[_sds(x.shape, BF16), _sds((bsz, HG_SLABS, seq, HG_WIDTH), F32),
                               _sds((bsz, seq // WINDOW, ATT_KV_HEADS, GROUP_ROWS, ATT_HEAD_DIM), BF16), kv_shape, kv_shape],
                    pro=pro, epi=epi, trans_b=True, comms=comms)


def _rope_tables(seq):
    half = ROPE_DIM // 2
    inv_freq = ROPE_THETA ** (-jnp.arange(0, ROPE_DIM, 2, dtype=F32) / ROPE_DIM)
    ang = jnp.arange(seq, dtype=F32)[:, None] * inv_freq[None, :]
    cos, sin = jnp.cos(ang), jnp.sin(ang)
    rest = ATT_HEAD_DIM - ROPE_DIM
    ones, zeros, zh = jnp.ones((seq, rest), F32), jnp.zeros((seq, rest), F32), jnp.zeros((seq, half), F32)
    reps = LANE // ATT_HEAD_DIM
    t_cos = jnp.tile(jnp.concatenate([cos, cos, ones], axis=1), (1, reps))
    t_up = jnp.tile(jnp.concatenate([zh, sin, zeros], axis=1), (1, reps))
    t_dn = jnp.tile(jnp.concatenate([-sin, zh, zeros], axis=1), (1, reps))
    return t_cos, t_up, t_dn


GROUP_ROWS = ATT_GROUP * WINDOW


ATT_BPS = 2


MASKED = -1e30


def _band_biases():
    row = jnp.arange(GROUP_ROWS)[:, None] % WINDOW
    col = jnp.arange(2 * WINDOW)[None, :]
    own = jnp.logical_and(col >= WINDOW, col - WINDOW <= row)
    before = jnp.logical_and(col < WINDOW, col > row)
    return (jnp.where(jnp.logical_or(own, before), 0.0, MASKED).astype(F32), jnp.where(own, 0.0, MASKED).astype(F32))


def _band_bias(full_ref, first_ref, has_prev):
    return full_ref[...] if has_prev is True else jnp.where(has_prev, full_ref[...], first_ref[...])


def _sink_column(sink_ref, g):
    head = lax.broadcasted_iota(jnp.int32, (GROUP_ROWS, 1), 0) // WINDOW
    col = jnp.full((GROUP_ROWS, 1), sink_ref[0, g * ATT_GROUP], F32)
    for hh in range(1, ATT_GROUP):
        col = jnp.where(head == hh, sink_ref[0, g * ATT_GROUP + hh], col)
    return col


def _sink_row(sink_ref, g):
    return jnp.concatenate([jnp.full((1, WINDOW), sink_ref[0, g * ATT_GROUP + hh], F32) for hh in range(ATT_GROUP)], axis=1)


def _bias_spec(transposed=False):
    shape = (2 * WINDOW, GROUP_ROWS) if transposed else (GROUP_ROWS, 2 * WINDOW)
    return pl.BlockSpec(shape, lambda b, i: (0, 0))


def _attn_specs():
    q_spec = pl.BlockSpec((None, ATT_BPS, ATT_KV_HEADS, GROUP_ROWS, ATT_HEAD_DIM), lambda b, i: (b, i, 0, 0, 0))
    kv_cur = pl.BlockSpec((None, ATT_KV_HEADS, ATT_BPS * WINDOW, ATT_HEAD_DIM), lambda b, i: (b, 0, i, 0))
    kv_prev = pl.BlockSpec((None, ATT_KV_HEADS, WINDOW, ATT_HEAD_DIM), lambda b, i: (b, 0, jnp.maximum(ATT_BPS * i - 1, 0), 0))
    return q_spec, kv_cur, kv_prev


def _band(prev_ref, cur_ref, g, blk):
    own = cur_ref[g, blk * WINDOW:(blk + 1) * WINDOW]
    before = prev_ref[g] if blk == 0 else cur_ref[g, (blk - 1) * WINDOW:blk * WINDOW]
    return jnp.concatenate([before, own], axis=0)


def _attn_fwd(qh, kh, vh, sinks, w_norm, biases, comms=()):
    bsz, nblk = qh.shape[0], qh.shape[1]
    seq = nblk * WINDOW
    rows = ATT_BPS * WINDOW

    def body(sink_ref, q_ref, kc_ref, kp_ref, vc_ref, vp_ref, w_ref, full_ref, first_ref, raw_ref, an_ref, l_ref):
        l_ref[...] = jnp.zeros_like(l_ref)
        def block(blk):
            bias = _band_bias(full_ref, first_ref, True if blk else pl.program_id(1) > 0)
            groups = range(ATT_KV_HEADS)
            keys, vals = [_band(kp_ref, kc_ref, g, blk) for g in groups], [_band(vp_ref, vc_ref, g, blk) for g in groups]
            sink = [_sink_column(sink_ref, g) for g in groups]
            s = [_dot(q_ref[blk, g], keys[g], NT_DIMS) * ATT_SCALE + bias for g in groups]
            yield
            m = [jnp.maximum(jnp.max(s[g], axis=-1, keepdims=True), sink[g]) for g in groups]
            p = [jnp.exp(s[g] - m[g]) for g in groups]
            den = [jnp.sum(p[g], axis=-1, keepdims=True) + jnp.exp(sink[g] - m[g]) for g in groups]
            yield
            o = [_dot(p[g] / den[g], vals[g]) for g in groups]
            yield
            lse = [m[g] + jnp.log(den[g]) for g in groups]
            tok = slice(blk * WINDOW, (blk + 1) * WINDOW)
            for g in groups:
                for hh in range(ATT_GROUP):
                    h = g * ATT_GROUP + hh
                    raw_ref[tok, h * ATT_HEAD_DIM:(h + 1) * ATT_HEAD_DIM] = o[g][hh * WINDOW:(hh + 1) * WINDOW]
                    l_ref[tok, h:h + 1] = lse[g][hh * WINDOW:(hh + 1) * WINDOW]

        _in_step(block(blk) for blk in range(ATT_BPS))
        y, _, _ = _rms_fwd(raw_ref[...], w_ref[...])
        an_ref[...] = y.astype(BF16)

    cur = lambda width: pl.BlockSpec((None, rows, width), lambda b, i: (b, i, 0))
    q_spec, kv_cur, kv_prev = _attn_specs()
    return _call(
        body, (sinks, qh, kh, kh, vh, vh, w_norm, *biases), name="attn_fwd", grid=(bsz, nblk // ATT_BPS),
        in_specs=[pl.BlockSpec(memory_space=pltpu.SMEM), q_spec, kv_cur, kv_prev, kv_cur, kv_prev, _vec_spec(ATT_WIDTH),
                  _bias_spec(), _bias_spec()],
        out_specs=[cur(ATT_WIDTH), cur(ATT_WIDTH), cur(LANE)],
        out_shape=[_sds((bsz, seq, ATT_WIDTH), F32), _sds((bsz, seq, MIX_WIDTH), BF16), _sds((bsz, seq, LANE), F32)],
        sem=("parallel", "parallel"), comms=comms)


HG_Q0 = ATT_COLS // LANE
HG_F0 = HG_Q0 + HG_HEADS
HG_I0 = HG_F0 + HG_HEADS
HG_G0 = HG_I0 + HG_HEADS
HG_SLABS = 4
HG_Q, HG_F, HG_I, HG_G = range(HG_SLABS)
HG_TOK = 256
HG_NCH = HG_TOK // HG_CHUNK
HG_HPS = 2


def _block_masks():
    row = jnp.arange(HG_TOK)[:, None]
    col = jnp.arange(HG_TOK)[None, :]
    same = (row // HG_CHUNK) == (col // HG_CHUNK)
    return jnp.logical_and(same, col <= row).astype(F32), jnp.logical_and(same, col >= row).astype(F32)


def _row_in_chunk():
    return lax.broadcasted_iota(jnp.int32, (HG_TOK, LANE), 0) % HG_CHUNK


def _chunk_cumsum(x, reverse=False):
    ric = _row_in_chunk()
    shift = 1
    while shift < HG_CHUNK:
        if reverse:
            x = x + jnp.where(ric < HG_CHUNK - shift, pltpu.roll(x, HG_TOK - shift, 0), 0.0)
        else:
            x = x + jnp.where(ric >= shift, pltpu.roll(x, shift, 0), 0.0)
        shift *= 2
    return x


def _chunk_rows(rows):
    stacked = jnp.concatenate([r[None] for r in rows], axis=0)
    return jnp.broadcast_to(stacked, (HG_NCH, HG_CHUNK, LANE)).reshape(HG_TOK, LANE)


def _chunk_slices(x):
    return [x[j * HG_CHUNK:(j + 1) * HG_CHUNK] for j in range(HG_NCH)]


def _in_step(stages):
    stages = list(stages)
    while stages:
        stages = [g for g in stages if next(g, stages) is not stages]


def _hgrn_common(tbl, hf, hq):
    lb = _sigmoid(tbl[1:2] - tbl[0:1])
    sig = _sigmoid(hf)
    f = lb + (1.0 - lb) * sig
    sq = _sigmoid(hq)
    q, k = hq * sq, 1.0 - f
    b = _chunk_cumsum(jnp.log(f))
    last = [b[(j + 1) * HG_CHUNK - 1:(j + 1) * HG_CHUNK] for j in range(HG_NCH)]
    bl = _chunk_rows(last)
    e_b, e_nb, e_rem = jnp.exp(b), jnp.exp(-b), jnp.exp(bl - b)
    e_last = [jnp.exp(r) for r in last]
    return dict(lb=lb, sig=sig, f=f, sq=sq, q=q, k=k, e_b=e_b, e_nb=e_nb, e_rem=e_rem, e_last=e_last,
                qd=q * e_b, kd=k * e_nb, ku=k * e_rem)


def _hgrn_fwd(proj, lb_table, norm_w, mix_in, masks, comms=()):
    bsz, _, seq, _ = proj.shape
    nstep = seq // HG_TOK

    def body(tbl_ref, nw_ref, p_ref, mix_ref, lower_ref, o_ref, rec_ref, st_ref, s_scr):
        @pl.when(pl.program_id(2) == 0)
        def _():
            s_scr[...] = jnp.zeros_like(s_scr)

        lower = lower_ref[...]

        def head(hp):
            ls = slice(hp * LANE, (hp + 1) * LANE)
            v, hg = p_ref[HG_I, :, ls], p_ref[HG_G, :, ls]
            t = _hgrn_common(tbl_ref[:, ls], p_ref[HG_F, :, ls], p_ref[HG_Q, :, ls])
            a = _dot(t["qd"], t["kd"], NT_DIMS) * lower
            o_intra = _dot(a, v)
            v_c, ku_c, qd_c = [_chunk_slices(z.astype(BF16)) for z in (v, t["ku"], t["qd"])]
            updates = [_dot(v_c[j], ku_c[j], TN_DIMS) for j in range(HG_NCH)]
            yield
            st = s_scr[hp]
            states = []
            for j in range(HG_NCH):
                states.append(st)
                st = st * t["e_last"][j] + updates[j]
            s_scr[hp] = st
            yield
            o = o_intra + jnp.concatenate([_dot(qd_c[j], states[j], NT_DIMS) for j in range(HG_NCH)], axis=0)
            st_ref[hp, 0] = states[0]
            o_ref[:, ls] = o
            y, _, _ = _rms_fwd(o, nw_ref[...])
            rec_ref[:, ls] = (y * (hg * _sigmoid(hg))).astype(BF16)

        _in_step(head(hp) for hp in range(HG_HPS))

    width = HG_HPS * LANE
    head_out = pl.BlockSpec((None, HG_TOK, width), lambda b, h, t: (b, t, h))
    mix_out = pl.BlockSpec((None, HG_TOK, width), lambda b, h, t: (b, t, ATT_WIDTH // width + h))
    return _call(
        body, (lb_table, norm_w, proj, mix_in, masks[0]), name="hgrn_fwd", grid=(bsz, HG_HEADS // HG_HPS, nstep),
        in_specs=[pl.BlockSpec((2, width), lambda b, h, t: (0, h)), pl.BlockSpec((1, LANE), lambda b, h, t: (0, 0)),
                  pl.BlockSpec((None, HG_SLABS, HG_TOK, width), lambda b, h, t: (b, 0, t, h)), pl.BlockSpec(memory_space=pl.ANY),
                  pl.BlockSpec((HG_TOK, HG_TOK), lambda b, h, t: (0, 0))],
        out_specs=[head_out, mix_out,
                   pl.BlockSpec((None, HG_HPS, 1, LANE, LANE), lambda b, h, t: (b, h, t, 0, 0))],
        out_shape=[_sds((bsz, seq, HG_WIDTH), F32), _sds(mix_in.shape, BF16),
                   _sds((bsz, HG_HEADS, nstep, LANE, LANE), F32)],
        scratch_shapes=[pltpu.VMEM((HG_HPS, LANE, LANE), F32)],
        sem=("parallel", "parallel", "arbitrary"), comms=comms, aliases={3: 1})


def _out_proj_fused(cat, w_out, x, post_w, g1, pre_w, sc2, sh2):
    tm = 512

    def epi(mix, ex, outs):
        x_ref, pw_ref, g1_ref, w2_ref, sc_ref, sh_ref = ex
        outs[0][...] = mix
        n1, _, _ = _rms_fwd(mix, pw_ref[...])
        x1 = x_ref[...] + g1_ref[...] * n1
        outs[1][...] = x1
        y2, _, _ = _rms_fwd(x1, w2_ref[...])
        outs[2][...] = (y2 * (1.0 + sc_ref[...]) + sh_ref[...]).astype(BF16)

    return _mm_rows(cat, w_out, name="out_proj", tm=tm, extra=(x, post_w, g1, pre_w, sc2, sh2),
                    extra_specs=[_tok_spec(tm), _vec_spec(), _row_spec(), _vec_spec(), _row_spec(), _row_spec()],
                    out_specs=[_tok_spec(tm), _tok_spec(tm), _tok_spec(tm)],
                    out_shape=[_sds(x.shape, F32), _sds(x.shape, F32), _sds(x.shape, BF16)], epi=epi)


def _acc_out(ref, first, value):
    @pl.when(first)
    def _():
        ref[...] = value

    @pl.when(jnp.logical_not(first))
    def _():
        ref[...] += value


def _down_proj_fused(r, w_down, x1, post_w, g2, target):
    tm = 512
    bsz = x1.shape[0]

    def pro(r_ref, ex, outs):
        rv = r_ref[...]
        return rv * rv

    def epi(down, ex, outs):
        x1_ref, w_ref, g2_ref, t_ref = ex
        loss_ref, dy_ref, dd_ref, dg2_ref, dw_ref = outs
        w, g2v = w_ref[...], g2_ref[...]
        gain = g2v * w
        dh, rstd = _rms_hat(down)
        err = x1_ref[...] + dh * gain - t_ref[...]
        part = (0.5 / D_MODEL) * jnp.sum(jnp.sum(err * err, axis=-1, keepdims=True), axis=0, keepdims=True)
        loss_ref[...] += jnp.broadcast_to(part, (1, LANE))
        dy = err * (1.0 / D_MODEL)
        dy_ref[...] = dy
        dd, per_col = _rms_bwd_gain(dy, gain, dh, rstd)
        dd_ref[...] = dd.astype(BF16)
        dg2_ref[...] += per_col * w
        dw_ref[...] += per_col * g2v

    return _mm_rows(r, w_down, name="down_proj", tm=tm, extra=(x1, post_w, g2, target),
                    extra_specs=[_tok_spec(tm), _vec_spec(), _row_spec(), _tok_spec(tm)],
                    out_specs=[_vec_spec(LANE), _tok_spec(tm), _tok_spec(tm), _row_spec(), _vec_spec()],
                    out_shape=[_sds((1, LANE), F32), _sds(x1.shape, F32), _sds(x1.shape, BF16), _sds((bsz, 1, D_MODEL), F32),
                               _sds((1, D_MODEL), F32)], pro=pro, epi=epi, parts=2, zero_per_seq=(3,), zero_once=(0, 4),
                    vmem_limit=VMEM_LIMIT_BIG)


def _up_bwd_fused(dpre, w_up4, dy, x1, mix, pre_w, sc2, post_w, g1, comms=()):
    tm = 512
    bsz = x1.shape[0]

    def epi(dh2v, ex, outs):
        dy_ref, x1_ref, mix_ref, w2_ref, sc_ref, pw_ref, g1_ref = ex
        dx1_ref, dmix_ref, dsc_ref, dsh_ref, dg1_ref, dw2_ref, dpw_ref = outs
        w2, pw, g1v = w2_ref[...], pw_ref[...], g1_ref[...]
        mod2 = 1.0 + sc_ref[...]
        xh2, rstd2 = _rms_hat(x1_ref[...])
        dsh_ref[...] += _colsum(dh2v)
        dx1n, per_col2 = _rms_bwd_gain(dh2v, mod2 * w2, xh2, rstd2)
        dsc_ref[...] += per_col2 * w2
        dw2_ref[...] += per_col2 * mod2
        dx1 = dy_ref[...] + dx1n
        dx1_ref[...] = dx1
        mh, rstd1 = _rms_hat(mix_ref[...])
        dmix, per_col1 = _rms_bwd_gain(dx1, g1v * pw, mh, rstd1)
        dmix_ref[...] = dmix.astype(BF16)
        dg1_ref[...] += per_col1 * pw
        dpw_ref[...] += per_col1 * g1v

    row_shape = _sds((bsz, 1, D_MODEL), F32)
    vec_shape = _sds((1, D_MODEL), F32)
    return _mm_rows(dpre, w_up4, name="up_bwd", tm=tm, extra=(dy, x1, mix, pre_w, sc2, post_w, g1),
                    extra_specs=[_tok_spec(tm), _tok_spec(tm), _tok_spec(tm), _vec_spec(), _row_spec(), _vec_spec(), _row_spec()],
                    out_specs=[_tok_spec(tm), _tok_spec(tm), _row_spec(), _row_spec(), _row_spec(), _vec_spec(), _vec_spec()],
                    out_shape=[_sds(x1.shape, F32), _sds(x1.shape, BF16), row_shape, row_shape, row_shape, vec_shape, vec_shape],
                    epi=epi, b_chunks=w_up4.shape[0], comms=comms, parts=2, zero_per_seq=(2, 3, 4), zero_once=(5, 6),
                    vmem_limit=VMEM_LIMIT_BIG)


def _norm1_bwd(dh1, dx1, x, pre_w, sc1, tm=512, comms=()):
    bsz, seq, _ = x.shape

    def body(dh_ref, dx1_ref, x_ref, w_ref, sc_ref, gx_ref, dsc_ref, dsh_ref, dw_ref):
        b, i = pl.program_id(0), pl.program_id(1)
        w = w_ref[...]
        dh = dh_ref[...]
        mod = 1.0 + sc_ref[...]
        xh, rstd = _rms_hat(x_ref[...])
        dx, per_col = _rms_bwd_gain(dh, mod * w, xh, rstd)
        _acc_out(dsh_ref, i == 0, _colsum(dh))
        _acc_out(dsc_ref, i == 0, per_col * w)
        _acc_out(dw_ref, jnp.logical_and(b == 0, i == 0), per_col * mod)
        gx_ref[...] = dx1_ref[...] + dx

    row_shape = _sds((bsz, 1, D_MODEL), F32)
    return _call(
        body, (dh1, dx1, x, pre_w, sc1), name="norm1_bwd", grid=(bsz, seq // tm),
        in_specs=[_tok_spec(tm), _tok_spec(tm), _tok_spec(tm), _vec_spec(), _row_spec()],
        out_specs=[_tok_spec(tm), _row_spec(), _row_spec(), _vec_spec()],
        out_shape=[_sds(x.shape, F32), row_shape, row_shape, _sds((1, D_MODEL), F32)],
        sem=("arbitrary", "arbitrary"), comms=comms)


def _hgrn_bwd(dcat, proj, o_raw, states, lb_table, norm_w, masks, comms=()):
    bsz, _, seq, _ = proj.shape
    nstep = seq // HG_TOK
    rec0 = ATT_WIDTH // LANE
    width = HG_HPS * LANE
    slabs = (HG_Q0, HG_F0, HG_I0, HG_G0)
    n_steps = (HG_HEADS // HG_HPS) * bsz * nstep
    assert n_steps >= 2

    def body(tbl_ref, nw_ref, dr_ref, p_ref, o_ref, st_ref, lower_ref, upper_ref,
             dproj_ref, dlb_ref, dnw_ref, ds_scr, grad_buf, grad_sem):
        h, b, t = pl.program_id(0), pl.program_id(1), pl.program_id(2)
        step = (h * bsz + b) * nstep + t
        slot = step % 2
        dq_k, df_k, di_k, dg_k = range(4)

        def grad_copies(of_step):
            hh, bb, tt = of_step // (bsz * nstep), (of_step // nstep) % bsz, of_step % nstep
            rows = pl.ds(pl.multiple_of((nstep - 1 - tt) * HG_TOK, HG_TOK), HG_TOK)
            return [pltpu.make_async_copy(
                grad_buf.at[of_step % 2, k],
                dproj_ref.at[bb, rows, pl.ds(pl.multiple_of(slabs[k] * LANE + hh * width, width), width)],
                grad_sem.at[of_step % 2, k]) for k in range(4)]

        @pl.when(step >= 2)
        def _():
            for cp in grad_copies(step - 2):
                cp.wait()

        @pl.when(t == 0)
        def _():
            ds_scr[...] = jnp.zeros_like(ds_scr)

        lower, upper = lower_ref[...], upper_ref[...]
        dlb_parts, dnw_parts = [None] * HG_HPS, [None] * HG_HPS

        def head(hp):
            ls = slice(hp * LANE, (hp + 1) * LANE)
            hq, v, hg = p_ref[HG_Q, :, ls], p_ref[HG_I, :, ls], p_ref[HG_G, :, ls]
            nw = nw_ref[...]
            c = _hgrn_common(tbl_ref[:, ls], p_ref[HG_F, :, ls], hq)
            qd, kd, ku = c["qd"], c["kd"], c["ku"]
            yield
            y, on, rstd = _rms_fwd(o_ref[:, ls], nw)
            sg = _sigmoid(hg)
            dr = dr_ref[:, ls]
            grad_buf[slot, dg_k, :, ls] = (dr * y * (sg * (1.0 + hg * (1.0 - sg)))).astype(BF16)
            do, dnw_rows = _rms_bwd(dr * (hg * sg), on, rstd, nw)
            yield
            at = _dot(kd, qd, NT_DIMS) * upper
            da = _dot(do, v, NT_DIMS) * lower
            dat = _dot(v, do, NT_DIMS) * upper
            yield
            dv = _dot(at, do)
            dqd = _dot(da, kd)
            dkd = _dot(dat, qd)
            yield
            do_c, qd_c, v_c, ku_c = [_chunk_slices(z.astype(BF16)) for z in (do, qd, v, ku)]
            outer = [_dot(do_c[j], qd_c[j], TN_DIMS) for j in range(HG_NCH)]
            yield
            ds = ds_scr[hp]
            ds_after = [None] * HG_NCH
            for j in reversed(range(HG_NCH)):
                ds_after[j] = ds
                ds = outer[j] + ds * c["e_last"][j]
            ds_scr[hp] = ds
            updates = [_dot(v_c[j], ku_c[j], TN_DIMS) for j in range(HG_NCH)]
            yield
            states = [st_ref[hp, 0]]
            for j in range(HG_NCH - 1):
                states.append(states[j] * c["e_last"][j] + updates[j])
            dv = dv + jnp.concatenate([_dot(ku_c[j], ds_after[j], NT_DIMS) for j in range(HG_NCH)], axis=0)
            dqd = dqd + jnp.concatenate([_dot(do_c[j], states[j]) for j in range(HG_NCH)], axis=0)
            dku = jnp.concatenate([_dot(v_c[j], ds_after[j]) for j in range(HG_NCH)], axis=0)
            yield
            dku_ku = dku * ku
            dbl = [_colsum(states[j] * ds_after[j]) * c["e_last"][j] + _colsum(dku_ku[j * HG_CHUNK:(j + 1) * HG_CHUNK])
                   for j in range(HG_NCH)]
            dk = dkd * c["e_nb"] + dku * c["e_rem"]
            db = dqd * qd - dkd * kd - dku_ku + jnp.where(_row_in_chunk() == HG_CHUNK - 1, _chunk_rows(dbl), 0.0)
            dfv = _chunk_cumsum(db, reverse=True) / c["f"] - dk
            sig, sq = c["sig"], c["sq"]
            grad_buf[slot, df_k, :, ls] = (dfv * (1.0 - c["lb"]) * sig * (1.0 - sig)).astype(BF16)
            grad_buf[slot, dq_k, :, ls] = (dqd * c["e_b"] * (sq * (1.0 + hq * (1.0 - sq)))).astype(BF16)
            grad_buf[slot, di_k, :, ls] = dv.astype(BF16)
            dlb_parts[hp] = _colsum(dfv * (1.0 - sig))
            dnw_parts[hp] = _colsum(dnw_rows)

        _in_step(head(hp) for hp in range(HG_HPS))
        _acc_out(dlb_ref, jnp.logical_and(b == 0, t == 0), jnp.concatenate(dlb_parts, axis=1))
        _acc_out(dnw_ref, jnp.logical_and(h == 0, jnp.logical_and(b == 0, t == 0)), sum(dnw_parts[1:], dnw_parts[0]))
        for cp in grad_copies(step):
            cp.start()

        @pl.when(step == n_steps - 1)
        def _():
            for cp in grad_copies(step - 1) + grad_copies(step):
                cp.wait()

    rev = lambda t: nstep - 1 - t
    slab = lambda first: pl.BlockSpec((None, HG_TOK, width), lambda h, b, t: (b, rev(t), first // HG_HPS + h))
    head = pl.BlockSpec((None, HG_TOK, width), lambda h, b, t: (b, rev(t), h))
    return _call(
        body, (lb_table, norm_w, dcat, proj, o_raw, states, *masks), name="hgrn_bwd",
        grid=(HG_HEADS // HG_HPS, bsz, nstep),
        in_specs=[pl.BlockSpec((2, width), lambda h, b, t: (0, h)), pl.BlockSpec((1, LANE), lambda h, b, t: (0, 0)),
                  slab(rec0), pl.BlockSpec((None, HG_SLABS, HG_TOK, width), lambda h, b, t: (b, 0, rev(t), h)), head,
                  pl.BlockSpec((None, HG_HPS, 1, LANE, LANE), lambda h, b, t: (b, h, rev(t), 0, 0)),
                  pl.BlockSpec((HG_TOK, HG_TOK), lambda h, b, t: (0, 0)), pl.BlockSpec((HG_TOK, HG_TOK), lambda h, b, t: (0, 0))],
        out_specs=[pl.BlockSpec(memory_space=pl.ANY), pl.BlockSpec((1, width), lambda h, b, t: (0, h)),
                   pl.BlockSpec((1, LANE), lambda h, b, t: (0, 0))],
        out_shape=[_sds((bsz, seq, IN_COLS), BF16), _sds((1, HG_WIDTH), F32), _sds((1, LANE), F32)],
        scratch_shapes=[pltpu.VMEM((HG_HPS, LANE, LANE), F32), pltpu.VMEM((2, 4, HG_TOK, width), BF16),
                        pltpu.SemaphoreType.DMA((2, 4))],
        sem=("arbitrary", "arbitrary", "arbitrary"), comms=comms)


def _attn_bwd(dcat, raw, w_norm, qh, kh, vh, lse, sinks, tables, biases, dproj, comms=()):
    bsz, nblk = qh.shape[0], qh.shape[1]
    seq = nblk * WINDOW
    nstep = nblk // ATT_BPS
    half = ROPE_DIM // 2

    def body(sink_ref, da_ref, raw_ref, w_ref, q_ref, kc_ref, kp_ref, vc_ref, vp_ref, l_ref, c_ref, u_ref, d_ref,
             full_ref, first_ref, dproj_ref, o_ref, dw_ref, dsink_ref, carry_k, carry_v):
        b, i = pl.program_id(0), pl.program_id(1)
        first = jnp.logical_and(b == 0, i == 0)

        @pl.when(i == 0)
        def _():
            carry_k[...] = jnp.zeros_like(carry_k)
            carry_v[...] = jnp.zeros_like(carry_v)

        w = w_ref[...]
        _, on, rstd = _rms_fwd(raw_ref[...], w)
        do_step, dw_rows = _rms_bwd(da_ref[...], on, rstd, w)
        _acc_out(dw_ref, first, _colsum(dw_rows))
        lane8 = lax.broadcasted_iota(jnp.int32, (1, ATT_Q_HEADS), 1)
        dsink = jnp.zeros((1, ATT_Q_HEADS), F32)
        head_cols = jnp.where(lax.broadcasted_iota(jnp.int32, (2 * ATT_Q_HEADS, ATT_WIDTH), 1) // ATT_HEAD_DIM
                              == lax.broadcasted_iota(jnp.int32, (2 * ATT_Q_HEADS, ATT_WIDTH), 0), 1.0, 0.0)
        from_next_k, from_next_v = carry_k[...], carry_v[...]
        for blk in reversed(range(ATT_BPS)):
            tok = slice(blk * WINDOW, (blk + 1) * WINDOW)
            bias = _band_bias(full_ref, first_ref, True if blk else i < nstep - 1)
            do_all = do_step[tok]
            c, u, d = c_ref[tok, :], u_ref[tok, :], d_ref[tok, :]
            lse_t = l_ref[tok, :].T
            prod = do_all * raw_ref[tok, :]
            prod_hi = prod.astype(BF16)
            prod_lo = prod - prod_hi.astype(F32)
            dsum_t = _dot(head_cols, prod_hi, NT_DIMS) + _dot(head_cols, prod_lo, NT_DIMS)

            def unrope(g):
                return (g * c + pltpu.roll(g * u, LANE - half, 1) + pltpu.roll(g * d, half, 1)).astype(BF16)

            groups = range(ATT_KV_HEADS)
            group_row = lambda z, g: jnp.concatenate(
                [z[g * ATT_GROUP + hh:g * ATT_GROUP + hh + 1, :] for hh in range(ATT_GROUP)], axis=1)
            q = [q_ref[blk, g] for g in groups]
            keys, vals = [_band(kp_ref, kc_ref, g, blk) for g in groups], [_band(vp_ref, vc_ref, g, blk) for g in groups]
            do_g = [jnp.concatenate([do_all[:, (g * ATT_GROUP + hh) * ATT_HEAD_DIM:(g * ATT_GROUP + hh + 1) * ATT_HEAD_DIM]
                                     for hh in range(ATT_GROUP)], axis=0) for g in groups]
            dsum, lse_g = [group_row(dsum_t, g) for g in groups], [group_row(lse_t, g) for g in groups]
            s_t = [_dot(keys[g], q[g], NT_DIMS) for g in groups]
            dp_t = [_dot(vals[g], do_g[g], NT_DIMS) for g in groups]
            p_t = [jnp.exp(s_t[g] * ATT_SCALE + bias - lse_g[g]) for g in groups]
            ds_t = [p_t[g] * (dp_t[g] - dsum[g]) * ATT_SCALE for g in groups]
            dq_g = [_dot(ds_t[g], keys[g], TN_DIMS) for g in groups]
            dk_g = [_dot(ds_t[g], q[g]) for g in groups]
            dv_g = [_dot(p_t[g], do_g[g]) for g in groups]
            for g in groups:
                sink_part = jnp.exp(_sink_row(sink_ref, g) - lse_g[g]) * dsum[g]
                for hh in range(ATT_GROUP):
                    head_sum = jnp.sum(sink_part[:, hh * WINDOW:(hh + 1) * WINDOW], axis=1, keepdims=True)
                    dsink = dsink - jnp.where(lane8 == g * ATT_GROUP + hh, head_sum, 0.0)
            dq_parts = [dq_g[g][hh * WINDOW:(hh + 1) * WINDOW] for g in groups for hh in range(ATT_GROUP)]
            dk_before, dk_own = [z[:WINDOW] for z in dk_g], [z[WINDOW:] for z in dk_g]
            dv_before, dv_own = [z[:WINDOW] for z in dv_g], [z[WINDOW:] for z in dv_g]
            per_slab = LANE // ATT_HEAD_DIM
            for s in range(ATT_WIDTH // LANE):
                slab = jnp.concatenate(dq_parts[s * per_slab:(s + 1) * per_slab], axis=1)
                o_ref[tok, s * LANE:(s + 1) * LANE] = unrope(slab)
            o_ref[tok, ATT_WIDTH:ATT_WIDTH + LANE] = unrope(jnp.concatenate(dk_own, axis=1) + from_next_k)
            o_ref[tok, ATT_WIDTH + LANE:ATT_COLS] = (jnp.concatenate(dv_own, axis=1) + from_next_v).astype(BF16)
            from_next_k, from_next_v = jnp.concatenate(dk_before, axis=1), jnp.concatenate(dv_before, axis=1)
        carry_k[...] = from_next_k
        carry_v[...] = from_next_v
        _acc_out(dsink_ref, first, dsink)

    rows = ATT_BPS * WINDOW
    rev = lambda i: nstep - 1 - i
    cur = lambda width: pl.BlockSpec((None, rows, width), lambda b, i: (b, rev(i), 0))
    q_spec = pl.BlockSpec((None, ATT_BPS, ATT_KV_HEADS, GROUP_ROWS, ATT_HEAD_DIM), lambda b, i: (b, rev(i), 0, 0, 0))
    kv_cur = pl.BlockSpec((None, ATT_KV_HEADS, rows, ATT_HEAD_DIM), lambda b, i: (b, 0, rev(i), 0))
    kv_prev = pl.BlockSpec((None, ATT_KV_HEADS, WINDOW, ATT_HEAD_DIM), lambda b, i: (b, 0, jnp.maximum(ATT_BPS * rev(i) - 1, 0), 0))
    tab = pl.BlockSpec((rows, LANE), lambda b, i: (rev(i), 0))
    return _call(
        body, (sinks, dcat, raw, w_norm, qh, kh, kh, vh, vh, lse, *tables, *biases, dproj), name="attn_bwd", grid=(bsz, nstep),
        in_specs=[pl.BlockSpec(memory_space=pltpu.SMEM), cur(ATT_WIDTH), cur(ATT_WIDTH), _vec_spec(ATT_WIDTH), q_spec,
                  kv_cur, kv_prev, kv_cur, kv_prev, cur(LANE), tab, tab, tab, _bias_spec(True), _bias_spec(True),
                  pl.BlockSpec(memory_space=pl.ANY)],
        out_specs=[cur(ATT_COLS), _vec_spec(ATT_WIDTH), _vec_spec(ATT_Q_HEADS)],
        out_shape=[_sds(dproj.shape, BF16), _sds((1, ATT_WIDTH), F32), _sds((1, ATT_Q_HEADS), F32)],
        scratch_shapes=[pltpu.VMEM((WINDOW, LANE), F32), pltpu.VMEM((WINDOW, LANE), F32)],
        sem=("arbitrary", "arbitrary"), comms=comms, aliases={15: 0})


def _other_chips(x, y):
    return [(1 - x, y), (x, 1 - y), (1 - x, 1 - y)]


def _sem_pair(n):
    return [pltpu.SemaphoreType.DMA((n,)), pltpu.SemaphoreType.DMA((n,))]


def _plan_pair_forward(bufs):
    n = len(bufs)

    def copies(outs, sems):
        x, y, c = _mesh_pos()
        sends, lands = [], []
        for a in range(n):
            for j, chip in enumerate(_other_chips(x, y)):
                k = 3 * a + j
                slot = outs[a].at[4 * chip[0] + 2 * chip[1] + c]
                sends.append(pltpu.make_async_remote_copy(
                    src_ref=slot, dst_ref=slot, send_sem=sems[0].at[k], recv_sem=sems[1].at[k],
                    device_id=(x, y, 1 - c), device_id_type=MESH))
                theirs = outs[a].at[4 * chip[0] + 2 * chip[1] + 1 - c]
                lands.append(pltpu.make_async_remote_copy(
                    src_ref=theirs, dst_ref=theirs, send_sem=sems[0].at[k], recv_sem=sems[1].at[k],
                    device_id=(x, y, 1 - c), device_id_type=MESH))
        return sends, lands

    def start(ins, outs, sems):
        for cp in copies(outs, sems)[0]:
            cp.start()

    def finish(ins, outs, sems):
        sends, lands = copies(outs, sems)
        for cp in lands:
            cp.wait_recv()
        for cp in sends:
            cp.wait_send()

    return _Comm(list(bufs), [_sds(b.shape, b.dtype) for b in bufs], _sem_pair(3 * n), start, finish,
                 aliases=[(a, a) for a in range(n)])


def _plan_pair(arrays, other_half):
    n = len(arrays)
    per = N_CHIPS if other_half == "chip_major" else 1

    def copies(ins, outs, sems):
        x, y, c = _mesh_pos()
        out = []
        for a in range(n):
            for k in range(per):
                if other_half == "chip_major":
                    src, dst = ins[a].at[k, 1 - c], outs[a].at[k]
                else:
                    src, dst = (ins[a].at[1 - c] if other_half else ins[a]), outs[a]
                out.append(pltpu.make_async_remote_copy(
                    src_ref=src, dst_ref=dst, send_sem=sems[0].at[per * a + k], recv_sem=sems[1].at[per * a + k],
                    device_id=(x, y, 1 - c), device_id_type=MESH))
        return out

    def start(ins, outs, sems):
        for cp in copies(ins, outs, sems):
            cp.start()

    def finish(ins, outs, sems):
        for cp in copies(ins, outs, sems):
            cp.wait()

    if other_half == "chip_major":
        shapes = [_sds((a.shape[0],) + a.shape[2:], a.dtype) for a in arrays]
    else:
        shapes = [_sds(a.shape[1:] if other_half else a.shape, a.dtype) for a in arrays]
    return _Comm(list(arrays), shapes, _sem_pair(per * n), start, finish)


def _plan_chip_exchange(arrays):
    n = len(arrays)

    def copies(ins, outs, sems):
        x, y, c = _mesh_pos()
        sends, lands = [], []
        for a in range(n):
            for j, chip in enumerate(_other_chips(x, y)):
                k = 3 * a + j
                sends.append(pltpu.make_async_remote_copy(
                    src_ref=ins[a].at[2 * chip[0] + chip[1]], dst_ref=outs[a].at[2 * x + y], send_sem=sems[0].at[k],
                    recv_sem=sems[1].at[k], device_id=(*chip, c), device_id_type=MESH))
                slot = outs[a].at[2 * chip[0] + chip[1]]
                lands.append(pltpu.make_async_remote_copy(
                    src_ref=slot, dst_ref=slot, send_sem=sems[0].at[k], recv_sem=sems[1].at[k],
                    device_id=(*chip, c), device_id_type=MESH))
        return sends, lands

    def start(ins, outs, sems):
        for cp in copies(ins, outs, sems)[0]:
            cp.start()

    def finish(ins, outs, sems):
        sends, lands = copies(ins, outs, sems)
        for cp in lands:
            cp.wait_recv()
        for cp in sends:
            cp.wait_send()

    return _Comm(list(arrays), [_sds(a.shape, a.dtype) for a in arrays], _sem_pair(3 * n), start, finish)


SEM_SPEC = pl.BlockSpec(memory_space=pltpu.SEMAPHORE)
N_OTHER = N_CHIPS - 1


def _exchange_copies(s_ref, land_ref, sems):
    x, y, c = _mesh_pos()
    return [pltpu.make_async_remote_copy(
        src_ref=s_ref.at[2 * chip[0] + chip[1]], dst_ref=land_ref.at[2 * x + y], send_sem=sems[j], recv_sem=sems[N_OTHER + j],
        device_id=(*chip, c), device_id_type=MESH) for j, chip in enumerate(_other_chips(x, y))]


def _exchange_start(s, name):
    def body(s_ref, land_ref, *outs):
        sems, token = outs[:2 * N_OTHER], outs[-1]
        for cp in _exchange_copies(s_ref, land_ref, sems):
            cp.start()
        token[...] = jnp.zeros_like(token)

    hbm = pltpu.HBM(s.shape, s.dtype)
    res = pl.pallas_call(
        body, name=name,
        out_shape=(pltpu.SemaphoreType.DMA(()),) * (2 * N_OTHER) + (hbm, hbm, _sds((SUBLANES, LANE), F32)),
        in_specs=(HBM_SPEC, HBM_SPEC),
        out_specs=(SEM_SPEC,) * (2 * N_OTHER) + (HBM_SPEC, HBM_SPEC, pl.BlockSpec(memory_space=pltpu.VMEM)),
        input_output_aliases={0: 2 * N_OTHER, 1: 2 * N_OTHER + 1},
        compiler_params=pltpu.CompilerParams(has_side_effects=pltpu.SideEffectType.DATAFLOW_SIDE_EFFECTING),
    )(pltpu.with_memory_space_constraint(s, pltpu.HBM), pltpu.with_memory_space_constraint(lax.empty(s.shape, s.dtype), pltpu.HBM))
    return res[:2 * N_OTHER], res[2 * N_OTHER], res[2 * N_OTHER + 1], res[-1]


def _exchange_wait(sems, s_thru, land_thru, afters, name):
    def body(s_ref, land_ref, *rest):
        for cp in _exchange_copies(s_ref, land_ref, rest[:2 * N_OTHER]):
            cp.wait_send()
            cp.wait_recv()

    hbm = pltpu.HBM(s_thru.shape, s_thru.dtype)
    return pl.pallas_call(
        body, name=name, out_shape=(hbm, hbm),
        in_specs=(HBM_SPEC, HBM_SPEC) + (SEM_SPEC,) * (2 * N_OTHER) + (pl.BlockSpec(memory_space=pl.ANY),) * len(afters),
        out_specs=(HBM_SPEC, HBM_SPEC), input_output_aliases={0: 0, 1: 1},
        compiler_params=pltpu.CompilerParams(has_side_effects=pltpu.SideEffectType.DATAFLOW_SIDE_EFFECTING),
    )(s_thru, land_thru, *sems, *afters)


def _gather_copies(block_ref, buf_ref, sems):
    x, y, c = _mesh_pos()
    return [pltpu.make_async_remote_copy(
        src_ref=block_ref, dst_ref=buf_ref.at[4 * x + 2 * y + c], send_sem=sems[j], recv_sem=sems[N_OTHER + j],
        device_id=(*chip, c), device_id_type=MESH) for j, chip in enumerate(_other_chips(x, y))]


def _gather_start(blocks, bufs, afters, name):
    n = len(blocks)
    per = 2 * N_OTHER

    def body(*refs):
        ins, outs = refs[:2 * n], refs[2 * n + len(afters):]
        for a in range(n):
            for cp in _gather_copies(ins[a], ins[n + a], outs[a * per:(a + 1) * per]):
                cp.start()
        outs[-1][...] = jnp.zeros_like(outs[-1])

    hbm = [pltpu.HBM(z.shape, z.dtype) for z in list(blocks) + list(bufs)]
    res = pl.pallas_call(
        body, name=name,
        out_shape=(pltpu.SemaphoreType.DMA(()),) * (n * per) + tuple(hbm) + (_sds((SUBLANES, LANE), F32),),
        in_specs=(HBM_SPEC,) * (2 * n) + (pl.BlockSpec(memory_space=pl.ANY),) * len(afters),
        out_specs=(SEM_SPEC,) * (n * per) + (HBM_SPEC,) * (2 * n) + (pl.BlockSpec(memory_space=pltpu.VMEM),),
        input_output_aliases={k: n * per + k for k in range(2 * n)},
        compiler_params=pltpu.CompilerParams(has_side_effects=pltpu.SideEffectType.DATAFLOW_SIDE_EFFECTING),
    )(*[pltpu.with_memory_space_constraint(z, pltpu.HBM) for z in list(blocks) + list(bufs)], *afters)
    parts = [(res[a * per:(a + 1) * per], res[n * per + a], res[n * per + n + a]) for a in range(n)]
    return parts, res[-1]


def _gather_wait(part, afters, name):
    sems, block, buf = part

    def body(block_ref, buf_ref, *rest):
        for cp in _gather_copies(block_ref, buf_ref, rest[:2 * N_OTHER]):
            cp.wait_send()
            cp.wait_recv()

    return pl.pallas_call(
        body, name=name, out_shape=(pltpu.HBM(block.shape, block.dtype), pltpu.HBM(buf.shape, buf.dtype)),
        in_specs=(HBM_SPEC, HBM_SPEC) + (SEM_SPEC,) * (2 * N_OTHER) + (pl.BlockSpec(memory_space=pl.ANY),) * len(afters),
        out_specs=(HBM_SPEC, HBM_SPEC), input_output_aliases={0: 0, 1: 1},
        compiler_params=pltpu.CompilerParams(has_side_effects=pltpu.SideEffectType.DATAFLOW_SIDE_EFFECTING),
    )(block, buf, *sems, *afters)[1]


def _comm_only(comms, name):
    return _call(lambda: None, (), name=name, grid=(), in_specs=[], out_specs=[], out_shape=[], sem=(), comms=comms)[1]


def _allgather8(arrays, name):
    return _comm_only([_plan_allgather8(arrays)], name)[0]


def _plan_allgather8(arrays):
    n = len(arrays)

    def parts(ins, outs, sems):
        send_sems, recv_sems, local_sems = sems
        x, y, c = _mesh_pos()
        me, sibling = (x, y, c), (x, y, 1 - c)
        chips = _other_chips(x, y)

        def copy(a, k, block, to, src=None):
            dst = outs[a].at[4 * block[0] + 2 * block[1] + block[2]]
            return pltpu.make_async_remote_copy(
                src_ref=dst if src is None else src, dst_ref=dst, send_sem=send_sems.at[7 * a + k],
                recv_sem=recv_sems.at[7 * a + k], device_id=to, device_id_type=MESH)

        mine = [pltpu.make_async_copy(ins[a], outs[a].at[4 * x + 2 * y + c], local_sems.at[a]) for a in range(n)]
        first = []
        for a in range(n):
            first.append(copy(a, 0, me, sibling, src=ins[a]))
            first += [copy(a, 1 + j, me, (*chip, c), src=ins[a]) for j, chip in enumerate(chips)]
        return copy, mine, first, me, sibling, chips, c

    def start(ins, outs, sems):
        _, mine, first, *_ = parts(ins, outs, sems)
        for cp in mine + first:
            cp.start()

    def finish(ins, outs, sems):
        copy, mine, first, me, sibling, chips, c = parts(ins, outs, sems)
        passed = []
        for j, chip in enumerate(chips):
            for a in range(n):
                copy(a, 1 + j, (*chip, c), me).wait_recv()
                fwd = copy(a, 4 + j, (*chip, c), sibling)
                fwd.start()
                passed.append(fwd)
        for a in range(n):
            copy(a, 0, sibling, me).wait_recv()
            for j, chip in enumerate(chips):
                copy(a, 4 + j, (*chip, 1 - c), me).wait_recv()
        for cp in first + passed:
            cp.wait_send()
        for cp in mine:
            cp.wait()

    sems = [pltpu.SemaphoreType.DMA((7 * n,)), pltpu.SemaphoreType.DMA((7 * n,)), pltpu.SemaphoreType.DMA((n,))]
    return _Comm(list(arrays), [_sds((N_DEV,) + a.shape, a.dtype) for a in arrays], sems, start, finish)


def _pair_sum(g, q, core, name, chip_major=False):
    rows, cols = g.shape[2:]
    tr = _row_tile(rows)

    def body(core_ref, g_ref, q_ref, o_ref):
        o_ref[...] = (g_ref[...] + q_ref[...]).astype(BF16)

    blk = pl.BlockSpec((None, tr, cols), lambda k, i, core_ref: (k, i, 0))
    if chip_major:
        own = pl.BlockSpec((None, None, tr, cols), lambda k, i, core_ref: (k, core_ref[0], i, 0))
    else:
        own = pl.BlockSpec((None, None, tr, cols), lambda k, i, core_ref: (core_ref[0], k, i, 0))
    return pl.pallas_call(
        body, name=name,
        grid_spec=pltpu.PrefetchScalarGridSpec(num_scalar_prefetch=1, grid=(N_CHIPS, rows // tr), in_specs=[own, blk], out_specs=blk),
        out_shape=_sds((N_CHIPS, rows, cols), BF16), compiler_params=_params("parallel", "parallel"),
    )(core, g, q)


def _sum_chips(own, landed, chip, name):
    _, rows, cols = own.shape
    tr = _row_tile(rows)

    def body(chip_ref, own_ref, a_ref, b_ref, c_ref, o_ref):
        acc = own_ref[...].astype(F32) + a_ref[...].astype(F32)
        o_ref[...] = (acc + b_ref[...].astype(F32)) + c_ref[...].astype(F32)

    blk = lambda flip: pl.BlockSpec((None, tr, cols), lambda i, chip_ref: (jnp.bitwise_xor(chip_ref[0], flip), i, 0))
    return pl.pallas_call(
        body, name=name,
        grid_spec=pltpu.PrefetchScalarGridSpec(num_scalar_prefetch=1, grid=(rows // tr,), in_specs=[blk(0), blk(1), blk(2), blk(3)],
                                               out_specs=pl.BlockSpec((tr, cols), lambda i, chip_ref: (i, 0))),
        out_shape=_sds((rows, cols), F32), compiler_params=_params("parallel"),
    )(chip, own, landed, landed, landed)


SUBLANES = 8


def _tile_rows(n_elems):
    return -(-n_elems // (SUBLANES * LANE)) * SUBLANES


SMALL_ITEMS = (("b_ada", N_MOD * D_MODEL), ("pre_w_mix", D_MODEL), ("post_w_mix", D_MODEL), ("pre_w_mlp", D_MODEL),
               ("post_w_mlp", D_MODEL), ("attn_out_w", ATT_WIDTH), ("hg_norm_w", HG_HEAD_DIM), ("attn_sinks", ATT_Q_HEADS),
               ("lb_0", HG_WIDTH), ("lb_1", HG_WIDTH))
SMALL_AT = {}
for _name, _size in SMALL_ITEMS:
    SMALL_AT[_name] = (sum(r for _, r in SMALL_AT.values()), _tile_rows(_size))
SMALL_ROWS = sum(r for _, r in SMALL_AT.values())
MOD_ROWS = SMALL_AT["b_ada"][1]
PLAIN_ROWS = SMALL_AT["lb_0"][0] - MOD_ROWS
LB_ROWS = SMALL_AT["lb_0"][1]


def _rows(a, nrows=None):
    flat = a.reshape(-1)
    nrows = _tile_rows(flat.shape[0]) if nrows is None else nrows
    return jnp.pad(flat, (0, nrows * LANE - flat.shape[0])).reshape(nrows, LANE)


def _pack_small(vals):
    vals = dict(vals, lb_0=vals["lb_table"][0], lb_1=vals["lb_table"][1])
    return jnp.concatenate([_rows(vals[name], SMALL_AT[name][1]) for name, _ in SMALL_ITEMS], axis=0)


def _unpack_small(p):
    def item(name, shape):
        first = SMALL_AT[name][0]
        size = shape[0] * shape[1]
        return p[first:first + SMALL_AT[name][1]].reshape(-1)[:size].reshape(shape)

    out = {name: item(name, (1, size)) for name, size in SMALL_ITEMS if not name.startswith("lb_")}
    out["lb_table"] = jnp.concatenate([item("lb_0", (1, HG_WIDTH)), item("lb_1", (1, HG_WIDTH))], axis=0)
    return out


def _pack_partials(dmod, plain, d_lb, loss_row):
    return jnp.concatenate([_rows(dmod, dmod.shape[0] * MOD_ROWS)] + [_rows(g) for g in plain] + [_rows(d_lb), _rows(loss_row)], axis=0)


def _small_update(packs, w, m, v, n_seq):
    mod_end = n_seq * MOD_ROWS
    lb_at = mod_end + PLAIN_ROWS
    t0, t1 = SMALL_AT["lb_0"][0], SMALL_AT["lb_1"][0]

    def body(p_ref, w_ref, m_ref, v_ref, g_ref, dl_ref, nm_ref, nv_ref, loss_ref):
        tot = p_ref[0]
        for d in range(1, N_DEV):
            tot = tot + p_ref[d]
        wv = w_ref[...]
        p1 = _sigmoid(wv[t1:t1 + LB_ROWS] - wv[t0:t0 + LB_ROWS])
        s = tot[lb_at:lb_at + LB_ROWS] * p1 * (1.0 - p1)
        g_bias = tot[0:MOD_ROWS]
        for q in range(1, n_seq):
            g_bias = g_bias + tot[q * MOD_ROWS:(q + 1) * MOD_ROWS]
        g = jnp.concatenate([g_bias, tot[mod_end:lb_at], -s, s], axis=0)
        g_ref[...] = g
        dl_ref[...], nm_ref[...], nv_ref[...] = _adamw_math(g, wv, m_ref[...], v_ref[...])
        loss_ref[...] = tot[lb_at + LB_ROWS:lb_at + LB_ROWS + SUBLANES]

    shp = _sds((SMALL_ROWS, LANE), F32)
    return pl.pallas_call(body, name="small_update", out_shape=[shp] * 4 + [_sds((SUBLANES, LANE), F32)],
                          compiler_params=_params())(packs, w, m, v)


def kernel(x, c, w_ada, b_ada, pre_w_mix, w_in, attn_sinks, attn_out_w, lb_table, hg_norm_w, w_out, post_w_mix, pre_w_mlp, w_up, w_down, post_w_mlp, loss_target, m_w_ada, m_b_ada, m_pre_w_mix, m_w_in, m_attn_sinks, m_attn_out_w, m_lb_table, m_hg_norm_w, m_w_out, m_post_w_mix, m_pre_w_mlp, m_w_up, m_w_down, m_post_w_mlp, v_w_ada, v_b_ada, v_pre_w_mix, v_w_in, v_attn_sinks, v_attn_out_w, v_lb_table, v_hg_norm_w, v_w_out, v_post_w_mix, v_pre_w_mlp, v_w_up, v_w_down, v_post_w_mlp):
    xi, yi, ci = _mesh_pos()
    chip = 2 * xi + yi
    dev = 2 * chip + ci
    bsz, seq, _ = x.shape
    ntok = bsz * seq
    ada_cols = w_ada.shape[2]
    core = jnp.reshape(ci, (1,)).astype(jnp.int32)
    chip_idx = jnp.reshape(chip, (1,)).astype(jnp.int32)
    flat = lambda a: a.reshape(ntok, a.shape[-1])
    unflat = lambda a: a.reshape(bsz, seq, a.shape[-1])
    tables = _rope_tables(seq)
    biases, chunk_masks = _band_biases(), _block_masks()

    def row_half(w):
        rows = w.shape[1] // 2
        return lax.dynamic_slice_in_dim(w[0], ci * rows, rows, axis=0).astype(BF16)

    def gather_buffer(w):
        rows, cols = w.shape[1] // 2, w.shape[2]
        own = w[0].astype(BF16).reshape(2, rows, cols)
        return lax.dynamic_update_slice(lax.empty((N_DEV, rows, cols), BF16), own, (2 * chip, 0, 0))

    w_in_t, m_in_t, v_in_t = [jnp.transpose(a[0])[None] for a in (w_in, m_w_in, v_w_in)]
    c_g, in_g = _allgather8([c, row_half(w_in_t)], "gather_first")
    c_all = c_g.reshape(N_DEV * bsz, D_MODEL)
    w_in_full = in_g.reshape(IN_COLS, D_MODEL)

    b_cols = lax.dynamic_slice_in_dim(b_ada, chip * ada_cols, ada_cols, axis=1)
    mod_part = _ada_fwd(c_all, w_ada[0], b_cols)
    half_rows = mod_part.shape[0] // 2
    (mod_g,) = _allgather8([lax.dynamic_slice_in_dim(mod_part, ci * half_rows, half_rows, axis=0)], "gather_mod")
    mod_all = mod_g.reshape(N_CHIPS, 2, half_rows, ada_cols).transpose(1, 2, 0, 3).reshape(N_DEV * bsz, N_MOD * D_MODEL)
    mod = lax.dynamic_slice_in_dim(mod_all, dev * bsz, bsz, axis=0)
    sh1, sc1, g1, sh2, sc2, g2 = [mod[:, i * D_MODEL:(i + 1) * D_MODEL].reshape(bsz, 1, D_MODEL) for i in range(N_MOD)]

    weights = (w_out, w_up, w_down)
    (out_part, up_part, down_part), started = _gather_start(
        [row_half(w) for w in weights], [gather_buffer(w) for w in weights], [mod_g], "gather_weights_start")

    h1, proj, qh, kh, vh = _in_proj_fused(x, pre_w_mix, sc1 + started[0:1, 0:1], sh1, w_in_full, tables)
    out_g = _gather_wait(out_part, [proj], "gather_out_wait")
    (attn_raw, cat, lse), ((out_g,),) = _attn_fwd(qh, kh, vh, attn_sinks, attn_out_w, biases, comms=[_plan_pair_forward([out_g])])
    up_g = _gather_wait(up_part, [attn_raw], "gather_up_wait")
    (o_raw, cat, states), ((up_g,),) = _hgrn_fwd(proj, lb_table, hg_norm_w, cat, chunk_masks, comms=[_plan_pair_forward([up_g])])
    down_g = _gather_wait(down_part, [o_raw], "gather_down_wait")
    w_out_full = out_g.reshape(D_MODEL, D_MODEL)
    w_up4 = up_g.reshape(N_CHIPS, D_MODEL, D_MODEL)
    mix, x1, h2 = _out_proj_fused(cat, w_out_full, x, post_w_mix, g1, pre_w_mlp, sc2, sh2)
    big_tm = min(ntok, 2048)
    up_spec = pl.BlockSpec((None, D_MODEL, D_MODEL), lambda i, j: (j, 0, 0))
    r, ((down_g,),) = _mm(flat(h2), w_up4, name="up_proj", out_dtype=BF16, tm=big_tm, tn=D_MODEL, n_out=D_FF, b_spec=up_spec,
                          epi=lambda acc: jnp.maximum(acc, 0.0), comms=[_plan_pair_forward([down_g])])
    w_down_full = down_g.reshape(D_FF, D_MODEL)
    square = lambda t: t * t
    loss_row, dy, dd, dg2, d_post_mlp = _down_proj_fused(unflat(r), w_down_full, x1, post_w_mlp, g2, loss_target)

    dpre = _mm(flat(dd), w_down_full, name="down_bwd", out_dtype=BF16, trans_b=True, tm=big_tm, tn=D_MODEL, extra=(r,),
               epi=lambda acc, rt: acc * (2.0 * rt.astype(F32)))
    half_rows = D_MODEL // 2
    g_down = _mm_tn(r, flat(dd), name="down_wgrad", tk=half_rows, tn=D_MODEL, a_fn=square,
                    out_shape=_sds((2, N_CHIPS, half_rows, D_MODEL), F32),
                    out_spec=pl.BlockSpec((None, None, half_rows, D_MODEL), lambda i, j: (i % 2, i // 2, 0, 0)))
    (dx1, dmix, dsc2, dsh2, dg1, d_pre_mlp, d_post_mix), ((q_down,),) = _up_bwd_fused(
        unflat(dpre), w_up4, dy, x1, mix, pre_w_mlp, sc2, post_w_mix, g1, comms=[_plan_pair([g_down], True)])
    g_up = _mm_tn(flat(h2), dpre, name="up_wgrad", tk=D_MODEL, tn=half_rows,
                  out_shape=_sds((2, N_CHIPS, half_rows, D_MODEL), F32),
                  out_spec=pl.BlockSpec((2, None, half_rows, half_rows), lambda i, j: (0, j // 2, 0, j % 2)))
    s_down = _pair_sum(g_down, q_down, core, "pair_sum_down")

    dcat, ((q_up,),) = _mm(flat(dmix), w_out_full, name="out_bwd", out_dtype=F32, trans_b=True, comms=[_plan_pair([g_up], True)])
    dcat = unflat(dcat)
    s_up = _pair_sum(g_up, q_up, core, "pair_sum_up")
    out_rows = D_MODEL // N_CHIPS
    g_out = _mm_tn(flat(cat), flat(dmix), name="out_wgrad", tk=2 * out_rows, tn=half_rows,
                   out_shape=_sds((2, N_CHIPS, out_rows, half_rows), F32),
                   out_spec=pl.BlockSpec((None, 2, out_rows, half_rows), lambda i, j: (j, i, 0, 0)))
    (dproj_rec, d_lb, d_hg_norm), ((x_down,), (q_out,)) = _hgrn_bwd(
        dcat, proj, o_raw, states, lb_table, hg_norm_w, chunk_masks, comms=[_plan_chip_exchange([s_down]), _plan_pair([g_out], True)])
    half_down = _sum_chips(s_down, x_down, chip_idx, "sum_chips_down")
    s_out = _pair_sum(g_out, q_out, core, "pair_sum_out")
    (dproj, d_attn_out, d_sinks), ((their_down,), (x_up,)) = _attn_bwd(
        dcat, attn_raw, attn_out_w, qh, kh, vh, lse, attn_sinks, tables, [bias.T for bias in biases], dproj_rec,
        comms=[_plan_pair([half_down], False), _plan_chip_exchange([s_up])])
    half_up = _sum_chips(s_up, x_up, chip_idx, "sum_chips_up")
    dproj = flat(dproj)
    in_rows = IN_COLS // N_CHIPS // 2
    g_in, ((x_out,),) = _mm_tn(dproj, flat(h1), name="in_wgrad", tk=2 * LANE, tn=D_MODEL, comms=[_plan_chip_exchange([s_out])])
    g_in = g_in.reshape(N_CHIPS, 2, in_rows, D_MODEL)
    half_out = _sum_chips(s_out, x_out, chip_idx, "sum_chips_out")
    dh1, ((q_in,), (their_up, their_out)) = _mm(
        dproj, w_in_full, name="in_bwd", out_dtype=F32,
        comms=[_plan_pair([g_in], "chip_major"), _plan_pair([half_up, half_out], False)])
    s_in = _pair_sum(g_in, q_in, core, "pair_sum_in", chip_major=True)
    in_sems, s_in, in_landing, started = _exchange_start(s_in, "exchange_in_start")
    grad_x, dsc1, dsh1, d_pre_mix = _norm1_bwd(unflat(dh1), dx1, x, pre_w_mix + started[0:1, 0:1], sc1)

    dmod = jnp.concatenate([dsh1, dsc1, dg1, dsh2, dsc2, dg2], axis=-1).reshape(bsz, N_MOD * D_MODEL)
    pack = _pack_partials(dmod, [d_pre_mix, d_post_mix, d_pre_mlp, d_post_mlp, d_attn_out, d_hg_norm, d_sinks], d_lb, loss_row)
    ((packs,),) = _comm_only([_plan_allgather8([pack])], "gather_small")
    w_small = dict(b_ada=b_ada, pre_w_mix=pre_w_mix, post_w_mix=post_w_mix, pre_w_mlp=pre_w_mlp, post_w_mlp=post_w_mlp,
                   attn_out_w=attn_out_w, hg_norm_w=hg_norm_w, attn_sinks=attn_sinks, lb_table=lb_table)
    m_small = dict(b_ada=m_b_ada, pre_w_mix=m_pre_w_mix, post_w_mix=m_post_w_mix, pre_w_mlp=m_pre_w_mlp, post_w_mlp=m_post_w_mlp,
                   attn_out_w=m_attn_out_w, hg_norm_w=m_hg_norm_w, attn_sinks=m_attn_sinks, lb_table=m_lb_table)
    v_small = dict(b_ada=v_b_ada, pre_w_mix=v_pre_w_mix, post_w_mix=v_post_w_mix, pre_w_mlp=v_pre_w_mlp, post_w_mlp=v_post_w_mlp,
                   attn_out_w=v_attn_out_w, hg_norm_w=v_hg_norm_w, attn_sinks=v_attn_sinks, lb_table=v_lb_table)
    *small_packed, loss_rows = _small_update(packs, _pack_small(w_small), _pack_small(m_small), _pack_small(v_small), bsz)
    small_out = [_unpack_small(p) for p in small_packed]
    loss = loss_rows[0, 0]

    dmod_all = packs[:, :bsz * MOD_ROWS, :].reshape(N_DEV * bsz, N_MOD * D_MODEL)
    dmod_cols = lax.dynamic_slice_in_dim(dmod_all, chip * ada_cols, ada_cols, axis=1)
    ada_out = _ada_bwd_adamw(c_all, dmod_cols, w_ada[0], m_w_ada[0], v_w_ada[0])

    s_in, x_in = _exchange_wait(in_sems, s_in, in_landing, [grad_x, ada_out[0]], "exchange_in_wait")
    half_in = _sum_chips(s_in, x_in, chip_idx, "sum_chips_in")
    ((their_in,),) = _comm_only([_plan_pair([half_in], False)], "pair_swap_in")
    big = dict(
        w_in=tuple(jnp.transpose(a) for a in _adamw_halves(half_in, their_in, core, w_in_t[0], m_in_t[0], v_in_t[0], axis=0,
                                                           name="adamw_in")),
        w_up=tuple(_adamw_halves(half_up, their_up, core, w_up[0], m_w_up[0], v_w_up[0], axis=0, name="adamw_up")),
        w_out=tuple(_adamw_halves(half_out, their_out, core, w_out[0], m_w_out[0], v_w_out[0], axis=1, name="adamw_out")),
        w_down=tuple(_adamw_halves(half_down, their_down, core, w_down[0], m_w_down[0], v_w_down[0], axis=0, name="adamw_down")),
        w_ada=tuple(ada_out),
    )
    order = ("w_ada", "b_ada", "pre_w_mix", "w_in", "attn_sinks", "attn_out_w", "lb_table", "hg_norm_w", "w_out", "post_w_mix",
             "pre_w_mlp", "w_up", "w_down", "post_w_mlp")
    outs = [loss, grad_x]
    for kind in range(4):
        for nm in order:
            outs.append(big[nm][kind][None] if nm in big else small_out[kind][nm])
    return tuple(outs)
```

```python
import jax
import jax.numpy as jnp
from jax import lax
from jax.experimental import pallas as pl
from jax.experimental.pallas import tpu as pltpu

F32 = jnp.float32
BF16 = jnp.bfloat16

D_MODEL = 1024
ATT_WIDTH = 512
ATT_HEAD_DIM = 64
ATT_Q_HEADS = 8
ATT_KV_HEADS = 2
ATT_GROUP = ATT_Q_HEADS // ATT_KV_HEADS
ATT_KV_COLS = ATT_KV_HEADS * ATT_HEAD_DIM
WINDOW = 128
ROPE_DIM = 16
ROPE_THETA = 500000.0
HG_WIDTH = 512
MIX_WIDTH = ATT_WIDTH + HG_WIDTH
HG_HEAD_DIM = 128
HG_HEADS = 4
HG_CHUNK = 32
IN_COLS = ATT_WIDTH + 2 * ATT_KV_COLS + 4 * HG_WIDTH
ATT_COLS = ATT_WIDTH + 2 * ATT_KV_COLS
D_FF = 4 * D_MODEL
N_MOD = 6
EPS = 1e-6
ATT_SCALE = ATT_HEAD_DIM ** -0.5

ADAM_LR = 0.001
ADAM_B1 = 0.9
ADAM_B2 = 0.999
ADAM_EPS = 1e-08
ADAM_WD = 0.01
ADAM_STEP = 10

N_CHIPS = 4
N_DEV = 8
LANE = 128
VMEM_LIMIT = 48 * 1024 * 1024
VMEM_LIMIT_BIG = 58 * 1024 * 1024
MESH = pl.DeviceIdType.MESH

NT_DIMS = (((1,), (1,)), ((), ()))
TN_DIMS = (((0,), (0,)), ((), ()))


def _sds(shape, dtype):
    return jax.ShapeDtypeStruct(tuple(shape), dtype)


def _params(*sem, vmem_limit=None):
    return pltpu.CompilerParams(dimension_semantics=sem, vmem_limit_bytes=VMEM_LIMIT if vmem_limit is None else vmem_limit)


def _sigmoid(x):
    return 1.0 / (1.0 + jnp.exp(-x))


def _dot(a, b, dims=None):
    a, b = a.astype(BF16), b.astype(BF16)
    if dims is None:
        return jnp.dot(a, b, preferred_element_type=F32)
    return lax.dot_general(a, b, dims, preferred_element_type=F32)


def _rms_fwd(x, w):
    rstd = lax.rsqrt(jnp.mean(x * x, axis=-1, keepdims=True) + EPS)
    xh = x * rstd
    return xh * w, xh, rstd


def _rms_bwd(dy, xh, rstd, w):
    dxh = dy * w
    dx = rstd * (dxh - xh * jnp.mean(dxh * xh, axis=-1, keepdims=True))
    return dx, dy * xh


def _colsum(x):
    return jnp.sum(x, axis=0, keepdims=True)


def _rms_hat(x):
    rstd = lax.rsqrt(jnp.mean(x * x, axis=-1, keepdims=True) + EPS)
    return x * rstd, rstd


def _rms_bwd_gain(dy, gain, xh, rstd):
    dxh = dy * gain
    dx = rstd * (dxh - xh * jnp.mean(dxh * xh, axis=-1, keepdims=True))
    return dx, _colsum(dy * xh)


def _row_tile(rows, cap=256):
    return max(t for t in range(16, cap + 1, 16) if rows % t == 0)


HBM_SPEC = pl.BlockSpec(memory_space=pltpu.HBM)


def _mesh_pos():
    return lax.axis_index("x"), lax.axis_index("y"), lax.axis_index("c")


class _Comm:
    def __init__(self, ins, outs, sems, start, finish, aliases=()):
        self.ins, self.outs, self.sems = list(ins), list(outs), list(sems)
        self.start, self.finish, self.aliases = start, finish, tuple(aliases)


def _call(body, args, *, name, grid, in_specs, out_specs, out_shape, sem, scratch_shapes=(), comms=(), aliases=None,
          vmem_limit=None):
    scratch_shapes = list(scratch_shapes)
    if not comms:
        return pl.pallas_call(body, name=name, grid=grid, in_specs=in_specs, out_specs=out_specs, out_shape=out_shape,
                              input_output_aliases=dict(aliases or {}), scratch_shapes=scratch_shapes,
                              compiler_params=_params(*sem, vmem_limit=vmem_limit))(*args)
    single = not isinstance(out_shape, (list, tuple))
    out_specs_l = [out_specs] if single else list(out_specs)
    out_shape_l = [out_shape] if single else list(out_shape)
    n_in, n_out, n_scr = len(in_specs), len(out_shape_l), len(scratch_shapes)
    n_ci = [len(cm.ins) for cm in comms]
    n_co = [len(cm.outs) for cm in comms]
    n_cs = [len(cm.sems) for cm in comms]
    aliases = dict(aliases or {})
    for k, cm in enumerate(comms):
        for i, o in cm.aliases:
            aliases[n_in + sum(n_ci[:k]) + i] = n_out + sum(n_co[:k]) + o

    def fused(*refs):
        pos = [0]

        def take(n):
            part = refs[pos[0]:pos[0] + n]
            pos[0] += n
            return part

        ins = take(n_in)
        c_ins = [take(n) for n in n_ci]
        outs = take(n_out)
        c_outs = [take(n) for n in n_co]
        scr = take(n_scr)
        c_sems = [take(n) for n in n_cs]
        first, last = True, True
        for d, size in enumerate(grid):
            first = jnp.logical_and(first, pl.program_id(d) == 0)
            last = jnp.logical_and(last, pl.program_id(d) == size - 1)

        def run(which):
            for cm, ci, co, cs in zip(comms, c_ins, c_outs, c_sems):
                getattr(cm, which)(ci, co, cs)

        if grid:
            pl.when(first)(lambda: run("start"))
        else:
            run("start")
        body(*ins, *outs, *scr)
        if grid:
            pl.when(last)(lambda: run("finish"))
        else:
            run("finish")

    res = pl.pallas_call(
        fused, name=name, grid=grid, in_specs=list(in_specs) + [HBM_SPEC] * sum(n_ci),
        out_specs=out_specs_l + [HBM_SPEC] * sum(n_co), out_shape=out_shape_l + [s for cm in comms for s in cm.outs],
        input_output_aliases=aliases, scratch_shapes=scratch_shapes + [s for cm in comms for s in cm.sems],
        compiler_params=_params(*["arbitrary"] * len(grid), vmem_limit=vmem_limit),
    )(*args, *[a for cm in comms for a in cm.ins])
    main = res[:n_out]
    extra, at = [], n_out
    for n in n_co:
        extra.append(list(res[at:at + n]))
        at += n
    return (main[0] if single else list(main)), extra


def _mm(a, b, *, name, out_dtype, trans_b=False, tm=512, tn=None, extra=(), epi=None, b_spec=None, n_out=None, comms=()):
    m_total, k_total = a.shape
    if n_out is None:
        n_out = b.shape[0] if trans_b else b.shape[1]
    tn = n_out if tn is None else tn
    grid = (m_total // tm, n_out // tn)
    dims = NT_DIMS if trans_b else None

    def body(*refs):
        a_ref, b_ref = refs[0], refs[1]
        extra_refs = refs[2:2 + len(extra)]
        o_ref = refs[2 + len(extra)]
        acc = _dot(a_ref[...], b_ref[...], dims)
        if epi is not None:
            acc = epi(acc, *[r[...] for r in extra_refs])
        o_ref[...] = acc.astype(out_dtype)

    if b_spec is None:
        if trans_b:
            b_spec = pl.BlockSpec((tn, k_total), lambda i, j: (j, 0))
        else:
            b_spec = pl.BlockSpec((k_total, tn), lambda i, j: (0, j))
    in_specs = [pl.BlockSpec((tm, k_total), lambda i, j: (i, 0)), b_spec]
    in_specs += [pl.BlockSpec((tm, tn), lambda i, j: (i, j)) for _ in extra]
    return _call(
        body, (a, b, *extra), name=name, grid=grid, in_specs=in_specs,
        out_specs=pl.BlockSpec((tm, tn), lambda i, j: (i, j)),
        out_shape=_sds((m_total, n_out), out_dtype),
        sem=("parallel", "parallel"), comms=comms)


def _mm_tn(a, b, *, name, tk, tn, a_fn=None, out_shape=None, out_spec=None, comms=()):
    m_total, k_total = a.shape
    n_total = b.shape[1]
    grid = (k_total // tk, n_total // tn)

    def body(a_ref, b_ref, o_ref):
        av = a_ref[...]
        part = _dot(av if a_fn is None else a_fn(av), b_ref[...], TN_DIMS)
        o_ref[...] = part.reshape(o_ref.shape)

    if out_shape is None:
        out_shape = _sds((k_total, n_total), F32)
        out_spec = pl.BlockSpec((tk, tn), lambda i, j: (i, j))
    return _call(
        body, (a, b), name=name, grid=grid,
        in_specs=[pl.BlockSpec((m_total, tk), lambda i, j: (0, i)), pl.BlockSpec((m_total, tn), lambda i, j: (0, j))],
        out_specs=out_spec, out_shape=out_shape, sem=("parallel", "parallel"), comms=comms)


def _ada_fwd(c_all, w_shard, b_shard):
    nb, ncol = c_all.shape[0], w_shard.shape[1]
    tn = 512

    def body(c_ref, w_ref, b_ref, o_ref):
        c = c_ref[...]
        o_ref[...] = _dot(c * _sigmoid(c), w_ref[...]) + b_ref[...]

    return pl.pallas_call(
        body, name="ada_fwd", grid=(ncol // tn,),
        in_specs=[pl.BlockSpec((nb, D_MODEL), lambda j: (0, 0)), pl.BlockSpec((D_MODEL, tn), lambda j: (0, j)),
                  pl.BlockSpec((1, tn), lambda j: (0, j))],
        out_specs=pl.BlockSpec((nb, tn), lambda j: (0, j)), out_shape=_sds((nb, ncol), F32),
        compiler_params=_params("parallel"),
    )(c_all, w_shard, b_shard)


def _adamw_math(g, w, m, v):
    m = ADAM_B1 * m + (1.0 - ADAM_B1) * g
    v = ADAM_B2 * v + (1.0 - ADAM_B2) * (g * g)
    m_hat = m / (1.0 - ADAM_B1 ** ADAM_STEP)
    v_hat = v / (1.0 - ADAM_B2 ** ADAM_STEP)
    delta = -ADAM_LR * (m_hat / (jnp.sqrt(v_hat) + ADAM_EPS) + ADAM_WD * w)
    return delta, m, v


def _ada_bwd_adamw(c_all, dmod_cols, w, m, v):
    nb, ncol = dmod_cols.shape
    tn = 256

    def body(c_ref, d_ref, w_ref, m_ref, v_ref, g_ref, dl_ref, nm_ref, nv_ref):
        c = c_ref[...]
        g = _dot(c * _sigmoid(c), d_ref[...], TN_DIMS)
        g_ref[...] = g
        dl_ref[...], nm_ref[...], nv_ref[...] = _adamw_math(g, w_ref[...], m_ref[...], v_ref[...])

    col = pl.BlockSpec((D_MODEL, tn), lambda j: (0, j))
    shp = _sds((D_MODEL, ncol), F32)
    return pl.pallas_call(
        body, name="ada_bwd_adamw", grid=(ncol // tn,),
        in_specs=[pl.BlockSpec((nb, D_MODEL), lambda j: (0, 0)), pl.BlockSpec((nb, tn), lambda j: (0, j)), col, col, col],
        out_specs=[col, col, col, col], out_shape=[shp, shp, shp, shp],
        compiler_params=_params("parallel"),
    )(c_all, dmod_cols, w, m, v)


def _adamw_halves(own, theirs, core, w, m, v, *, axis, name):
    r2, c2 = own.shape
    tr = _row_tile(r2)
    nt = r2 // tr

    def body(core_ref, own_ref, their_ref, w_ref, m_ref, v_ref, g_ref, dl_ref, nm_ref, nv_ref):
        g = jnp.where(pl.program_id(0) == core_ref[0], own_ref[...], their_ref[...])
        g_ref[...] = g
        dl_ref[...], nm_ref[...], nv_ref[...] = _adamw_math(g, w_ref[...], m_ref[...], v_ref[...])

    if axis == 0:
        full = pl.BlockSpec((tr, c2), lambda h, i, core_ref: (h * nt + i, 0))
    else:
        full = pl.BlockSpec((tr, c2), lambda h, i, core_ref: (i, h))
    half = pl.BlockSpec((tr, c2), lambda h, i, core_ref: (i, 0))
    shp = _sds(w.shape, F32)
    return pl.pallas_call(
        body, name=name,
        grid_spec=pltpu.PrefetchScalarGridSpec(num_scalar_prefetch=1, grid=(2, nt), in_specs=[half, half, full, full, full],
                                               out_specs=[full] * 4),
        out_shape=[shp] * 4, compiler_params=_params("parallel", "parallel"),
    )(core, own, theirs, w, m, v)


def _tok_spec(tm, width=D_MODEL):
    return pl.BlockSpec((None, tm, width), lambda b, i: (b, i, 0))


def _row_spec(width=D_MODEL):
    return pl.BlockSpec((None, 1, width), lambda b, i: (b, 0, 0))


def _vec_spec(width=D_MODEL):
    return pl.BlockSpec((1, width), lambda b, i: (0, 0))


class _RowsOf:
    def __init__(self, ref, first, count):
        self.ref, self.rows = ref, slice(first, first + count)

    def __getitem__(self, idx):
        return self.ref[self.rows, :]

    def __setitem__(self, idx, value):
        self.ref[self.rows, :] = value


def _mm_rows(a, b, *, name, tm, extra, extra_specs, out_specs, out_shape, epi, pro=None, trans_b=False, b_chunks=1, comms=(),
             parts=1, zero_per_seq=(), zero_once=(), vmem_limit=None):
    bsz, seq, k_total = a.shape
    kc = k_total // b_chunks
    dims = NT_DIMS if trans_b else None
    rows = tm // parts

    def body(*refs):
        a_ref, b_ref = refs[0], refs[1]
        ex, outs = refs[2:2 + len(extra)], refs[2 + len(extra):]
        if zero_per_seq:
            @pl.when(pl.program_id(1) == 0)
            def _():
                for k in zero_per_seq:
                    outs[k][...] = jnp.zeros_like(outs[k])
        if zero_once:
            @pl.when(jnp.logical_and(pl.program_id(0) == 0, pl.program_id(1) == 0))
            def _():
                for k in zero_once:
                    outs[k][...] = jnp.zeros_like(outs[k])

        def part_of(ref, p):
            tiled = len(ref.shape) == 2 and ref.shape[0] == tm
            return _RowsOf(ref, p * rows, rows) if tiled and parts > 1 else ref

        accs = []
        for p in range(parts):
            a_p, ex_p, outs_p = part_of(a_ref, p), [part_of(r, p) for r in ex], [part_of(r, p) for r in outs]
            if b_chunks == 1:
                accs.append(_dot(a_p[...] if pro is None else pro(a_p, ex_p, outs_p), b_ref[...], dims))
            else:
                acc = _dot(a_p[...][:, 0:kc], b_ref[0], NT_DIMS)
                for k in range(1, b_chunks):
                    acc = acc + _dot(a_p[...][:, k * kc:(k + 1) * kc], b_ref[k], NT_DIMS)
                accs.append(acc)
        for p in range(parts):
            epi(accs[p], [part_of(r, p) for r in ex], [part_of(r, p) for r in outs])

    b_spec = pl.BlockSpec(b.shape, lambda bb, i: (0,) * b.ndim)
    return _call(
        body, (a, b, *extra), name=name, grid=(bsz, seq // tm), in_specs=[_tok_spec(tm, k_total), b_spec, *extra_specs],
        out_specs=out_specs, out_shape=out_shape, sem=("arbitrary", "arbitrary"), comms=comms, vmem_limit=vmem_limit)


def _in_proj_fused(x, w, sc, sh, w_in_t, tables, comms=()):
    tm = 512
    bsz, seq, _ = x.shape
    half = ROPE_DIM // 2
    heads_per_slab = LANE // ATT_HEAD_DIM

    def pro(x_ref, ex, outs):
        y, _, _ = _rms_fwd(x_ref[...], ex[0][...])
        h = (y * (1.0 + ex[1][...]) + ex[2][...]).astype(BF16)
        outs[0][...] = h
        return h

    def epi(acc, ex, outs):
        c, u, d = ex[3][...], ex[4][...], ex[5][...]
        _, rec_ref, q_ref, k_ref, v_ref = outs
        for k in range(HG_SLABS):
            rec_ref[k] = acc[:, ATT_COLS + k * HG_WIDTH:ATT_COLS + (k + 1) * HG_WIDTH]

        def rope(z):
            return (z * c + pltpu.roll(z, half, 1) * u + pltpu.roll(z, LANE - half, 1) * d).astype(BF16)

        for s in range(ATT_WIDTH // LANE):
            slab = rope(acc[:, s * LANE:(s + 1) * LANE])
            for part in range(heads_per_slab):
                g, hh = divmod(s * heads_per_slab + part, ATT_GROUP)
                piece = slab[:, part * ATT_HEAD_DIM:(part + 1) * ATT_HEAD_DIM]
                for blk in range(tm // WINDOW):
                    q_ref[blk, g, hh * WINDOW:(hh + 1) * WINDOW, :] = piece[blk * WINDOW:(blk + 1) * WINDOW]
        rk = rope(acc[:, ATT_WIDTH:ATT_WIDTH + LANE])
        vv = acc[:, ATT_WIDTH + LANE:ATT_COLS].astype(BF16)
        for g in range(ATT_KV_HEADS):
            k_ref[g] = rk[:, g * ATT_HEAD_DIM:(g + 1) * ATT_HEAD_DIM]
            v_ref[g] = vv[:, g * ATT_HEAD_DIM:(g + 1) * ATT_HEAD_DIM]

    tab = pl.BlockSpec((tm, LANE), lambda b, i: (i, 0))
    kv_spec = pl.BlockSpec((None, ATT_KV_HEADS, tm, ATT_HEAD_DIM), lambda b, i: (b, 0, i, 0))
    kv_shape = _sds((bsz, ATT_KV_HEADS, seq, ATT_HEAD_DIM), BF16)
    q_spec = pl.BlockSpec((None, tm // WINDOW, ATT_KV_HEADS, GROUP_ROWS, ATT_HEAD_DIM), lambda b, i: (b, i, 0, 0, 0))
    return _mm_rows(x, w_in_t, name="in_proj", tm=tm, extra=(w, sc, sh, *tables),
                    extra_specs=[_vec_spec(), _row_spec(), _row_spec(), tab, tab, tab],
                    out_specs=[_tok_spec(tm), pl.BlockSpec((None, HG_SLABS, tm, HG_WIDTH), lambda b, i: (b, 0, i, 0)), q_spec,
                               kv_spec, kv_spec],
                    out_shape=[_sds(x.shape, BF16), _sds((bsz, HG_SLABS, seq, HG_WIDTH), F32),
                               _sds((bsz, seq // WINDOW, ATT_KV_HEADS, GROUP_ROWS, ATT_HEAD_DIM), BF16), kv_shape, kv_shape],
                    pro=pro, epi=epi, trans_b=True, comms=comms)


def _rope_tables(seq):
    half = ROPE_DIM // 2
    inv_freq = ROPE_THETA ** (-jnp.arange(0, ROPE_DIM, 2, dtype=F32) / ROPE_DIM)
    ang = jnp.arange(seq, dtype=F32)[:, None] * inv_freq[None, :]
    cos, sin = jnp.cos(ang), jnp.sin(ang)
    rest = ATT_HEAD_DIM - ROPE_DIM
    ones, zeros, zh = jnp.ones((seq, rest), F32), jnp.zeros((seq, rest), F32), jnp.zeros((seq, half), F32)
    reps = LANE // ATT_HEAD_DIM
    t_cos = jnp.tile(jnp.concatenate([cos, cos, ones], axis=1), (1, reps))
    t_up = jnp.tile(jnp.concatenate([zh, sin, zeros], axis=1), (1, reps))
    t_dn = jnp.tile(jnp.concatenate([-sin, zh, zeros], axis=1), (1, reps))
    return t_cos, t_up, t_dn


GROUP_ROWS = ATT_GROUP * WINDOW


ATT_BPS = 2


MASKED = -1e30


def _band_biases():
    row = jnp.arange(GROUP_ROWS)[:, None] % WINDOW
    col = jnp.arange(2 * WINDOW)[None, :]
    own = jnp.logical_and(col >= WINDOW, col - WINDOW <= row)
    before = jnp.logical_and(col < WINDOW, col > row)
    return (jnp.where(jnp.logical_or(own, before), 0.0, MASKED).astype(F32), jnp.where(own, 0.0, MASKED).astype(F32))


def _band_bias(full_ref, first_ref, has_prev):
    return full_ref[...] if has_prev is True else jnp.where(has_prev, full_ref[...], first_ref[...])


def _sink_column(sink_ref, g):
    head = lax.broadcasted_iota(jnp.int32, (GROUP_ROWS, 1), 0) // WINDOW
    col = jnp.full((GROUP_ROWS, 1), sink_ref[0, g * ATT_GROUP], F32)
    for hh in range(1, ATT_GROUP):
        col = jnp.where(head == hh, sink_ref[0, g * ATT_GROUP + hh], col)
    return col


def _sink_row(sink_ref, g):
    return jnp.concatenate([jnp.full((1, WINDOW), sink_ref[0, g * ATT_GROUP + hh], F32) for hh in range(ATT_GROUP)], axis=1)


def _bias_spec(transposed=False):
    shape = (2 * WINDOW, GROUP_ROWS) if transposed else (GROUP_ROWS, 2 * WINDOW)
    return pl.BlockSpec(shape, lambda b, i: (0, 0))


def _attn_specs():
    q_spec = pl.BlockSpec((None, ATT_BPS, ATT_KV_HEADS, GROUP_ROWS, ATT_HEAD_DIM), lambda b, i: (b, i, 0, 0, 0))
    kv_cur = pl.BlockSpec((None, ATT_KV_HEADS, ATT_BPS * WINDOW, ATT_HEAD_DIM), lambda b, i: (b, 0, i, 0))
    kv_prev = pl.BlockSpec((None, ATT_KV_HEADS, WINDOW, ATT_HEAD_DIM), lambda b, i: (b, 0, jnp.maximum(ATT_BPS * i - 1, 0), 0))
    return q_spec, kv_cur, kv_prev


def _band(prev_ref, cur_ref, g, blk):
    own = cur_ref[g, blk * WINDOW:(blk + 1) * WINDOW]
    before = prev_ref[g] if blk == 0 else cur_ref[g, (blk - 1) * WINDOW:blk * WINDOW]
    return jnp.concatenate([before, own], axis=0)


def _attn_fwd(qh, kh, vh, sinks, w_norm, biases, comms=()):
    bsz, nblk = qh.shape[0], qh.shape[1]
    seq = nblk * WINDOW
    rows = ATT_BPS * WINDOW

    def body(sink_ref, q_ref, kc_ref, kp_ref, vc_ref, vp_ref, w_ref, full_ref, first_ref, raw_ref, an_ref, l_ref):
        l_ref[...] = jnp.zeros_like(l_ref)
        def block(blk):
            bias = _band_bias(full_ref, first_ref, True if blk else pl.program_id(1) > 0)
            groups = range(ATT_KV_HEADS)
            keys, vals = [_band(kp_ref, kc_ref, g, blk) for g in groups], [_band(vp_ref, vc_ref, g, blk) for g in groups]
            sink = [_sink_column(sink_ref, g) for g in groups]
            s = [_dot(q_ref[blk, g], keys[g], NT_DIMS) * ATT_SCALE + bias for g in groups]
            yield
            m = [jnp.maximum(jnp.max(s[g], axis=-1, keepdims=True), sink[g]) for g in groups]
            p = [jnp.exp(s[g] - m[g]) for g in groups]
            den = [jnp.sum(p[g], axis=-1, keepdims=True) + jnp.exp(sink[g] - m[g]) for g in groups]
            yield
            o = [_dot(p[g] / den[g], vals[g]) for g in groups]
            yield
            lse = [m[g] + jnp.log(den[g]) for g in groups]
            tok = slice(blk * WINDOW, (blk + 1) * WINDOW)
            for g in groups:
                for hh in range(ATT_GROUP):
                    h = g * ATT_GROUP + hh
                    raw_ref[tok, h * ATT_HEAD_DIM:(h + 1) * ATT_HEAD_DIM] = o[g][hh * WINDOW:(hh + 1) * WINDOW]
                    l_ref[tok, h:h + 1] = lse[g][hh * WINDOW:(hh + 1) * WINDOW]

        _in_step(block(blk) for blk in range(ATT_BPS))
        y, _, _ = _rms_fwd(raw_ref[...], w_ref[...])
        an_ref[...] = y.astype(BF16)

    cur = lambda width: pl.BlockSpec((None, rows, width), lambda b, i: (b, i, 0))
    q_spec, kv_cur, kv_prev = _attn_specs()
    return _call(
        body, (sinks, qh, kh, kh, vh, vh, w_norm, *biases), name="attn_fwd", grid=(bsz, nblk // ATT_BPS),
        in_specs=[pl.BlockSpec(memory_space=pltpu.SMEM), q_spec, kv_cur, kv_prev, kv_cur, kv_prev, _vec_spec(ATT_WIDTH),
                  _bias_spec(), _bias_spec()],
        out_specs=[cur(ATT_WIDTH), cur(ATT_WIDTH), cur(LANE)],
        out_shape=[_sds((bsz, seq, ATT_WIDTH), F32), _sds((bsz, seq, MIX_WIDTH), BF16), _sds((bsz, seq, LANE), F32)],
        sem=("parallel", "parallel"), comms=comms)


HG_Q0 = ATT_COLS // LANE
HG_F0 = HG_Q0 + HG_HEADS
HG_I0 = HG_F0 + HG_HEADS
HG_G0 = HG_I0 + HG_HEADS
HG_SLABS = 4
HG_Q, HG_F, HG_I, HG_G = range(HG_SLABS)
HG_TOK = 256
HG_NCH = HG_TOK // HG_CHUNK
HG_HPS = 2


def _block_masks():
    row = jnp.arange(HG_TOK)[:, None]
    col = jnp.arange(HG_TOK)[None, :]
    same = (row // HG_CHUNK) == (col // HG_CHUNK)
    return jnp.logical_and(same, col <= row).astype(F32), jnp.logical_and(same, col >= row).astype(F32)


def _row_in_chunk():
    return lax.broadcasted_iota(jnp.int32, (HG_TOK, LANE), 0) % HG_CHUNK


def _chunk_cumsum(x, reverse=False):
    ric = _row_in_chunk()
    shift = 1
    while shift < HG_CHUNK:
        if reverse:
            x = x + jnp.where(ric < HG_CHUNK - shift, pltpu.roll(x, HG_TOK - shift, 0), 0.0)
        else:
            x = x + jnp.where(ric >= shift, pltpu.roll(x, shift, 0), 0.0)
        shift *= 2
    return x


def _chunk_rows(rows):
    stacked = jnp.concatenate([r[None] for r in rows], axis=0)
    return jnp.broadcast_to(stacked, (HG_NCH, HG_CHUNK, LANE)).reshape(HG_TOK, LANE)


def _chunk_slices(x):
    return [x[j * HG_CHUNK:(j + 1) * HG_CHUNK] for j in range(HG_NCH)]


def _in_step(stages):
    stages = list(stages)
    while stages:
        stages = [g for g in stages if next(g, stages) is not stages]


def _hgrn_common(tbl, hf, hq):
    lb = _sigmoid(tbl[1:2] - tbl[0:1])
    sig = _sigmoid(hf)
    f = lb + (1.0 - lb) * sig
    sq = _sigmoid(hq)
    q, k = hq * sq, 1.0 - f
    b = _chunk_cumsum(jnp.log(f))
    last = [b[(j + 1) * HG_CHUNK - 1:(j + 1) * HG_CHUNK] for j in range(HG_NCH)]
    bl = _chunk_rows(last)
    e_b, e_nb, e_rem = jnp.exp(b), jnp.exp(-b), jnp.exp(bl - b)
    e_last = [jnp.exp(r) for r in last]
    return dict(lb=lb, sig=sig, f=f, sq=sq, q=q, k=k, e_b=e_b, e_nb=e_nb, e_rem=e_rem, e_last=e_last,
                qd=q * e_b, kd=k * e_nb, ku=k * e_rem)


def _hgrn_fwd(proj, lb_table, norm_w, mix_in, masks, comms=()):
    bsz, _, seq, _ = proj.shape
    nstep = seq // HG_TOK

    def body(tbl_ref, nw_ref, p_ref, mix_ref, lower_ref, o_ref, rec_ref, st_ref, s_scr):
        @pl.when(pl.program_id(2) == 0)
        def _():
            s_scr[...] = jnp.zeros_like(s_scr)

        lower = lower_ref[...]

        def head(hp):
            ls = slice(hp * LANE, (hp + 1) * LANE)
            v, hg = p_ref[HG_I, :, ls], p_ref[HG_G, :, ls]
            t = _hgrn_common(tbl_ref[:, ls], p_ref[HG_F, :, ls], p_ref[HG_Q, :, ls])
            yield
            a = _dot(t["qd"], t["kd"], NT_DIMS) * lower
            o_intra = _dot(a, v)
            yield
            v_c, ku_c, qd_c = [_chunk_slices(z.astype(BF16)) for z in (v, t["ku"], t["qd"])]
            updates = [_dot(v_c[j], ku_c[j], TN_DIMS) for j in range(HG_NCH)]
            yield
            st = s_scr[hp]
            states = []
            for j in range(HG_NCH):
                states.append(st)
                st = st * t["e_last"][j] + updates[j]
            s_scr[hp] = st
            yield
            o = o_intra + jnp.concatenate([_dot(qd_c[j], states[j], NT_DIMS) for j in range(HG_NCH)], axis=0)
            yield
            st_ref[hp, 0] = states[0]
            o_ref[:, ls] = o
            y, _, _ = _rms_fwd(o, nw_ref[...])
            rec_ref[:, ls] = (y * (hg * _sigmoid(hg))).astype(BF16)

        _in_step(head(hp) for hp in range(HG_HPS))

    width = HG_HPS * LANE
    head_out = pl.BlockSpec((None, HG_TOK, width), lambda b, h, t: (b, t, h))
    mix_out = pl.BlockSpec((None, HG_TOK, width), lambda b, h, t: (b, t, ATT_WIDTH // width + h))
    return _call(
        body, (lb_table, norm_w, proj, mix_in, masks[0]), name="hgrn_fwd", grid=(bsz, HG_HEADS // HG_HPS, nstep),
        in_specs=[pl.BlockSpec((2, width), lambda b, h, t: (0, h)), pl.BlockSpec((1, LANE), lambda b, h, t: (0, 0)),
                  pl.BlockSpec((None, HG_SLABS, HG_TOK, width), lambda b, h, t: (b, 0, t, h)), pl.BlockSpec(memory_space=pl.ANY),
                  pl.BlockSpec((HG_TOK, HG_TOK), lambda b, h, t: (0, 0))],
        out_specs=[head_out, mix_out,
                   pl.BlockSpec((None, HG_HPS, 1, LANE, LANE), lambda b, h, t: (b, h, t, 0, 0))],
        out_shape=[_sds((bsz, seq, HG_WIDTH), F32), _sds(mix_in.shape, BF16),
                   _sds((bsz, HG_HEADS, nstep, LANE, LANE), F32)],
        scratch_shapes=[pltpu.VMEM((HG_HPS, LANE, LANE), F32)],
        sem=("parallel", "parallel", "arbitrary"), comms=comms, aliases={3: 1})


def _out_proj_fused(cat, w_out, x, post_w, g1, pre_w, sc2, sh2):
    tm = 512

    def epi(mix, ex, outs):
        x_ref, pw_ref, g1_ref, w2_ref, sc_ref, sh_ref = ex
        outs[0][...] = mix
        n1, _, _ = _rms_fwd(mix, pw_ref[...])
        x1 = x_ref[...] + g1_ref[...] * n1
        outs[1][...] = x1
        y2, _, _ = _rms_fwd(x1, w2_ref[...])
        outs[2][...] = (y2 * (1.0 + sc_ref[...]) + sh_ref[...]).astype(BF16)

    return _mm_rows(cat, w_out, name="out_proj", tm=tm, extra=(x, post_w, g1, pre_w, sc2, sh2),
                    extra_specs=[_tok_spec(tm), _vec_spec(), _row_spec(), _vec_spec(), _row_spec(), _row_spec()],
                    out_specs=[_tok_spec(tm), _tok_spec(tm), _tok_spec(tm)],
                    out_shape=[_sds(x.shape, F32), _sds(x.shape, F32), _sds(x.shape, BF16)], epi=epi)


def _acc_out(ref, first, value):
    @pl.when(first)
    def _():
        ref[...] = value

    @pl.when(jnp.logical_not(first))
    def _():
        ref[...] += value


def _down_proj_fused(r, w_down, x1, post_w, g2, target):
    tm = 512
    bsz = x1.shape[0]

    def pro(r_ref, ex, outs):
        rv = r_ref[...]
        return rv * rv

    def epi(down, ex, outs):
        x1_ref, w_ref, g2_ref, t_ref = ex
        loss_ref, dy_ref, dd_ref, dg2_ref, dw_ref = outs
        w, g2v = w_ref[...], g2_ref[...]
        gain = g2v * w
        dh, rstd = _rms_hat(down)
        err = x1_ref[...] + dh * gain - t_ref[...]
        part = (0.5 / D_MODEL) * jnp.sum(jnp.sum(err * err, axis=-1, keepdims=True), axis=0, keepdims=True)
        loss_ref[...] += jnp.broadcast_to(part, (1, LANE))
        dy = err * (1.0 / D_MODEL)
        dy_ref[...] = dy
        dd, per_col = _rms_bwd_gain(dy, gain, dh, rstd)
        dd_ref[...] = dd.astype(BF16)
        dg2_ref[...] += per_col * w
        dw_ref[...] += per_col * g2v

    return _mm_rows(r, w_down, name="down_proj", tm=tm, extra=(x1, post_w, g2, target),
                    extra_specs=[_tok_spec(tm), _vec_spec(), _row_spec(), _tok_spec(tm)],
                    out_specs=[_vec_spec(LANE), _tok_spec(tm), _tok_spec(tm), _row_spec(), _vec_spec()],
                    out_shape=[_sds((1, LANE), F32), _sds(x1.shape, F32), _sds(x1.shape, BF16), _sds((bsz, 1, D_MODEL), F32),
                               _sds((1, D_MODEL), F32)], pro=pro, epi=epi, parts=2, zero_per_seq=(3,), zero_once=(0, 4),
                    vmem_limit=VMEM_LIMIT_BIG)


def _up_bwd_fused(dpre, w_up4, dy, x1, mix, pre_w, sc2, post_w, g1, comms=()):
    tm = 512
    bsz = x1.shape[0]

    def epi(dh2v, ex, outs):
        dy_ref, x1_ref, mix_ref, w2_ref, sc_ref, pw_ref, g1_ref = ex
        dx1_ref, dmix_ref, dsc_ref, dsh_ref, dg1_ref, dw2_ref, dpw_ref = outs
        w2, pw, g1v = w2_ref[...], pw_ref[...], g1_ref[...]
        mod2 = 1.0 + sc_ref[...]
        xh2, rstd2 = _rms_hat(x1_ref[...])
        dsh_ref[...] += _colsum(dh2v)
        dx1n, per_col2 = _rms_bwd_gain(dh2v, mod2 * w2, xh2, rstd2)
        dsc_ref[...] += per_col2 * w2
        dw2_ref[...] += per_col2 * mod2
        dx1 = dy_ref[...] + dx1n
        dx1_ref[...] = dx1
        mh, rstd1 = _rms_hat(mix_ref[...])
        dmix, per_col1 = _rms_bwd_gain(dx1, g1v * pw, mh, rstd1)
        dmix_ref[...] = dmix.astype(BF16)
        dg1_ref[...] += per_col1 * pw
        dpw_ref[...] += per_col1 * g1v

    row_shape = _sds((bsz, 1, D_MODEL), F32)
    vec_shape = _sds((1, D_MODEL), F32)
    return _mm_rows(dpre, w_up4, name="up_bwd", tm=tm, extra=(dy, x1, mix, pre_w, sc2, post_w, g1),
                    extra_specs=[_tok_spec(tm), _tok_spec(tm), _tok_spec(tm), _vec_spec(), _row_spec(), _vec_spec(), _row_spec()],
                    out_specs=[_tok_spec(tm), _tok_spec(tm), _row_spec(), _row_spec(), _row_spec(), _vec_spec(), _vec_spec()],
                    out_shape=[_sds(x1.shape, F32), _sds(x1.shape, BF16), row_shape, row_shape, row_shape, vec_shape, vec_shape],
                    epi=epi, b_chunks=w_up4.shape[0], comms=comms, parts=2, zero_per_seq=(2, 3, 4), zero_once=(5, 6),
                    vmem_limit=VMEM_LIMIT_BIG)


def _norm1_bwd(dh1, dx1, x, pre_w, sc1, tm=512, comms=()):
    bsz, seq, _ = x.shape

    def body(dh_ref, dx1_ref, x_ref, w_ref, sc_ref, gx_ref, dsc_ref, dsh_ref, dw_ref):
        b, i = pl.program_id(0), pl.program_id(1)
        w = w_ref[...]
        dh = dh_ref[...]
        mod = 1.0 + sc_ref[...]
        xh, rstd = _rms_hat(x_ref[...])
        dx, per_col = _rms_bwd_gain(dh, mod * w, xh, rstd)
        _acc_out(dsh_ref, i == 0, _colsum(dh))
        _acc_out(dsc_ref, i == 0, per_col * w)
        _acc_out(dw_ref, jnp.logical_and(b == 0, i == 0), per_col * mod)
        gx_ref[...] = dx1_ref[...] + dx

    row_shape = _sds((bsz, 1, D_MODEL), F32)
    return _call(
        body, (dh1, dx1, x, pre_w, sc1), name="norm1_bwd", grid=(bsz, seq // tm),
        in_specs=[_tok_spec(tm), _tok_spec(tm), _tok_spec(tm), _vec_spec(), _row_spec()],
        out_specs=[_tok_spec(tm), _row_spec(), _row_spec(), _vec_spec()],
        out_shape=[_sds(x.shape, F32), row_shape, row_shape, _sds((1, D_MODEL), F32)],
        sem=("arbitrary", "arbitrary"), comms=comms)


def _hgrn_bwd(dcat, proj, o_raw, states, lb_table, norm_w, masks, comms=()):
    bsz, _, seq, _ = proj.shape
    nstep = seq // HG_TOK
    rec0 = ATT_WIDTH // LANE
    width = HG_HPS * LANE
    slabs = (HG_Q0, HG_F0, HG_I0, HG_G0)
    n_steps = (HG_HEADS // HG_HPS) * bsz * nstep
    assert n_steps >= 2

    def body(tbl_ref, nw_ref, dr_ref, p_ref, o_ref, st_ref, lower_ref, upper_ref,
             dproj_ref, dlb_ref, dnw_ref, ds_scr, grad_buf, grad_sem):
        h, b, t = pl.program_id(0), pl.program_id(1), pl.program_id(2)
        step = (h * bsz + b) * nstep + t
        slot = step % 2
        dq_k, df_k, di_k, dg_k = range(4)

        def grad_copies(of_step):
            hh, bb, tt = of_step // (bsz * nstep), (of_step // nstep) % bsz, of_step % nstep
            rows = pl.ds(pl.multiple_of((nstep - 1 - tt) * HG_TOK, HG_TOK), HG_TOK)
            return [pltpu.make_async_copy(
                grad_buf.at[of_step % 2, k],
                dproj_ref.at[bb, rows, pl.ds(pl.multiple_of(slabs[k] * LANE + hh * width, width), width)],
                grad_sem.at[of_step % 2, k]) for k in range(4)]

        @pl.when(step >= 2)
        def _():
            for cp in grad_copies(step - 2):
                cp.wait()

        @pl.when(t == 0)
        def _():
            ds_scr[...] = jnp.zeros_like(ds_scr)

        lower, upper = lower_ref[...], upper_ref[...]
        dlb_parts, dnw_parts = [None] * HG_HPS, [None] * HG_HPS

        def head(hp):
            ls = slice(hp * LANE, (hp + 1) * LANE)
            hq, v, hg = p_ref[HG_Q, :, ls], p_ref[HG_I, :, ls], p_ref[HG_G, :, ls]
            nw = nw_ref[...]
            c = _hgrn_common(tbl_ref[:, ls], p_ref[HG_F, :, ls], hq)
            qd, kd, ku = c["qd"], c["kd"], c["ku"]
            yield
            y, on, rstd = _rms_fwd(o_ref[:, ls], nw)
            sg = _sigmoid(hg)
            dr = dr_ref[:, ls]
            grad_buf[slot, dg_k, :, ls] = (dr * y * (sg * (1.0 + hg * (1.0 - sg)))).astype(BF16)
            do, dnw_rows = _rms_bwd(dr * (hg * sg), on, rstd, nw)
            yield
            at = _dot(kd, qd, NT_DIMS) * upper
            da = _dot(do, v, NT_DIMS) * lower
            dat = _dot(v, do, NT_DIMS) * upper
            yield
            dv = _dot(at, do)
            dqd = _dot(da, kd)
            dkd = _dot(dat, qd)
            yield
            do_c, qd_c, v_c, ku_c = [_chunk_slices(z.astype(BF16)) for z in (do, qd, v, ku)]
            outer = [_dot(do_c[j], qd_c[j], TN_DIMS) for j in range(HG_NCH)]
            yield
            ds = ds_scr[hp]
            ds_after = [None] * HG_NCH
            for j in reversed(range(HG_NCH)):
                ds_after[j] = ds
                ds = outer[j] + ds * c["e_last"][j]
            ds_scr[hp] = ds
            yield
            updates = [_dot(v_c[j], ku_c[j], TN_DIMS) for j in range(HG_NCH)]
            yield
            states = [st_ref[hp, 0]]
            for j in range(HG_NCH - 1):
                states.append(states[j] * c["e_last"][j] + updates[j])
            dv = dv + jnp.concatenate([_dot(ku_c[j], ds_after[j], NT_DIMS) for j in range(HG_NCH)], axis=0)
            dqd = dqd + jnp.concatenate([_dot(do_c[j], states[j]) for j in range(HG_NCH)], axis=0)
            dku = jnp.concatenate([_dot(v_c[j], ds_after[j]) for j in range(HG_NCH)], axis=0)
            yield
            dku_ku = dku * ku
            dbl = [_colsum(states[j] * ds_after[j]) * c["e_last"][j] + _colsum(dku_ku[j * HG_CHUNK:(j + 1) * HG_CHUNK])
                   for j in range(HG_NCH)]
            dk = dkd * c["e_nb"] + dku * c["e_rem"]
            db = dqd * qd - dkd * kd - dku_ku + jnp.where(_row_in_chunk() == HG_CHUNK - 1, _chunk_rows(dbl), 0.0)
            dfv = _chunk_cumsum(db, reverse=True) / c["f"] - dk
            sig, sq = c["sig"], c["sq"]
            grad_buf[slot, df_k, :, ls] = (dfv * (1.0 - c["lb"]) * sig * (1.0 - sig)).astype(BF16)
            grad_buf[slot, dq_k, :, ls] = (dqd * c["e_b"] * (sq * (1.0 + hq * (1.0 - sq)))).astype(BF16)
            grad_buf[slot, di_k, :, ls] = dv.astype(BF16)
            dlb_parts[hp] = _colsum(dfv * (1.0 - sig))
            dnw_parts[hp] = _colsum(dnw_rows)

        _in_step(head(hp) for hp in range(HG_HPS))
        _acc_out(dlb_ref, jnp.logical_and(b == 0, t == 0), jnp.concatenate(dlb_parts, axis=1))
        _acc_out(dnw_ref, jnp.logical_and(h == 0, jnp.logical_and(b == 0, t == 0)), sum(dnw_parts[1:], dnw_parts[0]))
        for cp in grad_copies(step):
            cp.start()

        @pl.when(step == n_steps - 1)
        def _():
            for cp in grad_copies(step - 1) + grad_copies(step):
                cp.wait()

    rev = lambda t: nstep - 1 - t
    slab = lambda first: pl.BlockSpec((None, HG_TOK, width), lambda h, b, t: (b, rev(t), first // HG_HPS + h))
    head = pl.BlockSpec((None, HG_TOK, width), lambda h, b, t: (b, rev(t), h))
    return _call(
        body, (lb_table, norm_w, dcat, proj, o_raw, states, *masks), name="hgrn_bwd",
        grid=(HG_HEADS // HG_HPS, bsz, nstep),
        in_specs=[pl.BlockSpec((2, width), lambda h, b, t: (0, h)), pl.BlockSpec((1, LANE), lambda h, b, t: (0, 0)),
                  slab(rec0), pl.BlockSpec((None, HG_SLABS, HG_TOK, width), lambda h, b, t: (b, 0, rev(t), h)), head,
                  pl.BlockSpec((None, HG_HPS, 1, LANE, LANE), lambda h, b, t: (b, h, rev(t), 0, 0)),
                  pl.BlockSpec((HG_TOK, HG_TOK), lambda h, b, t: (0, 0)), pl.BlockSpec((HG_TOK, HG_TOK), lambda h, b, t: (0, 0))],
        out_specs=[pl.BlockSpec(memory_space=pl.ANY), pl.BlockSpec((1, width), lambda h, b, t: (0, h)),
                   pl.BlockSpec((1, LANE), lambda h, b, t: (0, 0))],
        out_shape=[_sds((bsz, seq, IN_COLS), BF16), _sds((1, HG_WIDTH), F32), _sds((1, LANE), F32)],
        scratch_shapes=[pltpu.VMEM((HG_HPS, LANE, LANE), F32), pltpu.VMEM((2, 4, HG_TOK, width), BF16),
                        pltpu.SemaphoreType.DMA((2, 4))],
        sem=("arbitrary", "arbitrary", "arbitrary"), comms=comms)


def _attn_bwd(dcat, raw, w_norm, qh, kh, vh, lse, sinks, tables, biases, dproj, comms=()):
    bsz, nblk = qh.shape[0], qh.shape[1]
    seq = nblk * WINDOW
    nstep = nblk // ATT_BPS
    half = ROPE_DIM // 2

    def body(sink_ref, da_ref, raw_ref, w_ref, q_ref, kc_ref, kp_ref, vc_ref, vp_ref, l_ref, c_ref, u_ref, d_ref,
             full_ref, first_ref, dproj_ref, o_ref, dw_ref, dsink_ref, carry_k, carry_v):
        b, i = pl.program_id(0), pl.program_id(1)
        first = jnp.logical_and(b == 0, i == 0)

        @pl.when(i == 0)
        def _():
            carry_k[...] = jnp.zeros_like(carry_k)
            carry_v[...] = jnp.zeros_like(carry_v)

        w = w_ref[...]
        _, on, rstd = _rms_fwd(raw_ref[...], w)
        do_step, dw_rows = _rms_bwd(da_ref[...], on, rstd, w)
        _acc_out(dw_ref, first, _colsum(dw_rows))
        lane8 = lax.broadcasted_iota(jnp.int32, (1, ATT_Q_HEADS), 1)
        dsink = jnp.zeros((1, ATT_Q_HEADS), F32)
        head_cols = jnp.where(lax.broadcasted_iota(jnp.int32, (2 * ATT_Q_HEADS, ATT_WIDTH), 1) // ATT_HEAD_DIM
                              == lax.broadcasted_iota(jnp.int32, (2 * ATT_Q_HEADS, ATT_WIDTH), 0), 1.0, 0.0)
        from_next_k, from_next_v = carry_k[...], carry_v[...]
        for blk in reversed(range(ATT_BPS)):
            tok = slice(blk * WINDOW, (blk + 1) * WINDOW)
            bias = _band_bias(full_ref, first_ref, True if blk else i < nstep - 1)
            do_all = do_step[tok]
            c, u, d = c_ref[tok, :], u_ref[tok, :], d_ref[tok, :]
            lse_t = l_ref[tok, :].T
            prod = do_all * raw_ref[tok, :]
            prod_hi = prod.astype(BF16)
            prod_lo = prod - prod_hi.astype(F32)
            dsum_t = _dot(head_cols, prod_hi, NT_DIMS) + _dot(head_cols, prod_lo, NT_DIMS)

            def unrope(g):
                return (g * c + pltpu.roll(g * u, LANE - half, 1) + pltpu.roll(g * d, half, 1)).astype(BF16)

            groups = range(ATT_KV_HEADS)
            group_row = lambda z, g: jnp.concatenate(
                [z[g * ATT_GROUP + hh:g * ATT_GROUP + hh + 1, :] for hh in range(ATT_GROUP)], axis=1)
            q = [q_ref[blk, g] for g in groups]
            keys, vals = [_band(kp_ref, kc_ref, g, blk) for g in groups], [_band(vp_ref, vc_ref, g, blk) for g in groups]
            do_g = [jnp.concatenate([do_all[:, (g * ATT_GROUP + hh) * ATT_HEAD_DIM:(g * ATT_GROUP + hh + 1) * ATT_HEAD_DIM]
                                     for hh in range(ATT_GROUP)], axis=0) for g in groups]
            dsum, lse_g = [group_row(dsum_t, g) for g in groups], [group_row(lse_t, g) for g in groups]
            s_t = [_dot(keys[g], q[g], NT_DIMS) for g in groups]
            dp_t = [_dot(vals[g], do_g[g], NT_DIMS) for g in groups]
            p_t = [jnp.exp(s_t[g] * ATT_SCALE + bias - lse_g[g]) for g in groups]
            ds_t = [p_t[g] * (dp_t[g] - dsum[g]) * ATT_SCALE for g in groups]
            dq_g = [_dot(ds_t[g], keys[g], TN_DIMS) for g in groups]
            dk_g = [_dot(ds_t[g], q[g]) for g in groups]
            dv_g = [_dot(p_t[g], do_g[g]) for g in groups]
            for g in groups:
                sink_part = jnp.exp(_sink_row(sink_ref, g) - lse_g[g]) * dsum[g]
                for hh in range(ATT_GROUP):
                    head_sum = jnp.sum(sink_part[:, hh * WINDOW:(hh + 1) * WINDOW], axis=1, keepdims=True)
                    dsink = dsink - jnp.where(lane8 == g * ATT_GROUP + hh, head_sum, 0.0)
            dq_parts = [dq_g[g][hh * WINDOW:(hh + 1) * WINDOW] for g in groups for hh in range(ATT_GROUP)]
            dk_before, dk_own = [z[:WINDOW] for z in dk_g], [z[WINDOW:] for z in dk_g]
            dv_before, dv_own = [z[:WINDOW] for z in dv_g], [z[WINDOW:] for z in dv_g]
            per_slab = LANE // ATT_HEAD_DIM
            for s in range(ATT_WIDTH // LANE):
                slab = jnp.concatenate(dq_parts[s * per_slab:(s + 1) * per_slab], axis=1)
                o_ref[tok, s * LANE:(s + 1) * LANE] = unrope(slab)
            o_ref[tok, ATT_WIDTH:ATT_WIDTH + LANE] = unrope(jnp.concatenate(dk_own, axis=1) + from_next_k)
            o_ref[tok, ATT_WIDTH + LANE:ATT_COLS] = (jnp.concatenate(dv_own, axis=1) + from_next_v).astype(BF16)
            from_next_k, from_next_v = jnp.concatenate(dk_before, axis=1), jnp.concatenate(dv_before, axis=1)
        carry_k[...] = from_next_k
        carry_v[...] = from_next_v
        _acc_out(dsink_ref, first, dsink)

    rows = ATT_BPS * WINDOW
    rev = lambda i: nstep - 1 - i
    cur = lambda width: pl.BlockSpec((None, rows, width), lambda b, i: (b, rev(i), 0))
    q_spec = pl.BlockSpec((None, ATT_BPS, ATT_KV_HEADS, GROUP_ROWS, ATT_HEAD_DIM), lambda b, i: (b, rev(i), 0, 0, 0))
    kv_cur = pl.BlockSpec((None, ATT_KV_HEADS, rows, ATT_HEAD_DIM), lambda b, i: (b, 0, rev(i), 0))
    kv_prev = pl.BlockSpec((None, ATT_KV_HEADS, WINDOW, ATT_HEAD_DIM), lambda b, i: (b, 0, jnp.maximum(ATT_BPS * rev(i) - 1, 0), 0))
    tab = pl.BlockSpec((rows, LANE), lambda b, i: (rev(i), 0))
    return _call(
        body, (sinks, dcat, raw, w_norm, qh, kh, kh, vh, vh, lse, *tables, *biases, dproj), name="attn_bwd", grid=(bsz, nstep),
        in_specs=[pl.BlockSpec(memory_space=pltpu.SMEM), cur(ATT_WIDTH), cur(ATT_WIDTH), _vec_spec(ATT_WIDTH), q_spec,
                  kv_cur, kv_prev, kv_cur, kv_prev, cur(LANE), tab, tab, tab, _bias_spec(True), _bias_spec(True),
                  pl.BlockSpec(memory_space=pl.ANY)],
        out_specs=[cur(ATT_COLS), _vec_spec(ATT_WIDTH), _vec_spec(ATT_Q_HEADS)],
        out_shape=[_sds(dproj.shape, BF16), _sds((1, ATT_WIDTH), F32), _sds((1, ATT_Q_HEADS), F32)],
        scratch_shapes=[pltpu.VMEM((WINDOW, LANE), F32), pltpu.VMEM((WINDOW, LANE), F32)],
        sem=("arbitrary", "arbitrary"), comms=comms, aliases={15: 0})


def _other_chips(x, y):
    return [(1 - x, y), (x, 1 - y), (1 - x, 1 - y)]


def _sem_pair(n):
    return [pltpu.SemaphoreType.DMA((n,)), pltpu.SemaphoreType.DMA((n,))]


def _plan_pair_forward(bufs):
    n = len(bufs)

    def copies(outs, sems):
        x, y, c = _mesh_pos()
        sends, lands = [], []
        for a in range(n):
            for j, chip in enumerate(_other_chips(x, y)):
                k = 3 * a + j
                slot = outs[a].at[4 * chip[0] + 2 * chip[1] + c]
                sends.append(pltpu.make_async_remote_copy(
                    src_ref=slot, dst_ref=slot, send_sem=sems[0].at[k], recv_sem=sems[1].at[k],
                    device_id=(x, y, 1 - c), device_id_type=MESH))
                theirs = outs[a].at[4 * chip[0] + 2 * chip[1] + 1 - c]
                lands.append(pltpu.make_async_remote_copy(
                    src_ref=theirs, dst_ref=theirs, send_sem=sems[0].at[k], recv_sem=sems[1].at[k],
                    device_id=(x, y, 1 - c), device_id_type=MESH))
        return sends, lands

    def start(ins, outs, sems):
        for cp in copies(outs, sems)[0]:
            cp.start()

    def finish(ins, outs, sems):
        sends, lands = copies(outs, sems)
        for cp in lands:
            cp.wait_recv()
        for cp in sends:
            cp.wait_send()

    return _Comm(list(bufs), [_sds(b.shape, b.dtype) for b in bufs], _sem_pair(3 * n), start, finish,
                 aliases=[(a, a) for a in range(n)])


def _plan_pair(arrays, other_half):
    n = len(arrays)
    per = N_CHIPS if other_half == "chip_major" else 1

    def copies(ins, outs, sems):
        x, y, c = _mesh_pos()
        out = []
        for a in range(n):
            for k in range(per):
                if other_half == "chip_major":
                    src, dst = ins[a].at[k, 1 - c], outs[a].at[k]
                else:
                    src, dst = (ins[a].at[1 - c] if other_half else ins[a]), outs[a]
                out.append(pltpu.make_async_remote_copy(
                    src_ref=src, dst_ref=dst, send_sem=sems[0].at[per * a + k], recv_sem=sems[1].at[per * a + k],
                    device_id=(x, y, 1 - c), device_id_type=MESH))
        return out

    def start(ins, outs, sems):
        for cp in copies(ins, outs, sems):
            cp.start()

    def finish(ins, outs, sems):
        for cp in copies(ins, outs, sems):
            cp.wait()

    if other_half == "chip_major":
        shapes = [_sds((a.shape[0],) + a.shape[2:], a.dtype) for a in arrays]
    else:
        shapes = [_sds(a.shape[1:] if other_half else a.shape, a.dtype) for a in arrays]
    return _Comm(list(arrays), shapes, _sem_pair(per * n), start, finish)


def _plan_chip_exchange(arrays):
    n = len(arrays)

    def copies(ins, outs, sems):
        x, y, c = _mesh_pos()
        sends, lands = [], []
        for a in range(n):
            for j, chip in enumerate(_other_chips(x, y)):
                k = 3 * a + j
                sends.append(pltpu.make_async_remote_copy(
                    src_ref=ins[a].at[2 * chip[0] + chip[1]], dst_ref=outs[a].at[2 * x + y], send_sem=sems[0].at[k],
                    recv_sem=sems[1].at[k], device_id=(*chip, c), device_id_type=MESH))
                slot = outs[a].at[2 * chip[0] + chip[1]]
                lands.append(pltpu.make_async_remote_copy(
                    src_ref=slot, dst_ref=slot, send_sem=sems[0].at[k], recv_sem=sems[1].at[k],
                    device_id=(*chip, c), device_id_type=MESH))
        return sends, lands

    def start(ins, outs, sems):
        for cp in copies(ins, outs, sems)[0]:
            cp.start()

    def finish(ins, outs, sems):
        sends, lands = copies(ins, outs, sems)
        for cp in lands:
            cp.wait_recv()
        for cp in sends:
            cp.wait_send()

    return _Comm(list(arrays), [_sds(a.shape, a.dtype) for a in arrays], _sem_pair(3 * n), start, finish)


SEM_SPEC = pl.BlockSpec(memory_space=pltpu.SEMAPHORE)
N_OTHER = N_CHIPS - 1


def _exchange_copies(s_ref, land_ref, sems):
    x, y, c = _mesh_pos()
    return [pltpu.make_async_remote_copy(
        src_ref=s_ref.at[2 * chip[0] + chip[1]], dst_ref=land_ref.at[2 * x + y], send_sem=sems[j], recv_sem=sems[N_OTHER + j],
        device_id=(*chip, c), device_id_type=MESH) for j, chip in enumerate(_other_chips(x, y))]


def _exchange_start(s, name):
    def body(s_ref, land_ref, *outs):
        sems, token = outs[:2 * N_OTHER], outs[-1]
        for cp in _exchange_copies(s_ref, land_ref, sems):
            cp.start()
        token[...] = jnp.zeros_like(token)

    hbm = pltpu.HBM(s.shape, s.dtype)
    res = pl.pallas_call(
        body, name=name,
        out_shape=(pltpu.SemaphoreType.DMA(()),) * (2 * N_OTHER) + (hbm, hbm, _sds((SUBLANES, LANE), F32)),
        in_specs=(HBM_SPEC, HBM_SPEC),
        out_specs=(SEM_SPEC,) * (2 * N_OTHER) + (HBM_SPEC, HBM_SPEC, pl.BlockSpec(memory_space=pltpu.VMEM)),
        input_output_aliases={0: 2 * N_OTHER, 1: 2 * N_OTHER + 1},
        compiler_params=pltpu.CompilerParams(has_side_effects=pltpu.SideEffectType.DATAFLOW_SIDE_EFFECTING),
    )(pltpu.with_memory_space_constraint(s, pltpu.HBM), pltpu.with_memory_space_constraint(lax.empty(s.shape, s.dtype), pltpu.HBM))
    return res[:2 * N_OTHER], res[2 * N_OTHER], res[2 * N_OTHER + 1], res[-1]


def _exchange_wait(sems, s_thru, land_thru, afters, name):
    def body(s_ref, land_ref, *rest):
        for cp in _exchange_copies(s_ref, land_ref, rest[:2 * N_OTHER]):
            cp.wait_send()
            cp.wait_recv()

    hbm = pltpu.HBM(s_thru.shape, s_thru.dtype)
    return pl.pallas_call(
        body, name=name, out_shape=(hbm, hbm),
        in_specs=(HBM_SPEC, HBM_SPEC) + (SEM_SPEC,) * (2 * N_OTHER) + (pl.BlockSpec(memory_space=pl.ANY),) * len(afters),
        out_specs=(HBM_SPEC, HBM_SPEC), input_output_aliases={0: 0, 1: 1},
        compiler_params=pltpu.CompilerParams(has_side_effects=pltpu.SideEffectType.DATAFLOW_SIDE_EFFECTING),
    )(s_thru, land_thru, *sems, *afters)


def _gather_copies(block_ref, buf_ref, sems):
    x, y, c = _mesh_pos()
    return [pltpu.make_async_remote_copy(
        src_ref=block_ref, dst_ref=buf_ref.at[4 * x + 2 * y + c], send_sem=sems[j], recv_sem=sems[N_OTHER + j],
        device_id=(*chip, c), device_id_type=MESH) for j, chip in enumerate(_other_chips(x, y))]


def _gather_start(blocks, bufs, afters, name):
    n = len(blocks)
    per = 2 * N_OTHER

    def body(*refs):
        ins, outs = refs[:2 * n], refs[2 * n + len(afters):]
        for a in range(n):
            for cp in _gather_copies(ins[a], ins[n + a], outs[a * per:(a + 1) * per]):
                cp.start()
        outs[-1][...] = jnp.zeros_like(outs[-1])

    hbm = [pltpu.HBM(z.shape, z.dtype) for z in list(blocks) + list(bufs)]
    res = pl.pallas_call(
        body, name=name,
        out_shape=(pltpu.SemaphoreType.DMA(()),) * (n * per) + tuple(hbm) + (_sds((SUBLANES, LANE), F32),),
        in_specs=(HBM_SPEC,) * (2 * n) + (pl.BlockSpec(memory_space=pl.ANY),) * len(afters),
        out_specs=(SEM_SPEC,) * (n * per) + (HBM_SPEC,) * (2 * n) + (pl.BlockSpec(memory_space=pltpu.VMEM),),
        input_output_aliases={k: n * per + k for k in range(2 * n)},
        compiler_params=pltpu.CompilerParams(has_side_effects=pltpu.SideEffectType.DATAFLOW_SIDE_EFFECTING),
    )(*[pltpu.with_memory_space_constraint(z, pltpu.HBM) for z in list(blocks) + list(bufs)], *afters)
    parts = [(res[a * per:(a + 1) * per], res[n * per + a], res[n * per + n + a]) for a in range(n)]
    return parts, res[-1]


def _gather_wait(part, afters, name):
    sems, block, buf = part

    def body(block_ref, buf_ref, *rest):
        for cp in _gather_copies(block_ref, buf_ref, rest[:2 * N_OTHER]):
            cp.wait_send()
            cp.wait_recv()

    return pl.pallas_call(
        body, name=name, out_shape=(pltpu.HBM(block.shape, block.dtype), pltpu.HBM(buf.shape, buf.dtype)),
        in_specs=(HBM_SPEC, HBM_SPEC) + (SEM_SPEC,) * (2 * N_OTHER) + (pl.BlockSpec(memory_space=pl.ANY),) * len(afters),
        out_specs=(HBM_SPEC, HBM_SPEC), input_output_aliases={0: 0, 1: 1},
        compiler_params=pltpu.CompilerParams(has_side_effects=pltpu.SideEffectType.DATAFLOW_SIDE_EFFECTING),
    )(block, buf, *sems, *afters)[1]


def _comm_only(comms, name):
    return _call(lambda: None, (), name=name, grid=(), in_specs=[], out_specs=[], out_shape=[], sem=(), comms=comms)[1]


def _allgather8(arrays, name):
    return _comm_only([_plan_allgather8(arrays)], name)[0]


def _plan_allgather8(arrays):
    n = len(arrays)

    def parts(ins, outs, sems):
        send_sems, recv_sems, local_sems = sems
        x, y, c = _mesh_pos()
        me, sibling = (x, y, c), (x, y, 1 - c)
        chips = _other_chips(x, y)

        def copy(a, k, block, to, src=None):
            dst = outs[a].at[4 * block[0] + 2 * block[1] + block[2]]
            return pltpu.make_async_remote_copy(
                src_ref=dst if src is None else src, dst_ref=dst, send_sem=send_sems.at[7 * a + k],
                recv_sem=recv_sems.at[7 * a + k], device_id=to, device_id_type=MESH)

        mine = [pltpu.make_async_copy(ins[a], outs[a].at[4 * x + 2 * y + c], local_sems.at[a]) for a in range(n)]
        first = []
        for a in range(n):
            first.append(copy(a, 0, me, sibling, src=ins[a]))
            first += [copy(a, 1 + j, me, (*chip, c), src=ins[a]) for j, chip in enumerate(chips)]
        return copy, mine, first, me, sibling, chips, c

    def start(ins, outs, sems):
        _, mine, first, *_ = parts(ins, outs, sems)
        for cp in mine + first:
            cp.start()

    def finish(ins, outs, sems):
        copy, mine, first, me, sibling, chips, c = parts(ins, outs, sems)
        passed = []
        for j, chip in enumerate(chips):
            for a in range(n):
                copy(a, 1 + j, (*chip, c), me).wait_recv()
                fwd = copy(a, 4 + j, (*chip, c), sibling)
                fwd.start()
                passed.append(fwd)
        for a in range(n):
            copy(a, 0, sibling, me).wait_recv()
            for j, chip in enumerate(chips):
                copy(a, 4 + j, (*chip, 1 - c), me).wait_recv()
        for cp in first + passed:
            cp.wait_send()
        for cp in mine:
            cp.wait()

    sems = [pltpu.SemaphoreType.DMA((7 * n,)), pltpu.SemaphoreType.DMA((7 * n,)), pltpu.SemaphoreType.DMA((n,))]
    return _Comm(list(arrays), [_sds((N_DEV,) + a.shape, a.dtype) for a in arrays], sems, start, finish)


def _plan_allgather8_direct(arrays):
    n = len(arrays)
    flips = [(dx, dy, dc) for dx in (0, 1) for dy in (0, 1) for dc in (0, 1)][1:]

    def copies(ins, outs, sems):
        send_sems, recv_sems, local_sems = sems
        x, y, c = _mesh_pos()
        mine = [pltpu.make_async_copy(ins[a], outs[a].at[4 * x + 2 * y + c], local_sems.at[a]) for a in range(n)]
        sends, lands = [], []
        for a in range(n):
            for k, (dx, dy, dc) in enumerate(flips):
                peer = (1 - x if dx else x, 1 - y if dy else y, 1 - c if dc else c)
                pair = dict(send_sem=send_sems.at[7 * a + k], recv_sem=recv_sems.at[7 * a + k], device_id=peer, device_id_type=MESH)
                sends.append(pltpu.make_async_remote_copy(src_ref=ins[a], dst_ref=outs[a].at[4 * x + 2 * y + c], **pair))
                theirs = outs[a].at[4 * peer[0] + 2 * peer[1] + peer[2]]
                lands.append(pltpu.make_async_remote_copy(src_ref=theirs, dst_ref=theirs, **pair))
        return mine, sends, lands

    def start(ins, outs, sems):
        mine, sends, _ = copies(ins, outs, sems)
        for cp in mine + sends:
            cp.start()

    def finish(ins, outs, sems):
        mine, sends, lands = copies(ins, outs, sems)
        for cp in lands:
            cp.wait_recv()
        for cp in sends:
            cp.wait_send()
        for cp in mine:
            cp.wait()

    sems = [pltpu.SemaphoreType.DMA((7 * n,)), pltpu.SemaphoreType.DMA((7 * n,)), pltpu.SemaphoreType.DMA((n,))]
    return _Comm(list(arrays), [_sds((N_DEV,) + a.shape, a.dtype) for a in arrays], sems, start, finish)


def _pair_sum(g, q, core, name, chip_major=False):
    rows, cols = g.shape[2:]
    tr = _row_tile(rows)

    def body(core_ref, g_ref, q_ref, o_ref):
        o_ref[...] = (g_ref[...] + q_ref[...]).astype(BF16)

    blk = pl.BlockSpec((None, tr, cols), lambda k, i, core_ref: (k, i, 0))
    if chip_major:
        own = pl.BlockSpec((None, None, tr, cols), lambda k, i, core_ref: (k, core_ref[0], i, 0))
    else:
        own = pl.BlockSpec((None, None, tr, cols), lambda k, i, core_ref: (core_ref[0], k, i, 0))
    return pl.pallas_call(
        body, name=name,
        grid_spec=pltpu.PrefetchScalarGridSpec(num_scalar_prefetch=1, grid=(N_CHIPS, rows // tr), in_specs=[own, blk], out_specs=blk),
        out_shape=_sds((N_CHIPS, rows, cols), BF16), compiler_params=_params("parallel", "parallel"),
    )(core, g, q)


def _sum_chips(own, landed, chip, name):
    _, rows, cols = own.shape
    tr = _row_tile(rows)

    def body(chip_ref, own_ref, a_ref, b_ref, c_ref, o_ref):
        acc = own_ref[...].astype(F32) + a_ref[...].astype(F32)
        o_ref[...] = (acc + b_ref[...].astype(F32)) + c_ref[...].astype(F32)

    blk = lambda flip: pl.BlockSpec((None, tr, cols), lambda i, chip_ref: (jnp.bitwise_xor(chip_ref[0], flip), i, 0))
    return pl.pallas_call(
        body, name=name,
        grid_spec=pltpu.PrefetchScalarGridSpec(num_scalar_prefetch=1, grid=(rows // tr,), in_specs=[blk(0), blk(1), blk(2), blk(3)],
                                               out_specs=pl.BlockSpec((tr, cols), lambda i, chip_ref: (i, 0))),
        out_shape=_sds((rows, cols), F32), compiler_params=_params("parallel"),
    )(chip, own, landed, landed, landed)


SUBLANES = 8


def _tile_rows(n_elems):
    return -(-n_elems // (SUBLANES * LANE)) * SUBLANES


SMALL_ITEMS = (("b_ada", N_MOD * D_MODEL), ("pre_w_mix", D_MODEL), ("post_w_mix", D_MODEL), ("pre_w_mlp", D_MODEL),
               ("post_w_mlp", D_MODEL), ("attn_out_w", ATT_WIDTH), ("hg_norm_w", HG_HEAD_DIM), ("attn_sinks", ATT_Q_HEADS),
               ("lb_0", HG_WIDTH), ("lb_1", HG_WIDTH))
SMALL_AT = {}
for _name, _size in SMALL_ITEMS:
    SMALL_AT[_name] = (sum(r for _, r in SMALL_AT.values()), _tile_rows(_size))
SMALL_ROWS = sum(r for _, r in SMALL_AT.values())
MOD_ROWS = SMALL_AT["b_ada"][1]
PLAIN_ROWS = SMALL_AT["lb_0"][0] - MOD_ROWS
LB_ROWS = SMALL_AT["lb_0"][1]


def _rows(a, nrows=None):
    flat = a.reshape(-1)
    nrows = _tile_rows(flat.shape[0]) if nrows is None else nrows
    return jnp.pad(flat, (0, nrows * LANE - flat.shape[0])).reshape(nrows, LANE)


def _pack_small(vals):
    vals = dict(vals, lb_0=vals["lb_table"][0], lb_1=vals["lb_table"][1])
    return jnp.concatenate([_rows(vals[name], SMALL_AT[name][1]) for name, _ in SMALL_ITEMS], axis=0)


def _unpack_small(p):
    def item(name, shape):
        first = SMALL_AT[name][0]
        size = shape[0] * shape[1]
        return p[first:first + SMALL_AT[name][1]].reshape(-1)[:size].reshape(shape)

    out = {name: item(name, (1, size)) for name, size in SMALL_ITEMS if not name.startswith("lb_")}
    out["lb_table"] = jnp.concatenate([item("lb_0", (1, HG_WIDTH)), item("lb_1", (1, HG_WIDTH))], axis=0)
    return out


def _pack_partials(dmod, plain, d_lb, loss_row):
    return jnp.concatenate([_rows(dmod, dmod.shape[0] * MOD_ROWS)] + [_rows(g) for g in plain] + [_rows(d_lb), _rows(loss_row)], axis=0)


def _small_update(packs, w, m, v, n_seq):
    mod_end = n_seq * MOD_ROWS
    lb_at = mod_end + PLAIN_ROWS
    t0, t1 = SMALL_AT["lb_0"][0], SMALL_AT["lb_1"][0]

    def body(p_ref, w_ref, m_ref, v_ref, g_ref, dl_ref, nm_ref, nv_ref, loss_ref):
        tot = p_ref[0]
        for d in range(1, N_DEV):
            tot = tot + p_ref[d]
        wv = w_ref[...]
        p1 = _sigmoid(wv[t1:t1 + LB_ROWS] - wv[t0:t0 + LB_ROWS])
        s = tot[lb_at:lb_at + LB_ROWS] * p1 * (1.0 - p1)
        g_bias = tot[0:MOD_ROWS]
        for q in range(1, n_seq):
            g_bias = g_bias + tot[q * MOD_ROWS:(q + 1) * MOD_ROWS]
        g = jnp.concatenate([g_bias, tot[mod_end:lb_at], -s, s], axis=0)
        g_ref[...] = g
        dl_ref[...], nm_ref[...], nv_ref[...] = _adamw_math(g, wv, m_ref[...], v_ref[...])
        loss_ref[...] = tot[lb_at + LB_ROWS:lb_at + LB_ROWS + SUBLANES]

    shp = _sds((SMALL_ROWS, LANE), F32)
    return pl.pallas_call(body, name="small_update", out_shape=[shp] * 4 + [_sds((SUBLANES, LANE), F32)],
                          compiler_params=_params())(packs, w, m, v)


def kernel(x, c, w_ada, b_ada, pre_w_mix, w_in, attn_sinks, attn_out_w, lb_table, hg_norm_w, w_out, post_w_mix, pre_w_mlp, w_up, w_down, post_w_mlp, loss_target, m_w_ada, m_b_ada, m_pre_w_mix, m_w_in, m_attn_sinks, m_attn_out_w, m_lb_table, m_hg_norm_w, m_w_out, m_post_w_mix, m_pre_w_mlp, m_w_up, m_w_down, m_post_w_mlp, v_w_ada, v_b_ada, v_pre_w_mix, v_w_in, v_attn_sinks, v_attn_out_w, v_lb_table, v_hg_norm_w, v_w_out, v_post_w_mix, v_pre_w_mlp, v_w_up, v_w_down, v_post_w_mlp):
    xi, yi, ci = _mesh_pos()
    chip = 2 * xi + yi
    dev = 2 * chip + ci
    bsz, seq, _ = x.shape
    ntok = bsz * seq
    ada_cols = w_ada.shape[2]
    core = jnp.reshape(ci, (1,)).astype(jnp.int32)
    chip_idx = jnp.reshape(chip, (1,)).astype(jnp.int32)
    flat = lambda a: a.reshape(ntok, a.shape[-1])
    unflat = lambda a: a.reshape(bsz, seq, a.shape[-1])
    tables = _rope_tables(seq)
    biases, chunk_masks = _band_biases(), _block_masks()

    def row_half(w):
        rows = w.shape[1] // 2
        return lax.dynamic_slice_in_dim(w[0], ci * rows, rows, axis=0).astype(BF16)

    def gather_buffer(w):
        rows, cols = w.shape[1] // 2, w.shape[2]
        own = w[0].astype(BF16).reshape(2, rows, cols)
        return lax.dynamic_update_slice(lax.empty((N_DEV, rows, cols), BF16), own, (2 * chip, 0, 0))

    w_in_t, m_in_t, v_in_t = [jnp.transpose(a[0])[None] for a in (w_in, m_w_in, v_w_in)]
    c_g, in_g = _allgather8([c, row_half(w_in_t)], "gather_first")
    c_all = c_g.reshape(N_DEV * bsz, D_MODEL)
    w_in_full = in_g.reshape(IN_COLS, D_MODEL)

    b_cols = lax.dynamic_slice_in_dim(b_ada, chip * ada_cols, ada_cols, axis=1)
    mod_part = _ada_fwd(c_all, w_ada[0], b_cols)
    half_rows = mod_part.shape[0] // 2
    ((mod_g,),) = _comm_only(
        [_plan_allgather8_direct([lax.dynamic_slice_in_dim(mod_part, ci * half_rows, half_rows, axis=0)])], "gather_mod")
    mod_all = mod_g.reshape(N_CHIPS, 2, half_rows, ada_cols).transpose(1, 2, 0, 3).reshape(N_DEV * bsz, N_MOD * D_MODEL)
    mod = lax.dynamic_slice_in_dim(mod_all, dev * bsz, bsz, axis=0)
    sh1, sc1, g1, sh2, sc2, g2 = [mod[:, i * D_MODEL:(i + 1) * D_MODEL].reshape(bsz, 1, D_MODEL) for i in range(N_MOD)]

    weights = (w_out, w_up, w_down)
    (out_part, up_part, down_part), started = _gather_start(
        [row_half(w) for w in weights], [gather_buffer(w) for w in weights], [mod_g], "gather_weights_start")

    h1, proj, qh, kh, vh = _in_proj_fused(x, pre_w_mix, sc1 + started[0:1, 0:1], sh1, w_in_full, tables)
    out_g = _gather_wait(out_part, [proj], "gather_out_wait")
    (attn_raw, cat, lse), ((out_g,),) = _attn_fwd(qh, kh, vh, attn_sinks, attn_out_w, biases, comms=[_plan_pair_forward([out_g])])
    up_g = _gather_wait(up_part, [attn_raw], "gather_up_wait")
    (o_raw, cat, states), ((up_g,),) = _hgrn_fwd(proj, lb_table, hg_norm_w, cat, chunk_masks, comms=[_plan_pair_forward([up_g])])
    down_g = _gather_wait(down_part, [o_raw], "gather_down_wait")
    w_out_full = out_g.reshape(D_MODEL, D_MODEL)
    w_up4 = up_g.reshape(N_CHIPS, D_MODEL, D_MODEL)
    mix, x1, h2 = _out_proj_fused(cat, w_out_full, x, post_w_mix, g1, pre_w_mlp, sc2, sh2)
    big_tm = min(ntok, 2048)
    up_spec = pl.BlockSpec((None, D_MODEL, D_MODEL), lambda i, j: (j, 0, 0))
    r, ((down_g,),) = _mm(flat(h2), w_up4, name="up_proj", out_dtype=BF16, tm=big_tm, tn=D_MODEL, n_out=D_FF, b_spec=up_spec,
                          epi=lambda acc: jnp.maximum(acc, 0.0), comms=[_plan_pair_forward([down_g])])
    w_down_full = down_g.reshape(D_FF, D_MODEL)
    square = lambda t: t * t
    loss_row, dy, dd, dg2, d_post_mlp = _down_proj_fused(unflat(r), w_down_full, x1, post_w_mlp, g2, loss_target)

    dpre = _mm(flat(dd), w_down_full, name="down_bwd", out_dtype=BF16, trans_b=True, tm=big_tm, tn=D_MODEL, extra=(r,),
               epi=lambda acc, rt: acc * (2.0 * rt.astype(F32)))
    half_rows = D_MODEL // 2
    g_down = _mm_tn(r, flat(dd), name="down_wgrad", tk=half_rows, tn=D_MODEL, a_fn=square,
                    out_shape=_sds((2, N_CHIPS, half_rows, D_MODEL), F32),
                    out_spec=pl.BlockSpec((None, None, half_rows, D_MODEL), lambda i, j: (i % 2, i // 2, 0, 0)))
    (dx1, dmix, dsc2, dsh2, dg1, d_pre_mlp, d_post_mix), ((q_down,),) = _up_bwd_fused(
        unflat(dpre), w_up4, dy, x1, mix, pre_w_mlp, sc2, post_w_mix, g1, comms=[_plan_pair([g_down], True)])
    g_up = _mm_tn(flat(h2), dpre, name="up_wgrad", tk=D_MODEL, tn=half_rows,
                  out_shape=_sds((2, N_CHIPS, half_rows, D_MODEL), F32),
                  out_spec=pl.BlockSpec((2, None, half_rows, half_rows), lambda i, j: (0, j // 2, 0, j % 2)))
    s_down = _pair_sum(g_down, q_down, core, "pair_sum_down")

    dcat, ((q_up,),) = _mm(flat(dmix), w_out_full, name="out_bwd", out_dtype=F32, trans_b=True, comms=[_plan_pair([g_up], True)])
    dcat = unflat(dcat)
    s_up = _pair_sum(g_up, q_up, core, "pair_sum_up")
    out_rows = D_MODEL // N_CHIPS
    g_out = _mm_tn(flat(cat), flat(dmix), name="out_wgrad", tk=2 * out_rows, tn=half_rows,
                   out_shape=_sds((2, N_CHIPS, out_rows, half_rows), F32),
                   out_spec=pl.BlockSpec((None, 2, out_rows, half_rows), lambda i, j: (j, i, 0, 0)))
    (dproj_rec, d_lb, d_hg_norm), ((x_down,), (q_out,)) = _hgrn_bwd(
        dcat, proj, o_raw, states, lb_table, hg_norm_w, chunk_masks, comms=[_plan_chip_exchange([s_down]), _plan_pair([g_out], True)])
    half_down = _sum_chips(s_down, x_down, chip_idx, "sum_chips_down")
    s_out = _pair_sum(g_out, q_out, core, "pair_sum_out")
    (dproj, d_attn_out, d_sinks), ((their_down,), (x_up,)) = _attn_bwd(
        dcat, attn_raw, attn_out_w, qh, kh, vh, lse, attn_sinks, tables, [bias.T for bias in biases], dproj_rec,
        comms=[_plan_pair([half_down], False), _plan_chip_exchange([s_up])])
    half_up = _sum_chips(s_up, x_up, chip_idx, "sum_chips_up")
    dproj = flat(dproj)
    in_rows = IN_COLS // N_CHIPS // 2
    g_in, ((x_out,),) = _mm_tn(dproj, flat(h1), name="in_wgrad", tk=2 * LANE, tn=D_MODEL, comms=[_plan_chip_exchange([s_out])])
    g_in = g_in.reshape(N_CHIPS, 2, in_rows, D_MODEL)
    half_out = _sum_chips(s_out, x_out, chip_idx, "sum_chips_out")
    dh1, ((q_in,), (their_up, their_out)) = _mm(
        dproj, w_in_full, name="in_bwd", out_dtype=F32,
        comms=[_plan_pair([g_in], "chip_major"), _plan_pair([half_up, half_out], False)])
    s_in = _pair_sum(g_in, q_in, core, "pair_sum_in", chip_major=True)
    in_sems, s_in, in_landing, started = _exchange_start(s_in, "exchange_in_start")
    grad_x, dsc1, dsh1, d_pre_mix = _norm1_bwd(unflat(dh1), dx1, x, pre_w_mix + started[0:1, 0:1], sc1)

    dmod = jnp.concatenate([dsh1, dsc1, dg1, dsh2, dsc2, dg2], axis=-1).reshape(bsz, N_MOD * D_MODEL)
    pack = _pack_partials(dmod, [d_pre_mix, d_post_mix, d_pre_mlp, d_post_mlp, d_attn_out, d_hg_norm, d_sinks], d_lb, loss_row)
    ((packs,),) = _comm_only([_plan_allgather8_direct([pack])], "gather_small")
    w_small = dict(b_ada=b_ada, pre_w_mix=pre_w_mix, post_w_mix=post_w_mix, pre_w_mlp=pre_w_mlp, post_w_mlp=post_w_mlp,
                   attn_out_w=attn_out_w, hg_norm_w=hg_norm_w, attn_sinks=attn_sinks, lb_table=lb_table)
    m_small = dict(b_ada=m_b_ada, pre_w_mix=m_pre_w_mix, post_w_mix=m_post_w_mix, pre_w_mlp=m_pre_w_mlp, post_w_mlp=m_post_w_mlp,
                   attn_out_w=m_attn_out_w, hg_norm_w=m_hg_norm_w, attn_sinks=m_attn_sinks, lb_table=m_lb_table)
    v_small = dict(b_ada=v_b_ada, pre_w_mix=v_pre_w_mix, post_w_mix=v_post_w_mix, pre_w_mlp=v_pre_w_mlp, post_w_mlp=v_post_w_mlp,
                   attn_out_w=v_attn_out_w, hg_norm_w=v_hg_norm_w, attn_sinks=v_attn_sinks, lb_table=v_lb_table)
    *small_packed, loss_rows = _small_update(packs, _pack_small(w_small), _pack_small(m_small), _pack_small(v_small), bsz)
    small_out = [_unpack_small(p) for p in small_packed]
    loss = loss_rows[0, 0]

    dmod_all = packs[:, :bsz * MOD_ROWS, :].reshape(N_DEV * bsz, N_MOD * D_MODEL)
    dmod_cols = lax.dynamic_slice_in_dim(dmod_all, chip * ada_cols, ada_cols, axis=1)
    ada_out = _ada_bwd_adamw(c_all, dmod_cols, w_ada[0], m_w_ada[0], v_w_ada[0])

    s_in, x_in = _exchange_wait(in_sems, s_in, in_landing, [grad_x, ada_out[0]], "exchange_in_wait")
    half_in = _sum_chips(s_in, x_in, chip_idx, "sum_chips_in")
    ((their_in,),) = _comm_only([_plan_pair([half_in], False)], "pair_swap_in")
    big = dict(
        w_in=tuple(jnp.transpose(a) for a in _adamw_halves(half_in, their_in, core, w_in_t[0], m_in_t[0], v_in_t[0], axis=0,
                                                           name="adamw_in")),
        w_up=tuple(_adamw_halves(half_up, their_up, core, w_up[0], m_w_up[0], v_w_up[0], axis=0, name="adamw_up")),
        w_out=tuple(_adamw_halves(half_out, their_out, core, w_out[0], m_w_out[0], v_w_out[0], axis=1, name="adamw_out")),
        w_down=tuple(_adamw_halves(half_down, their_down, core, w_down[0], m_w_down[0], v_w_down[0], axis=0, name="adamw_down")),
        w_ada=tuple(ada_out),
    )
    order = ("w_ada", "b_ada", "pre_w_mix", "w_in", "attn_sinks", "attn_out_w", "lb_table", "hg_norm_w", "w_out", "post_w_mix",
             "pre_w_mlp", "w_up", "w_down", "post_w_mlp")
    outs = [loss, grad_x]
    for kind in range(4):
        for nm in order:
            outs.append(big[nm][kind][None] if nm in big else small_out[kind][nm])
    return tuple(outs)
```

```python
import jax
import jax.numpy as jnp
from jax import lax
from jax.experimental import pallas as pl
from jax.experimental.pallas import tpu as pltpu

F32 = jnp.float32
BF16 = jnp.bfloat16

D_MODEL = 1024
ATT_WIDTH = 512
ATT_HEAD_DIM = 64
ATT_Q_HEADS = 8
ATT_KV_HEADS = 2
ATT_GROUP = ATT_Q_HEADS // ATT_KV_HEADS
ATT_KV_COLS = ATT_KV_HEADS * ATT_HEAD_DIM
WINDOW = 128
ROPE_DIM = 16
ROPE_THETA = 500000.0
HG_WIDTH = 512
MIX_WIDTH = ATT_WIDTH + HG_WIDTH
HG_HEAD_DIM = 128
HG_HEADS = 4
HG_CHUNK = 32
IN_COLS = ATT_WIDTH + 2 * ATT_KV_COLS + 4 * HG_WIDTH
ATT_COLS = ATT_WIDTH + 2 * ATT_KV_COLS
D_FF = 4 * D_MODEL
N_MOD = 6
EPS = 1e-6
ATT_SCALE = ATT_HEAD_DIM ** -0.5

ADAM_LR = 0.001
ADAM_B1 = 0.9
ADAM_B2 = 0.999
ADAM_EPS = 1e-08
ADAM_WD = 0.01
ADAM_STEP = 10

N_CHIPS = 4
N_DEV = 8
LANE = 128
VMEM_LIMIT = 48 * 1024 * 1024
VMEM_LIMIT_BIG = 58 * 1024 * 1024
MESH = pl.DeviceIdType.MESH

NT_DIMS = (((1,), (1,)), ((), ()))
TN_DIMS = (((0,), (0,)), ((), ()))


def _sds(shape, dtype):
    return jax.ShapeDtypeStruct(tuple(shape), dtype)


def _params(*sem, vmem_limit=None):
    return pltpu.CompilerParams(dimension_semantics=sem, vmem_limit_bytes=VMEM_LIMIT if vmem_limit is None else vmem_limit)


def _sigmoid(x):
    return 1.0 / (1.0 + jnp.exp(-x))


def _dot(a, b, dims=None):
    a, b = a.astype(BF16), b.astype(BF16)
    if dims is None:
        return jnp.dot(a, b, preferred_element_type=F32)
    return lax.dot_general(a, b, dims, preferred_element_type=F32)


def _rms_fwd(x, w):
    rstd = lax.rsqrt(jnp.mean(x * x, axis=-1, keepdims=True) + EPS)
    xh = x * rstd
    return xh * w, xh, rstd


def _rms_bwd(dy, xh, rstd, w):
    dxh = dy * w
    dx = rstd * (dxh - xh * jnp.mean(dxh * xh, axis=-1, keepdims=True))
    return dx, dy * xh


def _colsum(x):
    return jnp.sum(x, axis=0, keepdims=True)


def _rms_hat(x):
    rstd = lax.rsqrt(jnp.mean(x * x, axis=-1, keepdims=True) + EPS)
    return x * rstd, rstd


def _rms_bwd_gain(dy, gain, xh, rstd):
    dxh = dy * gain
    dx = rstd * (dxh - xh * jnp.mean(dxh * xh, axis=-1, keepdims=True))
    return dx, _colsum(dy * xh)


def _row_tile(rows, cap=256):
    return max(t for t in range(16, cap + 1, 16) if rows % t == 0)


HBM_SPEC = pl.BlockSpec(memory_space=pltpu.HBM)


def _mesh_pos():
    return lax.axis_index("x"), lax.axis_index("y"), lax.axis_index("c")


class _Comm:
    def __init__(self, ins, outs, sems, start, finish, aliases=()):
        self.ins, self.outs, self.sems = list(ins), list(outs), list(sems)
        self.start, self.finish, self.aliases = start, finish, tuple(aliases)


def _call(body, args, *, name, grid, in_specs, out_specs, out_shape, sem, scratch_shapes=(), comms=(), aliases=None,
          vmem_limit=None):
    scratch_shapes = list(scratch_shapes)
    if not comms:
        return pl.pallas_call(body, name=name, grid=grid, in_specs=in_specs, out_specs=out_specs, out_shape=out_shape,
                              input_output_aliases=dict(aliases or {}), scratch_shapes=scratch_shapes,
                              compiler_params=_params(*sem, vmem_limit=vmem_limit))(*args)
    single = not isinstance(out_shape, (list, tuple))
    out_specs_l = [out_specs] if single else list(out_specs)
    out_shape_l = [out_shape] if single else list(out_shape)
    n_in, n_out, n_scr = len(in_specs), len(out_shape_l), len(scratch_shapes)
    n_ci = [len(cm.ins) for cm in comms]
    n_co = [len(cm.outs) for cm in comms]
    n_cs = [len(cm.sems) for cm in comms]
    aliases = dict(aliases or {})
    for k, cm in enumerate(comms):
        for i, o in cm.aliases:
            aliases[n_in + sum(n_ci[:k]) + i] = n_out + sum(n_co[:k]) + o

    def fused(*refs):
        pos = [0]

        def take(n):
            part = refs[pos[0]:pos[0] + n]
            pos[0] += n
            return part

        ins = take(n_in)
        c_ins = [take(n) for n in n_ci]
        outs = take(n_out)
        c_outs = [take(n) for n in n_co]
        scr = take(n_scr)
        c_sems = [take(n) for n in n_cs]
        first, last = True, True
        for d, size in enumerate(grid):
            first = jnp.logical_and(first, pl.program_id(d) == 0)
            last = jnp.logical_and(last, pl.program_id(d) == size - 1)

        def run(which):
            for cm, ci, co, cs in zip(comms, c_ins, c_outs, c_sems):
                getattr(cm, which)(ci, co, cs)

        if grid:
            pl.when(first)(lambda: run("start"))
        else:
            run("start")
        body(*ins, *outs, *scr)
        if grid:
            pl.when(last)(lambda: run("finish"))
        else:
            run("finish")

    res = pl.pallas_call(
        fused, name=name, grid=grid, in_specs=list(in_specs) + [HBM_SPEC] * sum(n_ci),
        out_specs=out_specs_l + [HBM_SPEC] * sum(n_co), out_shape=out_shape_l + [s for cm in comms for s in cm.outs],
        input_output_aliases=aliases, scratch_shapes=scratch_shapes + [s for cm in comms for s in cm.sems],
        compiler_params=_params(*["arbitrary"] * len(grid), vmem_limit=vmem_limit),
    )(*args, *[a for cm in comms for a in cm.ins])
    main = res[:n_out]
    extra, at = [], n_out
    for n in n_co:
        extra.append(list(res[at:at + n]))
        at += n
    return (main[0] if single else list(main)), extra


def _mm(a, b, *, name, out_dtype, trans_b=False, tm=512, tn=None, extra=(), epi=None, b_spec=None, n_out=None, comms=()):
    m_total, k_total = a.shape
    if n_out is None:
        n_out = b.shape[0] if trans_b else b.shape[1]
    tn = n_out if tn is None else tn
    grid = (m_total // tm, n_out // tn)
    dims = NT_DIMS if trans_b else None

    def body(*refs):
        a_ref, b_ref = refs[0], refs[1]
        extra_refs = refs[2:2 + len(extra)]
        o_ref = refs[2 + len(extra)]
        acc = _dot(a_ref[...], b_ref[...], dims)
        if epi is not None:
            acc = epi(acc, *[r[...] for r in extra_refs])
        o_ref[...] = acc.astype(out_dtype)

    if b_spec is None:
        if trans_b:
            b_spec = pl.BlockSpec((tn, k_total), lambda i, j: (j, 0))
        else:
            b_spec = pl.BlockSpec((k_total, tn), lambda i, j: (0, j))
    in_specs = [pl.BlockSpec((tm, k_total), lambda i, j: (i, 0)), b_spec]
    in_specs += [pl.BlockSpec((tm, tn), lambda i, j: (i, j)) for _ in extra]
    return _call(
        body, (a, b, *extra), name=name, grid=grid, in_specs=in_specs,
        out_specs=pl.BlockSpec((tm, tn), lambda i, j: (i, j)),
        out_shape=_sds((m_total, n_out), out_dtype),
        sem=("parallel", "parallel"), comms=comms)


def _mm_tn(a, b, *, name, tk, tn, a_fn=None, out_shape=None, out_spec=None, comms=()):
    m_total, k_total = a.shape
    n_total = b.shape[1]
    grid = (k_total // tk, n_total // tn)

    def body(a_ref, b_ref, o_ref):
        av = a_ref[...]
        part = _dot(av if a_fn is None else a_fn(av), b_ref[...], TN_DIMS)
        o_ref[...] = part.reshape(o_ref.shape)

    if out_shape is None:
        out_shape = _sds((k_total, n_total), F32)
        out_spec = pl.BlockSpec((tk, tn), lambda i, j: (i, j))
    return _call(
        body, (a, b), name=name, grid=grid,
        in_specs=[pl.BlockSpec((m_total, tk), lambda i, j: (0, i)), pl.BlockSpec((m_total, tn), lambda i, j: (0, j))],
        out_specs=out_spec, out_shape=out_shape, sem=("parallel", "parallel"), comms=comms)


def _ada_fwd(c_all, w_shard, b_shard):
    nb, ncol = c_all.shape[0], w_shard.shape[1]
    tn = 512

    def body(c_ref, w_ref, b_ref, o_ref):
        c = c_ref[...]
        o_ref[...] = _dot(c * _sigmoid(c), w_ref[...]) + b_ref[...]

    return pl.pallas_call(
        body, name="ada_fwd", grid=(ncol // tn,),
        in_specs=[pl.BlockSpec((nb, D_MODEL), lambda j: (0, 0)), pl.BlockSpec((D_MODEL, tn), lambda j: (0, j)),
                  pl.BlockSpec((1, tn), lambda j: (0, j))],
        out_specs=pl.BlockSpec((nb, tn), lambda j: (0, j)), out_shape=_sds((nb, ncol), F32),
        compiler_params=_params("parallel"),
    )(c_all, w_shard, b_shard)


def _adamw_math(g, w, m, v):
    m = ADAM_B1 * m + (1.0 - ADAM_B1) * g
    v = ADAM_B2 * v + (1.0 - ADAM_B2) * (g * g)
    m_hat = m / (1.0 - ADAM_B1 ** ADAM_STEP)
    v_hat = v / (1.0 - ADAM_B2 ** ADAM_STEP)
    delta = -ADAM_LR * (m_hat / (jnp.sqrt(v_hat) + ADAM_EPS) + ADAM_WD * w)
    return delta, m, v


def _ada_bwd_adamw(c_all, dmod_cols, w, m, v):
    nb, ncol = dmod_cols.shape
    tn = 256

    def body(c_ref, d_ref, w_ref, m_ref, v_ref, g_ref, dl_ref, nm_ref, nv_ref):
        c = c_ref[...]
        g = _dot(c * _sigmoid(c), d_ref[...], TN_DIMS)
        g_ref[...] = g
        dl_ref[...], nm_ref[...], nv_ref[...] = _adamw_math(g, w_ref[...], m_ref[...], v_ref[...])

    col = pl.BlockSpec((D_MODEL, tn), lambda j: (0, j))
    shp = _sds((D_MODEL, ncol), F32)
    return pl.pallas_call(
        body, name="ada_bwd_adamw", grid=(ncol // tn,),
        in_specs=[pl.BlockSpec((nb, D_MODEL), lambda j: (0, 0)), pl.BlockSpec((nb, tn), lambda j: (0, j)), col, col, col],
        out_specs=[col, col, col, col], out_shape=[shp, shp, shp, shp],
        compiler_params=_params("parallel"),
    )(c_all, dmod_cols, w, m, v)


def _adamw_halves(own, theirs, core, w, m, v, *, axis, name):
    r2, c2 = own.shape
    tr = _row_tile(r2)
    nt = r2 // tr

    def body(core_ref, own_ref, their_ref, w_ref, m_ref, v_ref, g_ref, dl_ref, nm_ref, nv_ref):
        g = jnp.where(pl.program_id(0) == core_ref[0], own_ref[...], their_ref[...])
        g_ref[...] = g
        dl_ref[...], nm_ref[...], nv_ref[...] = _adamw_math(g, w_ref[...], m_ref[...], v_ref[...])

    if axis == 0:
        full = pl.BlockSpec((tr, c2), lambda h, i, core_ref: (h * nt + i, 0))
    else:
        full = pl.BlockSpec((tr, c2), lambda h, i, core_ref: (i, h))
    half = pl.BlockSpec((tr, c2), lambda h, i, core_ref: (i, 0))
    shp = _sds(w.shape, F32)
    return pl.pallas_call(
        body, name=name,
        grid_spec=pltpu.PrefetchScalarGridSpec(num_scalar_prefetch=1, grid=(2, nt), in_specs=[half, half, full, full, full],
                                               out_specs=[full] * 4),
        out_shape=[shp] * 4, compiler_params=_params("parallel", "parallel"),
    )(core, own, theirs, w, m, v)


def _tok_spec(tm, width=D_MODEL):
    return pl.BlockSpec((None, tm, width), lambda b, i: (b, i, 0))


def _row_spec(width=D_MODEL):
    return pl.BlockSpec((None, 1, width), lambda b, i: (b, 0, 0))


def _vec_spec(width=D_MODEL):
    return pl.BlockSpec((1, width), lambda b, i: (0, 0))


class _RowsOf:
    def __init__(self, ref, first, count):
        self.ref, self.rows = ref, slice(first, first + count)

    def __getitem__(self, idx):
        return self.ref[self.rows, :]

    def __setitem__(self, idx, value):
        self.ref[self.rows, :] = value


def _mm_rows(a, b, *, name, tm, extra, extra_specs, out_specs, out_shape, epi, pro=None, trans_b=False, b_chunks=1, comms=(),
             parts=1, zero_per_seq=(), zero_once=(), vmem_limit=None):
    bsz, seq, k_total = a.shape
    kc = k_total // b_chunks
    dims = NT_DIMS if trans_b else None
    rows = tm // parts

    def body(*refs):
        a_ref, b_ref = refs[0], refs[1]
        ex, outs = refs[2:2 + len(extra)], refs[2 + len(extra):]
        if zero_per_seq:
            @pl.when(pl.program_id(1) == 0)
            def _():
                for k in zero_per_seq:
                    outs[k][...] = jnp.zeros_like(outs[k])
        if zero_once:
            @pl.when(jnp.logical_and(pl.program_id(0) == 0, pl.program_id(1) == 0))
            def _():
                for k in zero_once:
                    outs[k][...] = jnp.zeros_like(outs[k])

        def part_of(ref, p):
            tiled = len(ref.shape) == 2 and ref.shape[0] == tm
            return _RowsOf(ref, p * rows, rows) if tiled and parts > 1 else ref

        accs = []
        for p in range(parts):
            a_p, ex_p, outs_p = part_of(a_ref, p), [part_of(r, p) for r in ex], [part_of(r, p) for r in outs]
            if b_chunks == 1:
                accs.append(_dot(a_p[...] if pro is None else pro(a_p, ex_p, outs_p), b_ref[...], dims))
            else:
                acc = _dot(a_p[...][:, 0:kc], b_ref[0], NT_DIMS)
                for k in range(1, b_chunks):
                    acc = acc + _dot(a_p[...][:, k * kc:(k + 1) * kc], b_ref[k], NT_DIMS)
                accs.append(acc)
        for p in range(parts):
            epi(accs[p], [part_of(r, p) for r in ex], [part_of(r, p) for r in outs])

    b_spec = pl.BlockSpec(b.shape, lambda bb, i: (0,) * b.ndim)
    return _call(
        body, (a, b, *extra), name=name, grid=(bsz, seq // tm), in_specs=[_tok_spec(tm, k_total), b_spec, *extra_specs],
        out_specs=out_specs, out_shape=out_shape, sem=("arbitrary", "arbitrary"), comms=comms, vmem_limit=vmem_limit)


def _in_proj_fused(x, w, sc, sh, w_in_t, tables, comms=()):
    tm = 512
    bsz, seq, _ = x.shape
    half = ROPE_DIM // 2
    heads_per_slab = LANE // ATT_HEAD_DIM

    def pro(x_ref, ex, outs):
        y, _, _ = _rms_fwd(x_ref[...], ex[0][...])
        h = (y * (1.0 + ex[1][...]) + ex[2][...]).astype(BF16)
        outs[0][...] = h
        return h

    def epi(acc, ex, outs):
        c, u, d = ex[3][...], ex[4][...], ex[5][...]
        _, rec_ref, q_ref, k_ref, v_ref = outs
        for k in range(HG_SLABS):
            rec_ref[k] = acc[:, ATT_COLS + k * HG_WIDTH:ATT_COLS + (k + 1) * HG_WIDTH]

        def rope(z):
            return (z * c + pltpu.roll(z, half, 1) * u + pltpu.roll(z, LANE - half, 1) * d).astype(BF16)

        for s in range(ATT_WIDTH // LANE):
            slab = rope(acc[:, s * LANE:(s + 1) * LANE])
            for part in range(heads_per_slab):
                g, hh = divmod(s * heads_per_slab + part, ATT_GROUP)
                piece = slab[:, part * ATT_HEAD_DIM:(part + 1) * ATT_HEAD_DIM]
                for blk in range(tm // WINDOW):
                    q_ref[blk, g, hh * WINDOW:(hh + 1) * WINDOW, :] = piece[blk * WINDOW:(blk + 1) * WINDOW]
        rk = rope(acc[:, ATT_WIDTH:ATT_WIDTH + LANE])
        vv = acc[:, ATT_WIDTH + LANE:ATT_COLS].astype(BF16)
        for g in range(ATT_KV_HEADS):
            k_ref[g] = rk[:, g * ATT_HEAD_DIM:(g + 1) * ATT_HEAD_DIM]
            v_ref[g] = vv[:, g * ATT_HEAD_DIM:(g + 1) * ATT_HEAD_DIM]

    tab = pl.BlockSpec((tm, LANE), lambda b, i: (i, 0))
    kv_spec = pl.BlockSpec((None, ATT_KV_HEADS, tm, ATT_HEAD_DIM), lambda b, i: (b, 0, i, 0))
    kv_shape = _sds((bsz, ATT_KV_HEADS, seq, ATT_HEAD_DIM), BF16)
    q_spec = pl.BlockSpec((None, tm // WINDOW, ATT_KV_HEADS, GROUP_ROWS, ATT_HEAD_DIM), lambda b, i: (b, i, 0, 0, 0))
    return _mm_rows(x, w_in_t, name="in_proj", tm=tm, extra=(w, sc, sh, *tables),
                    extra_specs=[_vec_spec(), _row_spec(), _row_spec(), tab, tab, tab],
                    out_specs=[_tok_spec(tm), pl.BlockSpec((None, HG_SLABS, tm, HG_WIDTH), lambda b, i: (b, 0, i, 0)), q_spec,
                               kv_spec, kv_spec],
                    out_shape=[_sds(x.shape, BF16), _sds((bsz, HG_SLABS, seq, HG_WIDTH), F32),
                               _sds((bsz, seq // WINDOW, ATT_KV_HEADS, GROUP_ROWS, ATT_HEAD_DIM), BF16), kv_shape, kv_shape],
                    pro=pro, epi=epi, trans_b=True, comms=comms)


def _rope_tables(seq):
    half = ROPE_DIM // 2
    inv_freq = ROPE_THETA ** (-jnp.arange(0, ROPE_DIM, 2, dtype=F32) / ROPE_DIM)
    ang = jnp.arange(seq, dtype=F32)[:, None] * inv_freq[None, :]
    cos, sin = jnp.cos(ang), jnp.sin(ang)
    rest = ATT_HEAD_DIM - ROPE_DIM
    ones, zeros, zh = jnp.ones((seq, rest), F32), jnp.zeros((seq, rest), F32), jnp.zeros((seq, half), F32)
    reps = LANE // ATT_HEAD_DIM
    t_cos = jnp.tile(jnp.concatenate([cos, cos, ones], axis=1), (1, reps))
    t_up = jnp.tile(jnp.concatenate([zh, sin, zeros], axis=1), (1, reps))
    t_dn = jnp.tile(jnp.concatenate([-sin, zh, zeros], axis=1), (1, reps))
    return t_cos, t_up, t_dn


GROUP_ROWS = ATT_GROUP * WINDOW


ATT_BPS = 2


MASKED = -1e30


def _band_biases():
    row = jnp.arange(GROUP_ROWS)[:, None] % WINDOW
    col = jnp.arange(2 * WINDOW)[None, :]
    own = jnp.logical_and(col >= WINDOW, col - WINDOW <= row)
    before = jnp.logical_and(col < WINDOW, col > row)
    return (jnp.where(jnp.logical_or(own, before), 0.0, MASKED).astype(F32), jnp.where(own, 0.0, MASKED).astype(F32))


def _band_bias(full_ref, first_ref, has_prev):
    return full_ref[...] if has_prev is True else jnp.where(has_prev, full_ref[...], first_ref[...])


def _sink_column(sink_ref, g):
    head = lax.broadcasted_iota(jnp.int32, (GROUP_ROWS, 1), 0) // WINDOW
    col = jnp.full((GROUP_ROWS, 1), sink_ref[0, g * ATT_GROUP], F32)
    for hh in range(1, ATT_GROUP):
        col = jnp.where(head == hh, sink_ref[0, g * ATT_GROUP + hh], col)
    return col


def _sink_row(sink_ref, g):
    return jnp.concatenate([jnp.full((1, WINDOW), sink_ref[0, g * ATT_GROUP + hh], F32) for hh in range(ATT_GROUP)], axis=1)


def _bias_spec(transposed=False):
    shape = (2 * WINDOW, GROUP_ROWS) if transposed else (GROUP_ROWS, 2 * WINDOW)
    return pl.BlockSpec(shape, lambda b, i: (0, 0))


def _attn_specs():
    q_spec = pl.BlockSpec((None, ATT_BPS, ATT_KV_HEADS, GROUP_ROWS, ATT_HEAD_DIM), lambda b, i: (b, i, 0, 0, 0))
    kv_cur = pl.BlockSpec((None, ATT_KV_HEADS, ATT_BPS * WINDOW, ATT_HEAD_DIM), lambda b, i: (b, 0, i, 0))
    kv_prev = pl.BlockSpec((None, ATT_KV_HEADS, WINDOW, ATT_HEAD_DIM), lambda b, i: (b, 0, jnp.maximum(ATT_BPS * i - 1, 0), 0))
    return q_spec, kv_cur, kv_prev


def _band(prev_ref, cur_ref, g, blk):
    own = cur_ref[g, blk * WINDOW:(blk + 1) * WINDOW]
    before = prev_ref[g] if blk == 0 else cur_ref[g, (blk - 1) * WINDOW:blk * WINDOW]
    return jnp.concatenate([before, own], axis=0)


def _attn_fwd(qh, kh, vh, sinks, w_norm, biases, comms=()):
    bsz, nblk = qh.shape[0], qh.shape[1]
    seq = nblk * WINDOW
    rows = ATT_BPS * WINDOW

    def body(sink_ref, q_ref, kc_ref, kp_ref, vc_ref, vp_ref, w_ref, full_ref, first_ref, raw_ref, an_ref, l_ref):
        l_ref[...] = jnp.zeros_like(l_ref)
        def block(blk):
            bias = _band_bias(full_ref, first_ref, True if blk else pl.program_id(1) > 0)
            groups = range(ATT_KV_HEADS)
            keys, vals = [_band(kp_ref, kc_ref, g, blk) for g in groups], [_band(vp_ref, vc_ref, g, blk) for g in groups]
            sink = [_sink_column(sink_ref, g) for g in groups]
            s = [_dot(q_ref[blk, g], keys[g], NT_DIMS) * ATT_SCALE + bias for g in groups]
            yield
            m = [jnp.maximum(jnp.max(s[g], axis=-1, keepdims=True), sink[g]) for g in groups]
            p = [jnp.exp(s[g] - m[g]) for g in groups]
            den = [jnp.sum(p[g], axis=-1, keepdims=True) + jnp.exp(sink[g] - m[g]) for g in groups]
            yield
            o = [_dot(p[g] / den[g], vals[g]) for g in groups]
            yield
            lse = [m[g] + jnp.log(den[g]) for g in groups]
            tok = slice(blk * WINDOW, (blk + 1) * WINDOW)
            for g in groups:
                for hh in range(ATT_GROUP):
                    h = g * ATT_GROUP + hh
                    raw_ref[tok, h * ATT_HEAD_DIM:(h + 1) * ATT_HEAD_DIM] = o[g][hh * WINDOW:(hh + 1) * WINDOW]
                    l_ref[tok, h:h + 1] = lse[g][hh * WINDOW:(hh + 1) * WINDOW]

        _in_step(block(blk) for blk in range(ATT_BPS))
        y, _, _ = _rms_fwd(raw_ref[...], w_ref[...])
        an_ref[...] = y.astype(BF16)

    cur = lambda width: pl.BlockSpec((None, rows, width), lambda b, i: (b, i, 0))
    q_spec, kv_cur, kv_prev = _attn_specs()
    return _call(
        body, (sinks, qh, kh, kh, vh, vh, w_norm, *biases), name="attn_fwd", grid=(bsz, nblk // ATT_BPS),
        in_specs=[pl.BlockSpec(memory_space=pltpu.SMEM), q_spec, kv_cur, kv_prev, kv_cur, kv_prev, _vec_spec(ATT_WIDTH),
                  _bias_spec(), _bias_spec()],
        out_specs=[cur(ATT_WIDTH), cur(ATT_WIDTH), cur(LANE)],
        out_shape=[_sds((bsz, seq, ATT_WIDTH), F32), _sds((bsz, seq, MIX_WIDTH), BF16), _sds((bsz, seq, LANE), F32)],
        sem=("parallel", "parallel"), comms=comms)


HG_Q0 = ATT_COLS // LANE
HG_F0 = HG_Q0 + HG_HEADS
HG_I0 = HG_F0 + HG_HEADS
HG_G0 = HG_I0 + HG_HEADS
HG_SLABS = 4
HG_Q, HG_F, HG_I, HG_G = range(HG_SLABS)
HG_TOK = 256
HG_NCH = HG_TOK // HG_CHUNK
HG_HPS = 2


def _block_masks():
    row = jnp.arange(HG_TOK)[:, None]
    col = jnp.arange(HG_TOK)[None, :]
    same = (row // HG_CHUNK) == (col // HG_CHUNK)
    return jnp.logical_and(same, col <= row).astype(F32), jnp.logical_and(same, col >= row).astype(F32)


def _row_in_chunk():
    return lax.broadcasted_iota(jnp.int32, (HG_TOK, LANE), 0) % HG_CHUNK


def _chunk_cumsum(x, reverse=False):
    ric = _row_in_chunk()
    shift = 1
    while shift < HG_CHUNK:
        if reverse:
            x = x + jnp.where(ric < HG_CHUNK - shift, pltpu.roll(x, HG_TOK - shift, 0), 0.0)
        else:
            x = x + jnp.where(ric >= shift, pltpu.roll(x, shift, 0), 0.0)
        shift *= 2
    return x


def _chunk_rows(rows):
    stacked = jnp.concatenate([r[None] for r in rows], axis=0)
    return jnp.broadcast_to(stacked, (HG_NCH, HG_CHUNK, LANE)).reshape(HG_TOK, LANE)


def _chunk_slices(x):
    return [x[j * HG_CHUNK:(j + 1) * HG_CHUNK] for j in range(HG_NCH)]


def _in_step(stages):
    stages = list(stages)
    while stages:
        stages = [g for g in stages if next(g, stages) is not stages]


def _hgrn_common(tbl, hf, hq):
    lb = _sigmoid(tbl[1:2] - tbl[0:1])
    sig = _sigmoid(hf)
    f = lb + (1.0 - lb) * sig
    sq = _sigmoid(hq)
    q, k = hq * sq, 1.0 - f
    b = _chunk_cumsum(jnp.log(f))
    last = [b[(j + 1) * HG_CHUNK - 1:(j + 1) * HG_CHUNK] for j in range(HG_NCH)]
    bl = _chunk_rows(last)
    e_b, e_nb, e_rem = jnp.exp(b), jnp.exp(-b), jnp.exp(bl - b)
    e_last = [jnp.exp(r) for r in last]
    return dict(lb=lb, sig=sig, f=f, sq=sq, q=q, k=k, e_b=e_b, e_nb=e_nb, e_rem=e_rem, e_last=e_last,
                qd=q * e_b, kd=k * e_nb, ku=k * e_rem)


def _hgrn_fwd(proj, lb_table, norm_w, mix_in, masks, comms=()):
    bsz, _, seq, _ = proj.shape
    nstep = seq // HG_TOK

    def body(tbl_ref, nw_ref, p_ref, mix_ref, lower_ref, o_ref, rec_ref, st_ref, s_scr):
        @pl.when(pl.program_id(2) == 0)
        def _():
            s_scr[...] = jnp.zeros_like(s_scr)

        lower = lower_ref[...]

        def head(hp):
            ls = slice(hp * LANE, (hp + 1) * LANE)
            v, hg = p_ref[HG_I, :, ls], p_ref[HG_G, :, ls]
            t = _hgrn_common(tbl_ref[:, ls], p_ref[HG_F, :, ls], p_ref[HG_Q, :, ls])
            yield
            a = _dot(t["qd"], t["kd"], NT_DIMS) * lower
            o_intra = _dot(a, v)
            yield
            v_c, ku_c, qd_c = [_chunk_slices(z.astype(BF16)) for z in (v, t["ku"], t["qd"])]
            updates = [_dot(v_c[j], ku_c[j], TN_DIMS) for j in range(HG_NCH)]
            yield
            st = s_scr[hp]
            states = []
            for j in range(HG_NCH):
                states.append(st)
                st = st * t["e_last"][j] + updates[j]
            s_scr[hp] = st
            yield
            o = o_intra + jnp.concatenate([_dot(qd_c[j], states[j], NT_DIMS) for j in range(HG_NCH)], axis=0)
            yield
            st_ref[hp, 0] = states[0]
            o_ref[:, ls] = o
            y, _, _ = _rms_fwd(o, nw_ref[...])
            rec_ref[:, ls] = (y * (hg * _sigmoid(hg))).astype(BF16)

        _in_step(head(hp) for hp in range(HG_HPS))

    width = HG_HPS * LANE
    head_out = pl.BlockSpec((None, HG_TOK, width), lambda b, h, t: (b, t, h))
    mix_out = pl.BlockSpec((None, HG_TOK, width), lambda b, h, t: (b, t, ATT_WIDTH // width + h))
    return _call(
        body, (lb_table, norm_w, proj, mix_in, masks[0]), name="hgrn_fwd", grid=(bsz, HG_HEADS // HG_HPS, nstep),
        in_specs=[pl.BlockSpec((2, width), lambda b, h, t: (0, h)), pl.BlockSpec((1, LANE), lambda b, h, t: (0, 0)),
                  pl.BlockSpec((None, HG_SLABS, HG_TOK, width), lambda b, h, t: (b, 0, t, h)), pl.BlockSpec(memory_space=pl.ANY),
                  pl.BlockSpec((HG_TOK, HG_TOK), lambda b, h, t: (0, 0))],
        out_specs=[head_out, mix_out,
                   pl.BlockSpec((None, HG_HPS, 1, LANE, LANE), lambda b, h, t: (b, h, t, 0, 0))],
        out_shape=[_sds((bsz, seq, HG_WIDTH), F32), _sds(mix_in.shape, BF16),
                   _sds((bsz, HG_HEADS, nstep, LANE, LANE), F32)],
        scratch_shapes=[pltpu.VMEM((HG_HPS, LANE, LANE), F32)],
        sem=("parallel", "parallel", "arbitrary"), comms=comms, aliases={3: 1})


def _out_proj_fused(cat, w_out, x, post_w, g1, pre_w, sc2, sh2):
    tm = 512

    def epi(mix, ex, outs):
        x_ref, pw_ref, g1_ref, w2_ref, sc_ref, sh_ref = ex
        outs[0][...] = mix
        n1, _, _ = _rms_fwd(mix, pw_ref[...])
        x1 = x_ref[...] + g1_ref[...] * n1
        outs[1][...] = x1
        y2, _, _ = _rms_fwd(x1, w2_ref[...])
        outs[2][...] = (y2 * (1.0 + sc_ref[...]) + sh_ref[...]).astype(BF16)

    return _mm_rows(cat, w_out, name="out_proj", tm=tm, extra=(x, post_w, g1, pre_w, sc2, sh2),
                    extra_specs=[_tok_spec(tm), _vec_spec(), _row_spec(), _vec_spec(), _row_spec(), _row_spec()],
                    out_specs=[_tok_spec(tm), _tok_spec(tm), _tok_spec(tm)],
                    out_shape=[_sds(x.shape, F32), _sds(x.shape, F32), _sds(x.shape, BF16)], epi=epi)


def _acc_out(ref, first, value):
    @pl.when(first)
    def _():
        ref[...] = value

    @pl.when(jnp.logical_not(first))
    def _():
        ref[...] += value


def _down_proj_fused(r, w_down, x1, post_w, g2, target):
    tm = 512
    bsz = x1.shape[0]

    def pro(r_ref, ex, outs):
        rv = r_ref[...]
        return rv * rv

    def epi(down, ex, outs):
        x1_ref, w_ref, g2_ref, t_ref = ex
        loss_ref, dy_ref, dd_ref, dg2_ref, dw_ref = outs
        w, g2v = w_ref[...], g2_ref[...]
        gain = g2v * w
        dh, rstd = _rms_hat(down)
        err = x1_ref[...] + dh * gain - t_ref[...]
        part = (0.5 / D_MODEL) * jnp.sum(jnp.sum(err * err, axis=-1, keepdims=True), axis=0, keepdims=True)
        loss_ref[...] += jnp.broadcast_to(part, (1, LANE))
        dy = err * (1.0 / D_MODEL)
        dy_ref[...] = dy
        dd, per_col = _rms_bwd_gain(dy, gain, dh, rstd)
        dd_ref[...] = dd.astype(BF16)
        dg2_ref[...] += per_col * w
        dw_ref[...] += per_col * g2v

    return _mm_rows(r, w_down, name="down_proj", tm=tm, extra=(x1, post_w, g2, target),
                    extra_specs=[_tok_spec(tm), _vec_spec(), _row_spec(), _tok_spec(tm)],
                    out_specs=[_vec_spec(LANE), _tok_spec(tm), _tok_spec(tm), _row_spec(), _vec_spec()],
                    out_shape=[_sds((1, LANE), F32), _sds(x1.shape, F32), _sds(x1.shape, BF16), _sds((bsz, 1, D_MODEL), F32),
                               _sds((1, D_MODEL), F32)], pro=pro, epi=epi, parts=2, zero_per_seq=(3,), zero_once=(0, 4),
                    vmem_limit=VMEM_LIMIT_BIG)


def _up_bwd_fused(dpre, w_up4, dy, x1, mix, pre_w, sc2, post_w, g1, comms=()):
    tm = 512
    bsz = x1.shape[0]

    def epi(dh2v, ex, outs):
        dy_ref, x1_ref, mix_ref, w2_ref, sc_ref, pw_ref, g1_ref = ex
        dx1_ref, dmix_ref, dsc_ref, dsh_ref, dg1_ref, dw2_ref, dpw_ref = outs
        w2, pw, g1v = w2_ref[...], pw_ref[...], g1_ref[...]
        mod2 = 1.0 + sc_ref[...]
        xh2, rstd2 = _rms_hat(x1_ref[...])
        dsh_ref[...] += _colsum(dh2v)
        dx1n, per_col2 = _rms_bwd_gain(dh2v, mod2 * w2, xh2, rstd2)
        dsc_ref[...] += per_col2 * w2
        dw2_ref[...] += per_col2 * mod2
        dx1 = dy_ref[...] + dx1n
        dx1_ref[...] = dx1
        mh, rstd1 = _rms_hat(mix_ref[...])
        dmix, per_col1 = _rms_bwd_gain(dx1, g1v * pw, mh, rstd1)
        dmix_ref[...] = dmix.astype(BF16)
        dg1_ref[...] += per_col1 * pw
        dpw_ref[...] += per_col1 * g1v

    row_shape = _sds((bsz, 1, D_MODEL), F32)
    vec_shape = _sds((1, D_MODEL), F32)
    return _mm_rows(dpre, w_up4, name="up_bwd", tm=tm, extra=(dy, x1, mix, pre_w, sc2, post_w, g1),
                    extra_specs=[_tok_spec(tm), _tok_spec(tm), _tok_spec(tm), _vec_spec(), _row_spec(), _vec_spec(), _row_spec()],
                    out_specs=[_tok_spec(tm), _tok_spec(tm), _row_spec(), _row_spec(), _row_spec(), _vec_spec(), _vec_spec()],
                    out_shape=[_sds(x1.shape, F32), _sds(x1.shape, BF16), row_shape, row_shape, row_shape, vec_shape, vec_shape],
                    epi=epi, b_chunks=w_up4.shape[0], comms=comms, parts=2, zero_per_seq=(2, 3, 4), zero_once=(5, 6),
                    vmem_limit=VMEM_LIMIT_BIG)


def _norm1_bwd(dh1, dx1, x, pre_w, sc1, tm=512, comms=()):
    bsz, seq, _ = x.shape

    def body(dh_ref, dx1_ref, x_ref, w_ref, sc_ref, gx_ref, dsc_ref, dsh_ref, dw_ref):
        b, i = pl.program_id(0), pl.program_id(1)
        w = w_ref[...]
        dh = dh_ref[...]
        mod = 1.0 + sc_ref[...]
        xh, rstd = _rms_hat(x_ref[...])
        dx, per_col = _rms_bwd_gain(dh, mod * w, xh, rstd)
        _acc_out(dsh_ref, i == 0, _colsum(dh))
        _acc_out(dsc_ref, i == 0, per_col * w)
        _acc_out(dw_ref, jnp.logical_and(b == 0, i == 0), per_col * mod)
        gx_ref[...] = dx1_ref[...] + dx

    row_shape = _sds((bsz, 1, D_MODEL), F32)
    return _call(
        body, (dh1, dx1, x, pre_w, sc1), name="norm1_bwd", grid=(bsz, seq // tm),
        in_specs=[_tok_spec(tm), _tok_spec(tm), _tok_spec(tm), _vec_spec(), _row_spec()],
        out_specs=[_tok_spec(tm), _row_spec(), _row_spec(), _vec_spec()],
        out_shape=[_sds(x.shape, F32), row_shape, row_shape, _sds((1, D_MODEL), F32)],
        sem=("arbitrary", "arbitrary"), comms=comms)


def _hgrn_bwd(dcat, proj, o_raw, states, lb_table, norm_w, masks, comms=()):
    bsz, _, seq, _ = proj.shape
    nstep = seq // HG_TOK
    rec0 = ATT_WIDTH // LANE
    width = HG_HPS * LANE
    slabs = (HG_Q0, HG_F0, HG_I0, HG_G0)
    n_steps = (HG_HEADS // HG_HPS) * bsz * nstep
    assert n_steps >= 2

    def body(tbl_ref, nw_ref, dr_ref, p_ref, o_ref, st_ref, lower_ref, upper_ref,
             dproj_ref, dlb_ref, dnw_ref, ds_scr, grad_buf, grad_sem):
        h, b, t = pl.program_id(0), pl.program_id(1), pl.program_id(2)
        step = (h * bsz + b) * nstep + t
        slot = step % 2
        dq_k, df_k, di_k, dg_k = range(4)

        def grad_copies(of_step):
            hh, bb, tt = of_step // (bsz * nstep), (of_step // nstep) % bsz, of_step % nstep
            rows = pl.ds(pl.multiple_of((nstep - 1 - tt) * HG_TOK, HG_TOK), HG_TOK)
            return [pltpu.make_async_copy(
                grad_buf.at[of_step % 2, k],
                dproj_ref.at[bb, rows, pl.ds(pl.multiple_of(slabs[k] * LANE + hh * width, width), width)],
                grad_sem.at[of_step % 2, k]) for k in range(4)]

        @pl.when(step >= 2)
        def _():
            for cp in grad_copies(step - 2):
                cp.wait()

        @pl.when(t == 0)
        def _():
            ds_scr[...] = jnp.zeros_like(ds_scr)

        lower, upper = lower_ref[...], upper_ref[...]
        dlb_parts, dnw_parts = [None] * HG_HPS, [None] * HG_HPS

        def head(hp):
            ls = slice(hp * LANE, (hp + 1) * LANE)
            hq, v, hg = p_ref[HG_Q, :, ls], p_ref[HG_I, :, ls], p_ref[HG_G, :, ls]
            nw = nw_ref[...]
            c = _hgrn_common(tbl_ref[:, ls], p_ref[HG_F, :, ls], hq)
            qd, kd, ku = c["qd"], c["kd"], c["ku"]
            yield
            y, on, rstd = _rms_fwd(o_ref[:, ls], nw)
            sg = _sigmoid(hg)
            dr = dr_ref[:, ls]
            grad_buf[slot, dg_k, :, ls] = (dr * y * (sg * (1.0 + hg * (1.0 - sg)))).astype(BF16)
            do, dnw_rows = _rms_bwd(dr * (hg * sg), on, rstd, nw)
            yield
            at = _dot(kd, qd, NT_DIMS) * upper
            da = _dot(do, v, NT_DIMS) * lower
            dat = _dot(v, do, NT_DIMS) * upper
            yield
            dv = _dot(at, do)
            dqd = _dot(da, kd)
            dkd = _dot(dat, qd)
            yield
            do_c, qd_c, v_c, ku_c = [_chunk_slices(z.astype(BF16)) for z in (do, qd, v, ku)]
            outer = [_dot(do_c[j], qd_c[j], TN_DIMS) for j in range(HG_NCH)]
            yield
            ds = ds_scr[hp]
            ds_after = [None] * HG_NCH
            for j in reversed(range(HG_NCH)):
                ds_after[j] = ds
                ds = outer[j] + ds * c["e_last"][j]
            ds_scr[hp] = ds
            yield
            updates = [_dot(v_c[j], ku_c[j], TN_DIMS) for j in range(HG_NCH)]
            yield
            states = [st_ref[hp, 0]]
            for j in range(HG_NCH - 1):
                states.append(states[j] * c["e_last"][j] + updates[j])
            dv = dv + jnp.concatenate([_dot(ku_c[j], ds_after[j], NT_DIMS) for j in range(HG_NCH)], axis=0)
            dqd = dqd + jnp.concatenate([_dot(do_c[j], states[j]) for j in range(HG_NCH)], axis=0)
            dku = jnp.concatenate([_dot(v_c[j], ds_after[j]) for j in range(HG_NCH)], axis=0)
            yield
            dku_ku = dku * ku
            dbl = [_colsum(states[j] * ds_after[j]) * c["e_last"][j] + _colsum(dku_ku[j * HG_CHUNK:(j + 1) * HG_CHUNK])
                   for j in range(HG_NCH)]
            dk = dkd * c["e_nb"] + dku * c["e_rem"]
            db = dqd * qd - dkd * kd - dku_ku + jnp.where(_row_in_chunk() == HG_CHUNK - 1, _chunk_rows(dbl), 0.0)
            dfv = _chunk_cumsum(db, reverse=True) / c["f"] - dk
            sig, sq = c["sig"], c["sq"]
            grad_buf[slot, df_k, :, ls] = (dfv * (1.0 - c["lb"]) * sig * (1.0 - sig)).astype(BF16)
            grad_buf[slot, dq_k, :, ls] = (dqd * c["e_b"] * (sq * (1.0 + hq * (1.0 - sq)))).astype(BF16)
            grad_buf[slot, di_k, :, ls] = dv.astype(BF16)
            dlb_parts[hp] = _colsum(dfv * (1.0 - sig))
            dnw_parts[hp] = _colsum(dnw_rows)

        _in_step(head(hp) for hp in range(HG_HPS))
        _acc_out(dlb_ref, jnp.logical_and(b == 0, t == 0), jnp.concatenate(dlb_parts, axis=1))
        _acc_out(dnw_ref, jnp.logical_and(h == 0, jnp.logical_and(b == 0, t == 0)), sum(dnw_parts[1:], dnw_parts[0]))
        for cp in grad_copies(step):
            cp.start()

        @pl.when(step == n_steps - 1)
        def _():
            for cp in grad_copies(step - 1) + grad_copies(step):
                cp.wait()

    rev = lambda t: nstep - 1 - t
    slab = lambda first: pl.BlockSpec((None, HG_TOK, width), lambda h, b, t: (b, rev(t), first // HG_HPS + h))
    head = pl.BlockSpec((None, HG_TOK, width), lambda h, b, t: (b, rev(t), h))
    return _call(
        body, (lb_table, norm_w, dcat, proj, o_raw, states, *masks), name="hgrn_bwd",
        grid=(HG_HEADS // HG_HPS, bsz, nstep),
        in_specs=[pl.BlockSpec((2, width), lambda h, b, t: (0, h)), pl.BlockSpec((1, LANE), lambda h, b, t: (0, 0)),
                  slab(rec0), pl.BlockSpec((None, HG_SLABS, HG_TOK, width), lambda h, b, t: (b, 0, rev(t), h)), head,
                  pl.BlockSpec((None, HG_HPS, 1, LANE, LANE), lambda h, b, t: (b, h, rev(t), 0, 0)),
                  pl.BlockSpec((HG_TOK, HG_TOK), lambda h, b, t: (0, 0)), pl.BlockSpec((HG_TOK, HG_TOK), lambda h, b, t: (0, 0))],
        out_specs=[pl.BlockSpec(memory_space=pl.ANY), pl.BlockSpec((1, width), lambda h, b, t: (0, h)),
                   pl.BlockSpec((1, LANE), lambda h, b, t: (0, 0))],
        out_shape=[_sds((bsz, seq, IN_COLS), BF16), _sds((1, HG_WIDTH), F32), _sds((1, LANE), F32)],
        scratch_shapes=[pltpu.VMEM((HG_HPS, LANE, LANE), F32), pltpu.VMEM((2, 4, HG_TOK, width), BF16),
                        pltpu.SemaphoreType.DMA((2, 4))],
        sem=("arbitrary", "arbitrary", "arbitrary"), comms=comms)


def _attn_bwd(dcat, raw, w_norm, qh, kh, vh, lse, sinks, tables, biases, dproj, comms=()):
    bsz, nblk = qh.shape[0], qh.shape[1]
    seq = nblk * WINDOW
    nstep = nblk // ATT_BPS
    half = ROPE_DIM // 2

    def body(sink_ref, da_ref, raw_ref, w_ref, q_ref, kc_ref, kp_ref, vc_ref, vp_ref, l_ref, c_ref, u_ref, d_ref,
             full_ref, first_ref, dproj_ref, o_ref, dw_ref, dsink_ref, carry_k, carry_v):
        b, i = pl.program_id(0), pl.program_id(1)
        first = jnp.logical_and(b == 0, i == 0)

        @pl.when(i == 0)
        def _():
            carry_k[...] = jnp.zeros_like(carry_k)
            carry_v[...] = jnp.zeros_like(carry_v)

        w = w_ref[...]
        _, on, rstd = _rms_fwd(raw_ref[...], w)
        do_step, dw_rows = _rms_bwd(da_ref[...], on, rstd, w)
        _acc_out(dw_ref, first, _colsum(dw_rows))
        lane8 = lax.broadcasted_iota(jnp.int32, (1, ATT_Q_HEADS), 1)
        dsink = jnp.zeros((1, ATT_Q_HEADS), F32)
        head_cols = jnp.where(lax.broadcasted_iota(jnp.int32, (2 * ATT_Q_HEADS, ATT_WIDTH), 1) // ATT_HEAD_DIM
                              == lax.broadcasted_iota(jnp.int32, (2 * ATT_Q_HEADS, ATT_WIDTH), 0), 1.0, 0.0)
        from_next_k, from_next_v = carry_k[...], carry_v[...]
        for blk in reversed(range(ATT_BPS)):
            tok = slice(blk * WINDOW, (blk + 1) * WINDOW)
            bias = _band_bias(full_ref, first_ref, True if blk else i < nstep - 1)
            do_all = do_step[tok]
            c, u, d = c_ref[tok, :], u_ref[tok, :], d_ref[tok, :]
            lse_t = l_ref[tok, :].T
            prod = do_all * raw_ref[tok, :]
            prod_hi = prod.astype(BF16)
            prod_lo = prod - prod_hi.astype(F32)
            dsum_t = _dot(head_cols, prod_hi, NT_DIMS) + _dot(head_cols, prod_lo, NT_DIMS)

            def unrope(g):
                return (g * c + pltpu.roll(g * u, LANE - half, 1) + pltpu.roll(g * d, half, 1)).astype(BF16)

            groups = range(ATT_KV_HEADS)
            group_row = lambda z, g: jnp.concatenate(
                [z[g * ATT_GROUP + hh:g * ATT_GROUP + hh + 1, :] for hh in range(ATT_GROUP)], axis=1)
            q = [q_ref[blk, g] for g in groups]
            keys, vals = [_band(kp_ref, kc_ref, g, blk) for g in groups], [_band(vp_ref, vc_ref, g, blk) for g in groups]
            do_g = [jnp.concatenate([do_all[:, (g * ATT_GROUP + hh) * ATT_HEAD_DIM:(g * ATT_GROUP + hh + 1) * ATT_HEAD_DIM]
                                     for hh in range(ATT_GROUP)], axis=0) for g in groups]
            dsum, lse_g = [group_row(dsum_t, g) for g in groups], [group_row(lse_t, g) for g in groups]
            s_t = [_dot(keys[g], q[g], NT_DIMS) for g in groups]
            dp_t = [_dot(vals[g], do_g[g], NT_DIMS) for g in groups]
            p_t = [jnp.exp(s_t[g] * ATT_SCALE + bias - lse_g[g]) for g in groups]
            ds_t = [p_t[g] * (dp_t[g] - dsum[g]) * ATT_SCALE for g in groups]
            dq_g = [_dot(ds_t[g], keys[g], TN_DIMS) for g in groups]
            dk_g = [_dot(ds_t[g], q[g]) for g in groups]
            dv_g = [_dot(p_t[g], do_g[g]) for g in groups]
            for g in groups:
                sink_part = jnp.exp(_sink_row(sink_ref, g) - lse_g[g]) * dsum[g]
                for hh in range(ATT_GROUP):
                    head_sum = jnp.sum(sink_part[:, hh * WINDOW:(hh + 1) * WINDOW], axis=1, keepdims=True)
                    dsink = dsink - jnp.where(lane8 == g * ATT_GROUP + hh, head_sum, 0.0)
            dq_parts = [dq_g[g][hh * WINDOW:(hh + 1) * WINDOW] for g in groups for hh in range(ATT_GROUP)]
            dk_before, dk_own = [z[:WINDOW] for z in dk_g], [z[WINDOW:] for z in dk_g]
            dv_before, dv_own = [z[:WINDOW] for z in dv_g], [z[WINDOW:] for z in dv_g]
            per_slab = LANE // ATT_HEAD_DIM
            for s in range(ATT_WIDTH // LANE):
                slab = jnp.concatenate(dq_parts[s * per_slab:(s + 1) * per_slab], axis=1)
                o_ref[tok, s * LANE:(s + 1) * LANE] = unrope(slab)
            o_ref[tok, ATT_WIDTH:ATT_WIDTH + LANE] = unrope(jnp.concatenate(dk_own, axis=1) + from_next_k)
            o_ref[tok, ATT_WIDTH + LANE:ATT_COLS] = (jnp.concatenate(dv_own, axis=1) + from_next_v).astype(BF16)
            from_next_k, from_next_v = jnp.concatenate(dk_before, axis=1), jnp.concatenate(dv_before, axis=1)
        carry_k[...] = from_next_k
        carry_v[...] = from_next_v
        _acc_out(dsink_ref, first, dsink)

    rows = ATT_BPS * WINDOW
    rev = lambda i: nstep - 1 - i
    cur = lambda width: pl.BlockSpec((None, rows, width), lambda b, i: (b, rev(i), 0))
    q_spec = pl.BlockSpec((None, ATT_BPS, ATT_KV_HEADS, GROUP_ROWS, ATT_HEAD_DIM), lambda b, i: (b, rev(i), 0, 0, 0))
    kv_cur = pl.BlockSpec((None, ATT_KV_HEADS, rows, ATT_HEAD_DIM), lambda b, i: (b, 0, rev(i), 0))
    kv_prev = pl.BlockSpec((None, ATT_KV_HEADS, WINDOW, ATT_HEAD_DIM), lambda b, i: (b, 0, jnp.maximum(ATT_BPS * rev(i) - 1, 0), 0))
    tab = pl.BlockSpec((rows, LANE), lambda b, i: (rev(i), 0))
    return _call(
        body, (sinks, dcat, raw, w_norm, qh, kh, kh, vh, vh, lse, *tables, *biases, dproj), name="attn_bwd", grid=(bsz, nstep),
        in_specs=[pl.BlockSpec(memory_space=pltpu.SMEM), cur(ATT_WIDTH), cur(ATT_WIDTH), _vec_spec(ATT_WIDTH), q_spec,
                  kv_cur, kv_prev, kv_cur, kv_prev, cur(LANE), tab, tab, tab, _bias_spec(True), _bias_spec(True),
                  pl.BlockSpec(memory_space=pl.ANY)],
        out_specs=[cur(ATT_COLS), _vec_spec(ATT_WIDTH), _vec_spec(ATT_Q_HEADS)],
        out_shape=[_sds(dproj.shape, BF16), _sds((1, ATT_WIDTH), F32), _sds((1, ATT_Q_HEADS), F32)],
        scratch_shapes=[pltpu.VMEM((WINDOW, LANE), F32), pltpu.VMEM((WINDOW, LANE), F32)],
        sem=("arbitrary", "arbitrary"), comms=comms, aliases={15: 0})


def _other_chips(x, y):
    return [(1 - x, y), (x, 1 - y), (1 - x, 1 - y)]


def _sem_pair(n):
    return [pltpu.SemaphoreType.DMA((n,)), pltpu.SemaphoreType.DMA((n,))]


def _plan_pair_forward(bufs):
    n = len(bufs)

    def copies(outs, sems):
        x, y, c = _mesh_pos()
        sends, lands = [], []
        for a in range(n):
            for j, chip in enumerate(_other_chips(x, y)):
                k = 3 * a + j
                slot = outs[a].at[4 * chip[0] + 2 * chip[1] + c]
                sends.append(pltpu.make_async_remote_copy(
                    src_ref=slot, dst_ref=slot, send_sem=sems[0].at[k], recv_sem=sems[1].at[k],
                    device_id=(x, y, 1 - c), device_id_type=MESH))
                theirs = outs[a].at[4 * chip[0] + 2 * chip[1] + 1 - c]
                lands.append(pltpu.make_async_remote_copy(
                    src_ref=theirs, dst_ref=theirs, send_sem=sems[0].at[k], recv_sem=sems[1].at[k],
                    device_id=(x, y, 1 - c), device_id_type=MESH))
        return sends, lands

    def start(ins, outs, sems):
        for cp in copies(outs, sems)[0]:
            cp.start()

    def finish(ins, outs, sems):
        sends, lands = copies(outs, sems)
        for cp in lands:
            cp.wait_recv()
        for cp in sends:
            cp.wait_send()

    return _Comm(list(bufs), [_sds(b.shape, b.dtype) for b in bufs], _sem_pair(3 * n), start, finish,
                 aliases=[(a, a) for a in range(n)])


def _plan_pair(arrays, other_half):
    n = len(arrays)
    per = N_CHIPS if other_half == "chip_major" else 1

    def copies(ins, outs, sems):
        x, y, c = _mesh_pos()
        out = []
        for a in range(n):
            for k in range(per):
                if other_half == "chip_major":
                    src, dst = ins[a].at[k, 1 - c], outs[a].at[k]
                else:
                    src, dst = (ins[a].at[1 - c] if other_half else ins[a]), outs[a]
                out.append(pltpu.make_async_remote_copy(
                    src_ref=src, dst_ref=dst, send_sem=sems[0].at[per * a + k], recv_sem=sems[1].at[per * a + k],
                    device_id=(x, y, 1 - c), device_id_type=MESH))
        return out

    def start(ins, outs, sems):
        for cp in copies(ins, outs, sems):
            cp.start()

    def finish(ins, outs, sems):
        for cp in copies(ins, outs, sems):
            cp.wait()

    if other_half == "chip_major":
        shapes = [_sds((a.shape[0],) + a.shape[2:], a.dtype) for a in arrays]
    else:
        shapes = [_sds(a.shape[1:] if other_half else a.shape, a.dtype) for a in arrays]
    return _Comm(list(arrays), shapes, _sem_pair(per * n), start, finish)


def _plan_chip_exchange(arrays):
    n = len(arrays)

    def copies(ins, outs, sems):
        x, y, c = _mesh_pos()
        sends, lands = [], []
        for a in range(n):
            for j, chip in enumerate(_other_chips(x, y)):
                k = 3 * a + j
                sends.append(pltpu.make_async_remote_copy(
                    src_ref=ins[a].at[2 * chip[0] + chip[1]], dst_ref=outs[a].at[2 * x + y], send_sem=sems[0].at[k],
                    recv_sem=sems[1].at[k], device_id=(*chip, c), device_id_type=MESH))
                slot = outs[a].at[2 * chip[0] + chip[1]]
                lands.append(pltpu.make_async_remote_copy(
                    src_ref=slot, dst_ref=slot, send_sem=sems[0].at[k], recv_sem=sems[1].at[k],
                    device_id=(*chip, c), device_id_type=MESH))
        return sends, lands

    def start(ins, outs, sems):
        for cp in copies(ins, outs, sems)[0]:
            cp.start()

    def finish(ins, outs, sems):
        sends, lands = copies(ins, outs, sems)
        for cp in lands:
            cp.wait_recv()
        for cp in sends:
            cp.wait_send()

    return _Comm(list(arrays), [_sds(a.shape, a.dtype) for a in arrays], _sem_pair(3 * n), start, finish)


SEM_SPEC = pl.BlockSpec(memory_space=pltpu.SEMAPHORE)
N_OTHER = N_CHIPS - 1


def _exchange_copies(s_ref, land_ref, sems):
    x, y, c = _mesh_pos()
    return [pltpu.make_async_remote_copy(
        src_ref=s_ref.at[2 * chip[0] + chip[1]], dst_ref=land_ref.at[2 * x + y], send_sem=sems[j], recv_sem=sems[N_OTHER + j],
        device_id=(*chip, c), device_id_type=MESH) for j, chip in enumerate(_other_chips(x, y))]


def _exchange_start(s, name):
    def body(s_ref, land_ref, *outs):
        sems, token = outs[:2 * N_OTHER], outs[-1]
        for cp in _exchange_copies(s_ref, land_ref, sems):
            cp.start()
        token[...] = jnp.zeros_like(token)

    hbm = pltpu.HBM(s.shape, s.dtype)
    res = pl.pallas_call(
        body, name=name,
        out_shape=(pltpu.SemaphoreType.DMA(()),) * (2 * N_OTHER) + (hbm, hbm, _sds((SUBLANES, LANE), F32)),
        in_specs=(HBM_SPEC, HBM_SPEC),
        out_specs=(SEM_SPEC,) * (2 * N_OTHER) + (HBM_SPEC, HBM_SPEC, pl.BlockSpec(memory_space=pltpu.VMEM)),
        input_output_aliases={0: 2 * N_OTHER, 1: 2 * N_OTHER + 1},
        compiler_params=pltpu.CompilerParams(has_side_effects=pltpu.SideEffectType.DATAFLOW_SIDE_EFFECTING),
    )(pltpu.with_memory_space_constraint(s, pltpu.HBM), pltpu.with_memory_space_constraint(lax.empty(s.shape, s.dtype), pltpu.HBM))
    return res[:2 * N_OTHER], res[2 * N_OTHER], res[2 * N_OTHER + 1], res[-1]


def _exchange_wait(sems, s_thru, land_thru, afters, name):
    def body(s_ref, land_ref, *rest):
        for cp in _exchange_copies(s_ref, land_ref, rest[:2 * N_OTHER]):
            cp.wait_send()
            cp.wait_recv()

    hbm = pltpu.HBM(s_thru.shape, s_thru.dtype)
    return pl.pallas_call(
        body, name=name, out_shape=(hbm, hbm),
        in_specs=(HBM_SPEC, HBM_SPEC) + (SEM_SPEC,) * (2 * N_OTHER) + (pl.BlockSpec(memory_space=pl.ANY),) * len(afters),
        out_specs=(HBM_SPEC, HBM_SPEC), input_output_aliases={0: 0, 1: 1},
        compiler_params=pltpu.CompilerParams(has_side_effects=pltpu.SideEffectType.DATAFLOW_SIDE_EFFECTING),
    )(s_thru, land_thru, *sems, *afters)


def _gather_copies(block_ref, buf_ref, sems):
    x, y, c = _mesh_pos()
    return [pltpu.make_async_remote_copy(
        src_ref=block_ref, dst_ref=buf_ref.at[4 * x + 2 * y + c], send_sem=sems[j], recv_sem=sems[N_OTHER + j],
        device_id=(*chip, c), device_id_type=MESH) for j, chip in enumerate(_other_chips(x, y))]


def _gather_start(blocks, bufs, afters, name):
    n = len(blocks)
    per = 2 * N_OTHER

    def body(*refs):
        ins, outs = refs[:2 * n], refs[2 * n + len(afters):]
        for a in range(n):
            for cp in _gather_copies(ins[a], ins[n + a], outs[a * per:(a + 1) * per]):
                cp.start()
        outs[-1][...] = jnp.zeros_like(outs[-1])

    hbm = [pltpu.HBM(z.shape, z.dtype) for z in list(blocks) + list(bufs)]
    res = pl.pallas_call(
        body, name=name,
        out_shape=(pltpu.SemaphoreType.DMA(()),) * (n * per) + tuple(hbm) + (_sds((SUBLANES, LANE), F32),),
        in_specs=(HBM_SPEC,) * (2 * n) + (pl.BlockSpec(memory_space=pl.ANY),) * len(afters),
        out_specs=(SEM_SPEC,) * (n * per) + (HBM_SPEC,) * (2 * n) + (pl.BlockSpec(memory_space=pltpu.VMEM),),
        input_output_aliases={k: n * per + k for k in range(2 * n)},
        compiler_params=pltpu.CompilerParams(has_side_effects=pltpu.SideEffectType.DATAFLOW_SIDE_EFFECTING),
    )(*[pltpu.with_memory_space_constraint(z, pltpu.HBM) for z in list(blocks) + list(bufs)], *afters)
    parts = [(res[a * per:(a + 1) * per], res[n * per + a], res[n * per + n + a]) for a in range(n)]
    return parts, res[-1]


def _gather_wait(part, afters, name):
    sems, block, buf = part

    def body(block_ref, buf_ref, *rest):
        for cp in _gather_copies(block_ref, buf_ref, rest[:2 * N_OTHER]):
            cp.wait_send()
            cp.wait_recv()

    return pl.pallas_call(
        body, name=name, out_shape=(pltpu.HBM(block.shape, block.dtype), pltpu.HBM(buf.shape, buf.dtype)),
        in_specs=(HBM_SPEC, HBM_SPEC) + (SEM_SPEC,) * (2 * N_OTHER) + (pl.BlockSpec(memory_space=pl.ANY),) * len(afters),
        out_specs=(HBM_SPEC, HBM_SPEC), input_output_aliases={0: 0, 1: 1},
        compiler_params=pltpu.CompilerParams(has_side_effects=pltpu.SideEffectType.DATAFLOW_SIDE_EFFECTING),
    )(block, buf, *sems, *afters)[1]


def _comm_only(comms, name):
    return _call(lambda: None, (), name=name, grid=(), in_specs=[], out_specs=[], out_shape=[], sem=(), comms=comms)[1]


def _allgather8(arrays, name):
    return _comm_only([_plan_allgather8(arrays)], name)[0]


def _plan_allgather8(arrays):
    n = len(arrays)

    def parts(ins, outs, sems):
        send_sems, recv_sems, local_sems = sems
        x, y, c = _mesh_pos()
        me, sibling = (x, y, c), (x, y, 1 - c)
        chips = _other_chips(x, y)

        def copy(a, k, block, to, src=None):
            dst = outs[a].at[4 * block[0] + 2 * block[1] + block[2]]
            return pltpu.make_async_remote_copy(
                src_ref=dst if src is None else src, dst_ref=dst, send_sem=send_sems.at[7 * a + k],
                recv_sem=recv_sems.at[7 * a + k], device_id=to, device_id_type=MESH)

        mine = [pltpu.make_async_copy(ins[a], outs[a].at[4 * x + 2 * y + c], local_sems.at[a]) for a in range(n)]
        first = []
        for a in range(n):
            first.append(copy(a, 0, me, sibling, src=ins[a]))
            first += [copy(a, 1 + j, me, (*chip, c), src=ins[a]) for j, chip in enumerate(chips)]
        return copy, mine, first, me, sibling, chips, c

    def start(ins, outs, sems):
        _, mine, first, *_ = parts(ins, outs, sems)
        for cp in mine + first:
            cp.start()

    def finish(ins, outs, sems):
        copy, mine, first, me, sibling, chips, c = parts(ins, outs, sems)
        passed = []
        for j, chip in enumerate(chips):
            for a in range(n):
                copy(a, 1 + j, (*chip, c), me).wait_recv()
                fwd = copy(a, 4 + j, (*chip, c), sibling)
                fwd.start()
                passed.append(fwd)
        for a in range(n):
            copy(a, 0, sibling, me).wait_recv()
            for j, chip in enumerate(chips):
                copy(a, 4 + j, (*chip, 1 - c), me).wait_recv()
        for cp in first + passed:
            cp.wait_send()
        for cp in mine:
            cp.wait()

    sems = [pltpu.SemaphoreType.DMA((7 * n,)), pltpu.SemaphoreType.DMA((7 * n,)), pltpu.SemaphoreType.DMA((n,))]
    return _Comm(list(arrays), [_sds((N_DEV,) + a.shape, a.dtype) for a in arrays], sems, start, finish)


def _pair_sum(g, q, core, name, chip_major=False):
    rows, cols = g.shape[2:]
    tr = _row_tile(rows)

    def body(core_ref, g_ref, q_ref, o_ref):
        o_ref[...] = (g_ref[...] + q_ref[...]).astype(BF16)

    blk = pl.BlockSpec((None, tr, cols), lambda k, i, core_ref: (k, i, 0))
    if chip_major:
        own = pl.BlockSpec((None, None, tr, cols), lambda k, i, core_ref: (k, core_ref[0], i, 0))
    else:
        own = pl.BlockSpec((None, None, tr, cols), lambda k, i, core_ref: (core_ref[0], k, i, 0))
    return pl.pallas_call(
        body, name=name,
        grid_spec=pltpu.PrefetchScalarGridSpec(num_scalar_prefetch=1, grid=(N_CHIPS, rows // tr), in_specs=[own, blk], out_specs=blk),
        out_shape=_sds((N_CHIPS, rows, cols), BF16), compiler_params=_params("parallel", "parallel"),
    )(core, g, q)


def _sum_chips(own, landed, chip, name):
    _, rows, cols = own.shape
    tr = _row_tile(rows)

    def body(chip_ref, own_ref, a_ref, b_ref, c_ref, o_ref):
        acc = own_ref[...].astype(F32) + a_ref[...].astype(F32)
        o_ref[...] = (acc + b_ref[...].astype(F32)) + c_ref[...].astype(F32)

    blk = lambda flip: pl.BlockSpec((None, tr, cols), lambda i, chip_ref: (jnp.bitwise_xor(chip_ref[0], flip), i, 0))
    return pl.pallas_call(
        body, name=name,
        grid_spec=pltpu.PrefetchScalarGridSpec(num_scalar_prefetch=1, grid=(rows // tr,), in_specs=[blk(0), blk(1), blk(2), blk(3)],
                                               out_specs=pl.BlockSpec((tr, cols), lambda i, chip_ref: (i, 0))),
        out_shape=_sds((rows, cols), F32), compiler_params=_params("parallel"),
    )(chip, own, landed, landed, landed)


SUBLANES = 8


def _tile_rows(n_elems):
    return -(-n_elems // (SUBLANES * LANE)) * SUBLANES


SMALL_ITEMS = (("b_ada", N_MOD * D_MODEL), ("pre_w_mix", D_MODEL), ("post_w_mix", D_MODEL), ("pre_w_mlp", D_MODEL),
               ("post_w_mlp", D_MODEL), ("attn_out_w", ATT_WIDTH), ("hg_norm_w", HG_HEAD_DIM), ("attn_sinks", ATT_Q_HEADS),
               ("lb_0", HG_WIDTH), ("lb_1", HG_WIDTH))
SMALL_AT = {}
for _name, _size in SMALL_ITEMS:
    SMALL_AT[_name] = (sum(r for _, r in SMALL_AT.values()), _tile_rows(_size))
SMALL_ROWS = sum(r for _, r in SMALL_AT.values())
MOD_ROWS = SMALL_AT["b_ada"][1]
PLAIN_ROWS = SMALL_AT["lb_0"][0] - MOD_ROWS
LB_ROWS = SMALL_AT["lb_0"][1]


def _rows(a, nrows=None):
    flat = a.reshape(-1)
    nrows = _tile_rows(flat.shape[0]) if nrows is None else nrows
    return jnp.pad(flat, (0, nrows * LANE - flat.shape[0])).reshape(nrows, LANE)


def _pack_small(vals):
    vals = dict(vals, lb_0=vals["lb_table"][0], lb_1=vals["lb_table"][1])
    return jnp.concatenate([_rows(vals[name], SMALL_AT[name][1]) for name, _ in SMALL_ITEMS], axis=0)


def _unpack_small(p):
    def item(name, shape):
        first = SMALL_AT[name][0]
        size = shape[0] * shape[1]
        return p[first:first + SMALL_AT[name][1]].reshape(-1)[:size].reshape(shape)

    out = {name: item(name, (1, size)) for name, size in SMALL_ITEMS if not name.startswith("lb_")}
    out["lb_table"] = jnp.concatenate([item("lb_0", (1, HG_WIDTH)), item("lb_1", (1, HG_WIDTH))], axis=0)
    return out


def _pack_partials(dmod, plain, d_lb, loss_row):
    return jnp.concatenate([_rows(dmod, dmod.shape[0] * MOD_ROWS)] + [_rows(g) for g in plain] + [_rows(d_lb), _rows(loss_row)], axis=0)


def _small_update(packs, w, m, v, n_seq):
    mod_end = n_seq * MOD_ROWS
    lb_at = mod_end + PLAIN_ROWS
    t0, t1 = SMALL_AT["lb_0"][0], SMALL_AT["lb_1"][0]

    def body(p_ref, w_ref, m_ref, v_ref, g_ref, dl_ref, nm_ref, nv_ref, loss_ref):
        tot = p_ref[0]
        for d in range(1, N_DEV):
            tot = tot + p_ref[d]
        wv = w_ref[...]
        p1 = _sigmoid(wv[t1:t1 + LB_ROWS] - wv[t0:t0 + LB_ROWS])
        s = tot[lb_at:lb_at + LB_ROWS] * p1 * (1.0 - p1)
        g_bias = tot[0:MOD_ROWS]
        for q in range(1, n_seq):
            g_bias = g_bias + tot[q * MOD_ROWS:(q + 1) * MOD_ROWS]
        g = jnp.concatenate([g_bias, tot[mod_end:lb_at], -s, s], axis=0)
        g_ref[...] = g
        dl_ref[...], nm_ref[...], nv_ref[...] = _adamw_math(g, wv, m_ref[...], v_ref[...])
        loss_ref[...] = tot[lb_at + LB_ROWS:lb_at + LB_ROWS + SUBLANES]

    shp = _sds((SMALL_ROWS, LANE), F32)
    return pl.pallas_call(body, name="small_update", out_shape=[shp] * 4 + [_sds((SUBLANES, LANE), F32)],
                          compiler_params=_params())(packs, w, m, v)


def kernel(x, c, w_ada, b_ada, pre_w_mix, w_in, attn_sinks, attn_out_w, lb_table, hg_norm_w, w_out, post_w_mix, pre_w_mlp, w_up, w_down, post_w_mlp, loss_target, m_w_ada, m_b_ada, m_pre_w_mix, m_w_in, m_attn_sinks, m_attn_out_w, m_lb_table, m_hg_norm_w, m_w_out, m_post_w_mix, m_pre_w_mlp, m_w_up, m_w_down, m_post_w_mlp, v_w_ada, v_b_ada, v_pre_w_mix, v_w_in, v_attn_sinks, v_attn_out_w, v_lb_table, v_hg_norm_w, v_w_out, v_post_w_mix, v_pre_w_mlp, v_w_up, v_w_down, v_post_w_mlp):
    xi, yi, ci = _mesh_pos()
    chip = 2 * xi + yi
    dev = 2 * chip + ci
    bsz, seq, _ = x.shape
    ntok = bsz * seq
    ada_cols = w_ada.shape[2]
    core = jnp.reshape(ci, (1,)).astype(jnp.int32)
    chip_idx = jnp.reshape(chip, (1,)).astype(jnp.int32)
    flat = lambda a: a.reshape(ntok, a.shape[-1])
    unflat = lambda a: a.reshape(bsz, seq, a.shape[-1])
    tables = _rope_tables(seq)
    biases, chunk_masks = _band_biases(), _block_masks()

    def row_half(w):
        rows = w.shape[1] // 2
        return lax.dynamic_slice_in_dim(w[0], ci * rows, rows, axis=0).astype(BF16)

    def gather_buffer(w):
        rows, cols = w.shape[1] // 2, w.shape[2]
        own = w[0].astype(BF16).reshape(2, rows, cols)
        return lax.dynamic_update_slice(lax.empty((N_DEV, rows, cols), BF16), own, (2 * chip, 0, 0))

    w_in_t, m_in_t, v_in_t = [jnp.transpose(a[0])[None] for a in (w_in, m_w_in, v_w_in)]
    c_g, in_g = _allgather8([c, row_half(w_in_t)], "gather_first")
    c_all = c_g.reshape(N_DEV * bsz, D_MODEL)
    w_in_full = in_g.reshape(IN_COLS, D_MODEL)

    b_cols = lax.dynamic_slice_in_dim(b_ada, chip * ada_cols, ada_cols, axis=1)
    mod_part = _ada_fwd(c_all, w_ada[0], b_cols)
    half_rows = mod_part.shape[0] // 2
    (mod_g,) = _allgather8([lax.dynamic_slice_in_dim(mod_part, ci * half_rows, half_rows, axis=0)], "gather_mod")
    mod_all = mod_g.reshape(N_CHIPS, 2, half_rows, ada_cols).transpose(1, 2, 0, 3).reshape(N_DEV * bsz, N_MOD * D_MODEL)
    mod = lax.dynamic_slice_in_dim(mod_all, dev * bsz, bsz, axis=0)
    sh1, sc1, g1, sh2, sc2, g2 = [mod[:, i * D_MODEL:(i + 1) * D_MODEL].reshape(bsz, 1, D_MODEL) for i in range(N_MOD)]

    weights = (w_out, w_up, w_down)
    (out_part, up_part, down_part), started = _gather_start(
        [row_half(w) for w in weights], [gather_buffer(w) for w in weights], [mod_g], "gather_weights_start")

    h1, proj, qh, kh, vh = _in_proj_fused(x, pre_w_mix, sc1 + started[0:1, 0:1], sh1, w_in_full, tables)
    out_g = _gather_wait(out_part, [proj], "gather_out_wait")
    (attn_raw, cat, lse), ((out_g,),) = _attn_fwd(qh, kh, vh, attn_sinks, attn_out_w, biases, comms=[_plan_pair_forward([out_g])])
    up_g = _gather_wait(up_part, [attn_raw], "gather_up_wait")
    (o_raw, cat, states), ((up_g,),) = _hgrn_fwd(proj, lb_table, hg_norm_w, cat, chunk_masks, comms=[_plan_pair_forward([up_g])])
    down_g = _gather_wait(down_part, [o_raw], "gather_down_wait")
    w_out_full = out_g.reshape(D_MODEL, D_MODEL)
    w_up4 = up_g.reshape(N_CHIPS, D_MODEL, D_MODEL)
    mix, x1, h2 = _out_proj_fused(cat, w_out_full, x, post_w_mix, g1, pre_w_mlp, sc2, sh2)
    big_tm = min(ntok, 2048)
    up_spec = pl.BlockSpec((None, D_MODEL, D_MODEL), lambda i, j: (j, 0, 0))
    r, ((down_g,),) = _mm(flat(h2), w_up4, name="up_proj", out_dtype=BF16, tm=big_tm, tn=D_MODEL, n_out=D_FF, b_spec=up_spec,
                          epi=lambda acc: jnp.maximum(acc, 0.0), comms=[_plan_pair_forward([down_g])])
    w_down_full = down_g.reshape(D_FF, D_MODEL)
    square = lambda t: t * t
    loss_row, dy, dd, dg2, d_post_mlp = _down_proj_fused(unflat(r), w_down_full, x1, post_w_mlp, g2, loss_target)

    dpre = _mm(flat(dd), w_down_full, name="down_bwd", out_dtype=BF16, trans_b=True, tm=big_tm, tn=D_MODEL, extra=(r,),
               epi=lambda acc, rt: acc * (2.0 * rt.astype(F32)))
    half_rows = D_MODEL // 2
    g_down = _mm_tn(r, flat(dd), name="down_wgrad", tk=half_rows, tn=D_MODEL, a_fn=square,
                    out_shape=_sds((2, N_CHIPS, half_rows, D_MODEL), F32),
                    out_spec=pl.BlockSpec((None, None, half_rows, D_MODEL), lambda i, j: (i % 2, i // 2, 0, 0)))
    (dx1, dmix, dsc2, dsh2, dg1, d_pre_mlp, d_post_mix), ((q_down,),) = _up_bwd_fused(
        unflat(dpre), w_up4, dy, x1, mix, pre_w_mlp, sc2, post_w_mix, g1, comms=[_plan_pair([g_down], True)])
    g_up = _mm_tn(flat(h2), dpre, name="up_wgrad", tk=D_MODEL, tn=half_rows,
                  out_shape=_sds((2, N_CHIPS, half_rows, D_MODEL), F32),
                  out_spec=pl.BlockSpec((2, None, half_rows, half_rows), lambda i, j: (0, j // 2, 0, j % 2)))
    s_down = _pair_sum(g_down, q_down, core, "pair_sum_down")

    out_rows = D_MODEL // N_CHIPS
    g_out, ((q_up,),) = _mm_tn(flat(cat), flat(dmix), name="out_wgrad", tk=2 * out_rows, tn=half_rows,
                               out_shape=_sds((2, N_CHIPS, out_rows, half_rows), F32),
                               out_spec=pl.BlockSpec((None, 2, out_rows, half_rows), lambda i, j: (j, i, 0, 0)),
                               comms=[_plan_pair([g_up], True)])
    s_up = _pair_sum(g_up, q_up, core, "pair_sum_up")
    dcat = unflat(_mm(flat(dmix), w_out_full, name="out_bwd", out_dtype=F32, trans_b=True))
    (dproj_rec, d_lb, d_hg_norm), ((x_down,), (q_out,)) = _hgrn_bwd(
        dcat, proj, o_raw, states, lb_table, hg_norm_w, chunk_masks, comms=[_plan_chip_exchange([s_down]), _plan_pair([g_out], True)])
    half_down = _sum_chips(s_down, x_down, chip_idx, "sum_chips_down")
    s_out = _pair_sum(g_out, q_out, core, "pair_sum_out")
    (dproj, d_attn_out, d_sinks), ((their_down,), (x_up,)) = _attn_bwd(
        dcat, attn_raw, attn_out_w, qh, kh, vh, lse, attn_sinks, tables, [bias.T for bias in biases], dproj_rec,
        comms=[_plan_pair([half_down], False), _plan_chip_exchange([s_up])])
    half_up = _sum_chips(s_up, x_up, chip_idx, "sum_chips_up")
    dproj = flat(dproj)
    in_rows = IN_COLS // N_CHIPS // 2
    g_in, ((x_out,),) = _mm_tn(dproj, flat(h1), name="in_wgrad", tk=2 * LANE, tn=D_MODEL, comms=[_plan_chip_exchange([s_out])])
    g_in = g_in.reshape(N_CHIPS, 2, in_rows, D_MODEL)
    half_out = _sum_chips(s_out, x_out, chip_idx, "sum_chips_out")
    dh1, ((q_in,), (their_up, their_out)) = _mm(
        dproj, w_in_full, name="in_bwd", out_dtype=F32,
        comms=[_plan_pair([g_in], "chip_major"), _plan_pair([half_up, half_out], False)])
    s_in = _pair_sum(g_in, q_in, core, "pair_sum_in", chip_major=True)
    in_sems, s_in, in_landing, started = _exchange_start(s_in, "exchange_in_start")
    grad_x, dsc1, dsh1, d_pre_mix = _norm1_bwd(unflat(dh1), dx1, x, pre_w_mix + started[0:1, 0:1], sc1)

    dmod = jnp.concatenate([dsh1, dsc1, dg1, dsh2, dsc2, dg2], axis=-1).reshape(bsz, N_MOD * D_MODEL)
    pack = _pack_partials(dmod, [d_pre_mix, d_post_mix, d_pre_mlp, d_post_mlp, d_attn_out, d_hg_norm, d_sinks], d_lb, loss_row)
    ((packs,),) = _comm_only([_plan_allgather8([pack])], "gather_small")
    w_small = dict(b_ada=b_ada, pre_w_mix=pre_w_mix, post_w_mix=post_w_mix, pre_w_mlp=pre_w_mlp, post_w_mlp=post_w_mlp,
                   attn_out_w=attn_out_w, hg_norm_w=hg_norm_w, attn_sinks=attn_sinks, lb_table=lb_table)
    m_small = dict(b_ada=m_b_ada, pre_w_mix=m_pre_w_mix, post_w_mix=m_post_w_mix, pre_w_mlp=m_pre_w_mlp, post_w_mlp=m_post_w_mlp,
                   attn_out_w=m_attn_out_w, hg_norm_w=m_hg_norm_w, attn_sinks=m_attn_sinks, lb_table=m_lb_table)
    v_small = dict(b_ada=v_b_ada, pre_w_mix=v_pre_w_mix, post_w_mix=v_post_w_mix, pre_w_mlp=v_pre_w_mlp, post_w_mlp=v_post_w_mlp,
                   attn_out_w=v_attn_out_w, hg_norm_w=v_hg_norm_w, attn_sinks=v_attn_sinks, lb_table=v_lb_table)
    *small_packed, loss_rows = _small_update(packs, _pack_small(w_small), _pack_small(m_small), _pack_small(v_small), bsz)
    small_out = [_unpack_small(p) for p in small_packed]
    loss = loss_rows[0, 0]

    dmod_all = packs[:, :bsz * MOD_ROWS, :].reshape(N_DEV * bsz, N_MOD * D_MODEL)
    dmod_cols = lax.dynamic_slice_in_dim(dmod_all, chip * ada_cols, ada_cols, axis=1)
    ada_out = _ada_bwd_adamw(c_all, dmod_cols, w_ada[0], m_w_ada[0], v_w_ada[0])

    s_in, x_in = _exchange_wait(in_sems, s_in, in_landing, [grad_x, ada_out[0]], "exchange_in_wait")
    half_in = _sum_chips(s_in, x_in, chip_idx, "sum_chips_in")
    ((their_in,),) = _comm_only([_plan_pair([half_in], False)], "pair_swap_in")
    big = dict(
        w_in=tuple(jnp.transpose(a) for a in _adamw_halves(half_in, their_in, core, w_in_t[0], m_in_t[0], v_in_t[0], axis=0,
                                                           name="adamw_in")),
        w_up=tuple(_adamw_halves(half_up, their_up, core, w_up[0], m_w_up[0], v_w_up[0], axis=0, name="adamw_up")),
        w_out=tuple(_adamw_halves(half_out, their_out, core, w_out[0], m_w_out[0], v_w_out[0], axis=1, name="adamw_out")),
        w_down=tuple(_adamw_halves(half_down, their_down, core, w_down[0], m_w_down[0], v_w_down[0], axis=0, name="adamw_down")),
        w_ada=tuple(ada_out),
    )
    order = ("w_ada", "b_ada", "pre_w_mix", "w_in", "attn_sinks", "attn_out_w", "lb_table", "hg_norm_w", "w_out", "post_w_mix",
             "pre_w_mlp", "w_up", "w_down", "post_w_mlp")
    outs = [loss, grad_x]
    for kind in range(4):
        for nm in order:
            outs.append(big[nm][kind][None] if nm in big else small_out[kind][nm])
    return tuple(outs)
```

```python
import jax
import jax.numpy as jnp
from jax import lax
from jax.experimental import pallas as pl
from jax.experimental.pallas import tpu as pltpu

F32 = jnp.float32
BF16 = jnp.bfloat16

D_MODEL = 1024
ATT_WIDTH = 512
ATT_HEAD_DIM = 64
ATT_Q_HEADS = 8
ATT_KV_HEADS = 2
ATT_GROUP = ATT_Q_HEADS // ATT_KV_HEADS
ATT_KV_COLS = ATT_KV_HEADS * ATT_HEAD_DIM
WINDOW = 128
ROPE_DIM = 16
ROPE_THETA = 500000.0
HG_WIDTH = 512
MIX_WIDTH = ATT_WIDTH + HG_WIDTH
HG_HEAD_DIM = 128
HG_HEADS = 4
HG_CHUNK = 32
IN_COLS = ATT_WIDTH + 2 * ATT_KV_COLS + 4 * HG_WIDTH
ATT_COLS = ATT_WIDTH + 2 * ATT_KV_COLS
D_FF = 4 * D_MODEL
N_MOD = 6
EPS = 1e-6
ATT_SCALE = ATT_HEAD_DIM ** -0.5

ADAM_LR = 0.001
ADAM_B1 = 0.9
ADAM_B2 = 0.999
ADAM_EPS = 1e-08
ADAM_WD = 0.01
ADAM_STEP = 10

N_CHIPS = 4
N_DEV = 8
LANE = 128
VMEM_LIMIT = 48 * 1024 * 1024
VMEM_LIMIT_BIG = 58 * 1024 * 1024
MESH = pl.DeviceIdType.MESH

NT_DIMS = (((1,), (1,)), ((), ()))
TN_DIMS = (((0,), (0,)), ((), ()))


def _sds(shape, dtype):
    return jax.ShapeDtypeStruct(tuple(shape), dtype)


def _params(*sem, vmem_limit=None):
    return pltpu.CompilerParams(dimension_semantics=sem, vmem_limit_bytes=VMEM_LIMIT if vmem_limit is None else vmem_limit)


def _sigmoid(x):
    return 1.0 / (1.0 + jnp.exp(-x))


def _dot(a, b, dims=None):
    a, b = a.astype(BF16), b.astype(BF16)
    if dims is None:
        return jnp.dot(a, b, preferred_element_type=F32)
    return lax.dot_general(a, b, dims, preferred_element_type=F32)


def _rms_fwd(x, w):
    rstd = lax.rsqrt(jnp.mean(x * x, axis=-1, keepdims=True) + EPS)
    xh = x * rstd
    return xh * w, xh, rstd


def _rms_bwd(dy, xh, rstd, w):
    dxh = dy * w
    dx = rstd * (dxh - xh * jnp.mean(dxh * xh, axis=-1, keepdims=True))
    return dx, dy * xh


def _colsum(x):
    return jnp.sum(x, axis=0, keepdims=True)


def _rms_hat(x):
    rstd = lax.rsqrt(jnp.mean(x * x, axis=-1, keepdims=True) + EPS)
    return x * rstd, rstd


def _rms_bwd_gain(dy, gain, xh, rstd):
    dxh = dy * gain
    dx = rstd * (dxh - xh * jnp.mean(dxh * xh, axis=-1, keepdims=True))
    return dx, _colsum(dy * xh)


def _row_tile(rows, cap=256):
    return max(t for t in range(16, cap + 1, 16) if rows % t == 0)


HBM_SPEC = pl.BlockSpec(memory_space=pltpu.HBM)


def _mesh_pos():
    return lax.axis_index("x"), lax.axis_index("y"), lax.axis_index("c")


class _Comm:
    def __init__(self, ins, outs, sems, start, finish, aliases=()):
        self.ins, self.outs, self.sems = list(ins), list(outs), list(sems)
        self.start, self.finish, self.aliases = start, finish, tuple(aliases)


def _call(body, args, *, name, grid, in_specs, out_specs, out_shape, sem, scratch_shapes=(), comms=(), aliases=None,
          vmem_limit=None):
    scratch_shapes = list(scratch_shapes)
    if not comms:
        return pl.pallas_call(body, name=name, grid=grid, in_specs=in_specs, out_specs=out_specs, out_shape=out_shape,
                              input_output_aliases=dict(aliases or {}), scratch_shapes=scratch_shapes,
                              compiler_params=_params(*sem, vmem_limit=vmem_limit))(*args)
    single = not isinstance(out_shape, (list, tuple))
    out_specs_l = [out_specs] if single else list(out_specs)
    out_shape_l = [out_shape] if single else list(out_shape)
    n_in, n_out, n_scr = len(in_specs), len(out_shape_l), len(scratch_shapes)
    n_ci = [len(cm.ins) for cm in comms]
    n_co = [len(cm.outs) for cm in comms]
    n_cs = [len(cm.sems) for cm in comms]
    aliases = dict(aliases or {})
    for k, cm in enumerate(comms):
        for i, o in cm.aliases:
            aliases[n_in + sum(n_ci[:k]) + i] = n_out + sum(n_co[:k]) + o

    def fused(*refs):
        pos = [0]

        def take(n):
            part = refs[pos[0]:pos[0] + n]
            pos[0] += n
            return part

        ins = take(n_in)
        c_ins = [take(n) for n in n_ci]
        outs = take(n_out)
        c_outs = [take(n) for n in n_co]
        scr = take(n_scr)
        c_sems = [take(n) for n in n_cs]
        first, last = True, True
        for d, size in enumerate(grid):
            first = jnp.logical_and(first, pl.program_id(d) == 0)
            last = jnp.logical_and(last, pl.program_id(d) == size - 1)

        def run(which):
            for cm, ci, co, cs in zip(comms, c_ins, c_outs, c_sems):
                getattr(cm, which)(ci, co, cs)

        if grid:
            pl.when(first)(lambda: run("start"))
        else:
            run("start")
        body(*ins, *outs, *scr)
        if grid:
            pl.when(last)(lambda: run("finish"))
        else:
            run("finish")

    res = pl.pallas_call(
        fused, name=name, grid=grid, in_specs=list(in_specs) + [HBM_SPEC] * sum(n_ci),
        out_specs=out_specs_l + [HBM_SPEC] * sum(n_co), out_shape=out_shape_l + [s for cm in comms for s in cm.outs],
        input_output_aliases=aliases, scratch_shapes=scratch_shapes + [s for cm in comms for s in cm.sems],
        compiler_params=_params(*["arbitrary"] * len(grid), vmem_limit=vmem_limit),
    )(*args, *[a for cm in comms for a in cm.ins])
    main = res[:n_out]
    extra, at = [], n_out
    for n in n_co:
        extra.append(list(res[at:at + n]))
        at += n
    return (main[0] if single else list(main)), extra


def _mm(a, b, *, name, out_dtype, trans_b=False, tm=512, tn=None, extra=(), epi=None, b_spec=None, n_out=None, comms=()):
    m_total, k_total = a.shape
    if n_out is None:
        n_out = b.shape[0] if trans_b else b.shape[1]
    tn = n_out if tn is None else tn
    grid = (m_total // tm, n_out // tn)
    dims = NT_DIMS if trans_b else None

    def body(*refs):
        a_ref, b_ref = refs[0], refs[1]
        extra_refs = refs[2:2 + len(extra)]
        o_ref = refs[2 + len(extra)]
        acc = _dot(a_ref[...], b_ref[...], dims)
        if epi is not None:
            acc = epi(acc, *[r[...] for r in extra_refs])
        o_ref[...] = acc.astype(out_dtype)

    if b_spec is None:
        if trans_b:
            b_spec = pl.BlockSpec((tn, k_total), lambda i, j: (j, 0))
        else:
            b_spec = pl.BlockSpec((k_total, tn), lambda i, j: (0, j))
    in_specs = [pl.BlockSpec((tm, k_total), lambda i, j: (i, 0)), b_spec]
    in_specs += [pl.BlockSpec((tm, tn), lambda i, j: (i, j)) for _ in extra]
    return _call(
        body, (a, b, *extra), name=name, grid=grid, in_specs=in_specs,
        out_specs=pl.BlockSpec((tm, tn), lambda i, j: (i, j)),
        out_shape=_sds((m_total, n_out), out_dtype),
        sem=("parallel", "parallel"), comms=comms)


def _mm_tn(a, b, *, name, tk, tn, a_fn=None, out_shape=None, out_spec=None, comms=()):
    m_total, k_total = a.shape
    n_total = b.shape[1]
    grid = (k_total // tk, n_total // tn)

    def body(a_ref, b_ref, o_ref):
        av = a_ref[...]
        part = _dot(av if a_fn is None else a_fn(av), b_ref[...], TN_DIMS)
        o_ref[...] = part.reshape(o_ref.shape)

    if out_shape is None:
        out_shape = _sds((k_total, n_total), F32)
        out_spec = pl.BlockSpec((tk, tn), lambda i, j: (i, j))
    return _call(
        body, (a, b), name=name, grid=grid,
        in_specs=[pl.BlockSpec((m_total, tk), lambda i, j: (0, i)), pl.BlockSpec((m_total, tn), lambda i, j: (0, j))],
        out_specs=out_spec, out_shape=out_shape, sem=("parallel", "parallel"), comms=comms)


def _ada_fwd(c_all, w_shard, b_shard):
    nb, ncol = c_all.shape[0], w_shard.shape[1]
    tn = 512

    def body(c_ref, w_ref, b_ref, o_ref):
        c = c_ref[...]
        o_ref[...] = _dot(c * _sigmoid(c), w_ref[...]) + b_ref[...]

    return pl.pallas_call(
        body, name="ada_fwd", grid=(ncol // tn,),
        in_specs=[pl.BlockSpec((nb, D_MODEL), lambda j: (0, 0)), pl.BlockSpec((D_MODEL, tn), lambda j: (0, j)),
                  pl.BlockSpec((1, tn), lambda j: (0, j))],
        out_specs=pl.BlockSpec((nb, tn), lambda j: (0, j)), out_shape=_sds((nb, ncol), F32),
        compiler_params=_params("parallel"),
    )(c_all, w_shard, b_shard)


def _adamw_math(g, w, m, v):
    m = ADAM_B1 * m + (1.0 - ADAM_B1) * g
    v = ADAM_B2 * v + (1.0 - ADAM_B2) * (g * g)
    m_hat = m / (1.0 - ADAM_B1 ** ADAM_STEP)
    v_hat = v / (1.0 - ADAM_B2 ** ADAM_STEP)
    delta = -ADAM_LR * (m_hat / (jnp.sqrt(v_hat) + ADAM_EPS) + ADAM_WD * w)
    return delta, m, v


def _ada_bwd_adamw(c_all, dmod_cols, w, m, v):
    nb, ncol = dmod_cols.shape
    tn = 256

    def body(c_ref, d_ref, w_ref, m_ref, v_ref, g_ref, dl_ref, nm_ref, nv_ref):
        c = c_ref[...]
        g = _dot(c * _sigmoid(c), d_ref[...], TN_DIMS)
        g_ref[...] = g
        dl_ref[...], nm_ref[...], nv_ref[...] = _adamw_math(g, w_ref[...], m_ref[...], v_ref[...])

    col = pl.BlockSpec((D_MODEL, tn), lambda j: (0, j))
    shp = _sds((D_MODEL, ncol), F32)
    return pl.pallas_call(
        body, name="ada_bwd_adamw", grid=(ncol // tn,),
        in_specs=[pl.BlockSpec((nb, D_MODEL), lambda j: (0, 0)), pl.BlockSpec((nb, tn), lambda j: (0, j)), col, col, col],
        out_specs=[col, col, col, col], out_shape=[shp, shp, shp, shp],
        compiler_params=_params("parallel"),
    )(c_all, dmod_cols, w, m, v)


def _adamw_halves(own, theirs, core, w, m, v, *, axis, name):
    r2, c2 = own.shape
    tr = _row_tile(r2)
    nt = r2 // tr

    def body(core_ref, own_ref, their_ref, w_ref, m_ref, v_ref, g_ref, dl_ref, nm_ref, nv_ref):
        g = jnp.where(pl.program_id(0) == core_ref[0], own_ref[...], their_ref[...])
        g_ref[...] = g
        dl_ref[...], nm_ref[...], nv_ref[...] = _adamw_math(g, w_ref[...], m_ref[...], v_ref[...])

    if axis == 0:
        full = pl.BlockSpec((tr, c2), lambda h, i, core_ref: (h * nt + i, 0))
    else:
        full = pl.BlockSpec((tr, c2), lambda h, i, core_ref: (i, h))
    half = pl.BlockSpec((tr, c2), lambda h, i, core_ref: (i, 0))
    shp = _sds(w.shape, F32)
    return pl.pallas_call(
        body, name=name,
        grid_spec=pltpu.PrefetchScalarGridSpec(num_scalar_prefetch=1, grid=(2, nt), in_specs=[half, half, full, full, full],
                                               out_specs=[full] * 4),
        out_shape=[shp] * 4, compiler_params=_params("parallel", "parallel"),
    )(core, own, theirs, w, m, v)


def _tok_spec(tm, width=D_MODEL):
    return pl.BlockSpec((None, tm, width), lambda b, i: (b, i, 0))


def _row_spec(width=D_MODEL):
    return pl.BlockSpec((None, 1, width), lambda b, i: (b, 0, 0))


def _vec_spec(width=D_MODEL):
    return pl.BlockSpec((1, width), lambda b, i: (0, 0))


class _RowsOf:
    def __init__(self, ref, first, count):
        self.ref, self.rows = ref, slice(first, first + count)

    def __getitem__(self, idx):
        return self.ref[self.rows, :]

    def __setitem__(self, idx, value):
        self.ref[self.rows, :] = value


def _mm_rows(a, b, *, name, tm, extra, extra_specs, out_specs, out_shape, epi, pro=None, trans_b=False, b_chunks=1, comms=(),
             parts=1, zero_per_seq=(), zero_once=(), vmem_limit=None):
    bsz, seq, k_total = a.shape
    kc = k_total // b_chunks
    dims = NT_DIMS if trans_b else None
    rows = tm // parts

    def body(*refs):
        a_ref, b_ref = refs[0], refs[1]
        ex, outs = refs[2:2 + len(extra)], refs[2 + len(extra):]
        if zero_per_seq:
            @pl.when(pl.program_id(1) == 0)
            def _():
                for k in zero_per_seq:
                    outs[k][...] = jnp.zeros_like(outs[k])
        if zero_once:
            @pl.when(jnp.logical_and(pl.program_id(0) == 0, pl.program_id(1) == 0))
            def _():
                for k in zero_once:
                    outs[k][...] = jnp.zeros_like(outs[k])

        def part_of(ref, p):
            tiled = len(ref.shape) == 2 and ref.shape[0] == tm
            return _RowsOf(ref, p * rows, rows) if tiled and parts > 1 else ref

        accs = []
        for p in range(parts):
            a_p, ex_p, outs_p = part_of(a_ref, p), [part_of(r, p) for r in ex], [part_of(r, p) for r in outs]
            if b_chunks == 1:
                accs.append(_dot(a_p[...] if pro is None else pro(a_p, ex_p, outs_p), b_ref[...], dims))
            else:
                acc = _dot(a_p[...][:, 0:kc], b_ref[0], NT_DIMS)
                for k in range(1, b_chunks):
                    acc = acc + _dot(a_p[...][:, k * kc:(k + 1) * kc], b_ref[k], NT_DIMS)
                accs.append(acc)
        for p in range(parts):
            epi(accs[p], [part_of(r, p) for r in ex], [part_of(r, p) for r in outs])

    b_spec = pl.BlockSpec(b.shape, lambda bb, i: (0,) * b.ndim)
    return _call(
        body, (a, b, *extra), name=name, grid=(bsz, seq // tm), in_specs=[_tok_spec(tm, k_total), b_spec, *extra_specs],
        out_specs=out_specs, out_shape=out_shape, sem=("arbitrary", "arbitrary"), comms=comms, vmem_limit=vmem_limit)


def _in_proj_fused(x, w, sc, sh, w_in_t, tables, comms=()):
    tm = 512
    bsz, seq, _ = x.shape
    half = ROPE_DIM // 2
    heads_per_slab = LANE // ATT_HEAD_DIM

    def pro(x_ref, ex, outs):
        y, _, _ = _rms_fwd(x_ref[...], ex[0][...])
        h = (y * (1.0 + ex[1][...]) + ex[2][...]).astype(BF16)
        outs[0][...] = h
        return h

    def epi(acc, ex, outs):
        c, u, d = ex[3][...], ex[4][...], ex[5][...]
        _, rec_ref, q_ref, k_ref, v_ref = outs
        for k in range(HG_SLABS):
            rec_ref[k] = acc[:, ATT_COLS + k * HG_WIDTH:ATT_COLS + (k + 1) * HG_WIDTH]

        def rope(z):
            return (z * c + pltpu.roll(z, half, 1) * u + pltpu.roll(z, LANE - half, 1) * d).astype(BF16)

        for s in range(ATT_WIDTH // LANE):
            slab = rope(acc[:, s * LANE:(s + 1) * LANE])
            for part in range(heads_per_slab):
                g, hh = divmod(s * heads_per_slab + part, ATT_GROUP)
                piece = slab[:, part * ATT_HEAD_DIM:(part + 1) * ATT_HEAD_DIM]
                for blk in range(tm // WINDOW):
                    q_ref[blk, g, hh * WINDOW:(hh + 1) * WINDOW, :] = piece[blk * WINDOW:(blk + 1) * WINDOW]
        rk = rope(acc[:, ATT_WIDTH:ATT_WIDTH + LANE])
        vv = acc[:, ATT_WIDTH + LANE:ATT_COLS].astype(BF16)
        for g in range(ATT_KV_HEADS):
            k_ref[g] = rk[:, g * ATT_HEAD_DIM:(g + 1) * ATT_HEAD_DIM]
            v_ref[g] = vv[:, g * ATT_HEAD_DIM:(g + 1) * ATT_HEAD_DIM]

    tab = pl.BlockSpec((tm, LANE), lambda b, i: (i, 0))
    kv_spec = pl.BlockSpec((None, ATT_KV_HEADS, tm, ATT_HEAD_DIM), lambda b, i: (b, 0, i, 0))
    kv_shape = _sds((bsz, ATT_KV_HEADS, seq, ATT_HEAD_DIM), BF16)
    q_spec = pl.BlockSpec((None, tm // WINDOW, ATT_KV_HEADS, GROUP_ROWS, ATT_HEAD_DIM), lambda b, i: (b, i, 0, 0, 0))
    return _mm_rows(x, w_in_t, name="in_proj", tm=tm, extra=(w, sc, sh, *tables),
                    extra_specs=[_vec_spec(), _row_spec(), _row_spec(), tab, tab, tab],
                    out_specs=[_tok_spec(tm), pl.BlockSpec((None, HG_SLABS, tm, HG_WIDTH), lambda b, i: (b, 0, i, 0)), q_spec,
                               kv_spec, kv_spec],
                    out_shape=[_sds(x.shape, BF16), _sds((bsz, HG_SLABS, seq, HG_WIDTH), F32),
                               _sds((bsz, seq // WINDOW, ATT_KV_HEADS, GROUP_ROWS, ATT_HEAD_DIM), BF16), kv_shape, kv_shape],
                    pro=pro, epi=epi, trans_b=True, comms=comms)


def _rope_tables(seq):
    half = ROPE_DIM // 2
    inv_freq = ROPE_THETA ** (-jnp.arange(0, ROPE_DIM, 2, dtype=F32) / ROPE_DIM)
    ang = jnp.arange(seq, dtype=F32)[:, None] * inv_freq[None, :]
    cos, sin = jnp.cos(ang), jnp.sin(ang)
    rest = ATT_HEAD_DIM - ROPE_DIM
    ones, zeros, zh = jnp.ones((seq, rest), F32), jnp.zeros((seq, rest), F32), jnp.zeros((seq, half), F32)
    reps = LANE // ATT_HEAD_DIM
    t_cos = jnp.tile(jnp.concatenate([cos, cos, ones], axis=1), (1, reps))
    t_up = jnp.tile(jnp.concatenate([zh, sin, zeros], axis=1), (1, reps))
    t_dn = jnp.tile(jnp.concatenate([-sin, zh, zeros], axis=1), (1, reps))
    return t_cos, t_up, t_dn


GROUP_ROWS = ATT_GROUP * WINDOW


ATT_BPS = 2


MASKED = -1e30


def _band_biases():
    row = jnp.arange(GROUP_ROWS)[:, None] % WINDOW
    col = jnp.arange(2 * WINDOW)[None, :]
    own = jnp.logical_and(col >= WINDOW, col - WINDOW <= row)
    before = jnp.logical_and(col < WINDOW, col > row)
    return (jnp.where(jnp.logical_or(own, before), 0.0, MASKED).astype(F32), jnp.where(own, 0.0, MASKED).astype(F32))


def _band_bias(full_ref, first_ref, has_prev):
    return full_ref[...] if has_prev is True else jnp.where(has_prev, full_ref[...], first_ref[...])


def _sink_column(sink_ref, g):
    head = lax.broadcasted_iota(jnp.int32, (GROUP_ROWS, 1), 0) // WINDOW
    col = jnp.full((GROUP_ROWS, 1), sink_ref[0, g * ATT_GROUP], F32)
    for hh in range(1, ATT_GROUP):
        col = jnp.where(head == hh, sink_ref[0, g * ATT_GROUP + hh], col)
    return col


def _sink_row(sink_ref, g):
    return jnp.concatenate([jnp.full((1, WINDOW), sink_ref[0, g * ATT_GROUP + hh], F32) for hh in range(ATT_GROUP)], axis=1)


def _bias_spec(transposed=False):
    shape = (2 * WINDOW, GROUP_ROWS) if transposed else (GROUP_ROWS, 2 * WINDOW)
    return pl.BlockSpec(shape, lambda b, i: (0, 0))


def _attn_specs():
    q_spec = pl.BlockSpec((None, ATT_BPS, ATT_KV_HEADS, GROUP_ROWS, ATT_HEAD_DIM), lambda b, i: (b, i, 0, 0, 0))
    kv_cur = pl.BlockSpec((None, ATT_KV_HEADS, ATT_BPS * WINDOW, ATT_HEAD_DIM), lambda b, i: (b, 0, i, 0))
    kv_prev = pl.BlockSpec((None, ATT_KV_HEADS, WINDOW, ATT_HEAD_DIM), lambda b, i: (b, 0, jnp.maximum(ATT_BPS * i - 1, 0), 0))
    return q_spec, kv_cur, kv_prev


def _band(prev_ref, cur_ref, g, blk):
    own = cur_ref[g, blk * WINDOW:(blk + 1) * WINDOW]
    before = prev_ref[g] if blk == 0 else cur_ref[g, (blk - 1) * WINDOW:blk * WINDOW]
    return jnp.concatenate([before, own], axis=0)


def _attn_fwd(qh, kh, vh, sinks, w_norm, biases, comms=()):
    bsz, nblk = qh.shape[0], qh.shape[1]
    seq = nblk * WINDOW
    rows = ATT_BPS * WINDOW

    def body(sink_ref, q_ref, kc_ref, kp_ref, vc_ref, vp_ref, w_ref, full_ref, first_ref, raw_ref, an_ref, l_ref):
        l_ref[...] = jnp.zeros_like(l_ref)
        def block(blk):
            bias = _band_bias(full_ref, first_ref, True if blk else pl.program_id(1) > 0)
            groups = range(ATT_KV_HEADS)
            keys, vals = [_band(kp_ref, kc_ref, g, blk) for g in groups], [_band(vp_ref, vc_ref, g, blk) for g in groups]
            sink = [_sink_column(sink_ref, g) for g in groups]
            s = [_dot(q_ref[blk, g], keys[g], NT_DIMS) * ATT_SCALE + bias for g in groups]
            yield
            m = [jnp.maximum(jnp.max(s[g], axis=-1, keepdims=True), sink[g]) for g in groups]
            p = [jnp.exp(s[g] - m[g]) for g in groups]
            den = [jnp.sum(p[g], axis=-1, keepdims=True) + jnp.exp(sink[g] - m[g]) for g in groups]
            yield
            o = [_dot(p[g] / den[g], vals[g]) for g in groups]
            yield
            lse = [m[g] + jnp.log(den[g]) for g in groups]
            tok = slice(blk * WINDOW, (blk + 1) * WINDOW)
            for g in groups:
                for hh in range(ATT_GROUP):
                    h = g * ATT_GROUP + hh
                    raw_ref[tok, h * ATT_HEAD_DIM:(h + 1) * ATT_HEAD_DIM] = o[g][hh * WINDOW:(hh + 1) * WINDOW]
                    l_ref[tok, h:h + 1] = lse[g][hh * WINDOW:(hh + 1) * WINDOW]

        _in_step(block(blk) for blk in range(ATT_BPS))
        y, _, _ = _rms_fwd(raw_ref[...], w_ref[...])
        an_ref[...] = y.astype(BF16)

    cur = lambda width: pl.BlockSpec((None, rows, width), lambda b, i: (b, i, 0))
    q_spec, kv_cur, kv_prev = _attn_specs()
    return _call(
        body, (sinks, qh, kh, kh, vh, vh, w_norm, *biases), name="attn_fwd", grid=(bsz, nblk // ATT_BPS),
        in_specs=[pl.BlockSpec(memory_space=pltpu.SMEM), q_spec, kv_cur, kv_prev, kv_cur, kv_prev, _vec_spec(ATT_WIDTH),
                  _bias_spec(), _bias_spec()],
        out_specs=[cur(ATT_WIDTH), cur(ATT_WIDTH), cur(LANE)],
        out_shape=[_sds((bsz, seq, ATT_WIDTH), F32), _sds((bsz, seq, MIX_WIDTH), BF16), _sds((bsz, seq, LANE), F32)],
        sem=("parallel", "parallel"), comms=comms)


HG_Q0 = ATT_COLS // LANE
HG_F0 = HG_Q0 + HG_HEADS
HG_I0 = HG_F0 + HG_HEADS
HG_G0 = HG_I0 + HG_HEADS
HG_SLABS = 4
HG_Q, HG_F, HG_I, HG_G = range(HG_SLABS)
HG_TOK = 256
HG_NCH = HG_TOK // HG_CHUNK
HG_HPS = 2
HG_FWD_BLOCKS = 2


def _block_masks():
    row = jnp.arange(HG_TOK)[:, None]
    col = jnp.arange(HG_TOK)[None, :]
    same = (row // HG_CHUNK) == (col // HG_CHUNK)
    return jnp.logical_and(same, col <= row).astype(F32), jnp.logical_and(same, col >= row).astype(F32)


def _row_in_chunk():
    return lax.broadcasted_iota(jnp.int32, (HG_TOK, LANE), 0) % HG_CHUNK


def _chunk_cumsum(x, reverse=False):
    ric = _row_in_chunk()
    shift = 1
    while shift < HG_CHUNK:
        if reverse:
            x = x + jnp.where(ric < HG_CHUNK - shift, pltpu.roll(x, HG_TOK - shift, 0), 0.0)
        else:
            x = x + jnp.where(ric >= shift, pltpu.roll(x, shift, 0), 0.0)
        shift *= 2
    return x


def _chunk_rows(rows):
    stacked = jnp.concatenate([r[None] for r in rows], axis=0)
    return jnp.broadcast_to(stacked, (HG_NCH, HG_CHUNK, LANE)).reshape(HG_TOK, LANE)


def _chunk_slices(x):
    return [x[j * HG_CHUNK:(j + 1) * HG_CHUNK] for j in range(HG_NCH)]


def _in_step(stages):
    stages = list(stages)
    while stages:
        stages = [g for g in stages if next(g, stages) is not stages]


def _hgrn_common(tbl, hf, hq):
    lb = _sigmoid(tbl[1:2] - tbl[0:1])
    sig = _sigmoid(hf)
    f = lb + (1.0 - lb) * sig
    sq = _sigmoid(hq)
    q, k = hq * sq, 1.0 - f
    b = _chunk_cumsum(jnp.log(f))
    last = [b[(j + 1) * HG_CHUNK - 1:(j + 1) * HG_CHUNK] for j in range(HG_NCH)]
    bl = _chunk_rows(last)
    e_b, e_nb, e_rem = jnp.exp(b), jnp.exp(-b), jnp.exp(bl - b)
    e_last = [jnp.exp(r) for r in last]
    return dict(lb=lb, sig=sig, f=f, sq=sq, q=q, k=k, e_b=e_b, e_nb=e_nb, e_rem=e_rem, e_last=e_last,
                qd=q * e_b, kd=k * e_nb, ku=k * e_rem)


def _hgrn_fwd(proj, lb_table, norm_w, mix_in, masks, comms=()):
    bsz, _, seq, _ = proj.shape
    nstep = seq // HG_TOK

    def body(tbl_ref, nw_ref, p_ref, mix_ref, lower_ref, o_ref, rec_ref, st_ref, s_scr):
        @pl.when(pl.program_id(2) == 0)
        def _():
            s_scr[...] = jnp.zeros_like(s_scr)

        lower = lower_ref[...]

        def head(hp, sub):
            ls = slice(hp * LANE, (hp + 1) * LANE)
            rows = slice(sub * HG_TOK, (sub + 1) * HG_TOK)
            v, hg = p_ref[HG_I, rows, ls], p_ref[HG_G, rows, ls]
            t = _hgrn_common(tbl_ref[:, ls], p_ref[HG_F, rows, ls], p_ref[HG_Q, rows, ls])
            yield
            a = _dot(t["qd"], t["kd"], NT_DIMS) * lower
            o_intra = _dot(a, v)
            yield
            v_c, ku_c, qd_c = [_chunk_slices(z.astype(BF16)) for z in (v, t["ku"], t["qd"])]
            updates = [_dot(v_c[j], ku_c[j], TN_DIMS) for j in range(HG_NCH)]
            yield
            st = s_scr[hp]
            states = []
            for j in range(HG_NCH):
                states.append(st)
                st = st * t["e_last"][j] + updates[j]
            s_scr[hp] = st
            yield
            o = o_intra + jnp.concatenate([_dot(qd_c[j], states[j], NT_DIMS) for j in range(HG_NCH)], axis=0)
            yield
            st_ref[hp, sub] = states[0]
            o_ref[rows, ls] = o
            y, _, _ = _rms_fwd(o, nw_ref[...])
            rec_ref[rows, ls] = (y * (hg * _sigmoid(hg))).astype(BF16)

        for sub in range(HG_FWD_BLOCKS):
            _in_step(head(hp, sub) for hp in range(HG_HPS))

    width = HG_HPS * LANE
    tok = HG_FWD_BLOCKS * HG_TOK
    head_out = pl.BlockSpec((None, tok, width), lambda b, h, t: (b, t, h))
    mix_out = pl.BlockSpec((None, tok, width), lambda b, h, t: (b, t, ATT_WIDTH // width + h))
    return _call(
        body, (lb_table, norm_w, proj, mix_in, masks[0]), name="hgrn_fwd",
        grid=(bsz, HG_HEADS // HG_HPS, nstep // HG_FWD_BLOCKS),
        in_specs=[pl.BlockSpec((2, width), lambda b, h, t: (0, h)), pl.BlockSpec((1, LANE), lambda b, h, t: (0, 0)),
                  pl.BlockSpec((None, HG_SLABS, tok, width), lambda b, h, t: (b, 0, t, h)), pl.BlockSpec(memory_space=pl.ANY),
                  pl.BlockSpec((HG_TOK, HG_TOK), lambda b, h, t: (0, 0))],
        out_specs=[head_out, mix_out,
                   pl.BlockSpec((None, HG_HPS, HG_FWD_BLOCKS, LANE, LANE), lambda b, h, t: (b, h, t, 0, 0))],
        out_shape=[_sds((bsz, seq, HG_WIDTH), F32), _sds(mix_in.shape, BF16),
                   _sds((bsz, HG_HEADS, nstep, LANE, LANE), F32)],
        scratch_shapes=[pltpu.VMEM((HG_HPS, LANE, LANE), F32)],
        sem=("parallel", "parallel", "arbitrary"), comms=comms, aliases={3: 1})


def _out_proj_fused(cat, w_out, x, post_w, g1, pre_w, sc2, sh2):
    tm = 512

    def epi(mix, ex, outs):
        x_ref, pw_ref, g1_ref, w2_ref, sc_ref, sh_ref = ex
        outs[0][...] = mix
        n1, _, _ = _rms_fwd(mix, pw_ref[...])
        x1 = x_ref[...] + g1_ref[...] * n1
        outs[1][...] = x1
        y2, _, _ = _rms_fwd(x1, w2_ref[...])
        outs[2][...] = (y2 * (1.0 + sc_ref[...]) + sh_ref[...]).astype(BF16)

    return _mm_rows(cat, w_out, name="out_proj", tm=tm, extra=(x, post_w, g1, pre_w, sc2, sh2),
                    extra_specs=[_tok_spec(tm), _vec_spec(), _row_spec(), _vec_spec(), _row_spec(), _row_spec()],
                    out_specs=[_tok_spec(tm), _tok_spec(tm), _tok_spec(tm)],
                    out_shape=[_sds(x.shape, F32), _sds(x.shape, F32), _sds(x.shape, BF16)], epi=epi)


def _acc_out(ref, first, value):
    @pl.when(first)
    def _():
        ref[...] = value

    @pl.when(jnp.logical_not(first))
    def _():
        ref[...] += value


def _down_proj_fused(r, w_down, x1, post_w, g2, target):
    tm = 512
    bsz = x1.shape[0]

    def pro(r_ref, ex, outs):
        rv = r_ref[...]
        return rv * rv

    def epi(down, ex, outs):
        x1_ref, w_ref, g2_ref, t_ref = ex
        loss_ref, dy_ref, dd_ref, dg2_ref, dw_ref = outs
        w, g2v = w_ref[...], g2_ref[...]
        gain = g2v * w
        dh, rstd = _rms_hat(down)
        err = x1_ref[...] + dh * gain - t_ref[...]
        part = (0.5 / D_MODEL) * jnp.sum(jnp.sum(err * err, axis=-1, keepdims=True), axis=0, keepdims=True)
        loss_ref[...] += jnp.broadcast_to(part, (1, LANE))
        dy = err * (1.0 / D_MODEL)
        dy_ref[...] = dy
        dd, per_col = _rms_bwd_gain(dy, gain, dh, rstd)
        dd_ref[...] = dd.astype(BF16)
        dg2_ref[...] += per_col * w
        dw_ref[...] += per_col * g2v

    return _mm_rows(r, w_down, name="down_proj", tm=tm, extra=(x1, post_w, g2, target),
                    extra_specs=[_tok_spec(tm), _vec_spec(), _row_spec(), _tok_spec(tm)],
                    out_specs=[_vec_spec(LANE), _tok_spec(tm), _tok_spec(tm), _row_spec(), _vec_spec()],
                    out_shape=[_sds((1, LANE), F32), _sds(x1.shape, F32), _sds(x1.shape, BF16), _sds((bsz, 1, D_MODEL), F32),
                               _sds((1, D_MODEL), F32)], pro=pro, epi=epi, parts=2, zero_per_seq=(3,), zero_once=(0, 4),
                    vmem_limit=VMEM_LIMIT_BIG)


def _up_bwd_fused(dpre, w_up4, dy, x1, mix, pre_w, sc2, post_w, g1, comms=()):
    tm = 512
    bsz = x1.shape[0]

    def epi(dh2v, ex, outs):
        dy_ref, x1_ref, mix_ref, w2_ref, sc_ref, pw_ref, g1_ref = ex
        dx1_ref, dmix_ref, dsc_ref, dsh_ref, dg1_ref, dw2_ref, dpw_ref = outs
        w2, pw, g1v = w2_ref[...], pw_ref[...], g1_ref[...]
        mod2 = 1.0 + sc_ref[...]
        xh2, rstd2 = _rms_hat(x1_ref[...])
        dsh_ref[...] += _colsum(dh2v)
        dx1n, per_col2 = _rms_bwd_gain(dh2v, mod2 * w2, xh2, rstd2)
        dsc_ref[...] += per_col2 * w2
        dw2_ref[...] += per_col2 * mod2
        dx1 = dy_ref[...] + dx1n
        dx1_ref[...] = dx1
        mh, rstd1 = _rms_hat(mix_ref[...])
        dmix, per_col1 = _rms_bwd_gain(dx1, g1v * pw, mh, rstd1)
        dmix_ref[...] = dmix.astype(BF16)
        dg1_ref[...] += per_col1 * pw
        dpw_ref[...] += per_col1 * g1v

    row_shape = _sds((bsz, 1, D_MODEL), F32)
    vec_shape = _sds((1, D_MODEL), F32)
    return _mm_rows(dpre, w_up4, name="up_bwd", tm=tm, extra=(dy, x1, mix, pre_w, sc2, post_w, g1),
                    extra_specs=[_tok_spec(tm), _tok_spec(tm), _tok_spec(tm), _vec_spec(), _row_spec(), _vec_spec(), _row_spec()],
                    out_specs=[_tok_spec(tm), _tok_spec(tm), _row_spec(), _row_spec(), _row_spec(), _vec_spec(), _vec_spec()],
                    out_shape=[_sds(x1.shape, F32), _sds(x1.shape, BF16), row_shape, row_shape, row_shape, vec_shape, vec_shape],
                    epi=epi, b_chunks=w_up4.shape[0], comms=comms, parts=2, zero_per_seq=(2, 3, 4), zero_once=(5, 6),
                    vmem_limit=VMEM_LIMIT_BIG)


def _norm1_bwd(dh1, dx1, x, pre_w, sc1, tm=512, comms=()):
    bsz, seq, _ = x.shape

    def body(dh_ref, dx1_ref, x_ref, w_ref, sc_ref, gx_ref, dsc_ref, dsh_ref, dw_ref):
        b, i = pl.program_id(0), pl.program_id(1)
        w = w_ref[...]
        dh = dh_ref[...]
        mod = 1.0 + sc_ref[...]
        xh, rstd = _rms_hat(x_ref[...])
        dx, per_col = _rms_bwd_gain(dh, mod * w, xh, rstd)
        _acc_out(dsh_ref, i == 0, _colsum(dh))
        _acc_out(dsc_ref, i == 0, per_col * w)
        _acc_out(dw_ref, jnp.logical_and(b == 0, i == 0), per_col * mod)
        gx_ref[...] = dx1_ref[...] + dx

    row_shape = _sds((bsz, 1, D_MODEL), F32)
    return _call(
        body, (dh1, dx1, x, pre_w, sc1), name="norm1_bwd", grid=(bsz, seq // tm),
        in_specs=[_tok_spec(tm), _tok_spec(tm), _tok_spec(tm), _vec_spec(), _row_spec()],
        out_specs=[_tok_spec(tm), _row_spec(), _row_spec(), _vec_spec()],
        out_shape=[_sds(x.shape, F32), row_shape, row_shape, _sds((1, D_MODEL), F32)],
        sem=("arbitrary", "arbitrary"), comms=comms)


def _hgrn_bwd(dcat, proj, o_raw, states, lb_table, norm_w, masks, comms=()):
    bsz, _, seq, _ = proj.shape
    nstep = seq // HG_TOK
    rec0 = ATT_WIDTH // LANE
    width = HG_HPS * LANE
    slabs = (HG_Q0, HG_F0, HG_I0, HG_G0)
    n_steps = (HG_HEADS // HG_HPS) * bsz * nstep
    assert n_steps >= 2

    def body(tbl_ref, nw_ref, dr_ref, p_ref, o_ref, st_ref, lower_ref, upper_ref,
             dproj_ref, dlb_ref, dnw_ref, ds_scr, grad_buf, grad_sem):
        h, b, t = pl.program_id(0), pl.program_id(1), pl.program_id(2)
        step = (h * bsz + b) * nstep + t
        slot = step % 2
        dq_k, df_k, di_k, dg_k = range(4)

        def grad_copies(of_step):
            hh, bb, tt = of_step // (bsz * nstep), (of_step // nstep) % bsz, of_step % nstep
            rows = pl.ds(pl.multiple_of((nstep - 1 - tt) * HG_TOK, HG_TOK), HG_TOK)
            return [pltpu.make_async_copy(
                grad_buf.at[of_step % 2, k],
                dproj_ref.at[bb, rows, pl.ds(pl.multiple_of(slabs[k] * LANE + hh * width, width), width)],
                grad_sem.at[of_step % 2, k]) for k in range(4)]

        @pl.when(step >= 2)
        def _():
            for cp in grad_copies(step - 2):
                cp.wait()

        @pl.when(t == 0)
        def _():
            ds_scr[...] = jnp.zeros_like(ds_scr)

        lower, upper = lower_ref[...], upper_ref[...]
        dlb_parts, dnw_parts = [None] * HG_HPS, [None] * HG_HPS

        def head(hp):
            ls = slice(hp * LANE, (hp + 1) * LANE)
            hq, v, hg = p_ref[HG_Q, :, ls], p_ref[HG_I, :, ls], p_ref[HG_G, :, ls]
            nw = nw_ref[...]
            c = _hgrn_common(tbl_ref[:, ls], p_ref[HG_F, :, ls], hq)
            qd, kd, ku = c["qd"], c["kd"], c["ku"]
            yield
            y, on, rstd = _rms_fwd(o_ref[:, ls], nw)
            sg = _sigmoid(hg)
            dr = dr_ref[:, ls]
            grad_buf[slot, dg_k, :, ls] = (dr * y * (sg * (1.0 + hg * (1.0 - sg)))).astype(BF16)
            do, dnw_rows = _rms_bwd(dr * (hg * sg), on, rstd, nw)
            yield
            at = _dot(kd, qd, NT_DIMS) * upper
            da = _dot(do, v, NT_DIMS) * lower
            dat = _dot(v, do, NT_DIMS) * upper
            yield
            dv = _dot(at, do)
            dqd = _dot(da, kd)
            dkd = _dot(dat, qd)
            yield
            do_c, qd_c, v_c, ku_c = [_chunk_slices(z.astype(BF16)) for z in (do, qd, v, ku)]
            outer = [_dot(do_c[j], qd_c[j], TN_DIMS) for j in range(HG_NCH)]
            yield
            ds = ds_scr[hp]
            ds_after = [None] * HG_NCH
            for j in reversed(range(HG_NCH)):
                ds_after[j] = ds
                ds = outer[j] + ds * c["e_last"][j]
            ds_scr[hp] = ds
            yield
            updates = [_dot(v_c[j], ku_c[j], TN_DIMS) for j in range(HG_NCH)]
            yield
            states = [st_ref[hp, 0]]
            for j in range(HG_NCH - 1):
                states.append(states[j] * c["e_last"][j] + updates[j])
            dv = dv + jnp.concatenate([_dot(ku_c[j], ds_after[j], NT_DIMS) for j in range(HG_NCH)], axis=0)
            dqd = dqd + jnp.concatenate([_dot(do_c[j], states[j]) for j in range(HG_NCH)], axis=0)
            dku = jnp.concatenate([_dot(v_c[j], ds_after[j]) for j in range(HG_NCH)], axis=0)
            yield
            dku_ku = dku * ku
            dbl = [_colsum(states[j] * ds_after[j]) * c["e_last"][j] + _colsum(dku_ku[j * HG_CHUNK:(j + 1) * HG_CHUNK])
                   for j in range(HG_NCH)]
            dk = dkd * c["e_nb"] + dku * c["e_rem"]
            db = dqd * qd - dkd * kd - dku_ku + jnp.where(_row_in_chunk() == HG_CHUNK - 1, _chunk_rows(dbl), 0.0)
            dfv = _chunk_cumsum(db, reverse=True) / c["f"] - dk
            sig, sq = c["sig"], c["sq"]
            grad_buf[slot, df_k, :, ls] = (dfv * (1.0 - c["lb"]) * sig * (1.0 - sig)).astype(BF16)
            grad_buf[slot, dq_k, :, ls] = (dqd * c["e_b"] * (sq * (1.0 + hq * (1.0 - sq)))).astype(BF16)
            grad_buf[slot, di_k, :, ls] = dv.astype(BF16)
            dlb_parts[hp] = _colsum(dfv * (1.0 - sig))
            dnw_parts[hp] = _colsum(dnw_rows)

        _in_step(head(hp) for hp in range(HG_HPS))
        _acc_out(dlb_ref, jnp.logical_and(b == 0, t == 0), jnp.concatenate(dlb_parts, axis=1))
        _acc_out(dnw_ref, jnp.logical_and(h == 0, jnp.logical_and(b == 0, t == 0)), sum(dnw_parts[1:], dnw_parts[0]))
        for cp in grad_copies(step):
            cp.start()

        @pl.when(step == n_steps - 1)
        def _():
            for cp in grad_copies(step - 1) + grad_copies(step):
                cp.wait()

    rev = lambda t: nstep - 1 - t
    slab = lambda first: pl.BlockSpec((None, HG_TOK, width), lambda h, b, t: (b, rev(t), first // HG_HPS + h))
    head = pl.BlockSpec((None, HG_TOK, width), lambda h, b, t: (b, rev(t), h))
    return _call(
        body, (lb_table, norm_w, dcat, proj, o_raw, states, *masks), name="hgrn_bwd",
        grid=(HG_HEADS // HG_HPS, bsz, nstep),
        in_specs=[pl.BlockSpec((2, width), lambda h, b, t: (0, h)), pl.BlockSpec((1, LANE), lambda h, b, t: (0, 0)),
                  slab(rec0), pl.BlockSpec((None, HG_SLABS, HG_TOK, width), lambda h, b, t: (b, 0, rev(t), h)), head,
                  pl.BlockSpec((None, HG_HPS, 1, LANE, LANE), lambda h, b, t: (b, h, rev(t), 0, 0)),
                  pl.BlockSpec((HG_TOK, HG_TOK), lambda h, b, t: (0, 0)), pl.BlockSpec((HG_TOK, HG_TOK), lambda h, b, t: (0, 0))],
        out_specs=[pl.BlockSpec(memory_space=pl.ANY), pl.BlockSpec((1, width), lambda h, b, t: (0, h)),
                   pl.BlockSpec((1, LANE), lambda h, b, t: (0, 0))],
        out_shape=[_sds((bsz, seq, IN_COLS), BF16), _sds((1, HG_WIDTH), F32), _sds((1, LANE), F32)],
        scratch_shapes=[pltpu.VMEM((HG_HPS, LANE, LANE), F32), pltpu.VMEM((2, 4, HG_TOK, width), BF16),
                        pltpu.SemaphoreType.DMA((2, 4))],
        sem=("arbitrary", "arbitrary", "arbitrary"), comms=comms)


def _attn_bwd(dcat, raw, w_norm, qh, kh, vh, lse, sinks, tables, biases, dproj, comms=()):
    bsz, nblk = qh.shape[0], qh.shape[1]
    seq = nblk * WINDOW
    nstep = nblk // ATT_BPS
    half = ROPE_DIM // 2

    def body(sink_ref, da_ref, raw_ref, w_ref, q_ref, kc_ref, kp_ref, vc_ref, vp_ref, l_ref, c_ref, u_ref, d_ref,
             full_ref, first_ref, dproj_ref, o_ref, dw_ref, dsink_ref, carry_k, carry_v):
        b, i = pl.program_id(0), pl.program_id(1)
        first = jnp.logical_and(b == 0, i == 0)

        @pl.when(i == 0)
        def _():
            carry_k[...] = jnp.zeros_like(carry_k)
            carry_v[...] = jnp.zeros_like(carry_v)

        w = w_ref[...]
        _, on, rstd = _rms_fwd(raw_ref[...], w)
        do_step, dw_rows = _rms_bwd(da_ref[...], on, rstd, w)
        _acc_out(dw_ref, first, _colsum(dw_rows))
        lane8 = lax.broadcasted_iota(jnp.int32, (1, ATT_Q_HEADS), 1)
        dsink = jnp.zeros((1, ATT_Q_HEADS), F32)
        head_cols = jnp.where(lax.broadcasted_iota(jnp.int32, (2 * ATT_Q_HEADS, ATT_WIDTH), 1) // ATT_HEAD_DIM
                              == lax.broadcasted_iota(jnp.int32, (2 * ATT_Q_HEADS, ATT_WIDTH), 0), 1.0, 0.0)
        from_next_k, from_next_v = carry_k[...], carry_v[...]
        for blk in reversed(range(ATT_BPS)):
            tok = slice(blk * WINDOW, (blk + 1) * WINDOW)
            bias = _band_bias(full_ref, first_ref, True if blk else i < nstep - 1)
            do_all = do_step[tok]
            c, u, d = c_ref[tok, :], u_ref[tok, :], d_ref[tok, :]
            lse_t = l_ref[tok, :].T
            prod = do_all * raw_ref[tok, :]
            prod_hi = prod.astype(BF16)
            prod_lo = prod - prod_hi.astype(F32)
            dsum_t = _dot(head_cols, prod_hi, NT_DIMS) + _dot(head_cols, prod_lo, NT_DIMS)

            def unrope(g):
                return (g * c + pltpu.roll(g * u, LANE - half, 1) + pltpu.roll(g * d, half, 1)).astype(BF16)

            groups = range(ATT_KV_HEADS)
            group_row = lambda z, g: jnp.concatenate(
                [z[g * ATT_GROUP + hh:g * ATT_GROUP + hh + 1, :] for hh in range(ATT_GROUP)], axis=1)
            q = [q_ref[blk, g] for g in groups]
            keys, vals = [_band(kp_ref, kc_ref, g, blk) for g in groups], [_band(vp_ref, vc_ref, g, blk) for g in groups]
            do_g = [jnp.concatenate([do_all[:, (g * ATT_GROUP + hh) * ATT_HEAD_DIM:(g * ATT_GROUP + hh + 1) * ATT_HEAD_DIM]
                                     for hh in range(ATT_GROUP)], axis=0) for g in groups]
            dsum, lse_g = [group_row(dsum_t, g) for g in groups], [group_row(lse_t, g) for g in groups]
            s_t = [_dot(keys[g], q[g], NT_DIMS) for g in groups]
            dp_t = [_dot(vals[g], do_g[g], NT_DIMS) for g in groups]
            p_t = [jnp.exp(s_t[g] * ATT_SCALE + bias - lse_g[g]) for g in groups]
            ds_t = [p_t[g] * (dp_t[g] - dsum[g]) * ATT_SCALE for g in groups]
            dq_g = [_dot(ds_t[g], keys[g], TN_DIMS) for g in groups]
            dk_g = [_dot(ds_t[g], q[g]) for g in groups]
            dv_g = [_dot(p_t[g], do_g[g]) for g in groups]
            for g in groups:
                sink_part = jnp.exp(_sink_row(sink_ref, g) - lse_g[g]) * dsum[g]
                for hh in range(ATT_GROUP):
                    head_sum = jnp.sum(sink_part[:, hh * WINDOW:(hh + 1) * WINDOW], axis=1, keepdims=True)
                    dsink = dsink - jnp.where(lane8 == g * ATT_GROUP + hh, head_sum, 0.0)
            dq_parts = [dq_g[g][hh * WINDOW:(hh + 1) * WINDOW] for g in groups for hh in range(ATT_GROUP)]
            dk_before, dk_own = [z[:WINDOW] for z in dk_g], [z[WINDOW:] for z in dk_g]
            dv_before, dv_own = [z[:WINDOW] for z in dv_g], [z[WINDOW:] for z in dv_g]
            per_slab = LANE // ATT_HEAD_DIM
            for s in range(ATT_WIDTH // LANE):
                slab = jnp.concatenate(dq_parts[s * per_slab:(s + 1) * per_slab], axis=1)
                o_ref[tok, s * LANE:(s + 1) * LANE] = unrope(slab)
            o_ref[tok, ATT_WIDTH:ATT_WIDTH + LANE] = unrope(jnp.concatenate(dk_own, axis=1) + from_next_k)
            o_ref[tok, ATT_WIDTH + LANE:ATT_COLS] = (jnp.concatenate(dv_own, axis=1) + from_next_v).astype(BF16)
            from_next_k, from_next_v = jnp.concatenate(dk_before, axis=1), jnp.concatenate(dv_before, axis=1)
        carry_k[...] = from_next_k
        carry_v[...] = from_next_v
        _acc_out(dsink_ref, first, dsink)

    rows = ATT_BPS * WINDOW
    rev = lambda i: nstep - 1 - i
    cur = lambda width: pl.BlockSpec((None, rows, width), lambda b, i: (b, rev(i), 0))
    q_spec = pl.BlockSpec((None, ATT_BPS, ATT_KV_HEADS, GROUP_ROWS, ATT_HEAD_DIM), lambda b, i: (b, rev(i), 0, 0, 0))
    kv_cur = pl.BlockSpec((None, ATT_KV_HEADS, rows, ATT_HEAD_DIM), lambda b, i: (b, 0, rev(i), 0))
    kv_prev = pl.BlockSpec((None, ATT_KV_HEADS, WINDOW, ATT_HEAD_DIM), lambda b, i: (b, 0, jnp.maximum(ATT_BPS * rev(i) - 1, 0), 0))
    tab = pl.BlockSpec((rows, LANE), lambda b, i: (rev(i), 0))
    return _call(
        body, (sinks, dcat, raw, w_norm, qh, kh, kh, vh, vh, lse, *tables, *biases, dproj), name="attn_bwd", grid=(bsz, nstep),
        in_specs=[pl.BlockSpec(memory_space=pltpu.SMEM), cur(ATT_WIDTH), cur(ATT_WIDTH), _vec_spec(ATT_WIDTH), q_spec,
                  kv_cur, kv_prev, kv_cur, kv_prev, cur(LANE), tab, tab, tab, _bias_spec(True), _bias_spec(True),
                  pl.BlockSpec(memory_space=pl.ANY)],
        out_specs=[cur(ATT_COLS), _vec_spec(ATT_WIDTH), _vec_spec(ATT_Q_HEADS)],
        out_shape=[_sds(dproj.shape, BF16), _sds((1, ATT_WIDTH), F32), _sds((1, ATT_Q_HEADS), F32)],
        scratch_shapes=[pltpu.VMEM((WINDOW, LANE), F32), pltpu.VMEM((WINDOW, LANE), F32)],
        sem=("arbitrary", "arbitrary"), comms=comms, aliases={15: 0})


def _other_chips(x, y):
    return [(1 - x, y), (x, 1 - y), (1 - x, 1 - y)]


def _sem_pair(n):
    return [pltpu.SemaphoreType.DMA((n,)), pltpu.SemaphoreType.DMA((n,))]


def _plan_pair_forward(bufs):
    n = len(bufs)

    def copies(outs, sems):
        x, y, c = _mesh_pos()
        sends, lands = [], []
        for a in range(n):
            for j, chip in enumerate(_other_chips(x, y)):
                k = 3 * a + j
                slot = outs[a].at[4 * chip[0] + 2 * chip[1] + c]
                sends.append(pltpu.make_async_remote_copy(
                    src_ref=slot, dst_ref=slot, send_sem=sems[0].at[k], recv_sem=sems[1].at[k],
                    device_id=(x, y, 1 - c), device_id_type=MESH))
                theirs = outs[a].at[4 * chip[0] + 2 * chip[1] + 1 - c]
                lands.append(pltpu.make_async_remote_copy(
                    src_ref=theirs, dst_ref=theirs, send_sem=sems[0].at[k], recv_sem=sems[1].at[k],
                    device_id=(x, y, 1 - c), device_id_type=MESH))
        return sends, lands

    def start(ins, outs, sems):
        for cp in copies(outs, sems)[0]:
            cp.start()

    def finish(ins, outs, sems):
        sends, lands = copies(outs, sems)
        for cp in lands:
            cp.wait_recv()
        for cp in sends:
            cp.wait_send()

    return _Comm(list(bufs), [_sds(b.shape, b.dtype) for b in bufs], _sem_pair(3 * n), start, finish,
                 aliases=[(a, a) for a in range(n)])


def _plan_pair(arrays, other_half):
    n = len(arrays)
    per = N_CHIPS if other_half == "chip_major" else 1

    def copies(ins, outs, sems):
        x, y, c = _mesh_pos()
        out = []
        for a in range(n):
            for k in range(per):
                if other_half == "chip_major":
                    src, dst = ins[a].at[k, 1 - c], outs[a].at[k]
                else:
                    src, dst = (ins[a].at[1 - c] if other_half else ins[a]), outs[a]
                out.append(pltpu.make_async_remote_copy(
                    src_ref=src, dst_ref=dst, send_sem=sems[0].at[per * a + k], recv_sem=sems[1].at[per * a + k],
                    device_id=(x, y, 1 - c), device_id_type=MESH))
        return out

    def start(ins, outs, sems):
        for cp in copies(ins, outs, sems):
            cp.start()

    def finish(ins, outs, sems):
        for cp in copies(ins, outs, sems):
            cp.wait()

    if other_half == "chip_major":
        shapes = [_sds((a.shape[0],) + a.shape[2:], a.dtype) for a in arrays]
    else:
        shapes = [_sds(a.shape[1:] if other_half else a.shape, a.dtype) for a in arrays]
    return _Comm(list(arrays), shapes, _sem_pair(per * n), start, finish)


def _plan_chip_exchange(arrays):
    n = len(arrays)

    def copies(ins, outs, sems):
        x, y, c = _mesh_pos()
        sends, lands = [], []
        for a in range(n):
            for j, chip in enumerate(_other_chips(x, y)):
                k = 3 * a + j
                sends.append(pltpu.make_async_remote_copy(
                    src_ref=ins[a].at[2 * chip[0] + chip[1]], dst_ref=outs[a].at[2 * x + y], send_sem=sems[0].at[k],
                    recv_sem=sems[1].at[k], device_id=(*chip, c), device_id_type=MESH))
                slot = outs[a].at[2 * chip[0] + chip[1]]
                lands.append(pltpu.make_async_remote_copy(
                    src_ref=slot, dst_ref=slot, send_sem=sems[0].at[k], recv_sem=sems[1].at[k],
                    device_id=(*chip, c), device_id_type=MESH))
        return sends, lands

    def start(ins, outs, sems):
        for cp in copies(ins, outs, sems)[0]:
            cp.start()

    def finish(ins, outs, sems):
        sends, lands = copies(ins, outs, sems)
        for cp in lands:
            cp.wait_recv()
        for cp in sends:
            cp.wait_send()

    return _Comm(list(arrays), [_sds(a.shape, a.dtype) for a in arrays], _sem_pair(3 * n), start, finish)


SEM_SPEC = pl.BlockSpec(memory_space=pltpu.SEMAPHORE)
N_OTHER = N_CHIPS - 1


def _exchange_copies(s_ref, land_ref, sems):
    x, y, c = _mesh_pos()
    return [pltpu.make_async_remote_copy(
        src_ref=s_ref.at[2 * chip[0] + chip[1]], dst_ref=land_ref.at[2 * x + y], send_sem=sems[j], recv_sem=sems[N_OTHER + j],
        device_id=(*chip, c), device_id_type=MESH) for j, chip in enumerate(_other_chips(x, y))]


def _exchange_start(s, name):
    def body(s_ref, land_ref, *outs):
        sems, token = outs[:2 * N_OTHER], outs[-1]
        for cp in _exchange_copies(s_ref, land_ref, sems):
            cp.start()
        token[...] = jnp.zeros_like(token)

    hbm = pltpu.HBM(s.shape, s.dtype)
    res = pl.pallas_call(
        body, name=name,
        out_shape=(pltpu.SemaphoreType.DMA(()),) * (2 * N_OTHER) + (hbm, hbm, _sds((SUBLANES, LANE), F32)),
        in_specs=(HBM_SPEC, HBM_SPEC),
        out_specs=(SEM_SPEC,) * (2 * N_OTHER) + (HBM_SPEC, HBM_SPEC, pl.BlockSpec(memory_space=pltpu.VMEM)),
        input_output_aliases={0: 2 * N_OTHER, 1: 2 * N_OTHER + 1},
        compiler_params=pltpu.CompilerParams(has_side_effects=pltpu.SideEffectType.DATAFLOW_SIDE_EFFECTING),
    )(pltpu.with_memory_space_constraint(s, pltpu.HBM), pltpu.with_memory_space_constraint(lax.empty(s.shape, s.dtype), pltpu.HBM))
    return res[:2 * N_OTHER], res[2 * N_OTHER], res[2 * N_OTHER + 1], res[-1]


def _exchange_wait(sems, s_thru, land_thru, afters, name):
    def body(s_ref, land_ref, *rest):
        for cp in _exchange_copies(s_ref, land_ref, rest[:2 * N_OTHER]):
            cp.wait_send()
            cp.wait_recv()

    hbm = pltpu.HBM(s_thru.shape, s_thru.dtype)
    return pl.pallas_call(
        body, name=name, out_shape=(hbm, hbm),
        in_specs=(HBM_SPEC, HBM_SPEC) + (SEM_SPEC,) * (2 * N_OTHER) + (pl.BlockSpec(memory_space=pl.ANY),) * len(afters),
        out_specs=(HBM_SPEC, HBM_SPEC), input_output_aliases={0: 0, 1: 1},
        compiler_params=pltpu.CompilerParams(has_side_effects=pltpu.SideEffectType.DATAFLOW_SIDE_EFFECTING),
    )(s_thru, land_thru, *sems, *afters)


def _gather_copies(block_ref, buf_ref, sems):
    x, y, c = _mesh_pos()
    return [pltpu.make_async_remote_copy(
        src_ref=block_ref, dst_ref=buf_ref.at[4 * x + 2 * y + c], send_sem=sems[j], recv_sem=sems[N_OTHER + j],
        device_id=(*chip, c), device_id_type=MESH) for j, chip in enumerate(_other_chips(x, y))]


def _gather_start(blocks, bufs, afters, name):
    n = len(blocks)
    per = 2 * N_OTHER

    def body(*refs):
        ins, outs = refs[:2 * n], refs[2 * n + len(afters):]
        for a in range(n):
            for cp in _gather_copies(ins[a], ins[n + a], outs[a * per:(a + 1) * per]):
                cp.start()
        outs[-1][...] = jnp.zeros_like(outs[-1])

    hbm = [pltpu.HBM(z.shape, z.dtype) for z in list(blocks) + list(bufs)]
    res = pl.pallas_call(
        body, name=name,
        out_shape=(pltpu.SemaphoreType.DMA(()),) * (n * per) + tuple(hbm) + (_sds((SUBLANES, LANE), F32),),
        in_specs=(HBM_SPEC,) * (2 * n) + (pl.BlockSpec(memory_space=pl.ANY),) * len(afters),
        out_specs=(SEM_SPEC,) * (n * per) + (HBM_SPEC,) * (2 * n) + (pl.BlockSpec(memory_space=pltpu.VMEM),),
        input_output_aliases={k: n * per + k for k in range(2 * n)},
        compiler_params=pltpu.CompilerParams(has_side_effects=pltpu.SideEffectType.DATAFLOW_SIDE_EFFECTING),
    )(*[pltpu.with_memory_space_constraint(z, pltpu.HBM) for z in list(blocks) + list(bufs)], *afters)
    parts = [(res[a * per:(a + 1) * per], res[n * per + a], res[n * per + n + a]) for a in range(n)]
    return parts, res[-1]


def _gather_wait(part, afters, name):
    sems, block, buf = part

    def body(block_ref, buf_ref, *rest):
        for cp in _gather_copies(block_ref, buf_ref, rest[:2 * N_OTHER]):
            cp.wait_send()
            cp.wait_recv()

    return pl.pallas_call(
        body, name=name, out_shape=(pltpu.HBM(block.shape, block.dtype), pltpu.HBM(buf.shape, buf.dtype)),
        in_specs=(HBM_SPEC, HBM_SPEC) + (SEM_SPEC,) * (2 * N_OTHER) + (pl.BlockSpec(memory_space=pl.ANY),) * len(afters),
        out_specs=(HBM_SPEC, HBM_SPEC), input_output_aliases={0: 0, 1: 1},
        compiler_params=pltpu.CompilerParams(has_side_effects=pltpu.SideEffectType.DATAFLOW_SIDE_EFFECTING),
    )(block, buf, *sems, *afters)[1]


def _comm_only(comms, name):
    return _call(lambda: None, (), name=name, grid=(), in_specs=[], out_specs=[], out_shape=[], sem=(), comms=comms)[1]


def _allgather8(arrays, name):
    return _comm_only([_plan_allgather8(arrays)], name)[0]


def _plan_allgather8(arrays):
    n = len(arrays)

    def parts(ins, outs, sems):
        send_sems, recv_sems, local_sems = sems
        x, y, c = _mesh_pos()
        me, sibling = (x, y, c), (x, y, 1 - c)
        chips = _other_chips(x, y)

        def copy(a, k, block, to, src=None):
            dst = outs[a].at[4 * block[0] + 2 * block[1] + block[2]]
            return pltpu.make_async_remote_copy(
                src_ref=dst if src is None else src, dst_ref=dst, send_sem=send_sems.at[7 * a + k],
                recv_sem=recv_sems.at[7 * a + k], device_id=to, device_id_type=MESH)

        mine = [pltpu.make_async_copy(ins[a], outs[a].at[4 * x + 2 * y + c], local_sems.at[a]) for a in range(n)]
        first = []
        for a in range(n):
            first.append(copy(a, 0, me, sibling, src=ins[a]))
            first += [copy(a, 1 + j, me, (*chip, c), src=ins[a]) for j, chip in enumerate(chips)]
        return copy, mine, first, me, sibling, chips, c

    def start(ins, outs, sems):
        _, mine, first, *_ = parts(ins, outs, sems)
        for cp in mine + first:
            cp.start()

    def finish(ins, outs, sems):
        copy, mine, first, me, sibling, chips, c = parts(ins, outs, sems)
        passed = []
        for j, chip in enumerate(chips):
            for a in range(n):
                copy(a, 1 + j, (*chip, c), me).wait_recv()
                fwd = copy(a, 4 + j, (*chip, c), sibling)
                fwd.start()
                passed.append(fwd)
        for a in range(n):
            copy(a, 0, sibling, me).wait_recv()
            for j, chip in enumerate(chips):
                copy(a, 4 + j, (*chip, 1 - c), me).wait_recv()
        for cp in first + passed:
            cp.wait_send()
        for cp in mine:
            cp.wait()

    sems = [pltpu.SemaphoreType.DMA((7 * n,)), pltpu.SemaphoreType.DMA((7 * n,)), pltpu.SemaphoreType.DMA((n,))]
    return _Comm(list(arrays), [_sds((N_DEV,) + a.shape, a.dtype) for a in arrays], sems, start, finish)


def _pair_sum(g, q, core, name, chip_major=False):
    rows, cols = g.shape[2:]
    tr = _row_tile(rows)

    def body(core_ref, g_ref, q_ref, o_ref):
        o_ref[...] = (g_ref[...] + q_ref[...]).astype(BF16)

    blk = pl.BlockSpec((None, tr, cols), lambda k, i, core_ref: (k, i, 0))
    if chip_major:
        own = pl.BlockSpec((None, None, tr, cols), lambda k, i, core_ref: (k, core_ref[0], i, 0))
    else:
        own = pl.BlockSpec((None, None, tr, cols), lambda k, i, core_ref: (core_ref[0], k, i, 0))
    return pl.pallas_call(
        body, name=name,
        grid_spec=pltpu.PrefetchScalarGridSpec(num_scalar_prefetch=1, grid=(N_CHIPS, rows // tr), in_specs=[own, blk], out_specs=blk),
        out_shape=_sds((N_CHIPS, rows, cols), BF16), compiler_params=_params("parallel", "parallel"),
    )(core, g, q)


def _sum_chips(own, landed, chip, name):
    _, rows, cols = own.shape
    tr = _row_tile(rows)

    def body(chip_ref, own_ref, a_ref, b_ref, c_ref, o_ref):
        acc = own_ref[...].astype(F32) + a_ref[...].astype(F32)
        o_ref[...] = (acc + b_ref[...].astype(F32)) + c_ref[...].astype(F32)

    blk = lambda flip: pl.BlockSpec((None, tr, cols), lambda i, chip_ref: (jnp.bitwise_xor(chip_ref[0], flip), i, 0))
    return pl.pallas_call(
        body, name=name,
        grid_spec=pltpu.PrefetchScalarGridSpec(num_scalar_prefetch=1, grid=(rows // tr,), in_specs=[blk(0), blk(1), blk(2), blk(3)],
                                               out_specs=pl.BlockSpec((tr, cols), lambda i, chip_ref: (i, 0))),
        out_shape=_sds((rows, cols), F32), compiler_params=_params("parallel"),
    )(chip, own, landed, landed, landed)


SUBLANES = 8


def _tile_rows(n_elems):
    return -(-n_elems // (SUBLANES * LANE)) * SUBLANES


SMALL_ITEMS = (("b_ada", N_MOD * D_MODEL), ("pre_w_mix", D_MODEL), ("post_w_mix", D_MODEL), ("pre_w_mlp", D_MODEL),
               ("post_w_mlp", D_MODEL), ("attn_out_w", ATT_WIDTH), ("hg_norm_w", HG_HEAD_DIM), ("attn_sinks", ATT_Q_HEADS),
               ("lb_0", HG_WIDTH), ("lb_1", HG_WIDTH))
SMALL_AT = {}
for _name, _size in SMALL_ITEMS:
    SMALL_AT[_name] = (sum(r for _, r in SMALL_AT.values()), _tile_rows(_size))
SMALL_ROWS = sum(r for _, r in SMALL_AT.values())
MOD_ROWS = SMALL_AT["b_ada"][1]
PLAIN_ROWS = SMALL_AT["lb_0"][0] - MOD_ROWS
LB_ROWS = SMALL_AT["lb_0"][1]


def _rows(a, nrows=None):
    flat = a.reshape(-1)
    nrows = _tile_rows(flat.shape[0]) if nrows is None else nrows
    return jnp.pad(flat, (0, nrows * LANE - flat.shape[0])).reshape(nrows, LANE)


def _pack_small(vals):
    vals = dict(vals, lb_0=vals["lb_table"][0], lb_1=vals["lb_table"][1])
    return jnp.concatenate([_rows(vals[name], SMALL_AT[name][1]) for name, _ in SMALL_ITEMS], axis=0)


def _unpack_small(p):
    def item(name, shape):
        first = SMALL_AT[name][0]
        size = shape[0] * shape[1]
        return p[first:first + SMALL_AT[name][1]].reshape(-1)[:size].reshape(shape)

    out = {name: item(name, (1, size)) for name, size in SMALL_ITEMS if not name.startswith("lb_")}
    out["lb_table"] = jnp.concatenate([item("lb_0", (1, HG_WIDTH)), item("lb_1", (1, HG_WIDTH))], axis=0)
    return out


def _pack_partials(dmod, plain, d_lb, loss_row):
    return jnp.concatenate([_rows(dmod, dmod.shape[0] * MOD_ROWS)] + [_rows(g) for g in plain] + [_rows(d_lb), _rows(loss_row)], axis=0)


def _small_update(packs, w, m, v, n_seq):
    mod_end = n_seq * MOD_ROWS
    lb_at = mod_end + PLAIN_ROWS
    t0, t1 = SMALL_AT["lb_0"][0], SMALL_AT["lb_1"][0]

    def body(p_ref, w_ref, m_ref, v_ref, g_ref, dl_ref, nm_ref, nv_ref, loss_ref):
        tot = p_ref[0]
        for d in range(1, N_DEV):
            tot = tot + p_ref[d]
        wv = w_ref[...]
        p1 = _sigmoid(wv[t1:t1 + LB_ROWS] - wv[t0:t0 + LB_ROWS])
        s = tot[lb_at:lb_at + LB_ROWS] * p1 * (1.0 - p1)
        g_bias = tot[0:MOD_ROWS]
        for q in range(1, n_seq):
            g_bias = g_bias + tot[q * MOD_ROWS:(q + 1) * MOD_ROWS]
        g = jnp.concatenate([g_bias, tot[mod_end:lb_at], -s, s], axis=0)
        g_ref[...] = g
        dl_ref[...], nm_ref[...], nv_ref[...] = _adamw_math(g, wv, m_ref[...], v_ref[...])
        loss_ref[...] = tot[lb_at + LB_ROWS:lb_at + LB_ROWS + SUBLANES]

    shp = _sds((SMALL_ROWS, LANE), F32)
    return pl.pallas_call(body, name="small_update", out_shape=[shp] * 4 + [_sds((SUBLANES, LANE), F32)],
                          compiler_params=_params())(packs, w, m, v)


def kernel(x, c, w_ada, b_ada, pre_w_mix, w_in, attn_sinks, attn_out_w, lb_table, hg_norm_w, w_out, post_w_mix, pre_w_mlp, w_up, w_down, post_w_mlp, loss_target, m_w_ada, m_b_ada, m_pre_w_mix, m_w_in, m_attn_sinks, m_attn_out_w, m_lb_table, m_hg_norm_w, m_w_out, m_post_w_mix, m_pre_w_mlp, m_w_up, m_w_down, m_post_w_mlp, v_w_ada, v_b_ada, v_pre_w_mix, v_w_in, v_attn_sinks, v_attn_out_w, v_lb_table, v_hg_norm_w, v_w_out, v_post_w_mix, v_pre_w_mlp, v_w_up, v_w_down, v_post_w_mlp):
    xi, yi, ci = _mesh_pos()
    chip = 2 * xi + yi
    dev = 2 * chip + ci
    bsz, seq, _ = x.shape
    ntok = bsz * seq
    ada_cols = w_ada.shape[2]
    core = jnp.reshape(ci, (1,)).astype(jnp.int32)
    chip_idx = jnp.reshape(chip, (1,)).astype(jnp.int32)
    flat = lambda a: a.reshape(ntok, a.shape[-1])
    unflat = lambda a: a.reshape(bsz, seq, a.shape[-1])
    tables = _rope_tables(seq)
    biases, chunk_masks = _band_biases(), _block_masks()

    def row_half(w):
        rows = w.shape[1] // 2
        return lax.dynamic_slice_in_dim(w[0], ci * rows, rows, axis=0).astype(BF16)

    def gather_buffer(w):
        rows, cols = w.shape[1] // 2, w.shape[2]
        own = w[0].astype(BF16).reshape(2, rows, cols)
        return lax.dynamic_update_slice(lax.empty((N_DEV, rows, cols), BF16), own, (2 * chip, 0, 0))

    w_in_t, m_in_t, v_in_t = [jnp.transpose(a[0])[None] for a in (w_in, m_w_in, v_w_in)]
    c_g, in_g = _allgather8([c, row_half(w_in_t)], "gather_first")
    c_all = c_g.reshape(N_DEV * bsz, D_MODEL)
    w_in_full = in_g.reshape(IN_COLS, D_MODEL)

    b_cols = lax.dynamic_slice_in_dim(b_ada, chip * ada_cols, ada_cols, axis=1)
    mod_part = _ada_fwd(c_all, w_ada[0], b_cols)
    half_rows = mod_part.shape[0] // 2
    (mod_g,) = _allgather8([lax.dynamic_slice_in_dim(mod_part, ci * half_rows, half_rows, axis=0)], "gather_mod")
    mod_all = mod_g.reshape(N_CHIPS, 2, half_rows, ada_cols).transpose(1, 2, 0, 3).reshape(N_DEV * bsz, N_MOD * D_MODEL)
    mod = lax.dynamic_slice_in_dim(mod_all, dev * bsz, bsz, axis=0)
    sh1, sc1, g1, sh2, sc2, g2 = [mod[:, i * D_MODEL:(i + 1) * D_MODEL].reshape(bsz, 1, D_MODEL) for i in range(N_MOD)]

    weights = (w_out, w_up, w_down)
    (out_part, up_part, down_part), started = _gather_start(
        [row_half(w) for w in weights], [gather_buffer(w) for w in weights], [mod_g], "gather_weights_start")

    h1, proj, qh, kh, vh = _in_proj_fused(x, pre_w_mix, sc1 + started[0:1, 0:1], sh1, w_in_full, tables)
    out_g = _gather_wait(out_part, [proj], "gather_out_wait")
    (attn_raw, cat, lse), ((out_g,),) = _attn_fwd(qh, kh, vh, attn_sinks, attn_out_w, biases, comms=[_plan_pair_forward([out_g])])
    up_g = _gather_wait(up_part, [attn_raw], "gather_up_wait")
    (o_raw, cat, states), ((up_g,),) = _hgrn_fwd(proj, lb_table, hg_norm_w, cat, chunk_masks, comms=[_plan_pair_forward([up_g])])
    down_g = _gather_wait(down_part, [o_raw], "gather_down_wait")
    w_out_full = out_g.reshape(D_MODEL, D_MODEL)
    w_up4 = up_g.reshape(N_CHIPS, D_MODEL, D_MODEL)
    mix, x1, h2 = _out_proj_fused(cat, w_out_full, x, post_w_mix, g1, pre_w_mlp, sc2, sh2)
    big_tm = min(ntok, 2048)
    up_spec = pl.BlockSpec((None, D_MODEL, D_MODEL), lambda i, j: (j, 0, 0))
    r, ((down_g,),) = _mm(flat(h2), w_up4, name="up_proj", out_dtype=BF16, tm=big_tm, tn=D_MODEL, n_out=D_FF, b_spec=up_spec,
                          epi=lambda acc: jnp.maximum(acc, 0.0), comms=[_plan_pair_forward([down_g])])
    w_down_full = down_g.reshape(D_FF, D_MODEL)
    square = lambda t: t * t
    loss_row, dy, dd, dg2, d_post_mlp = _down_proj_fused(unflat(r), w_down_full, x1, post_w_mlp, g2, loss_target)

    dpre = _mm(flat(dd), w_down_full, name="down_bwd", out_dtype=BF16, trans_b=True, tm=big_tm, tn=D_MODEL, extra=(r,),
               epi=lambda acc, rt: acc * (2.0 * rt.astype(F32)))
    half_rows = D_MODEL // 2
    g_down = _mm_tn(r, flat(dd), name="down_wgrad", tk=half_rows, tn=D_MODEL, a_fn=square,
                    out_shape=_sds((2, N_CHIPS, half_rows, D_MODEL), F32),
                    out_spec=pl.BlockSpec((None, None, half_rows, D_MODEL), lambda i, j: (i % 2, i // 2, 0, 0)))
    (dx1, dmix, dsc2, dsh2, dg1, d_pre_mlp, d_post_mix), ((q_down,),) = _up_bwd_fused(
        unflat(dpre), w_up4, dy, x1, mix, pre_w_mlp, sc2, post_w_mix, g1, comms=[_plan_pair([g_down], True)])
    g_up = _mm_tn(flat(h2), dpre, name="up_wgrad", tk=D_MODEL, tn=half_rows,
                  out_shape=_sds((2, N_CHIPS, half_rows, D_MODEL), F32),
                  out_spec=pl.BlockSpec((2, None, half_rows, half_rows), lambda i, j: (0, j // 2, 0, j % 2)))
    s_down = _pair_sum(g_down, q_down, core, "pair_sum_down")

    dcat, ((q_up,),) = _mm(flat(dmix), w_out_full, name="out_bwd", out_dtype=F32, trans_b=True, comms=[_plan_pair([g_up], True)])
    dcat = unflat(dcat)
    s_up = _pair_sum(g_up, q_up, core, "pair_sum_up")
    out_rows = D_MODEL // N_CHIPS
    g_out = _mm_tn(flat(cat), flat(dmix), name="out_wgrad", tk=2 * out_rows, tn=half_rows,
                   out_shape=_sds((2, N_CHIPS, out_rows, half_rows), F32),
                   out_spec=pl.BlockSpec((None, 2, out_rows, half_rows), lambda i, j: (j, i, 0, 0)))
    (dproj_rec, d_lb, d_hg_norm), ((x_down,), (q_out,)) = _hgrn_bwd(
        dcat, proj, o_raw, states, lb_table, hg_norm_w, chunk_masks, comms=[_plan_chip_exchange([s_down]), _plan_pair([g_out], True)])
    half_down = _sum_chips(s_down, x_down, chip_idx, "sum_chips_down")
    s_out = _pair_sum(g_out, q_out, core, "pair_sum_out")
    (dproj, d_attn_out, d_sinks), ((their_down,), (x_up,)) = _attn_bwd(
        dcat, attn_raw, attn_out_w, qh, kh, vh, lse, attn_sinks, tables, [bias.T for bias in biases], dproj_rec,
        comms=[_plan_pair([half_down], False), _plan_chip_exchange([s_up])])
    half_up = _sum_chips(s_up, x_up, chip_idx, "sum_chips_up")
    dproj = flat(dproj)
    in_rows = IN_COLS // N_CHIPS // 2
    g_in, ((x_out,),) = _mm_tn(dproj, flat(h1), name="in_wgrad", tk=2 * LANE, tn=D_MODEL, comms=[_plan_chip_exchange([s_out])])
    g_in = g_in.reshape(N_CHIPS, 2, in_rows, D_MODEL)
    half_out = _sum_chips(s_out, x_out, chip_idx, "sum_chips_out")
    dh1, ((q_in,), (their_up, their_out)) = _mm(
        dproj, w_in_full, name="in_bwd", out_dtype=F32,
        comms=[_plan_pair([g_in], "chip_major"), _plan_pair([half_up, half_out], False)])
    s_in = _pair_sum(g_in, q_in, core, "pair_sum_in", chip_major=True)
    in_sems, s_in, in_landing, started = _exchange_start(s_in, "exchange_in_start")
    grad_x, dsc1, dsh1, d_pre_mix = _norm1_bwd(unflat(dh1), dx1, x, pre_w_mix + started[0:1, 0:1], sc1)

    dmod = jnp.concatenate([dsh1, dsc1, dg1, dsh2, dsc2, dg2], axis=-1).reshape(bsz, N_MOD * D_MODEL)
    pack = _pack_partials(dmod, [d_pre_mix, d_post_mix, d_pre_mlp, d_post_mlp, d_attn_out, d_hg_norm, d_sinks], d_lb, loss_row)
    ((packs,),) = _comm_only([_plan_allgather8([pack])], "gather_small")
    w_small = dict(b_ada=b_ada, pre_w_mix=pre_w_mix, post_w_mix=post_w_mix, pre_w_mlp=pre_w_mlp, post_w_mlp=post_w_mlp,
                   attn_out_w=attn_out_w, hg_norm_w=hg_norm_w, attn_sinks=attn_sinks, lb_table=lb_table)
    m_small = dict(b_ada=m_b_ada, pre_w_mix=m_pre_w_mix, post_w_mix=m_post_w_mix, pre_w_mlp=m_pre_w_mlp, post_w_mlp=m_post_w_mlp,
                   attn_out_w=m_attn_out_w, hg_norm_w=m_hg_norm_w, attn_sinks=m_attn_sinks, lb_table=m_lb_table)
    v_small = dict(b_ada=v_b_ada, pre_w_mix=v_pre_w_mix, post_w_mix=v_post_w_mix, pre_w_mlp=v_pre_w_mlp, post_w_mlp=v_post_w_mlp,
                   attn_out_w=v_attn_out_w, hg_norm_w=v_hg_norm_w, attn_sinks=v_attn_sinks, lb_table=v_lb_table)
    *small_packed, loss_rows = _small_update(packs, _pack_small(w_small), _pack_small(m_small), _pack_small(v_small), bsz)
    small_out = [_unpack_small(p) for p in small_packed]
    loss = loss_rows[0, 0]

    dmod_all = packs[:, :bsz * MOD_ROWS, :].reshape(N_DEV * bsz, N_MOD * D_MODEL)
    dmod_cols = lax.dynamic_slice_in_dim(dmod_all, chip * ada_cols, ada_cols, axis=1)
    ada_out = _ada_bwd_adamw(c_all, dmod_cols, w_ada[0], m_w_ada[0], v_w_ada[0])

    s_in, x_in = _exchange_wait(in_sems, s_in, in_landing, [grad_x, ada_out[0]], "exchange_in_wait")
    half_in = _sum_chips(s_in, x_in, chip_idx, "sum_chips_in")
    ((their_in,),) = _comm_only([_plan_pair([half_in], False)], "pair_swap_in")
    big = dict(
        w_in=tuple(jnp.transpose(a) for a in _adamw_halves(half_in, their_in, core, w_in_t[0], m_in_t[0], v_in_t[0], axis=0,
                                                           name="adamw_in")),
        w_up=tuple(_adamw_halves(half_up, their_up, core, w_up[0], m_w_up[0], v_w_up[0], axis=0, name="adamw_up")),
        w_out=tuple(_adamw_halves(half_out, their_out, core, w_out[0], m_w_out[0], v_w_out[0], axis=1, name="adamw_out")),
        w_down=tuple(_adamw_halves(half_down, their_down, core, w_down[0], m_w_down[0], v_w_down[0], axis=0, name="adamw_down")),
        w_ada=tuple(ada_out),
    )
    order = ("w_ada", "b_ada", "pre_w_mix", "w_in", "attn_sinks", "attn_out_w", "lb_table", "hg_norm_w", "w_out", "post_w_mix",
             "pre_w_mlp", "w_up", "w_down", "post_w_mlp")
    outs = [loss, grad_x]
    for kind in range(4):
        for nm in order:
            outs.append(big[nm][kind][None] if nm in big else small_out[kind][nm])
    return tuple(outs)
```

```python
import jax
import jax.numpy as jnp
from jax import lax
from jax.experimental import pallas as pl
from jax.experimental.pallas import tpu as pltpu

F32 = jnp.float32
BF16 = jnp.bfloat16

D_MODEL = 1024
ATT_WIDTH = 512
ATT_HEAD_DIM = 64
ATT_Q_HEADS = 8
ATT_KV_HEADS = 2
ATT_GROUP = ATT_Q_HEADS // ATT_KV_HEADS
ATT_KV_COLS = ATT_KV_HEADS * ATT_HEAD_DIM
WINDOW = 128
ROPE_DIM = 16
ROPE_THETA = 500000.0
HG_WIDTH = 512
MIX_WIDTH = ATT_WIDTH + HG_WIDTH
HG_HEAD_DIM = 128
HG_HEADS = 4
HG_CHUNK = 32
IN_COLS = ATT_WIDTH + 2 * ATT_KV_COLS + 4 * HG_WIDTH
ATT_COLS = ATT_WIDTH + 2 * ATT_KV_COLS
D_FF = 4 * D_MODEL
N_MOD = 6
EPS = 1e-6
ATT_SCALE = ATT_HEAD_DIM ** -0.5

ADAM_LR = 0.001
ADAM_B1 = 0.9
ADAM_B2 = 0.999
ADAM_EPS = 1e-08
ADAM_WD = 0.01
ADAM_STEP = 10

N_CHIPS = 4
N_DEV = 8
LANE = 128
VMEM_LIMIT = 48 * 1024 * 1024
VMEM_LIMIT_BIG = 58 * 1024 * 1024
MESH = pl.DeviceIdType.MESH

NT_DIMS = (((1,), (1,)), ((), ()))
TN_DIMS = (((0,), (0,)), ((), ()))


def _sds(shape, dtype):
    return jax.ShapeDtypeStruct(tuple(shape), dtype)


def _params(*sem, vmem_limit=None):
    return pltpu.CompilerParams(dimension_semantics=sem, vmem_limit_bytes=VMEM_LIMIT if vmem_limit is None else vmem_limit)


def _sigmoid(x):
    return 1.0 / (1.0 + jnp.exp(-x))


def _dot(a, b, dims=None):
    a, b = a.astype(BF16), b.astype(BF16)
    if dims is None:
        return jnp.dot(a, b, preferred_element_type=F32)
    return lax.dot_general(a, b, dims, preferred_element_type=F32)


def _rms_fwd(x, w):
    rstd = lax.rsqrt(jnp.mean(x * x, axis=-1, keepdims=True) + EPS)
    xh = x * rstd
    return xh * w, xh, rstd


def _rms_bwd(dy, xh, rstd, w):
    dxh = dy * w
    dx = rstd * (dxh - xh * jnp.mean(dxh * xh, axis=-1, keepdims=True))
    return dx, dy * xh


def _colsum(x):
    return jnp.sum(x, axis=0, keepdims=True)


def _rms_hat(x):
    rstd = lax.rsqrt(jnp.mean(x * x, axis=-1, keepdims=True) + EPS)
    return x * rstd, rstd


def _rms_bwd_gain(dy, gain, xh, rstd):
    dxh = dy * gain
    dx = rstd * (dxh - xh * jnp.mean(dxh * xh, axis=-1, keepdims=True))
    return dx, _colsum(dy * xh)


def _row_tile(rows, cap=256):
    return max(t for t in range(16, cap + 1, 16) if rows % t == 0)


HBM_SPEC = pl.BlockSpec(memory_space=pltpu.HBM)


def _mesh_pos():
    return lax.axis_index("x"), lax.axis_index("y"), lax.axis_index("c")


class _Comm:
    def __init__(self, ins, outs, sems, start, finish, aliases=()):
        self.ins, self.outs, self.sems = list(ins), list(outs), list(sems)
        self.start, self.finish, self.aliases = start, finish, tuple(aliases)


def _call(body, args, *, name, grid, in_specs, out_specs, out_shape, sem, scratch_shapes=(), comms=(), aliases=None,
          vmem_limit=None):
    scratch_shapes = list(scratch_shapes)
    if not comms:
        return pl.pallas_call(body, name=name, grid=grid, in_specs=in_specs, out_specs=out_specs, out_shape=out_shape,
                              input_output_aliases=dict(aliases or {}), scratch_shapes=scratch_shapes,
                              compiler_params=_params(*sem, vmem_limit=vmem_limit))(*args)
    single = not isinstance(out_shape, (list, tuple))
    out_specs_l = [out_specs] if single else list(out_specs)
    out_shape_l = [out_shape] if single else list(out_shape)
    n_in, n_out, n_scr = len(in_specs), len(out_shape_l), len(scratch_shapes)
    n_ci = [len(cm.ins) for cm in comms]
    n_co = [len(cm.outs) for cm in comms]
    n_cs = [len(cm.sems) for cm in comms]
    aliases = dict(aliases or {})
    for k, cm in enumerate(comms):
        for i, o in cm.aliases:
            aliases[n_in + sum(n_ci[:k]) + i] = n_out + sum(n_co[:k]) + o

    def fused(*refs):
        pos = [0]

        def take(n):
            part = refs[pos[0]:pos[0] + n]
            pos[0] += n
            return part

        ins = take(n_in)
        c_ins = [take(n) for n in n_ci]
        outs = take(n_out)
        c_outs = [take(n) for n in n_co]
        scr = take(n_scr)
        c_sems = [take(n) for n in n_cs]
        first, last = True, True
        for d, size in enumerate(grid):
            first = jnp.logical_and(first, pl.program_id(d) == 0)
            last = jnp.logical_and(last, pl.program_id(d) == size - 1)

        def run(which):
            for cm, ci, co, cs in zip(comms, c_ins, c_outs, c_sems):
                getattr(cm, which)(ci, co, cs)

        if grid:
            pl.when(first)(lambda: run("start"))
        else:
            run("start")
        body(*ins, *outs, *scr)
        if grid:
            pl.when(last)(lambda: run("finish"))
        else:
            run("finish")

    res = pl.pallas_call(
        fused, name=name, grid=grid, in_specs=list(in_specs) + [HBM_SPEC] * sum(n_ci),
        out_specs=out_specs_l + [HBM_SPEC] * sum(n_co), out_shape=out_shape_l + [s for cm in comms for s in cm.outs],
        input_output_aliases=aliases, scratch_shapes=scratch_shapes + [s for cm in comms for s in cm.sems],
        compiler_params=_params(*["arbitrary"] * len(grid), vmem_limit=vmem_limit),
    )(*args, *[a for cm in comms for a in cm.ins])
    main = res[:n_out]
    extra, at = [], n_out
    for n in n_co:
        extra.append(list(res[at:at + n]))
        at += n
    return (main[0] if single else list(main)), extra


def _mm(a, b, *, name, out_dtype, trans_b=False, tm=512, tn=None, extra=(), epi=None, b_spec=None, n_out=None, comms=()):
    m_total, k_total = a.shape
    if n_out is None:
        n_out = b.shape[0] if trans_b else b.shape[1]
    tn = n_out if tn is None else tn
    grid = (m_total // tm, n_out // tn)
    dims = NT_DIMS if trans_b else None

    def body(*refs):
        a_ref, b_ref = refs[0], refs[1]
        extra_refs = refs[2:2 + len(extra)]
        o_ref = refs[2 + len(extra)]
        acc = _dot(a_ref[...], b_ref[...], dims)
        if epi is not None:
            acc = epi(acc, *[r[...] for r in extra_refs])
        o_ref[...] = acc.astype(out_dtype)

    if b_spec is None:
        if trans_b:
            b_spec = pl.BlockSpec((tn, k_total), lambda i, j: (j, 0))
        else:
            b_spec = pl.BlockSpec((k_total, tn), lambda i, j: (0, j))
    in_specs = [pl.BlockSpec((tm, k_total), lambda i, j: (i, 0)), b_spec]
    in_specs += [pl.BlockSpec((tm, tn), lambda i, j: (i, j)) for _ in extra]
    return _call(
        body, (a, b, *extra), name=name, grid=grid, in_specs=in_specs,
        out_specs=pl.BlockSpec((tm, tn), lambda i, j: (i, j)),
        out_shape=_sds((m_total, n_out), out_dtype),
        sem=("parallel", "parallel"), comms=comms)


def _mm_tn(a, b, *, name, tk, tn, a_fn=None, out_shape=None, out_spec=None, comms=()):
    m_total, k_total = a.shape
    n_total = b.shape[1]
    grid = (k_total // tk, n_total // tn)

    def body(a_ref, b_ref, o_ref):
        av = a_ref[...]
        part = _dot(av if a_fn is None else a_fn(av), b_ref[...], TN_DIMS)
        o_ref[...] = part.reshape(o_ref.shape)

    if out_shape is None:
        out_shape = _sds((k_total, n_total), F32)
        out_spec = pl.BlockSpec((tk, tn), lambda i, j: (i, j))
    return _call(
        body, (a, b), name=name, grid=grid,
        in_specs=[pl.BlockSpec((m_total, tk), lambda i, j: (0, i)), pl.BlockSpec((m_total, tn), lambda i, j: (0, j))],
        out_specs=out_spec, out_shape=out_shape, sem=("parallel", "parallel"), comms=comms)


def _ada_fwd(c_all, w_shard, b_shard):
    nb, ncol = c_all.shape[0], w_shard.shape[1]
    tn = 512

    def body(c_ref, w_ref, b_ref, o_ref):
        c = c_ref[...]
        o_ref[...] = _dot(c * _sigmoid(c), w_ref[...]) + b_ref[...]

    return pl.pallas_call(
        body, name="ada_fwd", grid=(ncol // tn,),
        in_specs=[pl.BlockSpec((nb, D_MODEL), lambda j: (0, 0)), pl.BlockSpec((D_MODEL, tn), lambda j: (0, j)),
                  pl.BlockSpec((1, tn), lambda j: (0, j))],
        out_specs=pl.BlockSpec((nb, tn), lambda j: (0, j)), out_shape=_sds((nb, ncol), F32),
        compiler_params=_params("parallel"),
    )(c_all, w_shard, b_shard)


def _adamw_math(g, w, m, v):
    m = ADAM_B1 * m + (1.0 - ADAM_B1) * g
    v = ADAM_B2 * v + (1.0 - ADAM_B2) * (g * g)
    m_hat = m / (1.0 - ADAM_B1 ** ADAM_STEP)
    v_hat = v / (1.0 - ADAM_B2 ** ADAM_STEP)
    delta = -ADAM_LR * (m_hat / (jnp.sqrt(v_hat) + ADAM_EPS) + ADAM_WD * w)
    return delta, m, v


def _ada_bwd_adamw(c_all, dmod_cols, w, m, v):
    nb, ncol = dmod_cols.shape
    tn = 256

    def body(c_ref, d_ref, w_ref, m_ref, v_ref, g_ref, dl_ref, nm_ref, nv_ref):
        c = c_ref[...]
        g = _dot(c * _sigmoid(c), d_ref[...], TN_DIMS)
        g_ref[...] = g
        dl_ref[...], nm_ref[...], nv_ref[...] = _adamw_math(g, w_ref[...], m_ref[...], v_ref[...])

    col = pl.BlockSpec((D_MODEL, tn), lambda j: (0, j))
    shp = _sds((D_MODEL, ncol), F32)
    return pl.pallas_call(
        body, name="ada_bwd_adamw", grid=(ncol // tn,),
        in_specs=[pl.BlockSpec((nb, D_MODEL), lambda j: (0, 0)), pl.BlockSpec((nb, tn), lambda j: (0, j)), col, col, col],
        out_specs=[col, col, col, col], out_shape=[shp, shp, shp, shp],
        compiler_params=_params("parallel"),
    )(c_all, dmod_cols, w, m, v)


def _adamw_halves(own, theirs, core, w, m, v, *, axis, name):
    r2, c2 = own.shape
    tr = _row_tile(r2)
    nt = r2 // tr

    def body(core_ref, own_ref, their_ref, w_ref, m_ref, v_ref, g_ref, dl_ref, nm_ref, nv_ref):
        g = jnp.where(pl.program_id(0) == core_ref[0], own_ref[...], their_ref[...])
        g_ref[...] = g
        dl_ref[...], nm_ref[...], nv_ref[...] = _adamw_math(g, w_ref[...], m_ref[...], v_ref[...])

    if axis == 0:
        full = pl.BlockSpec((tr, c2), lambda h, i, core_ref: (h * nt + i, 0))
    else:
        full = pl.BlockSpec((tr, c2), lambda h, i, core_ref: (i, h))
    half = pl.BlockSpec((tr, c2), lambda h, i, core_ref: (i, 0))
    shp = _sds(w.shape, F32)
    return pl.pallas_call(
        body, name=name,
        grid_spec=pltpu.PrefetchScalarGridSpec(num_scalar_prefetch=1, grid=(2, nt), in_specs=[half, half, full, full, full],
                                               out_specs=[full] * 4),
        out_shape=[shp] * 4, compiler_params=_params("parallel", "parallel"),
    )(core, own, theirs, w, m, v)


def _tok_spec(tm, width=D_MODEL):
    return pl.BlockSpec((None, tm, width), lambda b, i: (b, i, 0))


def _row_spec(width=D_MODEL):
    return pl.BlockSpec((None, 1, width), lambda b, i: (b, 0, 0))


def _vec_spec(width=D_MODEL):
    return pl.BlockSpec((1, width), lambda b, i: (0, 0))


class _RowsOf:
    def __init__(self, ref, first, count):
        self.ref, self.rows = ref, slice(first, first + count)

    def __getitem__(self, idx):
        return self.ref[self.rows, :]

    def __setitem__(self, idx, value):
        self.ref[self.rows, :] = value


def _mm_rows(a, b, *, name, tm, extra, extra_specs, out_specs, out_shape, epi, pro=None, trans_b=False, b_chunks=1, comms=(),
             parts=1, zero_per_seq=(), zero_once=(), vmem_limit=None):
    bsz, seq, k_total = a.shape
    kc = k_total // b_chunks
    dims = NT_DIMS if trans_b else None
    rows = tm // parts

    def body(*refs):
        a_ref, b_ref = refs[0], refs[1]
        ex, outs = refs[2:2 + len(extra)], refs[2 + len(extra):]
        if zero_per_seq:
            @pl.when(pl.program_id(1) == 0)
            def _():
                for k in zero_per_seq:
                    outs[k][...] = jnp.zeros_like(outs[k])
        if zero_once:
            @pl.when(jnp.logical_and(pl.program_id(0) == 0, pl.program_id(1) == 0))
            def _():
                for k in zero_once:
                    outs[k][...] = jnp.zeros_like(outs[k])

        def part_of(ref, p):
            tiled = len(ref.shape) == 2 and ref.shape[0] == tm
            return _RowsOf(ref, p * rows, rows) if tiled and parts > 1 else ref

        accs = []
        for p in range(parts):
            a_p, ex_p, outs_p = part_of(a_ref, p), [part_of(r, p) for r in ex], [part_of(r, p) for r in outs]
            if b_chunks == 1:
                accs.append(_dot(a_p[...] if pro is None else pro(a_p, ex_p, outs_p), b_ref[...], dims))
            else:
                acc = _dot(a_p[...][:, 0:kc], b_ref[0], NT_DIMS)
                for k in range(1, b_chunks):
                    acc = acc + _dot(a_p[...][:, k * kc:(k + 1) * kc], b_ref[k], NT_DIMS)
                accs.append(acc)
        for p in range(parts):
            epi(accs[p], [part_of(r, p) for r in ex], [part_of(r, p) for r in outs])

    b_spec = pl.BlockSpec(b.shape, lambda bb, i: (0,) * b.ndim)
    return _call(
        body, (a, b, *extra), name=name, grid=(bsz, seq // tm), in_specs=[_tok_spec(tm, k_total), b_spec, *extra_specs],
        out_specs=out_specs, out_shape=out_shape, sem=("arbitrary", "arbitrary"), comms=comms, vmem_limit=vmem_limit)


def _in_proj_fused(x, w, sc, sh, w_in_t, tables, comms=()):
    tm = 512
    bsz, seq, _ = x.shape
    half = ROPE_DIM // 2
    heads_per_slab = LANE // ATT_HEAD_DIM

    def pro(x_ref, ex, outs):
        y, _, _ = _rms_fwd(x_ref[...], ex[0][...])
        h = (y * (1.0 + ex[1][...]) + ex[2][...]).astype(BF16)
        outs[0][...] = h
        return h

    def epi(acc, ex, outs):
        c, u, d = ex[3][...], ex[4][...], ex[5][...]
        _, rec_ref, q_ref, k_ref, v_ref = outs
        for k in range(HG_SLABS):
            rec_ref[k] = acc[:, ATT_COLS + k * HG_WIDTH:ATT_COLS + (k + 1) * HG_WIDTH]

        def rope(z):
            return (z * c + pltpu.roll(z, half, 1) * u + pltpu.roll(z, LANE - half, 1) * d).astype(BF16)

        for s in range(ATT_WIDTH // LANE):
            slab = rope(acc[:, s * LANE:(s + 1) * LANE])
            for part in range(heads_per_slab):
                g, hh = divmod(s * heads_per_slab + part, ATT_GROUP)
                piece = slab[:, part * ATT_HEAD_DIM:(part + 1) * ATT_HEAD_DIM]
                for blk in range(tm // WINDOW):
                    q_ref[blk, g, hh * WINDOW:(hh + 1) * WINDOW, :] = piece[blk * WINDOW:(blk + 1) * WINDOW]
        rk = rope(acc[:, ATT_WIDTH:ATT_WIDTH + LANE])
        vv = acc[:, ATT_WIDTH + LANE:ATT_COLS].astype(BF16)
        for g in range(ATT_KV_HEADS):
            k_ref[g] = rk[:, g * ATT_HEAD_DIM:(g + 1) * ATT_HEAD_DIM]
            v_ref[g] = vv[:, g * ATT_HEAD_DIM:(g + 1) * ATT_HEAD_DIM]

    tab = pl.BlockSpec((tm, LANE), lambda b, i: (i, 0))
    kv_spec = pl.BlockSpec((None, ATT_KV_HEADS, tm, ATT_HEAD_DIM), lambda b, i: (b, 0, i, 0))
    kv_shape = _sds((bsz, ATT_KV_HEADS, seq, ATT_HEAD_DIM), BF16)
    q_spec = pl.BlockSpec((None, tm // WINDOW, ATT_KV_HEADS, GROUP_ROWS, ATT_HEAD_DIM), lambda b, i: (b, i, 0, 0, 0))
    return _mm_rows(x, w_in_t, name="in_proj", tm=tm, extra=(w, sc, sh, *tables),
                    extra_specs=[_vec_spec(), _row_spec(), _row_spec(), tab, tab, tab],
                    out_specs=[_tok_spec(tm), pl.BlockSpec((None, HG_SLABS, tm, HG_WIDTH), lambda b, i: (b, 0, i, 0)), q_spec,
                               kv_spec, kv_spec],
                    out_shape=[_sds(x.shape, BF16), _sds((bsz, HG_SLABS, seq, HG_WIDTH), F32),
                               _sds((bsz, seq // WINDOW, ATT_KV_HEADS, GROUP_ROWS, ATT_HEAD_DIM), BF16), kv_shape, kv_shape],
                    pro=pro, epi=epi, trans_b=True, comms=comms)


def _rope_tables(seq):
    half = ROPE_DIM // 2
    inv_freq = ROPE_THETA ** (-jnp.arange(0, ROPE_DIM, 2, dtype=F32) / ROPE_DIM)
    ang = jnp.arange(seq, dtype=F32)[:, None] * inv_freq[None, :]
    cos, sin = jnp.cos(ang), jnp.sin(ang)
    rest = ATT_HEAD_DIM - ROPE_DIM
    ones, zeros, zh = jnp.ones((seq, rest), F32), jnp.zeros((seq, rest), F32), jnp.zeros((seq, half), F32)
    reps = LANE // ATT_HEAD_DIM
    t_cos = jnp.tile(jnp.concatenate([cos, cos, ones], axis=1), (1, reps))
    t_up = jnp.tile(jnp.concatenate([zh, sin, zeros], axis=1), (1, reps))
    t_dn = jnp.tile(jnp.concatenate([-sin, zh, zeros], axis=1), (1, reps))
    return t_cos, t_up, t_dn


GROUP_ROWS = ATT_GROUP * WINDOW


ATT_BPS = 2


MASKED = -1e30


def _band_biases():
    row = jnp.arange(GROUP_ROWS)[:, None] % WINDOW
    col = jnp.arange(2 * WINDOW)[None, :]
    own = jnp.logical_and(col >= WINDOW, col - WINDOW <= row)
    before = jnp.logical_and(col < WINDOW, col > row)
    return (jnp.where(jnp.logical_or(own, before), 0.0, MASKED).astype(F32), jnp.where(own, 0.0, MASKED).astype(F32))


def _band_bias(full_ref, first_ref, has_prev):
    return full_ref[...] if has_prev is True else jnp.where(has_prev, full_ref[...], first_ref[...])


def _sink_column(sink_ref, g):
    head = lax.broadcasted_iota(jnp.int32, (GROUP_ROWS, 1), 0) // WINDOW
    col = jnp.full((GROUP_ROWS, 1), sink_ref[0, g * ATT_GROUP], F32)
    for hh in range(1, ATT_GROUP):
        col = jnp.where(head == hh, sink_ref[0, g * ATT_GROUP + hh], col)
    return col


def _sink_row(sink_ref, g):
    return jnp.concatenate([jnp.full((1, WINDOW), sink_ref[0, g * ATT_GROUP + hh], F32) for hh in range(ATT_GROUP)], axis=1)


def _bias_spec(transposed=False):
    shape = (2 * WINDOW, GROUP_ROWS) if transposed else (GROUP_ROWS, 2 * WINDOW)
    return pl.BlockSpec(shape, lambda b, i: (0, 0))


def _attn_specs():
    q_spec = pl.BlockSpec((None, ATT_BPS, ATT_KV_HEADS, GROUP_ROWS, ATT_HEAD_DIM), lambda b, i: (b, i, 0, 0, 0))
    kv_cur = pl.BlockSpec((None, ATT_KV_HEADS, ATT_BPS * WINDOW, ATT_HEAD_DIM), lambda b, i: (b, 0, i, 0))
    kv_prev = pl.BlockSpec((None, ATT_KV_HEADS, WINDOW, ATT_HEAD_DIM), lambda b, i: (b, 0, jnp.maximum(ATT_BPS * i - 1, 0), 0))
    return q_spec, kv_cur, kv_prev


def _band(prev_ref, cur_ref, g, blk):
    own = cur_ref[g, blk * WINDOW:(blk + 1) * WINDOW]
    before = prev_ref[g] if blk == 0 else cur_ref[g, (blk - 1) * WINDOW:blk * WINDOW]
    return jnp.concatenate([before, own], axis=0)


def _attn_fwd(qh, kh, vh, sinks, w_norm, biases, comms=()):
    bsz, nblk = qh.shape[0], qh.shape[1]
    seq = nblk * WINDOW
    rows = ATT_BPS * WINDOW

    def body(sink_ref, q_ref, kc_ref, kp_ref, vc_ref, vp_ref, w_ref, full_ref, first_ref, raw_ref, an_ref, l_ref):
        l_ref[...] = jnp.zeros_like(l_ref)
        def block(blk):
            bias = _band_bias(full_ref, first_ref, True if blk else pl.program_id(1) > 0)
            groups = range(ATT_KV_HEADS)
            keys, vals = [_band(kp_ref, kc_ref, g, blk) for g in groups], [_band(vp_ref, vc_ref, g, blk) for g in groups]
            sink = [_sink_column(sink_ref, g) for g in groups]
            s = [_dot(q_ref[blk, g], keys[g], NT_DIMS) * ATT_SCALE + bias for g in groups]
            yield
            m = [jnp.maximum(jnp.max(s[g], axis=-1, keepdims=True), sink[g]) for g in groups]
            p = [jnp.exp(s[g] - m[g]) for g in groups]
            den = [jnp.sum(p[g], axis=-1, keepdims=True) + jnp.exp(sink[g] - m[g]) for g in groups]
            yield
            o = [_dot(p[g] / den[g], vals[g]) for g in groups]
            yield
            lse = [m[g] + jnp.log(den[g]) for g in groups]
            tok = slice(blk * WINDOW, (blk + 1) * WINDOW)
            for g in groups:
                for hh in range(ATT_GROUP):
                    h = g * ATT_GROUP + hh
                    raw_ref[tok, h * ATT_HEAD_DIM:(h + 1) * ATT_HEAD_DIM] = o[g][hh * WINDOW:(hh + 1) * WINDOW]
                    l_ref[tok, h:h + 1] = lse[g][hh * WINDOW:(hh + 1) * WINDOW]

        _in_step(block(blk) for blk in range(ATT_BPS))
        y, _, _ = _rms_fwd(raw_ref[...], w_ref[...])
        an_ref[...] = y.astype(BF16)

    cur = lambda width: pl.BlockSpec((None, rows, width), lambda b, i: (b, i, 0))
    q_spec, kv_cur, kv_prev = _attn_specs()
    return _call(
        body, (sinks, qh, kh, kh, vh, vh, w_norm, *biases), name="attn_fwd", grid=(bsz, nblk // ATT_BPS),
        in_specs=[pl.BlockSpec(memory_space=pltpu.SMEM), q_spec, kv_cur, kv_prev, kv_cur, kv_prev, _vec_spec(ATT_WIDTH),
                  _bias_spec(), _bias_spec()],
        out_specs=[cur(ATT_WIDTH), cur(ATT_WIDTH), cur(LANE)],
        out_shape=[_sds((bsz, seq, ATT_WIDTH), F32), _sds((bsz, seq, MIX_WIDTH), BF16), _sds((bsz, seq, LANE), F32)],
        sem=("parallel", "parallel"), comms=comms)


HG_Q0 = ATT_COLS // LANE
HG_F0 = HG_Q0 + HG_HEADS
HG_I0 = HG_F0 + HG_HEADS
HG_G0 = HG_I0 + HG_HEADS
HG_SLABS = 4
HG_Q, HG_F, HG_I, HG_G = range(HG_SLABS)
HG_TOK = 256
HG_NCH = HG_TOK // HG_CHUNK
HG_HPS = 2
HG_FWD_BLOCKS = 2
HG_BWD_BLOCKS = 2


def _block_masks():
    row = jnp.arange(HG_TOK)[:, None]
    col = jnp.arange(HG_TOK)[None, :]
    same = (row // HG_CHUNK) == (col // HG_CHUNK)
    return jnp.logical_and(same, col <= row).astype(F32), jnp.logical_and(same, col >= row).astype(F32)


def _row_in_chunk():
    return lax.broadcasted_iota(jnp.int32, (HG_TOK, LANE), 0) % HG_CHUNK


def _chunk_cumsum(x, reverse=False):
    ric = _row_in_chunk()
    shift = 1
    while shift < HG_CHUNK:
        if reverse:
            x = x + jnp.where(ric < HG_CHUNK - shift, pltpu.roll(x, HG_TOK - shift, 0), 0.0)
        else:
            x = x + jnp.where(ric >= shift, pltpu.roll(x, shift, 0), 0.0)
        shift *= 2
    return x


def _chunk_rows(rows):
    stacked = jnp.concatenate([r[None] for r in rows], axis=0)
    return jnp.broadcast_to(stacked, (HG_NCH, HG_CHUNK, LANE)).reshape(HG_TOK, LANE)


def _chunk_slices(x):
    return [x[j * HG_CHUNK:(j + 1) * HG_CHUNK] for j in range(HG_NCH)]


def _in_step(stages):
    stages = list(stages)
    while stages:
        stages = [g for g in stages if next(g, stages) is not stages]


def _hgrn_common(tbl, hf, hq):
    lb = _sigmoid(tbl[1:2] - tbl[0:1])
    sig = _sigmoid(hf)
    f = lb + (1.0 - lb) * sig
    sq = _sigmoid(hq)
    q, k = hq * sq, 1.0 - f
    b = _chunk_cumsum(jnp.log(f))
    last = [b[(j + 1) * HG_CHUNK - 1:(j + 1) * HG_CHUNK] for j in range(HG_NCH)]
    bl = _chunk_rows(last)
    e_b, e_nb, e_rem = jnp.exp(b), jnp.exp(-b), jnp.exp(bl - b)
    e_last = [jnp.exp(r) for r in last]
    return dict(lb=lb, sig=sig, f=f, sq=sq, q=q, k=k, e_b=e_b, e_nb=e_nb, e_rem=e_rem, e_last=e_last,
                qd=q * e_b, kd=k * e_nb, ku=k * e_rem)


def _hgrn_fwd(proj, lb_table, norm_w, mix_in, masks, comms=()):
    bsz, _, seq, _ = proj.shape
    nstep = seq // HG_TOK

    def body(tbl_ref, nw_ref, p_ref, mix_ref, lower_ref, o_ref, rec_ref, st_ref, s_scr):
        @pl.when(pl.program_id(2) == 0)
        def _():
            s_scr[...] = jnp.zeros_like(s_scr)

        lower = lower_ref[...]

        def head(hp, sub):
            ls = slice(hp * LANE, (hp + 1) * LANE)
            rows = slice(sub * HG_TOK, (sub + 1) * HG_TOK)
            v, hg = p_ref[HG_I, rows, ls], p_ref[HG_G, rows, ls]
            t = _hgrn_common(tbl_ref[:, ls], p_ref[HG_F, rows, ls], p_ref[HG_Q, rows, ls])
            yield
            a = _dot(t["qd"], t["kd"], NT_DIMS) * lower
            o_intra = _dot(a, v)
            yield
            v_c, ku_c, qd_c = [_chunk_slices(z.astype(BF16)) for z in (v, t["ku"], t["qd"])]
            updates = [_dot(v_c[j], ku_c[j], TN_DIMS) for j in range(HG_NCH)]
            yield
            st = s_scr[hp]
            states = []
            for j in range(HG_NCH):
                states.append(st)
                st = st * t["e_last"][j] + updates[j]
            s_scr[hp] = st
            yield
            o = o_intra + jnp.concatenate([_dot(qd_c[j], states[j], NT_DIMS) for j in range(HG_NCH)], axis=0)
            yield
            st_ref[hp, sub] = states[0]
            o_ref[rows, ls] = o
            y, _, _ = _rms_fwd(o, nw_ref[...])
            rec_ref[rows, ls] = (y * (hg * _sigmoid(hg))).astype(BF16)

        for sub in range(HG_FWD_BLOCKS):
            _in_step(head(hp, sub) for hp in range(HG_HPS))

    width = HG_HPS * LANE
    tok = HG_FWD_BLOCKS * HG_TOK
    head_out = pl.BlockSpec((None, tok, width), lambda b, h, t: (b, t, h))
    mix_out = pl.BlockSpec((None, tok, width), lambda b, h, t: (b, t, ATT_WIDTH // width + h))
    return _call(
        body, (lb_table, norm_w, proj, mix_in, masks[0]), name="hgrn_fwd",
        grid=(bsz, HG_HEADS // HG_HPS, nstep // HG_FWD_BLOCKS),
        in_specs=[pl.BlockSpec((2, width), lambda b, h, t: (0, h)), pl.BlockSpec((1, LANE), lambda b, h, t: (0, 0)),
                  pl.BlockSpec((None, HG_SLABS, tok, width), lambda b, h, t: (b, 0, t, h)), pl.BlockSpec(memory_space=pl.ANY),
                  pl.BlockSpec((HG_TOK, HG_TOK), lambda b, h, t: (0, 0))],
        out_specs=[head_out, mix_out,
                   pl.BlockSpec((None, HG_HPS, HG_FWD_BLOCKS, LANE, LANE), lambda b, h, t: (b, h, t, 0, 0))],
        out_shape=[_sds((bsz, seq, HG_WIDTH), F32), _sds(mix_in.shape, BF16),
                   _sds((bsz, HG_HEADS, nstep, LANE, LANE), F32)],
        scratch_shapes=[pltpu.VMEM((HG_HPS, LANE, LANE), F32)],
        sem=("parallel", "parallel", "arbitrary"), comms=comms, aliases={3: 1})


def _out_proj_fused(cat, w_out, x, post_w, g1, pre_w, sc2, sh2):
    tm = 512

    def epi(mix, ex, outs):
        x_ref, pw_ref, g1_ref, w2_ref, sc_ref, sh_ref = ex
        outs[0][...] = mix
        n1, _, _ = _rms_fwd(mix, pw_ref[...])
        x1 = x_ref[...] + g1_ref[...] * n1
        outs[1][...] = x1
        y2, _, _ = _rms_fwd(x1, w2_ref[...])
        outs[2][...] = (y2 * (1.0 + sc_ref[...]) + sh_ref[...]).astype(BF16)

    return _mm_rows(cat, w_out, name="out_proj", tm=tm, extra=(x, post_w, g1, pre_w, sc2, sh2),
                    extra_specs=[_tok_spec(tm), _vec_spec(), _row_spec(), _vec_spec(), _row_spec(), _row_spec()],
                    out_specs=[_tok_spec(tm), _tok_spec(tm), _tok_spec(tm)],
                    out_shape=[_sds(x.shape, F32), _sds(x.shape, F32), _sds(x.shape, BF16)], epi=epi)


def _acc_out(ref, first, value):
    @pl.when(first)
    def _():
        ref[...] = value

    @pl.when(jnp.logical_not(first))
    def _():
        ref[...] += value


def _down_proj_fused(r, w_down, x1, post_w, g2, target):
    tm = 512
    bsz = x1.shape[0]

    def pro(r_ref, ex, outs):
        rv = r_ref[...]
        return rv * rv

    def epi(down, ex, outs):
        x1_ref, w_ref, g2_ref, t_ref = ex
        loss_ref, dy_ref, dd_ref, dg2_ref, dw_ref = outs
        w, g2v = w_ref[...], g2_ref[...]
        gain = g2v * w
        dh, rstd = _rms_hat(down)
        err = x1_ref[...] + dh * gain - t_ref[...]
        part = (0.5 / D_MODEL) * jnp.sum(jnp.sum(err * err, axis=-1, keepdims=True), axis=0, keepdims=True)
        loss_ref[...] += jnp.broadcast_to(part, (1, LANE))
        dy = err * (1.0 / D_MODEL)
        dy_ref[...] = dy
        dd, per_col = _rms_bwd_gain(dy, gain, dh, rstd)
        dd_ref[...] = dd.astype(BF16)
        dg2_ref[...] += per_col * w
        dw_ref[...] += per_col * g2v

    return _mm_rows(r, w_down, name="down_proj", tm=tm, extra=(x1, post_w, g2, target),
                    extra_specs=[_tok_spec(tm), _vec_spec(), _row_spec(), _tok_spec(tm)],
                    out_specs=[_vec_spec(LANE), _tok_spec(tm), _tok_spec(tm), _row_spec(), _vec_spec()],
                    out_shape=[_sds((1, LANE), F32), _sds(x1.shape, F32), _sds(x1.shape, BF16), _sds((bsz, 1, D_MODEL), F32),
                               _sds((1, D_MODEL), F32)], pro=pro, epi=epi, parts=2, zero_per_seq=(3,), zero_once=(0, 4),
                    vmem_limit=VMEM_LIMIT_BIG)


def _up_bwd_fused(dpre, w_up4, dy, x1, mix, pre_w, sc2, post_w, g1, comms=()):
    tm = 512
    bsz = x1.shape[0]

    def epi(dh2v, ex, outs):
        dy_ref, x1_ref, mix_ref, w2_ref, sc_ref, pw_ref, g1_ref = ex
        dx1_ref, dmix_ref, dsc_ref, dsh_ref, dg1_ref, dw2_ref, dpw_ref = outs
        w2, pw, g1v = w2_ref[...], pw_ref[...], g1_ref[...]
        mod2 = 1.0 + sc_ref[...]
        xh2, rstd2 = _rms_hat(x1_ref[...])
        dsh_ref[...] += _colsum(dh2v)
        dx1n, per_col2 = _rms_bwd_gain(dh2v, mod2 * w2, xh2, rstd2)
        dsc_ref[...] += per_col2 * w2
        dw2_ref[...] += per_col2 * mod2
        dx1 = dy_ref[...] + dx1n
        dx1_ref[...] = dx1
        mh, rstd1 = _rms_hat(mix_ref[...])
        dmix, per_col1 = _rms_bwd_gain(dx1, g1v * pw, mh, rstd1)
        dmix_ref[...] = dmix.astype(BF16)
        dg1_ref[...] += per_col1 * pw
        dpw_ref[...] += per_col1 * g1v

    row_shape = _sds((bsz, 1, D_MODEL), F32)
    vec_shape = _sds((1, D_MODEL), F32)
    return _mm_rows(dpre, w_up4, name="up_bwd", tm=tm, extra=(dy, x1, mix, pre_w, sc2, post_w, g1),
                    extra_specs=[_tok_spec(tm), _tok_spec(tm), _tok_spec(tm), _vec_spec(), _row_spec(), _vec_spec(), _row_spec()],
                    out_specs=[_tok_spec(tm), _tok_spec(tm), _row_spec(), _row_spec(), _row_spec(), _vec_spec(), _vec_spec()],
                    out_shape=[_sds(x1.shape, F32), _sds(x1.shape, BF16), row_shape, row_shape, row_shape, vec_shape, vec_shape],
                    epi=epi, b_chunks=w_up4.shape[0], comms=comms, parts=2, zero_per_seq=(2, 3, 4), zero_once=(5, 6),
                    vmem_limit=VMEM_LIMIT_BIG)


def _norm1_bwd(dh1, dx1, x, pre_w, sc1, tm=512, comms=()):
    bsz, seq, _ = x.shape

    def body(dh_ref, dx1_ref, x_ref, w_ref, sc_ref, gx_ref, dsc_ref, dsh_ref, dw_ref):
        b, i = pl.program_id(0), pl.program_id(1)
        w = w_ref[...]
        dh = dh_ref[...]
        mod = 1.0 + sc_ref[...]
        xh, rstd = _rms_hat(x_ref[...])
        dx, per_col = _rms_bwd_gain(dh, mod * w, xh, rstd)
        _acc_out(dsh_ref, i == 0, _colsum(dh))
        _acc_out(dsc_ref, i == 0, per_col * w)
        _acc_out(dw_ref, jnp.logical_and(b == 0, i == 0), per_col * mod)
        gx_ref[...] = dx1_ref[...] + dx

    row_shape = _sds((bsz, 1, D_MODEL), F32)
    return _call(
        body, (dh1, dx1, x, pre_w, sc1), name="norm1_bwd", grid=(bsz, seq // tm),
        in_specs=[_tok_spec(tm), _tok_spec(tm), _tok_spec(tm), _vec_spec(), _row_spec()],
        out_specs=[_tok_spec(tm), _row_spec(), _row_spec(), _vec_spec()],
        out_shape=[_sds(x.shape, F32), row_shape, row_shape, _sds((1, D_MODEL), F32)],
        sem=("arbitrary", "arbitrary"), comms=comms)


def _hgrn_bwd(dcat, proj, o_raw, states, lb_table, norm_w, masks, comms=()):
    bsz, _, seq, _ = proj.shape
    tok = HG_BWD_BLOCKS * HG_TOK
    nstep = seq // tok
    rec0 = ATT_WIDTH // LANE
    width = HG_HPS * LANE
    slabs = (HG_Q0, HG_F0, HG_I0, HG_G0)
    n_steps = (HG_HEADS // HG_HPS) * bsz * nstep
    assert n_steps >= 2

    def body(tbl_ref, nw_ref, dr_ref, p_ref, o_ref, st_ref, lower_ref, upper_ref,
             dproj_ref, dlb_ref, dnw_ref, ds_scr, grad_buf, grad_sem):
        h, b, t = pl.program_id(0), pl.program_id(1), pl.program_id(2)
        step = (h * bsz + b) * nstep + t
        slot = step % 2
        dq_k, df_k, di_k, dg_k = range(4)

        def grad_copies(of_step):
            hh, bb, tt = of_step // (bsz * nstep), (of_step // nstep) % bsz, of_step % nstep
            rows = pl.ds(pl.multiple_of((nstep - 1 - tt) * tok, tok), tok)
            return [pltpu.make_async_copy(
                grad_buf.at[of_step % 2, k],
                dproj_ref.at[bb, rows, pl.ds(pl.multiple_of(slabs[k] * LANE + hh * width, width), width)],
                grad_sem.at[of_step % 2, k]) for k in range(4)]

        @pl.when(step >= 2)
        def _():
            for cp in grad_copies(step - 2):
                cp.wait()

        @pl.when(t == 0)
        def _():
            ds_scr[...] = jnp.zeros_like(ds_scr)

        lower, upper = lower_ref[...], upper_ref[...]
        dlb_parts, dnw_parts = [None] * HG_HPS, [None] * HG_HPS

        def head(hp, sub):
            ls = slice(hp * LANE, (hp + 1) * LANE)
            rows = slice(sub * HG_TOK, (sub + 1) * HG_TOK)
            hq, v, hg = p_ref[HG_Q, rows, ls], p_ref[HG_I, rows, ls], p_ref[HG_G, rows, ls]
            nw = nw_ref[...]
            c = _hgrn_common(tbl_ref[:, ls], p_ref[HG_F, rows, ls], hq)
            qd, kd, ku = c["qd"], c["kd"], c["ku"]
            yield
            y, on, rstd = _rms_fwd(o_ref[rows, ls], nw)
            sg = _sigmoid(hg)
            dr = dr_ref[rows, ls]
            grad_buf[slot, dg_k, rows, ls] = (dr * y * (sg * (1.0 + hg * (1.0 - sg)))).astype(BF16)
            do, dnw_rows = _rms_bwd(dr * (hg * sg), on, rstd, nw)
            yield
            at = _dot(kd, qd, NT_DIMS) * upper
            da = _dot(do, v, NT_DIMS) * lower
            dat = _dot(v, do, NT_DIMS) * upper
            yield
            dv = _dot(at, do)
            dqd = _dot(da, kd)
            dkd = _dot(dat, qd)
            yield
            do_c, qd_c, v_c, ku_c = [_chunk_slices(z.astype(BF16)) for z in (do, qd, v, ku)]
            outer = [_dot(do_c[j], qd_c[j], TN_DIMS) for j in range(HG_NCH)]
            yield
            ds = ds_scr[hp]
            ds_after = [None] * HG_NCH
            for j in reversed(range(HG_NCH)):
                ds_after[j] = ds
                ds = outer[j] + ds * c["e_last"][j]
            ds_scr[hp] = ds
            yield
            updates = [_dot(v_c[j], ku_c[j], TN_DIMS) for j in range(HG_NCH)]
            yield
            states = [st_ref[hp, sub]]
            for j in range(HG_NCH - 1):
                states.append(states[j] * c["e_last"][j] + updates[j])
            dv = dv + jnp.concatenate([_dot(ku_c[j], ds_after[j], NT_DIMS) for j in range(HG_NCH)], axis=0)
            dqd = dqd + jnp.concatenate([_dot(do_c[j], states[j]) for j in range(HG_NCH)], axis=0)
            dku = jnp.concatenate([_dot(v_c[j], ds_after[j]) for j in range(HG_NCH)], axis=0)
            yield
            dku_ku = dku * ku
            dbl = [_colsum(states[j] * ds_after[j]) * c["e_last"][j] + _colsum(dku_ku[j * HG_CHUNK:(j + 1) * HG_CHUNK])
                   for j in range(HG_NCH)]
            dk = dkd * c["e_nb"] + dku * c["e_rem"]
            db = dqd * qd - dkd * kd - dku_ku + jnp.where(_row_in_chunk() == HG_CHUNK - 1, _chunk_rows(dbl), 0.0)
            dfv = _chunk_cumsum(db, reverse=True) / c["f"] - dk
            sig, sq = c["sig"], c["sq"]
            grad_buf[slot, df_k, rows, ls] = (dfv * (1.0 - c["lb"]) * sig * (1.0 - sig)).astype(BF16)
            grad_buf[slot, dq_k, rows, ls] = (dqd * c["e_b"] * (sq * (1.0 + hq * (1.0 - sq)))).astype(BF16)
            grad_buf[slot, di_k, rows, ls] = dv.astype(BF16)
            d_lb, d_nw = _colsum(dfv * (1.0 - sig)), _colsum(dnw_rows)
            dlb_parts[hp] = d_lb if dlb_parts[hp] is None else dlb_parts[hp] + d_lb
            dnw_parts[hp] = d_nw if dnw_parts[hp] is None else dnw_parts[hp] + d_nw

        for sub in reversed(range(HG_BWD_BLOCKS)):
            _in_step(head(hp, sub) for hp in range(HG_HPS))
        _acc_out(dlb_ref, jnp.logical_and(b == 0, t == 0), jnp.concatenate(dlb_parts, axis=1))
        _acc_out(dnw_ref, jnp.logical_and(h == 0, jnp.logical_and(b == 0, t == 0)), sum(dnw_parts[1:], dnw_parts[0]))
        for cp in grad_copies(step):
            cp.start()

        @pl.when(step == n_steps - 1)
        def _():
            for cp in grad_copies(step - 1) + grad_copies(step):
                cp.wait()

    rev = lambda t: nstep - 1 - t
    slab = lambda first: pl.BlockSpec((None, tok, width), lambda h, b, t: (b, rev(t), first // HG_HPS + h))
    head = pl.BlockSpec((None, tok, width), lambda h, b, t: (b, rev(t), h))
    return _call(
        body, (lb_table, norm_w, dcat, proj, o_raw, states, *masks), name="hgrn_bwd",
        grid=(HG_HEADS // HG_HPS, bsz, nstep),
        in_specs=[pl.BlockSpec((2, width), lambda h, b, t: (0, h)), pl.BlockSpec((1, LANE), lambda h, b, t: (0, 0)),
                  slab(rec0), pl.BlockSpec((None, HG_SLABS, tok, width), lambda h, b, t: (b, 0, rev(t), h)), head,
                  pl.BlockSpec((None, HG_HPS, HG_BWD_BLOCKS, LANE, LANE), lambda h, b, t: (b, h, rev(t), 0, 0)),
                  pl.BlockSpec((HG_TOK, HG_TOK), lambda h, b, t: (0, 0)), pl.BlockSpec((HG_TOK, HG_TOK), lambda h, b, t: (0, 0))],
        out_specs=[pl.BlockSpec(memory_space=pl.ANY), pl.BlockSpec((1, width), lambda h, b, t: (0, h)),
                   pl.BlockSpec((1, LANE), lambda h, b, t: (0, 0))],
        out_shape=[_sds((bsz, seq, IN_COLS), BF16), _sds((1, HG_WIDTH), F32), _sds((1, LANE), F32)],
        scratch_shapes=[pltpu.VMEM((HG_HPS, LANE, LANE), F32), pltpu.VMEM((2, 4, tok, width), BF16),
                        pltpu.SemaphoreType.DMA((2, 4))],
        sem=("arbitrary", "arbitrary", "arbitrary"), comms=comms)


def _attn_bwd(dcat, raw, w_norm, qh, kh, vh, lse, sinks, tables, biases, dproj, comms=()):
    bsz, nblk = qh.shape[0], qh.shape[1]
    seq = nblk * WINDOW
    nstep = nblk // ATT_BPS
    half = ROPE_DIM // 2

    def body(sink_ref, da_ref, raw_ref, w_ref, q_ref, kc_ref, kp_ref, vc_ref, vp_ref, l_ref, c_ref, u_ref, d_ref,
             full_ref, first_ref, dproj_ref, o_ref, dw_ref, dsink_ref, carry_k, carry_v):
        b, i = pl.program_id(0), pl.program_id(1)
        first = jnp.logical_and(b == 0, i == 0)

        @pl.when(i == 0)
        def _():
            carry_k[...] = jnp.zeros_like(carry_k)
            carry_v[...] = jnp.zeros_like(carry_v)

        w = w_ref[...]
        _, on, rstd = _rms_fwd(raw_ref[...], w)
        do_step, dw_rows = _rms_bwd(da_ref[...], on, rstd, w)
        _acc_out(dw_ref, first, _colsum(dw_rows))
        lane8 = lax.broadcasted_iota(jnp.int32, (1, ATT_Q_HEADS), 1)
        dsink = jnp.zeros((1, ATT_Q_HEADS), F32)
        head_cols = jnp.where(lax.broadcasted_iota(jnp.int32, (2 * ATT_Q_HEADS, ATT_WIDTH), 1) // ATT_HEAD_DIM
                              == lax.broadcasted_iota(jnp.int32, (2 * ATT_Q_HEADS, ATT_WIDTH), 0), 1.0, 0.0)
        from_next_k, from_next_v = carry_k[...], carry_v[...]
        for blk in reversed(range(ATT_BPS)):
            tok = slice(blk * WINDOW, (blk + 1) * WINDOW)
            bias = _band_bias(full_ref, first_ref, True if blk else i < nstep - 1)
            do_all = do_step[tok]
            c, u, d = c_ref[tok, :], u_ref[tok, :], d_ref[tok, :]
            lse_t = l_ref[tok, :].T
            prod = do_all * raw_ref[tok, :]
            prod_hi = prod.astype(BF16)
            prod_lo = prod - prod_hi.astype(F32)
            dsum_t = _dot(head_cols, prod_hi, NT_DIMS) + _dot(head_cols, prod_lo, NT_DIMS)

            def unrope(g):
                return (g * c + pltpu.roll(g * u, LANE - half, 1) + pltpu.roll(g * d, half, 1)).astype(BF16)

            groups = range(ATT_KV_HEADS)
            group_row = lambda z, g: jnp.concatenate(
                [z[g * ATT_GROUP + hh:g * ATT_GROUP + hh + 1, :] for hh in range(ATT_GROUP)], axis=1)
            q = [q_ref[blk, g] for g in groups]
            keys, vals = [_band(kp_ref, kc_ref, g, blk) for g in groups], [_band(vp_ref, vc_ref, g, blk) for g in groups]
            do_g = [jnp.concatenate([do_all[:, (g * ATT_GROUP + hh) * ATT_HEAD_DIM:(g * ATT_GROUP + hh + 1) * ATT_HEAD_DIM]
                                     for hh in range(ATT_GROUP)], axis=0) for g in groups]
            dsum, lse_g = [group_row(dsum_t, g) for g in groups], [group_row(lse_t, g) for g in groups]
            s_t = [_dot(keys[g], q[g], NT_DIMS) for g in groups]
            dp_t = [_dot(vals[g], do_g[g], NT_DIMS) for g in groups]
            p_t = [jnp.exp(s_t[g] * ATT_SCALE + bias - lse_g[g]) for g in groups]
            ds_t = [p_t[g] * (dp_t[g] - dsum[g]) * ATT_SCALE for g in groups]
            dq_g = [_dot(ds_t[g], keys[g], TN_DIMS) for g in groups]
            dk_g = [_dot(ds_t[g], q[g]) for g in groups]
            dv_g = [_dot(p_t[g], do_g[g]) for g in groups]
            for g in groups:
                sink_part = jnp.exp(_sink_row(sink_ref, g) - lse_g[g]) * dsum[g]
                for hh in range(ATT_GROUP):
                    head_sum = jnp.sum(sink_part[:, hh * WINDOW:(hh + 1) * WINDOW], axis=1, keepdims=True)
                    dsink = dsink - jnp.where(lane8 == g * ATT_GROUP + hh, head_sum, 0.0)
            dq_parts = [dq_g[g][hh * WINDOW:(hh + 1) * WINDOW] for g in groups for hh in range(ATT_GROUP)]
            dk_before, dk_own = [z[:WINDOW] for z in dk_g], [z[WINDOW:] for z in dk_g]
            dv_before, dv_own = [z[:WINDOW] for z in dv_g], [z[WINDOW:] for z in dv_g]
            per_slab = LANE // ATT_HEAD_DIM
            for s in range(ATT_WIDTH // LANE):
                slab = jnp.concatenate(dq_parts[s * per_slab:(s + 1) * per_slab], axis=1)
                o_ref[tok, s * LANE:(s + 1) * LANE] = unrope(slab)
            o_ref[tok, ATT_WIDTH:ATT_WIDTH + LANE] = unrope(jnp.concatenate(dk_own, axis=1) + from_next_k)
            o_ref[tok, ATT_WIDTH + LANE:ATT_COLS] = (jnp.concatenate(dv_own, axis=1) + from_next_v).astype(BF16)
            from_next_k, from_next_v = jnp.concatenate(dk_before, axis=1), jnp.concatenate(dv_before, axis=1)
        carry_k[...] = from_next_k
        carry_v[...] = from_next_v
        _acc_out(dsink_ref, first, dsink)

    rows = ATT_BPS * WINDOW
    rev = lambda i: nstep - 1 - i
    cur = lambda width: pl.BlockSpec((None, rows, width), lambda b, i: (b, rev(i), 0))
    q_spec = pl.BlockSpec((None, ATT_BPS, ATT_KV_HEADS, GROUP_ROWS, ATT_HEAD_DIM), lambda b, i: (b, rev(i), 0, 0, 0))
    kv_cur = pl.BlockSpec((None, ATT_KV_HEADS, rows, ATT_HEAD_DIM), lambda b, i: (b, 0, rev(i), 0))
    kv_prev = pl.BlockSpec((None, ATT_KV_HEADS, WINDOW, ATT_HEAD_DIM), lambda b, i: (b, 0, jnp.maximum(ATT_BPS * rev(i) - 1, 0), 0))
    tab = pl.BlockSpec((rows, LANE), lambda b, i: (rev(i), 0))
    return _call(
        body, (sinks, dcat, raw, w_norm, qh, kh, kh, vh, vh, lse, *tables, *biases, dproj), name="attn_bwd", grid=(bsz, nstep),
        in_specs=[pl.BlockSpec(memory_space=pltpu.SMEM), cur(ATT_WIDTH), cur(ATT_WIDTH), _vec_spec(ATT_WIDTH), q_spec,
                  kv_cur, kv_prev, kv_cur, kv_prev, cur(LANE), tab, tab, tab, _bias_spec(True), _bias_spec(True),
                  pl.BlockSpec(memory_space=pl.ANY)],
        out_specs=[cur(ATT_COLS), _vec_spec(ATT_WIDTH), _vec_spec(ATT_Q_HEADS)],
        out_shape=[_sds(dproj.shape, BF16), _sds((1, ATT_WIDTH), F32), _sds((1, ATT_Q_HEADS), F32)],
        scratch_shapes=[pltpu.VMEM((WINDOW, LANE), F32), pltpu.VMEM((WINDOW, LANE), F32)],
        sem=("arbitrary", "arbitrary"), comms=comms, aliases={15: 0})


def _other_chips(x, y):
    return [(1 - x, y), (x, 1 - y), (1 - x, 1 - y)]


def _sem_pair(n):
    return [pltpu.SemaphoreType.DMA((n,)), pltpu.SemaphoreType.DMA((n,))]


def _plan_pair_forward(bufs):
    n = len(bufs)

    def copies(outs, sems):
        x, y, c = _mesh_pos()
        sends, lands = [], []
        for a in range(n):
            for j, chip in enumerate(_other_chips(x, y)):
                k = 3 * a + j
                slot = outs[a].at[4 * chip[0] + 2 * chip[1] + c]
                sends.append(pltpu.make_async_remote_copy(
                    src_ref=slot, dst_ref=slot, send_sem=sems[0].at[k], recv_sem=sems[1].at[k],
                    device_id=(x, y, 1 - c), device_id_type=MESH))
                theirs = outs[a].at[4 * chip[0] + 2 * chip[1] + 1 - c]
                lands.append(pltpu.make_async_remote_copy(
                    src_ref=theirs, dst_ref=theirs, send_sem=sems[0].at[k], recv_sem=sems[1].at[k],
                    device_id=(x, y, 1 - c), device_id_type=MESH))
        return sends, lands

    def start(ins, outs, sems):
        for cp in copies(outs, sems)[0]:
            cp.start()

    def finish(ins, outs, sems):
        sends, lands = copies(outs, sems)
        for cp in lands:
            cp.wait_recv()
        for cp in sends:
            cp.wait_send()

    return _Comm(list(bufs), [_sds(b.shape, b.dtype) for b in bufs], _sem_pair(3 * n), start, finish,
                 aliases=[(a, a) for a in range(n)])


def _plan_pair(arrays, other_half):
    n = len(arrays)
    per = N_CHIPS if other_half == "chip_major" else 1

    def copies(ins, outs, sems):
        x, y, c = _mesh_pos()
        out = []
        for a in range(n):
            for k in range(per):
                if other_half == "chip_major":
                    src, dst = ins[a].at[k, 1 - c], outs[a].at[k]
                else:
                    src, dst = (ins[a].at[1 - c] if other_half else ins[a]), outs[a]
                out.append(pltpu.make_async_remote_copy(
                    src_ref=src, dst_ref=dst, send_sem=sems[0].at[per * a + k], recv_sem=sems[1].at[per * a + k],
                    device_id=(x, y, 1 - c), device_id_type=MESH))
        return out

    def start(ins, outs, sems):
        for cp in copies(ins, outs, sems):
            cp.start()

    def finish(ins, outs, sems):
        for cp in copies(ins, outs, sems):
            cp.wait()

    if other_half == "chip_major":
        shapes = [_sds((a.shape[0],) + a.shape[2:], a.dtype) for a in arrays]
    else:
        shapes = [_sds(a.shape[1:] if other_half else a.shape, a.dtype) for a in arrays]
    return _Comm(list(arrays), shapes, _sem_pair(per * n), start, finish)


def _plan_chip_exchange(arrays):
    n = len(arrays)

    def copies(ins, outs, sems):
        x, y, c = _mesh_pos()
        sends, lands = [], []
        for a in range(n):
            for j, chip in enumerate(_other_chips(x, y)):
                k = 3 * a + j
                sends.append(pltpu.make_async_remote_copy(
                    src_ref=ins[a].at[2 * chip[0] + chip[1]], dst_ref=outs[a].at[2 * x + y], send_sem=sems[0].at[k],
                    recv_sem=sems[1].at[k], device_id=(*chip, c), device_id_type=MESH))
                slot = outs[a].at[2 * chip[0] + chip[1]]
                lands.append(pltpu.make_async_remote_copy(
                    src_ref=slot, dst_ref=slot, send_sem=sems[0].at[k], recv_sem=sems[1].at[k],
                    device_id=(*chip, c), device_id_type=MESH))
        return sends, lands

    def start(ins, outs, sems):
        for cp in copies(ins, outs, sems)[0]:
            cp.start()

    def finish(ins, outs, sems):
        sends, lands = copies(ins, outs, sems)
        for cp in lands:
            cp.wait_recv()
        for cp in sends:
            cp.wait_send()

    return _Comm(list(arrays), [_sds(a.shape, a.dtype) for a in arrays], _sem_pair(3 * n), start, finish)


SEM_SPEC = pl.BlockSpec(memory_space=pltpu.SEMAPHORE)
N_OTHER = N_CHIPS - 1


def _exchange_copies(s_ref, land_ref, sems):
    x, y, c = _mesh_pos()
    return [pltpu.make_async_remote_copy(
        src_ref=s_ref.at[2 * chip[0] + chip[1]], dst_ref=land_ref.at[2 * x + y], send_sem=sems[j], recv_sem=sems[N_OTHER + j],
        device_id=(*chip, c), device_id_type=MESH) for j, chip in enumerate(_other_chips(x, y))]


def _exchange_start(s, name):
    def body(s_ref, land_ref, *outs):
        sems, token = outs[:2 * N_OTHER], outs[-1]
        for cp in _exchange_copies(s_ref, land_ref, sems):
            cp.start()
        token[...] = jnp.zeros_like(token)

    hbm = pltpu.HBM(s.shape, s.dtype)
    res = pl.pallas_call(
        body, name=name,
        out_shape=(pltpu.SemaphoreType.DMA(()),) * (2 * N_OTHER) + (hbm, hbm, _sds((SUBLANES, LANE), F32)),
        in_specs=(HBM_SPEC, HBM_SPEC),
        out_specs=(SEM_SPEC,) * (2 * N_OTHER) + (HBM_SPEC, HBM_SPEC, pl.BlockSpec(memory_space=pltpu.VMEM)),
        input_output_aliases={0: 2 * N_OTHER, 1: 2 * N_OTHER + 1},
        compiler_params=pltpu.CompilerParams(has_side_effects=pltpu.SideEffectType.DATAFLOW_SIDE_EFFECTING),
    )(pltpu.with_memory_space_constraint(s, pltpu.HBM), pltpu.with_memory_space_constraint(lax.empty(s.shape, s.dtype), pltpu.HBM))
    return res[:2 * N_OTHER], res[2 * N_OTHER], res[2 * N_OTHER + 1], res[-1]


def _exchange_wait(sems, s_thru, land_thru, afters, name):
    def body(s_ref, land_ref, *rest):
        for cp in _exchange_copies(s_ref, land_ref, rest[:2 * N_OTHER]):
            cp.wait_send()
            cp.wait_recv()

    hbm = pltpu.HBM(s_thru.shape, s_thru.dtype)
    return pl.pallas_call(
        body, name=name, out_shape=(hbm, hbm),
        in_specs=(HBM_SPEC, HBM_SPEC) + (SEM_SPEC,) * (2 * N_OTHER) + (pl.BlockSpec(memory_space=pl.ANY),) * len(afters),
        out_specs=(HBM_SPEC, HBM_SPEC), input_output_aliases={0: 0, 1: 1},
        compiler_params=pltpu.CompilerParams(has_side_effects=pltpu.SideEffectType.DATAFLOW_SIDE_EFFECTING),
    )(s_thru, land_thru, *sems, *afters)


def _gather_copies(block_ref, buf_ref, sems):
    x, y, c = _mesh_pos()
    return [pltpu.make_async_remote_copy(
        src_ref=block_ref, dst_ref=buf_ref.at[4 * x + 2 * y + c], send_sem=sems[j], recv_sem=sems[N_OTHER + j],
        device_id=(*chip, c), device_id_type=MESH) for j, chip in enumerate(_other_chips(x, y))]


def _gather_start(blocks, bufs, afters, name):
    n = len(blocks)
    per = 2 * N_OTHER

    def body(*refs):
        ins, outs = refs[:2 * n], refs[2 * n + len(afters):]
        for a in range(n):
            for cp in _gather_copies(ins[a], ins[n + a], outs[a * per:(a + 1) * per]):
                cp.start()
        outs[-1][...] = jnp.zeros_like(outs[-1])

    hbm = [pltpu.HBM(z.shape, z.dtype) for z in list(blocks) + list(bufs)]
    res = pl.pallas_call(
        body, name=name,
        out_shape=(pltpu.SemaphoreType.DMA(()),) * (n * per) + tuple(hbm) + (_sds((SUBLANES, LANE), F32),),
        in_specs=(HBM_SPEC,) * (2 * n) + (pl.BlockSpec(memory_space=pl.ANY),) * len(afters),
        out_specs=(SEM_SPEC,) * (n * per) + (HBM_SPEC,) * (2 * n) + (pl.BlockSpec(memory_space=pltpu.VMEM),),
        input_output_aliases={k: n * per + k for k in range(2 * n)},
        compiler_params=pltpu.CompilerParams(has_side_effects=pltpu.SideEffectType.DATAFLOW_SIDE_EFFECTING),
    )(*[pltpu.with_memory_space_constraint(z, pltpu.HBM) for z in list(blocks) + list(bufs)], *afters)
    parts = [(res[a * per:(a + 1) * per], res[n * per + a], res[n * per + n + a]) for a in range(n)]
    return parts, res[-1]


def _gather_wait(part, afters, name):
    sems, block, buf = part

    def body(block_ref, buf_ref, *rest):
        for cp in _gather_copies(block_ref, buf_ref, rest[:2 * N_OTHER]):
            cp.wait_send()
            cp.wait_recv()

    return pl.pallas_call(
        body, name=name, out_shape=(pltpu.HBM(block.shape, block.dtype), pltpu.HBM(buf.shape, buf.dtype)),
        in_specs=(HBM_SPEC, HBM_SPEC) + (SEM_SPEC,) * (2 * N_OTHER) + (pl.BlockSpec(memory_space=pl.ANY),) * len(afters),
        out_specs=(HBM_SPEC, HBM_SPEC), input_output_aliases={0: 0, 1: 1},
        compiler_params=pltpu.CompilerParams(has_side_effects=pltpu.SideEffectType.DATAFLOW_SIDE_EFFECTING),
    )(block, buf, *sems, *afters)[1]


def _comm_only(comms, name):
    return _call(lambda: None, (), name=name, grid=(), in_specs=[], out_specs=[], out_shape=[], sem=(), comms=comms)[1]


def _allgather8(arrays, name):
    return _comm_only([_plan_allgather8(arrays)], name)[0]


def _plan_allgather8(arrays):
    n = len(arrays)

    def parts(ins, outs, sems):
        send_sems, recv_sems, local_sems = sems
        x, y, c = _mesh_pos()
        me, sibling = (x, y, c), (x, y, 1 - c)
        chips = _other_chips(x, y)

        def copy(a, k, block, to, src=None):
            dst = outs[a].at[4 * block[0] + 2 * block[1] + block[2]]
            return pltpu.make_async_remote_copy(
                src_ref=dst if src is None else src, dst_ref=dst, send_sem=send_sems.at[7 * a + k],
                recv_sem=recv_sems.at[7 * a + k], device_id=to, device_id_type=MESH)

        mine = [pltpu.make_async_copy(ins[a], outs[a].at[4 * x + 2 * y + c], local_sems.at[a]) for a in range(n)]
        first = []
        for a in range(n):
            first.append(copy(a, 0, me, sibling, src=ins[a]))
            first += [copy(a, 1 + j, me, (*chip, c), src=ins[a]) for j, chip in enumerate(chips)]
        return copy, mine, first, me, sibling, chips, c

    def start(ins, outs, sems):
        _, mine, first, *_ = parts(ins, outs, sems)
        for cp in mine + first:
            cp.start()

    def finish(ins, outs, sems):
        copy, mine, first, me, sibling, chips, c = parts(ins, outs, sems)
        passed = []
        for j, chip in enumerate(chips):
            for a in range(n):
                copy(a, 1 + j, (*chip, c), me).wait_recv()
                fwd = copy(a, 4 + j, (*chip, c), sibling)
                fwd.start()
                passed.append(fwd)
        for a in range(n):
            copy(a, 0, sibling, me).wait_recv()
            for j, chip in enumerate(chips):
                copy(a, 4 + j, (*chip, 1 - c), me).wait_recv()
        for cp in first + passed:
            cp.wait_send()
        for cp in mine:
            cp.wait()

    sems = [pltpu.SemaphoreType.DMA((7 * n,)), pltpu.SemaphoreType.DMA((7 * n,)), pltpu.SemaphoreType.DMA((n,))]
    return _Comm(list(arrays), [_sds((N_DEV,) + a.shape, a.dtype) for a in arrays], sems, start, finish)


def _pair_sum(g, q, core, name, chip_major=False):
    rows, cols = g.shape[2:]
    tr = _row_tile(rows)

    def body(core_ref, g_ref, q_ref, o_ref):
        o_ref[...] = (g_ref[...] + q_ref[...]).astype(BF16)

    blk = pl.BlockSpec((None, tr, cols), lambda k, i, core_ref: (k, i, 0))
    if chip_major:
        own = pl.BlockSpec((None, None, tr, cols), lambda k, i, core_ref: (k, core_ref[0], i, 0))
    else:
        own = pl.BlockSpec((None, None, tr, cols), lambda k, i, core_ref: (core_ref[0], k, i, 0))
    return pl.pallas_call(
        body, name=name,
        grid_spec=pltpu.PrefetchScalarGridSpec(num_scalar_prefetch=1, grid=(N_CHIPS, rows // tr), in_specs=[own, blk], out_specs=blk),
        out_shape=_sds((N_CHIPS, rows, cols), BF16), compiler_params=_params("parallel", "parallel"),
    )(core, g, q)


def _sum_chips(own, landed, chip, name):
    _, rows, cols = own.shape
    tr = _row_tile(rows)

    def body(chip_ref, own_ref, a_ref, b_ref, c_ref, o_ref):
        acc = own_ref[...].astype(F32) + a_ref[...].astype(F32)
        o_ref[...] = (acc + b_ref[...].astype(F32)) + c_ref[...].astype(F32)

    blk = lambda flip: pl.BlockSpec((None, tr, cols), lambda i, chip_ref: (jnp.bitwise_xor(chip_ref[0], flip), i, 0))
    return pl.pallas_call(
        body, name=name,
        grid_spec=pltpu.PrefetchScalarGridSpec(num_scalar_prefetch=1, grid=(rows // tr,), in_specs=[blk(0), blk(1), blk(2), blk(3)],
                                               out_specs=pl.BlockSpec((tr, cols), lambda i, chip_ref: (i, 0))),
        out_shape=_sds((rows, cols), F32), compiler_params=_params("parallel"),
    )(chip, own, landed, landed, landed)


SUBLANES = 8


def _tile_rows(n_elems):
    return -(-n_elems // (SUBLANES * LANE)) * SUBLANES


SMALL_ITEMS = (("b_ada", N_MOD * D_MODEL), ("pre_w_mix", D_MODEL), ("post_w_mix", D_MODEL), ("pre_w_mlp", D_MODEL),
               ("post_w_mlp", D_MODEL), ("attn_out_w", ATT_WIDTH), ("hg_norm_w", HG_HEAD_DIM), ("attn_sinks", ATT_Q_HEADS),
               ("lb_0", HG_WIDTH), ("lb_1", HG_WIDTH))
SMALL_AT = {}
for _name, _size in SMALL_ITEMS:
    SMALL_AT[_name] = (sum(r for _, r in SMALL_AT.values()), _tile_rows(_size))
SMALL_ROWS = sum(r for _, r in SMALL_AT.values())
MOD_ROWS = SMALL_AT["b_ada"][1]
PLAIN_ROWS = SMALL_AT["lb_0"][0] - MOD_ROWS
LB_ROWS = SMALL_AT["lb_0"][1]


def _rows(a, nrows=None):
    flat = a.reshape(-1)
    nrows = _tile_rows(flat.shape[0]) if nrows is None else nrows
    return jnp.pad(flat, (0, nrows * LANE - flat.shape[0])).reshape(nrows, LANE)


def _pack_small(vals):
    vals = dict(vals, lb_0=vals["lb_table"][0], lb_1=vals["lb_table"][1])
    return jnp.concatenate([_rows(vals[name], SMALL_AT[name][1]) for name, _ in SMALL_ITEMS], axis=0)


def _unpack_small(p):
    def item(name, shape):
        first = SMALL_AT[name][0]
        size = shape[0] * shape[1]
        return p[first:first + SMALL_AT[name][1]].reshape(-1)[:size].reshape(shape)

    out = {name: item(name, (1, size)) for name, size in SMALL_ITEMS if not name.startswith("lb_")}
    out["lb_table"] = jnp.concatenate([item("lb_0", (1, HG_WIDTH)), item("lb_1", (1, HG_WIDTH))], axis=0)
    return out


def _pack_partials(dmod, plain, d_lb, loss_row):
    return jnp.concatenate([_rows(dmod, dmod.shape[0] * MOD_ROWS)] + [_rows(g) for g in plain] + [_rows(d_lb), _rows(loss_row)], axis=0)


def _small_update(packs, w, m, v, n_seq):
    mod_end = n_seq * MOD_ROWS
    lb_at = mod_end + PLAIN_ROWS
    t0, t1 = SMALL_AT["lb_0"][0], SMALL_AT["lb_1"][0]

    def body(p_ref, w_ref, m_ref, v_ref, g_ref, dl_ref, nm_ref, nv_ref, loss_ref):
        tot = p_ref[0]
        for d in range(1, N_DEV):
            tot = tot + p_ref[d]
        wv = w_ref[...]
        p1 = _sigmoid(wv[t1:t1 + LB_ROWS] - wv[t0:t0 + LB_ROWS])
        s = tot[lb_at:lb_at + LB_ROWS] * p1 * (1.0 - p1)
        g_bias = tot[0:MOD_ROWS]
        for q in range(1, n_seq):
            g_bias = g_bias + tot[q * MOD_ROWS:(q + 1) * MOD_ROWS]
        g = jnp.concatenate([g_bias, tot[mod_end:lb_at], -s, s], axis=0)
        g_ref[...] = g
        dl_ref[...], nm_ref[...], nv_ref[...] = _adamw_math(g, wv, m_ref[...], v_ref[...])
        loss_ref[...] = tot[lb_at + LB_ROWS:lb_at + LB_ROWS + SUBLANES]

    shp = _sds((SMALL_ROWS, LANE), F32)
    return pl.pallas_call(body, name="small_update", out_shape=[shp] * 4 + [_sds((SUBLANES, LANE), F32)],
                          compiler_params=_params())(packs, w, m, v)


def kernel(x, c, w_ada, b_ada, pre_w_mix, w_in, attn_sinks, attn_out_w, lb_table, hg_norm_w, w_out, post_w_mix, pre_w_mlp, w_up, w_down, post_w_mlp, loss_target, m_w_ada, m_b_ada, m_pre_w_mix, m_w_in, m_attn_sinks, m_attn_out_w, m_lb_table, m_hg_norm_w, m_w_out, m_post_w_mix, m_pre_w_mlp, m_w_up, m_w_down, m_post_w_mlp, v_w_ada, v_b_ada, v_pre_w_mix, v_w_in, v_attn_sinks, v_attn_out_w, v_lb_table, v_hg_norm_w, v_w_out, v_post_w_mix, v_pre_w_mlp, v_w_up, v_w_down, v_post_w_mlp):
    xi, yi, ci = _mesh_pos()
    chip = 2 * xi + yi
    dev = 2 * chip + ci
    bsz, seq, _ = x.shape
    ntok = bsz * seq
    ada_cols = w_ada.shape[2]
    core = jnp.reshape(ci, (1,)).astype(jnp.int32)
    chip_idx = jnp.reshape(chip, (1,)).astype(jnp.int32)
    flat = lambda a: a.reshape(ntok, a.shape[-1])
    unflat = lambda a: a.reshape(bsz, seq, a.shape[-1])
    tables = _rope_tables(seq)
    biases, chunk_masks = _band_biases(), _block_masks()

    def row_half(w):
        rows = w.shape[1] // 2
        return lax.dynamic_slice_in_dim(w[0], ci * rows, rows, axis=0).astype(BF16)

    def gather_buffer(w):
        rows, cols = w.shape[1] // 2, w.shape[2]
        own = w[0].astype(BF16).reshape(2, rows, cols)
        return lax.dynamic_update_slice(lax.empty((N_DEV, rows, cols), BF16), own, (2 * chip, 0, 0))

    w_in_t, m_in_t, v_in_t = [jnp.transpose(a[0])[None] for a in (w_in, m_w_in, v_w_in)]
    c_g, in_g = _allgather8([c, row_half(w_in_t)], "gather_first")
    c_all = c_g.reshape(N_DEV * bsz, D_MODEL)
    w_in_full = in_g.reshape(IN_COLS, D_MODEL)

    b_cols = lax.dynamic_slice_in_dim(b_ada, chip * ada_cols, ada_cols, axis=1)
    mod_part = _ada_fwd(c_all, w_ada[0], b_cols)
    half_rows = mod_part.shape[0] // 2
    (mod_g,) = _allgather8([lax.dynamic_slice_in_dim(mod_part, ci * half_rows, half_rows, axis=0)], "gather_mod")
    mod_all = mod_g.reshape(N_CHIPS, 2, half_rows, ada_cols).transpose(1, 2, 0, 3).reshape(N_DEV * bsz, N_MOD * D_MODEL)
    mod = lax.dynamic_slice_in_dim(mod_all, dev * bsz, bsz, axis=0)
    sh1, sc1, g1, sh2, sc2, g2 = [mod[:, i * D_MODEL:(i + 1) * D_MODEL].reshape(bsz, 1, D_MODEL) for i in range(N_MOD)]

    weights = (w_out, w_up, w_down)
    (out_part, up_part, down_part), started = _gather_start(
        [row_half(w) for w in weights], [gather_buffer(w) for w in weights], [mod_g], "gather_weights_start")

    h1, proj, qh, kh, vh = _in_proj_fused(x, pre_w_mix, sc1 + started[0:1, 0:1], sh1, w_in_full, tables)
    out_g = _gather_wait(out_part, [proj], "gather_out_wait")
    (attn_raw, cat, lse), ((out_g,),) = _attn_fwd(qh, kh, vh, attn_sinks, attn_out_w, biases, comms=[_plan_pair_forward([out_g])])
    up_g = _gather_wait(up_part, [attn_raw], "gather_up_wait")
    (o_raw, cat, states), ((up_g,),) = _hgrn_fwd(proj, lb_table, hg_norm_w, cat, chunk_masks, comms=[_plan_pair_forward([up_g])])
    down_g = _gather_wait(down_part, [o_raw], "gather_down_wait")
    w_out_full = out_g.reshape(D_MODEL, D_MODEL)
    w_up4 = up_g.reshape(N_CHIPS, D_MODEL, D_MODEL)
    mix, x1, h2 = _out_proj_fused(cat, w_out_full, x, post_w_mix, g1, pre_w_mlp, sc2, sh2)
    big_tm = min(ntok, 2048)
    up_spec = pl.BlockSpec((None, D_MODEL, D_MODEL), lambda i, j: (j, 0, 0))
    r, ((down_g,),) = _mm(flat(h2), w_up4, name="up_proj", out_dtype=BF16, tm=big_tm, tn=D_MODEL, n_out=D_FF, b_spec=up_spec,
                          epi=lambda acc: jnp.maximum(acc, 0.0), comms=[_plan_pair_forward([down_g])])
    w_down_full = down_g.reshape(D_FF, D_MODEL)
    square = lambda t: t * t
    loss_row, dy, dd, dg2, d_post_mlp = _down_proj_fused(unflat(r), w_down_full, x1, post_w_mlp, g2, loss_target)

    dpre = _mm(flat(dd), w_down_full, name="down_bwd", out_dtype=BF16, trans_b=True, tm=big_tm, tn=D_MODEL, extra=(r,),
               epi=lambda acc, rt: acc * (2.0 * rt.astype(F32)))
    half_rows = D_MODEL // 2
    g_down = _mm_tn(r, flat(dd), name="down_wgrad", tk=half_rows, tn=D_MODEL, a_fn=square,
                    out_shape=_sds((2, N_CHIPS, half_rows, D_MODEL), F32),
                    out_spec=pl.BlockSpec((None, None, half_rows, D_MODEL), lambda i, j: (i % 2, i // 2, 0, 0)))
    (dx1, dmix, dsc2, dsh2, dg1, d_pre_mlp, d_post_mix), ((q_down,),) = _up_bwd_fused(
        unflat(dpre), w_up4, dy, x1, mix, pre_w_mlp, sc2, post_w_mix, g1, comms=[_plan_pair([g_down], True)])
    g_up = _mm_tn(flat(h2), dpre, name="up_wgrad", tk=D_MODEL, tn=half_rows,
                  out_shape=_sds((2, N_CHIPS, half_rows, D_MODEL), F32),
                  out_spec=pl.BlockSpec((2, None, half_rows, half_rows), lambda i, j: (0, j // 2, 0, j % 2)))
    s_down = _pair_sum(g_down, q_down, core, "pair_sum_down")

    dcat, ((q_up,),) = _mm(flat(dmix), w_out_full, name="out_bwd", out_dtype=F32, trans_b=True, comms=[_plan_pair([g_up], True)])
    dcat = unflat(dcat)
    s_up = _pair_sum(g_up, q_up, core, "pair_sum_up")
    out_rows = D_MODEL // N_CHIPS
    g_out = _mm_tn(flat(cat), flat(dmix), name="out_wgrad", tk=2 * out_rows, tn=half_rows,
                   out_shape=_sds((2, N_CHIPS, out_rows, half_rows), F32),
                   out_spec=pl.BlockSpec((None, 2, out_rows, half_rows), lambda i, j: (j, i, 0, 0)))
    (dproj_rec, d_lb, d_hg_norm), ((x_down,), (q_out,)) = _hgrn_bwd(
        dcat, proj, o_raw, states, lb_table, hg_norm_w, chunk_masks, comms=[_plan_chip_exchange([s_down]), _plan_pair([g_out], True)])
    half_down = _sum_chips(s_down, x_down, chip_idx, "sum_chips_down")
    s_out = _pair_sum(g_out, q_out, core, "pair_sum_out")
    (dproj, d_attn_out, d_sinks), ((their_down,), (x_up,)) = _attn_bwd(
        dcat, attn_raw, attn_out_w, qh, kh, vh, lse, attn_sinks, tables, [bias.T for bias in biases], dproj_rec,
        comms=[_plan_pair([half_down], False), _plan_chip_exchange([s_up])])
    half_up = _sum_chips(s_up, x_up, chip_idx, "sum_chips_up")
    dproj = flat(dproj)
    in_rows = IN_COLS // N_CHIPS // 2
    g_in, ((x_out,),) = _mm_tn(dproj, flat(h1), name="in_wgrad", tk=2 * LANE, tn=D_MODEL, comms=[_plan_chip_exchange([s_out])])
    g_in = g_in.reshape(N_CHIPS, 2, in_rows, D_MODEL)
    half_out = _sum_chips(s_out, x_out, chip_idx, "sum_chips_out")
    dh1, ((q_in,), (their_up, their_out)) = _mm(
        dproj, w_in_full, name="in_bwd", out_dtype=F32,
        comms=[_plan_pair([g_in], "chip_major"), _plan_pair([half_up, half_out], False)])
    s_in = _pair_sum(g_in, q_in, core, "pair_sum_in", chip_major=True)
    in_sems, s_in, in_landing, started = _exchange_start(s_in, "exchange_in_start")
    grad_x, dsc1, dsh1, d_pre_mix = _norm1_bwd(unflat(dh1), dx1, x, pre_w_mix + started[0:1, 0:1], sc1)

    dmod = jnp.concatenate([dsh1, dsc1, dg1, dsh2, dsc2, dg2], axis=-1).reshape(bsz, N_MOD * D_MODEL)
    pack = _pack_partials(dmod, [d_pre_mix, d_post_mix, d_pre_mlp, d_post_mlp, d_attn_out, d_hg_norm, d_sinks], d_lb, loss_row)
    ((packs,),) = _comm_only([_plan_allgather8([pack])], "gather_small")
    w_small = dict(b_ada=b_ada, pre_w_mix=pre_w_mix, post_w_mix=post_w_mix, pre_w_mlp=pre_w_mlp, post_w_mlp=post_w_mlp,
                   attn_out_w=attn_out_w, hg_norm_w=hg_norm_w, attn_sinks=attn_sinks, lb_table=lb_table)
    m_small = dict(b_ada=m_b_ada, pre_w_mix=m_pre_w_mix, post_w_mix=m_post_w_mix, pre_w_mlp=m_pre_w_mlp, post_w_mlp=m_post_w_mlp,
                   attn_out_w=m_attn_out_w, hg_norm_w=m_hg_norm_w, attn_sinks=m_attn_sinks, lb_table=m_lb_table)
    v_small = dict(b_ada=v_b_ada, pre_w_mix=v_pre_w_mix, post_w_mix=v_post_w_mix, pre_w_mlp=v_pre_w_mlp, post_w_mlp=v_post_w_mlp,
                   attn_out_w=v_attn_out_w, hg_norm_w=v_hg_norm_w, attn_sinks=v_attn_sinks, lb_table=v_lb_table)
    *small_packed, loss_rows = _small_update(packs, _pack_small(w_small), _pack_small(m_small), _pack_small(v_small), bsz)
    small_out = [_unpack_small(p) for p in small_packed]
    loss = loss_rows[0, 0]

    dmod_all = packs[:, :bsz * MOD_ROWS, :].reshape(N_DEV * bsz, N_MOD * D_MODEL)
    dmod_cols = lax.dynamic_slice_in_dim(dmod_all, chip * ada_cols, ada_cols, axis=1)
    ada_out = _ada_bwd_adamw(c_all, dmod_cols, w_ada[0], m_w_ada[0], v_w_ada[0])

    s_in, x_in = _exchange_wait(in_sems, s_in, in_landing, [grad_x, ada_out[0]], "exchange_in_wait")
    half_in = _sum_chips(s_in, x_in, chip_idx, "sum_chips_in")
    ((their_in,),) = _comm_only([_plan_pair([half_in], False)], "pair_swap_in")
    big = dict(
        w_in=tuple(jnp.transpose(a) for a in _adamw_halves(half_in, their_in, core, w_in_t[0], m_in_t[0], v_in_t[0], axis=0,
                                                           name="adamw_in")),
        w_up=tuple(_adamw_halves(half_up, their_up, core, w_up[0], m_w_up[0], v_w_up[0], axis=0, name="adamw_up")),
        w_out=tuple(_adamw_halves(half_out, their_out, core, w_out[0], m_w_out[0], v_w_out[0], axis=1, name="adamw_out")),
        w_down=tuple(_adamw_halves(half_down, their_down, core, w_down[0], m_w_down[0], v_w_down[0], axis=0, name="adamw_down")),
        w_ada=tuple(ada_out),
    )
    order = ("w_ada", "b_ada", "pre_w_mix", "w_in", "attn_sinks", "attn_out_w", "lb_table", "hg_norm_w", "w_out", "post_w_mix",
             "pre_w_mlp", "w_up", "w_down", "post_w_mlp")
    outs = [loss, grad_x]
    for kind in range(4):
        for nm in order:
            outs.append(big[nm][kind][None] if nm in big else small_out[kind][nm])
    return tuple(outs)
```

```python
import jax
import jax.numpy as jnp
from jax import lax
from jax.experimental import pallas as pl
from jax.experimental.pallas import tpu as pltpu

F32 = jnp.float32
BF16 = jnp.bfloat16

D_MODEL = 1024
ATT_WIDTH = 512
ATT_HEAD_DIM = 64
ATT_Q_HEADS = 8
ATT_KV_HEADS = 2
ATT_GROUP = ATT_Q_HEADS // ATT_KV_HEADS
ATT_KV_COLS = ATT_KV_HEADS * ATT_HEAD_DIM
WINDOW = 128
ROPE_DIM = 16
ROPE_THETA = 500000.0
HG_WIDTH = 512
MIX_WIDTH = ATT_WIDTH + HG_WIDTH
HG_HEAD_DIM = 128
HG_HEADS = 4
HG_CHUNK = 32
IN_COLS = ATT_WIDTH + 2 * ATT_KV_COLS + 4 * HG_WIDTH
ATT_COLS = ATT_WIDTH + 2 * ATT_KV_COLS
D_FF = 4 * D_MODEL
N_MOD = 6
EPS = 1e-6
ATT_SCALE = ATT_HEAD_DIM ** -0.5

ADAM_LR = 0.001
ADAM_B1 = 0.9
ADAM_B2 = 0.999
ADAM_EPS = 1e-08
ADAM_WD = 0.01
ADAM_STEP = 10

N_CHIPS = 4
N_DEV = 8
LANE = 128
VMEM_LIMIT = 48 * 1024 * 1024
VMEM_LIMIT_BIG = 58 * 1024 * 1024
MESH = pl.DeviceIdType.MESH

NT_DIMS = (((1,), (1,)), ((), ()))
TN_DIMS = (((0,), (0,)), ((), ()))


def _sds(shape, dtype):
    return jax.ShapeDtypeStruct(tuple(shape), dtype)


def _params(*sem, vmem_limit=None):
    return pltpu.CompilerParams(dimension_semantics=sem, vmem_limit_bytes=VMEM_LIMIT if vmem_limit is None else vmem_limit)


def _sigmoid(x):
    return 1.0 / (1.0 + jnp.exp(-x))


def _dot(a, b, dims=None):
    a, b = a.astype(BF16), b.astype(BF16)
    if dims is None:
        return jnp.dot(a, b, preferred_element_type=F32)
    return lax.dot_general(a, b, dims, preferred_element_type=F32)


def _rms_fwd(x, w):
    rstd = lax.rsqrt(jnp.mean(x * x, axis=-1, keepdims=True) + EPS)
    xh = x * rstd
    return xh * w, xh, rstd


def _rms_bwd(dy, xh, rstd, w):
    dxh = dy * w
    dx = rstd * (dxh - xh * jnp.mean(dxh * xh, axis=-1, keepdims=True))
    return dx, dy * xh


def _colsum(x):
    return jnp.sum(x, axis=0, keepdims=True)


def _rms_hat(x):
    rstd = lax.rsqrt(jnp.mean(x * x, axis=-1, keepdims=True) + EPS)
    return x * rstd, rstd


def _rms_bwd_gain(dy, gain, xh, rstd):
    dxh = dy * gain
    dx = rstd * (dxh - xh * jnp.mean(dxh * xh, axis=-1, keepdims=True))
    return dx, _colsum(dy * xh)


def _row_tile(rows, cap=256):
    return max(t for t in range(16, cap + 1, 16) if rows % t == 0)


HBM_SPEC = pl.BlockSpec(memory_space=pltpu.HBM)


def _mesh_pos():
    return lax.axis_index("x"), lax.axis_index("y"), lax.axis_index("c")


class _Comm:
    def __init__(self, ins, outs, sems, start, finish, aliases=()):
        self.ins, self.outs, self.sems = list(ins), list(outs), list(sems)
        self.start, self.finish, self.aliases = start, finish, tuple(aliases)


def _call(body, args, *, name, grid, in_specs, out_specs, out_shape, sem, scratch_shapes=(), comms=(), aliases=None,
          vmem_limit=None):
    scratch_shapes = list(scratch_shapes)
    if not comms:
        return pl.pallas_call(body, name=name, grid=grid, in_specs=in_specs, out_specs=out_specs, out_shape=out_shape,
                              input_output_aliases=dict(aliases or {}), scratch_shapes=scratch_shapes,
                              compiler_params=_params(*sem, vmem_limit=vmem_limit))(*args)
    single = not isinstance(out_shape, (list, tuple))
    out_specs_l = [out_specs] if single else list(out_specs)
    out_shape_l = [out_shape] if single else list(out_shape)
    n_in, n_out, n_scr = len(in_specs), len(out_shape_l), len(scratch_shapes)
    n_ci = [len(cm.ins) for cm in comms]
    n_co = [len(cm.outs) for cm in comms]
    n_cs = [len(cm.sems) for cm in comms]
    aliases = dict(aliases or {})
    for k, cm in enumerate(comms):
        for i, o in cm.aliases:
            aliases[n_in + sum(n_ci[:k]) + i] = n_out + sum(n_co[:k]) + o

    def fused(*refs):
        pos = [0]

        def take(n):
            part = refs[pos[0]:pos[0] + n]
            pos[0] += n
            return part

        ins = take(n_in)
        c_ins = [take(n) for n in n_ci]
        outs = take(n_out)
        c_outs = [take(n) for n in n_co]
        scr = take(n_scr)
        c_sems = [take(n) for n in n_cs]
        first, last = True, True
        for d, size in enumerate(grid):
            first = jnp.logical_and(first, pl.program_id(d) == 0)
            last = jnp.logical_and(last, pl.program_id(d) == size - 1)

        def run(which):
            for cm, ci, co, cs in zip(comms, c_ins, c_outs, c_sems):
                getattr(cm, which)(ci, co, cs)

        if grid:
            pl.when(first)(lambda: run("start"))
        else:
            run("start")
        body(*ins, *outs, *scr)
        if grid:
            pl.when(last)(lambda: run("finish"))
        else:
            run("finish")

    res = pl.pallas_call(
        fused, name=name, grid=grid, in_specs=list(in_specs) + [HBM_SPEC] * sum(n_ci),
        out_specs=out_specs_l + [HBM_SPEC] * sum(n_co), out_shape=out_shape_l + [s for cm in comms for s in cm.outs],
        input_output_aliases=aliases, scratch_shapes=scratch_shapes + [s for cm in comms for s in cm.sems],
        compiler_params=_params(*["arbitrary"] * len(grid), vmem_limit=vmem_limit),
    )(*args, *[a for cm in comms for a in cm.ins])
    main = res[:n_out]
    extra, at = [], n_out
    for n in n_co:
        extra.append(list(res[at:at + n]))
        at += n
    return (main[0] if single else list(main)), extra


def _mm(a, b, *, name, out_dtype, trans_b=False, tm=512, tn=None, extra=(), epi=None, b_spec=None, n_out=None, comms=()):
    m_total, k_total = a.shape
    if n_out is None:
        n_out = b.shape[0] if trans_b else b.shape[1]
    tn = n_out if tn is None else tn
    grid = (m_total // tm, n_out // tn)
    dims = NT_DIMS if trans_b else None

    def body(*refs):
        a_ref, b_ref = refs[0], refs[1]
        extra_refs = refs[2:2 + len(extra)]
        o_ref = refs[2 + len(extra)]
        acc = _dot(a_ref[...], b_ref[...], dims)
        if epi is not None:
            acc = epi(acc, *[r[...] for r in extra_refs])
        o_ref[...] = acc.astype(out_dtype)

    if b_spec is None:
        if trans_b:
            b_spec = pl.BlockSpec((tn, k_total), lambda i, j: (j, 0))
        else:
            b_spec = pl.BlockSpec((k_total, tn), lambda i, j: (0, j))
    in_specs = [pl.BlockSpec((tm, k_total), lambda i, j: (i, 0)), b_spec]
    in_specs += [pl.BlockSpec((tm, tn), lambda i, j: (i, j)) for _ in extra]
    return _call(
        body, (a, b, *extra), name=name, grid=grid, in_specs=in_specs,
        out_specs=pl.BlockSpec((tm, tn), lambda i, j: (i, j)),
        out_shape=_sds((m_total, n_out), out_dtype),
        sem=("parallel", "parallel"), comms=comms)


def _mm_tn(a, b, *, name, tk, tn, a_fn=None, out_shape=None, out_spec=None, comms=()):
    m_total, k_total = a.shape
    n_total = b.shape[1]
    grid = (k_total // tk, n_total // tn)

    def body(a_ref, b_ref, o_ref):
        av = a_ref[...]
        part = _dot(av if a_fn is None else a_fn(av), b_ref[...], TN_DIMS)
        o_ref[...] = part.reshape(o_ref.shape)

    if out_shape is None:
        out_shape = _sds((k_total, n_total), F32)
        out_spec = pl.BlockSpec((tk, tn), lambda i, j: (i, j))
    return _call(
        body, (a, b), name=name, grid=grid,
        in_specs=[pl.BlockSpec((m_total, tk), lambda i, j: (0, i)), pl.BlockSpec((m_total, tn), lambda i, j: (0, j))],
        out_specs=out_spec, out_shape=out_shape, sem=("parallel", "parallel"), comms=comms)


def _ada_fwd(c_all, w_shard, b_shard):
    nb, ncol = c_all.shape[0], w_shard.shape[1]
    tn = 512

    def body(c_ref, w_ref, b_ref, o_ref):
        c = c_ref[...]
        o_ref[...] = _dot(c * _sigmoid(c), w_ref[...]) + b_ref[...]

    return pl.pallas_call(
        body, name="ada_fwd", grid=(ncol // tn,),
        in_specs=[pl.BlockSpec((nb, D_MODEL), lambda j: (0, 0)), pl.BlockSpec((D_MODEL, tn), lambda j: (0, j)),
                  pl.BlockSpec((1, tn), lambda j: (0, j))],
        out_specs=pl.BlockSpec((nb, tn), lambda j: (0, j)), out_shape=_sds((nb, ncol), F32),
        compiler_params=_params("parallel"),
    )(c_all, w_shard, b_shard)


def _adamw_math(g, w, m, v):
    m = ADAM_B1 * m + (1.0 - ADAM_B1) * g
    v = ADAM_B2 * v + (1.0 - ADAM_B2) * (g * g)
    m_hat = m / (1.0 - ADAM_B1 ** ADAM_STEP)
    v_hat = v / (1.0 - ADAM_B2 ** ADAM_STEP)
    delta = -ADAM_LR * (m_hat / (jnp.sqrt(v_hat) + ADAM_EPS) + ADAM_WD * w)
    return delta, m, v


def _ada_bwd_adamw(c_all, dmod_cols, w, m, v):
    nb, ncol = dmod_cols.shape
    tn = 256

    def body(c_ref, d_ref, w_ref, m_ref, v_ref, g_ref, dl_ref, nm_ref, nv_ref):
        c = c_ref[...]
        g = _dot(c * _sigmoid(c), d_ref[...], TN_DIMS)
        g_ref[...] = g
        dl_ref[...], nm_ref[...], nv_ref[...] = _adamw_math(g, w_ref[...], m_ref[...], v_ref[...])

    col = pl.BlockSpec((D_MODEL, tn), lambda j: (0, j))
    shp = _sds((D_MODEL, ncol), F32)
    return pl.pallas_call(
        body, name="ada_bwd_adamw", grid=(ncol // tn,),
        in_specs=[pl.BlockSpec((nb, D_MODEL), lambda j: (0, 0)), pl.BlockSpec((nb, tn), lambda j: (0, j)), col, col, col],
        out_specs=[col, col, col, col], out_shape=[shp, shp, shp, shp],
        compiler_params=_params("parallel"),
    )(c_all, dmod_cols, w, m, v)


def _adamw_halves(own, theirs, core, w, m, v, *, axis, name):
    r2, c2 = own.shape
    tr = _row_tile(r2)
    nt = r2 // tr

    def body(core_ref, own_ref, their_ref, w_ref, m_ref, v_ref, g_ref, dl_ref, nm_ref, nv_ref):
        g = jnp.where(pl.program_id(0) == core_ref[0], own_ref[...], their_ref[...])
        g_ref[...] = g
        dl_ref[...], nm_ref[...], nv_ref[...] = _adamw_math(g, w_ref[...], m_ref[...], v_ref[...])

    if axis == 0:
        full = pl.BlockSpec((tr, c2), lambda h, i, core_ref: (h * nt + i, 0))
    else:
        full = pl.BlockSpec((tr, c2), lambda h, i, core_ref: (i, h))
    half = pl.BlockSpec((tr, c2), lambda h, i, core_ref: (i, 0))
    shp = _sds(w.shape, F32)
    return pl.pallas_call(
        body, name=name,
        grid_spec=pltpu.PrefetchScalarGridSpec(num_scalar_prefetch=1, grid=(2, nt), in_specs=[half, half, full, full, full],
                                               out_specs=[full] * 4),
        out_shape=[shp] * 4, compiler_params=_params("parallel", "parallel"),
    )(core, own, theirs, w, m, v)


def _tok_spec(tm, width=D_MODEL):
    return pl.BlockSpec((None, tm, width), lambda b, i: (b, i, 0))


def _row_spec(width=D_MODEL):
    return pl.BlockSpec((None, 1, width), lambda b, i: (b, 0, 0))


def _vec_spec(width=D_MODEL):
    return pl.BlockSpec((1, width), lambda b, i: (0, 0))


class _RowsOf:
    def __init__(self, ref, first, count):
        self.ref, self.rows = ref, slice(first, first + count)

    def __getitem__(self, idx):
        return self.ref[self.rows, :]

    def __setitem__(self, idx, value):
        self.ref[self.rows, :] = value


def _mm_rows(a, b, *, name, tm, extra, extra_specs, out_specs, out_shape, epi, pro=None, trans_b=False, b_chunks=1, comms=(),
             parts=1, zero_per_seq=(), zero_once=(), vmem_limit=None):
    bsz, seq, k_total = a.shape
    kc = k_total // b_chunks
    dims = NT_DIMS if trans_b else None
    rows = tm // parts

    def body(*refs):
        a_ref, b_ref = refs[0], refs[1]
        ex, outs = refs[2:2 + len(extra)], refs[2 + len(extra):]
        if zero_per_seq:
            @pl.when(pl.program_id(1) == 0)
            def _():
                for k in zero_per_seq:
                    outs[k][...] = jnp.zeros_like(outs[k])
        if zero_once:
            @pl.when(jnp.logical_and(pl.program_id(0) == 0, pl.program_id(1) == 0))
            def _():
                for k in zero_once:
                    outs[k][...] = jnp.zeros_like(outs[k])

        def part_of(ref, p):
            tiled = len(ref.shape) == 2 and ref.shape[0] == tm
            return _RowsOf(ref, p * rows, rows) if tiled and parts > 1 else ref

        accs = []
        for p in range(parts):
            a_p, ex_p, outs_p = part_of(a_ref, p), [part_of(r, p) for r in ex], [part_of(r, p) for r in outs]
            if b_chunks == 1:
                accs.append(_dot(a_p[...] if pro is None else pro(a_p, ex_p, outs_p), b_ref[...], dims))
            else:
                acc = _dot(a_p[...][:, 0:kc], b_ref[0], NT_DIMS)
                for k in range(1, b_chunks):
                    acc = acc + _dot(a_p[...][:, k * kc:(k + 1) * kc], b_ref[k], NT_DIMS)
                accs.append(acc)
        for p in range(parts):
            epi(accs[p], [part_of(r, p) for r in ex], [part_of(r, p) for r in outs])

    b_spec = pl.BlockSpec(b.shape, lambda bb, i: (0,) * b.ndim)
    return _call(
        body, (a, b, *extra), name=name, grid=(bsz, seq // tm), in_specs=[_tok_spec(tm, k_total), b_spec, *extra_specs],
        out_specs=out_specs, out_shape=out_shape, sem=("arbitrary", "arbitrary"), comms=comms, vmem_limit=vmem_limit)


def _in_proj_fused(x, w, sc, sh, w_in_t, tables, comms=()):
    tm = 512
    bsz, seq, _ = x.shape
    half = ROPE_DIM // 2
    heads_per_slab = LANE // ATT_HEAD_DIM

    def pro(x_ref, ex, outs):
        y, _, _ = _rms_fwd(x_ref[...], ex[0][...])
        h = (y * (1.0 + ex[1][...]) + ex[2][...]).astype(BF16)
        outs[0][...] = h
        return h

    def epi(acc, ex, outs):
        c, u, d = ex[3][...], ex[4][...], ex[5][...]
        _, rec_ref, q_ref, k_ref, v_ref = outs
        for k in range(HG_SLABS):
            rec_ref[k] = acc[:, ATT_COLS + k * HG_WIDTH:ATT_COLS + (k + 1) * HG_WIDTH]

        def rope(z):
            return (z * c + pltpu.roll(z, half, 1) * u + pltpu.roll(z, LANE - half, 1) * d).astype(BF16)

        for s in range(ATT_WIDTH // LANE):
            slab = rope(acc[:, s * LANE:(s + 1) * LANE])
            for part in range(heads_per_slab):
                g, hh = divmod(s * heads_per_slab + part, ATT_GROUP)
                piece = slab[:, part * ATT_HEAD_DIM:(part + 1) * ATT_HEAD_DIM]
                for blk in range(tm // WINDOW):
                    q_ref[blk, g, hh * WINDOW:(hh + 1) * WINDOW, :] = piece[blk * WINDOW:(blk + 1) * WINDOW]
        rk = rope(acc[:, ATT_WIDTH:ATT_WIDTH + LANE])
        vv = acc[:, ATT_WIDTH + LANE:ATT_COLS].astype(BF16)
        for g in range(ATT_KV_HEADS):
            k_ref[g] = rk[:, g * ATT_HEAD_DIM:(g + 1) * ATT_HEAD_DIM]
            v_ref[g] = vv[:, g * ATT_HEAD_DIM:(g + 1) * ATT_HEAD_DIM]

    tab = pl.BlockSpec((tm, LANE), lambda b, i: (i, 0))
    kv_spec = pl.BlockSpec((None, ATT_KV_HEADS, tm, ATT_HEAD_DIM), lambda b, i: (b, 0, i, 0))
    kv_shape = _sds((bsz, ATT_KV_HEADS, seq, ATT_HEAD_DIM), BF16)
    q_spec = pl.BlockSpec((None, tm // WINDOW, ATT_KV_HEADS, GROUP_ROWS, ATT_HEAD_DIM), lambda b, i: (b, i, 0, 0, 0))
    return _mm_rows(x, w_in_t, name="in_proj", tm=tm, extra=(w, sc, sh, *tables),
                    extra_specs=[_vec_spec(), _row_spec(), _row_spec(), tab, tab, tab],
                    out_specs=[_tok_spec(tm), pl.BlockSpec((None, HG_SLABS, tm, HG_WIDTH), lambda b, i: (b, 0, i, 0)), q_spec,
                               kv_spec, kv_spec],
                    out_shape=[_sds(x.shape, BF16), _sds((bsz, HG_SLABS, seq, HG_WIDTH), F32),
                               _sds((bsz, seq // WINDOW, ATT_KV_HEADS, GROUP_ROWS, ATT_HEAD_DIM), BF16), kv_shape, kv_shape],
                    pro=pro, epi=epi, trans_b=True, comms=comms)


def _rope_tables(seq):
    half = ROPE_DIM // 2
    inv_freq = ROPE_THETA ** (-jnp.arange(0, ROPE_DIM, 2, dtype=F32) / ROPE_DIM)
    ang = jnp.arange(seq, dtype=F32)[:, None] * inv_freq[None, :]
    cos, sin = jnp.cos(ang), jnp.sin(ang)
    rest = ATT_HEAD_DIM - ROPE_DIM
    ones, zeros, zh = jnp.ones((seq, rest), F32), jnp.zeros((seq, rest), F32), jnp.zeros((seq, half), F32)
    reps = LANE // ATT_HEAD_DIM
    t_cos = jnp.tile(jnp.concatenate([cos, cos, ones], axis=1), (1, reps))
    t_up = jnp.tile(jnp.concatenate([zh, sin, zeros], axis=1), (1, reps))
    t_dn = jnp.tile(jnp.concatenate([-sin, zh, zeros], axis=1), (1, reps))
    return t_cos, t_up, t_dn


GROUP_ROWS = ATT_GROUP * WINDOW


ATT_BPS = 2


MASKED = -1e30


def _band_biases():
    row = jnp.arange(GROUP_ROWS)[:, None] % WINDOW
    col = jnp.arange(2 * WINDOW)[None, :]
    own = jnp.logical_and(col >= WINDOW, col - WINDOW <= row)
    before = jnp.logical_and(col < WINDOW, col > row)
    return (jnp.where(jnp.logical_or(own, before), 0.0, MASKED).astype(F32), jnp.where(own, 0.0, MASKED).astype(F32))


def _band_bias(full_ref, first_ref, has_prev):
    return full_ref[...] if has_prev is True else jnp.where(has_prev, full_ref[...], first_ref[...])


def _sink_column(sink_ref, g):
    head = lax.broadcasted_iota(jnp.int32, (GROUP_ROWS, 1), 0) // WINDOW
    col = jnp.full((GROUP_ROWS, 1), sink_ref[0, g * ATT_GROUP], F32)
    for hh in range(1, ATT_GROUP):
        col = jnp.where(head == hh, sink_ref[0, g * ATT_GROUP + hh], col)
    return col


def _sink_row(sink_ref, g):
    return jnp.concatenate([jnp.full((1, WINDOW), sink_ref[0, g * ATT_GROUP + hh], F32) for hh in range(ATT_GROUP)], axis=1)


def _bias_spec(transposed=False):
    shape = (2 * WINDOW, GROUP_ROWS) if transposed else (GROUP_ROWS, 2 * WINDOW)
    return pl.BlockSpec(shape, lambda b, i: (0, 0))


def _attn_specs():
    q_spec = pl.BlockSpec((None, ATT_BPS, ATT_KV_HEADS, GROUP_ROWS, ATT_HEAD_DIM), lambda b, i: (b, i, 0, 0, 0))
    kv_cur = pl.BlockSpec((None, ATT_KV_HEADS, ATT_BPS * WINDOW, ATT_HEAD_DIM), lambda b, i: (b, 0, i, 0))
    kv_prev = pl.BlockSpec((None, ATT_KV_HEADS, WINDOW, ATT_HEAD_DIM), lambda b, i: (b, 0, jnp.maximum(ATT_BPS * i - 1, 0), 0))
    return q_spec, kv_cur, kv_prev


def _band(prev_ref, cur_ref, g, blk):
    own = cur_ref[g, blk * WINDOW:(blk + 1) * WINDOW]
    before = prev_ref[g] if blk == 0 else cur_ref[g, (blk - 1) * WINDOW:blk * WINDOW]
    return jnp.concatenate([before, own], axis=0)


def _attn_fwd(qh, kh, vh, sinks, w_norm, biases, comms=()):
    bsz, nblk = qh.shape[0], qh.shape[1]
    seq = nblk * WINDOW
    rows = ATT_BPS * WINDOW

    def body(sink_ref, q_ref, kc_ref, kp_ref, vc_ref, vp_ref, w_ref, full_ref, first_ref, raw_ref, an_ref, l_ref):
        l_ref[...] = jnp.zeros_like(l_ref)
        def block(blk):
            bias = _band_bias(full_ref, first_ref, True if blk else pl.program_id(1) > 0)
            groups = range(ATT_KV_HEADS)
            keys, vals = [_band(kp_ref, kc_ref, g, blk) for g in groups], [_band(vp_ref, vc_ref, g, blk) for g in groups]
            sink = [_sink_column(sink_ref, g) for g in groups]
            s = [_dot(q_ref[blk, g], keys[g], NT_DIMS) * ATT_SCALE + bias for g in groups]
            yield
            m = [jnp.maximum(jnp.max(s[g], axis=-1, keepdims=True), sink[g]) for g in groups]
            p = [jnp.exp(s[g] - m[g]) for g in groups]
            den = [jnp.sum(p[g], axis=-1, keepdims=True) + jnp.exp(sink[g] - m[g]) for g in groups]
            yield
            o = [_dot(p[g] / den[g], vals[g]) for g in groups]
            yield
            lse = [m[g] + jnp.log(den[g]) for g in groups]
            tok = slice(blk * WINDOW, (blk + 1) * WINDOW)
            for g in groups:
                for hh in range(ATT_GROUP):
                    h = g * ATT_GROUP + hh
                    raw_ref[tok, h * ATT_HEAD_DIM:(h + 1) * ATT_HEAD_DIM] = o[g][hh * WINDOW:(hh + 1) * WINDOW]
                    l_ref[tok, h:h + 1] = lse[g][hh * WINDOW:(hh + 1) * WINDOW]

        _in_step(block(blk) for blk in range(ATT_BPS))
        y, _, _ = _rms_fwd(raw_ref[...], w_ref[...])
        an_ref[...] = y.astype(BF16)

    cur = lambda width: pl.BlockSpec((None, rows, width), lambda b, i: (b, i, 0))
    q_spec, kv_cur, kv_prev = _attn_specs()
    return _call(
        body, (sinks, qh, kh, kh, vh, vh, w_norm, *biases), name="attn_fwd", grid=(bsz, nblk // ATT_BPS),
        in_specs=[pl.BlockSpec(memory_space=pltpu.SMEM), q_spec, kv_cur, kv_prev, kv_cur, kv_prev, _vec_spec(ATT_WIDTH),
                  _bias_spec(), _bias_spec()],
        out_specs=[cur(ATT_WIDTH), cur(ATT_WIDTH), cur(LANE)],
        out_shape=[_sds((bsz, seq, ATT_WIDTH), F32), _sds((bsz, seq, MIX_WIDTH), BF16), _sds((bsz, seq, LANE), F32)],
        sem=("parallel", "parallel"), comms=comms)


HG_Q0 = ATT_COLS // LANE
HG_F0 = HG_Q0 + HG_HEADS
HG_I0 = HG_F0 + HG_HEADS
HG_G0 = HG_I0 + HG_HEADS
HG_SLABS = 4
HG_Q, HG_F, HG_I, HG_G = range(HG_SLABS)
HG_TOK = 256
HG_NCH = HG_TOK // HG_CHUNK
HG_HPS = 2
HG_FWD_BLOCKS = 4
HG_BWD_BLOCKS = 2


def _block_masks():
    row = jnp.arange(HG_TOK)[:, None]
    col = jnp.arange(HG_TOK)[None, :]
    same = (row // HG_CHUNK) == (col // HG_CHUNK)
    return jnp.logical_and(same, col <= row).astype(F32), jnp.logical_and(same, col >= row).astype(F32)


def _row_in_chunk():
    return lax.broadcasted_iota(jnp.int32, (HG_TOK, LANE), 0) % HG_CHUNK


def _chunk_cumsum(x, reverse=False):
    ric = _row_in_chunk()
    shift = 1
    while shift < HG_CHUNK:
        if reverse:
            x = x + jnp.where(ric < HG_CHUNK - shift, pltpu.roll(x, HG_TOK - shift, 0), 0.0)
        else:
            x = x + jnp.where(ric >= shift, pltpu.roll(x, shift, 0), 0.0)
        shift *= 2
    return x


def _chunk_rows(rows):
    stacked = jnp.concatenate([r[None] for r in rows], axis=0)
    return jnp.broadcast_to(stacked, (HG_NCH, HG_CHUNK, LANE)).reshape(HG_TOK, LANE)


def _chunk_slices(x):
    return [x[j * HG_CHUNK:(j + 1) * HG_CHUNK] for j in range(HG_NCH)]


def _in_step(stages):
    stages = list(stages)
    while stages:
        stages = [g for g in stages if next(g, stages) is not stages]


def _hgrn_common(tbl, hf, hq):
    lb = _sigmoid(tbl[1:2] - tbl[0:1])
    sig = _sigmoid(hf)
    f = lb + (1.0 - lb) * sig
    sq = _sigmoid(hq)
    q, k = hq * sq, 1.0 - f
    b = _chunk_cumsum(jnp.log(f))
    last = [b[(j + 1) * HG_CHUNK - 1:(j + 1) * HG_CHUNK] for j in range(HG_NCH)]
    bl = _chunk_rows(last)
    e_b, e_nb, e_rem = jnp.exp(b), jnp.exp(-b), jnp.exp(bl - b)
    e_last = [jnp.exp(r) for r in last]
    return dict(lb=lb, sig=sig, f=f, sq=sq, q=q, k=k, e_b=e_b, e_nb=e_nb, e_rem=e_rem, e_last=e_last,
                qd=q * e_b, kd=k * e_nb, ku=k * e_rem)


def _hgrn_fwd(proj, lb_table, norm_w, mix_in, masks, comms=()):
    bsz, _, seq, _ = proj.shape
    nstep = seq // HG_TOK

    def body(tbl_ref, nw_ref, p_ref, mix_ref, lower_ref, o_ref, rec_ref, st_ref, s_scr):
        @pl.when(pl.program_id(2) == 0)
        def _():
            s_scr[...] = jnp.zeros_like(s_scr)

        lower = lower_ref[...]

        def head(hp, sub):
            ls = slice(hp * LANE, (hp + 1) * LANE)
            rows = slice(sub * HG_TOK, (sub + 1) * HG_TOK)
            v, hg = p_ref[HG_I, rows, ls], p_ref[HG_G, rows, ls]
            t = _hgrn_common(tbl_ref[:, ls], p_ref[HG_F, rows, ls], p_ref[HG_Q, rows, ls])
            yield
            a = _dot(t["qd"], t["kd"], NT_DIMS) * lower
            o_intra = _dot(a, v)
            yield
            v_c, ku_c, qd_c = [_chunk_slices(z.astype(BF16)) for z in (v, t["ku"], t["qd"])]
            updates = [_dot(v_c[j], ku_c[j], TN_DIMS) for j in range(HG_NCH)]
            yield
            st = s_scr[hp]
            states = []
            for j in range(HG_NCH):
                states.append(st)
                st = st * t["e_last"][j] + updates[j]
            s_scr[hp] = st
            yield
            o = o_intra + jnp.concatenate([_dot(qd_c[j], states[j], NT_DIMS) for j in range(HG_NCH)], axis=0)
            yield
            st_ref[hp, sub] = states[0]
            o_ref[rows, ls] = o
            y, _, _ = _rms_fwd(o, nw_ref[...])
            rec_ref[rows, ls] = (y * (hg * _sigmoid(hg))).astype(BF16)

        for sub in range(HG_FWD_BLOCKS):
            _in_step(head(hp, sub) for hp in range(HG_HPS))

    width = HG_HPS * LANE
    tok = HG_FWD_BLOCKS * HG_TOK
    head_out = pl.BlockSpec((None, tok, width), lambda b, h, t: (b, t, h))
    mix_out = pl.BlockSpec((None, tok, width), lambda b, h, t: (b, t, ATT_WIDTH // width + h))
    return _call(
        body, (lb_table, norm_w, proj, mix_in, masks[0]), name="hgrn_fwd",
        grid=(bsz, HG_HEADS // HG_HPS, nstep // HG_FWD_BLOCKS),
        in_specs=[pl.BlockSpec((2, width), lambda b, h, t: (0, h)), pl.BlockSpec((1, LANE), lambda b, h, t: (0, 0)),
                  pl.BlockSpec((None, HG_SLABS, tok, width), lambda b, h, t: (b, 0, t, h)), pl.BlockSpec(memory_space=pl.ANY),
                  pl.BlockSpec((HG_TOK, HG_TOK), lambda b, h, t: (0, 0))],
        out_specs=[head_out, mix_out,
                   pl.BlockSpec((None, HG_HPS, HG_FWD_BLOCKS, LANE, LANE), lambda b, h, t: (b, h, t, 0, 0))],
        out_shape=[_sds((bsz, seq, HG_WIDTH), F32), _sds(mix_in.shape, BF16),
                   _sds((bsz, HG_HEADS, nstep, LANE, LANE), F32)],
        scratch_shapes=[pltpu.VMEM((HG_HPS, LANE, LANE), F32)],
        sem=("parallel", "parallel", "arbitrary"), comms=comms, aliases={3: 1})


def _out_proj_fused(cat, w_out, x, post_w, g1, pre_w, sc2, sh2):
    tm = 512

    def epi(mix, ex, outs):
        x_ref, pw_ref, g1_ref, w2_ref, sc_ref, sh_ref = ex
        outs[0][...] = mix
        n1, _, _ = _rms_fwd(mix, pw_ref[...])
        x1 = x_ref[...] + g1_ref[...] * n1
        outs[1][...] = x1
        y2, _, _ = _rms_fwd(x1, w2_ref[...])
        outs[2][...] = (y2 * (1.0 + sc_ref[...]) + sh_ref[...]).astype(BF16)

    return _mm_rows(cat, w_out, name="out_proj", tm=tm, extra=(x, post_w, g1, pre_w, sc2, sh2),
                    extra_specs=[_tok_spec(tm), _vec_spec(), _row_spec(), _vec_spec(), _row_spec(), _row_spec()],
                    out_specs=[_tok_spec(tm), _tok_spec(tm), _tok_spec(tm)],
                    out_shape=[_sds(x.shape, F32), _sds(x.shape, F32), _sds(x.shape, BF16)], epi=epi)


def _acc_out(ref, first, value):
    @pl.when(first)
    def _():
        ref[...] = value

    @pl.when(jnp.logical_not(first))
    def _():
        ref[...] += value


def _down_proj_fused(r, w_down, x1, post_w, g2, target):
    tm = 512
    bsz = x1.shape[0]

    def pro(r_ref, ex, outs):
        rv = r_ref[...]
        return rv * rv

    def epi(down, ex, outs):
        x1_ref, w_ref, g2_ref, t_ref = ex
        loss_ref, dy_ref, dd_ref, dg2_ref, dw_ref = outs
        w, g2v = w_ref[...], g2_ref[...]
        gain = g2v * w
        dh, rstd = _rms_hat(down)
        err = x1_ref[...] + dh * gain - t_ref[...]
        part = (0.5 / D_MODEL) * jnp.sum(jnp.sum(err * err, axis=-1, keepdims=True), axis=0, keepdims=True)
        loss_ref[...] += jnp.broadcast_to(part, (1, LANE))
        dy = err * (1.0 / D_MODEL)
        dy_ref[...] = dy
        dd, per_col = _rms_bwd_gain(dy, gain, dh, rstd)
        dd_ref[...] = dd.astype(BF16)
        dg2_ref[...] += per_col * w
        dw_ref[...] += per_col * g2v

    return _mm_rows(r, w_down, name="down_proj", tm=tm, extra=(x1, post_w, g2, target),
                    extra_specs=[_tok_spec(tm), _vec_spec(), _row_spec(), _tok_spec(tm)],
                    out_specs=[_vec_spec(LANE), _tok_spec(tm), _tok_spec(tm), _row_spec(), _vec_spec()],
                    out_shape=[_sds((1, LANE), F32), _sds(x1.shape, F32), _sds(x1.shape, BF16), _sds((bsz, 1, D_MODEL), F32),
                               _sds((1, D_MODEL), F32)], pro=pro, epi=epi, parts=2, zero_per_seq=(3,), zero_once=(0, 4),
                    vmem_limit=VMEM_LIMIT_BIG)


def _up_bwd_fused(dpre, w_up4, dy, x1, mix, pre_w, sc2, post_w, g1, comms=()):
    tm = 512
    bsz = x1.shape[0]

    def epi(dh2v, ex, outs):
        dy_ref, x1_ref, mix_ref, w2_ref, sc_ref, pw_ref, g1_ref = ex
        dx1_ref, dmix_ref, dsc_ref, dsh_ref, dg1_ref, dw2_ref, dpw_ref = outs
        w2, pw, g1v = w2_ref[...], pw_ref[...], g1_ref[...]
        mod2 = 1.0 + sc_ref[...]
        xh2, rstd2 = _rms_hat(x1_ref[...])
        dsh_ref[...] += _colsum(dh2v)
        dx1n, per_col2 = _rms_bwd_gain(dh2v, mod2 * w2, xh2, rstd2)
        dsc_ref[...] += per_col2 * w2
        dw2_ref[...] += per_col2 * mod2
        dx1 = dy_ref[...] + dx1n
        dx1_ref[...] = dx1
        mh, rstd1 = _rms_hat(mix_ref[...])
        dmix, per_col1 = _rms_bwd_gain(dx1, g1v * pw, mh, rstd1)
        dmix_ref[...] = dmix.astype(BF16)
        dg1_ref[...] += per_col1 * pw
        dpw_ref[...] += per_col1 * g1v

    row_shape = _sds((bsz, 1, D_MODEL), F32)
    vec_shape = _sds((1, D_MODEL), F32)
    return _mm_rows(dpre, w_up4, name="up_bwd", tm=tm, extra=(dy, x1, mix, pre_w, sc2, post_w, g1),
                    extra_specs=[_tok_spec(tm), _tok_spec(tm), _tok_spec(tm), _vec_spec(), _row_spec(), _vec_spec(), _row_spec()],
                    out_specs=[_tok_spec(tm), _tok_spec(tm), _row_spec(), _row_spec(), _row_spec(), _vec_spec(), _vec_spec()],
                    out_shape=[_sds(x1.shape, F32), _sds(x1.shape, BF16), row_shape, row_shape, row_shape, vec_shape, vec_shape],
                    epi=epi, b_chunks=w_up4.shape[0], comms=comms, parts=2, zero_per_seq=(2, 3, 4), zero_once=(5, 6),
                    vmem_limit=VMEM_LIMIT_BIG)


def _norm1_bwd(dh1, dx1, x, pre_w, sc1, tm=512, comms=()):
    bsz, seq, _ = x.shape

    def body(dh_ref, dx1_ref, x_ref, w_ref, sc_ref, gx_ref, dsc_ref, dsh_ref, dw_ref):
        b, i = pl.program_id(0), pl.program_id(1)
        w = w_ref[...]
        dh = dh_ref[...]
        mod = 1.0 + sc_ref[...]
        xh, rstd = _rms_hat(x_ref[...])
        dx, per_col = _rms_bwd_gain(dh, mod * w, xh, rstd)
        _acc_out(dsh_ref, i == 0, _colsum(dh))
        _acc_out(dsc_ref, i == 0, per_col * w)
        _acc_out(dw_ref, jnp.logical_and(b == 0, i == 0), per_col * mod)
        gx_ref[...] = dx1_ref[...] + dx

    row_shape = _sds((bsz, 1, D_MODEL), F32)
    return _call(
        body, (dh1, dx1, x, pre_w, sc1), name="norm1_bwd", grid=(bsz, seq // tm),
        in_specs=[_tok_spec(tm), _tok_spec(tm), _tok_spec(tm), _vec_spec(), _row_spec()],
        out_specs=[_tok_spec(tm), _row_spec(), _row_spec(), _vec_spec()],
        out_shape=[_sds(x.shape, F32), row_shape, row_shape, _sds((1, D_MODEL), F32)],
        sem=("arbitrary", "arbitrary"), comms=comms)


def _hgrn_bwd(dcat, proj, o_raw, states, lb_table, norm_w, masks, comms=()):
    bsz, _, seq, _ = proj.shape
    tok = HG_BWD_BLOCKS * HG_TOK
    nstep = seq // tok
    rec0 = ATT_WIDTH // LANE
    width = HG_HPS * LANE
    slabs = (HG_Q0, HG_F0, HG_I0, HG_G0)
    n_steps = (HG_HEADS // HG_HPS) * bsz * nstep
    assert n_steps >= 2

    def body(tbl_ref, nw_ref, dr_ref, p_ref, o_ref, st_ref, lower_ref, upper_ref,
             dproj_ref, dlb_ref, dnw_ref, ds_scr, grad_buf, grad_sem):
        h, b, t = pl.program_id(0), pl.program_id(1), pl.program_id(2)
        step = (h * bsz + b) * nstep + t
        slot = step % 2
        dq_k, df_k, di_k, dg_k = range(4)

        def grad_copies(of_step):
            hh, bb, tt = of_step // (bsz * nstep), (of_step // nstep) % bsz, of_step % nstep
            rows = pl.ds(pl.multiple_of((nstep - 1 - tt) * tok, tok), tok)
            return [pltpu.make_async_copy(
                grad_buf.at[of_step % 2, k],
                dproj_ref.at[bb, rows, pl.ds(pl.multiple_of(slabs[k] * LANE + hh * width, width), width)],
                grad_sem.at[of_step % 2, k]) for k in range(4)]

        @pl.when(step >= 2)
        def _():
            for cp in grad_copies(step - 2):
                cp.wait()

        @pl.when(t == 0)
        def _():
            ds_scr[...] = jnp.zeros_like(ds_scr)

        lower, upper = lower_ref[...], upper_ref[...]
        dlb_parts, dnw_parts = [None] * HG_HPS, [None] * HG_HPS

        def head(hp, sub):
            ls = slice(hp * LANE, (hp + 1) * LANE)
            rows = slice(sub * HG_TOK, (sub + 1) * HG_TOK)
            hq, v, hg = p_ref[HG_Q, rows, ls], p_ref[HG_I, rows, ls], p_ref[HG_G, rows, ls]
            nw = nw_ref[...]
            c = _hgrn_common(tbl_ref[:, ls], p_ref[HG_F, rows, ls], hq)
            qd, kd, ku = c["qd"], c["kd"], c["ku"]
            yield
            y, on, rstd = _rms_fwd(o_ref[rows, ls], nw)
            sg = _sigmoid(hg)
            dr = dr_ref[rows, ls]
            grad_buf[slot, dg_k, rows, ls] = (dr * y * (sg * (1.0 + hg * (1.0 - sg)))).astype(BF16)
            do, dnw_rows = _rms_bwd(dr * (hg * sg), on, rstd, nw)
            yield
            at = _dot(kd, qd, NT_DIMS) * upper
            da = _dot(do, v, NT_DIMS) * lower
            dat = _dot(v, do, NT_DIMS) * upper
            yield
            dv = _dot(at, do)
            dqd = _dot(da, kd)
            dkd = _dot(dat, qd)
            yield
            do_c, qd_c, v_c, ku_c = [_chunk_slices(z.astype(BF16)) for z in (do, qd, v, ku)]
            outer = [_dot(do_c[j], qd_c[j], TN_DIMS) for j in range(HG_NCH)]
            yield
            ds = ds_scr[hp]
            ds_after = [None] * HG_NCH
            for j in reversed(range(HG_NCH)):
                ds_after[j] = ds
                ds = outer[j] + ds * c["e_last"][j]
            ds_scr[hp] = ds
            yield
            updates = [_dot(v_c[j], ku_c[j], TN_DIMS) for j in range(HG_NCH)]
            yield
            states = [st_ref[hp, sub]]
            for j in range(HG_NCH - 1):
                states.append(states[j] * c["e_last"][j] + updates[j])
            dv = dv + jnp.concatenate([_dot(ku_c[j], ds_after[j], NT_DIMS) for j in range(HG_NCH)], axis=0)
            dqd = dqd + jnp.concatenate([_dot(do_c[j], states[j]) for j in range(HG_NCH)], axis=0)
            dku = jnp.concatenate([_dot(v_c[j], ds_after[j]) for j in range(HG_NCH)], axis=0)
            yield
            dku_ku = dku * ku
            dbl = [_colsum(states[j] * ds_after[j]) * c["e_last"][j] + _colsum(dku_ku[j * HG_CHUNK:(j + 1) * HG_CHUNK])
                   for j in range(HG_NCH)]
            dk = dkd * c["e_nb"] + dku * c["e_rem"]
            db = dqd * qd - dkd * kd - dku_ku + jnp.where(_row_in_chunk() == HG_CHUNK - 1, _chunk_rows(dbl), 0.0)
            dfv = _chunk_cumsum(db, reverse=True) / c["f"] - dk
            sig, sq = c["sig"], c["sq"]
            grad_buf[slot, df_k, rows, ls] = (dfv * (1.0 - c["lb"]) * sig * (1.0 - sig)).astype(BF16)
            grad_buf[slot, dq_k, rows, ls] = (dqd * c["e_b"] * (sq * (1.0 + hq * (1.0 - sq)))).astype(BF16)
            grad_buf[slot, di_k, rows, ls] = dv.astype(BF16)
            d_lb, d_nw = _colsum(dfv * (1.0 - sig)), _colsum(dnw_rows)
            dlb_parts[hp] = d_lb if dlb_parts[hp] is None else dlb_parts[hp] + d_lb
            dnw_parts[hp] = d_nw if dnw_parts[hp] is None else dnw_parts[hp] + d_nw

        for sub in reversed(range(HG_BWD_BLOCKS)):
            _in_step(head(hp, sub) for hp in range(HG_HPS))
        _acc_out(dlb_ref, jnp.logical_and(b == 0, t == 0), jnp.concatenate(dlb_parts, axis=1))
        _acc_out(dnw_ref, jnp.logical_and(h == 0, jnp.logical_and(b == 0, t == 0)), sum(dnw_parts[1:], dnw_parts[0]))
        for cp in grad_copies(step):
            cp.start()

        @pl.when(step == n_steps - 1)
        def _():
            for cp in grad_copies(step - 1) + grad_copies(step):
                cp.wait()

    rev = lambda t: nstep - 1 - t
    slab = lambda first: pl.BlockSpec((None, tok, width), lambda h, b, t: (b, rev(t), first // HG_HPS + h))
    head = pl.BlockSpec((None, tok, width), lambda h, b, t: (b, rev(t), h))
    return _call(
        body, (lb_table, norm_w, dcat, proj, o_raw, states, *masks), name="hgrn_bwd",
        grid=(HG_HEADS // HG_HPS, bsz, nstep),
        in_specs=[pl.BlockSpec((2, width), lambda h, b, t: (0, h)), pl.BlockSpec((1, LANE), lambda h, b, t: (0, 0)),
                  slab(rec0), pl.BlockSpec((None, HG_SLABS, tok, width), lambda h, b, t: (b, 0, rev(t), h)), head,
                  pl.BlockSpec((None, HG_HPS, HG_BWD_BLOCKS, LANE, LANE), lambda h, b, t: (b, h, rev(t), 0, 0)),
                  pl.BlockSpec((HG_TOK, HG_TOK), lambda h, b, t: (0, 0)), pl.BlockSpec((HG_TOK, HG_TOK), lambda h, b, t: (0, 0))],
        out_specs=[pl.BlockSpec(memory_space=pl.ANY), pl.BlockSpec((1, width), lambda h, b, t: (0, h)),
                   pl.BlockSpec((1, LANE), lambda h, b, t: (0, 0))],
        out_shape=[_sds((bsz, seq, IN_COLS), BF16), _sds((1, HG_WIDTH), F32), _sds((1, LANE), F32)],
        scratch_shapes=[pltpu.VMEM((HG_HPS, LANE, LANE), F32), pltpu.VMEM((2, 4, tok, width), BF16),
                        pltpu.SemaphoreType.DMA((2, 4))],
        sem=("arbitrary", "arbitrary", "arbitrary"), comms=comms)


def _attn_bwd(dcat, raw, w_norm, qh, kh, vh, lse, sinks, tables, biases, dproj, comms=()):
    bsz, nblk = qh.shape[0], qh.shape[1]
    seq = nblk * WINDOW
    nstep = nblk // ATT_BPS
    half = ROPE_DIM // 2

    def body(sink_ref, da_ref, raw_ref, w_ref, q_ref, kc_ref, kp_ref, vc_ref, vp_ref, l_ref, c_ref, u_ref, d_ref,
             full_ref, first_ref, dproj_ref, o_ref, dw_ref, dsink_ref, carry_k, carry_v):
        b, i = pl.program_id(0), pl.program_id(1)
        first = jnp.logical_and(b == 0, i == 0)

        @pl.when(i == 0)
        def _():
            carry_k[...] = jnp.zeros_like(carry_k)
            carry_v[...] = jnp.zeros_like(carry_v)

        w = w_ref[...]
        _, on, rstd = _rms_fwd(raw_ref[...], w)
        do_step, dw_rows = _rms_bwd(da_ref[...], on, rstd, w)
        _acc_out(dw_ref, first, _colsum(dw_rows))
        lane8 = lax.broadcasted_iota(jnp.int32, (1, ATT_Q_HEADS), 1)
        dsink = jnp.zeros((1, ATT_Q_HEADS), F32)
        head_cols = jnp.where(lax.broadcasted_iota(jnp.int32, (2 * ATT_Q_HEADS, ATT_WIDTH), 1) // ATT_HEAD_DIM
                              == lax.broadcasted_iota(jnp.int32, (2 * ATT_Q_HEADS, ATT_WIDTH), 0), 1.0, 0.0)
        from_next_k, from_next_v = carry_k[...], carry_v[...]
        for blk in reversed(range(ATT_BPS)):
            tok = slice(blk * WINDOW, (blk + 1) * WINDOW)
            bias = _band_bias(full_ref, first_ref, True if blk else i < nstep - 1)
            do_all = do_step[tok]
            c, u, d = c_ref[tok, :], u_ref[tok, :], d_ref[tok, :]
            lse_t = l_ref[tok, :].T
            prod = do_all * raw_ref[tok, :]
            prod_hi = prod.astype(BF16)
            prod_lo = prod - prod_hi.astype(F32)
            dsum_t = _dot(head_cols, prod_hi, NT_DIMS) + _dot(head_cols, prod_lo, NT_DIMS)

            def unrope(g):
                return (g * c + pltpu.roll(g * u, LANE - half, 1) + pltpu.roll(g * d, half, 1)).astype(BF16)

            groups = range(ATT_KV_HEADS)
            group_row = lambda z, g: jnp.concatenate(
                [z[g * ATT_GROUP + hh:g * ATT_GROUP + hh + 1, :] for hh in range(ATT_GROUP)], axis=1)
            q = [q_ref[blk, g] for g in groups]
            keys, vals = [_band(kp_ref, kc_ref, g, blk) for g in groups], [_band(vp_ref, vc_ref, g, blk) for g in groups]
            do_g = [jnp.concatenate([do_all[:, (g * ATT_GROUP + hh) * ATT_HEAD_DIM:(g * ATT_GROUP + hh + 1) * ATT_HEAD_DIM]
                                     for hh in range(ATT_GROUP)], axis=0) for g in groups]
            dsum, lse_g = [group_row(dsum_t, g) for g in groups], [group_row(lse_t, g) for g in groups]
            s_t = [_dot(keys[g], q[g], NT_DIMS) for g in groups]
            dp_t = [_dot(vals[g], do_g[g], NT_DIMS) for g in groups]
            p_t = [jnp.exp(s_t[g] * ATT_SCALE + bias - lse_g[g]) for g in groups]
            ds_t = [p_t[g] * (dp_t[g] - dsum[g]) * ATT_SCALE for g in groups]
            dq_g = [_dot(ds_t[g], keys[g], TN_DIMS) for g in groups]
            dk_g = [_dot(ds_t[g], q[g]) for g in groups]
            dv_g = [_dot(p_t[g], do_g[g]) for g in groups]
            for g in groups:
                sink_part = jnp.exp(_sink_row(sink_ref, g) - lse_g[g]) * dsum[g]
                for hh in range(ATT_GROUP):
                    head_sum = jnp.sum(sink_part[:, hh * WINDOW:(hh + 1) * WINDOW], axis=1, keepdims=True)
                    dsink = dsink - jnp.where(lane8 == g * ATT_GROUP + hh, head_sum, 0.0)
            dq_parts = [dq_g[g][hh * WINDOW:(hh + 1) * WINDOW] for g in groups for hh in range(ATT_GROUP)]
            dk_before, dk_own = [z[:WINDOW] for z in dk_g], [z[WINDOW:] for z in dk_g]
            dv_before, dv_own = [z[:WINDOW] for z in dv_g], [z[WINDOW:] for z in dv_g]
            per_slab = LANE // ATT_HEAD_DIM
            for s in range(ATT_WIDTH // LANE):
                slab = jnp.concatenate(dq_parts[s * per_slab:(s + 1) * per_slab], axis=1)
                o_ref[tok, s * LANE:(s + 1) * LANE] = unrope(slab)
            o_ref[tok, ATT_WIDTH:ATT_WIDTH + LANE] = unrope(jnp.concatenate(dk_own, axis=1) + from_next_k)
            o_ref[tok, ATT_WIDTH + LANE:ATT_COLS] = (jnp.concatenate(dv_own, axis=1) + from_next_v).astype(BF16)
            from_next_k, from_next_v = jnp.concatenate(dk_before, axis=1), jnp.concatenate(dv_before, axis=1)
        carry_k[...] = from_next_k
        carry_v[...] = from_next_v
        _acc_out(dsink_ref, first, dsink)

    rows = ATT_BPS * WINDOW
    rev = lambda i: nstep - 1 - i
    cur = lambda width: pl.BlockSpec((None, rows, width), lambda b, i: (b, rev(i), 0))
    q_spec = pl.BlockSpec((None, ATT_BPS, ATT_KV_HEADS, GROUP_ROWS, ATT_HEAD_DIM), lambda b, i: (b, rev(i), 0, 0, 0))
    kv_cur = pl.BlockSpec((None, ATT_KV_HEADS, rows, ATT_HEAD_DIM), lambda b, i: (b, 0, rev(i), 0))
    kv_prev = pl.BlockSpec((None, ATT_KV_HEADS, WINDOW, ATT_HEAD_DIM), lambda b, i: (b, 0, jnp.maximum(ATT_BPS * rev(i) - 1, 0), 0))
    tab = pl.BlockSpec((rows, LANE), lambda b, i: (rev(i), 0))
    return _call(
        body, (sinks, dcat, raw, w_norm, qh, kh, kh, vh, vh, lse, *tables, *biases, dproj), name="attn_bwd", grid=(bsz, nstep),
        in_specs=[pl.BlockSpec(memory_space=pltpu.SMEM), cur(ATT_WIDTH), cur(ATT_WIDTH), _vec_spec(ATT_WIDTH), q_spec,
                  kv_cur, kv_prev, kv_cur, kv_prev, cur(LANE), tab, tab, tab, _bias_spec(True), _bias_spec(True),
                  pl.BlockSpec(memory_space=pl.ANY)],
        out_specs=[cur(ATT_COLS), _vec_spec(ATT_WIDTH), _vec_spec(ATT_Q_HEADS)],
        out_shape=[_sds(dproj.shape, BF16), _sds((1, ATT_WIDTH), F32), _sds((1, ATT_Q_HEADS), F32)],
        scratch_shapes=[pltpu.VMEM((WINDOW, LANE), F32), pltpu.VMEM((WINDOW, LANE), F32)],
        sem=("arbitrary", "arbitrary"), comms=comms, aliases={15: 0})


def _other_chips(x, y):
    return [(1 - x, y), (x, 1 - y), (1 - x, 1 - y)]


def _sem_pair(n):
    return [pltpu.SemaphoreType.DMA((n,)), pltpu.SemaphoreType.DMA((n,))]


def _plan_pair_forward(bufs):
    n = len(bufs)

    def copies(outs, sems):
        x, y, c = _mesh_pos()
        sends, lands = [], []
        for a in range(n):
            for j, chip in enumerate(_other_chips(x, y)):
                k = 3 * a + j
                slot = outs[a].at[4 * chip[0] + 2 * chip[1] + c]
                sends.append(pltpu.make_async_remote_copy(
                    src_ref=slot, dst_ref=slot, send_sem=sems[0].at[k], recv_sem=sems[1].at[k],
                    device_id=(x, y, 1 - c), device_id_type=MESH))
                theirs = outs[a].at[4 * chip[0] + 2 * chip[1] + 1 - c]
                lands.append(pltpu.make_async_remote_copy(
                    src_ref=theirs, dst_ref=theirs, send_sem=sems[0].at[k], recv_sem=sems[1].at[k],
                    device_id=(x, y, 1 - c), device_id_type=MESH))
        return sends, lands

    def start(ins, outs, sems):
        for cp in copies(outs, sems)[0]:
            cp.start()

    def finish(ins, outs, sems):
        sends, lands = copies(outs, sems)
        for cp in lands:
            cp.wait_recv()
        for cp in sends:
            cp.wait_send()

    return _Comm(list(bufs), [_sds(b.shape, b.dtype) for b in bufs], _sem_pair(3 * n), start, finish,
                 aliases=[(a, a) for a in range(n)])


def _plan_pair(arrays, other_half):
    n = len(arrays)
    per = N_CHIPS if other_half == "chip_major" else 1

    def copies(ins, outs, sems):
        x, y, c = _mesh_pos()
        out = []
        for a in range(n):
            for k in range(per):
                if other_half == "chip_major":
                    src, dst = ins[a].at[k, 1 - c], outs[a].at[k]
                else:
                    src, dst = (ins[a].at[1 - c] if other_half else ins[a]), outs[a]
                out.append(pltpu.make_async_remote_copy(
                    src_ref=src, dst_ref=dst, send_sem=sems[0].at[per * a + k], recv_sem=sems[1].at[per * a + k],
                    device_id=(x, y, 1 - c), device_id_type=MESH))
        return out

    def start(ins, outs, sems):
        for cp in copies(ins, outs, sems):
            cp.start()

    def finish(ins, outs, sems):
        for cp in copies(ins, outs, sems):
            cp.wait()

    if other_half == "chip_major":
        shapes = [_sds((a.shape[0],) + a.shape[2:], a.dtype) for a in arrays]
    else:
        shapes = [_sds(a.shape[1:] if other_half else a.shape, a.dtype) for a in arrays]
    return _Comm(list(arrays), shapes, _sem_pair(per * n), start, finish)


def _plan_chip_exchange(arrays):
    n = len(arrays)

    def copies(ins, outs, sems):
        x, y, c = _mesh_pos()
        sends, lands = [], []
        for a in range(n):
            for j, chip in enumerate(_other_chips(x, y)):
                k = 3 * a + j
                sends.append(pltpu.make_async_remote_copy(
                    src_ref=ins[a].at[2 * chip[0] + chip[1]], dst_ref=outs[a].at[2 * x + y], send_sem=sems[0].at[k],
                    recv_sem=sems[1].at[k], device_id=(*chip, c), device_id_type=MESH))
                slot = outs[a].at[2 * chip[0] + chip[1]]
                lands.append(pltpu.make_async_remote_copy(
                    src_ref=slot, dst_ref=slot, send_sem=sems[0].at[k], recv_sem=sems[1].at[k],
                    device_id=(*chip, c), device_id_type=MESH))
        return sends, lands

    def start(ins, outs, sems):
        for cp in copies(ins, outs, sems)[0]:
            cp.start()

    def finish(ins, outs, sems):
        sends, lands = copies(ins, outs, sems)
        for cp in lands:
            cp.wait_recv()
        for cp in sends:
            cp.wait_send()

    return _Comm(list(arrays), [_sds(a.shape, a.dtype) for a in arrays], _sem_pair(3 * n), start, finish)


SEM_SPEC = pl.BlockSpec(memory_space=pltpu.SEMAPHORE)
N_OTHER = N_CHIPS - 1


def _exchange_copies(s_ref, land_ref, sems):
    x, y, c = _mesh_pos()
    return [pltpu.make_async_remote_copy(
        src_ref=s_ref.at[2 * chip[0] + chip[1]], dst_ref=land_ref.at[2 * x + y], send_sem=sems[j], recv_sem=sems[N_OTHER + j],
        device_id=(*chip, c), device_id_type=MESH) for j, chip in enumerate(_other_chips(x, y))]


def _exchange_start(s, name):
    def body(s_ref, land_ref, *outs):
        sems, token = outs[:2 * N_OTHER], outs[-1]
        for cp in _exchange_copies(s_ref, land_ref, sems):
            cp.start()
        token[...] = jnp.zeros_like(token)

    hbm = pltpu.HBM(s.shape, s.dtype)
    res = pl.pallas_call(
        body, name=name,
        out_shape=(pltpu.SemaphoreType.DMA(()),) * (2 * N_OTHER) + (hbm, hbm, _sds((SUBLANES, LANE), F32)),
        in_specs=(HBM_SPEC, HBM_SPEC),
        out_specs=(SEM_SPEC,) * (2 * N_OTHER) + (HBM_SPEC, HBM_SPEC, pl.BlockSpec(memory_space=pltpu.VMEM)),
        input_output_aliases={0: 2 * N_OTHER, 1: 2 * N_OTHER + 1},
        compiler_params=pltpu.CompilerParams(has_side_effects=pltpu.SideEffectType.DATAFLOW_SIDE_EFFECTING),
    )(pltpu.with_memory_space_constraint(s, pltpu.HBM), pltpu.with_memory_space_constraint(lax.empty(s.shape, s.dtype), pltpu.HBM))
    return res[:2 * N_OTHER], res[2 * N_OTHER], res[2 * N_OTHER + 1], res[-1]


def _exchange_wait(sems, s_thru, land_thru, afters, name):
    def body(s_ref, land_ref, *rest):
        for cp in _exchange_copies(s_ref, land_ref, rest[:2 * N_OTHER]):
            cp.wait_send()
            cp.wait_recv()

    hbm = pltpu.HBM(s_thru.shape, s_thru.dtype)
    return pl.pallas_call(
        body, name=name, out_shape=(hbm, hbm),
        in_specs=(HBM_SPEC, HBM_SPEC) + (SEM_SPEC,) * (2 * N_OTHER) + (pl.BlockSpec(memory_space=pl.ANY),) * len(afters),
        out_specs=(HBM_SPEC, HBM_SPEC), input_output_aliases={0: 0, 1: 1},
        compiler_params=pltpu.CompilerParams(has_side_effects=pltpu.SideEffectType.DATAFLOW_SIDE_EFFECTING),
    )(s_thru, land_thru, *sems, *afters)


def _gather_copies(block_ref, buf_ref, sems):
    x, y, c = _mesh_pos()
    return [pltpu.make_async_remote_copy(
        src_ref=block_ref, dst_ref=buf_ref.at[4 * x + 2 * y + c], send_sem=sems[j], recv_sem=sems[N_OTHER + j],
        device_id=(*chip, c), device_id_type=MESH) for j, chip in enumerate(_other_chips(x, y))]


def _gather_start(blocks, bufs, afters, name):
    n = len(blocks)
    per = 2 * N_OTHER

    def body(*refs):
        ins, outs = refs[:2 * n], refs[2 * n + len(afters):]
        for a in range(n):
            for cp in _gather_copies(ins[a], ins[n + a], outs[a * per:(a + 1) * per]):
                cp.start()
        outs[-1][...] = jnp.zeros_like(outs[-1])

    hbm = [pltpu.HBM(z.shape, z.dtype) for z in list(blocks) + list(bufs)]
    res = pl.pallas_call(
        body, name=name,
        out_shape=(pltpu.SemaphoreType.DMA(()),) * (n * per) + tuple(hbm) + (_sds((SUBLANES, LANE), F32),),
        in_specs=(HBM_SPEC,) * (2 * n) + (pl.BlockSpec(memory_space=pl.ANY),) * len(afters),
        out_specs=(SEM_SPEC,) * (n * per) + (HBM_SPEC,) * (2 * n) + (pl.BlockSpec(memory_space=pltpu.VMEM),),
        input_output_aliases={k: n * per + k for k in range(2 * n)},
        compiler_params=pltpu.CompilerParams(has_side_effects=pltpu.SideEffectType.DATAFLOW_SIDE_EFFECTING),
    )(*[pltpu.with_memory_space_constraint(z, pltpu.HBM) for z in list(blocks) + list(bufs)], *afters)
    parts = [(res[a * per:(a + 1) * per], res[n * per + a], res[n * per + n + a]) for a in range(n)]
    return parts, res[-1]


def _gather_wait(part, afters, name):
    sems, block, buf = part

    def body(block_ref, buf_ref, *rest):
        for cp in _gather_copies(block_ref, buf_ref, rest[:2 * N_OTHER]):
            cp.wait_send()
            cp.wait_recv()

    return pl.pallas_call(
        body, name=name, out_shape=(pltpu.HBM(block.shape, block.dtype), pltpu.HBM(buf.shape, buf.dtype)),
        in_specs=(HBM_SPEC, HBM_SPEC) + (SEM_SPEC,) * (2 * N_OTHER) + (pl.BlockSpec(memory_space=pl.ANY),) * len(afters),
        out_specs=(HBM_SPEC, HBM_SPEC), input_output_aliases={0: 0, 1: 1},
        compiler_params=pltpu.CompilerParams(has_side_effects=pltpu.SideEffectType.DATAFLOW_SIDE_EFFECTING),
    )(block, buf, *sems, *afters)[1]


def _comm_only(comms, name):
    return _call(lambda: None, (), name=name, grid=(), in_specs=[], out_specs=[], out_shape=[], sem=(), comms=comms)[1]


def _allgather8(arrays, name):
    return _comm_only([_plan_allgather8(arrays)], name)[0]


def _plan_allgather8(arrays):
    n = len(arrays)

    def parts(ins, outs, sems):
        send_sems, recv_sems, local_sems = sems
        x, y, c = _mesh_pos()
        me, sibling = (x, y, c), (x, y, 1 - c)
        chips = _other_chips(x, y)

        def copy(a, k, block, to, src=None):
            dst = outs[a].at[4 * block[0] + 2 * block[1] + block[2]]
            return pltpu.make_async_remote_copy(
                src_ref=dst if src is None else src, dst_ref=dst, send_sem=send_sems.at[7 * a + k],
                recv_sem=recv_sems.at[7 * a + k], device_id=to, device_id_type=MESH)

        mine = [pltpu.make_async_copy(ins[a], outs[a].at[4 * x + 2 * y + c], local_sems.at[a]) for a in range(n)]
        first = []
        for a in range(n):
            first.append(copy(a, 0, me, sibling, src=ins[a]))
            first += [copy(a, 1 + j, me, (*chip, c), src=ins[a]) for j, chip in enumerate(chips)]
        return copy, mine, first, me, sibling, chips, c

    def start(ins, outs, sems):
        _, mine, first, *_ = parts(ins, outs, sems)
        for cp in mine + first:
            cp.start()

    def finish(ins, outs, sems):
        copy, mine, first, me, sibling, chips, c = parts(ins, outs, sems)
        passed = []
        for j, chip in enumerate(chips):
            for a in range(n):
                copy(a, 1 + j, (*chip, c), me).wait_recv()
                fwd = copy(a, 4 + j, (*chip, c), sibling)
                fwd.start()
                passed.append(fwd)
        for a in range(n):
            copy(a, 0, sibling, me).wait_recv()
            for j, chip in enumerate(chips):
                copy(a, 4 + j, (*chip, 1 - c), me).wait_recv()
        for cp in first + passed:
            cp.wait_send()
        for cp in mine:
            cp.wait()

    sems = [pltpu.SemaphoreType.DMA((7 * n,)), pltpu.SemaphoreType.DMA((7 * n,)), pltpu.SemaphoreType.DMA((n,))]
    return _Comm(list(arrays), [_sds((N_DEV,) + a.shape, a.dtype) for a in arrays], sems, start, finish)


def _pair_sum(g, q, core, name, chip_major=False):
    rows, cols = g.shape[2:]
    tr = _row_tile(rows)

    def body(core_ref, g_ref, q_ref, o_ref):
        o_ref[...] = (g_ref[...] + q_ref[...]).astype(BF16)

    blk = pl.BlockSpec((None, tr, cols), lambda k, i, core_ref: (k, i, 0))
    if chip_major:
        own = pl.BlockSpec((None, None, tr, cols), lambda k, i, core_ref: (k, core_ref[0], i, 0))
    else:
        own = pl.BlockSpec((None, None, tr, cols), lambda k, i, core_ref: (core_ref[0], k, i, 0))
    return pl.pallas_call(
        body, name=name,
        grid_spec=pltpu.PrefetchScalarGridSpec(num_scalar_prefetch=1, grid=(N_CHIPS, rows // tr), in_specs=[own, blk], out_specs=blk),
        out_shape=_sds((N_CHIPS, rows, cols), BF16), compiler_params=_params("parallel", "parallel"),
    )(core, g, q)


def _sum_chips(own, landed, chip, name):
    _, rows, cols = own.shape
    tr = _row_tile(rows)

    def body(chip_ref, own_ref, a_ref, b_ref, c_ref, o_ref):
        acc = own_ref[...].astype(F32) + a_ref[...].astype(F32)
        o_ref[...] = (acc + b_ref[...].astype(F32)) + c_ref[...].astype(F32)

    blk = lambda flip: pl.BlockSpec((None, tr, cols), lambda i, chip_ref: (jnp.bitwise_xor(chip_ref[0], flip), i, 0))
    return pl.pallas_call(
        body, name=name,
        grid_spec=pltpu.PrefetchScalarGridSpec(num_scalar_prefetch=1, grid=(rows // tr,), in_specs=[blk(0), blk(1), blk(2), blk(3)],
                                               out_specs=pl.BlockSpec((tr, cols), lambda i, chip_ref: (i, 0))),
        out_shape=_sds((rows, cols), F32), compiler_params=_params("parallel"),
    )(chip, own, landed, landed, landed)


SUBLANES = 8


def _tile_rows(n_elems):
    return -(-n_elems // (SUBLANES * LANE)) * SUBLANES


SMALL_ITEMS = (("b_ada", N_MOD * D_MODEL), ("pre_w_mix", D_MODEL), ("post_w_mix", D_MODEL), ("pre_w_mlp", D_MODEL),
               ("post_w_mlp", D_MODEL), ("attn_out_w", ATT_WIDTH), ("hg_norm_w", HG_HEAD_DIM), ("attn_sinks", ATT_Q_HEADS),
               ("lb_0", HG_WIDTH), ("lb_1", HG_WIDTH))
SMALL_AT = {}
for _name, _size in SMALL_ITEMS:
    SMALL_AT[_name] = (sum(r for _, r in SMALL_AT.values()), _tile_rows(_size))
SMALL_ROWS = sum(r for _, r in SMALL_AT.values())
MOD_ROWS = SMALL_AT["b_ada"][1]
PLAIN_ROWS = SMALL_AT["lb_0"][0] - MOD_ROWS
LB_ROWS = SMALL_AT["lb_0"][1]


def _rows(a, nrows=None):
    flat = a.reshape(-1)
    nrows = _tile_rows(flat.shape[0]) if nrows is None else nrows
    return jnp.pad(flat, (0, nrows * LANE - flat.shape[0])).reshape(nrows, LANE)


def _pack_small(vals):
    vals = dict(vals, lb_0=vals["lb_table"][0], lb_1=vals["lb_table"][1])
    return jnp.concatenate([_rows(vals[name], SMALL_AT[name][1]) for name, _ in SMALL_ITEMS], axis=0)


def _unpack_small(p):
    def item(name, shape):
        first = SMALL_AT[name][0]
        size = shape[0] * shape[1]
        return p[first:first + SMALL_AT[name][1]].reshape(-1)[:size].reshape(shape)

    out = {name: item(name, (1, size)) for name, size in SMALL_ITEMS if not name.startswith("lb_")}
    out["lb_table"] = jnp.concatenate([item("lb_0", (1, HG_WIDTH)), item("lb_1", (1, HG_WIDTH))], axis=0)
    return out


def _pack_partials(dmod, plain, d_lb, loss_row):
    return jnp.concatenate([_rows(dmod, dmod.shape[0] * MOD_ROWS)] + [_rows(g) for g in plain] + [_rows(d_lb), _rows(loss_row)], axis=0)


def _small_update(packs, w, m, v, n_seq):
    mod_end = n_seq * MOD_ROWS
    lb_at = mod_end + PLAIN_ROWS
    t0, t1 = SMALL_AT["lb_0"][0], SMALL_AT["lb_1"][0]

    def body(p_ref, w_ref, m_ref, v_ref, g_ref, dl_ref, nm_ref, nv_ref, loss_ref):
        tot = p_ref[0]
        for d in range(1, N_DEV):
            tot = tot + p_ref[d]
        wv = w_ref[...]
        p1 = _sigmoid(wv[t1:t1 + LB_ROWS] - wv[t0:t0 + LB_ROWS])
        s = tot[lb_at:lb_at + LB_ROWS] * p1 * (1.0 - p1)
        g_bias = tot[0:MOD_ROWS]
        for q in range(1, n_seq):
            g_bias = g_bias + tot[q * MOD_ROWS:(q + 1) * MOD_ROWS]
        g = jnp.concatenate([g_bias, tot[mod_end:lb_at], -s, s], axis=0)
        g_ref[...] = g
        dl_ref[...], nm_ref[...], nv_ref[...] = _adamw_math(g, wv, m_ref[...], v_ref[...])
        loss_ref[...] = tot[lb_at + LB_ROWS:lb_at + LB_ROWS + SUBLANES]

    shp = _sds((SMALL_ROWS, LANE), F32)
    return pl.pallas_call(body, name="small_update", out_shape=[shp] * 4 + [_sds((SUBLANES, LANE), F32)],
                          compiler_params=_params())(packs, w, m, v)


def kernel(x, c, w_ada, b_ada, pre_w_mix, w_in, attn_sinks, attn_out_w, lb_table, hg_norm_w, w_out, post_w_mix, pre_w_mlp, w_up, w_down, post_w_mlp, loss_target, m_w_ada, m_b_ada, m_pre_w_mix, m_w_in, m_attn_sinks, m_attn_out_w, m_lb_table, m_hg_norm_w, m_w_out, m_post_w_mix, m_pre_w_mlp, m_w_up, m_w_down, m_post_w_mlp, v_w_ada, v_b_ada, v_pre_w_mix, v_w_in, v_attn_sinks, v_attn_out_w, v_lb_table, v_hg_norm_w, v_w_out, v_post_w_mix, v_pre_w_mlp, v_w_up, v_w_down, v_post_w_mlp):
    xi, yi, ci = _mesh_pos()
    chip = 2 * xi + yi
    dev = 2 * chip + ci
    bsz, seq, _ = x.shape
    ntok = bsz * seq
    ada_cols = w_ada.shape[2]
    core = jnp.reshape(ci, (1,)).astype(jnp.int32)
    chip_idx = jnp.reshape(chip, (1,)).astype(jnp.int32)
    flat = lambda a: a.reshape(ntok, a.shape[-1])
    unflat = lambda a: a.reshape(bsz, seq, a.shape[-1])
    tables = _rope_tables(seq)
    biases, chunk_masks = _band_biases(), _block_masks()

    def row_half(w):
        rows = w.shape[1] // 2
        return lax.dynamic_slice_in_dim(w[0], ci * rows, rows, axis=0).astype(BF16)

    def gather_buffer(w):
        rows, cols = w.shape[1] // 2, w.shape[2]
        own = w[0].astype(BF16).reshape(2, rows, cols)
        return lax.dynamic_update_slice(lax.empty((N_DEV, rows, cols), BF16), own, (2 * chip, 0, 0))

    w_in_t, m_in_t, v_in_t = [jnp.transpose(a[0])[None] for a in (w_in, m_w_in, v_w_in)]
    c_g, in_g = _allgather8([c, row_half(w_in_t)], "gather_first")
    c_all = c_g.reshape(N_DEV * bsz, D_MODEL)
    w_in_full = in_g.reshape(IN_COLS, D_MODEL)

    b_cols = lax.dynamic_slice_in_dim(b_ada, chip * ada_cols, ada_cols, axis=1)
    mod_part = _ada_fwd(c_all, w_ada[0], b_cols)
    half_rows = mod_part.shape[0] // 2
    (mod_g,) = _allgather8([lax.dynamic_slice_in_dim(mod_part, ci * half_rows, half_rows, axis=0)], "gather_mod")
    mod_all = mod_g.reshape(N_CHIPS, 2, half_rows, ada_cols).transpose(1, 2, 0, 3).reshape(N_DEV * bsz, N_MOD * D_MODEL)
    mod = lax.dynamic_slice_in_dim(mod_all, dev * bsz, bsz, axis=0)
    sh1, sc1, g1, sh2, sc2, g2 = [mod[:, i * D_MODEL:(i + 1) * D_MODEL].reshape(bsz, 1, D_MODEL) for i in range(N_MOD)]

    weights = (w_out, w_up, w_down)
    (out_part, up_part, down_part), started = _gather_start(
        [row_half(w) for w in weights], [gather_buffer(w) for w in weights], [mod_g], "gather_weights_start")

    h1, proj, qh, kh, vh = _in_proj_fused(x, pre_w_mix, sc1 + started[0:1, 0:1], sh1, w_in_full, tables)
    out_g = _gather_wait(out_part, [proj], "gather_out_wait")
    (attn_raw, cat, lse), ((out_g,),) = _attn_fwd(qh, kh, vh, attn_sinks, attn_out_w, biases, comms=[_plan_pair_forward([out_g])])
    up_g = _gather_wait(up_part, [attn_raw], "gather_up_wait")
    (o_raw, cat, states), ((up_g,),) = _hgrn_fwd(proj, lb_table, hg_norm_w, cat, chunk_masks, comms=[_plan_pair_forward([up_g])])
    down_g = _gather_wait(down_part, [o_raw], "gather_down_wait")
    w_out_full = out_g.reshape(D_MODEL, D_MODEL)
    w_up4 = up_g.reshape(N_CHIPS, D_MODEL, D_MODEL)
    mix, x1, h2 = _out_proj_fused(cat, w_out_full, x, post_w_mix, g1, pre_w_mlp, sc2, sh2)
    big_tm = min(ntok, 2048)
    up_spec = pl.BlockSpec((None, D_MODEL, D_MODEL), lambda i, j: (j, 0, 0))
    r, ((down_g,),) = _mm(flat(h2), w_up4, name="up_proj", out_dtype=BF16, tm=big_tm, tn=D_MODEL, n_out=D_FF, b_spec=up_spec,
                          epi=lambda acc: jnp.maximum(acc, 0.0), comms=[_plan_pair_forward([down_g])])
    w_down_full = down_g.reshape(D_FF, D_MODEL)
    square = lambda t: t * t
    loss_row, dy, dd, dg2, d_post_mlp = _down_proj_fused(unflat(r), w_down_full, x1, post_w_mlp, g2, loss_target)

    dpre = _mm(flat(dd), w_down_full, name="down_bwd", out_dtype=BF16, trans_b=True, tm=big_tm, tn=D_MODEL, extra=(r,),
               epi=lambda acc, rt: acc * (2.0 * rt.astype(F32)))
    half_rows = D_MODEL // 2
    g_down = _mm_tn(r, flat(dd), name="down_wgrad", tk=half_rows, tn=D_MODEL, a_fn=square,
                    out_shape=_sds((2, N_CHIPS, half_rows, D_MODEL), F32),
                    out_spec=pl.BlockSpec((None, None, half_rows, D_MODEL), lambda i, j: (i % 2, i // 2, 0, 0)))
    (dx1, dmix, dsc2, dsh2, dg1, d_pre_mlp, d_post_mix), ((q_down,),) = _up_bwd_fused(
        unflat(dpre), w_up4, dy, x1, mix, pre_w_mlp, sc2, post_w_mix, g1, comms=[_plan_pair([g_down], True)])
    g_up = _mm_tn(flat(h2), dpre, name="up_wgrad", tk=D_MODEL, tn=half_rows,
                  out_shape=_sds((2, N_CHIPS, half_rows, D_MODEL), F32),
                  out_spec=pl.BlockSpec((2, None, half_rows, half_rows), lambda i, j: (0, j // 2, 0, j % 2)))
    s_down = _pair_sum(g_down, q_down, core, "pair_sum_down")

    dcat, ((q_up,),) = _mm(flat(dmix), w_out_full, name="out_bwd", out_dtype=F32, trans_b=True, comms=[_plan_pair([g_up], True)])
    dcat = unflat(dcat)
    s_up = _pair_sum(g_up, q_up, core, "pair_sum_up")
    out_rows = D_MODEL // N_CHIPS
    g_out = _mm_tn(flat(cat), flat(dmix), name="out_wgrad", tk=2 * out_rows, tn=half_rows,
                   out_shape=_sds((2, N_CHIPS, out_rows, half_rows), F32),
                   out_spec=pl.BlockSpec((None, 2, out_rows, half_rows), lambda i, j: (j, i, 0, 0)))
    (dproj_rec, d_lb, d_hg_norm), ((x_down,), (q_out,)) = _hgrn_bwd(
        dcat, proj, o_raw, states, lb_table, hg_norm_w, chunk_masks, comms=[_plan_chip_exchange([s_down]), _plan_pair([g_out], True)])
    half_down = _sum_chips(s_down, x_down, chip_idx, "sum_chips_down")
    s_out = _pair_sum(g_out, q_out, core, "pair_sum_out")
    (dproj, d_attn_out, d_sinks), ((their_down,), (x_up,)) = _attn_bwd(
        dcat, attn_raw, attn_out_w, qh, kh, vh, lse, attn_sinks, tables, [bias.T for bias in biases], dproj_rec,
        comms=[_plan_pair([half_down], False), _plan_chip_exchange([s_up])])
    half_up = _sum_chips(s_up, x_up, chip_idx, "sum_chips_up")
    dproj = flat(dproj)
    in_rows = IN_COLS // N_CHIPS // 2
    g_in, ((x_out,),) = _mm_tn(dproj, flat(h1), name="in_wgrad", tk=2 * LANE, tn=D_MODEL, comms=[_plan_chip_exchange([s_out])])
    g_in = g_in.reshape(N_CHIPS, 2, in_rows, D_MODEL)
    half_out = _sum_chips(s_out, x_out, chip_idx, "sum_chips_out")
    dh1, ((q_in,), (their_up, their_out)) = _mm(
        dproj, w_in_full, name="in_bwd", out_dtype=F32,
        comms=[_plan_pair([g_in], "chip_major"), _plan_pair([half_up, half_out], False)])
    s_in = _pair_sum(g_in, q_in, core, "pair_sum_in", chip_major=True)
    in_sems, s_in, in_landing, started = _exchange_start(s_in, "exchange_in_start")
    grad_x, dsc1, dsh1, d_pre_mix = _norm1_bwd(unflat(dh1), dx1, x, pre_w_mix + started[0:1, 0:1], sc1)

    dmod = jnp.concatenate([dsh1, dsc1, dg1, dsh2, dsc2, dg2], axis=-1).reshape(bsz, N_MOD * D_MODEL)
    pack = _pack_partials(dmod, [d_pre_mix, d_post_mix, d_pre_mlp, d_post_mlp, d_attn_out, d_hg_norm, d_sinks], d_lb, loss_row)
    ((packs,),) = _comm_only([_plan_allgather8([pack])], "gather_small")
    w_small = dict(b_ada=b_ada, pre_w_mix=pre_w_mix, post_w_mix=post_w_mix, pre_w_mlp=pre_w_mlp, post_w_mlp=post_w_mlp,
                   attn_out_w=attn_out_w, hg_norm_w=hg_norm_w, attn_sinks=attn_sinks, lb_table=lb_table)
    m_small = dict(b_ada=m_b_ada, pre_w_mix=m_pre_w_mix, post_w_mix=m_post_w_mix, pre_w_mlp=m_pre_w_mlp, post_w_mlp=m_post_w_mlp,
                   attn_out_w=m_attn_out_w, hg_norm_w=m_hg_norm_w, attn_sinks=m_attn_sinks, lb_table=m_lb_table)
    v_small = dict(b_ada=v_b_ada, pre_w_mix=v_pre_w_mix, post_w_mix=v_post_w_mix, pre_w_mlp=v_pre_w_mlp, post_w_mlp=v_post_w_mlp,
                   attn_out_w=v_attn_out_w, hg_norm_w=v_hg_norm_w, attn_sinks=v_attn_sinks, lb_table=v_lb_table)
    *small_packed, loss_rows = _small_update(packs, _pack_small(w_small), _pack_small(m_small), _pack_small(v_small), bsz)
    small_out = [_unpack_small(p) for p in small_packed]
    loss = loss_rows[0, 0]

    dmod_all = packs[:, :bsz * MOD_ROWS, :].reshape(N_DEV * bsz, N_MOD * D_MODEL)
    dmod_cols = lax.dynamic_slice_in_dim(dmod_all, chip * ada_cols, ada_cols, axis=1)
    ada_out = _ada_bwd_adamw(c_all, dmod_cols, w_ada[0], m_w_ada[0], v_w_ada[0])

    s_in, x_in = _exchange_wait(in_sems, s_in, in_landing, [grad_x, ada_out[0]], "exchange_in_wait")
    half_in = _sum_chips(s_in, x_in, chip_idx, "sum_chips_in")
    ((their_in,),) = _comm_only([_plan_pair([half_in], False)], "pair_swap_in")
    big = dict(
        w_in=tuple(jnp.transpose(a) for a in _adamw_halves(half_in, their_in, core, w_in_t[0], m_in_t[0], v_in_t[0], axis=0,
                                                           name="adamw_in")),
        w_up=tuple(_adamw_halves(half_up, their_up, core, w_up[0], m_w_up[0], v_w_up[0], axis=0, name="adamw_up")),
        w_out=tuple(_adamw_halves(half_out, their_out, core, w_out[0], m_w_out[0], v_w_out[0], axis=1, name="adamw_out")),
        w_down=tuple(_adamw_halves(half_down, their_down, core, w_down[0], m_w_down[0], v_w_down[0], axis=0, name="adamw_down")),
        w_ada=tuple(ada_out),
    )
    order = ("w_ada", "b_ada", "pre_w_mix", "w_in", "attn_sinks", "attn_out_w", "lb_table", "hg_norm_w", "w_out", "post_w_mix",
             "pre_w_mlp", "w_up", "w_down", "post_w_mlp")
    outs = [loss, grad_x]
    for kind in range(4):
        for nm in order:
            outs.append(big[nm][kind][None] if nm in big else small_out[kind][nm])
    return tuple(outs)
```

```python
import jax
import jax.numpy as jnp
from jax import lax
from jax.experimental import pallas as pl
from jax.experimental.pallas import tpu as pltpu

F32 = jnp.float32
BF16 = jnp.bfloat16

D_MODEL = 1024
ATT_WIDTH = 512
ATT_HEAD_DIM = 64
ATT_Q_HEADS = 8
ATT_KV_HEADS = 2
ATT_GROUP = ATT_Q_HEADS // ATT_KV_HEADS
ATT_KV_COLS = ATT_KV_HEADS * ATT_HEAD_DIM
WINDOW = 128
ROPE_DIM = 16
ROPE_THETA = 500000.0
HG_WIDTH = 512
MIX_WIDTH = ATT_WIDTH + HG_WIDTH
HG_HEAD_DIM = 128
HG_HEADS = 4
HG_CHUNK = 32
IN_COLS = ATT_WIDTH + 2 * ATT_KV_COLS + 4 * HG_WIDTH
ATT_COLS = ATT_WIDTH + 2 * ATT_KV_COLS
D_FF = 4 * D_MODEL
N_MOD = 6
EPS = 1e-6
ATT_SCALE = ATT_HEAD_DIM ** -0.5

ADAM_LR = 0.001
ADAM_B1 = 0.9
ADAM_B2 = 0.999
ADAM_EPS = 1e-08
ADAM_WD = 0.01
ADAM_STEP = 10

N_CHIPS = 4
N_DEV = 8
LANE = 128
VMEM_LIMIT = 48 * 1024 * 1024
VMEM_LIMIT_BIG = 58 * 1024 * 1024
MESH = pl.DeviceIdType.MESH

NT_DIMS = (((1,), (1,)), ((), ()))
TN_DIMS = (((0,), (0,)), ((), ()))


def _sds(shape, dtype):
    return jax.ShapeDtypeStruct(tuple(shape), dtype)


def _params(*sem, vmem_limit=None):
    return pltpu.CompilerParams(dimension_semantics=sem, vmem_limit_bytes=VMEM_LIMIT if vmem_limit is None else vmem_limit)


def _sigmoid(x):
    return 1.0 / (1.0 + jnp.exp(-x))


def _dot(a, b, dims=None):
    a, b = a.astype(BF16), b.astype(BF16)
    if dims is None:
        return jnp.dot(a, b, preferred_element_type=F32)
    return lax.dot_general(a, b, dims, preferred_element_type=F32)


def _rms_fwd(x, w):
    rstd = lax.rsqrt(jnp.mean(x * x, axis=-1, keepdims=True) + EPS)
    xh = x * rstd
    return xh * w, xh, rstd


def _rms_bwd(dy, xh, rstd, w):
    dxh = dy * w
    dx = rstd * (dxh - xh * jnp.mean(dxh * xh, axis=-1, keepdims=True))
    return dx, dy * xh


def _colsum(x):
    return jnp.sum(x, axis=0, keepdims=True)


def _rms_hat(x):
    rstd = lax.rsqrt(jnp.mean(x * x, axis=-1, keepdims=True) + EPS)
    return x * rstd, rstd


def _rms_bwd_gain(dy, gain, xh, rstd):
    dxh = dy * gain
    dx = rstd * (dxh - xh * jnp.mean(dxh * xh, axis=-1, keepdims=True))
    return dx, _colsum(dy * xh)


def _row_tile(rows, cap=256):
    return max(t for t in range(16, cap + 1, 16) if rows % t == 0)


HBM_SPEC = pl.BlockSpec(memory_space=pltpu.HBM)


def _mesh_pos():
    return lax.axis_index("x"), lax.axis_index("y"), lax.axis_index("c")


class _Comm:
    def __init__(self, ins, outs, sems, start, finish, aliases=()):
        self.ins, self.outs, self.sems = list(ins), list(outs), list(sems)
        self.start, self.finish, self.aliases = start, finish, tuple(aliases)


def _call(body, args, *, name, grid, in_specs, out_specs, out_shape, sem, scratch_shapes=(), comms=(), aliases=None,
          vmem_limit=None):
    scratch_shapes = list(scratch_shapes)
    if not comms:
        return pl.pallas_call(body, name=name, grid=grid, in_specs=in_specs, out_specs=out_specs, out_shape=out_shape,
                              input_output_aliases=dict(aliases or {}), scratch_shapes=scratch_shapes,
                              compiler_params=_params(*sem, vmem_limit=vmem_limit))(*args)
    single = not isinstance(out_shape, (list, tuple))
    out_specs_l = [out_specs] if single else list(out_specs)
    out_shape_l = [out_shape] if single else list(out_shape)
    n_in, n_out, n_scr = len(in_specs), len(out_shape_l), len(scratch_shapes)
    n_ci = [len(cm.ins) for cm in comms]
    n_co = [len(cm.outs) for cm in comms]
    n_cs = [len(cm.sems) for cm in comms]
    aliases = dict(aliases or {})
    for k, cm in enumerate(comms):
        for i, o in cm.aliases:
            aliases[n_in + sum(n_ci[:k]) + i] = n_out + sum(n_co[:k]) + o

    def fused(*refs):
        pos = [0]

        def take(n):
            part = refs[pos[0]:pos[0] + n]
            pos[0] += n
            return part

        ins = take(n_in)
        c_ins = [take(n) for n in n_ci]
        outs = take(n_out)
        c_outs = [take(n) for n in n_co]
        scr = take(n_scr)
        c_sems = [take(n) for n in n_cs]
        first, last = True, True
        for d, size in enumerate(grid):
            first = jnp.logical_and(first, pl.program_id(d) == 0)
            last = jnp.logical_and(last, pl.program_id(d) == size - 1)

        def run(which):
            for cm, ci, co, cs in zip(comms, c_ins, c_outs, c_sems):
                getattr(cm, which)(ci, co, cs)

        if grid:
            pl.when(first)(lambda: run("start"))
        else:
            run("start")
        body(*ins, *outs, *scr)
        if grid:
            pl.when(last)(lambda: run("finish"))
        else:
            run("finish")

    res = pl.pallas_call(
        fused, name=name, grid=grid, in_specs=list(in_specs) + [HBM_SPEC] * sum(n_ci),
        out_specs=out_specs_l + [HBM_SPEC] * sum(n_co), out_shape=out_shape_l + [s for cm in comms for s in cm.outs],
        input_output_aliases=aliases, scratch_shapes=scratch_shapes + [s for cm in comms for s in cm.sems],
        compiler_params=_params(*["arbitrary"] * len(grid), vmem_limit=vmem_limit),
    )(*args, *[a for cm in comms for a in cm.ins])
    main = res[:n_out]
    extra, at = [], n_out
    for n in n_co:
        extra.append(list(res[at:at + n]))
        at += n
    return (main[0] if single else list(main)), extra


def _mm(a, b, *, name, out_dtype, trans_b=False, tm=512, tn=None, extra=(), epi=None, b_spec=None, n_out=None, comms=()):
    m_total, k_total = a.shape
    if n_out is None:
        n_out = b.shape[0] if trans_b else b.shape[1]
    tn = n_out if tn is None else tn
    grid = (m_total // tm, n_out // tn)
    dims = NT_DIMS if trans_b else None

    def body(*refs):
        a_ref, b_ref = refs[0], refs[1]
        extra_refs = refs[2:2 + len(extra)]
        o_ref = refs[2 + len(extra)]
        acc = _dot(a_ref[...], b_ref[...], dims)
        if epi is not None:
            acc = epi(acc, *[r[...] for r in extra_refs])
        o_ref[...] = acc.astype(out_dtype)

    if b_spec is None:
        if trans_b:
            b_spec = pl.BlockSpec((tn, k_total), lambda i, j: (j, 0))
        else:
            b_spec = pl.BlockSpec((k_total, tn), lambda i, j: (0, j))
    in_specs = [pl.BlockSpec((tm, k_total), lambda i, j: (i, 0)), b_spec]
    in_specs += [pl.BlockSpec((tm, tn), lambda i, j: (i, j)) for _ in extra]
    return _call(
        body, (a, b, *extra), name=name, grid=grid, in_specs=in_specs,
        out_specs=pl.BlockSpec((tm, tn), lambda i, j: (i, j)),
        out_shape=_sds((m_total, n_out), out_dtype),
        sem=("parallel", "parallel"), comms=comms)


def _mm_tn(a, b, *, name, tk, tn, a_fn=None, out_shape=None, out_spec=None, comms=()):
    m_total, k_total = a.shape
    n_total = b.shape[1]
    grid = (k_total // tk, n_total // tn)

    def body(a_ref, b_ref, o_ref):
        av = a_ref[...]
        part = _dot(av if a_fn is None else a_fn(av), b_ref[...], TN_DIMS)
        o_ref[...] = part.reshape(o_ref.shape)

    if out_shape is None:
        out_shape = _sds((k_total, n_total), F32)
        out_spec = pl.BlockSpec((tk, tn), lambda i, j: (i, j))
    return _call(
        body, (a, b), name=name, grid=grid,
        in_specs=[pl.BlockSpec((m_total, tk), lambda i, j: (0, i)), pl.BlockSpec((m_total, tn), lambda i, j: (0, j))],
        out_specs=out_spec, out_shape=out_shape, sem=("parallel", "parallel"), comms=comms)


def _ada_fwd(c_all, w_shard, b_shard):
    nb, ncol = c_all.shape[0], w_shard.shape[1]
    tn = 512

    def body(c_ref, w_ref, b_ref, o_ref):
        c = c_ref[...]
        o_ref[...] = _dot(c * _sigmoid(c), w_ref[...]) + b_ref[...]

    return pl.pallas_call(
        body, name="ada_fwd", grid=(ncol // tn,),
        in_specs=[pl.BlockSpec((nb, D_MODEL), lambda j: (0, 0)), pl.BlockSpec((D_MODEL, tn), lambda j: (0, j)),
                  pl.BlockSpec((1, tn), lambda j: (0, j))],
        out_specs=pl.BlockSpec((nb, tn), lambda j: (0, j)), out_shape=_sds((nb, ncol), F32),
        compiler_params=_params("parallel"),
    )(c_all, w_shard, b_shard)


def _adamw_math(g, w, m, v):
    m = ADAM_B1 * m + (1.0 - ADAM_B1) * g
    v = ADAM_B2 * v + (1.0 - ADAM_B2) * (g * g)
    m_hat = m / (1.0 - ADAM_B1 ** ADAM_STEP)
    v_hat = v / (1.0 - ADAM_B2 ** ADAM_STEP)
    delta = -ADAM_LR * (m_hat / (jnp.sqrt(v_hat) + ADAM_EPS) + ADAM_WD * w)
    return delta, m, v


def _ada_bwd_adamw(c_all, dmod_cols, w, m, v):
    nb, ncol = dmod_cols.shape
    tn = 256

    def body(c_ref, d_ref, w_ref, m_ref, v_ref, g_ref, dl_ref, nm_ref, nv_ref):
        c = c_ref[...]
        g = _dot(c * _sigmoid(c), d_ref[...], TN_DIMS)
        g_ref[...] = g
        dl_ref[...], nm_ref[...], nv_ref[...] = _adamw_math(g, w_ref[...], m_ref[...], v_ref[...])

    col = pl.BlockSpec((D_MODEL, tn), lambda j: (0, j))
    shp = _sds((D_MODEL, ncol), F32)
    return pl.pallas_call(
        body, name="ada_bwd_adamw", grid=(ncol // tn,),
        in_specs=[pl.BlockSpec((nb, D_MODEL), lambda j: (0, 0)), pl.BlockSpec((nb, tn), lambda j: (0, j)), col, col, col],
        out_specs=[col, col, col, col], out_shape=[shp, shp, shp, shp],
        compiler_params=_params("parallel"),
    )(c_all, dmod_cols, w, m, v)


def _adamw_halves(own, theirs, core, w, m, v, *, axis, name):
    r2, c2 = own.shape
    tr = _row_tile(r2)
    nt = r2 // tr

    def body(core_ref, own_ref, their_ref, w_ref, m_ref, v_ref, g_ref, dl_ref, nm_ref, nv_ref):
        g = jnp.where(pl.program_id(0) == core_ref[0], own_ref[...], their_ref[...])
        g_ref[...] = g
        dl_ref[...], nm_ref[...], nv_ref[...] = _adamw_math(g, w_ref[...], m_ref[...], v_ref[...])

    if axis == 0:
        full = pl.BlockSpec((tr, c2), lambda h, i, core_ref: (h * nt + i, 0))
    else:
        full = pl.BlockSpec((tr, c2), lambda h, i, core_ref: (i, h))
    half = pl.BlockSpec((tr, c2), lambda h, i, core_ref: (i, 0))
    shp = _sds(w.shape, F32)
    return pl.pallas_call(
        body, name=name,
        grid_spec=pltpu.PrefetchScalarGridSpec(num_scalar_prefetch=1, grid=(2, nt), in_specs=[half, half, full, full, full],
                                               out_specs=[full] * 4),
        out_shape=[shp] * 4, compiler_params=_params("parallel", "parallel"),
    )(core, own, theirs, w, m, v)


def _tok_spec(tm, width=D_MODEL):
    return pl.BlockSpec((None, tm, width), lambda b, i: (b, i, 0))


def _row_spec(width=D_MODEL):
    return pl.BlockSpec((None, 1, width), lambda b, i: (b, 0, 0))


def _vec_spec(width=D_MODEL):
    return pl.BlockSpec((1, width), lambda b, i: (0, 0))


class _RowsOf:
    def __init__(self, ref, first, count):
        self.ref, self.rows = ref, slice(first, first + count)

    def __getitem__(self, idx):
        return self.ref[self.rows, :]

    def __setitem__(self, idx, value):
        self.ref[self.rows, :] = value


def _mm_rows(a, b, *, name, tm, extra, extra_specs, out_specs, out_shape, epi, pro=None, trans_b=False, b_chunks=1, comms=(),
             parts=1, zero_per_seq=(), zero_once=(), vmem_limit=None):
    bsz, seq, k_total = a.shape
    kc = k_total // b_chunks
    dims = NT_DIMS if trans_b else None
    rows = tm // parts

    def body(*refs):
        a_ref, b_ref = refs[0], refs[1]
        ex, outs = refs[2:2 + len(extra)], refs[2 + len(extra):]
        if zero_per_seq:
            @pl.when(pl.program_id(1) == 0)
            def _():
                for k in zero_per_seq:
                    outs[k][...] = jnp.zeros_like(outs[k])
        if zero_once:
            @pl.when(jnp.logical_and(pl.program_id(0) == 0, pl.program_id(1) == 0))
            def _():
                for k in zero_once:
                    outs[k][...] = jnp.zeros_like(outs[k])

        def part_of(ref, p):
            tiled = len(ref.shape) == 2 and ref.shape[0] == tm
            return _RowsOf(ref, p * rows, rows) if tiled and parts > 1 else ref

        accs = []
        for p in range(parts):
            a_p, ex_p, outs_p = part_of(a_ref, p), [part_of(r, p) for r in ex], [part_of(r, p) for r in outs]
            if b_chunks == 1:
                accs.append(_dot(a_p[...] if pro is None else pro(a_p, ex_p, outs_p), b_ref[...], dims))
            else:
                acc = _dot(a_p[...][:, 0:kc], b_ref[0], NT_DIMS)
                for k in range(1, b_chunks):
                    acc = acc + _dot(a_p[...][:, k * kc:(k + 1) * kc], b_ref[k], NT_DIMS)
                accs.append(acc)
        for p in range(parts):
            epi(accs[p], [part_of(r, p) for r in ex], [part_of(r, p) for r in outs])

    b_spec = pl.BlockSpec(b.shape, lambda bb, i: (0,) * b.ndim)
    return _call(
        body, (a, b, *extra), name=name, grid=(bsz, seq // tm), in_specs=[_tok_spec(tm, k_total), b_spec, *extra_specs],
        out_specs=out_specs, out_shape=out_shape, sem=("arbitrary", "arbitrary"), comms=comms, vmem_limit=vmem_limit)


def _in_proj_fused(x, w, sc, sh, w_in_t, tables, comms=()):
    tm = 512
    bsz, seq, _ = x.shape
    half = ROPE_DIM // 2
    heads_per_slab = LANE // ATT_HEAD_DIM

    def pro(x_ref, ex, outs):
        y, _, _ = _rms_fwd(x_ref[...], ex[0][...])
        h = (y * (1.0 + ex[1][...]) + ex[2][...]).astype(BF16)
        outs[0][...] = h
        return h

    def epi(acc, ex, outs):
        c, u, d = ex[3][...], ex[4][...], ex[5][...]
        _, rec_ref, q_ref, k_ref, v_ref = outs
        for k in range(HG_SLABS):
            rec_ref[k] = acc[:, ATT_COLS + k * HG_WIDTH:ATT_COLS + (k + 1) * HG_WIDTH]

        def rope(z):
            return (z * c + pltpu.roll(z, half, 1) * u + pltpu.roll(z, LANE - half, 1) * d).astype(BF16)

        for s in range(ATT_WIDTH // LANE):
            slab = rope(acc[:, s * LANE:(s + 1) * LANE])
            for part in range(heads_per_slab):
                g, hh = divmod(s * heads_per_slab + part, ATT_GROUP)
                piece = slab[:, part * ATT_HEAD_DIM:(part + 1) * ATT_HEAD_DIM]
                for blk in range(tm // WINDOW):
                    q_ref[blk, g, hh * WINDOW:(hh + 1) * WINDOW, :] = piece[blk * WINDOW:(blk + 1) * WINDOW]
        rk = rope(acc[:, ATT_WIDTH:ATT_WIDTH + LANE])
        vv = acc[:, ATT_WIDTH + LANE:ATT_COLS].astype(BF16)
        for g in range(ATT_KV_HEADS):
            k_ref[g] = rk[:, g * ATT_HEAD_DIM:(g + 1) * ATT_HEAD_DIM]
            v_ref[g] = vv[:, g * ATT_HEAD_DIM:(g + 1) * ATT_HEAD_DIM]

    tab = pl.BlockSpec((tm, LANE), lambda b, i: (i, 0))
    kv_spec = pl.BlockSpec((None, ATT_KV_HEADS, tm, ATT_HEAD_DIM), lambda b, i: (b, 0, i, 0))
    kv_shape = _sds((bsz, ATT_KV_HEADS, seq, ATT_HEAD_DIM), BF16)
    q_spec = pl.BlockSpec((None, tm // WINDOW, ATT_KV_HEADS, GROUP_ROWS, ATT_HEAD_DIM), lambda b, i: (b, i, 0, 0, 0))
    return _mm_rows(x, w_in_t, name="in_proj", tm=tm, extra=(w, sc, sh, *tables),
                    extra_specs=[_vec_spec(), _row_spec(), _row_spec(), tab, tab, tab],
                    out_specs=[_tok_spec(tm), pl.BlockSpec((None, HG_SLABS, tm, HG_WIDTH), lambda b, i: (b, 0, i, 0)), q_spec,
                               kv_spec, kv_spec],
                    out_shape=[_sds(x.shape, BF16), _sds((bsz, HG_SLABS, seq, HG_WIDTH), F32),
                               _sds((bsz, seq // WINDOW, ATT_KV_HEADS, GROUP_ROWS, ATT_HEAD_DIM), BF16), kv_shape, kv_shape],
                    pro=pro, epi=epi, trans_b=True, comms=comms)


def _rope_tables(seq):
    half = ROPE_DIM // 2
    inv_freq = ROPE_THETA ** (-jnp.arange(0, ROPE_DIM, 2, dtype=F32) / ROPE_DIM)
    ang = jnp.arange(seq, dtype=F32)[:, None] * inv_freq[None, :]
    cos, sin = jnp.cos(ang), jnp.sin(ang)
    rest = ATT_HEAD_DIM - ROPE_DIM
    ones, zeros, zh = jnp.ones((seq, rest), F32), jnp.zeros((seq, rest), F32), jnp.zeros((seq, half), F32)
    reps = LANE // ATT_HEAD_DIM
    t_cos = jnp.tile(jnp.concatenate([cos, cos, ones], axis=1), (1, reps))
    t_up = jnp.tile(jnp.concatenate([zh, sin, zeros], axis=1), (1, reps))
    t_dn = jnp.tile(jnp.concatenate([-sin, zh, zeros], axis=1), (1, reps))
    return t_cos, t_up, t_dn


GROUP_ROWS = ATT_GROUP * WINDOW


ATT_BPS = 2


MASKED = -1e30


def _band_biases():
    row = jnp.arange(GROUP_ROWS)[:, None] % WINDOW
    col = jnp.arange(2 * WINDOW)[None, :]
    own = jnp.logical_and(col >= WINDOW, col - WINDOW <= row)
    before = jnp.logical_and(col < WINDOW, col > row)
    return (jnp.where(jnp.logical_or(own, before), 0.0, MASKED).astype(F32), jnp.where(own, 0.0, MASKED).astype(F32))


def _band_bias(full_ref, first_ref, has_prev):
    return full_ref[...] if has_prev is True else jnp.where(has_prev, full_ref[...], first_ref[...])


def _sink_column(sink_ref, g):
    head = lax.broadcasted_iota(jnp.int32, (GROUP_ROWS, 1), 0) // WINDOW
    col = jnp.full((GROUP_ROWS, 1), sink_ref[0, g * ATT_GROUP], F32)
    for hh in range(1, ATT_GROUP):
        col = jnp.where(head == hh, sink_ref[0, g * ATT_GROUP + hh], col)
    return col


def _sink_row(sink_ref, g):
    return jnp.concatenate([jnp.full((1, WINDOW), sink_ref[0, g * ATT_GROUP + hh], F32) for hh in range(ATT_GROUP)], axis=1)


def _bias_spec(transposed=False):
    shape = (2 * WINDOW, GROUP_ROWS) if transposed else (GROUP_ROWS, 2 * WINDOW)
    return pl.BlockSpec(shape, lambda b, i: (0, 0))


def _attn_specs():
    q_spec = pl.BlockSpec((None, ATT_BPS, ATT_KV_HEADS, GROUP_ROWS, ATT_HEAD_DIM), lambda b, i: (b, i, 0, 0, 0))
    kv_cur = pl.BlockSpec((None, ATT_KV_HEADS, ATT_BPS * WINDOW, ATT_HEAD_DIM), lambda b, i: (b, 0, i, 0))
    kv_prev = pl.BlockSpec((None, ATT_KV_HEADS, WINDOW, ATT_HEAD_DIM), lambda b, i: (b, 0, jnp.maximum(ATT_BPS * i - 1, 0), 0))
    return q_spec, kv_cur, kv_prev


def _band(prev_ref, cur_ref, g, blk):
    own = cur_ref[g, blk * WINDOW:(blk + 1) * WINDOW]
    before = prev_ref[g] if blk == 0 else cur_ref[g, (blk - 1) * WINDOW:blk * WINDOW]
    return jnp.concatenate([before, own], axis=0)


def _attn_fwd(qh, kh, vh, sinks, w_norm, biases, comms=()):
    bsz, nblk = qh.shape[0], qh.shape[1]
    seq = nblk * WINDOW
    rows = ATT_BPS * WINDOW

    def body(sink_ref, q_ref, kc_ref, kp_ref, vc_ref, vp_ref, w_ref, full_ref, first_ref, raw_ref, an_ref, l_ref):
        l_ref[...] = jnp.zeros_like(l_ref)
        def block(blk):
            bias = _band_bias(full_ref, first_ref, True if blk else pl.program_id(1) > 0)
            groups = range(ATT_KV_HEADS)
            keys, vals = [_band(kp_ref, kc_ref, g, blk) for g in groups], [_band(vp_ref, vc_ref, g, blk) for g in groups]
            sink = [_sink_column(sink_ref, g) for g in groups]
            s = [_dot(q_ref[blk, g], keys[g], NT_DIMS) * ATT_SCALE + bias for g in groups]
            yield
            m = [jnp.maximum(jnp.max(s[g], axis=-1, keepdims=True), sink[g]) for g in groups]
            p = [jnp.exp(s[g] - m[g]) for g in groups]
            den = [jnp.sum(p[g], axis=-1, keepdims=True) + jnp.exp(sink[g] - m[g]) for g in groups]
            yield
            o = [_dot(p[g] / den[g], vals[g]) for g in groups]
            yield
            lse = [m[g] + jnp.log(den[g]) for g in groups]
            tok = slice(blk * WINDOW, (blk + 1) * WINDOW)
            for g in groups:
                for hh in range(ATT_GROUP):
                    h = g * ATT_GROUP + hh
                    raw_ref[tok, h * ATT_HEAD_DIM:(h + 1) * ATT_HEAD_DIM] = o[g][hh * WINDOW:(hh + 1) * WINDOW]
                    l_ref[tok, h:h + 1] = lse[g][hh * WINDOW:(hh + 1) * WINDOW]

        _in_step(block(blk) for blk in range(ATT_BPS))
        y, _, _ = _rms_fwd(raw_ref[...], w_ref[...])
        an_ref[...] = y.astype(BF16)

    cur = lambda width: pl.BlockSpec((None, rows, width), lambda b, i: (b, i, 0))
    q_spec, kv_cur, kv_prev = _attn_specs()
    return _call(
        body, (sinks, qh, kh, kh, vh, vh, w_norm, *biases), name="attn_fwd", grid=(bsz, nblk // ATT_BPS),
        in_specs=[pl.BlockSpec(memory_space=pltpu.SMEM), q_spec, kv_cur, kv_prev, kv_cur, kv_prev, _vec_spec(ATT_WIDTH),
                  _bias_spec(), _bias_spec()],
        out_specs=[cur(ATT_WIDTH), cur(ATT_WIDTH), cur(LANE)],
        out_shape=[_sds((bsz, seq, ATT_WIDTH), F32), _sds((bsz, seq, MIX_WIDTH), BF16), _sds((bsz, seq, LANE), F32)],
        sem=("parallel", "parallel"), comms=comms)


HG_Q0 = ATT_COLS // LANE
HG_F0 = HG_Q0 + HG_HEADS
HG_I0 = HG_F0 + HG_HEADS
HG_G0 = HG_I0 + HG_HEADS
HG_SLABS = 4
HG_Q, HG_F, HG_I, HG_G = range(HG_SLABS)
HG_TOK = 256
HG_NCH = HG_TOK // HG_CHUNK
HG_HPS = 2
HG_FWD_BLOCKS = 4
HG_BWD_BLOCKS = 4


def _block_masks():
    row = jnp.arange(HG_TOK)[:, None]
    col = jnp.arange(HG_TOK)[None, :]
    same = (row // HG_CHUNK) == (col // HG_CHUNK)
    return jnp.logical_and(same, col <= row).astype(F32), jnp.logical_and(same, col >= row).astype(F32)


def _row_in_chunk():
    return lax.broadcasted_iota(jnp.int32, (HG_TOK, LANE), 0) % HG_CHUNK


def _chunk_cumsum(x, reverse=False):
    ric = _row_in_chunk()
    shift = 1
    while shift < HG_CHUNK:
        if reverse:
            x = x + jnp.where(ric < HG_CHUNK - shift, pltpu.roll(x, HG_TOK - shift, 0), 0.0)
        else:
            x = x + jnp.where(ric >= shift, pltpu.roll(x, shift, 0), 0.0)
        shift *= 2
    return x


def _chunk_rows(rows):
    stacked = jnp.concatenate([r[None] for r in rows], axis=0)
    return jnp.broadcast_to(stacked, (HG_NCH, HG_CHUNK, LANE)).reshape(HG_TOK, LANE)


def _chunk_slices(x):
    return [x[j * HG_CHUNK:(j + 1) * HG_CHUNK] for j in range(HG_NCH)]


def _in_step(stages):
    stages = list(stages)
    while stages:
        stages = [g for g in stages if next(g, stages) is not stages]


def _hgrn_common(tbl, hf, hq):
    lb = _sigmoid(tbl[1:2] - tbl[0:1])
    sig = _sigmoid(hf)
    f = lb + (1.0 - lb) * sig
    sq = _sigmoid(hq)
    q, k = hq * sq, 1.0 - f
    b = _chunk_cumsum(jnp.log(f))
    last = [b[(j + 1) * HG_CHUNK - 1:(j + 1) * HG_CHUNK] for j in range(HG_NCH)]
    bl = _chunk_rows(last)
    e_b, e_nb, e_rem = jnp.exp(b), jnp.exp(-b), jnp.exp(bl - b)
    e_last = [jnp.exp(r) for r in last]
    return dict(lb=lb, sig=sig, f=f, sq=sq, q=q, k=k, e_b=e_b, e_nb=e_nb, e_rem=e_rem, e_last=e_last,
                qd=q * e_b, kd=k * e_nb, ku=k * e_rem)


def _hgrn_fwd(proj, lb_table, norm_w, mix_in, masks, comms=()):
    bsz, _, seq, _ = proj.shape
    nstep = seq // HG_TOK

    def body(tbl_ref, nw_ref, p_ref, mix_ref, lower_ref, o_ref, rec_ref, st_ref, s_scr):
        @pl.when(pl.program_id(2) == 0)
        def _():
            s_scr[...] = jnp.zeros_like(s_scr)

        lower = lower_ref[...]

        def head(hp, sub):
            ls = slice(hp * LANE, (hp + 1) * LANE)
            rows = slice(sub * HG_TOK, (sub + 1) * HG_TOK)
            v, hg = p_ref[HG_I, rows, ls], p_ref[HG_G, rows, ls]
            t = _hgrn_common(tbl_ref[:, ls], p_ref[HG_F, rows, ls], p_ref[HG_Q, rows, ls])
            yield
            a = _dot(t["qd"], t["kd"], NT_DIMS) * lower
            o_intra = _dot(a, v)
            yield
            v_c, ku_c, qd_c = [_chunk_slices(z.astype(BF16)) for z in (v, t["ku"], t["qd"])]
            updates = [_dot(v_c[j], ku_c[j], TN_DIMS) for j in range(HG_NCH)]
            yield
            st = s_scr[hp]
            states = []
            for j in range(HG_NCH):
                states.append(st)
                st = st * t["e_last"][j] + updates[j]
            s_scr[hp] = st
            yield
            o = o_intra + jnp.concatenate([_dot(qd_c[j], states[j], NT_DIMS) for j in range(HG_NCH)], axis=0)
            yield
            st_ref[hp, sub] = states[0]
            o_ref[rows, ls] = o
            y, _, _ = _rms_fwd(o, nw_ref[...])
            rec_ref[rows, ls] = (y * (hg * _sigmoid(hg))).astype(BF16)

        for sub in range(HG_FWD_BLOCKS):
            _in_step(head(hp, sub) for hp in range(HG_HPS))

    width = HG_HPS * LANE
    tok = HG_FWD_BLOCKS * HG_TOK
    head_out = pl.BlockSpec((None, tok, width), lambda b, h, t: (b, t, h))
    mix_out = pl.BlockSpec((None, tok, width), lambda b, h, t: (b, t, ATT_WIDTH // width + h))
    return _call(
        body, (lb_table, norm_w, proj, mix_in, masks[0]), name="hgrn_fwd",
        grid=(bsz, HG_HEADS // HG_HPS, nstep // HG_FWD_BLOCKS),
        in_specs=[pl.BlockSpec((2, width), lambda b, h, t: (0, h)), pl.BlockSpec((1, LANE), lambda b, h, t: (0, 0)),
                  pl.BlockSpec((None, HG_SLABS, tok, width), lambda b, h, t: (b, 0, t, h)), pl.BlockSpec(memory_space=pl.ANY),
                  pl.BlockSpec((HG_TOK, HG_TOK), lambda b, h, t: (0, 0))],
        out_specs=[head_out, mix_out,
                   pl.BlockSpec((None, HG_HPS, HG_FWD_BLOCKS, LANE, LANE), lambda b, h, t: (b, h, t, 0, 0))],
        out_shape=[_sds((bsz, seq, HG_WIDTH), F32), _sds(mix_in.shape, BF16),
                   _sds((bsz, HG_HEADS, nstep, LANE, LANE), F32)],
        scratch_shapes=[pltpu.VMEM((HG_HPS, LANE, LANE), F32)],
        sem=("parallel", "parallel", "arbitrary"), comms=comms, aliases={3: 1})


def _out_proj_fused(cat, w_out, x, post_w, g1, pre_w, sc2, sh2):
    tm = 512

    def epi(mix, ex, outs):
        x_ref, pw_ref, g1_ref, w2_ref, sc_ref, sh_ref = ex
        outs[0][...] = mix
        n1, _, _ = _rms_fwd(mix, pw_ref[...])
        x1 = x_ref[...] + g1_ref[...] * n1
        outs[1][...] = x1
        y2, _, _ = _rms_fwd(x1, w2_ref[...])
        outs[2][...] = (y2 * (1.0 + sc_ref[...]) + sh_ref[...]).astype(BF16)

    return _mm_rows(cat, w_out, name="out_proj", tm=tm, extra=(x, post_w, g1, pre_w, sc2, sh2),
                    extra_specs=[_tok_spec(tm), _vec_spec(), _row_spec(), _vec_spec(), _row_spec(), _row_spec()],
                    out_specs=[_tok_spec(tm), _tok_spec(tm), _tok_spec(tm)],
                    out_shape=[_sds(x.shape, F32), _sds(x.shape, F32), _sds(x.shape, BF16)], epi=epi)


def _acc_out(ref, first, value):
    @pl.when(first)
    def _():
        ref[...] = value

    @pl.when(jnp.logical_not(first))
    def _():
        ref[...] += value


def _down_proj_fused(r, w_down, x1, post_w, g2, target):
    tm = 512
    bsz = x1.shape[0]

    def pro(r_ref, ex, outs):
        rv = r_ref[...]
        return rv * rv

    def epi(down, ex, outs):
        x1_ref, w_ref, g2_ref, t_ref = ex
        loss_ref, dy_ref, dd_ref, dg2_ref, dw_ref = outs
        w, g2v = w_ref[...], g2_ref[...]
        gain = g2v * w
        dh, rstd = _rms_hat(down)
        err = x1_ref[...] + dh * gain - t_ref[...]
        part = (0.5 / D_MODEL) * jnp.sum(jnp.sum(err * err, axis=-1, keepdims=True), axis=0, keepdims=True)
        loss_ref[...] += jnp.broadcast_to(part, (1, LANE))
        dy = err * (1.0 / D_MODEL)
        dy_ref[...] = dy
        dd, per_col = _rms_bwd_gain(dy, gain, dh, rstd)
        dd_ref[...] = dd.astype(BF16)
        dg2_ref[...] += per_col * w
        dw_ref[...] += per_col * g2v

    return _mm_rows(r, w_down, name="down_proj", tm=tm, extra=(x1, post_w, g2, target),
                    extra_specs=[_tok_spec(tm), _vec_spec(), _row_spec(), _tok_spec(tm)],
                    out_specs=[_vec_spec(LANE), _tok_spec(tm), _tok_spec(tm), _row_spec(), _vec_spec()],
                    out_shape=[_sds((1, LANE), F32), _sds(x1.shape, F32), _sds(x1.shape, BF16), _sds((bsz, 1, D_MODEL), F32),
                               _sds((1, D_MODEL), F32)], pro=pro, epi=epi, parts=2, zero_per_seq=(3,), zero_once=(0, 4),
                    vmem_limit=VMEM_LIMIT_BIG)


def _up_bwd_fused(dpre, w_up4, dy, x1, mix, pre_w, sc2, post_w, g1, comms=()):
    tm = 512
    bsz = x1.shape[0]

    def epi(dh2v, ex, outs):
        dy_ref, x1_ref, mix_ref, w2_ref, sc_ref, pw_ref, g1_ref = ex
        dx1_ref, dmix_ref, dsc_ref, dsh_ref, dg1_ref, dw2_ref, dpw_ref = outs
        w2, pw, g1v = w2_ref[...], pw_ref[...], g1_ref[...]
        mod2 = 1.0 + sc_ref[...]
        xh2, rstd2 = _rms_hat(x1_ref[...])
        dsh_ref[...] += _colsum(dh2v)
        dx1n, per_col2 = _rms_bwd_gain(dh2v, mod2 * w2, xh2, rstd2)
        dsc_ref[...] += per_col2 * w2
        dw2_ref[...] += per_col2 * mod2
        dx1 = dy_ref[...] + dx1n
        dx1_ref[...] = dx1
        mh, rstd1 = _rms_hat(mix_ref[...])
        dmix, per_col1 = _rms_bwd_gain(dx1, g1v * pw, mh, rstd1)
        dmix_ref[...] = dmix.astype(BF16)
        dg1_ref[...] += per_col1 * pw
        dpw_ref[...] += per_col1 * g1v

    row_shape = _sds((bsz, 1, D_MODEL), F32)
    vec_shape = _sds((1, D_MODEL), F32)
    return _mm_rows(dpre, w_up4, name="up_bwd", tm=tm, extra=(dy, x1, mix, pre_w, sc2, post_w, g1),
                    extra_specs=[_tok_spec(tm), _tok_spec(tm), _tok_spec(tm), _vec_spec(), _row_spec(), _vec_spec(), _row_spec()],
                    out_specs=[_tok_spec(tm), _tok_spec(tm), _row_spec(), _row_spec(), _row_spec(), _vec_spec(), _vec_spec()],
                    out_shape=[_sds(x1.shape, F32), _sds(x1.shape, BF16), row_shape, row_shape, row_shape, vec_shape, vec_shape],
                    epi=epi, b_chunks=w_up4.shape[0], comms=comms, parts=2, zero_per_seq=(2, 3, 4), zero_once=(5, 6),
                    vmem_limit=VMEM_LIMIT_BIG)


def _norm1_bwd(dh1, dx1, x, pre_w, sc1, tm=512, comms=()):
    bsz, seq, _ = x.shape

    def body(dh_ref, dx1_ref, x_ref, w_ref, sc_ref, gx_ref, dsc_ref, dsh_ref, dw_ref):
        b, i = pl.program_id(0), pl.program_id(1)
        w = w_ref[...]
        dh = dh_ref[...]
        mod = 1.0 + sc_ref[...]
        xh, rstd = _rms_hat(x_ref[...])
        dx, per_col = _rms_bwd_gain(dh, mod * w, xh, rstd)
        _acc_out(dsh_ref, i == 0, _colsum(dh))
        _acc_out(dsc_ref, i == 0, per_col * w)
        _acc_out(dw_ref, jnp.logical_and(b == 0, i == 0), per_col * mod)
        gx_ref[...] = dx1_ref[...] + dx

    row_shape = _sds((bsz, 1, D_MODEL), F32)
    return _call(
        body, (dh1, dx1, x, pre_w, sc1), name="norm1_bwd", grid=(bsz, seq // tm),
        in_specs=[_tok_spec(tm), _tok_spec(tm), _tok_spec(tm), _vec_spec(), _row_spec()],
        out_specs=[_tok_spec(tm), _row_spec(), _row_spec(), _vec_spec()],
        out_shape=[_sds(x.shape, F32), row_shape, row_shape, _sds((1, D_MODEL), F32)],
        sem=("arbitrary", "arbitrary"), comms=comms)


def _hgrn_bwd(dcat, proj, o_raw, states, lb_table, norm_w, masks, comms=()):
    bsz, _, seq, _ = proj.shape
    tok = HG_BWD_BLOCKS * HG_TOK
    nstep = seq // tok
    rec0 = ATT_WIDTH // LANE
    width = HG_HPS * LANE
    slabs = (HG_Q0, HG_F0, HG_I0, HG_G0)
    n_steps = (HG_HEADS // HG_HPS) * bsz * nstep
    assert n_steps >= 2

    def body(tbl_ref, nw_ref, dr_ref, p_ref, o_ref, st_ref, lower_ref, upper_ref,
             dproj_ref, dlb_ref, dnw_ref, ds_scr, grad_buf, grad_sem):
        h, b, t = pl.program_id(0), pl.program_id(1), pl.program_id(2)
        step = (h * bsz + b) * nstep + t
        slot = step % 2
        dq_k, df_k, di_k, dg_k = range(4)

        def grad_copies(of_step):
            hh, bb, tt = of_step // (bsz * nstep), (of_step // nstep) % bsz, of_step % nstep
            rows = pl.ds(pl.multiple_of((nstep - 1 - tt) * tok, tok), tok)
            return [pltpu.make_async_copy(
                grad_buf.at[of_step % 2, k],
                dproj_ref.at[bb, rows, pl.ds(pl.multiple_of(slabs[k] * LANE + hh * width, width), width)],
                grad_sem.at[of_step % 2, k]) for k in range(4)]

        @pl.when(step >= 2)
        def _():
            for cp in grad_copies(step - 2):
                cp.wait()

        @pl.when(t == 0)
        def _():
            ds_scr[...] = jnp.zeros_like(ds_scr)

        lower, upper = lower_ref[...], upper_ref[...]
        dlb_parts, dnw_parts = [None] * HG_HPS, [None] * HG_HPS

        def head(hp, sub):
            ls = slice(hp * LANE, (hp + 1) * LANE)
            rows = slice(sub * HG_TOK, (sub + 1) * HG_TOK)
            hq, v, hg = p_ref[HG_Q, rows, ls], p_ref[HG_I, rows, ls], p_ref[HG_G, rows, ls]
            nw = nw_ref[...]
            c = _hgrn_common(tbl_ref[:, ls], p_ref[HG_F, rows, ls], hq)
            qd, kd, ku = c["qd"], c["kd"], c["ku"]
            yield
            y, on, rstd = _rms_fwd(o_ref[rows, ls], nw)
            sg = _sigmoid(hg)
            dr = dr_ref[rows, ls]
            grad_buf[slot, dg_k, rows, ls] = (dr * y * (sg * (1.0 + hg * (1.0 - sg)))).astype(BF16)
            do, dnw_rows = _rms_bwd(dr * (hg * sg), on, rstd, nw)
            yield
            at = _dot(kd, qd, NT_DIMS) * upper
            da = _dot(do, v, NT_DIMS) * lower
            dat = _dot(v, do, NT_DIMS) * upper
            yield
            dv = _dot(at, do)
            dqd = _dot(da, kd)
            dkd = _dot(dat, qd)
            yield
            do_c, qd_c, v_c, ku_c = [_chunk_slices(z.astype(BF16)) for z in (do, qd, v, ku)]
            outer = [_dot(do_c[j], qd_c[j], TN_DIMS) for j in range(HG_NCH)]
            yield
            ds = ds_scr[hp]
            ds_after = [None] * HG_NCH
            for j in reversed(range(HG_NCH)):
                ds_after[j] = ds
                ds = outer[j] + ds * c["e_last"][j]
            ds_scr[hp] = ds
            yield
            updates = [_dot(v_c[j], ku_c[j], TN_DIMS) for j in range(HG_NCH)]
            yield
            states = [st_ref[hp, sub]]
            for j in range(HG_NCH - 1):
                states.append(states[j] * c["e_last"][j] + updates[j])
            dv = dv + jnp.concatenate([_dot(ku_c[j], ds_after[j], NT_DIMS) for j in range(HG_NCH)], axis=0)
            dqd = dqd + jnp.concatenate([_dot(do_c[j], states[j]) for j in range(HG_NCH)], axis=0)
            dku = jnp.concatenate([_dot(v_c[j], ds_after[j]) for j in range(HG_NCH)], axis=0)
            yield
            dku_ku = dku * ku
            dbl = [_colsum(states[j] * ds_after[j]) * c["e_last"][j] + _colsum(dku_ku[j * HG_CHUNK:(j + 1) * HG_CHUNK])
                   for j in range(HG_NCH)]
            dk = dkd * c["e_nb"] + dku * c["e_rem"]
            db = dqd * qd - dkd * kd - dku_ku + jnp.where(_row_in_chunk() == HG_CHUNK - 1, _chunk_rows(dbl), 0.0)
            dfv = _chunk_cumsum(db, reverse=True) / c["f"] - dk
            sig, sq = c["sig"], c["sq"]
            grad_buf[slot, df_k, rows, ls] = (dfv * (1.0 - c["lb"]) * sig * (1.0 - sig)).astype(BF16)
            grad_buf[slot, dq_k, rows, ls] = (dqd * c["e_b"] * (sq * (1.0 + hq * (1.0 - sq)))).astype(BF16)
            grad_buf[slot, di_k, rows, ls] = dv.astype(BF16)
            d_lb, d_nw = _colsum(dfv * (1.0 - sig)), _colsum(dnw_rows)
            dlb_parts[hp] = d_lb if dlb_parts[hp] is None else dlb_parts[hp] + d_lb
            dnw_parts[hp] = d_nw if dnw_parts[hp] is None else dnw_parts[hp] + d_nw

        for sub in reversed(range(HG_BWD_BLOCKS)):
            _in_step(head(hp, sub) for hp in range(HG_HPS))
        _acc_out(dlb_ref, jnp.logical_and(b == 0, t == 0), jnp.concatenate(dlb_parts, axis=1))
        _acc_out(dnw_ref, jnp.logical_and(h == 0, jnp.logical_and(b == 0, t == 0)), sum(dnw_parts[1:], dnw_parts[0]))
        for cp in grad_copies(step):
            cp.start()

        @pl.when(step == n_steps - 1)
        def _():
            for cp in grad_copies(step - 1) + grad_copies(step):
                cp.wait()

    rev = lambda t: nstep - 1 - t
    slab = lambda first: pl.BlockSpec((None, tok, width), lambda h, b, t: (b, rev(t), first // HG_HPS + h))
    head = pl.BlockSpec((None, tok, width), lambda h, b, t: (b, rev(t), h))
    return _call(
        body, (lb_table, norm_w, dcat, proj, o_raw, states, *masks), name="hgrn_bwd",
        grid=(HG_HEADS // HG_HPS, bsz, nstep),
        in_specs=[pl.BlockSpec((2, width), lambda h, b, t: (0, h)), pl.BlockSpec((1, LANE), lambda h, b, t: (0, 0)),
                  slab(rec0), pl.BlockSpec((None, HG_SLABS, tok, width), lambda h, b, t: (b, 0, rev(t), h)), head,
                  pl.BlockSpec((None, HG_HPS, HG_BWD_BLOCKS, LANE, LANE), lambda h, b, t: (b, h, rev(t), 0, 0)),
                  pl.BlockSpec((HG_TOK, HG_TOK), lambda h, b, t: (0, 0)), pl.BlockSpec((HG_TOK, HG_TOK), lambda h, b, t: (0, 0))],
        out_specs=[pl.BlockSpec(memory_space=pl.ANY), pl.BlockSpec((1, width), lambda h, b, t: (0, h)),
                   pl.BlockSpec((1, LANE), lambda h, b, t: (0, 0))],
        out_shape=[_sds((bsz, seq, IN_COLS), BF16), _sds((1, HG_WIDTH), F32), _sds((1, LANE), F32)],
        scratch_shapes=[pltpu.VMEM((HG_HPS, LANE, LANE), F32), pltpu.VMEM((2, 4, tok, width), BF16),
                        pltpu.SemaphoreType.DMA((2, 4))],
        sem=("arbitrary", "arbitrary", "arbitrary"), comms=comms)


def _attn_bwd(dcat, raw, w_norm, qh, kh, vh, lse, sinks, tables, biases, dproj, comms=()):
    bsz, nblk = qh.shape[0], qh.shape[1]
    seq = nblk * WINDOW
    nstep = nblk // ATT_BPS
    half = ROPE_DIM // 2

    def body(sink_ref, da_ref, raw_ref, w_ref, q_ref, kc_ref, kp_ref, vc_ref, vp_ref, l_ref, c_ref, u_ref, d_ref,
             full_ref, first_ref, dproj_ref, o_ref, dw_ref, dsink_ref, carry_k, carry_v):
        b, i = pl.program_id(0), pl.program_id(1)
        first = jnp.logical_and(b == 0, i == 0)

        @pl.when(i == 0)
        def _():
            carry_k[...] = jnp.zeros_like(carry_k)
            carry_v[...] = jnp.zeros_like(carry_v)

        w = w_ref[...]
        _, on, rstd = _rms_fwd(raw_ref[...], w)
        do_step, dw_rows = _rms_bwd(da_ref[...], on, rstd, w)
        _acc_out(dw_ref, first, _colsum(dw_rows))
        lane8 = lax.broadcasted_iota(jnp.int32, (1, ATT_Q_HEADS), 1)
        dsink = jnp.zeros((1, ATT_Q_HEADS), F32)
        head_cols = jnp.where(lax.broadcasted_iota(jnp.int32, (2 * ATT_Q_HEADS, ATT_WIDTH), 1) // ATT_HEAD_DIM
                              == lax.broadcasted_iota(jnp.int32, (2 * ATT_Q_HEADS, ATT_WIDTH), 0), 1.0, 0.0)
        from_next_k, from_next_v = carry_k[...], carry_v[...]
        for blk in reversed(range(ATT_BPS)):
            tok = slice(blk * WINDOW, (blk + 1) * WINDOW)
            bias = _band_bias(full_ref, first_ref, True if blk else i < nstep - 1)
            do_all = do_step[tok]
            c, u, d = c_ref[tok, :], u_ref[tok, :], d_ref[tok, :]
            lse_t = l_ref[tok, :].T
            prod = do_all * raw_ref[tok, :]
            prod_hi = prod.astype(BF16)
            prod_lo = prod - prod_hi.astype(F32)
            dsum_t = _dot(head_cols, prod_hi, NT_DIMS) + _dot(head_cols, prod_lo, NT_DIMS)

            def unrope(g):
                return (g * c + pltpu.roll(g * u, LANE - half, 1) + pltpu.roll(g * d, half, 1)).astype(BF16)

            groups = range(ATT_KV_HEADS)
            group_row = lambda z, g: jnp.concatenate(
                [z[g * ATT_GROUP + hh:g * ATT_GROUP + hh + 1, :] for hh in range(ATT_GROUP)], axis=1)
            q = [q_ref[blk, g] for g in groups]
            keys, vals = [_band(kp_ref, kc_ref, g, blk) for g in groups], [_band(vp_ref, vc_ref, g, blk) for g in groups]
            do_g = [jnp.concatenate([do_all[:, (g * ATT_GROUP + hh) * ATT_HEAD_DIM:(g * ATT_GROUP + hh + 1) * ATT_HEAD_DIM]
                                     for hh in range(ATT_GROUP)], axis=0) for g in groups]
            dsum, lse_g = [group_row(dsum_t, g) for g in groups], [group_row(lse_t, g) for g in groups]
            s_t = [_dot(keys[g], q[g], NT_DIMS) for g in groups]
            dp_t = [_dot(vals[g], do_g[g], NT_DIMS) for g in groups]
            p_t = [jnp.exp(s_t[g] * ATT_SCALE + bias - lse_g[g]) for g in groups]
            ds_t = [p_t[g] * (dp_t[g] - dsum[g]) * ATT_SCALE for g in groups]
            dq_g = [_dot(ds_t[g], keys[g], TN_DIMS) for g in groups]
            dk_g = [_dot(ds_t[g], q[g]) for g in groups]
            dv_g = [_dot(p_t[g], do_g[g]) for g in groups]
            for g in groups:
                sink_part = jnp.exp(_sink_row(sink_ref, g) - lse_g[g]) * dsum[g]
                for hh in range(ATT_GROUP):
                    head_sum = jnp.sum(sink_part[:, hh * WINDOW:(hh + 1) * WINDOW], axis=1, keepdims=True)
                    dsink = dsink - jnp.where(lane8 == g * ATT_GROUP + hh, head_sum, 0.0)
            dq_parts = [dq_g[g][hh * WINDOW:(hh + 1) * WINDOW] for g in groups for hh in range(ATT_GROUP)]
            dk_before, dk_own = [z[:WINDOW] for z in dk_g], [z[WINDOW:] for z in dk_g]
            dv_before, dv_own = [z[:WINDOW] for z in dv_g], [z[WINDOW:] for z in dv_g]
            per_slab = LANE // ATT_HEAD_DIM
            for s in range(ATT_WIDTH // LANE):
                slab = jnp.concatenate(dq_parts[s * per_slab:(s + 1) * per_slab], axis=1)
                o_ref[tok, s * LANE:(s + 1) * LANE] = unrope(slab)
            o_ref[tok, ATT_WIDTH:ATT_WIDTH + LANE] = unrope(jnp.concatenate(dk_own, axis=1) + from_next_k)
            o_ref[tok, ATT_WIDTH + LANE:ATT_COLS] = (jnp.concatenate(dv_own, axis=1) + from_next_v).astype(BF16)
            from_next_k, from_next_v = jnp.concatenate(dk_before, axis=1), jnp.concatenate(dv_before, axis=1)
        carry_k[...] = from_next_k
        carry_v[...] = from_next_v
        _acc_out(dsink_ref, first, dsink)

    rows = ATT_BPS * WINDOW
    rev = lambda i: nstep - 1 - i
    cur = lambda width: pl.BlockSpec((None, rows, width), lambda b, i: (b, rev(i), 0))
    q_spec = pl.BlockSpec((None, ATT_BPS, ATT_KV_HEADS, GROUP_ROWS, ATT_HEAD_DIM), lambda b, i: (b, rev(i), 0, 0, 0))
    kv_cur = pl.BlockSpec((None, ATT_KV_HEADS, rows, ATT_HEAD_DIM), lambda b, i: (b, 0, rev(i), 0))
    kv_prev = pl.BlockSpec((None, ATT_KV_HEADS, WINDOW, ATT_HEAD_DIM), lambda b, i: (b, 0, jnp.maximum(ATT_BPS * rev(i) - 1, 0), 0))
    tab = pl.BlockSpec((rows, LANE), lambda b, i: (rev(i), 0))
    return _call(
        body, (sinks, dcat, raw, w_norm, qh, kh, kh, vh, vh, lse, *tables, *biases, dproj), name="attn_bwd", grid=(bsz, nstep),
        in_specs=[pl.BlockSpec(memory_space=pltpu.SMEM), cur(ATT_WIDTH), cur(ATT_WIDTH), _vec_spec(ATT_WIDTH), q_spec,
                  kv_cur, kv_prev, kv_cur, kv_prev, cur(LANE), tab, tab, tab, _bias_spec(True), _bias_spec(True),
                  pl.BlockSpec(memory_space=pl.ANY)],
        out_specs=[cur(ATT_COLS), _vec_spec(ATT_WIDTH), _vec_spec(ATT_Q_HEADS)],
        out_shape=[_sds(dproj.shape, BF16), _sds((1, ATT_WIDTH), F32), _sds((1, ATT_Q_HEADS), F32)],
        scratch_shapes=[pltpu.VMEM((WINDOW, LANE), F32), pltpu.VMEM((WINDOW, LANE), F32)],
        sem=("arbitrary", "arbitrary"), comms=comms, aliases={15: 0})


def _other_chips(x, y):
    return [(1 - x, y), (x, 1 - y), (1 - x, 1 - y)]


def _sem_pair(n):
    return [pltpu.SemaphoreType.DMA((n,)), pltpu.SemaphoreType.DMA((n,))]


def _plan_pair_forward(bufs):
    n = len(bufs)

    def copies(outs, sems):
        x, y, c = _mesh_pos()
        sends, lands = [], []
        for a in range(n):
            for j, chip in enumerate(_other_chips(x, y)):
                k = 3 * a + j
                slot = outs[a].at[4 * chip[0] + 2 * chip[1] + c]
                sends.append(pltpu.make_async_remote_copy(
                    src_ref=slot, dst_ref=slot, send_sem=sems[0].at[k], recv_sem=sems[1].at[k],
                    device_id=(x, y, 1 - c), device_id_type=MESH))
                theirs = outs[a].at[4 * chip[0] + 2 * chip[1] + 1 - c]
                lands.append(pltpu.make_async_remote_copy(
                    src_ref=theirs, dst_ref=theirs, send_sem=sems[0].at[k], recv_sem=sems[1].at[k],
                    device_id=(x, y, 1 - c), device_id_type=MESH))
        return sends, lands

    def start(ins, outs, sems):
        for cp in copies(outs, sems)[0]:
            cp.start()

    def finish(ins, outs, sems):
        sends, lands = copies(outs, sems)
        for cp in lands:
            cp.wait_recv()
        for cp in sends:
            cp.wait_send()

    return _Comm(list(bufs), [_sds(b.shape, b.dtype) for b in bufs], _sem_pair(3 * n), start, finish,
                 aliases=[(a, a) for a in range(n)])


def _plan_pair(arrays, other_half):
    n = len(arrays)
    per = N_CHIPS if other_half == "chip_major" else 1

    def copies(ins, outs, sems):
        x, y, c = _mesh_pos()
        out = []
        for a in range(n):
            for k in range(per):
                if other_half == "chip_major":
                    src, dst = ins[a].at[k, 1 - c], outs[a].at[k]
                else:
                    src, dst = (ins[a].at[1 - c] if other_half else ins[a]), outs[a]
                out.append(pltpu.make_async_remote_copy(
                    src_ref=src, dst_ref=dst, send_sem=sems[0].at[per * a + k], recv_sem=sems[1].at[per * a + k],
                    device_id=(x, y, 1 - c), device_id_type=MESH))
        return out

    def start(ins, outs, sems):
        for cp in copies(ins, outs, sems):
            cp.start()

    def finish(ins, outs, sems):
        for cp in copies(ins, outs, sems):
            cp.wait()

    if other_half == "chip_major":
        shapes = [_sds((a.shape[0],) + a.shape[2:], a.dtype) for a in arrays]
    else:
        shapes = [_sds(a.shape[1:] if other_half else a.shape, a.dtype) for a in arrays]
    return _Comm(list(arrays), shapes, _sem_pair(per * n), start, finish)


def _plan_chip_exchange(arrays):
    n = len(arrays)

    def copies(ins, outs, sems):
        x, y, c = _mesh_pos()
        sends, lands = [], []
        for a in range(n):
            for j, chip in enumerate(_other_chips(x, y)):
                k = 3 * a + j
                sends.append(pltpu.make_async_remote_copy(
                    src_ref=ins[a].at[2 * chip[0] + chip[1]], dst_ref=outs[a].at[2 * x + y], send_sem=sems[0].at[k],
                    recv_sem=sems[1].at[k], device_id=(*chip, c), device_id_type=MESH))
                slot = outs[a].at[2 * chip[0] + chip[1]]
                lands.append(pltpu.make_async_remote_copy(
                    src_ref=slot, dst_ref=slot, send_sem=sems[0].at[k], recv_sem=sems[1].at[k],
                    device_id=(*chip, c), device_id_type=MESH))
        return sends, lands

    def start(ins, outs, sems):
        for cp in copies(ins, outs, sems)[0]:
            cp.start()

    def finish(ins, outs, sems):
        sends, lands = copies(ins, outs, sems)
        for cp in lands:
            cp.wait_recv()
        for cp in sends:
            cp.wait_send()

    return _Comm(list(arrays), [_sds(a.shape, a.dtype) for a in arrays], _sem_pair(3 * n), start, finish)


SEM_SPEC = pl.BlockSpec(memory_space=pltpu.SEMAPHORE)
N_OTHER = N_CHIPS - 1


def _exchange_copies(s_ref, land_ref, sems):
    x, y, c = _mesh_pos()
    return [pltpu.make_async_remote_copy(
        src_ref=s_ref.at[2 * chip[0] + chip[1]], dst_ref=land_ref.at[2 * x + y], send_sem=sems[j], recv_sem=sems[N_OTHER + j],
        device_id=(*chip, c), device_id_type=MESH) for j, chip in enumerate(_other_chips(x, y))]


def _exchange_start(s, name):
    def body(s_ref, land_ref, *outs):
        sems, token = outs[:2 * N_OTHER], outs[-1]
        for cp in _exchange_copies(s_ref, land_ref, sems):
            cp.start()
        token[...] = jnp.zeros_like(token)

    hbm = pltpu.HBM(s.shape, s.dtype)
    res = pl.pallas_call(
        body, name=name,
        out_shape=(pltpu.SemaphoreType.DMA(()),) * (2 * N_OTHER) + (hbm, hbm, _sds((SUBLANES, LANE), F32)),
        in_specs=(HBM_SPEC, HBM_SPEC),
        out_specs=(SEM_SPEC,) * (2 * N_OTHER) + (HBM_SPEC, HBM_SPEC, pl.BlockSpec(memory_space=pltpu.VMEM)),
        input_output_aliases={0: 2 * N_OTHER, 1: 2 * N_OTHER + 1},
        compiler_params=pltpu.CompilerParams(has_side_effects=pltpu.SideEffectType.DATAFLOW_SIDE_EFFECTING),
    )(pltpu.with_memory_space_constraint(s, pltpu.HBM), pltpu.with_memory_space_constraint(lax.empty(s.shape, s.dtype), pltpu.HBM))
    return res[:2 * N_OTHER], res[2 * N_OTHER], res[2 * N_OTHER + 1], res[-1]


def _exchange_wait(sems, s_thru, land_thru, afters, name):
    def body(s_ref, land_ref, *rest):
        for cp in _exchange_copies(s_ref, land_ref, rest[:2 * N_OTHER]):
            cp.wait_send()
            cp.wait_recv()

    hbm = pltpu.HBM(s_thru.shape, s_thru.dtype)
    return pl.pallas_call(
        body, name=name, out_shape=(hbm, hbm),
        in_specs=(HBM_SPEC, HBM_SPEC) + (SEM_SPEC,) * (2 * N_OTHER) + (pl.BlockSpec(memory_space=pl.ANY),) * len(afters),
        out_specs=(HBM_SPEC, HBM_SPEC), input_output_aliases={0: 0, 1: 1},
        compiler_params=pltpu.CompilerParams(has_side_effects=pltpu.SideEffectType.DATAFLOW_SIDE_EFFECTING),
    )(s_thru, land_thru, *sems, *afters)


def _gather_copies(block_ref, buf_ref, sems):
    x, y, c = _mesh_pos()
    return [pltpu.make_async_remote_copy(
        src_ref=block_ref, dst_ref=buf_ref.at[4 * x + 2 * y + c], send_sem=sems[j], recv_sem=sems[N_OTHER + j],
        device_id=(*chip, c), device_id_type=MESH) for j, chip in enumerate(_other_chips(x, y))]


def _gather_start(blocks, bufs, afters, name):
    n = len(blocks)
    per = 2 * N_OTHER

    def body(*refs):
        ins, outs = refs[:2 * n], refs[2 * n + len(afters):]
        for a in range(n):
            for cp in _gather_copies(ins[a], ins[n + a], outs[a * per:(a + 1) * per]):
                cp.start()
        outs[-1][...] = jnp.zeros_like(outs[-1])

    hbm = [pltpu.HBM(z.shape, z.dtype) for z in list(blocks) + list(bufs)]
    res = pl.pallas_call(
        body, name=name,
        out_shape=(pltpu.SemaphoreType.DMA(()),) * (n * per) + tuple(hbm) + (_sds((SUBLANES, LANE), F32),),
        in_specs=(HBM_SPEC,) * (2 * n) + (pl.BlockSpec(memory_space=pl.ANY),) * len(afters),
        out_specs=(SEM_SPEC,) * (n * per) + (HBM_SPEC,) * (2 * n) + (pl.BlockSpec(memory_space=pltpu.VMEM),),
        input_output_aliases={k: n * per + k for k in range(2 * n)},
        compiler_params=pltpu.CompilerParams(has_side_effects=pltpu.SideEffectType.DATAFLOW_SIDE_EFFECTING),
    )(*[pltpu.with_memory_space_constraint(z, pltpu.HBM) for z in list(blocks) + list(bufs)], *afters)
    parts = [(res[a * per:(a + 1) * per], res[n * per + a], res[n * per + n + a]) for a in range(n)]
    return parts, res[-1]


def _gather_wait(part, afters, name):
    sems, block, buf = part

    def body(block_ref, buf_ref, *rest):
        for cp in _gather_copies(block_ref, buf_ref, rest[:2 * N_OTHER]):
            cp.wait_send()
            cp.wait_recv()

    return pl.pallas_call(
        body, name=name, out_shape=(pltpu.HBM(block.shape, block.dtype), pltpu.HBM(buf.shape, buf.dtype)),
        in_specs=(HBM_SPEC, HBM_SPEC) + (SEM_SPEC,) * (2 * N_OTHER) + (pl.BlockSpec(memory_space=pl.ANY),) * len(afters),
        out_specs=(HBM_SPEC, HBM_SPEC), input_output_aliases={0: 0, 1: 1},
        compiler_params=pltpu.CompilerParams(has_side_effects=pltpu.SideEffectType.DATAFLOW_SIDE_EFFECTING),
    )(block, buf, *sems, *afters)[1]


def _comm_only(comms, name):
    return _call(lambda: None, (), name=name, grid=(), in_specs=[], out_specs=[], out_shape=[], sem=(), comms=comms)[1]


def _allgather8(arrays, name):
    return _comm_only([_plan_allgather8(arrays)], name)[0]


def _plan_allgather8(arrays):
    n = len(arrays)

    def parts(ins, outs, sems):
        send_sems, recv_sems, local_sems = sems
        x, y, c = _mesh_pos()
        me, sibling = (x, y, c), (x, y, 1 - c)
        chips = _other_chips(x, y)

        def copy(a, k, block, to, src=None):
            dst = outs[a].at[4 * block[0] + 2 * block[1] + block[2]]
            return pltpu.make_async_remote_copy(
                src_ref=dst if src is None else src, dst_ref=dst, send_sem=send_sems.at[7 * a + k],
                recv_sem=recv_sems.at[7 * a + k], device_id=to, device_id_type=MESH)

        mine = [pltpu.make_async_copy(ins[a], outs[a].at[4 * x + 2 * y + c], local_sems.at[a]) for a in range(n)]
        first = []
        for a in range(n):
            first.append(copy(a, 0, me, sibling, src=ins[a]))
            first += [copy(a, 1 + j, me, (*chip, c), src=ins[a]) for j, chip in enumerate(chips)]
        return copy, mine, first, me, sibling, chips, c

    def start(ins, outs, sems):
        _, mine, first, *_ = parts(ins, outs, sems)
        for cp in mine + first:
            cp.start()

    def finish(ins, outs, sems):
        copy, mine, first, me, sibling, chips, c = parts(ins, outs, sems)
        passed = []
        for j, chip in enumerate(chips):
            for a in range(n):
                copy(a, 1 + j, (*chip, c), me).wait_recv()
                fwd = copy(a, 4 + j, (*chip, c), sibling)
                fwd.start()
                passed.append(fwd)
        for a in range(n):
            copy(a, 0, sibling, me).wait_recv()
            for j, chip in enumerate(chips):
                copy(a, 4 + j, (*chip, 1 - c), me).wait_recv()
        for cp in first + passed:
            cp.wait_send()
        for cp in mine:
            cp.wait()

    sems = [pltpu.SemaphoreType.DMA((7 * n,)), pltpu.SemaphoreType.DMA((7 * n,)), pltpu.SemaphoreType.DMA((n,))]
    return _Comm(list(arrays), [_sds((N_DEV,) + a.shape, a.dtype) for a in arrays], sems, start, finish)


def _pair_sum(g, q, core, name, chip_major=False):
    rows, cols = g.shape[2:]
    tr = _row_tile(rows)

    def body(core_ref, g_ref, q_ref, o_ref):
        o_ref[...] = (g_ref[...] + q_ref[...]).astype(BF16)

    blk = pl.BlockSpec((None, tr, cols), lambda k, i, core_ref: (k, i, 0))
    if chip_major:
        own = pl.BlockSpec((None, None, tr, cols), lambda k, i, core_ref: (k, core_ref[0], i, 0))
    else:
        own = pl.BlockSpec((None, None, tr, cols), lambda k, i, core_ref: (core_ref[0], k, i, 0))
    return pl.pallas_call(
        body, name=name,
        grid_spec=pltpu.PrefetchScalarGridSpec(num_scalar_prefetch=1, grid=(N_CHIPS, rows // tr), in_specs=[own, blk], out_specs=blk),
        out_shape=_sds((N_CHIPS, rows, cols), BF16), compiler_params=_params("parallel", "parallel"),
    )(core, g, q)


def _sum_chips(own, landed, chip, name):
    _, rows, cols = own.shape
    tr = _row_tile(rows)

    def body(chip_ref, own_ref, a_ref, b_ref, c_ref, o_ref):
        acc = own_ref[...].astype(F32) + a_ref[...].astype(F32)
        o_ref[...] = (acc + b_ref[...].astype(F32)) + c_ref[...].astype(F32)

    blk = lambda flip: pl.BlockSpec((None, tr, cols), lambda i, chip_ref: (jnp.bitwise_xor(chip_ref[0], flip), i, 0))
    return pl.pallas_call(
        body, name=name,
        grid_spec=pltpu.PrefetchScalarGridSpec(num_scalar_prefetch=1, grid=(rows // tr,), in_specs=[blk(0), blk(1), blk(2), blk(3)],
                                               out_specs=pl.BlockSpec((tr, cols), lambda i, chip_ref: (i, 0))),
        out_shape=_sds((rows, cols), F32), compiler_params=_params("parallel"),
    )(chip, own, landed, landed, landed)


SUBLANES = 8


def _tile_rows(n_elems):
    return -(-n_elems // (SUBLANES * LANE)) * SUBLANES


SMALL_ITEMS = (("b_ada", N_MOD * D_MODEL), ("pre_w_mix", D_MODEL), ("post_w_mix", D_MODEL), ("pre_w_mlp", D_MODEL),
               ("post_w_mlp", D_MODEL), ("attn_out_w", ATT_WIDTH), ("hg_norm_w", HG_HEAD_DIM), ("attn_sinks", ATT_Q_HEADS),
               ("lb_0", HG_WIDTH), ("lb_1", HG_WIDTH))
SMALL_AT = {}
for _name, _size in SMALL_ITEMS:
    SMALL_AT[_name] = (sum(r for _, r in SMALL_AT.values()), _tile_rows(_size))
SMALL_ROWS = sum(r for _, r in SMALL_AT.values())
MOD_ROWS = SMALL_AT["b_ada"][1]
PLAIN_ROWS = SMALL_AT["lb_0"][0] - MOD_ROWS
LB_ROWS = SMALL_AT["lb_0"][1]


def _rows(a, nrows=None):
    flat = a.reshape(-1)
    nrows = _tile_rows(flat.shape[0]) if nrows is None else nrows
    return jnp.pad(flat, (0, nrows * LANE - flat.shape[0])).reshape(nrows, LANE)


def _pack_small(vals):
    vals = dict(vals, lb_0=vals["lb_table"][0], lb_1=vals["lb_table"][1])
    return jnp.concatenate([_rows(vals[name], SMALL_AT[name][1]) for name, _ in SMALL_ITEMS], axis=0)


def _unpack_small(p):
    def item(name, shape):
        first = SMALL_AT[name][0]
        size = shape[0] * shape[1]
        return p[first:first + SMALL_AT[name][1]].reshape(-1)[:size].reshape(shape)

    out = {name: item(name, (1, size)) for name, size in SMALL_ITEMS if not name.startswith("lb_")}
    out["lb_table"] = jnp.concatenate([item("lb_0", (1, HG_WIDTH)), item("lb_1", (1, HG_WIDTH))], axis=0)
    return out


def _pack_partials(dmod, plain, d_lb, loss_row):
    return jnp.concatenate([_rows(dmod, dmod.shape[0] * MOD_ROWS)] + [_rows(g) for g in plain] + [_rows(d_lb), _rows(loss_row)], axis=0)


def _small_update(packs, w, m, v, n_seq):
    mod_end = n_seq * MOD_ROWS
    lb_at = mod_end + PLAIN_ROWS
    t0, t1 = SMALL_AT["lb_0"][0], SMALL_AT["lb_1"][0]

    def body(p_ref, w_ref, m_ref, v_ref, g_ref, dl_ref, nm_ref, nv_ref, loss_ref):
        tot = p_ref[0]
        for d in range(1, N_DEV):
            tot = tot + p_ref[d]
        wv = w_ref[...]
        p1 = _sigmoid(wv[t1:t1 + LB_ROWS] - wv[t0:t0 + LB_ROWS])
        s = tot[lb_at:lb_at + LB_ROWS] * p1 * (1.0 - p1)
        g_bias = tot[0:MOD_ROWS]
        for q in range(1, n_seq):
            g_bias = g_bias + tot[q * MOD_ROWS:(q + 1) * MOD_ROWS]
        g = jnp.concatenate([g_bias, tot[mod_end:lb_at], -s, s], axis=0)
        g_ref[...] = g
        dl_ref[...], nm_ref[...], nv_ref[...] = _adamw_math(g, wv, m_ref[...], v_ref[...])
        loss_ref[...] = tot[lb_at + LB_ROWS:lb_at + LB_ROWS + SUBLANES]

    shp = _sds((SMALL_ROWS, LANE), F32)
    return pl.pallas_call(body, name="small_update", out_shape=[shp] * 4 + [_sds((SUBLANES, LANE), F32)],
                          compiler_params=_params())(packs, w, m, v)


def kernel(x, c, w_ada, b_ada, pre_w_mix, w_in, attn_sinks, attn_out_w, lb_table, hg_norm_w, w_out, post_w_mix, pre_w_mlp, w_up, w_down, post_w_mlp, loss_target, m_w_ada, m_b_ada, m_pre_w_mix, m_w_in, m_attn_sinks, m_attn_out_w, m_lb_table, m_hg_norm_w, m_w_out, m_post_w_mix, m_pre_w_mlp, m_w_up, m_w_down, m_post_w_mlp, v_w_ada, v_b_ada, v_pre_w_mix, v_w_in, v_attn_sinks, v_attn_out_w, v_lb_table, v_hg_norm_w, v_w_out, v_post_w_mix, v_pre_w_mlp, v_w_up, v_w_down, v_post_w_mlp):
    xi, yi, ci = _mesh_pos()
    chip = 2 * xi + yi
    dev = 2 * chip + ci
    bsz, seq, _ = x.shape
    ntok = bsz * seq
    ada_cols = w_ada.shape[2]
    core = jnp.reshape(ci, (1,)).astype(jnp.int32)
    chip_idx = jnp.reshape(chip, (1,)).astype(jnp.int32)
    flat = lambda a: a.reshape(ntok, a.shape[-1])
    unflat = lambda a: a.reshape(bsz, seq, a.shape[-1])
    tables = _rope_tables(seq)
    biases, chunk_masks = _band_biases(), _block_masks()

    def row_half(w):
        rows = w.shape[1] // 2
        return lax.dynamic_slice_in_dim(w[0], ci * rows, rows, axis=0).astype(BF16)

    def gather_buffer(w):
        rows, cols = w.shape[1] // 2, w.shape[2]
        own = w[0].astype(BF16).reshape(2, rows, cols)
        return lax.dynamic_update_slice(lax.empty((N_DEV, rows, cols), BF16), own, (2 * chip, 0, 0))

    w_in_t, m_in_t, v_in_t = [jnp.transpose(a[0])[None] for a in (w_in, m_w_in, v_w_in)]
    c_g, in_g = _allgather8([c, row_half(w_in_t)], "gather_first")
    c_all = c_g.reshape(N_DEV * bsz, D_MODEL)
    w_in_full = in_g.reshape(IN_COLS, D_MODEL)

    b_cols = lax.dynamic_slice_in_dim(b_ada, chip * ada_cols, ada_cols, axis=1)
    mod_part = _ada_fwd(c_all, w_ada[0], b_cols)
    half_rows = mod_part.shape[0] // 2
    (mod_g,) = _allgather8([lax.dynamic_slice_in_dim(mod_part, ci * half_rows, half_rows, axis=0)], "gather_mod")
    mod_all = mod_g.reshape(N_CHIPS, 2, half_rows, ada_cols).transpose(1, 2, 0, 3).reshape(N_DEV * bsz, N_MOD * D_MODEL)
    mod = lax.dynamic_slice_in_dim(mod_all, dev * bsz, bsz, axis=0)
    sh1, sc1, g1, sh2, sc2, g2 = [mod[:, i * D_MODEL:(i + 1) * D_MODEL].reshape(bsz, 1, D_MODEL) for i in range(N_MOD)]

    weights = (w_out, w_up, w_down)
    (out_part, up_part, down_part), started = _gather_start(
        [row_half(w) for w in weights], [gather_buffer(w) for w in weights], [mod_g], "gather_weights_start")

    h1, proj, qh, kh, vh = _in_proj_fused(x, pre_w_mix, sc1 + started[0:1, 0:1], sh1, w_in_full, tables)
    out_g = _gather_wait(out_part, [proj], "gather_out_wait")
    (attn_raw, cat, lse), ((out_g,),) = _attn_fwd(qh, kh, vh, attn_sinks, attn_out_w, biases, comms=[_plan_pair_forward([out_g])])
    up_g = _gather_wait(up_part, [attn_raw], "gather_up_wait")
    (o_raw, cat, states), ((up_g,),) = _hgrn_fwd(proj, lb_table, hg_norm_w, cat, chunk_masks, comms=[_plan_pair_forward([up_g])])
    down_g = _gather_wait(down_part, [o_raw], "gather_down_wait")
    w_out_full = out_g.reshape(D_MODEL, D_MODEL)
    w_up4 = up_g.reshape(N_CHIPS, D_MODEL, D_MODEL)
    mix, x1, h2 = _out_proj_fused(cat, w_out_full, x, post_w_mix, g1, pre_w_mlp, sc2, sh2)
    big_tm = min(ntok, 2048)
    up_spec = pl.BlockSpec((None, D_MODEL, D_MODEL), lambda i, j: (j, 0, 0))
    r, ((down_g,),) = _mm(flat(h2), w_up4, name="up_proj", out_dtype=BF16, tm=big_tm, tn=D_MODEL, n_out=D_FF, b_spec=up_spec,
                          epi=lambda acc: jnp.maximum(acc, 0.0), comms=[_plan_pair_forward([down_g])])
    w_down_full = down_g.reshape(D_FF, D_MODEL)
    square = lambda t: t * t
    loss_row, dy, dd, dg2, d_post_mlp = _down_proj_fused(unflat(r), w_down_full, x1, post_w_mlp, g2, loss_target)

    dpre = _mm(flat(dd), w_down_full, name="down_bwd", out_dtype=BF16, trans_b=True, tm=big_tm, tn=D_MODEL, extra=(r,),
               epi=lambda acc, rt: acc * (2.0 * rt.astype(F32)))
    half_rows = D_MODEL // 2
    g_down = _mm_tn(r, flat(dd), name="down_wgrad", tk=half_rows, tn=D_MODEL, a_fn=square,
                    out_shape=_sds((2, N_CHIPS, half_rows, D_MODEL), F32),
                    out_spec=pl.BlockSpec((None, None, half_rows, D_MODEL), lambda i, j: (i % 2, i // 2, 0, 0)))
    (dx1, dmix, dsc2, dsh2, dg1, d_pre_mlp, d_post_mix), ((q_down,),) = _up_bwd_fused(
        unflat(dpre), w_up4, dy, x1, mix, pre_w_mlp, sc2, post_w_mix, g1, comms=[_plan_pair([g_down], True)])
    g_up = _mm_tn(flat(h2), dpre, name="up_wgrad", tk=D_MODEL, tn=half_rows,
                  out_shape=_sds((2, N_CHIPS, half_rows, D_MODEL), F32),
                  out_spec=pl.BlockSpec((2, None, half_rows, half_rows), lambda i, j: (0, j // 2, 0, j % 2)))
    s_down = _pair_sum(g_down, q_down, core, "pair_sum_down")

    dcat, ((q_up,),) = _mm(flat(dmix), w_out_full, name="out_bwd", out_dtype=F32, trans_b=True, comms=[_plan_pair([g_up], True)])
    dcat = unflat(dcat)
    s_up = _pair_sum(g_up, q_up, core, "pair_sum_up")
    out_rows = D_MODEL // N_CHIPS
    g_out = _mm_tn(flat(cat), flat(dmix), name="out_wgrad", tk=2 * out_rows, tn=half_rows,
                   out_shape=_sds((2, N_CHIPS, out_rows, half_rows), F32),
                   out_spec=pl.BlockSpec((None, 2, out_rows, half_rows), lambda i, j: (j, i, 0, 0)))
    (dproj_rec, d_lb, d_hg_norm), ((x_down,), (q_out,)) = _hgrn_bwd(
        dcat, proj, o_raw, states, lb_table, hg_norm_w, chunk_masks, comms=[_plan_chip_exchange([s_down]), _plan_pair([g_out], True)])
    half_down = _sum_chips(s_down, x_down, chip_idx, "sum_chips_down")
    s_out = _pair_sum(g_out, q_out, core, "pair_sum_out")
    (dproj, d_attn_out, d_sinks), ((their_down,), (x_up,)) = _attn_bwd(
        dcat, attn_raw, attn_out_w, qh, kh, vh, lse, attn_sinks, tables, [bias.T for bias in biases], dproj_rec,
        comms=[_plan_pair([half_down], False), _plan_chip_exchange([s_up])])
    half_up = _sum_chips(s_up, x_up, chip_idx, "sum_chips_up")
    dproj = flat(dproj)
    in_rows = IN_COLS // N_CHIPS // 2
    g_in, ((x_out,),) = _mm_tn(dproj, flat(h1), name="in_wgrad", tk=2 * LANE, tn=D_MODEL, comms=[_plan_chip_exchange([s_out])])
    g_in = g_in.reshape(N_CHIPS, 2, in_rows, D_MODEL)
    half_out = _sum_chips(s_out, x_out, chip_idx, "sum_chips_out")
    dh1, ((q_in,), (their_up, their_out)) = _mm(
        dproj, w_in_full, name="in_bwd", out_dtype=F32,
        comms=[_plan_pair([g_in], "chip_major"), _plan_pair([half_up, half_out], False)])
    s_in = _pair_sum(g_in, q_in, core, "pair_sum_in", chip_major=True)
    in_sems, s_in, in_landing, started = _exchange_start(s_in, "exchange_in_start")
    grad_x, dsc1, dsh1, d_pre_mix = _norm1_bwd(unflat(dh1), dx1, x, pre_w_mix + started[0:1, 0:1], sc1)

    dmod = jnp.concatenate([dsh1, dsc1, dg1, dsh2, dsc2, dg2], axis=-1).reshape(bsz, N_MOD * D_MODEL)
    pack = _pack_partials(dmod, [d_pre_mix, d_post_mix, d_pre_mlp, d_post_mlp, d_attn_out, d_hg_norm, d_sinks], d_lb, loss_row)
    ((packs,),) = _comm_only([_plan_allgather8([pack])], "gather_small")
    w_small = dict(b_ada=b_ada, pre_w_mix=pre_w_mix, post_w_mix=post_w_mix, pre_w_mlp=pre_w_mlp, post_w_mlp=post_w_mlp,
                   attn_out_w=attn_out_w, hg_norm_w=hg_norm_w, attn_sinks=attn_sinks, lb_table=lb_table)
    m_small = dict(b_ada=m_b_ada, pre_w_mix=m_pre_w_mix, post_w_mix=m_post_w_mix, pre_w_mlp=m_pre_w_mlp, post_w_mlp=m_post_w_mlp,
                   attn_out_w=m_attn_out_w, hg_norm_w=m_hg_norm_w, attn_sinks=m_attn_sinks, lb_table=m_lb_table)
    v_small = dict(b_ada=v_b_ada, pre_w_mix=v_pre_w_mix, post_w_mix=v_post_w_mix, pre_w_mlp=v_pre_w_mlp, post_w_mlp=v_post_w_mlp,
                   attn_out_w=v_attn_out_w, hg_norm_w=v_hg_norm_w, attn_sinks=v_attn_sinks, lb_table=v_lb_table)
    *small_packed, loss_rows = _small_update(packs, _pack_small(w_small), _pack_small(m_small), _pack_small(v_small), bsz)
    small_out = [_unpack_small(p) for p in small_packed]
    loss = loss_rows[0, 0]

    dmod_all = packs[:, :bsz * MOD_ROWS, :].reshape(N_DEV * bsz, N_MOD * D_MODEL)
    dmod_cols = lax.dynamic_slice_in_dim(dmod_all, chip * ada_cols, ada_cols, axis=1)
    ada_out = _ada_bwd_adamw(c_all, dmod_cols, w_ada[0], m_w_ada[0], v_w_ada[0])

    s_in, x_in = _exchange_wait(in_sems, s_in, in_landing, [grad_x, ada_out[0]], "exchange_in_wait")
    half_in = _sum_chips(s_in, x_in, chip_idx, "sum_chips_in")
    ((their_in,),) = _comm_only([_plan_pair([half_in], False)], "pair_swap_in")
    big = dict(
        w_in=tuple(jnp.transpose(a) for a in _adamw_halves(half_in, their_in, core, w_in_t[0], m_in_t[0], v_in_t[0], axis=0,
                                                           name="adamw_in")),
        w_up=tuple(_adamw_halves(half_up, their_up, core, w_up[0], m_w_up[0], v_w_up[0], axis=0, name="adamw_up")),
        w_out=tuple(_adamw_halves(half_out, their_out, core, w_out[0], m_w_out[0], v_w_out[0], axis=1, name="adamw_out")),
        w_down=tuple(_adamw_halves(half_down, their_down, core, w_down[0], m_w_down[0], v_w_down[0], axis=0, name="adamw_down")),
        w_ada=tuple(ada_out),
    )
    order = ("w_ada", "b_ada", "pre_w_mix", "w_in", "attn_sinks", "attn_out_w", "lb_table", "hg_norm_w", "w_out", "post_w_mix",
             "pre_w_mlp", "w_up", "w_down", "post_w_mlp")
    outs = [loss, grad_x]
    for kind in range(4):
        for nm in order:
            outs.append(big[nm][kind][None] if nm in big else small_out[kind][nm])
    return tuple(outs)
```

```python
import jax
import jax.numpy as jnp
from jax import lax
from jax.experimental import pallas as pl
from jax.experimental.pallas import tpu as pltpu

F32 = jnp.float32
BF16 = jnp.bfloat16

D_MODEL = 1024
ATT_WIDTH = 512
ATT_HEAD_DIM = 64
ATT_Q_HEADS = 8
ATT_KV_HEADS = 2
ATT_GROUP = ATT_Q_HEADS // ATT_KV_HEADS
ATT_KV_COLS = ATT_KV_HEADS * ATT_HEAD_DIM
WINDOW = 128
ROPE_DIM = 16
ROPE_THETA = 500000.0
HG_WIDTH = 512
MIX_WIDTH = ATT_WIDTH + HG_WIDTH
HG_HEAD_DIM = 128
HG_HEADS = 4
HG_CHUNK = 32
IN_COLS = ATT_WIDTH + 2 * ATT_KV_COLS + 4 * HG_WIDTH
ATT_COLS = ATT_WIDTH + 2 * ATT_KV_COLS
D_FF = 4 * D_MODEL
N_MOD = 6
EPS = 1e-6
ATT_SCALE = ATT_HEAD_DIM ** -0.5

ADAM_LR = 0.001
ADAM_B1 = 0.9
ADAM_B2 = 0.999
ADAM_EPS = 1e-08
ADAM_WD = 0.01
ADAM_STEP = 10

N_CHIPS = 4
N_DEV = 8
LANE = 128
VMEM_LIMIT = 48 * 1024 * 1024
VMEM_LIMIT_BIG = 58 * 1024 * 1024
MESH = pl.DeviceIdType.MESH

NT_DIMS = (((1,), (1,)), ((), ()))
TN_DIMS = (((0,), (0,)), ((), ()))


def _sds(shape, dtype):
    return jax.ShapeDtypeStruct(tuple(shape), dtype)


def _params(*sem, vmem_limit=None):
    return pltpu.CompilerParams(dimension_semantics=sem, vmem_limit_bytes=VMEM_LIMIT if vmem_limit is None else vmem_limit)


def _sigmoid(x):
    return 1.0 / (1.0 + jnp.exp(-x))


def _dot(a, b, dims=None):
    a, b = a.astype(BF16), b.astype(BF16)
    if dims is None:
        return jnp.dot(a, b, preferred_element_type=F32)
    return lax.dot_general(a, b, dims, preferred_element_type=F32)


def _rms_fwd(x, w):
    rstd = lax.rsqrt(jnp.mean(x * x, axis=-1, keepdims=True) + EPS)
    xh = x * rstd
    return xh * w, xh, rstd


def _rms_bwd(dy, xh, rstd, w):
    dxh = dy * w
    dx = rstd * (dxh - xh * jnp.mean(dxh * xh, axis=-1, keepdims=True))
    return dx, dy * xh


def _colsum(x):
    return jnp.sum(x, axis=0, keepdims=True)


def _rms_hat(x):
    rstd = lax.rsqrt(jnp.mean(x * x, axis=-1, keepdims=True) + EPS)
    return x * rstd, rstd


def _rms_bwd_gain(dy, gain, xh, rstd):
    dxh = dy * gain
    dx = rstd * (dxh - xh * jnp.mean(dxh * xh, axis=-1, keepdims=True))
    return dx, _colsum(dy * xh)


def _row_tile(rows, cap=256):
    return max(t for t in range(16, cap + 1, 16) if rows % t == 0)


HBM_SPEC = pl.BlockSpec(memory_space=pltpu.HBM)


def _mesh_pos():
    return lax.axis_index("x"), lax.axis_index("y"), lax.axis_index("c")


class _Comm:
    def __init__(self, ins, outs, sems, start, finish, aliases=()):
        self.ins, self.outs, self.sems = list(ins), list(outs), list(sems)
        self.start, self.finish, self.aliases = start, finish, tuple(aliases)


def _call(body, args, *, name, grid, in_specs, out_specs, out_shape, sem, scratch_shapes=(), comms=(), aliases=None,
          vmem_limit=None):
    scratch_shapes = list(scratch_shapes)
    if not comms:
        return pl.pallas_call(body, name=name, grid=grid, in_specs=in_specs, out_specs=out_specs, out_shape=out_shape,
                              input_output_aliases=dict(aliases or {}), scratch_shapes=scratch_shapes,
                              compiler_params=_params(*sem, vmem_limit=vmem_limit))(*args)
    single = not isinstance(out_shape, (list, tuple))
    out_specs_l = [out_specs] if single else list(out_specs)
    out_shape_l = [out_shape] if single else list(out_shape)
    n_in, n_out, n_scr = len(in_specs), len(out_shape_l), len(scratch_shapes)
    n_ci = [len(cm.ins) for cm in comms]
    n_co = [len(cm.outs) for cm in comms]
    n_cs = [len(cm.sems) for cm in comms]
    aliases = dict(aliases or {})
    for k, cm in enumerate(comms):
        for i, o in cm.aliases:
            aliases[n_in + sum(n_ci[:k]) + i] = n_out + sum(n_co[:k]) + o

    def fused(*refs):
        pos = [0]

        def take(n):
            part = refs[pos[0]:pos[0] + n]
            pos[0] += n
            return part

        ins = take(n_in)
        c_ins = [take(n) for n in n_ci]
        outs = take(n_out)
        c_outs = [take(n) for n in n_co]
        scr = take(n_scr)
        c_sems = [take(n) for n in n_cs]
        first, last = True, True
        for d, size in enumerate(grid):
            first = jnp.logical_and(first, pl.program_id(d) == 0)
            last = jnp.logical_and(last, pl.program_id(d) == size - 1)

        def run(which):
            for cm, ci, co, cs in zip(comms, c_ins, c_outs, c_sems):
                getattr(cm, which)(ci, co, cs)

        if grid:
            pl.when(first)(lambda: run("start"))
        else:
            run("start")
        body(*ins, *outs, *scr)
        if grid:
            pl.when(last)(lambda: run("finish"))
        else:
            run("finish")

    res = pl.pallas_call(
        fused, name=name, grid=grid, in_specs=list(in_specs) + [HBM_SPEC] * sum(n_ci),
        out_specs=out_specs_l + [HBM_SPEC] * sum(n_co), out_shape=out_shape_l + [s for cm in comms for s in cm.outs],
        input_output_aliases=aliases, scratch_shapes=scratch_shapes + [s for cm in comms for s in cm.sems],
        compiler_params=_params(*["arbitrary"] * len(grid), vmem_limit=vmem_limit),
    )(*args, *[a for cm in comms for a in cm.ins])
    main = res[:n_out]
    extra, at = [], n_out
    for n in n_co:
        extra.append(list(res[at:at + n]))
        at += n
    return (main[0] if single else list(main)), extra


def _mm(a, b, *, name, out_dtype, trans_b=False, tm=512, tn=None, extra=(), epi=None, b_spec=None, n_out=None, comms=()):
    m_total, k_total = a.shape
    if n_out is None:
        n_out = b.shape[0] if trans_b else b.shape[1]
    tn = n_out if tn is None else tn
    grid = (m_total // tm, n_out // tn)
    dims = NT_DIMS if trans_b else None

    def body(*refs):
        a_ref, b_ref = refs[0], refs[1]
        extra_refs = refs[2:2 + len(extra)]
        o_ref = refs[2 + len(extra)]
        acc = _dot(a_ref[...], b_ref[...], dims)
        if epi is not None:
            acc = epi(acc, *[r[...] for r in extra_refs])
        o_ref[...] = acc.astype(out_dtype)

    if b_spec is None:
        if trans_b:
            b_spec = pl.BlockSpec((tn, k_total), lambda i, j: (j, 0))
        else:
            b_spec = pl.BlockSpec((k_total, tn), lambda i, j: (0, j))
    in_specs = [pl.BlockSpec((tm, k_total), lambda i, j: (i, 0)), b_spec]
    in_specs += [pl.BlockSpec((tm, tn), lambda i, j: (i, j)) for _ in extra]
    return _call(
        body, (a, b, *extra), name=name, grid=grid, in_specs=in_specs,
        out_specs=pl.BlockSpec((tm, tn), lambda i, j: (i, j)),
        out_shape=_sds((m_total, n_out), out_dtype),
        sem=("parallel", "parallel"), comms=comms)


def _mm_tn(a, b, *, name, tk, tn, a_fn=None, out_shape=None, out_spec=None, comms=()):
    m_total, k_total = a.shape
    n_total = b.shape[1]
    grid = (k_total // tk, n_total // tn)

    def body(a_ref, b_ref, o_ref):
        av = a_ref[...]
        part = _dot(av if a_fn is None else a_fn(av), b_ref[...], TN_DIMS)
        o_ref[...] = part.reshape(o_ref.shape)

    if out_shape is None:
        out_shape = _sds((k_total, n_total), F32)
        out_spec = pl.BlockSpec((tk, tn), lambda i, j: (i, j))
    return _call(
        body, (a, b), name=name, grid=grid,
        in_specs=[pl.BlockSpec((m_total, tk), lambda i, j: (0, i)), pl.BlockSpec((m_total, tn), lambda i, j: (0, j))],
        out_specs=out_spec, out_shape=out_shape, sem=("parallel", "parallel"), comms=comms)


def _ada_fwd(c_all, w_shard, b_shard):
    nb, ncol = c_all.shape[0], w_shard.shape[1]
    tn = 512

    def body(c_ref, w_ref, b_ref, o_ref):
        c = c_ref[...]
        o_ref[...] = _dot(c * _sigmoid(c), w_ref[...]) + b_ref[...]

    return pl.pallas_call(
        body, name="ada_fwd", grid=(ncol // tn,),
        in_specs=[pl.BlockSpec((nb, D_MODEL), lambda j: (0, 0)), pl.BlockSpec((D_MODEL, tn), lambda j: (0, j)),
                  pl.BlockSpec((1, tn), lambda j: (0, j))],
        out_specs=pl.BlockSpec((nb, tn), lambda j: (0, j)), out_shape=_sds((nb, ncol), F32),
        compiler_params=_params("parallel"),
    )(c_all, w_shard, b_shard)


def _adamw_math(g, w, m, v):
    m = ADAM_B1 * m + (1.0 - ADAM_B1) * g
    v = ADAM_B2 * v + (1.0 - ADAM_B2) * (g * g)
    m_hat = m / (1.0 - ADAM_B1 ** ADAM_STEP)
    v_hat = v / (1.0 - ADAM_B2 ** ADAM_STEP)
    delta = -ADAM_LR * (m_hat / (jnp.sqrt(v_hat) + ADAM_EPS) + ADAM_WD * w)
    return delta, m, v


def _ada_bwd_adamw(c_all, dmod_cols, w, m, v):
    nb, ncol = dmod_cols.shape
    tn = 256

    def body(c_ref, d_ref, w_ref, m_ref, v_ref, g_ref, dl_ref, nm_ref, nv_ref):
        c = c_ref[...]
        g = _dot(c * _sigmoid(c), d_ref[...], TN_DIMS)
        g_ref[...] = g
        dl_ref[...], nm_ref[...], nv_ref[...] = _adamw_math(g, w_ref[...], m_ref[...], v_ref[...])

    col = pl.BlockSpec((D_MODEL, tn), lambda j: (0, j))
    shp = _sds((D_MODEL, ncol), F32)
    return pl.pallas_call(
        body, name="ada_bwd_adamw", grid=(ncol // tn,),
        in_specs=[pl.BlockSpec((nb, D_MODEL), lambda j: (0, 0)), pl.BlockSpec((nb, tn), lambda j: (0, j)), col, col, col],
        out_specs=[col, col, col, col], out_shape=[shp, shp, shp, shp],
        compiler_params=_params("parallel"),
    )(c_all, dmod_cols, w, m, v)


def _adamw_halves(own, theirs, core, w, m, v, *, axis, name):
    r2, c2 = own.shape
    tr = _row_tile(r2)
    nt = r2 // tr

    def body(core_ref, own_ref, their_ref, w_ref, m_ref, v_ref, g_ref, dl_ref, nm_ref, nv_ref):
        g = jnp.where(pl.program_id(0) == core_ref[0], own_ref[...], their_ref[...])
        g_ref[...] = g
        dl_ref[...], nm_ref[...], nv_ref[...] = _adamw_math(g, w_ref[...], m_ref[...], v_ref[...])

    if axis == 0:
        full = pl.BlockSpec((tr, c2), lambda h, i, core_ref: (h * nt + i, 0))
    else:
        full = pl.BlockSpec((tr, c2), lambda h, i, core_ref: (i, h))
    half = pl.BlockSpec((tr, c2), lambda h, i, core_ref: (i, 0))
    shp = _sds(w.shape, F32)
    return pl.pallas_call(
        body, name=name,
        grid_spec=pltpu.PrefetchScalarGridSpec(num_scalar_prefetch=1, grid=(2, nt), in_specs=[half, half, full, full, full],
                                               out_specs=[full] * 4),
        out_shape=[shp] * 4, compiler_params=_params("parallel", "parallel"),
    )(core, own, theirs, w, m, v)


def _tok_spec(tm, width=D_MODEL):
    return pl.BlockSpec((None, tm, width), lambda b, i: (b, i, 0))


def _row_spec(width=D_MODEL):
    return pl.BlockSpec((None, 1, width), lambda b, i: (b, 0, 0))


def _vec_spec(width=D_MODEL):
    return pl.BlockSpec((1, width), lambda b, i: (0, 0))


class _RowsOf:
    def __init__(self, ref, first, count):
        self.ref, self.rows = ref, slice(first, first + count)

    def __getitem__(self, idx):
        return self.ref[self.rows, :]

    def __setitem__(self, idx, value):
        self.ref[self.rows, :] = value


def _mm_rows(a, b, *, name, tm, extra, extra_specs, out_specs, out_shape, epi, pro=None, trans_b=False, b_chunks=1, comms=(),
             parts=1, zero_per_seq=(), zero_once=(), vmem_limit=None):
    bsz, seq, k_total = a.shape
    kc = k_total // b_chunks
    dims = NT_DIMS if trans_b else None
    rows = tm // parts

    def body(*refs):
        a_ref, b_ref = refs[0], refs[1]
        ex, outs = refs[2:2 + len(extra)], refs[2 + len(extra):]
        if zero_per_seq:
            @pl.when(pl.program_id(1) == 0)
            def _():
                for k in zero_per_seq:
                    outs[k][...] = jnp.zeros_like(outs[k])
        if zero_once:
            @pl.when(jnp.logical_and(pl.program_id(0) == 0, pl.program_id(1) == 0))
            def _():
                for k in zero_once:
                    outs[k][...] = jnp.zeros_like(outs[k])

        def part_of(ref, p):
            tiled = len(ref.shape) == 2 and ref.shape[0] == tm
            return _RowsOf(ref, p * rows, rows) if tiled and parts > 1 else ref

        accs = []
        for p in range(parts):
            a_p, ex_p, outs_p = part_of(a_ref, p), [part_of(r, p) for r in ex], [part_of(r, p) for r in outs]
            if b_chunks == 1:
                accs.append(_dot(a_p[...] if pro is None else pro(a_p, ex_p, outs_p), b_ref[...], dims))
            else:
                acc = _dot(a_p[...][:, 0:kc], b_ref[0], NT_DIMS)
                for k in range(1, b_chunks):
                    acc = acc + _dot(a_p[...][:, k * kc:(k + 1) * kc], b_ref[k], NT_DIMS)
                accs.append(acc)
        for p in range(parts):
            epi(accs[p], [part_of(r, p) for r in ex], [part_of(r, p) for r in outs])

    b_spec = pl.BlockSpec(b.shape, lambda bb, i: (0,) * b.ndim)
    return _call(
        body, (a, b, *extra), name=name, grid=(bsz, seq // tm), in_specs=[_tok_spec(tm, k_total), b_spec, *extra_specs],
        out_specs=out_specs, out_shape=out_shape, sem=("arbitrary", "arbitrary"), comms=comms, vmem_limit=vmem_limit)


def _in_proj_fused(x, w, sc, sh, w_in_t, tables, comms=()):
    tm = 512
    bsz, seq, _ = x.shape
    half = ROPE_DIM // 2
    heads_per_slab = LANE // ATT_HEAD_DIM

    def pro(x_ref, ex, outs):
        y, _, _ = _rms_fwd(x_ref[...], ex[0][...])
        h = (y * (1.0 + ex[1][...]) + ex[2][...]).astype(BF16)
        outs[0][...] = h
        return h

    def epi(acc, ex, outs):
        c, u, d = ex[3][...], ex[4][...], ex[5][...]
        _, rec_ref, q_ref, k_ref, v_ref = outs
        for k in range(HG_SLABS):
            rec_ref[k] = acc[:, ATT_COLS + k * HG_WIDTH:ATT_COLS + (k + 1) * HG_WIDTH]

        def rope(z):
            return (z * c + pltpu.roll(z, half, 1) * u + pltpu.roll(z, LANE - half, 1) * d).astype(BF16)

        for s in range(ATT_WIDTH // LANE):
            slab = rope(acc[:, s * LANE:(s + 1) * LANE])
            for part in range(heads_per_slab):
                g, hh = divmod(s * heads_per_slab + part, ATT_GROUP)
                piece = slab[:, part * ATT_HEAD_DIM:(part + 1) * ATT_HEAD_DIM]
                for blk in range(tm // WINDOW):
                    q_ref[blk, g, hh * WINDOW:(hh + 1) * WINDOW, :] = piece[blk * WINDOW:(blk + 1) * WINDOW]
        rk = rope(acc[:, ATT_WIDTH:ATT_WIDTH + LANE])
        vv = acc[:, ATT_WIDTH + LANE:ATT_COLS].astype(BF16)
        for g in range(ATT_KV_HEADS):
            k_ref[g] = rk[:, g * ATT_HEAD_DIM:(g + 1) * ATT_HEAD_DIM]
            v_ref[g] = vv[:, g * ATT_HEAD_DIM:(g + 1) * ATT_HEAD_DIM]

    tab = pl.BlockSpec((tm, LANE), lambda b, i: (i, 0))
    kv_spec = pl.BlockSpec((None, ATT_KV_HEADS, tm, ATT_HEAD_DIM), lambda b, i: (b, 0, i, 0))
    kv_shape = _sds((bsz, ATT_KV_HEADS, seq, ATT_HEAD_DIM), BF16)
    q_spec = pl.BlockSpec((None, tm // WINDOW, ATT_KV_HEADS, GROUP_ROWS, ATT_HEAD_DIM), lambda b, i: (b, i, 0, 0, 0))
    return _mm_rows(x, w_in_t, name="in_proj", tm=tm, extra=(w, sc, sh, *tables),
                    extra_specs=[_vec_spec(), _row_spec(), _row_spec(), tab, tab, tab],
                    out_specs=[_tok_spec(tm), pl.BlockSpec((None, HG_SLABS, tm, HG_WIDTH), lambda b, i: (b, 0, i, 0)), q_spec,
                               kv_spec, kv_spec],
                    out_shape=[_sds(x.shape, BF16), _sds((bsz, HG_SLABS, seq, HG_WIDTH), F32),
                               _sds((bsz, seq // WINDOW, ATT_KV_HEADS, GROUP_ROWS, ATT_HEAD_DIM), BF16), kv_shape, kv_shape],
                    pro=pro, epi=epi, trans_b=True, comms=comms)


def _rope_tables(seq):
    half = ROPE_DIM // 2
    inv_freq = ROPE_THETA ** (-jnp.arange(0, ROPE_DIM, 2, dtype=F32) / ROPE_DIM)
    ang = jnp.arange(seq, dtype=F32)[:, None] * inv_freq[None, :]
    cos, sin = jnp.cos(ang), jnp.sin(ang)
    rest = ATT_HEAD_DIM - ROPE_DIM
    ones, zeros, zh = jnp.ones((seq, rest), F32), jnp.zeros((seq, rest), F32), jnp.zeros((seq, half), F32)
    reps = LANE // ATT_HEAD_DIM
    t_cos = jnp.tile(jnp.concatenate([cos, cos, ones], axis=1), (1, reps))
    t_up = jnp.tile(jnp.concatenate([zh, sin, zeros], axis=1), (1, reps))
    t_dn = jnp.tile(jnp.concatenate([-sin, zh, zeros], axis=1), (1, reps))
    return t_cos, t_up, t_dn


GROUP_ROWS = ATT_GROUP * WINDOW


ATT_BPS = 2


MASKED = -1e30


def _band_biases():
    row = jnp.arange(GROUP_ROWS)[:, None] % WINDOW
    col = jnp.arange(2 * WINDOW)[None, :]
    own = jnp.logical_and(col >= WINDOW, col - WINDOW <= row)
    before = jnp.logical_and(col < WINDOW, col > row)
    return (jnp.where(jnp.logical_or(own, before), 0.0, MASKED).astype(F32), jnp.where(own, 0.0, MASKED).astype(F32))


def _band_bias(full_ref, first_ref, has_prev):
    return full_ref[...] if has_prev is True else jnp.where(has_prev, full_ref[...], first_ref[...])


def _sink_column(sink_ref, g):
    head = lax.broadcasted_iota(jnp.int32, (GROUP_ROWS, 1), 0) // WINDOW
    col = jnp.full((GROUP_ROWS, 1), sink_ref[0, g * ATT_GROUP], F32)
    for hh in range(1, ATT_GROUP):
        col = jnp.where(head == hh, sink_ref[0, g * ATT_GROUP + hh], col)
    return col


def _sink_row(sink_ref, g):
    return jnp.concatenate([jnp.full((1, WINDOW), sink_ref[0, g * ATT_GROUP + hh], F32) for hh in range(ATT_GROUP)], axis=1)


def _bias_spec(transposed=False):
    shape = (2 * WINDOW, GROUP_ROWS) if transposed else (GROUP_ROWS, 2 * WINDOW)
    return pl.BlockSpec(shape, lambda b, i: (0, 0))


def _attn_specs():
    q_spec = pl.BlockSpec((None, ATT_BPS, ATT_KV_HEADS, GROUP_ROWS, ATT_HEAD_DIM), lambda b, i: (b, i, 0, 0, 0))
    kv_cur = pl.BlockSpec((None, ATT_KV_HEADS, ATT_BPS * WINDOW, ATT_HEAD_DIM), lambda b, i: (b, 0, i, 0))
    kv_prev = pl.BlockSpec((None, ATT_KV_HEADS, WINDOW, ATT_HEAD_DIM), lambda b, i: (b, 0, jnp.maximum(ATT_BPS * i - 1, 0), 0))
    return q_spec, kv_cur, kv_prev


def _band(prev_ref, cur_ref, g, blk):
    own = cur_ref[g, blk * WINDOW:(blk + 1) * WINDOW]
    before = prev_ref[g] if blk == 0 else cur_ref[g, (blk - 1) * WINDOW:blk * WINDOW]
    return jnp.concatenate([before, own], axis=0)


def _attn_fwd(qh, kh, vh, sinks, w_norm, biases, comms=()):
    bsz, nblk = qh.shape[0], qh.shape[1]
    seq = nblk * WINDOW
    rows = ATT_BPS * WINDOW

    def body(sink_ref, q_ref, kc_ref, kp_ref, vc_ref, vp_ref, w_ref, full_ref, first_ref, raw_ref, an_ref, l_ref):
        l_ref[...] = jnp.zeros_like(l_ref)
        def block(blk):
            bias = _band_bias(full_ref, first_ref, True if blk else pl.program_id(1) > 0)
            groups = range(ATT_KV_HEADS)
            keys, vals = [_band(kp_ref, kc_ref, g, blk) for g in groups], [_band(vp_ref, vc_ref, g, blk) for g in groups]
            sink = [_sink_column(sink_ref, g) for g in groups]
            s = [_dot(q_ref[blk, g], keys[g], NT_DIMS) * ATT_SCALE + bias for g in groups]
            yield
            m = [jnp.maximum(jnp.max(s[g], axis=-1, keepdims=True), sink[g]) for g in groups]
            p = [jnp.exp(s[g] - m[g]) for g in groups]
            den = [jnp.sum(p[g], axis=-1, keepdims=True) + jnp.exp(sink[g] - m[g]) for g in groups]
            yield
            o = [_dot(p[g] / den[g], vals[g]) for g in groups]
            yield
            lse = [m[g] + jnp.log(den[g]) for g in groups]
            tok = slice(blk * WINDOW, (blk + 1) * WINDOW)
            for g in groups:
                for hh in range(ATT_GROUP):
                    h = g * ATT_GROUP + hh
                    raw_ref[tok, h * ATT_HEAD_DIM:(h + 1) * ATT_HEAD_DIM] = o[g][hh * WINDOW:(hh + 1) * WINDOW]
                    l_ref[tok, h:h + 1] = lse[g][hh * WINDOW:(hh + 1) * WINDOW]

        _in_step(block(blk) for blk in range(ATT_BPS))
        y, _, _ = _rms_fwd(raw_ref[...], w_ref[...])
        an_ref[...] = y.astype(BF16)

    cur = lambda width: pl.BlockSpec((None, rows, width), lambda b, i: (b, i, 0))
    q_spec, kv_cur, kv_prev = _attn_specs()
    return _call(
        body, (sinks, qh, kh, kh, vh, vh, w_norm, *biases), name="attn_fwd", grid=(bsz, nblk // ATT_BPS),
        in_specs=[pl.BlockSpec(memory_space=pltpu.SMEM), q_spec, kv_cur, kv_prev, kv_cur, kv_prev, _vec_spec(ATT_WIDTH),
                  _bias_spec(), _bias_spec()],
        out_specs=[cur(ATT_WIDTH), cur(ATT_WIDTH), cur(LANE)],
        out_shape=[_sds((bsz, seq, ATT_WIDTH), F32), _sds((bsz, seq, MIX_WIDTH), BF16), _sds((bsz, seq, LANE), F32)],
        sem=("parallel", "parallel"), comms=comms)


HG_Q0 = ATT_COLS // LANE
HG_F0 = HG_Q0 + HG_HEADS
HG_I0 = HG_F0 + HG_HEADS
HG_G0 = HG_I0 + HG_HEADS
HG_SLABS = 4
HG_Q, HG_F, HG_I, HG_G = range(HG_SLABS)
HG_TOK = 256
HG_NCH = HG_TOK // HG_CHUNK
HG_HPS = 2
HG_FWD_BLOCKS = 8
HG_BWD_BLOCKS = 4


def _block_masks():
    row = jnp.arange(HG_TOK)[:, None]
    col = jnp.arange(HG_TOK)[None, :]
    same = (row // HG_CHUNK) == (col // HG_CHUNK)
    return jnp.logical_and(same, col <= row).astype(F32), jnp.logical_and(same, col >= row).astype(F32)


def _row_in_chunk():
    return lax.broadcasted_iota(jnp.int32, (HG_TOK, LANE), 0) % HG_CHUNK


def _chunk_cumsum(x, reverse=False):
    ric = _row_in_chunk()
    shift = 1
    while shift < HG_CHUNK:
        if reverse:
            x = x + jnp.where(ric < HG_CHUNK - shift, pltpu.roll(x, HG_TOK - shift, 0), 0.0)
        else:
            x = x + jnp.where(ric >= shift, pltpu.roll(x, shift, 0), 0.0)
        shift *= 2
    return x


def _chunk_rows(rows):
    stacked = jnp.concatenate([r[None] for r in rows], axis=0)
    return jnp.broadcast_to(stacked, (HG_NCH, HG_CHUNK, LANE)).reshape(HG_TOK, LANE)


def _chunk_slices(x):
    return [x[j * HG_CHUNK:(j + 1) * HG_CHUNK] for j in range(HG_NCH)]


def _in_step(stages):
    stages = list(stages)
    while stages:
        stages = [g for g in stages if next(g, stages) is not stages]


def _hgrn_common(tbl, hf, hq):
    lb = _sigmoid(tbl[1:2] - tbl[0:1])
    sig = _sigmoid(hf)
    f = lb + (1.0 - lb) * sig
    sq = _sigmoid(hq)
    q, k = hq * sq, 1.0 - f
    b = _chunk_cumsum(jnp.log(f))
    last = [b[(j + 1) * HG_CHUNK - 1:(j + 1) * HG_CHUNK] for j in range(HG_NCH)]
    bl = _chunk_rows(last)
    e_b, e_nb, e_rem = jnp.exp(b), jnp.exp(-b), jnp.exp(bl - b)
    e_last = [jnp.exp(r) for r in last]
    return dict(lb=lb, sig=sig, f=f, sq=sq, q=q, k=k, e_b=e_b, e_nb=e_nb, e_rem=e_rem, e_last=e_last,
                qd=q * e_b, kd=k * e_nb, ku=k * e_rem)


def _hgrn_fwd(proj, lb_table, norm_w, mix_in, masks, comms=()):
    bsz, _, seq, _ = proj.shape
    nstep = seq // HG_TOK

    def body(tbl_ref, nw_ref, p_ref, mix_ref, lower_ref, o_ref, rec_ref, st_ref, s_scr):
        @pl.when(pl.program_id(2) == 0)
        def _():
            s_scr[...] = jnp.zeros_like(s_scr)

        lower = lower_ref[...]

        def head(hp, sub):
            ls = slice(hp * LANE, (hp + 1) * LANE)
            rows = slice(sub * HG_TOK, (sub + 1) * HG_TOK)
            v, hg = p_ref[HG_I, rows, ls], p_ref[HG_G, rows, ls]
            t = _hgrn_common(tbl_ref[:, ls], p_ref[HG_F, rows, ls], p_ref[HG_Q, rows, ls])
            yield
            a = _dot(t["qd"], t["kd"], NT_DIMS) * lower
            o_intra = _dot(a, v)
            yield
            v_c, ku_c, qd_c = [_chunk_slices(z.astype(BF16)) for z in (v, t["ku"], t["qd"])]
            updates = [_dot(v_c[j], ku_c[j], TN_DIMS) for j in range(HG_NCH)]
            yield
            st = s_scr[hp]
            states = []
            for j in range(HG_NCH):
                states.append(st)
                st = st * t["e_last"][j] + updates[j]
            s_scr[hp] = st
            yield
            o = o_intra + jnp.concatenate([_dot(qd_c[j], states[j], NT_DIMS) for j in range(HG_NCH)], axis=0)
            yield
            st_ref[hp, sub] = states[0]
            o_ref[rows, ls] = o
            y, _, _ = _rms_fwd(o, nw_ref[...])
            rec_ref[rows, ls] = (y * (hg * _sigmoid(hg))).astype(BF16)

        for sub in range(HG_FWD_BLOCKS):
            _in_step(head(hp, sub) for hp in range(HG_HPS))

    width = HG_HPS * LANE
    tok = HG_FWD_BLOCKS * HG_TOK
    head_out = pl.BlockSpec((None, tok, width), lambda b, h, t: (b, t, h))
    mix_out = pl.BlockSpec((None, tok, width), lambda b, h, t: (b, t, ATT_WIDTH // width + h))
    return _call(
        body, (lb_table, norm_w, proj, mix_in, masks[0]), name="hgrn_fwd",
        grid=(bsz, HG_HEADS // HG_HPS, nstep // HG_FWD_BLOCKS),
        in_specs=[pl.BlockSpec((2, width), lambda b, h, t: (0, h)), pl.BlockSpec((1, LANE), lambda b, h, t: (0, 0)),
                  pl.BlockSpec((None, HG_SLABS, tok, width), lambda b, h, t: (b, 0, t, h)), pl.BlockSpec(memory_space=pl.ANY),
                  pl.BlockSpec((HG_TOK, HG_TOK), lambda b, h, t: (0, 0))],
        out_specs=[head_out, mix_out,
                   pl.BlockSpec((None, HG_HPS, HG_FWD_BLOCKS, LANE, LANE), lambda b, h, t: (b, h, t, 0, 0))],
        out_shape=[_sds((bsz, seq, HG_WIDTH), F32), _sds(mix_in.shape, BF16),
                   _sds((bsz, HG_HEADS, nstep, LANE, LANE), F32)],
        scratch_shapes=[pltpu.VMEM((HG_HPS, LANE, LANE), F32)],
        sem=("parallel", "parallel", "arbitrary"), comms=comms, aliases={3: 1})


def _out_proj_fused(cat, w_out, x, post_w, g1, pre_w, sc2, sh2):
    tm = 512

    def epi(mix, ex, outs):
        x_ref, pw_ref, g1_ref, w2_ref, sc_ref, sh_ref = ex
        outs[0][...] = mix
        n1, _, _ = _rms_fwd(mix, pw_ref[...])
        x1 = x_ref[...] + g1_ref[...] * n1
        outs[1][...] = x1
        y2, _, _ = _rms_fwd(x1, w2_ref[...])
        outs[2][...] = (y2 * (1.0 + sc_ref[...]) + sh_ref[...]).astype(BF16)

    return _mm_rows(cat, w_out, name="out_proj", tm=tm, extra=(x, post_w, g1, pre_w, sc2, sh2),
                    extra_specs=[_tok_spec(tm), _vec_spec(), _row_spec(), _vec_spec(), _row_spec(), _row_spec()],
                    out_specs=[_tok_spec(tm), _tok_spec(tm), _tok_spec(tm)],
                    out_shape=[_sds(x.shape, F32), _sds(x.shape, F32), _sds(x.shape, BF16)], epi=epi)


def _acc_out(ref, first, value):
    @pl.when(first)
    def _():
        ref[...] = value

    @pl.when(jnp.logical_not(first))
    def _():
        ref[...] += value


def _down_proj_fused(r, w_down, x1, post_w, g2, target):
    tm = 512
    bsz = x1.shape[0]

    def pro(r_ref, ex, outs):
        rv = r_ref[...]
        return rv * rv

    def epi(down, ex, outs):
        x1_ref, w_ref, g2_ref, t_ref = ex
        loss_ref, dy_ref, dd_ref, dg2_ref, dw_ref = outs
        w, g2v = w_ref[...], g2_ref[...]
        gain = g2v * w
        dh, rstd = _rms_hat(down)
        err = x1_ref[...] + dh * gain - t_ref[...]
        part = (0.5 / D_MODEL) * jnp.sum(jnp.sum(err * err, axis=-1, keepdims=True), axis=0, keepdims=True)
        loss_ref[...] += jnp.broadcast_to(part, (1, LANE))
        dy = err * (1.0 / D_MODEL)
        dy_ref[...] = dy
        dd, per_col = _rms_bwd_gain(dy, gain, dh, rstd)
        dd_ref[...] = dd.astype(BF16)
        dg2_ref[...] += per_col * w
        dw_ref[...] += per_col * g2v

    return _mm_rows(r, w_down, name="down_proj", tm=tm, extra=(x1, post_w, g2, target),
                    extra_specs=[_tok_spec(tm), _vec_spec(), _row_spec(), _tok_spec(tm)],
                    out_specs=[_vec_spec(LANE), _tok_spec(tm), _tok_spec(tm), _row_spec(), _vec_spec()],
                    out_shape=[_sds((1, LANE), F32), _sds(x1.shape, F32), _sds(x1.shape, BF16), _sds((bsz, 1, D_MODEL), F32),
                               _sds((1, D_MODEL), F32)], pro=pro, epi=epi, parts=2, zero_per_seq=(3,), zero_once=(0, 4),
                    vmem_limit=VMEM_LIMIT_BIG)


def _up_bwd_fused(dpre, w_up4, dy, x1, mix, pre_w, sc2, post_w, g1, comms=()):
    tm = 512
    bsz = x1.shape[0]

    def epi(dh2v, ex, outs):
        dy_ref, x1_ref, mix_ref, w2_ref, sc_ref, pw_ref, g1_ref = ex
        dx1_ref, dmix_ref, dsc_ref, dsh_ref, dg1_ref, dw2_ref, dpw_ref = outs
        w2, pw, g1v = w2_ref[...], pw_ref[...], g1_ref[...]
        mod2 = 1.0 + sc_ref[...]
        xh2, rstd2 = _rms_hat(x1_ref[...])
        dsh_ref[...] += _colsum(dh2v)
        dx1n, per_col2 = _rms_bwd_gain(dh2v, mod2 * w2, xh2, rstd2)
        dsc_ref[...] += per_col2 * w2
        dw2_ref[...] += per_col2 * mod2
        dx1 = dy_ref[...] + dx1n
        dx1_ref[...] = dx1
        mh, rstd1 = _rms_hat(mix_ref[...])
        dmix, per_col1 = _rms_bwd_gain(dx1, g1v * pw, mh, rstd1)
        dmix_ref[...] = dmix.astype(BF16)
        dg1_ref[...] += per_col1 * pw
        dpw_ref[...] += per_col1 * g1v

    row_shape = _sds((bsz, 1, D_MODEL), F32)
    vec_shape = _sds((1, D_MODEL), F32)
    return _mm_rows(dpre, w_up4, name="up_bwd", tm=tm, extra=(dy, x1, mix, pre_w, sc2, post_w, g1),
                    extra_specs=[_tok_spec(tm), _tok_spec(tm), _tok_spec(tm), _vec_spec(), _row_spec(), _vec_spec(), _row_spec()],
                    out_specs=[_tok_spec(tm), _tok_spec(tm), _row_spec(), _row_spec(), _row_spec(), _vec_spec(), _vec_spec()],
                    out_shape=[_sds(x1.shape, F32), _sds(x1.shape, BF16), row_shape, row_shape, row_shape, vec_shape, vec_shape],
                    epi=epi, b_chunks=w_up4.shape[0], comms=comms, parts=2, zero_per_seq=(2, 3, 4), zero_once=(5, 6),
                    vmem_limit=VMEM_LIMIT_BIG)


def _norm1_bwd(dh1, dx1, x, pre_w, sc1, tm=512, comms=()):
    bsz, seq, _ = x.shape

    def body(dh_ref, dx1_ref, x_ref, w_ref, sc_ref, gx_ref, dsc_ref, dsh_ref, dw_ref):
        b, i = pl.program_id(0), pl.program_id(1)
        w = w_ref[...]
        dh = dh_ref[...]
        mod = 1.0 + sc_ref[...]
        xh, rstd = _rms_hat(x_ref[...])
        dx, per_col = _rms_bwd_gain(dh, mod * w, xh, rstd)
        _acc_out(dsh_ref, i == 0, _colsum(dh))
        _acc_out(dsc_ref, i == 0, per_col * w)
        _acc_out(dw_ref, jnp.logical_and(b == 0, i == 0), per_col * mod)
        gx_ref[...] = dx1_ref[...] + dx

    row_shape = _sds((bsz, 1, D_MODEL), F32)
    return _call(
        body, (dh1, dx1, x, pre_w, sc1), name="norm1_bwd", grid=(bsz, seq // tm),
        in_specs=[_tok_spec(tm), _tok_spec(tm), _tok_spec(tm), _vec_spec(), _row_spec()],
        out_specs=[_tok_spec(tm), _row_spec(), _row_spec(), _vec_spec()],
        out_shape=[_sds(x.shape, F32), row_shape, row_shape, _sds((1, D_MODEL), F32)],
        sem=("arbitrary", "arbitrary"), comms=comms)


def _hgrn_bwd(dcat, proj, o_raw, states, lb_table, norm_w, masks, comms=()):
    bsz, _, seq, _ = proj.shape
    tok = HG_BWD_BLOCKS * HG_TOK
    nstep = seq // tok
    rec0 = ATT_WIDTH // LANE
    width = HG_HPS * LANE
    slabs = (HG_Q0, HG_F0, HG_I0, HG_G0)
    n_steps = (HG_HEADS // HG_HPS) * bsz * nstep
    assert n_steps >= 2

    def body(tbl_ref, nw_ref, dr_ref, p_ref, o_ref, st_ref, lower_ref, upper_ref,
             dproj_ref, dlb_ref, dnw_ref, ds_scr, grad_buf, grad_sem):
        h, b, t = pl.program_id(0), pl.program_id(1), pl.program_id(2)
        step = (h * bsz + b) * nstep + t
        slot = step % 2
        dq_k, df_k, di_k, dg_k = range(4)

        def grad_copies(of_step):
            hh, bb, tt = of_step // (bsz * nstep), (of_step // nstep) % bsz, of_step % nstep
            rows = pl.ds(pl.multiple_of((nstep - 1 - tt) * tok, tok), tok)
            return [pltpu.make_async_copy(
                grad_buf.at[of_step % 2, k],
                dproj_ref.at[bb, rows, pl.ds(pl.multiple_of(slabs[k] * LANE + hh * width, width), width)],
                grad_sem.at[of_step % 2, k]) for k in range(4)]

        @pl.when(step >= 2)
        def _():
            for cp in grad_copies(step - 2):
                cp.wait()

        @pl.when(t == 0)
        def _():
            ds_scr[...] = jnp.zeros_like(ds_scr)

        lower, upper = lower_ref[...], upper_ref[...]
        dlb_parts, dnw_parts = [None] * HG_HPS, [None] * HG_HPS

        def head(hp, sub):
            ls = slice(hp * LANE, (hp + 1) * LANE)
            rows = slice(sub * HG_TOK, (sub + 1) * HG_TOK)
            hq, v, hg = p_ref[HG_Q, rows, ls], p_ref[HG_I, rows, ls], p_ref[HG_G, rows, ls]
            nw = nw_ref[...]
            c = _hgrn_common(tbl_ref[:, ls], p_ref[HG_F, rows, ls], hq)
            qd, kd, ku = c["qd"], c["kd"], c["ku"]
            yield
            y, on, rstd = _rms_fwd(o_ref[rows, ls], nw)
            sg = _sigmoid(hg)
            dr = dr_ref[rows, ls]
            grad_buf[slot, dg_k, rows, ls] = (dr * y * (sg * (1.0 + hg * (1.0 - sg)))).astype(BF16)
            do, dnw_rows = _rms_bwd(dr * (hg * sg), on, rstd, nw)
            yield
            at = _dot(kd, qd, NT_DIMS) * upper
            da = _dot(do, v, NT_DIMS) * lower
            dat = _dot(v, do, NT_DIMS) * upper
            yield
            dv = _dot(at, do)
            dqd = _dot(da, kd)
            dkd = _dot(dat, qd)
            yield
            do_c, qd_c, v_c, ku_c = [_chunk_slices(z.astype(BF16)) for z in (do, qd, v, ku)]
            outer = [_dot(do_c[j], qd_c[j], TN_DIMS) for j in range(HG_NCH)]
            yield
            ds = ds_scr[hp]
            ds_after = [None] * HG_NCH
            for j in reversed(range(HG_NCH)):
                ds_after[j] = ds
                ds = outer[j] + ds * c["e_last"][j]
            ds_scr[hp] = ds
            yield
            updates = [_dot(v_c[j], ku_c[j], TN_DIMS) for j in range(HG_NCH)]
            yield
            states = [st_ref[hp, sub]]
            for j in range(HG_NCH - 1):
                states.append(states[j] * c["e_last"][j] + updates[j])
            dv = dv + jnp.concatenate([_dot(ku_c[j], ds_after[j], NT_DIMS) for j in range(HG_NCH)], axis=0)
            dqd = dqd + jnp.concatenate([_dot(do_c[j], states[j]) for j in range(HG_NCH)], axis=0)
            dku = jnp.concatenate([_dot(v_c[j], ds_after[j]) for j in range(HG_NCH)], axis=0)
            yield
            dku_ku = dku * ku
            dbl = [_colsum(states[j] * ds_after[j]) * c["e_last"][j] + _colsum(dku_ku[j * HG_CHUNK:(j + 1) * HG_CHUNK])
                   for j in range(HG_NCH)]
            dk = dkd * c["e_nb"] + dku * c["e_rem"]
            db = dqd * qd - dkd * kd - dku_ku + jnp.where(_row_in_chunk() == HG_CHUNK - 1, _chunk_rows(dbl), 0.0)
            dfv = _chunk_cumsum(db, reverse=True) / c["f"] - dk
            sig, sq = c["sig"], c["sq"]
            grad_buf[slot, df_k, rows, ls] = (dfv * (1.0 - c["lb"]) * sig * (1.0 - sig)).astype(BF16)
            grad_buf[slot, dq_k, rows, ls] = (dqd * c["e_b"] * (sq * (1.0 + hq * (1.0 - sq)))).astype(BF16)
            grad_buf[slot, di_k, rows, ls] = dv.astype(BF16)
            d_lb, d_nw = _colsum(dfv * (1.0 - sig)), _colsum(dnw_rows)
            dlb_parts[hp] = d_lb if dlb_parts[hp] is None else dlb_parts[hp] + d_lb
            dnw_parts[hp] = d_nw if dnw_parts[hp] is None else dnw_parts[hp] + d_nw

        for sub in reversed(range(HG_BWD_BLOCKS)):
            _in_step(head(hp, sub) for hp in range(HG_HPS))
        _acc_out(dlb_ref, jnp.logical_and(b == 0, t == 0), jnp.concatenate(dlb_parts, axis=1))
        _acc_out(dnw_ref, jnp.logical_and(h == 0, jnp.logical_and(b == 0, t == 0)), sum(dnw_parts[1:], dnw_parts[0]))
        for cp in grad_copies(step):
            cp.start()

        @pl.when(step == n_steps - 1)
        def _():
            for cp in grad_copies(step - 1) + grad_copies(step):
                cp.wait()

    rev = lambda t: nstep - 1 - t
    slab = lambda first: pl.BlockSpec((None, tok, width), lambda h, b, t: (b, rev(t), first // HG_HPS + h))
    head = pl.BlockSpec((None, tok, width), lambda h, b, t: (b, rev(t), h))
    return _call(
        body, (lb_table, norm_w, dcat, proj, o_raw, states, *masks), name="hgrn_bwd",
        grid=(HG_HEADS // HG_HPS, bsz, nstep),
        in_specs=[pl.BlockSpec((2, width), lambda h, b, t: (0, h)), pl.BlockSpec((1, LANE), lambda h, b, t: (0, 0)),
                  slab(rec0), pl.BlockSpec((None, HG_SLABS, tok, width), lambda h, b, t: (b, 0, rev(t), h)), head,
                  pl.BlockSpec((None, HG_HPS, HG_BWD_BLOCKS, LANE, LANE), lambda h, b, t: (b, h, rev(t), 0, 0)),
                  pl.BlockSpec((HG_TOK, HG_TOK), lambda h, b, t: (0, 0)), pl.BlockSpec((HG_TOK, HG_TOK), lambda h, b, t: (0, 0))],
        out_specs=[pl.BlockSpec(memory_space=pl.ANY), pl.BlockSpec((1, width), lambda h, b, t: (0, h)),
                   pl.BlockSpec((1, LANE), lambda h, b, t: (0, 0))],
        out_shape=[_sds((bsz, seq, IN_COLS), BF16), _sds((1, HG_WIDTH), F32), _sds((1, LANE), F32)],
        scratch_shapes=[pltpu.VMEM((HG_HPS, LANE, LANE), F32), pltpu.VMEM((2, 4, tok, width), BF16),
                        pltpu.SemaphoreType.DMA((2, 4))],
        sem=("arbitrary", "arbitrary", "arbitrary"), comms=comms)


def _attn_bwd(dcat, raw, w_norm, qh, kh, vh, lse, sinks, tables, biases, dproj, comms=()):
    bsz, nblk = qh.shape[0], qh.shape[1]
    seq = nblk * WINDOW
    nstep = nblk // ATT_BPS
    half = ROPE_DIM // 2

    def body(sink_ref, da_ref, raw_ref, w_ref, q_ref, kc_ref, kp_ref, vc_ref, vp_ref, l_ref, c_ref, u_ref, d_ref,
             full_ref, first_ref, dproj_ref, o_ref, dw_ref, dsink_ref, carry_k, carry_v):
        b, i = pl.program_id(0), pl.program_id(1)
        first = jnp.logical_and(b == 0, i == 0)

        @pl.when(i == 0)
        def _():
            carry_k[...] = jnp.zeros_like(carry_k)
            carry_v[...] = jnp.zeros_like(carry_v)

        w = w_ref[...]
        _, on, rstd = _rms_fwd(raw_ref[...], w)
        do_step, dw_rows = _rms_bwd(da_ref[...], on, rstd, w)
        _acc_out(dw_ref, first, _colsum(dw_rows))
        lane8 = lax.broadcasted_iota(jnp.int32, (1, ATT_Q_HEADS), 1)
        dsink = jnp.zeros((1, ATT_Q_HEADS), F32)
        head_cols = jnp.where(lax.broadcasted_iota(jnp.int32, (2 * ATT_Q_HEADS, ATT_WIDTH), 1) // ATT_HEAD_DIM
                              == lax.broadcasted_iota(jnp.int32, (2 * ATT_Q_HEADS, ATT_WIDTH), 0), 1.0, 0.0)
        from_next_k, from_next_v = carry_k[...], carry_v[...]
        for blk in reversed(range(ATT_BPS)):
            tok = slice(blk * WINDOW, (blk + 1) * WINDOW)
            bias = _band_bias(full_ref, first_ref, True if blk else i < nstep - 1)
            do_all = do_step[tok]
            c, u, d = c_ref[tok, :], u_ref[tok, :], d_ref[tok, :]
            lse_t = l_ref[tok, :].T
            prod = do_all * raw_ref[tok, :]
            prod_hi = prod.astype(BF16)
            prod_lo = prod - prod_hi.astype(F32)
            dsum_t = _dot(head_cols, prod_hi, NT_DIMS) + _dot(head_cols, prod_lo, NT_DIMS)

            def unrope(g):
                return (g * c + pltpu.roll(g * u, LANE - half, 1) + pltpu.roll(g * d, half, 1)).astype(BF16)

            groups = range(ATT_KV_HEADS)
            group_row = lambda z, g: jnp.concatenate(
                [z[g * ATT_GROUP + hh:g * ATT_GROUP + hh + 1, :] for hh in range(ATT_GROUP)], axis=1)
            q = [q_ref[blk, g] for g in groups]
            keys, vals = [_band(kp_ref, kc_ref, g, blk) for g in groups], [_band(vp_ref, vc_ref, g, blk) for g in groups]
            do_g = [jnp.concatenate([do_all[:, (g * ATT_GROUP + hh) * ATT_HEAD_DIM:(g * ATT_GROUP + hh + 1) * ATT_HEAD_DIM]
                                     for hh in range(ATT_GROUP)], axis=0) for g in groups]
            dsum, lse_g = [group_row(dsum_t, g) for g in groups], [group_row(lse_t, g) for g in groups]
            s_t = [_dot(keys[g], q[g], NT_DIMS) for g in groups]
            dp_t = [_dot(vals[g], do_g[g], NT_DIMS) for g in groups]
            p_t = [jnp.exp(s_t[g] * ATT_SCALE + bias - lse_g[g]) for g in groups]
            ds_t = [p_t[g] * (dp_t[g] - dsum[g]) * ATT_SCALE for g in groups]
            dq_g = [_dot(ds_t[g], keys[g], TN_DIMS) for g in groups]
            dk_g = [_dot(ds_t[g], q[g]) for g in groups]
            dv_g = [_dot(p_t[g], do_g[g]) for g in groups]
            for g in groups:
                sink_part = jnp.exp(_sink_row(sink_ref, g) - lse_g[g]) * dsum[g]
                for hh in range(ATT_GROUP):
                    head_sum = jnp.sum(sink_part[:, hh * WINDOW:(hh + 1) * WINDOW], axis=1, keepdims=True)
                    dsink = dsink - jnp.where(lane8 == g * ATT_GROUP + hh, head_sum, 0.0)
            dq_parts = [dq_g[g][hh * WINDOW:(hh + 1) * WINDOW] for g in groups for hh in range(ATT_GROUP)]
            dk_before, dk_own = [z[:WINDOW] for z in dk_g], [z[WINDOW:] for z in dk_g]
            dv_before, dv_own = [z[:WINDOW] for z in dv_g], [z[WINDOW:] for z in dv_g]
            per_slab = LANE // ATT_HEAD_DIM
            for s in range(ATT_WIDTH // LANE):
                slab = jnp.concatenate(dq_parts[s * per_slab:(s + 1) * per_slab], axis=1)
                o_ref[tok, s * LANE:(s + 1) * LANE] = unrope(slab)
            o_ref[tok, ATT_WIDTH:ATT_WIDTH + LANE] = unrope(jnp.concatenate(dk_own, axis=1) + from_next_k)
            o_ref[tok, ATT_WIDTH + LANE:ATT_COLS] = (jnp.concatenate(dv_own, axis=1) + from_next_v).astype(BF16)
            from_next_k, from_next_v = jnp.concatenate(dk_before, axis=1), jnp.concatenate(dv_before, axis=1)
        carry_k[...] = from_next_k
        carry_v[...] = from_next_v
        _acc_out(dsink_ref, first, dsink)

    rows = ATT_BPS * WINDOW
    rev = lambda i: nstep - 1 - i
    cur = lambda width: pl.BlockSpec((None, rows, width), lambda b, i: (b, rev(i), 0))
    q_spec = pl.BlockSpec((None, ATT_BPS, ATT_KV_HEADS, GROUP_ROWS, ATT_HEAD_DIM), lambda b, i: (b, rev(i), 0, 0, 0))
    kv_cur = pl.BlockSpec((None, ATT_KV_HEADS, rows, ATT_HEAD_DIM), lambda b, i: (b, 0, rev(i), 0))
    kv_prev = pl.BlockSpec((None, ATT_KV_HEADS, WINDOW, ATT_HEAD_DIM), lambda b, i: (b, 0, jnp.maximum(ATT_BPS * rev(i) - 1, 0), 0))
    tab = pl.BlockSpec((rows, LANE), lambda b, i: (rev(i), 0))
    return _call(
        body, (sinks, dcat, raw, w_norm, qh, kh, kh, vh, vh, lse, *tables, *biases, dproj), name="attn_bwd", grid=(bsz, nstep),
        in_specs=[pl.BlockSpec(memory_space=pltpu.SMEM), cur(ATT_WIDTH), cur(ATT_WIDTH), _vec_spec(ATT_WIDTH), q_spec,
                  kv_cur, kv_prev, kv_cur, kv_prev, cur(LANE), tab, tab, tab, _bias_spec(True), _bias_spec(True),
                  pl.BlockSpec(memory_space=pl.ANY)],
        out_specs=[cur(ATT_COLS), _vec_spec(ATT_WIDTH), _vec_spec(ATT_Q_HEADS)],
        out_shape=[_sds(dproj.shape, BF16), _sds((1, ATT_WIDTH), F32), _sds((1, ATT_Q_HEADS), F32)],
        scratch_shapes=[pltpu.VMEM((WINDOW, LANE), F32), pltpu.VMEM((WINDOW, LANE), F32)],
        sem=("arbitrary", "arbitrary"), comms=comms, aliases={15: 0})


def _other_chips(x, y):
    return [(1 - x, y), (x, 1 - y), (1 - x, 1 - y)]


def _sem_pair(n):
    return [pltpu.SemaphoreType.DMA((n,)), pltpu.SemaphoreType.DMA((n,))]


def _plan_pair_forward(bufs):
    n = len(bufs)

    def copies(outs, sems):
        x, y, c = _mesh_pos()
        sends, lands = [], []
        for a in range(n):
            for j, chip in enumerate(_other_chips(x, y)):
                k = 3 * a + j
                slot = outs[a].at[4 * chip[0] + 2 * chip[1] + c]
                sends.append(pltpu.make_async_remote_copy(
                    src_ref=slot, dst_ref=slot, send_sem=sems[0].at[k], recv_sem=sems[1].at[k],
                    device_id=(x, y, 1 - c), device_id_type=MESH))
                theirs = outs[a].at[4 * chip[0] + 2 * chip[1] + 1 - c]
                lands.append(pltpu.make_async_remote_copy(
                    src_ref=theirs, dst_ref=theirs, send_sem=sems[0].at[k], recv_sem=sems[1].at[k],
                    device_id=(x, y, 1 - c), device_id_type=MESH))
        return sends, lands

    def start(ins, outs, sems):
        for cp in copies(outs, sems)[0]:
            cp.start()

    def finish(ins, outs, sems):
        sends, lands = copies(outs, sems)
        for cp in lands:
            cp.wait_recv()
        for cp in sends:
            cp.wait_send()

    return _Comm(list(bufs), [_sds(b.shape, b.dtype) for b in bufs], _sem_pair(3 * n), start, finish,
                 aliases=[(a, a) for a in range(n)])


def _plan_pair(arrays, other_half):
    n = len(arrays)
    per = N_CHIPS if other_half == "chip_major" else 1

    def copies(ins, outs, sems):
        x, y, c = _mesh_pos()
        out = []
        for a in range(n):
            for k in range(per):
                if other_half == "chip_major":
                    src, dst = ins[a].at[k, 1 - c], outs[a].at[k]
                else:
                    src, dst = (ins[a].at[1 - c] if other_half else ins[a]), outs[a]
                out.append(pltpu.make_async_remote_copy(
                    src_ref=src, dst_ref=dst, send_sem=sems[0].at[per * a + k], recv_sem=sems[1].at[per * a + k],
                    device_id=(x, y, 1 - c), device_id_type=MESH))
        return out

    def start(ins, outs, sems):
        for cp in copies(ins, outs, sems):
            cp.start()

    def finish(ins, outs, sems):
        for cp in copies(ins, outs, sems):
            cp.wait()

    if other_half == "chip_major":
        shapes = [_sds((a.shape[0],) + a.shape[2:], a.dtype) for a in arrays]
    else:
        shapes = [_sds(a.shape[1:] if other_half else a.shape, a.dtype) for a in arrays]
    return _Comm(list(arrays), shapes, _sem_pair(per * n), start, finish)


def _plan_chip_exchange(arrays):
    n = len(arrays)

    def copies(ins, outs, sems):
        x, y, c = _mesh_pos()
        sends, lands = [], []
        for a in range(n):
            for j, chip in enumerate(_other_chips(x, y)):
                k = 3 * a + j
                sends.append(pltpu.make_async_remote_copy(
                    src_ref=ins[a].at[2 * chip[0] + chip[1]], dst_ref=outs[a].at[2 * x + y], send_sem=sems[0].at[k],
                    recv_sem=sems[1].at[k], device_id=(*chip, c), device_id_type=MESH))
                slot = outs[a].at[2 * chip[0] + chip[1]]
                lands.append(pltpu.make_async_remote_copy(
                    src_ref=slot, dst_ref=slot, send_sem=sems[0].at[k], recv_sem=sems[1].at[k],
                    device_id=(*chip, c), device_id_type=MESH))
        return sends, lands

    def start(ins, outs, sems):
        for cp in copies(ins, outs, sems)[0]:
            cp.start()

    def finish(ins, outs, sems):
        sends, lands = copies(ins, outs, sems)
        for cp in lands:
            cp.wait_recv()
        for cp in sends:
            cp.wait_send()

    return _Comm(list(arrays), [_sds(a.shape, a.dtype) for a in arrays], _sem_pair(3 * n), start, finish)


SEM_SPEC = pl.BlockSpec(memory_space=pltpu.SEMAPHORE)
N_OTHER = N_CHIPS - 1


def _exchange_copies(s_ref, land_ref, sems):
    x, y, c = _mesh_pos()
    return [pltpu.make_async_remote_copy(
        src_ref=s_ref.at[2 * chip[0] + chip[1]], dst_ref=land_ref.at[2 * x + y], send_sem=sems[j], recv_sem=sems[N_OTHER + j],
        device_id=(*chip, c), device_id_type=MESH) for j, chip in enumerate(_other_chips(x, y))]


def _exchange_start(s, name):
    def body(s_ref, land_ref, *outs):
        sems, token = outs[:2 * N_OTHER], outs[-1]
        for cp in _exchange_copies(s_ref, land_ref, sems):
            cp.start()
        token[...] = jnp.zeros_like(token)

    hbm = pltpu.HBM(s.shape, s.dtype)
    res = pl.pallas_call(
        body, name=name,
        out_shape=(pltpu.SemaphoreType.DMA(()),) * (2 * N_OTHER) + (hbm, hbm, _sds((SUBLANES, LANE), F32)),
        in_specs=(HBM_SPEC, HBM_SPEC),
        out_specs=(SEM_SPEC,) * (2 * N_OTHER) + (HBM_SPEC, HBM_SPEC, pl.BlockSpec(memory_space=pltpu.VMEM)),
        input_output_aliases={0: 2 * N_OTHER, 1: 2 * N_OTHER + 1},
        compiler_params=pltpu.CompilerParams(has_side_effects=pltpu.SideEffectType.DATAFLOW_SIDE_EFFECTING),
    )(pltpu.with_memory_space_constraint(s, pltpu.HBM), pltpu.with_memory_space_constraint(lax.empty(s.shape, s.dtype), pltpu.HBM))
    return res[:2 * N_OTHER], res[2 * N_OTHER], res[2 * N_OTHER + 1], res[-1]


def _exchange_wait(sems, s_thru, land_thru, afters, name):
    def body(s_ref, land_ref, *rest):
        for cp in _exchange_copies(s_ref, land_ref, rest[:2 * N_OTHER]):
            cp.wait_send()
            cp.wait_recv()

    hbm = pltpu.HBM(s_thru.shape, s_thru.dtype)
    return pl.pallas_call(
        body, name=name, out_shape=(hbm, hbm),
        in_specs=(HBM_SPEC, HBM_SPEC) + (SEM_SPEC,) * (2 * N_OTHER) + (pl.BlockSpec(memory_space=pl.ANY),) * len(afters),
        out_specs=(HBM_SPEC, HBM_SPEC), input_output_aliases={0: 0, 1: 1},
        compiler_params=pltpu.CompilerParams(has_side_effects=pltpu.SideEffectType.DATAFLOW_SIDE_EFFECTING),
    )(s_thru, land_thru, *sems, *afters)


def _gather_copies(block_ref, buf_ref, sems):
    x, y, c = _mesh_pos()
    return [pltpu.make_async_remote_copy(
        src_ref=block_ref, dst_ref=buf_ref.at[4 * x + 2 * y + c], send_sem=sems[j], recv_sem=sems[N_OTHER + j],
        device_id=(*chip, c), device_id_type=MESH) for j, chip in enumerate(_other_chips(x, y))]


def _gather_start(blocks, bufs, afters, name):
    n = len(blocks)
    per = 2 * N_OTHER

    def body(*refs):
        ins, outs = refs[:2 * n], refs[2 * n + len(afters):]
        for a in range(n):
            for cp in _gather_copies(ins[a], ins[n + a], outs[a * per:(a + 1) * per]):
                cp.start()
        outs[-1][...] = jnp.zeros_like(outs[-1])

    hbm = [pltpu.HBM(z.shape, z.dtype) for z in list(blocks) + list(bufs)]
    res = pl.pallas_call(
        body, name=name,
        out_shape=(pltpu.SemaphoreType.DMA(()),) * (n * per) + tuple(hbm) + (_sds((SUBLANES, LANE), F32),),
        in_specs=(HBM_SPEC,) * (2 * n) + (pl.BlockSpec(memory_space=pl.ANY),) * len(afters),
        out_specs=(SEM_SPEC,) * (n * per) + (HBM_SPEC,) * (2 * n) + (pl.BlockSpec(memory_space=pltpu.VMEM),),
        input_output_aliases={k: n * per + k for k in range(2 * n)},
        compiler_params=pltpu.CompilerParams(has_side_effects=pltpu.SideEffectType.DATAFLOW_SIDE_EFFECTING),
    )(*[pltpu.with_memory_space_constraint(z, pltpu.HBM) for z in list(blocks) + list(bufs)], *afters)
    parts = [(res[a * per:(a + 1) * per], res[n * per + a], res[n * per + n + a]) for a in range(n)]
    return parts, res[-1]


def _gather_wait(part, afters, name):
    sems, block, buf = part

    def body(block_ref, buf_ref, *rest):
        for cp in _gather_copies(block_ref, buf_ref, rest[:2 * N_OTHER]):
            cp.wait_send()
            cp.wait_recv()

    return pl.pallas_call(
        body, name=name, out_shape=(pltpu.HBM(block.shape, block.dtype), pltpu.HBM(buf.shape, buf.dtype)),
        in_specs=(HBM_SPEC, HBM_SPEC) + (SEM_SPEC,) * (2 * N_OTHER) + (pl.BlockSpec(memory_space=pl.ANY),) * len(afters),
        out_specs=(HBM_SPEC, HBM_SPEC), input_output_aliases={0: 0, 1: 1},
        compiler_params=pltpu.CompilerParams(has_side_effects=pltpu.SideEffectType.DATAFLOW_SIDE_EFFECTING),
    )(block, buf, *sems, *afters)[1]


def _comm_only(comms, name):
    return _call(lambda: None, (), name=name, grid=(), in_specs=[], out_specs=[], out_shape=[], sem=(), comms=comms)[1]


def _allgather8(arrays, name):
    return _comm_only([_plan_allgather8(arrays)], name)[0]


def _plan_allgather8(arrays):
    n = len(arrays)

    def parts(ins, outs, sems):
        send_sems, recv_sems, local_sems = sems
        x, y, c = _mesh_pos()
        me, sibling = (x, y, c), (x, y, 1 - c)
        chips = _other_chips(x, y)

        def copy(a, k, block, to, src=None):
            dst = outs[a].at[4 * block[0] + 2 * block[1] + block[2]]
            return pltpu.make_async_remote_copy(
                src_ref=dst if src is None else src, dst_ref=dst, send_sem=send_sems.at[7 * a + k],
                recv_sem=recv_sems.at[7 * a + k], device_id=to, device_id_type=MESH)

        mine = [pltpu.make_async_copy(ins[a], outs[a].at[4 * x + 2 * y + c], local_sems.at[a]) for a in range(n)]
        first = []
        for a in range(n):
            first.append(copy(a, 0, me, sibling, src=ins[a]))
            first += [copy(a, 1 + j, me, (*chip, c), src=ins[a]) for j, chip in enumerate(chips)]
        return copy, mine, first, me, sibling, chips, c

    def start(ins, outs, sems):
        _, mine, first, *_ = parts(ins, outs, sems)
        for cp in mine + first:
            cp.start()

    def finish(ins, outs, sems):
        copy, mine, first, me, sibling, chips, c = parts(ins, outs, sems)
        passed = []
        for j, chip in enumerate(chips):
            for a in range(n):
                copy(a, 1 + j, (*chip, c), me).wait_recv()
                fwd = copy(a, 4 + j, (*chip, c), sibling)
                fwd.start()
                passed.append(fwd)
        for a in range(n):
            copy(a, 0, sibling, me).wait_recv()
            for j, chip in enumerate(chips):
                copy(a, 4 + j, (*chip, 1 - c), me).wait_recv()
        for cp in first + passed:
            cp.wait_send()
        for cp in mine:
            cp.wait()

    sems = [pltpu.SemaphoreType.DMA((7 * n,)), pltpu.SemaphoreType.DMA((7 * n,)), pltpu.SemaphoreType.DMA((n,))]
    return _Comm(list(arrays), [_sds((N_DEV,) + a.shape, a.dtype) for a in arrays], sems, start, finish)


def _pair_sum(g, q, core, name, chip_major=False):
    rows, cols = g.shape[2:]
    tr = _row_tile(rows)

    def body(core_ref, g_ref, q_ref, o_ref):
        o_ref[...] = (g_ref[...] + q_ref[...]).astype(BF16)

    blk = pl.BlockSpec((None, tr, cols), lambda k, i, core_ref: (k, i, 0))
    if chip_major:
        own = pl.BlockSpec((None, None, tr, cols), lambda k, i, core_ref: (k, core_ref[0], i, 0))
    else:
        own = pl.BlockSpec((None, None, tr, cols), lambda k, i, core_ref: (core_ref[0], k, i, 0))
    return pl.pallas_call(
        body, name=name,
        grid_spec=pltpu.PrefetchScalarGridSpec(num_scalar_prefetch=1, grid=(N_CHIPS, rows // tr), in_specs=[own, blk], out_specs=blk),
        out_shape=_sds((N_CHIPS, rows, cols), BF16), compiler_params=_params("parallel", "parallel"),
    )(core, g, q)


def _sum_chips(own, landed, chip, name):
    _, rows, cols = own.shape
    tr = _row_tile(rows)

    def body(chip_ref, own_ref, a_ref, b_ref, c_ref, o_ref):
        acc = own_ref[...].astype(F32) + a_ref[...].astype(F32)
        o_ref[...] = (acc + b_ref[...].astype(F32)) + c_ref[...].astype(F32)

    blk = lambda flip: pl.BlockSpec((None, tr, cols), lambda i, chip_ref: (jnp.bitwise_xor(chip_ref[0], flip), i, 0))
    return pl.pallas_call(
        body, name=name,
        grid_spec=pltpu.PrefetchScalarGridSpec(num_scalar_prefetch=1, grid=(rows // tr,), in_specs=[blk(0), blk(1), blk(2), blk(3)],
                                               out_specs=pl.BlockSpec((tr, cols), lambda i, chip_ref: (i, 0))),
        out_shape=_sds((rows, cols), F32), compiler_params=_params("parallel"),
    )(chip, own, landed, landed, landed)


SUBLANES = 8


def _tile_rows(n_elems):
    return -(-n_elems // (SUBLANES * LANE)) * SUBLANES


SMALL_ITEMS = (("b_ada", N_MOD * D_MODEL), ("pre_w_mix", D_MODEL), ("post_w_mix", D_MODEL), ("pre_w_mlp", D_MODEL),
               ("post_w_mlp", D_MODEL), ("attn_out_w", ATT_WIDTH), ("hg_norm_w", HG_HEAD_DIM), ("attn_sinks", ATT_Q_HEADS),
               ("lb_0", HG_WIDTH), ("lb_1", HG_WIDTH))
SMALL_AT = {}
for _name, _size in SMALL_ITEMS:
    SMALL_AT[_name] = (sum(r for _, r in SMALL_AT.values()), _tile_rows(_size))
SMALL_ROWS = sum(r for _, r in SMALL_AT.values())
MOD_ROWS = SMALL_AT["b_ada"][1]
PLAIN_ROWS = SMALL_AT["lb_0"][0] - MOD_ROWS
LB_ROWS = SMALL_AT["lb_0"][1]


def _rows(a, nrows=None):
    flat = a.reshape(-1)
    nrows = _tile_rows(flat.shape[0]) if nrows is None else nrows
    return jnp.pad(flat, (0, nrows * LANE - flat.shape[0])).reshape(nrows, LANE)


def _pack_small(vals):
    vals = dict(vals, lb_0=vals["lb_table"][0], lb_1=vals["lb_table"][1])
    return jnp.concatenate([_rows(vals[name], SMALL_AT[name][1]) for name, _ in SMALL_ITEMS], axis=0)


def _unpack_small(p):
    def item(name, shape):
        first = SMALL_AT[name][0]
        size = shape[0] * shape[1]
        return p[first:first + SMALL_AT[name][1]].reshape(-1)[:size].reshape(shape)

    out = {name: item(name, (1, size)) for name, size in SMALL_ITEMS if not name.startswith("lb_")}
    out["lb_table"] = jnp.concatenate([item("lb_0", (1, HG_WIDTH)), item("lb_1", (1, HG_WIDTH))], axis=0)
    return out


def _pack_partials(dmod, plain, d_lb, loss_row):
    return jnp.concatenate([_rows(dmod, dmod.shape[0] * MOD_ROWS)] + [_rows(g) for g in plain] + [_rows(d_lb), _rows(loss_row)], axis=0)


def _small_update(packs, w, m, v, n_seq):
    mod_end = n_seq * MOD_ROWS
    lb_at = mod_end + PLAIN_ROWS
    t0, t1 = SMALL_AT["lb_0"][0], SMALL_AT["lb_1"][0]

    def body(p_ref, w_ref, m_ref, v_ref, g_ref, dl_ref, nm_ref, nv_ref, loss_ref):
        tot = p_ref[0]
        for d in range(1, N_DEV):
            tot = tot + p_ref[d]
        wv = w_ref[...]
        p1 = _sigmoid(wv[t1:t1 + LB_ROWS] - wv[t0:t0 + LB_ROWS])
        s = tot[lb_at:lb_at + LB_ROWS] * p1 * (1.0 - p1)
        g_bias = tot[0:MOD_ROWS]
        for q in range(1, n_seq):
            g_bias = g_bias + tot[q * MOD_ROWS:(q + 1) * MOD_ROWS]
        g = jnp.concatenate([g_bias, tot[mod_end:lb_at], -s, s], axis=0)
        g_ref[...] = g
        dl_ref[...], nm_ref[...], nv_ref[...] = _adamw_math(g, wv, m_ref[...], v_ref[...])
        loss_ref[...] = tot[lb_at + LB_ROWS:lb_at + LB_ROWS + SUBLANES]

    shp = _sds((SMALL_ROWS, LANE), F32)
    return pl.pallas_call(body, name="small_update", out_shape=[shp] * 4 + [_sds((SUBLANES, LANE), F32)],
                          compiler_params=_params())(packs, w, m, v)


def kernel(x, c, w_ada, b_ada, pre_w_mix, w_in, attn_sinks, attn_out_w, lb_table, hg_norm_w, w_out, post_w_mix, pre_w_mlp, w_up, w_down, post_w_mlp, loss_target, m_w_ada, m_b_ada, m_pre_w_mix, m_w_in, m_attn_sinks, m_attn_out_w, m_lb_table, m_hg_norm_w, m_w_out, m_post_w_mix, m_pre_w_mlp, m_w_up, m_w_down, m_post_w_mlp, v_w_ada, v_b_ada, v_pre_w_mix, v_w_in, v_attn_sinks, v_attn_out_w, v_lb_table, v_hg_norm_w, v_w_out, v_post_w_mix, v_pre_w_mlp, v_w_up, v_w_down, v_post_w_mlp):
    xi, yi, ci = _mesh_pos()
    chip = 2 * xi + yi
    dev = 2 * chip + ci
    bsz, seq, _ = x.shape
    ntok = bsz * seq
    ada_cols = w_ada.shape[2]
    core = jnp.reshape(ci, (1,)).astype(jnp.int32)
    chip_idx = jnp.reshape(chip, (1,)).astype(jnp.int32)
    flat = lambda a: a.reshape(ntok, a.shape[-1])
    unflat = lambda a: a.reshape(bsz, seq, a.shape[-1])
    tables = _rope_tables(seq)
    biases, chunk_masks = _band_biases(), _block_masks()

    def row_half(w):
        rows = w.shape[1] // 2
        return lax.dynamic_slice_in_dim(w[0], ci * rows, rows, axis=0).astype(BF16)

    def gather_buffer(w):
        rows, cols = w.shape[1] // 2, w.shape[2]
        own = w[0].astype(BF16).reshape(2, rows, cols)
        return lax.dynamic_update_slice(lax.empty((N_DEV, rows, cols), BF16), own, (2 * chip, 0, 0))

    w_in_t, m_in_t, v_in_t = [jnp.transpose(a[0])[None] for a in (w_in, m_w_in, v_w_in)]
    c_g, in_g = _allgather8([c, row_half(w_in_t)], "gather_first")
    c_all = c_g.reshape(N_DEV * bsz, D_MODEL)
    w_in_full = in_g.reshape(IN_COLS, D_MODEL)

    b_cols = lax.dynamic_slice_in_dim(b_ada, chip * ada_cols, ada_cols, axis=1)
    mod_part = _ada_fwd(c_all, w_ada[0], b_cols)
    half_rows = mod_part.shape[0] // 2
    (mod_g,) = _allgather8([lax.dynamic_slice_in_dim(mod_part, ci * half_rows, half_rows, axis=0)], "gather_mod")
    mod_all = mod_g.reshape(N_CHIPS, 2, half_rows, ada_cols).transpose(1, 2, 0, 3).reshape(N_DEV * bsz, N_MOD * D_MODEL)
    mod = lax.dynamic_slice_in_dim(mod_all, dev * bsz, bsz, axis=0)
    sh1, sc1, g1, sh2, sc2, g2 = [mod[:, i * D_MODEL:(i + 1) * D_MODEL].reshape(bsz, 1, D_MODEL) for i in range(N_MOD)]

    weights = (w_out, w_up, w_down)
    (out_part, up_part, down_part), started = _gather_start(
        [row_half(w) for w in weights], [gather_buffer(w) for w in weights], [mod_g], "gather_weights_start")

    h1, proj, qh, kh, vh = _in_proj_fused(x, pre_w_mix, sc1 + started[0:1, 0:1], sh1, w_in_full, tables)
    out_g = _gather_wait(out_part, [proj], "gather_out_wait")
    (attn_raw, cat, lse), ((out_g,),) = _attn_fwd(qh, kh, vh, attn_sinks, attn_out_w, biases, comms=[_plan_pair_forward([out_g])])
    up_g = _gather_wait(up_part, [attn_raw], "gather_up_wait")
    (o_raw, cat, states), ((up_g,),) = _hgrn_fwd(proj, lb_table, hg_norm_w, cat, chunk_masks, comms=[_plan_pair_forward([up_g])])
    down_g = _gather_wait(down_part, [o_raw], "gather_down_wait")
    w_out_full = out_g.reshape(D_MODEL, D_MODEL)
    w_up4 = up_g.reshape(N_CHIPS, D_MODEL, D_MODEL)
    mix, x1, h2 = _out_proj_fused(cat, w_out_full, x, post_w_mix, g1, pre_w_mlp, sc2, sh2)
    big_tm = min(ntok, 2048)
    up_spec = pl.BlockSpec((None, D_MODEL, D_MODEL), lambda i, j: (j, 0, 0))
    r, ((down_g,),) = _mm(flat(h2), w_up4, name="up_proj", out_dtype=BF16, tm=big_tm, tn=D_MODEL, n_out=D_FF, b_spec=up_spec,
                          epi=lambda acc: jnp.maximum(acc, 0.0), comms=[_plan_pair_forward([down_g])])
    w_down_full = down_g.reshape(D_FF, D_MODEL)
    square = lambda t: t * t
    loss_row, dy, dd, dg2, d_post_mlp = _down_proj_fused(unflat(r), w_down_full, x1, post_w_mlp, g2, loss_target)

    dpre = _mm(flat(dd), w_down_full, name="down_bwd", out_dtype=BF16, trans_b=True, tm=big_tm, tn=D_MODEL, extra=(r,),
               epi=lambda acc, rt: acc * (2.0 * rt.astype(F32)))
    half_rows = D_MODEL // 2
    g_down = _mm_tn(r, flat(dd), name="down_wgrad", tk=half_rows, tn=D_MODEL, a_fn=square,
                    out_shape=_sds((2, N_CHIPS, half_rows, D_MODEL), F32),
                    out_spec=pl.BlockSpec((None, None, half_rows, D_MODEL), lambda i, j: (i % 2, i // 2, 0, 0)))
    (dx1, dmix, dsc2, dsh2, dg1, d_pre_mlp, d_post_mix), ((q_down,),) = _up_bwd_fused(
        unflat(dpre), w_up4, dy, x1, mix, pre_w_mlp, sc2, post_w_mix, g1, comms=[_plan_pair([g_down], True)])
    g_up = _mm_tn(flat(h2), dpre, name="up_wgrad", tk=D_MODEL, tn=half_rows,
                  out_shape=_sds((2, N_CHIPS, half_rows, D_MODEL), F32),
                  out_spec=pl.BlockSpec((2, None, half_rows, half_rows), lambda i, j: (0, j // 2, 0, j % 2)))
    s_down = _pair_sum(g_down, q_down, core, "pair_sum_down")

    dcat, ((q_up,),) = _mm(flat(dmix), w_out_full, name="out_bwd", out_dtype=F32, trans_b=True, comms=[_plan_pair([g_up], True)])
    dcat = unflat(dcat)
    s_up = _pair_sum(g_up, q_up, core, "pair_sum_up")
    out_rows = D_MODEL // N_CHIPS
    g_out = _mm_tn(flat(cat), flat(dmix), name="out_wgrad", tk=2 * out_rows, tn=half_rows,
                   out_shape=_sds((2, N_CHIPS, out_rows, half_rows), F32),
                   out_spec=pl.BlockSpec((None, 2, out_rows, half_rows), lambda i, j: (j, i, 0, 0)))
    (dproj_rec, d_lb, d_hg_norm), ((x_down,), (q_out,)) = _hgrn_bwd(
        dcat, proj, o_raw, states, lb_table, hg_norm_w, chunk_masks, comms=[_plan_chip_exchange([s_down]), _plan_pair([g_out], True)])
    half_down = _sum_chips(s_down, x_down, chip_idx, "sum_chips_down")
    s_out = _pair_sum(g_out, q_out, core, "pair_sum_out")
    (dproj, d_attn_out, d_sinks), ((their_down,), (x_up,)) = _attn_bwd(
        dcat, attn_raw, attn_out_w, qh, kh, vh, lse, attn_sinks, tables, [bias.T for bias in biases], dproj_rec,
        comms=[_plan_pair([half_down], False), _plan_chip_exchange([s_up])])
    half_up = _sum_chips(s_up, x_up, chip_idx, "sum_chips_up")
    dproj = flat(dproj)
    in_rows = IN_COLS // N_CHIPS // 2
    g_in, ((x_out,),) = _mm_tn(dproj, flat(h1), name="in_wgrad", tk=2 * LANE, tn=D_MODEL, comms=[_plan_chip_exchange([s_out])])
    g_in = g_in.reshape(N_CHIPS, 2, in_rows, D_MODEL)
    half_out = _sum_chips(s_out, x_out, chip_idx, "sum_chips_out")
    dh1, ((q_in,), (their_up, their_out)) = _mm(
        dproj, w_in_full, name="in_bwd", out_dtype=F32,
        comms=[_plan_pair([g_in], "chip_major"), _plan_pair([half_up, half_out], False)])
    s_in = _pair_sum(g_in, q_in, core, "pair_sum_in", chip_major=True)
    in_sems, s_in, in_landing, started = _exchange_start(s_in, "exchange_in_start")
    grad_x, dsc1, dsh1, d_pre_mix = _norm1_bwd(unflat(dh1), dx1, x, pre_w_mix + started[0:1, 0:1], sc1)

    dmod = jnp.concatenate([dsh1, dsc1, dg1, dsh2, dsc2, dg2], axis=-1).reshape(bsz, N_MOD * D_MODEL)
    pack = _pack_partials(dmod, [d_pre_mix, d_post_mix, d_pre_mlp, d_post_mlp, d_attn_out, d_hg_norm, d_sinks], d_lb, loss_row)
    ((packs,),) = _comm_only([_plan_allgather8([pack])], "gather_small")
    w_small = dict(b_ada=b_ada, pre_w_mix=pre_w_mix, post_w_mix=post_w_mix, pre_w_mlp=pre_w_mlp, post_w_mlp=post_w_mlp,
                   attn_out_w=attn_out_w, hg_norm_w=hg_norm_w, attn_sinks=attn_sinks, lb_table=lb_table)
    m_small = dict(b_ada=m_b_ada, pre_w_mix=m_pre_w_mix, post_w_mix=m_post_w_mix, pre_w_mlp=m_pre_w_mlp, post_w_mlp=m_post_w_mlp,
                   attn_out_w=m_attn_out_w, hg_norm_w=m_hg_norm_w, attn_sinks=m_attn_sinks, lb_table=m_lb_table)
    v_small = dict(b_ada=v_b_ada, pre_w_mix=v_pre_w_mix, post_w_mix=v_post_w_mix, pre_w_mlp=v_pre_w_mlp, post_w_mlp=v_post_w_mlp,
                   attn_out_w=v_attn_out_w, hg_norm_w=v_hg_norm_w, attn_sinks=v_attn_sinks, lb_table=v_lb_table)
    *small_packed, loss_rows = _small_update(packs, _pack_small(w_small), _pack_small(m_small), _pack_small(v_small), bsz)
    small_out = [_unpack_small(p) for p in small_packed]
    loss = loss_rows[0, 0]

    dmod_all = packs[:, :bsz * MOD_ROWS, :].reshape(N_DEV * bsz, N_MOD * D_MODEL)
    dmod_cols = lax.dynamic_slice_in_dim(dmod_all, chip * ada_cols, ada_cols, axis=1)
    ada_out = _ada_bwd_adamw(c_all, dmod_cols, w_ada[0], m_w_ada[0], v_w_ada[0])

    s_in, x_in = _exchange_wait(in_sems, s_in, in_landing, [grad_x, ada_out[0]], "exchange_in_wait")
    half_in = _sum_chips(s_in, x_in, chip_idx, "sum_chips_in")
    ((their_in,),) = _comm_only([_plan_pair([half_in], False)], "pair_swap_in")
    big = dict(
        w_in=tuple(jnp.transpose(a) for a in _adamw_halves(half_in, their_in, core, w_in_t[0], m_in_t[0], v_in_t[0], axis=0,
                                                           name="adamw_in")),
        w_up=tuple(_adamw_halves(half_up, their_up, core, w_up[0], m_w_up[0], v_w_up[0], axis=0, name="adamw_up")),
        w_out=tuple(_adamw_halves(half_out, their_out, core, w_out[0], m_w_out[0], v_w_out[0], axis=1, name="adamw_out")),
        w_down=tuple(_adamw_halves(half_down, their_down, core, w_down[0], m_w_down[0], v_w_down[0], axis=0, name="adamw_down")),
        w_ada=tuple(ada_out),
    )
    order = ("w_ada", "b_ada", "pre_w_mix", "w_in", "attn_sinks", "attn_out_w", "lb_table", "hg_norm_w", "w_out", "post_w_mix",
             "pre_w_mlp", "w_up", "w_down", "post_w_mlp")
    outs = [loss, grad_x]
    for kind in range(4):
        for nm in order:
            outs.append(big[nm][kind][None] if nm in big else small_out[kind][nm])
    return tuple(outs)
```
